```python
import jax
import jax.numpy as jnp
from jax import lax

D_MODEL = 1024
BATCH = 16
SEQ = 2048
DEPTH = 1

N_POOL_GROUPS = 4
POOL_WINDOWS = (2, 4, 8, 16)
POOL_WIDTH = D_MODEL
POOL_GROUP_DIM = POOL_WIDTH // N_POOL_GROUPS
CONV_WIDTH = D_MODEL
CONV_KSIZE = 3
N_BRANCHES = 2
D_IN = POOL_WIDTH + 3 * CONV_WIDTH + N_BRANCHES * D_MODEL
D_FF = ((8 * D_MODEL // 3 + 127) // 128) * 128
FFN_KSIZE = 3
RMS_EPS = 1e-6

kernel_name = 'hybrid_pool_shortconv_gated_block'


def rms_norm(x, g):
    xf = x.astype(jnp.float32)
    inv = lax.rsqrt(jnp.mean(xf * xf, axis=-1, keepdims=True) + RMS_EPS)
    return (xf * inv * g.astype(jnp.float32)).astype(x.dtype)


def causal_depthwise_conv(u, w):
    k, c = w.shape
    return lax.conv_general_dilated(
        u, w.astype(u.dtype)[:, None, :], window_strides=(1,), padding=[(k - 1, 0)],
        dimension_numbers=('NWC', 'WIO', 'NWC'), feature_group_count=c)


def causal_multiscale_pool(u):
    s = u.shape[1]
    pos = jnp.arange(1, s + 1, dtype=jnp.float32)
    outs = []
    for gi, win in enumerate(POOL_WINDOWS):
        ug = u[..., gi * POOL_GROUP_DIM:(gi + 1) * POOL_GROUP_DIM].astype(jnp.float32)
        csum = jnp.pad(jnp.cumsum(ug, axis=1), ((0, 0), (1, 0), (0, 0)))
        upper = csum[:, 1:]
        lower = jnp.pad(csum[:, :s + 1 - win], ((0, 0), (win - 1, 0), (0, 0)))
        count = jnp.minimum(pos, float(win))[None, :, None]
        outs.append(((upper - lower) / count - ug).astype(u.dtype))
    return jnp.stack(outs, axis=2)


def token_mixer(h, w_in, pool_w, pool_scale, w_pool_proj, conv_w, w_conv_out, w_o):
    b, s, _ = h.shape
    z = jnp.einsum('bsd,de->bse', h, w_in)
    splits = [POOL_WIDTH,
              POOL_WIDTH + CONV_WIDTH,
              POOL_WIDTH + 2 * CONV_WIDTH,
              POOL_WIDTH + 3 * CONV_WIDTH,
              POOL_WIDTH + 3 * CONV_WIDTH + D_MODEL]
    z_pool, z_b, z_c, z_v, z_gpool, z_gconv = jnp.split(z, splits, axis=-1)
    p = causal_multiscale_pool(z_pool)
    p = jnp.einsum('bsgc,gce->bsge', p, pool_w).reshape(b, s, POOL_WIDTH) * pool_scale
    y_pool = jnp.einsum('bsp,pd->bsd', p, w_pool_proj)
    y_conv = jnp.einsum('bsc,cd->bsd', z_b * causal_depthwise_conv(z_c * z_v, conv_w), w_conv_out)
    merged = jax.nn.sigmoid(z_gpool) * y_pool + jax.nn.sigmoid(z_gconv) * y_conv
    return jnp.einsum('bsd,de->bse', merged, w_o)


def channel_mixer(h, w_up, ffn_conv_w, ffn_conv_b, w_down):
    u = causal_depthwise_conv(jnp.einsum('bsd,df->bsf', h, w_up), ffn_conv_w) + ffn_conv_b
    gate, val = jnp.split(u, 2, axis=-1)
    return jnp.einsum('bsf,fd->bsd', jax.nn.silu(gate) * val, w_down)


def _normal(k, shape, scale):
    return jax.random.normal(k, shape, jnp.float32) * scale


def _fwd_setup_inputs(seed: int = 0) -> dict:
    key = jax.random.key(seed)
    ks = jax.random.split(key, 15)
    return {
        'x': _normal(ks[0], (BATCH, SEQ, D_MODEL), 1.0),
        'norm_mix': 1.0 + _normal(ks[1], (DEPTH, D_MODEL), 0.1),
        'w_in': _normal(ks[2], (DEPTH, D_MODEL, D_IN), D_MODEL ** -0.5),
        'pool_w': _normal(ks[3], (DEPTH, N_POOL_GROUPS, POOL_GROUP_DIM, POOL_GROUP_DIM), POOL_GROUP_DIM ** -0.5),
        'pool_scale': 1.0 + _normal(ks[4], (DEPTH, POOL_WIDTH), 0.1),
        'w_pool_proj': _normal(ks[5], (DEPTH, POOL_WIDTH, D_MODEL), POOL_WIDTH ** -0.5),
        'conv_w': _normal(ks[6], (DEPTH, CONV_KSIZE, CONV_WIDTH), CONV_KSIZE ** -0.5),
        'w_conv_out': _normal(ks[7], (DEPTH, CONV_WIDTH, D_MODEL), CONV_WIDTH ** -0.5),
        'w_o': _normal(ks[8], (DEPTH, D_MODEL, D_MODEL), D_MODEL ** -0.5),
        'norm_ffn': 1.0 + _normal(ks[9], (DEPTH, D_MODEL), 0.1),
        'w_up': _normal(ks[10], (DEPTH, D_MODEL, 2 * D_FF), D_MODEL ** -0.5),
        'ffn_conv_w': _normal(ks[11], (DEPTH, FFN_KSIZE, 2 * D_FF), FFN_KSIZE ** -0.5),
        'ffn_conv_b': _normal(ks[12], (DEPTH, 2 * D_FF), 0.02),
        'w_down': _normal(ks[13], (DEPTH, D_FF, D_MODEL), D_FF ** -0.5),
        'norm_final': 1.0 + _normal(ks[14], (D_MODEL,), 0.1),
    }


def _fwd_reference(x, norm_mix, w_in, pool_w, pool_scale, w_pool_proj, conv_w, w_conv_out, w_o,
              norm_ffn, w_up, ffn_conv_w, ffn_conv_b, w_down, norm_final):
    for layer in range(DEPTH):
        h = rms_norm(x, norm_mix[layer])
        x = x + token_mixer(h, w_in[layer], pool_w[layer], pool_scale[layer], w_pool_proj[layer],
                            conv_w[layer], w_conv_out[layer], w_o[layer])
        h = rms_norm(x, norm_ffn[layer])
        x = x + channel_mixer(h, w_up[layer], ffn_conv_w[layer], ffn_conv_b[layer], w_down[layer])
    return rms_norm(x, norm_final)


import jax as _jax
import jax.numpy as _jnp

TWIN_FORMAT = 'train_step'
FWD_PARAMS = ['x', 'norm_mix', 'w_in', 'pool_w', 'pool_scale', 'w_pool_proj', 'conv_w', 'w_conv_out', 'w_o', 'norm_ffn', 'w_up', 'ffn_conv_w', 'ffn_conv_b', 'w_down', 'norm_final']
TWIN_WEIGHTS = ['norm_mix', 'w_in', 'pool_w', 'pool_scale', 'w_pool_proj', 'conv_w', 'w_conv_out', 'w_o', 'norm_ffn', 'w_up', 'ffn_conv_w', 'ffn_conv_b', 'w_down', 'norm_final']
TWIN_DIFF_INPUT = 'x'
TWIN_INPUTS = ['x', 'norm_mix', 'w_in', 'pool_w', 'pool_scale', 'w_pool_proj', 'conv_w', 'w_conv_out', 'w_o', 'norm_ffn', 'w_up', 'ffn_conv_w', 'ffn_conv_b', 'w_down', 'norm_final', 'loss_target', 'm_norm_mix', 'm_w_in', 'm_pool_w', 'm_pool_scale', 'm_w_pool_proj', 'm_conv_w', 'm_w_conv_out', 'm_w_o', 'm_norm_ffn', 'm_w_up', 'm_ffn_conv_w', 'm_ffn_conv_b', 'm_w_down', 'm_norm_final', 'v_norm_mix', 'v_w_in', 'v_pool_w', 'v_pool_scale', 'v_w_pool_proj', 'v_conv_w', 'v_w_conv_out', 'v_w_o', 'v_norm_ffn', 'v_w_up', 'v_ffn_conv_w', 'v_ffn_conv_b', 'v_w_down', 'v_norm_final']
TWIN_OUTPUTS = ['loss', 'grad_x', 'grad_norm_mix', 'grad_w_in', 'grad_pool_w', 'grad_pool_scale', 'grad_w_pool_proj', 'grad_conv_w', 'grad_w_conv_out', 'grad_w_o', 'grad_norm_ffn', 'grad_w_up', 'grad_ffn_conv_w', 'grad_ffn_conv_b', 'grad_w_down', 'grad_norm_final', 'delta_norm_mix', 'delta_w_in', 'delta_pool_w', 'delta_pool_scale', 'delta_w_pool_proj', 'delta_conv_w', 'delta_w_conv_out', 'delta_w_o', 'delta_norm_ffn', 'delta_w_up', 'delta_ffn_conv_w', 'delta_ffn_conv_b', 'delta_w_down', 'delta_norm_final', 'new_m_norm_mix', 'new_m_w_in', 'new_m_pool_w', 'new_m_pool_scale', 'new_m_w_pool_proj', 'new_m_conv_w', 'new_m_w_conv_out', 'new_m_w_o', 'new_m_norm_ffn', 'new_m_w_up', 'new_m_ffn_conv_w', 'new_m_ffn_conv_b', 'new_m_w_down', 'new_m_norm_final', 'new_v_norm_mix', 'new_v_w_in', 'new_v_pool_w', 'new_v_pool_scale', 'new_v_w_pool_proj', 'new_v_conv_w', 'new_v_w_conv_out', 'new_v_w_o', 'new_v_norm_ffn', 'new_v_w_up', 'new_v_ffn_conv_w', 'new_v_ffn_conv_b', 'new_v_w_down', 'new_v_norm_final']
TWIN_LEAF_KINDS = {'loss': 'loss', 'grad_x': 'grad_x', 'grad_norm_mix': 'grad_w', 'grad_w_in': 'grad_w', 'grad_pool_w': 'grad_w', 'grad_pool_scale': 'grad_w', 'grad_w_pool_proj': 'grad_w', 'grad_conv_w': 'grad_w', 'grad_w_conv_out': 'grad_w', 'grad_w_o': 'grad_w', 'grad_norm_ffn': 'grad_w', 'grad_w_up': 'grad_w', 'grad_ffn_conv_w': 'grad_w', 'grad_ffn_conv_b': 'grad_w', 'grad_w_down': 'grad_w', 'grad_norm_final': 'grad_w', 'delta_norm_mix': 'delta_w', 'delta_w_in': 'delta_w', 'delta_pool_w': 'delta_w', 'delta_pool_scale': 'delta_w', 'delta_w_pool_proj': 'delta_w', 'delta_conv_w': 'delta_w', 'delta_w_conv_out': 'delta_w', 'delta_w_o': 'delta_w', 'delta_norm_ffn': 'delta_w', 'delta_w_up': 'delta_w', 'delta_ffn_conv_w': 'delta_w', 'delta_ffn_conv_b': 'delta_w', 'delta_w_down': 'delta_w', 'delta_norm_final': 'delta_w', 'new_m_norm_mix': 'new_m', 'new_m_w_in': 'new_m', 'new_m_pool_w': 'new_m', 'new_m_pool_scale': 'new_m', 'new_m_w_pool_proj': 'new_m', 'new_m_conv_w': 'new_m', 'new_m_w_conv_out': 'new_m', 'new_m_w_o': 'new_m', 'new_m_norm_ffn': 'new_m', 'new_m_w_up': 'new_m', 'new_m_ffn_conv_w': 'new_m', 'new_m_ffn_conv_b': 'new_m', 'new_m_w_down': 'new_m', 'new_m_norm_final': 'new_m', 'new_v_norm_mix': 'new_v', 'new_v_w_in': 'new_v', 'new_v_pool_w': 'new_v', 'new_v_pool_scale': 'new_v', 'new_v_w_pool_proj': 'new_v', 'new_v_conv_w': 'new_v', 'new_v_w_conv_out': 'new_v', 'new_v_w_o': 'new_v', 'new_v_norm_ffn': 'new_v', 'new_v_w_up': 'new_v', 'new_v_ffn_conv_w': 'new_v', 'new_v_ffn_conv_b': 'new_v', 'new_v_w_down': 'new_v', 'new_v_norm_final': 'new_v'}


def _forward(args):
    return _fwd_reference(*[args[k] for k in FWD_PARAMS])


def _output_shape():
    out = _jax.eval_shape(lambda: _forward(_fwd_setup_inputs(0)))
    return out.shape, out.dtype

N_MICROBATCH = 1
ADAM_LR = 0.001
ADAM_B1 = 0.9
ADAM_B2 = 0.999
ADAM_EPS = 1e-08
ADAM_WD = 0.01
ADAM_STEP = 10
PER_EXAMPLE_BATCH_AXIS = {'x': 0, 'loss_target': 0}
SHARED_INPUTS = []
_WEIGHT_DTYPES = {'norm_mix': _jnp.float32, 'w_in': _jnp.float32, 'pool_w': _jnp.float32, 'pool_scale': _jnp.float32, 'w_pool_proj': _jnp.float32, 'conv_w': _jnp.float32, 'w_conv_out': _jnp.float32, 'w_o': _jnp.float32, 'norm_ffn': _jnp.float32, 'w_up': _jnp.float32, 'ffn_conv_w': _jnp.float32, 'ffn_conv_b': _jnp.float32, 'w_down': _jnp.float32, 'norm_final': _jnp.float32}
MOMENT_SCALE = {'norm_mix': 1.735409e-01, 'w_in': 7.340047e-02, 'pool_w': 8.121563e-02, 'pool_scale': 7.739969e-02, 'w_pool_proj': 8.158074e-02, 'conv_w': 9.085462e-02, 'w_conv_out': 9.091132e-02, 'w_o': 1.250461e-01, 'norm_ffn': 1.239648e-01, 'w_up': 5.099027e-02, 'ffn_conv_w': 5.117180e-02, 'ffn_conv_b': 5.082009e-02, 'w_down': 8.681567e-02, 'norm_final': 3.214326e+01}


def _to_microbatches(a, axis):
    t = _jnp.moveaxis(a, axis, 0)
    t = t.reshape((N_MICROBATCH, t.shape[0] // N_MICROBATCH) + t.shape[1:])
    return _jnp.moveaxis(t, 1, axis + 1)


def setup_inputs(seed: int = 0) -> dict:
    inp = _fwd_setup_inputs(seed)
    key = _jax.random.fold_in(_jax.random.key(seed), 7919)
    shape, _ = _output_shape()
    out = dict(inp)
    out["loss_target"] = _jax.random.normal(_jax.random.fold_in(key, 0), shape, _jnp.float32)
    for i, name in enumerate(TWIN_WEIGHTS):
        w = inp[name].astype(_jnp.float32)
        if MOMENT_SCALE is None:
            s = _jnp.sqrt(_jnp.mean(_jnp.square(w)) + 1e-30)
        else:
            s = MOMENT_SCALE[name]
        km, kv = _jax.random.split(_jax.random.fold_in(key, i + 1))
        out[name] = w
        out["m_" + name] = s * _jax.random.normal(km, w.shape, _jnp.float32)
        out["v_" + name] = (s * s) * _jax.random.uniform(kv, w.shape, _jnp.float32, 0.5, 1.5)
    if N_MICROBATCH > 1:
        for name, axis in PER_EXAMPLE_BATCH_AXIS.items():
            out[name] = _to_microbatches(out[name], axis)
    return {'x': out['x'], 'norm_mix': out['norm_mix'], 'w_in': out['w_in'], 'pool_w': out['pool_w'], 'pool_scale': out['pool_scale'], 'w_pool_proj': out['w_pool_proj'], 'conv_w': out['conv_w'], 'w_conv_out': out['w_conv_out'], 'w_o': out['w_o'], 'norm_ffn': out['norm_ffn'], 'w_up': out['w_up'], 'ffn_conv_w': out['ffn_conv_w'], 'ffn_conv_b': out['ffn_conv_b'], 'w_down': out['w_down'], 'norm_final': out['norm_final'], 'loss_target': out['loss_target'], 'm_norm_mix': out['m_norm_mix'], 'm_w_in': out['m_w_in'], 'm_pool_w': out['m_pool_w'], 'm_pool_scale': out['m_pool_scale'], 'm_w_pool_proj': out['m_w_pool_proj'], 'm_conv_w': out['m_conv_w'], 'm_w_conv_out': out['m_w_conv_out'], 'm_w_o': out['m_w_o'], 'm_norm_ffn': out['m_norm_ffn'], 'm_w_up': out['m_w_up'], 'm_ffn_conv_w': out['m_ffn_conv_w'], 'm_ffn_conv_b': out['m_ffn_conv_b'], 'm_w_down': out['m_w_down'], 'm_norm_final': out['m_norm_final'], 'v_norm_mix': out['v_norm_mix'], 'v_w_in': out['v_w_in'], 'v_pool_w': out['v_pool_w'], 'v_pool_scale': out['v_pool_scale'], 'v_w_pool_proj': out['v_w_pool_proj'], 'v_conv_w': out['v_conv_w'], 'v_w_conv_out': out['v_w_conv_out'], 'v_w_o': out['v_w_o'], 'v_norm_ffn': out['v_norm_ffn'], 'v_w_up': out['v_w_up'], 'v_ffn_conv_w': out['v_ffn_conv_w'], 'v_ffn_conv_b': out['v_ffn_conv_b'], 'v_w_down': out['v_w_down'], 'v_norm_final': out['v_norm_final']}


def _loss(weights, diff, rest, loss_target):
    with _jax.named_scope("forward"):
        args = {**rest, TWIN_DIFF_INPUT: diff, **{k: w.astype(_WEIGHT_DTYPES[k]) for k, w in weights.items()}}
        y = _forward(args)
    with _jax.named_scope("loss_head"):
        err = _jnp.square(y.astype(_jnp.float32) - loss_target)
        return 0.5 * _jnp.sum(_jnp.mean(err, axis=-1)) if err.ndim else 0.5 * err


def _adamw(w, g, m, v):
    m = ADAM_B1 * m + (1.0 - ADAM_B1) * g
    v = ADAM_B2 * v + (1.0 - ADAM_B2) * _jnp.square(g)
    m_hat = m / (1.0 - ADAM_B1 ** ADAM_STEP)
    v_hat = v / (1.0 - ADAM_B2 ** ADAM_STEP)
    delta = -ADAM_LR * (m_hat / (_jnp.sqrt(v_hat) + ADAM_EPS) + ADAM_WD * w)
    return delta, m, v


def reference(x, norm_mix, w_in, pool_w, pool_scale, w_pool_proj, conv_w, w_conv_out, w_o, norm_ffn, w_up, ffn_conv_w, ffn_conv_b, w_down, norm_final, loss_target, m_norm_mix, m_w_in, m_pool_w, m_pool_scale, m_w_pool_proj, m_conv_w, m_w_conv_out, m_w_o, m_norm_ffn, m_w_up, m_ffn_conv_w, m_ffn_conv_b, m_w_down, m_norm_final, v_norm_mix, v_w_in, v_pool_w, v_pool_scale, v_w_pool_proj, v_conv_w, v_w_conv_out, v_w_o, v_norm_ffn, v_w_up, v_ffn_conv_w, v_ffn_conv_b, v_w_down, v_norm_final):
    given = dict(x=x, norm_mix=norm_mix, w_in=w_in, pool_w=pool_w, pool_scale=pool_scale, w_pool_proj=w_pool_proj, conv_w=conv_w, w_conv_out=w_conv_out, w_o=w_o, norm_ffn=norm_ffn, w_up=w_up, ffn_conv_w=ffn_conv_w, ffn_conv_b=ffn_conv_b, w_down=w_down, norm_final=norm_final, loss_target=loss_target, m_norm_mix=m_norm_mix, m_w_in=m_w_in, m_pool_w=m_pool_w, m_pool_scale=m_pool_scale, m_w_pool_proj=m_w_pool_proj, m_conv_w=m_conv_w, m_w_conv_out=m_w_conv_out, m_w_o=m_w_o, m_norm_ffn=m_norm_ffn, m_w_up=m_w_up, m_ffn_conv_w=m_ffn_conv_w, m_ffn_conv_b=m_ffn_conv_b, m_w_down=m_w_down, m_norm_final=m_norm_final, v_norm_mix=v_norm_mix, v_w_in=v_w_in, v_pool_w=v_pool_w, v_pool_scale=v_pool_scale, v_w_pool_proj=v_w_pool_proj, v_conv_w=v_conv_w, v_w_conv_out=v_w_conv_out, v_w_o=v_w_o, v_norm_ffn=v_norm_ffn, v_w_up=v_w_up, v_ffn_conv_w=v_ffn_conv_w, v_ffn_conv_b=v_ffn_conv_b, v_w_down=v_w_down, v_norm_final=v_norm_final)
    weights = {n: given[n] for n in TWIN_WEIGHTS}
    shared = {n: given[n] for n in SHARED_INPUTS}
    per_example = {n: given[n] for n in ['x']}
    grad_fn = _jax.value_and_grad(_loss, argnums=(0, 1))

    def one_microbatch(ex, loss_target):
        ex = dict(ex)
        diff = ex.pop(TWIN_DIFF_INPUT)
        return grad_fn(weights, diff, {**shared, **ex}, loss_target)

    if N_MICROBATCH == 1:
        loss, (grad_w, grad_x) = one_microbatch(per_example, given["loss_target"])
    else:
        def body(carry, xs):
            loss_sum, grad_sum = carry
            l_k, (gw_k, gx_k) = one_microbatch(xs[0], xs[1])
            with _jax.named_scope("update"):
                return (loss_sum + l_k, _jax.tree.map(_jnp.add, grad_sum, gw_k)), gx_k

        init = (_jnp.zeros((), _jnp.float32), _jax.tree.map(_jnp.zeros_like, weights))
        (loss, grad_w), grad_x = _jax.lax.scan(body, init, (per_example, given["loss_target"]))
    with _jax.named_scope("update"):
        delta_w, new_m, new_v = {}, {}, {}
        for n in TWIN_WEIGHTS:
            delta_w[n], new_m[n], new_v[n] = _adamw(weights[n], grad_w[n], given["m_" + n], given["v_" + n])
    return (loss, grad_x, *[grad_w[n] for n in TWIN_WEIGHTS], *[delta_w[n] for n in TWIN_WEIGHTS],
            *[new_m[n] for n in TWIN_WEIGHTS], *[new_v[n] for n in TWIN_WEIGHTS])
```

```python
import jax
import jax.numpy as jnp
from jax import lax
from jax.experimental import pallas as pl
from jax.experimental.pallas import tpu as pltpu

F32 = jnp.float32
BF16 = jnp.bfloat16
SDS = jax.ShapeDtypeStruct
MESH = pl.DeviceIdType.MESH

RMS_EPS = 1e-6
POOL_WINDOWS = (2, 4, 8, 16)
N_GROUPS = len(POOL_WINDOWS)
N_SPLITS = 6

ADAM_LR = 0.001
ADAM_B1 = 0.9
ADAM_B2 = 0.999
ADAM_EPS = 1e-08
ADAM_WD = 0.01
ADAM_STEP = 10

LANES = 128
SUBLANES_F32 = 8
SUBLANES_BF16 = 16
VMEM_BYTES = 64 * 1024 * 1024
VMEM_CAP = VMEM_BYTES - 8 * 1024 * 1024
VMEM_FLOOR = 16 * 1024 * 1024

DZ_COLBLK = (1, 2, 3, 0, 4, 5)


def _tile(dim, pref, align):
    if dim <= pref:
        return dim
    t = (pref // align) * align
    while t >= align:
        if dim % t == 0:
            return t
        t -= align
    return dim


def _nbytes(shape, dtype):
    n = 1
    for s in shape:
        n *= s
    return n * jnp.dtype(dtype).itemsize


def _params(sem, block_bytes, temp_bytes=0):
    need = 2 * block_bytes + temp_bytes + 4 * 1024 * 1024
    return pltpu.CompilerParams(dimension_semantics=sem, vmem_limit_bytes=int(min(max(need, VMEM_FLOOR), VMEM_CAP)))


def _dot(a, b):
    return jnp.dot(a, b, preferred_element_type=F32)


def _dot_tb(a, b):
    return lax.dot_general(a, b, (((1,), (1,)), ((), ())), preferred_element_type=F32)


def _dot_ta(a, b):
    return lax.dot_general(a, b, (((0,), (0,)), ((), ())), preferred_element_type=F32)


def _rms_fwd(x):
    inv = lax.rsqrt(jnp.mean(x * x, axis=-1, keepdims=True) + RMS_EPS)
    return x * inv, inv


def _rms_bwd(dy, xhat, inv, g):
    gd = dy * g
    return inv * (gd - xhat * jnp.mean(gd * xhat, axis=-1, keepdims=True))


def _sigmoid(x):
    return 1.0 / (1.0 + jnp.exp(-x))


def _shift_down(x, k, row):
    return jnp.where(row >= k, pltpu.roll(x, k, 0), 0.0)


def _shift_up(x, k, row):
    s = x.shape[0]
    return jnp.where(row < s - k, pltpu.roll(x, s - k, 0), 0.0)


def _pool_fwd(u, win, row):
    s = u
    k = 1
    while k < win:
        s = s + _shift_down(s, k, row)
        k *= 2
    cnt = jnp.minimum(row + 1, win).astype(F32)
    return s / cnt - u


def _pool_bwd(dp, win, row):
    cnt = jnp.minimum(row + 1, win).astype(F32)
    s = dp / cnt
    k = 1
    while k < win:
        s = s + _shift_up(s, k, row)
        k *= 2
    return s - dp


def _fwd_in(x, g, w):
    t, d = x.shape
    n = w.shape[1]
    tm = _tile(t, 1024, SUBLANES_BF16)
    tn = _tile(n, 1536, LANES)

    def body(x_ref, g_ref, w_ref, z_ref, h_ref, hs):
        @pl.when(pl.program_id(1) == 0)
        def _():
            xh, _ = _rms_fwd(x_ref[...])
            h = (xh * g_ref[...]).astype(BF16)
            hs[...] = h
            h_ref[...] = h

        z_ref[...] = _dot(hs[...], w_ref[...]).astype(BF16)

    blocks = _nbytes((tm, d), F32) + _nbytes((d, tn), BF16) + _nbytes((tm, tn), BF16) + _nbytes((tm, d), BF16)
    return pl.pallas_call(
        body, name="fwd_in", grid=(t // tm, n // tn),
        in_specs=[pl.BlockSpec((tm, d), lambda i, j: (i, 0)), pl.BlockSpec((1, d), lambda i, j: (0, 0)),
                  pl.BlockSpec((d, tn), lambda i, j: (0, j))],
        out_specs=[pl.BlockSpec((tm, tn), lambda i, j: (i, j)), pl.BlockSpec((tm, d), lambda i, j: (i, 0))],
        out_shape=[SDS((t, n), BF16), SDS((t, d), BF16)],
        scratch_shapes=[pltpu.VMEM((tm, d), BF16)],
        compiler_params=_params(("parallel", "arbitrary"), blocks, 3 * _nbytes((tm, d), F32)),
    )(x, g, w)


def _mixer_mid_fwd(z, pool_w, pool_scale, conv_w, nseq):
    t = z.shape[0]
    d = pool_scale.shape[1]
    s = t // nseq
    c = d // N_GROUPS

    def body(zp, zb, zc, zv, pw, ps, cw, o):
        j = pl.program_id(1)
        row = lax.broadcasted_iota(jnp.int32, (s, c), 0)
        for gi, win in enumerate(POOL_WINDOWS):
            @pl.when(j == gi)
            def _(win=win):
                pooled = _pool_fwd(zp[...].astype(F32), win, row)
                o[0] = (_dot(pooled.astype(BF16), pw[...]) * ps[...]).astype(BF16)

        cv = zc[...].astype(F32) * zv[...].astype(F32)
        cc = (cw[pl.ds(2, 1), :] * cv + cw[pl.ds(1, 1), :] * _shift_down(cv, 1, row)
              + cw[pl.ds(0, 1), :] * _shift_down(cv, 2, row))
        o[1] = (zb[...].astype(F32) * cc).astype(BF16)

    blocks = 4 * _nbytes((s, c), BF16) + _nbytes((c, c), BF16) + _nbytes((2, s, c), BF16)
    return pl.pallas_call(
        body, name="mixer_mid_fwd", grid=(nseq, N_GROUPS),
        in_specs=[pl.BlockSpec((s, c), lambda b, j: (b, j)),
                  pl.BlockSpec((s, c), lambda b, j: (b, N_GROUPS + j)),
                  pl.BlockSpec((s, c), lambda b, j: (b, 2 * N_GROUPS + j)),
                  pl.BlockSpec((s, c), lambda b, j: (b, 3 * N_GROUPS + j)),
                  pl.BlockSpec((None, c, c), lambda b, j: (j, 0, 0)),
                  pl.BlockSpec((1, c), lambda b, j: (0, j)),
                  pl.BlockSpec((3, c), lambda b, j: (0, j))],
        out_specs=pl.BlockSpec((2, s, c), lambda b, j: (0, b, j)),
        out_shape=SDS((3, t, d), BF16),
        compiler_params=_params(("parallel", "parallel"), blocks, 8 * _nbytes((s, c), F32)),
    )(z, z, z, z, pool_w, pool_scale, conv_w)


def _mixer_out(lhs3, z, x, w3, g_ffn):
    t, d = x.shape
    tm = _tile(t, 256, SUBLANES_BF16)

    def body(pq, zgp, zgc, x_ref, w_ref, g_ref, mrg, ypc, x1o, h2o):
        yp = _dot(pq[0], w_ref[0])
        yc = _dot(pq[1], w_ref[1])
        m = _sigmoid(zgp[...].astype(F32)) * yp + _sigmoid(zgc[...].astype(F32)) * yc
        mb = m.astype(BF16)
        x1 = x_ref[...] + _dot(mb, w_ref[2])
        ypc[0] = yp.astype(BF16)
        ypc[1] = yc.astype(BF16)
        mrg[...] = mb
        x1o[...] = x1
        xh, _ = _rms_fwd(x1)
        h2o[...] = (xh * g_ref[...]).astype(BF16)

    blocks = (_nbytes((2, tm, d), BF16) * 2 + _nbytes((tm, d), BF16) * 4 + _nbytes((tm, d), F32) * 2
              + _nbytes((3, d, d), BF16))
    return pl.pallas_call(
        body, name="mixer_out", grid=(t // tm,),
        in_specs=[pl.BlockSpec((2, tm, d), lambda i: (0, i, 0)),
                  pl.BlockSpec((tm, d), lambda i: (i, 4)),
                  pl.BlockSpec((tm, d), lambda i: (i, 5)),
                  pl.BlockSpec((tm, d), lambda i: (i, 0)),
                  pl.BlockSpec((3, d, d), lambda i: (0, 0, 0)),
                  pl.BlockSpec((1, d), lambda i: (0, 0))],
        out_specs=[pl.BlockSpec((None, tm, d), lambda i: (2, i, 0)),
                   pl.BlockSpec((2, tm, d), lambda i: (0, i, 0)),
                   pl.BlockSpec((tm, d), lambda i: (i, 0)),
                   pl.BlockSpec((tm, d), lambda i: (i, 0))],
        out_shape=[SDS(lhs3.shape, BF16), SDS((2, t, d), BF16), SDS((t, d), F32), SDS((t, d), BF16)],
        input_output_aliases={0: 0},
        compiler_params=_params(("parallel",), blocks, 8 * _nbytes((tm, d), F32)),
    )(lhs3, z, z, x, w3, g_ffn)


def _ffn_up(h2, w_up, f):
    t, d = h2.shape
    tm = _tile(t, 1024, SUBLANES_BF16)
    tn = _tile(f, 1408, LANES)
    npp = f // tn

    def body(h_ref, w_ref, o_ref):
        o_ref[...] = _dot(h_ref[...], w_ref[...]).astype(BF16)

    blocks = _nbytes((tm, d), BF16) + _nbytes((d, tn), BF16) + _nbytes((tm, tn), BF16)
    return pl.pallas_call(
        body, name="ffn_up", grid=(t // tm, 2 * npp),
        in_specs=[pl.BlockSpec((tm, d), lambda i, j: (i, 0)), pl.BlockSpec((d, tn), lambda i, j: (0, j))],
        out_specs=pl.BlockSpec((None, tm, tn), lambda i, j: (j // npp, i, j % npp)),
        out_shape=SDS((2, t, f), BF16),
        compiler_params=_params(("parallel", "parallel"), blocks, _nbytes((tm, tn), F32)),
    )(h2, w_up)


def _conv3_rows(u, u1, u2, w_ref, p):
    return w_ref[p, pl.ds(2, 1), :] * u + w_ref[p, pl.ds(1, 1), :] * u1 + w_ref[p, pl.ds(0, 1), :] * u2


def _ffn_mid_fwd(u0, cw, cb, nseq):
    _, t, f = u0.shape
    s = t // nseq
    c = _tile(f, 256, LANES)

    def body(u_ref, w_ref, b_ref, a_ref):
        row = lax.broadcasted_iota(jnp.int32, (s, c), 0)
        act = []
        for p in range(2):
            u = u_ref[p].astype(F32)
            act.append(_conv3_rows(u, _shift_down(u, 1, row), _shift_down(u, 2, row), w_ref, p) + b_ref[p])
        ug, uv = act
        a_ref[...] = (ug * _sigmoid(ug) * uv).astype(BF16)

    blocks = _nbytes((2, s, c), BF16) + _nbytes((s, c), BF16)
    return pl.pallas_call(
        body, name="ffn_mid_fwd", grid=(f // c, nseq),
        in_specs=[pl.BlockSpec((2, s, c), lambda j, b: (0, b, j)),
                  pl.BlockSpec((2, 3, c), lambda j, b: (0, 0, j)),
                  pl.BlockSpec((2, 1, c), lambda j, b: (0, 0, j))],
        out_specs=pl.BlockSpec((s, c), lambda j, b: (b, j)),
        out_shape=SDS((t, f), BF16),
        compiler_params=_params(("parallel", "parallel"), blocks, 8 * _nbytes((s, c), F32)),
    )(u0, cw, cb)


def _ffn_down_loss(a, w_down, x1, tgt, g_fin):
    t, f = a.shape
    d = x1.shape[1]
    tm = _tile(t, 256, SUBLANES_BF16)
    nsteps = t // tm

    def body(a_ref, w_ref, x1_ref, t_ref, g_ref, dx_ref, dxb_ref, loss_ref, gg_ref, lacc):
        i = pl.program_id(0)

        @pl.when(i == 0)
        def _():
            lacc[...] = jnp.zeros_like(lacc)
            gg_ref[...] = jnp.zeros_like(gg_ref)

        x2 = x1_ref[...] + _dot(a_ref[...], w_ref[...])
        xh, inv = _rms_fwd(x2)
        g = g_ref[...]
        e = xh * g - t_ref[...]
        lacc[...] += jnp.sum(e * e, axis=0, keepdims=True)
        dy = e * (1.0 / d)
        gg_ref[...] += jnp.sum(dy * xh, axis=0, keepdims=True)
        dx2 = _rms_bwd(dy, xh, inv, g)
        dx_ref[...] = dx2
        dxb_ref[...] = dx2.astype(BF16)

        @pl.when(i == nsteps - 1)
        def _():
            loss_ref[...] = jnp.sum(lacc[...], axis=1, keepdims=True) * (0.5 / d)

    blocks = (_nbytes((tm, f), BF16) + _nbytes((f, d), BF16) + 3 * _nbytes((tm, d), F32) + _nbytes((tm, d), BF16))
    return pl.pallas_call(
        body, name="ffn_down_loss", grid=(nsteps,),
        in_specs=[pl.BlockSpec((tm, f), lambda i: (i, 0)), pl.BlockSpec((f, d), lambda i: (0, 0)),
                  pl.BlockSpec((tm, d), lambda i: (i, 0)), pl.BlockSpec((tm, d), lambda i: (i, 0)),
                  pl.BlockSpec((1, d), lambda i: (0, 0))],
        out_specs=[pl.BlockSpec((tm, d), lambda i: (i, 0)), pl.BlockSpec((tm, d), lambda i: (i, 0)),
                   pl.BlockSpec((1, 1), lambda i: (0, 0)), pl.BlockSpec((1, d), lambda i: (0, 0))],
        out_shape=[SDS((t, d), F32), SDS((t, d), BF16), SDS((1, 1), F32), SDS((1, d), F32)],
        scratch_shapes=[pltpu.VMEM((1, d), F32)],
        compiler_params=_params(("arbitrary",), blocks, 8 * _nbytes((tm, d), F32)),
    )(a, w_down, x1, tgt, g_fin)


def _ffn_bwd_da(dxb, w_down):
    t, d = dxb.shape
    f = w_down.shape[0]
    tm = _tile(t, 1024, SUBLANES_BF16)
    tn = _tile(f, 1408, LANES)

    def body(x_ref, w_ref, o_ref):
        o_ref[...] = _dot_tb(x_ref[...], w_ref[...]).astype(BF16)

    blocks = _nbytes((tm, d), BF16) + _nbytes((tn, d), BF16) + _nbytes((tm, tn), BF16)
    return pl.pallas_call(
        body, name="ffn_bwd_da", grid=(t // tm, f // tn),
        in_specs=[pl.BlockSpec((tm, d), lambda i, j: (i, 0)), pl.BlockSpec((tn, d), lambda i, j: (j, 0))],
        out_specs=pl.BlockSpec((tm, tn), lambda i, j: (i, j)),
        out_shape=SDS((t, f), BF16),
        compiler_params=_params(("parallel", "parallel"), blocks, _nbytes((tm, tn), F32)),
    )(dxb, w_down)


def _ffn_mid_bwd(da, u0, cw, cb, nseq):
    _, t, f = u0.shape
    s = t // nseq
    c = _tile(f, 128, LANES)

    def body(da_ref, u_ref, w_ref, b_ref, du_ref, gw_ref, gb_ref):
        @pl.when(pl.program_id(1) == 0)
        def _():
            gw_ref[...] = jnp.zeros_like(gw_ref)
            gb_ref[...] = jnp.zeros_like(gb_ref)

        row = lax.broadcasted_iota(jnp.int32, (s, c), 0)
        us, act = [], []
        for p in range(2):
            u = u_ref[p].astype(F32)
            u1 = _shift_down(u, 1, row)
            u2 = _shift_down(u, 2, row)
            us.append((u, u1, u2))
            act.append(_conv3_rows(u, u1, u2, w_ref, p) + b_ref[p])
        ug, uv = act
        sg = _sigmoid(ug)
        dacc = da_ref[...].astype(F32)
        dug = dacc * uv * sg * (1.0 + ug * (1.0 - sg))
        duv = dacc * (ug * sg)
        for p, du in ((0, dug), (1, duv)):
            u, u1, u2 = us[p]
            gb_ref[p] += jnp.sum(du, axis=0, keepdims=True)
            gw_ref[p, pl.ds(0, 1), :] += jnp.sum(du * u2, axis=0, keepdims=True)
            gw_ref[p, pl.ds(1, 1), :] += jnp.sum(du * u1, axis=0, keepdims=True)
            gw_ref[p, pl.ds(2, 1), :] += jnp.sum(du * u, axis=0, keepdims=True)
            du_ref[p] = _conv3_rows(du, _shift_up(du, 1, row), _shift_up(du, 2, row), w_ref, p).astype(BF16)

    blocks = _nbytes((s, c), BF16) + 2 * _nbytes((2, s, c), BF16)
    return pl.pallas_call(
        body, name="ffn_mid_bwd", grid=(f // c, nseq),
        in_specs=[pl.BlockSpec((s, c), lambda j, b: (b, j)),
                  pl.BlockSpec((2, s, c), lambda j, b: (0, b, j)),
                  pl.BlockSpec((2, 3, c), lambda j, b: (0, 0, j)),
                  pl.BlockSpec((2, 1, c), lambda j, b: (0, 0, j))],
        out_specs=[pl.BlockSpec((2, s, c), lambda j, b: (0, b, j)),
                   pl.BlockSpec((2, 3, c), lambda j, b: (0, 0, j)),
                   pl.BlockSpec((2, 1, c), lambda j, b: (0, 0, j))],
        out_shape=[SDS((2, t, f), BF16), SDS((2, 3, f), F32), SDS((2, 1, f), F32)],
        compiler_params=_params(("parallel", "arbitrary"), blocks, 20 * _nbytes((s, c), F32)),
    )(da, u0, cw, cb)


def _wgrad(a, b, name, *, tr, tn, b_plane_of=None):
    t, m = a.shape
    n_total = b.shape[-1] * (b.shape[0] if b.ndim == 3 else 1)
    tk = _tile(t, 1024, SUBLANES_BF16)
    nk = t // tk

    def body(a_ref, b_ref, o_ref, acc):
        k = pl.program_id(2)
        part = _dot_ta(a_ref[...], b_ref[...])

        @pl.when(k == 0)
        def _():
            acc[...] = part

        @pl.when(k > 0)
        def _():
            acc[...] += part

        @pl.when(k == nk - 1)
        def _():
            o_ref[...] = acc[...].astype(BF16)

    if b.ndim == 3:
        b_spec = pl.BlockSpec((None, tk, tn), lambda r, n, k: (b_plane_of(n)[0], k, b_plane_of(n)[1]))
    else:
        b_spec = pl.BlockSpec((tk, tn), lambda r, n, k: (k, n))
    blocks = _nbytes((tk, tr), BF16) + _nbytes((tk, tn), BF16) + _nbytes((tr, tn), BF16)
    return pl.pallas_call(
        body, name=name, grid=(m // tr, n_total // tn, nk),
        in_specs=[pl.BlockSpec((tk, tr), lambda r, n, k: (k, r)), b_spec],
        out_specs=pl.BlockSpec((tr, tn), lambda r, n, k: (r, n)),
        out_shape=SDS((m, n_total), BF16),
        scratch_shapes=[pltpu.VMEM((tr, tn), F32)],
        compiler_params=_params(("parallel", "parallel", "arbitrary"), blocks, 2 * _nbytes((tr, tn), F32)),
    )(a, b)


def _wgrad3(lhs3, rhs3):
    nw, t, d = lhs3.shape
    tk = _tile(t, 1024, SUBLANES_BF16)
    nk = t // tk

    def body(a_ref, b_ref, o_ref, acc):
        k = pl.program_id(1)
        part = _dot_ta(a_ref[...], b_ref[...])

        @pl.when(k == 0)
        def _():
            acc[...] = part

        @pl.when(k > 0)
        def _():
            acc[...] += part

        @pl.when(k == nk - 1)
        def _():
            o_ref[...] = acc[...].astype(BF16)

    blocks = 2 * _nbytes((tk, d), BF16) + _nbytes((d, d), BF16)
    return pl.pallas_call(
        body, name="wgrad_sq3", grid=(nw, nk),
        in_specs=[pl.BlockSpec((None, tk, d), lambda w, k: (w, k, 0)),
                  pl.BlockSpec((None, tk, d), lambda w, k: (w, k, 0))],
        out_specs=pl.BlockSpec((None, d, d), lambda w, k: (w, 0, 0)),
        out_shape=SDS((nw, d, d), BF16),
        scratch_shapes=[pltpu.VMEM((d, d), F32)],
        compiler_params=_params(("parallel", "arbitrary"), blocks, 2 * _nbytes((d, d), F32)),
    )(lhs3, rhs3)


def _ffn_bwd_dx1(du0, w_up, x1, dx2, g_ffn, n_planes_out):
    _, t, f = du0.shape
    d = x1.shape[1]
    tm = _tile(t, 512, SUBLANES_BF16)
    tn = _tile(f, 1408, LANES)
    npp = f // tn
    nk = 2 * npp

    def body(du_ref, w_ref, x1_ref, dx2_ref, g_ref, dx1_ref, dxb_ref, gg_ref, acc):
        i = pl.program_id(0)
        k = pl.program_id(1)
        part = _dot_tb(du_ref[...], w_ref[...])

        @pl.when(k == 0)
        def _():
            acc[...] = part

        @pl.when(k > 0)
        def _():
            acc[...] += part

        @pl.when((i == 0) & (k == 0))
        def _():
            gg_ref[...] = jnp.zeros_like(gg_ref)

        @pl.when(k == nk - 1)
        def _():
            dh = acc[...]
            xh, inv = _rms_fwd(x1_ref[...])
            gg_ref[...] += jnp.sum(dh * xh, axis=0, keepdims=True)
            dx1 = dx2_ref[...] + _rms_bwd(dh, xh, inv, g_ref[...])
            dx1_ref[...] = dx1
            dxb_ref[...] = dx1.astype(BF16)

    blocks = (_nbytes((tm, tn), BF16) + _nbytes((d, tn), BF16) + 3 * _nbytes((tm, d), F32) + _nbytes((tm, d), BF16))
    return pl.pallas_call(
        body, name="ffn_bwd_dx1", grid=(t // tm, nk),
        in_specs=[pl.BlockSpec((None, tm, tn), lambda i, k: (k // npp, i, k % npp)),
                  pl.BlockSpec((d, tn), lambda i, k: (0, k)),
                  pl.BlockSpec((tm, d), lambda i, k: (i, 0)),
                  pl.BlockSpec((tm, d), lambda i, k: (i, 0)),
                  pl.BlockSpec((1, d), lambda i, k: (0, 0))],
        out_specs=[pl.BlockSpec((tm, d), lambda i, k: (i, 0)),
                   pl.BlockSpec((None, tm, d), lambda i, k: (n_planes_out - 1, i, 0)),
                   pl.BlockSpec((1, d), lambda i, k: (0, 0))],
        out_shape=[SDS((t, d), F32), SDS((n_planes_out, t, d), BF16), SDS((1, d), F32)],
        scratch_shapes=[pltpu.VMEM((tm, d), F32)],
        compiler_params=_params(("arbitrary", "arbitrary"), blocks, 8 * _nbytes((tm, d), F32)),
    )(du0, w_up, x1, dx2, g_ffn)


def _mixer_bwd(rhs3, z, ypc, w3):
    _, t, d = rhs3.shape
    tm = _tile(t, 256, SUBLANES_BF16)

    def body(dx_ref, zgp, zgc, ypc_ref, w_ref, dyo, dzo, dpq):
        dm = _dot_tb(dx_ref[...], w_ref[2])
        sp = _sigmoid(zgp[...].astype(F32))
        sc = _sigmoid(zgc[...].astype(F32))
        dyp = (dm * sp).astype(BF16)
        dyc = (dm * sc).astype(BF16)
        dzo[0] = (dm * ypc_ref[0].astype(F32) * sp * (1.0 - sp)).astype(BF16)
        dzo[1] = (dm * ypc_ref[1].astype(F32) * sc * (1.0 - sc)).astype(BF16)
        dyo[0] = dyp
        dyo[1] = dyc
        dpq[0] = _dot_tb(dyp, w_ref[0]).astype(BF16)
        dpq[1] = _dot_tb(dyc, w_ref[1]).astype(BF16)

    blocks = _nbytes((tm, d), BF16) * 3 + _nbytes((2, tm, d), BF16) * 4 + _nbytes((3, d, d), BF16)
    return pl.pallas_call(
        body, name="mixer_bwd", grid=(t // tm,),
        in_specs=[pl.BlockSpec((None, tm, d), lambda i: (2, i, 0)),
                  pl.BlockSpec((tm, d), lambda i: (i, 4)),
                  pl.BlockSpec((tm, d), lambda i: (i, 5)),
                  pl.BlockSpec((2, tm, d), lambda i: (0, i, 0)),
                  pl.BlockSpec((3, d, d), lambda i: (0, 0, 0))],
        out_specs=[pl.BlockSpec((2, tm, d), lambda i: (0, i, 0)),
                   pl.BlockSpec((2, tm, d), lambda i: (2, i, 0)),
                   pl.BlockSpec((2, tm, d), lambda i: (0, i, 0))],
        out_shape=[SDS(rhs3.shape, BF16), SDS((N_SPLITS, t, d), BF16), SDS((2, t, d), BF16)],
        input_output_aliases={0: 0},
        compiler_params=_params(("parallel",), blocks, 8 * _nbytes((tm, d), F32)),
    )(rhs3, z, z, ypc, w3)


def _conv_bwd(dz, dpq, z, conv_w, nseq):
    _, t, d = dz.shape
    s = t // nseq
    c = _tile(d, 128, LANES)
    nb = d // c

    def body(dz_in, dq_ref, zb, zc, zv, cw, dzo, gw_ref):
        del dz_in

        @pl.when(pl.program_id(1) == 0)
        def _():
            gw_ref[...] = jnp.zeros_like(gw_ref)

        row = lax.broadcasted_iota(jnp.int32, (s, c), 0)
        b = zb[...].astype(F32)
        cm = zc[...].astype(F32)
        v = zv[...].astype(F32)
        cv = cm * v
        cv1 = _shift_down(cv, 1, row)
        cv2 = _shift_down(cv, 2, row)
        w0, w1, w2 = cw[pl.ds(0, 1), :], cw[pl.ds(1, 1), :], cw[pl.ds(2, 1), :]
        cc = w2 * cv + w1 * cv1 + w0 * cv2
        dq = dq_ref[...].astype(F32)
        dzo[0] = (dq * cc).astype(BF16)
        dcc = dq * b
        gw_ref[pl.ds(0, 1), :] += jnp.sum(dcc * cv2, axis=0, keepdims=True)
        gw_ref[pl.ds(1, 1), :] += jnp.sum(dcc * cv1, axis=0, keepdims=True)
        gw_ref[pl.ds(2, 1), :] += jnp.sum(dcc * cv, axis=0, keepdims=True)
        dcv = w2 * dcc + w1 * _shift_up(dcc, 1, row) + w0 * _shift_up(dcc, 2, row)
        dzo[1] = (dcv * v).astype(BF16)
        dzo[2] = (dcv * cm).astype(BF16)

    blocks = 4 * _nbytes((s, c), BF16) + _nbytes((3, s, c), BF16)
    return pl.pallas_call(
        body, name="conv_bwd", grid=(nb, nseq),
        in_specs=[pl.BlockSpec(memory_space=pl.ANY),
                  pl.BlockSpec((None, s, c), lambda j, b: (1, b, j)),
                  pl.BlockSpec((s, c), lambda j, b: (b, nb + j)),
                  pl.BlockSpec((s, c), lambda j, b: (b, 2 * nb + j)),
                  pl.BlockSpec((s, c), lambda j, b: (b, 3 * nb + j)),
                  pl.BlockSpec((3, c), lambda j, b: (0, j))],
        out_specs=[pl.BlockSpec((3, s, c), lambda j, b: (0, b, j)),
                   pl.BlockSpec((3, c), lambda j, b: (0, j))],
        out_shape=[SDS(dz.shape, BF16), SDS((3, d), F32)],
        input_output_aliases={0: 0},
        compiler_params=_params(("parallel", "arbitrary"), blocks, 16 * _nbytes((s, c), F32)),
    )(dz, dpq, z, z, z, conv_w)


def _pool_bwd_call(dz, dpq, z, pool_w, pool_scale, nseq):
    _, t, d = dz.shape
    s = t // nseq
    c = d // N_GROUPS

    def body(dz_in, dp_ref, zp, pw, ps, dzo, gpw_ref, gps_ref):
        del dz_in
        j = pl.program_id(0)

        @pl.when(pl.program_id(1) == 0)
        def _():
            gpw_ref[...] = jnp.zeros_like(gpw_ref)
            gps_ref[...] = jnp.zeros_like(gps_ref)

        row = lax.broadcasted_iota(jnp.int32, (s, c), 0)
        for gi, win in enumerate(POOL_WINDOWS):
            @pl.when(j == gi)
            def _(win=win):
                pb = _pool_fwd(zp[...].astype(F32), win, row).astype(BF16)
                plin = _dot(pb, pw[...])
                dps = dp_ref[...].astype(F32)
                gps_ref[...] += jnp.sum(dps * plin, axis=0, keepdims=True)
                dplb = (dps * ps[...]).astype(BF16)
                gpw_ref[...] += _dot_ta(pb, dplb)
                dzo[...] = _pool_bwd(_dot_tb(dplb, pw[...]), win, row).astype(BF16)

    blocks = 3 * _nbytes((s, c), BF16) + _nbytes((c, c), BF16) + _nbytes((c, c), F32)
    return pl.pallas_call(
        body, name="pool_bwd", grid=(N_GROUPS, nseq),
        in_specs=[pl.BlockSpec(memory_space=pl.ANY),
                  pl.BlockSpec((None, s, c), lambda j, b: (0, b, j)),
                  pl.BlockSpec((s, c), lambda j, b: (b, j)),
                  pl.BlockSpec((None, c, c), lambda j, b: (j, 0, 0)),
                  pl.BlockSpec((1, c), lambda j, b: (0, j))],
        out_specs=[pl.BlockSpec((None, s, c), lambda j, b: (3, b, j)),
                   pl.BlockSpec((None, c, c), lambda j, b: (j, 0, 0)),
                   pl.BlockSpec((1, c), lambda j, b: (0, j))],
        out_shape=[SDS(dz.shape, BF16), SDS((N_GROUPS, c, c), F32), SDS((1, d), F32)],
        input_output_aliases={0: 0},
        compiler_params=_params(("parallel", "arbitrary"), blocks, 10 * _nbytes((s, c), F32)),
    )(dz, dpq, z, pool_w, pool_scale)


def _wgrad_in(h1, dz):
    t, d = h1.shape
    tk = _tile(t, 1024, SUBLANES_BF16)
    nk = t // tk

    def colblk(p):
        return jnp.where(p < 4, (p + 1) % 4, p)

    def body(a_ref, b_ref, o_ref, acc):
        k = pl.program_id(1)
        part = _dot_ta(a_ref[...], b_ref[...])

        @pl.when(k == 0)
        def _():
            acc[...] = part

        @pl.when(k > 0)
        def _():
            acc[...] += part

        @pl.when(k == nk - 1)
        def _():
            o_ref[...] = acc[...].astype(BF16)

    blocks = 2 * _nbytes((tk, d), BF16) + _nbytes((d, d), BF16)
    return pl.pallas_call(
        body, name="wgrad_in", grid=(N_SPLITS, nk),
        in_specs=[pl.BlockSpec((tk, d), lambda p, k: (k, 0)),
                  pl.BlockSpec((None, tk, d), lambda p, k: (p, k, 0))],
        out_specs=pl.BlockSpec((d, d), lambda p, k: (0, colblk(p))),
        out_shape=SDS((d, N_SPLITS * d), BF16),
        scratch_shapes=[pltpu.VMEM((d, d), F32)],
        compiler_params=_params(("parallel", "arbitrary"), blocks, 2 * _nbytes((d, d), F32)),
    )(h1, dz)


def _mixer_bwd_dx(dz, w_in, x, dx1, g_mix):
    _, t, d = dz.shape
    tm = _tile(t, 512, SUBLANES_BF16)

    def colblk(p):
        return jnp.where(p < 4, (p + 1) % 4, p)

    def body(dz_ref, w_ref, x_ref, dx1_ref, g_ref, dx_ref, gg_ref, acc):
        i = pl.program_id(0)
        k = pl.program_id(1)
        part = _dot_tb(dz_ref[...], w_ref[...])

        @pl.when(k == 0)
        def _():
            acc[...] = part

        @pl.when(k > 0)
        def _():
            acc[...] += part

        @pl.when((i == 0) & (k == 0))
        def _():
            gg_ref[...] = jnp.zeros_like(gg_ref)

        @pl.when(k == N_SPLITS - 1)
        def _():
            dh = acc[...]
            xh, inv = _rms_fwd(x_ref[...])
            gg_ref[...] += jnp.sum(dh * xh, axis=0, keepdims=True)
            dx_ref[...] = dx1_ref[...] + _rms_bwd(dh, xh, inv, g_ref[...])

    blocks = _nbytes((tm, d), BF16) + _nbytes((d, d), BF16) + 3 * _nbytes((tm, d), F32)
    return pl.pallas_call(
        body, name="mixer_bwd_dx", grid=(t // tm, N_SPLITS),
        in_specs=[pl.BlockSpec((None, tm, d), lambda i, k: (k, i, 0)),
                  pl.BlockSpec((d, d), lambda i, k: (0, colblk(k))),
                  pl.BlockSpec((tm, d), lambda i, k: (i, 0)),
                  pl.BlockSpec((tm, d), lambda i, k: (i, 0)),
                  pl.BlockSpec((1, d), lambda i, k: (0, 0))],
        out_specs=[pl.BlockSpec((tm, d), lambda i, k: (i, 0)),
                   pl.BlockSpec((1, d), lambda i, k: (0, 0))],
        out_shape=[SDS((t, d), F32), SDS((1, d), F32)],
        scratch_shapes=[pltpu.VMEM((tm, d), F32)],
        compiler_params=_params(("arbitrary", "arbitrary"), blocks, 8 * _nbytes((tm, d), F32)),
    )(dz, w_in, x, dx1, g_mix)


N_BIG = 5


def _ds(start, size, align):
    if isinstance(start, int):
        return pl.ds(start, size)
    return pl.ds(pl.multiple_of(start, align), size)


def _piece(a, ref, k, h):
    if a == 0:
        w = ref.shape[1] // 8
        return ref.at[:, _ds((2 * k + h) * w, w, LANES)]
    if a == 1 or a == 4:
        r = ref.shape[1] // 8
        return ref.at[:, _ds((2 * k + h) * r, r, SUBLANES_BF16), :]
    if a == 2:
        r = ref.shape[0] // 2
        w = ref.shape[1] // 4
        return ref.at[_ds(h * r, r, SUBLANES_BF16), _ds(k * w, w, LANES)]
    r = ref.shape[0] // 8
    return ref.at[_ds((2 * k + h) * r, r, SUBLANES_BF16), :]


def _half(a, ref, h):
    if a == 0:
        w = ref.shape[1] // 2
        return ref.at[:, _ds(h * w, w, LANES)]
    if a == 1 or a == 4:
        r = ref.shape[1] // 2
        return ref.at[:, _ds(h * r, r, SUBLANES_BF16), :]
    r = ref.shape[0] // 2
    return ref.at[_ds(h * r, r, SUBLANES_BF16), :]


def _piece_shape(a, full_shape):
    if a == 0:
        return (full_shape[0], full_shape[1] // 8)
    if a == 1 or a == 4:
        return (full_shape[0], full_shape[1] // 8, full_shape[2])
    if a == 2:
        return (full_shape[0] // 2, full_shape[1] // 4)
    return (full_shape[0] // 8, full_shape[1])


def _shard_shape(a, full_shape):
    if a == 0 or a == 2:
        return (full_shape[0], full_shape[1] // 4)
    if a == 1 or a == 4:
        return (full_shape[0], full_shape[1] // 4, full_shape[2])
    return (full_shape[0] // 4, full_shape[1])


def _piece_block(a, full_shape):
    ps = _piece_shape(a, full_shape)
    if a == 0:
        return ps, lambda k, c: (0, 2 * k + c)
    if a == 1 or a == 4:
        return ps, lambda k, c: (0, 2 * k + c, 0)
    if a == 2:
        return ps, lambda k, c: (c, k)
    return ps, lambda k, c: (2 * k + c, 0)


def _coords():
    x = lax.axis_index("x")
    y = lax.axis_index("y")
    c = lax.axis_index("c")
    return x, y, c


ANY = pl.BlockSpec(memory_space=pl.ANY)


def _gather_weights(locs, full_shapes):
    def body(*refs):
        loc = refs[:N_BIG]
        full = refs[N_BIG:2 * N_BIG]
        lsem, ssem, rsem, fsem, frsem = refs[2 * N_BIG:]
        x, y, c = _coords()
        j = 2 * x + y
        peers = [(1 - x, y), (x, 1 - y), (1 - x, 1 - y)]
        pending = []
        for a in range(N_BIG):
            for h in range(2):
                cp = pltpu.make_async_copy(_half(a, loc[a], h), _piece(a, full[a], j, h), lsem.at[2 * a + h])
                cp.start()
                pending.append(cp)
        sends = []
        for a in range(N_BIG):
            for i, (px, py) in enumerate(peers):
                cp = pltpu.make_async_remote_copy(
                    src_ref=_half(a, loc[a], c), dst_ref=_piece(a, full[a], j, c),
                    send_sem=ssem.at[3 * a + i], recv_sem=rsem.at[3 * a + i],
                    device_id=(px, py, c), device_id_type=MESH)
                cp.start()
                sends.append(cp)
        for a in range(N_BIG):
            for i, (px, py) in enumerate(peers):
                k = 2 * px + py
                landed = _piece(a, full[a], k, c)
                pltpu.make_async_remote_copy(
                    src_ref=landed, dst_ref=landed, send_sem=ssem.at[3 * a + i], recv_sem=rsem.at[3 * a + i],
                    device_id=(px, py, c), device_id_type=MESH).wait_recv()
                cp = pltpu.make_async_remote_copy(
                    src_ref=landed, dst_ref=landed, send_sem=fsem.at[3 * a + i], recv_sem=frsem.at[3 * a + i],
                    device_id=(x, y, 1 - c), device_id_type=MESH)
                cp.start()
                sends.append(cp)
        for a in range(N_BIG):
            for i, (px, py) in enumerate(peers):
                other = _piece(a, full[a], 2 * px + py, 1 - c)
                pltpu.make_async_remote_copy(
                    src_ref=other, dst_ref=other, send_sem=fsem.at[3 * a + i], recv_sem=frsem.at[3 * a + i],
                    device_id=(x, y, 1 - c), device_id_type=MESH).wait_recv()
        for cp in sends:
            cp.wait_send()
        for cp in pending:
            cp.wait()

    return pl.pallas_call(
        body, name="gather_weights",
        in_specs=[ANY] * N_BIG, out_specs=[ANY] * N_BIG,
        out_shape=[SDS(s, BF16) for s in full_shapes],
        scratch_shapes=[pltpu.SemaphoreType.DMA((2 * N_BIG,)), pltpu.SemaphoreType.DMA((3 * N_BIG,)),
                        pltpu.SemaphoreType.DMA((3 * N_BIG,)), pltpu.SemaphoreType.DMA((3 * N_BIG,)),
                        pltpu.SemaphoreType.DMA((3 * N_BIG,))],
    )(*locs)


def _exchange_halves(gbs):
    full_shapes = [g.shape for g in gbs]

    def body(*refs):
        gb = refs[:N_BIG]
        land = refs[N_BIG:2 * N_BIG]
        ssem, rsem = refs[2 * N_BIG:]
        x, y, c = _coords()
        copies = []
        for a in range(N_BIG):
            for k in range(4):
                cp = pltpu.make_async_remote_copy(
                    src_ref=_piece(a, gb[a], k, 1 - c), dst_ref=land[a].at[k],
                    send_sem=ssem.at[4 * a + k], recv_sem=rsem.at[4 * a + k],
                    device_id=(x, y, 1 - c), device_id_type=MESH)
                cp.start()
                copies.append(cp)
        for cp in copies:
            cp.wait()

    return pl.pallas_call(
        body, name="exchange_halves",
        in_specs=[ANY] * N_BIG, out_specs=[ANY] * N_BIG,
        out_shape=[SDS((4,) + _piece_shape(a, full_shapes[a]), BF16) for a in range(N_BIG)],
        scratch_shapes=[pltpu.SemaphoreType.DMA((4 * N_BIG,)), pltpu.SemaphoreType.DMA((4 * N_BIG,))],
    )(*gbs)


def _add_halves(gbs, lands, c_arr):
    full_shapes = [g.shape for g in gbs]

    def body(c_ref, *refs):
        del c_ref
        g = refs[:N_BIG]
        l = refs[N_BIG:2 * N_BIG]
        o = refs[2 * N_BIG:]
        for a in range(N_BIG):
            o[a][...] = (g[a][...].astype(F32) + l[a][...].astype(F32)).astype(BF16)

    g_specs, l_specs, o_specs, blocks = [], [], [], 0
    for a in range(N_BIG):
        bs, imap = _piece_block(a, full_shapes[a])
        g_specs.append(pl.BlockSpec(bs, lambda k, c_ref, imap=imap: imap(k, c_ref[0])))
        nd = len(bs)
        l_specs.append(pl.BlockSpec((None,) + bs, lambda k, c_ref, nd=nd: (k,) + (0,) * nd))
        o_specs.append(pl.BlockSpec((None,) + bs, lambda k, c_ref, nd=nd: (k,) + (0,) * nd))
        blocks += 3 * _nbytes(bs, BF16)
    return pl.pallas_call(
        body, name="add_halves",
        grid_spec=pltpu.PrefetchScalarGridSpec(
            num_scalar_prefetch=1, grid=(4,), in_specs=g_specs + l_specs, out_specs=o_specs),
        out_shape=[SDS((4,) + _piece_shape(a, full_shapes[a]), BF16) for a in range(N_BIG)],
        compiler_params=_params(("parallel",), blocks, blocks),
    )(c_arr, *gbs, *lands)


def _exchange_chips(ps):
    def body(*refs):
        p = refs[:N_BIG]
        land = refs[N_BIG:2 * N_BIG]
        lsem, ssem, rsem = refs[2 * N_BIG:]
        x, y, c = _coords()
        j = 2 * x + y
        peers = [(1 - x, y), (x, 1 - y), (1 - x, 1 - y)]
        local, copies = [], []
        for a in range(N_BIG):
            cp = pltpu.make_async_copy(p[a].at[j], land[a].at[j], lsem.at[a])
            cp.start()
            local.append(cp)
        for a in range(N_BIG):
            for i, (px, py) in enumerate(peers):
                cp = pltpu.make_async_remote_copy(
                    src_ref=p[a].at[2 * px + py], dst_ref=land[a].at[j],
                    send_sem=ssem.at[3 * a + i], recv_sem=rsem.at[3 * a + i],
                    device_id=(px, py, c), device_id_type=MESH)
                cp.start()
                copies.append(cp)
        for cp in copies:
            cp.wait()
        for cp in local:
            cp.wait()

    return pl.pallas_call(
        body, name="exchange_chips",
        in_specs=[ANY] * N_BIG, out_specs=[ANY] * N_BIG,
        out_shape=[SDS(q.shape, BF16) for q in ps],
        scratch_shapes=[pltpu.SemaphoreType.DMA((N_BIG,)), pltpu.SemaphoreType.DMA((3 * N_BIG,)),
                        pltpu.SemaphoreType.DMA((3 * N_BIG,))],
    )(*ps)


def _sum_chips(land, name):
    ps = land.shape[1:]
    rows_axis = len(ps) - 2
    rows = ps[rows_axis]
    nsub = 2 if rows % (2 * SUBLANES_BF16) == 0 else 1
    bs = tuple(r // nsub if ax == rows_axis else r for ax, r in enumerate(ps))

    def imap(s):
        return tuple(s if ax == rows_axis else 0 for ax in range(len(ps)))

    def body(l_ref, o_ref):
        acc = l_ref[0].astype(F32) + l_ref[1].astype(F32)
        acc = acc + l_ref[2].astype(F32)
        o_ref[...] = acc + l_ref[3].astype(F32)

    blocks = 4 * _nbytes(bs, BF16) + _nbytes(bs, F32)
    return pl.pallas_call(
        body, name=name, grid=(nsub,),
        in_specs=[pl.BlockSpec((4,) + bs, lambda s: (0,) + imap(s))],
        out_specs=pl.BlockSpec(bs, imap),
        out_shape=SDS(ps, F32),
        compiler_params=_params(("parallel",), blocks, 2 * _nbytes(bs, F32)),
    )(land)


def _exchange_result(rs, full_shapes):
    def body(*refs):
        r = refs[:N_BIG]
        out = refs[N_BIG:2 * N_BIG]
        lsem, ssem, rsem = refs[2 * N_BIG:]
        x, y, c = _coords()
        local, copies = [], []
        for a in range(N_BIG):
            cp = pltpu.make_async_copy(r[a], _half(a, out[a], c), lsem.at[a])
            cp.start()
            local.append(cp)
            rc = pltpu.make_async_remote_copy(
                src_ref=r[a], dst_ref=_half(a, out[a], c), send_sem=ssem.at[a], recv_sem=rsem.at[a],
                device_id=(x, y, 1 - c), device_id_type=MESH)
            rc.start()
            copies.append(rc)
        for cp in copies:
            cp.wait()
        for cp in local:
            cp.wait()

    return pl.pallas_call(
        body, name="exchange_result",
        in_specs=[ANY] * N_BIG, out_specs=[ANY] * N_BIG,
        out_shape=[SDS(_shard_shape(a, full_shapes[a]), F32) for a in range(N_BIG)],
        scratch_shapes=[pltpu.SemaphoreType.DMA((N_BIG,)), pltpu.SemaphoreType.DMA((N_BIG,)),
                        pltpu.SemaphoreType.DMA((N_BIG,))],
    )(*rs)


def _allreduce_small(v, name):
    rows = v.shape[0]

    def body(v_ref, o_ref, slots, lsem, ssem, rsem):
        x, y, c = _coords()
        me = 4 * x + 2 * y + c
        own = pltpu.make_async_copy(v_ref, slots.at[me], lsem)
        own.start()
        copies = []
        for dlt in range(1, 8):
            px = 1 - x if (dlt >> 2) & 1 else x
            py = 1 - y if (dlt >> 1) & 1 else y
            pc = 1 - c if dlt & 1 else c
            cp = pltpu.make_async_remote_copy(
                src_ref=v_ref, dst_ref=slots.at[me], send_sem=ssem.at[dlt - 1], recv_sem=rsem.at[dlt - 1],
                device_id=(px, py, pc), device_id_type=MESH)
            cp.start()
            copies.append(cp)
        for cp in copies:
            cp.wait()
        own.wait()
        acc = slots[0]
        for i in range(1, 8):
            acc = acc + slots[i]
        o_ref[...] = acc

    return pl.pallas_call(
        body, name=name,
        in_specs=[pl.BlockSpec(memory_space=pltpu.VMEM)],
        out_specs=pl.BlockSpec(memory_space=pltpu.VMEM),
        out_shape=SDS((rows, LANES), F32),
        scratch_shapes=[pltpu.VMEM((8, rows, LANES), F32), pltpu.SemaphoreType.DMA,
                        pltpu.SemaphoreType.DMA((7,)), pltpu.SemaphoreType.DMA((7,))],
    )(v)


def _adamw(w, g, m, v, name, g_plane=None):
    rows, cols = w.shape
    tr = _tile(rows, max(SUBLANES_F32, (256 * 1024 // cols) // SUBLANES_F32 * SUBLANES_F32), SUBLANES_F32)

    def body(w_ref, g_ref, m_ref, v_ref, d_ref, mo_ref, vo_ref):
        gr = g_ref[...]
        mn = ADAM_B1 * m_ref[...] + (1.0 - ADAM_B1) * gr
        vn = ADAM_B2 * v_ref[...] + (1.0 - ADAM_B2) * (gr * gr)
        m_hat = mn / (1.0 - ADAM_B1 ** ADAM_STEP)
        v_hat = vn / (1.0 - ADAM_B2 ** ADAM_STEP)
        d_ref[...] = -ADAM_LR * (m_hat / (jnp.sqrt(v_hat) + ADAM_EPS) + ADAM_WD * w_ref[...])
        mo_ref[...] = mn
        vo_ref[...] = vn

    spec = pl.BlockSpec((tr, cols), lambda i: (i, 0))
    g_spec = spec if g_plane is None else pl.BlockSpec((None, tr, cols), lambda i: (g_plane, i, 0))
    return pl.pallas_call(
        body, name=name, grid=(rows // tr,),
        in_specs=[spec, g_spec, spec, spec], out_specs=[spec, spec, spec],
        out_shape=[SDS((rows, cols), F32)] * 3,
        compiler_params=_params(("parallel",), 7 * _nbytes((tr, cols), F32), 4 * _nbytes((tr, cols), F32)),
    )(w, g, m, v)


def _pack(parts):
    rows = []
    for p in parts:
        r = p.reshape(-1, LANES)
        pad = (-r.shape[0]) % SUBLANES_F32
        if pad:
            r = jnp.pad(r, ((0, pad), (0, 0)))
        rows.append(r)
    return jnp.concatenate(rows, axis=0)


def _unpack(packed, shapes):
    out, at = [], 0
    for s in shapes:
        n = 1
        for q in s:
            n *= q
        r = n // LANES
        out.append(packed[at:at + r].reshape(s))
        at += r + (-r) % SUBLANES_F32
    return out


def kernel(x, norm_mix, w_in, pool_w, pool_scale, w_pool_proj, conv_w, w_conv_out, w_o, norm_ffn, w_up, ffn_conv_w, ffn_conv_b, w_down, norm_final, loss_target, m_norm_mix, m_w_in, m_pool_w, m_pool_scale, m_w_pool_proj, m_conv_w, m_w_conv_out, m_w_o, m_norm_ffn, m_w_up, m_ffn_conv_w, m_ffn_conv_b, m_w_down, m_norm_final, v_norm_mix, v_w_in, v_pool_w, v_pool_scale, v_w_pool_proj, v_conv_w, v_w_conv_out, v_w_o, v_norm_ffn, v_w_up, v_ffn_conv_w, v_ffn_conv_b, v_w_down, v_norm_final):
    nseq, seq, d = x.shape
    t = nseq * seq
    f = w_down.shape[1] * 4
    c = d // N_GROUPS
    xy = lax.axis_index("x") * 2 + lax.axis_index("y")
    c_arr = lax.axis_index("c").astype(jnp.int32).reshape(1)

    locs = [w_in[0].astype(BF16),
            jnp.stack([w_pool_proj[0], w_conv_out[0], w_o[0]]).astype(BF16),
            w_up[0].astype(BF16), w_down[0].astype(BF16), pool_w[0].astype(BF16)]
    full_shapes = [(d, N_SPLITS * d), (3, d, d), (d, 2 * f), (f, d), (N_GROUPS, c, c)]
    w_in_f, w3_f, w_up_f, w_down_f, pool_w_f = _gather_weights(locs, full_shapes)

    zero = jnp.zeros((), jnp.int32)
    cw_pad = lax.dynamic_update_slice(jnp.zeros((3, d), F32), conv_w[0], (zero, xy * (d // 4)))
    fw_pad = lax.dynamic_update_slice(jnp.zeros((3, 2 * f), F32), ffn_conv_w[0], (zero, xy * (f // 2)))
    small_w = _pack([cw_pad, fw_pad]) * 0.5
    conv_w_f, ffn_cw_f = _unpack(_allreduce_small(small_w, "gather_small"), [(3, d), (3, 2 * f)])
    ffn_cw_p = ffn_cw_f.reshape(3, 2, f).transpose(1, 0, 2)
    ffn_cb_p = ffn_conv_b.reshape(2, 1, f)

    x2d = x.reshape(t, d)
    tgt = loss_target.reshape(t, d)
    z, h1 = _fwd_in(x2d, norm_mix, w_in_f)
    lhs3 = _mixer_mid_fwd(z, pool_w_f, pool_scale, conv_w_f, nseq)
    lhs3, ypc, x1, h2 = _mixer_out(lhs3, z, x2d, w3_f, norm_ffn)
    u0 = _ffn_up(h2, w_up_f, f)
    act = _ffn_mid_fwd(u0, ffn_cw_p, ffn_cb_p, nseq)
    dx2, dx2b, loss11, g_norm_final = _ffn_down_loss(act, w_down_f, x1, tgt, norm_final.reshape(1, d))

    da = _ffn_bwd_da(dx2b, w_down_f)
    du0, g_ffn_cw_p, g_ffn_cb_p = _ffn_mid_bwd(da, u0, ffn_cw_p, ffn_cb_p, nseq)
    tn_up = _tile(f, 1408, LANES)
    npp = f // tn_up
    g_w_down = _wgrad(act, dx2b, "wgrad_down", tr=tn_up, tn=d)
    g_w_up = _wgrad(h2, du0, "wgrad_up", tr=d, tn=tn_up, b_plane_of=lambda n: (n // npp, n % npp))
    dx1, rhs3, g_norm_ffn = _ffn_bwd_dx1(du0, w_up_f, x1, dx2, norm_ffn, 3)
    rhs3, dz, dpq = _mixer_bwd(rhs3, z, ypc, w3_f)
    g_w3 = _wgrad3(lhs3, rhs3)
    dz, g_conv_w = _conv_bwd(dz, dpq, z, conv_w_f, nseq)
    dz, g_pool_w, g_pool_scale = _pool_bwd_call(dz, dpq, z, pool_w_f, pool_scale, nseq)
    g_w_in = _wgrad_in(h1, dz)
    grad_x, g_norm_mix = _mixer_bwd_dx(dz, w_in_f, x2d, dx1, norm_mix)

    gbs = [g_w_in, g_w3, g_w_up, g_w_down, g_pool_w.astype(BF16)]
    lands = _exchange_halves(gbs)
    ps = _add_halves(gbs, lands, c_arr)
    lands2 = _exchange_chips(ps)
    rs = [_sum_chips(lands2[a], "sum_chips_%d" % a) for a in range(N_BIG)]
    g_in_s, g3_s, g_up_s, g_down_s, g_pool_s = _exchange_result(rs, full_shapes)

    g_ffn_cw = g_ffn_cw_p.transpose(1, 0, 2).reshape(3, 2 * f)
    small_shapes = [(1, d), (1, d), (1, d), (1, 2 * f), (d,), (3, d), (3, 2 * f)]
    small = _allreduce_small(_pack([g_norm_mix, g_pool_scale, g_norm_ffn, g_ffn_cb_p.reshape(1, 2 * f),
                                    g_norm_final.reshape(d), g_conv_w, g_ffn_cw]), "allreduce_small")
    gs_norm_mix, gs_pool_scale, gs_norm_ffn, gs_ffn_cb, gs_norm_final, gs_conv_w, gs_ffn_cw = _unpack(small, small_shapes)
    gs_conv_w = lax.dynamic_slice(gs_conv_w, (zero, xy * (d // 4)), (3, d // 4))
    gs_ffn_cw = lax.dynamic_slice(gs_ffn_cw, (zero, xy * (f // 2)), (3, f // 2))

    def upd(w, g, m, v, name, g_plane=None):
        shape = w.shape
        rows = 1
        for q in shape[:-1]:
            rows *= q
        g2 = g if g_plane is not None else g.reshape(rows, shape[-1])
        dlt, mn, vn = _adamw(w.reshape(rows, shape[-1]), g2, m.reshape(rows, shape[-1]), v.reshape(rows, shape[-1]),
                             name, g_plane)
        return dlt.reshape(shape), mn.reshape(shape), vn.reshape(shape)

    big = {
        "w_in": (g_in_s.reshape(w_in.shape), upd(w_in, g_in_s, m_w_in, v_w_in, "adamw_w_in")),
        "pool_w": (g_pool_s.reshape(pool_w.shape), upd(pool_w, g_pool_s, m_pool_w, v_pool_w, "adamw_pool_w")),
        "w_pool_proj": (g3_s[0][None], upd(w_pool_proj, g3_s, m_w_pool_proj, v_w_pool_proj, "adamw_w_pool_proj", 0)),
        "w_conv_out": (g3_s[1][None], upd(w_conv_out, g3_s, m_w_conv_out, v_w_conv_out, "adamw_w_conv_out", 1)),
        "w_o": (g3_s[2][None], upd(w_o, g3_s, m_w_o, v_w_o, "adamw_w_o", 2)),
        "w_up": (g_up_s.reshape(w_up.shape), upd(w_up, g_up_s, m_w_up, v_w_up, "adamw_w_up")),
        "w_down": (g_down_s.reshape(w_down.shape), upd(w_down, g_down_s, m_w_down, v_w_down, "adamw_w_down")),
    }

    small_names = ["norm_mix", "pool_scale", "norm_ffn", "ffn_conv_b", "norm_final", "conv_w", "ffn_conv_w"]
    small_ws = [norm_mix, pool_scale, norm_ffn, ffn_conv_b, norm_final, conv_w, ffn_conv_w]
    small_ms = [m_norm_mix, m_pool_scale, m_norm_ffn, m_ffn_conv_b, m_norm_final, m_conv_w, m_ffn_conv_w]
    small_vs = [v_norm_mix, v_pool_scale, v_norm_ffn, v_ffn_conv_b, v_norm_final, v_conv_w, v_ffn_conv_w]
    small_gs = [gs_norm_mix, gs_pool_scale, gs_norm_ffn, gs_ffn_cb, gs_norm_final, gs_conv_w, gs_ffn_cw]
    sd, sm, sv = _adamw(_pack(small_ws), _pack(small_gs), _pack(small_ms), _pack(small_vs), "adamw_small")
    shapes = [w.shape for w in small_ws]
    sd, sm, sv = _unpack(sd, shapes), _unpack(sm, shapes), _unpack(sv, shapes)
    res = dict(big)
    for i, nm in enumerate(small_names):
        res[nm] = (small_gs[i].reshape(shapes[i]), (sd[i], sm[i], sv[i]))

    loss = lax.psum(loss11[0, 0], ("x", "y", "c"))
    order = ["norm_mix", "w_in", "pool_w", "pool_scale", "w_pool_proj", "conv_w", "w_conv_out", "w_o", "norm_ffn",
             "w_up", "ffn_conv_w", "ffn_conv_b", "w_down", "norm_final"]
    return (loss, grad_x.reshape(x.shape), *[res[n][0] for n in order], *[res[n][1][0] for n in order],
            *[res[n][1][1] for n in order], *[res[n][1][2] for n in order])
```

```python
import math

import jax
import jax.numpy as jnp
from jax import lax
from jax.experimental import pallas as pl
from jax.experimental.pallas import tpu as pltpu

F32 = jnp.float32
BF16 = jnp.bfloat16
SDS = jax.ShapeDtypeStruct
MESH = pl.DeviceIdType.MESH

RMS_EPS = 1e-6
POOL_WINDOWS = (2, 4, 8, 16)
N_GROUPS = len(POOL_WINDOWS)
N_SPLITS = 6

ADAM_LR = 0.001
ADAM_B1 = 0.9
ADAM_B2 = 0.999
ADAM_EPS = 1e-08
ADAM_WD = 0.01
ADAM_STEP = 10

LANES = 128
SUBLANES_F32 = 8
SUBLANES_BF16 = 16
VMEM_BYTES = 64 * 1024 * 1024
VMEM_CAP = VMEM_BYTES - 8 * 1024 * 1024
VMEM_FLOOR = 16 * 1024 * 1024


def _tile(dim, pref, align):
    if dim <= pref:
        return dim
    t = (pref // align) * align
    while t >= align:
        if dim % t == 0:
            return t
        t -= align
    return dim


def _nbytes(shape, dtype):
    n = 1
    for s in shape:
        n *= s
    return n * jnp.dtype(dtype).itemsize


def _params(sem, block_bytes, temp_bytes=0):
    need = 2 * block_bytes + temp_bytes + 4 * 1024 * 1024
    return pltpu.CompilerParams(dimension_semantics=sem, vmem_limit_bytes=int(min(max(need, VMEM_FLOOR), VMEM_CAP)))


def _dot(a, b):
    return jnp.dot(a, b, preferred_element_type=F32)


def _dot_tb(a, b):
    return lax.dot_general(a, b, (((1,), (1,)), ((), ())), preferred_element_type=F32)


def _dot_ta(a, b):
    return lax.dot_general(a, b, (((0,), (0,)), ((), ())), preferred_element_type=F32)


def _rms_fwd(x):
    inv = lax.rsqrt(jnp.mean(x * x, axis=-1, keepdims=True) + RMS_EPS)
    return x * inv, inv


def _rms_bwd(dy, xhat, inv, g):
    gd = dy * g
    return inv * (gd - xhat * jnp.mean(gd * xhat, axis=-1, keepdims=True))


def _sigmoid(x):
    return 1.0 / (1.0 + jnp.exp(-x))


def _shift_down(x, k, row):
    return jnp.where(row >= k, pltpu.roll(x, k, 0), 0.0)


def _shift_up(x, k, row):
    s = x.shape[0]
    return jnp.where(row < s - k, pltpu.roll(x, s - k, 0), 0.0)


def _pool_fwd(u, win, row):
    s = u
    k = 1
    while k < win:
        s = s + _shift_down(s, k, row)
        k *= 2
    cnt = jnp.minimum(row + 1, win).astype(F32)
    return s / cnt - u


def _pool_bwd(dp, win, row):
    cnt = jnp.minimum(row + 1, win).astype(F32)
    s = dp / cnt
    k = 1
    while k < win:
        s = s + _shift_up(s, k, row)
        k *= 2
    return s - dp


def _fwd_in(x, g, w):
    t, d = x.shape
    nsh, _, ws = w.shape
    n = nsh * ws
    tm = _tile(t, 1024, SUBLANES_BF16)
    tn = _tile(ws, 1536, LANES)
    nps = ws // tn

    def body(x_ref, g_ref, w_ref, z_ref, h_ref, hs):
        @pl.when(pl.program_id(1) == 0)
        def _():
            xh, _ = _rms_fwd(x_ref[...])
            h = (xh * g_ref[...]).astype(BF16)
            hs[...] = h
            h_ref[...] = h

        z_ref[...] = _dot(hs[...], w_ref[...]).astype(BF16)

    blocks = _nbytes((tm, d), F32) + _nbytes((d, tn), BF16) + _nbytes((tm, tn), BF16) + _nbytes((tm, d), BF16)
    return pl.pallas_call(
        body, name="fwd_in", grid=(t // tm, n // tn),
        in_specs=[pl.BlockSpec((tm, d), lambda i, j: (i, 0)), pl.BlockSpec((1, d), lambda i, j: (0, 0)),
                  pl.BlockSpec((None, d, tn), lambda i, j: (j // nps, 0, j % nps))],
        out_specs=[pl.BlockSpec((tm, tn), lambda i, j: (i, j)), pl.BlockSpec((tm, d), lambda i, j: (i, 0))],
        out_shape=[SDS((t, n), BF16), SDS((t, d), BF16)],
        scratch_shapes=[pltpu.VMEM((tm, d), BF16)],
        compiler_params=_params(("parallel", "arbitrary"), blocks, 3 * _nbytes((tm, d), F32)),
    )(x, g, w)


def _mixer_mid_fwd(z, pool_w, pool_scale, conv_w, nseq):
    t = z.shape[0]
    d = pool_scale.shape[1]
    s = t // nseq
    c = d // N_GROUPS

    def body(zp, zb, zc, zv, pw, ps, cw, o):
        j = pl.program_id(1)
        row = lax.broadcasted_iota(jnp.int32, (s, c), 0)
        for gi, win in enumerate(POOL_WINDOWS):
            @pl.when(j == gi)
            def _(win=win):
                pooled = _pool_fwd(zp[...].astype(F32), win, row)
                o[0] = (_dot(pooled.astype(BF16), pw[...]) * ps[...]).astype(BF16)

        cv = zc[...].astype(F32) * zv[...].astype(F32)
        cc = (cw[pl.ds(2, 1), :] * cv + cw[pl.ds(1, 1), :] * _shift_down(cv, 1, row)
              + cw[pl.ds(0, 1), :] * _shift_down(cv, 2, row))
        o[1] = (zb[...].astype(F32) * cc).astype(BF16)

    blocks = 4 * _nbytes((s, c), BF16) + _nbytes((c, c), BF16) + _nbytes((2, s, c), BF16)
    return pl.pallas_call(
        body, name="mixer_mid_fwd", grid=(nseq, N_GROUPS),
        in_specs=[pl.BlockSpec((s, c), lambda b, j: (b, j)),
                  pl.BlockSpec((s, c), lambda b, j: (b, N_GROUPS + j)),
                  pl.BlockSpec((s, c), lambda b, j: (b, 2 * N_GROUPS + j)),
                  pl.BlockSpec((s, c), lambda b, j: (b, 3 * N_GROUPS + j)),
                  pl.BlockSpec((None, c, c), lambda b, j: (j, 0, 0)),
                  pl.BlockSpec((1, c), lambda b, j: (0, j)),
                  pl.BlockSpec((3, c), lambda b, j: (0, j))],
        out_specs=pl.BlockSpec((2, s, c), lambda b, j: (0, b, j)),
        out_shape=SDS((3, t, d), BF16),
        compiler_params=_params(("parallel", "parallel"), blocks, 8 * _nbytes((s, c), F32)),
    )(z, z, z, z, pool_w, pool_scale, conv_w)


def _mixer_out(lhs3, z, x, w3, g_ffn):
    t, d = x.shape
    tm = _tile(t, 256, SUBLANES_BF16)

    def body(pq, zgp, zgc, x_ref, w_ref, g_ref, mrg, ypc, x1o, h2o):
        yp = _dot(pq[0], w_ref[0])
        yc = _dot(pq[1], w_ref[1])
        m = _sigmoid(zgp[...].astype(F32)) * yp + _sigmoid(zgc[...].astype(F32)) * yc
        mb = m.astype(BF16)
        x1 = x_ref[...] + _dot(mb, w_ref[2])
        ypc[0] = yp.astype(BF16)
        ypc[1] = yc.astype(BF16)
        mrg[...] = mb
        x1o[...] = x1
        xh, _ = _rms_fwd(x1)
        h2o[...] = (xh * g_ref[...]).astype(BF16)

    blocks = (_nbytes((2, tm, d), BF16) * 2 + _nbytes((tm, d), BF16) * 4 + _nbytes((tm, d), F32) * 2
              + _nbytes((3, d, d), BF16))
    return pl.pallas_call(
        body, name="mixer_out", grid=(t // tm,),
        in_specs=[pl.BlockSpec((2, tm, d), lambda i: (0, i, 0)),
                  pl.BlockSpec((tm, d), lambda i: (i, 4)),
                  pl.BlockSpec((tm, d), lambda i: (i, 5)),
                  pl.BlockSpec((tm, d), lambda i: (i, 0)),
                  pl.BlockSpec((3, d, d), lambda i: (0, 0, 0)),
                  pl.BlockSpec((1, d), lambda i: (0, 0))],
        out_specs=[pl.BlockSpec((None, tm, d), lambda i: (2, i, 0)),
                   pl.BlockSpec((2, tm, d), lambda i: (0, i, 0)),
                   pl.BlockSpec((tm, d), lambda i: (i, 0)),
                   pl.BlockSpec((tm, d), lambda i: (i, 0))],
        out_shape=[SDS(lhs3.shape, BF16), SDS((2, t, d), BF16), SDS((t, d), F32), SDS((t, d), BF16)],
        input_output_aliases={0: 0},
        compiler_params=_params(("parallel",), blocks, 8 * _nbytes((tm, d), F32)),
    )(lhs3, z, z, x, w3, g_ffn)


def _ffn_up(h2, w_up, f):
    t, d = h2.shape
    nsh, _, ws = w_up.shape
    tm = _tile(t, 1024, SUBLANES_BF16)
    tn = _tile(ws, 1408, LANES)
    nps = ws // tn
    npp = f // tn

    def body(h_ref, w_ref, o_ref):
        o_ref[...] = _dot(h_ref[...], w_ref[...]).astype(BF16)

    blocks = _nbytes((tm, d), BF16) + _nbytes((d, tn), BF16) + _nbytes((tm, tn), BF16)
    return pl.pallas_call(
        body, name="ffn_up", grid=(t // tm, 2 * npp),
        in_specs=[pl.BlockSpec((tm, d), lambda i, j: (i, 0)),
                  pl.BlockSpec((None, d, tn), lambda i, j: (j // nps, 0, j % nps))],
        out_specs=pl.BlockSpec((None, tm, tn), lambda i, j: (j // npp, i, j % npp)),
        out_shape=SDS((2, t, f), BF16),
        compiler_params=_params(("parallel", "parallel"), blocks, _nbytes((tm, tn), F32)),
    )(h2, w_up)


def _conv3_rows(u, u1, u2, w_ref, p):
    return w_ref[p, pl.ds(2, 1), :] * u + w_ref[p, pl.ds(1, 1), :] * u1 + w_ref[p, pl.ds(0, 1), :] * u2


def _ffn_mid_fwd(u0, cw, cb, nseq):
    _, t, f = u0.shape
    s = t // nseq
    c = _tile(f, 256, LANES)

    def body(u_ref, w_ref, b_ref, a_ref):
        row = lax.broadcasted_iota(jnp.int32, (s, c), 0)
        act = []
        for p in range(2):
            u = u_ref[p].astype(F32)
            act.append(_conv3_rows(u, _shift_down(u, 1, row), _shift_down(u, 2, row), w_ref, p) + b_ref[p])
        ug, uv = act
        a_ref[...] = (ug * _sigmoid(ug) * uv).astype(BF16)

    blocks = _nbytes((2, s, c), BF16) + _nbytes((s, c), BF16)
    return pl.pallas_call(
        body, name="ffn_mid_fwd", grid=(f // c, nseq),
        in_specs=[pl.BlockSpec((2, s, c), lambda j, b: (0, b, j)),
                  pl.BlockSpec((2, 3, c), lambda j, b: (0, 0, j)),
                  pl.BlockSpec((2, 1, c), lambda j, b: (0, 0, j))],
        out_specs=pl.BlockSpec((s, c), lambda j, b: (b, j)),
        out_shape=SDS((t, f), BF16),
        compiler_params=_params(("parallel", "parallel"), blocks, 8 * _nbytes((s, c), F32)),
    )(u0, cw, cb)


def _ffn_down_loss(a, w_down, x1, tgt, g_fin):
    t, f = a.shape
    d = x1.shape[1]
    tm = _tile(t, 256, SUBLANES_BF16)
    nsteps = t // tm

    def body(a_ref, w_ref, x1_ref, t_ref, g_ref, dx_ref, dxb_ref, loss_ref, gg_ref, lacc):
        i = pl.program_id(0)

        @pl.when(i == 0)
        def _():
            lacc[...] = jnp.zeros_like(lacc)
            gg_ref[...] = jnp.zeros_like(gg_ref)

        x2 = x1_ref[...] + _dot(a_ref[...], w_ref[...])
        xh, inv = _rms_fwd(x2)
        g = g_ref[...]
        e = xh * g - t_ref[...]
        lacc[...] += jnp.sum(e * e, axis=0, keepdims=True)
        dy = e * (1.0 / d)
        gg_ref[...] += jnp.sum(dy * xh, axis=0, keepdims=True)
        dx2 = _rms_bwd(dy, xh, inv, g)
        dx_ref[...] = dx2
        dxb_ref[...] = dx2.astype(BF16)

        @pl.when(i == nsteps - 1)
        def _():
            loss_ref[...] = jnp.sum(lacc[...], axis=1, keepdims=True) * (0.5 / d)

    blocks = (_nbytes((tm, f), BF16) + _nbytes((f, d), BF16) + 3 * _nbytes((tm, d), F32) + _nbytes((tm, d), BF16))
    return pl.pallas_call(
        body, name="ffn_down_loss", grid=(nsteps,),
        in_specs=[pl.BlockSpec((tm, f), lambda i: (i, 0)), pl.BlockSpec((f, d), lambda i: (0, 0)),
                  pl.BlockSpec((tm, d), lambda i: (i, 0)), pl.BlockSpec((tm, d), lambda i: (i, 0)),
                  pl.BlockSpec((1, d), lambda i: (0, 0))],
        out_specs=[pl.BlockSpec((tm, d), lambda i: (i, 0)), pl.BlockSpec((tm, d), lambda i: (i, 0)),
                   pl.BlockSpec((1, 1), lambda i: (0, 0)), pl.BlockSpec((1, d), lambda i: (0, 0))],
        out_shape=[SDS((t, d), F32), SDS((t, d), BF16), SDS((1, 1), F32), SDS((1, d), F32)],
        scratch_shapes=[pltpu.VMEM((1, d), F32)],
        compiler_params=_params(("arbitrary",), blocks, 8 * _nbytes((tm, d), F32)),
    )(a, w_down, x1, tgt, g_fin)


def _ffn_bwd_da(dxb, w_down):
    t, d = dxb.shape
    f = w_down.shape[0]
    tm = _tile(t, 1024, SUBLANES_BF16)
    tn = _tile(f, 1408, LANES)

    def body(x_ref, w_ref, o_ref):
        o_ref[...] = _dot_tb(x_ref[...], w_ref[...]).astype(BF16)

    blocks = _nbytes((tm, d), BF16) + _nbytes((tn, d), BF16) + _nbytes((tm, tn), BF16)
    return pl.pallas_call(
        body, name="ffn_bwd_da", grid=(t // tm, f // tn),
        in_specs=[pl.BlockSpec((tm, d), lambda i, j: (i, 0)), pl.BlockSpec((tn, d), lambda i, j: (j, 0))],
        out_specs=pl.BlockSpec((tm, tn), lambda i, j: (i, j)),
        out_shape=SDS((t, f), BF16),
        compiler_params=_params(("parallel", "parallel"), blocks, _nbytes((tm, tn), F32)),
    )(dxb, w_down)


def _ffn_mid_bwd(da, u0, cw, cb, nseq):
    _, t, f = u0.shape
    s = t // nseq
    c = _tile(f, 128, LANES)

    def body(da_ref, u_ref, w_ref, b_ref, du_ref, gw_ref, gb_ref):
        @pl.when(pl.program_id(1) == 0)
        def _():
            gw_ref[...] = jnp.zeros_like(gw_ref)
            gb_ref[...] = jnp.zeros_like(gb_ref)

        row = lax.broadcasted_iota(jnp.int32, (s, c), 0)
        us, act = [], []
        for p in range(2):
            u = u_ref[p].astype(F32)
            u1 = _shift_down(u, 1, row)
            u2 = _shift_down(u, 2, row)
            us.append((u, u1, u2))
            act.append(_conv3_rows(u, u1, u2, w_ref, p) + b_ref[p])
        ug, uv = act
        sg = _sigmoid(ug)
        dacc = da_ref[...].astype(F32)
        dug = dacc * uv * sg * (1.0 + ug * (1.0 - sg))
        duv = dacc * (ug * sg)
        for p, du in ((0, dug), (1, duv)):
            u, u1, u2 = us[p]
            gb_ref[p] += jnp.sum(du, axis=0, keepdims=True)
            gw_ref[p, pl.ds(0, 1), :] += jnp.sum(du * u2, axis=0, keepdims=True)
            gw_ref[p, pl.ds(1, 1), :] += jnp.sum(du * u1, axis=0, keepdims=True)
            gw_ref[p, pl.ds(2, 1), :] += jnp.sum(du * u, axis=0, keepdims=True)
            du_ref[p] = _conv3_rows(du, _shift_up(du, 1, row), _shift_up(du, 2, row), w_ref, p).astype(BF16)

    blocks = _nbytes((s, c), BF16) + 2 * _nbytes((2, s, c), BF16)
    return pl.pallas_call(
        body, name="ffn_mid_bwd", grid=(f // c, nseq),
        in_specs=[pl.BlockSpec((s, c), lambda j, b: (b, j)),
                  pl.BlockSpec((2, s, c), lambda j, b: (0, b, j)),
                  pl.BlockSpec((2, 3, c), lambda j, b: (0, 0, j)),
                  pl.BlockSpec((2, 1, c), lambda j, b: (0, 0, j))],
        out_specs=[pl.BlockSpec((2, s, c), lambda j, b: (0, b, j)),
                   pl.BlockSpec((2, 3, c), lambda j, b: (0, 0, j)),
                   pl.BlockSpec((2, 1, c), lambda j, b: (0, 0, j))],
        out_shape=[SDS((2, t, f), BF16), SDS((2, 3, f), F32), SDS((2, 1, f), F32)],
        compiler_params=_params(("parallel", "arbitrary"), blocks, 20 * _nbytes((s, c), F32)),
    )(da, u0, cw, cb)


def _wgrad(a, b, name, *, tr, tn, b_plane_of=None, out_shards=None):
    t, m = a.shape
    n_total = b.shape[-1] * (b.shape[0] if b.ndim == 3 else 1)
    tk = _tile(t, 1024, SUBLANES_BF16)
    nk = t // tk

    def body(a_ref, b_ref, o_ref, acc):
        k = pl.program_id(2)
        part = _dot_ta(a_ref[...], b_ref[...])

        @pl.when(k == 0)
        def _():
            acc[...] = part

        @pl.when(k > 0)
        def _():
            acc[...] += part

        @pl.when(k == nk - 1)
        def _():
            o_ref[...] = acc[...].astype(BF16)

    if b.ndim == 3:
        b_spec = pl.BlockSpec((None, tk, tn), lambda r, n, k: (b_plane_of(n)[0], k, b_plane_of(n)[1]))
    else:
        b_spec = pl.BlockSpec((tk, tn), lambda r, n, k: (k, n))
    if out_shards is None:
        o_spec = pl.BlockSpec((tr, tn), lambda r, n, k: (r, n))
        o_shape = SDS((m, n_total), BF16)
    else:
        nps = n_total // out_shards // tn
        o_spec = pl.BlockSpec((None, tr, tn), lambda r, n, k: (n // nps, r, n % nps))
        o_shape = SDS((out_shards, m, n_total // out_shards), BF16)
    blocks = _nbytes((tk, tr), BF16) + _nbytes((tk, tn), BF16) + _nbytes((tr, tn), BF16)
    return pl.pallas_call(
        body, name=name, grid=(m // tr, n_total // tn, nk),
        in_specs=[pl.BlockSpec((tk, tr), lambda r, n, k: (k, r)), b_spec],
        out_specs=o_spec,
        out_shape=o_shape,
        scratch_shapes=[pltpu.VMEM((tr, tn), F32)],
        compiler_params=_params(("parallel", "parallel", "arbitrary"), blocks, 2 * _nbytes((tr, tn), F32)),
    )(a, b)


def _wgrad3(lhs3, rhs3):
    nw, t, d = lhs3.shape
    tk = _tile(t, 1024, SUBLANES_BF16)
    nk = t // tk

    def body(a_ref, b_ref, o_ref, acc):
        k = pl.program_id(1)
        part = _dot_ta(a_ref[...], b_ref[...])

        @pl.when(k == 0)
        def _():
            acc[...] = part

        @pl.when(k > 0)
        def _():
            acc[...] += part

        @pl.when(k == nk - 1)
        def _():
            o_ref[...] = acc[...].astype(BF16)

    blocks = 2 * _nbytes((tk, d), BF16) + _nbytes((d, d), BF16)
    return pl.pallas_call(
        body, name="wgrad_sq3", grid=(nw, nk),
        in_specs=[pl.BlockSpec((None, tk, d), lambda w, k: (w, k, 0)),
                  pl.BlockSpec((None, tk, d), lambda w, k: (w, k, 0))],
        out_specs=pl.BlockSpec((None, d, d), lambda w, k: (w, 0, 0)),
        out_shape=SDS((nw, d, d), BF16),
        scratch_shapes=[pltpu.VMEM((d, d), F32)],
        compiler_params=_params(("parallel", "arbitrary"), blocks, 2 * _nbytes((d, d), F32)),
    )(lhs3, rhs3)


def _ffn_bwd_dx1(du0, w_up, x1, dx2, g_ffn, n_planes_out):
    _, t, f = du0.shape
    d = x1.shape[1]
    ws = w_up.shape[2]
    tm = _tile(t, 512, SUBLANES_BF16)
    tn = _tile(ws, 1408, LANES)
    nps = ws // tn
    npp = f // tn
    nk = 2 * npp

    def body(du_ref, w_ref, x1_ref, dx2_ref, g_ref, dx1_ref, dxb_ref, gg_ref, acc):
        i = pl.program_id(0)
        k = pl.program_id(1)
        part = _dot_tb(du_ref[...], w_ref[...])

        @pl.when(k == 0)
        def _():
            acc[...] = part

        @pl.when(k > 0)
        def _():
            acc[...] += part

        @pl.when((i == 0) & (k == 0))
        def _():
            gg_ref[...] = jnp.zeros_like(gg_ref)

        @pl.when(k == nk - 1)
        def _():
            dh = acc[...]
            xh, inv = _rms_fwd(x1_ref[...])
            gg_ref[...] += jnp.sum(dh * xh, axis=0, keepdims=True)
            dx1 = dx2_ref[...] + _rms_bwd(dh, xh, inv, g_ref[...])
            dx1_ref[...] = dx1
            dxb_ref[...] = dx1.astype(BF16)

    blocks = (_nbytes((tm, tn), BF16) + _nbytes((d, tn), BF16) + 3 * _nbytes((tm, d), F32) + _nbytes((tm, d), BF16))
    return pl.pallas_call(
        body, name="ffn_bwd_dx1", grid=(t // tm, nk),
        in_specs=[pl.BlockSpec((None, tm, tn), lambda i, k: (k // npp, i, k % npp)),
                  pl.BlockSpec((None, d, tn), lambda i, k: (k // nps, 0, k % nps)),
                  pl.BlockSpec((tm, d), lambda i, k: (i, 0)),
                  pl.BlockSpec((tm, d), lambda i, k: (i, 0)),
                  pl.BlockSpec((1, d), lambda i, k: (0, 0))],
        out_specs=[pl.BlockSpec((tm, d), lambda i, k: (i, 0)),
                   pl.BlockSpec((None, tm, d), lambda i, k: (n_planes_out - 1, i, 0)),
                   pl.BlockSpec((1, d), lambda i, k: (0, 0))],
        out_shape=[SDS((t, d), F32), SDS((n_planes_out, t, d), BF16), SDS((1, d), F32)],
        scratch_shapes=[pltpu.VMEM((tm, d), F32)],
        compiler_params=_params(("arbitrary", "arbitrary"), blocks, 8 * _nbytes((tm, d), F32)),
    )(du0, w_up, x1, dx2, g_ffn)


def _mixer_bwd(rhs3, z, ypc, w3):
    _, t, d = rhs3.shape
    tm = _tile(t, 256, SUBLANES_BF16)

    def body(dx_ref, zgp, zgc, ypc_ref, w_ref, dyo, dzo, dpq):
        dm = _dot_tb(dx_ref[...], w_ref[2])
        sp = _sigmoid(zgp[...].astype(F32))
        sc = _sigmoid(zgc[...].astype(F32))
        dyp = (dm * sp).astype(BF16)
        dyc = (dm * sc).astype(BF16)
        dzo[0] = (dm * ypc_ref[0].astype(F32) * sp * (1.0 - sp)).astype(BF16)
        dzo[1] = (dm * ypc_ref[1].astype(F32) * sc * (1.0 - sc)).astype(BF16)
        dyo[0] = dyp
        dyo[1] = dyc
        dpq[0] = _dot_tb(dyp, w_ref[0]).astype(BF16)
        dpq[1] = _dot_tb(dyc, w_ref[1]).astype(BF16)

    blocks = _nbytes((tm, d), BF16) * 3 + _nbytes((2, tm, d), BF16) * 4 + _nbytes((3, d, d), BF16)
    return pl.pallas_call(
        body, name="mixer_bwd", grid=(t // tm,),
        in_specs=[pl.BlockSpec((None, tm, d), lambda i: (2, i, 0)),
                  pl.BlockSpec((tm, d), lambda i: (i, 4)),
                  pl.BlockSpec((tm, d), lambda i: (i, 5)),
                  pl.BlockSpec((2, tm, d), lambda i: (0, i, 0)),
                  pl.BlockSpec((3, d, d), lambda i: (0, 0, 0))],
        out_specs=[pl.BlockSpec((2, tm, d), lambda i: (0, i, 0)),
                   pl.BlockSpec((2, tm, d), lambda i: (2, i, 0)),
                   pl.BlockSpec((2, tm, d), lambda i: (0, i, 0))],
        out_shape=[SDS(rhs3.shape, BF16), SDS((N_SPLITS, t, d), BF16), SDS((2, t, d), BF16)],
        input_output_aliases={0: 0},
        compiler_params=_params(("parallel",), blocks, 8 * _nbytes((tm, d), F32)),
    )(rhs3, z, z, ypc, w3)


def _conv_bwd(dz, dpq, z, conv_w, nseq):
    _, t, d = dz.shape
    s = t // nseq
    c = _tile(d, 128, LANES)
    nb = d // c

    def body(dz_in, dq_ref, zb, zc, zv, cw, dzo, gw_ref):
        del dz_in

        @pl.when(pl.program_id(1) == 0)
        def _():
            gw_ref[...] = jnp.zeros_like(gw_ref)

        row = lax.broadcasted_iota(jnp.int32, (s, c), 0)
        b = zb[...].astype(F32)
        cm = zc[...].astype(F32)
        v = zv[...].astype(F32)
        cv = cm * v
        cv1 = _shift_down(cv, 1, row)
        cv2 = _shift_down(cv, 2, row)
        w0, w1, w2 = cw[pl.ds(0, 1), :], cw[pl.ds(1, 1), :], cw[pl.ds(2, 1), :]
        cc = w2 * cv + w1 * cv1 + w0 * cv2
        dq = dq_ref[...].astype(F32)
        dzo[0] = (dq * cc).astype(BF16)
        dcc = dq * b
        gw_ref[pl.ds(0, 1), :] += jnp.sum(dcc * cv2, axis=0, keepdims=True)
        gw_ref[pl.ds(1, 1), :] += jnp.sum(dcc * cv1, axis=0, keepdims=True)
        gw_ref[pl.ds(2, 1), :] += jnp.sum(dcc * cv, axis=0, keepdims=True)
        dcv = w2 * dcc + w1 * _shift_up(dcc, 1, row) + w0 * _shift_up(dcc, 2, row)
        dzo[1] = (dcv * v).astype(BF16)
        dzo[2] = (dcv * cm).astype(BF16)

    blocks = 4 * _nbytes((s, c), BF16) + _nbytes((3, s, c), BF16)
    return pl.pallas_call(
        body, name="conv_bwd", grid=(nb, nseq),
        in_specs=[pl.BlockSpec(memory_space=pl.ANY),
                  pl.BlockSpec((None, s, c), lambda j, b: (1, b, j)),
                  pl.BlockSpec((s, c), lambda j, b: (b, nb + j)),
                  pl.BlockSpec((s, c), lambda j, b: (b, 2 * nb + j)),
                  pl.BlockSpec((s, c), lambda j, b: (b, 3 * nb + j)),
                  pl.BlockSpec((3, c), lambda j, b: (0, j))],
        out_specs=[pl.BlockSpec((3, s, c), lambda j, b: (0, b, j)),
                   pl.BlockSpec((3, c), lambda j, b: (0, j))],
        out_shape=[SDS(dz.shape, BF16), SDS((3, d), F32)],
        input_output_aliases={0: 0},
        compiler_params=_params(("parallel", "arbitrary"), blocks, 16 * _nbytes((s, c), F32)),
    )(dz, dpq, z, z, z, conv_w)


def _pool_bwd_call(dz, dpq, z, pool_w, pool_scale, nseq):
    _, t, d = dz.shape
    s = t // nseq
    c = d // N_GROUPS

    def body(dz_in, dp_ref, zp, pw, ps, dzo, gpw_ref, gps_ref):
        del dz_in
        j = pl.program_id(0)

        @pl.when(pl.program_id(1) == 0)
        def _():
            gpw_ref[...] = jnp.zeros_like(gpw_ref)
            gps_ref[...] = jnp.zeros_like(gps_ref)

        row = lax.broadcasted_iota(jnp.int32, (s, c), 0)
        for gi, win in enumerate(POOL_WINDOWS):
            @pl.when(j == gi)
            def _(win=win):
                pb = _pool_fwd(zp[...].astype(F32), win, row).astype(BF16)
                plin = _dot(pb, pw[...])
                dps = dp_ref[...].astype(F32)
                gps_ref[...] += jnp.sum(dps * plin, axis=0, keepdims=True)
                dplb = (dps * ps[...]).astype(BF16)
                gpw_ref[...] += _dot_ta(pb, dplb)
                dzo[...] = _pool_bwd(_dot_tb(dplb, pw[...]), win, row).astype(BF16)

    blocks = 3 * _nbytes((s, c), BF16) + _nbytes((c, c), BF16) + _nbytes((c, c), F32)
    return pl.pallas_call(
        body, name="pool_bwd", grid=(N_GROUPS, nseq),
        in_specs=[pl.BlockSpec(memory_space=pl.ANY),
                  pl.BlockSpec((None, s, c), lambda j, b: (0, b, j)),
                  pl.BlockSpec((s, c), lambda j, b: (b, j)),
                  pl.BlockSpec((None, c, c), lambda j, b: (j, 0, 0)),
                  pl.BlockSpec((1, c), lambda j, b: (0, j))],
        out_specs=[pl.BlockSpec((None, s, c), lambda j, b: (3, b, j)),
                   pl.BlockSpec((None, c, c), lambda j, b: (j, 0, 0)),
                   pl.BlockSpec((1, c), lambda j, b: (0, j))],
        out_shape=[SDS(dz.shape, BF16), SDS((N_GROUPS, c, c), F32), SDS((1, d), F32)],
        input_output_aliases={0: 0},
        compiler_params=_params(("parallel", "arbitrary"), blocks, 10 * _nbytes((s, c), F32)),
    )(dz, dpq, z, pool_w, pool_scale)


def _dz_plane(zb):
    return jnp.where(zb < 4, (zb + 3) % 4, zb)


def _wgrad_in(h1, dz, nsh):
    t, d = h1.shape
    ws = N_SPLITS * d // nsh
    kb = _tile(math.gcd(d, ws), 512, LANES)
    npl = d // kb
    nps = ws // kb
    tk = _tile(t, 1024, SUBLANES_BF16)
    nk = t // tk

    def body(a_ref, b_ref, o_ref, acc):
        k = pl.program_id(1)
        part = _dot_ta(a_ref[...], b_ref[...])

        @pl.when(k == 0)
        def _():
            acc[...] = part

        @pl.when(k > 0)
        def _():
            acc[...] += part

        @pl.when(k == nk - 1)
        def _():
            o_ref[...] = acc[...].astype(BF16)

    blocks = _nbytes((tk, d), BF16) + _nbytes((tk, kb), BF16) + _nbytes((d, kb), BF16)
    return pl.pallas_call(
        body, name="wgrad_in", grid=(N_SPLITS * npl, nk),
        in_specs=[pl.BlockSpec((tk, d), lambda cb, k: (k, 0)),
                  pl.BlockSpec((None, tk, kb), lambda cb, k: (_dz_plane(cb // npl), k, cb % npl))],
        out_specs=pl.BlockSpec((None, d, kb), lambda cb, k: (cb // nps, 0, cb % nps)),
        out_shape=SDS((nsh, d, ws), BF16),
        scratch_shapes=[pltpu.VMEM((d, kb), F32)],
        compiler_params=_params(("parallel", "arbitrary"), blocks, 2 * _nbytes((d, kb), F32)),
    )(h1, dz)


def _mixer_bwd_dx(dz, w_in, x, dx1, g_mix):
    _, t, d = dz.shape
    ws = w_in.shape[2]
    tm = _tile(t, 512, SUBLANES_BF16)
    kb = _tile(math.gcd(d, ws), 512, LANES)
    npl = d // kb
    nps = ws // kb
    nk = N_SPLITS * npl

    def body(dz_ref, w_ref, x_ref, dx1_ref, g_ref, dx_ref, gg_ref, acc):
        i = pl.program_id(0)
        k = pl.program_id(1)
        part = _dot_tb(dz_ref[...], w_ref[...])

        @pl.when(k == 0)
        def _():
            acc[...] = part

        @pl.when(k > 0)
        def _():
            acc[...] += part

        @pl.when((i == 0) & (k == 0))
        def _():
            gg_ref[...] = jnp.zeros_like(gg_ref)

        @pl.when(k == nk - 1)
        def _():
            dh = acc[...]
            xh, inv = _rms_fwd(x_ref[...])
            gg_ref[...] += jnp.sum(dh * xh, axis=0, keepdims=True)
            dx_ref[...] = dx1_ref[...] + _rms_bwd(dh, xh, inv, g_ref[...])

    blocks = _nbytes((tm, kb), BF16) + _nbytes((d, kb), BF16) + 3 * _nbytes((tm, d), F32)
    return pl.pallas_call(
        body, name="mixer_bwd_dx", grid=(t // tm, nk),
        in_specs=[pl.BlockSpec((None, tm, kb), lambda i, k: (_dz_plane(k // npl), i, k % npl)),
                  pl.BlockSpec((None, d, kb), lambda i, k: (k // nps, 0, k % nps)),
                  pl.BlockSpec((tm, d), lambda i, k: (i, 0)),
                  pl.BlockSpec((tm, d), lambda i, k: (i, 0)),
                  pl.BlockSpec((1, d), lambda i, k: (0, 0))],
        out_specs=[pl.BlockSpec((tm, d), lambda i, k: (i, 0)),
                   pl.BlockSpec((1, d), lambda i, k: (0, 0))],
        out_shape=[SDS((t, d), F32), SDS((1, d), F32)],
        scratch_shapes=[pltpu.VMEM((tm, d), F32)],
        compiler_params=_params(("arbitrary", "arbitrary"), blocks, 8 * _nbytes((tm, d), F32)),
    )(dz, w_in, x, dx1, g_mix)


N_BIG = 5
SHARD_MAJOR = (0, 2)
ROWS_DIM1 = (1, 4)


def _ds(start, size, align):
    if isinstance(start, int):
        return pl.ds(start, size)
    return pl.ds(pl.multiple_of(start, align), size)


def _piece(a, ref, k, h):
    if a in SHARD_MAJOR:
        r = ref.shape[1] // 2
        return ref.at[k, _ds(h * r, r, SUBLANES_BF16), :]
    if a in ROWS_DIM1:
        r = ref.shape[1] // 8
        return ref.at[:, _ds((2 * k + h) * r, r, SUBLANES_BF16), :]
    r = ref.shape[0] // 8
    return ref.at[_ds((2 * k + h) * r, r, SUBLANES_BF16), :]


def _half(a, ref, h):
    if a in ROWS_DIM1:
        r = ref.shape[1] // 2
        return ref.at[:, _ds(h * r, r, SUBLANES_BF16), :]
    r = ref.shape[0] // 2
    return ref.at[_ds(h * r, r, SUBLANES_BF16), :]


def _piece_shape(a, full_shape):
    if a in SHARD_MAJOR:
        return (full_shape[1] // 2, full_shape[2])
    if a in ROWS_DIM1:
        return (full_shape[0], full_shape[1] // 8, full_shape[2])
    return (full_shape[0] // 8, full_shape[1])


def _shard_shape(a, full_shape):
    if a in SHARD_MAJOR:
        return (full_shape[1], full_shape[2])
    if a in ROWS_DIM1:
        return (full_shape[0], full_shape[1] // 4, full_shape[2])
    return (full_shape[0] // 4, full_shape[1])


def _rows_axis(a):
    return 1 if a in ROWS_DIM1 else 0


def _piece_block(a, full_shape):
    ps = _piece_shape(a, full_shape)
    if a in SHARD_MAJOR:
        return (None,) + ps, lambda k, c: (k, c, 0)
    if a in ROWS_DIM1:
        return ps, lambda k, c: (0, 2 * k + c, 0)
    return ps, lambda k, c: (2 * k + c, 0)


def _coords():
    x = lax.axis_index("x")
    y = lax.axis_index("y")
    c = lax.axis_index("c")
    return x, y, c


ANY = pl.BlockSpec(memory_space=pl.ANY)


def _gather_weights(locs, full_shapes):
    def body(*refs):
        loc = refs[:N_BIG]
        full = refs[N_BIG:2 * N_BIG]
        lsem, lrsem, ssem, rsem, fsem, frsem = refs[2 * N_BIG:]
        x, y, c = _coords()
        j = 2 * x + y
        peers = [(1 - x, y), (x, 1 - y), (1 - x, 1 - y)]
        pending = []
        for a in range(N_BIG):
            for h in range(2):
                cp = pltpu.make_async_remote_copy(
                    src_ref=_half(a, loc[a], h), dst_ref=_piece(a, full[a], j, h),
                    send_sem=lsem.at[2 * a + h], recv_sem=lrsem.at[2 * a + h],
                    device_id=(x, y, 1 - c), device_id_type=MESH)
                cp.start()
                pending.append(cp)
        sends = []
        for a in range(N_BIG):
            for i, (px, py) in enumerate(peers):
                cp = pltpu.make_async_remote_copy(
                    src_ref=_half(a, loc[a], c), dst_ref=_piece(a, full[a], j, c),
                    send_sem=ssem.at[3 * a + i], recv_sem=rsem.at[3 * a + i],
                    device_id=(px, py, c), device_id_type=MESH)
                cp.start()
                sends.append(cp)
        for a in range(N_BIG):
            for i, (px, py) in enumerate(peers):
                k = 2 * px + py
                landed = _piece(a, full[a], k, c)
                pltpu.make_async_remote_copy(
                    src_ref=landed, dst_ref=landed, send_sem=ssem.at[3 * a + i], recv_sem=rsem.at[3 * a + i],
                    device_id=(px, py, c), device_id_type=MESH).wait_recv()
                cp = pltpu.make_async_remote_copy(
                    src_ref=landed, dst_ref=landed, send_sem=fsem.at[3 * a + i], recv_sem=frsem.at[3 * a + i],
                    device_id=(x, y, 1 - c), device_id_type=MESH)
                cp.start()
                sends.append(cp)
        for a in range(N_BIG):
            for i, (px, py) in enumerate(peers):
                other = _piece(a, full[a], 2 * px + py, 1 - c)
                pltpu.make_async_remote_copy(
                    src_ref=other, dst_ref=other, send_sem=fsem.at[3 * a + i], recv_sem=frsem.at[3 * a + i],
                    device_id=(x, y, 1 - c), device_id_type=MESH).wait_recv()
        for cp in sends:
            cp.wait_send()
        for cp in pending:
            cp.wait()

    return pl.pallas_call(
        body, name="gather_weights",
        in_specs=[ANY] * N_BIG, out_specs=[ANY] * N_BIG,
        out_shape=[SDS(s, BF16) for s in full_shapes],
        scratch_shapes=[pltpu.SemaphoreType.DMA((2 * N_BIG,)), pltpu.SemaphoreType.DMA((2 * N_BIG,)),
                        pltpu.SemaphoreType.DMA((3 * N_BIG,)), pltpu.SemaphoreType.DMA((3 * N_BIG,)),
                        pltpu.SemaphoreType.DMA((3 * N_BIG,)), pltpu.SemaphoreType.DMA((3 * N_BIG,))],
    )(*locs)


def _exchange_halves(gbs):
    full_shapes = [g.shape for g in gbs]

    def body(*refs):
        gb = refs[:N_BIG]
        land = refs[N_BIG:2 * N_BIG]
        ssem, rsem = refs[2 * N_BIG:]
        x, y, c = _coords()
        copies = []
        for a in range(N_BIG):
            for k in range(4):
                cp = pltpu.make_async_remote_copy(
                    src_ref=_piece(a, gb[a], k, 1 - c), dst_ref=land[a].at[k],
                    send_sem=ssem.at[4 * a + k], recv_sem=rsem.at[4 * a + k],
                    device_id=(x, y, 1 - c), device_id_type=MESH)
                cp.start()
                copies.append(cp)
        for cp in copies:
            cp.wait()

    return pl.pallas_call(
        body, name="exchange_halves",
        in_specs=[ANY] * N_BIG, out_specs=[ANY] * N_BIG,
        out_shape=[SDS((4,) + _piece_shape(a, full_shapes[a]), BF16) for a in range(N_BIG)],
        scratch_shapes=[pltpu.SemaphoreType.DMA((4 * N_BIG,)), pltpu.SemaphoreType.DMA((4 * N_BIG,))],
    )(*gbs)


def _add_halves(gbs, lands, c_arr):
    full_shapes = [g.shape for g in gbs]

    def body(c_ref, *refs):
        del c_ref
        g = refs[:N_BIG]
        l = refs[N_BIG:2 * N_BIG]
        o = refs[2 * N_BIG:]
        for a in range(N_BIG):
            o[a][...] = (g[a][...].astype(F32) + l[a][...].astype(F32)).astype(BF16)

    g_specs, l_specs, o_specs, blocks = [], [], [], 0
    for a in range(N_BIG):
        bs, imap = _piece_block(a, full_shapes[a])
        ps = _piece_shape(a, full_shapes[a])
        g_specs.append(pl.BlockSpec(bs, lambda k, c_ref, imap=imap: imap(k, c_ref[0])))
        nd = len(ps)
        l_specs.append(pl.BlockSpec((None,) + ps, lambda k, c_ref, nd=nd: (k,) + (0,) * nd))
        o_specs.append(pl.BlockSpec((None,) + ps, lambda k, c_ref, nd=nd: (k,) + (0,) * nd))
        blocks += 3 * _nbytes(ps, BF16)
    return pl.pallas_call(
        body, name="add_halves",
        grid_spec=pltpu.PrefetchScalarGridSpec(
            num_scalar_prefetch=1, grid=(4,), in_specs=g_specs + l_specs, out_specs=o_specs),
        out_shape=[SDS((4,) + _piece_shape(a, full_shapes[a]), BF16) for a in range(N_BIG)],
        compiler_params=_params(("parallel",), blocks, blocks),
    )(c_arr, *gbs, *lands)


def _exchange_chips(ps):
    def body(*refs):
        p = refs[:N_BIG]
        land = refs[N_BIG:2 * N_BIG]
        ssem, rsem = refs[2 * N_BIG:]
        x, y, c = _coords()
        peers = [(1 - x, y), (x, 1 - y), (1 - x, 1 - y)]
        copies = []
        for a in range(N_BIG):
            for i, (px, py) in enumerate(peers):
                cp = pltpu.make_async_remote_copy(
                    src_ref=p[a].at[2 * px + py], dst_ref=land[a].at[i],
                    send_sem=ssem.at[3 * a + i], recv_sem=rsem.at[3 * a + i],
                    device_id=(px, py, c), device_id_type=MESH)
                cp.start()
                copies.append(cp)
        for cp in copies:
            cp.wait()

    return pl.pallas_call(
        body, name="exchange_chips",
        in_specs=[ANY] * N_BIG, out_specs=[ANY] * N_BIG,
        out_shape=[SDS((3,) + q.shape[1:], BF16) for q in ps],
        scratch_shapes=[pltpu.SemaphoreType.DMA((3 * N_BIG,)), pltpu.SemaphoreType.DMA((3 * N_BIG,))],
    )(*ps)


def _sum_chips(a, p, land, shard_shape, jc_arr, name):
    ps = land.shape[1:]
    ax = _rows_axis(a)
    rows = ps[ax]
    nsub = 2 if rows % (2 * SUBLANES_BF16) == 0 else 1
    bs = tuple(r // nsub if q == ax else r for q, r in enumerate(ps))
    nd = len(ps)

    def at_rows(v):
        return tuple(v if q == ax else 0 for q in range(nd))

    def body(jc_ref, p_ref, l_ref, o_ref):
        del jc_ref
        acc = p_ref[...].astype(F32) + l_ref[0].astype(F32)
        acc = acc + l_ref[1].astype(F32)
        o_ref[...] = acc + l_ref[2].astype(F32)

    blocks = 4 * _nbytes(bs, BF16) + _nbytes(bs, F32)
    return pl.pallas_call(
        body, name=name,
        grid_spec=pltpu.PrefetchScalarGridSpec(
            num_scalar_prefetch=1, grid=(nsub,),
            in_specs=[pl.BlockSpec((None,) + bs, lambda s, jc: (jc[0],) + at_rows(s)),
                      pl.BlockSpec((3,) + bs, lambda s, jc: (0,) + at_rows(s))],
            out_specs=pl.BlockSpec(bs, lambda s, jc: at_rows(jc[1] * nsub + s))),
        out_shape=SDS(shard_shape, F32),
        compiler_params=_params(("parallel",), blocks, 2 * _nbytes(bs, F32)),
    )(jc_arr, p, land)


def _exchange_result(gs):
    def body(*refs):
        src = refs[:N_BIG]
        out = refs[N_BIG:2 * N_BIG]
        ssem, rsem = refs[2 * N_BIG:]
        x, y, c = _coords()
        copies = []
        for a in range(N_BIG):
            rc = pltpu.make_async_remote_copy(
                src_ref=_half(a, src[a], c), dst_ref=_half(a, out[a], c), send_sem=ssem.at[a], recv_sem=rsem.at[a],
                device_id=(x, y, 1 - c), device_id_type=MESH)
            rc.start()
            copies.append(rc)
        for cp in copies:
            cp.wait()

    return pl.pallas_call(
        body, name="exchange_result",
        in_specs=[ANY] * N_BIG, out_specs=[ANY] * N_BIG,
        out_shape=[SDS(g.shape, F32) for g in gs],
        input_output_aliases={a: a for a in range(N_BIG)},
        scratch_shapes=[pltpu.SemaphoreType.DMA((N_BIG,)), pltpu.SemaphoreType.DMA((N_BIG,))],
    )(*gs)


def _allreduce_small(v, name):
    rows = v.shape[0]

    def body(v_ref, o_ref, slots, lsem, ssem, rsem):
        x, y, c = _coords()
        me = 4 * x + 2 * y + c
        own = pltpu.make_async_copy(v_ref, slots.at[me], lsem)
        own.start()
        copies = []
        for dlt in range(1, 8):
            px = 1 - x if (dlt >> 2) & 1 else x
            py = 1 - y if (dlt >> 1) & 1 else y
            pc = 1 - c if dlt & 1 else c
            cp = pltpu.make_async_remote_copy(
                src_ref=v_ref, dst_ref=slots.at[me], send_sem=ssem.at[dlt - 1], recv_sem=rsem.at[dlt - 1],
                device_id=(px, py, pc), device_id_type=MESH)
            cp.start()
            copies.append(cp)
        for cp in copies:
            cp.wait()
        own.wait()
        acc = slots[0]
        for i in range(1, 8):
            acc = acc + slots[i]
        o_ref[...] = acc

    return pl.pallas_call(
        body, name=name,
        in_specs=[pl.BlockSpec(memory_space=pltpu.VMEM)],
        out_specs=pl.BlockSpec(memory_space=pltpu.VMEM),
        out_shape=SDS((rows, LANES), F32),
        scratch_shapes=[pltpu.VMEM((8, rows, LANES), F32), pltpu.SemaphoreType.DMA,
                        pltpu.SemaphoreType.DMA((7,)), pltpu.SemaphoreType.DMA((7,))],
    )(v)


def _adamw(w, g, m, v, name, g_plane=None):
    rows, cols = w.shape
    tr = _tile(rows, max(SUBLANES_F32, (256 * 1024 // cols) // SUBLANES_F32 * SUBLANES_F32), SUBLANES_F32)

    def body(w_ref, g_ref, m_ref, v_ref, d_ref, mo_ref, vo_ref):
        gr = g_ref[...]
        mn = ADAM_B1 * m_ref[...] + (1.0 - ADAM_B1) * gr
        vn = ADAM_B2 * v_ref[...] + (1.0 - ADAM_B2) * (gr * gr)
        m_hat = mn / (1.0 - ADAM_B1 ** ADAM_STEP)
        v_hat = vn / (1.0 - ADAM_B2 ** ADAM_STEP)
        d_ref[...] = -ADAM_LR * (m_hat / (jnp.sqrt(v_hat) + ADAM_EPS) + ADAM_WD * w_ref[...])
        mo_ref[...] = mn
        vo_ref[...] = vn

    spec = pl.BlockSpec((tr, cols), lambda i: (i, 0))
    g_spec = spec if g_plane is None else pl.BlockSpec((None, tr, cols), lambda i: (g_plane, i, 0))
    return pl.pallas_call(
        body, name=name, grid=(rows // tr,),
        in_specs=[spec, g_spec, spec, spec], out_specs=[spec, spec, spec],
        out_shape=[SDS((rows, cols), F32)] * 3,
        compiler_params=_params(("parallel",), 7 * _nbytes((tr, cols), F32), 4 * _nbytes((tr, cols), F32)),
    )(w, g, m, v)


def _pack(parts):
    rows = []
    for p in parts:
        r = p.reshape(-1, LANES)
        pad = (-r.shape[0]) % SUBLANES_F32
        if pad:
            r = jnp.pad(r, ((0, pad), (0, 0)))
        rows.append(r)
    return jnp.concatenate(rows, axis=0)


def _unpack(packed, shapes):
    out, at = [], 0
    for s in shapes:
        n = 1
        for q in s:
            n *= q
        r = n // LANES
        out.append(packed[at:at + r].reshape(s))
        at += r + (-r) % SUBLANES_F32
    return out


def kernel(x, norm_mix, w_in, pool_w, pool_scale, w_pool_proj, conv_w, w_conv_out, w_o, norm_ffn, w_up, ffn_conv_w, ffn_conv_b, w_down, norm_final, loss_target, m_norm_mix, m_w_in, m_pool_w, m_pool_scale, m_w_pool_proj, m_conv_w, m_w_conv_out, m_w_o, m_norm_ffn, m_w_up, m_ffn_conv_w, m_ffn_conv_b, m_w_down, m_norm_final, v_norm_mix, v_w_in, v_pool_w, v_pool_scale, v_w_pool_proj, v_conv_w, v_w_conv_out, v_w_o, v_norm_ffn, v_w_up, v_ffn_conv_w, v_ffn_conv_b, v_w_down, v_norm_final):
    nseq, seq, d = x.shape
    t = nseq * seq
    f = w_down.shape[1] * 4
    c = d // N_GROUPS
    xy = lax.axis_index("x") * 2 + lax.axis_index("y")
    c_arr = lax.axis_index("c").astype(jnp.int32).reshape(1)
    jc_arr = jnp.stack([xy, lax.axis_index("c")]).astype(jnp.int32)
    nsh = 4

    locs = [w_in[0].astype(BF16),
            jnp.stack([w_pool_proj[0], w_conv_out[0], w_o[0]]).astype(BF16),
            w_up[0].astype(BF16), w_down[0].astype(BF16), pool_w[0].astype(BF16)]
    full_shapes = [(nsh, d, N_SPLITS * d // nsh), (3, d, d), (nsh, d, 2 * f // nsh), (f, d), (N_GROUPS, c, c)]
    w_in_f, w3_f, w_up_f, w_down_f, pool_w_f = _gather_weights(locs, full_shapes)

    zero = jnp.zeros((), jnp.int32)
    cw_pad = lax.dynamic_update_slice(jnp.zeros((3, d), F32), conv_w[0], (zero, xy * (d // 4)))
    fw_pad = lax.dynamic_update_slice(jnp.zeros((3, 2 * f), F32), ffn_conv_w[0], (zero, xy * (f // 2)))
    small_w = _pack([cw_pad, fw_pad]) * 0.5
    conv_w_f, ffn_cw_f = _unpack(_allreduce_small(small_w, "gather_small"), [(3, d), (3, 2 * f)])
    ffn_cw_p = ffn_cw_f.reshape(3, 2, f).transpose(1, 0, 2)
    ffn_cb_p = ffn_conv_b.reshape(2, 1, f)

    x2d = x.reshape(t, d)
    tgt = loss_target.reshape(t, d)
    z, h1 = _fwd_in(x2d, norm_mix, w_in_f)
    lhs3 = _mixer_mid_fwd(z, pool_w_f, pool_scale, conv_w_f, nseq)
    lhs3, ypc, x1, h2 = _mixer_out(lhs3, z, x2d, w3_f, norm_ffn)
    u0 = _ffn_up(h2, w_up_f, f)
    act = _ffn_mid_fwd(u0, ffn_cw_p, ffn_cb_p, nseq)
    dx2, dx2b, loss11, g_norm_final = _ffn_down_loss(act, w_down_f, x1, tgt, norm_final.reshape(1, d))

    da = _ffn_bwd_da(dx2b, w_down_f)
    du0, g_ffn_cw_p, g_ffn_cb_p = _ffn_mid_bwd(da, u0, ffn_cw_p, ffn_cb_p, nseq)
    tn_up = _tile(2 * f // nsh, 1408, LANES)
    npp = f // tn_up
    g_w_down = _wgrad(act, dx2b, "wgrad_down", tr=tn_up, tn=d)
    g_w_up = _wgrad(h2, du0, "wgrad_up", tr=d, tn=tn_up, b_plane_of=lambda n: (n // npp, n % npp), out_shards=nsh)
    dx1, rhs3, g_norm_ffn = _ffn_bwd_dx1(du0, w_up_f, x1, dx2, norm_ffn, 3)
    rhs3, dz, dpq = _mixer_bwd(rhs3, z, ypc, w3_f)
    g_w3 = _wgrad3(lhs3, rhs3)
    dz, g_conv_w = _conv_bwd(dz, dpq, z, conv_w_f, nseq)
    dz, g_pool_w, g_pool_scale = _pool_bwd_call(dz, dpq, z, pool_w_f, pool_scale, nseq)
    g_w_in = _wgrad_in(h1, dz, nsh)
    grad_x, g_norm_mix = _mixer_bwd_dx(dz, w_in_f, x2d, dx1, norm_mix)

    gbs = [g_w_in, g_w3, g_w_up, g_w_down, g_pool_w.astype(BF16)]
    lands = _exchange_halves(gbs)
    ps = _add_halves(gbs, lands, c_arr)
    lands2 = _exchange_chips(ps)
    rs = [_sum_chips(a, ps[a], lands2[a], _shard_shape(a, full_shapes[a]), jc_arr, "sum_chips_%d" % a)
          for a in range(N_BIG)]
    g_in_s, g3_s, g_up_s, g_down_s, g_pool_s = _exchange_result(rs)

    g_ffn_cw = g_ffn_cw_p.transpose(1, 0, 2).reshape(3, 2 * f)
    small_shapes = [(1, d), (1, d), (1, d), (1, 2 * f), (d,), (3, d), (3, 2 * f)]
    small = _allreduce_small(_pack([g_norm_mix, g_pool_scale, g_norm_ffn, g_ffn_cb_p.reshape(1, 2 * f),
                                    g_norm_final.reshape(d), g_conv_w, g_ffn_cw]), "allreduce_small")
    gs_norm_mix, gs_pool_scale, gs_norm_ffn, gs_ffn_cb, gs_norm_final, gs_conv_w, gs_ffn_cw = _unpack(small, small_shapes)
    gs_conv_w = lax.dynamic_slice(gs_conv_w, (zero, xy * (d // 4)), (3, d // 4))
    gs_ffn_cw = lax.dynamic_slice(gs_ffn_cw, (zero, xy * (f // 2)), (3, f // 2))

    def upd(w, g, m, v, name, g_plane=None):
        shape = w.shape
        rows = 1
        for q in shape[:-1]:
            rows *= q
        g2 = g if g_plane is not None else g.reshape(rows, shape[-1])
        dlt, mn, vn = _adamw(w.reshape(rows, shape[-1]), g2, m.reshape(rows, shape[-1]), v.reshape(rows, shape[-1]),
                             name, g_plane)
        return dlt.reshape(shape), mn.reshape(shape), vn.reshape(shape)

    big = {
        "w_in": (g_in_s.reshape(w_in.shape), upd(w_in, g_in_s, m_w_in, v_w_in, "adamw_w_in")),
        "pool_w": (g_pool_s.reshape(pool_w.shape), upd(pool_w, g_pool_s, m_pool_w, v_pool_w, "adamw_pool_w")),
        "w_pool_proj": (g3_s[0][None], upd(w_pool_proj, g3_s, m_w_pool_proj, v_w_pool_proj, "adamw_w_pool_proj", 0)),
        "w_conv_out": (g3_s[1][None], upd(w_conv_out, g3_s, m_w_conv_out, v_w_conv_out, "adamw_w_conv_out", 1)),
        "w_o": (g3_s[2][None], upd(w_o, g3_s, m_w_o, v_w_o, "adamw_w_o", 2)),
        "w_up": (g_up_s.reshape(w_up.shape), upd(w_up, g_up_s, m_w_up, v_w_up, "adamw_w_up")),
        "w_down": (g_down_s.reshape(w_down.shape), upd(w_down, g_down_s, m_w_down, v_w_down, "adamw_w_down")),
    }

    small_names = ["norm_mix", "pool_scale", "norm_ffn", "ffn_conv_b", "norm_final", "conv_w", "ffn_conv_w"]
    small_ws = [norm_mix, pool_scale, norm_ffn, ffn_conv_b, norm_final, conv_w, ffn_conv_w]
    small_ms = [m_norm_mix, m_pool_scale, m_norm_ffn, m_ffn_conv_b, m_norm_final, m_conv_w, m_ffn_conv_w]
    small_vs = [v_norm_mix, v_pool_scale, v_norm_ffn, v_ffn_conv_b, v_norm_final, v_conv_w, v_ffn_conv_w]
    small_gs = [gs_norm_mix, gs_pool_scale, gs_norm_ffn, gs_ffn_cb, gs_norm_final, gs_conv_w, gs_ffn_cw]
    sd, sm, sv = _adamw(_pack(small_ws), _pack(small_gs), _pack(small_ms), _pack(small_vs), "adamw_small")
    shapes = [w.shape for w in small_ws]
    sd, sm, sv = _unpack(sd, shapes), _unpack(sm, shapes), _unpack(sv, shapes)
    res = dict(big)
    for i, nm in enumerate(small_names):
        res[nm] = (small_gs[i].reshape(shapes[i]), (sd[i], sm[i], sv[i]))

    loss = lax.psum(loss11[0, 0], ("x", "y", "c"))
    order = ["norm_mix", "w_in", "pool_w", "pool_scale", "w_pool_proj", "conv_w", "w_conv_out", "w_o", "norm_ffn",
             "w_up", "ffn_conv_w", "ffn_conv_b", "w_down", "norm_final"]
    return (loss, grad_x.reshape(x.shape), *[res[n][0] for n in order], *[res[n][1][0] for n in order],
            *[res[n][1][1] for n in order], *[res[n][1][2] for n in order])
```

```python
import math

import jax
import jax.numpy as jnp
from jax import lax
from jax.experimental import pallas as pl
from jax.experimental.pallas import tpu as pltpu

F32 = jnp.float32
BF16 = jnp.bfloat16
SDS = jax.ShapeDtypeStruct
MESH = pl.DeviceIdType.MESH

RMS_EPS = 1e-6
POOL_WINDOWS = (2, 4, 8, 16)
N_GROUPS = len(POOL_WINDOWS)
N_SPLITS = 6

ADAM_LR = 0.001
ADAM_B1 = 0.9
ADAM_B2 = 0.999
ADAM_EPS = 1e-08
ADAM_WD = 0.01
ADAM_STEP = 10

LANES = 128
SUBLANES_F32 = 8
SUBLANES_BF16 = 16
VMEM_BYTES = 64 * 1024 * 1024
VMEM_CAP = VMEM_BYTES - 8 * 1024 * 1024
VMEM_FLOOR = 16 * 1024 * 1024

ANY = pl.BlockSpec(memory_space=pl.ANY)


def _tile(dim, pref, align):
    if dim <= pref:
        return dim
    t = (pref // align) * align
    while t >= align:
        if dim % t == 0:
            return t
        t -= align
    return dim


def _nbytes(shape, dtype):
    n = 1
    for s in shape:
        n *= s
    return n * jnp.dtype(dtype).itemsize


def _params(sem, block_bytes, temp_bytes=0):
    need = 2 * block_bytes + temp_bytes + 4 * 1024 * 1024
    return pltpu.CompilerParams(dimension_semantics=sem, vmem_limit_bytes=int(min(max(need, VMEM_FLOOR), VMEM_CAP)))


class _Comm:
    def __init__(self, ins, out_shapes, sems, start, finish, aliases=None):
        self.ins = list(ins)
        self.out_shapes = list(out_shapes)
        self.sems = list(sems)
        self.start = start
        self.finish = finish
        self.aliases = dict(aliases or {})


def _pcall(body, *, name, grid, in_specs, out_specs, out_shape, sem, blocks, temps=0, scratch_shapes=(),
           input_output_aliases=None, comm=None):
    in_specs = list(in_specs)
    out_specs = list(out_specs)
    out_shape = list(out_shape)
    scratch_shapes = list(scratch_shapes)
    aliases = dict(input_output_aliases or {})
    n_in, n_out, n_scr = len(in_specs), len(out_shape), len(scratch_shapes)
    if comm is None:
        call = pl.pallas_call(
            body, name=name, grid=grid, in_specs=in_specs, out_specs=out_specs, out_shape=out_shape,
            scratch_shapes=scratch_shapes, input_output_aliases=aliases,
            compiler_params=_params(sem, blocks, temps))
        return lambda *args: (list(call(*args)), [])

    nci, nco = len(comm.ins), len(comm.out_shapes)

    def hosted(*refs):
        ins = refs[:n_in]
        cins = refs[n_in:n_in + nci]
        outs = refs[n_in + nci:n_in + nci + n_out]
        couts = refs[n_in + nci + n_out:n_in + nci + n_out + nco]
        scr = refs[n_in + nci + n_out + nco:n_in + nci + n_out + nco + n_scr]
        csems = refs[n_in + nci + n_out + nco + n_scr:]
        first = None
        last = None
        for q, g in enumerate(grid):
            pid = pl.program_id(q)
            first = (pid == 0) if first is None else first & (pid == 0)
            last = (pid == g - 1) if last is None else last & (pid == g - 1)

        @pl.when(first)
        def _():
            comm.start(cins, couts, csems)

        body(*ins, *outs, *scr)

        @pl.when(last)
        def _():
            comm.finish(cins, couts, csems)

    for i, o in comm.aliases.items():
        aliases[n_in + i] = n_out + o
    call = pl.pallas_call(
        hosted, name=name, grid=grid, in_specs=in_specs + [ANY] * nci, out_specs=out_specs + [ANY] * nco,
        out_shape=out_shape + comm.out_shapes, scratch_shapes=scratch_shapes + comm.sems,
        input_output_aliases=aliases,
        compiler_params=_params(("arbitrary",) * len(grid), blocks, temps))

    def run(*args):
        res = call(*args, *comm.ins)
        return list(res[:n_out]), list(res[n_out:])

    return run


def _run_comm(comm, name):
    def body(*refs):
        nci, nco = len(comm.ins), len(comm.out_shapes)
        cins, couts, csems = refs[:nci], refs[nci:nci + nco], refs[nci + nco:]
        comm.start(cins, couts, csems)
        comm.finish(cins, couts, csems)

    return list(pl.pallas_call(
        body, name=name, in_specs=[ANY] * len(comm.ins), out_specs=[ANY] * len(comm.out_shapes),
        out_shape=comm.out_shapes, scratch_shapes=comm.sems, input_output_aliases=comm.aliases,
    )(*comm.ins))


def _dot(a, b):
    return jnp.dot(a, b, preferred_element_type=F32)


def _dot_tb(a, b):
    return lax.dot_general(a, b, (((1,), (1,)), ((), ())), preferred_element_type=F32)


def _dot_ta(a, b):
    return lax.dot_general(a, b, (((0,), (0,)), ((), ())), preferred_element_type=F32)


def _rms_fwd(x):
    inv = lax.rsqrt(jnp.mean(x * x, axis=-1, keepdims=True) + RMS_EPS)
    return x * inv, inv


def _rms_bwd(dy, xhat, inv, g):
    gd = dy * g
    return inv * (gd - xhat * jnp.mean(gd * xhat, axis=-1, keepdims=True))


def _sigmoid(x):
    return 1.0 / (1.0 + jnp.exp(-x))


def _shift_down(x, k, row):
    return jnp.where(row >= k, pltpu.roll(x, k, 0), 0.0)


def _shift_up(x, k, row):
    s = x.shape[0]
    return jnp.where(row < s - k, pltpu.roll(x, s - k, 0), 0.0)


def _pool_fwd(u, win, row):
    s = u
    k = 1
    while k < win:
        s = s + _shift_down(s, k, row)
        k *= 2
    cnt = jnp.minimum(row + 1, win).astype(F32)
    return s / cnt - u


def _pool_bwd(dp, win, row):
    cnt = jnp.minimum(row + 1, win).astype(F32)
    s = dp / cnt
    k = 1
    while k < win:
        s = s + _shift_up(s, k, row)
        k *= 2
    return s - dp


def _acc_over(k, nk, part, acc, o_ref):
    @pl.when(k == 0)
    def _():
        acc[...] = part

    @pl.when(k > 0)
    def _():
        acc[...] += part

    @pl.when(k == nk - 1)
    def _():
        o_ref[...] = acc[...].astype(o_ref.dtype)


def _fwd_in(x, g, w, comm=None):
    t, d = x.shape
    nsh, _, ws = w.shape
    n = nsh * ws
    tm = _tile(t, 1024, SUBLANES_BF16)
    tn = _tile(ws, 1536, LANES)
    nps = ws // tn

    def body(x_ref, g_ref, w_ref, z_ref, h_ref, hs):
        @pl.when(pl.program_id(1) == 0)
        def _():
            xh, _ = _rms_fwd(x_ref[...])
            h = (xh * g_ref[...]).astype(BF16)
            hs[...] = h
            h_ref[...] = h

        z_ref[...] = _dot(hs[...], w_ref[...]).astype(BF16)

    blocks = _nbytes((tm, d), F32) + _nbytes((d, tn), BF16) + _nbytes((tm, tn), BF16) + _nbytes((tm, d), BF16)
    return _pcall(
        body, name="fwd_in", grid=(t // tm, n // tn),
        in_specs=[pl.BlockSpec((tm, d), lambda i, j: (i, 0)), pl.BlockSpec((1, d), lambda i, j: (0, 0)),
                  pl.BlockSpec((None, d, tn), lambda i, j: (j // nps, 0, j % nps))],
        out_specs=[pl.BlockSpec((tm, tn), lambda i, j: (i, j)), pl.BlockSpec((tm, d), lambda i, j: (i, 0))],
        out_shape=[SDS((t, n), BF16), SDS((t, d), BF16)],
        scratch_shapes=[pltpu.VMEM((tm, d), BF16)],
        sem=("parallel", "arbitrary"), blocks=blocks, temps=3 * _nbytes((tm, d), F32), comm=comm,
    )(x, g, w)


def _mixer_mid_fwd(z, pool_w, pool_scale, conv_w, nseq):
    t = z.shape[0]
    d = pool_scale.shape[1]
    s = t // nseq
    c = d // N_GROUPS

    def body(zp, zb, zc, zv, pw, ps, cw, o):
        j = pl.program_id(1)
        row = lax.broadcasted_iota(jnp.int32, (s, c), 0)
        for gi, win in enumerate(POOL_WINDOWS):
            @pl.when(j == gi)
            def _(win=win):
                pooled = _pool_fwd(zp[...].astype(F32), win, row)
                o[0] = (_dot(pooled.astype(BF16), pw[...]) * ps[...]).astype(BF16)

        cv = zc[...].astype(F32) * zv[...].astype(F32)
        cc = (cw[pl.ds(2, 1), :] * cv + cw[pl.ds(1, 1), :] * _shift_down(cv, 1, row)
              + cw[pl.ds(0, 1), :] * _shift_down(cv, 2, row))
        o[1] = (zb[...].astype(F32) * cc).astype(BF16)

    blocks = 4 * _nbytes((s, c), BF16) + _nbytes((c, c), BF16) + _nbytes((2, s, c), BF16)
    outs, _ = _pcall(
        body, name="mixer_mid_fwd", grid=(nseq, N_GROUPS),
        in_specs=[pl.BlockSpec((s, c), lambda b, j: (b, j)),
                  pl.BlockSpec((s, c), lambda b, j: (b, N_GROUPS + j)),
                  pl.BlockSpec((s, c), lambda b, j: (b, 2 * N_GROUPS + j)),
                  pl.BlockSpec((s, c), lambda b, j: (b, 3 * N_GROUPS + j)),
                  pl.BlockSpec((None, c, c), lambda b, j: (j, 0, 0)),
                  pl.BlockSpec((1, c), lambda b, j: (0, j)),
                  pl.BlockSpec((3, c), lambda b, j: (0, j))],
        out_specs=[pl.BlockSpec((2, s, c), lambda b, j: (0, b, j))],
        out_shape=[SDS((3, t, d), BF16)],
        sem=("parallel", "parallel"), blocks=blocks, temps=8 * _nbytes((s, c), F32),
    )(z, z, z, z, pool_w, pool_scale, conv_w)
    return outs[0]


def _mixer_out(lhs3, z, x, w3, g_ffn, comm=None):
    t, d = x.shape
    tm = _tile(t, 256, SUBLANES_BF16)

    def body(pq, zgp, zgc, x_ref, w_ref, g_ref, mrg, ypc, x1o, h2o):
        yp = _dot(pq[0], w_ref[0])
        yc = _dot(pq[1], w_ref[1])
        m = _sigmoid(zgp[...].astype(F32)) * yp + _sigmoid(zgc[...].astype(F32)) * yc
        mb = m.astype(BF16)
        x1 = x_ref[...] + _dot(mb, w_ref[2])
        ypc[0] = yp.astype(BF16)
        ypc[1] = yc.astype(BF16)
        mrg[...] = mb
        x1o[...] = x1
        xh, _ = _rms_fwd(x1)
        h2o[...] = (xh * g_ref[...]).astype(BF16)

    blocks = (_nbytes((2, tm, d), BF16) * 2 + _nbytes((tm, d), BF16) * 4 + _nbytes((tm, d), F32) * 2
              + _nbytes((3, d, d), BF16))
    return _pcall(
        body, name="mixer_out", grid=(t // tm,),
        in_specs=[pl.BlockSpec((2, tm, d), lambda i: (0, i, 0)),
                  pl.BlockSpec((tm, d), lambda i: (i, 4)),
                  pl.BlockSpec((tm, d), lambda i: (i, 5)),
                  pl.BlockSpec((tm, d), lambda i: (i, 0)),
                  pl.BlockSpec((3, d, d), lambda i: (0, 0, 0)),
                  pl.BlockSpec((1, d), lambda i: (0, 0))],
        out_specs=[pl.BlockSpec((None, tm, d), lambda i: (2, i, 0)),
                   pl.BlockSpec((2, tm, d), lambda i: (0, i, 0)),
                   pl.BlockSpec((tm, d), lambda i: (i, 0)),
                   pl.BlockSpec((tm, d), lambda i: (i, 0))],
        out_shape=[SDS(lhs3.shape, BF16), SDS((2, t, d), BF16), SDS((t, d), F32), SDS((t, d), BF16)],
        input_output_aliases={0: 0},
        sem=("parallel",), blocks=blocks, temps=8 * _nbytes((tm, d), F32), comm=comm,
    )(lhs3, z, z, x, w3, g_ffn)


def _ffn_up(h2, w_up, f):
    t, d = h2.shape
    _, _, ws = w_up.shape
    tm = _tile(t, 1024, SUBLANES_BF16)
    tn = _tile(ws, 1408, LANES)
    nps = ws // tn
    npp = f // tn

    def body(h_ref, w_ref, o_ref):
        o_ref[...] = _dot(h_ref[...], w_ref[...]).astype(BF16)

    blocks = _nbytes((tm, d), BF16) + _nbytes((d, tn), BF16) + _nbytes((tm, tn), BF16)
    outs, _ = _pcall(
        body, name="ffn_up", grid=(t // tm, 2 * npp),
        in_specs=[pl.BlockSpec((tm, d), lambda i, j: (i, 0)),
                  pl.BlockSpec((None, d, tn), lambda i, j: (j // nps, 0, j % nps))],
        out_specs=[pl.BlockSpec((None, tm, tn), lambda i, j: (j // npp, i, j % npp))],
        out_shape=[SDS((2, t, f), BF16)],
        sem=("parallel", "parallel"), blocks=blocks, temps=_nbytes((tm, tn), F32),
    )(h2, w_up)
    return outs[0]


def _conv3_rows(u, u1, u2, w_ref, p):
    return w_ref[p, pl.ds(2, 1), :] * u + w_ref[p, pl.ds(1, 1), :] * u1 + w_ref[p, pl.ds(0, 1), :] * u2


def _ffn_mid_fwd(u0, cw, cb, nseq):
    _, t, f = u0.shape
    s = t // nseq
    c = _tile(f, 256, LANES)

    def body(u_ref, w_ref, b_ref, a_ref):
        row = lax.broadcasted_iota(jnp.int32, (s, c), 0)
        act = []
        for p in range(2):
            u = u_ref[p].astype(F32)
            act.append(_conv3_rows(u, _shift_down(u, 1, row), _shift_down(u, 2, row), w_ref, p) + b_ref[p])
        ug, uv = act
        a_ref[...] = (ug * _sigmoid(ug) * uv).astype(BF16)

    blocks = _nbytes((2, s, c), BF16) + _nbytes((s, c), BF16)
    outs, _ = _pcall(
        body, name="ffn_mid_fwd", grid=(f // c, nseq),
        in_specs=[pl.BlockSpec((2, s, c), lambda j, b: (0, b, j)),
                  pl.BlockSpec((2, 3, c), lambda j, b: (0, 0, j)),
                  pl.BlockSpec((2, 1, c), lambda j, b: (0, 0, j))],
        out_specs=[pl.BlockSpec((s, c), lambda j, b: (b, j))],
        out_shape=[SDS((t, f), BF16)],
        sem=("parallel", "parallel"), blocks=blocks, temps=8 * _nbytes((s, c), F32),
    )(u0, cw, cb)
    return outs[0]


def _ffn_down_loss(a, w_down, x1, tgt, g_fin):
    t, f = a.shape
    d = x1.shape[1]
    tm = _tile(t, 256, SUBLANES_BF16)
    nsteps = t // tm

    def body(a_ref, w_ref, x1_ref, t_ref, g_ref, dx_ref, dxb_ref, loss_ref, gg_ref, lacc):
        i = pl.program_id(0)

        @pl.when(i == 0)
        def _():
            lacc[...] = jnp.zeros_like(lacc)
            gg_ref[...] = jnp.zeros_like(gg_ref)

        x2 = x1_ref[...] + _dot(a_ref[...], w_ref[...])
        xh, inv = _rms_fwd(x2)
        g = g_ref[...]
        e = xh * g - t_ref[...]
        lacc[...] += jnp.sum(e * e, axis=0, keepdims=True)
        dy = e * (1.0 / d)
        gg_ref[...] += jnp.sum(dy * xh, axis=0, keepdims=True)
        dx2 = _rms_bwd(dy, xh, inv, g)
        dx_ref[...] = dx2
        dxb_ref[...] = dx2.astype(BF16)

        @pl.when(i == nsteps - 1)
        def _():
            loss_ref[...] = jnp.sum(lacc[...], axis=1, keepdims=True) * (0.5 / d)

    blocks = (_nbytes((tm, f), BF16) + _nbytes((f, d), BF16) + 3 * _nbytes((tm, d), F32) + _nbytes((tm, d), BF16))
    outs, _ = _pcall(
        body, name="ffn_down_loss", grid=(nsteps,),
        in_specs=[pl.BlockSpec((tm, f), lambda i: (i, 0)), pl.BlockSpec((f, d), lambda i: (0, 0)),
                  pl.BlockSpec((tm, d), lambda i: (i, 0)), pl.BlockSpec((tm, d), lambda i: (i, 0)),
                  pl.BlockSpec((1, d), lambda i: (0, 0))],
        out_specs=[pl.BlockSpec((tm, d), lambda i: (i, 0)), pl.BlockSpec((tm, d), lambda i: (i, 0)),
                   pl.BlockSpec((1, 1), lambda i: (0, 0)), pl.BlockSpec((1, d), lambda i: (0, 0))],
        out_shape=[SDS((t, d), F32), SDS((t, d), BF16), SDS((1, 1), F32), SDS((1, d), F32)],
        scratch_shapes=[pltpu.VMEM((1, d), F32)],
        sem=("arbitrary",), blocks=blocks, temps=8 * _nbytes((tm, d), F32),
    )(a, w_down, x1, tgt, g_fin)
    return outs


def _ffn_bwd_da(dxb, w_down, comm=None):
    t, d = dxb.shape
    f = w_down.shape[0]
    tm = _tile(t, 1024, SUBLANES_BF16)
    tn = _tile(f, 1408, LANES)

    def body(x_ref, w_ref, o_ref):
        o_ref[...] = _dot_tb(x_ref[...], w_ref[...]).astype(BF16)

    blocks = _nbytes((tm, d), BF16) + _nbytes((tn, d), BF16) + _nbytes((tm, tn), BF16)
    return _pcall(
        body, name="ffn_bwd_da", grid=(t // tm, f // tn),
        in_specs=[pl.BlockSpec((tm, d), lambda i, j: (i, 0)), pl.BlockSpec((tn, d), lambda i, j: (j, 0))],
        out_specs=[pl.BlockSpec((tm, tn), lambda i, j: (i, j))],
        out_shape=[SDS((t, f), BF16)],
        sem=("parallel", "parallel"), blocks=blocks, temps=_nbytes((tm, tn), F32), comm=comm,
    )(dxb, w_down)


def _ffn_mid_bwd(da, u0, cw, cb, nseq, comm=None):
    _, t, f = u0.shape
    s = t // nseq
    c = _tile(f, 128, LANES)

    def body(da_ref, u_ref, w_ref, b_ref, du_ref, gw_ref, gb_ref):
        @pl.when(pl.program_id(1) == 0)
        def _():
            gw_ref[...] = jnp.zeros_like(gw_ref)
            gb_ref[...] = jnp.zeros_like(gb_ref)

        row = lax.broadcasted_iota(jnp.int32, (s, c), 0)
        us, act = [], []
        for p in range(2):
            u = u_ref[p].astype(F32)
            u1 = _shift_down(u, 1, row)
            u2 = _shift_down(u, 2, row)
            us.append((u, u1, u2))
            act.append(_conv3_rows(u, u1, u2, w_ref, p) + b_ref[p])
        ug, uv = act
        sg = _sigmoid(ug)
        dacc = da_ref[...].astype(F32)
        dug = dacc * uv * sg * (1.0 + ug * (1.0 - sg))
        duv = dacc * (ug * sg)
        for p, du in ((0, dug), (1, duv)):
            u, u1, u2 = us[p]
            gb_ref[p] += jnp.sum(du, axis=0, keepdims=True)
            gw_ref[p, pl.ds(0, 1), :] += jnp.sum(du * u2, axis=0, keepdims=True)
            gw_ref[p, pl.ds(1, 1), :] += jnp.sum(du * u1, axis=0, keepdims=True)
            gw_ref[p, pl.ds(2, 1), :] += jnp.sum(du * u, axis=0, keepdims=True)
            du_ref[p] = _conv3_rows(du, _shift_up(du, 1, row), _shift_up(du, 2, row), w_ref, p).astype(BF16)

    blocks = _nbytes((s, c), BF16) + 2 * _nbytes((2, s, c), BF16)
    return _pcall(
        body, name="ffn_mid_bwd", grid=(f // c, nseq),
        in_specs=[pl.BlockSpec((s, c), lambda j, b: (b, j)),
                  pl.BlockSpec((2, s, c), lambda j, b: (0, b, j)),
                  pl.BlockSpec((2, 3, c), lambda j, b: (0, 0, j)),
                  pl.BlockSpec((2, 1, c), lambda j, b: (0, 0, j))],
        out_specs=[pl.BlockSpec((2, s, c), lambda j, b: (0, b, j)),
                   pl.BlockSpec((2, 3, c), lambda j, b: (0, 0, j)),
                   pl.BlockSpec((2, 1, c), lambda j, b: (0, 0, j))],
        out_shape=[SDS((2, t, f), BF16), SDS((2, 3, f), F32), SDS((2, 1, f), F32)],
        sem=("parallel", "arbitrary"), blocks=blocks, temps=20 * _nbytes((s, c), F32), comm=comm,
    )(da, u0, cw, cb)


def _wgrad(a, b, name, *, tr, tn, b_plane_of=None, out_shards=None, comm=None):
    t, m = a.shape
    n_total = b.shape[-1] * (b.shape[0] if b.ndim == 3 else 1)
    tk = _tile(t, 1024, SUBLANES_BF16)
    nk = t // tk

    def body(a_ref, b_ref, o_ref, acc):
        _acc_over(pl.program_id(2), nk, _dot_ta(a_ref[...], b_ref[...]), acc, o_ref)

    if b.ndim == 3:
        b_spec = pl.BlockSpec((None, tk, tn), lambda r, n, k: (b_plane_of(n)[0], k, b_plane_of(n)[1]))
    else:
        b_spec = pl.BlockSpec((tk, tn), lambda r, n, k: (k, n))
    if out_shards is None:
        o_spec = pl.BlockSpec((tr, tn), lambda r, n, k: (r, n))
        o_shape = SDS((m, n_total), BF16)
    else:
        nps = n_total // out_shards // tn
        o_spec = pl.BlockSpec((None, tr, tn), lambda r, n, k: (n // nps, r, n % nps))
        o_shape = SDS((out_shards, m, n_total // out_shards), BF16)
    blocks = _nbytes((tk, tr), BF16) + _nbytes((tk, tn), BF16) + _nbytes((tr, tn), BF16)
    return _pcall(
        body, name=name, grid=(m // tr, n_total // tn, nk),
        in_specs=[pl.BlockSpec((tk, tr), lambda r, n, k: (k, r)), b_spec],
        out_specs=[o_spec], out_shape=[o_shape],
        scratch_shapes=[pltpu.VMEM((tr, tn), F32)],
        sem=("parallel", "parallel", "arbitrary"), blocks=blocks, temps=2 * _nbytes((tr, tn), F32), comm=comm,
    )(a, b)


def _wgrad3(lhs3, rhs3, comm=None):
    nw, t, d = lhs3.shape
    tk = _tile(t, 1024, SUBLANES_BF16)
    nk = t // tk

    def body(a_ref, b_ref, o_ref, acc):
        _acc_over(pl.program_id(1), nk, _dot_ta(a_ref[...], b_ref[...]), acc, o_ref)

    blocks = 2 * _nbytes((tk, d), BF16) + _nbytes((d, d), BF16)
    return _pcall(
        body, name="wgrad_sq3", grid=(nw, nk),
        in_specs=[pl.BlockSpec((None, tk, d), lambda w, k: (w, k, 0)),
                  pl.BlockSpec((None, tk, d), lambda w, k: (w, k, 0))],
        out_specs=[pl.BlockSpec((None, d, d), lambda w, k: (w, 0, 0))],
        out_shape=[SDS((nw, d, d), BF16)],
        scratch_shapes=[pltpu.VMEM((d, d), F32)],
        sem=("parallel", "arbitrary"), blocks=blocks, temps=2 * _nbytes((d, d), F32), comm=comm,
    )(lhs3, rhs3)


def _ffn_bwd_dx1(du0, w_up, x1, dx2, g_ffn, n_planes_out, comm=None):
    _, t, f = du0.shape
    d = x1.shape[1]
    ws = w_up.shape[2]
    tm = _tile(t, 512, SUBLANES_BF16)
    tn = _tile(ws, 1408, LANES)
    nps = ws // tn
    npp = f // tn
    nk = 2 * npp

    def body(du_ref, w_ref, x1_ref, dx2_ref, g_ref, dx1_ref, dxb_ref, gg_ref, acc):
        i = pl.program_id(0)
        k = pl.program_id(1)
        part = _dot_tb(du_ref[...], w_ref[...])

        @pl.when(k == 0)
        def _():
            acc[...] = part

        @pl.when(k > 0)
        def _():
            acc[...] += part

        @pl.when((i == 0) & (k == 0))
        def _():
            gg_ref[...] = jnp.zeros_like(gg_ref)

        @pl.when(k == nk - 1)
        def _():
            dh = acc[...]
            xh, inv = _rms_fwd(x1_ref[...])
            gg_ref[...] += jnp.sum(dh * xh, axis=0, keepdims=True)
            dx1 = dx2_ref[...] + _rms_bwd(dh, xh, inv, g_ref[...])
            dx1_ref[...] = dx1
            dxb_ref[...] = dx1.astype(BF16)

    blocks = (_nbytes((tm, tn), BF16) + _nbytes((d, tn), BF16) + 3 * _nbytes((tm, d), F32) + _nbytes((tm, d), BF16))
    return _pcall(
        body, name="ffn_bwd_dx1", grid=(t // tm, nk),
        in_specs=[pl.BlockSpec((None, tm, tn), lambda i, k: (k // npp, i, k % npp)),
                  pl.BlockSpec((None, d, tn), lambda i, k: (k // nps, 0, k % nps)),
                  pl.BlockSpec((tm, d), lambda i, k: (i, 0)),
                  pl.BlockSpec((tm, d), lambda i, k: (i, 0)),
                  pl.BlockSpec((1, d), lambda i, k: (0, 0))],
        out_specs=[pl.BlockSpec((tm, d), lambda i, k: (i, 0)),
                   pl.BlockSpec((None, tm, d), lambda i, k: (n_planes_out - 1, i, 0)),
                   pl.BlockSpec((1, d), lambda i, k: (0, 0))],
        out_shape=[SDS((t, d), F32), SDS((n_planes_out, t, d), BF16), SDS((1, d), F32)],
        scratch_shapes=[pltpu.VMEM((tm, d), F32)],
        sem=("arbitrary", "arbitrary"), blocks=blocks, temps=8 * _nbytes((tm, d), F32), comm=comm,
    )(du0, w_up, x1, dx2, g_ffn)


def _mixer_bwd(rhs3, z, ypc, w3, comm=None):
    _, t, d = rhs3.shape
    tm = _tile(t, 256, SUBLANES_BF16)

    def body(dx_ref, zgp, zgc, ypc_ref, w_ref, dyo, dzo, dpq):
        dm = _dot_tb(dx_ref[...], w_ref[2])
        sp = _sigmoid(zgp[...].astype(F32))
        sc = _sigmoid(zgc[...].astype(F32))
        dyp = (dm * sp).astype(BF16)
        dyc = (dm * sc).astype(BF16)
        dzo[0] = (dm * ypc_ref[0].astype(F32) * sp * (1.0 - sp)).astype(BF16)
        dzo[1] = (dm * ypc_ref[1].astype(F32) * sc * (1.0 - sc)).astype(BF16)
        dyo[0] = dyp
        dyo[1] = dyc
        dpq[0] = _dot_tb(dyp, w_ref[0]).astype(BF16)
        dpq[1] = _dot_tb(dyc, w_ref[1]).astype(BF16)

    blocks = _nbytes((tm, d), BF16) * 3 + _nbytes((2, tm, d), BF16) * 4 + _nbytes((3, d, d), BF16)
    return _pcall(
        body, name="mixer_bwd", grid=(t // tm,),
        in_specs=[pl.BlockSpec((None, tm, d), lambda i: (2, i, 0)),
                  pl.BlockSpec((tm, d), lambda i: (i, 4)),
                  pl.BlockSpec((tm, d), lambda i: (i, 5)),
                  pl.BlockSpec((2, tm, d), lambda i: (0, i, 0)),
                  pl.BlockSpec((3, d, d), lambda i: (0, 0, 0))],
        out_specs=[pl.BlockSpec((2, tm, d), lambda i: (0, i, 0)),
                   pl.BlockSpec((2, tm, d), lambda i: (2, i, 0)),
                   pl.BlockSpec((2, tm, d), lambda i: (0, i, 0))],
        out_shape=[SDS(rhs3.shape, BF16), SDS((N_SPLITS, t, d), BF16), SDS((2, t, d), BF16)],
        input_output_aliases={0: 0},
        sem=("parallel",), blocks=blocks, temps=8 * _nbytes((tm, d), F32), comm=comm,
    )(rhs3, z, z, ypc, w3)


def _conv_bwd(dz, dpq, z, conv_w, nseq, comm=None):
    _, t, d = dz.shape
    s = t // nseq
    c = _tile(d, 128, LANES)
    nb = d // c

    def body(dz_in, dq_ref, zb, zc, zv, cw, dzo, gw_ref):
        del dz_in

        @pl.when(pl.program_id(1) == 0)
        def _():
            gw_ref[...] = jnp.zeros_like(gw_ref)

        row = lax.broadcasted_iota(jnp.int32, (s, c), 0)
        b = zb[...].astype(F32)
        cm = zc[...].astype(F32)
        v = zv[...].astype(F32)
        cv = cm * v
        cv1 = _shift_down(cv, 1, row)
        cv2 = _shift_down(cv, 2, row)
        w0, w1, w2 = cw[pl.ds(0, 1), :], cw[pl.ds(1, 1), :], cw[pl.ds(2, 1), :]
        cc = w2 * cv + w1 * cv1 + w0 * cv2
        dq = dq_ref[...].astype(F32)
        dzo[0] = (dq * cc).astype(BF16)
        dcc = dq * b
        gw_ref[pl.ds(0, 1), :] += jnp.sum(dcc * cv2, axis=0, keepdims=True)
        gw_ref[pl.ds(1, 1), :] += jnp.sum(dcc * cv1, axis=0, keepdims=True)
        gw_ref[pl.ds(2, 1), :] += jnp.sum(dcc * cv, axis=0, keepdims=True)
        dcv = w2 * dcc + w1 * _shift_up(dcc, 1, row) + w0 * _shift_up(dcc, 2, row)
        dzo[1] = (dcv * v).astype(BF16)
        dzo[2] = (dcv * cm).astype(BF16)

    blocks = 4 * _nbytes((s, c), BF16) + _nbytes((3, s, c), BF16)
    return _pcall(
        body, name="conv_bwd", grid=(nb, nseq),
        in_specs=[ANY,
                  pl.BlockSpec((None, s, c), lambda j, b: (1, b, j)),
                  pl.BlockSpec((s, c), lambda j, b: (b, nb + j)),
                  pl.BlockSpec((s, c), lambda j, b: (b, 2 * nb + j)),
                  pl.BlockSpec((s, c), lambda j, b: (b, 3 * nb + j)),
                  pl.BlockSpec((3, c), lambda j, b: (0, j))],
        out_specs=[pl.BlockSpec((3, s, c), lambda j, b: (0, b, j)),
                   pl.BlockSpec((3, c), lambda j, b: (0, j))],
        out_shape=[SDS(dz.shape, BF16), SDS((3, d), F32)],
        input_output_aliases={0: 0},
        sem=("parallel", "arbitrary"), blocks=blocks, temps=16 * _nbytes((s, c), F32), comm=comm,
    )(dz, dpq, z, z, z, conv_w)


def _pool_bwd_call(dz, dpq, z, pool_w, pool_scale, nseq, comm=None):
    _, t, d = dz.shape
    s = t // nseq
    c = d // N_GROUPS

    def body(dz_in, dp_ref, zp, pw, ps, dzo, gpw_ref, gps_ref):
        del dz_in
        j = pl.program_id(0)

        @pl.when(pl.program_id(1) == 0)
        def _():
            gpw_ref[...] = jnp.zeros_like(gpw_ref)
            gps_ref[...] = jnp.zeros_like(gps_ref)

        row = lax.broadcasted_iota(jnp.int32, (s, c), 0)
        for gi, win in enumerate(POOL_WINDOWS):
            @pl.when(j == gi)
            def _(win=win):
                pb = _pool_fwd(zp[...].astype(F32), win, row).astype(BF16)
                plin = _dot(pb, pw[...])
                dps = dp_ref[...].astype(F32)
                gps_ref[...] += jnp.sum(dps * plin, axis=0, keepdims=True)
                dplb = (dps * ps[...]).astype(BF16)
                gpw_ref[...] += _dot_ta(pb, dplb)
                dzo[...] = _pool_bwd(_dot_tb(dplb, pw[...]), win, row).astype(BF16)

    blocks = 3 * _nbytes((s, c), BF16) + _nbytes((c, c), BF16) + _nbytes((c, c), F32)
    return _pcall(
        body, name="pool_bwd", grid=(N_GROUPS, nseq),
        in_specs=[ANY,
                  pl.BlockSpec((None, s, c), lambda j, b: (0, b, j)),
                  pl.BlockSpec((s, c), lambda j, b: (b, j)),
                  pl.BlockSpec((None, c, c), lambda j, b: (j, 0, 0)),
                  pl.BlockSpec((1, c), lambda j, b: (0, j))],
        out_specs=[pl.BlockSpec((None, s, c), lambda j, b: (3, b, j)),
                   pl.BlockSpec((None, c, c), lambda j, b: (j, 0, 0)),
                   pl.BlockSpec((1, c), lambda j, b: (0, j))],
        out_shape=[SDS(dz.shape, BF16), SDS((N_GROUPS, c, c), F32), SDS((1, d), F32)],
        input_output_aliases={0: 0},
        sem=("parallel", "arbitrary"), blocks=blocks, temps=10 * _nbytes((s, c), F32), comm=comm,
    )(dz, dpq, z, pool_w, pool_scale)


def _dz_plane(zb):
    return jnp.where(zb < 4, (zb + 3) % 4, zb)


def _wgrad_in(h1, dz, nsh, comm=None):
    t, d = h1.shape
    ws = N_SPLITS * d // nsh
    kb = _tile(math.gcd(d, ws), 512, LANES)
    npl = d // kb
    nps = ws // kb
    tk = _tile(t, 1024, SUBLANES_BF16)
    nk = t // tk

    def body(a_ref, b_ref, o_ref, acc):
        _acc_over(pl.program_id(1), nk, _dot_ta(a_ref[...], b_ref[...]), acc, o_ref)

    blocks = _nbytes((tk, d), BF16) + _nbytes((tk, kb), BF16) + _nbytes((d, kb), BF16)
    return _pcall(
        body, name="wgrad_in", grid=(N_SPLITS * npl, nk),
        in_specs=[pl.BlockSpec((tk, d), lambda cb, k: (k, 0)),
                  pl.BlockSpec((None, tk, kb), lambda cb, k: (_dz_plane(cb // npl), k, cb % npl))],
        out_specs=[pl.BlockSpec((None, d, kb), lambda cb, k: (cb // nps, 0, cb % nps))],
        out_shape=[SDS((nsh, d, ws), BF16)],
        scratch_shapes=[pltpu.VMEM((d, kb), F32)],
        sem=("parallel", "arbitrary"), blocks=blocks, temps=2 * _nbytes((d, kb), F32), comm=comm,
    )(h1, dz)


def _mixer_bwd_dx(dz, w_in, x, dx1, g_mix, comm=None):
    _, t, d = dz.shape
    ws = w_in.shape[2]
    tm = _tile(t, 512, SUBLANES_BF16)
    kb = _tile(math.gcd(d, ws), 512, LANES)
    npl = d // kb
    nps = ws // kb
    nk = N_SPLITS * npl

    def body(dz_ref, w_ref, x_ref, dx1_ref, g_ref, dx_ref, gg_ref, acc):
        i = pl.program_id(0)
        k = pl.program_id(1)
        part = _dot_tb(dz_ref[...], w_ref[...])

        @pl.when(k == 0)
        def _():
            acc[...] = part

        @pl.when(k > 0)
        def _():
            acc[...] += part

        @pl.when((i == 0) & (k == 0))
        def _():
            gg_ref[...] = jnp.zeros_like(gg_ref)

        @pl.when(k == nk - 1)
        def _():
            dh = acc[...]
            xh, inv = _rms_fwd(x_ref[...])
            gg_ref[...] += jnp.sum(dh * xh, axis=0, keepdims=True)
            dx_ref[...] = dx1_ref[...] + _rms_bwd(dh, xh, inv, g_ref[...])

    blocks = _nbytes((tm, kb), BF16) + _nbytes((d, kb), BF16) + 3 * _nbytes((tm, d), F32)
    return _pcall(
        body, name="mixer_bwd_dx", grid=(t // tm, nk),
        in_specs=[pl.BlockSpec((None, tm, kb), lambda i, k: (_dz_plane(k // npl), i, k % npl)),
                  pl.BlockSpec((None, d, kb), lambda i, k: (k // nps, 0, k % nps)),
                  pl.BlockSpec((tm, d), lambda i, k: (i, 0)),
                  pl.BlockSpec((tm, d), lambda i, k: (i, 0)),
                  pl.BlockSpec((1, d), lambda i, k: (0, 0))],
        out_specs=[pl.BlockSpec((tm, d), lambda i, k: (i, 0)),
                   pl.BlockSpec((1, d), lambda i, k: (0, 0))],
        out_shape=[SDS((t, d), F32), SDS((1, d), F32)],
        scratch_shapes=[pltpu.VMEM((tm, d), F32)],
        sem=("arbitrary", "arbitrary"), blocks=blocks, temps=8 * _nbytes((tm, d), F32), comm=comm,
    )(dz, w_in, x, dx1, g_mix)


N_BIG = 5
SHARD_MAJOR = (0, 2)
ROWS_DIM1 = (1, 4)


def _ds(start, size, align):
    if isinstance(start, int):
        return pl.ds(start, size)
    return pl.ds(pl.multiple_of(start, align), size)


def _piece(a, ref, k, h):
    if a in SHARD_MAJOR:
        r = ref.shape[1] // 2
        return ref.at[k, _ds(h * r, r, SUBLANES_BF16), :]
    if a in ROWS_DIM1:
        r = ref.shape[1] // 8
        return ref.at[:, _ds((2 * k + h) * r, r, SUBLANES_BF16), :]
    r = ref.shape[0] // 8
    return ref.at[_ds((2 * k + h) * r, r, SUBLANES_BF16), :]


def _half(a, ref, h):
    if a in ROWS_DIM1:
        r = ref.shape[1] // 2
        return ref.at[:, _ds(h * r, r, SUBLANES_BF16), :]
    r = ref.shape[0] // 2
    return ref.at[_ds(h * r, r, SUBLANES_BF16), :]


def _piece_shape(a, full_shape):
    if a in SHARD_MAJOR:
        return (full_shape[1] // 2, full_shape[2])
    if a in ROWS_DIM1:
        return (full_shape[0], full_shape[1] // 8, full_shape[2])
    return (full_shape[0] // 8, full_shape[1])


def _shard_shape(a, full_shape):
    if a in SHARD_MAJOR:
        return (full_shape[1], full_shape[2])
    if a in ROWS_DIM1:
        return (full_shape[0], full_shape[1] // 4, full_shape[2])
    return (full_shape[0] // 4, full_shape[1])


def _rows_axis(a):
    return 1 if a in ROWS_DIM1 else 0


def _piece_block(a, full_shape):
    ps = _piece_shape(a, full_shape)
    if a in SHARD_MAJOR:
        return (None,) + ps, lambda k, c: (k, c, 0)
    if a in ROWS_DIM1:
        return ps, lambda k, c: (0, 2 * k + c, 0)
    return ps, lambda k, c: (2 * k + c, 0)


def _coords():
    return lax.axis_index("x"), lax.axis_index("y"), lax.axis_index("c")


def _peer_chips(x, y):
    return [(1 - x, y), (x, 1 - y), (1 - x, 1 - y)]


def _remote(src, dst, ssem, rsem, dev):
    return pltpu.make_async_remote_copy(src_ref=src, dst_ref=dst, send_sem=ssem, recv_sem=rsem,
                                        device_id=dev, device_id_type=MESH)


def _dma_sems(*counts):
    return [pltpu.SemaphoreType.DMA((n,)) for n in counts]


def _symmetric(ins, out_shapes, sems, copies, aliases=None):
    def start(cins, couts, csems):
        for cp in copies(cins, couts, csems):
            cp.start()

    def finish(cins, couts, csems):
        for cp in copies(cins, couts, csems):
            cp.wait()

    return _Comm(ins, out_shapes, sems, start, finish, aliases)


def _gather_comm(arrs, locs, full_shapes):
    n = len(arrs)

    def own(cins, couts, csems):
        x, y, c = _coords()
        j = 2 * x + y
        return [_remote(_half(a, cins[q], h), _piece(a, couts[q], j, h), csems[0].at[2 * q + h],
                        csems[1].at[2 * q + h], (x, y, 1 - c)) for q, a in enumerate(arrs) for h in range(2)]

    def sends(cins, couts, csems):
        x, y, c = _coords()
        j = 2 * x + y
        return [_remote(_half(a, cins[q], c), _piece(a, couts[q], j, c), csems[2].at[3 * q + i],
                        csems[3].at[3 * q + i], (px, py, c))
                for q, a in enumerate(arrs) for i, (px, py) in enumerate(_peer_chips(x, y))]

    def forwards(couts, csems, half_of):
        x, y, c = _coords()
        out = []
        for q, a in enumerate(arrs):
            for i, (px, py) in enumerate(_peer_chips(x, y)):
                landed = _piece(a, couts[q], 2 * px + py, half_of(c))
                out.append(_remote(landed, landed, csems[4].at[3 * q + i], csems[5].at[3 * q + i], (x, y, 1 - c)))
        return out

    def start(cins, couts, csems):
        for cp in sends(cins, couts, csems) + own(cins, couts, csems):
            cp.start()

    def finish(cins, couts, csems):
        fw = forwards(couts, csems, lambda c: c)
        for cp, f in zip(sends(cins, couts, csems), fw):
            cp.wait_recv()
            f.start()
        for f in forwards(couts, csems, lambda c: 1 - c):
            f.wait_recv()
        for cp in sends(cins, couts, csems) + fw:
            cp.wait_send()
        for cp in own(cins, couts, csems):
            cp.wait()

    return _Comm([locs[a] for a in arrs], [SDS(full_shapes[a], BF16) for a in arrs],
                 _dma_sems(2 * n, 2 * n, 3 * n, 3 * n, 3 * n, 3 * n), start, finish)


def _halves_comm(arrs, gbs):
    n = len(arrs)

    def copies(cins, couts, csems):
        x, y, c = _coords()
        return [_remote(_piece(a, cins[q], k, 1 - c), couts[q].at[k], csems[0].at[4 * q + k], csems[1].at[4 * q + k],
                        (x, y, 1 - c)) for q, a in enumerate(arrs) for k in range(4)]

    return _symmetric([gbs[a] for a in arrs], [SDS((4,) + _piece_shape(a, gbs[a].shape), BF16) for a in arrs],
                      _dma_sems(4 * n, 4 * n), copies)


def _chips_comm(arrs, ps):
    n = len(arrs)

    def copies(cins, couts, csems):
        x, y, c = _coords()
        return [_remote(cins[q].at[2 * px + py], couts[q].at[i], csems[0].at[3 * q + i], csems[1].at[3 * q + i],
                        (px, py, c)) for q in range(n) for i, (px, py) in enumerate(_peer_chips(x, y))]

    return _symmetric([ps[a] for a in arrs], [SDS((3,) + ps[a].shape[1:], BF16) for a in arrs],
                      _dma_sems(3 * n, 3 * n), copies)


def _result_comm(arrs, gs):
    n = len(arrs)

    def copies(cins, couts, csems):
        x, y, c = _coords()
        return [_remote(_half(a, cins[q], c), _half(a, couts[q], c), csems[0].at[q], csems[1].at[q], (x, y, 1 - c))
                for q, a in enumerate(arrs)]

    return _symmetric([gs[a] for a in arrs], [SDS(gs[a].shape, F32) for a in arrs], _dma_sems(n, n), copies,
                      aliases={q: q for q in range(n)})


def _add_halves(arrs, gbs, lands, c_arr, name):
    n = len(arrs)

    def body(c_ref, *refs):
        del c_ref
        for q in range(n):
            refs[2 * n + q][...] = (refs[q][...].astype(F32) + refs[n + q][...].astype(F32)).astype(BF16)

    g_specs, l_specs, o_specs, blocks = [], [], [], 0
    for a in arrs:
        bs, imap = _piece_block(a, gbs[a].shape)
        ps = _piece_shape(a, gbs[a].shape)
        g_specs.append(pl.BlockSpec(bs, lambda k, c_ref, imap=imap: imap(k, c_ref[0])))
        nd = len(ps)
        l_specs.append(pl.BlockSpec((None,) + ps, lambda k, c_ref, nd=nd: (k,) + (0,) * nd))
        o_specs.append(pl.BlockSpec((None,) + ps, lambda k, c_ref, nd=nd: (k,) + (0,) * nd))
        blocks += 3 * _nbytes(ps, BF16)
    return list(pl.pallas_call(
        body, name=name,
        grid_spec=pltpu.PrefetchScalarGridSpec(
            num_scalar_prefetch=1, grid=(4,), in_specs=g_specs + l_specs, out_specs=o_specs),
        out_shape=[SDS((4,) + _piece_shape(a, gbs[a].shape), BF16) for a in arrs],
        compiler_params=_params(("parallel",), blocks, blocks),
    )(c_arr, *[gbs[a] for a in arrs], *lands))


def _sum_chips(a, p, land, shard_shape, jc_arr, name):
    ps = land.shape[1:]
    ax = _rows_axis(a)
    rows = ps[ax]
    nsub = 2 if rows % (2 * SUBLANES_BF16) == 0 else 1
    bs = tuple(r // nsub if q == ax else r for q, r in enumerate(ps))
    nd = len(ps)

    def at_rows(v):
        return tuple(v if q == ax else 0 for q in range(nd))

    def body(jc_ref, p_ref, l_ref, o_ref):
        del jc_ref
        acc = p_ref[...].astype(F32) + l_ref[0].astype(F32)
        acc = acc + l_ref[1].astype(F32)
        o_ref[...] = acc + l_ref[2].astype(F32)

    blocks = 4 * _nbytes(bs, BF16) + _nbytes(bs, F32)
    return pl.pallas_call(
        body, name=name,
        grid_spec=pltpu.PrefetchScalarGridSpec(
            num_scalar_prefetch=1, grid=(nsub,),
            in_specs=[pl.BlockSpec((None,) + bs, lambda s, jc: (jc[0],) + at_rows(s)),
                      pl.BlockSpec((3,) + bs, lambda s, jc: (0,) + at_rows(s))],
            out_specs=pl.BlockSpec(bs, lambda s, jc: at_rows(jc[1] * nsub + s))),
        out_shape=SDS(shard_shape, F32),
        compiler_params=_params(("parallel",), blocks, 2 * _nbytes(bs, F32)),
    )(jc_arr, p, land)


def _allreduce_small(v, name):
    rows = v.shape[0]

    def body(v_ref, o_ref, slots, lsem, ssem, rsem):
        x, y, c = _coords()
        me = 4 * x + 2 * y + c
        own = pltpu.make_async_copy(v_ref, slots.at[me], lsem)
        own.start()
        copies = []
        for dlt in range(1, 8):
            px = 1 - x if (dlt >> 2) & 1 else x
            py = 1 - y if (dlt >> 1) & 1 else y
            pc = 1 - c if dlt & 1 else c
            cp = _remote(v_ref, slots.at[me], ssem.at[dlt - 1], rsem.at[dlt - 1], (px, py, pc))
            cp.start()
            copies.append(cp)
        for cp in copies:
            cp.wait()
        own.wait()
        acc = slots[0]
        for i in range(1, 8):
            acc = acc + slots[i]
        o_ref[...] = acc

    return pl.pallas_call(
        body, name=name,
        in_specs=[pl.BlockSpec(memory_space=pltpu.VMEM)],
        out_specs=pl.BlockSpec(memory_space=pltpu.VMEM),
        out_shape=SDS((rows, LANES), F32),
        scratch_shapes=[pltpu.VMEM((8, rows, LANES), F32), pltpu.SemaphoreType.DMA,
                        pltpu.SemaphoreType.DMA((7,)), pltpu.SemaphoreType.DMA((7,))],
    )(v)


def _adamw(w, g, m, v, name, g_plane=None):
    rows, cols = w.shape
    tr = _tile(rows, max(SUBLANES_F32, (256 * 1024 // cols) // SUBLANES_F32 * SUBLANES_F32), SUBLANES_F32)

    def body(w_ref, g_ref, m_ref, v_ref, d_ref, mo_ref, vo_ref):
        gr = g_ref[...]
        mn = ADAM_B1 * m_ref[...] + (1.0 - ADAM_B1) * gr
        vn = ADAM_B2 * v_ref[...] + (1.0 - ADAM_B2) * (gr * gr)
        m_hat = mn / (1.0 - ADAM_B1 ** ADAM_STEP)
        v_hat = vn / (1.0 - ADAM_B2 ** ADAM_STEP)
        d_ref[...] = -ADAM_LR * (m_hat / (jnp.sqrt(v_hat) + ADAM_EPS) + ADAM_WD * w_ref[...])
        mo_ref[...] = mn
        vo_ref[...] = vn

    spec = pl.BlockSpec((tr, cols), lambda i: (i, 0))
    g_spec = spec if g_plane is None else pl.BlockSpec((None, tr, cols), lambda i: (g_plane, i, 0))
    return pl.pallas_call(
        body, name=name, grid=(rows // tr,),
        in_specs=[spec, g_spec, spec, spec], out_specs=[spec, spec, spec],
        out_shape=[SDS((rows, cols), F32)] * 3,
        compiler_params=_params(("parallel",), 7 * _nbytes((tr, cols), F32), 4 * _nbytes((tr, cols), F32)),
    )(w, g, m, v)


def _pack(parts):
    rows = []
    for p in parts:
        r = p.reshape(-1, LANES)
        pad = (-r.shape[0]) % SUBLANES_F32
        if pad:
            r = jnp.pad(r, ((0, pad), (0, 0)))
        rows.append(r)
    return jnp.concatenate(rows, axis=0)


def _unpack(packed, shapes):
    out, at = [], 0
    for s in shapes:
        n = 1
        for q in s:
            n *= q
        r = n // LANES
        out.append(packed[at:at + r].reshape(s))
        at += r + (-r) % SUBLANES_F32
    return out


def kernel(x, norm_mix, w_in, pool_w, pool_scale, w_pool_proj, conv_w, w_conv_out, w_o, norm_ffn, w_up, ffn_conv_w, ffn_conv_b, w_down, norm_final, loss_target, m_norm_mix, m_w_in, m_pool_w, m_pool_scale, m_w_pool_proj, m_conv_w, m_w_conv_out, m_w_o, m_norm_ffn, m_w_up, m_ffn_conv_w, m_ffn_conv_b, m_w_down, m_norm_final, v_norm_mix, v_w_in, v_pool_w, v_pool_scale, v_w_pool_proj, v_conv_w, v_w_conv_out, v_w_o, v_norm_ffn, v_w_up, v_ffn_conv_w, v_ffn_conv_b, v_w_down, v_norm_final):
    nseq, seq, d = x.shape
    t = nseq * seq
    f = w_down.shape[1] * 4
    c = d // N_GROUPS
    xy = lax.axis_index("x") * 2 + lax.axis_index("y")
    c_arr = lax.axis_index("c").astype(jnp.int32).reshape(1)
    jc_arr = jnp.stack([xy, lax.axis_index("c")]).astype(jnp.int32)
    nsh = 4
    zero = jnp.zeros((), jnp.int32)

    locs = [w_in[0].astype(BF16),
            jnp.stack([w_pool_proj[0], w_conv_out[0], w_o[0]]).astype(BF16),
            w_up[0].astype(BF16), w_down[0].astype(BF16), pool_w[0].astype(BF16)]
    full_shapes = [(nsh, d, N_SPLITS * d // nsh), (3, d, d), (nsh, d, 2 * f // nsh), (f, d), (N_GROUPS, c, c)]

    cw_pad = lax.dynamic_update_slice(jnp.zeros((3, d), F32), conv_w[0], (zero, xy * (d // 4)))
    fw_pad = lax.dynamic_update_slice(jnp.zeros((3, 2 * f), F32), ffn_conv_w[0], (zero, xy * (f // 2)))
    small_w = _pack([cw_pad, fw_pad]) * 0.5
    conv_w_f, ffn_cw_f = _unpack(_allreduce_small(small_w, "gather_small"), [(3, d), (3, 2 * f)])
    ffn_cw_p = ffn_cw_f.reshape(3, 2, f).transpose(1, 0, 2)
    ffn_cb_p = ffn_conv_b.reshape(2, 1, f)

    x2d = x.reshape(t, d)
    tgt = loss_target.reshape(t, d)
    (w_in_f,) = _run_comm(_gather_comm([0], locs, full_shapes), "gather_w_in")
    (z, h1), (w3_f, pool_w_f, w_down_f) = _fwd_in(x2d, norm_mix, w_in_f, _gather_comm([1, 4, 3], locs, full_shapes))
    lhs3 = _mixer_mid_fwd(z, pool_w_f, pool_scale, conv_w_f, nseq)
    (lhs3, ypc, x1, h2), (w_up_f,) = _mixer_out(lhs3, z, x2d, w3_f, norm_ffn, _gather_comm([2], locs, full_shapes))
    u0 = _ffn_up(h2, w_up_f, f)
    act = _ffn_mid_fwd(u0, ffn_cw_p, ffn_cb_p, nseq)
    dx2, dx2b, loss11, g_norm_final = _ffn_down_loss(act, w_down_f, x1, tgt, norm_final.reshape(1, d))

    gbs, lands, ps, lands2, rs = {}, {}, {}, {}, {}
    tn_up = _tile(2 * f // nsh, 1408, LANES)
    npp = f // tn_up

    def add(arrs, name):
        for a, p in zip(arrs, _add_halves(arrs, gbs, [lands[a] for a in arrs], c_arr, name)):
            ps[a] = p

    def summed(a):
        rs[a] = _sum_chips(a, ps[a], lands2[a], _shard_shape(a, full_shapes[a]), jc_arr, "sum_chips_%d" % a)

    (gbs[3],), _ = _wgrad(act, dx2b, "wgrad_down", tr=tn_up, tn=d)
    (da,), (lands[3],) = _ffn_bwd_da(dx2b, w_down_f, _halves_comm([3], gbs))
    add([3], "add_halves_down")
    (du0, g_ffn_cw_p, g_ffn_cb_p), (lands2[3],) = _ffn_mid_bwd(da, u0, ffn_cw_p, ffn_cb_p, nseq, _chips_comm([3], ps))
    summed(3)
    (gbs[2],), (rs[3],) = _wgrad(h2, du0, "wgrad_up", tr=d, tn=tn_up, b_plane_of=lambda n: (n // npp, n % npp),
                                 out_shards=nsh, comm=_result_comm([3], rs))
    (dx1, rhs3, g_norm_ffn), (lands[2],) = _ffn_bwd_dx1(du0, w_up_f, x1, dx2, norm_ffn, 3, _halves_comm([2], gbs))
    add([2], "add_halves_up")
    (rhs3, dz, dpq), (lands2[2],) = _mixer_bwd(rhs3, z, ypc, w3_f, _chips_comm([2], ps))
    summed(2)
    (gbs[1],), (rs[2],) = _wgrad3(lhs3, rhs3, _result_comm([2], rs))
    (dz, g_conv_w), (lands[1],) = _conv_bwd(dz, dpq, z, conv_w_f, nseq, _halves_comm([1], gbs))
    add([1], "add_halves_sq3")
    (dz, g_pool_w, g_pool_scale), (lands2[1],) = _pool_bwd_call(dz, dpq, z, pool_w_f, pool_scale, nseq,
                                                                 _chips_comm([1], ps))
    summed(1)
    gbs[4] = g_pool_w.astype(BF16)
    (gbs[0],), (rs[1],) = _wgrad_in(h1, dz, nsh, _result_comm([1], rs))
    lands[0], lands[4] = _run_comm(_halves_comm([0, 4], gbs), "exchange_halves_in")
    add([0, 4], "add_halves_in")
    (grad_x, g_norm_mix), (lands2[0], lands2[4]) = _mixer_bwd_dx(dz, w_in_f, x2d, dx1, norm_mix,
                                                                 _chips_comm([0, 4], ps))
    summed(0)
    summed(4)
    rs[0], rs[4] = _run_comm(_result_comm([0, 4], rs), "exchange_result_in")
    g_in_s, g3_s, g_up_s, g_down_s, g_pool_s = rs[0], rs[1], rs[2], rs[3], rs[4]

    g_ffn_cw = g_ffn_cw_p.transpose(1, 0, 2).reshape(3, 2 * f)
    small_shapes = [(1, d), (1, d), (1, d), (1, 2 * f), (d,), (3, d), (3, 2 * f)]
    small = _allreduce_small(_pack([g_norm_mix, g_pool_scale, g_norm_ffn, g_ffn_cb_p.reshape(1, 2 * f),
                                    g_norm_final.reshape(d), g_conv_w, g_ffn_cw]), "allreduce_small")
    gs_norm_mix, gs_pool_scale, gs_norm_ffn, gs_ffn_cb, gs_norm_final, gs_conv_w, gs_ffn_cw = _unpack(small, small_shapes)
    gs_conv_w = lax.dynamic_slice(gs_conv_w, (zero, xy * (d // 4)), (3, d // 4))
    gs_ffn_cw = lax.dynamic_slice(gs_ffn_cw, (zero, xy * (f // 2)), (3, f // 2))

    def upd(w, g, m, v, name, g_plane=None):
        shape = w.shape
        rows = 1
        for q in shape[:-1]:
            rows *= q
        g2 = g if g_plane is not None else g.reshape(rows, shape[-1])
        dlt, mn, vn = _adamw(w.reshape(rows, shape[-1]), g2, m.reshape(rows, shape[-1]), v.reshape(rows, shape[-1]),
                             name, g_plane)
        return dlt.reshape(shape), mn.reshape(shape), vn.reshape(shape)

    big = {
        "w_in": (g_in_s.reshape(w_in.shape), upd(w_in, g_in_s, m_w_in, v_w_in, "adamw_w_in")),
        "pool_w": (g_pool_s.reshape(pool_w.shape), upd(pool_w, g_pool_s, m_pool_w, v_pool_w, "adamw_pool_w")),
        "w_pool_proj": (g3_s[0][None], upd(w_pool_proj, g3_s, m_w_pool_proj, v_w_pool_proj, "adamw_w_pool_proj", 0)),
        "w_conv_out": (g3_s[1][None], upd(w_conv_out, g3_s, m_w_conv_out, v_w_conv_out, "adamw_w_conv_out", 1)),
        "w_o": (g3_s[2][None], upd(w_o, g3_s, m_w_o, v_w_o, "adamw_w_o", 2)),
        "w_up": (g_up_s.reshape(w_up.shape), upd(w_up, g_up_s, m_w_up, v_w_up, "adamw_w_up")),
        "w_down": (g_down_s.reshape(w_down.shape), upd(w_down, g_down_s, m_w_down, v_w_down, "adamw_w_down")),
    }

    small_names = ["norm_mix", "pool_scale", "norm_ffn", "ffn_conv_b", "norm_final", "conv_w", "ffn_conv_w"]
    small_ws = [norm_mix, pool_scale, norm_ffn, ffn_conv_b, norm_final, conv_w, ffn_conv_w]
    small_ms = [m_norm_mix, m_pool_scale, m_norm_ffn, m_ffn_conv_b, m_norm_final, m_conv_w, m_ffn_conv_w]
    small_vs = [v_norm_mix, v_pool_scale, v_norm_ffn, v_ffn_conv_b, v_norm_final, v_conv_w, v_ffn_conv_w]
    small_gs = [gs_norm_mix, gs_pool_scale, gs_norm_ffn, gs_ffn_cb, gs_norm_final, gs_conv_w, gs_ffn_cw]
    sd, sm, sv = _adamw(_pack(small_ws), _pack(small_gs), _pack(small_ms), _pack(small_vs), "adamw_small")
    shapes = [w.shape for w in small_ws]
    sd, sm, sv = _unpack(sd, shapes), _unpack(sm, shapes), _unpack(sv, shapes)
    res = dict(big)
    for i, nm in enumerate(small_names):
        res[nm] = (small_gs[i].reshape(shapes[i]), (sd[i], sm[i], sv[i]))

    loss = lax.psum(loss11[0, 0], ("x", "y", "c"))
    order = ["norm_mix", "w_in", "pool_w", "pool_scale", "w_pool_proj", "conv_w", "w_conv_out", "w_o", "norm_ffn",
             "w_up", "ffn_conv_w", "ffn_conv_b", "w_down", "norm_final"]
    return (loss, grad_x.reshape(x.shape), *[res[n][0] for n in order], *[res[n][1][0] for n in order],
            *[res[n][1][1] for n in order], *[res[n][1][2] for n in order])
```

```python
import math

import jax
import jax.numpy as jnp
from jax import lax
from jax.experimental import pallas as pl
from jax.experimental.pallas import tpu as pltpu

F32 = jnp.float32
BF16 = jnp.bfloat16
SDS = jax.ShapeDtypeStruct
MESH = pl.DeviceIdType.MESH

RMS_EPS = 1e-6
POOL_WINDOWS = (2, 4, 8, 16)
N_GROUPS = len(POOL_WINDOWS)
N_SPLITS = 6

ADAM_LR = 0.001
ADAM_B1 = 0.9
ADAM_B2 = 0.999
ADAM_EPS = 1e-08
ADAM_WD = 0.01
ADAM_STEP = 10

LANES = 128
SUBLANES_F32 = 8
SUBLANES_BF16 = 16
VMEM_BYTES = 64 * 1024 * 1024
VMEM_CAP = VMEM_BYTES - 8 * 1024 * 1024
VMEM_FLOOR = 16 * 1024 * 1024

ANY = pl.BlockSpec(memory_space=pl.ANY)


def _tile(dim, pref, align):
    if dim <= pref:
        return dim
    t = (pref // align) * align
    while t >= align:
        if dim % t == 0:
            return t
        t -= align
    return dim


def _nbytes(shape, dtype):
    n = 1
    for s in shape:
        n *= s
    return n * jnp.dtype(dtype).itemsize


def _params(sem, block_bytes, temp_bytes=0):
    need = 2 * block_bytes + temp_bytes + 4 * 1024 * 1024
    return pltpu.CompilerParams(dimension_semantics=sem, vmem_limit_bytes=int(min(max(need, VMEM_FLOOR), VMEM_CAP)))


class _Comm:
    def __init__(self, ins, out_shapes, sems, start, finish, aliases=None):
        self.ins = list(ins)
        self.out_shapes = list(out_shapes)
        self.sems = list(sems)
        self.start = start
        self.finish = finish
        self.aliases = dict(aliases or {})


def _pcall(body, *, name, grid, in_specs, out_specs, out_shape, sem, blocks, temps=0, scratch_shapes=(),
           input_output_aliases=None, comm=None):
    in_specs = list(in_specs)
    out_specs = list(out_specs)
    out_shape = list(out_shape)
    scratch_shapes = list(scratch_shapes)
    aliases = dict(input_output_aliases or {})
    n_in, n_out, n_scr = len(in_specs), len(out_shape), len(scratch_shapes)
    if comm is None:
        call = pl.pallas_call(
            body, name=name, grid=grid, in_specs=in_specs, out_specs=out_specs, out_shape=out_shape,
            scratch_shapes=scratch_shapes, input_output_aliases=aliases,
            compiler_params=_params(sem, blocks, temps))
        return lambda *args: (list(call(*args)), [])

    nci, nco = len(comm.ins), len(comm.out_shapes)

    def hosted(*refs):
        ins = refs[:n_in]
        cins = refs[n_in:n_in + nci]
        outs = refs[n_in + nci:n_in + nci + n_out]
        couts = refs[n_in + nci + n_out:n_in + nci + n_out + nco]
        scr = refs[n_in + nci + n_out + nco:n_in + nci + n_out + nco + n_scr]
        csems = refs[n_in + nci + n_out + nco + n_scr:]
        first = None
        last = None
        for q, g in enumerate(grid):
            pid = pl.program_id(q)
            first = (pid == 0) if first is None else first & (pid == 0)
            last = (pid == g - 1) if last is None else last & (pid == g - 1)

        @pl.when(first)
        def _():
            comm.start(cins, couts, csems)

        body(*ins, *outs, *scr)

        @pl.when(last)
        def _():
            comm.finish(cins, couts, csems)

    for i, o in comm.aliases.items():
        aliases[n_in + i] = n_out + o
    call = pl.pallas_call(
        hosted, name=name, grid=grid, in_specs=in_specs + [ANY] * nci, out_specs=out_specs + [ANY] * nco,
        out_shape=out_shape + comm.out_shapes, scratch_shapes=scratch_shapes + comm.sems,
        input_output_aliases=aliases,
        compiler_params=_params(("arbitrary",) * len(grid), blocks, temps))

    def run(*args):
        res = call(*args, *comm.ins)
        return list(res[:n_out]), list(res[n_out:])

    return run


def _run_comm(comm, name):
    def body(*refs):
        nci, nco = len(comm.ins), len(comm.out_shapes)
        cins, couts, csems = refs[:nci], refs[nci:nci + nco], refs[nci + nco:]
        comm.start(cins, couts, csems)
        comm.finish(cins, couts, csems)

    return list(pl.pallas_call(
        body, name=name, in_specs=[ANY] * len(comm.ins), out_specs=[ANY] * len(comm.out_shapes),
        out_shape=comm.out_shapes, scratch_shapes=comm.sems, input_output_aliases=comm.aliases,
    )(*comm.ins))


def _dot(a, b):
    return jnp.dot(a, b, preferred_element_type=F32)


def _dot_tb(a, b):
    return lax.dot_general(a, b, (((1,), (1,)), ((), ())), preferred_element_type=F32)


def _dot_ta(a, b):
    return lax.dot_general(a, b, (((0,), (0,)), ((), ())), preferred_element_type=F32)


def _rms_fwd(x):
    inv = lax.rsqrt(jnp.mean(x * x, axis=-1, keepdims=True) + RMS_EPS)
    return x * inv, inv


def _rms_bwd(dy, xhat, inv, g):
    gd = dy * g
    return inv * (gd - xhat * jnp.mean(gd * xhat, axis=-1, keepdims=True))


def _sigmoid(x):
    return 1.0 / (1.0 + jnp.exp(-x))


def _shift_down(x, k, row):
    return jnp.where(row >= k, pltpu.roll(x, k, 0), 0.0)


def _shift_up(x, k, row):
    s = x.shape[0]
    return jnp.where(row < s - k, pltpu.roll(x, s - k, 0), 0.0)


def _pool_fwd(u, win, row):
    s = u
    k = 1
    while k < win:
        s = s + _shift_down(s, k, row)
        k *= 2
    cnt = jnp.minimum(row + 1, win).astype(F32)
    return s / cnt - u


def _pool_bwd(dp, win, row):
    cnt = jnp.minimum(row + 1, win).astype(F32)
    s = dp / cnt
    k = 1
    while k < win:
        s = s + _shift_up(s, k, row)
        k *= 2
    return s - dp


def _acc_over(k, nk, part, acc, o_ref):
    @pl.when(k == 0)
    def _():
        acc[...] = part

    @pl.when(k > 0)
    def _():
        acc[...] += part

    @pl.when(k == nk - 1)
    def _():
        o_ref[...] = acc[...].astype(o_ref.dtype)


def _fwd_in(x, g, w, comm=None):
    t, d = x.shape
    nsh, _, ws = w.shape
    n = nsh * ws
    tm = _tile(t, 1024, SUBLANES_BF16)
    tn = _tile(ws, 1536, LANES)
    nps = ws // tn

    def body(x_ref, g_ref, w_ref, z_ref, h_ref, hs):
        @pl.when(pl.program_id(1) == 0)
        def _():
            xh, _ = _rms_fwd(x_ref[...])
            h = (xh * g_ref[...]).astype(BF16)
            hs[...] = h
            h_ref[...] = h

        z_ref[...] = _dot(hs[...], w_ref[...]).astype(BF16)

    blocks = _nbytes((tm, d), F32) + _nbytes((d, tn), BF16) + _nbytes((tm, tn), BF16) + _nbytes((tm, d), BF16)
    return _pcall(
        body, name="fwd_in", grid=(t // tm, n // tn),
        in_specs=[pl.BlockSpec((tm, d), lambda i, j: (i, 0)), pl.BlockSpec((1, d), lambda i, j: (0, 0)),
                  pl.BlockSpec((None, d, tn), lambda i, j: (j // nps, 0, j % nps))],
        out_specs=[pl.BlockSpec((tm, tn), lambda i, j: (i, j)), pl.BlockSpec((tm, d), lambda i, j: (i, 0))],
        out_shape=[SDS((t, n), BF16), SDS((t, d), BF16)],
        scratch_shapes=[pltpu.VMEM((tm, d), BF16)],
        sem=("parallel", "arbitrary"), blocks=blocks, temps=3 * _nbytes((tm, d), F32), comm=comm,
    )(x, g, w)


def _mixer_mid_fwd(z, pool_w, pool_scale, conv_w, nseq):
    t = z.shape[0]
    d = pool_scale.shape[1]
    s = t // nseq
    c = d // N_GROUPS

    def body(zp, zb, zc, zv, pw, ps, cw, o):
        j = pl.program_id(1)
        row = lax.broadcasted_iota(jnp.int32, (s, c), 0)
        for gi, win in enumerate(POOL_WINDOWS):
            @pl.when(j == gi)
            def _(win=win):
                pooled = _pool_fwd(zp[...].astype(F32), win, row)
                o[0] = (_dot(pooled.astype(BF16), pw[...]) * ps[...]).astype(BF16)

        cv = zc[...].astype(F32) * zv[...].astype(F32)
        cc = (cw[pl.ds(2, 1), :] * cv + cw[pl.ds(1, 1), :] * _shift_down(cv, 1, row)
              + cw[pl.ds(0, 1), :] * _shift_down(cv, 2, row))
        o[1] = (zb[...].astype(F32) * cc).astype(BF16)

    blocks = 4 * _nbytes((s, c), BF16) + _nbytes((c, c), BF16) + _nbytes((2, s, c), BF16)
    outs, _ = _pcall(
        body, name="mixer_mid_fwd", grid=(nseq, N_GROUPS),
        in_specs=[pl.BlockSpec((s, c), lambda b, j: (b, j)),
                  pl.BlockSpec((s, c), lambda b, j: (b, N_GROUPS + j)),
                  pl.BlockSpec((s, c), lambda b, j: (b, 2 * N_GROUPS + j)),
                  pl.BlockSpec((s, c), lambda b, j: (b, 3 * N_GROUPS + j)),
                  pl.BlockSpec((None, c, c), lambda b, j: (j, 0, 0)),
                  pl.BlockSpec((1, c), lambda b, j: (0, j)),
                  pl.BlockSpec((3, c), lambda b, j: (0, j))],
        out_specs=[pl.BlockSpec((2, s, c), lambda b, j: (0, b, j))],
        out_shape=[SDS((3, t, d), BF16)],
        sem=("parallel", "parallel"), blocks=blocks, temps=8 * _nbytes((s, c), F32),
    )(z, z, z, z, pool_w, pool_scale, conv_w)
    return outs[0]


def _mixer_out(lhs3, z, x, w3, g_ffn, comm=None):
    t, d = x.shape
    tm = _tile(t, 256, SUBLANES_BF16)

    def body(pq, zgp, zgc, x_ref, w_ref, g_ref, mrg, ypc, x1o, h2o):
        yp = _dot(pq[0], w_ref[0])
        yc = _dot(pq[1], w_ref[1])
        m = _sigmoid(zgp[...].astype(F32)) * yp + _sigmoid(zgc[...].astype(F32)) * yc
        mb = m.astype(BF16)
        x1 = x_ref[...] + _dot(mb, w_ref[2])
        ypc[0] = yp.astype(BF16)
        ypc[1] = yc.astype(BF16)
        mrg[...] = mb
        x1o[...] = x1
        xh, _ = _rms_fwd(x1)
        h2o[...] = (xh * g_ref[...]).astype(BF16)

    blocks = (_nbytes((2, tm, d), BF16) * 2 + _nbytes((tm, d), BF16) * 4 + _nbytes((tm, d), F32) * 2
              + _nbytes((3, d, d), BF16))
    return _pcall(
        body, name="mixer_out", grid=(t // tm,),
        in_specs=[pl.BlockSpec((2, tm, d), lambda i: (0, i, 0)),
                  pl.BlockSpec((tm, d), lambda i: (i, 4)),
                  pl.BlockSpec((tm, d), lambda i: (i, 5)),
                  pl.BlockSpec((tm, d), lambda i: (i, 0)),
                  pl.BlockSpec((3, d, d), lambda i: (0, 0, 0)),
                  pl.BlockSpec((1, d), lambda i: (0, 0))],
        out_specs=[pl.BlockSpec((None, tm, d), lambda i: (2, i, 0)),
                   pl.BlockSpec((2, tm, d), lambda i: (0, i, 0)),
                   pl.BlockSpec((tm, d), lambda i: (i, 0)),
                   pl.BlockSpec((tm, d), lambda i: (i, 0))],
        out_shape=[SDS(lhs3.shape, BF16), SDS((2, t, d), BF16), SDS((t, d), F32), SDS((t, d), BF16)],
        input_output_aliases={0: 0},
        sem=("parallel",), blocks=blocks, temps=8 * _nbytes((tm, d), F32), comm=comm,
    )(lhs3, z, z, x, w3, g_ffn)


def _ffn_up(h2, w_up, f, comm=None):
    t, d = h2.shape
    _, _, ws = w_up.shape
    tm = _tile(t, 1024, SUBLANES_BF16)
    tn = _tile(ws, 1408, LANES)
    nps = ws // tn
    npp = f // tn

    def body(h_ref, w_ref, o_ref):
        o_ref[...] = _dot(h_ref[...], w_ref[...]).astype(BF16)

    blocks = _nbytes((tm, d), BF16) + _nbytes((d, tn), BF16) + _nbytes((tm, tn), BF16)
    return _pcall(
        body, name="ffn_up", grid=(t // tm, 2 * npp),
        in_specs=[pl.BlockSpec((tm, d), lambda i, j: (i, 0)),
                  pl.BlockSpec((None, d, tn), lambda i, j: (j // nps, 0, j % nps))],
        out_specs=[pl.BlockSpec((None, tm, tn), lambda i, j: (j // npp, i, j % npp))],
        out_shape=[SDS((2, t, f), BF16)],
        sem=("parallel", "parallel"), blocks=blocks, temps=_nbytes((tm, tn), F32), comm=comm,
    )(h2, w_up)


def _conv3_rows(u, u1, u2, w_ref, p):
    return w_ref[p, pl.ds(2, 1), :] * u + w_ref[p, pl.ds(1, 1), :] * u1 + w_ref[p, pl.ds(0, 1), :] * u2


def _ffn_mid_fwd(u0, cw, cb, nseq):
    _, t, f = u0.shape
    s = t // nseq
    c = _tile(f, 256, LANES)

    def body(u_ref, w_ref, b_ref, a_ref):
        row = lax.broadcasted_iota(jnp.int32, (s, c), 0)
        act = []
        for p in range(2):
            u = u_ref[p].astype(F32)
            act.append(_conv3_rows(u, _shift_down(u, 1, row), _shift_down(u, 2, row), w_ref, p) + b_ref[p])
        ug, uv = act
        a_ref[...] = (ug * _sigmoid(ug) * uv).astype(BF16)

    blocks = _nbytes((2, s, c), BF16) + _nbytes((s, c), BF16)
    outs, _ = _pcall(
        body, name="ffn_mid_fwd", grid=(f // c, nseq),
        in_specs=[pl.BlockSpec((2, s, c), lambda j, b: (0, b, j)),
                  pl.BlockSpec((2, 3, c), lambda j, b: (0, 0, j)),
                  pl.BlockSpec((2, 1, c), lambda j, b: (0, 0, j))],
        out_specs=[pl.BlockSpec((s, c), lambda j, b: (b, j))],
        out_shape=[SDS((t, f), BF16)],
        sem=("parallel", "parallel"), blocks=blocks, temps=8 * _nbytes((s, c), F32),
    )(u0, cw, cb)
    return outs[0]


def _ffn_down_loss(a, w_down, x1, tgt, g_fin):
    t, f = a.shape
    d = x1.shape[1]
    tm = _tile(t, 256, SUBLANES_BF16)
    nsteps = t // tm

    def body(a_ref, w_ref, x1_ref, t_ref, g_ref, dx_ref, dxb_ref, loss_ref, gg_ref, lacc):
        i = pl.program_id(0)

        @pl.when(i == 0)
        def _():
            lacc[...] = jnp.zeros_like(lacc)
            gg_ref[...] = jnp.zeros_like(gg_ref)

        x2 = x1_ref[...] + _dot(a_ref[...], w_ref[...])
        xh, inv = _rms_fwd(x2)
        g = g_ref[...]
        e = xh * g - t_ref[...]
        lacc[...] += jnp.sum(e * e, axis=0, keepdims=True)
        dy = e * (1.0 / d)
        gg_ref[...] += jnp.sum(dy * xh, axis=0, keepdims=True)
        dx2 = _rms_bwd(dy, xh, inv, g)
        dx_ref[...] = dx2
        dxb_ref[...] = dx2.astype(BF16)

        @pl.when(i == nsteps - 1)
        def _():
            loss_ref[...] = jnp.sum(lacc[...], axis=1, keepdims=True) * (0.5 / d)

    blocks = (_nbytes((tm, f), BF16) + _nbytes((f, d), BF16) + 3 * _nbytes((tm, d), F32) + _nbytes((tm, d), BF16))
    outs, _ = _pcall(
        body, name="ffn_down_loss", grid=(nsteps,),
        in_specs=[pl.BlockSpec((tm, f), lambda i: (i, 0)), pl.BlockSpec((f, d), lambda i: (0, 0)),
                  pl.BlockSpec((tm, d), lambda i: (i, 0)), pl.BlockSpec((tm, d), lambda i: (i, 0)),
                  pl.BlockSpec((1, d), lambda i: (0, 0))],
        out_specs=[pl.BlockSpec((tm, d), lambda i: (i, 0)), pl.BlockSpec((tm, d), lambda i: (i, 0)),
                   pl.BlockSpec((1, 1), lambda i: (0, 0)), pl.BlockSpec((1, d), lambda i: (0, 0))],
        out_shape=[SDS((t, d), F32), SDS((t, d), BF16), SDS((1, 1), F32), SDS((1, d), F32)],
        scratch_shapes=[pltpu.VMEM((1, d), F32)],
        sem=("arbitrary",), blocks=blocks, temps=8 * _nbytes((tm, d), F32),
    )(a, w_down, x1, tgt, g_fin)
    return outs


def _ffn_bwd_da(dxb, w_down, comm=None):
    t, d = dxb.shape
    f = w_down.shape[0]
    tm = _tile(t, 1024, SUBLANES_BF16)
    tn = _tile(f, 1408, LANES)

    def body(x_ref, w_ref, o_ref):
        o_ref[...] = _dot_tb(x_ref[...], w_ref[...]).astype(BF16)

    blocks = _nbytes((tm, d), BF16) + _nbytes((tn, d), BF16) + _nbytes((tm, tn), BF16)
    return _pcall(
        body, name="ffn_bwd_da", grid=(t // tm, f // tn),
        in_specs=[pl.BlockSpec((tm, d), lambda i, j: (i, 0)), pl.BlockSpec((tn, d), lambda i, j: (j, 0))],
        out_specs=[pl.BlockSpec((tm, tn), lambda i, j: (i, j))],
        out_shape=[SDS((t, f), BF16)],
        sem=("parallel", "parallel"), blocks=blocks, temps=_nbytes((tm, tn), F32), comm=comm,
    )(dxb, w_down)


def _ffn_mid_bwd(da, u0, cw, cb, nseq, comm=None):
    _, t, f = u0.shape
    s = t // nseq
    c = _tile(f, 128, LANES)

    def body(da_ref, u_ref, w_ref, b_ref, du_ref, gw_ref, gb_ref):
        @pl.when(pl.program_id(1) == 0)
        def _():
            gw_ref[...] = jnp.zeros_like(gw_ref)
            gb_ref[...] = jnp.zeros_like(gb_ref)

        row = lax.broadcasted_iota(jnp.int32, (s, c), 0)
        us, act = [], []
        for p in range(2):
            u = u_ref[p].astype(F32)
            u1 = _shift_down(u, 1, row)
            u2 = _shift_down(u, 2, row)
            us.append((u, u1, u2))
            act.append(_conv3_rows(u, u1, u2, w_ref, p) + b_ref[p])
        ug, uv = act
        sg = _sigmoid(ug)
        dacc = da_ref[...].astype(F32)
        dug = dacc * uv * sg * (1.0 + ug * (1.0 - sg))
        duv = dacc * (ug * sg)
        for p, du in ((0, dug), (1, duv)):
            u, u1, u2 = us[p]
            gb_ref[p] += jnp.sum(du, axis=0, keepdims=True)
            gw_ref[p, pl.ds(0, 1), :] += jnp.sum(du * u2, axis=0, keepdims=True)
            gw_ref[p, pl.ds(1, 1), :] += jnp.sum(du * u1, axis=0, keepdims=True)
            gw_ref[p, pl.ds(2, 1), :] += jnp.sum(du * u, axis=0, keepdims=True)
            du_ref[p] = _conv3_rows(du, _shift_up(du, 1, row), _shift_up(du, 2, row), w_ref, p).astype(BF16)

    blocks = _nbytes((s, c), BF16) + 2 * _nbytes((2, s, c), BF16)
    return _pcall(
        body, name="ffn_mid_bwd", grid=(f // c, nseq),
        in_specs=[pl.BlockSpec((s, c), lambda j, b: (b, j)),
                  pl.BlockSpec((2, s, c), lambda j, b: (0, b, j)),
                  pl.BlockSpec((2, 3, c), lambda j, b: (0, 0, j)),
                  pl.BlockSpec((2, 1, c), lambda j, b: (0, 0, j))],
        out_specs=[pl.BlockSpec((2, s, c), lambda j, b: (0, b, j)),
                   pl.BlockSpec((2, 3, c), lambda j, b: (0, 0, j)),
                   pl.BlockSpec((2, 1, c), lambda j, b: (0, 0, j))],
        out_shape=[SDS((2, t, f), BF16), SDS((2, 3, f), F32), SDS((2, 1, f), F32)],
        sem=("parallel", "arbitrary"), blocks=blocks, temps=20 * _nbytes((s, c), F32), comm=comm,
    )(da, u0, cw, cb)


def _wgrad(a, b, name, *, tr, tn, b_plane_of=None, out_shards=None, comm=None):
    t, m = a.shape
    n_total = b.shape[-1] * (b.shape[0] if b.ndim == 3 else 1)
    tk = _tile(t, 1024, SUBLANES_BF16)
    nk = t // tk

    def body(a_ref, b_ref, o_ref, acc):
        _acc_over(pl.program_id(2), nk, _dot_ta(a_ref[...], b_ref[...]), acc, o_ref)

    if b.ndim == 3:
        b_spec = pl.BlockSpec((None, tk, tn), lambda r, n, k: (b_plane_of(n)[0], k, b_plane_of(n)[1]))
    else:
        b_spec = pl.BlockSpec((tk, tn), lambda r, n, k: (k, n))
    if out_shards is None:
        o_spec = pl.BlockSpec((tr, tn), lambda r, n, k: (r, n))
        o_shape = SDS((m, n_total), BF16)
    else:
        nps = n_total // out_shards // tn
        o_spec = pl.BlockSpec((None, tr, tn), lambda r, n, k: (n // nps, r, n % nps))
        o_shape = SDS((out_shards, m, n_total // out_shards), BF16)
    blocks = _nbytes((tk, tr), BF16) + _nbytes((tk, tn), BF16) + _nbytes((tr, tn), BF16)
    return _pcall(
        body, name=name, grid=(m // tr, n_total // tn, nk),
        in_specs=[pl.BlockSpec((tk, tr), lambda r, n, k: (k, r)), b_spec],
        out_specs=[o_spec], out_shape=[o_shape],
        scratch_shapes=[pltpu.VMEM((tr, tn), F32)],
        sem=("parallel", "parallel", "arbitrary"), blocks=blocks, temps=2 * _nbytes((tr, tn), F32), comm=comm,
    )(a, b)


def _wgrad3(lhs3, rhs3, comm=None):
    nw, t, d = lhs3.shape
    tk = _tile(t, 1024, SUBLANES_BF16)
    nk = t // tk

    def body(a_ref, b_ref, o_ref, acc):
        _acc_over(pl.program_id(1), nk, _dot_ta(a_ref[...], b_ref[...]), acc, o_ref)

    blocks = 2 * _nbytes((tk, d), BF16) + _nbytes((d, d), BF16)
    return _pcall(
        body, name="wgrad_sq3", grid=(nw, nk),
        in_specs=[pl.BlockSpec((None, tk, d), lambda w, k: (w, k, 0)),
                  pl.BlockSpec((None, tk, d), lambda w, k: (w, k, 0))],
        out_specs=[pl.BlockSpec((None, d, d), lambda w, k: (w, 0, 0))],
        out_shape=[SDS((nw, d, d), BF16)],
        scratch_shapes=[pltpu.VMEM((d, d), F32)],
        sem=("parallel", "arbitrary"), blocks=blocks, temps=2 * _nbytes((d, d), F32), comm=comm,
    )(lhs3, rhs3)


def _ffn_bwd_dx1(du0, w_up, x1, dx2, g_ffn, n_planes_out, comm=None):
    _, t, f = du0.shape
    d = x1.shape[1]
    nsh, _, ws = w_up.shape
    tm = _tile(t, 256, SUBLANES_BF16)
    spp = f // ws

    def body(du_ref, w_ref, x1_ref, dx2_ref, g_ref, dx1_ref, dxb_ref, gg_ref):
        @pl.when(pl.program_id(0) == 0)
        def _():
            gg_ref[...] = jnp.zeros_like(gg_ref)

        dh = None
        for k in range(nsh):
            part = _dot_tb(du_ref[k // spp, :, (k % spp) * ws:(k % spp + 1) * ws], w_ref[k])
            dh = part if dh is None else dh + part
        xh, inv = _rms_fwd(x1_ref[...])
        gg_ref[...] += jnp.sum(dh * xh, axis=0, keepdims=True)
        dx1 = dx2_ref[...] + _rms_bwd(dh, xh, inv, g_ref[...])
        dx1_ref[...] = dx1
        dxb_ref[...] = dx1.astype(BF16)

    blocks = _nbytes((2, tm, f), BF16) + 3 * _nbytes((tm, d), F32) + _nbytes((tm, d), BF16)
    return _pcall(
        body, name="ffn_bwd_dx1", grid=(t // tm,),
        in_specs=[pl.BlockSpec((2, tm, f), lambda i: (0, i, 0)),
                  pl.BlockSpec((nsh, d, ws), lambda i: (0, 0, 0), pipeline_mode=pl.Buffered(1)),
                  pl.BlockSpec((tm, d), lambda i: (i, 0)),
                  pl.BlockSpec((tm, d), lambda i: (i, 0)),
                  pl.BlockSpec((1, d), lambda i: (0, 0))],
        out_specs=[pl.BlockSpec((tm, d), lambda i: (i, 0)),
                   pl.BlockSpec((None, tm, d), lambda i: (n_planes_out - 1, i, 0)),
                   pl.BlockSpec((1, d), lambda i: (0, 0))],
        out_shape=[SDS((t, d), F32), SDS((n_planes_out, t, d), BF16), SDS((1, d), F32)],
        sem=("arbitrary",), blocks=blocks, temps=_nbytes(w_up.shape, BF16) + 8 * _nbytes((tm, d), F32), comm=comm,
    )(du0, w_up, x1, dx2, g_ffn)


def _mixer_bwd(rhs3, z, ypc, w3, comm=None):
    _, t, d = rhs3.shape
    tm = _tile(t, 256, SUBLANES_BF16)

    def body(dx_ref, zgp, zgc, ypc_ref, w_ref, dyo, dzo, dpq):
        dm = _dot_tb(dx_ref[...], w_ref[2])
        sp = _sigmoid(zgp[...].astype(F32))
        sc = _sigmoid(zgc[...].astype(F32))
        dyp = (dm * sp).astype(BF16)
        dyc = (dm * sc).astype(BF16)
        dzo[0] = (dm * ypc_ref[0].astype(F32) * sp * (1.0 - sp)).astype(BF16)
        dzo[1] = (dm * ypc_ref[1].astype(F32) * sc * (1.0 - sc)).astype(BF16)
        dyo[0] = dyp
        dyo[1] = dyc
        dpq[0] = _dot_tb(dyp, w_ref[0]).astype(BF16)
        dpq[1] = _dot_tb(dyc, w_ref[1]).astype(BF16)

    blocks = _nbytes((tm, d), BF16) * 3 + _nbytes((2, tm, d), BF16) * 4 + _nbytes((3, d, d), BF16)
    return _pcall(
        body, name="mixer_bwd", grid=(t // tm,),
        in_specs=[pl.BlockSpec((None, tm, d), lambda i: (2, i, 0)),
                  pl.BlockSpec((tm, d), lambda i: (i, 4)),
                  pl.BlockSpec((tm, d), lambda i: (i, 5)),
                  pl.BlockSpec((2, tm, d), lambda i: (0, i, 0)),
                  pl.BlockSpec((3, d, d), lambda i: (0, 0, 0))],
        out_specs=[pl.BlockSpec((2, tm, d), lambda i: (0, i, 0)),
                   pl.BlockSpec((2, tm, d), lambda i: (2, i, 0)),
                   pl.BlockSpec((2, tm, d), lambda i: (0, i, 0))],
        out_shape=[SDS(rhs3.shape, BF16), SDS((N_SPLITS, t, d), BF16), SDS((2, t, d), BF16)],
        input_output_aliases={0: 0},
        sem=("parallel",), blocks=blocks, temps=8 * _nbytes((tm, d), F32), comm=comm,
    )(rhs3, z, z, ypc, w3)


def _conv_bwd(dz, dpq, z, conv_w, nseq, comm=None):
    _, t, d = dz.shape
    s = t // nseq
    c = _tile(d, 128, LANES)
    nb = d // c

    def body(dz_in, dq_ref, zb, zc, zv, cw, dzo, gw_ref):
        del dz_in

        @pl.when(pl.program_id(1) == 0)
        def _():
            gw_ref[...] = jnp.zeros_like(gw_ref)

        row = lax.broadcasted_iota(jnp.int32, (s, c), 0)
        b = zb[...].astype(F32)
        cm = zc[...].astype(F32)
        v = zv[...].astype(F32)
        cv = cm * v
        cv1 = _shift_down(cv, 1, row)
        cv2 = _shift_down(cv, 2, row)
        w0, w1, w2 = cw[pl.ds(0, 1), :], cw[pl.ds(1, 1), :], cw[pl.ds(2, 1), :]
        cc = w2 * cv + w1 * cv1 + w0 * cv2
        dq = dq_ref[...].astype(F32)
        dzo[0] = (dq * cc).astype(BF16)
        dcc = dq * b
        gw_ref[pl.ds(0, 1), :] += jnp.sum(dcc * cv2, axis=0, keepdims=True)
        gw_ref[pl.ds(1, 1), :] += jnp.sum(dcc * cv1, axis=0, keepdims=True)
        gw_ref[pl.ds(2, 1), :] += jnp.sum(dcc * cv, axis=0, keepdims=True)
        dcv = w2 * dcc + w1 * _shift_up(dcc, 1, row) + w0 * _shift_up(dcc, 2, row)
        dzo[1] = (dcv * v).astype(BF16)
        dzo[2] = (dcv * cm).astype(BF16)

    blocks = 4 * _nbytes((s, c), BF16) + _nbytes((3, s, c), BF16)
    return _pcall(
        body, name="conv_bwd", grid=(nb, nseq),
        in_specs=[ANY,
                  pl.BlockSpec((None, s, c), lambda j, b: (1, b, j)),
                  pl.BlockSpec((s, c), lambda j, b: (b, nb + j)),
                  pl.BlockSpec((s, c), lambda j, b: (b, 2 * nb + j)),
                  pl.BlockSpec((s, c), lambda j, b: (b, 3 * nb + j)),
                  pl.BlockSpec((3, c), lambda j, b: (0, j))],
        out_specs=[pl.BlockSpec((3, s, c), lambda j, b: (0, b, j)),
                   pl.BlockSpec((3, c), lambda j, b: (0, j))],
        out_shape=[SDS(dz.shape, BF16), SDS((3, d), F32)],
        input_output_aliases={0: 0},
        sem=("parallel", "arbitrary"), blocks=blocks, temps=16 * _nbytes((s, c), F32), comm=comm,
    )(dz, dpq, z, z, z, conv_w)


def _pool_bwd_call(dz, dpq, z, pool_w, pool_scale, nseq, comm=None):
    _, t, d = dz.shape
    s = t // nseq
    c = d // N_GROUPS

    def body(dz_in, dp_ref, zp, pw, ps, dzo, gpw_ref, gps_ref):
        del dz_in
        j = pl.program_id(0)

        @pl.when(pl.program_id(1) == 0)
        def _():
            gpw_ref[...] = jnp.zeros_like(gpw_ref)
            gps_ref[...] = jnp.zeros_like(gps_ref)

        row = lax.broadcasted_iota(jnp.int32, (s, c), 0)
        for gi, win in enumerate(POOL_WINDOWS):
            @pl.when(j == gi)
            def _(win=win):
                pb = _pool_fwd(zp[...].astype(F32), win, row).astype(BF16)
                plin = _dot(pb, pw[...])
                dps = dp_ref[...].astype(F32)
                gps_ref[...] += jnp.sum(dps * plin, axis=0, keepdims=True)
                dplb = (dps * ps[...]).astype(BF16)
                gpw_ref[...] += _dot_ta(pb, dplb)
                dzo[...] = _pool_bwd(_dot_tb(dplb, pw[...]), win, row).astype(BF16)

    blocks = 3 * _nbytes((s, c), BF16) + _nbytes((c, c), BF16) + _nbytes((c, c), F32)
    return _pcall(
        body, name="pool_bwd", grid=(N_GROUPS, nseq),
        in_specs=[ANY,
                  pl.BlockSpec((None, s, c), lambda j, b: (0, b, j)),
                  pl.BlockSpec((s, c), lambda j, b: (b, j)),
                  pl.BlockSpec((None, c, c), lambda j, b: (j, 0, 0)),
                  pl.BlockSpec((1, c), lambda j, b: (0, j))],
        out_specs=[pl.BlockSpec((None, s, c), lambda j, b: (3, b, j)),
                   pl.BlockSpec((None, c, c), lambda j, b: (j, 0, 0)),
                   pl.BlockSpec((1, c), lambda j, b: (0, j))],
        out_shape=[SDS(dz.shape, BF16), SDS((N_GROUPS, c, c), F32), SDS((1, d), F32)],
        input_output_aliases={0: 0},
        sem=("parallel", "arbitrary"), blocks=blocks, temps=10 * _nbytes((s, c), F32), comm=comm,
    )(dz, dpq, z, pool_w, pool_scale)


def _dz_plane(zb):
    return jnp.where(zb < 4, (zb + 3) % 4, zb)


def _wgrad_in(h1, dz, nsh, comm=None):
    t, d = h1.shape
    ws = N_SPLITS * d // nsh
    kb = _tile(math.gcd(d, ws), 512, LANES)
    npl = d // kb
    nps = ws // kb
    tk = _tile(t, 1024, SUBLANES_BF16)
    nk = t // tk

    def body(a_ref, b_ref, o_ref, acc):
        _acc_over(pl.program_id(1), nk, _dot_ta(a_ref[...], b_ref[...]), acc, o_ref)

    blocks = _nbytes((tk, d), BF16) + _nbytes((tk, kb), BF16) + _nbytes((d, kb), BF16)
    return _pcall(
        body, name="wgrad_in", grid=(N_SPLITS * npl, nk),
        in_specs=[pl.BlockSpec((tk, d), lambda cb, k: (k, 0)),
                  pl.BlockSpec((None, tk, kb), lambda cb, k: (_dz_plane(cb // npl), k, cb % npl))],
        out_specs=[pl.BlockSpec((None, d, kb), lambda cb, k: (cb // nps, 0, cb % nps))],
        out_shape=[SDS((nsh, d, ws), BF16)],
        scratch_shapes=[pltpu.VMEM((d, kb), F32)],
        sem=("parallel", "arbitrary"), blocks=blocks, temps=2 * _nbytes((d, kb), F32), comm=comm,
    )(h1, dz)


def _mixer_bwd_dx(dz, w_in, x, dx1, g_mix, comm=None):
    npln, t, d = dz.shape
    nsh, _, ws = w_in.shape
    tm = _tile(t, 256, SUBLANES_BF16)
    kb = _tile(math.gcd(d, ws), 512, LANES)
    npl = d // kb
    nps = ws // kb

    def body(dz_ref, w_ref, x_ref, dx1_ref, g_ref, dx_ref, gg_ref):
        @pl.when(pl.program_id(0) == 0)
        def _():
            gg_ref[...] = jnp.zeros_like(gg_ref)

        dh = None
        for cb in range(npln * npl):
            zb = cb // npl
            plane = (zb + 3) % 4 if zb < 4 else zb
            part = _dot_tb(dz_ref[plane, :, (cb % npl) * kb:(cb % npl + 1) * kb],
                           w_ref[cb // nps, :, (cb % nps) * kb:(cb % nps + 1) * kb])
            dh = part if dh is None else dh + part
        xh, inv = _rms_fwd(x_ref[...])
        gg_ref[...] += jnp.sum(dh * xh, axis=0, keepdims=True)
        dx_ref[...] = dx1_ref[...] + _rms_bwd(dh, xh, inv, g_ref[...])

    blocks = _nbytes((npln, tm, d), BF16) + 3 * _nbytes((tm, d), F32)
    return _pcall(
        body, name="mixer_bwd_dx", grid=(t // tm,),
        in_specs=[pl.BlockSpec((npln, tm, d), lambda i: (0, i, 0)),
                  pl.BlockSpec((nsh, d, ws), lambda i: (0, 0, 0), pipeline_mode=pl.Buffered(1)),
                  pl.BlockSpec((tm, d), lambda i: (i, 0)),
                  pl.BlockSpec((tm, d), lambda i: (i, 0)),
                  pl.BlockSpec((1, d), lambda i: (0, 0))],
        out_specs=[pl.BlockSpec((tm, d), lambda i: (i, 0)),
                   pl.BlockSpec((1, d), lambda i: (0, 0))],
        out_shape=[SDS((t, d), F32), SDS((1, d), F32)],
        sem=("arbitrary",), blocks=blocks, temps=_nbytes(w_in.shape, BF16) + 8 * _nbytes((tm, d), F32), comm=comm,
    )(dz, w_in, x, dx1, g_mix)


N_BIG = 5
SHARD_MAJOR = (0, 2)
ROWS_DIM1 = (1, 4)


def _ds(start, size, align):
    if isinstance(start, int):
        return pl.ds(start, size)
    return pl.ds(pl.multiple_of(start, align), size)


def _piece(a, ref, k, h):
    if a in SHARD_MAJOR:
        r = ref.shape[1] // 2
        return ref.at[k, _ds(h * r, r, SUBLANES_BF16), :]
    if a in ROWS_DIM1:
        r = ref.shape[1] // 8
        return ref.at[:, _ds((2 * k + h) * r, r, SUBLANES_BF16), :]
    r = ref.shape[0] // 8
    return ref.at[_ds((2 * k + h) * r, r, SUBLANES_BF16), :]


def _half(a, ref, h):
    if a in ROWS_DIM1:
        r = ref.shape[1] // 2
        return ref.at[:, _ds(h * r, r, SUBLANES_BF16), :]
    r = ref.shape[0] // 2
    return ref.at[_ds(h * r, r, SUBLANES_BF16), :]


def _piece_shape(a, full_shape):
    if a in SHARD_MAJOR:
        return (full_shape[1] // 2, full_shape[2])
    if a in ROWS_DIM1:
        return (full_shape[0], full_shape[1] // 8, full_shape[2])
    return (full_shape[0] // 8, full_shape[1])


def _shard_shape(a, full_shape):
    if a in SHARD_MAJOR:
        return (full_shape[1], full_shape[2])
    if a in ROWS_DIM1:
        return (full_shape[0], full_shape[1] // 4, full_shape[2])
    return (full_shape[0] // 4, full_shape[1])


def _rows_axis(a):
    return 1 if a in ROWS_DIM1 else 0


def _piece_block(a, full_shape):
    ps = _piece_shape(a, full_shape)
    if a in SHARD_MAJOR:
        return (None,) + ps, lambda k, c: (k, c, 0)
    if a in ROWS_DIM1:
        return ps, lambda k, c: (0, 2 * k + c, 0)
    return ps, lambda k, c: (2 * k + c, 0)


def _coords():
    return lax.axis_index("x"), lax.axis_index("y"), lax.axis_index("c")


def _peer_chips(x, y):
    return [(1 - x, y), (x, 1 - y), (1 - x, 1 - y)]


def _remote(src, dst, ssem, rsem, dev):
    return pltpu.make_async_remote_copy(src_ref=src, dst_ref=dst, send_sem=ssem, recv_sem=rsem,
                                        device_id=dev, device_id_type=MESH)


def _dma_sems(*counts):
    return [pltpu.SemaphoreType.DMA((n,)) for n in counts]


def _symmetric(ins, out_shapes, sems, copies, aliases=None):
    def start(cins, couts, csems):
        for cp in copies(cins, couts, csems):
            cp.start()

    def finish(cins, couts, csems):
        for cp in copies(cins, couts, csems):
            cp.wait()

    return _Comm(ins, out_shapes, sems, start, finish, aliases)


def _rows_part(a, ref, part):
    if part is None:
        return ref
    p, n = part
    ax = _rows_axis(a)
    r = ref.shape[ax] // n
    return ref.at[tuple(pl.ds(p * r, r) if q == ax else slice(None) for q in range(len(ref.shape)))]


def _merge(comms):
    ins, outs, sems, aliases, spans = [], [], [], {}, []
    for cm in comms:
        spans.append((len(ins), len(outs), len(sems)))
        for i, o in cm.aliases.items():
            aliases[len(ins) + i] = len(outs) + o
        ins += cm.ins
        outs += cm.out_shapes
        sems += cm.sems

    def each(fn_name):
        def run(cins, couts, csems):
            for cm, (i0, o0, s0) in zip(comms, spans):
                getattr(cm, fn_name)(cins[i0:i0 + len(cm.ins)], couts[o0:o0 + len(cm.out_shapes)],
                                     csems[s0:s0 + len(cm.sems)])
        return run

    return _Comm(ins, outs, sems, each("start"), each("finish"), aliases)


def _gather_comm(arrs, locs, full_shapes, part=None, into=None):
    n = len(arrs)

    def own(cins, couts, csems):
        x, y, c = _coords()
        j = 2 * x + y
        return [_remote(_rows_part(a, _half(a, cins[q], h), part), _rows_part(a, _piece(a, couts[q], j, h), part),
                        csems[0].at[2 * q + h], csems[1].at[2 * q + h], (x, y, 1 - c))
                for q, a in enumerate(arrs) for h in range(2)]

    def sends(cins, couts, csems):
        x, y, c = _coords()
        j = 2 * x + y
        return [_remote(_rows_part(a, _half(a, cins[q], c), part), _rows_part(a, _piece(a, couts[q], j, c), part),
                        csems[2].at[3 * q + i], csems[3].at[3 * q + i], (px, py, c))
                for q, a in enumerate(arrs) for i, (px, py) in enumerate(_peer_chips(x, y))]

    def forwards(couts, csems, half_of):
        x, y, c = _coords()
        out = []
        for q, a in enumerate(arrs):
            for i, (px, py) in enumerate(_peer_chips(x, y)):
                landed = _rows_part(a, _piece(a, couts[q], 2 * px + py, half_of(c)), part)
                out.append(_remote(landed, landed, csems[4].at[3 * q + i], csems[5].at[3 * q + i], (x, y, 1 - c)))
        return out

    def start(cins, couts, csems):
        for cp in sends(cins, couts, csems) + own(cins, couts, csems):
            cp.start()

    def finish(cins, couts, csems):
        fw = forwards(couts, csems, lambda c: c)
        for cp, f in zip(sends(cins, couts, csems), fw):
            cp.wait_recv()
            f.start()
        for f in forwards(couts, csems, lambda c: 1 - c):
            f.wait_recv()
        for cp in sends(cins, couts, csems) + fw:
            cp.wait_send()
        for cp in own(cins, couts, csems):
            cp.wait()

    ins = [locs[a] for a in arrs] + ([into[a] for a in arrs] if into else [])
    return _Comm(ins, [SDS(full_shapes[a], BF16) for a in arrs],
                 _dma_sems(2 * n, 2 * n, 3 * n, 3 * n, 3 * n, 3 * n), start, finish,
                 aliases={n + q: q for q in range(n)} if into else None)


def _halves_comm(arrs, gbs):
    n = len(arrs)

    def copies(cins, couts, csems):
        x, y, c = _coords()
        return [_remote(_piece(a, cins[q], k, 1 - c), couts[q].at[k], csems[0].at[4 * q + k], csems[1].at[4 * q + k],
                        (x, y, 1 - c)) for q, a in enumerate(arrs) for k in range(4)]

    return _symmetric([gbs[a] for a in arrs], [SDS((4,) + _piece_shape(a, gbs[a].shape), BF16) for a in arrs],
                      _dma_sems(4 * n, 4 * n), copies)


def _chips_comm(arrs, ps, part=None, into=None):
    n = len(arrs)

    def copies(cins, couts, csems):
        x, y, c = _coords()
        return [_remote(_rows_part(a, cins[q].at[2 * px + py], part), _rows_part(a, couts[q].at[i], part),
                        csems[0].at[3 * q + i], csems[1].at[3 * q + i], (px, py, c))
                for q, a in enumerate(arrs) for i, (px, py) in enumerate(_peer_chips(x, y))]

    ins = [ps[a] for a in arrs] + ([into[a] for a in arrs] if into else [])
    return _symmetric(ins, [SDS((3,) + ps[a].shape[1:], BF16) for a in arrs], _dma_sems(3 * n, 3 * n), copies,
                      aliases={n + q: q for q in range(n)} if into else None)


def _result_comm(arrs, gs):
    n = len(arrs)

    def copies(cins, couts, csems):
        x, y, c = _coords()
        return [_remote(_half(a, cins[q], c), _half(a, couts[q], c), csems[0].at[q], csems[1].at[q], (x, y, 1 - c))
                for q, a in enumerate(arrs)]

    return _symmetric([gs[a] for a in arrs], [SDS(gs[a].shape, F32) for a in arrs], _dma_sems(n, n), copies,
                      aliases={q: q for q in range(n)})


def _add_halves(arrs, gbs, lands, c_arr, name):
    n = len(arrs)

    def body(c_ref, *refs):
        del c_ref
        for q in range(n):
            refs[2 * n + q][...] = (refs[q][...].astype(F32) + refs[n + q][...].astype(F32)).astype(BF16)

    g_specs, l_specs, o_specs, blocks = [], [], [], 0
    for a in arrs:
        bs, imap = _piece_block(a, gbs[a].shape)
        ps = _piece_shape(a, gbs[a].shape)
        g_specs.append(pl.BlockSpec(bs, lambda k, c_ref, imap=imap: imap(k, c_ref[0])))
        nd = len(ps)
        l_specs.append(pl.BlockSpec((None,) + ps, lambda k, c_ref, nd=nd: (k,) + (0,) * nd))
        o_specs.append(pl.BlockSpec((None,) + ps, lambda k, c_ref, nd=nd: (k,) + (0,) * nd))
        blocks += 3 * _nbytes(ps, BF16)
    return list(pl.pallas_call(
        body, name=name,
        grid_spec=pltpu.PrefetchScalarGridSpec(
            num_scalar_prefetch=1, grid=(4,), in_specs=g_specs + l_specs, out_specs=o_specs),
        out_shape=[SDS((4,) + _piece_shape(a, gbs[a].shape), BF16) for a in arrs],
        compiler_params=_params(("parallel",), blocks, blocks),
    )(c_arr, *[gbs[a] for a in arrs], *lands))


def _sum_chips(a, p, land, shard_shape, jc_arr, name):
    ps = land.shape[1:]
    ax = _rows_axis(a)
    rows = ps[ax]
    nsub = 2 if rows % (2 * SUBLANES_BF16) == 0 else 1
    bs = tuple(r // nsub if q == ax else r for q, r in enumerate(ps))
    nd = len(ps)

    def at_rows(v):
        return tuple(v if q == ax else 0 for q in range(nd))

    def body(jc_ref, p_ref, l_ref, o_ref):
        del jc_ref
        acc = p_ref[...].astype(F32) + l_ref[0].astype(F32)
        acc = acc + l_ref[1].astype(F32)
        o_ref[...] = acc + l_ref[2].astype(F32)

    blocks = 4 * _nbytes(bs, BF16) + _nbytes(bs, F32)
    return pl.pallas_call(
        body, name=name,
        grid_spec=pltpu.PrefetchScalarGridSpec(
            num_scalar_prefetch=1, grid=(nsub,),
            in_specs=[pl.BlockSpec((None,) + bs, lambda s, jc: (jc[0],) + at_rows(s)),
                      pl.BlockSpec((3,) + bs, lambda s, jc: (0,) + at_rows(s))],
            out_specs=pl.BlockSpec(bs, lambda s, jc: at_rows(jc[1] * nsub + s))),
        out_shape=SDS(shard_shape, F32),
        compiler_params=_params(("parallel",), blocks, 2 * _nbytes(bs, F32)),
    )(jc_arr, p, land)


def _small_comm(v):
    rows = v.shape[0]

    def copies(cins, couts, csems):
        x, y, c = _coords()
        me = 4 * x + 2 * y + c
        out = [pltpu.make_async_copy(cins[0], couts[0].at[me], csems[0].at[0])]
        for dlt in range(1, 8):
            px = 1 - x if (dlt >> 2) & 1 else x
            py = 1 - y if (dlt >> 1) & 1 else y
            pc = 1 - c if dlt & 1 else c
            out.append(_remote(cins[0], couts[0].at[me], csems[1].at[dlt - 1], csems[2].at[dlt - 1], (px, py, pc)))
        return out

    return _symmetric([v], [SDS((8, rows, LANES), F32)], _dma_sems(1, 7, 7), copies)


def _sum8(slots, name):
    def body(s_ref, o_ref):
        acc = s_ref[0]
        for i in range(1, 8):
            acc = acc + s_ref[i]
        o_ref[...] = acc

    return pl.pallas_call(
        body, name=name,
        in_specs=[pl.BlockSpec(memory_space=pltpu.VMEM)], out_specs=pl.BlockSpec(memory_space=pltpu.VMEM),
        out_shape=SDS(slots.shape[1:], F32),
    )(slots)


def _adamw(w, g, m, v, name, g_plane=None):
    rows, cols = w.shape
    tr = _tile(rows, max(SUBLANES_F32, (256 * 1024 // cols) // SUBLANES_F32 * SUBLANES_F32), SUBLANES_F32)

    def body(w_ref, g_ref, m_ref, v_ref, go_ref, d_ref, mo_ref, vo_ref):
        gr = g_ref[...]
        mn = ADAM_B1 * m_ref[...] + (1.0 - ADAM_B1) * gr
        vn = ADAM_B2 * v_ref[...] + (1.0 - ADAM_B2) * (gr * gr)
        m_hat = mn / (1.0 - ADAM_B1 ** ADAM_STEP)
        v_hat = vn / (1.0 - ADAM_B2 ** ADAM_STEP)
        d_ref[...] = -ADAM_LR * (m_hat / (jnp.sqrt(v_hat) + ADAM_EPS) + ADAM_WD * w_ref[...])
        go_ref[...] = gr
        mo_ref[...] = mn
        vo_ref[...] = vn

    spec = pl.BlockSpec((tr, cols), lambda i: (i, 0))
    g_spec = spec if g_plane is None else pl.BlockSpec((None, tr, cols), lambda i: (g_plane, i, 0))
    return pl.pallas_call(
        body, name=name, grid=(rows // tr,),
        in_specs=[spec, g_spec, spec, spec], out_specs=[spec, spec, spec, spec],
        out_shape=[SDS((rows, cols), F32)] * 4,
        compiler_params=_params(("parallel",), 8 * _nbytes((tr, cols), F32), 4 * _nbytes((tr, cols), F32)),
    )(w, g, m, v)


def _pack(parts):
    rows = []
    for p in parts:
        r = p.reshape(-1, LANES)
        pad = (-r.shape[0]) % SUBLANES_F32
        if pad:
            r = jnp.pad(r, ((0, pad), (0, 0)))
        rows.append(r)
    return jnp.concatenate(rows, axis=0)


def _unpack(packed, shapes):
    out, at = [], 0
    for s in shapes:
        n = 1
        for q in s:
            n *= q
        r = n // LANES
        out.append(packed[at:at + r].reshape(s))
        at += r + (-r) % SUBLANES_F32
    return out


def kernel(x, norm_mix, w_in, pool_w, pool_scale, w_pool_proj, conv_w, w_conv_out, w_o, norm_ffn, w_up, ffn_conv_w, ffn_conv_b, w_down, norm_final, loss_target, m_norm_mix, m_w_in, m_pool_w, m_pool_scale, m_w_pool_proj, m_conv_w, m_w_conv_out, m_w_o, m_norm_ffn, m_w_up, m_ffn_conv_w, m_ffn_conv_b, m_w_down, m_norm_final, v_norm_mix, v_w_in, v_pool_w, v_pool_scale, v_w_pool_proj, v_conv_w, v_w_conv_out, v_w_o, v_norm_ffn, v_w_up, v_ffn_conv_w, v_ffn_conv_b, v_w_down, v_norm_final):
    nseq, seq, d = x.shape
    t = nseq * seq
    f = w_down.shape[1] * 4
    c = d // N_GROUPS
    xy = lax.axis_index("x") * 2 + lax.axis_index("y")
    c_arr = lax.axis_index("c").astype(jnp.int32).reshape(1)
    jc_arr = jnp.stack([xy, lax.axis_index("c")]).astype(jnp.int32)
    nsh = 4
    zero = jnp.zeros((), jnp.int32)

    locs = [w_in[0].astype(BF16),
            jnp.stack([w_pool_proj[0], w_conv_out[0], w_o[0]]).astype(BF16),
            w_up[0].astype(BF16), w_down[0].astype(BF16), pool_w[0].astype(BF16)]
    full_shapes = [(nsh, d, N_SPLITS * d // nsh), (3, d, d), (nsh, d, 2 * f // nsh), (f, d), (N_GROUPS, c, c)]

    cw_pad = lax.dynamic_update_slice(jnp.zeros((3, d), F32), conv_w[0], (zero, xy * (d // 4)))
    fw_pad = lax.dynamic_update_slice(jnp.zeros((3, 2 * f), F32), ffn_conv_w[0], (zero, xy * (f // 2)))
    small_w = _pack([cw_pad, fw_pad]) * 0.5

    x2d = x.reshape(t, d)
    tgt = loss_target.reshape(t, d)
    w_in_f, slots_w = _run_comm(_merge([_gather_comm([0], locs, full_shapes), _small_comm(small_w)]), "gather_w_in")
    conv_w_f, ffn_cw_f = _unpack(_sum8(slots_w, "sum8_weights"), [(3, d), (3, 2 * f)])
    ffn_cw_p = ffn_cw_f.reshape(3, 2, f).transpose(1, 0, 2)
    ffn_cb_p = ffn_conv_b.reshape(2, 1, f)
    (z, h1), (w3_f, pool_w_f, w_up_f) = _fwd_in(
        x2d, norm_mix, w_in_f,
        _merge([_gather_comm([1, 4], locs, full_shapes), _gather_comm([2], locs, full_shapes, part=(0, 2))]))
    lhs3 = _mixer_mid_fwd(z, pool_w_f, pool_scale, conv_w_f, nseq)
    (lhs3, ypc, x1, h2), (w_up_f,) = _mixer_out(
        lhs3, z, x2d, w3_f, norm_ffn, _gather_comm([2], locs, full_shapes, part=(1, 2), into={2: w_up_f}))
    (u0,), (w_down_f,) = _ffn_up(h2, w_up_f, f, _gather_comm([3], locs, full_shapes))
    act = _ffn_mid_fwd(u0, ffn_cw_p, ffn_cb_p, nseq)
    dx2, dx2b, loss11, g_norm_final = _ffn_down_loss(act, w_down_f, x1, tgt, norm_final.reshape(1, d))

    gbs, lands, ps, lands2, rs = {}, {}, {}, {}, {}
    tn_up = _tile(2 * f // nsh, 1408, LANES)
    npp = f // tn_up

    def add(arrs, name):
        for a, p in zip(arrs, _add_halves(arrs, gbs, [lands[a] for a in arrs], c_arr, name)):
            ps[a] = p

    def summed(a):
        rs[a] = _sum_chips(a, ps[a], lands2[a], _shard_shape(a, full_shapes[a]), jc_arr, "sum_chips_%d" % a)

    (gbs[3],), _ = _wgrad(act, dx2b, "wgrad_down", tr=tn_up, tn=d)
    (da,), (lands[3],) = _ffn_bwd_da(dx2b, w_down_f, _halves_comm([3], gbs))
    add([3], "add_halves_down")
    (du0, g_ffn_cw_p, g_ffn_cb_p), (lands2[3],) = _ffn_mid_bwd(da, u0, ffn_cw_p, ffn_cb_p, nseq, _chips_comm([3], ps))
    summed(3)
    (gbs[2],), (rs[3],) = _wgrad(h2, du0, "wgrad_up", tr=d, tn=tn_up, b_plane_of=lambda n: (n // npp, n % npp),
                                 out_shards=nsh, comm=_result_comm([3], rs))
    (dx1, rhs3, g_norm_ffn), (lands[2],) = _ffn_bwd_dx1(du0, w_up_f, x1, dx2, norm_ffn, 3, _halves_comm([2], gbs))
    add([2], "add_halves_up")
    (rhs3, dz, dpq), (lands2[2],) = _mixer_bwd(rhs3, z, ypc, w3_f, _chips_comm([2], ps, part=(0, 2)))
    (gbs[1],), (lands2[2],) = _wgrad3(lhs3, rhs3, _chips_comm([2], ps, part=(1, 2), into=lands2))
    summed(2)
    (dz, g_conv_w), (lands[1], rs[2]) = _conv_bwd(dz, dpq, z, conv_w_f, nseq,
                                                  _merge([_halves_comm([1], gbs), _result_comm([2], rs)]))
    add([1], "add_halves_sq3")
    (dz, g_pool_w, g_pool_scale), _ = _pool_bwd_call(dz, dpq, z, pool_w_f, pool_scale, nseq)
    gbs[4] = g_pool_w.astype(BF16)
    (gbs[0],), (lands2[1],) = _wgrad_in(h1, dz, nsh, _chips_comm([1], ps))
    summed(1)
    lands[0], lands[4] = _run_comm(_halves_comm([0, 4], gbs), "exchange_halves_in")
    add([0, 4], "add_halves_in")
    g_ffn_cw = g_ffn_cw_p.transpose(1, 0, 2).reshape(3, 2 * f)
    small_a = _pack([g_pool_scale, g_norm_ffn, g_ffn_cb_p.reshape(1, 2 * f), g_norm_final.reshape(d), g_conv_w,
                     g_ffn_cw, jnp.pad(loss11, ((0, SUBLANES_F32 - 1), (0, LANES - 1)))])
    (grad_x, g_norm_mix), (lands2[0], lands2[4], rs[1], slots_a) = _mixer_bwd_dx(
        dz, w_in_f, x2d, dx1, norm_mix,
        _merge([_chips_comm([0, 4], ps), _result_comm([1], rs), _small_comm(small_a)]))
    summed(0)
    summed(4)
    rs[0], rs[4], slots_b = _run_comm(_merge([_result_comm([0, 4], rs), _small_comm(_pack([g_norm_mix]))]),
                                      "exchange_result_in")
    shapes_a = [(1, d), (1, d), (1, 2 * f), (d,), (3, d), (3, 2 * f), (SUBLANES_F32, LANES)]
    gs_pool_scale, gs_norm_ffn, gs_ffn_cb, gs_norm_final, gs_conv_w, gs_ffn_cw, loss_blk = _unpack(
        _sum8(slots_a, "sum8_grads"), shapes_a)
    (gs_norm_mix,) = _unpack(_sum8(slots_b, "sum8_norm_mix"), [(1, d)])
    gs_conv_w = lax.dynamic_slice(gs_conv_w, (zero, xy * (d // 4)), (3, d // 4))
    gs_ffn_cw = lax.dynamic_slice(gs_ffn_cw, (zero, xy * (f // 2)), (3, f // 2))

    def upd(w, g, m, v, name, g_plane=None):
        shape = w.shape
        rows = 1
        for q in shape[:-1]:
            rows *= q
        g2 = g if g_plane is not None else g.reshape(rows, shape[-1])
        outs = _adamw(w.reshape(rows, shape[-1]), g2, m.reshape(rows, shape[-1]), v.reshape(rows, shape[-1]),
                      name, g_plane)
        return [o.reshape(shape) for o in outs]

    res = {
        "w_in": upd(w_in, rs[0], m_w_in, v_w_in, "adamw_w_in"),
        "pool_w": upd(pool_w, rs[4], m_pool_w, v_pool_w, "adamw_pool_w"),
        "w_pool_proj": upd(w_pool_proj, rs[1], m_w_pool_proj, v_w_pool_proj, "adamw_w_pool_proj", 0),
        "w_conv_out": upd(w_conv_out, rs[1], m_w_conv_out, v_w_conv_out, "adamw_w_conv_out", 1),
        "w_o": upd(w_o, rs[1], m_w_o, v_w_o, "adamw_w_o", 2),
        "w_up": upd(w_up, rs[2], m_w_up, v_w_up, "adamw_w_up"),
        "w_down": upd(w_down, rs[3], m_w_down, v_w_down, "adamw_w_down"),
    }

    small_names = ["norm_mix", "pool_scale", "norm_ffn", "ffn_conv_b", "norm_final", "conv_w", "ffn_conv_w"]
    small_ws = [norm_mix, pool_scale, norm_ffn, ffn_conv_b, norm_final, conv_w, ffn_conv_w]
    small_ms = [m_norm_mix, m_pool_scale, m_norm_ffn, m_ffn_conv_b, m_norm_final, m_conv_w, m_ffn_conv_w]
    small_vs = [v_norm_mix, v_pool_scale, v_norm_ffn, v_ffn_conv_b, v_norm_final, v_conv_w, v_ffn_conv_w]
    small_gs = [gs_norm_mix, gs_pool_scale, gs_norm_ffn, gs_ffn_cb, gs_norm_final, gs_conv_w, gs_ffn_cw]
    _, sd, sm, sv = _adamw(_pack(small_ws), _pack(small_gs), _pack(small_ms), _pack(small_vs), "adamw_small")
    shapes = [w.shape for w in small_ws]
    sd, sm, sv = _unpack(sd, shapes), _unpack(sm, shapes), _unpack(sv, shapes)
    for i, nm in enumerate(small_names):
        res[nm] = [small_gs[i].reshape(shapes[i]), sd[i], sm[i], sv[i]]

    order = ["norm_mix", "w_in", "pool_w", "pool_scale", "w_pool_proj", "conv_w", "w_conv_out", "w_o", "norm_ffn",
             "w_up", "ffn_conv_w", "ffn_conv_b", "w_down", "norm_final"]
    return (loss_blk[0, 0], grad_x.reshape(x.shape), *[res[n][0] for n in order], *[res[n][1] for n in order],
            *[res[n][2] for n in order], *[res[n][3] for n in order])
```

```python
import math

import jax
import jax.numpy as jnp
from jax import lax
from jax.experimental import pallas as pl
from jax.experimental.pallas import tpu as pltpu

F32 = jnp.float32
BF16 = jnp.bfloat16
SDS = jax.ShapeDtypeStruct
MESH = pl.DeviceIdType.MESH

RMS_EPS = 1e-6
POOL_WINDOWS = (2, 4, 8, 16)
N_GROUPS = len(POOL_WINDOWS)
N_SPLITS = 6

ADAM_LR = 0.001
ADAM_B1 = 0.9
ADAM_B2 = 0.999
ADAM_EPS = 1e-08
ADAM_WD = 0.01
ADAM_STEP = 10

LANES = 128
SUBLANES_F32 = 8
SUBLANES_BF16 = 16
VMEM_BYTES = 64 * 1024 * 1024
VMEM_CAP = VMEM_BYTES - 8 * 1024 * 1024
VMEM_FLOOR = 16 * 1024 * 1024

ANY = pl.BlockSpec(memory_space=pl.ANY)


def _tile(dim, pref, align):
    if dim <= pref:
        return dim
    t = (pref // align) * align
    while t >= align:
        if dim % t == 0:
            return t
        t -= align
    return dim


def _nbytes(shape, dtype):
    n = 1
    for s in shape:
        n *= s
    return n * jnp.dtype(dtype).itemsize


def _params(sem, block_bytes, temp_bytes=0):
    need = 2 * block_bytes + temp_bytes + 4 * 1024 * 1024
    return pltpu.CompilerParams(dimension_semantics=sem, vmem_limit_bytes=int(min(max(need, VMEM_FLOOR), VMEM_CAP)))


class _Comm:
    def __init__(self, ins, out_shapes, sems, start, finish, aliases=None):
        self.ins = list(ins)
        self.out_shapes = list(out_shapes)
        self.sems = list(sems)
        self.start = start
        self.finish = finish
        self.aliases = dict(aliases or {})


def _pcall(body, *, name, grid, in_specs, out_specs, out_shape, sem, blocks, temps=0, scratch_shapes=(),
           input_output_aliases=None, comm=None):
    in_specs = list(in_specs)
    out_specs = list(out_specs)
    out_shape = list(out_shape)
    scratch_shapes = list(scratch_shapes)
    aliases = dict(input_output_aliases or {})
    n_in, n_out, n_scr = len(in_specs), len(out_shape), len(scratch_shapes)
    if comm is None:
        call = pl.pallas_call(
            body, name=name, grid=grid, in_specs=in_specs, out_specs=out_specs, out_shape=out_shape,
            scratch_shapes=scratch_shapes, input_output_aliases=aliases,
            compiler_params=_params(sem, blocks, temps))
        return lambda *args: (list(call(*args)), [])

    nci, nco = len(comm.ins), len(comm.out_shapes)

    def hosted(*refs):
        ins = refs[:n_in]
        cins = refs[n_in:n_in + nci]
        outs = refs[n_in + nci:n_in + nci + n_out]
        couts = refs[n_in + nci + n_out:n_in + nci + n_out + nco]
        scr = refs[n_in + nci + n_out + nco:n_in + nci + n_out + nco + n_scr]
        csems = refs[n_in + nci + n_out + nco + n_scr:]
        first = None
        last = None
        for q, g in enumerate(grid):
            pid = pl.program_id(q)
            first = (pid == 0) if first is None else first & (pid == 0)
            last = (pid == g - 1) if last is None else last & (pid == g - 1)

        @pl.when(first)
        def _():
            comm.start(cins, couts, csems)

        body(*ins, *outs, *scr)

        @pl.when(last)
        def _():
            comm.finish(cins, couts, csems)

    for i, o in comm.aliases.items():
        aliases[n_in + i] = n_out + o
    call = pl.pallas_call(
        hosted, name=name, grid=grid, in_specs=in_specs + [ANY] * nci, out_specs=out_specs + [ANY] * nco,
        out_shape=out_shape + comm.out_shapes, scratch_shapes=scratch_shapes + comm.sems,
        input_output_aliases=aliases,
        compiler_params=_params(("arbitrary",) * len(grid), blocks, temps))

    def run(*args):
        res = call(*args, *comm.ins)
        return list(res[:n_out]), list(res[n_out:])

    return run


def _run_comm(comm, name):
    def body(*refs):
        nci, nco = len(comm.ins), len(comm.out_shapes)
        cins, couts, csems = refs[:nci], refs[nci:nci + nco], refs[nci + nco:]
        comm.start(cins, couts, csems)
        comm.finish(cins, couts, csems)

    return list(pl.pallas_call(
        body, name=name, in_specs=[ANY] * len(comm.ins), out_specs=[ANY] * len(comm.out_shapes),
        out_shape=comm.out_shapes, scratch_shapes=comm.sems, input_output_aliases=comm.aliases,
    )(*comm.ins))


def _dot(a, b):
    return jnp.dot(a, b, preferred_element_type=F32)


def _dot_tb(a, b):
    return lax.dot_general(a, b, (((1,), (1,)), ((), ())), preferred_element_type=F32)


def _dot_ta(a, b):
    return lax.dot_general(a, b, (((0,), (0,)), ((), ())), preferred_element_type=F32)


def _rms_fwd(x):
    inv = lax.rsqrt(jnp.mean(x * x, axis=-1, keepdims=True) + RMS_EPS)
    return x * inv, inv


def _rms_bwd(dy, xhat, inv, g):
    gd = dy * g
    return inv * (gd - xhat * jnp.mean(gd * xhat, axis=-1, keepdims=True))


def _sigmoid(x):
    return 1.0 / (1.0 + jnp.exp(-x))


def _shift_down(x, k, row):
    return jnp.where(row >= k, pltpu.roll(x, k, 0), 0.0)


def _shift_up(x, k, row):
    s = x.shape[0]
    return jnp.where(row < s - k, pltpu.roll(x, s - k, 0), 0.0)


def _pool_fwd(u, win, row):
    s = u
    k = 1
    while k < win:
        s = s + _shift_down(s, k, row)
        k *= 2
    cnt = jnp.minimum(row + 1, win).astype(F32)
    return s / cnt - u


def _pool_bwd(dp, win, row):
    cnt = jnp.minimum(row + 1, win).astype(F32)
    s = dp / cnt
    k = 1
    while k < win:
        s = s + _shift_up(s, k, row)
        k *= 2
    return s - dp


def _acc_over(k, nk, part, acc, o_ref):
    @pl.when(k == 0)
    def _():
        acc[...] = part

    @pl.when(k > 0)
    def _():
        acc[...] += part

    @pl.when(k == nk - 1)
    def _():
        o_ref[...] = acc[...].astype(o_ref.dtype)


def _fwd_in(x, g, w, comm=None):
    t, d = x.shape
    nsh, _, ws = w.shape
    n = nsh * ws
    tm = _tile(t, 1024, SUBLANES_BF16)
    tn = _tile(ws, 1536, LANES)
    nps = ws // tn

    def body(x_ref, g_ref, w_ref, z_ref, h_ref, hs):
        @pl.when(pl.program_id(1) == 0)
        def _():
            xh, _ = _rms_fwd(x_ref[...])
            h = (xh * g_ref[...]).astype(BF16)
            hs[...] = h
            h_ref[...] = h

        z_ref[...] = _dot(hs[...], w_ref[...]).astype(BF16)

    blocks = _nbytes((tm, d), F32) + _nbytes((d, tn), BF16) + _nbytes((tm, tn), BF16) + _nbytes((tm, d), BF16)
    return _pcall(
        body, name="fwd_in", grid=(t // tm, n // tn),
        in_specs=[pl.BlockSpec((tm, d), lambda i, j: (i, 0)), pl.BlockSpec((1, d), lambda i, j: (0, 0)),
                  pl.BlockSpec((None, d, tn), lambda i, j: (j // nps, 0, j % nps))],
        out_specs=[pl.BlockSpec((tm, tn), lambda i, j: (i, j)), pl.BlockSpec((tm, d), lambda i, j: (i, 0))],
        out_shape=[SDS((t, n), BF16), SDS((t, d), BF16)],
        scratch_shapes=[pltpu.VMEM((tm, d), BF16)],
        sem=("parallel", "arbitrary"), blocks=blocks, temps=3 * _nbytes((tm, d), F32), comm=comm,
    )(x, g, w)


def _mixer_mid_fwd(z, pool_w, pool_scale, conv_w, nseq):
    t = z.shape[0]
    d = pool_scale.shape[1]
    s = t // nseq
    c = d // N_GROUPS

    def body(zp, zb, zc, zv, pw, ps, cw, o):
        j = pl.program_id(1)
        row = lax.broadcasted_iota(jnp.int32, (s, c), 0)
        for gi, win in enumerate(POOL_WINDOWS):
            @pl.when(j == gi)
            def _(win=win):
                pooled = _pool_fwd(zp[...].astype(F32), win, row)
                o[0] = (_dot(pooled.astype(BF16), pw[...]) * ps[...]).astype(BF16)

        cv = zc[...].astype(F32) * zv[...].astype(F32)
        cc = (cw[pl.ds(2, 1), :] * cv + cw[pl.ds(1, 1), :] * _shift_down(cv, 1, row)
              + cw[pl.ds(0, 1), :] * _shift_down(cv, 2, row))
        o[1] = (zb[...].astype(F32) * cc).astype(BF16)

    blocks = 4 * _nbytes((s, c), BF16) + _nbytes((c, c), BF16) + _nbytes((2, s, c), BF16)
    outs, _ = _pcall(
        body, name="mixer_mid_fwd", grid=(nseq, N_GROUPS),
        in_specs=[pl.BlockSpec((s, c), lambda b, j: (b, j)),
                  pl.BlockSpec((s, c), lambda b, j: (b, N_GROUPS + j)),
                  pl.BlockSpec((s, c), lambda b, j: (b, 2 * N_GROUPS + j)),
                  pl.BlockSpec((s, c), lambda b, j: (b, 3 * N_GROUPS + j)),
                  pl.BlockSpec((None, c, c), lambda b, j: (j, 0, 0)),
                  pl.BlockSpec((1, c), lambda b, j: (0, j)),
                  pl.BlockSpec((3, c), lambda b, j: (0, j))],
        out_specs=[pl.BlockSpec((2, s, c), lambda b, j: (0, b, j))],
        out_shape=[SDS((3, t, d), BF16)],
        sem=("parallel", "parallel"), blocks=blocks, temps=8 * _nbytes((s, c), F32),
    )(z, z, z, z, pool_w, pool_scale, conv_w)
    return outs[0]


def _mixer_out(lhs3, z, x, w3, g_ffn, comm=None):
    t, d = x.shape
    tm = _tile(t, 256, SUBLANES_BF16)

    def body(pq, zgp, zgc, x_ref, w_ref, g_ref, mrg, ypc, x1o, h2o):
        yp = _dot(pq[0], w_ref[0])
        yc = _dot(pq[1], w_ref[1])
        m = _sigmoid(zgp[...].astype(F32)) * yp + _sigmoid(zgc[...].astype(F32)) * yc
        mb = m.astype(BF16)
        x1 = x_ref[...] + _dot(mb, w_ref[2])
        ypc[0] = yp.astype(BF16)
        ypc[1] = yc.astype(BF16)
        mrg[...] = mb
        x1o[...] = x1
        xh, _ = _rms_fwd(x1)
        h2o[...] = (xh * g_ref[...]).astype(BF16)

    blocks = (_nbytes((2, tm, d), BF16) * 2 + _nbytes((tm, d), BF16) * 4 + _nbytes((tm, d), F32) * 2
              + _nbytes((3, d, d), BF16))
    return _pcall(
        body, name="mixer_out", grid=(t // tm,),
        in_specs=[pl.BlockSpec((2, tm, d), lambda i: (0, i, 0)),
                  pl.BlockSpec((tm, d), lambda i: (i, 4)),
                  pl.BlockSpec((tm, d), lambda i: (i, 5)),
                  pl.BlockSpec((tm, d), lambda i: (i, 0)),
                  pl.BlockSpec((3, d, d), lambda i: (0, 0, 0)),
                  pl.BlockSpec((1, d), lambda i: (0, 0))],
        out_specs=[pl.BlockSpec((None, tm, d), lambda i: (2, i, 0)),
                   pl.BlockSpec((2, tm, d), lambda i: (0, i, 0)),
                   pl.BlockSpec((tm, d), lambda i: (i, 0)),
                   pl.BlockSpec((tm, d), lambda i: (i, 0))],
        out_shape=[SDS(lhs3.shape, BF16), SDS((2, t, d), BF16), SDS((t, d), F32), SDS((t, d), BF16)],
        input_output_aliases={0: 0},
        sem=("parallel",), blocks=blocks, temps=8 * _nbytes((tm, d), F32), comm=comm,
    )(lhs3, z, z, x, w3, g_ffn)


def _ffn_up(h2, w_up, f, comm=None):
    t, d = h2.shape
    _, _, ws = w_up.shape
    tm = _tile(t, 1024, SUBLANES_BF16)
    tn = _tile(ws, 1408, LANES)
    nps = ws // tn
    npp = f // tn

    def body(h_ref, w_ref, o_ref):
        o_ref[...] = _dot(h_ref[...], w_ref[...]).astype(BF16)

    blocks = _nbytes((tm, d), BF16) + _nbytes((d, tn), BF16) + _nbytes((tm, tn), BF16)
    return _pcall(
        body, name="ffn_up", grid=(t // tm, 2 * npp),
        in_specs=[pl.BlockSpec((tm, d), lambda i, j: (i, 0)),
                  pl.BlockSpec((None, d, tn), lambda i, j: (j // nps, 0, j % nps))],
        out_specs=[pl.BlockSpec((None, tm, tn), lambda i, j: (j // npp, i, j % npp))],
        out_shape=[SDS((2, t, f), BF16)],
        sem=("parallel", "parallel"), blocks=blocks, temps=_nbytes((tm, tn), F32), comm=comm,
    )(h2, w_up)


def _conv3_rows(u, u1, u2, w_ref, p):
    return w_ref[p, pl.ds(2, 1), :] * u + w_ref[p, pl.ds(1, 1), :] * u1 + w_ref[p, pl.ds(0, 1), :] * u2


CHUNK = 64
HALO = SUBLANES_F32


def _up1_up2(u, nxt):
    rows = u.shape[0]
    ext = jnp.concatenate([u, nxt], axis=0)
    n = rows + HALO
    return pltpu.roll(ext, n - 1, 0)[:rows], pltpu.roll(ext, n - 2, 0)[:rows]


def _fold8(x):
    return jnp.sum(x.reshape(x.shape[0] // SUBLANES_F32, SUBLANES_F32, x.shape[1]), axis=0)


def _ffn_mid_fwd(u0, cw, cb, nseq):
    _, t, f = u0.shape
    s = t // nseq
    c = _tile(f, 256, LANES)

    def body(u_ref, w_ref, b_ref, a_ref, uo_ref):
        row = lax.broadcasted_iota(jnp.int32, (s, c), 0)
        act = []
        for p in range(2):
            u = u_ref[p].astype(F32)
            act.append(_conv3_rows(u, _shift_down(u, 1, row), _shift_down(u, 2, row), w_ref, p) + b_ref[p])
            uo_ref[p] = act[p].astype(BF16)
        ug, uv = act
        a_ref[...] = (ug * _sigmoid(ug) * uv).astype(BF16)

    blocks = 2 * _nbytes((2, s, c), BF16) + _nbytes((s, c), BF16)
    outs, _ = _pcall(
        body, name="ffn_mid_fwd", grid=(f // c, nseq),
        in_specs=[pl.BlockSpec((2, s, c), lambda j, b: (0, b, j)),
                  pl.BlockSpec((2, 3, c), lambda j, b: (0, 0, j)),
                  pl.BlockSpec((2, 1, c), lambda j, b: (0, 0, j))],
        out_specs=[pl.BlockSpec((s, c), lambda j, b: (b, j)),
                   pl.BlockSpec((2, s, c), lambda j, b: (0, b, j))],
        out_shape=[SDS((t, f), BF16), SDS((2, t, f), BF16)],
        sem=("parallel", "parallel"), blocks=blocks, temps=8 * _nbytes((s, c), F32),
    )(u0, cw, cb)
    return outs


def _ffn_down_loss(a, w_down, x1, tgt, g_fin):
    t, f = a.shape
    d = x1.shape[1]
    tm = _tile(t, 256, SUBLANES_BF16)
    nsteps = t // tm

    def body(a_ref, w_ref, x1_ref, t_ref, g_ref, dx_ref, dxb_ref, loss_ref, gg_ref, lacc):
        i = pl.program_id(0)

        @pl.when(i == 0)
        def _():
            lacc[...] = jnp.zeros_like(lacc)
            gg_ref[...] = jnp.zeros_like(gg_ref)

        x2 = x1_ref[...] + _dot(a_ref[...], w_ref[...])
        xh, inv = _rms_fwd(x2)
        g = g_ref[...]
        e = xh * g - t_ref[...]
        lacc[...] += jnp.sum(e * e, axis=0, keepdims=True)
        dy = e * (1.0 / d)
        gg_ref[...] += jnp.sum(dy * xh, axis=0, keepdims=True)
        dx2 = _rms_bwd(dy, xh, inv, g)
        dx_ref[...] = dx2
        dxb_ref[...] = dx2.astype(BF16)

        @pl.when(i == nsteps - 1)
        def _():
            loss_ref[...] = jnp.sum(lacc[...], axis=1, keepdims=True) * (0.5 / d)

    blocks = (_nbytes((tm, f), BF16) + _nbytes((f, d), BF16) + 3 * _nbytes((tm, d), F32) + _nbytes((tm, d), BF16))
    outs, _ = _pcall(
        body, name="ffn_down_loss", grid=(nsteps,),
        in_specs=[pl.BlockSpec((tm, f), lambda i: (i, 0)), pl.BlockSpec((f, d), lambda i: (0, 0)),
                  pl.BlockSpec((tm, d), lambda i: (i, 0)), pl.BlockSpec((tm, d), lambda i: (i, 0)),
                  pl.BlockSpec((1, d), lambda i: (0, 0))],
        out_specs=[pl.BlockSpec((tm, d), lambda i: (i, 0)), pl.BlockSpec((tm, d), lambda i: (i, 0)),
                   pl.BlockSpec((1, 1), lambda i: (0, 0)), pl.BlockSpec((1, d), lambda i: (0, 0))],
        out_shape=[SDS((t, d), F32), SDS((t, d), BF16), SDS((1, 1), F32), SDS((1, d), F32)],
        scratch_shapes=[pltpu.VMEM((1, d), F32)],
        sem=("arbitrary",), blocks=blocks, temps=8 * _nbytes((tm, d), F32),
    )(a, w_down, x1, tgt, g_fin)
    return outs


def _ffn_bwd_da(dxb, w_down, comm=None):
    t, d = dxb.shape
    f = w_down.shape[0]
    tm = _tile(t, 1024, SUBLANES_BF16)
    tn = _tile(f, 1408, LANES)

    def body(x_ref, w_ref, o_ref):
        o_ref[...] = _dot_tb(x_ref[...], w_ref[...]).astype(BF16)

    blocks = _nbytes((tm, d), BF16) + _nbytes((tn, d), BF16) + _nbytes((tm, tn), BF16)
    return _pcall(
        body, name="ffn_bwd_da", grid=(t // tm, f // tn),
        in_specs=[pl.BlockSpec((tm, d), lambda i, j: (i, 0)), pl.BlockSpec((tn, d), lambda i, j: (j, 0))],
        out_specs=[pl.BlockSpec((tm, tn), lambda i, j: (i, j))],
        out_shape=[SDS((t, f), BF16)],
        sem=("parallel", "parallel"), blocks=blocks, temps=_nbytes((tm, tn), F32), comm=comm,
    )(dxb, w_down)


def _ffn_mid_bwd(da, u0, ua, cw, nseq, comm=None):
    _, t, f = u0.shape
    s = t // nseq
    c = _tile(f, 128, LANES)
    r = _tile(s, CHUNK, SUBLANES_BF16)
    n = s // r

    def body(da_ref, u_ref, ua_ref, w_ref, du_ref, gw_ref, gb_ref):
        @pl.when(pl.program_id(1) == 0)
        def _():
            gw_ref[...] = jnp.zeros_like(gw_ref)
            gb_ref[...] = jnp.zeros_like(gb_ref)

        def step(i, carry):
            nxt, sums = carry
            rows = pl.ds(pl.multiple_of((n - 1 - i) * r, r), r)
            ug = ua_ref[0, rows, :].astype(F32)
            uv = ua_ref[1, rows, :].astype(F32)
            sg = _sigmoid(ug)
            dacc = da_ref[rows, :].astype(F32)
            dus = (dacc * uv * sg * (1.0 + ug * (1.0 - sg)), dacc * (ug * sg))
            first, new_sums = [], []
            for p in range(2):
                du = dus[p]
                d1, d2 = _up1_up2(du, nxt[p])
                du_ref[p, rows, :] = _conv3_rows(du, d1, d2, w_ref, p).astype(BF16)
                u = u_ref[p, rows, :].astype(F32)
                sb, s0, s1, s2 = sums[p]
                new_sums.append((sb + _fold8(du), s0 + _fold8(d2 * u), s1 + _fold8(d1 * u), s2 + _fold8(du * u)))
                first.append(du[:HALO])
            return tuple(first), tuple(new_sums)

        zero = jnp.zeros((HALO, c), F32)
        _, sums = lax.fori_loop(0, n, step, ((zero, zero), ((zero,) * 4,) * 2))
        for p in range(2):
            sb, s0, s1, s2 = sums[p]
            gb_ref[p] += jnp.sum(sb, axis=0, keepdims=True)
            gw_ref[p, pl.ds(0, 1), :] += jnp.sum(s0, axis=0, keepdims=True)
            gw_ref[p, pl.ds(1, 1), :] += jnp.sum(s1, axis=0, keepdims=True)
            gw_ref[p, pl.ds(2, 1), :] += jnp.sum(s2, axis=0, keepdims=True)

    blocks = _nbytes((s, c), BF16) + 3 * _nbytes((2, s, c), BF16)
    return _pcall(
        body, name="ffn_mid_bwd", grid=(f // c, nseq),
        in_specs=[pl.BlockSpec((s, c), lambda j, b: (b, j)),
                  pl.BlockSpec((2, s, c), lambda j, b: (0, b, j)),
                  pl.BlockSpec((2, s, c), lambda j, b: (0, b, j)),
                  pl.BlockSpec((2, 3, c), lambda j, b: (0, 0, j))],
        out_specs=[pl.BlockSpec((2, s, c), lambda j, b: (0, b, j)),
                   pl.BlockSpec((2, 3, c), lambda j, b: (0, 0, j)),
                   pl.BlockSpec((2, 1, c), lambda j, b: (0, 0, j))],
        out_shape=[SDS((2, t, f), BF16), SDS((2, 3, f), F32), SDS((2, 1, f), F32)],
        sem=("parallel", "arbitrary"), blocks=blocks, temps=4 * 1024 * 1024, comm=comm,
    )(da, u0, ua, cw)


def _wgrad(a, b, name, *, tr, tn, b_plane_of=None, out_shards=None, comm=None):
    t, m = a.shape
    n_total = b.shape[-1] * (b.shape[0] if b.ndim == 3 else 1)
    tk = _tile(t, 1024, SUBLANES_BF16)
    nk = t // tk

    def body(a_ref, b_ref, o_ref, acc):
        _acc_over(pl.program_id(2), nk, _dot_ta(a_ref[...], b_ref[...]), acc, o_ref)

    if b.ndim == 3:
        b_spec = pl.BlockSpec((None, tk, tn), lambda r, n, k: (b_plane_of(n)[0], k, b_plane_of(n)[1]))
    else:
        b_spec = pl.BlockSpec((tk, tn), lambda r, n, k: (k, n))
    if out_shards is None:
        o_spec = pl.BlockSpec((tr, tn), lambda r, n, k: (r, n))
        o_shape = SDS((m, n_total), BF16)
    else:
        nps = n_total // out_shards // tn
        o_spec = pl.BlockSpec((None, tr, tn), lambda r, n, k: (n // nps, r, n % nps))
        o_shape = SDS((out_shards, m, n_total // out_shards), BF16)
    blocks = _nbytes((tk, tr), BF16) + _nbytes((tk, tn), BF16) + _nbytes((tr, tn), BF16)
    return _pcall(
        body, name=name, grid=(m // tr, n_total // tn, nk),
        in_specs=[pl.BlockSpec((tk, tr), lambda r, n, k: (k, r)), b_spec],
        out_specs=[o_spec], out_shape=[o_shape],
        scratch_shapes=[pltpu.VMEM((tr, tn), F32)],
        sem=("parallel", "parallel", "arbitrary"), blocks=blocks, temps=2 * _nbytes((tr, tn), F32), comm=comm,
    )(a, b)


def _wgrad3(lhs3, rhs3, comm=None):
    nw, t, d = lhs3.shape
    tk = _tile(t, 1024, SUBLANES_BF16)
    nk = t // tk

    def body(a_ref, b_ref, o_ref, acc):
        _acc_over(pl.program_id(1), nk, _dot_ta(a_ref[...], b_ref[...]), acc, o_ref)

    blocks = 2 * _nbytes((tk, d), BF16) + _nbytes((d, d), BF16)
    return _pcall(
        body, name="wgrad_sq3", grid=(nw, nk),
        in_specs=[pl.BlockSpec((None, tk, d), lambda w, k: (w, k, 0)),
                  pl.BlockSpec((None, tk, d), lambda w, k: (w, k, 0))],
        out_specs=[pl.BlockSpec((None, d, d), lambda w, k: (w, 0, 0))],
        out_shape=[SDS((nw, d, d), BF16)],
        scratch_shapes=[pltpu.VMEM((d, d), F32)],
        sem=("parallel", "arbitrary"), blocks=blocks, temps=2 * _nbytes((d, d), F32), comm=comm,
    )(lhs3, rhs3)


def _ffn_bwd_dx1(du0, w_up, x1, dx2, g_ffn, n_planes_out, comm=None):
    _, t, f = du0.shape
    d = x1.shape[1]
    nsh, _, ws = w_up.shape
    tm = _tile(t, 256, SUBLANES_BF16)
    spp = f // ws

    def body(du_ref, w_ref, x1_ref, dx2_ref, g_ref, dx1_ref, dxb_ref, gg_ref):
        @pl.when(pl.program_id(0) == 0)
        def _():
            gg_ref[...] = jnp.zeros_like(gg_ref)

        dh = None
        for k in range(nsh):
            part = _dot_tb(du_ref[k // spp, :, (k % spp) * ws:(k % spp + 1) * ws], w_ref[k])
            dh = part if dh is None else dh + part
        xh, inv = _rms_fwd(x1_ref[...])
        gg_ref[...] += jnp.sum(dh * xh, axis=0, keepdims=True)
        dx1 = dx2_ref[...] + _rms_bwd(dh, xh, inv, g_ref[...])
        dx1_ref[...] = dx1
        dxb_ref[...] = dx1.astype(BF16)

    blocks = _nbytes((2, tm, f), BF16) + 3 * _nbytes((tm, d), F32) + _nbytes((tm, d), BF16)
    return _pcall(
        body, name="ffn_bwd_dx1", grid=(t // tm,),
        in_specs=[pl.BlockSpec((2, tm, f), lambda i: (0, i, 0)),
                  pl.BlockSpec((nsh, d, ws), lambda i: (0, 0, 0), pipeline_mode=pl.Buffered(1)),
                  pl.BlockSpec((tm, d), lambda i: (i, 0)),
                  pl.BlockSpec((tm, d), lambda i: (i, 0)),
                  pl.BlockSpec((1, d), lambda i: (0, 0))],
        out_specs=[pl.BlockSpec((tm, d), lambda i: (i, 0)),
                   pl.BlockSpec((None, tm, d), lambda i: (n_planes_out - 1, i, 0)),
                   pl.BlockSpec((1, d), lambda i: (0, 0))],
        out_shape=[SDS((t, d), F32), SDS((n_planes_out, t, d), BF16), SDS((1, d), F32)],
        sem=("arbitrary",), blocks=blocks, temps=_nbytes(w_up.shape, BF16) + 8 * _nbytes((tm, d), F32), comm=comm,
    )(du0, w_up, x1, dx2, g_ffn)


def _mixer_bwd(rhs3, z, ypc, w3, comm=None):
    _, t, d = rhs3.shape
    tm = _tile(t, 256, SUBLANES_BF16)

    def body(dx_ref, zgp, zgc, ypc_ref, w_ref, dyo, dzo, dpq):
        dm = _dot_tb(dx_ref[...], w_ref[2])
        sp = _sigmoid(zgp[...].astype(F32))
        sc = _sigmoid(zgc[...].astype(F32))
        dyp = (dm * sp).astype(BF16)
        dyc = (dm * sc).astype(BF16)
        dzo[0] = (dm * ypc_ref[0].astype(F32) * sp * (1.0 - sp)).astype(BF16)
        dzo[1] = (dm * ypc_ref[1].astype(F32) * sc * (1.0 - sc)).astype(BF16)
        dyo[0] = dyp
        dyo[1] = dyc
        dpq[0] = _dot_tb(dyp, w_ref[0]).astype(BF16)
        dpq[1] = _dot_tb(dyc, w_ref[1]).astype(BF16)

    blocks = _nbytes((tm, d), BF16) * 3 + _nbytes((2, tm, d), BF16) * 4 + _nbytes((3, d, d), BF16)
    return _pcall(
        body, name="mixer_bwd", grid=(t // tm,),
        in_specs=[pl.BlockSpec((None, tm, d), lambda i: (2, i, 0)),
                  pl.BlockSpec((tm, d), lambda i: (i, 4)),
                  pl.BlockSpec((tm, d), lambda i: (i, 5)),
                  pl.BlockSpec((2, tm, d), lambda i: (0, i, 0)),
                  pl.BlockSpec((3, d, d), lambda i: (0, 0, 0))],
        out_specs=[pl.BlockSpec((2, tm, d), lambda i: (0, i, 0)),
                   pl.BlockSpec((2, tm, d), lambda i: (2, i, 0)),
                   pl.BlockSpec((2, tm, d), lambda i: (0, i, 0))],
        out_shape=[SDS(rhs3.shape, BF16), SDS((N_SPLITS, t, d), BF16), SDS((2, t, d), BF16)],
        input_output_aliases={0: 0},
        sem=("parallel",), blocks=blocks, temps=8 * _nbytes((tm, d), F32), comm=comm,
    )(rhs3, z, z, ypc, w3)


def _conv_bwd(dz, dpq, z, conv_w, nseq, comm=None):
    _, t, d = dz.shape
    s = t // nseq
    c = _tile(d, 128, LANES)
    nb = d // c

    def body(dz_in, dq_ref, zb, zc, zv, cw, dzo, gw_ref):
        del dz_in

        @pl.when(pl.program_id(1) == 0)
        def _():
            gw_ref[...] = jnp.zeros_like(gw_ref)

        row = lax.broadcasted_iota(jnp.int32, (s, c), 0)
        b = zb[...].astype(F32)
        cm = zc[...].astype(F32)
        v = zv[...].astype(F32)
        cv = cm * v
        cv1 = _shift_down(cv, 1, row)
        cv2 = _shift_down(cv, 2, row)
        w0, w1, w2 = cw[pl.ds(0, 1), :], cw[pl.ds(1, 1), :], cw[pl.ds(2, 1), :]
        cc = w2 * cv + w1 * cv1 + w0 * cv2
        dq = dq_ref[...].astype(F32)
        dzo[0] = (dq * cc).astype(BF16)
        dcc = dq * b
        gw_ref[pl.ds(0, 1), :] += jnp.sum(dcc * cv2, axis=0, keepdims=True)
        gw_ref[pl.ds(1, 1), :] += jnp.sum(dcc * cv1, axis=0, keepdims=True)
        gw_ref[pl.ds(2, 1), :] += jnp.sum(dcc * cv, axis=0, keepdims=True)
        dcv = w2 * dcc + w1 * _shift_up(dcc, 1, row) + w0 * _shift_up(dcc, 2, row)
        dzo[1] = (dcv * v).astype(BF16)
        dzo[2] = (dcv * cm).astype(BF16)

    blocks = 4 * _nbytes((s, c), BF16) + _nbytes((3, s, c), BF16)
    return _pcall(
        body, name="conv_bwd", grid=(nb, nseq),
        in_specs=[ANY,
                  pl.BlockSpec((None, s, c), lambda j, b: (1, b, j)),
                  pl.BlockSpec((s, c), lambda j, b: (b, nb + j)),
                  pl.BlockSpec((s, c), lambda j, b: (b, 2 * nb + j)),
                  pl.BlockSpec((s, c), lambda j, b: (b, 3 * nb + j)),
                  pl.BlockSpec((3, c), lambda j, b: (0, j))],
        out_specs=[pl.BlockSpec((3, s, c), lambda j, b: (0, b, j)),
                   pl.BlockSpec((3, c), lambda j, b: (0, j))],
        out_shape=[SDS(dz.shape, BF16), SDS((3, d), F32)],
        input_output_aliases={0: 0},
        sem=("parallel", "arbitrary"), blocks=blocks, temps=16 * _nbytes((s, c), F32), comm=comm,
    )(dz, dpq, z, z, z, conv_w)


def _pool_bwd_call(dz, dpq, z, pool_w, pool_scale, nseq, comm=None):
    _, t, d = dz.shape
    s = t // nseq
    c = d // N_GROUPS

    def body(dz_in, dp_ref, zp, pw, ps, dzo, gpw_ref, gps_ref):
        del dz_in
        j = pl.program_id(0)

        @pl.when(pl.program_id(1) == 0)
        def _():
            gpw_ref[...] = jnp.zeros_like(gpw_ref)
            gps_ref[...] = jnp.zeros_like(gps_ref)

        row = lax.broadcasted_iota(jnp.int32, (s, c), 0)
        for gi, win in enumerate(POOL_WINDOWS):
            @pl.when(j == gi)
            def _(win=win):
                pb = _pool_fwd(zp[...].astype(F32), win, row).astype(BF16)
                plin = _dot(pb, pw[...])
                dps = dp_ref[...].astype(F32)
                gps_ref[...] += jnp.sum(dps * plin, axis=0, keepdims=True)
                dplb = (dps * ps[...]).astype(BF16)
                gpw_ref[...] += _dot_ta(pb, dplb)
                dzo[...] = _pool_bwd(_dot_tb(dplb, pw[...]), win, row).astype(BF16)

    blocks = 3 * _nbytes((s, c), BF16) + _nbytes((c, c), BF16) + _nbytes((c, c), F32)
    return _pcall(
        body, name="pool_bwd", grid=(N_GROUPS, nseq),
        in_specs=[ANY,
                  pl.BlockSpec((None, s, c), lambda j, b: (0, b, j)),
                  pl.BlockSpec((s, c), lambda j, b: (b, j)),
                  pl.BlockSpec((None, c, c), lambda j, b: (j, 0, 0)),
                  pl.BlockSpec((1, c), lambda j, b: (0, j))],
        out_specs=[pl.BlockSpec((None, s, c), lambda j, b: (3, b, j)),
                   pl.BlockSpec((None, c, c), lambda j, b: (j, 0, 0)),
                   pl.BlockSpec((1, c), lambda j, b: (0, j))],
        out_shape=[SDS(dz.shape, BF16), SDS((N_GROUPS, c, c), F32), SDS((1, d), F32)],
        input_output_aliases={0: 0},
        sem=("parallel", "arbitrary"), blocks=blocks, temps=10 * _nbytes((s, c), F32), comm=comm,
    )(dz, dpq, z, pool_w, pool_scale)


def _dz_plane(zb):
    return jnp.where(zb < 4, (zb + 3) % 4, zb)


def _wgrad_in(h1, dz, nsh, comm=None):
    t, d = h1.shape
    ws = N_SPLITS * d // nsh
    kb = _tile(math.gcd(d, ws), 512, LANES)
    npl = d // kb
    nps = ws // kb
    tk = _tile(t, 1024, SUBLANES_BF16)
    nk = t // tk

    def body(a_ref, b_ref, o_ref, acc):
        _acc_over(pl.program_id(1), nk, _dot_ta(a_ref[...], b_ref[...]), acc, o_ref)

    blocks = _nbytes((tk, d), BF16) + _nbytes((tk, kb), BF16) + _nbytes((d, kb), BF16)
    return _pcall(
        body, name="wgrad_in", grid=(N_SPLITS * npl, nk),
        in_specs=[pl.BlockSpec((tk, d), lambda cb, k: (k, 0)),
                  pl.BlockSpec((None, tk, kb), lambda cb, k: (_dz_plane(cb // npl), k, cb % npl))],
        out_specs=[pl.BlockSpec((None, d, kb), lambda cb, k: (cb // nps, 0, cb % nps))],
        out_shape=[SDS((nsh, d, ws), BF16)],
        scratch_shapes=[pltpu.VMEM((d, kb), F32)],
        sem=("parallel", "arbitrary"), blocks=blocks, temps=2 * _nbytes((d, kb), F32), comm=comm,
    )(h1, dz)


def _mixer_bwd_dx(dz, w_in, x, dx1, g_mix, comm=None):
    npln, t, d = dz.shape
    nsh, _, ws = w_in.shape
    tm = _tile(t, 256, SUBLANES_BF16)
    kb = _tile(math.gcd(d, ws), 512, LANES)
    npl = d // kb
    nps = ws // kb

    def body(dz_ref, w_ref, x_ref, dx1_ref, g_ref, dx_ref, gg_ref):
        @pl.when(pl.program_id(0) == 0)
        def _():
            gg_ref[...] = jnp.zeros_like(gg_ref)

        dh = None
        for cb in range(npln * npl):
            zb = cb // npl
            plane = (zb + 3) % 4 if zb < 4 else zb
            part = _dot_tb(dz_ref[plane, :, (cb % npl) * kb:(cb % npl + 1) * kb],
                           w_ref[cb // nps, :, (cb % nps) * kb:(cb % nps + 1) * kb])
            dh = part if dh is None else dh + part
        xh, inv = _rms_fwd(x_ref[...])
        gg_ref[...] += jnp.sum(dh * xh, axis=0, keepdims=True)
        dx_ref[...] = dx1_ref[...] + _rms_bwd(dh, xh, inv, g_ref[...])

    blocks = _nbytes((npln, tm, d), BF16) + 3 * _nbytes((tm, d), F32)
    return _pcall(
        body, name="mixer_bwd_dx", grid=(t // tm,),
        in_specs=[pl.BlockSpec((npln, tm, d), lambda i: (0, i, 0)),
                  pl.BlockSpec((nsh, d, ws), lambda i: (0, 0, 0), pipeline_mode=pl.Buffered(1)),
                  pl.BlockSpec((tm, d), lambda i: (i, 0)),
                  pl.BlockSpec((tm, d), lambda i: (i, 0)),
                  pl.BlockSpec((1, d), lambda i: (0, 0))],
        out_specs=[pl.BlockSpec((tm, d), lambda i: (i, 0)),
                   pl.BlockSpec((1, d), lambda i: (0, 0))],
        out_shape=[SDS((t, d), F32), SDS((1, d), F32)],
        sem=("arbitrary",), blocks=blocks, temps=_nbytes(w_in.shape, BF16) + 8 * _nbytes((tm, d), F32), comm=comm,
    )(dz, w_in, x, dx1, g_mix)


N_BIG = 5
SHARD_MAJOR = (0, 2)
ROWS_DIM1 = (1, 4)


def _ds(start, size, align):
    if isinstance(start, int):
        return pl.ds(start, size)
    return pl.ds(pl.multiple_of(start, align), size)


def _piece(a, ref, k, h):
    if a in SHARD_MAJOR:
        r = ref.shape[1] // 2
        return ref.at[k, _ds(h * r, r, SUBLANES_BF16), :]
    if a in ROWS_DIM1:
        r = ref.shape[1] // 8
        return ref.at[:, _ds((2 * k + h) * r, r, SUBLANES_BF16), :]
    r = ref.shape[0] // 8
    return ref.at[_ds((2 * k + h) * r, r, SUBLANES_BF16), :]


def _half(a, ref, h):
    if a in ROWS_DIM1:
        r = ref.shape[1] // 2
        return ref.at[:, _ds(h * r, r, SUBLANES_BF16), :]
    r = ref.shape[0] // 2
    return ref.at[_ds(h * r, r, SUBLANES_BF16), :]


def _piece_shape(a, full_shape):
    if a in SHARD_MAJOR:
        return (full_shape[1] // 2, full_shape[2])
    if a in ROWS_DIM1:
        return (full_shape[0], full_shape[1] // 8, full_shape[2])
    return (full_shape[0] // 8, full_shape[1])


def _shard_shape(a, full_shape):
    if a in SHARD_MAJOR:
        return (full_shape[1], full_shape[2])
    if a in ROWS_DIM1:
        return (full_shape[0], full_shape[1] // 4, full_shape[2])
    return (full_shape[0] // 4, full_shape[1])


def _rows_axis(a):
    return 1 if a in ROWS_DIM1 else 0


def _piece_block(a, full_shape):
    ps = _piece_shape(a, full_shape)
    if a in SHARD_MAJOR:
        return (None,) + ps, lambda k, c: (k, c, 0)
    if a in ROWS_DIM1:
        return ps, lambda k, c: (0, 2 * k + c, 0)
    return ps, lambda k, c: (2 * k + c, 0)


def _coords():
    return lax.axis_index("x"), lax.axis_index("y"), lax.axis_index("c")


def _peer_chips(x, y):
    return [(1 - x, y), (x, 1 - y), (1 - x, 1 - y)]


def _remote(src, dst, ssem, rsem, dev):
    return pltpu.make_async_remote_copy(src_ref=src, dst_ref=dst, send_sem=ssem, recv_sem=rsem,
                                        device_id=dev, device_id_type=MESH)


def _dma_sems(*counts):
    return [pltpu.SemaphoreType.DMA((n,)) for n in counts]


def _symmetric(ins, out_shapes, sems, copies, aliases=None):
    def start(cins, couts, csems):
        for cp in copies(cins, couts, csems):
            cp.start()

    def finish(cins, couts, csems):
        for cp in copies(cins, couts, csems):
            cp.wait()

    return _Comm(ins, out_shapes, sems, start, finish, aliases)


def _rows_part(a, ref, part):
    if part is None:
        return ref
    p, n = part
    ax = _rows_axis(a)
    r = ref.shape[ax] // n
    return ref.at[tuple(pl.ds(p * r, r) if q == ax else slice(None) for q in range(len(ref.shape)))]


def _merge(comms):
    ins, outs, sems, aliases, spans = [], [], [], {}, []
    for cm in comms:
        spans.append((len(ins), len(outs), len(sems)))
        for i, o in cm.aliases.items():
            aliases[len(ins) + i] = len(outs) + o
        ins += cm.ins
        outs += cm.out_shapes
        sems += cm.sems

    def each(fn_name):
        def run(cins, couts, csems):
            for cm, (i0, o0, s0) in zip(comms, spans):
                getattr(cm, fn_name)(cins[i0:i0 + len(cm.ins)], couts[o0:o0 + len(cm.out_shapes)],
                                     csems[s0:s0 + len(cm.sems)])
        return run

    return _Comm(ins, outs, sems, each("start"), each("finish"), aliases)


def _gather_comm(arrs, locs, full_shapes, part=None, into=None):
    n = len(arrs)

    def own(cins, couts, csems):
        x, y, c = _coords()
        j = 2 * x + y
        return [_remote(_rows_part(a, _half(a, cins[q], h), part), _rows_part(a, _piece(a, couts[q], j, h), part),
                        csems[0].at[2 * q + h], csems[1].at[2 * q + h], (x, y, 1 - c))
                for q, a in enumerate(arrs) for h in range(2)]

    def sends(cins, couts, csems):
        x, y, c = _coords()
        j = 2 * x + y
        return [_remote(_rows_part(a, _half(a, cins[q], c), part), _rows_part(a, _piece(a, couts[q], j, c), part),
                        csems[2].at[3 * q + i], csems[3].at[3 * q + i], (px, py, c))
                for q, a in enumerate(arrs) for i, (px, py) in enumerate(_peer_chips(x, y))]

    def forwards(couts, csems, half_of):
        x, y, c = _coords()
        out = []
        for q, a in enumerate(arrs):
            for i, (px, py) in enumerate(_peer_chips(x, y)):
                landed = _rows_part(a, _piece(a, couts[q], 2 * px + py, half_of(c)), part)
                out.append(_remote(landed, landed, csems[4].at[3 * q + i], csems[5].at[3 * q + i], (x, y, 1 - c)))
        return out

    def start(cins, couts, csems):
        for cp in sends(cins, couts, csems) + own(cins, couts, csems):
            cp.start()

    def finish(cins, couts, csems):
        fw = forwards(couts, csems, lambda c: c)
        for cp, f in zip(sends(cins, couts, csems), fw):
            cp.wait_recv()
            f.start()
        for f in forwards(couts, csems, lambda c: 1 - c):
            f.wait_recv()
        for cp in sends(cins, couts, csems) + fw:
            cp.wait_send()
        for cp in own(cins, couts, csems):
            cp.wait()

    ins = [locs[a] for a in arrs] + ([into[a] for a in arrs] if into else [])
    return _Comm(ins, [SDS(full_shapes[a], BF16) for a in arrs],
                 _dma_sems(2 * n, 2 * n, 3 * n, 3 * n, 3 * n, 3 * n), start, finish,
                 aliases={n + q: q for q in range(n)} if into else None)


def _halves_comm(arrs, gbs):
    n = len(arrs)

    def copies(cins, couts, csems):
        x, y, c = _coords()
        return [_remote(_piece(a, cins[q], k, 1 - c), couts[q].at[k], csems[0].at[4 * q + k], csems[1].at[4 * q + k],
                        (x, y, 1 - c)) for q, a in enumerate(arrs) for k in range(4)]

    return _symmetric([gbs[a] for a in arrs], [SDS((4,) + _piece_shape(a, gbs[a].shape), BF16) for a in arrs],
                      _dma_sems(4 * n, 4 * n), copies)


def _chips_comm(arrs, ps, part=None, into=None):
    n = len(arrs)

    def copies(cins, couts, csems):
        x, y, c = _coords()
        return [_remote(_rows_part(a, cins[q].at[2 * px + py], part), _rows_part(a, couts[q].at[i], part),
                        csems[0].at[3 * q + i], csems[1].at[3 * q + i], (px, py, c))
                for q, a in enumerate(arrs) for i, (px, py) in enumerate(_peer_chips(x, y))]

    ins = [ps[a] for a in arrs] + ([into[a] for a in arrs] if into else [])
    return _symmetric(ins, [SDS((3,) + ps[a].shape[1:], BF16) for a in arrs], _dma_sems(3 * n, 3 * n), copies,
                      aliases={n + q: q for q in range(n)} if into else None)


def _result_comm(arrs, gs):
    n = len(arrs)

    def copies(cins, couts, csems):
        x, y, c = _coords()
        return [_remote(_half(a, cins[q], c), _half(a, couts[q], c), csems[0].at[q], csems[1].at[q], (x, y, 1 - c))
                for q, a in enumerate(arrs)]

    return _symmetric([gs[a] for a in arrs], [SDS(gs[a].shape, F32) for a in arrs], _dma_sems(n, n), copies,
                      aliases={q: q for q in range(n)})


def _add_halves(arrs, gbs, lands, c_arr, name):
    n = len(arrs)

    def body(c_ref, *refs):
        del c_ref
        for q in range(n):
            refs[2 * n + q][...] = (refs[q][...].astype(F32) + refs[n + q][...].astype(F32)).astype(BF16)

    g_specs, l_specs, o_specs, blocks = [], [], [], 0
    for a in arrs:
        bs, imap = _piece_block(a, gbs[a].shape)
        ps = _piece_shape(a, gbs[a].shape)
        g_specs.append(pl.BlockSpec(bs, lambda k, c_ref, imap=imap: imap(k, c_ref[0])))
        nd = len(ps)
        l_specs.append(pl.BlockSpec((None,) + ps, lambda k, c_ref, nd=nd: (k,) + (0,) * nd))
        o_specs.append(pl.BlockSpec((None,) + ps, lambda k, c_ref, nd=nd: (k,) + (0,) * nd))
        blocks += 3 * _nbytes(ps, BF16)
    return list(pl.pallas_call(
        body, name=name,
        grid_spec=pltpu.PrefetchScalarGridSpec(
            num_scalar_prefetch=1, grid=(4,), in_specs=g_specs + l_specs, out_specs=o_specs),
        out_shape=[SDS((4,) + _piece_shape(a, gbs[a].shape), BF16) for a in arrs],
        compiler_params=_params(("parallel",), blocks, blocks),
    )(c_arr, *[gbs[a] for a in arrs], *lands))


def _sum_chips(a, p, land, shard_shape, jc_arr, name):
    ps = land.shape[1:]
    ax = _rows_axis(a)
    rows = ps[ax]
    nsub = 2 if rows % (2 * SUBLANES_BF16) == 0 else 1
    bs = tuple(r // nsub if q == ax else r for q, r in enumerate(ps))
    nd = len(ps)

    def at_rows(v):
        return tuple(v if q == ax else 0 for q in range(nd))

    def body(jc_ref, p_ref, l_ref, o_ref):
        del jc_ref
        acc = p_ref[...].astype(F32) + l_ref[0].astype(F32)
        acc = acc + l_ref[1].astype(F32)
        o_ref[...] = acc + l_ref[2].astype(F32)

    blocks = 4 * _nbytes(bs, BF16) + _nbytes(bs, F32)
    return pl.pallas_call(
        body, name=name,
        grid_spec=pltpu.PrefetchScalarGridSpec(
            num_scalar_prefetch=1, grid=(nsub,),
            in_specs=[pl.BlockSpec((None,) + bs, lambda s, jc: (jc[0],) + at_rows(s)),
                      pl.BlockSpec((3,) + bs, lambda s, jc: (0,) + at_rows(s))],
            out_specs=pl.BlockSpec(bs, lambda s, jc: at_rows(jc[1] * nsub + s))),
        out_shape=SDS(shard_shape, F32),
        compiler_params=_params(("parallel",), blocks, 2 * _nbytes(bs, F32)),
    )(jc_arr, p, land)


def _small_comm(v):
    rows = v.shape[0]

    def copies(cins, couts, csems):
        x, y, c = _coords()
        me = 4 * x + 2 * y + c
        out = [pltpu.make_async_copy(cins[0], couts[0].at[me], csems[0].at[0])]
        for dlt in range(1, 8):
            px = 1 - x if (dlt >> 2) & 1 else x
            py = 1 - y if (dlt >> 1) & 1 else y
            pc = 1 - c if dlt & 1 else c
            out.append(_remote(cins[0], couts[0].at[me], csems[1].at[dlt - 1], csems[2].at[dlt - 1], (px, py, pc)))
        return out

    return _symmetric([v], [SDS((8, rows, LANES), F32)], _dma_sems(1, 7, 7), copies)


def _sum8(slots, name):
    def body(s_ref, o_ref):
        acc = s_ref[0]
        for i in range(1, 8):
            acc = acc + s_ref[i]
        o_ref[...] = acc

    return pl.pallas_call(
        body, name=name,
        in_specs=[pl.BlockSpec(memory_space=pltpu.VMEM)], out_specs=pl.BlockSpec(memory_space=pltpu.VMEM),
        out_shape=SDS(slots.shape[1:], F32),
    )(slots)


def _adamw(w, g, m, v, name, g_plane=None):
    rows, cols = w.shape
    tr = _tile(rows, max(SUBLANES_F32, (256 * 1024 // cols) // SUBLANES_F32 * SUBLANES_F32), SUBLANES_F32)

    def body(w_ref, g_ref, m_ref, v_ref, go_ref, d_ref, mo_ref, vo_ref):
        gr = g_ref[...]
        mn = ADAM_B1 * m_ref[...] + (1.0 - ADAM_B1) * gr
        vn = ADAM_B2 * v_ref[...] + (1.0 - ADAM_B2) * (gr * gr)
        m_hat = mn / (1.0 - ADAM_B1 ** ADAM_STEP)
        v_hat = vn / (1.0 - ADAM_B2 ** ADAM_STEP)
        d_ref[...] = -ADAM_LR * (m_hat / (jnp.sqrt(v_hat) + ADAM_EPS) + ADAM_WD * w_ref[...])
        go_ref[...] = gr
        mo_ref[...] = mn
        vo_ref[...] = vn

    spec = pl.BlockSpec((tr, cols), lambda i: (i, 0))
    g_spec = spec if g_plane is None else pl.BlockSpec((None, tr, cols), lambda i: (g_plane, i, 0))
    return pl.pallas_call(
        body, name=name, grid=(rows // tr,),
        in_specs=[spec, g_spec, spec, spec], out_specs=[spec, spec, spec, spec],
        out_shape=[SDS((rows, cols), F32)] * 4,
        compiler_params=_params(("parallel",), 8 * _nbytes((tr, cols), F32), 4 * _nbytes((tr, cols), F32)),
    )(w, g, m, v)


def _pack(parts):
    rows = []
    for p in parts:
        r = p.reshape(-1, LANES)
        pad = (-r.shape[0]) % SUBLANES_F32
        if pad:
            r = jnp.pad(r, ((0, pad), (0, 0)))
        rows.append(r)
    return jnp.concatenate(rows, axis=0)


def _unpack(packed, shapes):
    out, at = [], 0
    for s in shapes:
        n = 1
        for q in s:
            n *= q
        r = n // LANES
        out.append(packed[at:at + r].reshape(s))
        at += r + (-r) % SUBLANES_F32
    return out


def kernel(x, norm_mix, w_in, pool_w, pool_scale, w_pool_proj, conv_w, w_conv_out, w_o, norm_ffn, w_up, ffn_conv_w, ffn_conv_b, w_down, norm_final, loss_target, m_norm_mix, m_w_in, m_pool_w, m_pool_scale, m_w_pool_proj, m_conv_w, m_w_conv_out, m_w_o, m_norm_ffn, m_w_up, m_ffn_conv_w, m_ffn_conv_b, m_w_down, m_norm_final, v_norm_mix, v_w_in, v_pool_w, v_pool_scale, v_w_pool_proj, v_conv_w, v_w_conv_out, v_w_o, v_norm_ffn, v_w_up, v_ffn_conv_w, v_ffn_conv_b, v_w_down, v_norm_final):
    nseq, seq, d = x.shape
    t = nseq * seq
    f = w_down.shape[1] * 4
    c = d // N_GROUPS
    xy = lax.axis_index("x") * 2 + lax.axis_index("y")
    c_arr = lax.axis_index("c").astype(jnp.int32).reshape(1)
    jc_arr = jnp.stack([xy, lax.axis_index("c")]).astype(jnp.int32)
    nsh = 4
    zero = jnp.zeros((), jnp.int32)

    locs = [w_in[0].astype(BF16),
            jnp.stack([w_pool_proj[0], w_conv_out[0], w_o[0]]).astype(BF16),
            w_up[0].astype(BF16), w_down[0].astype(BF16), pool_w[0].astype(BF16)]
    full_shapes = [(nsh, d, N_SPLITS * d // nsh), (3, d, d), (nsh, d, 2 * f // nsh), (f, d), (N_GROUPS, c, c)]

    cw_pad = lax.dynamic_update_slice(jnp.zeros((3, d), F32), conv_w[0], (zero, xy * (d // 4)))
    fw_pad = lax.dynamic_update_slice(jnp.zeros((3, 2 * f), F32), ffn_conv_w[0], (zero, xy * (f // 2)))
    small_w = _pack([cw_pad, fw_pad]) * 0.5

    x2d = x.reshape(t, d)
    tgt = loss_target.reshape(t, d)
    w_in_f, slots_w = _run_comm(_merge([_gather_comm([0], locs, full_shapes), _small_comm(small_w)]), "gather_w_in")
    conv_w_f, ffn_cw_f = _unpack(_sum8(slots_w, "sum8_weights"), [(3, d), (3, 2 * f)])
    ffn_cw_p = ffn_cw_f.reshape(3, 2, f).transpose(1, 0, 2)
    ffn_cb_p = ffn_conv_b.reshape(2, 1, f)
    (z, h1), (w3_f, pool_w_f, w_up_f) = _fwd_in(
        x2d, norm_mix, w_in_f,
        _merge([_gather_comm([1, 4], locs, full_shapes), _gather_comm([2], locs, full_shapes, part=(0, 2))]))
    lhs3 = _mixer_mid_fwd(z, pool_w_f, pool_scale, conv_w_f, nseq)
    (lhs3, ypc, x1, h2), (w_up_f,) = _mixer_out(
        lhs3, z, x2d, w3_f, norm_ffn, _gather_comm([2], locs, full_shapes, part=(1, 2), into={2: w_up_f}))
    (u0,), (w_down_f,) = _ffn_up(h2, w_up_f, f, _gather_comm([3], locs, full_shapes))
    act, ua = _ffn_mid_fwd(u0, ffn_cw_p, ffn_cb_p, nseq)
    dx2, dx2b, loss11, g_norm_final = _ffn_down_loss(act, w_down_f, x1, tgt, norm_final.reshape(1, d))

    gbs, lands, ps, lands2, rs = {}, {}, {}, {}, {}
    tn_up = _tile(2 * f // nsh, 1408, LANES)
    npp = f // tn_up

    def add(arrs, name):
        for a, p in zip(arrs, _add_halves(arrs, gbs, [lands[a] for a in arrs], c_arr, name)):
            ps[a] = p

    def summed(a):
        rs[a] = _sum_chips(a, ps[a], lands2[a], _shard_shape(a, full_shapes[a]), jc_arr, "sum_chips_%d" % a)

    (gbs[3],), _ = _wgrad(act, dx2b, "wgrad_down", tr=tn_up, tn=d)
    (da,), (lands[3],) = _ffn_bwd_da(dx2b, w_down_f, _halves_comm([3], gbs))
    add([3], "add_halves_down")
    (du0, g_ffn_cw_p, g_ffn_cb_p), (lands2[3],) = _ffn_mid_bwd(da, u0, ua, ffn_cw_p, nseq, _chips_comm([3], ps))
    summed(3)
    (gbs[2],), (rs[3],) = _wgrad(h2, du0, "wgrad_up", tr=d, tn=tn_up, b_plane_of=lambda n: (n // npp, n % npp),
                                 out_shards=nsh, comm=_result_comm([3], rs))
    (dx1, rhs3, g_norm_ffn), (lands[2],) = _ffn_bwd_dx1(du0, w_up_f, x1, dx2, norm_ffn, 3, _halves_comm([2], gbs))
    add([2], "add_halves_up")
    (rhs3, dz, dpq), (lands2[2],) = _mixer_bwd(rhs3, z, ypc, w3_f, _chips_comm([2], ps, part=(0, 2)))
    (gbs[1],), (lands2[2],) = _wgrad3(lhs3, rhs3, _chips_comm([2], ps, part=(1, 2), into=lands2))
    summed(2)
    (dz, g_conv_w), (lands[1], rs[2]) = _conv_bwd(dz, dpq, z, conv_w_f, nseq,
                                                  _merge([_halves_comm([1], gbs), _result_comm([2], rs)]))
    add([1], "add_halves_sq3")
    (dz, g_pool_w, g_pool_scale), _ = _pool_bwd_call(dz, dpq, z, pool_w_f, pool_scale, nseq)
    gbs[4] = g_pool_w.astype(BF16)
    (gbs[0],), (lands2[1],) = _wgrad_in(h1, dz, nsh, _chips_comm([1], ps))
    summed(1)
    lands[0], lands[4] = _run_comm(_halves_comm([0, 4], gbs), "exchange_halves_in")
    add([0, 4], "add_halves_in")
    g_ffn_cw = g_ffn_cw_p.transpose(1, 0, 2).reshape(3, 2 * f)
    small_a = _pack([g_pool_scale, g_norm_ffn, g_ffn_cb_p.reshape(1, 2 * f), g_norm_final.reshape(d), g_conv_w,
                     g_ffn_cw, jnp.pad(loss11, ((0, SUBLANES_F32 - 1), (0, LANES - 1)))])
    (grad_x, g_norm_mix), (lands2[0], lands2[4], rs[1], slots_a) = _mixer_bwd_dx(
        dz, w_in_f, x2d, dx1, norm_mix,
        _merge([_chips_comm([0, 4], ps), _result_comm([1], rs), _small_comm(small_a)]))
    summed(0)
    summed(4)
    rs[0], rs[4], slots_b = _run_comm(_merge([_result_comm([0, 4], rs), _small_comm(_pack([g_norm_mix]))]),
                                      "exchange_result_in")
    shapes_a = [(1, d), (1, d), (1, 2 * f), (d,), (3, d), (3, 2 * f), (SUBLANES_F32, LANES)]
    gs_pool_scale, gs_norm_ffn, gs_ffn_cb, gs_norm_final, gs_conv_w, gs_ffn_cw, loss_blk = _unpack(
        _sum8(slots_a, "sum8_grads"), shapes_a)
    (gs_norm_mix,) = _unpack(_sum8(slots_b, "sum8_norm_mix"), [(1, d)])
    gs_conv_w = lax.dynamic_slice(gs_conv_w, (zero, xy * (d // 4)), (3, d // 4))
    gs_ffn_cw = lax.dynamic_slice(gs_ffn_cw, (zero, xy * (f // 2)), (3, f // 2))

    def upd(w, g, m, v, name, g_plane=None):
        shape = w.shape
        rows = 1
        for q in shape[:-1]:
            rows *= q
        g2 = g if g_plane is not None else g.reshape(rows, shape[-1])
        outs = _adamw(w.reshape(rows, shape[-1]), g2, m.reshape(rows, shape[-1]), v.reshape(rows, shape[-1]),
                      name, g_plane)
        return [o.reshape(shape) for o in outs]

    res = {
        "w_in": upd(w_in, rs[0], m_w_in, v_w_in, "adamw_w_in"),
        "pool_w": upd(pool_w, rs[4], m_pool_w, v_pool_w, "adamw_pool_w"),
        "w_pool_proj": upd(w_pool_proj, rs[1], m_w_pool_proj, v_w_pool_proj, "adamw_w_pool_proj", 0),
        "w_conv_out": upd(w_conv_out, rs[1], m_w_conv_out, v_w_conv_out, "adamw_w_conv_out", 1),
        "w_o": upd(w_o, rs[1], m_w_o, v_w_o, "adamw_w_o", 2),
        "w_up": upd(w_up, rs[2], m_w_up, v_w_up, "adamw_w_up"),
        "w_down": upd(w_down, rs[3], m_w_down, v_w_down, "adamw_w_down"),
    }

    small_names = ["norm_mix", "pool_scale", "norm_ffn", "ffn_conv_b", "norm_final", "conv_w", "ffn_conv_w"]
    small_ws = [norm_mix, pool_scale, norm_ffn, ffn_conv_b, norm_final, conv_w, ffn_conv_w]
    small_ms = [m_norm_mix, m_pool_scale, m_norm_ffn, m_ffn_conv_b, m_norm_final, m_conv_w, m_ffn_conv_w]
    small_vs = [v_norm_mix, v_pool_scale, v_norm_ffn, v_ffn_conv_b, v_norm_final, v_conv_w, v_ffn_conv_w]
    small_gs = [gs_norm_mix, gs_pool_scale, gs_norm_ffn, gs_ffn_cb, gs_norm_final, gs_conv_w, gs_ffn_cw]
    _, sd, sm, sv = _adamw(_pack(small_ws), _pack(small_gs), _pack(small_ms), _pack(small_vs), "adamw_small")
    shapes = [w.shape for w in small_ws]
    sd, sm, sv = _unpack(sd, shapes), _unpack(sm, shapes), _unpack(sv, shapes)
    for i, nm in enumerate(small_names):
        res[nm] = [small_gs[i].reshape(shapes[i]), sd[i], sm[i], sv[i]]

    order = ["norm_mix", "w_in", "pool_w", "pool_scale", "w_pool_proj", "conv_w", "w_conv_out", "w_o", "norm_ffn",
             "w_up", "ffn_conv_w", "ffn_conv_b", "w_down", "norm_final"]
    return (loss_blk[0, 0], grad_x.reshape(x.shape), *[res[n][0] for n in order], *[res[n][1] for n in order],
            *[res[n][2] for n in order], *[res[n][3] for n in order])
```

```python
import math

import jax
import jax.numpy as jnp
from jax import lax
from jax.experimental import pallas as pl
from jax.experimental.pallas import tpu as pltpu

F32 = jnp.float32
BF16 = jnp.bfloat16
SDS = jax.ShapeDtypeStruct
MESH = pl.DeviceIdType.MESH

RMS_EPS = 1e-6
POOL_WINDOWS = (2, 4, 8, 16)
N_GROUPS = len(POOL_WINDOWS)
N_SPLITS = 6

ADAM_LR = 0.001
ADAM_B1 = 0.9
ADAM_B2 = 0.999
ADAM_EPS = 1e-08
ADAM_WD = 0.01
ADAM_STEP = 10

LANES = 128
SUBLANES_F32 = 8
SUBLANES_BF16 = 16
VMEM_BYTES = 64 * 1024 * 1024
VMEM_CAP = VMEM_BYTES - 8 * 1024 * 1024
VMEM_FLOOR = 16 * 1024 * 1024

ANY = pl.BlockSpec(memory_space=pl.ANY)


def _tile(dim, pref, align):
    if dim <= pref:
        return dim
    t = (pref // align) * align
    while t >= align:
        if dim % t == 0:
            return t
        t -= align
    return dim


def _nbytes(shape, dtype):
    n = 1
    for s in shape:
        n *= s
    return n * jnp.dtype(dtype).itemsize


def _params(sem, block_bytes, temp_bytes=0):
    need = 2 * block_bytes + temp_bytes + 4 * 1024 * 1024
    return pltpu.CompilerParams(dimension_semantics=sem, vmem_limit_bytes=int(min(max(need, VMEM_FLOOR), VMEM_CAP)))


def _in_hbm(args):
    return [pltpu.with_memory_space_constraint(a, pltpu.HBM) for a in args]


def _out_hbm(shapes):
    return [pltpu.HBM(o.shape, o.dtype) for o in shapes]


class _Comm:
    def __init__(self, ins, out_shapes, sems, start, finish, aliases=None):
        self.ins = list(ins)
        self.out_shapes = list(out_shapes)
        self.sems = list(sems)
        self.start = start
        self.finish = finish
        self.aliases = dict(aliases or {})


def _pcall(body, *, name, grid, in_specs, out_specs, out_shape, sem, blocks, temps=0, scratch_shapes=(),
           input_output_aliases=None, comm=None):
    in_specs = list(in_specs)
    out_specs = list(out_specs)
    out_shape = _out_hbm(out_shape)
    scratch_shapes = list(scratch_shapes)
    aliases = dict(input_output_aliases or {})
    n_in, n_out, n_scr = len(in_specs), len(out_shape), len(scratch_shapes)
    in_hbm = _in_hbm
    if comm is None:
        call = pl.pallas_call(
            body, name=name, grid=grid, in_specs=in_specs, out_specs=out_specs, out_shape=out_shape,
            scratch_shapes=scratch_shapes, input_output_aliases=aliases,
            compiler_params=_params(sem, blocks, temps))
        return lambda *args: (list(call(*in_hbm(args))), [])

    nci, nco = len(comm.ins), len(comm.out_shapes)

    def hosted(*refs):
        ins = refs[:n_in]
        cins = refs[n_in:n_in + nci]
        outs = refs[n_in + nci:n_in + nci + n_out]
        couts = refs[n_in + nci + n_out:n_in + nci + n_out + nco]
        scr = refs[n_in + nci + n_out + nco:n_in + nci + n_out + nco + n_scr]
        csems = refs[n_in + nci + n_out + nco + n_scr:]
        first = None
        last = None
        for q, g in enumerate(grid):
            pid = pl.program_id(q)
            first = (pid == 0) if first is None else first & (pid == 0)
            last = (pid == g - 1) if last is None else last & (pid == g - 1)

        @pl.when(first)
        def _():
            comm.start(cins, couts, csems)

        body(*ins, *outs, *scr)

        @pl.when(last)
        def _():
            comm.finish(cins, couts, csems)

    for i, o in comm.aliases.items():
        aliases[n_in + i] = n_out + o
    call = pl.pallas_call(
        hosted, name=name, grid=grid, in_specs=in_specs + [ANY] * nci, out_specs=out_specs + [ANY] * nco,
        out_shape=out_shape + _out_hbm(comm.out_shapes),
        scratch_shapes=scratch_shapes + comm.sems,
        input_output_aliases=aliases,
        compiler_params=_params(("arbitrary",) * len(grid), blocks, temps))

    def run(*args):
        res = call(*in_hbm(args), *in_hbm(comm.ins))
        return list(res[:n_out]), list(res[n_out:])

    return run


def _run_comm(comm, name):
    def body(*refs):
        nci, nco = len(comm.ins), len(comm.out_shapes)
        cins, couts, csems = refs[:nci], refs[nci:nci + nco], refs[nci + nco:]
        comm.start(cins, couts, csems)
        comm.finish(cins, couts, csems)

    return list(pl.pallas_call(
        body, name=name, in_specs=[ANY] * len(comm.ins), out_specs=[ANY] * len(comm.out_shapes),
        out_shape=_out_hbm(comm.out_shapes), scratch_shapes=comm.sems, input_output_aliases=comm.aliases,
    )(*_in_hbm(comm.ins)))


def _dot(a, b):
    return jnp.dot(a, b, preferred_element_type=F32)


def _dot_tb(a, b):
    return lax.dot_general(a, b, (((1,), (1,)), ((), ())), preferred_element_type=F32)


def _dot_ta(a, b):
    return lax.dot_general(a, b, (((0,), (0,)), ((), ())), preferred_element_type=F32)


def _rms_fwd(x):
    inv = lax.rsqrt(jnp.mean(x * x, axis=-1, keepdims=True) + RMS_EPS)
    return x * inv, inv


def _rms_bwd(dy, xhat, inv, g):
    gd = dy * g
    return inv * (gd - xhat * jnp.mean(gd * xhat, axis=-1, keepdims=True))


def _sigmoid(x):
    return 1.0 / (1.0 + jnp.exp(-x))


def _shift_down(x, k, row):
    return jnp.where(row >= k, pltpu.roll(x, k, 0), 0.0)


def _shift_up(x, k, row):
    s = x.shape[0]
    return jnp.where(row < s - k, pltpu.roll(x, s - k, 0), 0.0)


def _pool_fwd(u, win, row):
    s = u
    k = 1
    while k < win:
        s = s + _shift_down(s, k, row)
        k *= 2
    cnt = jnp.minimum(row + 1, win).astype(F32)
    return s / cnt - u


def _pool_bwd(dp, win, row):
    cnt = jnp.minimum(row + 1, win).astype(F32)
    s = dp / cnt
    k = 1
    while k < win:
        s = s + _shift_up(s, k, row)
        k *= 2
    return s - dp


def _acc_over(k, nk, part, acc, o_ref):
    @pl.when(k == 0)
    def _():
        acc[...] = part

    @pl.when(k > 0)
    def _():
        acc[...] += part

    @pl.when(k == nk - 1)
    def _():
        o_ref[...] = acc[...].astype(o_ref.dtype)


def _fwd_in(x, g, w, comm=None):
    t, d = x.shape
    nsh, _, ws = w.shape
    n = nsh * ws
    tm = _tile(t, 1024, SUBLANES_BF16)
    tn = _tile(ws, 1536, LANES)
    nps = ws // tn

    def body(x_ref, g_ref, w_ref, z_ref, h_ref, hs):
        @pl.when(pl.program_id(1) == 0)
        def _():
            xh, _ = _rms_fwd(x_ref[...])
            h = (xh * g_ref[...]).astype(BF16)
            hs[...] = h
            h_ref[...] = h

        z_ref[...] = _dot(hs[...], w_ref[...]).astype(BF16)

    blocks = _nbytes((tm, d), F32) + _nbytes((d, tn), BF16) + _nbytes((tm, tn), BF16) + _nbytes((tm, d), BF16)
    return _pcall(
        body, name="fwd_in", grid=(t // tm, n // tn),
        in_specs=[pl.BlockSpec((tm, d), lambda i, j: (i, 0)), pl.BlockSpec((1, d), lambda i, j: (0, 0)),
                  pl.BlockSpec((None, d, tn), lambda i, j: (j // nps, 0, j % nps))],
        out_specs=[pl.BlockSpec((tm, tn), lambda i, j: (i, j)), pl.BlockSpec((tm, d), lambda i, j: (i, 0))],
        out_shape=[SDS((t, n), BF16), SDS((t, d), BF16)],
        scratch_shapes=[pltpu.VMEM((tm, d), BF16)],
        sem=("parallel", "arbitrary"), blocks=blocks, temps=3 * _nbytes((tm, d), F32), comm=comm,
    )(x, g, w)


def _mixer_mid_fwd(z, pool_w, pool_scale, conv_w, nseq):
    t = z.shape[0]
    d = pool_scale.shape[1]
    s = t // nseq
    c = d // N_GROUPS

    def body(zp, zb, zc, zv, pw, ps, cw, o):
        j = pl.program_id(1)
        row = lax.broadcasted_iota(jnp.int32, (s, c), 0)
        for gi, win in enumerate(POOL_WINDOWS):
            @pl.when(j == gi)
            def _(win=win):
                pooled = _pool_fwd(zp[...].astype(F32), win, row)
                o[0] = (_dot(pooled.astype(BF16), pw[...]) * ps[...]).astype(BF16)

        cv = zc[...].astype(F32) * zv[...].astype(F32)
        cc = (cw[pl.ds(2, 1), :] * cv + cw[pl.ds(1, 1), :] * _shift_down(cv, 1, row)
              + cw[pl.ds(0, 1), :] * _shift_down(cv, 2, row))
        o[1] = (zb[...].astype(F32) * cc).astype(BF16)

    blocks = 4 * _nbytes((s, c), BF16) + _nbytes((c, c), BF16) + _nbytes((2, s, c), BF16)
    outs, _ = _pcall(
        body, name="mixer_mid_fwd", grid=(nseq, N_GROUPS),
        in_specs=[pl.BlockSpec((s, c), lambda b, j: (b, j)),
                  pl.BlockSpec((s, c), lambda b, j: (b, N_GROUPS + j)),
                  pl.BlockSpec((s, c), lambda b, j: (b, 2 * N_GROUPS + j)),
                  pl.BlockSpec((s, c), lambda b, j: (b, 3 * N_GROUPS + j)),
                  pl.BlockSpec((None, c, c), lambda b, j: (j, 0, 0)),
                  pl.BlockSpec((1, c), lambda b, j: (0, j)),
                  pl.BlockSpec((3, c), lambda b, j: (0, j))],
        out_specs=[pl.BlockSpec((2, s, c), lambda b, j: (0, b, j))],
        out_shape=[SDS((3, t, d), BF16)],
        sem=("parallel", "parallel"), blocks=blocks, temps=8 * _nbytes((s, c), F32),
    )(z, z, z, z, pool_w, pool_scale, conv_w)
    return outs[0]


def _mixer_out(lhs3, z, x, w3, g_ffn, comm=None):
    t, d = x.shape
    tm = _tile(t, 256, SUBLANES_BF16)

    def body(pq, zgp, zgc, x_ref, w_ref, g_ref, mrg, ypc, x1o, h2o):
        yp = _dot(pq[0], w_ref[0])
        yc = _dot(pq[1], w_ref[1])
        m = _sigmoid(zgp[...].astype(F32)) * yp + _sigmoid(zgc[...].astype(F32)) * yc
        mb = m.astype(BF16)
        x1 = x_ref[...] + _dot(mb, w_ref[2])
        ypc[0] = yp.astype(BF16)
        ypc[1] = yc.astype(BF16)
        mrg[...] = mb
        x1o[...] = x1
        xh, _ = _rms_fwd(x1)
        h2o[...] = (xh * g_ref[...]).astype(BF16)

    blocks = (_nbytes((2, tm, d), BF16) * 2 + _nbytes((tm, d), BF16) * 4 + _nbytes((tm, d), F32) * 2
              + _nbytes((3, d, d), BF16))
    return _pcall(
        body, name="mixer_out", grid=(t // tm,),
        in_specs=[pl.BlockSpec((2, tm, d), lambda i: (0, i, 0)),
                  pl.BlockSpec((tm, d), lambda i: (i, 4)),
                  pl.BlockSpec((tm, d), lambda i: (i, 5)),
                  pl.BlockSpec((tm, d), lambda i: (i, 0)),
                  pl.BlockSpec((3, d, d), lambda i: (0, 0, 0)),
                  pl.BlockSpec((1, d), lambda i: (0, 0))],
        out_specs=[pl.BlockSpec((None, tm, d), lambda i: (2, i, 0)),
                   pl.BlockSpec((2, tm, d), lambda i: (0, i, 0)),
                   pl.BlockSpec((tm, d), lambda i: (i, 0)),
                   pl.BlockSpec((tm, d), lambda i: (i, 0))],
        out_shape=[SDS(lhs3.shape, BF16), SDS((2, t, d), BF16), SDS((t, d), F32), SDS((t, d), BF16)],
        input_output_aliases={0: 0},
        sem=("parallel",), blocks=blocks, temps=8 * _nbytes((tm, d), F32), comm=comm,
    )(lhs3, z, z, x, w3, g_ffn)


def _ffn_up(h2, w_up, f, comm=None):
    t, d = h2.shape
    _, _, ws = w_up.shape
    tm = _tile(t, 1024, SUBLANES_BF16)
    tn = _tile(ws, 1408, LANES)
    nps = ws // tn
    npp = f // tn

    def body(h_ref, w_ref, o_ref):
        o_ref[...] = _dot(h_ref[...], w_ref[...]).astype(BF16)

    blocks = _nbytes((tm, d), BF16) + _nbytes((d, tn), BF16) + _nbytes((tm, tn), BF16)
    return _pcall(
        body, name="ffn_up", grid=(t // tm, 2 * npp),
        in_specs=[pl.BlockSpec((tm, d), lambda i, j: (i, 0)),
                  pl.BlockSpec((None, d, tn), lambda i, j: (j // nps, 0, j % nps))],
        out_specs=[pl.BlockSpec((None, tm, tn), lambda i, j: (j // npp, i, j % npp))],
        out_shape=[SDS((2, t, f), BF16)],
        sem=("parallel", "parallel"), blocks=blocks, temps=_nbytes((tm, tn), F32), comm=comm,
    )(h2, w_up)


def _conv3_rows(u, u1, u2, w_ref, p):
    return w_ref[p, pl.ds(2, 1), :] * u + w_ref[p, pl.ds(1, 1), :] * u1 + w_ref[p, pl.ds(0, 1), :] * u2


CHUNK = 64
HALO = SUBLANES_F32


def _up1_up2(u, nxt):
    rows = u.shape[0]
    ext = jnp.concatenate([u, nxt], axis=0)
    n = rows + HALO
    return pltpu.roll(ext, n - 1, 0)[:rows], pltpu.roll(ext, n - 2, 0)[:rows]


def _fold8(x):
    return jnp.sum(x.reshape(x.shape[0] // SUBLANES_F32, SUBLANES_F32, x.shape[1]), axis=0)


def _ffn_mid_fwd(u0, cw, cb, nseq):
    _, t, f = u0.shape
    s = t // nseq
    c = _tile(f, 256, LANES)

    def body(u_ref, w_ref, b_ref, a_ref, uo_ref):
        row = lax.broadcasted_iota(jnp.int32, (s, c), 0)
        act = []
        for p in range(2):
            u = u_ref[p].astype(F32)
            act.append(_conv3_rows(u, _shift_down(u, 1, row), _shift_down(u, 2, row), w_ref, p) + b_ref[p])
            uo_ref[p] = act[p].astype(BF16)
        ug, uv = act
        a_ref[...] = (ug * _sigmoid(ug) * uv).astype(BF16)

    blocks = 2 * _nbytes((2, s, c), BF16) + _nbytes((s, c), BF16)
    outs, _ = _pcall(
        body, name="ffn_mid_fwd", grid=(f // c, nseq),
        in_specs=[pl.BlockSpec((2, s, c), lambda j, b: (0, b, j)),
                  pl.BlockSpec((2, 3, c), lambda j, b: (0, 0, j)),
                  pl.BlockSpec((2, 1, c), lambda j, b: (0, 0, j))],
        out_specs=[pl.BlockSpec((s, c), lambda j, b: (b, j)),
                   pl.BlockSpec((2, s, c), lambda j, b: (0, b, j))],
        out_shape=[SDS((t, f), BF16), SDS((2, t, f), BF16)],
        sem=("parallel", "parallel"), blocks=blocks, temps=8 * _nbytes((s, c), F32),
    )(u0, cw, cb)
    return outs


def _ffn_down_loss(a, w_down, x1, tgt, g_fin):
    t, f = a.shape
    d = x1.shape[1]
    tm = _tile(t, 256, SUBLANES_BF16)
    nsteps = t // tm

    def body(a_ref, w_ref, x1_ref, t_ref, g_ref, dx_ref, dxb_ref, loss_ref, gg_ref, lacc):
        i = pl.program_id(0)

        @pl.when(i == 0)
        def _():
            lacc[...] = jnp.zeros_like(lacc)
            gg_ref[...] = jnp.zeros_like(gg_ref)

        x2 = x1_ref[...] + _dot(a_ref[...], w_ref[...])
        xh, inv = _rms_fwd(x2)
        g = g_ref[...]
        e = xh * g - t_ref[...]
        lacc[...] += jnp.sum(e * e, axis=0, keepdims=True)
        dy = e * (1.0 / d)
        gg_ref[...] += jnp.sum(dy * xh, axis=0, keepdims=True)
        dx2 = _rms_bwd(dy, xh, inv, g)
        dx_ref[...] = dx2
        dxb_ref[...] = dx2.astype(BF16)

        @pl.when(i == nsteps - 1)
        def _():
            loss_ref[...] = jnp.sum(lacc[...], axis=1, keepdims=True) * (0.5 / d)

    blocks = (_nbytes((tm, f), BF16) + _nbytes((f, d), BF16) + 3 * _nbytes((tm, d), F32) + _nbytes((tm, d), BF16))
    outs, _ = _pcall(
        body, name="ffn_down_loss", grid=(nsteps,),
        in_specs=[pl.BlockSpec((tm, f), lambda i: (i, 0)), pl.BlockSpec((f, d), lambda i: (0, 0)),
                  pl.BlockSpec((tm, d), lambda i: (i, 0)), pl.BlockSpec((tm, d), lambda i: (i, 0)),
                  pl.BlockSpec((1, d), lambda i: (0, 0))],
        out_specs=[pl.BlockSpec((tm, d), lambda i: (i, 0)), pl.BlockSpec((tm, d), lambda i: (i, 0)),
                   pl.BlockSpec((1, 1), lambda i: (0, 0)), pl.BlockSpec((1, d), lambda i: (0, 0))],
        out_shape=[SDS((t, d), F32), SDS((t, d), BF16), SDS((1, 1), F32), SDS((1, d), F32)],
        scratch_shapes=[pltpu.VMEM((1, d), F32)],
        sem=("arbitrary",), blocks=blocks, temps=8 * _nbytes((tm, d), F32),
    )(a, w_down, x1, tgt, g_fin)
    return outs


def _ffn_bwd_da(dxb, w_down, comm=None):
    t, d = dxb.shape
    f = w_down.shape[0]
    tm = _tile(t, 1024, SUBLANES_BF16)
    tn = _tile(f, 1408, LANES)

    def body(x_ref, w_ref, o_ref):
        o_ref[...] = _dot_tb(x_ref[...], w_ref[...]).astype(BF16)

    blocks = _nbytes((tm, d), BF16) + _nbytes((tn, d), BF16) + _nbytes((tm, tn), BF16)
    return _pcall(
        body, name="ffn_bwd_da", grid=(t // tm, f // tn),
        in_specs=[pl.BlockSpec((tm, d), lambda i, j: (i, 0)), pl.BlockSpec((tn, d), lambda i, j: (j, 0))],
        out_specs=[pl.BlockSpec((tm, tn), lambda i, j: (i, j))],
        out_shape=[SDS((t, f), BF16)],
        sem=("parallel", "parallel"), blocks=blocks, temps=_nbytes((tm, tn), F32), comm=comm,
    )(dxb, w_down)


def _ffn_mid_bwd(da, u0, ua, cw, nseq, comm=None):
    _, t, f = u0.shape
    s = t // nseq
    c = _tile(f, 128, LANES)
    r = _tile(s, CHUNK, SUBLANES_BF16)
    n = s // r

    def body(da_ref, u_ref, ua_ref, w_ref, du_ref, gw_ref, gb_ref):
        @pl.when(pl.program_id(1) == 0)
        def _():
            gw_ref[...] = jnp.zeros_like(gw_ref)
            gb_ref[...] = jnp.zeros_like(gb_ref)

        def step(i, carry):
            nxt, sums = carry
            rows = pl.ds(pl.multiple_of((n - 1 - i) * r, r), r)
            ug = ua_ref[0, rows, :].astype(F32)
            uv = ua_ref[1, rows, :].astype(F32)
            sg = _sigmoid(ug)
            dacc = da_ref[rows, :].astype(F32)
            dus = (dacc * uv * sg * (1.0 + ug * (1.0 - sg)), dacc * (ug * sg))
            first, new_sums = [], []
            for p in range(2):
                du = dus[p]
                d1, d2 = _up1_up2(du, nxt[p])
                du_ref[p, rows, :] = _conv3_rows(du, d1, d2, w_ref, p).astype(BF16)
                u = u_ref[p, rows, :].astype(F32)
                sb, s0, s1, s2 = sums[p]
                new_sums.append((sb + _fold8(du), s0 + _fold8(d2 * u), s1 + _fold8(d1 * u), s2 + _fold8(du * u)))
                first.append(du[:HALO])
            return tuple(first), tuple(new_sums)

        zero = jnp.zeros((HALO, c), F32)
        _, sums = lax.fori_loop(0, n, step, ((zero, zero), ((zero,) * 4,) * 2))
        for p in range(2):
            sb, s0, s1, s2 = sums[p]
            gb_ref[p] += jnp.sum(sb, axis=0, keepdims=True)
            gw_ref[p, pl.ds(0, 1), :] += jnp.sum(s0, axis=0, keepdims=True)
            gw_ref[p, pl.ds(1, 1), :] += jnp.sum(s1, axis=0, keepdims=True)
            gw_ref[p, pl.ds(2, 1), :] += jnp.sum(s2, axis=0, keepdims=True)

    blocks = _nbytes((s, c), BF16) + 3 * _nbytes((2, s, c), BF16)
    return _pcall(
        body, name="ffn_mid_bwd", grid=(f // c, nseq),
        in_specs=[pl.BlockSpec((s, c), lambda j, b: (b, j)),
                  pl.BlockSpec((2, s, c), lambda j, b: (0, b, j)),
                  pl.BlockSpec((2, s, c), lambda j, b: (0, b, j)),
                  pl.BlockSpec((2, 3, c), lambda j, b: (0, 0, j))],
        out_specs=[pl.BlockSpec((2, s, c), lambda j, b: (0, b, j)),
                   pl.BlockSpec((2, 3, c), lambda j, b: (0, 0, j)),
                   pl.BlockSpec((2, 1, c), lambda j, b: (0, 0, j))],
        out_shape=[SDS((2, t, f), BF16), SDS((2, 3, f), F32), SDS((2, 1, f), F32)],
        sem=("parallel", "arbitrary"), blocks=blocks, temps=4 * 1024 * 1024, comm=comm,
    )(da, u0, ua, cw)


def _wgrad(a, b, name, *, tr, tn, b_plane_of=None, out_shards=None, comm=None):
    t, m = a.shape
    n_total = b.shape[-1] * (b.shape[0] if b.ndim == 3 else 1)
    tk = _tile(t, 1024, SUBLANES_BF16)
    nk = t // tk

    def body(a_ref, b_ref, o_ref, acc):
        _acc_over(pl.program_id(2), nk, _dot_ta(a_ref[...], b_ref[...]), acc, o_ref)

    if b.ndim == 3:
        b_spec = pl.BlockSpec((None, tk, tn), lambda r, n, k: (b_plane_of(n)[0], k, b_plane_of(n)[1]))
    else:
        b_spec = pl.BlockSpec((tk, tn), lambda r, n, k: (k, n))
    if out_shards is None:
        o_spec = pl.BlockSpec((tr, tn), lambda r, n, k: (r, n))
        o_shape = SDS((m, n_total), BF16)
    else:
        nps = n_total // out_shards // tn
        o_spec = pl.BlockSpec((None, tr, tn), lambda r, n, k: (n // nps, r, n % nps))
        o_shape = SDS((out_shards, m, n_total // out_shards), BF16)
    blocks = _nbytes((tk, tr), BF16) + _nbytes((tk, tn), BF16) + _nbytes((tr, tn), BF16)
    return _pcall(
        body, name=name, grid=(m // tr, n_total // tn, nk),
        in_specs=[pl.BlockSpec((tk, tr), lambda r, n, k: (k, r)), b_spec],
        out_specs=[o_spec], out_shape=[o_shape],
        scratch_shapes=[pltpu.VMEM((tr, tn), F32)],
        sem=("parallel", "parallel", "arbitrary"), blocks=blocks, temps=2 * _nbytes((tr, tn), F32), comm=comm,
    )(a, b)


def _wgrad3(lhs3, rhs3, comm=None):
    nw, t, d = lhs3.shape
    tk = _tile(t, 1024, SUBLANES_BF16)
    nk = t // tk

    def body(a_ref, b_ref, o_ref, acc):
        _acc_over(pl.program_id(1), nk, _dot_ta(a_ref[...], b_ref[...]), acc, o_ref)

    blocks = 2 * _nbytes((tk, d), BF16) + _nbytes((d, d), BF16)
    return _pcall(
        body, name="wgrad_sq3", grid=(nw, nk),
        in_specs=[pl.BlockSpec((None, tk, d), lambda w, k: (w, k, 0)),
                  pl.BlockSpec((None, tk, d), lambda w, k: (w, k, 0))],
        out_specs=[pl.BlockSpec((None, d, d), lambda w, k: (w, 0, 0))],
        out_shape=[SDS((nw, d, d), BF16)],
        scratch_shapes=[pltpu.VMEM((d, d), F32)],
        sem=("parallel", "arbitrary"), blocks=blocks, temps=2 * _nbytes((d, d), F32), comm=comm,
    )(lhs3, rhs3)


def _ffn_bwd_dx1(du0, w_up, x1, dx2, g_ffn, n_planes_out, comm=None):
    _, t, f = du0.shape
    d = x1.shape[1]
    nsh, _, ws = w_up.shape
    tm = _tile(t, 256, SUBLANES_BF16)
    spp = f // ws

    def body(du_ref, w_ref, x1_ref, dx2_ref, g_ref, dx1_ref, dxb_ref, gg_ref):
        @pl.when(pl.program_id(0) == 0)
        def _():
            gg_ref[...] = jnp.zeros_like(gg_ref)

        dh = None
        for k in range(nsh):
            part = _dot_tb(du_ref[k // spp, :, (k % spp) * ws:(k % spp + 1) * ws], w_ref[k])
            dh = part if dh is None else dh + part
        xh, inv = _rms_fwd(x1_ref[...])
        gg_ref[...] += jnp.sum(dh * xh, axis=0, keepdims=True)
        dx1 = dx2_ref[...] + _rms_bwd(dh, xh, inv, g_ref[...])
        dx1_ref[...] = dx1
        dxb_ref[...] = dx1.astype(BF16)

    blocks = _nbytes((2, tm, f), BF16) + 3 * _nbytes((tm, d), F32) + _nbytes((tm, d), BF16)
    return _pcall(
        body, name="ffn_bwd_dx1", grid=(t // tm,),
        in_specs=[pl.BlockSpec((2, tm, f), lambda i: (0, i, 0)),
                  pl.BlockSpec((nsh, d, ws), lambda i: (0, 0, 0), pipeline_mode=pl.Buffered(1)),
                  pl.BlockSpec((tm, d), lambda i: (i, 0)),
                  pl.BlockSpec((tm, d), lambda i: (i, 0)),
                  pl.BlockSpec((1, d), lambda i: (0, 0))],
        out_specs=[pl.BlockSpec((tm, d), lambda i: (i, 0)),
                   pl.BlockSpec((None, tm, d), lambda i: (n_planes_out - 1, i, 0)),
                   pl.BlockSpec((1, d), lambda i: (0, 0))],
        out_shape=[SDS((t, d), F32), SDS((n_planes_out, t, d), BF16), SDS((1, d), F32)],
        sem=("arbitrary",), blocks=blocks, temps=_nbytes(w_up.shape, BF16) + 8 * _nbytes((tm, d), F32), comm=comm,
    )(du0, w_up, x1, dx2, g_ffn)


def _mixer_bwd(rhs3, z, ypc, w3, comm=None):
    _, t, d = rhs3.shape
    tm = _tile(t, 256, SUBLANES_BF16)

    def body(dx_ref, zgp, zgc, ypc_ref, w_ref, dyo, dzo, dpq):
        dm = _dot_tb(dx_ref[...], w_ref[2])
        sp = _sigmoid(zgp[...].astype(F32))
        sc = _sigmoid(zgc[...].astype(F32))
        dyp = (dm * sp).astype(BF16)
        dyc = (dm * sc).astype(BF16)
        dzo[0] = (dm * ypc_ref[0].astype(F32) * sp * (1.0 - sp)).astype(BF16)
        dzo[1] = (dm * ypc_ref[1].astype(F32) * sc * (1.0 - sc)).astype(BF16)
        dyo[0] = dyp
        dyo[1] = dyc
        dpq[0] = _dot_tb(dyp, w_ref[0]).astype(BF16)
        dpq[1] = _dot_tb(dyc, w_ref[1]).astype(BF16)

    blocks = _nbytes((tm, d), BF16) * 3 + _nbytes((2, tm, d), BF16) * 4 + _nbytes((3, d, d), BF16)
    return _pcall(
        body, name="mixer_bwd", grid=(t // tm,),
        in_specs=[pl.BlockSpec((None, tm, d), lambda i: (2, i, 0)),
                  pl.BlockSpec((tm, d), lambda i: (i, 4)),
                  pl.BlockSpec((tm, d), lambda i: (i, 5)),
                  pl.BlockSpec((2, tm, d), lambda i: (0, i, 0)),
                  pl.BlockSpec((3, d, d), lambda i: (0, 0, 0))],
        out_specs=[pl.BlockSpec((2, tm, d), lambda i: (0, i, 0)),
                   pl.BlockSpec((2, tm, d), lambda i: (2, i, 0)),
                   pl.BlockSpec((2, tm, d), lambda i: (0, i, 0))],
        out_shape=[SDS(rhs3.shape, BF16), SDS((N_SPLITS, t, d), BF16), SDS((2, t, d), BF16)],
        input_output_aliases={0: 0},
        sem=("parallel",), blocks=blocks, temps=8 * _nbytes((tm, d), F32), comm=comm,
    )(rhs3, z, z, ypc, w3)


def _conv_bwd(dz, dpq, z, conv_w, nseq, comm=None):
    _, t, d = dz.shape
    s = t // nseq
    c = _tile(d, 128, LANES)
    nb = d // c

    def body(dz_in, dq_ref, zb, zc, zv, cw, dzo, gw_ref):
        del dz_in

        @pl.when(pl.program_id(1) == 0)
        def _():
            gw_ref[...] = jnp.zeros_like(gw_ref)

        row = lax.broadcasted_iota(jnp.int32, (s, c), 0)
        b = zb[...].astype(F32)
        cm = zc[...].astype(F32)
        v = zv[...].astype(F32)
        cv = cm * v
        cv1 = _shift_down(cv, 1, row)
        cv2 = _shift_down(cv, 2, row)
        w0, w1, w2 = cw[pl.ds(0, 1), :], cw[pl.ds(1, 1), :], cw[pl.ds(2, 1), :]
        cc = w2 * cv + w1 * cv1 + w0 * cv2
        dq = dq_ref[...].astype(F32)
        dzo[0] = (dq * cc).astype(BF16)
        dcc = dq * b
        gw_ref[pl.ds(0, 1), :] += jnp.sum(dcc * cv2, axis=0, keepdims=True)
        gw_ref[pl.ds(1, 1), :] += jnp.sum(dcc * cv1, axis=0, keepdims=True)
        gw_ref[pl.ds(2, 1), :] += jnp.sum(dcc * cv, axis=0, keepdims=True)
        dcv = w2 * dcc + w1 * _shift_up(dcc, 1, row) + w0 * _shift_up(dcc, 2, row)
        dzo[1] = (dcv * v).astype(BF16)
        dzo[2] = (dcv * cm).astype(BF16)

    blocks = 4 * _nbytes((s, c), BF16) + _nbytes((3, s, c), BF16)
    return _pcall(
        body, name="conv_bwd", grid=(nb, nseq),
        in_specs=[ANY,
                  pl.BlockSpec((None, s, c), lambda j, b: (1, b, j)),
                  pl.BlockSpec((s, c), lambda j, b: (b, nb + j)),
                  pl.BlockSpec((s, c), lambda j, b: (b, 2 * nb + j)),
                  pl.BlockSpec((s, c), lambda j, b: (b, 3 * nb + j)),
                  pl.BlockSpec((3, c), lambda j, b: (0, j))],
        out_specs=[pl.BlockSpec((3, s, c), lambda j, b: (0, b, j)),
                   pl.BlockSpec((3, c), lambda j, b: (0, j))],
        out_shape=[SDS(dz.shape, BF16), SDS((3, d), F32)],
        input_output_aliases={0: 0},
        sem=("parallel", "arbitrary"), blocks=blocks, temps=16 * _nbytes((s, c), F32), comm=comm,
    )(dz, dpq, z, z, z, conv_w)


def _pool_bwd_call(dz, dpq, z, pool_w, pool_scale, nseq, comm=None):
    _, t, d = dz.shape
    s = t // nseq
    c = d // N_GROUPS

    def body(dz_in, dp_ref, zp, pw, ps, dzo, gpw_ref, gps_ref):
        del dz_in
        j = pl.program_id(0)

        @pl.when(pl.program_id(1) == 0)
        def _():
            gpw_ref[...] = jnp.zeros_like(gpw_ref)
            gps_ref[...] = jnp.zeros_like(gps_ref)

        row = lax.broadcasted_iota(jnp.int32, (s, c), 0)
        for gi, win in enumerate(POOL_WINDOWS):
            @pl.when(j == gi)
            def _(win=win):
                pb = _pool_fwd(zp[...].astype(F32), win, row).astype(BF16)
                plin = _dot(pb, pw[...])
                dps = dp_ref[...].astype(F32)
                gps_ref[...] += jnp.sum(dps * plin, axis=0, keepdims=True)
                dplb = (dps * ps[...]).astype(BF16)
                gpw_ref[...] += _dot_ta(pb, dplb)
                dzo[...] = _pool_bwd(_dot_tb(dplb, pw[...]), win, row).astype(BF16)

    blocks = 3 * _nbytes((s, c), BF16) + _nbytes((c, c), BF16) + _nbytes((c, c), F32)
    return _pcall(
        body, name="pool_bwd", grid=(N_GROUPS, nseq),
        in_specs=[ANY,
                  pl.BlockSpec((None, s, c), lambda j, b: (0, b, j)),
                  pl.BlockSpec((s, c), lambda j, b: (b, j)),
                  pl.BlockSpec((None, c, c), lambda j, b: (j, 0, 0)),
                  pl.BlockSpec((1, c), lambda j, b: (0, j))],
        out_specs=[pl.BlockSpec((None, s, c), lambda j, b: (3, b, j)),
                   pl.BlockSpec((None, c, c), lambda j, b: (j, 0, 0)),
                   pl.BlockSpec((1, c), lambda j, b: (0, j))],
        out_shape=[SDS(dz.shape, BF16), SDS((N_GROUPS, c, c), F32), SDS((1, d), F32)],
        input_output_aliases={0: 0},
        sem=("parallel", "arbitrary"), blocks=blocks, temps=10 * _nbytes((s, c), F32), comm=comm,
    )(dz, dpq, z, pool_w, pool_scale)


def _dz_plane(zb):
    return jnp.where(zb < 4, (zb + 3) % 4, zb)


def _wgrad_in(h1, dz, nsh, comm=None):
    t, d = h1.shape
    ws = N_SPLITS * d // nsh
    kb = _tile(math.gcd(d, ws), 512, LANES)
    npl = d // kb
    nps = ws // kb
    tk = _tile(t, 1024, SUBLANES_BF16)
    nk = t // tk

    def body(a_ref, b_ref, o_ref, acc):
        _acc_over(pl.program_id(1), nk, _dot_ta(a_ref[...], b_ref[...]), acc, o_ref)

    blocks = _nbytes((tk, d), BF16) + _nbytes((tk, kb), BF16) + _nbytes((d, kb), BF16)
    return _pcall(
        body, name="wgrad_in", grid=(N_SPLITS * npl, nk),
        in_specs=[pl.BlockSpec((tk, d), lambda cb, k: (k, 0)),
                  pl.BlockSpec((None, tk, kb), lambda cb, k: (_dz_plane(cb // npl), k, cb % npl))],
        out_specs=[pl.BlockSpec((None, d, kb), lambda cb, k: (cb // nps, 0, cb % nps))],
        out_shape=[SDS((nsh, d, ws), BF16)],
        scratch_shapes=[pltpu.VMEM((d, kb), F32)],
        sem=("parallel", "arbitrary"), blocks=blocks, temps=2 * _nbytes((d, kb), F32), comm=comm,
    )(h1, dz)


def _mixer_bwd_dx(dz, w_in, x, dx1, g_mix, comm=None):
    npln, t, d = dz.shape
    nsh, _, ws = w_in.shape
    tm = _tile(t, 256, SUBLANES_BF16)
    kb = _tile(math.gcd(d, ws), 512, LANES)
    npl = d // kb
    nps = ws // kb

    def body(dz_ref, w_ref, x_ref, dx1_ref, g_ref, dx_ref, gg_ref):
        @pl.when(pl.program_id(0) == 0)
        def _():
            gg_ref[...] = jnp.zeros_like(gg_ref)

        dh = None
        for cb in range(npln * npl):
            zb = cb // npl
            plane = (zb + 3) % 4 if zb < 4 else zb
            part = _dot_tb(dz_ref[plane, :, (cb % npl) * kb:(cb % npl + 1) * kb],
                           w_ref[cb // nps, :, (cb % nps) * kb:(cb % nps + 1) * kb])
            dh = part if dh is None else dh + part
        xh, inv = _rms_fwd(x_ref[...])
        gg_ref[...] += jnp.sum(dh * xh, axis=0, keepdims=True)
        dx_ref[...] = dx1_ref[...] + _rms_bwd(dh, xh, inv, g_ref[...])

    blocks = _nbytes((npln, tm, d), BF16) + 3 * _nbytes((tm, d), F32)
    return _pcall(
        body, name="mixer_bwd_dx", grid=(t // tm,),
        in_specs=[pl.BlockSpec((npln, tm, d), lambda i: (0, i, 0)),
                  pl.BlockSpec((nsh, d, ws), lambda i: (0, 0, 0), pipeline_mode=pl.Buffered(1)),
                  pl.BlockSpec((tm, d), lambda i: (i, 0)),
                  pl.BlockSpec((tm, d), lambda i: (i, 0)),
                  pl.BlockSpec((1, d), lambda i: (0, 0))],
        out_specs=[pl.BlockSpec((tm, d), lambda i: (i, 0)),
                   pl.BlockSpec((1, d), lambda i: (0, 0))],
        out_shape=[SDS((t, d), F32), SDS((1, d), F32)],
        sem=("arbitrary",), blocks=blocks, temps=_nbytes(w_in.shape, BF16) + 8 * _nbytes((tm, d), F32), comm=comm,
    )(dz, w_in, x, dx1, g_mix)


N_BIG = 5
SHARD_MAJOR = (0, 2)
ROWS_DIM1 = (1, 4)


def _ds(start, size, align):
    if isinstance(start, int):
        return pl.ds(start, size)
    return pl.ds(pl.multiple_of(start, align), size)


def _piece(a, ref, k, h):
    if a in SHARD_MAJOR:
        r = ref.shape[1] // 2
        return ref.at[k, _ds(h * r, r, SUBLANES_BF16), :]
    if a in ROWS_DIM1:
        r = ref.shape[1] // 8
        return ref.at[:, _ds((2 * k + h) * r, r, SUBLANES_BF16), :]
    r = ref.shape[0] // 8
    return ref.at[_ds((2 * k + h) * r, r, SUBLANES_BF16), :]


def _half(a, ref, h):
    if a in ROWS_DIM1:
        r = ref.shape[1] // 2
        return ref.at[:, _ds(h * r, r, SUBLANES_BF16), :]
    r = ref.shape[0] // 2
    return ref.at[_ds(h * r, r, SUBLANES_BF16), :]


def _piece_shape(a, full_shape):
    if a in SHARD_MAJOR:
        return (full_shape[1] // 2, full_shape[2])
    if a in ROWS_DIM1:
        return (full_shape[0], full_shape[1] // 8, full_shape[2])
    return (full_shape[0] // 8, full_shape[1])


def _shard_shape(a, full_shape):
    if a in SHARD_MAJOR:
        return (full_shape[1], full_shape[2])
    if a in ROWS_DIM1:
        return (full_shape[0], full_shape[1] // 4, full_shape[2])
    return (full_shape[0] // 4, full_shape[1])


def _rows_axis(a):
    return 1 if a in ROWS_DIM1 else 0


def _piece_block(a, full_shape):
    ps = _piece_shape(a, full_shape)
    if a in SHARD_MAJOR:
        return (None,) + ps, lambda k, c: (k, c, 0)
    if a in ROWS_DIM1:
        return ps, lambda k, c: (0, 2 * k + c, 0)
    return ps, lambda k, c: (2 * k + c, 0)


def _coords():
    return lax.axis_index("x"), lax.axis_index("y"), lax.axis_index("c")


def _peer_chips(x, y):
    return [(1 - x, y), (x, 1 - y), (1 - x, 1 - y)]


def _remote(src, dst, ssem, rsem, dev):
    return pltpu.make_async_remote_copy(src_ref=src, dst_ref=dst, send_sem=ssem, recv_sem=rsem,
                                        device_id=dev, device_id_type=MESH)


def _dma_sems(*counts):
    return [pltpu.SemaphoreType.DMA((n,)) for n in counts]


def _symmetric(ins, out_shapes, sems, copies, aliases=None):
    def start(cins, couts, csems):
        for cp in copies(cins, couts, csems):
            cp.start()

    def finish(cins, couts, csems):
        for cp in copies(cins, couts, csems):
            cp.wait()

    return _Comm(ins, out_shapes, sems, start, finish, aliases)


def _rows_part(a, ref, part):
    if part is None:
        return ref
    p, n = part
    ax = _rows_axis(a)
    r = ref.shape[ax] // n
    return ref.at[tuple(pl.ds(p * r, r) if q == ax else slice(None) for q in range(len(ref.shape)))]


def _merge(comms):
    ins, outs, sems, aliases, spans = [], [], [], {}, []
    for cm in comms:
        spans.append((len(ins), len(outs), len(sems)))
        for i, o in cm.aliases.items():
            aliases[len(ins) + i] = len(outs) + o
        ins += cm.ins
        outs += cm.out_shapes
        sems += cm.sems

    def each(fn_name):
        def run(cins, couts, csems):
            for cm, (i0, o0, s0) in zip(comms, spans):
                getattr(cm, fn_name)(cins[i0:i0 + len(cm.ins)], couts[o0:o0 + len(cm.out_shapes)],
                                     csems[s0:s0 + len(cm.sems)])
        return run

    return _Comm(ins, outs, sems, each("start"), each("finish"), aliases)


def _gather_comm(arrs, locs, full_shapes, part=None, into=None):
    n = len(arrs)

    def own(cins, couts, csems):
        x, y, c = _coords()
        j = 2 * x + y
        return [_remote(_rows_part(a, _half(a, cins[q], h), part), _rows_part(a, _piece(a, couts[q], j, h), part),
                        csems[0].at[2 * q + h], csems[1].at[2 * q + h], (x, y, 1 - c))
                for q, a in enumerate(arrs) for h in range(2)]

    def sends(cins, couts, csems):
        x, y, c = _coords()
        j = 2 * x + y
        return [_remote(_rows_part(a, _half(a, cins[q], c), part), _rows_part(a, _piece(a, couts[q], j, c), part),
                        csems[2].at[3 * q + i], csems[3].at[3 * q + i], (px, py, c))
                for q, a in enumerate(arrs) for i, (px, py) in enumerate(_peer_chips(x, y))]

    def forwards(couts, csems, half_of):
        x, y, c = _coords()
        out = []
        for q, a in enumerate(arrs):
            for i, (px, py) in enumerate(_peer_chips(x, y)):
                landed = _rows_part(a, _piece(a, couts[q], 2 * px + py, half_of(c)), part)
                out.append(_remote(landed, landed, csems[4].at[3 * q + i], csems[5].at[3 * q + i], (x, y, 1 - c)))
        return out

    def start(cins, couts, csems):
        for cp in sends(cins, couts, csems) + own(cins, couts, csems):
            cp.start()

    def finish(cins, couts, csems):
        fw = forwards(couts, csems, lambda c: c)
        for cp, f in zip(sends(cins, couts, csems), fw):
            cp.wait_recv()
            f.start()
        for f in forwards(couts, csems, lambda c: 1 - c):
            f.wait_recv()
        for cp in sends(cins, couts, csems) + fw:
            cp.wait_send()
        for cp in own(cins, couts, csems):
            cp.wait()

    ins = [locs[a] for a in arrs] + ([into[a] for a in arrs] if into else [])
    return _Comm(ins, [SDS(full_shapes[a], BF16) for a in arrs],
                 _dma_sems(2 * n, 2 * n, 3 * n, 3 * n, 3 * n, 3 * n), start, finish,
                 aliases={n + q: q for q in range(n)} if into else None)


def _halves_comm(arrs, gbs):
    n = len(arrs)

    def copies(cins, couts, csems):
        x, y, c = _coords()
        return [_remote(_piece(a, cins[q], k, 1 - c), couts[q].at[k], csems[0].at[4 * q + k], csems[1].at[4 * q + k],
                        (x, y, 1 - c)) for q, a in enumerate(arrs) for k in range(4)]

    return _symmetric([gbs[a] for a in arrs], [SDS((4,) + _piece_shape(a, gbs[a].shape), BF16) for a in arrs],
                      _dma_sems(4 * n, 4 * n), copies)


def _chips_comm(arrs, ps, part=None, into=None):
    n = len(arrs)

    def copies(cins, couts, csems):
        x, y, c = _coords()
        return [_remote(_rows_part(a, cins[q].at[2 * px + py], part), _rows_part(a, couts[q].at[i], part),
                        csems[0].at[3 * q + i], csems[1].at[3 * q + i], (px, py, c))
                for q, a in enumerate(arrs) for i, (px, py) in enumerate(_peer_chips(x, y))]

    ins = [ps[a] for a in arrs] + ([into[a] for a in arrs] if into else [])
    return _symmetric(ins, [SDS((3,) + ps[a].shape[1:], BF16) for a in arrs], _dma_sems(3 * n, 3 * n), copies,
                      aliases={n + q: q for q in range(n)} if into else None)


def _result_comm(arrs, gs):
    n = len(arrs)

    def copies(cins, couts, csems):
        x, y, c = _coords()
        return [_remote(_half(a, cins[q], c), _half(a, couts[q], c), csems[0].at[q], csems[1].at[q], (x, y, 1 - c))
                for q, a in enumerate(arrs)]

    return _symmetric([gs[a] for a in arrs], [SDS(gs[a].shape, F32) for a in arrs], _dma_sems(n, n), copies,
                      aliases={q: q for q in range(n)})


def _add_halves(arrs, gbs, lands, c_arr, name):
    n = len(arrs)

    def body(c_ref, *refs):
        del c_ref
        for q in range(n):
            refs[2 * n + q][...] = (refs[q][...].astype(F32) + refs[n + q][...].astype(F32)).astype(BF16)

    g_specs, l_specs, o_specs, blocks = [], [], [], 0
    for a in arrs:
        bs, imap = _piece_block(a, gbs[a].shape)
        ps = _piece_shape(a, gbs[a].shape)
        g_specs.append(pl.BlockSpec(bs, lambda k, c_ref, imap=imap: imap(k, c_ref[0])))
        nd = len(ps)
        l_specs.append(pl.BlockSpec((None,) + ps, lambda k, c_ref, nd=nd: (k,) + (0,) * nd))
        o_specs.append(pl.BlockSpec((None,) + ps, lambda k, c_ref, nd=nd: (k,) + (0,) * nd))
        blocks += 3 * _nbytes(ps, BF16)
    return list(pl.pallas_call(
        body, name=name,
        grid_spec=pltpu.PrefetchScalarGridSpec(
            num_scalar_prefetch=1, grid=(4,), in_specs=g_specs + l_specs, out_specs=o_specs),
        out_shape=_out_hbm([SDS((4,) + _piece_shape(a, gbs[a].shape), BF16) for a in arrs]),
        compiler_params=_params(("parallel",), blocks, blocks),
    )(c_arr, *_in_hbm([gbs[a] for a in arrs]), *_in_hbm(lands)))


def _sum_chips(a, p, land, shard_shape, jc_arr, name):
    ps = land.shape[1:]
    ax = _rows_axis(a)
    rows = ps[ax]
    nsub = 2 if rows % (2 * SUBLANES_BF16) == 0 else 1
    bs = tuple(r // nsub if q == ax else r for q, r in enumerate(ps))
    nd = len(ps)

    def at_rows(v):
        return tuple(v if q == ax else 0 for q in range(nd))

    def body(jc_ref, p_ref, l_ref, o_ref):
        del jc_ref
        acc = p_ref[...].astype(F32) + l_ref[0].astype(F32)
        acc = acc + l_ref[1].astype(F32)
        o_ref[...] = acc + l_ref[2].astype(F32)

    blocks = 4 * _nbytes(bs, BF16) + _nbytes(bs, F32)
    return pl.pallas_call(
        body, name=name,
        grid_spec=pltpu.PrefetchScalarGridSpec(
            num_scalar_prefetch=1, grid=(nsub,),
            in_specs=[pl.BlockSpec((None,) + bs, lambda s, jc: (jc[0],) + at_rows(s)),
                      pl.BlockSpec((3,) + bs, lambda s, jc: (0,) + at_rows(s))],
            out_specs=pl.BlockSpec(bs, lambda s, jc: at_rows(jc[1] * nsub + s))),
        out_shape=pltpu.HBM(shard_shape, F32),
        compiler_params=_params(("parallel",), blocks, 2 * _nbytes(bs, F32)),
    )(jc_arr, *_in_hbm([p, land]))


def _small_comm(v):
    rows = v.shape[0]

    def copies(cins, couts, csems):
        x, y, c = _coords()
        me = 4 * x + 2 * y + c
        out = [pltpu.make_async_copy(cins[0], couts[0].at[me], csems[0].at[0])]
        for dlt in range(1, 8):
            px = 1 - x if (dlt >> 2) & 1 else x
            py = 1 - y if (dlt >> 1) & 1 else y
            pc = 1 - c if dlt & 1 else c
            out.append(_remote(cins[0], couts[0].at[me], csems[1].at[dlt - 1], csems[2].at[dlt - 1], (px, py, pc)))
        return out

    return _symmetric([v], [SDS((8, rows, LANES), F32)], _dma_sems(1, 7, 7), copies)


def _sum8(slots, name):
    def body(s_ref, o_ref):
        acc = s_ref[0]
        for i in range(1, 8):
            acc = acc + s_ref[i]
        o_ref[...] = acc

    return pl.pallas_call(
        body, name=name,
        in_specs=[pl.BlockSpec(memory_space=pltpu.VMEM)], out_specs=pl.BlockSpec(memory_space=pltpu.VMEM),
        out_shape=SDS(slots.shape[1:], F32),
    )(slots)


def _adamw(w, g, m, v, name, g_plane=None):
    rows, cols = w.shape
    tr = _tile(rows, max(SUBLANES_F32, (256 * 1024 // cols) // SUBLANES_F32 * SUBLANES_F32), SUBLANES_F32)

    def body(w_ref, g_ref, m_ref, v_ref, go_ref, d_ref, mo_ref, vo_ref):
        gr = g_ref[...]
        mn = ADAM_B1 * m_ref[...] + (1.0 - ADAM_B1) * gr
        vn = ADAM_B2 * v_ref[...] + (1.0 - ADAM_B2) * (gr * gr)
        m_hat = mn / (1.0 - ADAM_B1 ** ADAM_STEP)
        v_hat = vn / (1.0 - ADAM_B2 ** ADAM_STEP)
        d_ref[...] = -ADAM_LR * (m_hat / (jnp.sqrt(v_hat) + ADAM_EPS) + ADAM_WD * w_ref[...])
        go_ref[...] = gr
        mo_ref[...] = mn
        vo_ref[...] = vn

    spec = pl.BlockSpec((tr, cols), lambda i: (i, 0))
    g_spec = spec if g_plane is None else pl.BlockSpec((None, tr, cols), lambda i: (g_plane, i, 0))
    return pl.pallas_call(
        body, name=name, grid=(rows // tr,),
        in_specs=[spec, g_spec, spec, spec], out_specs=[spec, spec, spec, spec],
        out_shape=_out_hbm([SDS((rows, cols), F32)] * 4),
        compiler_params=_params(("parallel",), 8 * _nbytes((tr, cols), F32), 4 * _nbytes((tr, cols), F32)),
    )(*_in_hbm([w, g, m, v]))


def _pack(parts):
    rows = []
    for p in parts:
        r = p.reshape(-1, LANES)
        pad = (-r.shape[0]) % SUBLANES_F32
        if pad:
            r = jnp.pad(r, ((0, pad), (0, 0)))
        rows.append(r)
    return jnp.concatenate(rows, axis=0)


def _unpack(packed, shapes):
    out, at = [], 0
    for s in shapes:
        n = 1
        for q in s:
            n *= q
        r = n // LANES
        out.append(packed[at:at + r].reshape(s))
        at += r + (-r) % SUBLANES_F32
    return out


def kernel(x, norm_mix, w_in, pool_w, pool_scale, w_pool_proj, conv_w, w_conv_out, w_o, norm_ffn, w_up, ffn_conv_w, ffn_conv_b, w_down, norm_final, loss_target, m_norm_mix, m_w_in, m_pool_w, m_pool_scale, m_w_pool_proj, m_conv_w, m_w_conv_out, m_w_o, m_norm_ffn, m_w_up, m_ffn_conv_w, m_ffn_conv_b, m_w_down, m_norm_final, v_norm_mix, v_w_in, v_pool_w, v_pool_scale, v_w_pool_proj, v_conv_w, v_w_conv_out, v_w_o, v_norm_ffn, v_w_up, v_ffn_conv_w, v_ffn_conv_b, v_w_down, v_norm_final):
    nseq, seq, d = x.shape
    t = nseq * seq
    f = w_down.shape[1] * 4
    c = d // N_GROUPS
    xy = lax.axis_index("x") * 2 + lax.axis_index("y")
    c_arr = lax.axis_index("c").astype(jnp.int32).reshape(1)
    jc_arr = jnp.stack([xy, lax.axis_index("c")]).astype(jnp.int32)
    nsh = 4
    zero = jnp.zeros((), jnp.int32)

    locs = [w_in[0].astype(BF16),
            jnp.stack([w_pool_proj[0], w_conv_out[0], w_o[0]]).astype(BF16),
            w_up[0].astype(BF16), w_down[0].astype(BF16), pool_w[0].astype(BF16)]
    full_shapes = [(nsh, d, N_SPLITS * d // nsh), (3, d, d), (nsh, d, 2 * f // nsh), (f, d), (N_GROUPS, c, c)]

    cw_pad = lax.dynamic_update_slice(jnp.zeros((3, d), F32), conv_w[0], (zero, xy * (d // 4)))
    fw_pad = lax.dynamic_update_slice(jnp.zeros((3, 2 * f), F32), ffn_conv_w[0], (zero, xy * (f // 2)))
    small_w = _pack([cw_pad, fw_pad]) * 0.5

    x2d = x.reshape(t, d)
    tgt = loss_target.reshape(t, d)
    w_in_f, slots_w = _run_comm(_merge([_gather_comm([0], locs, full_shapes), _small_comm(small_w)]), "gather_w_in")
    conv_w_f, ffn_cw_f = _unpack(_sum8(slots_w, "sum8_weights"), [(3, d), (3, 2 * f)])
    ffn_cw_p = ffn_cw_f.reshape(3, 2, f).transpose(1, 0, 2)
    ffn_cb_p = ffn_conv_b.reshape(2, 1, f)
    (z, h1), (w3_f, pool_w_f, w_up_f) = _fwd_in(
        x2d, norm_mix, w_in_f,
        _merge([_gather_comm([1, 4], locs, full_shapes), _gather_comm([2], locs, full_shapes, part=(0, 2))]))
    lhs3 = _mixer_mid_fwd(z, pool_w_f, pool_scale, conv_w_f, nseq)
    (lhs3, ypc, x1, h2), (w_up_f,) = _mixer_out(
        lhs3, z, x2d, w3_f, norm_ffn, _gather_comm([2], locs, full_shapes, part=(1, 2), into={2: w_up_f}))
    (u0,), (w_down_f,) = _ffn_up(h2, w_up_f, f, _gather_comm([3], locs, full_shapes))
    act, ua = _ffn_mid_fwd(u0, ffn_cw_p, ffn_cb_p, nseq)
    dx2, dx2b, loss11, g_norm_final = _ffn_down_loss(act, w_down_f, x1, tgt, norm_final.reshape(1, d))

    gbs, lands, ps, lands2, rs = {}, {}, {}, {}, {}
    tn_up = _tile(2 * f // nsh, 1408, LANES)
    npp = f // tn_up

    def add(arrs, name):
        for a, p in zip(arrs, _add_halves(arrs, gbs, [lands[a] for a in arrs], c_arr, name)):
            ps[a] = p

    def summed(a):
        rs[a] = _sum_chips(a, ps[a], lands2[a], _shard_shape(a, full_shapes[a]), jc_arr, "sum_chips_%d" % a)

    (gbs[3],), _ = _wgrad(act, dx2b, "wgrad_down", tr=tn_up, tn=d)
    (da,), (lands[3],) = _ffn_bwd_da(dx2b, w_down_f, _halves_comm([3], gbs))
    add([3], "add_halves_down")
    (du0, g_ffn_cw_p, g_ffn_cb_p), (lands2[3],) = _ffn_mid_bwd(da, u0, ua, ffn_cw_p, nseq, _chips_comm([3], ps))
    summed(3)
    (gbs[2],), (rs[3],) = _wgrad(h2, du0, "wgrad_up", tr=d, tn=tn_up, b_plane_of=lambda n: (n // npp, n % npp),
                                 out_shards=nsh, comm=_result_comm([3], rs))
    (dx1, rhs3, g_norm_ffn), (lands[2],) = _ffn_bwd_dx1(du0, w_up_f, x1, dx2, norm_ffn, 3, _halves_comm([2], gbs))
    add([2], "add_halves_up")
    (rhs3, dz, dpq), (lands2[2],) = _mixer_bwd(rhs3, z, ypc, w3_f, _chips_comm([2], ps, part=(0, 2)))
    (gbs[1],), (lands2[2],) = _wgrad3(lhs3, rhs3, _chips_comm([2], ps, part=(1, 2), into=lands2))
    summed(2)
    (dz, g_conv_w), (lands[1], rs[2]) = _conv_bwd(dz, dpq, z, conv_w_f, nseq,
                                                  _merge([_halves_comm([1], gbs), _result_comm([2], rs)]))
    add([1], "add_halves_sq3")
    (dz, g_pool_w, g_pool_scale), _ = _pool_bwd_call(dz, dpq, z, pool_w_f, pool_scale, nseq)
    gbs[4] = g_pool_w.astype(BF16)
    (gbs[0],), (lands2[1],) = _wgrad_in(h1, dz, nsh, _chips_comm([1], ps))
    summed(1)
    lands[0], lands[4] = _run_comm(_halves_comm([0, 4], gbs), "exchange_halves_in")
    add([0, 4], "add_halves_in")
    g_ffn_cw = g_ffn_cw_p.transpose(1, 0, 2).reshape(3, 2 * f)
    small_a = _pack([g_pool_scale, g_norm_ffn, g_ffn_cb_p.reshape(1, 2 * f), g_norm_final.reshape(d), g_conv_w,
                     g_ffn_cw, jnp.pad(loss11, ((0, SUBLANES_F32 - 1), (0, LANES - 1)))])
    (grad_x, g_norm_mix), (lands2[0], lands2[4], rs[1], slots_a) = _mixer_bwd_dx(
        dz, w_in_f, x2d, dx1, norm_mix,
        _merge([_chips_comm([0, 4], ps), _result_comm([1], rs), _small_comm(small_a)]))
    summed(0)
    summed(4)
    rs[0], rs[4], slots_b = _run_comm(_merge([_result_comm([0, 4], rs), _small_comm(_pack([g_norm_mix]))]),
                                      "exchange_result_in")
    shapes_a = [(1, d), (1, d), (1, 2 * f), (d,), (3, d), (3, 2 * f), (SUBLANES_F32, LANES)]
    gs_pool_scale, gs_norm_ffn, gs_ffn_cb, gs_norm_final, gs_conv_w, gs_ffn_cw, loss_blk = _unpack(
        _sum8(slots_a, "sum8_grads"), shapes_a)
    (gs_norm_mix,) = _unpack(_sum8(slots_b, "sum8_norm_mix"), [(1, d)])
    gs_conv_w = lax.dynamic_slice(gs_conv_w, (zero, xy * (d // 4)), (3, d // 4))
    gs_ffn_cw = lax.dynamic_slice(gs_ffn_cw, (zero, xy * (f // 2)), (3, f // 2))

    def upd(w, g, m, v, name, g_plane=None):
        shape = w.shape
        rows = 1
        for q in shape[:-1]:
            rows *= q
        g2 = g if g_plane is not None else g.reshape(rows, shape[-1])
        outs = _adamw(w.reshape(rows, shape[-1]), g2, m.reshape(rows, shape[-1]), v.reshape(rows, shape[-1]),
                      name, g_plane)
        return [o.reshape(shape) for o in outs]

    res = {
        "w_in": upd(w_in, rs[0], m_w_in, v_w_in, "adamw_w_in"),
        "pool_w": upd(pool_w, rs[4], m_pool_w, v_pool_w, "adamw_pool_w"),
        "w_pool_proj": upd(w_pool_proj, rs[1], m_w_pool_proj, v_w_pool_proj, "adamw_w_pool_proj", 0),
        "w_conv_out": upd(w_conv_out, rs[1], m_w_conv_out, v_w_conv_out, "adamw_w_conv_out", 1),
        "w_o": upd(w_o, rs[1], m_w_o, v_w_o, "adamw_w_o", 2),
        "w_up": upd(w_up, rs[2], m_w_up, v_w_up, "adamw_w_up"),
        "w_down": upd(w_down, rs[3], m_w_down, v_w_down, "adamw_w_down"),
    }

    small_names = ["norm_mix", "pool_scale", "norm_ffn", "ffn_conv_b", "norm_final", "conv_w", "ffn_conv_w"]
    small_ws = [norm_mix, pool_scale, norm_ffn, ffn_conv_b, norm_final, conv_w, ffn_conv_w]
    small_ms = [m_norm_mix, m_pool_scale, m_norm_ffn, m_ffn_conv_b, m_norm_final, m_conv_w, m_ffn_conv_w]
    small_vs = [v_norm_mix, v_pool_scale, v_norm_ffn, v_ffn_conv_b, v_norm_final, v_conv_w, v_ffn_conv_w]
    small_gs = [gs_norm_mix, gs_pool_scale, gs_norm_ffn, gs_ffn_cb, gs_norm_final, gs_conv_w, gs_ffn_cw]
    _, sd, sm, sv = _adamw(_pack(small_ws), _pack(small_gs), _pack(small_ms), _pack(small_vs), "adamw_small")
    shapes = [w.shape for w in small_ws]
    sd, sm, sv = _unpack(sd, shapes), _unpack(sm, shapes), _unpack(sv, shapes)
    for i, nm in enumerate(small_names):
        res[nm] = [small_gs[i].reshape(shapes[i]), sd[i], sm[i], sv[i]]

    order = ["norm_mix", "w_in", "pool_w", "pool_scale", "w_pool_proj", "conv_w", "w_conv_out", "w_o", "norm_ffn",
             "w_up", "ffn_conv_w", "ffn_conv_b", "w_down", "norm_final"]
    return (loss_blk[0, 0], grad_x.reshape(x.shape), *[res[n][0] for n in order], *[res[n][1] for n in order],
            *[res[n][2] for n in order], *[res[n][3] for n in order])
```

```python
import math

import jax
import jax.numpy as jnp
from jax import lax
from jax.experimental import pallas as pl
from jax.experimental.pallas import tpu as pltpu

F32 = jnp.float32
BF16 = jnp.bfloat16
SDS = jax.ShapeDtypeStruct
MESH = pl.DeviceIdType.MESH

RMS_EPS = 1e-6
POOL_WINDOWS = (2, 4, 8, 16)
N_GROUPS = len(POOL_WINDOWS)
N_SPLITS = 6

ADAM_LR = 0.001
ADAM_B1 = 0.9
ADAM_B2 = 0.999
ADAM_EPS = 1e-08
ADAM_WD = 0.01
ADAM_STEP = 10

LANES = 128
SUBLANES_F32 = 8
SUBLANES_BF16 = 16
VMEM_BYTES = 64 * 1024 * 1024
VMEM_CAP = VMEM_BYTES - 8 * 1024 * 1024
VMEM_FLOOR = 16 * 1024 * 1024

ANY = pl.BlockSpec(memory_space=pl.ANY)


def _tile(dim, pref, align):
    if dim <= pref:
        return dim
    t = (pref // align) * align
    while t >= align:
        if dim % t == 0:
            return t
        t -= align
    return dim


def _nbytes(shape, dtype):
    n = 1
    for s in shape:
        n *= s
    return n * jnp.dtype(dtype).itemsize


def _params(sem, block_bytes, temp_bytes=0):
    need = 2 * block_bytes + temp_bytes + 4 * 1024 * 1024
    return pltpu.CompilerParams(dimension_semantics=sem, vmem_limit_bytes=int(min(max(need, VMEM_FLOOR), VMEM_CAP)))


class _Comm:
    def __init__(self, ins, out_shapes, sems, start, finish, aliases=None):
        self.ins = list(ins)
        self.out_shapes = list(out_shapes)
        self.sems = list(sems)
        self.start = start
        self.finish = finish
        self.aliases = dict(aliases or {})


def _pcall(body, *, name, grid, in_specs, out_specs, out_shape, sem, blocks, temps=0, scratch_shapes=(),
           input_output_aliases=None, comm=None):
    in_specs = list(in_specs)
    out_specs = list(out_specs)
    out_shape = list(out_shape)
    scratch_shapes = list(scratch_shapes)
    aliases = dict(input_output_aliases or {})
    n_in, n_out, n_scr = len(in_specs), len(out_shape), len(scratch_shapes)
    if comm is None:
        call = pl.pallas_call(
            body, name=name, grid=grid, in_specs=in_specs, out_specs=out_specs, out_shape=out_shape,
            scratch_shapes=scratch_shapes, input_output_aliases=aliases,
            compiler_params=_params(sem, blocks, temps))
        return lambda *args: (list(call(*args)), [])

    nci, nco = len(comm.ins), len(comm.out_shapes)

    def hosted(*refs):
        ins = refs[:n_in]
        cins = refs[n_in:n_in + nci]
        outs = refs[n_in + nci:n_in + nci + n_out]
        couts = refs[n_in + nci + n_out:n_in + nci + n_out + nco]
        scr = refs[n_in + nci + n_out + nco:n_in + nci + n_out + nco + n_scr]
        csems = refs[n_in + nci + n_out + nco + n_scr:]
        first = None
        last = None
        for q, g in enumerate(grid):
            pid = pl.program_id(q)
            first = (pid == 0) if first is None else first & (pid == 0)
            last = (pid == g - 1) if last is None else last & (pid == g - 1)

        @pl.when(first)
        def _():
            comm.start(cins, couts, csems)

        body(*ins, *outs, *scr)

        @pl.when(last)
        def _():
            comm.finish(cins, couts, csems)

    for i, o in comm.aliases.items():
        aliases[n_in + i] = n_out + o
    call = pl.pallas_call(
        hosted, name=name, grid=grid, in_specs=in_specs + [ANY] * nci, out_specs=out_specs + [ANY] * nco,
        out_shape=out_shape + comm.out_shapes, scratch_shapes=scratch_shapes + comm.sems,
        input_output_aliases=aliases,
        compiler_params=_params(("arbitrary",) * len(grid), blocks, temps))

    def run(*args):
        res = call(*args, *comm.ins)
        return list(res[:n_out]), list(res[n_out:])

    return run


def _run_comm(comm, name):
    def body(*refs):
        nci, nco = len(comm.ins), len(comm.out_shapes)
        cins, couts, csems = refs[:nci], refs[nci:nci + nco], refs[nci + nco:]
        comm.start(cins, couts, csems)
        comm.finish(cins, couts, csems)

    return list(pl.pallas_call(
        body, name=name, in_specs=[ANY] * len(comm.ins), out_specs=[ANY] * len(comm.out_shapes),
        out_shape=comm.out_shapes, scratch_shapes=comm.sems, input_output_aliases=comm.aliases,
    )(*comm.ins))


def _dot(a, b):
    return jnp.dot(a, b, preferred_element_type=F32)


def _dot_tb(a, b):
    return lax.dot_general(a, b, (((1,), (1,)), ((), ())), preferred_element_type=F32)


def _dot_ta(a, b):
    return lax.dot_general(a, b, (((0,), (0,)), ((), ())), preferred_element_type=F32)


def _rms_fwd(x):
    inv = lax.rsqrt(jnp.mean(x * x, axis=-1, keepdims=True) + RMS_EPS)
    return x * inv, inv


def _rms_bwd(dy, xhat, inv, g):
    gd = dy * g
    return inv * (gd - xhat * jnp.mean(gd * xhat, axis=-1, keepdims=True))


def _sigmoid(x):
    return 1.0 / (1.0 + jnp.exp(-x))


def _shift_down(x, k, row):
    return jnp.where(row >= k, pltpu.roll(x, k, 0), 0.0)


def _shift_up(x, k, row):
    s = x.shape[0]
    return jnp.where(row < s - k, pltpu.roll(x, s - k, 0), 0.0)


def _pool_fwd(u, win, row):
    s = u
    k = 1
    while k < win:
        s = s + _shift_down(s, k, row)
        k *= 2
    cnt = jnp.minimum(row + 1, win).astype(F32)
    return s / cnt - u


def _pool_bwd(dp, win, row):
    cnt = jnp.minimum(row + 1, win).astype(F32)
    s = dp / cnt
    k = 1
    while k < win:
        s = s + _shift_up(s, k, row)
        k *= 2
    return s - dp


def _acc_over(k, nk, part, acc, o_ref):
    @pl.when(k == 0)
    def _():
        acc[...] = part

    @pl.when(k > 0)
    def _():
        acc[...] += part

    @pl.when(k == nk - 1)
    def _():
        o_ref[...] = acc[...].astype(o_ref.dtype)


def _fwd_in(x, g, w_loc, order, comm):
    t, d = x.shape
    ws = w_loc.shape[1]
    nsh = order.shape[0]
    tm = _tile(t, 1024, SUBLANES_BF16)
    ni = t // tm
    nci, nco = len(comm.ins), len(comm.out_shapes)

    def body(order_ref, x_ref, g_ref, loc_ref, *rest):
        del order_ref
        cins = rest[:nci]
        z_ref, h_ref, full_ref = rest[nci:nci + 3]
        couts = rest[nci + 3:nci + 3 + nco]
        hs, wbuf, wsem, own_s, own_r, snd_s, snd_r, fwd_s, fwd_r = rest[nci + 3 + nco:nci + 12 + nco]
        csems = rest[nci + 12 + nco:]
        j = pl.program_id(0)
        i = pl.program_id(1)
        x_, y_, c_ = _coords()
        own = 2 * x_ + y_
        sib = (x_, y_, 1 - c_)
        peers = _peer_chips(x_, y_)

        def sends():
            return [_remote(_half(0, loc_ref, c_), _piece(0, full_ref, own, c_), snd_s.at[p], snd_r.at[p], (px, py, c_))
                    for p, (px, py) in enumerate(peers)]

        def owns():
            return [_remote(_half(0, loc_ref, h), _piece(0, full_ref, own, h), own_s.at[h], own_r.at[h], sib)
                    for h in range(2)]

        def forward(p, half):
            px, py = peers[p]
            landed = _piece(0, full_ref, 2 * px + py, half)
            return _remote(landed, landed, fwd_s.at[p], fwd_r.at[p], sib)

        def load(src, slot):
            return pltpu.make_async_copy(src, wbuf.at[slot], wsem.at[slot])

        @pl.when((j == 0) & (i == 0))
        def _():
            for cp in sends() + owns():
                cp.start()
            load(loc_ref, 0).start()
            comm.start(cins, couts, csems)

        @pl.when(j == 0)
        def _():
            xh, _ = _rms_fwd(x_ref[...])
            h = (xh * g_ref[...]).astype(BF16)
            hs[pl.ds(pl.multiple_of(i * tm, tm), tm), :] = h
            h_ref[...] = h

        slot = j % 2

        @pl.when(i == 0)
        def _():
            load(loc_ref, slot).wait()

        z_ref[...] = _dot(hs[pl.ds(pl.multiple_of(i * tm, tm), tm), :], wbuf[slot]).astype(BF16)

        for p in range(nsh - 1):
            @pl.when((i == ni - 1) & (j == p))
            def _(p=p):
                px, py = peers[p]
                sends()[p].wait_recv()
                forward(p, c_).start()
                forward(p, 1 - c_).wait_recv()
                load(full_ref.at[2 * px + py], 1 - slot).start()

        @pl.when((j == nsh - 1) & (i == ni - 1))
        def _():
            for cp in sends() + [forward(p, c_) for p in range(nsh - 1)]:
                cp.wait_send()
            for cp in owns():
                cp.wait()
            comm.finish(cins, couts, csems)

    last = ni - 1
    blocks = _nbytes((tm, d), F32) + _nbytes((tm, ws), BF16) + _nbytes((tm, d), BF16)
    scratch = _nbytes((t, d), BF16) + 2 * _nbytes((d, ws), BF16)
    res = pl.pallas_call(
        body, name="fwd_in",
        grid_spec=pltpu.PrefetchScalarGridSpec(
            num_scalar_prefetch=1, grid=(nsh, ni),
            in_specs=[pl.BlockSpec((tm, d), lambda j, i, o: (jnp.where(j == 0, i, last), 0)),
                      pl.BlockSpec((1, d), lambda j, i, o: (0, 0)), ANY] + [ANY] * nci,
            out_specs=[pl.BlockSpec((tm, ws), lambda j, i, o: (i, o[j])),
                       pl.BlockSpec((tm, d), lambda j, i, o: (jnp.where(j == 0, i, last), 0)), ANY] + [ANY] * nco,
            scratch_shapes=[pltpu.VMEM((t, d), BF16), pltpu.VMEM((2, d, ws), BF16)]
            + _dma_sems(2, 2, 2, nsh - 1, nsh - 1, nsh - 1, nsh - 1) + comm.sems),
        out_shape=[SDS((t, nsh * ws), BF16), SDS((t, d), BF16), SDS((nsh, d, ws), BF16)] + comm.out_shapes,
        input_output_aliases={4 + i: 3 + o for i, o in comm.aliases.items()},
        compiler_params=_params(("arbitrary", "arbitrary"), blocks, scratch + 3 * _nbytes((tm, d), F32)),
    )(order, x, g, w_loc, *comm.ins)
    return list(res[:3]), list(res[3:])


def _mixer_mid_fwd(z, pool_w, pool_scale, conv_w, nseq, comm=None):
    t = z.shape[0]
    d = pool_scale.shape[1]
    s = t // nseq
    c = d // N_GROUPS

    def body(zp, zb, zc, zv, pw, ps, cw, o):
        j = pl.program_id(1)
        row = lax.broadcasted_iota(jnp.int32, (s, c), 0)
        for gi, win in enumerate(POOL_WINDOWS):
            @pl.when(j == gi)
            def _(win=win):
                pooled = _pool_fwd(zp[...].astype(F32), win, row)
                o[0] = (_dot(pooled.astype(BF16), pw[...]) * ps[...]).astype(BF16)

        cv = zc[...].astype(F32) * zv[...].astype(F32)
        cc = (cw[pl.ds(2, 1), :] * cv + cw[pl.ds(1, 1), :] * _shift_down(cv, 1, row)
              + cw[pl.ds(0, 1), :] * _shift_down(cv, 2, row))
        o[1] = (zb[...].astype(F32) * cc).astype(BF16)

    blocks = 4 * _nbytes((s, c), BF16) + _nbytes((c, c), BF16) + _nbytes((2, s, c), BF16)
    return _pcall(
        body, name="mixer_mid_fwd", grid=(nseq, N_GROUPS),
        in_specs=[pl.BlockSpec((s, c), lambda b, j: (b, j)),
                  pl.BlockSpec((s, c), lambda b, j: (b, N_GROUPS + j)),
                  pl.BlockSpec((s, c), lambda b, j: (b, 2 * N_GROUPS + j)),
                  pl.BlockSpec((s, c), lambda b, j: (b, 3 * N_GROUPS + j)),
                  pl.BlockSpec((None, c, c), lambda b, j: (j, 0, 0)),
                  pl.BlockSpec((1, c), lambda b, j: (0, j)),
                  pl.BlockSpec((3, c), lambda b, j: (0, j))],
        out_specs=[pl.BlockSpec((2, s, c), lambda b, j: (0, b, j))],
        out_shape=[SDS((3, t, d), BF16)],
        sem=("parallel", "parallel"), blocks=blocks, temps=8 * _nbytes((s, c), F32), comm=comm,
    )(z, z, z, z, pool_w, pool_scale, conv_w)


def _mixer_out(lhs3, z, x, w3, g_ffn, comm=None):
    t, d = x.shape
    tm = _tile(t, 256, SUBLANES_BF16)

    def body(pq, zgp, zgc, x_ref, w_ref, g_ref, mrg, ypc, x1o, h2o):
        yp = _dot(pq[0], w_ref[0])
        yc = _dot(pq[1], w_ref[1])
        m = _sigmoid(zgp[...].astype(F32)) * yp + _sigmoid(zgc[...].astype(F32)) * yc
        mb = m.astype(BF16)
        x1 = x_ref[...] + _dot(mb, w_ref[2])
        ypc[0] = yp.astype(BF16)
        ypc[1] = yc.astype(BF16)
        mrg[...] = mb
        x1o[...] = x1
        xh, _ = _rms_fwd(x1)
        h2o[...] = (xh * g_ref[...]).astype(BF16)

    blocks = (_nbytes((2, tm, d), BF16) * 2 + _nbytes((tm, d), BF16) * 4 + _nbytes((tm, d), F32) * 2
              + _nbytes((3, d, d), BF16))
    return _pcall(
        body, name="mixer_out", grid=(t // tm,),
        in_specs=[pl.BlockSpec((2, tm, d), lambda i: (0, i, 0)),
                  pl.BlockSpec((tm, d), lambda i: (i, 4)),
                  pl.BlockSpec((tm, d), lambda i: (i, 5)),
                  pl.BlockSpec((tm, d), lambda i: (i, 0)),
                  pl.BlockSpec((3, d, d), lambda i: (0, 0, 0)),
                  pl.BlockSpec((1, d), lambda i: (0, 0))],
        out_specs=[pl.BlockSpec((None, tm, d), lambda i: (2, i, 0)),
                   pl.BlockSpec((2, tm, d), lambda i: (0, i, 0)),
                   pl.BlockSpec((tm, d), lambda i: (i, 0)),
                   pl.BlockSpec((tm, d), lambda i: (i, 0))],
        out_shape=[SDS(lhs3.shape, BF16), SDS((2, t, d), BF16), SDS((t, d), F32), SDS((t, d), BF16)],
        input_output_aliases={0: 0},
        sem=("parallel",), blocks=blocks, temps=8 * _nbytes((tm, d), F32), comm=comm,
    )(lhs3, z, z, x, w3, g_ffn)


def _ffn_up(h2, w_up, f, comm=None):
    t, d = h2.shape
    _, _, ws = w_up.shape
    tm = _tile(t, 1024, SUBLANES_BF16)
    tn = _tile(ws, 1408, LANES)
    nps = ws // tn
    npp = f // tn

    def body(h_ref, w_ref, o_ref):
        o_ref[...] = _dot(h_ref[...], w_ref[...]).astype(BF16)

    blocks = _nbytes((tm, d), BF16) + _nbytes((d, tn), BF16) + _nbytes((tm, tn), BF16)
    return _pcall(
        body, name="ffn_up", grid=(t // tm, 2 * npp),
        in_specs=[pl.BlockSpec((tm, d), lambda i, j: (i, 0)),
                  pl.BlockSpec((None, d, tn), lambda i, j: (j // nps, 0, j % nps))],
        out_specs=[pl.BlockSpec((None, tm, tn), lambda i, j: (j // npp, i, j % npp))],
        out_shape=[SDS((2, t, f), BF16)],
        sem=("parallel", "parallel"), blocks=blocks, temps=_nbytes((tm, tn), F32), comm=comm,
    )(h2, w_up)


def _conv3_rows(u, u1, u2, w_ref, p):
    return w_ref[p, pl.ds(2, 1), :] * u + w_ref[p, pl.ds(1, 1), :] * u1 + w_ref[p, pl.ds(0, 1), :] * u2


CHUNK = 64
HALO = SUBLANES_F32


def _up1_up2(u, nxt):
    rows = u.shape[0]
    ext = jnp.concatenate([u, nxt], axis=0)
    n = rows + HALO
    return pltpu.roll(ext, n - 1, 0)[:rows], pltpu.roll(ext, n - 2, 0)[:rows]


def _fold8(x):
    return jnp.sum(x.reshape(x.shape[0] // SUBLANES_F32, SUBLANES_F32, x.shape[1]), axis=0)


def _ffn_mid_fwd(u0, cw, cb, nseq):
    _, t, f = u0.shape
    s = t // nseq
    c = _tile(f, 256, LANES)

    def body(u_ref, w_ref, b_ref, a_ref, uo_ref):
        row = lax.broadcasted_iota(jnp.int32, (s, c), 0)
        act = []
        for p in range(2):
            u = u_ref[p].astype(F32)
            act.append(_conv3_rows(u, _shift_down(u, 1, row), _shift_down(u, 2, row), w_ref, p) + b_ref[p])
            uo_ref[p] = act[p].astype(BF16)
        ug, uv = act
        a_ref[...] = (ug * _sigmoid(ug) * uv).astype(BF16)

    blocks = 2 * _nbytes((2, s, c), BF16) + _nbytes((s, c), BF16)
    outs, _ = _pcall(
        body, name="ffn_mid_fwd", grid=(f // c, nseq),
        in_specs=[pl.BlockSpec((2, s, c), lambda j, b: (0, b, j)),
                  pl.BlockSpec((2, 3, c), lambda j, b: (0, 0, j)),
                  pl.BlockSpec((2, 1, c), lambda j, b: (0, 0, j))],
        out_specs=[pl.BlockSpec((s, c), lambda j, b: (b, j)),
                   pl.BlockSpec((2, s, c), lambda j, b: (0, b, j))],
        out_shape=[SDS((t, f), BF16), SDS((2, t, f), BF16)],
        sem=("parallel", "parallel"), blocks=blocks, temps=8 * _nbytes((s, c), F32),
    )(u0, cw, cb)
    return outs


def _ffn_down_loss(a, w_down, x1, tgt, g_fin):
    t, f = a.shape
    d = x1.shape[1]
    tm = _tile(t, 256, SUBLANES_BF16)
    nsteps = t // tm

    def body(a_ref, w_ref, x1_ref, t_ref, g_ref, dx_ref, dxb_ref, loss_ref, gg_ref, lacc):
        i = pl.program_id(0)

        @pl.when(i == 0)
        def _():
            lacc[...] = jnp.zeros_like(lacc)
            gg_ref[...] = jnp.zeros_like(gg_ref)

        x2 = x1_ref[...] + _dot(a_ref[...], w_ref[...])
        xh, inv = _rms_fwd(x2)
        g = g_ref[...]
        e = xh * g - t_ref[...]
        lacc[...] += jnp.sum(e * e, axis=0, keepdims=True)
        dy = e * (1.0 / d)
        gg_ref[...] += jnp.sum(dy * xh, axis=0, keepdims=True)
        dx2 = _rms_bwd(dy, xh, inv, g)
        dx_ref[...] = dx2
        dxb_ref[...] = dx2.astype(BF16)

        @pl.when(i == nsteps - 1)
        def _():
            loss_ref[...] = jnp.sum(lacc[...], axis=1, keepdims=True) * (0.5 / d)

    blocks = (_nbytes((tm, f), BF16) + _nbytes((f, d), BF16) + 3 * _nbytes((tm, d), F32) + _nbytes((tm, d), BF16))
    outs, _ = _pcall(
        body, name="ffn_down_loss", grid=(nsteps,),
        in_specs=[pl.BlockSpec((tm, f), lambda i: (i, 0)), pl.BlockSpec((f, d), lambda i: (0, 0)),
                  pl.BlockSpec((tm, d), lambda i: (i, 0)), pl.BlockSpec((tm, d), lambda i: (i, 0)),
                  pl.BlockSpec((1, d), lambda i: (0, 0))],
        out_specs=[pl.BlockSpec((tm, d), lambda i: (i, 0)), pl.BlockSpec((tm, d), lambda i: (i, 0)),
                   pl.BlockSpec((1, 1), lambda i: (0, 0)), pl.BlockSpec((1, d), lambda i: (0, 0))],
        out_shape=[SDS((t, d), F32), SDS((t, d), BF16), SDS((1, 1), F32), SDS((1, d), F32)],
        scratch_shapes=[pltpu.VMEM((1, d), F32)],
        sem=("arbitrary",), blocks=blocks, temps=8 * _nbytes((tm, d), F32),
    )(a, w_down, x1, tgt, g_fin)
    return outs


def _ffn_bwd_da(dxb, w_down, comm=None):
    t, d = dxb.shape
    f = w_down.shape[0]
    tm = _tile(t, 1024, SUBLANES_BF16)
    tn = _tile(f, 1408, LANES)

    def body(x_ref, w_ref, o_ref):
        o_ref[...] = _dot_tb(x_ref[...], w_ref[...]).astype(BF16)

    blocks = _nbytes((tm, d), BF16) + _nbytes((tn, d), BF16) + _nbytes((tm, tn), BF16)
    return _pcall(
        body, name="ffn_bwd_da", grid=(t // tm, f // tn),
        in_specs=[pl.BlockSpec((tm, d), lambda i, j: (i, 0)), pl.BlockSpec((tn, d), lambda i, j: (j, 0))],
        out_specs=[pl.BlockSpec((tm, tn), lambda i, j: (i, j))],
        out_shape=[SDS((t, f), BF16)],
        sem=("parallel", "parallel"), blocks=blocks, temps=_nbytes((tm, tn), F32), comm=comm,
    )(dxb, w_down)


def _ffn_mid_bwd(da, u0, ua, cw, nseq, comm=None):
    _, t, f = u0.shape
    s = t // nseq
    c = _tile(f, 128, LANES)
    r = _tile(s, CHUNK, SUBLANES_BF16)
    n = s // r

    def body(da_ref, u_ref, ua_ref, w_ref, du_ref, gw_ref, gb_ref):
        @pl.when(pl.program_id(1) == 0)
        def _():
            gw_ref[...] = jnp.zeros_like(gw_ref)
            gb_ref[...] = jnp.zeros_like(gb_ref)

        def step(i, carry):
            nxt, sums = carry
            rows = pl.ds(pl.multiple_of((n - 1 - i) * r, r), r)
            ug = ua_ref[0, rows, :].astype(F32)
            uv = ua_ref[1, rows, :].astype(F32)
            sg = _sigmoid(ug)
            dacc = da_ref[rows, :].astype(F32)
            dus = (dacc * uv * sg * (1.0 + ug * (1.0 - sg)), dacc * (ug * sg))
            first, new_sums = [], []
            for p in range(2):
                du = dus[p]
                d1, d2 = _up1_up2(du, nxt[p])
                du_ref[p, rows, :] = _conv3_rows(du, d1, d2, w_ref, p).astype(BF16)
                u = u_ref[p, rows, :].astype(F32)
                sb, s0, s1, s2 = sums[p]
                new_sums.append((sb + _fold8(du), s0 + _fold8(d2 * u), s1 + _fold8(d1 * u), s2 + _fold8(du * u)))
                first.append(du[:HALO])
            return tuple(first), tuple(new_sums)

        zero = jnp.zeros((HALO, c), F32)
        _, sums = lax.fori_loop(0, n, step, ((zero, zero), ((zero,) * 4,) * 2))
        for p in range(2):
            sb, s0, s1, s2 = sums[p]
            gb_ref[p] += jnp.sum(sb, axis=0, keepdims=True)
            gw_ref[p, pl.ds(0, 1), :] += jnp.sum(s0, axis=0, keepdims=True)
            gw_ref[p, pl.ds(1, 1), :] += jnp.sum(s1, axis=0, keepdims=True)
            gw_ref[p, pl.ds(2, 1), :] += jnp.sum(s2, axis=0, keepdims=True)

    blocks = _nbytes((s, c), BF16) + 3 * _nbytes((2, s, c), BF16)
    return _pcall(
        body, name="ffn_mid_bwd", grid=(f // c, nseq),
        in_specs=[pl.BlockSpec((s, c), lambda j, b: (b, j)),
                  pl.BlockSpec((2, s, c), lambda j, b: (0, b, j)),
                  pl.BlockSpec((2, s, c), lambda j, b: (0, b, j)),
                  pl.BlockSpec((2, 3, c), lambda j, b: (0, 0, j))],
        out_specs=[pl.BlockSpec((2, s, c), lambda j, b: (0, b, j)),
                   pl.BlockSpec((2, 3, c), lambda j, b: (0, 0, j)),
                   pl.BlockSpec((2, 1, c), lambda j, b: (0, 0, j))],
        out_shape=[SDS((2, t, f), BF16), SDS((2, 3, f), F32), SDS((2, 1, f), F32)],
        sem=("parallel", "arbitrary"), blocks=blocks, temps=4 * 1024 * 1024, comm=comm,
    )(da, u0, ua, cw)


def _wgrad(a, b, name, *, tr, tn, b_plane_of=None, out_shards=None, comm=None):
    t, m = a.shape
    n_total = b.shape[-1] * (b.shape[0] if b.ndim == 3 else 1)
    tk = _tile(t, 1024, SUBLANES_BF16)
    nk = t // tk

    def body(a_ref, b_ref, o_ref, acc):
        _acc_over(pl.program_id(2), nk, _dot_ta(a_ref[...], b_ref[...]), acc, o_ref)

    if b.ndim == 3:
        b_spec = pl.BlockSpec((None, tk, tn), lambda r, n, k: (b_plane_of(n)[0], k, b_plane_of(n)[1]))
    else:
        b_spec = pl.BlockSpec((tk, tn), lambda r, n, k: (k, n))
    if out_shards is None:
        o_spec = pl.BlockSpec((tr, tn), lambda r, n, k: (r, n))
        o_shape = SDS((m, n_total), BF16)
    else:
        nps = n_total // out_shards // tn
        o_spec = pl.BlockSpec((None, tr, tn), lambda r, n, k: (n // nps, r, n % nps))
        o_shape = SDS((out_shards, m, n_total // out_shards), BF16)
    blocks = _nbytes((tk, tr), BF16) + _nbytes((tk, tn), BF16) + _nbytes((tr, tn), BF16)
    return _pcall(
        body, name=name, grid=(m // tr, n_total // tn, nk),
        in_specs=[pl.BlockSpec((tk, tr), lambda r, n, k: (k, r)), b_spec],
        out_specs=[o_spec], out_shape=[o_shape],
        scratch_shapes=[pltpu.VMEM((tr, tn), F32)],
        sem=("parallel", "parallel", "arbitrary"), blocks=blocks, temps=2 * _nbytes((tr, tn), F32), comm=comm,
    )(a, b)


def _wgrad3(lhs3, rhs3, comm=None):
    nw, t, d = lhs3.shape
    tk = _tile(t, 1024, SUBLANES_BF16)
    nk = t // tk

    def body(a_ref, b_ref, o_ref, acc):
        _acc_over(pl.program_id(1), nk, _dot_ta(a_ref[...], b_ref[...]), acc, o_ref)

    blocks = 2 * _nbytes((tk, d), BF16) + _nbytes((d, d), BF16)
    return _pcall(
        body, name="wgrad_sq3", grid=(nw, nk),
        in_specs=[pl.BlockSpec((None, tk, d), lambda w, k: (w, k, 0)),
                  pl.BlockSpec((None, tk, d), lambda w, k: (w, k, 0))],
        out_specs=[pl.BlockSpec((None, d, d), lambda w, k: (w, 0, 0))],
        out_shape=[SDS((nw, d, d), BF16)],
        scratch_shapes=[pltpu.VMEM((d, d), F32)],
        sem=("parallel", "arbitrary"), blocks=blocks, temps=2 * _nbytes((d, d), F32), comm=comm,
    )(lhs3, rhs3)


def _ffn_bwd_dx1(du0, w_up, x1, dx2, g_ffn, n_planes_out, comm=None):
    _, t, f = du0.shape
    d = x1.shape[1]
    nsh, _, ws = w_up.shape
    tm = _tile(t, 256, SUBLANES_BF16)
    spp = f // ws

    def body(du_ref, w_ref, x1_ref, dx2_ref, g_ref, dx1_ref, dxb_ref, gg_ref):
        @pl.when(pl.program_id(0) == 0)
        def _():
            gg_ref[...] = jnp.zeros_like(gg_ref)

        dh = None
        for k in range(nsh):
            part = _dot_tb(du_ref[k // spp, :, (k % spp) * ws:(k % spp + 1) * ws], w_ref[k])
            dh = part if dh is None else dh + part
        xh, inv = _rms_fwd(x1_ref[...])
        gg_ref[...] += jnp.sum(dh * xh, axis=0, keepdims=True)
        dx1 = dx2_ref[...] + _rms_bwd(dh, xh, inv, g_ref[...])
        dx1_ref[...] = dx1
        dxb_ref[...] = dx1.astype(BF16)

    blocks = _nbytes((2, tm, f), BF16) + 3 * _nbytes((tm, d), F32) + _nbytes((tm, d), BF16)
    return _pcall(
        body, name="ffn_bwd_dx1", grid=(t // tm,),
        in_specs=[pl.BlockSpec((2, tm, f), lambda i: (0, i, 0)),
                  pl.BlockSpec((nsh, d, ws), lambda i: (0, 0, 0), pipeline_mode=pl.Buffered(1)),
                  pl.BlockSpec((tm, d), lambda i: (i, 0)),
                  pl.BlockSpec((tm, d), lambda i: (i, 0)),
                  pl.BlockSpec((1, d), lambda i: (0, 0))],
        out_specs=[pl.BlockSpec((tm, d), lambda i: (i, 0)),
                   pl.BlockSpec((None, tm, d), lambda i: (n_planes_out - 1, i, 0)),
                   pl.BlockSpec((1, d), lambda i: (0, 0))],
        out_shape=[SDS((t, d), F32), SDS((n_planes_out, t, d), BF16), SDS((1, d), F32)],
        sem=("arbitrary",), blocks=blocks, temps=_nbytes(w_up.shape, BF16) + 8 * _nbytes((tm, d), F32), comm=comm,
    )(du0, w_up, x1, dx2, g_ffn)


def _mixer_bwd(rhs3, z, ypc, w3, comm=None):
    _, t, d = rhs3.shape
    tm = _tile(t, 256, SUBLANES_BF16)

    def body(dx_ref, zgp, zgc, ypc_ref, w_ref, dyo, dzo, dpq):
        dm = _dot_tb(dx_ref[...], w_ref[2])
        sp = _sigmoid(zgp[...].astype(F32))
        sc = _sigmoid(zgc[...].astype(F32))
        dyp = (dm * sp).astype(BF16)
        dyc = (dm * sc).astype(BF16)
        dzo[0] = (dm * ypc_ref[0].astype(F32) * sp * (1.0 - sp)).astype(BF16)
        dzo[1] = (dm * ypc_ref[1].astype(F32) * sc * (1.0 - sc)).astype(BF16)
        dyo[0] = dyp
        dyo[1] = dyc
        dpq[0] = _dot_tb(dyp, w_ref[0]).astype(BF16)
        dpq[1] = _dot_tb(dyc, w_ref[1]).astype(BF16)

    blocks = _nbytes((tm, d), BF16) * 3 + _nbytes((2, tm, d), BF16) * 4 + _nbytes((3, d, d), BF16)
    return _pcall(
        body, name="mixer_bwd", grid=(t // tm,),
        in_specs=[pl.BlockSpec((None, tm, d), lambda i: (2, i, 0)),
                  pl.BlockSpec((tm, d), lambda i: (i, 4)),
                  pl.BlockSpec((tm, d), lambda i: (i, 5)),
                  pl.BlockSpec((2, tm, d), lambda i: (0, i, 0)),
                  pl.BlockSpec((3, d, d), lambda i: (0, 0, 0))],
        out_specs=[pl.BlockSpec((2, tm, d), lambda i: (0, i, 0)),
                   pl.BlockSpec((2, tm, d), lambda i: (2, i, 0)),
                   pl.BlockSpec((2, tm, d), lambda i: (0, i, 0))],
        out_shape=[SDS(rhs3.shape, BF16), SDS((N_SPLITS, t, d), BF16), SDS((2, t, d), BF16)],
        input_output_aliases={0: 0},
        sem=("parallel",), blocks=blocks, temps=8 * _nbytes((tm, d), F32), comm=comm,
    )(rhs3, z, z, ypc, w3)


def _conv_bwd(dz, dpq, z, conv_w, nseq, comm=None):
    _, t, d = dz.shape
    s = t // nseq
    c = _tile(d, 128, LANES)
    nb = d // c

    def body(dz_in, dq_ref, zb, zc, zv, cw, dzo, gw_ref):
        del dz_in

        @pl.when(pl.program_id(1) == 0)
        def _():
            gw_ref[...] = jnp.zeros_like(gw_ref)

        row = lax.broadcasted_iota(jnp.int32, (s, c), 0)
        b = zb[...].astype(F32)
        cm = zc[...].astype(F32)
        v = zv[...].astype(F32)
        cv = cm * v
        cv1 = _shift_down(cv, 1, row)
        cv2 = _shift_down(cv, 2, row)
        w0, w1, w2 = cw[pl.ds(0, 1), :], cw[pl.ds(1, 1), :], cw[pl.ds(2, 1), :]
        cc = w2 * cv + w1 * cv1 + w0 * cv2
        dq = dq_ref[...].astype(F32)
        dzo[0] = (dq * cc).astype(BF16)
        dcc = dq * b
        gw_ref[pl.ds(0, 1), :] += jnp.sum(dcc * cv2, axis=0, keepdims=True)
        gw_ref[pl.ds(1, 1), :] += jnp.sum(dcc * cv1, axis=0, keepdims=True)
        gw_ref[pl.ds(2, 1), :] += jnp.sum(dcc * cv, axis=0, keepdims=True)
        dcv = w2 * dcc + w1 * _shift_up(dcc, 1, row) + w0 * _shift_up(dcc, 2, row)
        dzo[1] = (dcv * v).astype(BF16)
        dzo[2] = (dcv * cm).astype(BF16)

    blocks = 4 * _nbytes((s, c), BF16) + _nbytes((3, s, c), BF16)
    return _pcall(
        body, name="conv_bwd", grid=(nb, nseq),
        in_specs=[ANY,
                  pl.BlockSpec((None, s, c), lambda j, b: (1, b, j)),
                  pl.BlockSpec((s, c), lambda j, b: (b, nb + j)),
                  pl.BlockSpec((s, c), lambda j, b: (b, 2 * nb + j)),
                  pl.BlockSpec((s, c), lambda j, b: (b, 3 * nb + j)),
                  pl.BlockSpec((3, c), lambda j, b: (0, j))],
        out_specs=[pl.BlockSpec((3, s, c), lambda j, b: (0, b, j)),
                   pl.BlockSpec((3, c), lambda j, b: (0, j))],
        out_shape=[SDS(dz.shape, BF16), SDS((3, d), F32)],
        input_output_aliases={0: 0},
        sem=("parallel", "arbitrary"), blocks=blocks, temps=16 * _nbytes((s, c), F32), comm=comm,
    )(dz, dpq, z, z, z, conv_w)


def _pool_bwd_call(dz, dpq, z, pool_w, pool_scale, nseq, comm=None):
    _, t, d = dz.shape
    s = t // nseq
    c = d // N_GROUPS

    def body(dz_in, dp_ref, zp, pw, ps, dzo, gpw_ref, gps_ref):
        del dz_in
        j = pl.program_id(0)

        @pl.when(pl.program_id(1) == 0)
        def _():
            gpw_ref[...] = jnp.zeros_like(gpw_ref)
            gps_ref[...] = jnp.zeros_like(gps_ref)

        row = lax.broadcasted_iota(jnp.int32, (s, c), 0)
        for gi, win in enumerate(POOL_WINDOWS):
            @pl.when(j == gi)
            def _(win=win):
                pb = _pool_fwd(zp[...].astype(F32), win, row).astype(BF16)
                plin = _dot(pb, pw[...])
                dps = dp_ref[...].astype(F32)
                gps_ref[...] += jnp.sum(dps * plin, axis=0, keepdims=True)
                dplb = (dps * ps[...]).astype(BF16)
                gpw_ref[...] += _dot_ta(pb, dplb)
                dzo[...] = _pool_bwd(_dot_tb(dplb, pw[...]), win, row).astype(BF16)

    blocks = 3 * _nbytes((s, c), BF16) + _nbytes((c, c), BF16) + _nbytes((c, c), F32)
    return _pcall(
        body, name="pool_bwd", grid=(N_GROUPS, nseq),
        in_specs=[ANY,
                  pl.BlockSpec((None, s, c), lambda j, b: (0, b, j)),
                  pl.BlockSpec((s, c), lambda j, b: (b, j)),
                  pl.BlockSpec((None, c, c), lambda j, b: (j, 0, 0)),
                  pl.BlockSpec((1, c), lambda j, b: (0, j))],
        out_specs=[pl.BlockSpec((None, s, c), lambda j, b: (3, b, j)),
                   pl.BlockSpec((None, c, c), lambda j, b: (j, 0, 0)),
                   pl.BlockSpec((1, c), lambda j, b: (0, j))],
        out_shape=[SDS(dz.shape, BF16), SDS((N_GROUPS, c, c), F32), SDS((1, d), F32)],
        input_output_aliases={0: 0},
        sem=("parallel", "arbitrary"), blocks=blocks, temps=10 * _nbytes((s, c), F32), comm=comm,
    )(dz, dpq, z, pool_w, pool_scale)


def _dz_plane(zb):
    return jnp.where(zb < 4, (zb + 3) % 4, zb)


def _wgrad_in(h1, dz, nsh, comm=None):
    t, d = h1.shape
    ws = N_SPLITS * d // nsh
    kb = _tile(math.gcd(d, ws), 512, LANES)
    npl = d // kb
    nps = ws // kb
    tk = _tile(t, 1024, SUBLANES_BF16)
    nk = t // tk

    def body(a_ref, b_ref, o_ref, acc):
        _acc_over(pl.program_id(1), nk, _dot_ta(a_ref[...], b_ref[...]), acc, o_ref)

    blocks = _nbytes((tk, d), BF16) + _nbytes((tk, kb), BF16) + _nbytes((d, kb), BF16)
    return _pcall(
        body, name="wgrad_in", grid=(N_SPLITS * npl, nk),
        in_specs=[pl.BlockSpec((tk, d), lambda cb, k: (k, 0)),
                  pl.BlockSpec((None, tk, kb), lambda cb, k: (_dz_plane(cb // npl), k, cb % npl))],
        out_specs=[pl.BlockSpec((None, d, kb), lambda cb, k: (cb // nps, 0, cb % nps))],
        out_shape=[SDS((nsh, d, ws), BF16)],
        scratch_shapes=[pltpu.VMEM((d, kb), F32)],
        sem=("parallel", "arbitrary"), blocks=blocks, temps=2 * _nbytes((d, kb), F32), comm=comm,
    )(h1, dz)


def _mixer_bwd_dx(dz, w_in, x, dx1, g_mix, comm=None):
    npln, t, d = dz.shape
    nsh, _, ws = w_in.shape
    tm = _tile(t, 256, SUBLANES_BF16)
    kb = _tile(math.gcd(d, ws), 512, LANES)
    npl = d // kb
    nps = ws // kb

    def body(dz_ref, w_ref, x_ref, dx1_ref, g_ref, dx_ref, gg_ref):
        @pl.when(pl.program_id(0) == 0)
        def _():
            gg_ref[...] = jnp.zeros_like(gg_ref)

        dh = None
        for cb in range(npln * npl):
            zb = cb // npl
            plane = (zb + 3) % 4 if zb < 4 else zb
            part = _dot_tb(dz_ref[plane, :, (cb % npl) * kb:(cb % npl + 1) * kb],
                           w_ref[cb // nps, :, (cb % nps) * kb:(cb % nps + 1) * kb])
            dh = part if dh is None else dh + part
        xh, inv = _rms_fwd(x_ref[...])
        gg_ref[...] += jnp.sum(dh * xh, axis=0, keepdims=True)
        dx_ref[...] = dx1_ref[...] + _rms_bwd(dh, xh, inv, g_ref[...])

    blocks = _nbytes((npln, tm, d), BF16) + 3 * _nbytes((tm, d), F32)
    return _pcall(
        body, name="mixer_bwd_dx", grid=(t // tm,),
        in_specs=[pl.BlockSpec((npln, tm, d), lambda i: (0, i, 0)),
                  pl.BlockSpec((nsh, d, ws), lambda i: (0, 0, 0), pipeline_mode=pl.Buffered(1)),
                  pl.BlockSpec((tm, d), lambda i: (i, 0)),
                  pl.BlockSpec((tm, d), lambda i: (i, 0)),
                  pl.BlockSpec((1, d), lambda i: (0, 0))],
        out_specs=[pl.BlockSpec((tm, d), lambda i: (i, 0)),
                   pl.BlockSpec((1, d), lambda i: (0, 0))],
        out_shape=[SDS((t, d), F32), SDS((1, d), F32)],
        sem=("arbitrary",), blocks=blocks, temps=_nbytes(w_in.shape, BF16) + 8 * _nbytes((tm, d), F32), comm=comm,
    )(dz, w_in, x, dx1, g_mix)


N_BIG = 5
SHARD_MAJOR = (0, 2)
ROWS_DIM1 = (1, 4)


def _ds(start, size, align):
    if isinstance(start, int):
        return pl.ds(start, size)
    return pl.ds(pl.multiple_of(start, align), size)


def _piece(a, ref, k, h):
    if a in SHARD_MAJOR:
        r = ref.shape[1] // 2
        return ref.at[k, _ds(h * r, r, SUBLANES_BF16), :]
    if a in ROWS_DIM1:
        r = ref.shape[1] // 8
        return ref.at[:, _ds((2 * k + h) * r, r, SUBLANES_BF16), :]
    r = ref.shape[0] // 8
    return ref.at[_ds((2 * k + h) * r, r, SUBLANES_BF16), :]


def _half(a, ref, h):
    if a in ROWS_DIM1:
        r = ref.shape[1] // 2
        return ref.at[:, _ds(h * r, r, SUBLANES_BF16), :]
    r = ref.shape[0] // 2
    return ref.at[_ds(h * r, r, SUBLANES_BF16), :]


def _piece_shape(a, full_shape):
    if a in SHARD_MAJOR:
        return (full_shape[1] // 2, full_shape[2])
    if a in ROWS_DIM1:
        return (full_shape[0], full_shape[1] // 8, full_shape[2])
    return (full_shape[0] // 8, full_shape[1])


def _shard_shape(a, full_shape):
    if a in SHARD_MAJOR:
        return (full_shape[1], full_shape[2])
    if a in ROWS_DIM1:
        return (full_shape[0], full_shape[1] // 4, full_shape[2])
    return (full_shape[0] // 4, full_shape[1])


def _rows_axis(a):
    return 1 if a in ROWS_DIM1 else 0


def _piece_block(a, full_shape):
    ps = _piece_shape(a, full_shape)
    if a in SHARD_MAJOR:
        return (None,) + ps, lambda k, c: (k, c, 0)
    if a in ROWS_DIM1:
        return ps, lambda k, c: (0, 2 * k + c, 0)
    return ps, lambda k, c: (2 * k + c, 0)


def _coords():
    return lax.axis_index("x"), lax.axis_index("y"), lax.axis_index("c")


def _peer_chips(x, y):
    return [(1 - x, y), (x, 1 - y), (1 - x, 1 - y)]


def _remote(src, dst, ssem, rsem, dev):
    return pltpu.make_async_remote_copy(src_ref=src, dst_ref=dst, send_sem=ssem, recv_sem=rsem,
                                        device_id=dev, device_id_type=MESH)


def _dma_sems(*counts):
    return [pltpu.SemaphoreType.DMA((n,)) for n in counts]


def _symmetric(ins, out_shapes, sems, copies, aliases=None):
    def start(cins, couts, csems):
        for cp in copies(cins, couts, csems):
            cp.start()

    def finish(cins, couts, csems):
        for cp in copies(cins, couts, csems):
            cp.wait()

    return _Comm(ins, out_shapes, sems, start, finish, aliases)


def _rows_part(a, ref, part):
    if part is None:
        return ref
    p, q, n = part
    ax = _rows_axis(a)
    r = ref.shape[ax] // n
    return ref.at[tuple(pl.ds(p * r, (q - p) * r) if d == ax else slice(None) for d in range(len(ref.shape)))]


def _merge(comms):
    ins, outs, sems, aliases, spans = [], [], [], {}, []
    for cm in comms:
        spans.append((len(ins), len(outs), len(sems)))
        for i, o in cm.aliases.items():
            aliases[len(ins) + i] = len(outs) + o
        ins += cm.ins
        outs += cm.out_shapes
        sems += cm.sems

    def each(fn_name):
        def run(cins, couts, csems):
            for cm, (i0, o0, s0) in zip(comms, spans):
                getattr(cm, fn_name)(cins[i0:i0 + len(cm.ins)], couts[o0:o0 + len(cm.out_shapes)],
                                     csems[s0:s0 + len(cm.sems)])
        return run

    return _Comm(ins, outs, sems, each("start"), each("finish"), aliases)


def _gather_comm(arrs, locs, full_shapes, part=None, into=None):
    n = len(arrs)

    def own(cins, couts, csems):
        x, y, c = _coords()
        j = 2 * x + y
        return [_remote(_rows_part(a, _half(a, cins[q], h), part), _rows_part(a, _piece(a, couts[q], j, h), part),
                        csems[0].at[2 * q + h], csems[1].at[2 * q + h], (x, y, 1 - c))
                for q, a in enumerate(arrs) for h in range(2)]

    def sends(cins, couts, csems):
        x, y, c = _coords()
        j = 2 * x + y
        return [_remote(_rows_part(a, _half(a, cins[q], c), part), _rows_part(a, _piece(a, couts[q], j, c), part),
                        csems[2].at[3 * q + i], csems[3].at[3 * q + i], (px, py, c))
                for q, a in enumerate(arrs) for i, (px, py) in enumerate(_peer_chips(x, y))]

    def forwards(couts, csems, half_of):
        x, y, c = _coords()
        out = []
        for q, a in enumerate(arrs):
            for i, (px, py) in enumerate(_peer_chips(x, y)):
                landed = _rows_part(a, _piece(a, couts[q], 2 * px + py, half_of(c)), part)
                out.append(_remote(landed, landed, csems[4].at[3 * q + i], csems[5].at[3 * q + i], (x, y, 1 - c)))
        return out

    def start(cins, couts, csems):
        for cp in sends(cins, couts, csems) + own(cins, couts, csems):
            cp.start()

    def finish(cins, couts, csems):
        fw = forwards(couts, csems, lambda c: c)
        for cp, f in zip(sends(cins, couts, csems), fw):
            cp.wait_recv()
            f.start()
        for f in forwards(couts, csems, lambda c: 1 - c):
            f.wait_recv()
        for cp in sends(cins, couts, csems) + fw:
            cp.wait_send()
        for cp in own(cins, couts, csems):
            cp.wait()

    ins = [locs[a] for a in arrs] + ([into[a] for a in arrs] if into else [])
    return _Comm(ins, [SDS(full_shapes[a], BF16) for a in arrs],
                 _dma_sems(2 * n, 2 * n, 3 * n, 3 * n, 3 * n, 3 * n), start, finish,
                 aliases={n + q: q for q in range(n)} if into else None)


def _halves_comm(arrs, gbs):
    n = len(arrs)

    def copies(cins, couts, csems):
        x, y, c = _coords()
        return [_remote(_piece(a, cins[q], k, 1 - c), couts[q].at[k], csems[0].at[4 * q + k], csems[1].at[4 * q + k],
                        (x, y, 1 - c)) for q, a in enumerate(arrs) for k in range(4)]

    return _symmetric([gbs[a] for a in arrs], [SDS((4,) + _piece_shape(a, gbs[a].shape), BF16) for a in arrs],
                      _dma_sems(4 * n, 4 * n), copies)


def _chips_comm(arrs, ps, part=None, into=None):
    n = len(arrs)

    def copies(cins, couts, csems):
        x, y, c = _coords()
        return [_remote(_rows_part(a, cins[q].at[2 * px + py], part), _rows_part(a, couts[q].at[i], part),
                        csems[0].at[3 * q + i], csems[1].at[3 * q + i], (px, py, c))
                for q, a in enumerate(arrs) for i, (px, py) in enumerate(_peer_chips(x, y))]

    ins = [ps[a] for a in arrs] + ([into[a] for a in arrs] if into else [])
    return _symmetric(ins, [SDS((3,) + ps[a].shape[1:], BF16) for a in arrs], _dma_sems(3 * n, 3 * n), copies,
                      aliases={n + q: q for q in range(n)} if into else None)


def _result_comm(arrs, gs):
    n = len(arrs)

    def copies(cins, couts, csems):
        x, y, c = _coords()
        return [_remote(_half(a, cins[q], c), _half(a, couts[q], c), csems[0].at[q], csems[1].at[q], (x, y, 1 - c))
                for q, a in enumerate(arrs)]

    return _symmetric([gs[a] for a in arrs], [SDS(gs[a].shape, F32) for a in arrs], _dma_sems(n, n), copies,
                      aliases={q: q for q in range(n)})


def _add_halves(arrs, gbs, lands, c_arr, name):
    n = len(arrs)

    def body(c_ref, *refs):
        del c_ref
        for q in range(n):
            refs[2 * n + q][...] = (refs[q][...].astype(F32) + refs[n + q][...].astype(F32)).astype(BF16)

    g_specs, l_specs, o_specs, blocks = [], [], [], 0
    for a in arrs:
        bs, imap = _piece_block(a, gbs[a].shape)
        ps = _piece_shape(a, gbs[a].shape)
        g_specs.append(pl.BlockSpec(bs, lambda k, c_ref, imap=imap: imap(k, c_ref[0])))
        nd = len(ps)
        l_specs.append(pl.BlockSpec((None,) + ps, lambda k, c_ref, nd=nd: (k,) + (0,) * nd))
        o_specs.append(pl.BlockSpec((None,) + ps, lambda k, c_ref, nd=nd: (k,) + (0,) * nd))
        blocks += 3 * _nbytes(ps, BF16)
    return list(pl.pallas_call(
        body, name=name,
        grid_spec=pltpu.PrefetchScalarGridSpec(
            num_scalar_prefetch=1, grid=(4,), in_specs=g_specs + l_specs, out_specs=o_specs),
        out_shape=[SDS((4,) + _piece_shape(a, gbs[a].shape), BF16) for a in arrs],
        compiler_params=_params(("parallel",), blocks, blocks),
    )(c_arr, *[gbs[a] for a in arrs], *lands))


def _sum_chips(a, p, land, shard_shape, jc_arr, name):
    ps = land.shape[1:]
    ax = _rows_axis(a)
    rows = ps[ax]
    nsub = 2 if rows % (2 * SUBLANES_BF16) == 0 else 1
    bs = tuple(r // nsub if q == ax else r for q, r in enumerate(ps))
    nd = len(ps)

    def at_rows(v):
        return tuple(v if q == ax else 0 for q in range(nd))

    def body(jc_ref, p_ref, l_ref, o_ref):
        del jc_ref
        acc = p_ref[...].astype(F32) + l_ref[0].astype(F32)
        acc = acc + l_ref[1].astype(F32)
        o_ref[...] = acc + l_ref[2].astype(F32)

    blocks = 4 * _nbytes(bs, BF16) + _nbytes(bs, F32)
    return pl.pallas_call(
        body, name=name,
        grid_spec=pltpu.PrefetchScalarGridSpec(
            num_scalar_prefetch=1, grid=(nsub,),
            in_specs=[pl.BlockSpec((None,) + bs, lambda s, jc: (jc[0],) + at_rows(s)),
                      pl.BlockSpec((3,) + bs, lambda s, jc: (0,) + at_rows(s))],
            out_specs=pl.BlockSpec(bs, lambda s, jc: at_rows(jc[1] * nsub + s))),
        out_shape=SDS(shard_shape, F32),
        compiler_params=_params(("parallel",), blocks, 2 * _nbytes(bs, F32)),
    )(jc_arr, p, land)


def _small_comm(v):
    rows = v.shape[0]

    def copies(cins, couts, csems):
        x, y, c = _coords()
        me = 4 * x + 2 * y + c
        out = [pltpu.make_async_copy(cins[0], couts[0].at[me], csems[0].at[0])]
        for dlt in range(1, 8):
            px = 1 - x if (dlt >> 2) & 1 else x
            py = 1 - y if (dlt >> 1) & 1 else y
            pc = 1 - c if dlt & 1 else c
            out.append(_remote(cins[0], couts[0].at[me], csems[1].at[dlt - 1], csems[2].at[dlt - 1], (px, py, pc)))
        return out

    return _symmetric([v], [SDS((8, rows, LANES), F32)], _dma_sems(1, 7, 7), copies)


def _sum8(slots, name):
    def body(s_ref, o_ref):
        acc = s_ref[0]
        for i in range(1, 8):
            acc = acc + s_ref[i]
        o_ref[...] = acc

    return pl.pallas_call(
        body, name=name,
        in_specs=[pl.BlockSpec(memory_space=pltpu.VMEM)], out_specs=pl.BlockSpec(memory_space=pltpu.VMEM),
        out_shape=SDS(slots.shape[1:], F32),
    )(slots)


def _adamw(w, g, m, v, name, g_plane=None):
    rows, cols = w.shape
    tr = _tile(rows, max(SUBLANES_F32, (256 * 1024 // cols) // SUBLANES_F32 * SUBLANES_F32), SUBLANES_F32)

    def body(w_ref, g_ref, m_ref, v_ref, go_ref, d_ref, mo_ref, vo_ref):
        gr = g_ref[...]
        mn = ADAM_B1 * m_ref[...] + (1.0 - ADAM_B1) * gr
        vn = ADAM_B2 * v_ref[...] + (1.0 - ADAM_B2) * (gr * gr)
        m_hat = mn / (1.0 - ADAM_B1 ** ADAM_STEP)
        v_hat = vn / (1.0 - ADAM_B2 ** ADAM_STEP)
        d_ref[...] = -ADAM_LR * (m_hat / (jnp.sqrt(v_hat) + ADAM_EPS) + ADAM_WD * w_ref[...])
        go_ref[...] = gr
        mo_ref[...] = mn
        vo_ref[...] = vn

    spec = pl.BlockSpec((tr, cols), lambda i: (i, 0))
    g_spec = spec if g_plane is None else pl.BlockSpec((None, tr, cols), lambda i: (g_plane, i, 0))
    return pl.pallas_call(
        body, name=name, grid=(rows // tr,),
        in_specs=[spec, g_spec, spec, spec], out_specs=[spec, spec, spec, spec],
        out_shape=[SDS((rows, cols), F32)] * 4,
        compiler_params=_params(("parallel",), 8 * _nbytes((tr, cols), F32), 4 * _nbytes((tr, cols), F32)),
    )(w, g, m, v)


def _pack(parts):
    rows = []
    for p in parts:
        r = p.reshape(-1, LANES)
        pad = (-r.shape[0]) % SUBLANES_F32
        if pad:
            r = jnp.pad(r, ((0, pad), (0, 0)))
        rows.append(r)
    return jnp.concatenate(rows, axis=0)


def _unpack(packed, shapes):
    out, at = [], 0
    for s in shapes:
        n = 1
        for q in s:
            n *= q
        r = n // LANES
        out.append(packed[at:at + r].reshape(s))
        at += r + (-r) % SUBLANES_F32
    return out


def kernel(x, norm_mix, w_in, pool_w, pool_scale, w_pool_proj, conv_w, w_conv_out, w_o, norm_ffn, w_up, ffn_conv_w, ffn_conv_b, w_down, norm_final, loss_target, m_norm_mix, m_w_in, m_pool_w, m_pool_scale, m_w_pool_proj, m_conv_w, m_w_conv_out, m_w_o, m_norm_ffn, m_w_up, m_ffn_conv_w, m_ffn_conv_b, m_w_down, m_norm_final, v_norm_mix, v_w_in, v_pool_w, v_pool_scale, v_w_pool_proj, v_conv_w, v_w_conv_out, v_w_o, v_norm_ffn, v_w_up, v_ffn_conv_w, v_ffn_conv_b, v_w_down, v_norm_final):
    nseq, seq, d = x.shape
    t = nseq * seq
    f = w_down.shape[1] * 4
    c = d // N_GROUPS
    xy = lax.axis_index("x") * 2 + lax.axis_index("y")
    c_arr = lax.axis_index("c").astype(jnp.int32).reshape(1)
    jc_arr = jnp.stack([xy, lax.axis_index("c")]).astype(jnp.int32)
    nsh = 4
    zero = jnp.zeros((), jnp.int32)

    locs = [w_in[0].astype(BF16),
            jnp.stack([w_pool_proj[0], w_conv_out[0], w_o[0]]).astype(BF16),
            w_up[0].astype(BF16), w_down[0].astype(BF16), pool_w[0].astype(BF16)]
    full_shapes = [(nsh, d, N_SPLITS * d // nsh), (3, d, d), (nsh, d, 2 * f // nsh), (f, d), (N_GROUPS, c, c)]

    cw_pad = lax.dynamic_update_slice(jnp.zeros((3, d), F32), conv_w[0], (zero, xy * (d // 4)))
    fw_pad = lax.dynamic_update_slice(jnp.zeros((3, 2 * f), F32), ffn_conv_w[0], (zero, xy * (f // 2)))
    small_w = _pack([cw_pad, fw_pad]) * 0.5

    x2d = x.reshape(t, d)
    tgt = loss_target.reshape(t, d)
    ax, ay = lax.axis_index("x"), lax.axis_index("y")
    order = jnp.stack([xy, 2 * (1 - ax) + ay, 2 * ax + 1 - ay, 2 * (1 - ax) + 1 - ay]).astype(jnp.int32)
    (z, h1, w_in_f), (pool_w_f, w3_f, slots_w) = _fwd_in(
        x2d, norm_mix, locs[0], order,
        _merge([_gather_comm([4], locs, full_shapes), _gather_comm([1], locs, full_shapes, part=(0, 1, 2)),
                _small_comm(small_w)]))
    conv_w_f, ffn_cw_f = _unpack(_sum8(slots_w, "sum8_weights"), [(3, d), (3, 2 * f)])
    ffn_cw_p = ffn_cw_f.reshape(3, 2, f).transpose(1, 0, 2)
    ffn_cb_p = ffn_conv_b.reshape(2, 1, f)
    (lhs3,), (w3_f,) = _mixer_mid_fwd(z, pool_w_f, pool_scale, conv_w_f, nseq,
                                      _gather_comm([1], locs, full_shapes, part=(1, 2, 2), into={1: w3_f}))
    (lhs3, ypc, x1, h2), (w_up_f,) = _mixer_out(lhs3, z, x2d, w3_f, norm_ffn, _gather_comm([2], locs, full_shapes))
    (u0,), (w_down_f,) = _ffn_up(h2, w_up_f, f, _gather_comm([3], locs, full_shapes))
    act, ua = _ffn_mid_fwd(u0, ffn_cw_p, ffn_cb_p, nseq)
    dx2, dx2b, loss11, g_norm_final = _ffn_down_loss(act, w_down_f, x1, tgt, norm_final.reshape(1, d))

    gbs, lands, ps, lands2, rs = {}, {}, {}, {}, {}
    tn_up = _tile(2 * f // nsh, 1408, LANES)
    npp = f // tn_up

    def add(arrs, name):
        for a, p in zip(arrs, _add_halves(arrs, gbs, [lands[a] for a in arrs], c_arr, name)):
            ps[a] = p

    def summed(a):
        rs[a] = _sum_chips(a, ps[a], lands2[a], _shard_shape(a, full_shapes[a]), jc_arr, "sum_chips_%d" % a)

    (gbs[3],), _ = _wgrad(act, dx2b, "wgrad_down", tr=tn_up, tn=d)
    (da,), (lands[3],) = _ffn_bwd_da(dx2b, w_down_f, _halves_comm([3], gbs))
    add([3], "add_halves_down")
    (du0, g_ffn_cw_p, g_ffn_cb_p), (lands2[3],) = _ffn_mid_bwd(da, u0, ua, ffn_cw_p, nseq, _chips_comm([3], ps))
    summed(3)
    (gbs[2],), (rs[3],) = _wgrad(h2, du0, "wgrad_up", tr=d, tn=tn_up, b_plane_of=lambda n: (n // npp, n % npp),
                                 out_shards=nsh, comm=_result_comm([3], rs))
    (dx1, rhs3, g_norm_ffn), (lands[2],) = _ffn_bwd_dx1(du0, w_up_f, x1, dx2, norm_ffn, 3, _halves_comm([2], gbs))
    add([2], "add_halves_up")
    (rhs3, dz, dpq), (lands2[2],) = _mixer_bwd(rhs3, z, ypc, w3_f, _chips_comm([2], ps, part=(0, 1, 2)))
    (gbs[1],), (lands2[2],) = _wgrad3(lhs3, rhs3, _chips_comm([2], ps, part=(1, 2, 2), into=lands2))
    summed(2)
    (dz, g_conv_w), (lands[1], rs[2]) = _conv_bwd(dz, dpq, z, conv_w_f, nseq,
                                                  _merge([_halves_comm([1], gbs), _result_comm([2], rs)]))
    add([1], "add_halves_sq3")
    (dz, g_pool_w, g_pool_scale), _ = _pool_bwd_call(dz, dpq, z, pool_w_f, pool_scale, nseq)
    gbs[4] = g_pool_w.astype(BF16)
    (gbs[0],), (lands2[1],) = _wgrad_in(h1, dz, nsh, _chips_comm([1], ps))
    summed(1)
    lands[0], lands[4] = _run_comm(_halves_comm([0, 4], gbs), "exchange_halves_in")
    add([0, 4], "add_halves_in")
    g_ffn_cw = g_ffn_cw_p.transpose(1, 0, 2).reshape(3, 2 * f)
    small_a = _pack([g_pool_scale, g_norm_ffn, g_ffn_cb_p.reshape(1, 2 * f), g_norm_final.reshape(d), g_conv_w,
                     g_ffn_cw, jnp.pad(loss11, ((0, SUBLANES_F32 - 1), (0, LANES - 1)))])
    (grad_x, g_norm_mix), (lands2[0], lands2[4], rs[1], slots_a) = _mixer_bwd_dx(
        dz, w_in_f, x2d, dx1, norm_mix,
        _merge([_chips_comm([0, 4], ps), _result_comm([1], rs), _small_comm(small_a)]))
    summed(0)
    summed(4)
    rs[0], rs[4], slots_b = _run_comm(_merge([_result_comm([0, 4], rs), _small_comm(_pack([g_norm_mix]))]),
                                      "exchange_result_in")
    shapes_a = [(1, d), (1, d), (1, 2 * f), (d,), (3, d), (3, 2 * f), (SUBLANES_F32, LANES)]
    gs_pool_scale, gs_norm_ffn, gs_ffn_cb, gs_norm_final, gs_conv_w, gs_ffn_cw, loss_blk = _unpack(
        _sum8(slots_a, "sum8_grads"), shapes_a)
    (gs_norm_mix,) = _unpack(_sum8(slots_b, "sum8_norm_mix"), [(1, d)])
    gs_conv_w = lax.dynamic_slice(gs_conv_w, (zero, xy * (d // 4)), (3, d // 4))
    gs_ffn_cw = lax.dynamic_slice(gs_ffn_cw, (zero, xy * (f // 2)), (3, f // 2))

    def upd(w, g, m, v, name, g_plane=None):
        shape = w.shape
        rows = 1
        for q in shape[:-1]:
            rows *= q
        g2 = g if g_plane is not None else g.reshape(rows, shape[-1])
        outs = _adamw(w.reshape(rows, shape[-1]), g2, m.reshape(rows, shape[-1]), v.reshape(rows, shape[-1]),
                      name, g_plane)
        return [o.reshape(shape) for o in outs]

    res = {
        "w_in": upd(w_in, rs[0], m_w_in, v_w_in, "adamw_w_in"),
        "pool_w": upd(pool_w, rs[4], m_pool_w, v_pool_w, "adamw_pool_w"),
        "w_pool_proj": upd(w_pool_proj, rs[1], m_w_pool_proj, v_w_pool_proj, "adamw_w_pool_proj", 0),
        "w_conv_out": upd(w_conv_out, rs[1], m_w_conv_out, v_w_conv_out, "adamw_w_conv_out", 1),
        "w_o": upd(w_o, rs[1], m_w_o, v_w_o, "adamw_w_o", 2),
        "w_up": upd(w_up, rs[2], m_w_up, v_w_up, "adamw_w_up"),
        "w_down": upd(w_down, rs[3], m_w_down, v_w_down, "adamw_w_down"),
    }

    small_names = ["norm_mix", "pool_scale", "norm_ffn", "ffn_conv_b", "norm_final", "conv_w", "ffn_conv_w"]
    small_ws = [norm_mix, pool_scale, norm_ffn, ffn_conv_b, norm_final, conv_w, ffn_conv_w]
    small_ms = [m_norm_mix, m_pool_scale, m_norm_ffn, m_ffn_conv_b, m_norm_final, m_conv_w, m_ffn_conv_w]
    small_vs = [v_norm_mix, v_pool_scale, v_norm_ffn, v_ffn_conv_b, v_norm_final, v_conv_w, v_ffn_conv_w]
    small_gs = [gs_norm_mix, gs_pool_scale, gs_norm_ffn, gs_ffn_cb, gs_norm_final, gs_conv_w, gs_ffn_cw]
    _, sd, sm, sv = _adamw(_pack(small_ws), _pack(small_gs), _pack(small_ms), _pack(small_vs), "adamw_small")
    shapes = [w.shape for w in small_ws]
    sd, sm, sv = _unpack(sd, shapes), _unpack(sm, shapes), _unpack(sv, shapes)
    for i, nm in enumerate(small_names):
        res[nm] = [small_gs[i].reshape(shapes[i]), sd[i], sm[i], sv[i]]

    order = ["norm_mix", "w_in", "pool_w", "pool_scale", "w_pool_proj", "conv_w", "w_conv_out", "w_o", "norm_ffn",
             "w_up", "ffn_conv_w", "ffn_conv_b", "w_down", "norm_final"]
    return (loss_blk[0, 0], grad_x.reshape(x.shape), *[res[n][0] for n in order], *[res[n][1] for n in order],
            *[res[n][2] for n in order], *[res[n][3] for n in order])
```

```python
import math

import jax
import jax.numpy as jnp
from jax import lax
from jax.experimental import pallas as pl
from jax.experimental.pallas import tpu as pltpu

F32 = jnp.float32
BF16 = jnp.bfloat16
SDS = jax.ShapeDtypeStruct
MESH = pl.DeviceIdType.MESH

RMS_EPS = 1e-6
POOL_WINDOWS = (2, 4, 8, 16)
N_GROUPS = len(POOL_WINDOWS)
N_SPLITS = 6

ADAM_LR = 0.001
ADAM_B1 = 0.9
ADAM_B2 = 0.999
ADAM_EPS = 1e-08
ADAM_WD = 0.01
ADAM_STEP = 10

LANES = 128
SUBLANES_F32 = 8
SUBLANES_BF16 = 16
VMEM_BYTES = 64 * 1024 * 1024
VMEM_CAP = VMEM_BYTES - 8 * 1024 * 1024
VMEM_FLOOR = 16 * 1024 * 1024

ANY = pl.BlockSpec(memory_space=pl.ANY)


def _tile(dim, pref, align):
    if dim <= pref:
        return dim
    t = (pref // align) * align
    while t >= align:
        if dim % t == 0:
            return t
        t -= align
    return dim


def _nbytes(shape, dtype):
    n = 1
    for s in shape:
        n *= s
    return n * jnp.dtype(dtype).itemsize


def _params(sem, block_bytes, temp_bytes=0):
    need = 2 * block_bytes + temp_bytes + 4 * 1024 * 1024
    return pltpu.CompilerParams(dimension_semantics=sem, vmem_limit_bytes=int(min(max(need, VMEM_FLOOR), VMEM_CAP)))


class _Comm:
    def __init__(self, ins, out_shapes, sems, start, finish, aliases=None):
        self.ins = list(ins)
        self.out_shapes = list(out_shapes)
        self.sems = list(sems)
        self.start = start
        self.finish = finish
        self.aliases = dict(aliases or {})


def _pcall(body, *, name, grid, in_specs, out_specs, out_shape, sem, blocks, temps=0, scratch_shapes=(),
           input_output_aliases=None, comm=None):
    in_specs = list(in_specs)
    out_specs = list(out_specs)
    out_shape = list(out_shape)
    scratch_shapes = list(scratch_shapes)
    aliases = dict(input_output_aliases or {})
    n_in, n_out, n_scr = len(in_specs), len(out_shape), len(scratch_shapes)
    if comm is None:
        call = pl.pallas_call(
            body, name=name, grid=grid, in_specs=in_specs, out_specs=out_specs, out_shape=out_shape,
            scratch_shapes=scratch_shapes, input_output_aliases=aliases,
            compiler_params=_params(sem, blocks, temps))
        return lambda *args: (list(call(*args)), [])

    nci, nco = len(comm.ins), len(comm.out_shapes)

    def hosted(*refs):
        ins = refs[:n_in]
        cins = refs[n_in:n_in + nci]
        outs = refs[n_in + nci:n_in + nci + n_out]
        couts = refs[n_in + nci + n_out:n_in + nci + n_out + nco]
        scr = refs[n_in + nci + n_out + nco:n_in + nci + n_out + nco + n_scr]
        csems = refs[n_in + nci + n_out + nco + n_scr:]
        first = None
        last = None
        for q, g in enumerate(grid):
            pid = pl.program_id(q)
            first = (pid == 0) if first is None else first & (pid == 0)
            last = (pid == g - 1) if last is None else last & (pid == g - 1)

        @pl.when(first)
        def _():
            comm.start(cins, couts, csems)

        body(*ins, *outs, *scr)

        @pl.when(last)
        def _():
            comm.finish(cins, couts, csems)

    for i, o in comm.aliases.items():
        aliases[n_in + i] = n_out + o
    call = pl.pallas_call(
        hosted, name=name, grid=grid, in_specs=in_specs + [ANY] * nci, out_specs=out_specs + [ANY] * nco,
        out_shape=out_shape + comm.out_shapes, scratch_shapes=scratch_shapes + comm.sems,
        input_output_aliases=aliases,
        compiler_params=_params(("arbitrary",) * len(grid), blocks, temps))

    def run(*args):
        res = call(*args, *comm.ins)
        return list(res[:n_out]), list(res[n_out:])

    return run


def _run_comm(comm, name):
    def body(*refs):
        nci, nco = len(comm.ins), len(comm.out_shapes)
        cins, couts, csems = refs[:nci], refs[nci:nci + nco], refs[nci + nco:]
        comm.start(cins, couts, csems)
        comm.finish(cins, couts, csems)

    return list(pl.pallas_call(
        body, name=name, in_specs=[ANY] * len(comm.ins), out_specs=[ANY] * len(comm.out_shapes),
        out_shape=comm.out_shapes, scratch_shapes=comm.sems, input_output_aliases=comm.aliases,
    )(*comm.ins))


def _dot(a, b):
    return jnp.dot(a, b, preferred_element_type=F32)


def _dot_tb(a, b):
    return lax.dot_general(a, b, (((1,), (1,)), ((), ())), preferred_element_type=F32)


def _dot_ta(a, b):
    return lax.dot_general(a, b, (((0,), (0,)), ((), ())), preferred_element_type=F32)


def _rms_fwd(x):
    inv = lax.rsqrt(jnp.mean(x * x, axis=-1, keepdims=True) + RMS_EPS)
    return x * inv, inv


def _rms_bwd(dy, xhat, inv, g):
    gd = dy * g
    return inv * (gd - xhat * jnp.mean(gd * xhat, axis=-1, keepdims=True))


def _sigmoid(x):
    return 1.0 / (1.0 + jnp.exp(-x))


def _shift_down(x, k, row):
    return jnp.where(row >= k, pltpu.roll(x, k, 0), 0.0)


def _shift_up(x, k, row):
    s = x.shape[0]
    return jnp.where(row < s - k, pltpu.roll(x, s - k, 0), 0.0)


def _pool_fwd(u, win, row):
    s = u
    k = 1
    while k < win:
        s = s + _shift_down(s, k, row)
        k *= 2
    cnt = jnp.minimum(row + 1, win).astype(F32)
    return s / cnt - u


def _pool_bwd(dp, win, row):
    cnt = jnp.minimum(row + 1, win).astype(F32)
    s = dp / cnt
    k = 1
    while k < win:
        s = s + _shift_up(s, k, row)
        k *= 2
    return s - dp


def _acc_over(k, nk, part, acc, o_ref):
    @pl.when(k == 0)
    def _():
        acc[...] = part

    @pl.when(k > 0)
    def _():
        acc[...] += part

    @pl.when(k == nk - 1)
    def _():
        o_ref[...] = acc[...].astype(o_ref.dtype)


def _fwd_in(x, g, w_loc, order, comm):
    t, d = x.shape
    ws = w_loc.shape[1]
    nsh = order.shape[0]
    tm = _tile(t, 1024, SUBLANES_BF16)
    ni = t // tm
    nci, nco = len(comm.ins), len(comm.out_shapes)

    def body(order_ref, x_ref, g_ref, loc_ref, *rest):
        del order_ref
        cins = rest[:nci]
        z_ref, h_ref, full_ref = rest[nci:nci + 3]
        couts = rest[nci + 3:nci + 3 + nco]
        hs, wbuf, wsem, own_s, own_r, snd_s, snd_r, fwd_s, fwd_r = rest[nci + 3 + nco:nci + 12 + nco]
        csems = rest[nci + 12 + nco:]
        j = pl.program_id(0)
        i = pl.program_id(1)
        x_, y_, c_ = _coords()
        own = 2 * x_ + y_
        sib = (x_, y_, 1 - c_)
        peers = _peer_chips(x_, y_)

        def sends():
            return [_remote(_half(0, loc_ref, c_), _piece(0, full_ref, own, c_), snd_s.at[p], snd_r.at[p], (px, py, c_))
                    for p, (px, py) in enumerate(peers)]

        def owns():
            return [_remote(_half(0, loc_ref, h), _piece(0, full_ref, own, h), own_s.at[h], own_r.at[h], sib)
                    for h in range(2)]

        def forward(p, half):
            px, py = peers[p]
            landed = _piece(0, full_ref, 2 * px + py, half)
            return _remote(landed, landed, fwd_s.at[p], fwd_r.at[p], sib)

        def load(src, slot):
            return pltpu.make_async_copy(src, wbuf.at[slot], wsem.at[slot])

        @pl.when((j == 0) & (i == 0))
        def _():
            for cp in sends() + owns():
                cp.start()
            load(loc_ref, 0).start()
            comm.start(cins, couts, csems)

        @pl.when(j == 0)
        def _():
            xh, _ = _rms_fwd(x_ref[...])
            h = (xh * g_ref[...]).astype(BF16)
            hs[pl.ds(pl.multiple_of(i * tm, tm), tm), :] = h
            h_ref[...] = h

        slot = j % 2

        @pl.when(i == 0)
        def _():
            load(loc_ref, slot).wait()

        z_ref[...] = _dot(hs[pl.ds(pl.multiple_of(i * tm, tm), tm), :], wbuf[slot]).astype(BF16)

        for p in range(nsh - 1):
            @pl.when((i == ni - 1) & (j == p))
            def _(p=p):
                px, py = peers[p]
                sends()[p].wait_recv()
                forward(p, c_).start()
                forward(p, 1 - c_).wait_recv()
                load(full_ref.at[2 * px + py], 1 - slot).start()

        @pl.when((j == nsh - 1) & (i == ni - 1))
        def _():
            for cp in sends() + [forward(p, c_) for p in range(nsh - 1)]:
                cp.wait_send()
            for cp in owns():
                cp.wait()
            comm.finish(cins, couts, csems)

    last = ni - 1
    blocks = _nbytes((tm, d), F32) + _nbytes((tm, ws), BF16) + _nbytes((tm, d), BF16)
    scratch = _nbytes((t, d), BF16) + 2 * _nbytes((d, ws), BF16)
    res = pl.pallas_call(
        body, name="fwd_in",
        grid_spec=pltpu.PrefetchScalarGridSpec(
            num_scalar_prefetch=1, grid=(nsh, ni),
            in_specs=[pl.BlockSpec((tm, d), lambda j, i, o: (jnp.where(j == 0, i, last), 0)),
                      pl.BlockSpec((1, d), lambda j, i, o: (0, 0)), ANY] + [ANY] * nci,
            out_specs=[pl.BlockSpec((tm, ws), lambda j, i, o: (i, o[j])),
                       pl.BlockSpec((tm, d), lambda j, i, o: (jnp.where(j == 0, i, last), 0)), ANY] + [ANY] * nco,
            scratch_shapes=[pltpu.VMEM((t, d), BF16), pltpu.VMEM((2, d, ws), BF16)]
            + _dma_sems(2, 2, 2, nsh - 1, nsh - 1, nsh - 1, nsh - 1) + comm.sems),
        out_shape=[SDS((t, nsh * ws), BF16), SDS((t, d), BF16), SDS((nsh, d, ws), BF16)] + comm.out_shapes,
        input_output_aliases={4 + i: 3 + o for i, o in comm.aliases.items()},
        compiler_params=_params(("arbitrary", "arbitrary"), blocks, scratch + 3 * _nbytes((tm, d), F32)),
    )(order, x, g, w_loc, *comm.ins)
    return list(res[:3]), list(res[3:])


def _mixer_mid_fwd(z, pool_w, pool_scale, conv_w, nseq, comm=None):
    t = z.shape[0]
    d = pool_scale.shape[1]
    s = t // nseq
    c = d // N_GROUPS

    def body(zp, zb, zc, zv, pw, ps, cw, o):
        j = pl.program_id(1)
        row = lax.broadcasted_iota(jnp.int32, (s, c), 0)
        for gi, win in enumerate(POOL_WINDOWS):
            @pl.when(j == gi)
            def _(win=win):
                pooled = _pool_fwd(zp[...].astype(F32), win, row)
                o[0] = (_dot(pooled.astype(BF16), pw[...]) * ps[...]).astype(BF16)

        cv = zc[...].astype(F32) * zv[...].astype(F32)
        cc = (cw[pl.ds(2, 1), :] * cv + cw[pl.ds(1, 1), :] * _shift_down(cv, 1, row)
              + cw[pl.ds(0, 1), :] * _shift_down(cv, 2, row))
        o[1] = (zb[...].astype(F32) * cc).astype(BF16)

    blocks = 4 * _nbytes((s, c), BF16) + _nbytes((c, c), BF16) + _nbytes((2, s, c), BF16)
    return _pcall(
        body, name="mixer_mid_fwd", grid=(nseq, N_GROUPS),
        in_specs=[pl.BlockSpec((s, c), lambda b, j: (b, j)),
                  pl.BlockSpec((s, c), lambda b, j: (b, N_GROUPS + j)),
                  pl.BlockSpec((s, c), lambda b, j: (b, 2 * N_GROUPS + j)),
                  pl.BlockSpec((s, c), lambda b, j: (b, 3 * N_GROUPS + j)),
                  pl.BlockSpec((None, c, c), lambda b, j: (j, 0, 0)),
                  pl.BlockSpec((1, c), lambda b, j: (0, j)),
                  pl.BlockSpec((3, c), lambda b, j: (0, j))],
        out_specs=[pl.BlockSpec((2, s, c), lambda b, j: (0, b, j))],
        out_shape=[SDS((3, t, d), BF16)],
        sem=("parallel", "parallel"), blocks=blocks, temps=8 * _nbytes((s, c), F32), comm=comm,
    )(z, z, z, z, pool_w, pool_scale, conv_w)


def _mixer_out(lhs3, z, x, w3, g_ffn, comm=None):
    t, d = x.shape
    tm = _tile(t, 256, SUBLANES_BF16)

    def body(pq, zgp, zgc, x_ref, w_ref, g_ref, mrg, ypc, x1o, h2o):
        yp = _dot(pq[0], w_ref[0])
        yc = _dot(pq[1], w_ref[1])
        m = _sigmoid(zgp[...].astype(F32)) * yp + _sigmoid(zgc[...].astype(F32)) * yc
        mb = m.astype(BF16)
        x1 = x_ref[...] + _dot(mb, w_ref[2])
        ypc[0] = yp.astype(BF16)
        ypc[1] = yc.astype(BF16)
        mrg[...] = mb
        x1o[...] = x1
        xh, _ = _rms_fwd(x1)
        h2o[...] = (xh * g_ref[...]).astype(BF16)

    blocks = (_nbytes((2, tm, d), BF16) * 2 + _nbytes((tm, d), BF16) * 4 + _nbytes((tm, d), F32) * 2
              + _nbytes((3, d, d), BF16))
    return _pcall(
        body, name="mixer_out", grid=(t // tm,),
        in_specs=[pl.BlockSpec((2, tm, d), lambda i: (0, i, 0)),
                  pl.BlockSpec((tm, d), lambda i: (i, 4)),
                  pl.BlockSpec((tm, d), lambda i: (i, 5)),
                  pl.BlockSpec((tm, d), lambda i: (i, 0)),
                  pl.BlockSpec((3, d, d), lambda i: (0, 0, 0)),
                  pl.BlockSpec((1, d), lambda i: (0, 0))],
        out_specs=[pl.BlockSpec((None, tm, d), lambda i: (2, i, 0)),
                   pl.BlockSpec((2, tm, d), lambda i: (0, i, 0)),
                   pl.BlockSpec((tm, d), lambda i: (i, 0)),
                   pl.BlockSpec((tm, d), lambda i: (i, 0))],
        out_shape=[SDS(lhs3.shape, BF16), SDS((2, t, d), BF16), SDS((t, d), F32), SDS((t, d), BF16)],
        input_output_aliases={0: 0},
        sem=("parallel",), blocks=blocks, temps=8 * _nbytes((tm, d), F32), comm=comm,
    )(lhs3, z, z, x, w3, g_ffn)


def _ffn_up(h2, w_up, f, comm=None):
    t, d = h2.shape
    _, _, ws = w_up.shape
    tm = _tile(t, 1024, SUBLANES_BF16)
    tn = _tile(ws, 1408, LANES)
    nps = ws // tn
    npp = f // tn

    def body(h_ref, w_ref, o_ref):
        o_ref[...] = _dot(h_ref[...], w_ref[...]).astype(BF16)

    blocks = _nbytes((tm, d), BF16) + _nbytes((d, tn), BF16) + _nbytes((tm, tn), BF16)
    return _pcall(
        body, name="ffn_up", grid=(t // tm, 2 * npp),
        in_specs=[pl.BlockSpec((tm, d), lambda i, j: (i, 0)),
                  pl.BlockSpec((None, d, tn), lambda i, j: (j // nps, 0, j % nps))],
        out_specs=[pl.BlockSpec((None, tm, tn), lambda i, j: (j // npp, i, j % npp))],
        out_shape=[SDS((2, t, f), BF16)],
        sem=("parallel", "parallel"), blocks=blocks, temps=_nbytes((tm, tn), F32), comm=comm,
    )(h2, w_up)


def _conv3_rows(u, u1, u2, w_ref, p):
    return w_ref[p, pl.ds(2, 1), :] * u + w_ref[p, pl.ds(1, 1), :] * u1 + w_ref[p, pl.ds(0, 1), :] * u2


WGRAD_TOKENS = 2048
CHUNK = 64
HALO = SUBLANES_F32


def _up1_up2(u, nxt):
    rows = u.shape[0]
    ext = jnp.concatenate([u, nxt], axis=0)
    n = rows + HALO
    return pltpu.roll(ext, n - 1, 0)[:rows], pltpu.roll(ext, n - 2, 0)[:rows]


def _fold8(x):
    return jnp.sum(x.reshape(x.shape[0] // SUBLANES_F32, SUBLANES_F32, x.shape[1]), axis=0)


def _ffn_mid_fwd(u0, cw, cb, nseq):
    _, t, f = u0.shape
    s = t // nseq
    c = _tile(f, 256, LANES)

    def body(u_ref, w_ref, b_ref, a_ref, uo_ref):
        row = lax.broadcasted_iota(jnp.int32, (s, c), 0)
        act = []
        for p in range(2):
            u = u_ref[p].astype(F32)
            act.append(_conv3_rows(u, _shift_down(u, 1, row), _shift_down(u, 2, row), w_ref, p) + b_ref[p])
            uo_ref[p] = act[p].astype(BF16)
        ug, uv = act
        a_ref[...] = (ug * _sigmoid(ug) * uv).astype(BF16)

    blocks = 2 * _nbytes((2, s, c), BF16) + _nbytes((s, c), BF16)
    outs, _ = _pcall(
        body, name="ffn_mid_fwd", grid=(f // c, nseq),
        in_specs=[pl.BlockSpec((2, s, c), lambda j, b: (0, b, j)),
                  pl.BlockSpec((2, 3, c), lambda j, b: (0, 0, j)),
                  pl.BlockSpec((2, 1, c), lambda j, b: (0, 0, j))],
        out_specs=[pl.BlockSpec((s, c), lambda j, b: (b, j)),
                   pl.BlockSpec((2, s, c), lambda j, b: (0, b, j))],
        out_shape=[SDS((t, f), BF16), SDS((2, t, f), BF16)],
        sem=("parallel", "parallel"), blocks=blocks, temps=8 * _nbytes((s, c), F32),
    )(u0, cw, cb)
    return outs


def _ffn_down_loss(a, w_down, x1, tgt, g_fin):
    t, f = a.shape
    d = x1.shape[1]
    tm = _tile(t, 256, SUBLANES_BF16)
    nsteps = t // tm

    def body(a_ref, w_ref, x1_ref, t_ref, g_ref, dx_ref, dxb_ref, loss_ref, gg_ref, lacc):
        i = pl.program_id(0)

        @pl.when(i == 0)
        def _():
            lacc[...] = jnp.zeros_like(lacc)
            gg_ref[...] = jnp.zeros_like(gg_ref)

        x2 = x1_ref[...] + _dot(a_ref[...], w_ref[...])
        xh, inv = _rms_fwd(x2)
        g = g_ref[...]
        e = xh * g - t_ref[...]
        lacc[...] += jnp.sum(e * e, axis=0, keepdims=True)
        dy = e * (1.0 / d)
        gg_ref[...] += jnp.sum(dy * xh, axis=0, keepdims=True)
        dx2 = _rms_bwd(dy, xh, inv, g)
        dx_ref[...] = dx2
        dxb_ref[...] = dx2.astype(BF16)

        @pl.when(i == nsteps - 1)
        def _():
            loss_ref[...] = jnp.sum(lacc[...], axis=1, keepdims=True) * (0.5 / d)

    blocks = (_nbytes((tm, f), BF16) + _nbytes((f, d), BF16) + 3 * _nbytes((tm, d), F32) + _nbytes((tm, d), BF16))
    outs, _ = _pcall(
        body, name="ffn_down_loss", grid=(nsteps,),
        in_specs=[pl.BlockSpec((tm, f), lambda i: (i, 0)), pl.BlockSpec((f, d), lambda i: (0, 0)),
                  pl.BlockSpec((tm, d), lambda i: (i, 0)), pl.BlockSpec((tm, d), lambda i: (i, 0)),
                  pl.BlockSpec((1, d), lambda i: (0, 0))],
        out_specs=[pl.BlockSpec((tm, d), lambda i: (i, 0)), pl.BlockSpec((tm, d), lambda i: (i, 0)),
                   pl.BlockSpec((1, 1), lambda i: (0, 0)), pl.BlockSpec((1, d), lambda i: (0, 0))],
        out_shape=[SDS((t, d), F32), SDS((t, d), BF16), SDS((1, 1), F32), SDS((1, d), F32)],
        scratch_shapes=[pltpu.VMEM((1, d), F32)],
        sem=("arbitrary",), blocks=blocks, temps=8 * _nbytes((tm, d), F32),
    )(a, w_down, x1, tgt, g_fin)
    return outs


def _ffn_bwd_da(dxb, w_down, comm=None):
    t, d = dxb.shape
    f = w_down.shape[0]
    tm = _tile(t, 1024, SUBLANES_BF16)
    tn = _tile(f, 1408, LANES)

    def body(x_ref, w_ref, o_ref):
        o_ref[...] = _dot_tb(x_ref[...], w_ref[...]).astype(BF16)

    blocks = _nbytes((tm, d), BF16) + _nbytes((tn, d), BF16) + _nbytes((tm, tn), BF16)
    return _pcall(
        body, name="ffn_bwd_da", grid=(t // tm, f // tn),
        in_specs=[pl.BlockSpec((tm, d), lambda i, j: (i, 0)), pl.BlockSpec((tn, d), lambda i, j: (j, 0))],
        out_specs=[pl.BlockSpec((tm, tn), lambda i, j: (i, j))],
        out_shape=[SDS((t, f), BF16)],
        sem=("parallel", "parallel"), blocks=blocks, temps=_nbytes((tm, tn), F32), comm=comm,
    )(dxb, w_down)


def _ffn_mid_bwd(da, u0, ua, cw, nseq, comm=None):
    _, t, f = u0.shape
    s = t // nseq
    c = _tile(f, 128, LANES)
    r = _tile(s, CHUNK, SUBLANES_BF16)
    n = s // r

    def body(da_ref, u_ref, ua_ref, w_ref, du_ref, gw_ref, gb_ref):
        @pl.when(pl.program_id(1) == 0)
        def _():
            gw_ref[...] = jnp.zeros_like(gw_ref)
            gb_ref[...] = jnp.zeros_like(gb_ref)

        def step(i, carry):
            nxt, sums = carry
            rows = pl.ds(pl.multiple_of((n - 1 - i) * r, r), r)
            ug = ua_ref[0, rows, :].astype(F32)
            uv = ua_ref[1, rows, :].astype(F32)
            sg = _sigmoid(ug)
            dacc = da_ref[rows, :].astype(F32)
            dus = (dacc * uv * sg * (1.0 + ug * (1.0 - sg)), dacc * (ug * sg))
            first, new_sums = [], []
            for p in range(2):
                du = dus[p]
                d1, d2 = _up1_up2(du, nxt[p])
                du_ref[p, rows, :] = _conv3_rows(du, d1, d2, w_ref, p).astype(BF16)
                u = u_ref[p, rows, :].astype(F32)
                sb, s0, s1, s2 = sums[p]
                new_sums.append((sb + _fold8(du), s0 + _fold8(d2 * u), s1 + _fold8(d1 * u), s2 + _fold8(du * u)))
                first.append(du[:HALO])
            return tuple(first), tuple(new_sums)

        zero = jnp.zeros((HALO, c), F32)
        _, sums = lax.fori_loop(0, n, step, ((zero, zero), ((zero,) * 4,) * 2))
        for p in range(2):
            sb, s0, s1, s2 = sums[p]
            gb_ref[p] += jnp.sum(sb, axis=0, keepdims=True)
            gw_ref[p, pl.ds(0, 1), :] += jnp.sum(s0, axis=0, keepdims=True)
            gw_ref[p, pl.ds(1, 1), :] += jnp.sum(s1, axis=0, keepdims=True)
            gw_ref[p, pl.ds(2, 1), :] += jnp.sum(s2, axis=0, keepdims=True)

    blocks = _nbytes((s, c), BF16) + 3 * _nbytes((2, s, c), BF16)
    return _pcall(
        body, name="ffn_mid_bwd", grid=(f // c, nseq),
        in_specs=[pl.BlockSpec((s, c), lambda j, b: (b, j)),
                  pl.BlockSpec((2, s, c), lambda j, b: (0, b, j)),
                  pl.BlockSpec((2, s, c), lambda j, b: (0, b, j)),
                  pl.BlockSpec((2, 3, c), lambda j, b: (0, 0, j))],
        out_specs=[pl.BlockSpec((2, s, c), lambda j, b: (0, b, j)),
                   pl.BlockSpec((2, 3, c), lambda j, b: (0, 0, j)),
                   pl.BlockSpec((2, 1, c), lambda j, b: (0, 0, j))],
        out_shape=[SDS((2, t, f), BF16), SDS((2, 3, f), F32), SDS((2, 1, f), F32)],
        sem=("parallel", "arbitrary"), blocks=blocks, temps=4 * 1024 * 1024, comm=comm,
    )(da, u0, ua, cw)


def _wgrad(a, b, name, *, tr, tn, b_plane_of=None, out_shards=None, comm=None):
    t, m = a.shape
    n_total = b.shape[-1] * (b.shape[0] if b.ndim == 3 else 1)
    tk = _tile(t, WGRAD_TOKENS, SUBLANES_BF16)
    nk = t // tk

    def body(a_ref, b_ref, o_ref, acc):
        _acc_over(pl.program_id(2), nk, _dot_ta(a_ref[...], b_ref[...]), acc, o_ref)

    if b.ndim == 3:
        b_spec = pl.BlockSpec((None, tk, tn), lambda r, n, k: (b_plane_of(n)[0], k, b_plane_of(n)[1]))
    else:
        b_spec = pl.BlockSpec((tk, tn), lambda r, n, k: (k, n))
    if out_shards is None:
        o_spec = pl.BlockSpec((tr, tn), lambda r, n, k: (r, n))
        o_shape = SDS((m, n_total), BF16)
    else:
        nps = n_total // out_shards // tn
        o_spec = pl.BlockSpec((None, tr, tn), lambda r, n, k: (n // nps, r, n % nps))
        o_shape = SDS((out_shards, m, n_total // out_shards), BF16)
    blocks = _nbytes((tk, tr), BF16) + _nbytes((tk, tn), BF16) + _nbytes((tr, tn), BF16)
    return _pcall(
        body, name=name, grid=(m // tr, n_total // tn, nk),
        in_specs=[pl.BlockSpec((tk, tr), lambda r, n, k: (k, r)), b_spec],
        out_specs=[o_spec], out_shape=[o_shape],
        scratch_shapes=[pltpu.VMEM((tr, tn), F32)],
        sem=("parallel", "parallel", "arbitrary"), blocks=blocks, temps=2 * _nbytes((tr, tn), F32), comm=comm,
    )(a, b)


def _wgrad3(lhs3, rhs3, comm=None):
    nw, t, d = lhs3.shape
    tk = _tile(t, WGRAD_TOKENS, SUBLANES_BF16)
    nk = t // tk

    def body(a_ref, b_ref, o_ref, acc):
        _acc_over(pl.program_id(1), nk, _dot_ta(a_ref[...], b_ref[...]), acc, o_ref)

    blocks = 2 * _nbytes((tk, d), BF16) + _nbytes((d, d), BF16)
    return _pcall(
        body, name="wgrad_sq3", grid=(nw, nk),
        in_specs=[pl.BlockSpec((None, tk, d), lambda w, k: (w, k, 0)),
                  pl.BlockSpec((None, tk, d), lambda w, k: (w, k, 0))],
        out_specs=[pl.BlockSpec((None, d, d), lambda w, k: (w, 0, 0))],
        out_shape=[SDS((nw, d, d), BF16)],
        scratch_shapes=[pltpu.VMEM((d, d), F32)],
        sem=("parallel", "arbitrary"), blocks=blocks, temps=2 * _nbytes((d, d), F32), comm=comm,
    )(lhs3, rhs3)


def _ffn_bwd_dx1(du0, w_up, x1, dx2, g_ffn, n_planes_out, comm=None):
    _, t, f = du0.shape
    d = x1.shape[1]
    nsh, _, ws = w_up.shape
    tm = _tile(t, 256, SUBLANES_BF16)
    spp = f // ws

    def body(du_ref, w_ref, x1_ref, dx2_ref, g_ref, dx1_ref, dxb_ref, gg_ref):
        @pl.when(pl.program_id(0) == 0)
        def _():
            gg_ref[...] = jnp.zeros_like(gg_ref)

        dh = None
        for k in range(nsh):
            part = _dot_tb(du_ref[k // spp, :, (k % spp) * ws:(k % spp + 1) * ws], w_ref[k])
            dh = part if dh is None else dh + part
        xh, inv = _rms_fwd(x1_ref[...])
        gg_ref[...] += jnp.sum(dh * xh, axis=0, keepdims=True)
        dx1 = dx2_ref[...] + _rms_bwd(dh, xh, inv, g_ref[...])
        dx1_ref[...] = dx1
        dxb_ref[...] = dx1.astype(BF16)

    blocks = _nbytes((2, tm, f), BF16) + 3 * _nbytes((tm, d), F32) + _nbytes((tm, d), BF16)
    return _pcall(
        body, name="ffn_bwd_dx1", grid=(t // tm,),
        in_specs=[pl.BlockSpec((2, tm, f), lambda i: (0, i, 0)),
                  pl.BlockSpec((nsh, d, ws), lambda i: (0, 0, 0), pipeline_mode=pl.Buffered(1)),
                  pl.BlockSpec((tm, d), lambda i: (i, 0)),
                  pl.BlockSpec((tm, d), lambda i: (i, 0)),
                  pl.BlockSpec((1, d), lambda i: (0, 0))],
        out_specs=[pl.BlockSpec((tm, d), lambda i: (i, 0)),
                   pl.BlockSpec((None, tm, d), lambda i: (n_planes_out - 1, i, 0)),
                   pl.BlockSpec((1, d), lambda i: (0, 0))],
        out_shape=[SDS((t, d), F32), SDS((n_planes_out, t, d), BF16), SDS((1, d), F32)],
        sem=("arbitrary",), blocks=blocks, temps=_nbytes(w_up.shape, BF16) + 8 * _nbytes((tm, d), F32), comm=comm,
    )(du0, w_up, x1, dx2, g_ffn)


def _mixer_bwd(rhs3, z, ypc, w3, comm=None):
    _, t, d = rhs3.shape
    tm = _tile(t, 256, SUBLANES_BF16)

    def body(dx_ref, zgp, zgc, ypc_ref, w_ref, dyo, dzo, dpq):
        dm = _dot_tb(dx_ref[...], w_ref[2])
        sp = _sigmoid(zgp[...].astype(F32))
        sc = _sigmoid(zgc[...].astype(F32))
        dyp = (dm * sp).astype(BF16)
        dyc = (dm * sc).astype(BF16)
        dzo[0] = (dm * ypc_ref[0].astype(F32) * sp * (1.0 - sp)).astype(BF16)
        dzo[1] = (dm * ypc_ref[1].astype(F32) * sc * (1.0 - sc)).astype(BF16)
        dyo[0] = dyp
        dyo[1] = dyc
        dpq[0] = _dot_tb(dyp, w_ref[0]).astype(BF16)
        dpq[1] = _dot_tb(dyc, w_ref[1]).astype(BF16)

    blocks = _nbytes((tm, d), BF16) * 3 + _nbytes((2, tm, d), BF16) * 4 + _nbytes((3, d, d), BF16)
    return _pcall(
        body, name="mixer_bwd", grid=(t // tm,),
        in_specs=[pl.BlockSpec((None, tm, d), lambda i: (2, i, 0)),
                  pl.BlockSpec((tm, d), lambda i: (i, 4)),
                  pl.BlockSpec((tm, d), lambda i: (i, 5)),
                  pl.BlockSpec((2, tm, d), lambda i: (0, i, 0)),
                  pl.BlockSpec((3, d, d), lambda i: (0, 0, 0))],
        out_specs=[pl.BlockSpec((2, tm, d), lambda i: (0, i, 0)),
                   pl.BlockSpec((2, tm, d), lambda i: (2, i, 0)),
                   pl.BlockSpec((2, tm, d), lambda i: (0, i, 0))],
        out_shape=[SDS(rhs3.shape, BF16), SDS((N_SPLITS, t, d), BF16), SDS((2, t, d), BF16)],
        input_output_aliases={0: 0},
        sem=("parallel",), blocks=blocks, temps=8 * _nbytes((tm, d), F32), comm=comm,
    )(rhs3, z, z, ypc, w3)


def _conv_bwd(dz, dpq, z, conv_w, nseq, comm=None):
    _, t, d = dz.shape
    s = t // nseq
    c = _tile(d, 128, LANES)
    nb = d // c

    def body(dz_in, dq_ref, zb, zc, zv, cw, dzo, gw_ref):
        del dz_in

        @pl.when(pl.program_id(1) == 0)
        def _():
            gw_ref[...] = jnp.zeros_like(gw_ref)

        row = lax.broadcasted_iota(jnp.int32, (s, c), 0)
        b = zb[...].astype(F32)
        cm = zc[...].astype(F32)
        v = zv[...].astype(F32)
        cv = cm * v
        cv1 = _shift_down(cv, 1, row)
        cv2 = _shift_down(cv, 2, row)
        w0, w1, w2 = cw[pl.ds(0, 1), :], cw[pl.ds(1, 1), :], cw[pl.ds(2, 1), :]
        cc = w2 * cv + w1 * cv1 + w0 * cv2
        dq = dq_ref[...].astype(F32)
        dzo[0] = (dq * cc).astype(BF16)
        dcc = dq * b
        gw_ref[pl.ds(0, 1), :] += jnp.sum(dcc * cv2, axis=0, keepdims=True)
        gw_ref[pl.ds(1, 1), :] += jnp.sum(dcc * cv1, axis=0, keepdims=True)
        gw_ref[pl.ds(2, 1), :] += jnp.sum(dcc * cv, axis=0, keepdims=True)
        dcv = w2 * dcc + w1 * _shift_up(dcc, 1, row) + w0 * _shift_up(dcc, 2, row)
        dzo[1] = (dcv * v).astype(BF16)
        dzo[2] = (dcv * cm).astype(BF16)

    blocks = 4 * _nbytes((s, c), BF16) + _nbytes((3, s, c), BF16)
    return _pcall(
        body, name="conv_bwd", grid=(nb, nseq),
        in_specs=[ANY,
                  pl.BlockSpec((None, s, c), lambda j, b: (1, b, j)),
                  pl.BlockSpec((s, c), lambda j, b: (b, nb + j)),
                  pl.BlockSpec((s, c), lambda j, b: (b, 2 * nb + j)),
                  pl.BlockSpec((s, c), lambda j, b: (b, 3 * nb + j)),
                  pl.BlockSpec((3, c), lambda j, b: (0, j))],
        out_specs=[pl.BlockSpec((3, s, c), lambda j, b: (0, b, j)),
                   pl.BlockSpec((3, c), lambda j, b: (0, j))],
        out_shape=[SDS(dz.shape, BF16), SDS((3, d), F32)],
        input_output_aliases={0: 0},
        sem=("parallel", "arbitrary"), blocks=blocks, temps=16 * _nbytes((s, c), F32), comm=comm,
    )(dz, dpq, z, z, z, conv_w)


def _pool_bwd_call(dz, dpq, z, pool_w, pool_scale, nseq, comm=None):
    _, t, d = dz.shape
    s = t // nseq
    c = d // N_GROUPS

    def body(dz_in, dp_ref, zp, pw, ps, dzo, gpw_ref, gps_ref):
        del dz_in
        j = pl.program_id(0)

        @pl.when(pl.program_id(1) == 0)
        def _():
            gpw_ref[...] = jnp.zeros_like(gpw_ref)
            gps_ref[...] = jnp.zeros_like(gps_ref)

        row = lax.broadcasted_iota(jnp.int32, (s, c), 0)
        for gi, win in enumerate(POOL_WINDOWS):
            @pl.when(j == gi)
            def _(win=win):
                pb = _pool_fwd(zp[...].astype(F32), win, row).astype(BF16)
                plin = _dot(pb, pw[...])
                dps = dp_ref[...].astype(F32)
                gps_ref[...] += jnp.sum(dps * plin, axis=0, keepdims=True)
                dplb = (dps * ps[...]).astype(BF16)
                gpw_ref[...] += _dot_ta(pb, dplb)
                dzo[...] = _pool_bwd(_dot_tb(dplb, pw[...]), win, row).astype(BF16)

    blocks = 3 * _nbytes((s, c), BF16) + _nbytes((c, c), BF16) + _nbytes((c, c), F32)
    return _pcall(
        body, name="pool_bwd", grid=(N_GROUPS, nseq),
        in_specs=[ANY,
                  pl.BlockSpec((None, s, c), lambda j, b: (0, b, j)),
                  pl.BlockSpec((s, c), lambda j, b: (b, j)),
                  pl.BlockSpec((None, c, c), lambda j, b: (j, 0, 0)),
                  pl.BlockSpec((1, c), lambda j, b: (0, j))],
        out_specs=[pl.BlockSpec((None, s, c), lambda j, b: (3, b, j)),
                   pl.BlockSpec((None, c, c), lambda j, b: (j, 0, 0)),
                   pl.BlockSpec((1, c), lambda j, b: (0, j))],
        out_shape=[SDS(dz.shape, BF16), SDS((N_GROUPS, c, c), F32), SDS((1, d), F32)],
        input_output_aliases={0: 0},
        sem=("parallel", "arbitrary"), blocks=blocks, temps=10 * _nbytes((s, c), F32), comm=comm,
    )(dz, dpq, z, pool_w, pool_scale)


def _dz_plane(zb):
    return jnp.where(zb < 4, (zb + 3) % 4, zb)


def _wgrad_in(h1, dz, nsh, comm=None):
    t, d = h1.shape
    ws = N_SPLITS * d // nsh
    kb = _tile(math.gcd(d, ws), 512, LANES)
    npl = d // kb
    nps = ws // kb
    tk = _tile(t, WGRAD_TOKENS, SUBLANES_BF16)
    nk = t // tk

    def body(a_ref, b_ref, o_ref, acc):
        _acc_over(pl.program_id(1), nk, _dot_ta(a_ref[...], b_ref[...]), acc, o_ref)

    blocks = _nbytes((tk, d), BF16) + _nbytes((tk, kb), BF16) + _nbytes((d, kb), BF16)
    return _pcall(
        body, name="wgrad_in", grid=(N_SPLITS * npl, nk),
        in_specs=[pl.BlockSpec((tk, d), lambda cb, k: (k, 0)),
                  pl.BlockSpec((None, tk, kb), lambda cb, k: (_dz_plane(cb // npl), k, cb % npl))],
        out_specs=[pl.BlockSpec((None, d, kb), lambda cb, k: (cb // nps, 0, cb % nps))],
        out_shape=[SDS((nsh, d, ws), BF16)],
        scratch_shapes=[pltpu.VMEM((d, kb), F32)],
        sem=("parallel", "arbitrary"), blocks=blocks, temps=2 * _nbytes((d, kb), F32), comm=comm,
    )(h1, dz)


def _mixer_bwd_dx(dz, w_in, x, dx1, g_mix, name, rows, dx_prev=None, comm=None):
    npln, t, d = dz.shape
    nsh, _, ws = w_in.shape
    tm = _tile(t, 256, SUBLANES_BF16)
    kb = _tile(math.gcd(d, ws), 512, LANES)
    npl = d // kb
    nps = ws // kb

    def body(dz_ref, w_ref, x_ref, dx1_ref, g_ref, dx_ref, gg_ref):
        @pl.when(pl.program_id(0) == 0)
        def _():
            gg_ref[...] = jnp.zeros_like(gg_ref)

        dh = None
        for cb in range(npln * npl):
            zb = cb // npl
            plane = (zb + 3) % 4 if zb < 4 else zb
            part = _dot_tb(dz_ref[plane, :, (cb % npl) * kb:(cb % npl + 1) * kb],
                           w_ref[cb // nps, :, (cb % nps) * kb:(cb % nps + 1) * kb])
            dh = part if dh is None else dh + part
        xh, inv = _rms_fwd(x_ref[...])
        gg_ref[...] += jnp.sum(dh * xh, axis=0, keepdims=True)
        dx_ref[...] = dx1_ref[...] + _rms_bwd(dh, xh, inv, g_ref[...])

    lo, hi, parts = rows
    i0 = lo * (t // tm) // parts
    steps = hi * (t // tm) // parts - i0
    blocks = _nbytes((npln, tm, d), BF16) + 3 * _nbytes((tm, d), F32)
    in_specs = [pl.BlockSpec((npln, tm, d), lambda i: (0, i + i0, 0)),
                pl.BlockSpec((nsh, d, ws), lambda i: (0, 0, 0), pipeline_mode=pl.Buffered(1)),
                pl.BlockSpec((tm, d), lambda i: (i + i0, 0)),
                pl.BlockSpec((tm, d), lambda i: (i + i0, 0)),
                pl.BlockSpec((1, d), lambda i: (0, 0))]
    args = [dz, w_in, x, dx1, g_mix]
    if dx_prev is not None:
        in_specs.append(ANY)
        args.append(dx_prev)
    return _pcall(
        lambda *refs: body(*refs[:5], *refs[-2:]), name=name, grid=(steps,),
        in_specs=in_specs,
        out_specs=[pl.BlockSpec((tm, d), lambda i: (i + i0, 0)),
                   pl.BlockSpec((1, d), lambda i: (0, 0))],
        out_shape=[SDS((t, d), F32), SDS((1, d), F32)],
        input_output_aliases={5: 0} if dx_prev is not None else None,
        sem=("arbitrary",), blocks=blocks, temps=_nbytes(w_in.shape, BF16) + 8 * _nbytes((tm, d), F32), comm=comm,
    )(*args)


N_BIG = 5
SHARD_MAJOR = (0, 2)
ROWS_DIM1 = (1, 4)


def _ds(start, size, align):
    if isinstance(start, int):
        return pl.ds(start, size)
    return pl.ds(pl.multiple_of(start, align), size)


def _piece(a, ref, k, h):
    if a in SHARD_MAJOR:
        r = ref.shape[1] // 2
        return ref.at[k, _ds(h * r, r, SUBLANES_BF16), :]
    if a in ROWS_DIM1:
        r = ref.shape[1] // 8
        return ref.at[:, _ds((2 * k + h) * r, r, SUBLANES_BF16), :]
    r = ref.shape[0] // 8
    return ref.at[_ds((2 * k + h) * r, r, SUBLANES_BF16), :]


def _half(a, ref, h):
    if a in ROWS_DIM1:
        r = ref.shape[1] // 2
        return ref.at[:, _ds(h * r, r, SUBLANES_BF16), :]
    r = ref.shape[0] // 2
    return ref.at[_ds(h * r, r, SUBLANES_BF16), :]


def _piece_shape(a, full_shape):
    if a in SHARD_MAJOR:
        return (full_shape[1] // 2, full_shape[2])
    if a in ROWS_DIM1:
        return (full_shape[0], full_shape[1] // 8, full_shape[2])
    return (full_shape[0] // 8, full_shape[1])


def _shard_shape(a, full_shape):
    if a in SHARD_MAJOR:
        return (full_shape[1], full_shape[2])
    if a in ROWS_DIM1:
        return (full_shape[0], full_shape[1] // 4, full_shape[2])
    return (full_shape[0] // 4, full_shape[1])


def _rows_axis(a):
    return 1 if a in ROWS_DIM1 else 0


def _piece_block(a, full_shape):
    ps = _piece_shape(a, full_shape)
    if a in SHARD_MAJOR:
        return (None,) + ps, lambda k, c: (k, c, 0)
    if a in ROWS_DIM1:
        return ps, lambda k, c: (0, 2 * k + c, 0)
    return ps, lambda k, c: (2 * k + c, 0)


def _coords():
    return lax.axis_index("x"), lax.axis_index("y"), lax.axis_index("c")


def _peer_chips(x, y):
    return [(1 - x, y), (x, 1 - y), (1 - x, 1 - y)]


def _remote(src, dst, ssem, rsem, dev):
    return pltpu.make_async_remote_copy(src_ref=src, dst_ref=dst, send_sem=ssem, recv_sem=rsem,
                                        device_id=dev, device_id_type=MESH)


def _dma_sems(*counts):
    return [pltpu.SemaphoreType.DMA((n,)) for n in counts]


def _symmetric(ins, out_shapes, sems, copies, aliases=None):
    def start(cins, couts, csems):
        for cp in copies(cins, couts, csems):
            cp.start()

    def finish(cins, couts, csems):
        for cp in copies(cins, couts, csems):
            cp.wait()

    return _Comm(ins, out_shapes, sems, start, finish, aliases)


def _rows_part(a, ref, part):
    if part is None:
        return ref
    p, q, n = part
    ax = _rows_axis(a)
    r = ref.shape[ax] // n
    return ref.at[tuple(pl.ds(p * r, (q - p) * r) if d == ax else slice(None) for d in range(len(ref.shape)))]


def _merge(comms):
    ins, outs, sems, aliases, spans = [], [], [], {}, []
    for cm in comms:
        spans.append((len(ins), len(outs), len(sems)))
        for i, o in cm.aliases.items():
            aliases[len(ins) + i] = len(outs) + o
        ins += cm.ins
        outs += cm.out_shapes
        sems += cm.sems

    def each(fn_name):
        def run(cins, couts, csems):
            for cm, (i0, o0, s0) in zip(comms, spans):
                getattr(cm, fn_name)(cins[i0:i0 + len(cm.ins)], couts[o0:o0 + len(cm.out_shapes)],
                                     csems[s0:s0 + len(cm.sems)])
        return run

    return _Comm(ins, outs, sems, each("start"), each("finish"), aliases)


def _gather_comm(arrs, locs, full_shapes, part=None, into=None):
    n = len(arrs)

    def own(cins, couts, csems):
        x, y, c = _coords()
        j = 2 * x + y
        return [_remote(_rows_part(a, _half(a, cins[q], h), part), _rows_part(a, _piece(a, couts[q], j, h), part),
                        csems[0].at[2 * q + h], csems[1].at[2 * q + h], (x, y, 1 - c))
                for q, a in enumerate(arrs) for h in range(2)]

    def sends(cins, couts, csems):
        x, y, c = _coords()
        j = 2 * x + y
        return [_remote(_rows_part(a, _half(a, cins[q], c), part), _rows_part(a, _piece(a, couts[q], j, c), part),
                        csems[2].at[3 * q + i], csems[3].at[3 * q + i], (px, py, c))
                for q, a in enumerate(arrs) for i, (px, py) in enumerate(_peer_chips(x, y))]

    def forwards(couts, csems, half_of):
        x, y, c = _coords()
        out = []
        for q, a in enumerate(arrs):
            for i, (px, py) in enumerate(_peer_chips(x, y)):
                landed = _rows_part(a, _piece(a, couts[q], 2 * px + py, half_of(c)), part)
                out.append(_remote(landed, landed, csems[4].at[3 * q + i], csems[5].at[3 * q + i], (x, y, 1 - c)))
        return out

    def start(cins, couts, csems):
        for cp in sends(cins, couts, csems) + own(cins, couts, csems):
            cp.start()

    def finish(cins, couts, csems):
        fw = forwards(couts, csems, lambda c: c)
        for cp, f in zip(sends(cins, couts, csems), fw):
            cp.wait_recv()
            f.start()
        for f in forwards(couts, csems, lambda c: 1 - c):
            f.wait_recv()
        for cp in sends(cins, couts, csems) + fw:
            cp.wait_send()
        for cp in own(cins, couts, csems):
            cp.wait()

    ins = [locs[a] for a in arrs] + ([into[a] for a in arrs] if into else [])
    return _Comm(ins, [SDS(full_shapes[a], BF16) for a in arrs],
                 _dma_sems(2 * n, 2 * n, 3 * n, 3 * n, 3 * n, 3 * n), start, finish,
                 aliases={n + q: q for q in range(n)} if into else None)


def _halves_comm(arrs, gbs):
    n = len(arrs)

    def copies(cins, couts, csems):
        x, y, c = _coords()
        return [_remote(_piece(a, cins[q], k, 1 - c), couts[q].at[k], csems[0].at[4 * q + k], csems[1].at[4 * q + k],
                        (x, y, 1 - c)) for q, a in enumerate(arrs) for k in range(4)]

    return _symmetric([gbs[a] for a in arrs], [SDS((4,) + _piece_shape(a, gbs[a].shape), BF16) for a in arrs],
                      _dma_sems(4 * n, 4 * n), copies)


def _chips_comm(arrs, ps, part=None, into=None):
    n = len(arrs)

    def copies(cins, couts, csems):
        x, y, c = _coords()
        return [_remote(_rows_part(a, cins[q].at[2 * px + py], part), _rows_part(a, couts[q].at[i], part),
                        csems[0].at[3 * q + i], csems[1].at[3 * q + i], (px, py, c))
                for q, a in enumerate(arrs) for i, (px, py) in enumerate(_peer_chips(x, y))]

    ins = [ps[a] for a in arrs] + ([into[a] for a in arrs] if into else [])
    return _symmetric(ins, [SDS((3,) + ps[a].shape[1:], BF16) for a in arrs], _dma_sems(3 * n, 3 * n), copies,
                      aliases={n + q: q for q in range(n)} if into else None)


def _result_comm(arrs, gs):
    n = len(arrs)

    def copies(cins, couts, csems):
        x, y, c = _coords()
        return [_remote(_half(a, cins[q], c), _half(a, couts[q], c), csems[0].at[q], csems[1].at[q], (x, y, 1 - c))
                for q, a in enumerate(arrs)]

    return _symmetric([gs[a] for a in arrs], [SDS(gs[a].shape, F32) for a in arrs], _dma_sems(n, n), copies,
                      aliases={q: q for q in range(n)})


def _add_halves(arrs, gbs, lands, c_arr, name):
    n = len(arrs)

    def body(c_ref, *refs):
        del c_ref
        for q in range(n):
            refs[2 * n + q][...] = (refs[q][...].astype(F32) + refs[n + q][...].astype(F32)).astype(BF16)

    g_specs, l_specs, o_specs, blocks = [], [], [], 0
    for a in arrs:
        bs, imap = _piece_block(a, gbs[a].shape)
        ps = _piece_shape(a, gbs[a].shape)
        g_specs.append(pl.BlockSpec(bs, lambda k, c_ref, imap=imap: imap(k, c_ref[0])))
        nd = len(ps)
        l_specs.append(pl.BlockSpec((None,) + ps, lambda k, c_ref, nd=nd: (k,) + (0,) * nd))
        o_specs.append(pl.BlockSpec((None,) + ps, lambda k, c_ref, nd=nd: (k,) + (0,) * nd))
        blocks += 3 * _nbytes(ps, BF16)
    return list(pl.pallas_call(
        body, name=name,
        grid_spec=pltpu.PrefetchScalarGridSpec(
            num_scalar_prefetch=1, grid=(4,), in_specs=g_specs + l_specs, out_specs=o_specs),
        out_shape=[SDS((4,) + _piece_shape(a, gbs[a].shape), BF16) for a in arrs],
        compiler_params=_params(("parallel",), blocks, blocks),
    )(c_arr, *[gbs[a] for a in arrs], *lands))


def _sum_chips(a, p, land, shard_shape, jc_arr, name):
    ps = land.shape[1:]
    ax = _rows_axis(a)
    rows = ps[ax]
    nsub = 2 if rows % (2 * SUBLANES_BF16) == 0 else 1
    bs = tuple(r // nsub if q == ax else r for q, r in enumerate(ps))
    nd = len(ps)

    def at_rows(v):
        return tuple(v if q == ax else 0 for q in range(nd))

    def body(jc_ref, p_ref, l_ref, o_ref):
        del jc_ref
        acc = p_ref[...].astype(F32) + l_ref[0].astype(F32)
        acc = acc + l_ref[1].astype(F32)
        o_ref[...] = acc + l_ref[2].astype(F32)

    blocks = 4 * _nbytes(bs, BF16) + _nbytes(bs, F32)
    return pl.pallas_call(
        body, name=name,
        grid_spec=pltpu.PrefetchScalarGridSpec(
            num_scalar_prefetch=1, grid=(nsub,),
            in_specs=[pl.BlockSpec((None,) + bs, lambda s, jc: (jc[0],) + at_rows(s)),
                      pl.BlockSpec((3,) + bs, lambda s, jc: (0,) + at_rows(s))],
            out_specs=pl.BlockSpec(bs, lambda s, jc: at_rows(jc[1] * nsub + s))),
        out_shape=SDS(shard_shape, F32),
        compiler_params=_params(("parallel",), blocks, 2 * _nbytes(bs, F32)),
    )(jc_arr, p, land)


def _small_comm(v):
    rows = v.shape[0]

    def copies(cins, couts, csems):
        x, y, c = _coords()
        me = 4 * x + 2 * y + c
        out = [pltpu.make_async_copy(cins[0], couts[0].at[me], csems[0].at[0])]
        for dlt in range(1, 8):
            px = 1 - x if (dlt >> 2) & 1 else x
            py = 1 - y if (dlt >> 1) & 1 else y
            pc = 1 - c if dlt & 1 else c
            out.append(_remote(cins[0], couts[0].at[me], csems[1].at[dlt - 1], csems[2].at[dlt - 1], (px, py, pc)))
        return out

    return _symmetric([v], [SDS((8, rows, LANES), F32)], _dma_sems(1, 7, 7), copies)


def _sum8(slots, name):
    def body(s_ref, o_ref):
        acc = s_ref[0]
        for i in range(1, 8):
            acc = acc + s_ref[i]
        o_ref[...] = acc

    return pl.pallas_call(
        body, name=name,
        in_specs=[pl.BlockSpec(memory_space=pltpu.VMEM)], out_specs=pl.BlockSpec(memory_space=pltpu.VMEM),
        out_shape=SDS(slots.shape[1:], F32),
    )(slots)


def _adamw(w, g, m, v, name, g_plane=None):
    rows, cols = w.shape
    tr = _tile(rows, max(SUBLANES_F32, (256 * 1024 // cols) // SUBLANES_F32 * SUBLANES_F32), SUBLANES_F32)

    def body(w_ref, g_ref, m_ref, v_ref, go_ref, d_ref, mo_ref, vo_ref):
        gr = g_ref[...]
        mn = ADAM_B1 * m_ref[...] + (1.0 - ADAM_B1) * gr
        vn = ADAM_B2 * v_ref[...] + (1.0 - ADAM_B2) * (gr * gr)
        m_hat = mn / (1.0 - ADAM_B1 ** ADAM_STEP)
        v_hat = vn / (1.0 - ADAM_B2 ** ADAM_STEP)
        d_ref[...] = -ADAM_LR * (m_hat / (jnp.sqrt(v_hat) + ADAM_EPS) + ADAM_WD * w_ref[...])
        go_ref[...] = gr
        mo_ref[...] = mn
        vo_ref[...] = vn

    spec = pl.BlockSpec((tr, cols), lambda i: (i, 0))
    g_spec = spec if g_plane is None else pl.BlockSpec((None, tr, cols), lambda i: (g_plane, i, 0))
    return pl.pallas_call(
        body, name=name, grid=(rows // tr,),
        in_specs=[spec, g_spec, spec, spec], out_specs=[spec, spec, spec, spec],
        out_shape=[SDS((rows, cols), F32)] * 4,
        compiler_params=_params(("parallel",), 8 * _nbytes((tr, cols), F32), 4 * _nbytes((tr, cols), F32)),
    )(w, g, m, v)


def _pack(parts):
    rows = []
    for p in parts:
        r = p.reshape(-1, LANES)
        pad = (-r.shape[0]) % SUBLANES_F32
        if pad:
            r = jnp.pad(r, ((0, pad), (0, 0)))
        rows.append(r)
    return jnp.concatenate(rows, axis=0)


def _unpack(packed, shapes):
    out, at = [], 0
    for s in shapes:
        n = 1
        for q in s:
            n *= q
        r = n // LANES
        out.append(packed[at:at + r].reshape(s))
        at += r + (-r) % SUBLANES_F32
    return out


def kernel(x, norm_mix, w_in, pool_w, pool_scale, w_pool_proj, conv_w, w_conv_out, w_o, norm_ffn, w_up, ffn_conv_w, ffn_conv_b, w_down, norm_final, loss_target, m_norm_mix, m_w_in, m_pool_w, m_pool_scale, m_w_pool_proj, m_conv_w, m_w_conv_out, m_w_o, m_norm_ffn, m_w_up, m_ffn_conv_w, m_ffn_conv_b, m_w_down, m_norm_final, v_norm_mix, v_w_in, v_pool_w, v_pool_scale, v_w_pool_proj, v_conv_w, v_w_conv_out, v_w_o, v_norm_ffn, v_w_up, v_ffn_conv_w, v_ffn_conv_b, v_w_down, v_norm_final):
    nseq, seq, d = x.shape
    t = nseq * seq
    f = w_down.shape[1] * 4
    c = d // N_GROUPS
    xy = lax.axis_index("x") * 2 + lax.axis_index("y")
    c_arr = lax.axis_index("c").astype(jnp.int32).reshape(1)
    jc_arr = jnp.stack([xy, lax.axis_index("c")]).astype(jnp.int32)
    nsh = 4
    zero = jnp.zeros((), jnp.int32)

    locs = [w_in[0].astype(BF16),
            jnp.stack([w_pool_proj[0], w_conv_out[0], w_o[0]]).astype(BF16),
            w_up[0].astype(BF16), w_down[0].astype(BF16), pool_w[0].astype(BF16)]
    full_shapes = [(nsh, d, N_SPLITS * d // nsh), (3, d, d), (nsh, d, 2 * f // nsh), (f, d), (N_GROUPS, c, c)]

    cw_pad = lax.dynamic_update_slice(jnp.zeros((3, d), F32), conv_w[0], (zero, xy * (d // 4)))
    fw_pad = lax.dynamic_update_slice(jnp.zeros((3, 2 * f), F32), ffn_conv_w[0], (zero, xy * (f // 2)))
    small_w = _pack([cw_pad, fw_pad]) * 0.5

    x2d = x.reshape(t, d)
    tgt = loss_target.reshape(t, d)
    ax, ay = lax.axis_index("x"), lax.axis_index("y")
    order = jnp.stack([xy, 2 * (1 - ax) + ay, 2 * ax + 1 - ay, 2 * (1 - ax) + 1 - ay]).astype(jnp.int32)
    (z, h1, w_in_f), (pool_w_f, w3_f, slots_w) = _fwd_in(
        x2d, norm_mix, locs[0], order,
        _merge([_gather_comm([4], locs, full_shapes), _gather_comm([1], locs, full_shapes, part=(0, 1, 2)),
                _small_comm(small_w)]))
    conv_w_f, ffn_cw_f = _unpack(_sum8(slots_w, "sum8_weights"), [(3, d), (3, 2 * f)])
    ffn_cw_p = ffn_cw_f.reshape(3, 2, f).transpose(1, 0, 2)
    ffn_cb_p = ffn_conv_b.reshape(2, 1, f)
    (lhs3,), (w3_f,) = _mixer_mid_fwd(z, pool_w_f, pool_scale, conv_w_f, nseq,
                                      _gather_comm([1], locs, full_shapes, part=(1, 2, 2), into={1: w3_f}))
    (lhs3, ypc, x1, h2), (w_up_f,) = _mixer_out(lhs3, z, x2d, w3_f, norm_ffn, _gather_comm([2], locs, full_shapes))
    (u0,), (w_down_f,) = _ffn_up(h2, w_up_f, f, _gather_comm([3], locs, full_shapes))
    act, ua = _ffn_mid_fwd(u0, ffn_cw_p, ffn_cb_p, nseq)
    dx2, dx2b, loss11, g_norm_final = _ffn_down_loss(act, w_down_f, x1, tgt, norm_final.reshape(1, d))

    gbs, lands, ps, lands2, rs = {}, {}, {}, {}, {}
    tn_up = _tile(2 * f // nsh, 1408, LANES)
    npp = f // tn_up

    def add(arrs, name):
        for a, p in zip(arrs, _add_halves(arrs, gbs, [lands[a] for a in arrs], c_arr, name)):
            ps[a] = p

    def summed(a):
        rs[a] = _sum_chips(a, ps[a], lands2[a], _shard_shape(a, full_shapes[a]), jc_arr, "sum_chips_%d" % a)

    (gbs[3],), _ = _wgrad(act, dx2b, "wgrad_down", tr=tn_up, tn=d)
    (da,), (lands[3],) = _ffn_bwd_da(dx2b, w_down_f, _halves_comm([3], gbs))
    add([3], "add_halves_down")
    (du0, g_ffn_cw_p, g_ffn_cb_p), (lands2[3],) = _ffn_mid_bwd(da, u0, ua, ffn_cw_p, nseq, _chips_comm([3], ps))
    summed(3)
    (gbs[2],), (rs[3],) = _wgrad(h2, du0, "wgrad_up", tr=d, tn=tn_up, b_plane_of=lambda n: (n // npp, n % npp),
                                 out_shards=nsh, comm=_result_comm([3], rs))
    (dx1, rhs3, g_norm_ffn), (lands[2],) = _ffn_bwd_dx1(du0, w_up_f, x1, dx2, norm_ffn, 3, _halves_comm([2], gbs))
    add([2], "add_halves_up")
    (rhs3, dz, dpq), (lands2[2],) = _mixer_bwd(rhs3, z, ypc, w3_f, _chips_comm([2], ps, part=(0, 1, 2)))
    (gbs[1],), (lands2[2],) = _wgrad3(lhs3, rhs3, _chips_comm([2], ps, part=(1, 2, 2), into=lands2))
    summed(2)
    (dz, g_conv_w), (lands[1], rs[2]) = _conv_bwd(dz, dpq, z, conv_w_f, nseq,
                                                  _merge([_halves_comm([1], gbs), _result_comm([2], rs)]))
    add([1], "add_halves_sq3")
    (dz, g_pool_w, g_pool_scale), _ = _pool_bwd_call(dz, dpq, z, pool_w_f, pool_scale, nseq)
    gbs[4] = g_pool_w.astype(BF16)
    (gbs[0],), (lands2[1],) = _wgrad_in(h1, dz, nsh, _chips_comm([1], ps))
    summed(1)
    lands[0], lands[4] = _run_comm(_halves_comm([0, 4], gbs), "exchange_halves_in")
    add([0, 4], "add_halves_in")
    g_ffn_cw = g_ffn_cw_p.transpose(1, 0, 2).reshape(3, 2 * f)
    small_a = _pack([g_pool_scale, g_norm_ffn, g_ffn_cb_p.reshape(1, 2 * f), g_norm_final.reshape(d), g_conv_w,
                     g_ffn_cw, jnp.pad(loss11, ((0, SUBLANES_F32 - 1), (0, LANES - 1)))])
    (grad_x, g_norm_mix), (lands2[0], lands2[4], rs[1], slots_a) = _mixer_bwd_dx(
        dz, w_in_f, x2d, dx1, norm_mix, "mixer_bwd_dx", (0, 3, 4),
        comm=_merge([_chips_comm([0, 4], ps), _result_comm([1], rs), _small_comm(small_a)]))
    (grad_x, g_norm_mix_tail), _ = _mixer_bwd_dx(dz, w_in_f, x2d, dx1, norm_mix, "mixer_bwd_dx_tail", (3, 4, 4),
                                                 dx_prev=grad_x)
    g_norm_mix = g_norm_mix + g_norm_mix_tail
    summed(0)
    summed(4)
    rs[0], rs[4], slots_b = _run_comm(_merge([_result_comm([0, 4], rs), _small_comm(_pack([g_norm_mix]))]),
                                      "exchange_result_in")
    shapes_a = [(1, d), (1, d), (1, 2 * f), (d,), (3, d), (3, 2 * f), (SUBLANES_F32, LANES)]
    gs_pool_scale, gs_norm_ffn, gs_ffn_cb, gs_norm_final, gs_conv_w, gs_ffn_cw, loss_blk = _unpack(
        _sum8(slots_a, "sum8_grads"), shapes_a)
    (gs_norm_mix,) = _unpack(_sum8(slots_b, "sum8_norm_mix"), [(1, d)])
    gs_conv_w = lax.dynamic_slice(gs_conv_w, (zero, xy * (d // 4)), (3, d // 4))
    gs_ffn_cw = lax.dynamic_slice(gs_ffn_cw, (zero, xy * (f // 2)), (3, f // 2))

    def upd(w, g, m, v, name, g_plane=None):
        shape = w.shape
        rows = 1
        for q in shape[:-1]:
            rows *= q
        g2 = g if g_plane is not None else g.reshape(rows, shape[-1])
        outs = _adamw(w.reshape(rows, shape[-1]), g2, m.reshape(rows, shape[-1]), v.reshape(rows, shape[-1]),
                      name, g_plane)
        return [o.reshape(shape) for o in outs]

    res = {
        "w_in": upd(w_in, rs[0], m_w_in, v_w_in, "adamw_w_in"),
        "pool_w": upd(pool_w, rs[4], m_pool_w, v_pool_w, "adamw_pool_w"),
        "w_pool_proj": upd(w_pool_proj, rs[1], m_w_pool_proj, v_w_pool_proj, "adamw_w_pool_proj", 0),
        "w_conv_out": upd(w_conv_out, rs[1], m_w_conv_out, v_w_conv_out, "adamw_w_conv_out", 1),
        "w_o": upd(w_o, rs[1], m_w_o, v_w_o, "adamw_w_o", 2),
        "w_up": upd(w_up, rs[2], m_w_up, v_w_up, "adamw_w_up"),
        "w_down": upd(w_down, rs[3], m_w_down, v_w_down, "adamw_w_down"),
    }

    small_names = ["norm_mix", "pool_scale", "norm_ffn", "ffn_conv_b", "norm_final", "conv_w", "ffn_conv_w"]
    small_ws = [norm_mix, pool_scale, norm_ffn, ffn_conv_b, norm_final, conv_w, ffn_conv_w]
    small_ms = [m_norm_mix, m_pool_scale, m_norm_ffn, m_ffn_conv_b, m_norm_final, m_conv_w, m_ffn_conv_w]
    small_vs = [v_norm_mix, v_pool_scale, v_norm_ffn, v_ffn_conv_b, v_norm_final, v_conv_w, v_ffn_conv_w]
    small_gs = [gs_norm_mix, gs_pool_scale, gs_norm_ffn, gs_ffn_cb, gs_norm_final, gs_conv_w, gs_ffn_cw]
    _, sd, sm, sv = _adamw(_pack(small_ws), _pack(small_gs), _pack(small_ms), _pack(small_vs), "adamw_small")
    shapes = [w.shape for w in small_ws]
    sd, sm, sv = _unpack(sd, shapes), _unpack(sm, shapes), _unpack(sv, shapes)
    for i, nm in enumerate(small_names):
        res[nm] = [small_gs[i].reshape(shapes[i]), sd[i], sm[i], sv[i]]

    order = ["norm_mix", "w_in", "pool_w", "pool_scale", "w_pool_proj", "conv_w", "w_conv_out", "w_o", "norm_ffn",
             "w_up", "ffn_conv_w", "ffn_conv_b", "w_down", "norm_final"]
    return (loss_blk[0, 0], grad_x.reshape(x.shape), *[res[n][0] for n in order], *[res[n][1] for n in order],
            *[res[n][2] for n in order], *[res[n][3] for n in order])
```

```python
import math

import jax
import jax.numpy as jnp
from jax import lax
from jax.experimental import pallas as pl
from jax.experimental.pallas import tpu as pltpu

F32 = jnp.float32
BF16 = jnp.bfloat16
SDS = jax.ShapeDtypeStruct
MESH = pl.DeviceIdType.MESH

RMS_EPS = 1e-6
POOL_WINDOWS = (2, 4, 8, 16)
N_GROUPS = len(POOL_WINDOWS)
N_SPLITS = 6

ADAM_LR = 0.001
ADAM_B1 = 0.9
ADAM_B2 = 0.999
ADAM_EPS = 1e-08
ADAM_WD = 0.01
ADAM_STEP = 10

LANES = 128
SUBLANES_F32 = 8
SUBLANES_BF16 = 16
VMEM_BYTES = 64 * 1024 * 1024
VMEM_CAP = VMEM_BYTES - 8 * 1024 * 1024
VMEM_FLOOR = 16 * 1024 * 1024

ANY = pl.BlockSpec(memory_space=pl.ANY)


def _tile(dim, pref, align):
    if dim <= pref:
        return dim
    t = (pref // align) * align
    while t >= align:
        if dim % t == 0:
            return t
        t -= align
    return dim


def _nbytes(shape, dtype):
    n = 1
    for s in shape:
        n *= s
    return n * jnp.dtype(dtype).itemsize


def _params(sem, block_bytes, temp_bytes=0, collective_id=None):
    need = 2 * block_bytes + temp_bytes + 4 * 1024 * 1024
    return pltpu.CompilerParams(dimension_semantics=sem, collective_id=collective_id,
                                vmem_limit_bytes=int(min(max(need, VMEM_FLOOR), VMEM_CAP)))


SIBLING = (0, 0, 1)
CHIPS = ((1, 0, 0), (0, 1, 0), (1, 1, 0))
EVERYONE = tuple((a, b, c) for a in range(2) for b in range(2) for c in range(2) if a + b + c)
PEER_SETS = (frozenset([SIBLING]), frozenset(CHIPS), frozenset(CHIPS + (SIBLING,)), frozenset(EVERYONE))


def _collective_id(peers):
    return PEER_SETS.index(frozenset(peers))


def _handshake(peers):
    x, y, c = lax.axis_index("x"), lax.axis_index("y"), lax.axis_index("c")
    bar = pltpu.get_barrier_semaphore()
    for fx, fy, fc in sorted(peers):
        dev = (1 - x if fx else x, 1 - y if fy else y, 1 - c if fc else c)
        pl.semaphore_signal(bar, inc=1, device_id=dev, device_id_type=MESH)
    pl.semaphore_wait(bar, len(peers))


class _Comm:
    def __init__(self, ins, out_shapes, sems, start, finish, peers, aliases=None):
        self.ins = list(ins)
        self.out_shapes = list(out_shapes)
        self.sems = list(sems)
        self.start = start
        self.finish = finish
        self.peers = frozenset(peers)
        self.aliases = dict(aliases or {})


def _pcall(body, *, name, grid, in_specs, out_specs, out_shape, sem, blocks, temps=0, scratch_shapes=(),
           input_output_aliases=None, comm=None):
    in_specs = list(in_specs)
    out_specs = list(out_specs)
    out_shape = list(out_shape)
    scratch_shapes = list(scratch_shapes)
    aliases = dict(input_output_aliases or {})
    n_in, n_out, n_scr = len(in_specs), len(out_shape), len(scratch_shapes)
    if comm is None:
        call = pl.pallas_call(
            body, name=name, grid=grid, in_specs=in_specs, out_specs=out_specs, out_shape=out_shape,
            scratch_shapes=scratch_shapes, input_output_aliases=aliases,
            compiler_params=_params(sem, blocks, temps))
        return lambda *args: (list(call(*args)), [])

    nci, nco = len(comm.ins), len(comm.out_shapes)

    def hosted(*refs):
        ins = refs[:n_in]
        cins = refs[n_in:n_in + nci]
        outs = refs[n_in + nci:n_in + nci + n_out]
        couts = refs[n_in + nci + n_out:n_in + nci + n_out + nco]
        scr = refs[n_in + nci + n_out + nco:n_in + nci + n_out + nco + n_scr]
        csems = refs[n_in + nci + n_out + nco + n_scr:]
        first = None
        last = None
        for q, g in enumerate(grid):
            pid = pl.program_id(q)
            first = (pid == 0) if first is None else first & (pid == 0)
            last = (pid == g - 1) if last is None else last & (pid == g - 1)

        @pl.when(first)
        def _():
            _handshake(comm.peers)
            comm.start(cins, couts, csems)

        body(*ins, *outs, *scr)

        @pl.when(last)
        def _():
            comm.finish(cins, couts, csems)

    for i, o in comm.aliases.items():
        aliases[n_in + i] = n_out + o
    call = pl.pallas_call(
        hosted, name=name, grid=grid, in_specs=in_specs + [ANY] * nci, out_specs=out_specs + [ANY] * nco,
        out_shape=out_shape + comm.out_shapes, scratch_shapes=scratch_shapes + comm.sems,
        input_output_aliases=aliases,
        compiler_params=_params(("arbitrary",) * len(grid), blocks, temps, _collective_id(comm.peers)))

    def run(*args):
        res = call(*args, *comm.ins)
        return list(res[:n_out]), list(res[n_out:])

    return run


def _run_comm(comm, name):
    def body(*refs):
        nci, nco = len(comm.ins), len(comm.out_shapes)
        cins, couts, csems = refs[:nci], refs[nci:nci + nco], refs[nci + nco:]
        _handshake(comm.peers)
        comm.start(cins, couts, csems)
        comm.finish(cins, couts, csems)

    return list(pl.pallas_call(
        body, name=name, in_specs=[ANY] * len(comm.ins), out_specs=[ANY] * len(comm.out_shapes),
        out_shape=comm.out_shapes, scratch_shapes=comm.sems, input_output_aliases=comm.aliases,
        compiler_params=pltpu.CompilerParams(collective_id=_collective_id(comm.peers)),
    )(*comm.ins))


def _dot(a, b):
    return jnp.dot(a, b, preferred_element_type=F32)


def _dot_tb(a, b):
    return lax.dot_general(a, b, (((1,), (1,)), ((), ())), preferred_element_type=F32)


def _dot_ta(a, b):
    return lax.dot_general(a, b, (((0,), (0,)), ((), ())), preferred_element_type=F32)


def _rms_fwd(x):
    inv = lax.rsqrt(jnp.mean(x * x, axis=-1, keepdims=True) + RMS_EPS)
    return x * inv, inv


def _rms_bwd(dy, xhat, inv, g):
    gd = dy * g
    return inv * (gd - xhat * jnp.mean(gd * xhat, axis=-1, keepdims=True))


def _sigmoid(x):
    return 1.0 / (1.0 + jnp.exp(-x))


def _shift_down(x, k, row):
    return jnp.where(row >= k, pltpu.roll(x, k, 0), 0.0)


def _shift_up(x, k, row):
    s = x.shape[0]
    return jnp.where(row < s - k, pltpu.roll(x, s - k, 0), 0.0)


def _pool_fwd(u, win, row):
    s = u
    k = 1
    while k < win:
        s = s + _shift_down(s, k, row)
        k *= 2
    cnt = jnp.minimum(row + 1, win).astype(F32)
    return s / cnt - u


def _pool_bwd(dp, win, row):
    cnt = jnp.minimum(row + 1, win).astype(F32)
    s = dp / cnt
    k = 1
    while k < win:
        s = s + _shift_up(s, k, row)
        k *= 2
    return s - dp


def _acc_over(k, nk, part, acc, o_ref):
    @pl.when(k == 0)
    def _():
        acc[...] = part

    @pl.when(k > 0)
    def _():
        acc[...] += part

    @pl.when(k == nk - 1)
    def _():
        o_ref[...] = acc[...].astype(o_ref.dtype)


def _fwd_in(x, g, w_loc, order, comm):
    t, d = x.shape
    ws = w_loc.shape[1]
    nsh = order.shape[0]
    tm = _tile(t, 1024, SUBLANES_BF16)
    ni = t // tm
    nci, nco = len(comm.ins), len(comm.out_shapes)
    all_peers = comm.peers | frozenset(CHIPS + (SIBLING,))

    def body(order_ref, x_ref, g_ref, loc_ref, *rest):
        del order_ref
        cins = rest[:nci]
        z_ref, h_ref, full_ref = rest[nci:nci + 3]
        couts = rest[nci + 3:nci + 3 + nco]
        hs, wbuf, wsem, own_s, own_r, snd_s, snd_r, fwd_s, fwd_r = rest[nci + 3 + nco:nci + 12 + nco]
        csems = rest[nci + 12 + nco:]
        j = pl.program_id(0)
        i = pl.program_id(1)
        x_, y_, c_ = _coords()
        own = 2 * x_ + y_
        sib = (x_, y_, 1 - c_)
        peers = _peer_chips(x_, y_)

        def sends():
            return [_remote(_half(0, loc_ref, c_), _piece(0, full_ref, own, c_), snd_s.at[p], snd_r.at[p], (px, py, c_))
                    for p, (px, py) in enumerate(peers)]

        def owns():
            return [_remote(_half(0, loc_ref, h), _piece(0, full_ref, own, h), own_s.at[h], own_r.at[h], sib)
                    for h in range(2)]

        def forward(p, half):
            px, py = peers[p]
            landed = _piece(0, full_ref, 2 * px + py, half)
            return _remote(landed, landed, fwd_s.at[p], fwd_r.at[p], sib)

        def load(src, slot):
            return pltpu.make_async_copy(src, wbuf.at[slot], wsem.at[slot])

        @pl.when((j == 0) & (i == 0))
        def _():
            _handshake(all_peers)
            for cp in sends() + owns():
                cp.start()
            load(loc_ref, 0).start()
            comm.start(cins, couts, csems)

        @pl.when(j == 0)
        def _():
            xh, _ = _rms_fwd(x_ref[...])
            h = (xh * g_ref[...]).astype(BF16)
            hs[pl.ds(pl.multiple_of(i * tm, tm), tm), :] = h
            h_ref[...] = h

        slot = j % 2

        @pl.when(i == 0)
        def _():
            load(loc_ref, slot).wait()

        z_ref[...] = _dot(hs[pl.ds(pl.multiple_of(i * tm, tm), tm), :], wbuf[slot]).astype(BF16)

        for p in range(nsh - 1):
            @pl.when((i == ni - 1) & (j == p))
            def _(p=p):
                px, py = peers[p]
                sends()[p].wait_recv()
                forward(p, c_).start()
                forward(p, 1 - c_).wait_recv()
                load(full_ref.at[2 * px + py], 1 - slot).start()

        @pl.when((j == nsh - 1) & (i == ni - 1))
        def _():
            for cp in sends() + [forward(p, c_) for p in range(nsh - 1)]:
                cp.wait_send()
            for cp in owns():
                cp.wait()
            comm.finish(cins, couts, csems)

    last = ni - 1
    blocks = _nbytes((tm, d), F32) + _nbytes((tm, ws), BF16) + _nbytes((tm, d), BF16)
    scratch = _nbytes((t, d), BF16) + 2 * _nbytes((d, ws), BF16)
    res = pl.pallas_call(
        body, name="fwd_in",
        grid_spec=pltpu.PrefetchScalarGridSpec(
            num_scalar_prefetch=1, grid=(nsh, ni),
            in_specs=[pl.BlockSpec((tm, d), lambda j, i, o: (jnp.where(j == 0, i, last), 0)),
                      pl.BlockSpec((1, d), lambda j, i, o: (0, 0)), ANY] + [ANY] * nci,
            out_specs=[pl.BlockSpec((tm, ws), lambda j, i, o: (i, o[j])),
                       pl.BlockSpec((tm, d), lambda j, i, o: (jnp.where(j == 0, i, last), 0)), ANY] + [ANY] * nco,
            scratch_shapes=[pltpu.VMEM((t, d), BF16), pltpu.VMEM((2, d, ws), BF16)]
            + _dma_sems(2, 2, 2, nsh - 1, nsh - 1, nsh - 1, nsh - 1) + comm.sems),
        out_shape=[SDS((t, nsh * ws), BF16), SDS((t, d), BF16), SDS((nsh, d, ws), BF16)] + comm.out_shapes,
        input_output_aliases={4 + i: 3 + o for i, o in comm.aliases.items()},
        compiler_params=_params(("arbitrary", "arbitrary"), blocks, scratch + 3 * _nbytes((tm, d), F32),
                                _collective_id(all_peers)),
    )(order, x, g, w_loc, *comm.ins)
    return list(res[:3]), list(res[3:])


def _mixer_mid_fwd(z, pool_w, pool_scale, conv_w, nseq, comm=None):
    t = z.shape[0]
    d = pool_scale.shape[1]
    s = t // nseq
    c = d // N_GROUPS

    def body(zp, zb, zc, zv, pw, ps, cw, o):
        j = pl.program_id(1)
        row = lax.broadcasted_iota(jnp.int32, (s, c), 0)
        for gi, win in enumerate(POOL_WINDOWS):
            @pl.when(j == gi)
            def _(win=win):
                pooled = _pool_fwd(zp[...].astype(F32), win, row)
                o[0] = (_dot(pooled.astype(BF16), pw[...]) * ps[...]).astype(BF16)

        cv = zc[...].astype(F32) * zv[...].astype(F32)
        cc = (cw[pl.ds(2, 1), :] * cv + cw[pl.ds(1, 1), :] * _shift_down(cv, 1, row)
              + cw[pl.ds(0, 1), :] * _shift_down(cv, 2, row))
        o[1] = (zb[...].astype(F32) * cc).astype(BF16)

    blocks = 4 * _nbytes((s, c), BF16) + _nbytes((c, c), BF16) + _nbytes((2, s, c), BF16)
    return _pcall(
        body, name="mixer_mid_fwd", grid=(nseq, N_GROUPS),
        in_specs=[pl.BlockSpec((s, c), lambda b, j: (b, j)),
                  pl.BlockSpec((s, c), lambda b, j: (b, N_GROUPS + j)),
                  pl.BlockSpec((s, c), lambda b, j: (b, 2 * N_GROUPS + j)),
                  pl.BlockSpec((s, c), lambda b, j: (b, 3 * N_GROUPS + j)),
                  pl.BlockSpec((None, c, c), lambda b, j: (j, 0, 0)),
                  pl.BlockSpec((1, c), lambda b, j: (0, j)),
                  pl.BlockSpec((3, c), lambda b, j: (0, j))],
        out_specs=[pl.BlockSpec((2, s, c), lambda b, j: (0, b, j))],
        out_shape=[SDS((3, t, d), BF16)],
        sem=("parallel", "parallel"), blocks=blocks, temps=8 * _nbytes((s, c), F32), comm=comm,
    )(z, z, z, z, pool_w, pool_scale, conv_w)


def _mixer_out(lhs3, z, x, w3, g_ffn, comm=None):
    t, d = x.shape
    tm = _tile(t, 256, SUBLANES_BF16)

    def body(pq, zgp, zgc, x_ref, w_ref, g_ref, mrg, ypc, x1o, h2o):
        yp = _dot(pq[0], w_ref[0])
        yc = _dot(pq[1], w_ref[1])
        m = _sigmoid(zgp[...].astype(F32)) * yp + _sigmoid(zgc[...].astype(F32)) * yc
        mb = m.astype(BF16)
        x1 = x_ref[...] + _dot(mb, w_ref[2])
        ypc[0] = yp.astype(BF16)
        ypc[1] = yc.astype(BF16)
        mrg[...] = mb
        x1o[...] = x1
        xh, _ = _rms_fwd(x1)
        h2o[...] = (xh * g_ref[...]).astype(BF16)

    blocks = (_nbytes((2, tm, d), BF16) * 2 + _nbytes((tm, d), BF16) * 4 + _nbytes((tm, d), F32) * 2
              + _nbytes((3, d, d), BF16))
    return _pcall(
        body, name="mixer_out", grid=(t // tm,),
        in_specs=[pl.BlockSpec((2, tm, d), lambda i: (0, i, 0)),
                  pl.BlockSpec((tm, d), lambda i: (i, 4)),
                  pl.BlockSpec((tm, d), lambda i: (i, 5)),
                  pl.BlockSpec((tm, d), lambda i: (i, 0)),
                  pl.BlockSpec((3, d, d), lambda i: (0, 0, 0)),
                  pl.BlockSpec((1, d), lambda i: (0, 0))],
        out_specs=[pl.BlockSpec((None, tm, d), lambda i: (2, i, 0)),
                   pl.BlockSpec((2, tm, d), lambda i: (0, i, 0)),
                   pl.BlockSpec((tm, d), lambda i: (i, 0)),
                   pl.BlockSpec((tm, d), lambda i: (i, 0))],
        out_shape=[SDS(lhs3.shape, BF16), SDS((2, t, d), BF16), SDS((t, d), F32), SDS((t, d), BF16)],
        input_output_aliases={0: 0},
        sem=("parallel",), blocks=blocks, temps=8 * _nbytes((tm, d), F32), comm=comm,
    )(lhs3, z, z, x, w3, g_ffn)


def _ffn_up(h2, w_up, f, comm=None):
    t, d = h2.shape
    _, _, ws = w_up.shape
    tm = _tile(t, 1024, SUBLANES_BF16)
    tn = _tile(ws, 1408, LANES)
    nps = ws // tn
    npp = f // tn

    def body(h_ref, w_ref, o_ref):
        o_ref[...] = _dot(h_ref[...], w_ref[...]).astype(BF16)

    blocks = _nbytes((tm, d), BF16) + _nbytes((d, tn), BF16) + _nbytes((tm, tn), BF16)
    return _pcall(
        body, name="ffn_up", grid=(t // tm, 2 * npp),
        in_specs=[pl.BlockSpec((tm, d), lambda i, j: (i, 0)),
                  pl.BlockSpec((None, d, tn), lambda i, j: (j // nps, 0, j % nps))],
        out_specs=[pl.BlockSpec((None, tm, tn), lambda i, j: (j // npp, i, j % npp))],
        out_shape=[SDS((2, t, f), BF16)],
        sem=("parallel", "parallel"), blocks=blocks, temps=_nbytes((tm, tn), F32), comm=comm,
    )(h2, w_up)


def _conv3_rows(u, u1, u2, w_ref, p):
    return w_ref[p, pl.ds(2, 1), :] * u + w_ref[p, pl.ds(1, 1), :] * u1 + w_ref[p, pl.ds(0, 1), :] * u2


WGRAD_TOKENS = 2048
CHUNK = 64
HALO = SUBLANES_F32


def _up1_up2(u, nxt):
    rows = u.shape[0]
    ext = jnp.concatenate([u, nxt], axis=0)
    n = rows + HALO
    return pltpu.roll(ext, n - 1, 0)[:rows], pltpu.roll(ext, n - 2, 0)[:rows]


def _fold8(x):
    return jnp.sum(x.reshape(x.shape[0] // SUBLANES_F32, SUBLANES_F32, x.shape[1]), axis=0)


def _ffn_mid_fwd(u0, cw, cb, nseq):
    _, t, f = u0.shape
    s = t // nseq
    c = _tile(f, 256, LANES)

    def body(u_ref, w_ref, b_ref, a_ref, uo_ref):
        row = lax.broadcasted_iota(jnp.int32, (s, c), 0)
        act = []
        for p in range(2):
            u = u_ref[p].astype(F32)
            act.append(_conv3_rows(u, _shift_down(u, 1, row), _shift_down(u, 2, row), w_ref, p) + b_ref[p])
            uo_ref[p] = act[p].astype(BF16)
        ug, uv = act
        a_ref[...] = (ug * _sigmoid(ug) * uv).astype(BF16)

    blocks = 2 * _nbytes((2, s, c), BF16) + _nbytes((s, c), BF16)
    outs, _ = _pcall(
        body, name="ffn_mid_fwd", grid=(f // c, nseq),
        in_specs=[pl.BlockSpec((2, s, c), lambda j, b: (0, b, j)),
                  pl.BlockSpec((2, 3, c), lambda j, b: (0, 0, j)),
                  pl.BlockSpec((2, 1, c), lambda j, b: (0, 0, j))],
        out_specs=[pl.BlockSpec((s, c), lambda j, b: (b, j)),
                   pl.BlockSpec((2, s, c), lambda j, b: (0, b, j))],
        out_shape=[SDS((t, f), BF16), SDS((2, t, f), BF16)],
        sem=("parallel", "parallel"), blocks=blocks, temps=8 * _nbytes((s, c), F32),
    )(u0, cw, cb)
    return outs


def _ffn_down_loss(a, w_down, x1, tgt, g_fin):
    t, f = a.shape
    d = x1.shape[1]
    tm = _tile(t, 256, SUBLANES_BF16)
    nsteps = t // tm

    def body(a_ref, w_ref, x1_ref, t_ref, g_ref, dx_ref, dxb_ref, loss_ref, gg_ref, lacc):
        i = pl.program_id(0)

        @pl.when(i == 0)
        def _():
            lacc[...] = jnp.zeros_like(lacc)
            gg_ref[...] = jnp.zeros_like(gg_ref)

        x2 = x1_ref[...] + _dot(a_ref[...], w_ref[...])
        xh, inv = _rms_fwd(x2)
        g = g_ref[...]
        e = xh * g - t_ref[...]
        lacc[...] += jnp.sum(e * e, axis=0, keepdims=True)
        dy = e * (1.0 / d)
        gg_ref[...] += jnp.sum(dy * xh, axis=0, keepdims=True)
        dx2 = _rms_bwd(dy, xh, inv, g)
        dx_ref[...] = dx2
        dxb_ref[...] = dx2.astype(BF16)

        @pl.when(i == nsteps - 1)
        def _():
            loss_ref[...] = jnp.sum(lacc[...], axis=1, keepdims=True) * (0.5 / d)

    blocks = (_nbytes((tm, f), BF16) + _nbytes((f, d), BF16) + 3 * _nbytes((tm, d), F32) + _nbytes((tm, d), BF16))
    outs, _ = _pcall(
        body, name="ffn_down_loss", grid=(nsteps,),
        in_specs=[pl.BlockSpec((tm, f), lambda i: (i, 0)), pl.BlockSpec((f, d), lambda i: (0, 0)),
                  pl.BlockSpec((tm, d), lambda i: (i, 0)), pl.BlockSpec((tm, d), lambda i: (i, 0)),
                  pl.BlockSpec((1, d), lambda i: (0, 0))],
        out_specs=[pl.BlockSpec((tm, d), lambda i: (i, 0)), pl.BlockSpec((tm, d), lambda i: (i, 0)),
                   pl.BlockSpec((1, 1), lambda i: (0, 0)), pl.BlockSpec((1, d), lambda i: (0, 0))],
        out_shape=[SDS((t, d), F32), SDS((t, d), BF16), SDS((1, 1), F32), SDS((1, d), F32)],
        scratch_shapes=[pltpu.VMEM((1, d), F32)],
        sem=("arbitrary",), blocks=blocks, temps=8 * _nbytes((tm, d), F32),
    )(a, w_down, x1, tgt, g_fin)
    return outs


def _ffn_bwd_da(dxb, w_down, comm=None):
    t, d = dxb.shape
    f = w_down.shape[0]
    tm = _tile(t, 1024, SUBLANES_BF16)
    tn = _tile(f, 1408, LANES)

    def body(x_ref, w_ref, o_ref):
        o_ref[...] = _dot_tb(x_ref[...], w_ref[...]).astype(BF16)

    blocks = _nbytes((tm, d), BF16) + _nbytes((tn, d), BF16) + _nbytes((tm, tn), BF16)
    return _pcall(
        body, name="ffn_bwd_da", grid=(t // tm, f // tn),
        in_specs=[pl.BlockSpec((tm, d), lambda i, j: (i, 0)), pl.BlockSpec((tn, d), lambda i, j: (j, 0))],
        out_specs=[pl.BlockSpec((tm, tn), lambda i, j: (i, j))],
        out_shape=[SDS((t, f), BF16)],
        sem=("parallel", "parallel"), blocks=blocks, temps=_nbytes((tm, tn), F32), comm=comm,
    )(dxb, w_down)


def _ffn_mid_bwd(da, u0, ua, cw, nseq, comm=None):
    _, t, f = u0.shape
    s = t // nseq
    c = _tile(f, 128, LANES)
    r = _tile(s, CHUNK, SUBLANES_BF16)
    n = s // r

    def body(da_ref, u_ref, ua_ref, w_ref, du_ref, gw_ref, gb_ref):
        @pl.when(pl.program_id(1) == 0)
        def _():
            gw_ref[...] = jnp.zeros_like(gw_ref)
            gb_ref[...] = jnp.zeros_like(gb_ref)

        def step(i, carry):
            nxt, sums = carry
            rows = pl.ds(pl.multiple_of((n - 1 - i) * r, r), r)
            ug = ua_ref[0, rows, :].astype(F32)
            uv = ua_ref[1, rows, :].astype(F32)
            sg = _sigmoid(ug)
            dacc = da_ref[rows, :].astype(F32)
            dus = (dacc * uv * sg * (1.0 + ug * (1.0 - sg)), dacc * (ug * sg))
            first, new_sums = [], []
            for p in range(2):
                du = dus[p]
                d1, d2 = _up1_up2(du, nxt[p])
                du_ref[p, rows, :] = _conv3_rows(du, d1, d2, w_ref, p).astype(BF16)
                u = u_ref[p, rows, :].astype(F32)
                sb, s0, s1, s2 = sums[p]
                new_sums.append((sb + _fold8(du), s0 + _fold8(d2 * u), s1 + _fold8(d1 * u), s2 + _fold8(du * u)))
                first.append(du[:HALO])
            return tuple(first), tuple(new_sums)

        zero = jnp.zeros((HALO, c), F32)
        _, sums = lax.fori_loop(0, n, step, ((zero, zero), ((zero,) * 4,) * 2))
        for p in range(2):
            sb, s0, s1, s2 = sums[p]
            gb_ref[p] += jnp.sum(sb, axis=0, keepdims=True)
            gw_ref[p, pl.ds(0, 1), :] += jnp.sum(s0, axis=0, keepdims=True)
            gw_ref[p, pl.ds(1, 1), :] += jnp.sum(s1, axis=0, keepdims=True)
            gw_ref[p, pl.ds(2, 1), :] += jnp.sum(s2, axis=0, keepdims=True)

    blocks = _nbytes((s, c), BF16) + 3 * _nbytes((2, s, c), BF16)
    return _pcall(
        body, name="ffn_mid_bwd", grid=(f // c, nseq),
        in_specs=[pl.BlockSpec((s, c), lambda j, b: (b, j)),
                  pl.BlockSpec((2, s, c), lambda j, b: (0, b, j)),
                  pl.BlockSpec((2, s, c), lambda j, b: (0, b, j)),
                  pl.BlockSpec((2, 3, c), lambda j, b: (0, 0, j))],
        out_specs=[pl.BlockSpec((2, s, c), lambda j, b: (0, b, j)),
                   pl.BlockSpec((2, 3, c), lambda j, b: (0, 0, j)),
                   pl.BlockSpec((2, 1, c), lambda j, b: (0, 0, j))],
        out_shape=[SDS((2, t, f), BF16), SDS((2, 3, f), F32), SDS((2, 1, f), F32)],
        sem=("parallel", "arbitrary"), blocks=blocks, temps=4 * 1024 * 1024, comm=comm,
    )(da, u0, ua, cw)


def _wgrad(a, b, name, *, tr, tn, b_plane_of=None, out_shards=None, comm=None):
    t, m = a.shape
    n_total = b.shape[-1] * (b.shape[0] if b.ndim == 3 else 1)
    tk = _tile(t, WGRAD_TOKENS, SUBLANES_BF16)
    nk = t // tk

    def body(a_ref, b_ref, o_ref, acc):
        _acc_over(pl.program_id(2), nk, _dot_ta(a_ref[...], b_ref[...]), acc, o_ref)

    if b.ndim == 3:
        b_spec = pl.BlockSpec((None, tk, tn), lambda r, n, k: (b_plane_of(n)[0], k, b_plane_of(n)[1]))
    else:
        b_spec = pl.BlockSpec((tk, tn), lambda r, n, k: (k, n))
    if out_shards is None:
        o_spec = pl.BlockSpec((tr, tn), lambda r, n, k: (r, n))
        o_shape = SDS((m, n_total), BF16)
    else:
        nps = n_total // out_shards // tn
        o_spec = pl.BlockSpec((None, tr, tn), lambda r, n, k: (n // nps, r, n % nps))
        o_shape = SDS((out_shards, m, n_total // out_shards), BF16)
    blocks = _nbytes((tk, tr), BF16) + _nbytes((tk, tn), BF16) + _nbytes((tr, tn), BF16)
    return _pcall(
        body, name=name, grid=(m // tr, n_total // tn, nk),
        in_specs=[pl.BlockSpec((tk, tr), lambda r, n, k: (k, r)), b_spec],
        out_specs=[o_spec], out_shape=[o_shape],
        scratch_shapes=[pltpu.VMEM((tr, tn), F32)],
        sem=("parallel", "parallel", "arbitrary"), blocks=blocks, temps=2 * _nbytes((tr, tn), F32), comm=comm,
    )(a, b)


def _wgrad3(lhs3, rhs3, comm=None):
    nw, t, d = lhs3.shape
    tk = _tile(t, WGRAD_TOKENS, SUBLANES_BF16)
    nk = t // tk

    def body(a_ref, b_ref, o_ref, acc):
        _acc_over(pl.program_id(1), nk, _dot_ta(a_ref[...], b_ref[...]), acc, o_ref)

    blocks = 2 * _nbytes((tk, d), BF16) + _nbytes((d, d), BF16)
    return _pcall(
        body, name="wgrad_sq3", grid=(nw, nk),
        in_specs=[pl.BlockSpec((None, tk, d), lambda w, k: (w, k, 0)),
                  pl.BlockSpec((None, tk, d), lambda w, k: (w, k, 0))],
        out_specs=[pl.BlockSpec((None, d, d), lambda w, k: (w, 0, 0))],
        out_shape=[SDS((nw, d, d), BF16)],
        scratch_shapes=[pltpu.VMEM((d, d), F32)],
        sem=("parallel", "arbitrary"), blocks=blocks, temps=2 * _nbytes((d, d), F32), comm=comm,
    )(lhs3, rhs3)


def _ffn_bwd_dx1(du0, w_up, x1, dx2, g_ffn, n_planes_out, comm=None):
    _, t, f = du0.shape
    d = x1.shape[1]
    nsh, _, ws = w_up.shape
    tm = _tile(t, 256, SUBLANES_BF16)
    spp = f // ws

    def body(du_ref, w_ref, x1_ref, dx2_ref, g_ref, dx1_ref, dxb_ref, gg_ref):
        @pl.when(pl.program_id(0) == 0)
        def _():
            gg_ref[...] = jnp.zeros_like(gg_ref)

        dh = None
        for k in range(nsh):
            part = _dot_tb(du_ref[k // spp, :, (k % spp) * ws:(k % spp + 1) * ws], w_ref[k])
            dh = part if dh is None else dh + part
        xh, inv = _rms_fwd(x1_ref[...])
        gg_ref[...] += jnp.sum(dh * xh, axis=0, keepdims=True)
        dx1 = dx2_ref[...] + _rms_bwd(dh, xh, inv, g_ref[...])
        dx1_ref[...] = dx1
        dxb_ref[...] = dx1.astype(BF16)

    blocks = _nbytes((2, tm, f), BF16) + 3 * _nbytes((tm, d), F32) + _nbytes((tm, d), BF16)
    return _pcall(
        body, name="ffn_bwd_dx1", grid=(t // tm,),
        in_specs=[pl.BlockSpec((2, tm, f), lambda i: (0, i, 0)),
                  pl.BlockSpec((nsh, d, ws), lambda i: (0, 0, 0), pipeline_mode=pl.Buffered(1)),
                  pl.BlockSpec((tm, d), lambda i: (i, 0)),
                  pl.BlockSpec((tm, d), lambda i: (i, 0)),
                  pl.BlockSpec((1, d), lambda i: (0, 0))],
        out_specs=[pl.BlockSpec((tm, d), lambda i: (i, 0)),
                   pl.BlockSpec((None, tm, d), lambda i: (n_planes_out - 1, i, 0)),
                   pl.BlockSpec((1, d), lambda i: (0, 0))],
        out_shape=[SDS((t, d), F32), SDS((n_planes_out, t, d), BF16), SDS((1, d), F32)],
        sem=("arbitrary",), blocks=blocks, temps=_nbytes(w_up.shape, BF16) + 8 * _nbytes((tm, d), F32), comm=comm,
    )(du0, w_up, x1, dx2, g_ffn)


def _mixer_bwd(rhs3, z, ypc, w3, comm=None):
    _, t, d = rhs3.shape
    tm = _tile(t, 256, SUBLANES_BF16)

    def body(dx_ref, zgp, zgc, ypc_ref, w_ref, dyo, dzo, dpq):
        dm = _dot_tb(dx_ref[...], w_ref[2])
        sp = _sigmoid(zgp[...].astype(F32))
        sc = _sigmoid(zgc[...].astype(F32))
        dyp = (dm * sp).astype(BF16)
        dyc = (dm * sc).astype(BF16)
        dzo[0] = (dm * ypc_ref[0].astype(F32) * sp * (1.0 - sp)).astype(BF16)
        dzo[1] = (dm * ypc_ref[1].astype(F32) * sc * (1.0 - sc)).astype(BF16)
        dyo[0] = dyp
        dyo[1] = dyc
        dpq[0] = _dot_tb(dyp, w_ref[0]).astype(BF16)
        dpq[1] = _dot_tb(dyc, w_ref[1]).astype(BF16)

    blocks = _nbytes((tm, d), BF16) * 3 + _nbytes((2, tm, d), BF16) * 4 + _nbytes((3, d, d), BF16)
    return _pcall(
        body, name="mixer_bwd", grid=(t // tm,),
        in_specs=[pl.BlockSpec((None, tm, d), lambda i: (2, i, 0)),
                  pl.BlockSpec((tm, d), lambda i: (i, 4)),
                  pl.BlockSpec((tm, d), lambda i: (i, 5)),
                  pl.BlockSpec((2, tm, d), lambda i: (0, i, 0)),
                  pl.BlockSpec((3, d, d), lambda i: (0, 0, 0))],
        out_specs=[pl.BlockSpec((2, tm, d), lambda i: (0, i, 0)),
                   pl.BlockSpec((2, tm, d), lambda i: (2, i, 0)),
                   pl.BlockSpec((2, tm, d), lambda i: (0, i, 0))],
        out_shape=[SDS(rhs3.shape, BF16), SDS((N_SPLITS, t, d), BF16), SDS((2, t, d), BF16)],
        input_output_aliases={0: 0},
        sem=("parallel",), blocks=blocks, temps=8 * _nbytes((tm, d), F32), comm=comm,
    )(rhs3, z, z, ypc, w3)


def _conv_bwd(dz, dpq, z, conv_w, nseq, comm=None):
    _, t, d = dz.shape
    s = t // nseq
    c = _tile(d, 128, LANES)
    nb = d // c

    def body(dz_in, dq_ref, zb, zc, zv, cw, dzo, gw_ref):
        del dz_in

        @pl.when(pl.program_id(1) == 0)
        def _():
            gw_ref[...] = jnp.zeros_like(gw_ref)

        row = lax.broadcasted_iota(jnp.int32, (s, c), 0)
        b = zb[...].astype(F32)
        cm = zc[...].astype(F32)
        v = zv[...].astype(F32)
        cv = cm * v
        cv1 = _shift_down(cv, 1, row)
        cv2 = _shift_down(cv, 2, row)
        w0, w1, w2 = cw[pl.ds(0, 1), :], cw[pl.ds(1, 1), :], cw[pl.ds(2, 1), :]
        cc = w2 * cv + w1 * cv1 + w0 * cv2
        dq = dq_ref[...].astype(F32)
        dzo[0] = (dq * cc).astype(BF16)
        dcc = dq * b
        gw_ref[pl.ds(0, 1), :] += jnp.sum(dcc * cv2, axis=0, keepdims=True)
        gw_ref[pl.ds(1, 1), :] += jnp.sum(dcc * cv1, axis=0, keepdims=True)
        gw_ref[pl.ds(2, 1), :] += jnp.sum(dcc * cv, axis=0, keepdims=True)
        dcv = w2 * dcc + w1 * _shift_up(dcc, 1, row) + w0 * _shift_up(dcc, 2, row)
        dzo[1] = (dcv * v).astype(BF16)
        dzo[2] = (dcv * cm).astype(BF16)

    blocks = 4 * _nbytes((s, c), BF16) + _nbytes((3, s, c), BF16)
    return _pcall(
        body, name="conv_bwd", grid=(nb, nseq),
        in_specs=[ANY,
                  pl.BlockSpec((None, s, c), lambda j, b: (1, b, j)),
                  pl.BlockSpec((s, c), lambda j, b: (b, nb + j)),
                  pl.BlockSpec((s, c), lambda j, b: (b, 2 * nb + j)),
                  pl.BlockSpec((s, c), lambda j, b: (b, 3 * nb + j)),
                  pl.BlockSpec((3, c), lambda j, b: (0, j))],
        out_specs=[pl.BlockSpec((3, s, c), lambda j, b: (0, b, j)),
                   pl.BlockSpec((3, c), lambda j, b: (0, j))],
        out_shape=[SDS(dz.shape, BF16), SDS((3, d), F32)],
        input_output_aliases={0: 0},
        sem=("parallel", "arbitrary"), blocks=blocks, temps=16 * _nbytes((s, c), F32), comm=comm,
    )(dz, dpq, z, z, z, conv_w)


def _pool_bwd_call(dz, dpq, z, pool_w, pool_scale, nseq, comm=None):
    _, t, d = dz.shape
    s = t // nseq
    c = d // N_GROUPS

    def body(dz_in, dp_ref, zp, pw, ps, dzo, gpw_ref, gps_ref):
        del dz_in
        j = pl.program_id(0)

        @pl.when(pl.program_id(1) == 0)
        def _():
            gpw_ref[...] = jnp.zeros_like(gpw_ref)
            gps_ref[...] = jnp.zeros_like(gps_ref)

        row = lax.broadcasted_iota(jnp.int32, (s, c), 0)
        for gi, win in enumerate(POOL_WINDOWS):
            @pl.when(j == gi)
            def _(win=win):
                pb = _pool_fwd(zp[...].astype(F32), win, row).astype(BF16)
                plin = _dot(pb, pw[...])
                dps = dp_ref[...].astype(F32)
                gps_ref[...] += jnp.sum(dps * plin, axis=0, keepdims=True)
                dplb = (dps * ps[...]).astype(BF16)
                gpw_ref[...] += _dot_ta(pb, dplb)
                dzo[...] = _pool_bwd(_dot_tb(dplb, pw[...]), win, row).astype(BF16)

    blocks = 3 * _nbytes((s, c), BF16) + _nbytes((c, c), BF16) + _nbytes((c, c), F32)
    return _pcall(
        body, name="pool_bwd", grid=(N_GROUPS, nseq),
        in_specs=[ANY,
                  pl.BlockSpec((None, s, c), lambda j, b: (0, b, j)),
                  pl.BlockSpec((s, c), lambda j, b: (b, j)),
                  pl.BlockSpec((None, c, c), lambda j, b: (j, 0, 0)),
                  pl.BlockSpec((1, c), lambda j, b: (0, j))],
        out_specs=[pl.BlockSpec((None, s, c), lambda j, b: (3, b, j)),
                   pl.BlockSpec((None, c, c), lambda j, b: (j, 0, 0)),
                   pl.BlockSpec((1, c), lambda j, b: (0, j))],
        out_shape=[SDS(dz.shape, BF16), SDS((N_GROUPS, c, c), F32), SDS((1, d), F32)],
        input_output_aliases={0: 0},
        sem=("parallel", "arbitrary"), blocks=blocks, temps=10 * _nbytes((s, c), F32), comm=comm,
    )(dz, dpq, z, pool_w, pool_scale)


def _dz_plane(zb):
    return jnp.where(zb < 4, (zb + 3) % 4, zb)


def _wgrad_in(h1, dz, nsh, comm=None):
    t, d = h1.shape
    ws = N_SPLITS * d // nsh
    kb = _tile(math.gcd(d, ws), 512, LANES)
    npl = d // kb
    nps = ws // kb
    tk = _tile(t, WGRAD_TOKENS, SUBLANES_BF16)
    nk = t // tk

    def body(a_ref, b_ref, o_ref, acc):
        _acc_over(pl.program_id(1), nk, _dot_ta(a_ref[...], b_ref[...]), acc, o_ref)

    blocks = _nbytes((tk, d), BF16) + _nbytes((tk, kb), BF16) + _nbytes((d, kb), BF16)
    return _pcall(
        body, name="wgrad_in", grid=(N_SPLITS * npl, nk),
        in_specs=[pl.BlockSpec((tk, d), lambda cb, k: (k, 0)),
                  pl.BlockSpec((None, tk, kb), lambda cb, k: (_dz_plane(cb // npl), k, cb % npl))],
        out_specs=[pl.BlockSpec((None, d, kb), lambda cb, k: (cb // nps, 0, cb % nps))],
        out_shape=[SDS((nsh, d, ws), BF16)],
        scratch_shapes=[pltpu.VMEM((d, kb), F32)],
        sem=("parallel", "arbitrary"), blocks=blocks, temps=2 * _nbytes((d, kb), F32), comm=comm,
    )(h1, dz)


def _mixer_bwd_dx(dz, w_in, x, dx1, g_mix, comm=None):
    npln, t, d = dz.shape
    nsh, _, ws = w_in.shape
    tm = _tile(t, 256, SUBLANES_BF16)
    kb = _tile(math.gcd(d, ws), 512, LANES)
    npl = d // kb
    nps = ws // kb

    def body(dz_ref, w_ref, x_ref, dx1_ref, g_ref, dx_ref, gg_ref):
        @pl.when(pl.program_id(0) == 0)
        def _():
            gg_ref[...] = jnp.zeros_like(gg_ref)

        dh = None
        for cb in range(npln * npl):
            zb = cb // npl
            plane = (zb + 3) % 4 if zb < 4 else zb
            part = _dot_tb(dz_ref[plane, :, (cb % npl) * kb:(cb % npl + 1) * kb],
                           w_ref[cb // nps, :, (cb % nps) * kb:(cb % nps + 1) * kb])
            dh = part if dh is None else dh + part
        xh, inv = _rms_fwd(x_ref[...])
        gg_ref[...] += jnp.sum(dh * xh, axis=0, keepdims=True)
        dx_ref[...] = dx1_ref[...] + _rms_bwd(dh, xh, inv, g_ref[...])

    blocks = _nbytes((npln, tm, d), BF16) + 3 * _nbytes((tm, d), F32)
    return _pcall(
        body, name="mixer_bwd_dx", grid=(t // tm,),
        in_specs=[pl.BlockSpec((npln, tm, d), lambda i: (0, i, 0)),
                  pl.BlockSpec((nsh, d, ws), lambda i: (0, 0, 0), pipeline_mode=pl.Buffered(1)),
                  pl.BlockSpec((tm, d), lambda i: (i, 0)),
                  pl.BlockSpec((tm, d), lambda i: (i, 0)),
                  pl.BlockSpec((1, d), lambda i: (0, 0))],
        out_specs=[pl.BlockSpec((tm, d), lambda i: (i, 0)),
                   pl.BlockSpec((1, d), lambda i: (0, 0))],
        out_shape=[SDS((t, d), F32), SDS((1, d), F32)],
        sem=("arbitrary",), blocks=blocks, temps=_nbytes(w_in.shape, BF16) + 8 * _nbytes((tm, d), F32), comm=comm,
    )(dz, w_in, x, dx1, g_mix)


N_BIG = 5
SHARD_MAJOR = (0, 2)
ROWS_DIM1 = (1, 4)


def _ds(start, size, align):
    if isinstance(start, int):
        return pl.ds(start, size)
    return pl.ds(pl.multiple_of(start, align), size)


def _piece(a, ref, k, h):
    if a in SHARD_MAJOR:
        r = ref.shape[1] // 2
        return ref.at[k, _ds(h * r, r, SUBLANES_BF16), :]
    if a in ROWS_DIM1:
        r = ref.shape[1] // 8
        return ref.at[:, _ds((2 * k + h) * r, r, SUBLANES_BF16), :]
    r = ref.shape[0] // 8
    return ref.at[_ds((2 * k + h) * r, r, SUBLANES_BF16), :]


def _half(a, ref, h):
    if a in ROWS_DIM1:
        r = ref.shape[1] // 2
        return ref.at[:, _ds(h * r, r, SUBLANES_BF16), :]
    r = ref.shape[0] // 2
    return ref.at[_ds(h * r, r, SUBLANES_BF16), :]


def _piece_shape(a, full_shape):
    if a in SHARD_MAJOR:
        return (full_shape[1] // 2, full_shape[2])
    if a in ROWS_DIM1:
        return (full_shape[0], full_shape[1] // 8, full_shape[2])
    return (full_shape[0] // 8, full_shape[1])


def _shard_shape(a, full_shape):
    if a in SHARD_MAJOR:
        return (full_shape[1], full_shape[2])
    if a in ROWS_DIM1:
        return (full_shape[0], full_shape[1] // 4, full_shape[2])
    return (full_shape[0] // 4, full_shape[1])


def _rows_axis(a):
    return 1 if a in ROWS_DIM1 else 0


def _piece_block(a, full_shape):
    ps = _piece_shape(a, full_shape)
    if a in SHARD_MAJOR:
        return (None,) + ps, lambda k, c: (k, c, 0)
    if a in ROWS_DIM1:
        return ps, lambda k, c: (0, 2 * k + c, 0)
    return ps, lambda k, c: (2 * k + c, 0)


def _coords():
    return lax.axis_index("x"), lax.axis_index("y"), lax.axis_index("c")


def _peer_chips(x, y):
    return [(1 - x, y), (x, 1 - y), (1 - x, 1 - y)]


def _remote(src, dst, ssem, rsem, dev):
    return pltpu.make_async_remote_copy(src_ref=src, dst_ref=dst, send_sem=ssem, recv_sem=rsem,
                                        device_id=dev, device_id_type=MESH)


def _dma_sems(*counts):
    return [pltpu.SemaphoreType.DMA((n,)) for n in counts]


def _symmetric(ins, out_shapes, sems, copies, peers, aliases=None):
    def start(cins, couts, csems):
        for cp in copies(cins, couts, csems):
            cp.start()

    def finish(cins, couts, csems):
        for cp in copies(cins, couts, csems):
            cp.wait()

    return _Comm(ins, out_shapes, sems, start, finish, peers, aliases)


def _rows_part(a, ref, part):
    if part is None:
        return ref
    p, q, n = part
    ax = _rows_axis(a)
    r = ref.shape[ax] // n
    return ref.at[tuple(pl.ds(p * r, (q - p) * r) if d == ax else slice(None) for d in range(len(ref.shape)))]


def _merge(comms):
    ins, outs, sems, aliases, spans = [], [], [], {}, []
    for cm in comms:
        spans.append((len(ins), len(outs), len(sems)))
        for i, o in cm.aliases.items():
            aliases[len(ins) + i] = len(outs) + o
        ins += cm.ins
        outs += cm.out_shapes
        sems += cm.sems

    def each(fn_name):
        def run(cins, couts, csems):
            for cm, (i0, o0, s0) in zip(comms, spans):
                getattr(cm, fn_name)(cins[i0:i0 + len(cm.ins)], couts[o0:o0 + len(cm.out_shapes)],
                                     csems[s0:s0 + len(cm.sems)])
        return run

    return _Comm(ins, outs, sems, each("start"), each("finish"), frozenset().union(*[cm.peers for cm in comms]),
                 aliases)


def _gather_comm(arrs, locs, full_shapes, part=None, into=None):
    n = len(arrs)

    def own(cins, couts, csems):
        x, y, c = _coords()
        j = 2 * x + y
        return [_remote(_rows_part(a, _half(a, cins[q], h), part), _rows_part(a, _piece(a, couts[q], j, h), part),
                        csems[0].at[2 * q + h], csems[1].at[2 * q + h], (x, y, 1 - c))
                for q, a in enumerate(arrs) for h in range(2)]

    def sends(cins, couts, csems):
        x, y, c = _coords()
        j = 2 * x + y
        return [_remote(_rows_part(a, _half(a, cins[q], c), part), _rows_part(a, _piece(a, couts[q], j, c), part),
                        csems[2].at[3 * q + i], csems[3].at[3 * q + i], (px, py, c))
                for q, a in enumerate(arrs) for i, (px, py) in enumerate(_peer_chips(x, y))]

    def forwards(couts, csems, half_of):
        x, y, c = _coords()
        out = []
        for q, a in enumerate(arrs):
            for i, (px, py) in enumerate(_peer_chips(x, y)):
                landed = _rows_part(a, _piece(a, couts[q], 2 * px + py, half_of(c)), part)
                out.append(_remote(landed, landed, csems[4].at[3 * q + i], csems[5].at[3 * q + i], (x, y, 1 - c)))
        return out

    def start(cins, couts, csems):
        for cp in sends(cins, couts, csems) + own(cins, couts, csems):
            cp.start()

    def finish(cins, couts, csems):
        fw = forwards(couts, csems, lambda c: c)
        for cp, f in zip(sends(cins, couts, csems), fw):
            cp.wait_recv()
            f.start()
        for f in forwards(couts, csems, lambda c: 1 - c):
            f.wait_recv()
        for cp in sends(cins, couts, csems) + fw:
            cp.wait_send()
        for cp in own(cins, couts, csems):
            cp.wait()

    ins = [locs[a] for a in arrs] + ([into[a] for a in arrs] if into else [])
    return _Comm(ins, [SDS(full_shapes[a], BF16) for a in arrs],
                 _dma_sems(2 * n, 2 * n, 3 * n, 3 * n, 3 * n, 3 * n), start, finish, CHIPS + (SIBLING,),
                 aliases={n + q: q for q in range(n)} if into else None)


def _halves_comm(arrs, gbs):
    n = len(arrs)

    def copies(cins, couts, csems):
        x, y, c = _coords()
        return [_remote(_piece(a, cins[q], k, 1 - c), couts[q].at[k], csems[0].at[4 * q + k], csems[1].at[4 * q + k],
                        (x, y, 1 - c)) for q, a in enumerate(arrs) for k in range(4)]

    return _symmetric([gbs[a] for a in arrs], [SDS((4,) + _piece_shape(a, gbs[a].shape), BF16) for a in arrs],
                      _dma_sems(4 * n, 4 * n), copies, [SIBLING])


def _chips_comm(arrs, ps, part=None, into=None):
    n = len(arrs)

    def copies(cins, couts, csems):
        x, y, c = _coords()
        return [_remote(_rows_part(a, cins[q].at[2 * px + py], part), _rows_part(a, couts[q].at[i], part),
                        csems[0].at[3 * q + i], csems[1].at[3 * q + i], (px, py, c))
                for q, a in enumerate(arrs) for i, (px, py) in enumerate(_peer_chips(x, y))]

    ins = [ps[a] for a in arrs] + ([into[a] for a in arrs] if into else [])
    return _symmetric(ins, [SDS((3,) + ps[a].shape[1:], BF16) for a in arrs], _dma_sems(3 * n, 3 * n), copies, CHIPS,
                      aliases={n + q: q for q in range(n)} if into else None)


def _result_comm(arrs, gs):
    n = len(arrs)

    def copies(cins, couts, csems):
        x, y, c = _coords()
        return [_remote(_half(a, cins[q], c), _half(a, couts[q], c), csems[0].at[q], csems[1].at[q], (x, y, 1 - c))
                for q, a in enumerate(arrs)]

    return _symmetric([gs[a] for a in arrs], [SDS(gs[a].shape, F32) for a in arrs], _dma_sems(n, n), copies,
                      [SIBLING], aliases={q: q for q in range(n)})


def _add_halves(arrs, gbs, lands, c_arr, name):
    n = len(arrs)

    def body(c_ref, *refs):
        del c_ref
        for q in range(n):
            refs[2 * n + q][...] = (refs[q][...].astype(F32) + refs[n + q][...].astype(F32)).astype(BF16)

    g_specs, l_specs, o_specs, blocks = [], [], [], 0
    for a in arrs:
        bs, imap = _piece_block(a, gbs[a].shape)
        ps = _piece_shape(a, gbs[a].shape)
        g_specs.append(pl.BlockSpec(bs, lambda k, c_ref, imap=imap: imap(k, c_ref[0])))
        nd = len(ps)
        l_specs.append(pl.BlockSpec((None,) + ps, lambda k, c_ref, nd=nd: (k,) + (0,) * nd))
        o_specs.append(pl.BlockSpec((None,) + ps, lambda k, c_ref, nd=nd: (k,) + (0,) * nd))
        blocks += 3 * _nbytes(ps, BF16)
    return list(pl.pallas_call(
        body, name=name,
        grid_spec=pltpu.PrefetchScalarGridSpec(
            num_scalar_prefetch=1, grid=(4,), in_specs=g_specs + l_specs, out_specs=o_specs),
        out_shape=[SDS((4,) + _piece_shape(a, gbs[a].shape), BF16) for a in arrs],
        compiler_params=_params(("parallel",), blocks, blocks),
    )(c_arr, *[gbs[a] for a in arrs], *lands))


def _sum_chips(a, p, land, shard_shape, jc_arr, name):
    ps = land.shape[1:]
    ax = _rows_axis(a)
    rows = ps[ax]
    nsub = 2 if rows % (2 * SUBLANES_BF16) == 0 else 1
    bs = tuple(r // nsub if q == ax else r for q, r in enumerate(ps))
    nd = len(ps)

    def at_rows(v):
        return tuple(v if q == ax else 0 for q in range(nd))

    def body(jc_ref, p_ref, l_ref, o_ref):
        del jc_ref
        acc = p_ref[...].astype(F32) + l_ref[0].astype(F32)
        acc = acc + l_ref[1].astype(F32)
        o_ref[...] = acc + l_ref[2].astype(F32)

    blocks = 4 * _nbytes(bs, BF16) + _nbytes(bs, F32)
    return pl.pallas_call(
        body, name=name,
        grid_spec=pltpu.PrefetchScalarGridSpec(
            num_scalar_prefetch=1, grid=(nsub,),
            in_specs=[pl.BlockSpec((None,) + bs, lambda s, jc: (jc[0],) + at_rows(s)),
                      pl.BlockSpec((3,) + bs, lambda s, jc: (0,) + at_rows(s))],
            out_specs=pl.BlockSpec(bs, lambda s, jc: at_rows(jc[1] * nsub + s))),
        out_shape=SDS(shard_shape, F32),
        compiler_params=_params(("parallel",), blocks, 2 * _nbytes(bs, F32)),
    )(jc_arr, p, land)


def _small_comm(v):
    rows = v.shape[0]

    def copies(cins, couts, csems):
        x, y, c = _coords()
        me = 4 * x + 2 * y + c
        out = [pltpu.make_async_copy(cins[0], couts[0].at[me], csems[0].at[0])]
        for dlt in range(1, 8):
            px = 1 - x if (dlt >> 2) & 1 else x
            py = 1 - y if (dlt >> 1) & 1 else y
            pc = 1 - c if dlt & 1 else c
            out.append(_remote(cins[0], couts[0].at[me], csems[1].at[dlt - 1], csems[2].at[dlt - 1], (px, py, pc)))
        return out

    return _symmetric([v], [SDS((8, rows, LANES), F32)], _dma_sems(1, 7, 7), copies, EVERYONE)


def _sum8(slots, name):
    def body(s_ref, o_ref):
        acc = s_ref[0]
        for i in range(1, 8):
            acc = acc + s_ref[i]
        o_ref[...] = acc

    return pl.pallas_call(
        body, name=name,
        in_specs=[pl.BlockSpec(memory_space=pltpu.VMEM)], out_specs=pl.BlockSpec(memory_space=pltpu.VMEM),
        out_shape=SDS(slots.shape[1:], F32),
    )(slots)


def _adamw(w, g, m, v, name, g_plane=None):
    rows, cols = w.shape
    tr = _tile(rows, max(SUBLANES_F32, (256 * 1024 // cols) // SUBLANES_F32 * SUBLANES_F32), SUBLANES_F32)

    def body(w_ref, g_ref, m_ref, v_ref, go_ref, d_ref, mo_ref, vo_ref):
        gr = g_ref[...]
        mn = ADAM_B1 * m_ref[...] + (1.0 - ADAM_B1) * gr
        vn = ADAM_B2 * v_ref[...] + (1.0 - ADAM_B2) * (gr * gr)
        m_hat = mn / (1.0 - ADAM_B1 ** ADAM_STEP)
        v_hat = vn / (1.0 - ADAM_B2 ** ADAM_STEP)
        d_ref[...] = -ADAM_LR * (m_hat / (jnp.sqrt(v_hat) + ADAM_EPS) + ADAM_WD * w_ref[...])
        go_ref[...] = gr
        mo_ref[...] = mn
        vo_ref[...] = vn

    spec = pl.BlockSpec((tr, cols), lambda i: (i, 0))
    g_spec = spec if g_plane is None else pl.BlockSpec((None, tr, cols), lambda i: (g_plane, i, 0))
    return pl.pallas_call(
        body, name=name, grid=(rows // tr,),
        in_specs=[spec, g_spec, spec, spec], out_specs=[spec, spec, spec, spec],
        out_shape=[SDS((rows, cols), F32)] * 4,
        compiler_params=_params(("parallel",), 8 * _nbytes((tr, cols), F32), 4 * _nbytes((tr, cols), F32)),
    )(w, g, m, v)


def _pack(parts):
    rows = []
    for p in parts:
        r = p.reshape(-1, LANES)
        pad = (-r.shape[0]) % SUBLANES_F32
        if pad:
            r = jnp.pad(r, ((0, pad), (0, 0)))
        rows.append(r)
    return jnp.concatenate(rows, axis=0)


def _unpack(packed, shapes):
    out, at = [], 0
    for s in shapes:
        n = 1
        for q in s:
            n *= q
        r = n // LANES
        out.append(packed[at:at + r].reshape(s))
        at += r + (-r) % SUBLANES_F32
    return out


def kernel(x, norm_mix, w_in, pool_w, pool_scale, w_pool_proj, conv_w, w_conv_out, w_o, norm_ffn, w_up, ffn_conv_w, ffn_conv_b, w_down, norm_final, loss_target, m_norm_mix, m_w_in, m_pool_w, m_pool_scale, m_w_pool_proj, m_conv_w, m_w_conv_out, m_w_o, m_norm_ffn, m_w_up, m_ffn_conv_w, m_ffn_conv_b, m_w_down, m_norm_final, v_norm_mix, v_w_in, v_pool_w, v_pool_scale, v_w_pool_proj, v_conv_w, v_w_conv_out, v_w_o, v_norm_ffn, v_w_up, v_ffn_conv_w, v_ffn_conv_b, v_w_down, v_norm_final):
    nseq, seq, d = x.shape
    t = nseq * seq
    f = w_down.shape[1] * 4
    c = d // N_GROUPS
    xy = lax.axis_index("x") * 2 + lax.axis_index("y")
    c_arr = lax.axis_index("c").astype(jnp.int32).reshape(1)
    jc_arr = jnp.stack([xy, lax.axis_index("c")]).astype(jnp.int32)
    nsh = 4
    zero = jnp.zeros((), jnp.int32)

    locs = [w_in[0].astype(BF16),
            jnp.stack([w_pool_proj[0], w_conv_out[0], w_o[0]]).astype(BF16),
            w_up[0].astype(BF16), w_down[0].astype(BF16), pool_w[0].astype(BF16)]
    full_shapes = [(nsh, d, N_SPLITS * d // nsh), (3, d, d), (nsh, d, 2 * f // nsh), (f, d), (N_GROUPS, c, c)]

    cw_pad = lax.dynamic_update_slice(jnp.zeros((3, d), F32), conv_w[0], (zero, xy * (d // 4)))
    fw_pad = lax.dynamic_update_slice(jnp.zeros((3, 2 * f), F32), ffn_conv_w[0], (zero, xy * (f // 2)))
    small_w = _pack([cw_pad, fw_pad]) * 0.5

    x2d = x.reshape(t, d)
    tgt = loss_target.reshape(t, d)
    ax, ay = lax.axis_index("x"), lax.axis_index("y")
    order = jnp.stack([xy, 2 * (1 - ax) + ay, 2 * ax + 1 - ay, 2 * (1 - ax) + 1 - ay]).astype(jnp.int32)
    (z, h1, w_in_f), (pool_w_f, w3_f, slots_w) = _fwd_in(
        x2d, norm_mix, locs[0], order,
        _merge([_gather_comm([4], locs, full_shapes), _gather_comm([1], locs, full_shapes, part=(0, 1, 2)),
                _small_comm(small_w)]))
    conv_w_f, ffn_cw_f = _unpack(_sum8(slots_w, "sum8_weights"), [(3, d), (3, 2 * f)])
    ffn_cw_p = ffn_cw_f.reshape(3, 2, f).transpose(1, 0, 2)
    ffn_cb_p = ffn_conv_b.reshape(2, 1, f)
    (lhs3,), (w3_f,) = _mixer_mid_fwd(z, pool_w_f, pool_scale, conv_w_f, nseq,
                                      _gather_comm([1], locs, full_shapes, part=(1, 2, 2), into={1: w3_f}))
    (lhs3, ypc, x1, h2), (w_up_f,) = _mixer_out(lhs3, z, x2d, w3_f, norm_ffn, _gather_comm([2], locs, full_shapes))
    (u0,), (w_down_f,) = _ffn_up(h2, w_up_f, f, _gather_comm([3], locs, full_shapes))
    act, ua = _ffn_mid_fwd(u0, ffn_cw_p, ffn_cb_p, nseq)
    dx2, dx2b, loss11, g_norm_final = _ffn_down_loss(act, w_down_f, x1, tgt, norm_final.reshape(1, d))

    gbs, lands, ps, lands2, rs = {}, {}, {}, {}, {}
    tn_up = _tile(2 * f // nsh, 1408, LANES)
    npp = f // tn_up

    def add(arrs, name):
        for a, p in zip(arrs, _add_halves(arrs, gbs, [lands[a] for a in arrs], c_arr, name)):
            ps[a] = p

    def summed(a):
        rs[a] = _sum_chips(a, ps[a], lands2[a], _shard_shape(a, full_shapes[a]), jc_arr, "sum_chips_%d" % a)

    (gbs[3],), _ = _wgrad(act, dx2b, "wgrad_down", tr=tn_up, tn=d)
    (da,), (lands[3],) = _ffn_bwd_da(dx2b, w_down_f, _halves_comm([3], gbs))
    add([3], "add_halves_down")
    (du0, g_ffn_cw_p, g_ffn_cb_p), (lands2[3],) = _ffn_mid_bwd(da, u0, ua, ffn_cw_p, nseq, _chips_comm([3], ps))
    summed(3)
    (gbs[2],), (rs[3],) = _wgrad(h2, du0, "wgrad_up", tr=d, tn=tn_up, b_plane_of=lambda n: (n // npp, n % npp),
                                 out_shards=nsh, comm=_result_comm([3], rs))
    (dx1, rhs3, g_norm_ffn), (lands[2],) = _ffn_bwd_dx1(du0, w_up_f, x1, dx2, norm_ffn, 3, _halves_comm([2], gbs))
    add([2], "add_halves_up")
    (rhs3, dz, dpq), (lands2[2],) = _mixer_bwd(rhs3, z, ypc, w3_f, _chips_comm([2], ps, part=(0, 1, 2)))
    (gbs[1],), (lands2[2],) = _wgrad3(lhs3, rhs3, _chips_comm([2], ps, part=(1, 2, 2), into=lands2))
    summed(2)
    (dz, g_conv_w), (lands[1], rs[2]) = _conv_bwd(dz, dpq, z, conv_w_f, nseq,
                                                  _merge([_halves_comm([1], gbs), _result_comm([2], rs)]))
    add([1], "add_halves_sq3")
    (dz, g_pool_w, g_pool_scale), _ = _pool_bwd_call(dz, dpq, z, pool_w_f, pool_scale, nseq)
    gbs[4] = g_pool_w.astype(BF16)
    (gbs[0],), (lands2[1],) = _wgrad_in(h1, dz, nsh, _chips_comm([1], ps))
    summed(1)
    lands[0], lands[4] = _run_comm(_halves_comm([0, 4], gbs), "exchange_halves_in")
    add([0, 4], "add_halves_in")
    g_ffn_cw = g_ffn_cw_p.transpose(1, 0, 2).reshape(3, 2 * f)
    small_a = _pack([g_pool_scale, g_norm_ffn, g_ffn_cb_p.reshape(1, 2 * f), g_norm_final.reshape(d), g_conv_w,
                     g_ffn_cw, jnp.pad(loss11, ((0, SUBLANES_F32 - 1), (0, LANES - 1)))])
    (grad_x, g_norm_mix), (lands2[0], lands2[4], rs[1], slots_a) = _mixer_bwd_dx(
        dz, w_in_f, x2d, dx1, norm_mix,
        _merge([_chips_comm([0, 4], ps), _result_comm([1], rs), _small_comm(small_a)]))
    summed(0)
    summed(4)
    rs[0], rs[4], slots_b = _run_comm(_merge([_result_comm([0, 4], rs), _small_comm(_pack([g_norm_mix]))]),
                                      "exchange_result_in")
    shapes_a = [(1, d), (1, d), (1, 2 * f), (d,), (3, d), (3, 2 * f), (SUBLANES_F32, LANES)]
    gs_pool_scale, gs_norm_ffn, gs_ffn_cb, gs_norm_final, gs_conv_w, gs_ffn_cw, loss_blk = _unpack(
        _sum8(slots_a, "sum8_grads"), shapes_a)
    (gs_norm_mix,) = _unpack(_sum8(slots_b, "sum8_norm_mix"), [(1, d)])
    gs_conv_w = lax.dynamic_slice(gs_conv_w, (zero, xy * (d // 4)), (3, d // 4))
    gs_ffn_cw = lax.dynamic_slice(gs_ffn_cw, (zero, xy * (f // 2)), (3, f // 2))

    def upd(w, g, m, v, name, g_plane=None):
        shape = w.shape
        rows = 1
        for q in shape[:-1]:
            rows *= q
        g2 = g if g_plane is not None else g.reshape(rows, shape[-1])
        outs = _adamw(w.reshape(rows, shape[-1]), g2, m.reshape(rows, shape[-1]), v.reshape(rows, shape[-1]),
                      name, g_plane)
        return [o.reshape(shape) for o in outs]

    res = {
        "w_in": upd(w_in, rs[0], m_w_in, v_w_in, "adamw_w_in"),
        "pool_w": upd(pool_w, rs[4], m_pool_w, v_pool_w, "adamw_pool_w"),
        "w_pool_proj": upd(w_pool_proj, rs[1], m_w_pool_proj, v_w_pool_proj, "adamw_w_pool_proj", 0),
        "w_conv_out": upd(w_conv_out, rs[1], m_w_conv_out, v_w_conv_out, "adamw_w_conv_out", 1),
        "w_o": upd(w_o, rs[1], m_w_o, v_w_o, "adamw_w_o", 2),
        "w_up": upd(w_up, rs[2], m_w_up, v_w_up, "adamw_w_up"),
        "w_down": upd(w_down, rs[3], m_w_down, v_w_down, "adamw_w_down"),
    }

    small_names = ["norm_mix", "pool_scale", "norm_ffn", "ffn_conv_b", "norm_final", "conv_w", "ffn_conv_w"]
    small_ws = [norm_mix, pool_scale, norm_ffn, ffn_conv_b, norm_final, conv_w, ffn_conv_w]
    small_ms = [m_norm_mix, m_pool_scale, m_norm_ffn, m_ffn_conv_b, m_norm_final, m_conv_w, m_ffn_conv_w]
    small_vs = [v_norm_mix, v_pool_scale, v_norm_ffn, v_ffn_conv_b, v_norm_final, v_conv_w, v_ffn_conv_w]
    small_gs = [gs_norm_mix, gs_pool_scale, gs_norm_ffn, gs_ffn_cb, gs_norm_final, gs_conv_w, gs_ffn_cw]
    _, sd, sm, sv = _adamw(_pack(small_ws), _pack(small_gs), _pack(small_ms), _pack(small_vs), "adamw_small")
    shapes = [w.shape for w in small_ws]
    sd, sm, sv = _unpack(sd, shapes), _unpack(sm, shapes), _unpack(sv, shapes)
    for i, nm in enumerate(small_names):
        res[nm] = [small_gs[i].reshape(shapes[i]), sd[i], sm[i], sv[i]]

    order = ["norm_mix", "w_in", "pool_w", "pool_scale", "w_pool_proj", "conv_w", "w_conv_out", "w_o", "norm_ffn",
             "w_up", "ffn_conv_w", "ffn_conv_b", "w_down", "norm_final"]
    return (loss_blk[0, 0], grad_x.reshape(x.shape), *[res[n][0] for n in order], *[res[n][1] for n in order],
            *[res[n][2] for n in order], *[res[n][3] for n in order])
```

```python
import math

import jax
import jax.numpy as jnp
from jax import lax
from jax.experimental import pallas as pl
from jax.experimental.pallas import tpu as pltpu

F32 = jnp.float32
BF16 = jnp.bfloat16
SDS = jax.ShapeDtypeStruct
MESH = pl.DeviceIdType.MESH

RMS_EPS = 1e-6
POOL_WINDOWS = (2, 4, 8, 16)
N_GROUPS = len(POOL_WINDOWS)
N_SPLITS = 6

ADAM_LR = 0.001
ADAM_B1 = 0.9
ADAM_B2 = 0.999
ADAM_EPS = 1e-08
ADAM_WD = 0.01
ADAM_STEP = 10

LANES = 128
SUBLANES_F32 = 8
SUBLANES_BF16 = 16
VMEM_BYTES = 64 * 1024 * 1024
VMEM_CAP = VMEM_BYTES - 8 * 1024 * 1024
VMEM_FLOOR = 16 * 1024 * 1024

ANY = pl.BlockSpec(memory_space=pl.ANY)


def _tile(dim, pref, align):
    if dim <= pref:
        return dim
    t = (pref // align) * align
    while t >= align:
        if dim % t == 0:
            return t
        t -= align
    return dim


def _nbytes(shape, dtype):
    n = 1
    for s in shape:
        n *= s
    return n * jnp.dtype(dtype).itemsize


def _params(sem, block_bytes, temp_bytes=0, collective_id=None):
    need = 2 * block_bytes + temp_bytes + 4 * 1024 * 1024
    return pltpu.CompilerParams(dimension_semantics=sem, collective_id=collective_id,
                                vmem_limit_bytes=int(min(max(need, VMEM_FLOOR), VMEM_CAP)))


SIBLING = (0, 0, 1)
CHIPS = ((1, 0, 0), (0, 1, 0), (1, 1, 0))
EVERYONE = tuple((a, b, c) for a in range(2) for b in range(2) for c in range(2) if a + b + c)
PEER_SETS = (frozenset([SIBLING]), frozenset(CHIPS), frozenset(CHIPS + (SIBLING,)), frozenset(EVERYONE))


def _collective_id(peers):
    return PEER_SETS.index(frozenset(peers))


def _handshake(peers):
    x, y, c = lax.axis_index("x"), lax.axis_index("y"), lax.axis_index("c")
    bar = pltpu.get_barrier_semaphore()
    for fx, fy, fc in sorted(peers):
        dev = (1 - x if fx else x, 1 - y if fy else y, 1 - c if fc else c)
        pl.semaphore_signal(bar, inc=1, device_id=dev, device_id_type=MESH)
    pl.semaphore_wait(bar, len(peers))


class _Comm:
    def __init__(self, ins, out_shapes, sems, start, finish, peers, aliases=None):
        self.ins = list(ins)
        self.out_shapes = list(out_shapes)
        self.sems = list(sems)
        self.start = start
        self.finish = finish
        self.peers = frozenset(peers)
        self.aliases = dict(aliases or {})


def _pcall(body, *, name, grid, in_specs, out_specs, out_shape, sem, blocks, temps=0, scratch_shapes=(),
           input_output_aliases=None, comm=None):
    in_specs = list(in_specs)
    out_specs = list(out_specs)
    out_shape = list(out_shape)
    scratch_shapes = list(scratch_shapes)
    aliases = dict(input_output_aliases or {})
    n_in, n_out, n_scr = len(in_specs), len(out_shape), len(scratch_shapes)
    if comm is None:
        call = pl.pallas_call(
            body, name=name, grid=grid, in_specs=in_specs, out_specs=out_specs, out_shape=out_shape,
            scratch_shapes=scratch_shapes, input_output_aliases=aliases,
            compiler_params=_params(sem, blocks, temps))
        return lambda *args: (list(call(*args)), [])

    nci, nco = len(comm.ins), len(comm.out_shapes)

    def hosted(*refs):
        ins = refs[:n_in]
        cins = refs[n_in:n_in + nci]
        outs = refs[n_in + nci:n_in + nci + n_out]
        couts = refs[n_in + nci + n_out:n_in + nci + n_out + nco]
        scr = refs[n_in + nci + n_out + nco:n_in + nci + n_out + nco + n_scr]
        csems = refs[n_in + nci + n_out + nco + n_scr:]
        first = None
        last = None
        for q, g in enumerate(grid):
            pid = pl.program_id(q)
            first = (pid == 0) if first is None else first & (pid == 0)
            last = (pid == g - 1) if last is None else last & (pid == g - 1)

        @pl.when(first)
        def _():
            _handshake(comm.peers)
            comm.start(cins, couts, csems)

        body(*ins, *outs, *scr)

        @pl.when(last)
        def _():
            comm.finish(cins, couts, csems)

    for i, o in comm.aliases.items():
        aliases[n_in + i] = n_out + o
    call = pl.pallas_call(
        hosted, name=name, grid=grid, in_specs=in_specs + [ANY] * nci, out_specs=out_specs + [ANY] * nco,
        out_shape=out_shape + comm.out_shapes, scratch_shapes=scratch_shapes + comm.sems,
        input_output_aliases=aliases,
        compiler_params=_params(("arbitrary",) * len(grid), blocks, temps, _collective_id(comm.peers)))

    def run(*args):
        res = call(*args, *comm.ins)
        return list(res[:n_out]), list(res[n_out:])

    return run


def _run_comm(comm, name):
    def body(*refs):
        nci, nco = len(comm.ins), len(comm.out_shapes)
        cins, couts, csems = refs[:nci], refs[nci:nci + nco], refs[nci + nco:]
        _handshake(comm.peers)
        comm.start(cins, couts, csems)
        comm.finish(cins, couts, csems)

    return list(pl.pallas_call(
        body, name=name, in_specs=[ANY] * len(comm.ins), out_specs=[ANY] * len(comm.out_shapes),
        out_shape=comm.out_shapes, scratch_shapes=comm.sems, input_output_aliases=comm.aliases,
        compiler_params=pltpu.CompilerParams(collective_id=_collective_id(comm.peers)),
    )(*comm.ins))


def _dot(a, b):
    return jnp.dot(a, b, preferred_element_type=F32)


def _dot_tb(a, b):
    return lax.dot_general(a, b, (((1,), (1,)), ((), ())), preferred_element_type=F32)


def _dot_ta(a, b):
    return lax.dot_general(a, b, (((0,), (0,)), ((), ())), preferred_element_type=F32)


def _rms_fwd(x):
    inv = lax.rsqrt(jnp.mean(x * x, axis=-1, keepdims=True) + RMS_EPS)
    return x * inv, inv


def _rms_bwd(dy, xhat, inv, g):
    gd = dy * g
    return inv * (gd - xhat * jnp.mean(gd * xhat, axis=-1, keepdims=True))


def _sigmoid(x):
    return 1.0 / (1.0 + jnp.exp(-x))


def _shift_down(x, k, row):
    return jnp.where(row >= k, pltpu.roll(x, k, 0), 0.0)


def _shift_up(x, k, row):
    s = x.shape[0]
    return jnp.where(row < s - k, pltpu.roll(x, s - k, 0), 0.0)


def _pool_fwd(u, win, row):
    s = u
    k = 1
    while k < win:
        s = s + _shift_down(s, k, row)
        k *= 2
    cnt = jnp.minimum(row + 1, win).astype(F32)
    return s / cnt - u


def _pool_bwd(dp, win, row):
    cnt = jnp.minimum(row + 1, win).astype(F32)
    s = dp / cnt
    k = 1
    while k < win:
        s = s + _shift_up(s, k, row)
        k *= 2
    return s - dp


def _acc_over(k, nk, part, acc, o_ref):
    @pl.when(k == 0)
    def _():
        acc[...] = part

    @pl.when(k > 0)
    def _():
        acc[...] += part

    @pl.when(k == nk - 1)
    def _():
        o_ref[...] = acc[...].astype(o_ref.dtype)


def _fwd_in(x, g, w_loc, order, comm):
    t, d = x.shape
    ws = w_loc.shape[1]
    nsh = order.shape[0]
    assert nsh == 4, "the shard walk below is written for the 2 x 2 chips of the mesh"
    tm = _tile(t, 1024, SUBLANES_BF16)
    ni = t // tm
    nci, nco = len(comm.ins), len(comm.out_shapes)
    all_peers = comm.peers | frozenset(CHIPS + (SIBLING,))

    def body(order_ref, x_ref, g_ref, loc_ref, *rest):
        del order_ref
        cins = rest[:nci]
        z_ref, h_ref, full_ref = rest[nci:nci + 3]
        couts = rest[nci + 3:nci + 3 + nco]
        (hs, wbuf, wsem, own_s, own_r, snd_s, snd_r, fwd_s, fwd_r, rly_s, rly_r) = rest[nci + 3 + nco:nci + 14 + nco]
        csems = rest[nci + 14 + nco:]
        j = pl.program_id(0)
        i = pl.program_id(1)
        x_, y_, c_ = _coords()
        own = 2 * x_ + y_
        sib = (x_, y_, 1 - c_)
        peers = _peer_chips(x_, y_)

        def sends():
            return [_remote(_half(0, loc_ref, c_), _piece(0, full_ref, own, c_), snd_s.at[p], snd_r.at[p], (px, py, c_))
                    for p, (px, py) in enumerate(peers[:2])]

        def relays():
            out = []
            for q, (src_p, dst_p) in enumerate(((0, 1), (1, 0))):
                sx, sy = peers[src_p]
                part = _rows_part(0, _piece(0, full_ref, 2 * sx + sy, c_), (q, q + 1, 2))
                out.append(_remote(part, part, rly_s.at[q], rly_r.at[q], (*peers[dst_p], c_)))
            return out

        def owns():
            return [_remote(_half(0, loc_ref, h), _piece(0, full_ref, own, h), own_s.at[h], own_r.at[h], sib)
                    for h in range(2)]

        def forward(p, half):
            px, py = peers[p]
            landed = _piece(0, full_ref, 2 * px + py, half)
            return _remote(landed, landed, fwd_s.at[p], fwd_r.at[p], sib)

        def load(src, slot):
            return pltpu.make_async_copy(src, wbuf.at[slot], wsem.at[slot])

        @pl.when((j == 0) & (i == 0))
        def _():
            _handshake(all_peers)
            for cp in sends() + owns():
                cp.start()
            load(loc_ref, 0).start()
            comm.start(cins, couts, csems)

        @pl.when(j == 0)
        def _():
            xh, _ = _rms_fwd(x_ref[...])
            h = (xh * g_ref[...]).astype(BF16)
            hs[pl.ds(pl.multiple_of(i * tm, tm), tm), :] = h
            h_ref[...] = h

        slot = j % 2

        @pl.when(i == 0)
        def _():
            load(loc_ref, slot).wait()

        z_ref[...] = _dot(hs[pl.ds(pl.multiple_of(i * tm, tm), tm), :], wbuf[slot]).astype(BF16)

        for p in range(nsh - 1):
            @pl.when((i == ni - 1) & (j == p))
            def _(p=p):
                px, py = peers[p]
                if p == 0:
                    for cp in sends():
                        cp.wait_recv()
                    for cp in relays():
                        cp.start()
                if p == 2:
                    for cp in relays():
                        cp.wait_recv()
                forward(p, c_).start()
                forward(p, 1 - c_).wait_recv()
                load(full_ref.at[2 * px + py], 1 - slot).start()

        @pl.when((j == nsh - 1) & (i == ni - 1))
        def _():
            for cp in sends() + relays() + [forward(p, c_) for p in range(nsh - 1)]:
                cp.wait_send()
            for cp in owns():
                cp.wait()
            comm.finish(cins, couts, csems)

    last = ni - 1
    blocks = _nbytes((tm, d), F32) + _nbytes((tm, ws), BF16) + _nbytes((tm, d), BF16)
    scratch = _nbytes((t, d), BF16) + 2 * _nbytes((d, ws), BF16)
    res = pl.pallas_call(
        body, name="fwd_in",
        grid_spec=pltpu.PrefetchScalarGridSpec(
            num_scalar_prefetch=1, grid=(nsh, ni),
            in_specs=[pl.BlockSpec((tm, d), lambda j, i, o: (jnp.where(j == 0, i, last), 0)),
                      pl.BlockSpec((1, d), lambda j, i, o: (0, 0)), ANY] + [ANY] * nci,
            out_specs=[pl.BlockSpec((tm, ws), lambda j, i, o: (i, o[j])),
                       pl.BlockSpec((tm, d), lambda j, i, o: (jnp.where(j == 0, i, last), 0)), ANY] + [ANY] * nco,
            scratch_shapes=[pltpu.VMEM((t, d), BF16), pltpu.VMEM((2, d, ws), BF16)]
            + _dma_sems(2, 2, 2, 2, 2, nsh - 1, nsh - 1, 2, 2) + comm.sems),
        out_shape=[SDS((t, nsh * ws), BF16), SDS((t, d), BF16), SDS((nsh, d, ws), BF16)] + comm.out_shapes,
        input_output_aliases={4 + i: 3 + o for i, o in comm.aliases.items()},
        compiler_params=_params(("arbitrary", "arbitrary"), blocks, scratch + 3 * _nbytes((tm, d), F32),
                                _collective_id(all_peers)),
    )(order, x, g, w_loc, *comm.ins)
    return list(res[:3]), list(res[3:])


def _mixer_mid_fwd(z, pool_w, pool_scale, conv_w, nseq, comm=None):
    t = z.shape[0]
    d = pool_scale.shape[1]
    s = t // nseq
    c = d // N_GROUPS

    def body(zp, zb, zc, zv, pw, ps, cw, o):
        j = pl.program_id(1)
        row = lax.broadcasted_iota(jnp.int32, (s, c), 0)
        for gi, win in enumerate(POOL_WINDOWS):
            @pl.when(j == gi)
            def _(win=win):
                pooled = _pool_fwd(zp[...].astype(F32), win, row)
                o[0] = (_dot(pooled.astype(BF16), pw[...]) * ps[...]).astype(BF16)

        cv = zc[...].astype(F32) * zv[...].astype(F32)
        cc = (cw[pl.ds(2, 1), :] * cv + cw[pl.ds(1, 1), :] * _shift_down(cv, 1, row)
              + cw[pl.ds(0, 1), :] * _shift_down(cv, 2, row))
        o[1] = (zb[...].astype(F32) * cc).astype(BF16)

    blocks = 4 * _nbytes((s, c), BF16) + _nbytes((c, c), BF16) + _nbytes((2, s, c), BF16)
    return _pcall(
        body, name="mixer_mid_fwd", grid=(nseq, N_GROUPS),
        in_specs=[pl.BlockSpec((s, c), lambda b, j: (b, j)),
                  pl.BlockSpec((s, c), lambda b, j: (b, N_GROUPS + j)),
                  pl.BlockSpec((s, c), lambda b, j: (b, 2 * N_GROUPS + j)),
                  pl.BlockSpec((s, c), lambda b, j: (b, 3 * N_GROUPS + j)),
                  pl.BlockSpec((None, c, c), lambda b, j: (j, 0, 0)),
                  pl.BlockSpec((1, c), lambda b, j: (0, j)),
                  pl.BlockSpec((3, c), lambda b, j: (0, j))],
        out_specs=[pl.BlockSpec((2, s, c), lambda b, j: (0, b, j))],
        out_shape=[SDS((3, t, d), BF16)],
        sem=("parallel", "parallel"), blocks=blocks, temps=8 * _nbytes((s, c), F32), comm=comm,
    )(z, z, z, z, pool_w, pool_scale, conv_w)


def _mixer_out(lhs3, z, x, w3, g_ffn, comm=None):
    t, d = x.shape
    tm = _tile(t, 256, SUBLANES_BF16)

    def body(pq, zgp, zgc, x_ref, w_ref, g_ref, mrg, ypc, x1o, h2o):
        yp = _dot(pq[0], w_ref[0])
        yc = _dot(pq[1], w_ref[1])
        m = _sigmoid(zgp[...].astype(F32)) * yp + _sigmoid(zgc[...].astype(F32)) * yc
        mb = m.astype(BF16)
        x1 = x_ref[...] + _dot(mb, w_ref[2])
        ypc[0] = yp.astype(BF16)
        ypc[1] = yc.astype(BF16)
        mrg[...] = mb
        x1o[...] = x1
        xh, _ = _rms_fwd(x1)
        h2o[...] = (xh * g_ref[...]).astype(BF16)

    blocks = (_nbytes((2, tm, d), BF16) * 2 + _nbytes((tm, d), BF16) * 4 + _nbytes((tm, d), F32) * 2
              + _nbytes((3, d, d), BF16))
    return _pcall(
        body, name="mixer_out", grid=(t // tm,),
        in_specs=[pl.BlockSpec((2, tm, d), lambda i: (0, i, 0)),
                  pl.BlockSpec((tm, d), lambda i: (i, 4)),
                  pl.BlockSpec((tm, d), lambda i: (i, 5)),
                  pl.BlockSpec((tm, d), lambda i: (i, 0)),
                  pl.BlockSpec((3, d, d), lambda i: (0, 0, 0)),
                  pl.BlockSpec((1, d), lambda i: (0, 0))],
        out_specs=[pl.BlockSpec((None, tm, d), lambda i: (2, i, 0)),
                   pl.BlockSpec((2, tm, d), lambda i: (0, i, 0)),
                   pl.BlockSpec((tm, d), lambda i: (i, 0)),
                   pl.BlockSpec((tm, d), lambda i: (i, 0))],
        out_shape=[SDS(lhs3.shape, BF16), SDS((2, t, d), BF16), SDS((t, d), F32), SDS((t, d), BF16)],
        input_output_aliases={0: 0},
        sem=("parallel",), blocks=blocks, temps=8 * _nbytes((tm, d), F32), comm=comm,
    )(lhs3, z, z, x, w3, g_ffn)


def _ffn_up(h2, w_up, f, comm=None):
    t, d = h2.shape
    _, _, ws = w_up.shape
    tm = _tile(t, 1024, SUBLANES_BF16)
    tn = _tile(ws, 1408, LANES)
    nps = ws // tn
    npp = f // tn

    def body(h_ref, w_ref, o_ref):
        o_ref[...] = _dot(h_ref[...], w_ref[...]).astype(BF16)

    blocks = _nbytes((tm, d), BF16) + _nbytes((d, tn), BF16) + _nbytes((tm, tn), BF16)
    return _pcall(
        body, name="ffn_up", grid=(t // tm, 2 * npp),
        in_specs=[pl.BlockSpec((tm, d), lambda i, j: (i, 0)),
                  pl.BlockSpec((None, d, tn), lambda i, j: (j // nps, 0, j % nps))],
        out_specs=[pl.BlockSpec((None, tm, tn), lambda i, j: (j // npp, i, j % npp))],
        out_shape=[SDS((2, t, f), BF16)],
        sem=("parallel", "parallel"), blocks=blocks, temps=_nbytes((tm, tn), F32), comm=comm,
    )(h2, w_up)


def _conv3_rows(u, u1, u2, w_ref, p):
    return w_ref[p, pl.ds(2, 1), :] * u + w_ref[p, pl.ds(1, 1), :] * u1 + w_ref[p, pl.ds(0, 1), :] * u2


WGRAD_TOKENS = 2048
CHUNK = 64
HALO = SUBLANES_F32


def _up1_up2(u, nxt):
    rows = u.shape[0]
    ext = jnp.concatenate([u, nxt], axis=0)
    n = rows + HALO
    return pltpu.roll(ext, n - 1, 0)[:rows], pltpu.roll(ext, n - 2, 0)[:rows]


def _fold8(x):
    return jnp.sum(x.reshape(x.shape[0] // SUBLANES_F32, SUBLANES_F32, x.shape[1]), axis=0)


def _ffn_mid_fwd(u0, cw, cb, nseq):
    _, t, f = u0.shape
    s = t // nseq
    c = _tile(f, 256, LANES)

    def body(u_ref, w_ref, b_ref, a_ref, uo_ref):
        row = lax.broadcasted_iota(jnp.int32, (s, c), 0)
        act = []
        for p in range(2):
            u = u_ref[p].astype(F32)
            act.append(_conv3_rows(u, _shift_down(u, 1, row), _shift_down(u, 2, row), w_ref, p) + b_ref[p])
            uo_ref[p] = act[p].astype(BF16)
        ug, uv = act
        a_ref[...] = (ug * _sigmoid(ug) * uv).astype(BF16)

    blocks = 2 * _nbytes((2, s, c), BF16) + _nbytes((s, c), BF16)
    outs, _ = _pcall(
        body, name="ffn_mid_fwd", grid=(f // c, nseq),
        in_specs=[pl.BlockSpec((2, s, c), lambda j, b: (0, b, j)),
                  pl.BlockSpec((2, 3, c), lambda j, b: (0, 0, j)),
                  pl.BlockSpec((2, 1, c), lambda j, b: (0, 0, j))],
        out_specs=[pl.BlockSpec((s, c), lambda j, b: (b, j)),
                   pl.BlockSpec((2, s, c), lambda j, b: (0, b, j))],
        out_shape=[SDS((t, f), BF16), SDS((2, t, f), BF16)],
        sem=("parallel", "parallel"), blocks=blocks, temps=8 * _nbytes((s, c), F32),
    )(u0, cw, cb)
    return outs


def _ffn_down_loss(a, w_down, x1, tgt, g_fin):
    t, f = a.shape
    d = x1.shape[1]
    tm = _tile(t, 256, SUBLANES_BF16)
    nsteps = t // tm

    def body(a_ref, w_ref, x1_ref, t_ref, g_ref, dx_ref, dxb_ref, loss_ref, gg_ref, lacc):
        i = pl.program_id(0)

        @pl.when(i == 0)
        def _():
            lacc[...] = jnp.zeros_like(lacc)
            gg_ref[...] = jnp.zeros_like(gg_ref)

        x2 = x1_ref[...] + _dot(a_ref[...], w_ref[...])
        xh, inv = _rms_fwd(x2)
        g = g_ref[...]
        e = xh * g - t_ref[...]
        lacc[...] += jnp.sum(e * e, axis=0, keepdims=True)
        dy = e * (1.0 / d)
        gg_ref[...] += jnp.sum(dy * xh, axis=0, keepdims=True)
        dx2 = _rms_bwd(dy, xh, inv, g)
        dx_ref[...] = dx2
        dxb_ref[...] = dx2.astype(BF16)

        @pl.when(i == nsteps - 1)
        def _():
            loss_ref[...] = jnp.sum(lacc[...], axis=1, keepdims=True) * (0.5 / d)

    blocks = (_nbytes((tm, f), BF16) + _nbytes((f, d), BF16) + 3 * _nbytes((tm, d), F32) + _nbytes((tm, d), BF16))
    outs, _ = _pcall(
        body, name="ffn_down_loss", grid=(nsteps,),
        in_specs=[pl.BlockSpec((tm, f), lambda i: (i, 0)), pl.BlockSpec((f, d), lambda i: (0, 0)),
                  pl.BlockSpec((tm, d), lambda i: (i, 0)), pl.BlockSpec((tm, d), lambda i: (i, 0)),
                  pl.BlockSpec((1, d), lambda i: (0, 0))],
        out_specs=[pl.BlockSpec((tm, d), lambda i: (i, 0)), pl.BlockSpec((tm, d), lambda i: (i, 0)),
                   pl.BlockSpec((1, 1), lambda i: (0, 0)), pl.BlockSpec((1, d), lambda i: (0, 0))],
        out_shape=[SDS((t, d), F32), SDS((t, d), BF16), SDS((1, 1), F32), SDS((1, d), F32)],
        scratch_shapes=[pltpu.VMEM((1, d), F32)],
        sem=("arbitrary",), blocks=blocks, temps=8 * _nbytes((tm, d), F32),
    )(a, w_down, x1, tgt, g_fin)
    return outs


def _ffn_bwd_da(dxb, w_down, comm=None):
    t, d = dxb.shape
    f = w_down.shape[0]
    tm = _tile(t, 1024, SUBLANES_BF16)
    tn = _tile(f, 1408, LANES)

    def body(x_ref, w_ref, o_ref):
        o_ref[...] = _dot_tb(x_ref[...], w_ref[...]).astype(BF16)

    blocks = _nbytes((tm, d), BF16) + _nbytes((tn, d), BF16) + _nbytes((tm, tn), BF16)
    return _pcall(
        body, name="ffn_bwd_da", grid=(t // tm, f // tn),
        in_specs=[pl.BlockSpec((tm, d), lambda i, j: (i, 0)), pl.BlockSpec((tn, d), lambda i, j: (j, 0))],
        out_specs=[pl.BlockSpec((tm, tn), lambda i, j: (i, j))],
        out_shape=[SDS((t, f), BF16)],
        sem=("parallel", "parallel"), blocks=blocks, temps=_nbytes((tm, tn), F32), comm=comm,
    )(dxb, w_down)


def _ffn_mid_bwd(da, u0, ua, cw, nseq, comm=None):
    _, t, f = u0.shape
    s = t // nseq
    c = _tile(f, 128, LANES)
    r = _tile(s, CHUNK, SUBLANES_BF16)
    n = s // r

    def body(da_ref, u_ref, ua_ref, w_ref, du_ref, gw_ref, gb_ref):
        @pl.when(pl.program_id(1) == 0)
        def _():
            gw_ref[...] = jnp.zeros_like(gw_ref)
            gb_ref[...] = jnp.zeros_like(gb_ref)

        def step(i, carry):
            nxt, sums = carry
            rows = pl.ds(pl.multiple_of((n - 1 - i) * r, r), r)
            ug = ua_ref[0, rows, :].astype(F32)
            uv = ua_ref[1, rows, :].astype(F32)
            sg = _sigmoid(ug)
            dacc = da_ref[rows, :].astype(F32)
            dus = (dacc * uv * sg * (1.0 + ug * (1.0 - sg)), dacc * (ug * sg))
            first, new_sums = [], []
            for p in range(2):
                du = dus[p]
                d1, d2 = _up1_up2(du, nxt[p])
                du_ref[p, rows, :] = _conv3_rows(du, d1, d2, w_ref, p).astype(BF16)
                u = u_ref[p, rows, :].astype(F32)
                sb, s0, s1, s2 = sums[p]
                new_sums.append((sb + _fold8(du), s0 + _fold8(d2 * u), s1 + _fold8(d1 * u), s2 + _fold8(du * u)))
                first.append(du[:HALO])
            return tuple(first), tuple(new_sums)

        zero = jnp.zeros((HALO, c), F32)
        _, sums = lax.fori_loop(0, n, step, ((zero, zero), ((zero,) * 4,) * 2))
        for p in range(2):
            sb, s0, s1, s2 = sums[p]
            gb_ref[p] += jnp.sum(sb, axis=0, keepdims=True)
            gw_ref[p, pl.ds(0, 1), :] += jnp.sum(s0, axis=0, keepdims=True)
            gw_ref[p, pl.ds(1, 1), :] += jnp.sum(s1, axis=0, keepdims=True)
            gw_ref[p, pl.ds(2, 1), :] += jnp.sum(s2, axis=0, keepdims=True)

    blocks = _nbytes((s, c), BF16) + 3 * _nbytes((2, s, c), BF16)
    return _pcall(
        body, name="ffn_mid_bwd", grid=(f // c, nseq),
        in_specs=[pl.BlockSpec((s, c), lambda j, b: (b, j)),
                  pl.BlockSpec((2, s, c), lambda j, b: (0, b, j)),
                  pl.BlockSpec((2, s, c), lambda j, b: (0, b, j)),
                  pl.BlockSpec((2, 3, c), lambda j, b: (0, 0, j))],
        out_specs=[pl.BlockSpec((2, s, c), lambda j, b: (0, b, j)),
                   pl.BlockSpec((2, 3, c), lambda j, b: (0, 0, j)),
                   pl.BlockSpec((2, 1, c), lambda j, b: (0, 0, j))],
        out_shape=[SDS((2, t, f), BF16), SDS((2, 3, f), F32), SDS((2, 1, f), F32)],
        sem=("parallel", "arbitrary"), blocks=blocks, temps=4 * 1024 * 1024, comm=comm,
    )(da, u0, ua, cw)


def _wgrad(a, b, name, *, tr, tn, b_plane_of=None, out_shards=None, comm=None):
    t, m = a.shape
    n_total = b.shape[-1] * (b.shape[0] if b.ndim == 3 else 1)
    tk = _tile(t, WGRAD_TOKENS, SUBLANES_BF16)
    nk = t // tk

    def body(a_ref, b_ref, o_ref, acc):
        _acc_over(pl.program_id(2), nk, _dot_ta(a_ref[...], b_ref[...]), acc, o_ref)

    if b.ndim == 3:
        b_spec = pl.BlockSpec((None, tk, tn), lambda r, n, k: (b_plane_of(n)[0], k, b_plane_of(n)[1]))
    else:
        b_spec = pl.BlockSpec((tk, tn), lambda r, n, k: (k, n))
    if out_shards is None:
        o_spec = pl.BlockSpec((tr, tn), lambda r, n, k: (r, n))
        o_shape = SDS((m, n_total), BF16)
    else:
        nps = n_total // out_shards // tn
        o_spec = pl.BlockSpec((None, tr, tn), lambda r, n, k: (n // nps, r, n % nps))
        o_shape = SDS((out_shards, m, n_total // out_shards), BF16)
    blocks = _nbytes((tk, tr), BF16) + _nbytes((tk, tn), BF16) + _nbytes((tr, tn), BF16)
    return _pcall(
        body, name=name, grid=(m // tr, n_total // tn, nk),
        in_specs=[pl.BlockSpec((tk, tr), lambda r, n, k: (k, r)), b_spec],
        out_specs=[o_spec], out_shape=[o_shape],
        scratch_shapes=[pltpu.VMEM((tr, tn), F32)],
        sem=("parallel", "parallel", "arbitrary"), blocks=blocks, temps=2 * _nbytes((tr, tn), F32), comm=comm,
    )(a, b)


def _wgrad3(lhs3, rhs3, comm=None):
    nw, t, d = lhs3.shape
    tk = _tile(t, WGRAD_TOKENS, SUBLANES_BF16)
    nk = t // tk

    def body(a_ref, b_ref, o_ref, acc):
        _acc_over(pl.program_id(1), nk, _dot_ta(a_ref[...], b_ref[...]), acc, o_ref)

    blocks = 2 * _nbytes((tk, d), BF16) + _nbytes((d, d), BF16)
    return _pcall(
        body, name="wgrad_sq3", grid=(nw, nk),
        in_specs=[pl.BlockSpec((None, tk, d), lambda w, k: (w, k, 0)),
                  pl.BlockSpec((None, tk, d), lambda w, k: (w, k, 0))],
        out_specs=[pl.BlockSpec((None, d, d), lambda w, k: (w, 0, 0))],
        out_shape=[SDS((nw, d, d), BF16)],
        scratch_shapes=[pltpu.VMEM((d, d), F32)],
        sem=("parallel", "arbitrary"), blocks=blocks, temps=2 * _nbytes((d, d), F32), comm=comm,
    )(lhs3, rhs3)


def _ffn_bwd_dx1(du0, w_up, x1, dx2, g_ffn, n_planes_out, comm=None):
    _, t, f = du0.shape
    d = x1.shape[1]
    nsh, _, ws = w_up.shape
    tm = _tile(t, 256, SUBLANES_BF16)
    spp = f // ws

    def body(du_ref, w_ref, x1_ref, dx2_ref, g_ref, dx1_ref, dxb_ref, gg_ref):
        @pl.when(pl.program_id(0) == 0)
        def _():
            gg_ref[...] = jnp.zeros_like(gg_ref)

        dh = None
        for k in range(nsh):
            part = _dot_tb(du_ref[k // spp, :, (k % spp) * ws:(k % spp + 1) * ws], w_ref[k])
            dh = part if dh is None else dh + part
        xh, inv = _rms_fwd(x1_ref[...])
        gg_ref[...] += jnp.sum(dh * xh, axis=0, keepdims=True)
        dx1 = dx2_ref[...] + _rms_bwd(dh, xh, inv, g_ref[...])
        dx1_ref[...] = dx1
        dxb_ref[...] = dx1.astype(BF16)

    blocks = _nbytes((2, tm, f), BF16) + 3 * _nbytes((tm, d), F32) + _nbytes((tm, d), BF16)
    return _pcall(
        body, name="ffn_bwd_dx1", grid=(t // tm,),
        in_specs=[pl.BlockSpec((2, tm, f), lambda i: (0, i, 0)),
                  pl.BlockSpec((nsh, d, ws), lambda i: (0, 0, 0), pipeline_mode=pl.Buffered(1)),
                  pl.BlockSpec((tm, d), lambda i: (i, 0)),
                  pl.BlockSpec((tm, d), lambda i: (i, 0)),
                  pl.BlockSpec((1, d), lambda i: (0, 0))],
        out_specs=[pl.BlockSpec((tm, d), lambda i: (i, 0)),
                   pl.BlockSpec((None, tm, d), lambda i: (n_planes_out - 1, i, 0)),
                   pl.BlockSpec((1, d), lambda i: (0, 0))],
        out_shape=[SDS((t, d), F32), SDS((n_planes_out, t, d), BF16), SDS((1, d), F32)],
        sem=("arbitrary",), blocks=blocks, temps=_nbytes(w_up.shape, BF16) + 8 * _nbytes((tm, d), F32), comm=comm,
    )(du0, w_up, x1, dx2, g_ffn)


def _mixer_bwd(rhs3, z, ypc, w3, comm=None):
    _, t, d = rhs3.shape
    tm = _tile(t, 256, SUBLANES_BF16)

    def body(dx_ref, zgp, zgc, ypc_ref, w_ref, dyo, dzo, dpq):
        dm = _dot_tb(dx_ref[...], w_ref[2])
        sp = _sigmoid(zgp[...].astype(F32))
        sc = _sigmoid(zgc[...].astype(F32))
        dyp = (dm * sp).astype(BF16)
        dyc = (dm * sc).astype(BF16)
        dzo[0] = (dm * ypc_ref[0].astype(F32) * sp * (1.0 - sp)).astype(BF16)
        dzo[1] = (dm * ypc_ref[1].astype(F32) * sc * (1.0 - sc)).astype(BF16)
        dyo[0] = dyp
        dyo[1] = dyc
        dpq[0] = _dot_tb(dyp, w_ref[0]).astype(BF16)
        dpq[1] = _dot_tb(dyc, w_ref[1]).astype(BF16)

    blocks = _nbytes((tm, d), BF16) * 3 + _nbytes((2, tm, d), BF16) * 4 + _nbytes((3, d, d), BF16)
    return _pcall(
        body, name="mixer_bwd", grid=(t // tm,),
        in_specs=[pl.BlockSpec((None, tm, d), lambda i: (2, i, 0)),
                  pl.BlockSpec((tm, d), lambda i: (i, 4)),
                  pl.BlockSpec((tm, d), lambda i: (i, 5)),
                  pl.BlockSpec((2, tm, d), lambda i: (0, i, 0)),
                  pl.BlockSpec((3, d, d), lambda i: (0, 0, 0))],
        out_specs=[pl.BlockSpec((2, tm, d), lambda i: (0, i, 0)),
                   pl.BlockSpec((2, tm, d), lambda i: (2, i, 0)),
                   pl.BlockSpec((2, tm, d), lambda i: (0, i, 0))],
        out_shape=[SDS(rhs3.shape, BF16), SDS((N_SPLITS, t, d), BF16), SDS((2, t, d), BF16)],
        input_output_aliases={0: 0},
        sem=("parallel",), blocks=blocks, temps=8 * _nbytes((tm, d), F32), comm=comm,
    )(rhs3, z, z, ypc, w3)


def _conv_bwd(dz, dpq, z, conv_w, nseq, comm=None):
    _, t, d = dz.shape
    s = t // nseq
    c = _tile(d, 128, LANES)
    nb = d // c

    def body(dz_in, dq_ref, zb, zc, zv, cw, dzo, gw_ref):
        del dz_in

        @pl.when(pl.program_id(1) == 0)
        def _():
            gw_ref[...] = jnp.zeros_like(gw_ref)

        row = lax.broadcasted_iota(jnp.int32, (s, c), 0)
        b = zb[...].astype(F32)
        cm = zc[...].astype(F32)
        v = zv[...].astype(F32)
        cv = cm * v
        cv1 = _shift_down(cv, 1, row)
        cv2 = _shift_down(cv, 2, row)
        w0, w1, w2 = cw[pl.ds(0, 1), :], cw[pl.ds(1, 1), :], cw[pl.ds(2, 1), :]
        cc = w2 * cv + w1 * cv1 + w0 * cv2
        dq = dq_ref[...].astype(F32)
        dzo[0] = (dq * cc).astype(BF16)
        dcc = dq * b
        gw_ref[pl.ds(0, 1), :] += jnp.sum(dcc * cv2, axis=0, keepdims=True)
        gw_ref[pl.ds(1, 1), :] += jnp.sum(dcc * cv1, axis=0, keepdims=True)
        gw_ref[pl.ds(2, 1), :] += jnp.sum(dcc * cv, axis=0, keepdims=True)
        dcv = w2 * dcc + w1 * _shift_up(dcc, 1, row) + w0 * _shift_up(dcc, 2, row)
        dzo[1] = (dcv * v).astype(BF16)
        dzo[2] = (dcv * cm).astype(BF16)

    blocks = 4 * _nbytes((s, c), BF16) + _nbytes((3, s, c), BF16)
    return _pcall(
        body, name="conv_bwd", grid=(nb, nseq),
        in_specs=[ANY,
                  pl.BlockSpec((None, s, c), lambda j, b: (1, b, j)),
                  pl.BlockSpec((s, c), lambda j, b: (b, nb + j)),
                  pl.BlockSpec((s, c), lambda j, b: (b, 2 * nb + j)),
                  pl.BlockSpec((s, c), lambda j, b: (b, 3 * nb + j)),
                  pl.BlockSpec((3, c), lambda j, b: (0, j))],
        out_specs=[pl.BlockSpec((3, s, c), lambda j, b: (0, b, j)),
                   pl.BlockSpec((3, c), lambda j, b: (0, j))],
        out_shape=[SDS(dz.shape, BF16), SDS((3, d), F32)],
        input_output_aliases={0: 0},
        sem=("parallel", "arbitrary"), blocks=blocks, temps=16 * _nbytes((s, c), F32), comm=comm,
    )(dz, dpq, z, z, z, conv_w)


def _pool_bwd_call(dz, dpq, z, pool_w, pool_scale, nseq, comm=None):
    _, t, d = dz.shape
    s = t // nseq
    c = d // N_GROUPS

    def body(dz_in, dp_ref, zp, pw, ps, dzo, gpw_ref, gps_ref):
        del dz_in
        j = pl.program_id(0)

        @pl.when(pl.program_id(1) == 0)
        def _():
            gpw_ref[...] = jnp.zeros_like(gpw_ref)
            gps_ref[...] = jnp.zeros_like(gps_ref)

        row = lax.broadcasted_iota(jnp.int32, (s, c), 0)
        for gi, win in enumerate(POOL_WINDOWS):
            @pl.when(j == gi)
            def _(win=win):
                pb = _pool_fwd(zp[...].astype(F32), win, row).astype(BF16)
                plin = _dot(pb, pw[...])
                dps = dp_ref[...].astype(F32)
                gps_ref[...] += jnp.sum(dps * plin, axis=0, keepdims=True)
                dplb = (dps * ps[...]).astype(BF16)
                gpw_ref[...] += _dot_ta(pb, dplb)
                dzo[...] = _pool_bwd(_dot_tb(dplb, pw[...]), win, row).astype(BF16)

    blocks = 3 * _nbytes((s, c), BF16) + _nbytes((c, c), BF16) + _nbytes((c, c), F32)
    return _pcall(
        body, name="pool_bwd", grid=(N_GROUPS, nseq),
        in_specs=[ANY,
                  pl.BlockSpec((None, s, c), lambda j, b: (0, b, j)),
                  pl.BlockSpec((s, c), lambda j, b: (b, j)),
                  pl.BlockSpec((None, c, c), lambda j, b: (j, 0, 0)),
                  pl.BlockSpec((1, c), lambda j, b: (0, j))],
        out_specs=[pl.BlockSpec((None, s, c), lambda j, b: (3, b, j)),
                   pl.BlockSpec((None, c, c), lambda j, b: (j, 0, 0)),
                   pl.BlockSpec((1, c), lambda j, b: (0, j))],
        out_shape=[SDS(dz.shape, BF16), SDS((N_GROUPS, c, c), F32), SDS((1, d), F32)],
        input_output_aliases={0: 0},
        sem=("parallel", "arbitrary"), blocks=blocks, temps=10 * _nbytes((s, c), F32), comm=comm,
    )(dz, dpq, z, pool_w, pool_scale)


def _dz_plane(zb):
    return jnp.where(zb < 4, (zb + 3) % 4, zb)


def _wgrad_in(h1, dz, nsh, comm=None):
    t, d = h1.shape
    ws = N_SPLITS * d // nsh
    kb = _tile(math.gcd(d, ws), 512, LANES)
    npl = d // kb
    nps = ws // kb
    tk = _tile(t, WGRAD_TOKENS, SUBLANES_BF16)
    nk = t // tk

    def body(a_ref, b_ref, o_ref, acc):
        _acc_over(pl.program_id(1), nk, _dot_ta(a_ref[...], b_ref[...]), acc, o_ref)

    blocks = _nbytes((tk, d), BF16) + _nbytes((tk, kb), BF16) + _nbytes((d, kb), BF16)
    return _pcall(
        body, name="wgrad_in", grid=(N_SPLITS * npl, nk),
        in_specs=[pl.BlockSpec((tk, d), lambda cb, k: (k, 0)),
                  pl.BlockSpec((None, tk, kb), lambda cb, k: (_dz_plane(cb // npl), k, cb % npl))],
        out_specs=[pl.BlockSpec((None, d, kb), lambda cb, k: (cb // nps, 0, cb % nps))],
        out_shape=[SDS((nsh, d, ws), BF16)],
        scratch_shapes=[pltpu.VMEM((d, kb), F32)],
        sem=("parallel", "arbitrary"), blocks=blocks, temps=2 * _nbytes((d, kb), F32), comm=comm,
    )(h1, dz)


def _mixer_bwd_dx(dz, w_in, x, dx1, g_mix, comm=None):
    npln, t, d = dz.shape
    nsh, _, ws = w_in.shape
    tm = _tile(t, 256, SUBLANES_BF16)
    kb = _tile(math.gcd(d, ws), 512, LANES)
    npl = d // kb
    nps = ws // kb

    def body(dz_ref, w_ref, x_ref, dx1_ref, g_ref, dx_ref, gg_ref):
        @pl.when(pl.program_id(0) == 0)
        def _():
            gg_ref[...] = jnp.zeros_like(gg_ref)

        dh = None
        for cb in range(npln * npl):
            zb = cb // npl
            plane = (zb + 3) % 4 if zb < 4 else zb
            part = _dot_tb(dz_ref[plane, :, (cb % npl) * kb:(cb % npl + 1) * kb],
                           w_ref[cb // nps, :, (cb % nps) * kb:(cb % nps + 1) * kb])
            dh = part if dh is None else dh + part
        xh, inv = _rms_fwd(x_ref[...])
        gg_ref[...] += jnp.sum(dh * xh, axis=0, keepdims=True)
        dx_ref[...] = dx1_ref[...] + _rms_bwd(dh, xh, inv, g_ref[...])

    blocks = _nbytes((npln, tm, d), BF16) + 3 * _nbytes((tm, d), F32)
    return _pcall(
        body, name="mixer_bwd_dx", grid=(t // tm,),
        in_specs=[pl.BlockSpec((npln, tm, d), lambda i: (0, i, 0)),
                  pl.BlockSpec((nsh, d, ws), lambda i: (0, 0, 0), pipeline_mode=pl.Buffered(1)),
                  pl.BlockSpec((tm, d), lambda i: (i, 0)),
                  pl.BlockSpec((tm, d), lambda i: (i, 0)),
                  pl.BlockSpec((1, d), lambda i: (0, 0))],
        out_specs=[pl.BlockSpec((tm, d), lambda i: (i, 0)),
                   pl.BlockSpec((1, d), lambda i: (0, 0))],
        out_shape=[SDS((t, d), F32), SDS((1, d), F32)],
        sem=("arbitrary",), blocks=blocks, temps=_nbytes(w_in.shape, BF16) + 8 * _nbytes((tm, d), F32), comm=comm,
    )(dz, w_in, x, dx1, g_mix)


N_BIG = 5
SHARD_MAJOR = (0, 2)
ROWS_DIM1 = (1, 4)


def _ds(start, size, align):
    if isinstance(start, int):
        return pl.ds(start, size)
    return pl.ds(pl.multiple_of(start, align), size)


def _piece(a, ref, k, h):
    if a in SHARD_MAJOR:
        r = ref.shape[1] // 2
        return ref.at[k, _ds(h * r, r, SUBLANES_BF16), :]
    if a in ROWS_DIM1:
        r = ref.shape[1] // 8
        return ref.at[:, _ds((2 * k + h) * r, r, SUBLANES_BF16), :]
    r = ref.shape[0] // 8
    return ref.at[_ds((2 * k + h) * r, r, SUBLANES_BF16), :]


def _half(a, ref, h):
    if a in ROWS_DIM1:
        r = ref.shape[1] // 2
        return ref.at[:, _ds(h * r, r, SUBLANES_BF16), :]
    r = ref.shape[0] // 2
    return ref.at[_ds(h * r, r, SUBLANES_BF16), :]


def _piece_shape(a, full_shape):
    if a in SHARD_MAJOR:
        return (full_shape[1] // 2, full_shape[2])
    if a in ROWS_DIM1:
        return (full_shape[0], full_shape[1] // 8, full_shape[2])
    return (full_shape[0] // 8, full_shape[1])


def _shard_shape(a, full_shape):
    if a in SHARD_MAJOR:
        return (full_shape[1], full_shape[2])
    if a in ROWS_DIM1:
        return (full_shape[0], full_shape[1] // 4, full_shape[2])
    return (full_shape[0] // 4, full_shape[1])


def _rows_axis(a):
    return 1 if a in ROWS_DIM1 else 0


def _piece_block(a, full_shape):
    ps = _piece_shape(a, full_shape)
    if a in SHARD_MAJOR:
        return (None,) + ps, lambda k, c: (k, c, 0)
    if a in ROWS_DIM1:
        return ps, lambda k, c: (0, 2 * k + c, 0)
    return ps, lambda k, c: (2 * k + c, 0)


def _coords():
    return lax.axis_index("x"), lax.axis_index("y"), lax.axis_index("c")


def _peer_chips(x, y):
    return [(1 - x, y), (x, 1 - y), (1 - x, 1 - y)]


def _remote(src, dst, ssem, rsem, dev):
    return pltpu.make_async_remote_copy(src_ref=src, dst_ref=dst, send_sem=ssem, recv_sem=rsem,
                                        device_id=dev, device_id_type=MESH)


def _dma_sems(*counts):
    return [pltpu.SemaphoreType.DMA((n,)) for n in counts]


def _symmetric(ins, out_shapes, sems, copies, peers, aliases=None):
    def start(cins, couts, csems):
        for cp in copies(cins, couts, csems):
            cp.start()

    def finish(cins, couts, csems):
        for cp in copies(cins, couts, csems):
            cp.wait()

    return _Comm(ins, out_shapes, sems, start, finish, peers, aliases)


def _rows_part(a, ref, part):
    if part is None:
        return ref
    p, q, n = part
    ax = _rows_axis(a)
    r = ref.shape[ax] // n
    return ref.at[tuple(pl.ds(p * r, (q - p) * r) if d == ax else slice(None) for d in range(len(ref.shape)))]


def _merge(comms):
    ins, outs, sems, aliases, spans = [], [], [], {}, []
    for cm in comms:
        spans.append((len(ins), len(outs), len(sems)))
        for i, o in cm.aliases.items():
            aliases[len(ins) + i] = len(outs) + o
        ins += cm.ins
        outs += cm.out_shapes
        sems += cm.sems

    def each(fn_name):
        def run(cins, couts, csems):
            for cm, (i0, o0, s0) in zip(comms, spans):
                getattr(cm, fn_name)(cins[i0:i0 + len(cm.ins)], couts[o0:o0 + len(cm.out_shapes)],
                                     csems[s0:s0 + len(cm.sems)])
        return run

    return _Comm(ins, outs, sems, each("start"), each("finish"), frozenset().union(*[cm.peers for cm in comms]),
                 aliases)


def _gather_comm(arrs, locs, full_shapes, part=None, into=None):
    n = len(arrs)

    def own(cins, couts, csems):
        x, y, c = _coords()
        j = 2 * x + y
        return [_remote(_rows_part(a, _half(a, cins[q], h), part), _rows_part(a, _piece(a, couts[q], j, h), part),
                        csems[0].at[2 * q + h], csems[1].at[2 * q + h], (x, y, 1 - c))
                for q, a in enumerate(arrs) for h in range(2)]

    def sends(cins, couts, csems):
        x, y, c = _coords()
        j = 2 * x + y
        return [_remote(_rows_part(a, _half(a, cins[q], c), part), _rows_part(a, _piece(a, couts[q], j, c), part),
                        csems[2].at[3 * q + i], csems[3].at[3 * q + i], (px, py, c))
                for q, a in enumerate(arrs) for i, (px, py) in enumerate(_peer_chips(x, y))]

    def forwards(couts, csems, half_of):
        x, y, c = _coords()
        out = []
        for q, a in enumerate(arrs):
            for i, (px, py) in enumerate(_peer_chips(x, y)):
                landed = _rows_part(a, _piece(a, couts[q], 2 * px + py, half_of(c)), part)
                out.append(_remote(landed, landed, csems[4].at[3 * q + i], csems[5].at[3 * q + i], (x, y, 1 - c)))
        return out

    def start(cins, couts, csems):
        for cp in sends(cins, couts, csems) + own(cins, couts, csems):
            cp.start()

    def finish(cins, couts, csems):
        fw = forwards(couts, csems, lambda c: c)
        for cp, f in zip(sends(cins, couts, csems), fw):
            cp.wait_recv()
            f.start()
        for f in forwards(couts, csems, lambda c: 1 - c):
            f.wait_recv()
        for cp in sends(cins, couts, csems) + fw:
            cp.wait_send()
        for cp in own(cins, couts, csems):
            cp.wait()

    ins = [locs[a] for a in arrs] + ([into[a] for a in arrs] if into else [])
    return _Comm(ins, [SDS(full_shapes[a], BF16) for a in arrs],
                 _dma_sems(2 * n, 2 * n, 3 * n, 3 * n, 3 * n, 3 * n), start, finish, CHIPS + (SIBLING,),
                 aliases={n + q: q for q in range(n)} if into else None)


def _halves_comm(arrs, gbs):
    n = len(arrs)

    def copies(cins, couts, csems):
        x, y, c = _coords()
        return [_remote(_piece(a, cins[q], k, 1 - c), couts[q].at[k], csems[0].at[4 * q + k], csems[1].at[4 * q + k],
                        (x, y, 1 - c)) for q, a in enumerate(arrs) for k in range(4)]

    return _symmetric([gbs[a] for a in arrs], [SDS((4,) + _piece_shape(a, gbs[a].shape), BF16) for a in arrs],
                      _dma_sems(4 * n, 4 * n), copies, [SIBLING])


def _chips_comm(arrs, ps, part=None, into=None):
    n = len(arrs)

    def copies(cins, couts, csems):
        x, y, c = _coords()
        return [_remote(_rows_part(a, cins[q].at[2 * px + py], part), _rows_part(a, couts[q].at[i], part),
                        csems[0].at[3 * q + i], csems[1].at[3 * q + i], (px, py, c))
                for q, a in enumerate(arrs) for i, (px, py) in enumerate(_peer_chips(x, y))]

    ins = [ps[a] for a in arrs] + ([into[a] for a in arrs] if into else [])
    return _symmetric(ins, [SDS((3,) + ps[a].shape[1:], BF16) for a in arrs], _dma_sems(3 * n, 3 * n), copies, CHIPS,
                      aliases={n + q: q for q in range(n)} if into else None)


def _result_comm(arrs, gs):
    n = len(arrs)

    def copies(cins, couts, csems):
        x, y, c = _coords()
        return [_remote(_half(a, cins[q], c), _half(a, couts[q], c), csems[0].at[q], csems[1].at[q], (x, y, 1 - c))
                for q, a in enumerate(arrs)]

    return _symmetric([gs[a] for a in arrs], [SDS(gs[a].shape, F32) for a in arrs], _dma_sems(n, n), copies,
                      [SIBLING], aliases={q: q for q in range(n)})


def _add_halves(arrs, gbs, lands, c_arr, name):
    n = len(arrs)

    def body(c_ref, *refs):
        del c_ref
        for q in range(n):
            refs[2 * n + q][...] = (refs[q][...].astype(F32) + refs[n + q][...].astype(F32)).astype(BF16)

    g_specs, l_specs, o_specs, blocks = [], [], [], 0
    for a in arrs:
        bs, imap = _piece_block(a, gbs[a].shape)
        ps = _piece_shape(a, gbs[a].shape)
        g_specs.append(pl.BlockSpec(bs, lambda k, c_ref, imap=imap: imap(k, c_ref[0])))
        nd = len(ps)
        l_specs.append(pl.BlockSpec((None,) + ps, lambda k, c_ref, nd=nd: (k,) + (0,) * nd))
        o_specs.append(pl.BlockSpec((None,) + ps, lambda k, c_ref, nd=nd: (k,) + (0,) * nd))
        blocks += 3 * _nbytes(ps, BF16)
    return list(pl.pallas_call(
        body, name=name,
        grid_spec=pltpu.PrefetchScalarGridSpec(
            num_scalar_prefetch=1, grid=(4,), in_specs=g_specs + l_specs, out_specs=o_specs),
        out_shape=[SDS((4,) + _piece_shape(a, gbs[a].shape), BF16) for a in arrs],
        compiler_params=_params(("parallel",), blocks, blocks),
    )(c_arr, *[gbs[a] for a in arrs], *lands))


def _sum_chips(a, p, land, shard_shape, jc_arr, name):
    ps = land.shape[1:]
    ax = _rows_axis(a)
    rows = ps[ax]
    nsub = 2 if rows % (2 * SUBLANES_BF16) == 0 else 1
    bs = tuple(r // nsub if q == ax else r for q, r in enumerate(ps))
    nd = len(ps)

    def at_rows(v):
        return tuple(v if q == ax else 0 for q in range(nd))

    def body(jc_ref, p_ref, l_ref, o_ref):
        del jc_ref
        acc = p_ref[...].astype(F32) + l_ref[0].astype(F32)
        acc = acc + l_ref[1].astype(F32)
        o_ref[...] = acc + l_ref[2].astype(F32)

    blocks = 4 * _nbytes(bs, BF16) + _nbytes(bs, F32)
    return pl.pallas_call(
        body, name=name,
        grid_spec=pltpu.PrefetchScalarGridSpec(
            num_scalar_prefetch=1, grid=(nsub,),
            in_specs=[pl.BlockSpec((None,) + bs, lambda s, jc: (jc[0],) + at_rows(s)),
                      pl.BlockSpec((3,) + bs, lambda s, jc: (0,) + at_rows(s))],
            out_specs=pl.BlockSpec(bs, lambda s, jc: at_rows(jc[1] * nsub + s))),
        out_shape=SDS(shard_shape, F32),
        compiler_params=_params(("parallel",), blocks, 2 * _nbytes(bs, F32)),
    )(jc_arr, p, land)


def _small_comm(v):
    rows = v.shape[0]

    def copies(cins, couts, csems):
        x, y, c = _coords()
        me = 4 * x + 2 * y + c
        out = [pltpu.make_async_copy(cins[0], couts[0].at[me], csems[0].at[0])]
        for dlt in range(1, 8):
            px = 1 - x if (dlt >> 2) & 1 else x
            py = 1 - y if (dlt >> 1) & 1 else y
            pc = 1 - c if dlt & 1 else c
            out.append(_remote(cins[0], couts[0].at[me], csems[1].at[dlt - 1], csems[2].at[dlt - 1], (px, py, pc)))
        return out

    return _symmetric([v], [SDS((8, rows, LANES), F32)], _dma_sems(1, 7, 7), copies, EVERYONE)


def _sum8(slots, name):
    def body(s_ref, o_ref):
        acc = s_ref[0]
        for i in range(1, 8):
            acc = acc + s_ref[i]
        o_ref[...] = acc

    return pl.pallas_call(
        body, name=name,
        in_specs=[pl.BlockSpec(memory_space=pltpu.VMEM)], out_specs=pl.BlockSpec(memory_space=pltpu.VMEM),
        out_shape=SDS(slots.shape[1:], F32),
    )(slots)


def _adamw(w, g, m, v, name, g_plane=None):
    rows, cols = w.shape
    tr = _tile(rows, max(SUBLANES_F32, (256 * 1024 // cols) // SUBLANES_F32 * SUBLANES_F32), SUBLANES_F32)

    def body(w_ref, g_ref, m_ref, v_ref, go_ref, d_ref, mo_ref, vo_ref):
        gr = g_ref[...]
        mn = ADAM_B1 * m_ref[...] + (1.0 - ADAM_B1) * gr
        vn = ADAM_B2 * v_ref[...] + (1.0 - ADAM_B2) * (gr * gr)
        m_hat = mn / (1.0 - ADAM_B1 ** ADAM_STEP)
        v_hat = vn / (1.0 - ADAM_B2 ** ADAM_STEP)
        d_ref[...] = -ADAM_LR * (m_hat / (jnp.sqrt(v_hat) + ADAM_EPS) + ADAM_WD * w_ref[...])
        go_ref[...] = gr
        mo_ref[...] = mn
        vo_ref[...] = vn

    spec = pl.BlockSpec((tr, cols), lambda i: (i, 0))
    g_spec = spec if g_plane is None else pl.BlockSpec((None, tr, cols), lambda i: (g_plane, i, 0))
    return pl.pallas_call(
        body, name=name, grid=(rows // tr,),
        in_specs=[spec, g_spec, spec, spec], out_specs=[spec, spec, spec, spec],
        out_shape=[SDS((rows, cols), F32)] * 4,
        compiler_params=_params(("parallel",), 8 * _nbytes((tr, cols), F32), 4 * _nbytes((tr, cols), F32)),
    )(w, g, m, v)


def _pack(parts):
    rows = []
    for p in parts:
        r = p.reshape(-1, LANES)
        pad = (-r.shape[0]) % SUBLANES_F32
        if pad:
            r = jnp.pad(r, ((0, pad), (0, 0)))
        rows.append(r)
    return jnp.concatenate(rows, axis=0)


def _unpack(packed, shapes):
    out, at = [], 0
    for s in shapes:
        n = 1
        for q in s:
            n *= q
        r = n // LANES
        out.append(packed[at:at + r].reshape(s))
        at += r + (-r) % SUBLANES_F32
    return out


def kernel(x, norm_mix, w_in, pool_w, pool_scale, w_pool_proj, conv_w, w_conv_out, w_o, norm_ffn, w_up, ffn_conv_w, ffn_conv_b, w_down, norm_final, loss_target, m_norm_mix, m_w_in, m_pool_w, m_pool_scale, m_w_pool_proj, m_conv_w, m_w_conv_out, m_w_o, m_norm_ffn, m_w_up, m_ffn_conv_w, m_ffn_conv_b, m_w_down, m_norm_final, v_norm_mix, v_w_in, v_pool_w, v_pool_scale, v_w_pool_proj, v_conv_w, v_w_conv_out, v_w_o, v_norm_ffn, v_w_up, v_ffn_conv_w, v_ffn_conv_b, v_w_down, v_norm_final):
    nseq, seq, d = x.shape
    t = nseq * seq
    f = w_down.shape[1] * 4
    c = d // N_GROUPS
    xy = lax.axis_index("x") * 2 + lax.axis_index("y")
    c_arr = lax.axis_index("c").astype(jnp.int32).reshape(1)
    jc_arr = jnp.stack([xy, lax.axis_index("c")]).astype(jnp.int32)
    nsh = 4
    zero = jnp.zeros((), jnp.int32)

    locs = [w_in[0].astype(BF16),
            jnp.stack([w_pool_proj[0], w_conv_out[0], w_o[0]]).astype(BF16),
            w_up[0].astype(BF16), w_down[0].astype(BF16), pool_w[0].astype(BF16)]
    full_shapes = [(nsh, d, N_SPLITS * d // nsh), (3, d, d), (nsh, d, 2 * f // nsh), (f, d), (N_GROUPS, c, c)]

    cw_pad = lax.dynamic_update_slice(jnp.zeros((3, d), F32), conv_w[0], (zero, xy * (d // 4)))
    fw_pad = lax.dynamic_update_slice(jnp.zeros((3, 2 * f), F32), ffn_conv_w[0], (zero, xy * (f // 2)))
    small_w = _pack([cw_pad, fw_pad]) * 0.5

    x2d = x.reshape(t, d)
    tgt = loss_target.reshape(t, d)
    ax, ay = lax.axis_index("x"), lax.axis_index("y")
    order = jnp.stack([xy, 2 * (1 - ax) + ay, 2 * ax + 1 - ay, 2 * (1 - ax) + 1 - ay]).astype(jnp.int32)
    (z, h1, w_in_f), (pool_w_f, w3_f, slots_w) = _fwd_in(
        x2d, norm_mix, locs[0], order,
        _merge([_gather_comm([4], locs, full_shapes), _gather_comm([1], locs, full_shapes, part=(0, 1, 2)),
                _small_comm(small_w)]))
    conv_w_f, ffn_cw_f = _unpack(_sum8(slots_w, "sum8_weights"), [(3, d), (3, 2 * f)])
    ffn_cw_p = ffn_cw_f.reshape(3, 2, f).transpose(1, 0, 2)
    ffn_cb_p = ffn_conv_b.reshape(2, 1, f)
    (lhs3,), (w3_f,) = _mixer_mid_fwd(z, pool_w_f, pool_scale, conv_w_f, nseq,
                                      _gather_comm([1], locs, full_shapes, part=(1, 2, 2), into={1: w3_f}))
    (lhs3, ypc, x1, h2), (w_up_f,) = _mixer_out(lhs3, z, x2d, w3_f, norm_ffn, _gather_comm([2], locs, full_shapes))
    (u0,), (w_down_f,) = _ffn_up(h2, w_up_f, f, _gather_comm([3], locs, full_shapes))
    act, ua = _ffn_mid_fwd(u0, ffn_cw_p, ffn_cb_p, nseq)
    dx2, dx2b, loss11, g_norm_final = _ffn_down_loss(act, w_down_f, x1, tgt, norm_final.reshape(1, d))

    gbs, lands, ps, lands2, rs = {}, {}, {}, {}, {}
    tn_up = _tile(2 * f // nsh, 1408, LANES)
    npp = f // tn_up

    def add(arrs, name):
        for a, p in zip(arrs, _add_halves(arrs, gbs, [lands[a] for a in arrs], c_arr, name)):
            ps[a] = p

    def summed(a):
        rs[a] = _sum_chips(a, ps[a], lands2[a], _shard_shape(a, full_shapes[a]), jc_arr, "sum_chips_%d" % a)

    (gbs[3],), _ = _wgrad(act, dx2b, "wgrad_down", tr=tn_up, tn=d)
    (da,), (lands[3],) = _ffn_bwd_da(dx2b, w_down_f, _halves_comm([3], gbs))
    add([3], "add_halves_down")
    (du0, g_ffn_cw_p, g_ffn_cb_p), (lands2[3],) = _ffn_mid_bwd(da, u0, ua, ffn_cw_p, nseq, _chips_comm([3], ps))
    summed(3)
    (gbs[2],), (rs[3],) = _wgrad(h2, du0, "wgrad_up", tr=d, tn=tn_up, b_plane_of=lambda n: (n // npp, n % npp),
                                 out_shards=nsh, comm=_result_comm([3], rs))
    (dx1, rhs3, g_norm_ffn), (lands[2],) = _ffn_bwd_dx1(du0, w_up_f, x1, dx2, norm_ffn, 3, _halves_comm([2], gbs))
    add([2], "add_halves_up")
    (rhs3, dz, dpq), (lands2[2],) = _mixer_bwd(rhs3, z, ypc, w3_f, _chips_comm([2], ps, part=(0, 1, 2)))
    (gbs[1],), (lands2[2],) = _wgrad3(lhs3, rhs3, _chips_comm([2], ps, part=(1, 2, 2), into=lands2))
    summed(2)
    (dz, g_conv_w), (lands[1], rs[2]) = _conv_bwd(dz, dpq, z, conv_w_f, nseq,
                                                  _merge([_halves_comm([1], gbs), _result_comm([2], rs)]))
    add([1], "add_halves_sq3")
    (dz, g_pool_w, g_pool_scale), _ = _pool_bwd_call(dz, dpq, z, pool_w_f, pool_scale, nseq)
    gbs[4] = g_pool_w.astype(BF16)
    (gbs[0],), (lands2[1],) = _wgrad_in(h1, dz, nsh, _chips_comm([1], ps))
    summed(1)
    lands[0], lands[4] = _run_comm(_halves_comm([0, 4], gbs), "exchange_halves_in")
    add([0, 4], "add_halves_in")
    g_ffn_cw = g_ffn_cw_p.transpose(1, 0, 2).reshape(3, 2 * f)
    small_a = _pack([g_pool_scale, g_norm_ffn, g_ffn_cb_p.reshape(1, 2 * f), g_norm_final.reshape(d), g_conv_w,
                     g_ffn_cw, jnp.pad(loss11, ((0, SUBLANES_F32 - 1), (0, LANES - 1)))])
    (grad_x, g_norm_mix), (lands2[0], lands2[4], rs[1], slots_a) = _mixer_bwd_dx(
        dz, w_in_f, x2d, dx1, norm_mix,
        _merge([_chips_comm([0, 4], ps), _result_comm([1], rs), _small_comm(small_a)]))
    summed(0)
    summed(4)
    rs[0], rs[4], slots_b = _run_comm(_merge([_result_comm([0, 4], rs), _small_comm(_pack([g_norm_mix]))]),
                                      "exchange_result_in")
    shapes_a = [(1, d), (1, d), (1, 2 * f), (d,), (3, d), (3, 2 * f), (SUBLANES_F32, LANES)]
    gs_pool_scale, gs_norm_ffn, gs_ffn_cb, gs_norm_final, gs_conv_w, gs_ffn_cw, loss_blk = _unpack(
        _sum8(slots_a, "sum8_grads"), shapes_a)
    (gs_norm_mix,) = _unpack(_sum8(slots_b, "sum8_norm_mix"), [(1, d)])
    gs_conv_w = lax.dynamic_slice(gs_conv_w, (zero, xy * (d // 4)), (3, d // 4))
    gs_ffn_cw = lax.dynamic_slice(gs_ffn_cw, (zero, xy * (f // 2)), (3, f // 2))

    def upd(w, g, m, v, name, g_plane=None):
        shape = w.shape
        rows = 1
        for q in shape[:-1]:
            rows *= q
        g2 = g if g_plane is not None else g.reshape(rows, shape[-1])
        outs = _adamw(w.reshape(rows, shape[-1]), g2, m.reshape(rows, shape[-1]), v.reshape(rows, shape[-1]),
                      name, g_plane)
        return [o.reshape(shape) for o in outs]

    res = {
        "w_in": upd(w_in, rs[0], m_w_in, v_w_in, "adamw_w_in"),
        "pool_w": upd(pool_w, rs[4], m_pool_w, v_pool_w, "adamw_pool_w"),
        "w_pool_proj": upd(w_pool_proj, rs[1], m_w_pool_proj, v_w_pool_proj, "adamw_w_pool_proj", 0),
        "w_conv_out": upd(w_conv_out, rs[1], m_w_conv_out, v_w_conv_out, "adamw_w_conv_out", 1),
        "w_o": upd(w_o, rs[1], m_w_o, v_w_o, "adamw_w_o", 2),
        "w_up": upd(w_up, rs[2], m_w_up, v_w_up, "adamw_w_up"),
        "w_down": upd(w_down, rs[3], m_w_down, v_w_down, "adamw_w_down"),
    }

    small_names = ["norm_mix", "pool_scale", "norm_ffn", "ffn_conv_b", "norm_final", "conv_w", "ffn_conv_w"]
    small_ws = [norm_mix, pool_scale, norm_ffn, ffn_conv_b, norm_final, conv_w, ffn_conv_w]
    small_ms = [m_norm_mix, m_pool_scale, m_norm_ffn, m_ffn_conv_b, m_norm_final, m_conv_w, m_ffn_conv_w]
    small_vs = [v_norm_mix, v_pool_scale, v_norm_ffn, v_ffn_conv_b, v_norm_final, v_conv_w, v_ffn_conv_w]
    small_gs = [gs_norm_mix, gs_pool_scale, gs_norm_ffn, gs_ffn_cb, gs_norm_final, gs_conv_w, gs_ffn_cw]
    _, sd, sm, sv = _adamw(_pack(small_ws), _pack(small_gs), _pack(small_ms), _pack(small_vs), "adamw_small")
    shapes = [w.shape for w in small_ws]
    sd, sm, sv = _unpack(sd, shapes), _unpack(sm, shapes), _unpack(sv, shapes)
    for i, nm in enumerate(small_names):
        res[nm] = [small_gs[i].reshape(shapes[i]), sd[i], sm[i], sv[i]]

    order = ["norm_mix", "w_in", "pool_w", "pool_scale", "w_pool_proj", "conv_w", "w_conv_out", "w_o", "norm_ffn",
             "w_up", "ffn_conv_w", "ffn_conv_b", "w_down", "norm_final"]
    return (loss_blk[0, 0], grad_x.reshape(x.shape), *[res[n][0] for n in order], *[res[n][1] for n in order],
            *[res[n][2] for n in order], *[res[n][3] for n in order])
```

```python
import math

import jax
import jax.numpy as jnp
from jax import lax
from jax.experimental import pallas as pl
from jax.experimental.pallas import tpu as pltpu

F32 = jnp.float32
BF16 = jnp.bfloat16
SDS = jax.ShapeDtypeStruct
MESH = pl.DeviceIdType.MESH

RMS_EPS = 1e-6
POOL_WINDOWS = (2, 4, 8, 16)
N_GROUPS = len(POOL_WINDOWS)
N_SPLITS = 6

ADAM_LR = 0.001
ADAM_B1 = 0.9
ADAM_B2 = 0.999
ADAM_EPS = 1e-08
ADAM_WD = 0.01
ADAM_STEP = 10

LANES = 128
SUBLANES_F32 = 8
SUBLANES_BF16 = 16
VMEM_BYTES = 64 * 1024 * 1024
VMEM_CAP = VMEM_BYTES - 8 * 1024 * 1024
VMEM_FLOOR = 16 * 1024 * 1024

ANY = pl.BlockSpec(memory_space=pl.ANY)


def _tile(dim, pref, align):
    if dim <= pref:
        return dim
    t = (pref // align) * align
    while t >= align:
        if dim % t == 0:
            return t
        t -= align
    return dim


def _nbytes(shape, dtype):
    n = 1
    for s in shape:
        n *= s
    return n * jnp.dtype(dtype).itemsize


def _params(sem, block_bytes, temp_bytes=0, collective_id=None):
    need = 2 * block_bytes + temp_bytes + 4 * 1024 * 1024
    return pltpu.CompilerParams(dimension_semantics=sem, collective_id=collective_id,
                                vmem_limit_bytes=int(min(max(need, VMEM_FLOOR), VMEM_CAP)))


SIBLING = (0, 0, 1)
CHIPS = ((1, 0, 0), (0, 1, 0), (1, 1, 0))
EVERYONE = tuple((a, b, c) for a in range(2) for b in range(2) for c in range(2) if a + b + c)
PEER_SETS = (frozenset([SIBLING]), frozenset(CHIPS), frozenset(CHIPS + (SIBLING,)), frozenset(EVERYONE))
MID_AT = 0.75


def _collective_id(peers):
    return PEER_SETS.index(frozenset(peers))


def _handshake(peers):
    x, y, c = lax.axis_index("x"), lax.axis_index("y"), lax.axis_index("c")
    bar = pltpu.get_barrier_semaphore()
    for fx, fy, fc in sorted(peers):
        dev = (1 - x if fx else x, 1 - y if fy else y, 1 - c if fc else c)
        pl.semaphore_signal(bar, inc=1, device_id=dev, device_id_type=MESH)
    pl.semaphore_wait(bar, len(peers))


class _Comm:
    def __init__(self, ins, out_shapes, sems, start, finish, peers, aliases=None, mid=None):
        self.ins = list(ins)
        self.out_shapes = list(out_shapes)
        self.sems = list(sems)
        self.start = start
        self.finish = finish
        self.mid = mid
        self.peers = frozenset(peers)
        self.aliases = dict(aliases or {})


def _pcall(body, *, name, grid, in_specs, out_specs, out_shape, sem, blocks, temps=0, scratch_shapes=(),
           input_output_aliases=None, comm=None):
    in_specs = list(in_specs)
    out_specs = list(out_specs)
    out_shape = list(out_shape)
    scratch_shapes = list(scratch_shapes)
    aliases = dict(input_output_aliases or {})
    n_in, n_out, n_scr = len(in_specs), len(out_shape), len(scratch_shapes)
    if comm is None:
        call = pl.pallas_call(
            body, name=name, grid=grid, in_specs=in_specs, out_specs=out_specs, out_shape=out_shape,
            scratch_shapes=scratch_shapes, input_output_aliases=aliases,
            compiler_params=_params(sem, blocks, temps))
        return lambda *args: (list(call(*args)), [])

    nci, nco = len(comm.ins), len(comm.out_shapes)
    n_steps = 1
    for g in grid:
        n_steps *= g

    def hosted(*refs):
        ins = refs[:n_in]
        cins = refs[n_in:n_in + nci]
        outs = refs[n_in + nci:n_in + nci + n_out]
        couts = refs[n_in + nci + n_out:n_in + nci + n_out + nco]
        scr = refs[n_in + nci + n_out + nco:n_in + nci + n_out + nco + n_scr]
        csems = refs[n_in + nci + n_out + nco + n_scr:]
        first = None
        last = None
        step = 0
        for q, g in enumerate(grid):
            pid = pl.program_id(q)
            first = (pid == 0) if first is None else first & (pid == 0)
            last = (pid == g - 1) if last is None else last & (pid == g - 1)
            step = step * g + pid

        @pl.when(first)
        def _():
            _handshake(comm.peers)
            comm.start(cins, couts, csems)

        if comm.mid is not None:
            @pl.when(step == int(MID_AT * n_steps))
            def _():
                comm.mid(cins, couts, csems)

        body(*ins, *outs, *scr)

        @pl.when(last)
        def _():
            comm.finish(cins, couts, csems)

    for i, o in comm.aliases.items():
        aliases[n_in + i] = n_out + o
    call = pl.pallas_call(
        hosted, name=name, grid=grid, in_specs=in_specs + [ANY] * nci, out_specs=out_specs + [ANY] * nco,
        out_shape=out_shape + comm.out_shapes, scratch_shapes=scratch_shapes + comm.sems,
        input_output_aliases=aliases,
        compiler_params=_params(("arbitrary",) * len(grid), blocks, temps, _collective_id(comm.peers)))

    def run(*args):
        res = call(*args, *comm.ins)
        return list(res[:n_out]), list(res[n_out:])

    return run


def _run_comm(comm, name):
    def body(*refs):
        nci, nco = len(comm.ins), len(comm.out_shapes)
        cins, couts, csems = refs[:nci], refs[nci:nci + nco], refs[nci + nco:]
        _handshake(comm.peers)
        comm.start(cins, couts, csems)
        if comm.mid is not None:
            comm.mid(cins, couts, csems)
        comm.finish(cins, couts, csems)

    return list(pl.pallas_call(
        body, name=name, in_specs=[ANY] * len(comm.ins), out_specs=[ANY] * len(comm.out_shapes),
        out_shape=comm.out_shapes, scratch_shapes=comm.sems, input_output_aliases=comm.aliases,
        compiler_params=pltpu.CompilerParams(collective_id=_collective_id(comm.peers)),
    )(*comm.ins))


def _dot(a, b):
    return jnp.dot(a, b, preferred_element_type=F32)


def _dot_tb(a, b):
    return lax.dot_general(a, b, (((1,), (1,)), ((), ())), preferred_element_type=F32)


def _dot_ta(a, b):
    return lax.dot_general(a, b, (((0,), (0,)), ((), ())), preferred_element_type=F32)


def _rms_fwd(x):
    inv = lax.rsqrt(jnp.mean(x * x, axis=-1, keepdims=True) + RMS_EPS)
    return x * inv, inv


def _rms_bwd(dy, xhat, inv, g):
    gd = dy * g
    return inv * (gd - xhat * jnp.mean(gd * xhat, axis=-1, keepdims=True))


def _sigmoid(x):
    return 1.0 / (1.0 + jnp.exp(-x))


def _shift_down(x, k, row):
    return jnp.where(row >= k, pltpu.roll(x, k, 0), 0.0)


def _shift_up(x, k, row):
    s = x.shape[0]
    return jnp.where(row < s - k, pltpu.roll(x, s - k, 0), 0.0)


def _pool_fwd(u, win, row):
    s = u
    k = 1
    while k < win:
        s = s + _shift_down(s, k, row)
        k *= 2
    cnt = jnp.minimum(row + 1, win).astype(F32)
    return s / cnt - u


def _pool_bwd(dp, win, row):
    cnt = jnp.minimum(row + 1, win).astype(F32)
    s = dp / cnt
    k = 1
    while k < win:
        s = s + _shift_up(s, k, row)
        k *= 2
    return s - dp


def _acc_over(k, nk, part, acc, o_ref):
    @pl.when(k == 0)
    def _():
        acc[...] = part

    @pl.when(k > 0)
    def _():
        acc[...] += part

    @pl.when(k == nk - 1)
    def _():
        o_ref[...] = acc[...].astype(o_ref.dtype)


def _fwd_in(x, g, w_loc, order, comm):
    t, d = x.shape
    ws = w_loc.shape[1]
    nsh = order.shape[0]
    assert nsh == 4, "the shard walk below is written for the 2 x 2 chips of the mesh"
    tm = _tile(t, 1024, SUBLANES_BF16)
    ni = t // tm
    nci, nco = len(comm.ins), len(comm.out_shapes)
    all_peers = comm.peers | frozenset(CHIPS + (SIBLING,))

    def body(order_ref, x_ref, g_ref, loc_ref, *rest):
        del order_ref
        cins = rest[:nci]
        z_ref, h_ref, full_ref = rest[nci:nci + 3]
        couts = rest[nci + 3:nci + 3 + nco]
        (hs, wbuf, wsem, own_s, own_r, snd_s, snd_r, fwd_s, fwd_r, rly_s, rly_r) = rest[nci + 3 + nco:nci + 14 + nco]
        csems = rest[nci + 14 + nco:]
        j = pl.program_id(0)
        i = pl.program_id(1)
        x_, y_, c_ = _coords()
        own = 2 * x_ + y_
        sib = (x_, y_, 1 - c_)
        peers = _peer_chips(x_, y_)

        def sends():
            return [_remote(_half(0, loc_ref, c_), _piece(0, full_ref, own, c_), snd_s.at[p], snd_r.at[p], (px, py, c_))
                    for p, (px, py) in enumerate(peers[:2])]

        def relays():
            out = []
            for q, (src_p, dst_p) in enumerate(((0, 1), (1, 0))):
                sx, sy = peers[src_p]
                part = _rows_part(0, _piece(0, full_ref, 2 * sx + sy, c_), (q, q + 1, 2))
                out.append(_remote(part, part, rly_s.at[q], rly_r.at[q], (*peers[dst_p], c_)))
            return out

        def owns():
            return [_remote(_half(0, loc_ref, h), _piece(0, full_ref, own, h), own_s.at[h], own_r.at[h], sib)
                    for h in range(2)]

        def forward(p, half):
            px, py = peers[p]
            landed = _piece(0, full_ref, 2 * px + py, half)
            return _remote(landed, landed, fwd_s.at[p], fwd_r.at[p], sib)

        def load(src, slot):
            return pltpu.make_async_copy(src, wbuf.at[slot], wsem.at[slot])

        @pl.when((j == 0) & (i == 0))
        def _():
            _handshake(all_peers)
            for cp in sends() + owns():
                cp.start()
            load(loc_ref, 0).start()
            comm.start(cins, couts, csems)

        @pl.when(j == 0)
        def _():
            xh, _ = _rms_fwd(x_ref[...])
            h = (xh * g_ref[...]).astype(BF16)
            hs[pl.ds(pl.multiple_of(i * tm, tm), tm), :] = h
            h_ref[...] = h

        slot = j % 2

        @pl.when(i == 0)
        def _():
            load(loc_ref, slot).wait()

        z_ref[...] = _dot(hs[pl.ds(pl.multiple_of(i * tm, tm), tm), :], wbuf[slot]).astype(BF16)

        for p in range(nsh - 1):
            @pl.when((i == ni - 1) & (j == p))
            def _(p=p):
                px, py = peers[p]
                if p == 0:
                    for cp in sends():
                        cp.wait_recv()
                    for cp in relays():
                        cp.start()
                if p == 2:
                    for cp in relays():
                        cp.wait_recv()
                forward(p, c_).start()
                forward(p, 1 - c_).wait_recv()
                load(full_ref.at[2 * px + py], 1 - slot).start()

        @pl.when((j == nsh - 1) & (i == ni - 1))
        def _():
            for cp in sends() + relays() + [forward(p, c_) for p in range(nsh - 1)]:
                cp.wait_send()
            for cp in owns():
                cp.wait()
            comm.finish(cins, couts, csems)

    last = ni - 1
    blocks = _nbytes((tm, d), F32) + _nbytes((tm, ws), BF16) + _nbytes((tm, d), BF16)
    scratch = _nbytes((t, d), BF16) + 2 * _nbytes((d, ws), BF16)
    res = pl.pallas_call(
        body, name="fwd_in",
        grid_spec=pltpu.PrefetchScalarGridSpec(
            num_scalar_prefetch=1, grid=(nsh, ni),
            in_specs=[pl.BlockSpec((tm, d), lambda j, i, o: (jnp.where(j == 0, i, last), 0)),
                      pl.BlockSpec((1, d), lambda j, i, o: (0, 0)), ANY] + [ANY] * nci,
            out_specs=[pl.BlockSpec((tm, ws), lambda j, i, o: (i, o[j])),
                       pl.BlockSpec((tm, d), lambda j, i, o: (jnp.where(j == 0, i, last), 0)), ANY] + [ANY] * nco,
            scratch_shapes=[pltpu.VMEM((t, d), BF16), pltpu.VMEM((2, d, ws), BF16)]
            + _dma_sems(2, 2, 2, 2, 2, nsh - 1, nsh - 1, 2, 2) + comm.sems),
        out_shape=[SDS((t, nsh * ws), BF16), SDS((t, d), BF16), SDS((nsh, d, ws), BF16)] + comm.out_shapes,
        input_output_aliases={4 + i: 3 + o for i, o in comm.aliases.items()},
        compiler_params=_params(("arbitrary", "arbitrary"), blocks, scratch + 3 * _nbytes((tm, d), F32),
                                _collective_id(all_peers)),
    )(order, x, g, w_loc, *comm.ins)
    return list(res[:3]), list(res[3:])


def _mixer_mid_fwd(z, pool_w, pool_scale, conv_w, nseq, comm=None):
    t = z.shape[0]
    d = pool_scale.shape[1]
    s = t // nseq
    c = d // N_GROUPS

    def body(zp, zb, zc, zv, pw, ps, cw, o):
        j = pl.program_id(1)
        row = lax.broadcasted_iota(jnp.int32, (s, c), 0)
        for gi, win in enumerate(POOL_WINDOWS):
            @pl.when(j == gi)
            def _(win=win):
                pooled = _pool_fwd(zp[...].astype(F32), win, row)
                o[0] = (_dot(pooled.astype(BF16), pw[...]) * ps[...]).astype(BF16)

        cv = zc[...].astype(F32) * zv[...].astype(F32)
        cc = (cw[pl.ds(2, 1), :] * cv + cw[pl.ds(1, 1), :] * _shift_down(cv, 1, row)
              + cw[pl.ds(0, 1), :] * _shift_down(cv, 2, row))
        o[1] = (zb[...].astype(F32) * cc).astype(BF16)

    blocks = 4 * _nbytes((s, c), BF16) + _nbytes((c, c), BF16) + _nbytes((2, s, c), BF16)
    return _pcall(
        body, name="mixer_mid_fwd", grid=(nseq, N_GROUPS),
        in_specs=[pl.BlockSpec((s, c), lambda b, j: (b, j)),
                  pl.BlockSpec((s, c), lambda b, j: (b, N_GROUPS + j)),
                  pl.BlockSpec((s, c), lambda b, j: (b, 2 * N_GROUPS + j)),
                  pl.BlockSpec((s, c), lambda b, j: (b, 3 * N_GROUPS + j)),
                  pl.BlockSpec((None, c, c), lambda b, j: (j, 0, 0)),
                  pl.BlockSpec((1, c), lambda b, j: (0, j)),
                  pl.BlockSpec((3, c), lambda b, j: (0, j))],
        out_specs=[pl.BlockSpec((2, s, c), lambda b, j: (0, b, j))],
        out_shape=[SDS((3, t, d), BF16)],
        sem=("parallel", "parallel"), blocks=blocks, temps=8 * _nbytes((s, c), F32), comm=comm,
    )(z, z, z, z, pool_w, pool_scale, conv_w)


def _mixer_out(lhs3, z, x, w3, g_ffn, comm=None):
    t, d = x.shape
    tm = _tile(t, 256, SUBLANES_BF16)

    def body(pq, zgp, zgc, x_ref, w_ref, g_ref, mrg, ypc, x1o, h2o):
        yp = _dot(pq[0], w_ref[0])
        yc = _dot(pq[1], w_ref[1])
        m = _sigmoid(zgp[...].astype(F32)) * yp + _sigmoid(zgc[...].astype(F32)) * yc
        mb = m.astype(BF16)
        x1 = x_ref[...] + _dot(mb, w_ref[2])
        ypc[0] = yp.astype(BF16)
        ypc[1] = yc.astype(BF16)
        mrg[...] = mb
        x1o[...] = x1
        xh, _ = _rms_fwd(x1)
        h2o[...] = (xh * g_ref[...]).astype(BF16)

    blocks = (_nbytes((2, tm, d), BF16) * 2 + _nbytes((tm, d), BF16) * 4 + _nbytes((tm, d), F32) * 2
              + _nbytes((3, d, d), BF16))
    return _pcall(
        body, name="mixer_out", grid=(t // tm,),
        in_specs=[pl.BlockSpec((2, tm, d), lambda i: (0, i, 0)),
                  pl.BlockSpec((tm, d), lambda i: (i, 4)),
                  pl.BlockSpec((tm, d), lambda i: (i, 5)),
                  pl.BlockSpec((tm, d), lambda i: (i, 0)),
                  pl.BlockSpec((3, d, d), lambda i: (0, 0, 0)),
                  pl.BlockSpec((1, d), lambda i: (0, 0))],
        out_specs=[pl.BlockSpec((None, tm, d), lambda i: (2, i, 0)),
                   pl.BlockSpec((2, tm, d), lambda i: (0, i, 0)),
                   pl.BlockSpec((tm, d), lambda i: (i, 0)),
                   pl.BlockSpec((tm, d), lambda i: (i, 0))],
        out_shape=[SDS(lhs3.shape, BF16), SDS((2, t, d), BF16), SDS((t, d), F32), SDS((t, d), BF16)],
        input_output_aliases={0: 0},
        sem=("parallel",), blocks=blocks, temps=8 * _nbytes((tm, d), F32), comm=comm,
    )(lhs3, z, z, x, w3, g_ffn)


def _ffn_up(h2, w_up, f, comm=None):
    t, d = h2.shape
    _, _, ws = w_up.shape
    tm = _tile(t, 1024, SUBLANES_BF16)
    tn = _tile(ws, 1408, LANES)
    nps = ws // tn
    npp = f // tn

    def body(h_ref, w_ref, o_ref):
        o_ref[...] = _dot(h_ref[...], w_ref[...]).astype(BF16)

    blocks = _nbytes((tm, d), BF16) + _nbytes((d, tn), BF16) + _nbytes((tm, tn), BF16)
    return _pcall(
        body, name="ffn_up", grid=(t // tm, 2 * npp),
        in_specs=[pl.BlockSpec((tm, d), lambda i, j: (i, 0)),
                  pl.BlockSpec((None, d, tn), lambda i, j: (j // nps, 0, j % nps))],
        out_specs=[pl.BlockSpec((None, tm, tn), lambda i, j: (j // npp, i, j % npp))],
        out_shape=[SDS((2, t, f), BF16)],
        sem=("parallel", "parallel"), blocks=blocks, temps=_nbytes((tm, tn), F32), comm=comm,
    )(h2, w_up)


def _conv3_rows(u, u1, u2, w_ref, p):
    return w_ref[p, pl.ds(2, 1), :] * u + w_ref[p, pl.ds(1, 1), :] * u1 + w_ref[p, pl.ds(0, 1), :] * u2


WGRAD_TOKENS = 2048
CHUNK = 64
HALO = SUBLANES_F32


def _up1_up2(u, nxt):
    rows = u.shape[0]
    ext = jnp.concatenate([u, nxt], axis=0)
    n = rows + HALO
    return pltpu.roll(ext, n - 1, 0)[:rows], pltpu.roll(ext, n - 2, 0)[:rows]


def _fold8(x):
    return jnp.sum(x.reshape(x.shape[0] // SUBLANES_F32, SUBLANES_F32, x.shape[1]), axis=0)


def _ffn_mid_fwd(u0, cw, cb, nseq):
    _, t, f = u0.shape
    s = t // nseq
    c = _tile(f, 256, LANES)

    def body(u_ref, w_ref, b_ref, a_ref, uo_ref):
        row = lax.broadcasted_iota(jnp.int32, (s, c), 0)
        act = []
        for p in range(2):
            u = u_ref[p].astype(F32)
            act.append(_conv3_rows(u, _shift_down(u, 1, row), _shift_down(u, 2, row), w_ref, p) + b_ref[p])
            uo_ref[p] = act[p].astype(BF16)
        ug, uv = act
        a_ref[...] = (ug * _sigmoid(ug) * uv).astype(BF16)

    blocks = 2 * _nbytes((2, s, c), BF16) + _nbytes((s, c), BF16)
    outs, _ = _pcall(
        body, name="ffn_mid_fwd", grid=(f // c, nseq),
        in_specs=[pl.BlockSpec((2, s, c), lambda j, b: (0, b, j)),
                  pl.BlockSpec((2, 3, c), lambda j, b: (0, 0, j)),
                  pl.BlockSpec((2, 1, c), lambda j, b: (0, 0, j))],
        out_specs=[pl.BlockSpec((s, c), lambda j, b: (b, j)),
                   pl.BlockSpec((2, s, c), lambda j, b: (0, b, j))],
        out_shape=[SDS((t, f), BF16), SDS((2, t, f), BF16)],
        sem=("parallel", "parallel"), blocks=blocks, temps=8 * _nbytes((s, c), F32),
    )(u0, cw, cb)
    return outs


def _ffn_down_loss(a, w_down, x1, tgt, g_fin):
    t, f = a.shape
    d = x1.shape[1]
    tm = _tile(t, 256, SUBLANES_BF16)
    nsteps = t // tm

    def body(a_ref, w_ref, x1_ref, t_ref, g_ref, dx_ref, dxb_ref, loss_ref, gg_ref, lacc):
        i = pl.program_id(0)

        @pl.when(i == 0)
        def _():
            lacc[...] = jnp.zeros_like(lacc)
            gg_ref[...] = jnp.zeros_like(gg_ref)

        x2 = x1_ref[...] + _dot(a_ref[...], w_ref[...])
        xh, inv = _rms_fwd(x2)
        g = g_ref[...]
        e = xh * g - t_ref[...]
        lacc[...] += jnp.sum(e * e, axis=0, keepdims=True)
        dy = e * (1.0 / d)
        gg_ref[...] += jnp.sum(dy * xh, axis=0, keepdims=True)
        dx2 = _rms_bwd(dy, xh, inv, g)
        dx_ref[...] = dx2
        dxb_ref[...] = dx2.astype(BF16)

        @pl.when(i == nsteps - 1)
        def _():
            loss_ref[...] = jnp.sum(lacc[...], axis=1, keepdims=True) * (0.5 / d)

    blocks = (_nbytes((tm, f), BF16) + _nbytes((f, d), BF16) + 3 * _nbytes((tm, d), F32) + _nbytes((tm, d), BF16))
    outs, _ = _pcall(
        body, name="ffn_down_loss", grid=(nsteps,),
        in_specs=[pl.BlockSpec((tm, f), lambda i: (i, 0)), pl.BlockSpec((f, d), lambda i: (0, 0)),
                  pl.BlockSpec((tm, d), lambda i: (i, 0)), pl.BlockSpec((tm, d), lambda i: (i, 0)),
                  pl.BlockSpec((1, d), lambda i: (0, 0))],
        out_specs=[pl.BlockSpec((tm, d), lambda i: (i, 0)), pl.BlockSpec((tm, d), lambda i: (i, 0)),
                   pl.BlockSpec((1, 1), lambda i: (0, 0)), pl.BlockSpec((1, d), lambda i: (0, 0))],
        out_shape=[SDS((t, d), F32), SDS((t, d), BF16), SDS((1, 1), F32), SDS((1, d), F32)],
        scratch_shapes=[pltpu.VMEM((1, d), F32)],
        sem=("arbitrary",), blocks=blocks, temps=8 * _nbytes((tm, d), F32),
    )(a, w_down, x1, tgt, g_fin)
    return outs


def _ffn_bwd_da(dxb, w_down, comm=None):
    t, d = dxb.shape
    f = w_down.shape[0]
    tm = _tile(t, 1024, SUBLANES_BF16)
    tn = _tile(f, 1408, LANES)

    def body(x_ref, w_ref, o_ref):
        o_ref[...] = _dot_tb(x_ref[...], w_ref[...]).astype(BF16)

    blocks = _nbytes((tm, d), BF16) + _nbytes((tn, d), BF16) + _nbytes((tm, tn), BF16)
    return _pcall(
        body, name="ffn_bwd_da", grid=(t // tm, f // tn),
        in_specs=[pl.BlockSpec((tm, d), lambda i, j: (i, 0)), pl.BlockSpec((tn, d), lambda i, j: (j, 0))],
        out_specs=[pl.BlockSpec((tm, tn), lambda i, j: (i, j))],
        out_shape=[SDS((t, f), BF16)],
        sem=("parallel", "parallel"), blocks=blocks, temps=_nbytes((tm, tn), F32), comm=comm,
    )(dxb, w_down)


def _ffn_mid_bwd(da, u0, ua, cw, nseq, comm=None):
    _, t, f = u0.shape
    s = t // nseq
    c = _tile(f, 128, LANES)
    r = _tile(s, CHUNK, SUBLANES_BF16)
    n = s // r

    def body(da_ref, u_ref, ua_ref, w_ref, du_ref, gw_ref, gb_ref):
        @pl.when(pl.program_id(1) == 0)
        def _():
            gw_ref[...] = jnp.zeros_like(gw_ref)
            gb_ref[...] = jnp.zeros_like(gb_ref)

        def step(i, carry):
            nxt, sums = carry
            rows = pl.ds(pl.multiple_of((n - 1 - i) * r, r), r)
            ug = ua_ref[0, rows, :].astype(F32)
            uv = ua_ref[1, rows, :].astype(F32)
            sg = _sigmoid(ug)
            dacc = da_ref[rows, :].astype(F32)
            dus = (dacc * uv * sg * (1.0 + ug * (1.0 - sg)), dacc * (ug * sg))
            first, new_sums = [], []
            for p in range(2):
                du = dus[p]
                d1, d2 = _up1_up2(du, nxt[p])
                du_ref[p, rows, :] = _conv3_rows(du, d1, d2, w_ref, p).astype(BF16)
                u = u_ref[p, rows, :].astype(F32)
                sb, s0, s1, s2 = sums[p]
                new_sums.append((sb + _fold8(du), s0 + _fold8(d2 * u), s1 + _fold8(d1 * u), s2 + _fold8(du * u)))
                first.append(du[:HALO])
            return tuple(first), tuple(new_sums)

        zero = jnp.zeros((HALO, c), F32)
        _, sums = lax.fori_loop(0, n, step, ((zero, zero), ((zero,) * 4,) * 2))
        for p in range(2):
            sb, s0, s1, s2 = sums[p]
            gb_ref[p] += jnp.sum(sb, axis=0, keepdims=True)
            gw_ref[p, pl.ds(0, 1), :] += jnp.sum(s0, axis=0, keepdims=True)
            gw_ref[p, pl.ds(1, 1), :] += jnp.sum(s1, axis=0, keepdims=True)
            gw_ref[p, pl.ds(2, 1), :] += jnp.sum(s2, axis=0, keepdims=True)

    blocks = _nbytes((s, c), BF16) + 3 * _nbytes((2, s, c), BF16)
    return _pcall(
        body, name="ffn_mid_bwd", grid=(f // c, nseq),
        in_specs=[pl.BlockSpec((s, c), lambda j, b: (b, j)),
                  pl.BlockSpec((2, s, c), lambda j, b: (0, b, j)),
                  pl.BlockSpec((2, s, c), lambda j, b: (0, b, j)),
                  pl.BlockSpec((2, 3, c), lambda j, b: (0, 0, j))],
        out_specs=[pl.BlockSpec((2, s, c), lambda j, b: (0, b, j)),
                   pl.BlockSpec((2, 3, c), lambda j, b: (0, 0, j)),
                   pl.BlockSpec((2, 1, c), lambda j, b: (0, 0, j))],
        out_shape=[SDS((2, t, f), BF16), SDS((2, 3, f), F32), SDS((2, 1, f), F32)],
        sem=("parallel", "arbitrary"), blocks=blocks, temps=4 * 1024 * 1024, comm=comm,
    )(da, u0, ua, cw)


def _wgrad(a, b, name, *, tr, tn, b_plane_of=None, out_shards=None, comm=None):
    t, m = a.shape
    n_total = b.shape[-1] * (b.shape[0] if b.ndim == 3 else 1)
    tk = _tile(t, WGRAD_TOKENS, SUBLANES_BF16)
    nk = t // tk

    def body(a_ref, b_ref, o_ref, acc):
        _acc_over(pl.program_id(2), nk, _dot_ta(a_ref[...], b_ref[...]), acc, o_ref)

    if b.ndim == 3:
        b_spec = pl.BlockSpec((None, tk, tn), lambda r, n, k: (b_plane_of(n)[0], k, b_plane_of(n)[1]))
    else:
        b_spec = pl.BlockSpec((tk, tn), lambda r, n, k: (k, n))
    if out_shards is None:
        o_spec = pl.BlockSpec((tr, tn), lambda r, n, k: (r, n))
        o_shape = SDS((m, n_total), BF16)
    else:
        nps = n_total // out_shards // tn
        o_spec = pl.BlockSpec((None, tr, tn), lambda r, n, k: (n // nps, r, n % nps))
        o_shape = SDS((out_shards, m, n_total // out_shards), BF16)
    blocks = _nbytes((tk, tr), BF16) + _nbytes((tk, tn), BF16) + _nbytes((tr, tn), BF16)
    return _pcall(
        body, name=name, grid=(m // tr, n_total // tn, nk),
        in_specs=[pl.BlockSpec((tk, tr), lambda r, n, k: (k, r)), b_spec],
        out_specs=[o_spec], out_shape=[o_shape],
        scratch_shapes=[pltpu.VMEM((tr, tn), F32)],
        sem=("parallel", "parallel", "arbitrary"), blocks=blocks, temps=2 * _nbytes((tr, tn), F32), comm=comm,
    )(a, b)


def _wgrad3(lhs3, rhs3, comm=None):
    nw, t, d = lhs3.shape
    tk = _tile(t, WGRAD_TOKENS, SUBLANES_BF16)
    nk = t // tk

    def body(a_ref, b_ref, o_ref, acc):
        _acc_over(pl.program_id(1), nk, _dot_ta(a_ref[...], b_ref[...]), acc, o_ref)

    blocks = 2 * _nbytes((tk, d), BF16) + _nbytes((d, d), BF16)
    return _pcall(
        body, name="wgrad_sq3", grid=(nw, nk),
        in_specs=[pl.BlockSpec((None, tk, d), lambda w, k: (w, k, 0)),
                  pl.BlockSpec((None, tk, d), lambda w, k: (w, k, 0))],
        out_specs=[pl.BlockSpec((None, d, d), lambda w, k: (w, 0, 0))],
        out_shape=[SDS((nw, d, d), BF16)],
        scratch_shapes=[pltpu.VMEM((d, d), F32)],
        sem=("parallel", "arbitrary"), blocks=blocks, temps=2 * _nbytes((d, d), F32), comm=comm,
    )(lhs3, rhs3)


def _ffn_bwd_dx1(du0, w_up, x1, dx2, g_ffn, n_planes_out, comm=None):
    _, t, f = du0.shape
    d = x1.shape[1]
    nsh, _, ws = w_up.shape
    tm = _tile(t, 256, SUBLANES_BF16)
    spp = f // ws

    def body(du_ref, w_ref, x1_ref, dx2_ref, g_ref, dx1_ref, dxb_ref, gg_ref):
        @pl.when(pl.program_id(0) == 0)
        def _():
            gg_ref[...] = jnp.zeros_like(gg_ref)

        dh = None
        for k in range(nsh):
            part = _dot_tb(du_ref[k // spp, :, (k % spp) * ws:(k % spp + 1) * ws], w_ref[k])
            dh = part if dh is None else dh + part
        xh, inv = _rms_fwd(x1_ref[...])
        gg_ref[...] += jnp.sum(dh * xh, axis=0, keepdims=True)
        dx1 = dx2_ref[...] + _rms_bwd(dh, xh, inv, g_ref[...])
        dx1_ref[...] = dx1
        dxb_ref[...] = dx1.astype(BF16)

    blocks = _nbytes((2, tm, f), BF16) + 3 * _nbytes((tm, d), F32) + _nbytes((tm, d), BF16)
    return _pcall(
        body, name="ffn_bwd_dx1", grid=(t // tm,),
        in_specs=[pl.BlockSpec((2, tm, f), lambda i: (0, i, 0)),
                  pl.BlockSpec((nsh, d, ws), lambda i: (0, 0, 0), pipeline_mode=pl.Buffered(1)),
                  pl.BlockSpec((tm, d), lambda i: (i, 0)),
                  pl.BlockSpec((tm, d), lambda i: (i, 0)),
                  pl.BlockSpec((1, d), lambda i: (0, 0))],
        out_specs=[pl.BlockSpec((tm, d), lambda i: (i, 0)),
                   pl.BlockSpec((None, tm, d), lambda i: (n_planes_out - 1, i, 0)),
                   pl.BlockSpec((1, d), lambda i: (0, 0))],
        out_shape=[SDS((t, d), F32), SDS((n_planes_out, t, d), BF16), SDS((1, d), F32)],
        sem=("arbitrary",), blocks=blocks, temps=_nbytes(w_up.shape, BF16) + 8 * _nbytes((tm, d), F32), comm=comm,
    )(du0, w_up, x1, dx2, g_ffn)


def _mixer_bwd(rhs3, z, ypc, w3, comm=None):
    _, t, d = rhs3.shape
    tm = _tile(t, 256, SUBLANES_BF16)

    def body(dx_ref, zgp, zgc, ypc_ref, w_ref, dyo, dzo, dpq):
        dm = _dot_tb(dx_ref[...], w_ref[2])
        sp = _sigmoid(zgp[...].astype(F32))
        sc = _sigmoid(zgc[...].astype(F32))
        dyp = (dm * sp).astype(BF16)
        dyc = (dm * sc).astype(BF16)
        dzo[0] = (dm * ypc_ref[0].astype(F32) * sp * (1.0 - sp)).astype(BF16)
        dzo[1] = (dm * ypc_ref[1].astype(F32) * sc * (1.0 - sc)).astype(BF16)
        dyo[0] = dyp
        dyo[1] = dyc
        dpq[0] = _dot_tb(dyp, w_ref[0]).astype(BF16)
        dpq[1] = _dot_tb(dyc, w_ref[1]).astype(BF16)

    blocks = _nbytes((tm, d), BF16) * 3 + _nbytes((2, tm, d), BF16) * 4 + _nbytes((3, d, d), BF16)
    return _pcall(
        body, name="mixer_bwd", grid=(t // tm,),
        in_specs=[pl.BlockSpec((None, tm, d), lambda i: (2, i, 0)),
                  pl.BlockSpec((tm, d), lambda i: (i, 4)),
                  pl.BlockSpec((tm, d), lambda i: (i, 5)),
                  pl.BlockSpec((2, tm, d), lambda i: (0, i, 0)),
                  pl.BlockSpec((3, d, d), lambda i: (0, 0, 0))],
        out_specs=[pl.BlockSpec((2, tm, d), lambda i: (0, i, 0)),
                   pl.BlockSpec((2, tm, d), lambda i: (2, i, 0)),
                   pl.BlockSpec((2, tm, d), lambda i: (0, i, 0))],
        out_shape=[SDS(rhs3.shape, BF16), SDS((N_SPLITS, t, d), BF16), SDS((2, t, d), BF16)],
        input_output_aliases={0: 0},
        sem=("parallel",), blocks=blocks, temps=8 * _nbytes((tm, d), F32), comm=comm,
    )(rhs3, z, z, ypc, w3)


def _conv_bwd(dz, dpq, z, conv_w, nseq, comm=None):
    _, t, d = dz.shape
    s = t // nseq
    c = _tile(d, 128, LANES)
    nb = d // c

    def body(dz_in, dq_ref, zb, zc, zv, cw, dzo, gw_ref):
        del dz_in

        @pl.when(pl.program_id(1) == 0)
        def _():
            gw_ref[...] = jnp.zeros_like(gw_ref)

        row = lax.broadcasted_iota(jnp.int32, (s, c), 0)
        b = zb[...].astype(F32)
        cm = zc[...].astype(F32)
        v = zv[...].astype(F32)
        cv = cm * v
        cv1 = _shift_down(cv, 1, row)
        cv2 = _shift_down(cv, 2, row)
        w0, w1, w2 = cw[pl.ds(0, 1), :], cw[pl.ds(1, 1), :], cw[pl.ds(2, 1), :]
        cc = w2 * cv + w1 * cv1 + w0 * cv2
        dq = dq_ref[...].astype(F32)
        dzo[0] = (dq * cc).astype(BF16)
        dcc = dq * b
        gw_ref[pl.ds(0, 1), :] += jnp.sum(dcc * cv2, axis=0, keepdims=True)
        gw_ref[pl.ds(1, 1), :] += jnp.sum(dcc * cv1, axis=0, keepdims=True)
        gw_ref[pl.ds(2, 1), :] += jnp.sum(dcc * cv, axis=0, keepdims=True)
        dcv = w2 * dcc + w1 * _shift_up(dcc, 1, row) + w0 * _shift_up(dcc, 2, row)
        dzo[1] = (dcv * v).astype(BF16)
        dzo[2] = (dcv * cm).astype(BF16)

    blocks = 4 * _nbytes((s, c), BF16) + _nbytes((3, s, c), BF16)
    return _pcall(
        body, name="conv_bwd", grid=(nb, nseq),
        in_specs=[ANY,
                  pl.BlockSpec((None, s, c), lambda j, b: (1, b, j)),
                  pl.BlockSpec((s, c), lambda j, b: (b, nb + j)),
                  pl.BlockSpec((s, c), lambda j, b: (b, 2 * nb + j)),
                  pl.BlockSpec((s, c), lambda j, b: (b, 3 * nb + j)),
                  pl.BlockSpec((3, c), lambda j, b: (0, j))],
        out_specs=[pl.BlockSpec((3, s, c), lambda j, b: (0, b, j)),
                   pl.BlockSpec((3, c), lambda j, b: (0, j))],
        out_shape=[SDS(dz.shape, BF16), SDS((3, d), F32)],
        input_output_aliases={0: 0},
        sem=("parallel", "arbitrary"), blocks=blocks, temps=16 * _nbytes((s, c), F32), comm=comm,
    )(dz, dpq, z, z, z, conv_w)


def _pool_bwd_call(dz, dpq, z, pool_w, pool_scale, nseq, comm=None):
    _, t, d = dz.shape
    s = t // nseq
    c = d // N_GROUPS

    def body(dz_in, dp_ref, zp, pw, ps, dzo, gpw_ref, gps_ref):
        del dz_in
        j = pl.program_id(0)

        @pl.when(pl.program_id(1) == 0)
        def _():
            gpw_ref[...] = jnp.zeros_like(gpw_ref)
            gps_ref[...] = jnp.zeros_like(gps_ref)

        row = lax.broadcasted_iota(jnp.int32, (s, c), 0)
        for gi, win in enumerate(POOL_WINDOWS):
            @pl.when(j == gi)
            def _(win=win):
                pb = _pool_fwd(zp[...].astype(F32), win, row).astype(BF16)
                plin = _dot(pb, pw[...])
                dps = dp_ref[...].astype(F32)
                gps_ref[...] += jnp.sum(dps * plin, axis=0, keepdims=True)
                dplb = (dps * ps[...]).astype(BF16)
                gpw_ref[...] += _dot_ta(pb, dplb)
                dzo[...] = _pool_bwd(_dot_tb(dplb, pw[...]), win, row).astype(BF16)

    blocks = 3 * _nbytes((s, c), BF16) + _nbytes((c, c), BF16) + _nbytes((c, c), F32)
    return _pcall(
        body, name="pool_bwd", grid=(N_GROUPS, nseq),
        in_specs=[ANY,
                  pl.BlockSpec((None, s, c), lambda j, b: (0, b, j)),
                  pl.BlockSpec((s, c), lambda j, b: (b, j)),
                  pl.BlockSpec((None, c, c), lambda j, b: (j, 0, 0)),
                  pl.BlockSpec((1, c), lambda j, b: (0, j))],
        out_specs=[pl.BlockSpec((None, s, c), lambda j, b: (3, b, j)),
                   pl.BlockSpec((None, c, c), lambda j, b: (j, 0, 0)),
                   pl.BlockSpec((1, c), lambda j, b: (0, j))],
        out_shape=[SDS(dz.shape, BF16), SDS((N_GROUPS, c, c), F32), SDS((1, d), F32)],
        input_output_aliases={0: 0},
        sem=("parallel", "arbitrary"), blocks=blocks, temps=10 * _nbytes((s, c), F32), comm=comm,
    )(dz, dpq, z, pool_w, pool_scale)


def _dz_plane(zb):
    return jnp.where(zb < 4, (zb + 3) % 4, zb)


def _wgrad_in(h1, dz, nsh, comm=None):
    t, d = h1.shape
    ws = N_SPLITS * d // nsh
    kb = _tile(math.gcd(d, ws), 512, LANES)
    npl = d // kb
    nps = ws // kb
    tk = _tile(t, WGRAD_TOKENS, SUBLANES_BF16)
    nk = t // tk

    def body(a_ref, b_ref, o_ref, acc):
        _acc_over(pl.program_id(1), nk, _dot_ta(a_ref[...], b_ref[...]), acc, o_ref)

    blocks = _nbytes((tk, d), BF16) + _nbytes((tk, kb), BF16) + _nbytes((d, kb), BF16)
    return _pcall(
        body, name="wgrad_in", grid=(N_SPLITS * npl, nk),
        in_specs=[pl.BlockSpec((tk, d), lambda cb, k: (k, 0)),
                  pl.BlockSpec((None, tk, kb), lambda cb, k: (_dz_plane(cb // npl), k, cb % npl))],
        out_specs=[pl.BlockSpec((None, d, kb), lambda cb, k: (cb // nps, 0, cb % nps))],
        out_shape=[SDS((nsh, d, ws), BF16)],
        scratch_shapes=[pltpu.VMEM((d, kb), F32)],
        sem=("parallel", "arbitrary"), blocks=blocks, temps=2 * _nbytes((d, kb), F32), comm=comm,
    )(h1, dz)


def _mixer_bwd_dx(dz, w_in, x, dx1, g_mix, comm=None):
    npln, t, d = dz.shape
    nsh, _, ws = w_in.shape
    tm = _tile(t, 256, SUBLANES_BF16)
    kb = _tile(math.gcd(d, ws), 512, LANES)
    npl = d // kb
    nps = ws // kb

    def body(dz_ref, w_ref, x_ref, dx1_ref, g_ref, dx_ref, gg_ref):
        @pl.when(pl.program_id(0) == 0)
        def _():
            gg_ref[...] = jnp.zeros_like(gg_ref)

        dh = None
        for cb in range(npln * npl):
            zb = cb // npl
            plane = (zb + 3) % 4 if zb < 4 else zb
            part = _dot_tb(dz_ref[plane, :, (cb % npl) * kb:(cb % npl + 1) * kb],
                           w_ref[cb // nps, :, (cb % nps) * kb:(cb % nps + 1) * kb])
            dh = part if dh is None else dh + part
        xh, inv = _rms_fwd(x_ref[...])
        gg_ref[...] += jnp.sum(dh * xh, axis=0, keepdims=True)
        dx_ref[...] = dx1_ref[...] + _rms_bwd(dh, xh, inv, g_ref[...])

    blocks = _nbytes((npln, tm, d), BF16) + 3 * _nbytes((tm, d), F32)
    return _pcall(
        body, name="mixer_bwd_dx", grid=(t // tm,),
        in_specs=[pl.BlockSpec((npln, tm, d), lambda i: (0, i, 0)),
                  pl.BlockSpec((nsh, d, ws), lambda i: (0, 0, 0), pipeline_mode=pl.Buffered(1)),
                  pl.BlockSpec((tm, d), lambda i: (i, 0)),
                  pl.BlockSpec((tm, d), lambda i: (i, 0)),
                  pl.BlockSpec((1, d), lambda i: (0, 0))],
        out_specs=[pl.BlockSpec((tm, d), lambda i: (i, 0)),
                   pl.BlockSpec((1, d), lambda i: (0, 0))],
        out_shape=[SDS((t, d), F32), SDS((1, d), F32)],
        sem=("arbitrary",), blocks=blocks, temps=_nbytes(w_in.shape, BF16) + 8 * _nbytes((tm, d), F32), comm=comm,
    )(dz, w_in, x, dx1, g_mix)


N_BIG = 5
SHARD_MAJOR = (0, 2)
ROWS_DIM1 = (1, 4)


def _ds(start, size, align):
    if isinstance(start, int):
        return pl.ds(start, size)
    return pl.ds(pl.multiple_of(start, align), size)


def _piece(a, ref, k, h):
    if a in SHARD_MAJOR:
        r = ref.shape[1] // 2
        return ref.at[k, _ds(h * r, r, SUBLANES_BF16), :]
    if a in ROWS_DIM1:
        r = ref.shape[1] // 8
        return ref.at[:, _ds((2 * k + h) * r, r, SUBLANES_BF16), :]
    r = ref.shape[0] // 8
    return ref.at[_ds((2 * k + h) * r, r, SUBLANES_BF16), :]


def _half(a, ref, h):
    if a in ROWS_DIM1:
        r = ref.shape[1] // 2
        return ref.at[:, _ds(h * r, r, SUBLANES_BF16), :]
    r = ref.shape[0] // 2
    return ref.at[_ds(h * r, r, SUBLANES_BF16), :]


def _piece_shape(a, full_shape):
    if a in SHARD_MAJOR:
        return (full_shape[1] // 2, full_shape[2])
    if a in ROWS_DIM1:
        return (full_shape[0], full_shape[1] // 8, full_shape[2])
    return (full_shape[0] // 8, full_shape[1])


def _shard_shape(a, full_shape):
    if a in SHARD_MAJOR:
        return (full_shape[1], full_shape[2])
    if a in ROWS_DIM1:
        return (full_shape[0], full_shape[1] // 4, full_shape[2])
    return (full_shape[0] // 4, full_shape[1])


def _rows_axis(a):
    return 1 if a in ROWS_DIM1 else 0


def _piece_block(a, full_shape):
    ps = _piece_shape(a, full_shape)
    if a in SHARD_MAJOR:
        return (None,) + ps, lambda k, c: (k, c, 0)
    if a in ROWS_DIM1:
        return ps, lambda k, c: (0, 2 * k + c, 0)
    return ps, lambda k, c: (2 * k + c, 0)


def _coords():
    return lax.axis_index("x"), lax.axis_index("y"), lax.axis_index("c")


def _peer_chips(x, y):
    return [(1 - x, y), (x, 1 - y), (1 - x, 1 - y)]


def _remote(src, dst, ssem, rsem, dev):
    return pltpu.make_async_remote_copy(src_ref=src, dst_ref=dst, send_sem=ssem, recv_sem=rsem,
                                        device_id=dev, device_id_type=MESH)


def _dma_sems(*counts):
    return [pltpu.SemaphoreType.DMA((n,)) for n in counts]


def _symmetric(ins, out_shapes, sems, copies, peers, aliases=None):
    def start(cins, couts, csems):
        for cp in copies(cins, couts, csems):
            cp.start()

    def finish(cins, couts, csems):
        for cp in copies(cins, couts, csems):
            cp.wait()

    return _Comm(ins, out_shapes, sems, start, finish, peers, aliases)


def _rows_part(a, ref, part):
    if part is None:
        return ref
    p, q, n = part
    ax = _rows_axis(a)
    r = ref.shape[ax] // n
    return ref.at[tuple(pl.ds(p * r, (q - p) * r) if d == ax else slice(None) for d in range(len(ref.shape)))]


def _merge(comms):
    ins, outs, sems, aliases, spans = [], [], [], {}, []
    for cm in comms:
        spans.append((len(ins), len(outs), len(sems)))
        for i, o in cm.aliases.items():
            aliases[len(ins) + i] = len(outs) + o
        ins += cm.ins
        outs += cm.out_shapes
        sems += cm.sems

    def each(fn_name):
        def run(cins, couts, csems):
            for cm, (i0, o0, s0) in zip(comms, spans):
                fn = getattr(cm, fn_name)
                if fn is not None:
                    fn(cins[i0:i0 + len(cm.ins)], couts[o0:o0 + len(cm.out_shapes)], csems[s0:s0 + len(cm.sems)])
        return run

    return _Comm(ins, outs, sems, each("start"), each("finish"), frozenset().union(*[cm.peers for cm in comms]),
                 aliases, mid=each("mid") if any(cm.mid is not None for cm in comms) else None)


def _gather_comm(arrs, locs, full_shapes, part=None, into=None):
    n = len(arrs)

    def own(cins, couts, csems):
        x, y, c = _coords()
        j = 2 * x + y
        return [_remote(_rows_part(a, _half(a, cins[q], h), part), _rows_part(a, _piece(a, couts[q], j, h), part),
                        csems[0].at[2 * q + h], csems[1].at[2 * q + h], (x, y, 1 - c))
                for q, a in enumerate(arrs) for h in range(2)]

    def sends(cins, couts, csems):
        x, y, c = _coords()
        j = 2 * x + y
        return [_remote(_rows_part(a, _half(a, cins[q], c), part), _rows_part(a, _piece(a, couts[q], j, c), part),
                        csems[2].at[3 * q + i], csems[3].at[3 * q + i], (px, py, c))
                for q, a in enumerate(arrs) for i, (px, py) in enumerate(_peer_chips(x, y))]

    def forwards(couts, csems, half_of):
        x, y, c = _coords()
        out = []
        for q, a in enumerate(arrs):
            for i, (px, py) in enumerate(_peer_chips(x, y)):
                landed = _rows_part(a, _piece(a, couts[q], 2 * px + py, half_of(c)), part)
                out.append(_remote(landed, landed, csems[4].at[3 * q + i], csems[5].at[3 * q + i], (x, y, 1 - c)))
        return out

    def start(cins, couts, csems):
        for cp in sends(cins, couts, csems) + own(cins, couts, csems):
            cp.start()

    def finish(cins, couts, csems):
        fw = forwards(couts, csems, lambda c: c)
        for cp, f in zip(sends(cins, couts, csems), fw):
            cp.wait_recv()
            f.start()
        for f in forwards(couts, csems, lambda c: 1 - c):
            f.wait_recv()
        for cp in sends(cins, couts, csems) + fw:
            cp.wait_send()
        for cp in own(cins, couts, csems):
            cp.wait()

    ins = [locs[a] for a in arrs] + ([into[a] for a in arrs] if into else [])
    return _Comm(ins, [SDS(full_shapes[a], BF16) for a in arrs],
                 _dma_sems(2 * n, 2 * n, 3 * n, 3 * n, 3 * n, 3 * n), start, finish, CHIPS + (SIBLING,),
                 aliases={n + q: q for q in range(n)} if into else None)


def _ring_gather_comm(arrs, locs, full_shapes):
    n = len(arrs)

    def own(cins, couts, csems):
        x, y, c = _coords()
        j = 2 * x + y
        return [_remote(_half(a, cins[q], h), _piece(a, couts[q], j, h), csems[0].at[2 * q + h],
                        csems[1].at[2 * q + h], (x, y, 1 - c)) for q, a in enumerate(arrs) for h in range(2)]

    def sends(cins, couts, csems):
        x, y, c = _coords()
        j = 2 * x + y
        return [_remote(_half(a, cins[q], c), _piece(a, couts[q], j, c), csems[2].at[2 * q + i],
                        csems[3].at[2 * q + i], (px, py, c))
                for q, a in enumerate(arrs) for i, (px, py) in enumerate(_peer_chips(x, y)[:2])]

    def relays(couts, csems):
        x, y, c = _coords()
        peers = _peer_chips(x, y)
        out = []
        for q, a in enumerate(arrs):
            for r, (src_p, dst_p) in enumerate(((0, 1), (1, 0))):
                sx, sy = peers[src_p]
                rows = _rows_part(a, _piece(a, couts[q], 2 * sx + sy, c), (r, r + 1, 2))
                out.append(_remote(rows, rows, csems[6].at[2 * q + r], csems[7].at[2 * q + r], (*peers[dst_p], c)))
        return out

    def forwards(couts, csems, half_of, which):
        x, y, c = _coords()
        out = []
        for q, a in enumerate(arrs):
            for i in which:
                px, py = _peer_chips(x, y)[i]
                landed = _piece(a, couts[q], 2 * px + py, half_of(c))
                out.append(_remote(landed, landed, csems[4].at[3 * q + i], csems[5].at[3 * q + i], (x, y, 1 - c)))
        return out

    def start(cins, couts, csems):
        for cp in sends(cins, couts, csems) + own(cins, couts, csems):
            cp.start()

    def mid(cins, couts, csems):
        for cp in sends(cins, couts, csems):
            cp.wait_recv()
        for cp in relays(couts, csems) + forwards(couts, csems, lambda c: c, (0, 1)):
            cp.start()

    def finish(cins, couts, csems):
        for cp in relays(couts, csems):
            cp.wait_recv()
        fw_diag = forwards(couts, csems, lambda c: c, (2,))
        for f in fw_diag:
            f.start()
        for f in forwards(couts, csems, lambda c: 1 - c, (0, 1, 2)):
            f.wait_recv()
        for cp in (sends(cins, couts, csems) + relays(couts, csems)
                   + forwards(couts, csems, lambda c: c, (0, 1)) + fw_diag):
            cp.wait_send()
        for cp in own(cins, couts, csems):
            cp.wait()

    return _Comm([locs[a] for a in arrs], [SDS(full_shapes[a], BF16) for a in arrs],
                 _dma_sems(2 * n, 2 * n, 2 * n, 2 * n, 3 * n, 3 * n, 2 * n, 2 * n), start, finish,
                 CHIPS + (SIBLING,), mid=mid)


def _halves_comm(arrs, gbs):
    n = len(arrs)

    def copies(cins, couts, csems):
        x, y, c = _coords()
        return [_remote(_piece(a, cins[q], k, 1 - c), couts[q].at[k], csems[0].at[4 * q + k], csems[1].at[4 * q + k],
                        (x, y, 1 - c)) for q, a in enumerate(arrs) for k in range(4)]

    return _symmetric([gbs[a] for a in arrs], [SDS((4,) + _piece_shape(a, gbs[a].shape), BF16) for a in arrs],
                      _dma_sems(4 * n, 4 * n), copies, [SIBLING])


def _chips_comm(arrs, ps, part=None, into=None):
    n = len(arrs)

    def copies(cins, couts, csems):
        x, y, c = _coords()
        return [_remote(_rows_part(a, cins[q].at[2 * px + py], part), _rows_part(a, couts[q].at[i], part),
                        csems[0].at[3 * q + i], csems[1].at[3 * q + i], (px, py, c))
                for q, a in enumerate(arrs) for i, (px, py) in enumerate(_peer_chips(x, y))]

    ins = [ps[a] for a in arrs] + ([into[a] for a in arrs] if into else [])
    return _symmetric(ins, [SDS((3,) + ps[a].shape[1:], BF16) for a in arrs], _dma_sems(3 * n, 3 * n), copies, CHIPS,
                      aliases={n + q: q for q in range(n)} if into else None)


def _result_comm(arrs, gs):
    n = len(arrs)

    def copies(cins, couts, csems):
        x, y, c = _coords()
        return [_remote(_half(a, cins[q], c), _half(a, couts[q], c), csems[0].at[q], csems[1].at[q], (x, y, 1 - c))
                for q, a in enumerate(arrs)]

    return _symmetric([gs[a] for a in arrs], [SDS(gs[a].shape, F32) for a in arrs], _dma_sems(n, n), copies,
                      [SIBLING], aliases={q: q for q in range(n)})


def _add_halves(arrs, gbs, lands, c_arr, name):
    n = len(arrs)

    def body(c_ref, *refs):
        del c_ref
        for q in range(n):
            refs[2 * n + q][...] = (refs[q][...].astype(F32) + refs[n + q][...].astype(F32)).astype(BF16)

    g_specs, l_specs, o_specs, blocks = [], [], [], 0
    for a in arrs:
        bs, imap = _piece_block(a, gbs[a].shape)
        ps = _piece_shape(a, gbs[a].shape)
        g_specs.append(pl.BlockSpec(bs, lambda k, c_ref, imap=imap: imap(k, c_ref[0])))
        nd = len(ps)
        l_specs.append(pl.BlockSpec((None,) + ps, lambda k, c_ref, nd=nd: (k,) + (0,) * nd))
        o_specs.append(pl.BlockSpec((None,) + ps, lambda k, c_ref, nd=nd: (k,) + (0,) * nd))
        blocks += 3 * _nbytes(ps, BF16)
    return list(pl.pallas_call(
        body, name=name,
        grid_spec=pltpu.PrefetchScalarGridSpec(
            num_scalar_prefetch=1, grid=(4,), in_specs=g_specs + l_specs, out_specs=o_specs),
        out_shape=[SDS((4,) + _piece_shape(a, gbs[a].shape), BF16) for a in arrs],
        compiler_params=_params(("parallel",), blocks, blocks),
    )(c_arr, *[gbs[a] for a in arrs], *lands))


def _sum_chips(a, p, land, shard_shape, jc_arr, name):
    ps = land.shape[1:]
    ax = _rows_axis(a)
    rows = ps[ax]
    nsub = 2 if rows % (2 * SUBLANES_BF16) == 0 else 1
    bs = tuple(r // nsub if q == ax else r for q, r in enumerate(ps))
    nd = len(ps)

    def at_rows(v):
        return tuple(v if q == ax else 0 for q in range(nd))

    def body(jc_ref, p_ref, l_ref, o_ref):
        del jc_ref
        acc = p_ref[...].astype(F32) + l_ref[0].astype(F32)
        acc = acc + l_ref[1].astype(F32)
        o_ref[...] = acc + l_ref[2].astype(F32)

    blocks = 4 * _nbytes(bs, BF16) + _nbytes(bs, F32)
    return pl.pallas_call(
        body, name=name,
        grid_spec=pltpu.PrefetchScalarGridSpec(
            num_scalar_prefetch=1, grid=(nsub,),
            in_specs=[pl.BlockSpec((None,) + bs, lambda s, jc: (jc[0],) + at_rows(s)),
                      pl.BlockSpec((3,) + bs, lambda s, jc: (0,) + at_rows(s))],
            out_specs=pl.BlockSpec(bs, lambda s, jc: at_rows(jc[1] * nsub + s))),
        out_shape=SDS(shard_shape, F32),
        compiler_params=_params(("parallel",), blocks, 2 * _nbytes(bs, F32)),
    )(jc_arr, p, land)


def _small_comm(v):
    rows = v.shape[0]

    def copies(cins, couts, csems):
        x, y, c = _coords()
        me = 4 * x + 2 * y + c
        out = [pltpu.make_async_copy(cins[0], couts[0].at[me], csems[0].at[0])]
        for dlt in range(1, 8):
            px = 1 - x if (dlt >> 2) & 1 else x
            py = 1 - y if (dlt >> 1) & 1 else y
            pc = 1 - c if dlt & 1 else c
            out.append(_remote(cins[0], couts[0].at[me], csems[1].at[dlt - 1], csems[2].at[dlt - 1], (px, py, pc)))
        return out

    return _symmetric([v], [SDS((8, rows, LANES), F32)], _dma_sems(1, 7, 7), copies, EVERYONE)


def _sum8(slots, name):
    def body(s_ref, o_ref):
        acc = s_ref[0]
        for i in range(1, 8):
            acc = acc + s_ref[i]
        o_ref[...] = acc

    return pl.pallas_call(
        body, name=name,
        in_specs=[pl.BlockSpec(memory_space=pltpu.VMEM)], out_specs=pl.BlockSpec(memory_space=pltpu.VMEM),
        out_shape=SDS(slots.shape[1:], F32),
    )(slots)


def _adamw(w, g, m, v, name, g_plane=None):
    rows, cols = w.shape
    tr = _tile(rows, max(SUBLANES_F32, (256 * 1024 // cols) // SUBLANES_F32 * SUBLANES_F32), SUBLANES_F32)

    def body(w_ref, g_ref, m_ref, v_ref, go_ref, d_ref, mo_ref, vo_ref):
        gr = g_ref[...]
        mn = ADAM_B1 * m_ref[...] + (1.0 - ADAM_B1) * gr
        vn = ADAM_B2 * v_ref[...] + (1.0 - ADAM_B2) * (gr * gr)
        m_hat = mn / (1.0 - ADAM_B1 ** ADAM_STEP)
        v_hat = vn / (1.0 - ADAM_B2 ** ADAM_STEP)
        d_ref[...] = -ADAM_LR * (m_hat / (jnp.sqrt(v_hat) + ADAM_EPS) + ADAM_WD * w_ref[...])
        go_ref[...] = gr
        mo_ref[...] = mn
        vo_ref[...] = vn

    spec = pl.BlockSpec((tr, cols), lambda i: (i, 0))
    g_spec = spec if g_plane is None else pl.BlockSpec((None, tr, cols), lambda i: (g_plane, i, 0))
    return pl.pallas_call(
        body, name=name, grid=(rows // tr,),
        in_specs=[spec, g_spec, spec, spec], out_specs=[spec, spec, spec, spec],
        out_shape=[SDS((rows, cols), F32)] * 4,
        compiler_params=_params(("parallel",), 8 * _nbytes((tr, cols), F32), 4 * _nbytes((tr, cols), F32)),
    )(w, g, m, v)


def _pack(parts):
    rows = []
    for p in parts:
        r = p.reshape(-1, LANES)
        pad = (-r.shape[0]) % SUBLANES_F32
        if pad:
            r = jnp.pad(r, ((0, pad), (0, 0)))
        rows.append(r)
    return jnp.concatenate(rows, axis=0)


def _unpack(packed, shapes):
    out, at = [], 0
    for s in shapes:
        n = 1
        for q in s:
            n *= q
        r = n // LANES
        out.append(packed[at:at + r].reshape(s))
        at += r + (-r) % SUBLANES_F32
    return out


def kernel(x, norm_mix, w_in, pool_w, pool_scale, w_pool_proj, conv_w, w_conv_out, w_o, norm_ffn, w_up, ffn_conv_w, ffn_conv_b, w_down, norm_final, loss_target, m_norm_mix, m_w_in, m_pool_w, m_pool_scale, m_w_pool_proj, m_conv_w, m_w_conv_out, m_w_o, m_norm_ffn, m_w_up, m_ffn_conv_w, m_ffn_conv_b, m_w_down, m_norm_final, v_norm_mix, v_w_in, v_pool_w, v_pool_scale, v_w_pool_proj, v_conv_w, v_w_conv_out, v_w_o, v_norm_ffn, v_w_up, v_ffn_conv_w, v_ffn_conv_b, v_w_down, v_norm_final):
    nseq, seq, d = x.shape
    t = nseq * seq
    f = w_down.shape[1] * 4
    c = d // N_GROUPS
    xy = lax.axis_index("x") * 2 + lax.axis_index("y")
    c_arr = lax.axis_index("c").astype(jnp.int32).reshape(1)
    jc_arr = jnp.stack([xy, lax.axis_index("c")]).astype(jnp.int32)
    nsh = 4
    zero = jnp.zeros((), jnp.int32)

    locs = [w_in[0].astype(BF16),
            jnp.stack([w_pool_proj[0], w_conv_out[0], w_o[0]]).astype(BF16),
            w_up[0].astype(BF16), w_down[0].astype(BF16), pool_w[0].astype(BF16)]
    full_shapes = [(nsh, d, N_SPLITS * d // nsh), (3, d, d), (nsh, d, 2 * f // nsh), (f, d), (N_GROUPS, c, c)]

    cw_pad = lax.dynamic_update_slice(jnp.zeros((3, d), F32), conv_w[0], (zero, xy * (d // 4)))
    fw_pad = lax.dynamic_update_slice(jnp.zeros((3, 2 * f), F32), ffn_conv_w[0], (zero, xy * (f // 2)))
    small_w = _pack([cw_pad, fw_pad]) * 0.5

    x2d = x.reshape(t, d)
    tgt = loss_target.reshape(t, d)
    ax, ay = lax.axis_index("x"), lax.axis_index("y")
    order = jnp.stack([xy, 2 * (1 - ax) + ay, 2 * ax + 1 - ay, 2 * (1 - ax) + 1 - ay]).astype(jnp.int32)
    (z, h1, w_in_f), (pool_w_f, w3_f, slots_w) = _fwd_in(
        x2d, norm_mix, locs[0], order,
        _merge([_gather_comm([4], locs, full_shapes), _gather_comm([1], locs, full_shapes, part=(0, 1, 2)),
                _small_comm(small_w)]))
    conv_w_f, ffn_cw_f = _unpack(_sum8(slots_w, "sum8_weights"), [(3, d), (3, 2 * f)])
    ffn_cw_p = ffn_cw_f.reshape(3, 2, f).transpose(1, 0, 2)
    ffn_cb_p = ffn_conv_b.reshape(2, 1, f)
    (lhs3,), (w3_f,) = _mixer_mid_fwd(z, pool_w_f, pool_scale, conv_w_f, nseq,
                                      _gather_comm([1], locs, full_shapes, part=(1, 2, 2), into={1: w3_f}))
    (lhs3, ypc, x1, h2), (w_up_f,) = _mixer_out(lhs3, z, x2d, w3_f, norm_ffn,
                                                _ring_gather_comm([2], locs, full_shapes))
    (u0,), (w_down_f,) = _ffn_up(h2, w_up_f, f, _gather_comm([3], locs, full_shapes))
    act, ua = _ffn_mid_fwd(u0, ffn_cw_p, ffn_cb_p, nseq)
    dx2, dx2b, loss11, g_norm_final = _ffn_down_loss(act, w_down_f, x1, tgt, norm_final.reshape(1, d))

    gbs, lands, ps, lands2, rs = {}, {}, {}, {}, {}
    tn_up = _tile(2 * f // nsh, 1408, LANES)
    npp = f // tn_up

    def add(arrs, name):
        for a, p in zip(arrs, _add_halves(arrs, gbs, [lands[a] for a in arrs], c_arr, name)):
            ps[a] = p

    def summed(a):
        rs[a] = _sum_chips(a, ps[a], lands2[a], _shard_shape(a, full_shapes[a]), jc_arr, "sum_chips_%d" % a)

    (gbs[3],), _ = _wgrad(act, dx2b, "wgrad_down", tr=tn_up, tn=d)
    (da,), (lands[3],) = _ffn_bwd_da(dx2b, w_down_f, _halves_comm([3], gbs))
    add([3], "add_halves_down")
    (du0, g_ffn_cw_p, g_ffn_cb_p), (lands2[3],) = _ffn_mid_bwd(da, u0, ua, ffn_cw_p, nseq, _chips_comm([3], ps))
    summed(3)
    (gbs[2],), (rs[3],) = _wgrad(h2, du0, "wgrad_up", tr=d, tn=tn_up, b_plane_of=lambda n: (n // npp, n % npp),
                                 out_shards=nsh, comm=_result_comm([3], rs))
    (dx1, rhs3, g_norm_ffn), (lands[2],) = _ffn_bwd_dx1(du0, w_up_f, x1, dx2, norm_ffn, 3, _halves_comm([2], gbs))
    add([2], "add_halves_up")
    (rhs3, dz, dpq), (lands2[2],) = _mixer_bwd(rhs3, z, ypc, w3_f, _chips_comm([2], ps, part=(0, 1, 2)))
    (gbs[1],), (lands2[2],) = _wgrad3(lhs3, rhs3, _chips_comm([2], ps, part=(1, 2, 2), into=lands2))
    summed(2)
    (dz, g_conv_w), (lands[1], rs[2]) = _conv_bwd(dz, dpq, z, conv_w_f, nseq,
                                                  _merge([_halves_comm([1], gbs), _result_comm([2], rs)]))
    add([1], "add_halves_sq3")
    (dz, g_pool_w, g_pool_scale), _ = _pool_bwd_call(dz, dpq, z, pool_w_f, pool_scale, nseq)
    gbs[4] = g_pool_w.astype(BF16)
    (gbs[0],), (lands2[1],) = _wgrad_in(h1, dz, nsh, _chips_comm([1], ps))
    summed(1)
    lands[0], lands[4] = _run_comm(_halves_comm([0, 4], gbs), "exchange_halves_in")
    add([0, 4], "add_halves_in")
    g_ffn_cw = g_ffn_cw_p.transpose(1, 0, 2).reshape(3, 2 * f)
    small_a = _pack([g_pool_scale, g_norm_ffn, g_ffn_cb_p.reshape(1, 2 * f), g_norm_final.reshape(d), g_conv_w,
                     g_ffn_cw, jnp.pad(loss11, ((0, SUBLANES_F32 - 1), (0, LANES - 1)))])
    (grad_x, g_norm_mix), (lands2[0], lands2[4], rs[1], slots_a) = _mixer_bwd_dx(
        dz, w_in_f, x2d, dx1, norm_mix,
        _merge([_chips_comm([0, 4], ps), _result_comm([1], rs), _small_comm(small_a)]))
    summed(0)
    summed(4)
    rs[0], rs[4], slots_b = _run_comm(_merge([_result_comm([0, 4], rs), _small_comm(_pack([g_norm_mix]))]),
                                      "exchange_result_in")
    shapes_a = [(1, d), (1, d), (1, 2 * f), (d,), (3, d), (3, 2 * f), (SUBLANES_F32, LANES)]
    gs_pool_scale, gs_norm_ffn, gs_ffn_cb, gs_norm_final, gs_conv_w, gs_ffn_cw, loss_blk = _unpack(
        _sum8(slots_a, "sum8_grads"), shapes_a)
    (gs_norm_mix,) = _unpack(_sum8(slots_b, "sum8_norm_mix"), [(1, d)])
    gs_conv_w = lax.dynamic_slice(gs_conv_w, (zero, xy * (d // 4)), (3, d // 4))
    gs_ffn_cw = lax.dynamic_slice(gs_ffn_cw, (zero, xy * (f // 2)), (3, f // 2))

    def upd(w, g, m, v, name, g_plane=None):
        shape = w.shape
        rows = 1
        for q in shape[:-1]:
            rows *= q
        g2 = g if g_plane is not None else g.reshape(rows, shape[-1])
        outs = _adamw(w.reshape(rows, shape[-1]), g2, m.reshape(rows, shape[-1]), v.reshape(rows, shape[-1]),
                      name, g_plane)
        return [o.reshape(shape) for o in outs]

    res = {
        "w_in": upd(w_in, rs[0], m_w_in, v_w_in, "adamw_w_in"),
        "pool_w": upd(pool_w, rs[4], m_pool_w, v_pool_w, "adamw_pool_w"),
        "w_pool_proj": upd(w_pool_proj, rs[1], m_w_pool_proj, v_w_pool_proj, "adamw_w_pool_proj", 0),
        "w_conv_out": upd(w_conv_out, rs[1], m_w_conv_out, v_w_conv_out, "adamw_w_conv_out", 1),
        "w_o": upd(w_o, rs[1], m_w_o, v_w_o, "adamw_w_o", 2),
        "w_up": upd(w_up, rs[2], m_w_up, v_w_up, "adamw_w_up"),
        "w_down": upd(w_down, rs[3], m_w_down, v_w_down, "adamw_w_down"),
    }

    small_names = ["norm_mix", "pool_scale", "norm_ffn", "ffn_conv_b", "norm_final", "conv_w", "ffn_conv_w"]
    small_ws = [norm_mix, pool_scale, norm_ffn, ffn_conv_b, norm_final, conv_w, ffn_conv_w]
    small_ms = [m_norm_mix, m_pool_scale, m_norm_ffn, m_ffn_conv_b, m_norm_final, m_conv_w, m_ffn_conv_w]
    small_vs = [v_norm_mix, v_pool_scale, v_norm_ffn, v_ffn_conv_b, v_norm_final, v_conv_w, v_ffn_conv_w]
    small_gs = [gs_norm_mix, gs_pool_scale, gs_norm_ffn, gs_ffn_cb, gs_norm_final, gs_conv_w, gs_ffn_cw]
    _, sd, sm, sv = _adamw(_pack(small_ws), _pack(small_gs), _pack(small_ms), _pack(small_vs), "adamw_small")
    shapes = [w.shape for w in small_ws]
    sd, sm, sv = _unpack(sd, shapes), _unpack(sm, shapes), _unpack(sv, shapes)
    for i, nm in enumerate(small_names):
        res[nm] = [small_gs[i].reshape(shapes[i]), sd[i], sm[i], sv[i]]

    order = ["norm_mix", "w_in", "pool_w", "pool_scale", "w_pool_proj", "conv_w", "w_conv_out", "w_o", "norm_ffn",
             "w_up", "ffn_conv_w", "ffn_conv_b", "w_down", "norm_final"]
    return (loss_blk[0, 0], grad_x.reshape(x.shape), *[res[n][0] for n in order], *[res[n][1] for n in order],
            *[res[n][2] for n in order], *[res[n][3] for n in order])
```

```python
import math

import jax
import jax.numpy as jnp
from jax import lax
from jax.experimental import pallas as pl
from jax.experimental.pallas import tpu as pltpu

F32 = jnp.float32
BF16 = jnp.bfloat16
SDS = jax.ShapeDtypeStruct
MESH = pl.DeviceIdType.MESH

RMS_EPS = 1e-6
POOL_WINDOWS = (2, 4, 8, 16)
N_GROUPS = len(POOL_WINDOWS)
N_SPLITS = 6

ADAM_LR = 0.001
ADAM_B1 = 0.9
ADAM_B2 = 0.999
ADAM_EPS = 1e-08
ADAM_WD = 0.01
ADAM_STEP = 10

LANES = 128
SUBLANES_F32 = 8
SUBLANES_BF16 = 16
VMEM_BYTES = 64 * 1024 * 1024
VMEM_CAP = VMEM_BYTES - 8 * 1024 * 1024
VMEM_FLOOR = 16 * 1024 * 1024

ANY = pl.BlockSpec(memory_space=pl.ANY)


def _tile(dim, pref, align):
    if dim <= pref:
        return dim
    t = (pref // align) * align
    while t >= align:
        if dim % t == 0:
            return t
        t -= align
    return dim


def _nbytes(shape, dtype):
    n = 1
    for s in shape:
        n *= s
    return n * jnp.dtype(dtype).itemsize


def _params(sem, block_bytes, temp_bytes=0, collective_id=None):
    need = 2 * block_bytes + temp_bytes + 4 * 1024 * 1024
    return pltpu.CompilerParams(dimension_semantics=sem, collective_id=collective_id,
                                vmem_limit_bytes=int(min(max(need, VMEM_FLOOR), VMEM_CAP)))


SIBLING = (0, 0, 1)
CHIPS = ((1, 0, 0), (0, 1, 0), (1, 1, 0))
EVERYONE = tuple((a, b, c) for a in range(2) for b in range(2) for c in range(2) if a + b + c)
PEER_SETS = (frozenset([SIBLING]), frozenset(CHIPS), frozenset(CHIPS + (SIBLING,)), frozenset(EVERYONE))
MID_AT = 0.75


def _collective_id(peers):
    return PEER_SETS.index(frozenset(peers))


def _handshake(peers):
    x, y, c = lax.axis_index("x"), lax.axis_index("y"), lax.axis_index("c")
    bar = pltpu.get_barrier_semaphore()
    for fx, fy, fc in sorted(peers):
        dev = (1 - x if fx else x, 1 - y if fy else y, 1 - c if fc else c)
        pl.semaphore_signal(bar, inc=1, device_id=dev, device_id_type=MESH)
    pl.semaphore_wait(bar, len(peers))


class _Comm:
    def __init__(self, ins, out_shapes, sems, start, finish, peers, aliases=None, mid=None):
        self.ins = list(ins)
        self.out_shapes = list(out_shapes)
        self.sems = list(sems)
        self.start = start
        self.finish = finish
        self.mid = mid
        self.peers = frozenset(peers)
        self.aliases = dict(aliases or {})


def _pcall(body, *, name, grid, in_specs, out_specs, out_shape, sem, blocks, temps=0, scratch_shapes=(),
           input_output_aliases=None, comm=None):
    in_specs = list(in_specs)
    out_specs = list(out_specs)
    out_shape = list(out_shape)
    scratch_shapes = list(scratch_shapes)
    aliases = dict(input_output_aliases or {})
    n_in, n_out, n_scr = len(in_specs), len(out_shape), len(scratch_shapes)
    if comm is None:
        call = pl.pallas_call(
            body, name=name, grid=grid, in_specs=in_specs, out_specs=out_specs, out_shape=out_shape,
            scratch_shapes=scratch_shapes, input_output_aliases=aliases,
            compiler_params=_params(sem, blocks, temps))
        return lambda *args: (list(call(*args)), [])

    nci, nco = len(comm.ins), len(comm.out_shapes)
    n_steps = 1
    for g in grid:
        n_steps *= g

    def hosted(*refs):
        ins = refs[:n_in]
        cins = refs[n_in:n_in + nci]
        outs = refs[n_in + nci:n_in + nci + n_out]
        couts = refs[n_in + nci + n_out:n_in + nci + n_out + nco]
        scr = refs[n_in + nci + n_out + nco:n_in + nci + n_out + nco + n_scr]
        csems = refs[n_in + nci + n_out + nco + n_scr:]
        first = None
        last = None
        step = 0
        for q, g in enumerate(grid):
            pid = pl.program_id(q)
            first = (pid == 0) if first is None else first & (pid == 0)
            last = (pid == g - 1) if last is None else last & (pid == g - 1)
            step = step * g + pid

        @pl.when(first)
        def _():
            _handshake(comm.peers)
            comm.start(cins, couts, csems)

        if comm.mid is not None:
            @pl.when(step == int(MID_AT * n_steps))
            def _():
                comm.mid(cins, couts, csems)

        body(*ins, *outs, *scr)

        @pl.when(last)
        def _():
            comm.finish(cins, couts, csems)

    for i, o in comm.aliases.items():
        aliases[n_in + i] = n_out + o
    call = pl.pallas_call(
        hosted, name=name, grid=grid, in_specs=in_specs + [ANY] * nci, out_specs=out_specs + [ANY] * nco,
        out_shape=out_shape + comm.out_shapes, scratch_shapes=scratch_shapes + comm.sems,
        input_output_aliases=aliases,
        compiler_params=_params(("arbitrary",) * len(grid), blocks, temps, _collective_id(comm.peers)))

    def run(*args):
        res = call(*args, *comm.ins)
        return list(res[:n_out]), list(res[n_out:])

    return run


def _run_comm(comm, name):
    def body(*refs):
        nci, nco = len(comm.ins), len(comm.out_shapes)
        cins, couts, csems = refs[:nci], refs[nci:nci + nco], refs[nci + nco:]
        _handshake(comm.peers)
        comm.start(cins, couts, csems)
        if comm.mid is not None:
            comm.mid(cins, couts, csems)
        comm.finish(cins, couts, csems)

    return list(pl.pallas_call(
        body, name=name, in_specs=[ANY] * len(comm.ins), out_specs=[ANY] * len(comm.out_shapes),
        out_shape=comm.out_shapes, scratch_shapes=comm.sems, input_output_aliases=comm.aliases,
        compiler_params=pltpu.CompilerParams(collective_id=_collective_id(comm.peers)),
    )(*comm.ins))


def _dot(a, b):
    return jnp.dot(a, b, preferred_element_type=F32)


def _dot_tb(a, b):
    return lax.dot_general(a, b, (((1,), (1,)), ((), ())), preferred_element_type=F32)


def _dot_ta(a, b):
    return lax.dot_general(a, b, (((0,), (0,)), ((), ())), preferred_element_type=F32)


def _rms_fwd(x):
    inv = lax.rsqrt(jnp.mean(x * x, axis=-1, keepdims=True) + RMS_EPS)
    return x * inv, inv


def _rms_bwd(dy, xhat, inv, g):
    gd = dy * g
    return inv * (gd - xhat * jnp.mean(gd * xhat, axis=-1, keepdims=True))


def _sigmoid(x):
    return 1.0 / (1.0 + jnp.exp(-x))


def _shift_down(x, k, row):
    return jnp.where(row >= k, pltpu.roll(x, k, 0), 0.0)


def _shift_up(x, k, row):
    s = x.shape[0]
    return jnp.where(row < s - k, pltpu.roll(x, s - k, 0), 0.0)


def _pool_fwd(u, win, row):
    s = u
    k = 1
    while k < win:
        s = s + _shift_down(s, k, row)
        k *= 2
    cnt = jnp.minimum(row + 1, win).astype(F32)
    return s / cnt - u


def _pool_bwd(dp, win, row):
    cnt = jnp.minimum(row + 1, win).astype(F32)
    s = dp / cnt
    k = 1
    while k < win:
        s = s + _shift_up(s, k, row)
        k *= 2
    return s - dp


def _acc_over(k, nk, part, acc, o_ref):
    @pl.when(k == 0)
    def _():
        acc[...] = part

    @pl.when(k > 0)
    def _():
        acc[...] += part

    @pl.when(k == nk - 1)
    def _():
        o_ref[...] = acc[...].astype(o_ref.dtype)


def _fwd_in(x, g, w_loc, order, comm):
    t, d = x.shape
    ws = w_loc.shape[1]
    nsh = order.shape[0]
    assert nsh == 4, "the shard walk below is written for the 2 x 2 chips of the mesh"
    tm = _tile(t, 1024, SUBLANES_BF16)
    ni = t // tm
    nci, nco = len(comm.ins), len(comm.out_shapes)
    all_peers = comm.peers | frozenset(CHIPS + (SIBLING,))

    def body(order_ref, x_ref, g_ref, loc_ref, *rest):
        del order_ref
        cins = rest[:nci]
        z_ref, h_ref, full_ref = rest[nci:nci + 3]
        couts = rest[nci + 3:nci + 3 + nco]
        (hs, wbuf, wsem, own_s, own_r, snd_s, snd_r, fwd_s, fwd_r, rly_s, rly_r) = rest[nci + 3 + nco:nci + 14 + nco]
        csems = rest[nci + 14 + nco:]
        j = pl.program_id(0)
        i = pl.program_id(1)
        x_, y_, c_ = _coords()
        own = 2 * x_ + y_
        sib = (x_, y_, 1 - c_)
        peers = _peer_chips(x_, y_)

        def sends():
            return [_remote(_half(0, loc_ref, c_), _piece(0, full_ref, own, c_), snd_s.at[p], snd_r.at[p], (px, py, c_))
                    for p, (px, py) in enumerate(peers[:2])]

        def relays():
            out = []
            for q, (src_p, dst_p) in enumerate(((0, 1), (1, 0))):
                sx, sy = peers[src_p]
                part = _rows_part(0, _piece(0, full_ref, 2 * sx + sy, c_), (q, q + 1, 2))
                out.append(_remote(part, part, rly_s.at[q], rly_r.at[q], (*peers[dst_p], c_)))
            return out

        def owns():
            return [_remote(_half(0, loc_ref, h), _piece(0, full_ref, own, h), own_s.at[h], own_r.at[h], sib)
                    for h in range(2)]

        def forward(p, half):
            px, py = peers[p]
            landed = _piece(0, full_ref, 2 * px + py, half)
            return _remote(landed, landed, fwd_s.at[p], fwd_r.at[p], sib)

        def load(src, slot):
            return pltpu.make_async_copy(src, wbuf.at[slot], wsem.at[slot])

        @pl.when((j == 0) & (i == 0))
        def _():
            _handshake(all_peers)
            for cp in sends() + owns():
                cp.start()
            load(loc_ref, 0).start()
            comm.start(cins, couts, csems)

        @pl.when(j == 0)
        def _():
            xh, _ = _rms_fwd(x_ref[...])
            h = (xh * g_ref[...]).astype(BF16)
            hs[pl.ds(pl.multiple_of(i * tm, tm), tm), :] = h
            h_ref[...] = h

        slot = j % 2

        @pl.when(i == 0)
        def _():
            load(loc_ref, slot).wait()

        z_ref[...] = _dot(hs[pl.ds(pl.multiple_of(i * tm, tm), tm), :], wbuf[slot]).astype(BF16)

        for p in range(nsh - 1):
            @pl.when((i == ni - 1) & (j == p))
            def _(p=p):
                px, py = peers[p]
                if p == 0:
                    for cp in sends():
                        cp.wait_recv()
                    for cp in relays():
                        cp.start()
                if p == 2:
                    for cp in relays():
                        cp.wait_recv()
                forward(p, c_).start()
                forward(p, 1 - c_).wait_recv()
                load(full_ref.at[2 * px + py], 1 - slot).start()

        @pl.when((j == nsh - 1) & (i == ni - 1))
        def _():
            for cp in sends() + relays() + [forward(p, c_) for p in range(nsh - 1)]:
                cp.wait_send()
            for cp in owns():
                cp.wait()
            comm.finish(cins, couts, csems)

    last = ni - 1
    blocks = _nbytes((tm, d), F32) + _nbytes((tm, ws), BF16) + _nbytes((tm, d), BF16)
    scratch = _nbytes((t, d), BF16) + 2 * _nbytes((d, ws), BF16)
    res = pl.pallas_call(
        body, name="fwd_in",
        grid_spec=pltpu.PrefetchScalarGridSpec(
            num_scalar_prefetch=1, grid=(nsh, ni),
            in_specs=[pl.BlockSpec((tm, d), lambda j, i, o: (jnp.where(j == 0, i, last), 0)),
                      pl.BlockSpec((1, d), lambda j, i, o: (0, 0)), ANY] + [ANY] * nci,
            out_specs=[pl.BlockSpec((tm, ws), lambda j, i, o: (i, o[j])),
                       pl.BlockSpec((tm, d), lambda j, i, o: (jnp.where(j == 0, i, last), 0)), ANY] + [ANY] * nco,
            scratch_shapes=[pltpu.VMEM((t, d), BF16), pltpu.VMEM((2, d, ws), BF16)]
            + _dma_sems(2, 2, 2, 2, 2, nsh - 1, nsh - 1, 2, 2) + comm.sems),
        out_shape=[SDS((t, nsh * ws), BF16), SDS((t, d), BF16), SDS((nsh, d, ws), BF16)] + comm.out_shapes,
        input_output_aliases={4 + i: 3 + o for i, o in comm.aliases.items()},
        compiler_params=_params(("arbitrary", "arbitrary"), blocks, scratch + 3 * _nbytes((tm, d), F32),
                                _collective_id(all_peers)),
    )(order, x, g, w_loc, *comm.ins)
    return list(res[:3]), list(res[3:])


def _mixer_mid_fwd(z, pool_w, pool_scale, conv_w, nseq, comm=None):
    t = z.shape[0]
    d = pool_scale.shape[1]
    s = t // nseq
    c = d // N_GROUPS

    def body(zp, zb, zc, zv, pw, ps, cw, o):
        j = pl.program_id(1)
        row = lax.broadcasted_iota(jnp.int32, (s, c), 0)
        for gi, win in enumerate(POOL_WINDOWS):
            @pl.when(j == gi)
            def _(win=win):
                pooled = _pool_fwd(zp[...].astype(F32), win, row)
                o[0] = (_dot(pooled.astype(BF16), pw[...]) * ps[...]).astype(BF16)

        cv = zc[...].astype(F32) * zv[...].astype(F32)
        cc = (cw[pl.ds(2, 1), :] * cv + cw[pl.ds(1, 1), :] * _shift_down(cv, 1, row)
              + cw[pl.ds(0, 1), :] * _shift_down(cv, 2, row))
        o[1] = (zb[...].astype(F32) * cc).astype(BF16)

    blocks = 4 * _nbytes((s, c), BF16) + _nbytes((c, c), BF16) + _nbytes((2, s, c), BF16)
    return _pcall(
        body, name="mixer_mid_fwd", grid=(nseq, N_GROUPS),
        in_specs=[pl.BlockSpec((s, c), lambda b, j: (b, j)),
                  pl.BlockSpec((s, c), lambda b, j: (b, N_GROUPS + j)),
                  pl.BlockSpec((s, c), lambda b, j: (b, 2 * N_GROUPS + j)),
                  pl.BlockSpec((s, c), lambda b, j: (b, 3 * N_GROUPS + j)),
                  pl.BlockSpec((None, c, c), lambda b, j: (j, 0, 0)),
                  pl.BlockSpec((1, c), lambda b, j: (0, j)),
                  pl.BlockSpec((3, c), lambda b, j: (0, j))],
        out_specs=[pl.BlockSpec((2, s, c), lambda b, j: (0, b, j))],
        out_shape=[SDS((3, t, d), BF16)],
        sem=("parallel", "parallel"), blocks=blocks, temps=8 * _nbytes((s, c), F32), comm=comm,
    )(z, z, z, z, pool_w, pool_scale, conv_w)


def _mixer_out(lhs3, z, x, w3, g_ffn, comm=None):
    t, d = x.shape
    tm = _tile(t, 256, SUBLANES_BF16)

    def body(pq, zgp, zgc, x_ref, w_ref, g_ref, mrg, ypc, x1o, h2o):
        yp = _dot(pq[0], w_ref[0])
        yc = _dot(pq[1], w_ref[1])
        m = _sigmoid(zgp[...].astype(F32)) * yp + _sigmoid(zgc[...].astype(F32)) * yc
        mb = m.astype(BF16)
        x1 = x_ref[...] + _dot(mb, w_ref[2])
        ypc[0] = yp.astype(BF16)
        ypc[1] = yc.astype(BF16)
        mrg[...] = mb
        x1o[...] = x1
        xh, _ = _rms_fwd(x1)
        h2o[...] = (xh * g_ref[...]).astype(BF16)

    blocks = (_nbytes((2, tm, d), BF16) * 2 + _nbytes((tm, d), BF16) * 4 + _nbytes((tm, d), F32) * 2
              + _nbytes((3, d, d), BF16))
    return _pcall(
        body, name="mixer_out", grid=(t // tm,),
        in_specs=[pl.BlockSpec((2, tm, d), lambda i: (0, i, 0)),
                  pl.BlockSpec((tm, d), lambda i: (i, 4)),
                  pl.BlockSpec((tm, d), lambda i: (i, 5)),
                  pl.BlockSpec((tm, d), lambda i: (i, 0)),
                  pl.BlockSpec((3, d, d), lambda i: (0, 0, 0)),
                  pl.BlockSpec((1, d), lambda i: (0, 0))],
        out_specs=[pl.BlockSpec((None, tm, d), lambda i: (2, i, 0)),
                   pl.BlockSpec((2, tm, d), lambda i: (0, i, 0)),
                   pl.BlockSpec((tm, d), lambda i: (i, 0)),
                   pl.BlockSpec((tm, d), lambda i: (i, 0))],
        out_shape=[SDS(lhs3.shape, BF16), SDS((2, t, d), BF16), SDS((t, d), F32), SDS((t, d), BF16)],
        input_output_aliases={0: 0},
        sem=("parallel",), blocks=blocks, temps=8 * _nbytes((tm, d), F32), comm=comm,
    )(lhs3, z, z, x, w3, g_ffn)


def _ffn_up(h2, w_up, f, comm=None):
    t, d = h2.shape
    nsh, _, ws = w_up.shape
    tm = _tile(t, 512, SUBLANES_BF16)
    spp = f // ws

    def body(h_ref, w_ref, o_ref):
        h = h_ref[...]
        for k in range(nsh):
            o_ref[k // spp, :, (k % spp) * ws:(k % spp + 1) * ws] = _dot(h, w_ref[k]).astype(BF16)

    blocks = _nbytes((tm, d), BF16) + _nbytes((2, tm, f), BF16)
    return _pcall(
        body, name="ffn_up", grid=(t // tm,),
        in_specs=[pl.BlockSpec((tm, d), lambda i: (i, 0)),
                  pl.BlockSpec((nsh, d, ws), lambda i: (0, 0, 0), pipeline_mode=pl.Buffered(1))],
        out_specs=[pl.BlockSpec((2, tm, f), lambda i: (0, i, 0))],
        out_shape=[SDS((2, t, f), BF16)],
        sem=("parallel",), blocks=blocks, temps=_nbytes(w_up.shape, BF16) + 2 * _nbytes((tm, ws), F32), comm=comm,
    )(h2, w_up)


def _conv3_rows(u, u1, u2, w_ref, p):
    return w_ref[p, pl.ds(2, 1), :] * u + w_ref[p, pl.ds(1, 1), :] * u1 + w_ref[p, pl.ds(0, 1), :] * u2


WGRAD_TOKENS = 4096
CHUNK = 64
HALO = SUBLANES_F32


def _up1_up2(u, nxt):
    rows = u.shape[0]
    ext = jnp.concatenate([u, nxt], axis=0)
    n = rows + HALO
    return pltpu.roll(ext, n - 1, 0)[:rows], pltpu.roll(ext, n - 2, 0)[:rows]


def _fold8(x):
    return jnp.sum(x.reshape(x.shape[0] // SUBLANES_F32, SUBLANES_F32, x.shape[1]), axis=0)


def _ffn_mid_fwd(u0, cw, cb, nseq):
    _, t, f = u0.shape
    s = t // nseq
    c = _tile(f, 256, LANES)

    def body(u_ref, w_ref, b_ref, a_ref, uo_ref):
        row = lax.broadcasted_iota(jnp.int32, (s, c), 0)
        act = []
        for p in range(2):
            u = u_ref[p].astype(F32)
            act.append(_conv3_rows(u, _shift_down(u, 1, row), _shift_down(u, 2, row), w_ref, p) + b_ref[p])
            uo_ref[p] = act[p].astype(BF16)
        ug, uv = act
        a_ref[...] = (ug * _sigmoid(ug) * uv).astype(BF16)

    blocks = 2 * _nbytes((2, s, c), BF16) + _nbytes((s, c), BF16)
    outs, _ = _pcall(
        body, name="ffn_mid_fwd", grid=(f // c, nseq),
        in_specs=[pl.BlockSpec((2, s, c), lambda j, b: (0, b, j)),
                  pl.BlockSpec((2, 3, c), lambda j, b: (0, 0, j)),
                  pl.BlockSpec((2, 1, c), lambda j, b: (0, 0, j))],
        out_specs=[pl.BlockSpec((s, c), lambda j, b: (b, j)),
                   pl.BlockSpec((2, s, c), lambda j, b: (0, b, j))],
        out_shape=[SDS((t, f), BF16), SDS((2, t, f), BF16)],
        sem=("parallel", "parallel"), blocks=blocks, temps=8 * _nbytes((s, c), F32),
    )(u0, cw, cb)
    return outs


def _ffn_down_loss(a, w_down, x1, tgt, g_fin):
    t, f = a.shape
    d = x1.shape[1]
    tm = _tile(t, 256, SUBLANES_BF16)
    nsteps = t // tm

    def body(a_ref, w_ref, x1_ref, t_ref, g_ref, dx_ref, dxb_ref, loss_ref, gg_ref, lacc):
        i = pl.program_id(0)

        @pl.when(i == 0)
        def _():
            lacc[...] = jnp.zeros_like(lacc)
            gg_ref[...] = jnp.zeros_like(gg_ref)

        x2 = x1_ref[...] + _dot(a_ref[...], w_ref[...])
        xh, inv = _rms_fwd(x2)
        g = g_ref[...]
        e = xh * g - t_ref[...]
        lacc[...] += jnp.sum(e * e, axis=0, keepdims=True)
        dy = e * (1.0 / d)
        gg_ref[...] += jnp.sum(dy * xh, axis=0, keepdims=True)
        dx2 = _rms_bwd(dy, xh, inv, g)
        dx_ref[...] = dx2
        dxb_ref[...] = dx2.astype(BF16)

        @pl.when(i == nsteps - 1)
        def _():
            loss_ref[...] = jnp.sum(lacc[...], axis=1, keepdims=True) * (0.5 / d)

    blocks = (_nbytes((tm, f), BF16) + _nbytes((f, d), BF16) + 3 * _nbytes((tm, d), F32) + _nbytes((tm, d), BF16))
    outs, _ = _pcall(
        body, name="ffn_down_loss", grid=(nsteps,),
        in_specs=[pl.BlockSpec((tm, f), lambda i: (i, 0)), pl.BlockSpec((f, d), lambda i: (0, 0)),
                  pl.BlockSpec((tm, d), lambda i: (i, 0)), pl.BlockSpec((tm, d), lambda i: (i, 0)),
                  pl.BlockSpec((1, d), lambda i: (0, 0))],
        out_specs=[pl.BlockSpec((tm, d), lambda i: (i, 0)), pl.BlockSpec((tm, d), lambda i: (i, 0)),
                   pl.BlockSpec((1, 1), lambda i: (0, 0)), pl.BlockSpec((1, d), lambda i: (0, 0))],
        out_shape=[SDS((t, d), F32), SDS((t, d), BF16), SDS((1, 1), F32), SDS((1, d), F32)],
        scratch_shapes=[pltpu.VMEM((1, d), F32)],
        sem=("arbitrary",), blocks=blocks, temps=8 * _nbytes((tm, d), F32),
    )(a, w_down, x1, tgt, g_fin)
    return outs


def _ffn_bwd_da(dxb, w_down, comm=None):
    t, d = dxb.shape
    f = w_down.shape[0]
    tm = _tile(t, 512, SUBLANES_BF16)

    def body(x_ref, w_ref, o_ref):
        o_ref[...] = _dot_tb(x_ref[...], w_ref[...]).astype(BF16)

    blocks = _nbytes((tm, d), BF16) + _nbytes((tm, f), BF16)
    return _pcall(
        body, name="ffn_bwd_da", grid=(t // tm,),
        in_specs=[pl.BlockSpec((tm, d), lambda i: (i, 0)),
                  pl.BlockSpec((f, d), lambda i: (0, 0), pipeline_mode=pl.Buffered(1))],
        out_specs=[pl.BlockSpec((tm, f), lambda i: (i, 0))],
        out_shape=[SDS((t, f), BF16)],
        sem=("parallel",), blocks=blocks, temps=_nbytes((f, d), BF16) + _nbytes((tm, f), F32), comm=comm,
    )(dxb, w_down)


def _ffn_mid_bwd(da, u0, ua, cw, nseq, comm=None):
    _, t, f = u0.shape
    s = t // nseq
    c = _tile(f, 128, LANES)
    r = _tile(s, CHUNK, SUBLANES_BF16)
    n = s // r

    def body(da_ref, u_ref, ua_ref, w_ref, du_ref, gw_ref, gb_ref):
        @pl.when(pl.program_id(1) == 0)
        def _():
            gw_ref[...] = jnp.zeros_like(gw_ref)
            gb_ref[...] = jnp.zeros_like(gb_ref)

        def step(i, carry):
            nxt, sums = carry
            rows = pl.ds(pl.multiple_of((n - 1 - i) * r, r), r)
            ug = ua_ref[0, rows, :].astype(F32)
            uv = ua_ref[1, rows, :].astype(F32)
            sg = _sigmoid(ug)
            dacc = da_ref[rows, :].astype(F32)
            dus = (dacc * uv * sg * (1.0 + ug * (1.0 - sg)), dacc * (ug * sg))
            first, new_sums = [], []
            for p in range(2):
                du = dus[p]
                d1, d2 = _up1_up2(du, nxt[p])
                du_ref[p, rows, :] = _conv3_rows(du, d1, d2, w_ref, p).astype(BF16)
                u = u_ref[p, rows, :].astype(F32)
                sb, s0, s1, s2 = sums[p]
                new_sums.append((sb + _fold8(du), s0 + _fold8(d2 * u), s1 + _fold8(d1 * u), s2 + _fold8(du * u)))
                first.append(du[:HALO])
            return tuple(first), tuple(new_sums)

        zero = jnp.zeros((HALO, c), F32)
        _, sums = lax.fori_loop(0, n, step, ((zero, zero), ((zero,) * 4,) * 2))
        for p in range(2):
            sb, s0, s1, s2 = sums[p]
            gb_ref[p] += jnp.sum(sb, axis=0, keepdims=True)
            gw_ref[p, pl.ds(0, 1), :] += jnp.sum(s0, axis=0, keepdims=True)
            gw_ref[p, pl.ds(1, 1), :] += jnp.sum(s1, axis=0, keepdims=True)
            gw_ref[p, pl.ds(2, 1), :] += jnp.sum(s2, axis=0, keepdims=True)

    blocks = _nbytes((s, c), BF16) + 3 * _nbytes((2, s, c), BF16)
    return _pcall(
        body, name="ffn_mid_bwd", grid=(f // c, nseq),
        in_specs=[pl.BlockSpec((s, c), lambda j, b: (b, j)),
                  pl.BlockSpec((2, s, c), lambda j, b: (0, b, j)),
                  pl.BlockSpec((2, s, c), lambda j, b: (0, b, j)),
                  pl.BlockSpec((2, 3, c), lambda j, b: (0, 0, j))],
        out_specs=[pl.BlockSpec((2, s, c), lambda j, b: (0, b, j)),
                   pl.BlockSpec((2, 3, c), lambda j, b: (0, 0, j)),
                   pl.BlockSpec((2, 1, c), lambda j, b: (0, 0, j))],
        out_shape=[SDS((2, t, f), BF16), SDS((2, 3, f), F32), SDS((2, 1, f), F32)],
        sem=("parallel", "arbitrary"), blocks=blocks, temps=4 * 1024 * 1024, comm=comm,
    )(da, u0, ua, cw)


def _wgrad(a, b, name, *, tr, tn, b_plane_of=None, out_shards=None, comm=None):
    t, m = a.shape
    n_total = b.shape[-1] * (b.shape[0] if b.ndim == 3 else 1)
    tk = _tile(t, WGRAD_TOKENS, SUBLANES_BF16)
    nk = t // tk
    once = pl.Buffered(1) if nk == 1 else None

    def body(a_ref, b_ref, o_ref, *acc):
        part = _dot_ta(a_ref[...], b_ref[...])
        if nk == 1:
            o_ref[...] = part.astype(BF16)
        else:
            _acc_over(pl.program_id(2), nk, part, acc[0], o_ref)

    if b.ndim == 3:
        b_spec = pl.BlockSpec((None, tk, tn), lambda r, n, k: (b_plane_of(n)[0], k, b_plane_of(n)[1]))
    else:
        b_spec = pl.BlockSpec((tk, tn), lambda r, n, k: (k, n), pipeline_mode=once if n_total == tn else None)
    if out_shards is None:
        o_spec = pl.BlockSpec((tr, tn), lambda r, n, k: (r, n))
        o_shape = SDS((m, n_total), BF16)
    else:
        nps = n_total // out_shards // tn
        o_spec = pl.BlockSpec((None, tr, tn), lambda r, n, k: (n // nps, r, n % nps))
        o_shape = SDS((out_shards, m, n_total // out_shards), BF16)
    blocks = _nbytes((tk, tr), BF16) + _nbytes((tk, tn), BF16) + _nbytes((tr, tn), BF16)
    return _pcall(
        body, name=name, grid=(m // tr, n_total // tn, nk),
        in_specs=[pl.BlockSpec((tk, tr), lambda r, n, k: (k, r), pipeline_mode=once if m == tr else None), b_spec],
        out_specs=[o_spec], out_shape=[o_shape],
        scratch_shapes=[] if nk == 1 else [pltpu.VMEM((tr, tn), F32)],
        sem=("parallel", "parallel", "arbitrary"), blocks=blocks, temps=2 * _nbytes((tr, tn), F32), comm=comm,
    )(a, b)


def _wgrad3(lhs3, rhs3, comm=None):
    nw, t, d = lhs3.shape
    tk = _tile(t, WGRAD_TOKENS, SUBLANES_BF16)
    nk = t // tk

    def body(a_ref, b_ref, o_ref, *acc):
        part = _dot_ta(a_ref[...], b_ref[...])
        if nk == 1:
            o_ref[...] = part.astype(BF16)
        else:
            _acc_over(pl.program_id(1), nk, part, acc[0], o_ref)

    blocks = 2 * _nbytes((tk, d), BF16) + _nbytes((d, d), BF16)
    return _pcall(
        body, name="wgrad_sq3", grid=(nw, nk),
        in_specs=[pl.BlockSpec((None, tk, d), lambda w, k: (w, k, 0)),
                  pl.BlockSpec((None, tk, d), lambda w, k: (w, k, 0))],
        out_specs=[pl.BlockSpec((None, d, d), lambda w, k: (w, 0, 0))],
        out_shape=[SDS((nw, d, d), BF16)],
        scratch_shapes=[] if nk == 1 else [pltpu.VMEM((d, d), F32)],
        sem=("parallel", "arbitrary"), blocks=blocks, temps=2 * _nbytes((d, d), F32), comm=comm,
    )(lhs3, rhs3)


def _ffn_bwd_dx1(du0, w_up, x1, dx2, g_ffn, n_planes_out, comm=None):
    _, t, f = du0.shape
    d = x1.shape[1]
    nsh, _, ws = w_up.shape
    tm = _tile(t, 256, SUBLANES_BF16)
    spp = f // ws

    def body(du_ref, w_ref, x1_ref, dx2_ref, g_ref, dx1_ref, dxb_ref, gg_ref):
        @pl.when(pl.program_id(0) == 0)
        def _():
            gg_ref[...] = jnp.zeros_like(gg_ref)

        dh = None
        for k in range(nsh):
            part = _dot_tb(du_ref[k // spp, :, (k % spp) * ws:(k % spp + 1) * ws], w_ref[k])
            dh = part if dh is None else dh + part
        xh, inv = _rms_fwd(x1_ref[...])
        gg_ref[...] += jnp.sum(dh * xh, axis=0, keepdims=True)
        dx1 = dx2_ref[...] + _rms_bwd(dh, xh, inv, g_ref[...])
        dx1_ref[...] = dx1
        dxb_ref[...] = dx1.astype(BF16)

    blocks = _nbytes((2, tm, f), BF16) + 3 * _nbytes((tm, d), F32) + _nbytes((tm, d), BF16)
    return _pcall(
        body, name="ffn_bwd_dx1", grid=(t // tm,),
        in_specs=[pl.BlockSpec((2, tm, f), lambda i: (0, i, 0)),
                  pl.BlockSpec((nsh, d, ws), lambda i: (0, 0, 0), pipeline_mode=pl.Buffered(1)),
                  pl.BlockSpec((tm, d), lambda i: (i, 0)),
                  pl.BlockSpec((tm, d), lambda i: (i, 0)),
                  pl.BlockSpec((1, d), lambda i: (0, 0))],
        out_specs=[pl.BlockSpec((tm, d), lambda i: (i, 0)),
                   pl.BlockSpec((None, tm, d), lambda i: (n_planes_out - 1, i, 0)),
                   pl.BlockSpec((1, d), lambda i: (0, 0))],
        out_shape=[SDS((t, d), F32), SDS((n_planes_out, t, d), BF16), SDS((1, d), F32)],
        sem=("arbitrary",), blocks=blocks, temps=_nbytes(w_up.shape, BF16) + 8 * _nbytes((tm, d), F32), comm=comm,
    )(du0, w_up, x1, dx2, g_ffn)


def _mixer_bwd(rhs3, z, ypc, w3, comm=None):
    _, t, d = rhs3.shape
    tm = _tile(t, 256, SUBLANES_BF16)

    def body(dx_ref, zgp, zgc, ypc_ref, w_ref, dyo, dzo, dpq):
        dm = _dot_tb(dx_ref[...], w_ref[2])
        sp = _sigmoid(zgp[...].astype(F32))
        sc = _sigmoid(zgc[...].astype(F32))
        dyp = (dm * sp).astype(BF16)
        dyc = (dm * sc).astype(BF16)
        dzo[0] = (dm * ypc_ref[0].astype(F32) * sp * (1.0 - sp)).astype(BF16)
        dzo[1] = (dm * ypc_ref[1].astype(F32) * sc * (1.0 - sc)).astype(BF16)
        dyo[0] = dyp
        dyo[1] = dyc
        dpq[0] = _dot_tb(dyp, w_ref[0]).astype(BF16)
        dpq[1] = _dot_tb(dyc, w_ref[1]).astype(BF16)

    blocks = _nbytes((tm, d), BF16) * 3 + _nbytes((2, tm, d), BF16) * 4 + _nbytes((3, d, d), BF16)
    return _pcall(
        body, name="mixer_bwd", grid=(t // tm,),
        in_specs=[pl.BlockSpec((None, tm, d), lambda i: (2, i, 0)),
                  pl.BlockSpec((tm, d), lambda i: (i, 4)),
                  pl.BlockSpec((tm, d), lambda i: (i, 5)),
                  pl.BlockSpec((2, tm, d), lambda i: (0, i, 0)),
                  pl.BlockSpec((3, d, d), lambda i: (0, 0, 0))],
        out_specs=[pl.BlockSpec((2, tm, d), lambda i: (0, i, 0)),
                   pl.BlockSpec((2, tm, d), lambda i: (2, i, 0)),
                   pl.BlockSpec((2, tm, d), lambda i: (0, i, 0))],
        out_shape=[SDS(rhs3.shape, BF16), SDS((N_SPLITS, t, d), BF16), SDS((2, t, d), BF16)],
        input_output_aliases={0: 0},
        sem=("parallel",), blocks=blocks, temps=8 * _nbytes((tm, d), F32), comm=comm,
    )(rhs3, z, z, ypc, w3)


def _conv_bwd(dz, dpq, z, conv_w, nseq, comm=None):
    _, t, d = dz.shape
    s = t // nseq
    c = _tile(d, 128, LANES)
    nb = d // c

    def body(dz_in, dq_ref, zb, zc, zv, cw, dzo, gw_ref):
        del dz_in

        @pl.when(pl.program_id(1) == 0)
        def _():
            gw_ref[...] = jnp.zeros_like(gw_ref)

        row = lax.broadcasted_iota(jnp.int32, (s, c), 0)
        b = zb[...].astype(F32)
        cm = zc[...].astype(F32)
        v = zv[...].astype(F32)
        cv = cm * v
        cv1 = _shift_down(cv, 1, row)
        cv2 = _shift_down(cv, 2, row)
        w0, w1, w2 = cw[pl.ds(0, 1), :], cw[pl.ds(1, 1), :], cw[pl.ds(2, 1), :]
        cc = w2 * cv + w1 * cv1 + w0 * cv2
        dq = dq_ref[...].astype(F32)
        dzo[0] = (dq * cc).astype(BF16)
        dcc = dq * b
        gw_ref[pl.ds(0, 1), :] += jnp.sum(dcc * cv2, axis=0, keepdims=True)
        gw_ref[pl.ds(1, 1), :] += jnp.sum(dcc * cv1, axis=0, keepdims=True)
        gw_ref[pl.ds(2, 1), :] += jnp.sum(dcc * cv, axis=0, keepdims=True)
        dcv = w2 * dcc + w1 * _shift_up(dcc, 1, row) + w0 * _shift_up(dcc, 2, row)
        dzo[1] = (dcv * v).astype(BF16)
        dzo[2] = (dcv * cm).astype(BF16)

    blocks = 4 * _nbytes((s, c), BF16) + _nbytes((3, s, c), BF16)
    return _pcall(
        body, name="conv_bwd", grid=(nb, nseq),
        in_specs=[ANY,
                  pl.BlockSpec((None, s, c), lambda j, b: (1, b, j)),
                  pl.BlockSpec((s, c), lambda j, b: (b, nb + j)),
                  pl.BlockSpec((s, c), lambda j, b: (b, 2 * nb + j)),
                  pl.BlockSpec((s, c), lambda j, b: (b, 3 * nb + j)),
                  pl.BlockSpec((3, c), lambda j, b: (0, j))],
        out_specs=[pl.BlockSpec((3, s, c), lambda j, b: (0, b, j)),
                   pl.BlockSpec((3, c), lambda j, b: (0, j))],
        out_shape=[SDS(dz.shape, BF16), SDS((3, d), F32)],
        input_output_aliases={0: 0},
        sem=("parallel", "arbitrary"), blocks=blocks, temps=16 * _nbytes((s, c), F32), comm=comm,
    )(dz, dpq, z, z, z, conv_w)


def _pool_bwd_call(dz, dpq, z, pool_w, pool_scale, nseq, comm=None):
    _, t, d = dz.shape
    s = t // nseq
    c = d // N_GROUPS

    def body(dz_in, dp_ref, zp, pw, ps, dzo, gpw_ref, gps_ref):
        del dz_in
        j = pl.program_id(0)

        @pl.when(pl.program_id(1) == 0)
        def _():
            gpw_ref[...] = jnp.zeros_like(gpw_ref)
            gps_ref[...] = jnp.zeros_like(gps_ref)

        row = lax.broadcasted_iota(jnp.int32, (s, c), 0)
        for gi, win in enumerate(POOL_WINDOWS):
            @pl.when(j == gi)
            def _(win=win):
                pb = _pool_fwd(zp[...].astype(F32), win, row).astype(BF16)
                plin = _dot(pb, pw[...])
                dps = dp_ref[...].astype(F32)
                gps_ref[...] += jnp.sum(dps * plin, axis=0, keepdims=True)
                dplb = (dps * ps[...]).astype(BF16)
                gpw_ref[...] += _dot_ta(pb, dplb)
                dzo[...] = _pool_bwd(_dot_tb(dplb, pw[...]), win, row).astype(BF16)

    blocks = 3 * _nbytes((s, c), BF16) + _nbytes((c, c), BF16) + _nbytes((c, c), F32)
    return _pcall(
        body, name="pool_bwd", grid=(N_GROUPS, nseq),
        in_specs=[ANY,
                  pl.BlockSpec((None, s, c), lambda j, b: (0, b, j)),
                  pl.BlockSpec((s, c), lambda j, b: (b, j)),
                  pl.BlockSpec((None, c, c), lambda j, b: (j, 0, 0)),
                  pl.BlockSpec((1, c), lambda j, b: (0, j))],
        out_specs=[pl.BlockSpec((None, s, c), lambda j, b: (3, b, j)),
                   pl.BlockSpec((None, c, c), lambda j, b: (j, 0, 0)),
                   pl.BlockSpec((1, c), lambda j, b: (0, j))],
        out_shape=[SDS(dz.shape, BF16), SDS((N_GROUPS, c, c), F32), SDS((1, d), F32)],
        input_output_aliases={0: 0},
        sem=("parallel", "arbitrary"), blocks=blocks, temps=10 * _nbytes((s, c), F32), comm=comm,
    )(dz, dpq, z, pool_w, pool_scale)


def _dz_plane(zb):
    return jnp.where(zb < 4, (zb + 3) % 4, zb)


def _wgrad_in(h1, dz, nsh, comm=None):
    t, d = h1.shape
    ws = N_SPLITS * d // nsh
    kb = _tile(math.gcd(d, ws), 512, LANES)
    npl = d // kb
    nps = ws // kb
    tk = _tile(t, WGRAD_TOKENS, SUBLANES_BF16)
    nk = t // tk

    def body(a_ref, b_ref, o_ref, *acc):
        part = _dot_ta(a_ref[...], b_ref[...])
        if nk == 1:
            o_ref[...] = part.astype(BF16)
        else:
            _acc_over(pl.program_id(1), nk, part, acc[0], o_ref)

    blocks = _nbytes((tk, d), BF16) + _nbytes((tk, kb), BF16) + _nbytes((d, kb), BF16)
    return _pcall(
        body, name="wgrad_in", grid=(N_SPLITS * npl, nk),
        in_specs=[pl.BlockSpec((tk, d), lambda cb, k: (k, 0), pipeline_mode=pl.Buffered(1) if nk == 1 else None),
                  pl.BlockSpec((None, tk, kb), lambda cb, k: (_dz_plane(cb // npl), k, cb % npl))],
        out_specs=[pl.BlockSpec((None, d, kb), lambda cb, k: (cb // nps, 0, cb % nps))],
        out_shape=[SDS((nsh, d, ws), BF16)],
        scratch_shapes=[] if nk == 1 else [pltpu.VMEM((d, kb), F32)],
        sem=("parallel", "arbitrary"), blocks=blocks, temps=2 * _nbytes((d, kb), F32), comm=comm,
    )(h1, dz)


def _mixer_bwd_dx(dz, w_in, x, dx1, g_mix, comm=None):
    npln, t, d = dz.shape
    nsh, _, ws = w_in.shape
    tm = _tile(t, 256, SUBLANES_BF16)
    kb = _tile(math.gcd(d, ws), 512, LANES)
    npl = d // kb
    nps = ws // kb

    def body(dz_ref, w_ref, x_ref, dx1_ref, g_ref, dx_ref, gg_ref):
        @pl.when(pl.program_id(0) == 0)
        def _():
            gg_ref[...] = jnp.zeros_like(gg_ref)

        dh = None
        for cb in range(npln * npl):
            zb = cb // npl
            plane = (zb + 3) % 4 if zb < 4 else zb
            part = _dot_tb(dz_ref[plane, :, (cb % npl) * kb:(cb % npl + 1) * kb],
                           w_ref[cb // nps, :, (cb % nps) * kb:(cb % nps + 1) * kb])
            dh = part if dh is None else dh + part
        xh, inv = _rms_fwd(x_ref[...])
        gg_ref[...] += jnp.sum(dh * xh, axis=0, keepdims=True)
        dx_ref[...] = dx1_ref[...] + _rms_bwd(dh, xh, inv, g_ref[...])

    blocks = _nbytes((npln, tm, d), BF16) + 3 * _nbytes((tm, d), F32)
    return _pcall(
        body, name="mixer_bwd_dx", grid=(t // tm,),
        in_specs=[pl.BlockSpec((npln, tm, d), lambda i: (0, i, 0)),
                  pl.BlockSpec((nsh, d, ws), lambda i: (0, 0, 0), pipeline_mode=pl.Buffered(1)),
                  pl.BlockSpec((tm, d), lambda i: (i, 0)),
                  pl.BlockSpec((tm, d), lambda i: (i, 0)),
                  pl.BlockSpec((1, d), lambda i: (0, 0))],
        out_specs=[pl.BlockSpec((tm, d), lambda i: (i, 0)),
                   pl.BlockSpec((1, d), lambda i: (0, 0))],
        out_shape=[SDS((t, d), F32), SDS((1, d), F32)],
        sem=("arbitrary",), blocks=blocks, temps=_nbytes(w_in.shape, BF16) + 8 * _nbytes((tm, d), F32), comm=comm,
    )(dz, w_in, x, dx1, g_mix)


N_BIG = 5
SHARD_MAJOR = (0, 2)
ROWS_DIM1 = (1, 4)


def _ds(start, size, align):
    if isinstance(start, int):
        return pl.ds(start, size)
    return pl.ds(pl.multiple_of(start, align), size)


def _piece(a, ref, k, h):
    if a in SHARD_MAJOR:
        r = ref.shape[1] // 2
        return ref.at[k, _ds(h * r, r, SUBLANES_BF16), :]
    if a in ROWS_DIM1:
        r = ref.shape[1] // 8
        return ref.at[:, _ds((2 * k + h) * r, r, SUBLANES_BF16), :]
    r = ref.shape[0] // 8
    return ref.at[_ds((2 * k + h) * r, r, SUBLANES_BF16), :]


def _half(a, ref, h):
    if a in ROWS_DIM1:
        r = ref.shape[1] // 2
        return ref.at[:, _ds(h * r, r, SUBLANES_BF16), :]
    r = ref.shape[0] // 2
    return ref.at[_ds(h * r, r, SUBLANES_BF16), :]


def _piece_shape(a, full_shape):
    if a in SHARD_MAJOR:
        return (full_shape[1] // 2, full_shape[2])
    if a in ROWS_DIM1:
        return (full_shape[0], full_shape[1] // 8, full_shape[2])
    return (full_shape[0] // 8, full_shape[1])


def _shard_shape(a, full_shape):
    if a in SHARD_MAJOR:
        return (full_shape[1], full_shape[2])
    if a in ROWS_DIM1:
        return (full_shape[0], full_shape[1] // 4, full_shape[2])
    return (full_shape[0] // 4, full_shape[1])


def _rows_axis(a):
    return 1 if a in ROWS_DIM1 else 0


def _piece_block(a, full_shape):
    ps = _piece_shape(a, full_shape)
    if a in SHARD_MAJOR:
        return (None,) + ps, lambda k, c: (k, c, 0)
    if a in ROWS_DIM1:
        return ps, lambda k, c: (0, 2 * k + c, 0)
    return ps, lambda k, c: (2 * k + c, 0)


def _coords():
    return lax.axis_index("x"), lax.axis_index("y"), lax.axis_index("c")


def _peer_chips(x, y):
    return [(1 - x, y), (x, 1 - y), (1 - x, 1 - y)]


def _remote(src, dst, ssem, rsem, dev):
    return pltpu.make_async_remote_copy(src_ref=src, dst_ref=dst, send_sem=ssem, recv_sem=rsem,
                                        device_id=dev, device_id_type=MESH)


def _dma_sems(*counts):
    return [pltpu.SemaphoreType.DMA((n,)) for n in counts]


def _symmetric(ins, out_shapes, sems, copies, peers, aliases=None):
    def start(cins, couts, csems):
        for cp in copies(cins, couts, csems):
            cp.start()

    def finish(cins, couts, csems):
        for cp in copies(cins, couts, csems):
            cp.wait()

    return _Comm(ins, out_shapes, sems, start, finish, peers, aliases)


def _rows_part(a, ref, part):
    if part is None:
        return ref
    p, q, n = part
    ax = _rows_axis(a)
    r = ref.shape[ax] // n
    return ref.at[tuple(pl.ds(p * r, (q - p) * r) if d == ax else slice(None) for d in range(len(ref.shape)))]


def _merge(comms):
    ins, outs, sems, aliases, spans = [], [], [], {}, []
    for cm in comms:
        spans.append((len(ins), len(outs), len(sems)))
        for i, o in cm.aliases.items():
            aliases[len(ins) + i] = len(outs) + o
        ins += cm.ins
        outs += cm.out_shapes
        sems += cm.sems

    def each(fn_name):
        def run(cins, couts, csems):
            for cm, (i0, o0, s0) in zip(comms, spans):
                fn = getattr(cm, fn_name)
                if fn is not None:
                    fn(cins[i0:i0 + len(cm.ins)], couts[o0:o0 + len(cm.out_shapes)], csems[s0:s0 + len(cm.sems)])
        return run

    return _Comm(ins, outs, sems, each("start"), each("finish"), frozenset().union(*[cm.peers for cm in comms]),
                 aliases, mid=each("mid") if any(cm.mid is not None for cm in comms) else None)


def _gather_comm(arrs, locs, full_shapes, part=None, into=None):
    n = len(arrs)

    def own(cins, couts, csems):
        x, y, c = _coords()
        j = 2 * x + y
        return [_remote(_rows_part(a, _half(a, cins[q], h), part), _rows_part(a, _piece(a, couts[q], j, h), part),
                        csems[0].at[2 * q + h], csems[1].at[2 * q + h], (x, y, 1 - c))
                for q, a in enumerate(arrs) for h in range(2)]

    def sends(cins, couts, csems):
        x, y, c = _coords()
        j = 2 * x + y
        return [_remote(_rows_part(a, _half(a, cins[q], c), part), _rows_part(a, _piece(a, couts[q], j, c), part),
                        csems[2].at[3 * q + i], csems[3].at[3 * q + i], (px, py, c))
                for q, a in enumerate(arrs) for i, (px, py) in enumerate(_peer_chips(x, y))]

    def forwards(couts, csems, half_of):
        x, y, c = _coords()
        out = []
        for q, a in enumerate(arrs):
            for i, (px, py) in enumerate(_peer_chips(x, y)):
                landed = _rows_part(a, _piece(a, couts[q], 2 * px + py, half_of(c)), part)
                out.append(_remote(landed, landed, csems[4].at[3 * q + i], csems[5].at[3 * q + i], (x, y, 1 - c)))
        return out

    def start(cins, couts, csems):
        for cp in sends(cins, couts, csems) + own(cins, couts, csems):
            cp.start()

    def finish(cins, couts, csems):
        fw = forwards(couts, csems, lambda c: c)
        for cp, f in zip(sends(cins, couts, csems), fw):
            cp.wait_recv()
            f.start()
        for f in forwards(couts, csems, lambda c: 1 - c):
            f.wait_recv()
        for cp in sends(cins, couts, csems) + fw:
            cp.wait_send()
        for cp in own(cins, couts, csems):
            cp.wait()

    ins = [locs[a] for a in arrs] + ([into[a] for a in arrs] if into else [])
    return _Comm(ins, [SDS(full_shapes[a], BF16) for a in arrs],
                 _dma_sems(2 * n, 2 * n, 3 * n, 3 * n, 3 * n, 3 * n), start, finish, CHIPS + (SIBLING,),
                 aliases={n + q: q for q in range(n)} if into else None)


def _ring_gather_comm(arrs, locs, full_shapes):
    n = len(arrs)

    def own(cins, couts, csems):
        x, y, c = _coords()
        j = 2 * x + y
        return [_remote(_half(a, cins[q], h), _piece(a, couts[q], j, h), csems[0].at[2 * q + h],
                        csems[1].at[2 * q + h], (x, y, 1 - c)) for q, a in enumerate(arrs) for h in range(2)]

    def sends(cins, couts, csems):
        x, y, c = _coords()
        j = 2 * x + y
        return [_remote(_half(a, cins[q], c), _piece(a, couts[q], j, c), csems[2].at[2 * q + i],
                        csems[3].at[2 * q + i], (px, py, c))
                for q, a in enumerate(arrs) for i, (px, py) in enumerate(_peer_chips(x, y)[:2])]

    def relays(couts, csems):
        x, y, c = _coords()
        peers = _peer_chips(x, y)
        out = []
        for q, a in enumerate(arrs):
            for r, (src_p, dst_p) in enumerate(((0, 1), (1, 0))):
                sx, sy = peers[src_p]
                rows = _rows_part(a, _piece(a, couts[q], 2 * sx + sy, c), (r, r + 1, 2))
                out.append(_remote(rows, rows, csems[6].at[2 * q + r], csems[7].at[2 * q + r], (*peers[dst_p], c)))
        return out

    def forwards(couts, csems, half_of, which):
        x, y, c = _coords()
        out = []
        for q, a in enumerate(arrs):
            for i in which:
                px, py = _peer_chips(x, y)[i]
                landed = _piece(a, couts[q], 2 * px + py, half_of(c))
                out.append(_remote(landed, landed, csems[4].at[3 * q + i], csems[5].at[3 * q + i], (x, y, 1 - c)))
        return out

    def start(cins, couts, csems):
        for cp in sends(cins, couts, csems) + own(cins, couts, csems):
            cp.start()

    def mid(cins, couts, csems):
        for cp in sends(cins, couts, csems):
            cp.wait_recv()
        for cp in relays(couts, csems) + forwards(couts, csems, lambda c: c, (0, 1)):
            cp.start()

    def finish(cins, couts, csems):
        for cp in relays(couts, csems):
            cp.wait_recv()
        fw_diag = forwards(couts, csems, lambda c: c, (2,))
        for f in fw_diag:
            f.start()
        for f in forwards(couts, csems, lambda c: 1 - c, (0, 1, 2)):
            f.wait_recv()
        for cp in (sends(cins, couts, csems) + relays(couts, csems)
                   + forwards(couts, csems, lambda c: c, (0, 1)) + fw_diag):
            cp.wait_send()
        for cp in own(cins, couts, csems):
            cp.wait()

    return _Comm([locs[a] for a in arrs], [SDS(full_shapes[a], BF16) for a in arrs],
                 _dma_sems(2 * n, 2 * n, 2 * n, 2 * n, 3 * n, 3 * n, 2 * n, 2 * n), start, finish,
                 CHIPS + (SIBLING,), mid=mid)


def _halves_comm(arrs, gbs):
    n = len(arrs)

    def copies(cins, couts, csems):
        x, y, c = _coords()
        return [_remote(_piece(a, cins[q], k, 1 - c), couts[q].at[k], csems[0].at[4 * q + k], csems[1].at[4 * q + k],
                        (x, y, 1 - c)) for q, a in enumerate(arrs) for k in range(4)]

    return _symmetric([gbs[a] for a in arrs], [SDS((4,) + _piece_shape(a, gbs[a].shape), BF16) for a in arrs],
                      _dma_sems(4 * n, 4 * n), copies, [SIBLING])


def _chips_comm(arrs, ps, part=None, into=None):
    n = len(arrs)

    def copies(cins, couts, csems):
        x, y, c = _coords()
        return [_remote(_rows_part(a, cins[q].at[2 * px + py], part), _rows_part(a, couts[q].at[i], part),
                        csems[0].at[3 * q + i], csems[1].at[3 * q + i], (px, py, c))
                for q, a in enumerate(arrs) for i, (px, py) in enumerate(_peer_chips(x, y))]

    ins = [ps[a] for a in arrs] + ([into[a] for a in arrs] if into else [])
    return _symmetric(ins, [SDS((3,) + ps[a].shape[1:], BF16) for a in arrs], _dma_sems(3 * n, 3 * n), copies, CHIPS,
                      aliases={n + q: q for q in range(n)} if into else None)


def _result_comm(arrs, gs):
    n = len(arrs)

    def copies(cins, couts, csems):
        x, y, c = _coords()
        return [_remote(_half(a, cins[q], c), _half(a, couts[q], c), csems[0].at[q], csems[1].at[q], (x, y, 1 - c))
                for q, a in enumerate(arrs)]

    return _symmetric([gs[a] for a in arrs], [SDS(gs[a].shape, F32) for a in arrs], _dma_sems(n, n), copies,
                      [SIBLING], aliases={q: q for q in range(n)})


def _add_halves(arrs, gbs, lands, c_arr, name):
    n = len(arrs)

    def body(c_ref, *refs):
        del c_ref
        for q in range(n):
            refs[2 * n + q][...] = (refs[q][...].astype(F32) + refs[n + q][...].astype(F32)).astype(BF16)

    g_specs, l_specs, o_specs, blocks = [], [], [], 0
    for a in arrs:
        bs, imap = _piece_block(a, gbs[a].shape)
        ps = _piece_shape(a, gbs[a].shape)
        g_specs.append(pl.BlockSpec(bs, lambda k, c_ref, imap=imap: imap(k, c_ref[0])))
        nd = len(ps)
        l_specs.append(pl.BlockSpec((None,) + ps, lambda k, c_ref, nd=nd: (k,) + (0,) * nd))
        o_specs.append(pl.BlockSpec((None,) + ps, lambda k, c_ref, nd=nd: (k,) + (0,) * nd))
        blocks += 3 * _nbytes(ps, BF16)
    return list(pl.pallas_call(
        body, name=name,
        grid_spec=pltpu.PrefetchScalarGridSpec(
            num_scalar_prefetch=1, grid=(4,), in_specs=g_specs + l_specs, out_specs=o_specs),
        out_shape=[SDS((4,) + _piece_shape(a, gbs[a].shape), BF16) for a in arrs],
        compiler_params=_params(("parallel",), blocks, blocks),
    )(c_arr, *[gbs[a] for a in arrs], *lands))


def _sum_chips(a, p, land, shard_shape, jc_arr, name):
    ps = land.shape[1:]
    ax = _rows_axis(a)
    rows = ps[ax]
    nsub = 2 if rows % (2 * SUBLANES_BF16) == 0 else 1
    bs = tuple(r // nsub if q == ax else r for q, r in enumerate(ps))
    nd = len(ps)

    def at_rows(v):
        return tuple(v if q == ax else 0 for q in range(nd))

    def body(jc_ref, p_ref, l_ref, o_ref):
        del jc_ref
        acc = p_ref[...].astype(F32) + l_ref[0].astype(F32)
        acc = acc + l_ref[1].astype(F32)
        o_ref[...] = acc + l_ref[2].astype(F32)

    blocks = 4 * _nbytes(bs, BF16) + _nbytes(bs, F32)
    return pl.pallas_call(
        body, name=name,
        grid_spec=pltpu.PrefetchScalarGridSpec(
            num_scalar_prefetch=1, grid=(nsub,),
            in_specs=[pl.BlockSpec((None,) + bs, lambda s, jc: (jc[0],) + at_rows(s)),
                      pl.BlockSpec((3,) + bs, lambda s, jc: (0,) + at_rows(s))],
            out_specs=pl.BlockSpec(bs, lambda s, jc: at_rows(jc[1] * nsub + s))),
        out_shape=SDS(shard_shape, F32),
        compiler_params=_params(("parallel",), blocks, 2 * _nbytes(bs, F32)),
    )(jc_arr, p, land)


def _small_comm(v):
    rows = v.shape[0]

    def copies(cins, couts, csems):
        x, y, c = _coords()
        me = 4 * x + 2 * y + c
        out = [pltpu.make_async_copy(cins[0], couts[0].at[me], csems[0].at[0])]
        for dlt in range(1, 8):
            px = 1 - x if (dlt >> 2) & 1 else x
            py = 1 - y if (dlt >> 1) & 1 else y
            pc = 1 - c if dlt & 1 else c
            out.append(_remote(cins[0], couts[0].at[me], csems[1].at[dlt - 1], csems[2].at[dlt - 1], (px, py, pc)))
        return out

    return _symmetric([v], [SDS((8, rows, LANES), F32)], _dma_sems(1, 7, 7), copies, EVERYONE)


def _sum8(slots, name):
    def body(s_ref, o_ref):
        acc = s_ref[0]
        for i in range(1, 8):
            acc = acc + s_ref[i]
        o_ref[...] = acc

    return pl.pallas_call(
        body, name=name,
        in_specs=[pl.BlockSpec(memory_space=pltpu.VMEM)], out_specs=pl.BlockSpec(memory_space=pltpu.VMEM),
        out_shape=SDS(slots.shape[1:], F32),
    )(slots)


def _adamw(w, g, m, v, name, g_plane=None):
    rows, cols = w.shape
    tr = _tile(rows, max(SUBLANES_F32, (256 * 1024 // cols) // SUBLANES_F32 * SUBLANES_F32), SUBLANES_F32)

    def body(w_ref, g_ref, m_ref, v_ref, go_ref, d_ref, mo_ref, vo_ref):
        gr = g_ref[...]
        mn = ADAM_B1 * m_ref[...] + (1.0 - ADAM_B1) * gr
        vn = ADAM_B2 * v_ref[...] + (1.0 - ADAM_B2) * (gr * gr)
        m_hat = mn / (1.0 - ADAM_B1 ** ADAM_STEP)
        v_hat = vn / (1.0 - ADAM_B2 ** ADAM_STEP)
        d_ref[...] = -ADAM_LR * (m_hat / (jnp.sqrt(v_hat) + ADAM_EPS) + ADAM_WD * w_ref[...])
        go_ref[...] = gr
        mo_ref[...] = mn
        vo_ref[...] = vn

    spec = pl.BlockSpec((tr, cols), lambda i: (i, 0))
    g_spec = spec if g_plane is None else pl.BlockSpec((None, tr, cols), lambda i: (g_plane, i, 0))
    return pl.pallas_call(
        body, name=name, grid=(rows // tr,),
        in_specs=[spec, g_spec, spec, spec], out_specs=[spec, spec, spec, spec],
        out_shape=[SDS((rows, cols), F32)] * 4,
        compiler_params=_params(("parallel",), 8 * _nbytes((tr, cols), F32), 4 * _nbytes((tr, cols), F32)),
    )(w, g, m, v)


def _pack(parts):
    rows = []
    for p in parts:
        r = p.reshape(-1, LANES)
        pad = (-r.shape[0]) % SUBLANES_F32
        if pad:
            r = jnp.pad(r, ((0, pad), (0, 0)))
        rows.append(r)
    return jnp.concatenate(rows, axis=0)


def _unpack(packed, shapes):
    out, at = [], 0
    for s in shapes:
        n = 1
        for q in s:
            n *= q
        r = n // LANES
        out.append(packed[at:at + r].reshape(s))
        at += r + (-r) % SUBLANES_F32
    return out


def kernel(x, norm_mix, w_in, pool_w, pool_scale, w_pool_proj, conv_w, w_conv_out, w_o, norm_ffn, w_up, ffn_conv_w, ffn_conv_b, w_down, norm_final, loss_target, m_norm_mix, m_w_in, m_pool_w, m_pool_scale, m_w_pool_proj, m_conv_w, m_w_conv_out, m_w_o, m_norm_ffn, m_w_up, m_ffn_conv_w, m_ffn_conv_b, m_w_down, m_norm_final, v_norm_mix, v_w_in, v_pool_w, v_pool_scale, v_w_pool_proj, v_conv_w, v_w_conv_out, v_w_o, v_norm_ffn, v_w_up, v_ffn_conv_w, v_ffn_conv_b, v_w_down, v_norm_final):
    nseq, seq, d = x.shape
    t = nseq * seq
    f = w_down.shape[1] * 4
    c = d // N_GROUPS
    xy = lax.axis_index("x") * 2 + lax.axis_index("y")
    c_arr = lax.axis_index("c").astype(jnp.int32).reshape(1)
    jc_arr = jnp.stack([xy, lax.axis_index("c")]).astype(jnp.int32)
    nsh = 4
    zero = jnp.zeros((), jnp.int32)

    locs = [w_in[0].astype(BF16),
            jnp.stack([w_pool_proj[0], w_conv_out[0], w_o[0]]).astype(BF16),
            w_up[0].astype(BF16), w_down[0].astype(BF16), pool_w[0].astype(BF16)]
    full_shapes = [(nsh, d, N_SPLITS * d // nsh), (3, d, d), (nsh, d, 2 * f // nsh), (f, d), (N_GROUPS, c, c)]

    cw_pad = lax.dynamic_update_slice(jnp.zeros((3, d), F32), conv_w[0], (zero, xy * (d // 4)))
    fw_pad = lax.dynamic_update_slice(jnp.zeros((3, 2 * f), F32), ffn_conv_w[0], (zero, xy * (f // 2)))
    small_w = _pack([cw_pad, fw_pad]) * 0.5

    x2d = x.reshape(t, d)
    tgt = loss_target.reshape(t, d)
    ax, ay = lax.axis_index("x"), lax.axis_index("y")
    order = jnp.stack([xy, 2 * (1 - ax) + ay, 2 * ax + 1 - ay, 2 * (1 - ax) + 1 - ay]).astype(jnp.int32)
    (z, h1, w_in_f), (pool_w_f, w3_f, slots_w) = _fwd_in(
        x2d, norm_mix, locs[0], order,
        _merge([_gather_comm([4], locs, full_shapes), _gather_comm([1], locs, full_shapes, part=(0, 1, 2)),
                _small_comm(small_w)]))
    conv_w_f, ffn_cw_f = _unpack(_sum8(slots_w, "sum8_weights"), [(3, d), (3, 2 * f)])
    ffn_cw_p = ffn_cw_f.reshape(3, 2, f).transpose(1, 0, 2)
    ffn_cb_p = ffn_conv_b.reshape(2, 1, f)
    (lhs3,), (w3_f,) = _mixer_mid_fwd(z, pool_w_f, pool_scale, conv_w_f, nseq,
                                      _gather_comm([1], locs, full_shapes, part=(1, 2, 2), into={1: w3_f}))
    (lhs3, ypc, x1, h2), (w_up_f,) = _mixer_out(lhs3, z, x2d, w3_f, norm_ffn,
                                                _ring_gather_comm([2], locs, full_shapes))
    (u0,), (w_down_f,) = _ffn_up(h2, w_up_f, f, _gather_comm([3], locs, full_shapes))
    act, ua = _ffn_mid_fwd(u0, ffn_cw_p, ffn_cb_p, nseq)
    dx2, dx2b, loss11, g_norm_final = _ffn_down_loss(act, w_down_f, x1, tgt, norm_final.reshape(1, d))

    gbs, lands, ps, lands2, rs = {}, {}, {}, {}, {}
    tn_up = _tile(2 * f // nsh, 1408, LANES)
    npp = f // tn_up

    def add(arrs, name):
        for a, p in zip(arrs, _add_halves(arrs, gbs, [lands[a] for a in arrs], c_arr, name)):
            ps[a] = p

    def summed(a):
        rs[a] = _sum_chips(a, ps[a], lands2[a], _shard_shape(a, full_shapes[a]), jc_arr, "sum_chips_%d" % a)

    (gbs[3],), _ = _wgrad(act, dx2b, "wgrad_down", tr=tn_up, tn=d)
    (da,), (lands[3],) = _ffn_bwd_da(dx2b, w_down_f, _halves_comm([3], gbs))
    add([3], "add_halves_down")
    (du0, g_ffn_cw_p, g_ffn_cb_p), (lands2[3],) = _ffn_mid_bwd(da, u0, ua, ffn_cw_p, nseq, _chips_comm([3], ps))
    summed(3)
    (gbs[2],), (rs[3],) = _wgrad(h2, du0, "wgrad_up", tr=d, tn=tn_up, b_plane_of=lambda n: (n // npp, n % npp),
                                 out_shards=nsh, comm=_result_comm([3], rs))
    (dx1, rhs3, g_norm_ffn), (lands[2],) = _ffn_bwd_dx1(du0, w_up_f, x1, dx2, norm_ffn, 3, _halves_comm([2], gbs))
    add([2], "add_halves_up")
    (rhs3, dz, dpq), (lands2[2],) = _mixer_bwd(rhs3, z, ypc, w3_f, _chips_comm([2], ps, part=(0, 1, 2)))
    (gbs[1],), (lands2[2],) = _wgrad3(lhs3, rhs3, _chips_comm([2], ps, part=(1, 2, 2), into=lands2))
    summed(2)
    (dz, g_conv_w), (lands[1], rs[2]) = _conv_bwd(dz, dpq, z, conv_w_f, nseq,
                                                  _merge([_halves_comm([1], gbs), _result_comm([2], rs)]))
    add([1], "add_halves_sq3")
    (dz, g_pool_w, g_pool_scale), _ = _pool_bwd_call(dz, dpq, z, pool_w_f, pool_scale, nseq)
    gbs[4] = g_pool_w.astype(BF16)
    (gbs[0],), (lands2[1],) = _wgrad_in(h1, dz, nsh, _chips_comm([1], ps))
    summed(1)
    lands[0], lands[4] = _run_comm(_halves_comm([0, 4], gbs), "exchange_halves_in")
    add([0, 4], "add_halves_in")
    g_ffn_cw = g_ffn_cw_p.transpose(1, 0, 2).reshape(3, 2 * f)
    small_a = _pack([g_pool_scale, g_norm_ffn, g_ffn_cb_p.reshape(1, 2 * f), g_norm_final.reshape(d), g_conv_w,
                     g_ffn_cw, jnp.pad(loss11, ((0, SUBLANES_F32 - 1), (0, LANES - 1)))])
    (grad_x, g_norm_mix), (lands2[0], lands2[4], rs[1], slots_a) = _mixer_bwd_dx(
        dz, w_in_f, x2d, dx1, norm_mix,
        _merge([_chips_comm([0, 4], ps), _result_comm([1], rs), _small_comm(small_a)]))
    summed(0)
    summed(4)
    rs[0], rs[4], slots_b = _run_comm(_merge([_result_comm([0, 4], rs), _small_comm(_pack([g_norm_mix]))]),
                                      "exchange_result_in")
    shapes_a = [(1, d), (1, d), (1, 2 * f), (d,), (3, d), (3, 2 * f), (SUBLANES_F32, LANES)]
    gs_pool_scale, gs_norm_ffn, gs_ffn_cb, gs_norm_final, gs_conv_w, gs_ffn_cw, loss_blk = _unpack(
        _sum8(slots_a, "sum8_grads"), shapes_a)
    (gs_norm_mix,) = _unpack(_sum8(slots_b, "sum8_norm_mix"), [(1, d)])
    gs_conv_w = lax.dynamic_slice(gs_conv_w, (zero, xy * (d // 4)), (3, d // 4))
    gs_ffn_cw = lax.dynamic_slice(gs_ffn_cw, (zero, xy * (f // 2)), (3, f // 2))

    def upd(w, g, m, v, name, g_plane=None):
        shape = w.shape
        rows = 1
        for q in shape[:-1]:
            rows *= q
        g2 = g if g_plane is not None else g.reshape(rows, shape[-1])
        outs = _adamw(w.reshape(rows, shape[-1]), g2, m.reshape(rows, shape[-1]), v.reshape(rows, shape[-1]),
                      name, g_plane)
        return [o.reshape(shape) for o in outs]

    res = {
        "w_in": upd(w_in, rs[0], m_w_in, v_w_in, "adamw_w_in"),
        "pool_w": upd(pool_w, rs[4], m_pool_w, v_pool_w, "adamw_pool_w"),
        "w_pool_proj": upd(w_pool_proj, rs[1], m_w_pool_proj, v_w_pool_proj, "adamw_w_pool_proj", 0),
        "w_conv_out": upd(w_conv_out, rs[1], m_w_conv_out, v_w_conv_out, "adamw_w_conv_out", 1),
        "w_o": upd(w_o, rs[1], m_w_o, v_w_o, "adamw_w_o", 2),
        "w_up": upd(w_up, rs[2], m_w_up, v_w_up, "adamw_w_up"),
        "w_down": upd(w_down, rs[3], m_w_down, v_w_down, "adamw_w_down"),
    }

    small_names = ["norm_mix", "pool_scale", "norm_ffn", "ffn_conv_b", "norm_final", "conv_w", "ffn_conv_w"]
    small_ws = [norm_mix, pool_scale, norm_ffn, ffn_conv_b, norm_final, conv_w, ffn_conv_w]
    small_ms = [m_norm_mix, m_pool_scale, m_norm_ffn, m_ffn_conv_b, m_norm_final, m_conv_w, m_ffn_conv_w]
    small_vs = [v_norm_mix, v_pool_scale, v_norm_ffn, v_ffn_conv_b, v_norm_final, v_conv_w, v_ffn_conv_w]
    small_gs = [gs_norm_mix, gs_pool_scale, gs_norm_ffn, gs_ffn_cb, gs_norm_final, gs_conv_w, gs_ffn_cw]
    _, sd, sm, sv = _adamw(_pack(small_ws), _pack(small_gs), _pack(small_ms), _pack(small_vs), "adamw_small")
    shapes = [w.shape for w in small_ws]
    sd, sm, sv = _unpack(sd, shapes), _unpack(sm, shapes), _unpack(sv, shapes)
    for i, nm in enumerate(small_names):
        res[nm] = [small_gs[i].reshape(shapes[i]), sd[i], sm[i], sv[i]]

    order = ["norm_mix", "w_in", "pool_w", "pool_scale", "w_pool_proj", "conv_w", "w_conv_out", "w_o", "norm_ffn",
             "w_up", "ffn_conv_w", "ffn_conv_b", "w_down", "norm_final"]
    return (loss_blk[0, 0], grad_x.reshape(x.shape), *[res[n][0] for n in order], *[res[n][1] for n in order],
            *[res[n][2] for n in order], *[res[n][3] for n in order])
```

```python
import math

import jax
import jax.numpy as jnp
from jax import lax
from jax.experimental import pallas as pl
from jax.experimental.pallas import tpu as pltpu

F32 = jnp.float32
BF16 = jnp.bfloat16
SDS = jax.ShapeDtypeStruct
MESH = pl.DeviceIdType.MESH

RMS_EPS = 1e-6
POOL_WINDOWS = (2, 4, 8, 16)
N_GROUPS = len(POOL_WINDOWS)
N_SPLITS = 6

ADAM_LR = 0.001
ADAM_B1 = 0.9
ADAM_B2 = 0.999
ADAM_EPS = 1e-08
ADAM_WD = 0.01
ADAM_STEP = 10

LANES = 128
SUBLANES_F32 = 8
SUBLANES_BF16 = 16
VMEM_BYTES = 64 * 1024 * 1024
VMEM_CAP = VMEM_BYTES - 8 * 1024 * 1024
VMEM_FLOOR = 16 * 1024 * 1024

ANY = pl.BlockSpec(memory_space=pl.ANY)


def _tile(dim, pref, align):
    if dim <= pref:
        return dim
    t = (pref // align) * align
    while t >= align:
        if dim % t == 0:
            return t
        t -= align
    return dim


def _nbytes(shape, dtype):
    n = 1
    for s in shape:
        n *= s
    return n * jnp.dtype(dtype).itemsize


def _params(sem, block_bytes, temp_bytes=0, collective_id=None):
    need = 2 * block_bytes + temp_bytes + 4 * 1024 * 1024
    return pltpu.CompilerParams(dimension_semantics=sem, collective_id=collective_id,
                                vmem_limit_bytes=int(min(max(need, VMEM_FLOOR), VMEM_CAP)))


SIBLING = (0, 0, 1)
CHIPS = ((1, 0, 0), (0, 1, 0), (1, 1, 0))
EVERYONE = tuple((a, b, c) for a in range(2) for b in range(2) for c in range(2) if a + b + c)
PEER_SETS = (frozenset([SIBLING]), frozenset(CHIPS), frozenset(CHIPS + (SIBLING,)), frozenset(EVERYONE))
MID_AT = 0.75


def _collective_id(peers):
    return PEER_SETS.index(frozenset(peers))


def _handshake(peers):
    x, y, c = lax.axis_index("x"), lax.axis_index("y"), lax.axis_index("c")
    bar = pltpu.get_barrier_semaphore()
    for fx, fy, fc in sorted(peers):
        dev = (1 - x if fx else x, 1 - y if fy else y, 1 - c if fc else c)
        pl.semaphore_signal(bar, inc=1, device_id=dev, device_id_type=MESH)
    pl.semaphore_wait(bar, len(peers))


class _Comm:
    def __init__(self, ins, out_shapes, sems, start, finish, peers, aliases=None, mid=None):
        self.ins = list(ins)
        self.out_shapes = list(out_shapes)
        self.sems = list(sems)
        self.start = start
        self.finish = finish
        self.mid = mid
        self.peers = frozenset(peers)
        self.aliases = dict(aliases or {})


def _pcall(body, *, name, grid, in_specs, out_specs, out_shape, sem, blocks, temps=0, scratch_shapes=(),
           input_output_aliases=None, comm=None):
    in_specs = list(in_specs)
    out_specs = list(out_specs)
    out_shape = list(out_shape)
    scratch_shapes = list(scratch_shapes)
    aliases = dict(input_output_aliases or {})
    n_in, n_out, n_scr = len(in_specs), len(out_shape), len(scratch_shapes)
    if comm is None:
        call = pl.pallas_call(
            body, name=name, grid=grid, in_specs=in_specs, out_specs=out_specs, out_shape=out_shape,
            scratch_shapes=scratch_shapes, input_output_aliases=aliases,
            compiler_params=_params(sem, blocks, temps))
        return lambda *args: (list(call(*args)), [])

    nci, nco = len(comm.ins), len(comm.out_shapes)
    n_steps = 1
    for g in grid:
        n_steps *= g

    def hosted(*refs):
        ins = refs[:n_in]
        cins = refs[n_in:n_in + nci]
        outs = refs[n_in + nci:n_in + nci + n_out]
        couts = refs[n_in + nci + n_out:n_in + nci + n_out + nco]
        scr = refs[n_in + nci + n_out + nco:n_in + nci + n_out + nco + n_scr]
        csems = refs[n_in + nci + n_out + nco + n_scr:]
        first = None
        last = None
        step = 0
        for q, g in enumerate(grid):
            pid = pl.program_id(q)
            first = (pid == 0) if first is None else first & (pid == 0)
            last = (pid == g - 1) if last is None else last & (pid == g - 1)
            step = step * g + pid

        @pl.when(first)
        def _():
            _handshake(comm.peers)
            comm.start(cins, couts, csems)

        if comm.mid is not None:
            @pl.when(step == int(MID_AT * n_steps))
            def _():
                comm.mid(cins, couts, csems)

        body(*ins, *outs, *scr)

        @pl.when(last)
        def _():
            comm.finish(cins, couts, csems)

    for i, o in comm.aliases.items():
        aliases[n_in + i] = n_out + o
    call = pl.pallas_call(
        hosted, name=name, grid=grid, in_specs=in_specs + [ANY] * nci, out_specs=out_specs + [ANY] * nco,
        out_shape=out_shape + comm.out_shapes, scratch_shapes=scratch_shapes + comm.sems,
        input_output_aliases=aliases,
        compiler_params=_params(("arbitrary",) * len(grid), blocks, temps, _collective_id(comm.peers)))

    def run(*args):
        res = call(*args, *comm.ins)
        return list(res[:n_out]), list(res[n_out:])

    return run


def _run_comm(comm, name):
    def body(*refs):
        nci, nco = len(comm.ins), len(comm.out_shapes)
        cins, couts, csems = refs[:nci], refs[nci:nci + nco], refs[nci + nco:]
        _handshake(comm.peers)
        comm.start(cins, couts, csems)
        if comm.mid is not None:
            comm.mid(cins, couts, csems)
        comm.finish(cins, couts, csems)

    return list(pl.pallas_call(
        body, name=name, in_specs=[ANY] * len(comm.ins), out_specs=[ANY] * len(comm.out_shapes),
        out_shape=comm.out_shapes, scratch_shapes=comm.sems, input_output_aliases=comm.aliases,
        compiler_params=pltpu.CompilerParams(collective_id=_collective_id(comm.peers)),
    )(*comm.ins))


def _dot(a, b):
    return jnp.dot(a, b, preferred_element_type=F32)


def _dot_tb(a, b):
    return lax.dot_general(a, b, (((1,), (1,)), ((), ())), preferred_element_type=F32)


def _dot_ta(a, b):
    return lax.dot_general(a, b, (((0,), (0,)), ((), ())), preferred_element_type=F32)


def _rms_fwd(x):
    inv = lax.rsqrt(jnp.mean(x * x, axis=-1, keepdims=True) + RMS_EPS)
    return x * inv, inv


def _rms_bwd(dy, xhat, inv, g):
    gd = dy * g
    return inv * (gd - xhat * jnp.mean(gd * xhat, axis=-1, keepdims=True))


def _sigmoid(x):
    return 1.0 / (1.0 + jnp.exp(-x))


def _shift_down(x, k, row):
    return jnp.where(row >= k, pltpu.roll(x, k, 0), 0.0)


def _shift_up(x, k, row):
    s = x.shape[0]
    return jnp.where(row < s - k, pltpu.roll(x, s - k, 0), 0.0)


def _pool_fwd(u, win, row):
    s = u
    k = 1
    while k < win:
        s = s + _shift_down(s, k, row)
        k *= 2
    cnt = jnp.minimum(row + 1, win).astype(F32)
    return s / cnt - u


def _pool_bwd(dp, win, row):
    cnt = jnp.minimum(row + 1, win).astype(F32)
    s = dp / cnt
    k = 1
    while k < win:
        s = s + _shift_up(s, k, row)
        k *= 2
    return s - dp


def _acc_over(k, nk, part, acc, o_ref):
    @pl.when(k == 0)
    def _():
        acc[...] = part

    @pl.when(k > 0)
    def _():
        acc[...] += part

    @pl.when(k == nk - 1)
    def _():
        o_ref[...] = acc[...].astype(o_ref.dtype)


def _fwd_in(x, g, w_loc, order, comm):
    t, d = x.shape
    ws = w_loc.shape[1]
    nsh = order.shape[0]
    assert nsh == 4, "the shard walk below is written for the 2 x 2 chips of the mesh"
    tm = _tile(t, 1024, SUBLANES_BF16)
    ni = t // tm
    nci, nco = len(comm.ins), len(comm.out_shapes)
    all_peers = comm.peers | frozenset(CHIPS + (SIBLING,))

    def body(order_ref, x_ref, g_ref, loc_ref, *rest):
        del order_ref
        cins = rest[:nci]
        z_ref, h_ref, full_ref = rest[nci:nci + 3]
        couts = rest[nci + 3:nci + 3 + nco]
        (hs, wbuf, wsem, own_s, own_r, snd_s, snd_r, fwd_s, fwd_r, rly_s, rly_r) = rest[nci + 3 + nco:nci + 14 + nco]
        csems = rest[nci + 14 + nco:]
        j = pl.program_id(0)
        i = pl.program_id(1)
        x_, y_, c_ = _coords()
        own = 2 * x_ + y_
        sib = (x_, y_, 1 - c_)
        peers = _peer_chips(x_, y_)

        def sends():
            return [_remote(_half(0, loc_ref, c_), _piece(0, full_ref, own, c_), snd_s.at[p], snd_r.at[p], (px, py, c_))
                    for p, (px, py) in enumerate(peers[:2])]

        def relays():
            out = []
            for q, (src_p, dst_p) in enumerate(((0, 1), (1, 0))):
                sx, sy = peers[src_p]
                part = _rows_part(0, _piece(0, full_ref, 2 * sx + sy, c_), (q, q + 1, 2))
                out.append(_remote(part, part, rly_s.at[q], rly_r.at[q], (*peers[dst_p], c_)))
            return out

        def owns():
            return [_remote(_half(0, loc_ref, h), _piece(0, full_ref, own, h), own_s.at[h], own_r.at[h], sib)
                    for h in range(2)]

        def forward(p, half):
            px, py = peers[p]
            landed = _piece(0, full_ref, 2 * px + py, half)
            return _remote(landed, landed, fwd_s.at[p], fwd_r.at[p], sib)

        def load(src, slot):
            return pltpu.make_async_copy(src, wbuf.at[slot], wsem.at[slot])

        @pl.when((j == 0) & (i == 0))
        def _():
            _handshake(all_peers)
            for cp in sends() + owns():
                cp.start()
            load(loc_ref, 0).start()
            comm.start(cins, couts, csems)

        @pl.when(j == 0)
        def _():
            xh, _ = _rms_fwd(x_ref[...])
            h = (xh * g_ref[...]).astype(BF16)
            hs[pl.ds(pl.multiple_of(i * tm, tm), tm), :] = h
            h_ref[...] = h

        slot = j % 2

        @pl.when(i == 0)
        def _():
            load(loc_ref, slot).wait()

        z_ref[...] = _dot(hs[pl.ds(pl.multiple_of(i * tm, tm), tm), :], wbuf[slot]).astype(BF16)

        def load_shard(p, into):
            px, py = peers[p]
            forward(p, 1 - c_).wait_recv()
            load(full_ref.at[2 * px + py], into).start()

        @pl.when((j == 0) & (i == ni - 1))
        def _():
            for cp in sends():
                cp.wait_recv()
            for cp in relays() + [forward(0, c_), forward(1, c_)]:
                cp.start()
            load_shard(0, 1)

        @pl.when((j == 1) & (i == 0))
        def _():
            load_shard(1, 0)

        @pl.when((j == 1) & (i == ni - 1))
        def _():
            for cp in relays():
                cp.wait_recv()
            forward(2, c_).start()
            load_shard(2, 1)

        @pl.when((j == nsh - 1) & (i == ni - 1))
        def _():
            for cp in sends() + relays() + [forward(p, c_) for p in range(nsh - 1)]:
                cp.wait_send()
            for cp in owns():
                cp.wait()
            comm.finish(cins, couts, csems)

    last = ni - 1
    blocks = _nbytes((tm, d), F32) + _nbytes((tm, ws), BF16) + _nbytes((tm, d), BF16)
    scratch = _nbytes((t, d), BF16) + 2 * _nbytes((d, ws), BF16)
    res = pl.pallas_call(
        body, name="fwd_in",
        grid_spec=pltpu.PrefetchScalarGridSpec(
            num_scalar_prefetch=1, grid=(nsh, ni),
            in_specs=[pl.BlockSpec((tm, d), lambda j, i, o: (jnp.where(j == 0, i, last), 0)),
                      pl.BlockSpec((1, d), lambda j, i, o: (0, 0)), ANY] + [ANY] * nci,
            out_specs=[pl.BlockSpec((tm, ws), lambda j, i, o: (i, o[j])),
                       pl.BlockSpec((tm, d), lambda j, i, o: (jnp.where(j == 0, i, last), 0)), ANY] + [ANY] * nco,
            scratch_shapes=[pltpu.VMEM((t, d), BF16), pltpu.VMEM((2, d, ws), BF16)]
            + _dma_sems(2, 2, 2, 2, 2, nsh - 1, nsh - 1, 2, 2) + comm.sems),
        out_shape=[SDS((t, nsh * ws), BF16), SDS((t, d), BF16), SDS((nsh, d, ws), BF16)] + comm.out_shapes,
        input_output_aliases={4 + i: 3 + o for i, o in comm.aliases.items()},
        compiler_params=_params(("arbitrary", "arbitrary"), blocks, scratch + 3 * _nbytes((tm, d), F32),
                                _collective_id(all_peers)),
    )(order, x, g, w_loc, *comm.ins)
    return list(res[:3]), list(res[3:])


def _mixer_mid_fwd(z, pool_w, pool_scale, conv_w, nseq, comm=None):
    t = z.shape[0]
    d = pool_scale.shape[1]
    s = t // nseq
    c = d // N_GROUPS

    def body(zp, zb, zc, zv, pw, ps, cw, o):
        j = pl.program_id(1)
        row = lax.broadcasted_iota(jnp.int32, (s, c), 0)
        for gi, win in enumerate(POOL_WINDOWS):
            @pl.when(j == gi)
            def _(win=win):
                pooled = _pool_fwd(zp[...].astype(F32), win, row)
                o[0] = (_dot(pooled.astype(BF16), pw[...]) * ps[...]).astype(BF16)

        cv = zc[...].astype(F32) * zv[...].astype(F32)
        cc = (cw[pl.ds(2, 1), :] * cv + cw[pl.ds(1, 1), :] * _shift_down(cv, 1, row)
              + cw[pl.ds(0, 1), :] * _shift_down(cv, 2, row))
        o[1] = (zb[...].astype(F32) * cc).astype(BF16)

    blocks = 4 * _nbytes((s, c), BF16) + _nbytes((c, c), BF16) + _nbytes((2, s, c), BF16)
    return _pcall(
        body, name="mixer_mid_fwd", grid=(nseq, N_GROUPS),
        in_specs=[pl.BlockSpec((s, c), lambda b, j: (b, j)),
                  pl.BlockSpec((s, c), lambda b, j: (b, N_GROUPS + j)),
                  pl.BlockSpec((s, c), lambda b, j: (b, 2 * N_GROUPS + j)),
                  pl.BlockSpec((s, c), lambda b, j: (b, 3 * N_GROUPS + j)),
                  pl.BlockSpec((None, c, c), lambda b, j: (j, 0, 0)),
                  pl.BlockSpec((1, c), lambda b, j: (0, j)),
                  pl.BlockSpec((3, c), lambda b, j: (0, j))],
        out_specs=[pl.BlockSpec((2, s, c), lambda b, j: (0, b, j))],
        out_shape=[SDS((3, t, d), BF16)],
        sem=("parallel", "parallel"), blocks=blocks, temps=8 * _nbytes((s, c), F32), comm=comm,
    )(z, z, z, z, pool_w, pool_scale, conv_w)


def _mixer_out(lhs3, z, x, w3, g_ffn, comm=None):
    t, d = x.shape
    tm = _tile(t, 256, SUBLANES_BF16)

    def body(pq, zgp, zgc, x_ref, w_ref, g_ref, mrg, ypc, x1o, h2o):
        yp = _dot(pq[0], w_ref[0])
        yc = _dot(pq[1], w_ref[1])
        m = _sigmoid(zgp[...].astype(F32)) * yp + _sigmoid(zgc[...].astype(F32)) * yc
        mb = m.astype(BF16)
        x1 = x_ref[...] + _dot(mb, w_ref[2])
        ypc[0] = yp.astype(BF16)
        ypc[1] = yc.astype(BF16)
        mrg[...] = mb
        x1o[...] = x1
        xh, _ = _rms_fwd(x1)
        h2o[...] = (xh * g_ref[...]).astype(BF16)

    blocks = (_nbytes((2, tm, d), BF16) * 2 + _nbytes((tm, d), BF16) * 4 + _nbytes((tm, d), F32) * 2
              + _nbytes((3, d, d), BF16))
    return _pcall(
        body, name="mixer_out", grid=(t // tm,),
        in_specs=[pl.BlockSpec((2, tm, d), lambda i: (0, i, 0)),
                  pl.BlockSpec((tm, d), lambda i: (i, 4)),
                  pl.BlockSpec((tm, d), lambda i: (i, 5)),
                  pl.BlockSpec((tm, d), lambda i: (i, 0)),
                  pl.BlockSpec((3, d, d), lambda i: (0, 0, 0)),
                  pl.BlockSpec((1, d), lambda i: (0, 0))],
        out_specs=[pl.BlockSpec((None, tm, d), lambda i: (2, i, 0)),
                   pl.BlockSpec((2, tm, d), lambda i: (0, i, 0)),
                   pl.BlockSpec((tm, d), lambda i: (i, 0)),
                   pl.BlockSpec((tm, d), lambda i: (i, 0))],
        out_shape=[SDS(lhs3.shape, BF16), SDS((2, t, d), BF16), SDS((t, d), F32), SDS((t, d), BF16)],
        input_output_aliases={0: 0},
        sem=("parallel",), blocks=blocks, temps=8 * _nbytes((tm, d), F32), comm=comm,
    )(lhs3, z, z, x, w3, g_ffn)


def _ffn_up(h2, w_up, f, comm=None):
    t, d = h2.shape
    _, _, ws = w_up.shape
    tm = _tile(t, 1024, SUBLANES_BF16)
    tn = _tile(ws, 1408, LANES)
    nps = ws // tn
    npp = f // tn

    def body(h_ref, w_ref, o_ref):
        o_ref[...] = _dot(h_ref[...], w_ref[...]).astype(BF16)

    blocks = _nbytes((tm, d), BF16) + _nbytes((d, tn), BF16) + _nbytes((tm, tn), BF16)
    return _pcall(
        body, name="ffn_up", grid=(t // tm, 2 * npp),
        in_specs=[pl.BlockSpec((tm, d), lambda i, j: (i, 0)),
                  pl.BlockSpec((None, d, tn), lambda i, j: (j // nps, 0, j % nps))],
        out_specs=[pl.BlockSpec((None, tm, tn), lambda i, j: (j // npp, i, j % npp))],
        out_shape=[SDS((2, t, f), BF16)],
        sem=("parallel", "parallel"), blocks=blocks, temps=_nbytes((tm, tn), F32), comm=comm,
    )(h2, w_up)


def _conv3_rows(u, u1, u2, w_ref, p):
    return w_ref[p, pl.ds(2, 1), :] * u + w_ref[p, pl.ds(1, 1), :] * u1 + w_ref[p, pl.ds(0, 1), :] * u2


WGRAD_TOKENS = 2048
WGRAD_TOKENS_WIDE = 4096
CHUNK = 64
HALO = SUBLANES_F32


def _up1_up2(u, nxt):
    rows = u.shape[0]
    ext = jnp.concatenate([u, nxt], axis=0)
    n = rows + HALO
    return pltpu.roll(ext, n - 1, 0)[:rows], pltpu.roll(ext, n - 2, 0)[:rows]


def _fold8(x):
    return jnp.sum(x.reshape(x.shape[0] // SUBLANES_F32, SUBLANES_F32, x.shape[1]), axis=0)


def _ffn_mid_fwd(u0, cw, cb, nseq):
    _, t, f = u0.shape
    s = t // nseq
    c = _tile(f, 256, LANES)

    def body(u_ref, w_ref, b_ref, a_ref, uo_ref):
        row = lax.broadcasted_iota(jnp.int32, (s, c), 0)
        act = []
        for p in range(2):
            u = u_ref[p].astype(F32)
            act.append(_conv3_rows(u, _shift_down(u, 1, row), _shift_down(u, 2, row), w_ref, p) + b_ref[p])
            uo_ref[p] = act[p].astype(BF16)
        ug, uv = act
        a_ref[...] = (ug * _sigmoid(ug) * uv).astype(BF16)

    blocks = 2 * _nbytes((2, s, c), BF16) + _nbytes((s, c), BF16)
    outs, _ = _pcall(
        body, name="ffn_mid_fwd", grid=(f // c, nseq),
        in_specs=[pl.BlockSpec((2, s, c), lambda j, b: (0, b, j)),
                  pl.BlockSpec((2, 3, c), lambda j, b: (0, 0, j)),
                  pl.BlockSpec((2, 1, c), lambda j, b: (0, 0, j))],
        out_specs=[pl.BlockSpec((s, c), lambda j, b: (b, j)),
                   pl.BlockSpec((2, s, c), lambda j, b: (0, b, j))],
        out_shape=[SDS((t, f), BF16), SDS((2, t, f), BF16)],
        sem=("parallel", "parallel"), blocks=blocks, temps=8 * _nbytes((s, c), F32),
    )(u0, cw, cb)
    return outs


def _ffn_down_loss(a, w_down, x1, tgt, g_fin):
    t, f = a.shape
    d = x1.shape[1]
    tm = _tile(t, 256, SUBLANES_BF16)
    nsteps = t // tm

    def body(a_ref, w_ref, x1_ref, t_ref, g_ref, dx_ref, dxb_ref, loss_ref, gg_ref, lacc):
        i = pl.program_id(0)

        @pl.when(i == 0)
        def _():
            lacc[...] = jnp.zeros_like(lacc)
            gg_ref[...] = jnp.zeros_like(gg_ref)

        x2 = x1_ref[...] + _dot(a_ref[...], w_ref[...])
        xh, inv = _rms_fwd(x2)
        g = g_ref[...]
        e = xh * g - t_ref[...]
        lacc[...] += jnp.sum(e * e, axis=0, keepdims=True)
        dy = e * (1.0 / d)
        gg_ref[...] += jnp.sum(dy * xh, axis=0, keepdims=True)
        dx2 = _rms_bwd(dy, xh, inv, g)
        dx_ref[...] = dx2
        dxb_ref[...] = dx2.astype(BF16)

        @pl.when(i == nsteps - 1)
        def _():
            loss_ref[...] = jnp.sum(lacc[...], axis=1, keepdims=True) * (0.5 / d)

    blocks = (_nbytes((tm, f), BF16) + _nbytes((f, d), BF16) + 3 * _nbytes((tm, d), F32) + _nbytes((tm, d), BF16))
    outs, _ = _pcall(
        body, name="ffn_down_loss", grid=(nsteps,),
        in_specs=[pl.BlockSpec((tm, f), lambda i: (i, 0)), pl.BlockSpec((f, d), lambda i: (0, 0)),
                  pl.BlockSpec((tm, d), lambda i: (i, 0)), pl.BlockSpec((tm, d), lambda i: (i, 0)),
                  pl.BlockSpec((1, d), lambda i: (0, 0))],
        out_specs=[pl.BlockSpec((tm, d), lambda i: (i, 0)), pl.BlockSpec((tm, d), lambda i: (i, 0)),
                   pl.BlockSpec((1, 1), lambda i: (0, 0)), pl.BlockSpec((1, d), lambda i: (0, 0))],
        out_shape=[SDS((t, d), F32), SDS((t, d), BF16), SDS((1, 1), F32), SDS((1, d), F32)],
        scratch_shapes=[pltpu.VMEM((1, d), F32)],
        sem=("arbitrary",), blocks=blocks, temps=8 * _nbytes((tm, d), F32),
    )(a, w_down, x1, tgt, g_fin)
    return outs


def _ffn_bwd_da(dxb, w_down, comm=None):
    t, d = dxb.shape
    f = w_down.shape[0]
    tm = _tile(t, 512, SUBLANES_BF16)

    def body(x_ref, w_ref, o_ref):
        o_ref[...] = _dot_tb(x_ref[...], w_ref[...]).astype(BF16)

    blocks = _nbytes((tm, d), BF16) + _nbytes((tm, f), BF16)
    return _pcall(
        body, name="ffn_bwd_da", grid=(t // tm,),
        in_specs=[pl.BlockSpec((tm, d), lambda i: (i, 0)),
                  pl.BlockSpec((f, d), lambda i: (0, 0), pipeline_mode=pl.Buffered(1))],
        out_specs=[pl.BlockSpec((tm, f), lambda i: (i, 0))],
        out_shape=[SDS((t, f), BF16)],
        sem=("parallel",), blocks=blocks, temps=_nbytes((f, d), BF16) + _nbytes((tm, f), F32), comm=comm,
    )(dxb, w_down)


def _ffn_mid_bwd(da, u0, ua, cw, nseq, comm=None):
    _, t, f = u0.shape
    s = t // nseq
    c = _tile(f, 128, LANES)
    r = _tile(s, CHUNK, SUBLANES_BF16)
    n = s // r

    def body(da_ref, u_ref, ua_ref, w_ref, du_ref, gw_ref, gb_ref):
        @pl.when(pl.program_id(1) == 0)
        def _():
            gw_ref[...] = jnp.zeros_like(gw_ref)
            gb_ref[...] = jnp.zeros_like(gb_ref)

        def step(i, carry):
            nxt, sums = carry
            rows = pl.ds(pl.multiple_of((n - 1 - i) * r, r), r)
            ug = ua_ref[0, rows, :].astype(F32)
            uv = ua_ref[1, rows, :].astype(F32)
            sg = _sigmoid(ug)
            dacc = da_ref[rows, :].astype(F32)
            dus = (dacc * uv * sg * (1.0 + ug * (1.0 - sg)), dacc * (ug * sg))
            first, new_sums = [], []
            for p in range(2):
                du = dus[p]
                d1, d2 = _up1_up2(du, nxt[p])
                du_ref[p, rows, :] = _conv3_rows(du, d1, d2, w_ref, p).astype(BF16)
                u = u_ref[p, rows, :].astype(F32)
                sb, s0, s1, s2 = sums[p]
                new_sums.append((sb + _fold8(du), s0 + _fold8(d2 * u), s1 + _fold8(d1 * u), s2 + _fold8(du * u)))
                first.append(du[:HALO])
            return tuple(first), tuple(new_sums)

        zero = jnp.zeros((HALO, c), F32)
        _, sums = lax.fori_loop(0, n, step, ((zero, zero), ((zero,) * 4,) * 2))
        for p in range(2):
            sb, s0, s1, s2 = sums[p]
            gb_ref[p] += jnp.sum(sb, axis=0, keepdims=True)
            gw_ref[p, pl.ds(0, 1), :] += jnp.sum(s0, axis=0, keepdims=True)
            gw_ref[p, pl.ds(1, 1), :] += jnp.sum(s1, axis=0, keepdims=True)
            gw_ref[p, pl.ds(2, 1), :] += jnp.sum(s2, axis=0, keepdims=True)

    blocks = _nbytes((s, c), BF16) + 3 * _nbytes((2, s, c), BF16)
    return _pcall(
        body, name="ffn_mid_bwd", grid=(f // c, nseq),
        in_specs=[pl.BlockSpec((s, c), lambda j, b: (b, j)),
                  pl.BlockSpec((2, s, c), lambda j, b: (0, b, j)),
                  pl.BlockSpec((2, s, c), lambda j, b: (0, b, j)),
                  pl.BlockSpec((2, 3, c), lambda j, b: (0, 0, j))],
        out_specs=[pl.BlockSpec((2, s, c), lambda j, b: (0, b, j)),
                   pl.BlockSpec((2, 3, c), lambda j, b: (0, 0, j)),
                   pl.BlockSpec((2, 1, c), lambda j, b: (0, 0, j))],
        out_shape=[SDS((2, t, f), BF16), SDS((2, 3, f), F32), SDS((2, 1, f), F32)],
        sem=("parallel", "arbitrary"), blocks=blocks, temps=4 * 1024 * 1024, comm=comm,
    )(da, u0, ua, cw)


def _wgrad(a, b, name, *, tr, tn, b_plane_of=None, out_shards=None, comm=None):
    t, m = a.shape
    n_total = b.shape[-1] * (b.shape[0] if b.ndim == 3 else 1)
    tk = _tile(t, WGRAD_TOKENS_WIDE if n_total > tn and m == tr else WGRAD_TOKENS, SUBLANES_BF16)
    nk = t // tk
    once = pl.Buffered(1) if nk == 1 else None

    def body(a_ref, b_ref, o_ref, *acc):
        part = _dot_ta(a_ref[...], b_ref[...])
        if nk == 1:
            o_ref[...] = part.astype(BF16)
        else:
            _acc_over(pl.program_id(2), nk, part, acc[0], o_ref)

    if b.ndim == 3:
        b_spec = pl.BlockSpec((None, tk, tn), lambda r, n, k: (b_plane_of(n)[0], k, b_plane_of(n)[1]))
    else:
        b_spec = pl.BlockSpec((tk, tn), lambda r, n, k: (k, n), pipeline_mode=once if n_total == tn else None)
    if out_shards is None:
        o_spec = pl.BlockSpec((tr, tn), lambda r, n, k: (r, n))
        o_shape = SDS((m, n_total), BF16)
    else:
        nps = n_total // out_shards // tn
        o_spec = pl.BlockSpec((None, tr, tn), lambda r, n, k: (n // nps, r, n % nps))
        o_shape = SDS((out_shards, m, n_total // out_shards), BF16)
    blocks = _nbytes((tk, tr), BF16) + _nbytes((tk, tn), BF16) + _nbytes((tr, tn), BF16)
    return _pcall(
        body, name=name, grid=(m // tr, n_total // tn, nk),
        in_specs=[pl.BlockSpec((tk, tr), lambda r, n, k: (k, r), pipeline_mode=once if m == tr else None), b_spec],
        out_specs=[o_spec], out_shape=[o_shape],
        scratch_shapes=[] if nk == 1 else [pltpu.VMEM((tr, tn), F32)],
        sem=("parallel", "parallel", "arbitrary"), blocks=blocks, temps=2 * _nbytes((tr, tn), F32), comm=comm,
    )(a, b)


def _wgrad3(lhs3, rhs3, comm=None):
    nw, t, d = lhs3.shape
    tk = _tile(t, WGRAD_TOKENS, SUBLANES_BF16)
    nk = t // tk

    def body(a_ref, b_ref, o_ref, *acc):
        part = _dot_ta(a_ref[...], b_ref[...])
        if nk == 1:
            o_ref[...] = part.astype(BF16)
        else:
            _acc_over(pl.program_id(1), nk, part, acc[0], o_ref)

    blocks = 2 * _nbytes((tk, d), BF16) + _nbytes((d, d), BF16)
    return _pcall(
        body, name="wgrad_sq3", grid=(nw, nk),
        in_specs=[pl.BlockSpec((None, tk, d), lambda w, k: (w, k, 0)),
                  pl.BlockSpec((None, tk, d), lambda w, k: (w, k, 0))],
        out_specs=[pl.BlockSpec((None, d, d), lambda w, k: (w, 0, 0))],
        out_shape=[SDS((nw, d, d), BF16)],
        scratch_shapes=[] if nk == 1 else [pltpu.VMEM((d, d), F32)],
        sem=("parallel", "arbitrary"), blocks=blocks, temps=2 * _nbytes((d, d), F32), comm=comm,
    )(lhs3, rhs3)


def _ffn_bwd_dx1(du0, w_up, x1, dx2, g_ffn, n_planes_out, comm=None):
    _, t, f = du0.shape
    d = x1.shape[1]
    nsh, _, ws = w_up.shape
    tm = _tile(t, 256, SUBLANES_BF16)
    spp = f // ws

    def body(du_ref, w_ref, x1_ref, dx2_ref, g_ref, dx1_ref, dxb_ref, gg_ref):
        @pl.when(pl.program_id(0) == 0)
        def _():
            gg_ref[...] = jnp.zeros_like(gg_ref)

        dh = None
        for k in range(nsh):
            part = _dot_tb(du_ref[k // spp, :, (k % spp) * ws:(k % spp + 1) * ws], w_ref[k])
            dh = part if dh is None else dh + part
        xh, inv = _rms_fwd(x1_ref[...])
        gg_ref[...] += jnp.sum(dh * xh, axis=0, keepdims=True)
        dx1 = dx2_ref[...] + _rms_bwd(dh, xh, inv, g_ref[...])
        dx1_ref[...] = dx1
        dxb_ref[...] = dx1.astype(BF16)

    blocks = _nbytes((2, tm, f), BF16) + 3 * _nbytes((tm, d), F32) + _nbytes((tm, d), BF16)
    return _pcall(
        body, name="ffn_bwd_dx1", grid=(t // tm,),
        in_specs=[pl.BlockSpec((2, tm, f), lambda i: (0, i, 0)),
                  pl.BlockSpec((nsh, d, ws), lambda i: (0, 0, 0), pipeline_mode=pl.Buffered(1)),
                  pl.BlockSpec((tm, d), lambda i: (i, 0)),
                  pl.BlockSpec((tm, d), lambda i: (i, 0)),
                  pl.BlockSpec((1, d), lambda i: (0, 0))],
        out_specs=[pl.BlockSpec((tm, d), lambda i: (i, 0)),
                   pl.BlockSpec((None, tm, d), lambda i: (n_planes_out - 1, i, 0)),
                   pl.BlockSpec((1, d), lambda i: (0, 0))],
        out_shape=[SDS((t, d), F32), SDS((n_planes_out, t, d), BF16), SDS((1, d), F32)],
        sem=("arbitrary",), blocks=blocks, temps=_nbytes(w_up.shape, BF16) + 8 * _nbytes((tm, d), F32), comm=comm,
    )(du0, w_up, x1, dx2, g_ffn)


def _mixer_bwd(rhs3, z, ypc, w3, comm=None):
    _, t, d = rhs3.shape
    tm = _tile(t, 256, SUBLANES_BF16)

    def body(dx_ref, zgp, zgc, ypc_ref, w_ref, dyo, dzo, dpq):
        dm = _dot_tb(dx_ref[...], w_ref[2])
        sp = _sigmoid(zgp[...].astype(F32))
        sc = _sigmoid(zgc[...].astype(F32))
        dyp = (dm * sp).astype(BF16)
        dyc = (dm * sc).astype(BF16)
        dzo[0] = (dm * ypc_ref[0].astype(F32) * sp * (1.0 - sp)).astype(BF16)
        dzo[1] = (dm * ypc_ref[1].astype(F32) * sc * (1.0 - sc)).astype(BF16)
        dyo[0] = dyp
        dyo[1] = dyc
        dpq[0] = _dot_tb(dyp, w_ref[0]).astype(BF16)
        dpq[1] = _dot_tb(dyc, w_ref[1]).astype(BF16)

    blocks = _nbytes((tm, d), BF16) * 3 + _nbytes((2, tm, d), BF16) * 4 + _nbytes((3, d, d), BF16)
    return _pcall(
        body, name="mixer_bwd", grid=(t // tm,),
        in_specs=[pl.BlockSpec((None, tm, d), lambda i: (2, i, 0)),
                  pl.BlockSpec((tm, d), lambda i: (i, 4)),
                  pl.BlockSpec((tm, d), lambda i: (i, 5)),
                  pl.BlockSpec((2, tm, d), lambda i: (0, i, 0)),
                  pl.BlockSpec((3, d, d), lambda i: (0, 0, 0))],
        out_specs=[pl.BlockSpec((2, tm, d), lambda i: (0, i, 0)),
                   pl.BlockSpec((2, tm, d), lambda i: (2, i, 0)),
                   pl.BlockSpec((2, tm, d), lambda i: (0, i, 0))],
        out_shape=[SDS(rhs3.shape, BF16), SDS((N_SPLITS, t, d), BF16), SDS((2, t, d), BF16)],
        input_output_aliases={0: 0},
        sem=("parallel",), blocks=blocks, temps=8 * _nbytes((tm, d), F32), comm=comm,
    )(rhs3, z, z, ypc, w3)


def _conv_bwd(dz, dpq, z, conv_w, nseq, comm=None):
    _, t, d = dz.shape
    s = t // nseq
    c = _tile(d, 128, LANES)
    nb = d // c

    def body(dz_in, dq_ref, zb, zc, zv, cw, dzo, gw_ref):
        del dz_in

        @pl.when(pl.program_id(1) == 0)
        def _():
            gw_ref[...] = jnp.zeros_like(gw_ref)

        row = lax.broadcasted_iota(jnp.int32, (s, c), 0)
        b = zb[...].astype(F32)
        cm = zc[...].astype(F32)
        v = zv[...].astype(F32)
        cv = cm * v
        cv1 = _shift_down(cv, 1, row)
        cv2 = _shift_down(cv, 2, row)
        w0, w1, w2 = cw[pl.ds(0, 1), :], cw[pl.ds(1, 1), :], cw[pl.ds(2, 1), :]
        cc = w2 * cv + w1 * cv1 + w0 * cv2
        dq = dq_ref[...].astype(F32)
        dzo[0] = (dq * cc).astype(BF16)
        dcc = dq * b
        gw_ref[pl.ds(0, 1), :] += jnp.sum(dcc * cv2, axis=0, keepdims=True)
        gw_ref[pl.ds(1, 1), :] += jnp.sum(dcc * cv1, axis=0, keepdims=True)
        gw_ref[pl.ds(2, 1), :] += jnp.sum(dcc * cv, axis=0, keepdims=True)
        dcv = w2 * dcc + w1 * _shift_up(dcc, 1, row) + w0 * _shift_up(dcc, 2, row)
        dzo[1] = (dcv * v).astype(BF16)
        dzo[2] = (dcv * cm).astype(BF16)

    blocks = 4 * _nbytes((s, c), BF16) + _nbytes((3, s, c), BF16)
    return _pcall(
        body, name="conv_bwd", grid=(nb, nseq),
        in_specs=[ANY,
                  pl.BlockSpec((None, s, c), lambda j, b: (1, b, j)),
                  pl.BlockSpec((s, c), lambda j, b: (b, nb + j)),
                  pl.BlockSpec((s, c), lambda j, b: (b, 2 * nb + j)),
                  pl.BlockSpec((s, c), lambda j, b: (b, 3 * nb + j)),
                  pl.BlockSpec((3, c), lambda j, b: (0, j))],
        out_specs=[pl.BlockSpec((3, s, c), lambda j, b: (0, b, j)),
                   pl.BlockSpec((3, c), lambda j, b: (0, j))],
        out_shape=[SDS(dz.shape, BF16), SDS((3, d), F32)],
        input_output_aliases={0: 0},
        sem=("parallel", "arbitrary"), blocks=blocks, temps=16 * _nbytes((s, c), F32), comm=comm,
    )(dz, dpq, z, z, z, conv_w)


def _pool_bwd_call(dz, dpq, z, pool_w, pool_scale, nseq, comm=None):
    _, t, d = dz.shape
    s = t // nseq
    c = d // N_GROUPS

    def body(dz_in, dp_ref, zp, pw, ps, dzo, gpw_ref, gps_ref):
        del dz_in
        j = pl.program_id(0)

        @pl.when(pl.program_id(1) == 0)
        def _():
            gpw_ref[...] = jnp.zeros_like(gpw_ref)
            gps_ref[...] = jnp.zeros_like(gps_ref)

        row = lax.broadcasted_iota(jnp.int32, (s, c), 0)
        for gi, win in enumerate(POOL_WINDOWS):
            @pl.when(j == gi)
            def _(win=win):
                pb = _pool_fwd(zp[...].astype(F32), win, row).astype(BF16)
                plin = _dot(pb, pw[...])
                dps = dp_ref[...].astype(F32)
                gps_ref[...] += jnp.sum(dps * plin, axis=0, keepdims=True)
                dplb = (dps * ps[...]).astype(BF16)
                gpw_ref[...] += _dot_ta(pb, dplb)
                dzo[...] = _pool_bwd(_dot_tb(dplb, pw[...]), win, row).astype(BF16)

    blocks = 3 * _nbytes((s, c), BF16) + _nbytes((c, c), BF16) + _nbytes((c, c), F32)
    return _pcall(
        body, name="pool_bwd", grid=(N_GROUPS, nseq),
        in_specs=[ANY,
                  pl.BlockSpec((None, s, c), lambda j, b: (0, b, j)),
                  pl.BlockSpec((s, c), lambda j, b: (b, j)),
                  pl.BlockSpec((None, c, c), lambda j, b: (j, 0, 0)),
                  pl.BlockSpec((1, c), lambda j, b: (0, j))],
        out_specs=[pl.BlockSpec((None, s, c), lambda j, b: (3, b, j)),
                   pl.BlockSpec((None, c, c), lambda j, b: (j, 0, 0)),
                   pl.BlockSpec((1, c), lambda j, b: (0, j))],
        out_shape=[SDS(dz.shape, BF16), SDS((N_GROUPS, c, c), F32), SDS((1, d), F32)],
        input_output_aliases={0: 0},
        sem=("parallel", "arbitrary"), blocks=blocks, temps=10 * _nbytes((s, c), F32), comm=comm,
    )(dz, dpq, z, pool_w, pool_scale)


def _dz_plane(zb):
    return jnp.where(zb < 4, (zb + 3) % 4, zb)


def _wgrad_in(h1, dz, nsh, comm=None):
    t, d = h1.shape
    ws = N_SPLITS * d // nsh
    kb = _tile(math.gcd(d, ws), 512, LANES)
    npl = d // kb
    nps = ws // kb
    tk = _tile(t, WGRAD_TOKENS_WIDE, SUBLANES_BF16)
    nk = t // tk

    def body(a_ref, b_ref, o_ref, *acc):
        part = _dot_ta(a_ref[...], b_ref[...])
        if nk == 1:
            o_ref[...] = part.astype(BF16)
        else:
            _acc_over(pl.program_id(1), nk, part, acc[0], o_ref)

    blocks = _nbytes((tk, d), BF16) + _nbytes((tk, kb), BF16) + _nbytes((d, kb), BF16)
    return _pcall(
        body, name="wgrad_in", grid=(N_SPLITS * npl, nk),
        in_specs=[pl.BlockSpec((tk, d), lambda cb, k: (k, 0), pipeline_mode=pl.Buffered(1) if nk == 1 else None),
                  pl.BlockSpec((None, tk, kb), lambda cb, k: (_dz_plane(cb // npl), k, cb % npl))],
        out_specs=[pl.BlockSpec((None, d, kb), lambda cb, k: (cb // nps, 0, cb % nps))],
        out_shape=[SDS((nsh, d, ws), BF16)],
        scratch_shapes=[] if nk == 1 else [pltpu.VMEM((d, kb), F32)],
        sem=("parallel", "arbitrary"), blocks=blocks, temps=2 * _nbytes((d, kb), F32), comm=comm,
    )(h1, dz)


def _mixer_bwd_dx(dz, w_in, x, dx1, g_mix, comm=None):
    npln, t, d = dz.shape
    nsh, _, ws = w_in.shape
    tm = _tile(t, 256, SUBLANES_BF16)
    kb = _tile(math.gcd(d, ws), 512, LANES)
    npl = d // kb
    nps = ws // kb

    def body(dz_ref, w_ref, x_ref, dx1_ref, g_ref, dx_ref, gg_ref):
        @pl.when(pl.program_id(0) == 0)
        def _():
            gg_ref[...] = jnp.zeros_like(gg_ref)

        dh = None
        for cb in range(npln * npl):
            zb = cb // npl
            plane = (zb + 3) % 4 if zb < 4 else zb
            part = _dot_tb(dz_ref[plane, :, (cb % npl) * kb:(cb % npl + 1) * kb],
                           w_ref[cb // nps, :, (cb % nps) * kb:(cb % nps + 1) * kb])
            dh = part if dh is None else dh + part
        xh, inv = _rms_fwd(x_ref[...])
        gg_ref[...] += jnp.sum(dh * xh, axis=0, keepdims=True)
        dx_ref[...] = dx1_ref[...] + _rms_bwd(dh, xh, inv, g_ref[...])

    blocks = _nbytes((npln, tm, d), BF16) + 3 * _nbytes((tm, d), F32)
    return _pcall(
        body, name="mixer_bwd_dx", grid=(t // tm,),
        in_specs=[pl.BlockSpec((npln, tm, d), lambda i: (0, i, 0)),
                  pl.BlockSpec((nsh, d, ws), lambda i: (0, 0, 0), pipeline_mode=pl.Buffered(1)),
                  pl.BlockSpec((tm, d), lambda i: (i, 0)),
                  pl.BlockSpec((tm, d), lambda i: (i, 0)),
                  pl.BlockSpec((1, d), lambda i: (0, 0))],
        out_specs=[pl.BlockSpec((tm, d), lambda i: (i, 0)),
                   pl.BlockSpec((1, d), lambda i: (0, 0))],
        out_shape=[SDS((t, d), F32), SDS((1, d), F32)],
        sem=("arbitrary",), blocks=blocks, temps=_nbytes(w_in.shape, BF16) + 8 * _nbytes((tm, d), F32), comm=comm,
    )(dz, w_in, x, dx1, g_mix)


N_BIG = 5
SHARD_MAJOR = (0, 2)
ROWS_DIM1 = (1, 4)


def _ds(start, size, align):
    if isinstance(start, int):
        return pl.ds(start, size)
    return pl.ds(pl.multiple_of(start, align), size)


def _piece(a, ref, k, h):
    if a in SHARD_MAJOR:
        r = ref.shape[1] // 2
        return ref.at[k, _ds(h * r, r, SUBLANES_BF16), :]
    if a in ROWS_DIM1:
        r = ref.shape[1] // 8
        return ref.at[:, _ds((2 * k + h) * r, r, SUBLANES_BF16), :]
    r = ref.shape[0] // 8
    return ref.at[_ds((2 * k + h) * r, r, SUBLANES_BF16), :]


def _half(a, ref, h):
    if a in ROWS_DIM1:
        r = ref.shape[1] // 2
        return ref.at[:, _ds(h * r, r, SUBLANES_BF16), :]
    r = ref.shape[0] // 2
    return ref.at[_ds(h * r, r, SUBLANES_BF16), :]


def _piece_shape(a, full_shape):
    if a in SHARD_MAJOR:
        return (full_shape[1] // 2, full_shape[2])
    if a in ROWS_DIM1:
        return (full_shape[0], full_shape[1] // 8, full_shape[2])
    return (full_shape[0] // 8, full_shape[1])


def _shard_shape(a, full_shape):
    if a in SHARD_MAJOR:
        return (full_shape[1], full_shape[2])
    if a in ROWS_DIM1:
        return (full_shape[0], full_shape[1] // 4, full_shape[2])
    return (full_shape[0] // 4, full_shape[1])


def _rows_axis(a):
    return 1 if a in ROWS_DIM1 else 0


def _piece_block(a, full_shape):
    ps = _piece_shape(a, full_shape)
    if a in SHARD_MAJOR:
        return (None,) + ps, lambda k, c: (k, c, 0)
    if a in ROWS_DIM1:
        return ps, lambda k, c: (0, 2 * k + c, 0)
    return ps, lambda k, c: (2 * k + c, 0)


def _coords():
    return lax.axis_index("x"), lax.axis_index("y"), lax.axis_index("c")


def _peer_chips(x, y):
    return [(1 - x, y), (x, 1 - y), (1 - x, 1 - y)]


def _remote(src, dst, ssem, rsem, dev):
    return pltpu.make_async_remote_copy(src_ref=src, dst_ref=dst, send_sem=ssem, recv_sem=rsem,
                                        device_id=dev, device_id_type=MESH)


def _dma_sems(*counts):
    return [pltpu.SemaphoreType.DMA((n,)) for n in counts]


def _symmetric(ins, out_shapes, sems, copies, peers, aliases=None):
    def start(cins, couts, csems):
        for cp in copies(cins, couts, csems):
            cp.start()

    def finish(cins, couts, csems):
        for cp in copies(cins, couts, csems):
            cp.wait()

    return _Comm(ins, out_shapes, sems, start, finish, peers, aliases)


def _rows_part(a, ref, part):
    if part is None:
        return ref
    p, q, n = part
    ax = _rows_axis(a)
    r = ref.shape[ax] // n
    return ref.at[tuple(pl.ds(p * r, (q - p) * r) if d == ax else slice(None) for d in range(len(ref.shape)))]


def _merge(comms):
    ins, outs, sems, aliases, spans = [], [], [], {}, []
    for cm in comms:
        spans.append((len(ins), len(outs), len(sems)))
        for i, o in cm.aliases.items():
            aliases[len(ins) + i] = len(outs) + o
        ins += cm.ins
        outs += cm.out_shapes
        sems += cm.sems

    def each(fn_name):
        def run(cins, couts, csems):
            for cm, (i0, o0, s0) in zip(comms, spans):
                fn = getattr(cm, fn_name)
                if fn is not None:
                    fn(cins[i0:i0 + len(cm.ins)], couts[o0:o0 + len(cm.out_shapes)], csems[s0:s0 + len(cm.sems)])
        return run

    return _Comm(ins, outs, sems, each("start"), each("finish"), frozenset().union(*[cm.peers for cm in comms]),
                 aliases, mid=each("mid") if any(cm.mid is not None for cm in comms) else None)


def _gather_comm(arrs, locs, full_shapes, part=None, into=None):
    n = len(arrs)

    def own(cins, couts, csems):
        x, y, c = _coords()
        j = 2 * x + y
        return [_remote(_rows_part(a, _half(a, cins[q], h), part), _rows_part(a, _piece(a, couts[q], j, h), part),
                        csems[0].at[2 * q + h], csems[1].at[2 * q + h], (x, y, 1 - c))
                for q, a in enumerate(arrs) for h in range(2)]

    def sends(cins, couts, csems):
        x, y, c = _coords()
        j = 2 * x + y
        return [_remote(_rows_part(a, _half(a, cins[q], c), part), _rows_part(a, _piece(a, couts[q], j, c), part),
                        csems[2].at[3 * q + i], csems[3].at[3 * q + i], (px, py, c))
                for q, a in enumerate(arrs) for i, (px, py) in enumerate(_peer_chips(x, y))]

    def forwards(couts, csems, half_of):
        x, y, c = _coords()
        out = []
        for q, a in enumerate(arrs):
            for i, (px, py) in enumerate(_peer_chips(x, y)):
                landed = _rows_part(a, _piece(a, couts[q], 2 * px + py, half_of(c)), part)
                out.append(_remote(landed, landed, csems[4].at[3 * q + i], csems[5].at[3 * q + i], (x, y, 1 - c)))
        return out

    def start(cins, couts, csems):
        for cp in sends(cins, couts, csems) + own(cins, couts, csems):
            cp.start()

    def finish(cins, couts, csems):
        fw = forwards(couts, csems, lambda c: c)
        for cp, f in zip(sends(cins, couts, csems), fw):
            cp.wait_recv()
            f.start()
        for f in forwards(couts, csems, lambda c: 1 - c):
            f.wait_recv()
        for cp in sends(cins, couts, csems) + fw:
            cp.wait_send()
        for cp in own(cins, couts, csems):
            cp.wait()

    ins = [locs[a] for a in arrs] + ([into[a] for a in arrs] if into else [])
    return _Comm(ins, [SDS(full_shapes[a], BF16) for a in arrs],
                 _dma_sems(2 * n, 2 * n, 3 * n, 3 * n, 3 * n, 3 * n), start, finish, CHIPS + (SIBLING,),
                 aliases={n + q: q for q in range(n)} if into else None)


def _ring_gather_comm(arrs, locs, full_shapes):
    n = len(arrs)

    def own(cins, couts, csems):
        x, y, c = _coords()
        j = 2 * x + y
        return [_remote(_half(a, cins[q], h), _piece(a, couts[q], j, h), csems[0].at[2 * q + h],
                        csems[1].at[2 * q + h], (x, y, 1 - c)) for q, a in enumerate(arrs) for h in range(2)]

    def sends(cins, couts, csems):
        x, y, c = _coords()
        j = 2 * x + y
        return [_remote(_half(a, cins[q], c), _piece(a, couts[q], j, c), csems[2].at[2 * q + i],
                        csems[3].at[2 * q + i], (px, py, c))
                for q, a in enumerate(arrs) for i, (px, py) in enumerate(_peer_chips(x, y)[:2])]

    def relays(couts, csems):
        x, y, c = _coords()
        peers = _peer_chips(x, y)
        out = []
        for q, a in enumerate(arrs):
            for r, (src_p, dst_p) in enumerate(((0, 1), (1, 0))):
                sx, sy = peers[src_p]
                rows = _rows_part(a, _piece(a, couts[q], 2 * sx + sy, c), (r, r + 1, 2))
                out.append(_remote(rows, rows, csems[6].at[2 * q + r], csems[7].at[2 * q + r], (*peers[dst_p], c)))
        return out

    def forwards(couts, csems, half_of, which):
        x, y, c = _coords()
        out = []
        for q, a in enumerate(arrs):
            for i in which:
                px, py = _peer_chips(x, y)[i]
                landed = _piece(a, couts[q], 2 * px + py, half_of(c))
                out.append(_remote(landed, landed, csems[4].at[3 * q + i], csems[5].at[3 * q + i], (x, y, 1 - c)))
        return out

    def start(cins, couts, csems):
        for cp in sends(cins, couts, csems) + own(cins, couts, csems):
            cp.start()

    def mid(cins, couts, csems):
        for cp in sends(cins, couts, csems):
            cp.wait_recv()
        for cp in relays(couts, csems) + forwards(couts, csems, lambda c: c, (0, 1)):
            cp.start()

    def finish(cins, couts, csems):
        for cp in relays(couts, csems):
            cp.wait_recv()
        fw_diag = forwards(couts, csems, lambda c: c, (2,))
        for f in fw_diag:
            f.start()
        for f in forwards(couts, csems, lambda c: 1 - c, (0, 1, 2)):
            f.wait_recv()
        for cp in (sends(cins, couts, csems) + relays(couts, csems)
                   + forwards(couts, csems, lambda c: c, (0, 1)) + fw_diag):
            cp.wait_send()
        for cp in own(cins, couts, csems):
            cp.wait()

    return _Comm([locs[a] for a in arrs], [SDS(full_shapes[a], BF16) for a in arrs],
                 _dma_sems(2 * n, 2 * n, 2 * n, 2 * n, 3 * n, 3 * n, 2 * n, 2 * n), start, finish,
                 CHIPS + (SIBLING,), mid=mid)


def _halves_comm(arrs, gbs):
    n = len(arrs)

    def copies(cins, couts, csems):
        x, y, c = _coords()
        return [_remote(_piece(a, cins[q], k, 1 - c), couts[q].at[k], csems[0].at[4 * q + k], csems[1].at[4 * q + k],
                        (x, y, 1 - c)) for q, a in enumerate(arrs) for k in range(4)]

    return _symmetric([gbs[a] for a in arrs], [SDS((4,) + _piece_shape(a, gbs[a].shape), BF16) for a in arrs],
                      _dma_sems(4 * n, 4 * n), copies, [SIBLING])


def _chips_comm(arrs, ps, part=None, into=None):
    n = len(arrs)

    def copies(cins, couts, csems):
        x, y, c = _coords()
        return [_remote(_rows_part(a, cins[q].at[2 * px + py], part), _rows_part(a, couts[q].at[i], part),
                        csems[0].at[3 * q + i], csems[1].at[3 * q + i], (px, py, c))
                for q, a in enumerate(arrs) for i, (px, py) in enumerate(_peer_chips(x, y))]

    ins = [ps[a] for a in arrs] + ([into[a] for a in arrs] if into else [])
    return _symmetric(ins, [SDS((3,) + ps[a].shape[1:], BF16) for a in arrs], _dma_sems(3 * n, 3 * n), copies, CHIPS,
                      aliases={n + q: q for q in range(n)} if into else None)


def _result_comm(arrs, gs):
    n = len(arrs)

    def copies(cins, couts, csems):
        x, y, c = _coords()
        return [_remote(_half(a, cins[q], c), _half(a, couts[q], c), csems[0].at[q], csems[1].at[q], (x, y, 1 - c))
                for q, a in enumerate(arrs)]

    return _symmetric([gs[a] for a in arrs], [SDS(gs[a].shape, F32) for a in arrs], _dma_sems(n, n), copies,
                      [SIBLING], aliases={q: q for q in range(n)})


def _add_halves(arrs, gbs, lands, c_arr, name):
    n = len(arrs)

    def body(c_ref, *refs):
        del c_ref
        for q in range(n):
            refs[2 * n + q][...] = (refs[q][...].astype(F32) + refs[n + q][...].astype(F32)).astype(BF16)

    g_specs, l_specs, o_specs, blocks = [], [], [], 0
    for a in arrs:
        bs, imap = _piece_block(a, gbs[a].shape)
        ps = _piece_shape(a, gbs[a].shape)
        g_specs.append(pl.BlockSpec(bs, lambda k, c_ref, imap=imap: imap(k, c_ref[0])))
        nd = len(ps)
        l_specs.append(pl.BlockSpec((None,) + ps, lambda k, c_ref, nd=nd: (k,) + (0,) * nd))
        o_specs.append(pl.BlockSpec((None,) + ps, lambda k, c_ref, nd=nd: (k,) + (0,) * nd))
        blocks += 3 * _nbytes(ps, BF16)
    return list(pl.pallas_call(
        body, name=name,
        grid_spec=pltpu.PrefetchScalarGridSpec(
            num_scalar_prefetch=1, grid=(4,), in_specs=g_specs + l_specs, out_specs=o_specs),
        out_shape=[SDS((4,) + _piece_shape(a, gbs[a].shape), BF16) for a in arrs],
        compiler_params=_params(("parallel",), blocks, blocks),
    )(c_arr, *[gbs[a] for a in arrs], *lands))


def _sum_chips(a, p, land, shard_shape, jc_arr, name):
    ps = land.shape[1:]
    ax = _rows_axis(a)
    rows = ps[ax]
    nsub = 2 if rows % (2 * SUBLANES_BF16) == 0 else 1
    bs = tuple(r // nsub if q == ax else r for q, r in enumerate(ps))
    nd = len(ps)

    def at_rows(v):
        return tuple(v if q == ax else 0 for q in range(nd))

    def body(jc_ref, p_ref, l_ref, o_ref):
        del jc_ref
        acc = p_ref[...].astype(F32) + l_ref[0].astype(F32)
        acc = acc + l_ref[1].astype(F32)
        o_ref[...] = acc + l_ref[2].astype(F32)

    blocks = 4 * _nbytes(bs, BF16) + _nbytes(bs, F32)
    return pl.pallas_call(
        body, name=name,
        grid_spec=pltpu.PrefetchScalarGridSpec(
            num_scalar_prefetch=1, grid=(nsub,),
            in_specs=[pl.BlockSpec((None,) + bs, lambda s, jc: (jc[0],) + at_rows(s)),
                      pl.BlockSpec((3,) + bs, lambda s, jc: (0,) + at_rows(s))],
            out_specs=pl.BlockSpec(bs, lambda s, jc: at_rows(jc[1] * nsub + s))),
        out_shape=SDS(shard_shape, F32),
        compiler_params=_params(("parallel",), blocks, 2 * _nbytes(bs, F32)),
    )(jc_arr, p, land)


def _small_comm(v):
    rows = v.shape[0]

    def copies(cins, couts, csems):
        x, y, c = _coords()
        me = 4 * x + 2 * y + c
        out = [pltpu.make_async_copy(cins[0], couts[0].at[me], csems[0].at[0])]
        for dlt in range(1, 8):
            px = 1 - x if (dlt >> 2) & 1 else x
            py = 1 - y if (dlt >> 1) & 1 else y
            pc = 1 - c if dlt & 1 else c
            out.append(_remote(cins[0], couts[0].at[me], csems[1].at[dlt - 1], csems[2].at[dlt - 1], (px, py, pc)))
        return out

    return _symmetric([v], [SDS((8, rows, LANES), F32)], _dma_sems(1, 7, 7), copies, EVERYONE)


def _sum8(slots, name):
    def body(s_ref, o_ref):
        acc = s_ref[0]
        for i in range(1, 8):
            acc = acc + s_ref[i]
        o_ref[...] = acc

    return pl.pallas_call(
        body, name=name,
        in_specs=[pl.BlockSpec(memory_space=pltpu.VMEM)], out_specs=pl.BlockSpec(memory_space=pltpu.VMEM),
        out_shape=SDS(slots.shape[1:], F32),
    )(slots)


def _adamw(w, g, m, v, name, g_plane=None):
    rows, cols = w.shape
    tr = _tile(rows, max(SUBLANES_F32, (256 * 1024 // cols) // SUBLANES_F32 * SUBLANES_F32), SUBLANES_F32)

    def body(w_ref, g_ref, m_ref, v_ref, go_ref, d_ref, mo_ref, vo_ref):
        gr = g_ref[...]
        mn = ADAM_B1 * m_ref[...] + (1.0 - ADAM_B1) * gr
        vn = ADAM_B2 * v_ref[...] + (1.0 - ADAM_B2) * (gr * gr)
        m_hat = mn / (1.0 - ADAM_B1 ** ADAM_STEP)
        v_hat = vn / (1.0 - ADAM_B2 ** ADAM_STEP)
        d_ref[...] = -ADAM_LR * (m_hat / (jnp.sqrt(v_hat) + ADAM_EPS) + ADAM_WD * w_ref[...])
        go_ref[...] = gr
        mo_ref[...] = mn
        vo_ref[...] = vn

    spec = pl.BlockSpec((tr, cols), lambda i: (i, 0))
    g_spec = spec if g_plane is None else pl.BlockSpec((None, tr, cols), lambda i: (g_plane, i, 0))
    return pl.pallas_call(
        body, name=name, grid=(rows // tr,),
        in_specs=[spec, g_spec, spec, spec], out_specs=[spec, spec, spec, spec],
        out_shape=[SDS((rows, cols), F32)] * 4,
        compiler_params=_params(("parallel",), 8 * _nbytes((tr, cols), F32), 4 * _nbytes((tr, cols), F32)),
    )(w, g, m, v)


def _pack(parts):
    rows = []
    for p in parts:
        r = p.reshape(-1, LANES)
        pad = (-r.shape[0]) % SUBLANES_F32
        if pad:
            r = jnp.pad(r, ((0, pad), (0, 0)))
        rows.append(r)
    return jnp.concatenate(rows, axis=0)


def _unpack(packed, shapes):
    out, at = [], 0
    for s in shapes:
        n = 1
        for q in s:
            n *= q
        r = n // LANES
        out.append(packed[at:at + r].reshape(s))
        at += r + (-r) % SUBLANES_F32
    return out


def kernel(x, norm_mix, w_in, pool_w, pool_scale, w_pool_proj, conv_w, w_conv_out, w_o, norm_ffn, w_up, ffn_conv_w, ffn_conv_b, w_down, norm_final, loss_target, m_norm_mix, m_w_in, m_pool_w, m_pool_scale, m_w_pool_proj, m_conv_w, m_w_conv_out, m_w_o, m_norm_ffn, m_w_up, m_ffn_conv_w, m_ffn_conv_b, m_w_down, m_norm_final, v_norm_mix, v_w_in, v_pool_w, v_pool_scale, v_w_pool_proj, v_conv_w, v_w_conv_out, v_w_o, v_norm_ffn, v_w_up, v_ffn_conv_w, v_ffn_conv_b, v_w_down, v_norm_final):
    nseq, seq, d = x.shape
    t = nseq * seq
    f = w_down.shape[1] * 4
    c = d // N_GROUPS
    xy = lax.axis_index("x") * 2 + lax.axis_index("y")
    c_arr = lax.axis_index("c").astype(jnp.int32).reshape(1)
    jc_arr = jnp.stack([xy, lax.axis_index("c")]).astype(jnp.int32)
    nsh = 4
    zero = jnp.zeros((), jnp.int32)

    locs = [w_in[0].astype(BF16),
            jnp.stack([w_pool_proj[0], w_conv_out[0], w_o[0]]).astype(BF16),
            w_up[0].astype(BF16), w_down[0].astype(BF16), pool_w[0].astype(BF16)]
    full_shapes = [(nsh, d, N_SPLITS * d // nsh), (3, d, d), (nsh, d, 2 * f // nsh), (f, d), (N_GROUPS, c, c)]

    cw_pad = lax.dynamic_update_slice(jnp.zeros((3, d), F32), conv_w[0], (zero, xy * (d // 4)))
    fw_pad = lax.dynamic_update_slice(jnp.zeros((3, 2 * f), F32), ffn_conv_w[0], (zero, xy * (f // 2)))
    small_w = _pack([cw_pad, fw_pad]) * 0.5

    x2d = x.reshape(t, d)
    tgt = loss_target.reshape(t, d)
    ax, ay = lax.axis_index("x"), lax.axis_index("y")
    order = jnp.stack([xy, 2 * (1 - ax) + ay, 2 * ax + 1 - ay, 2 * (1 - ax) + 1 - ay]).astype(jnp.int32)
    (z, h1, w_in_f), (pool_w_f, w3_f, slots_w) = _fwd_in(
        x2d, norm_mix, locs[0], order,
        _merge([_gather_comm([4], locs, full_shapes), _gather_comm([1], locs, full_shapes, part=(0, 1, 2)),
                _small_comm(small_w)]))
    conv_w_f, ffn_cw_f = _unpack(_sum8(slots_w, "sum8_weights"), [(3, d), (3, 2 * f)])
    ffn_cw_p = ffn_cw_f.reshape(3, 2, f).transpose(1, 0, 2)
    ffn_cb_p = ffn_conv_b.reshape(2, 1, f)
    (lhs3,), (w3_f,) = _mixer_mid_fwd(z, pool_w_f, pool_scale, conv_w_f, nseq,
                                      _gather_comm([1], locs, full_shapes, part=(1, 2, 2), into={1: w3_f}))
    (lhs3, ypc, x1, h2), (w_up_f,) = _mixer_out(lhs3, z, x2d, w3_f, norm_ffn,
                                                _ring_gather_comm([2], locs, full_shapes))
    (u0,), (w_down_f,) = _ffn_up(h2, w_up_f, f, _gather_comm([3], locs, full_shapes))
    act, ua = _ffn_mid_fwd(u0, ffn_cw_p, ffn_cb_p, nseq)
    dx2, dx2b, loss11, g_norm_final = _ffn_down_loss(act, w_down_f, x1, tgt, norm_final.reshape(1, d))

    gbs, lands, ps, lands2, rs = {}, {}, {}, {}, {}
    tn_up = _tile(2 * f // nsh, 1408, LANES)
    npp = f // tn_up

    def add(arrs, name):
        for a, p in zip(arrs, _add_halves(arrs, gbs, [lands[a] for a in arrs], c_arr, name)):
            ps[a] = p

    def summed(a):
        rs[a] = _sum_chips(a, ps[a], lands2[a], _shard_shape(a, full_shapes[a]), jc_arr, "sum_chips_%d" % a)

    (gbs[3],), _ = _wgrad(act, dx2b, "wgrad_down", tr=tn_up, tn=d)
    (da,), (lands[3],) = _ffn_bwd_da(dx2b, w_down_f, _halves_comm([3], gbs))
    add([3], "add_halves_down")
    (du0, g_ffn_cw_p, g_ffn_cb_p), (lands2[3],) = _ffn_mid_bwd(da, u0, ua, ffn_cw_p, nseq, _chips_comm([3], ps))
    summed(3)
    (gbs[2],), (rs[3],) = _wgrad(h2, du0, "wgrad_up", tr=d, tn=tn_up, b_plane_of=lambda n: (n // npp, n % npp),
                                 out_shards=nsh, comm=_result_comm([3], rs))
    (dx1, rhs3, g_norm_ffn), (lands[2],) = _ffn_bwd_dx1(du0, w_up_f, x1, dx2, norm_ffn, 3, _halves_comm([2], gbs))
    add([2], "add_halves_up")
    (rhs3, dz, dpq), (lands2[2],) = _mixer_bwd(rhs3, z, ypc, w3_f, _chips_comm([2], ps, part=(0, 1, 2)))
    (gbs[1],), (lands2[2],) = _wgrad3(lhs3, rhs3, _chips_comm([2], ps, part=(1, 2, 2), into=lands2))
    summed(2)
    (dz, g_conv_w), (lands[1], rs[2]) = _conv_bwd(dz, dpq, z, conv_w_f, nseq,
                                                  _merge([_halves_comm([1], gbs), _result_comm([2], rs)]))
    add([1], "add_halves_sq3")
    (dz, g_pool_w, g_pool_scale), _ = _pool_bwd_call(dz, dpq, z, pool_w_f, pool_scale, nseq)
    gbs[4] = g_pool_w.astype(BF16)
    (gbs[0],), (lands2[1],) = _wgrad_in(h1, dz, nsh, _chips_comm([1], ps))
    summed(1)
    lands[0], lands[4] = _run_comm(_halves_comm([0, 4], gbs), "exchange_halves_in")
    add([0, 4], "add_halves_in")
    g_ffn_cw = g_ffn_cw_p.transpose(1, 0, 2).reshape(3, 2 * f)
    small_a = _pack([g_pool_scale, g_norm_ffn, g_ffn_cb_p.reshape(1, 2 * f), g_norm_final.reshape(d), g_conv_w,
                     g_ffn_cw, jnp.pad(loss11, ((0, SUBLANES_F32 - 1), (0, LANES - 1)))])
    (grad_x, g_norm_mix), (lands2[0], lands2[4], rs[1], slots_a) = _mixer_bwd_dx(
        dz, w_in_f, x2d, dx1, norm_mix,
        _merge([_chips_comm([0, 4], ps), _result_comm([1], rs), _small_comm(small_a)]))
    summed(0)
    summed(4)
    rs[0], rs[4], slots_b = _run_comm(_merge([_result_comm([0, 4], rs), _small_comm(_pack([g_norm_mix]))]),
                                      "exchange_result_in")
    shapes_a = [(1, d), (1, d), (1, 2 * f), (d,), (3, d), (3, 2 * f), (SUBLANES_F32, LANES)]
    gs_pool_scale, gs_norm_ffn, gs_ffn_cb, gs_norm_final, gs_conv_w, gs_ffn_cw, loss_blk = _unpack(
        _sum8(slots_a, "sum8_grads"), shapes_a)
    (gs_norm_mix,) = _unpack(_sum8(slots_b, "sum8_norm_mix"), [(1, d)])
    gs_conv_w = lax.dynamic_slice(gs_conv_w, (zero, xy * (d // 4)), (3, d // 4))
    gs_ffn_cw = lax.dynamic_slice(gs_ffn_cw, (zero, xy * (f // 2)), (3, f // 2))

    def upd(w, g, m, v, name, g_plane=None):
        shape = w.shape
        rows = 1
        for q in shape[:-1]:
            rows *= q
        g2 = g if g_plane is not None else g.reshape(rows, shape[-1])
        outs = _adamw(w.reshape(rows, shape[-1]), g2, m.reshape(rows, shape[-1]), v.reshape(rows, shape[-1]),
                      name, g_plane)
        return [o.reshape(shape) for o in outs]

    res = {
        "w_in": upd(w_in, rs[0], m_w_in, v_w_in, "adamw_w_in"),
        "pool_w": upd(pool_w, rs[4], m_pool_w, v_pool_w, "adamw_pool_w"),
        "w_pool_proj": upd(w_pool_proj, rs[1], m_w_pool_proj, v_w_pool_proj, "adamw_w_pool_proj", 0),
        "w_conv_out": upd(w_conv_out, rs[1], m_w_conv_out, v_w_conv_out, "adamw_w_conv_out", 1),
        "w_o": upd(w_o, rs[1], m_w_o, v_w_o, "adamw_w_o", 2),
        "w_up": upd(w_up, rs[2], m_w_up, v_w_up, "adamw_w_up"),
        "w_down": upd(w_down, rs[3], m_w_down, v_w_down, "adamw_w_down"),
    }

    small_names = ["norm_mix", "pool_scale", "norm_ffn", "ffn_conv_b", "norm_final", "conv_w", "ffn_conv_w"]
    small_ws = [norm_mix, pool_scale, norm_ffn, ffn_conv_b, norm_final, conv_w, ffn_conv_w]
    small_ms = [m_norm_mix, m_pool_scale, m_norm_ffn, m_ffn_conv_b, m_norm_final, m_conv_w, m_ffn_conv_w]
    small_vs = [v_norm_mix, v_pool_scale, v_norm_ffn, v_ffn_conv_b, v_norm_final, v_conv_w, v_ffn_conv_w]
    small_gs = [gs_norm_mix, gs_pool_scale, gs_norm_ffn, gs_ffn_cb, gs_norm_final, gs_conv_w, gs_ffn_cw]
    _, sd, sm, sv = _adamw(_pack(small_ws), _pack(small_gs), _pack(small_ms), _pack(small_vs), "adamw_small")
    shapes = [w.shape for w in small_ws]
    sd, sm, sv = _unpack(sd, shapes), _unpack(sm, shapes), _unpack(sv, shapes)
    for i, nm in enumerate(small_names):
        res[nm] = [small_gs[i].reshape(shapes[i]), sd[i], sm[i], sv[i]]

    order = ["norm_mix", "w_in", "pool_w", "pool_scale", "w_pool_proj", "conv_w", "w_conv_out", "w_o", "norm_ffn",
             "w_up", "ffn_conv_w", "ffn_conv_b", "w_down", "norm_final"]
    return (loss_blk[0, 0], grad_x.reshape(x.shape), *[res[n][0] for n in order], *[res[n][1] for n in order],
            *[res[n][2] for n in order], *[res[n][3] for n in order])
```

```python
import math

import jax
import jax.numpy as jnp
from jax import lax
from jax.experimental import pallas as pl
from jax.experimental.pallas import tpu as pltpu

F32 = jnp.float32
BF16 = jnp.bfloat16
SDS = jax.ShapeDtypeStruct
MESH = pl.DeviceIdType.MESH

RMS_EPS = 1e-6
POOL_WINDOWS = (2, 4, 8, 16)
N_GROUPS = len(POOL_WINDOWS)
N_SPLITS = 6

ADAM_LR = 0.001
ADAM_B1 = 0.9
ADAM_B2 = 0.999
ADAM_EPS = 1e-08
ADAM_WD = 0.01
ADAM_STEP = 10

LANES = 128
SUBLANES_F32 = 8
SUBLANES_BF16 = 16
VMEM_BYTES = 64 * 1024 * 1024
VMEM_CAP = VMEM_BYTES - 8 * 1024 * 1024
VMEM_FLOOR = 16 * 1024 * 1024

ANY = pl.BlockSpec(memory_space=pl.ANY)


def _tile(dim, pref, align):
    if dim <= pref:
        return dim
    t = (pref // align) * align
    while t >= align:
        if dim % t == 0:
            return t
        t -= align
    return dim


def _nbytes(shape, dtype):
    n = 1
    for s in shape:
        n *= s
    return n * jnp.dtype(dtype).itemsize


def _params(sem, block_bytes, temp_bytes=0, collective_id=None):
    need = 2 * block_bytes + temp_bytes + 4 * 1024 * 1024
    return pltpu.CompilerParams(dimension_semantics=sem, collective_id=collective_id,
                                vmem_limit_bytes=int(min(max(need, VMEM_FLOOR), VMEM_CAP)))


SIBLING = (0, 0, 1)
CHIPS = ((1, 0, 0), (0, 1, 0), (1, 1, 0))
EVERYONE = tuple((a, b, c) for a in range(2) for b in range(2) for c in range(2) if a + b + c)
PEER_SETS = (frozenset([SIBLING]), frozenset(CHIPS), frozenset(CHIPS + (SIBLING,)), frozenset(EVERYONE))
MID_AT = 0.75


def _collective_id(peers):
    return PEER_SETS.index(frozenset(peers))


def _handshake(peers):
    x, y, c = lax.axis_index("x"), lax.axis_index("y"), lax.axis_index("c")
    bar = pltpu.get_barrier_semaphore()
    for fx, fy, fc in sorted(peers):
        dev = (1 - x if fx else x, 1 - y if fy else y, 1 - c if fc else c)
        pl.semaphore_signal(bar, inc=1, device_id=dev, device_id_type=MESH)
    pl.semaphore_wait(bar, len(peers))


def _in_hbm(args):
    return [pltpu.with_memory_space_constraint(a, pltpu.HBM) for a in args]


def _out_hbm(shapes):
    return [pltpu.HBM(o.shape, o.dtype) for o in shapes]


class _Comm:
    def __init__(self, ins, out_shapes, sems, start, finish, peers, aliases=None, mid=None):
        self.ins = _in_hbm(ins)
        self.out_shapes = _out_hbm(out_shapes)
        self.sems = list(sems)
        self.start = start
        self.finish = finish
        self.mid = mid
        self.peers = frozenset(peers)
        self.aliases = dict(aliases or {})


def _pcall(body, *, name, grid, in_specs, out_specs, out_shape, sem, blocks, temps=0, scratch_shapes=(),
           input_output_aliases=None, comm=None):
    in_specs = list(in_specs)
    out_specs = list(out_specs)
    out_shape = list(out_shape)
    scratch_shapes = list(scratch_shapes)
    aliases = dict(input_output_aliases or {})
    n_in, n_out, n_scr = len(in_specs), len(out_shape), len(scratch_shapes)
    if comm is None:
        call = pl.pallas_call(
            body, name=name, grid=grid, in_specs=in_specs, out_specs=out_specs, out_shape=out_shape,
            scratch_shapes=scratch_shapes, input_output_aliases=aliases,
            compiler_params=_params(sem, blocks, temps))
        return lambda *args: (list(call(*args)), [])

    nci, nco = len(comm.ins), len(comm.out_shapes)
    n_steps = 1
    for g in grid:
        n_steps *= g

    def hosted(*refs):
        ins = refs[:n_in]
        cins = refs[n_in:n_in + nci]
        outs = refs[n_in + nci:n_in + nci + n_out]
        couts = refs[n_in + nci + n_out:n_in + nci + n_out + nco]
        scr = refs[n_in + nci + n_out + nco:n_in + nci + n_out + nco + n_scr]
        csems = refs[n_in + nci + n_out + nco + n_scr:]
        first = None
        last = None
        step = 0
        for q, g in enumerate(grid):
            pid = pl.program_id(q)
            first = (pid == 0) if first is None else first & (pid == 0)
            last = (pid == g - 1) if last is None else last & (pid == g - 1)
            step = step * g + pid

        @pl.when(first)
        def _():
            _handshake(comm.peers)
            comm.start(cins, couts, csems)

        if comm.mid is not None:
            @pl.when(step == int(MID_AT * n_steps))
            def _():
                comm.mid(cins, couts, csems)

        body(*ins, *outs, *scr)

        @pl.when(last)
        def _():
            comm.finish(cins, couts, csems)

    for i, o in comm.aliases.items():
        aliases[n_in + i] = n_out + o
    call = pl.pallas_call(
        hosted, name=name, grid=grid, in_specs=in_specs + [ANY] * nci, out_specs=out_specs + [ANY] * nco,
        out_shape=out_shape + comm.out_shapes, scratch_shapes=scratch_shapes + comm.sems,
        input_output_aliases=aliases,
        compiler_params=_params(("arbitrary",) * len(grid), blocks, temps, _collective_id(comm.peers)))

    def run(*args):
        res = call(*args, *comm.ins)
        return list(res[:n_out]), list(res[n_out:])

    return run


def _run_comm(comm, name):
    def body(*refs):
        nci, nco = len(comm.ins), len(comm.out_shapes)
        cins, couts, csems = refs[:nci], refs[nci:nci + nco], refs[nci + nco:]
        _handshake(comm.peers)
        comm.start(cins, couts, csems)
        if comm.mid is not None:
            comm.mid(cins, couts, csems)
        comm.finish(cins, couts, csems)

    return list(pl.pallas_call(
        body, name=name, in_specs=[ANY] * len(comm.ins), out_specs=[ANY] * len(comm.out_shapes),
        out_shape=comm.out_shapes, scratch_shapes=comm.sems, input_output_aliases=comm.aliases,
        compiler_params=pltpu.CompilerParams(collective_id=_collective_id(comm.peers)),
    )(*comm.ins))


def _dot(a, b):
    return jnp.dot(a, b, preferred_element_type=F32)


def _dot_tb(a, b):
    return lax.dot_general(a, b, (((1,), (1,)), ((), ())), preferred_element_type=F32)


def _dot_ta(a, b):
    return lax.dot_general(a, b, (((0,), (0,)), ((), ())), preferred_element_type=F32)


def _rms_fwd(x):
    inv = lax.rsqrt(jnp.mean(x * x, axis=-1, keepdims=True) + RMS_EPS)
    return x * inv, inv


def _rms_bwd(dy, xhat, inv, g):
    gd = dy * g
    return inv * (gd - xhat * jnp.mean(gd * xhat, axis=-1, keepdims=True))


def _sigmoid(x):
    return 1.0 / (1.0 + jnp.exp(-x))


def _shift_down(x, k, row):
    return jnp.where(row >= k, pltpu.roll(x, k, 0), 0.0)


def _shift_up(x, k, row):
    s = x.shape[0]
    return jnp.where(row < s - k, pltpu.roll(x, s - k, 0), 0.0)


def _pool_fwd(u, win, row):
    s = u
    k = 1
    while k < win:
        s = s + _shift_down(s, k, row)
        k *= 2
    cnt = jnp.minimum(row + 1, win).astype(F32)
    return s / cnt - u


def _pool_bwd(dp, win, row):
    cnt = jnp.minimum(row + 1, win).astype(F32)
    s = dp / cnt
    k = 1
    while k < win:
        s = s + _shift_up(s, k, row)
        k *= 2
    return s - dp


def _acc_over(k, nk, part, acc, o_ref):
    @pl.when(k == 0)
    def _():
        acc[...] = part

    @pl.when(k > 0)
    def _():
        acc[...] += part

    @pl.when(k == nk - 1)
    def _():
        o_ref[...] = acc[...].astype(o_ref.dtype)


def _fwd_in(x, g, w_loc, order, comm):
    t, d = x.shape
    ws = w_loc.shape[1]
    nsh = order.shape[0]
    assert nsh == 4, "the shard walk below is written for the 2 x 2 chips of the mesh"
    tm = _tile(t, 1024, SUBLANES_BF16)
    ni = t // tm
    nci, nco = len(comm.ins), len(comm.out_shapes)
    all_peers = comm.peers | frozenset(CHIPS + (SIBLING,))

    def body(order_ref, x_ref, g_ref, loc_ref, *rest):
        del order_ref
        cins = rest[:nci]
        z_ref, h_ref, full_ref = rest[nci:nci + 3]
        couts = rest[nci + 3:nci + 3 + nco]
        (hs, wbuf, wsem, own_s, own_r, snd_s, snd_r, fwd_s, fwd_r, rly_s, rly_r) = rest[nci + 3 + nco:nci + 14 + nco]
        csems = rest[nci + 14 + nco:]
        j = pl.program_id(0)
        i = pl.program_id(1)
        x_, y_, c_ = _coords()
        own = 2 * x_ + y_
        sib = (x_, y_, 1 - c_)
        peers = _peer_chips(x_, y_)

        def sends():
            return [_remote(_half(0, loc_ref, c_), _piece(0, full_ref, own, c_), snd_s.at[p], snd_r.at[p], (px, py, c_))
                    for p, (px, py) in enumerate(peers[:2])]

        def relays():
            out = []
            for q, (src_p, dst_p) in enumerate(((0, 1), (1, 0))):
                sx, sy = peers[src_p]
                part = _rows_part(0, _piece(0, full_ref, 2 * sx + sy, c_), (q, q + 1, 2))
                out.append(_remote(part, part, rly_s.at[q], rly_r.at[q], (*peers[dst_p], c_)))
            return out

        def owns():
            return [_remote(_half(0, loc_ref, h), _piece(0, full_ref, own, h), own_s.at[h], own_r.at[h], sib)
                    for h in range(2)]

        def forward(p, half):
            px, py = peers[p]
            landed = _piece(0, full_ref, 2 * px + py, half)
            return _remote(landed, landed, fwd_s.at[p], fwd_r.at[p], sib)

        def load(src, slot):
            return pltpu.make_async_copy(src, wbuf.at[slot], wsem.at[slot])

        @pl.when((j == 0) & (i == 0))
        def _():
            load(loc_ref, 0).start()
            _handshake(all_peers)
            for cp in sends() + owns():
                cp.start()
            comm.start(cins, couts, csems)

        @pl.when(j == 0)
        def _():
            xh, _ = _rms_fwd(x_ref[...])
            h = (xh * g_ref[...]).astype(BF16)
            hs[pl.ds(pl.multiple_of(i * tm, tm), tm), :] = h
            h_ref[...] = h

        slot = j % 2

        @pl.when(i == 0)
        def _():
            load(loc_ref, slot).wait()

        z_ref[...] = _dot(hs[pl.ds(pl.multiple_of(i * tm, tm), tm), :], wbuf[slot]).astype(BF16)

        def load_shard(p, into):
            px, py = peers[p]
            forward(p, 1 - c_).wait_recv()
            load(full_ref.at[2 * px + py], into).start()

        @pl.when((j == 0) & (i == ni - 1))
        def _():
            for cp in sends():
                cp.wait_recv()
            for cp in [forward(0, c_), forward(1, c_)] + relays():
                cp.start()
            load_shard(0, 1)

        @pl.when((j == 1) & (i == 0))
        def _():
            load_shard(1, 0)

        @pl.when((j == 1) & (i == ni - 1))
        def _():
            for cp in relays():
                cp.wait_recv()
            forward(2, c_).start()
            load_shard(2, 1)

        @pl.when((j == nsh - 1) & (i == ni - 1))
        def _():
            for cp in sends() + relays() + [forward(p, c_) for p in range(nsh - 1)]:
                cp.wait_send()
            for cp in owns():
                cp.wait()
            comm.finish(cins, couts, csems)

    last = ni - 1
    blocks = _nbytes((tm, d), F32) + _nbytes((tm, ws), BF16) + _nbytes((tm, d), BF16)
    scratch = _nbytes((t, d), BF16) + 2 * _nbytes((d, ws), BF16)
    res = pl.pallas_call(
        body, name="fwd_in",
        grid_spec=pltpu.PrefetchScalarGridSpec(
            num_scalar_prefetch=1, grid=(nsh, ni),
            in_specs=[pl.BlockSpec((tm, d), lambda j, i, o: (jnp.where(j == 0, i, last), 0)),
                      pl.BlockSpec((1, d), lambda j, i, o: (0, 0)), ANY] + [ANY] * nci,
            out_specs=[pl.BlockSpec((tm, ws), lambda j, i, o: (i, o[j])),
                       pl.BlockSpec((tm, d), lambda j, i, o: (jnp.where(j == 0, i, last), 0)), ANY] + [ANY] * nco,
            scratch_shapes=[pltpu.VMEM((t, d), BF16), pltpu.VMEM((2, d, ws), BF16)]
            + _dma_sems(2, 2, 2, 2, 2, nsh - 1, nsh - 1, 2, 2) + comm.sems),
        out_shape=[SDS((t, nsh * ws), BF16), SDS((t, d), BF16), SDS((nsh, d, ws), BF16)] + comm.out_shapes,
        input_output_aliases={4 + i: 3 + o for i, o in comm.aliases.items()},
        compiler_params=_params(("arbitrary", "arbitrary"), blocks, scratch + 3 * _nbytes((tm, d), F32),
                                _collective_id(all_peers)),
    )(order, x, g, w_loc, *comm.ins)
    return list(res[:3]), list(res[3:])


def _mixer_mid_fwd(z, pool_w, pool_scale, conv_w, nseq, comm=None):
    t = z.shape[0]
    d = pool_scale.shape[1]
    s = t // nseq
    c = d // N_GROUPS

    def body(zp, zb, zc, zv, pw, ps, cw, o):
        j = pl.program_id(1)
        row = lax.broadcasted_iota(jnp.int32, (s, c), 0)
        for gi, win in enumerate(POOL_WINDOWS):
            @pl.when(j == gi)
            def _(win=win):
                pooled = _pool_fwd(zp[...].astype(F32), win, row)
                o[0] = (_dot(pooled.astype(BF16), pw[...]) * ps[...]).astype(BF16)

        cv = zc[...].astype(F32) * zv[...].astype(F32)
        cc = (cw[pl.ds(2, 1), :] * cv + cw[pl.ds(1, 1), :] * _shift_down(cv, 1, row)
              + cw[pl.ds(0, 1), :] * _shift_down(cv, 2, row))
        o[1] = (zb[...].astype(F32) * cc).astype(BF16)

    blocks = 4 * _nbytes((s, c), BF16) + _nbytes((c, c), BF16) + _nbytes((2, s, c), BF16)
    return _pcall(
        body, name="mixer_mid_fwd", grid=(nseq, N_GROUPS),
        in_specs=[pl.BlockSpec((s, c), lambda b, j: (b, j)),
                  pl.BlockSpec((s, c), lambda b, j: (b, N_GROUPS + j)),
                  pl.BlockSpec((s, c), lambda b, j: (b, 2 * N_GROUPS + j)),
                  pl.BlockSpec((s, c), lambda b, j: (b, 3 * N_GROUPS + j)),
                  pl.BlockSpec((None, c, c), lambda b, j: (j, 0, 0)),
                  pl.BlockSpec((1, c), lambda b, j: (0, j)),
                  pl.BlockSpec((3, c), lambda b, j: (0, j))],
        out_specs=[pl.BlockSpec((2, s, c), lambda b, j: (0, b, j))],
        out_shape=[SDS((3, t, d), BF16)],
        sem=("parallel", "parallel"), blocks=blocks, temps=8 * _nbytes((s, c), F32), comm=comm,
    )(z, z, z, z, pool_w, pool_scale, conv_w)


def _mixer_out(lhs3, z, x, w3, g_ffn, comm=None):
    t, d = x.shape
    tm = _tile(t, 256, SUBLANES_BF16)

    def body(pq, zgp, zgc, x_ref, w_ref, g_ref, mrg, ypc, x1o, h2o):
        yp = _dot(pq[0], w_ref[0])
        yc = _dot(pq[1], w_ref[1])
        m = _sigmoid(zgp[...].astype(F32)) * yp + _sigmoid(zgc[...].astype(F32)) * yc
        mb = m.astype(BF16)
        x1 = x_ref[...] + _dot(mb, w_ref[2])
        ypc[0] = yp.astype(BF16)
        ypc[1] = yc.astype(BF16)
        mrg[...] = mb
        x1o[...] = x1
        xh, _ = _rms_fwd(x1)
        h2o[...] = (xh * g_ref[...]).astype(BF16)

    blocks = (_nbytes((2, tm, d), BF16) * 2 + _nbytes((tm, d), BF16) * 4 + _nbytes((tm, d), F32) * 2
              + _nbytes((3, d, d), BF16))
    return _pcall(
        body, name="mixer_out", grid=(t // tm,),
        in_specs=[pl.BlockSpec((2, tm, d), lambda i: (0, i, 0)),
                  pl.BlockSpec((tm, d), lambda i: (i, 4)),
                  pl.BlockSpec((tm, d), lambda i: (i, 5)),
                  pl.BlockSpec((tm, d), lambda i: (i, 0)),
                  pl.BlockSpec((3, d, d), lambda i: (0, 0, 0)),
                  pl.BlockSpec((1, d), lambda i: (0, 0))],
        out_specs=[pl.BlockSpec((None, tm, d), lambda i: (2, i, 0)),
                   pl.BlockSpec((2, tm, d), lambda i: (0, i, 0)),
                   pl.BlockSpec((tm, d), lambda i: (i, 0)),
                   pl.BlockSpec((tm, d), lambda i: (i, 0))],
        out_shape=[SDS(lhs3.shape, BF16), SDS((2, t, d), BF16), SDS((t, d), F32), SDS((t, d), BF16)],
        input_output_aliases={0: 0},
        sem=("parallel",), blocks=blocks, temps=8 * _nbytes((tm, d), F32), comm=comm,
    )(lhs3, z, z, x, w3, g_ffn)


def _ffn_up(h2, w_up, f, comm=None):
    t, d = h2.shape
    _, _, ws = w_up.shape
    tm = _tile(t, 1024, SUBLANES_BF16)
    tn = _tile(ws, 1408, LANES)
    nps = ws // tn
    npp = f // tn

    def body(h_ref, w_ref, o_ref):
        o_ref[...] = _dot(h_ref[...], w_ref[...]).astype(BF16)

    blocks = _nbytes((tm, d), BF16) + _nbytes((d, tn), BF16) + _nbytes((tm, tn), BF16)
    return _pcall(
        body, name="ffn_up", grid=(t // tm, 2 * npp),
        in_specs=[pl.BlockSpec((tm, d), lambda i, j: (i, 0)),
                  pl.BlockSpec((None, d, tn), lambda i, j: (j // nps, 0, j % nps))],
        out_specs=[pl.BlockSpec((None, tm, tn), lambda i, j: (j // npp, i, j % npp))],
        out_shape=[SDS((2, t, f), BF16)],
        sem=("parallel", "parallel"), blocks=blocks, temps=_nbytes((tm, tn), F32), comm=comm,
    )(h2, w_up)


def _conv3_rows(u, u1, u2, w_ref, p):
    return w_ref[p, pl.ds(2, 1), :] * u + w_ref[p, pl.ds(1, 1), :] * u1 + w_ref[p, pl.ds(0, 1), :] * u2


WGRAD_TOKENS = 2048
WGRAD_TOKENS_WIDE = 4096
CHUNK = 64
HALO = SUBLANES_F32


def _up1_up2(u, nxt):
    rows = u.shape[0]
    ext = jnp.concatenate([u, nxt], axis=0)
    n = rows + HALO
    return pltpu.roll(ext, n - 1, 0)[:rows], pltpu.roll(ext, n - 2, 0)[:rows]


def _fold8(x):
    return jnp.sum(x.reshape(x.shape[0] // SUBLANES_F32, SUBLANES_F32, x.shape[1]), axis=0)


def _ffn_mid_fwd(u0, cw, cb, nseq):
    _, t, f = u0.shape
    s = t // nseq
    c = _tile(f, 256, LANES)

    def body(u_ref, w_ref, b_ref, a_ref, uo_ref):
        row = lax.broadcasted_iota(jnp.int32, (s, c), 0)
        act = []
        for p in range(2):
            u = u_ref[p].astype(F32)
            act.append(_conv3_rows(u, _shift_down(u, 1, row), _shift_down(u, 2, row), w_ref, p) + b_ref[p])
            uo_ref[p] = act[p].astype(BF16)
        ug, uv = act
        a_ref[...] = (ug * _sigmoid(ug) * uv).astype(BF16)

    blocks = 2 * _nbytes((2, s, c), BF16) + _nbytes((s, c), BF16)
    outs, _ = _pcall(
        body, name="ffn_mid_fwd", grid=(f // c, nseq),
        in_specs=[pl.BlockSpec((2, s, c), lambda j, b: (0, b, j)),
                  pl.BlockSpec((2, 3, c), lambda j, b: (0, 0, j)),
                  pl.BlockSpec((2, 1, c), lambda j, b: (0, 0, j))],
        out_specs=[pl.BlockSpec((s, c), lambda j, b: (b, j)),
                   pl.BlockSpec((2, s, c), lambda j, b: (0, b, j))],
        out_shape=[SDS((t, f), BF16), SDS((2, t, f), BF16)],
        sem=("parallel", "parallel"), blocks=blocks, temps=8 * _nbytes((s, c), F32),
    )(u0, cw, cb)
    return outs


def _ffn_down_loss(a, w_down, x1, tgt, g_fin):
    t, f = a.shape
    d = x1.shape[1]
    tm = _tile(t, 256, SUBLANES_BF16)
    nsteps = t // tm

    def body(a_ref, w_ref, x1_ref, t_ref, g_ref, dx_ref, dxb_ref, loss_ref, gg_ref, lacc):
        i = pl.program_id(0)

        @pl.when(i == 0)
        def _():
            lacc[...] = jnp.zeros_like(lacc)
            gg_ref[...] = jnp.zeros_like(gg_ref)

        x2 = x1_ref[...] + _dot(a_ref[...], w_ref[...])
        xh, inv = _rms_fwd(x2)
        g = g_ref[...]
        e = xh * g - t_ref[...]
        lacc[...] += jnp.sum(e * e, axis=0, keepdims=True)
        dy = e * (1.0 / d)
        gg_ref[...] += jnp.sum(dy * xh, axis=0, keepdims=True)
        dx2 = _rms_bwd(dy, xh, inv, g)
        dx_ref[...] = dx2
        dxb_ref[...] = dx2.astype(BF16)

        @pl.when(i == nsteps - 1)
        def _():
            loss_ref[...] = jnp.sum(lacc[...], axis=1, keepdims=True) * (0.5 / d)

    blocks = (_nbytes((tm, f), BF16) + _nbytes((f, d), BF16) + 3 * _nbytes((tm, d), F32) + _nbytes((tm, d), BF16))
    outs, _ = _pcall(
        body, name="ffn_down_loss", grid=(nsteps,),
        in_specs=[pl.BlockSpec((tm, f), lambda i: (i, 0)), pl.BlockSpec((f, d), lambda i: (0, 0)),
                  pl.BlockSpec((tm, d), lambda i: (i, 0)), pl.BlockSpec((tm, d), lambda i: (i, 0)),
                  pl.BlockSpec((1, d), lambda i: (0, 0))],
        out_specs=[pl.BlockSpec((tm, d), lambda i: (i, 0)), pl.BlockSpec((tm, d), lambda i: (i, 0)),
                   pl.BlockSpec((1, 1), lambda i: (0, 0)), pl.BlockSpec((1, d), lambda i: (0, 0))],
        out_shape=[SDS((t, d), F32), SDS((t, d), BF16), SDS((1, 1), F32), SDS((1, d), F32)],
        scratch_shapes=[pltpu.VMEM((1, d), F32)],
        sem=("arbitrary",), blocks=blocks, temps=8 * _nbytes((tm, d), F32),
    )(a, w_down, x1, tgt, g_fin)
    return outs


def _ffn_bwd_da(dxb, w_down, comm=None):
    t, d = dxb.shape
    f = w_down.shape[0]
    tm = _tile(t, 512, SUBLANES_BF16)

    def body(x_ref, w_ref, o_ref):
        o_ref[...] = _dot_tb(x_ref[...], w_ref[...]).astype(BF16)

    blocks = _nbytes((tm, d), BF16) + _nbytes((tm, f), BF16)
    return _pcall(
        body, name="ffn_bwd_da", grid=(t // tm,),
        in_specs=[pl.BlockSpec((tm, d), lambda i: (i, 0)),
                  pl.BlockSpec((f, d), lambda i: (0, 0), pipeline_mode=pl.Buffered(1))],
        out_specs=[pl.BlockSpec((tm, f), lambda i: (i, 0))],
        out_shape=[SDS((t, f), BF16)],
        sem=("parallel",), blocks=blocks, temps=_nbytes((f, d), BF16) + _nbytes((tm, f), F32), comm=comm,
    )(dxb, w_down)


def _ffn_mid_bwd(da, u0, ua, cw, nseq, comm=None):
    _, t, f = u0.shape
    s = t // nseq
    c = _tile(f, 128, LANES)
    r = _tile(s, CHUNK, SUBLANES_BF16)
    n = s // r

    def body(da_ref, u_ref, ua_ref, w_ref, du_ref, gw_ref, gb_ref):
        @pl.when(pl.program_id(1) == 0)
        def _():
            gw_ref[...] = jnp.zeros_like(gw_ref)
            gb_ref[...] = jnp.zeros_like(gb_ref)

        def step(i, carry):
            nxt, sums = carry
            rows = pl.ds(pl.multiple_of((n - 1 - i) * r, r), r)
            ug = ua_ref[0, rows, :].astype(F32)
            uv = ua_ref[1, rows, :].astype(F32)
            sg = _sigmoid(ug)
            dacc = da_ref[rows, :].astype(F32)
            dus = (dacc * uv * sg * (1.0 + ug * (1.0 - sg)), dacc * (ug * sg))
            first, new_sums = [], []
            for p in range(2):
                du = dus[p]
                d1, d2 = _up1_up2(du, nxt[p])
                du_ref[p, rows, :] = _conv3_rows(du, d1, d2, w_ref, p).astype(BF16)
                u = u_ref[p, rows, :].astype(F32)
                sb, s0, s1, s2 = sums[p]
                new_sums.append((sb + _fold8(du), s0 + _fold8(d2 * u), s1 + _fold8(d1 * u), s2 + _fold8(du * u)))
                first.append(du[:HALO])
            return tuple(first), tuple(new_sums)

        zero = jnp.zeros((HALO, c), F32)
        _, sums = lax.fori_loop(0, n, step, ((zero, zero), ((zero,) * 4,) * 2))
        for p in range(2):
            sb, s0, s1, s2 = sums[p]
            gb_ref[p] += jnp.sum(sb, axis=0, keepdims=True)
            gw_ref[p, pl.ds(0, 1), :] += jnp.sum(s0, axis=0, keepdims=True)
            gw_ref[p, pl.ds(1, 1), :] += jnp.sum(s1, axis=0, keepdims=True)
            gw_ref[p, pl.ds(2, 1), :] += jnp.sum(s2, axis=0, keepdims=True)

    blocks = _nbytes((s, c), BF16) + 3 * _nbytes((2, s, c), BF16)
    return _pcall(
        body, name="ffn_mid_bwd", grid=(f // c, nseq),
        in_specs=[pl.BlockSpec((s, c), lambda j, b: (b, j)),
                  pl.BlockSpec((2, s, c), lambda j, b: (0, b, j)),
                  pl.BlockSpec((2, s, c), lambda j, b: (0, b, j)),
                  pl.BlockSpec((2, 3, c), lambda j, b: (0, 0, j))],
        out_specs=[pl.BlockSpec((2, s, c), lambda j, b: (0, b, j)),
                   pl.BlockSpec((2, 3, c), lambda j, b: (0, 0, j)),
                   pl.BlockSpec((2, 1, c), lambda j, b: (0, 0, j))],
        out_shape=[SDS((2, t, f), BF16), SDS((2, 3, f), F32), SDS((2, 1, f), F32)],
        sem=("parallel", "arbitrary"), blocks=blocks, temps=4 * 1024 * 1024, comm=comm,
    )(da, u0, ua, cw)


def _wgrad(a, b, name, *, tr, tn, b_plane_of=None, out_shards=None, comm=None):
    t, m = a.shape
    n_total = b.shape[-1] * (b.shape[0] if b.ndim == 3 else 1)
    tk = _tile(t, WGRAD_TOKENS_WIDE if n_total > tn and m == tr else WGRAD_TOKENS, SUBLANES_BF16)
    nk = t // tk
    once = pl.Buffered(1) if nk == 1 else None

    def body(a_ref, b_ref, o_ref, *acc):
        part = _dot_ta(a_ref[...], b_ref[...])
        if nk == 1:
            o_ref[...] = part.astype(BF16)
        else:
            _acc_over(pl.program_id(2), nk, part, acc[0], o_ref)

    if b.ndim == 3:
        b_spec = pl.BlockSpec((None, tk, tn), lambda r, n, k: (b_plane_of(n)[0], k, b_plane_of(n)[1]))
    else:
        b_spec = pl.BlockSpec((tk, tn), lambda r, n, k: (k, n), pipeline_mode=once if n_total == tn else None)
    if out_shards is None:
        o_spec = pl.BlockSpec((tr, tn), lambda r, n, k: (r, n))
        o_shape = SDS((m, n_total), BF16)
    else:
        nps = n_total // out_shards // tn
        o_spec = pl.BlockSpec((None, tr, tn), lambda r, n, k: (n // nps, r, n % nps))
        o_shape = SDS((out_shards, m, n_total // out_shards), BF16)
    blocks = _nbytes((tk, tr), BF16) + _nbytes((tk, tn), BF16) + _nbytes((tr, tn), BF16)
    return _pcall(
        body, name=name, grid=(m // tr, n_total // tn, nk),
        in_specs=[pl.BlockSpec((tk, tr), lambda r, n, k: (k, r), pipeline_mode=once if m == tr else None), b_spec],
        out_specs=[o_spec], out_shape=[o_shape],
        scratch_shapes=[] if nk == 1 else [pltpu.VMEM((tr, tn), F32)],
        sem=("parallel", "parallel", "arbitrary"), blocks=blocks, temps=2 * _nbytes((tr, tn), F32), comm=comm,
    )(a, b)


def _wgrad3(lhs3, rhs3, comm=None):
    nw, t, d = lhs3.shape
    tk = _tile(t, WGRAD_TOKENS, SUBLANES_BF16)
    nk = t // tk

    def body(a_ref, b_ref, o_ref, *acc):
        part = _dot_ta(a_ref[...], b_ref[...])
        if nk == 1:
            o_ref[...] = part.astype(BF16)
        else:
            _acc_over(pl.program_id(1), nk, part, acc[0], o_ref)

    blocks = 2 * _nbytes((tk, d), BF16) + _nbytes((d, d), BF16)
    return _pcall(
        body, name="wgrad_sq3", grid=(nw, nk),
        in_specs=[pl.BlockSpec((None, tk, d), lambda w, k: (w, k, 0)),
                  pl.BlockSpec((None, tk, d), lambda w, k: (w, k, 0))],
        out_specs=[pl.BlockSpec((None, d, d), lambda w, k: (w, 0, 0))],
        out_shape=[SDS((nw, d, d), BF16)],
        scratch_shapes=[] if nk == 1 else [pltpu.VMEM((d, d), F32)],
        sem=("parallel", "arbitrary"), blocks=blocks, temps=2 * _nbytes((d, d), F32), comm=comm,
    )(lhs3, rhs3)


def _ffn_bwd_dx1(du0, w_up, x1, dx2, g_ffn, n_planes_out, comm=None):
    _, t, f = du0.shape
    d = x1.shape[1]
    nsh, _, ws = w_up.shape
    tm = _tile(t, 256, SUBLANES_BF16)
    spp = f // ws

    def body(du_ref, w_ref, x1_ref, dx2_ref, g_ref, dx1_ref, dxb_ref, gg_ref):
        @pl.when(pl.program_id(0) == 0)
        def _():
            gg_ref[...] = jnp.zeros_like(gg_ref)

        dh = None
        for k in range(nsh):
            part = _dot_tb(du_ref[k // spp, :, (k % spp) * ws:(k % spp + 1) * ws], w_ref[k])
            dh = part if dh is None else dh + part
        xh, inv = _rms_fwd(x1_ref[...])
        gg_ref[...] += jnp.sum(dh * xh, axis=0, keepdims=True)
        dx1 = dx2_ref[...] + _rms_bwd(dh, xh, inv, g_ref[...])
        dx1_ref[...] = dx1
        dxb_ref[...] = dx1.astype(BF16)

    blocks = _nbytes((2, tm, f), BF16) + 3 * _nbytes((tm, d), F32) + _nbytes((tm, d), BF16)
    return _pcall(
        body, name="ffn_bwd_dx1", grid=(t // tm,),
        in_specs=[pl.BlockSpec((2, tm, f), lambda i: (0, i, 0)),
                  pl.BlockSpec((nsh, d, ws), lambda i: (0, 0, 0), pipeline_mode=pl.Buffered(1)),
                  pl.BlockSpec((tm, d), lambda i: (i, 0)),
                  pl.BlockSpec((tm, d), lambda i: (i, 0)),
                  pl.BlockSpec((1, d), lambda i: (0, 0))],
        out_specs=[pl.BlockSpec((tm, d), lambda i: (i, 0)),
                   pl.BlockSpec((None, tm, d), lambda i: (n_planes_out - 1, i, 0)),
                   pl.BlockSpec((1, d), lambda i: (0, 0))],
        out_shape=[SDS((t, d), F32), SDS((n_planes_out, t, d), BF16), SDS((1, d), F32)],
        sem=("arbitrary",), blocks=blocks, temps=_nbytes(w_up.shape, BF16) + 8 * _nbytes((tm, d), F32), comm=comm,
    )(du0, w_up, x1, dx2, g_ffn)


def _mixer_bwd(rhs3, z, ypc, w3, comm=None):
    _, t, d = rhs3.shape
    tm = _tile(t, 256, SUBLANES_BF16)

    def body(dx_ref, zgp, zgc, ypc_ref, w_ref, dyo, dzo, dpq):
        dm = _dot_tb(dx_ref[...], w_ref[2])
        sp = _sigmoid(zgp[...].astype(F32))
        sc = _sigmoid(zgc[...].astype(F32))
        dyp = (dm * sp).astype(BF16)
        dyc = (dm * sc).astype(BF16)
        dzo[0] = (dm * ypc_ref[0].astype(F32) * sp * (1.0 - sp)).astype(BF16)
        dzo[1] = (dm * ypc_ref[1].astype(F32) * sc * (1.0 - sc)).astype(BF16)
        dyo[0] = dyp
        dyo[1] = dyc
        dpq[0] = _dot_tb(dyp, w_ref[0]).astype(BF16)
        dpq[1] = _dot_tb(dyc, w_ref[1]).astype(BF16)

    blocks = _nbytes((tm, d), BF16) * 3 + _nbytes((2, tm, d), BF16) * 4 + _nbytes((3, d, d), BF16)
    return _pcall(
        body, name="mixer_bwd", grid=(t // tm,),
        in_specs=[pl.BlockSpec((None, tm, d), lambda i: (2, i, 0)),
                  pl.BlockSpec((tm, d), lambda i: (i, 4)),
                  pl.BlockSpec((tm, d), lambda i: (i, 5)),
                  pl.BlockSpec((2, tm, d), lambda i: (0, i, 0)),
                  pl.BlockSpec((3, d, d), lambda i: (0, 0, 0))],
        out_specs=[pl.BlockSpec((2, tm, d), lambda i: (0, i, 0)),
                   pl.BlockSpec((2, tm, d), lambda i: (2, i, 0)),
                   pl.BlockSpec((2, tm, d), lambda i: (0, i, 0))],
        out_shape=[SDS(rhs3.shape, BF16), SDS((N_SPLITS, t, d), BF16), SDS((2, t, d), BF16)],
        input_output_aliases={0: 0},
        sem=("parallel",), blocks=blocks, temps=8 * _nbytes((tm, d), F32), comm=comm,
    )(rhs3, z, z, ypc, w3)


def _conv_bwd(dz, dpq, z, conv_w, nseq, comm=None):
    _, t, d = dz.shape
    s = t // nseq
    c = _tile(d, 128, LANES)
    nb = d // c

    def body(dz_in, dq_ref, zb, zc, zv, cw, dzo, gw_ref):
        del dz_in

        @pl.when(pl.program_id(1) == 0)
        def _():
            gw_ref[...] = jnp.zeros_like(gw_ref)

        row = lax.broadcasted_iota(jnp.int32, (s, c), 0)
        b = zb[...].astype(F32)
        cm = zc[...].astype(F32)
        v = zv[...].astype(F32)
        cv = cm * v
        cv1 = _shift_down(cv, 1, row)
        cv2 = _shift_down(cv, 2, row)
        w0, w1, w2 = cw[pl.ds(0, 1), :], cw[pl.ds(1, 1), :], cw[pl.ds(2, 1), :]
        cc = w2 * cv + w1 * cv1 + w0 * cv2
        dq = dq_ref[...].astype(F32)
        dzo[0] = (dq * cc).astype(BF16)
        dcc = dq * b
        gw_ref[pl.ds(0, 1), :] += jnp.sum(dcc * cv2, axis=0, keepdims=True)
        gw_ref[pl.ds(1, 1), :] += jnp.sum(dcc * cv1, axis=0, keepdims=True)
        gw_ref[pl.ds(2, 1), :] += jnp.sum(dcc * cv, axis=0, keepdims=True)
        dcv = w2 * dcc + w1 * _shift_up(dcc, 1, row) + w0 * _shift_up(dcc, 2, row)
        dzo[1] = (dcv * v).astype(BF16)
        dzo[2] = (dcv * cm).astype(BF16)

    blocks = 4 * _nbytes((s, c), BF16) + _nbytes((3, s, c), BF16)
    return _pcall(
        body, name="conv_bwd", grid=(nb, nseq),
        in_specs=[ANY,
                  pl.BlockSpec((None, s, c), lambda j, b: (1, b, j)),
                  pl.BlockSpec((s, c), lambda j, b: (b, nb + j)),
                  pl.BlockSpec((s, c), lambda j, b: (b, 2 * nb + j)),
                  pl.BlockSpec((s, c), lambda j, b: (b, 3 * nb + j)),
                  pl.BlockSpec((3, c), lambda j, b: (0, j))],
        out_specs=[pl.BlockSpec((3, s, c), lambda j, b: (0, b, j)),
                   pl.BlockSpec((3, c), lambda j, b: (0, j))],
        out_shape=[SDS(dz.shape, BF16), SDS((3, d), F32)],
        input_output_aliases={0: 0},
        sem=("parallel", "arbitrary"), blocks=blocks, temps=16 * _nbytes((s, c), F32), comm=comm,
    )(dz, dpq, z, z, z, conv_w)


def _pool_bwd_call(dz, dpq, z, pool_w, pool_scale, nseq, comm=None):
    _, t, d = dz.shape
    s = t // nseq
    c = d // N_GROUPS

    def body(dz_in, dp_ref, zp, pw, ps, dzo, gpw_ref, gps_ref):
        del dz_in
        j = pl.program_id(0)

        @pl.when(pl.program_id(1) == 0)
        def _():
            gpw_ref[...] = jnp.zeros_like(gpw_ref)
            gps_ref[...] = jnp.zeros_like(gps_ref)

        row = lax.broadcasted_iota(jnp.int32, (s, c), 0)
        for gi, win in enumerate(POOL_WINDOWS):
            @pl.when(j == gi)
            def _(win=win):
                pb = _pool_fwd(zp[...].astype(F32), win, row).astype(BF16)
                plin = _dot(pb, pw[...])
                dps = dp_ref[...].astype(F32)
                gps_ref[...] += jnp.sum(dps * plin, axis=0, keepdims=True)
                dplb = (dps * ps[...]).astype(BF16)
                gpw_ref[...] += _dot_ta(pb, dplb)
                dzo[...] = _pool_bwd(_dot_tb(dplb, pw[...]), win, row).astype(BF16)

    blocks = 3 * _nbytes((s, c), BF16) + _nbytes((c, c), BF16) + _nbytes((c, c), F32)
    return _pcall(
        body, name="pool_bwd", grid=(N_GROUPS, nseq),
        in_specs=[ANY,
                  pl.BlockSpec((None, s, c), lambda j, b: (0, b, j)),
                  pl.BlockSpec((s, c), lambda j, b: (b, j)),
                  pl.BlockSpec((None, c, c), lambda j, b: (j, 0, 0)),
                  pl.BlockSpec((1, c), lambda j, b: (0, j))],
        out_specs=[pl.BlockSpec((None, s, c), lambda j, b: (3, b, j)),
                   pl.BlockSpec((None, c, c), lambda j, b: (j, 0, 0)),
                   pl.BlockSpec((1, c), lambda j, b: (0, j))],
        out_shape=[SDS(dz.shape, BF16), SDS((N_GROUPS, c, c), F32), SDS((1, d), F32)],
        input_output_aliases={0: 0},
        sem=("parallel", "arbitrary"), blocks=blocks, temps=10 * _nbytes((s, c), F32), comm=comm,
    )(dz, dpq, z, pool_w, pool_scale)


def _dz_plane(zb):
    return jnp.where(zb < 4, (zb + 3) % 4, zb)


def _wgrad_in(h1, dz, nsh, comm=None):
    t, d = h1.shape
    ws = N_SPLITS * d // nsh
    kb = _tile(math.gcd(d, ws), 512, LANES)
    npl = d // kb
    nps = ws // kb
    tk = _tile(t, WGRAD_TOKENS_WIDE, SUBLANES_BF16)
    nk = t // tk

    def body(a_ref, b_ref, o_ref, *acc):
        part = _dot_ta(a_ref[...], b_ref[...])
        if nk == 1:
            o_ref[...] = part.astype(BF16)
        else:
            _acc_over(pl.program_id(1), nk, part, acc[0], o_ref)

    blocks = _nbytes((tk, d), BF16) + _nbytes((tk, kb), BF16) + _nbytes((d, kb), BF16)
    return _pcall(
        body, name="wgrad_in", grid=(N_SPLITS * npl, nk),
        in_specs=[pl.BlockSpec((tk, d), lambda cb, k: (k, 0), pipeline_mode=pl.Buffered(1) if nk == 1 else None),
                  pl.BlockSpec((None, tk, kb), lambda cb, k: (_dz_plane(cb // npl), k, cb % npl))],
        out_specs=[pl.BlockSpec((None, d, kb), lambda cb, k: (cb // nps, 0, cb % nps))],
        out_shape=[SDS((nsh, d, ws), BF16)],
        scratch_shapes=[] if nk == 1 else [pltpu.VMEM((d, kb), F32)],
        sem=("parallel", "arbitrary"), blocks=blocks, temps=2 * _nbytes((d, kb), F32), comm=comm,
    )(h1, dz)


def _mixer_bwd_dx(dz, w_in, x, dx1, g_mix, comm=None):
    npln, t, d = dz.shape
    nsh, _, ws = w_in.shape
    tm = _tile(t, 256, SUBLANES_BF16)
    kb = _tile(math.gcd(d, ws), 512, LANES)
    npl = d // kb
    nps = ws // kb

    def body(dz_ref, w_ref, x_ref, dx1_ref, g_ref, dx_ref, gg_ref):
        @pl.when(pl.program_id(0) == 0)
        def _():
            gg_ref[...] = jnp.zeros_like(gg_ref)

        dh = None
        for cb in range(npln * npl):
            zb = cb // npl
            plane = (zb + 3) % 4 if zb < 4 else zb
            part = _dot_tb(dz_ref[plane, :, (cb % npl) * kb:(cb % npl + 1) * kb],
                           w_ref[cb // nps, :, (cb % nps) * kb:(cb % nps + 1) * kb])
            dh = part if dh is None else dh + part
        xh, inv = _rms_fwd(x_ref[...])
        gg_ref[...] += jnp.sum(dh * xh, axis=0, keepdims=True)
        dx_ref[...] = dx1_ref[...] + _rms_bwd(dh, xh, inv, g_ref[...])

    blocks = _nbytes((npln, tm, d), BF16) + 3 * _nbytes((tm, d), F32)
    return _pcall(
        body, name="mixer_bwd_dx", grid=(t // tm,),
        in_specs=[pl.BlockSpec((npln, tm, d), lambda i: (0, i, 0)),
                  pl.BlockSpec((nsh, d, ws), lambda i: (0, 0, 0), pipeline_mode=pl.Buffered(1)),
                  pl.BlockSpec((tm, d), lambda i: (i, 0)),
                  pl.BlockSpec((tm, d), lambda i: (i, 0)),
                  pl.BlockSpec((1, d), lambda i: (0, 0))],
        out_specs=[pl.BlockSpec((tm, d), lambda i: (i, 0)),
                   pl.BlockSpec((1, d), lambda i: (0, 0))],
        out_shape=[SDS((t, d), F32), SDS((1, d), F32)],
        sem=("arbitrary",), blocks=blocks, temps=_nbytes(w_in.shape, BF16) + 8 * _nbytes((tm, d), F32), comm=comm,
    )(dz, w_in, x, dx1, g_mix)


N_BIG = 5
SHARD_MAJOR = (0, 2)
ROWS_DIM1 = (1, 4)


def _ds(start, size, align):
    if isinstance(start, int):
        return pl.ds(start, size)
    return pl.ds(pl.multiple_of(start, align), size)


def _piece(a, ref, k, h):
    if a in SHARD_MAJOR:
        r = ref.shape[1] // 2
        return ref.at[k, _ds(h * r, r, SUBLANES_BF16), :]
    if a in ROWS_DIM1:
        r = ref.shape[1] // 8
        return ref.at[:, _ds((2 * k + h) * r, r, SUBLANES_BF16), :]
    r = ref.shape[0] // 8
    return ref.at[_ds((2 * k + h) * r, r, SUBLANES_BF16), :]


def _half(a, ref, h):
    if a in ROWS_DIM1:
        r = ref.shape[1] // 2
        return ref.at[:, _ds(h * r, r, SUBLANES_BF16), :]
    r = ref.shape[0] // 2
    return ref.at[_ds(h * r, r, SUBLANES_BF16), :]


def _piece_shape(a, full_shape):
    if a in SHARD_MAJOR:
        return (full_shape[1] // 2, full_shape[2])
    if a in ROWS_DIM1:
        return (full_shape[0], full_shape[1] // 8, full_shape[2])
    return (full_shape[0] // 8, full_shape[1])


def _shard_shape(a, full_shape):
    if a in SHARD_MAJOR:
        return (full_shape[1], full_shape[2])
    if a in ROWS_DIM1:
        return (full_shape[0], full_shape[1] // 4, full_shape[2])
    return (full_shape[0] // 4, full_shape[1])


def _rows_axis(a):
    return 1 if a in ROWS_DIM1 else 0


def _piece_block(a, full_shape):
    ps = _piece_shape(a, full_shape)
    if a in SHARD_MAJOR:
        return (None,) + ps, lambda k, c: (k, c, 0)
    if a in ROWS_DIM1:
        return ps, lambda k, c: (0, 2 * k + c, 0)
    return ps, lambda k, c: (2 * k + c, 0)


def _coords():
    return lax.axis_index("x"), lax.axis_index("y"), lax.axis_index("c")


def _peer_chips(x, y):
    return [(1 - x, y), (x, 1 - y), (1 - x, 1 - y)]


def _remote(src, dst, ssem, rsem, dev):
    return pltpu.make_async_remote_copy(src_ref=src, dst_ref=dst, send_sem=ssem, recv_sem=rsem,
                                        device_id=dev, device_id_type=MESH)


def _dma_sems(*counts):
    return [pltpu.SemaphoreType.DMA((n,)) for n in counts]


def _symmetric(ins, out_shapes, sems, copies, peers, aliases=None):
    def start(cins, couts, csems):
        for cp in copies(cins, couts, csems):
            cp.start()

    def finish(cins, couts, csems):
        for cp in copies(cins, couts, csems):
            cp.wait()

    return _Comm(ins, out_shapes, sems, start, finish, peers, aliases)


def _rows_part(a, ref, part):
    if part is None:
        return ref
    p, q, n = part
    ax = _rows_axis(a)
    r = ref.shape[ax] // n
    return ref.at[tuple(pl.ds(p * r, (q - p) * r) if d == ax else slice(None) for d in range(len(ref.shape)))]


def _merge(comms):
    ins, outs, sems, aliases, spans = [], [], [], {}, []
    for cm in comms:
        spans.append((len(ins), len(outs), len(sems)))
        for i, o in cm.aliases.items():
            aliases[len(ins) + i] = len(outs) + o
        ins += cm.ins
        outs += cm.out_shapes
        sems += cm.sems

    def each(fn_name):
        def run(cins, couts, csems):
            for cm, (i0, o0, s0) in zip(comms, spans):
                fn = getattr(cm, fn_name)
                if fn is not None:
                    fn(cins[i0:i0 + len(cm.ins)], couts[o0:o0 + len(cm.out_shapes)], csems[s0:s0 + len(cm.sems)])
        return run

    return _Comm(ins, outs, sems, each("start"), each("finish"), frozenset().union(*[cm.peers for cm in comms]),
                 aliases, mid=each("mid") if any(cm.mid is not None for cm in comms) else None)


def _gather_comm(arrs, locs, full_shapes, part=None, into=None):
    n = len(arrs)

    def own(cins, couts, csems):
        x, y, c = _coords()
        j = 2 * x + y
        return [_remote(_rows_part(a, _half(a, cins[q], h), part), _rows_part(a, _piece(a, couts[q], j, h), part),
                        csems[0].at[2 * q + h], csems[1].at[2 * q + h], (x, y, 1 - c))
                for q, a in enumerate(arrs) for h in range(2)]

    def sends(cins, couts, csems):
        x, y, c = _coords()
        j = 2 * x + y
        return [_remote(_rows_part(a, _half(a, cins[q], c), part), _rows_part(a, _piece(a, couts[q], j, c), part),
                        csems[2].at[3 * q + i], csems[3].at[3 * q + i], (px, py, c))
                for q, a in enumerate(arrs) for i, (px, py) in enumerate(_peer_chips(x, y))]

    def forwards(couts, csems, half_of):
        x, y, c = _coords()
        out = []
        for q, a in enumerate(arrs):
            for i, (px, py) in enumerate(_peer_chips(x, y)):
                landed = _rows_part(a, _piece(a, couts[q], 2 * px + py, half_of(c)), part)
                out.append(_remote(landed, landed, csems[4].at[3 * q + i], csems[5].at[3 * q + i], (x, y, 1 - c)))
        return out

    def start(cins, couts, csems):
        for cp in sends(cins, couts, csems) + own(cins, couts, csems):
            cp.start()

    def finish(cins, couts, csems):
        fw = forwards(couts, csems, lambda c: c)
        for cp, f in zip(sends(cins, couts, csems), fw):
            cp.wait_recv()
            f.start()
        for f in forwards(couts, csems, lambda c: 1 - c):
            f.wait_recv()
        for cp in sends(cins, couts, csems) + fw:
            cp.wait_send()
        for cp in own(cins, couts, csems):
            cp.wait()

    ins = [locs[a] for a in arrs] + ([into[a] for a in arrs] if into else [])
    return _Comm(ins, [SDS(full_shapes[a], BF16) for a in arrs],
                 _dma_sems(2 * n, 2 * n, 3 * n, 3 * n, 3 * n, 3 * n), start, finish, CHIPS + (SIBLING,),
                 aliases={n + q: q for q in range(n)} if into else None)


def _ring_gather_comm(arrs, locs, full_shapes):
    n = len(arrs)

    def own(cins, couts, csems):
        x, y, c = _coords()
        j = 2 * x + y
        return [_remote(_half(a, cins[q], h), _piece(a, couts[q], j, h), csems[0].at[2 * q + h],
                        csems[1].at[2 * q + h], (x, y, 1 - c)) for q, a in enumerate(arrs) for h in range(2)]

    def sends(cins, couts, csems):
        x, y, c = _coords()
        j = 2 * x + y
        return [_remote(_half(a, cins[q], c), _piece(a, couts[q], j, c), csems[2].at[2 * q + i],
                        csems[3].at[2 * q + i], (px, py, c))
                for q, a in enumerate(arrs) for i, (px, py) in enumerate(_peer_chips(x, y)[:2])]

    def relays(couts, csems):
        x, y, c = _coords()
        peers = _peer_chips(x, y)
        out = []
        for q, a in enumerate(arrs):
            for r, (src_p, dst_p) in enumerate(((0, 1), (1, 0))):
                sx, sy = peers[src_p]
                rows = _rows_part(a, _piece(a, couts[q], 2 * sx + sy, c), (r, r + 1, 2))
                out.append(_remote(rows, rows, csems[6].at[2 * q + r], csems[7].at[2 * q + r], (*peers[dst_p], c)))
        return out

    def forwards(couts, csems, half_of, which):
        x, y, c = _coords()
        out = []
        for q, a in enumerate(arrs):
            for i in which:
                px, py = _peer_chips(x, y)[i]
                landed = _piece(a, couts[q], 2 * px + py, half_of(c))
                out.append(_remote(landed, landed, csems[4].at[3 * q + i], csems[5].at[3 * q + i], (x, y, 1 - c)))
        return out

    def start(cins, couts, csems):
        for cp in sends(cins, couts, csems) + own(cins, couts, csems):
            cp.start()

    def mid(cins, couts, csems):
        for cp in sends(cins, couts, csems):
            cp.wait_recv()
        for cp in relays(couts, csems) + forwards(couts, csems, lambda c: c, (0, 1)):
            cp.start()

    def finish(cins, couts, csems):
        for cp in relays(couts, csems):
            cp.wait_recv()
        fw_diag = forwards(couts, csems, lambda c: c, (2,))
        for f in fw_diag:
            f.start()
        for f in forwards(couts, csems, lambda c: 1 - c, (0, 1, 2)):
            f.wait_recv()
        for cp in (sends(cins, couts, csems) + relays(couts, csems)
                   + forwards(couts, csems, lambda c: c, (0, 1)) + fw_diag):
            cp.wait_send()
        for cp in own(cins, couts, csems):
            cp.wait()

    return _Comm([locs[a] for a in arrs], [SDS(full_shapes[a], BF16) for a in arrs],
                 _dma_sems(2 * n, 2 * n, 2 * n, 2 * n, 3 * n, 3 * n, 2 * n, 2 * n), start, finish,
                 CHIPS + (SIBLING,), mid=mid)


def _halves_comm(arrs, gbs):
    n = len(arrs)

    def copies(cins, couts, csems):
        x, y, c = _coords()
        return [_remote(_piece(a, cins[q], k, 1 - c), couts[q].at[k], csems[0].at[4 * q + k], csems[1].at[4 * q + k],
                        (x, y, 1 - c)) for q, a in enumerate(arrs) for k in range(4)]

    return _symmetric([gbs[a] for a in arrs], [SDS((4,) + _piece_shape(a, gbs[a].shape), BF16) for a in arrs],
                      _dma_sems(4 * n, 4 * n), copies, [SIBLING])


def _chips_comm(arrs, ps, part=None, into=None):
    n = len(arrs)

    def copies(cins, couts, csems):
        x, y, c = _coords()
        return [_remote(_rows_part(a, cins[q].at[2 * px + py], part), _rows_part(a, couts[q].at[i], part),
                        csems[0].at[3 * q + i], csems[1].at[3 * q + i], (px, py, c))
                for q, a in enumerate(arrs) for i, (px, py) in enumerate(_peer_chips(x, y))]

    ins = [ps[a] for a in arrs] + ([into[a] for a in arrs] if into else [])
    return _symmetric(ins, [SDS((3,) + ps[a].shape[1:], BF16) for a in arrs], _dma_sems(3 * n, 3 * n), copies, CHIPS,
                      aliases={n + q: q for q in range(n)} if into else None)


def _result_comm(arrs, gs):
    n = len(arrs)

    def copies(cins, couts, csems):
        x, y, c = _coords()
        return [_remote(_half(a, cins[q], c), _half(a, couts[q], c), csems[0].at[q], csems[1].at[q], (x, y, 1 - c))
                for q, a in enumerate(arrs)]

    return _symmetric([gs[a] for a in arrs], [SDS(gs[a].shape, F32) for a in arrs], _dma_sems(n, n), copies,
                      [SIBLING], aliases={q: q for q in range(n)})


def _add_halves(arrs, gbs, lands, c_arr, name):
    n = len(arrs)

    def body(c_ref, *refs):
        del c_ref
        for q in range(n):
            refs[2 * n + q][...] = (refs[q][...].astype(F32) + refs[n + q][...].astype(F32)).astype(BF16)

    g_specs, l_specs, o_specs, blocks = [], [], [], 0
    for a in arrs:
        bs, imap = _piece_block(a, gbs[a].shape)
        ps = _piece_shape(a, gbs[a].shape)
        g_specs.append(pl.BlockSpec(bs, lambda k, c_ref, imap=imap: imap(k, c_ref[0])))
        nd = len(ps)
        l_specs.append(pl.BlockSpec((None,) + ps, lambda k, c_ref, nd=nd: (k,) + (0,) * nd))
        o_specs.append(pl.BlockSpec((None,) + ps, lambda k, c_ref, nd=nd: (k,) + (0,) * nd))
        blocks += 3 * _nbytes(ps, BF16)
    return list(pl.pallas_call(
        body, name=name,
        grid_spec=pltpu.PrefetchScalarGridSpec(
            num_scalar_prefetch=1, grid=(4,), in_specs=g_specs + l_specs, out_specs=o_specs),
        out_shape=[SDS((4,) + _piece_shape(a, gbs[a].shape), BF16) for a in arrs],
        compiler_params=_params(("parallel",), blocks, blocks),
    )(c_arr, *[gbs[a] for a in arrs], *lands))


def _sum_chips(a, p, land, shard_shape, jc_arr, name):
    ps = land.shape[1:]
    ax = _rows_axis(a)
    rows = ps[ax]
    nsub = 2 if rows % (2 * SUBLANES_BF16) == 0 else 1
    bs = tuple(r // nsub if q == ax else r for q, r in enumerate(ps))
    nd = len(ps)

    def at_rows(v):
        return tuple(v if q == ax else 0 for q in range(nd))

    def body(jc_ref, p_ref, l_ref, o_ref):
        del jc_ref
        acc = p_ref[...].astype(F32) + l_ref[0].astype(F32)
        acc = acc + l_ref[1].astype(F32)
        o_ref[...] = acc + l_ref[2].astype(F32)

    blocks = 4 * _nbytes(bs, BF16) + _nbytes(bs, F32)
    return pl.pallas_call(
        body, name=name,
        grid_spec=pltpu.PrefetchScalarGridSpec(
            num_scalar_prefetch=1, grid=(nsub,),
            in_specs=[pl.BlockSpec((None,) + bs, lambda s, jc: (jc[0],) + at_rows(s)),
                      pl.BlockSpec((3,) + bs, lambda s, jc: (0,) + at_rows(s))],
            out_specs=pl.BlockSpec(bs, lambda s, jc: at_rows(jc[1] * nsub + s))),
        out_shape=SDS(shard_shape, F32),
        compiler_params=_params(("parallel",), blocks, 2 * _nbytes(bs, F32)),
    )(jc_arr, p, land)


def _small_comm(v):
    rows = v.shape[0]

    def copies(cins, couts, csems):
        x, y, c = _coords()
        me = 4 * x + 2 * y + c
        out = [pltpu.make_async_copy(cins[0], couts[0].at[me], csems[0].at[0])]
        for dlt in range(1, 8):
            px = 1 - x if (dlt >> 2) & 1 else x
            py = 1 - y if (dlt >> 1) & 1 else y
            pc = 1 - c if dlt & 1 else c
            out.append(_remote(cins[0], couts[0].at[me], csems[1].at[dlt - 1], csems[2].at[dlt - 1], (px, py, pc)))
        return out

    return _symmetric([v], [SDS((8, rows, LANES), F32)], _dma_sems(1, 7, 7), copies, EVERYONE)


def _sum8(slots, name):
    def body(s_ref, o_ref):
        acc = s_ref[0]
        for i in range(1, 8):
            acc = acc + s_ref[i]
        o_ref[...] = acc

    return pl.pallas_call(
        body, name=name,
        in_specs=[pl.BlockSpec(memory_space=pltpu.VMEM)], out_specs=pl.BlockSpec(memory_space=pltpu.VMEM),
        out_shape=SDS(slots.shape[1:], F32),
    )(slots)


def _adamw(w, g, m, v, name, g_plane=None):
    rows, cols = w.shape
    tr = _tile(rows, max(SUBLANES_F32, (256 * 1024 // cols) // SUBLANES_F32 * SUBLANES_F32), SUBLANES_F32)

    def body(w_ref, g_ref, m_ref, v_ref, go_ref, d_ref, mo_ref, vo_ref):
        gr = g_ref[...]
        mn = ADAM_B1 * m_ref[...] + (1.0 - ADAM_B1) * gr
        vn = ADAM_B2 * v_ref[...] + (1.0 - ADAM_B2) * (gr * gr)
        m_hat = mn / (1.0 - ADAM_B1 ** ADAM_STEP)
        v_hat = vn / (1.0 - ADAM_B2 ** ADAM_STEP)
        d_ref[...] = -ADAM_LR * (m_hat / (jnp.sqrt(v_hat) + ADAM_EPS) + ADAM_WD * w_ref[...])
        go_ref[...] = gr
        mo_ref[...] = mn
        vo_ref[...] = vn

    spec = pl.BlockSpec((tr, cols), lambda i: (i, 0))
    g_spec = spec if g_plane is None else pl.BlockSpec((None, tr, cols), lambda i: (g_plane, i, 0))
    return pl.pallas_call(
        body, name=name, grid=(rows // tr,),
        in_specs=[spec, g_spec, spec, spec], out_specs=[spec, spec, spec, spec],
        out_shape=_out_hbm([SDS((rows, cols), F32)] * 4),
        compiler_params=_params(("parallel",), 8 * _nbytes((tr, cols), F32), 4 * _nbytes((tr, cols), F32)),
    )(*_in_hbm([w, g, m, v]))


def _pack(parts):
    rows = []
    for p in parts:
        r = p.reshape(-1, LANES)
        pad = (-r.shape[0]) % SUBLANES_F32
        if pad:
            r = jnp.pad(r, ((0, pad), (0, 0)))
        rows.append(r)
    return jnp.concatenate(rows, axis=0)


def _unpack(packed, shapes):
    out, at = [], 0
    for s in shapes:
        n = 1
        for q in s:
            n *= q
        r = n // LANES
        out.append(packed[at:at + r].reshape(s))
        at += r + (-r) % SUBLANES_F32
    return out


def kernel(x, norm_mix, w_in, pool_w, pool_scale, w_pool_proj, conv_w, w_conv_out, w_o, norm_ffn, w_up, ffn_conv_w, ffn_conv_b, w_down, norm_final, loss_target, m_norm_mix, m_w_in, m_pool_w, m_pool_scale, m_w_pool_proj, m_conv_w, m_w_conv_out, m_w_o, m_norm_ffn, m_w_up, m_ffn_conv_w, m_ffn_conv_b, m_w_down, m_norm_final, v_norm_mix, v_w_in, v_pool_w, v_pool_scale, v_w_pool_proj, v_conv_w, v_w_conv_out, v_w_o, v_norm_ffn, v_w_up, v_ffn_conv_w, v_ffn_conv_b, v_w_down, v_norm_final):
    nseq, seq, d = x.shape
    t = nseq * seq
    f = w_down.shape[1] * 4
    c = d // N_GROUPS
    xy = lax.axis_index("x") * 2 + lax.axis_index("y")
    c_arr = lax.axis_index("c").astype(jnp.int32).reshape(1)
    jc_arr = jnp.stack([xy, lax.axis_index("c")]).astype(jnp.int32)
    nsh = 4
    zero = jnp.zeros((), jnp.int32)

    locs = [w_in[0].astype(BF16),
            jnp.stack([w_pool_proj[0], w_conv_out[0], w_o[0]]).astype(BF16),
            w_up[0].astype(BF16), w_down[0].astype(BF16), pool_w[0].astype(BF16)]
    full_shapes = [(nsh, d, N_SPLITS * d // nsh), (3, d, d), (nsh, d, 2 * f // nsh), (f, d), (N_GROUPS, c, c)]

    cw_pad = lax.dynamic_update_slice(jnp.zeros((3, d), F32), conv_w[0], (zero, xy * (d // 4)))
    fw_pad = lax.dynamic_update_slice(jnp.zeros((3, 2 * f), F32), ffn_conv_w[0], (zero, xy * (f // 2)))
    small_w = _pack([cw_pad, fw_pad]) * 0.5

    x2d = x.reshape(t, d)
    tgt = loss_target.reshape(t, d)
    ax, ay = lax.axis_index("x"), lax.axis_index("y")
    order = jnp.stack([xy, 2 * (1 - ax) + ay, 2 * ax + 1 - ay, 2 * (1 - ax) + 1 - ay]).astype(jnp.int32)
    (z, h1, w_in_f), (pool_w_f, w3_f, slots_w) = _fwd_in(
        x2d, norm_mix, locs[0], order,
        _merge([_gather_comm([4], locs, full_shapes), _gather_comm([1], locs, full_shapes, part=(0, 1, 2)),
                _small_comm(small_w)]))
    conv_w_f, ffn_cw_f = _unpack(_sum8(slots_w, "sum8_weights"), [(3, d), (3, 2 * f)])
    ffn_cw_p = ffn_cw_f.reshape(3, 2, f).transpose(1, 0, 2)
    ffn_cb_p = ffn_conv_b.reshape(2, 1, f)
    (lhs3,), (w3_f,) = _mixer_mid_fwd(z, pool_w_f, pool_scale, conv_w_f, nseq,
                                      _gather_comm([1], locs, full_shapes, part=(1, 2, 2), into={1: w3_f}))
    (lhs3, ypc, x1, h2), (w_up_f,) = _mixer_out(lhs3, z, x2d, w3_f, norm_ffn,
                                                _ring_gather_comm([2], locs, full_shapes))
    (u0,), (w_down_f,) = _ffn_up(h2, w_up_f, f, _gather_comm([3], locs, full_shapes))
    act, ua = _ffn_mid_fwd(u0, ffn_cw_p, ffn_cb_p, nseq)
    dx2, dx2b, loss11, g_norm_final = _ffn_down_loss(act, w_down_f, x1, tgt, norm_final.reshape(1, d))

    gbs, lands, ps, lands2, rs = {}, {}, {}, {}, {}
    tn_up = _tile(2 * f // nsh, 1408, LANES)
    npp = f // tn_up

    def add(arrs, name):
        for a, p in zip(arrs, _add_halves(arrs, gbs, [lands[a] for a in arrs], c_arr, name)):
            ps[a] = p

    def summed(a):
        rs[a] = _sum_chips(a, ps[a], lands2[a], _shard_shape(a, full_shapes[a]), jc_arr, "sum_chips_%d" % a)

    (gbs[3],), _ = _wgrad(act, dx2b, "wgrad_down", tr=tn_up, tn=d)
    (da,), (lands[3],) = _ffn_bwd_da(dx2b, w_down_f, _halves_comm([3], gbs))
    add([3], "add_halves_down")
    (du0, g_ffn_cw_p, g_ffn_cb_p), (lands2[3],) = _ffn_mid_bwd(da, u0, ua, ffn_cw_p, nseq, _chips_comm([3], ps))
    summed(3)
    (gbs[2],), (rs[3],) = _wgrad(h2, du0, "wgrad_up", tr=d, tn=tn_up, b_plane_of=lambda n: (n // npp, n % npp),
                                 out_shards=nsh, comm=_result_comm([3], rs))
    (dx1, rhs3, g_norm_ffn), (lands[2],) = _ffn_bwd_dx1(du0, w_up_f, x1, dx2, norm_ffn, 3, _halves_comm([2], gbs))
    add([2], "add_halves_up")
    (rhs3, dz, dpq), (lands2[2],) = _mixer_bwd(rhs3, z, ypc, w3_f, _chips_comm([2], ps, part=(0, 1, 2)))
    (gbs[1],), (lands2[2],) = _wgrad3(lhs3, rhs3, _chips_comm([2], ps, part=(1, 2, 2), into=lands2))
    summed(2)
    (dz, g_conv_w), (lands[1], rs[2]) = _conv_bwd(dz, dpq, z, conv_w_f, nseq,
                                                  _merge([_halves_comm([1], gbs), _result_comm([2], rs)]))
    add([1], "add_halves_sq3")
    (dz, g_pool_w, g_pool_scale), _ = _pool_bwd_call(dz, dpq, z, pool_w_f, pool_scale, nseq)
    gbs[4] = g_pool_w.astype(BF16)
    (gbs[0],), (lands2[1],) = _wgrad_in(h1, dz, nsh, _chips_comm([1], ps))
    summed(1)
    lands[0], lands[4] = _run_comm(_halves_comm([0, 4], gbs), "exchange_halves_in")
    add([0, 4], "add_halves_in")
    g_ffn_cw = g_ffn_cw_p.transpose(1, 0, 2).reshape(3, 2 * f)
    small_a = _pack([g_pool_scale, g_norm_ffn, g_ffn_cb_p.reshape(1, 2 * f), g_norm_final.reshape(d), g_conv_w,
                     g_ffn_cw, jnp.pad(loss11, ((0, SUBLANES_F32 - 1), (0, LANES - 1)))])
    (grad_x, g_norm_mix), (lands2[0], lands2[4], rs[1], slots_a) = _mixer_bwd_dx(
        dz, w_in_f, x2d, dx1, norm_mix,
        _merge([_chips_comm([0, 4], ps), _result_comm([1], rs), _small_comm(small_a)]))
    summed(0)
    summed(4)
    rs[0], rs[4], slots_b = _run_comm(_merge([_result_comm([0, 4], rs), _small_comm(_pack([g_norm_mix]))]),
                                      "exchange_result_in")
    shapes_a = [(1, d), (1, d), (1, 2 * f), (d,), (3, d), (3, 2 * f), (SUBLANES_F32, LANES)]
    gs_pool_scale, gs_norm_ffn, gs_ffn_cb, gs_norm_final, gs_conv_w, gs_ffn_cw, loss_blk = _unpack(
        _sum8(slots_a, "sum8_grads"), shapes_a)
    (gs_norm_mix,) = _unpack(_sum8(slots_b, "sum8_norm_mix"), [(1, d)])
    gs_conv_w = lax.dynamic_slice(gs_conv_w, (zero, xy * (d // 4)), (3, d // 4))
    gs_ffn_cw = lax.dynamic_slice(gs_ffn_cw, (zero, xy * (f // 2)), (3, f // 2))

    def upd(w, g, m, v, name, g_plane=None):
        shape = w.shape
        rows = 1
        for q in shape[:-1]:
            rows *= q
        g2 = g if g_plane is not None else g.reshape(rows, shape[-1])
        outs = _adamw(w.reshape(rows, shape[-1]), g2, m.reshape(rows, shape[-1]), v.reshape(rows, shape[-1]),
                      name, g_plane)
        return [o.reshape(shape) for o in outs]

    res = {
        "w_in": upd(w_in, rs[0], m_w_in, v_w_in, "adamw_w_in"),
        "pool_w": upd(pool_w, rs[4], m_pool_w, v_pool_w, "adamw_pool_w"),
        "w_pool_proj": upd(w_pool_proj, rs[1], m_w_pool_proj, v_w_pool_proj, "adamw_w_pool_proj", 0),
        "w_conv_out": upd(w_conv_out, rs[1], m_w_conv_out, v_w_conv_out, "adamw_w_conv_out", 1),
        "w_o": upd(w_o, rs[1], m_w_o, v_w_o, "adamw_w_o", 2),
        "w_up": upd(w_up, rs[2], m_w_up, v_w_up, "adamw_w_up"),
        "w_down": upd(w_down, rs[3], m_w_down, v_w_down, "adamw_w_down"),
    }

    small_names = ["norm_mix", "pool_scale", "norm_ffn", "ffn_conv_b", "norm_final", "conv_w", "ffn_conv_w"]
    small_ws = [norm_mix, pool_scale, norm_ffn, ffn_conv_b, norm_final, conv_w, ffn_conv_w]
    small_ms = [m_norm_mix, m_pool_scale, m_norm_ffn, m_ffn_conv_b, m_norm_final, m_conv_w, m_ffn_conv_w]
    small_vs = [v_norm_mix, v_pool_scale, v_norm_ffn, v_ffn_conv_b, v_norm_final, v_conv_w, v_ffn_conv_w]
    small_gs = [gs_norm_mix, gs_pool_scale, gs_norm_ffn, gs_ffn_cb, gs_norm_final, gs_conv_w, gs_ffn_cw]
    _, sd, sm, sv = _adamw(_pack(small_ws), _pack(small_gs), _pack(small_ms), _pack(small_vs), "adamw_small")
    shapes = [w.shape for w in small_ws]
    sd, sm, sv = _unpack(sd, shapes), _unpack(sm, shapes), _unpack(sv, shapes)
    for i, nm in enumerate(small_names):
        res[nm] = [small_gs[i].reshape(shapes[i]), sd[i], sm[i], sv[i]]

    order = ["norm_mix", "w_in", "pool_w", "pool_scale", "w_pool_proj", "conv_w", "w_conv_out", "w_o", "norm_ffn",
             "w_up", "ffn_conv_w", "ffn_conv_b", "w_down", "norm_final"]
    return (loss_blk[0, 0], grad_x.reshape(x.shape), *[res[n][0] for n in order], *[res[n][1] for n in order],
            *[res[n][2] for n in order], *[res[n][3] for n in order])
```

```python
import math

import jax
import jax.numpy as jnp
from jax import lax
from jax.experimental import pallas as pl
from jax.experimental.pallas import tpu as pltpu

F32 = jnp.float32
BF16 = jnp.bfloat16
SDS = jax.ShapeDtypeStruct
MESH = pl.DeviceIdType.MESH

RMS_EPS = 1e-6
POOL_WINDOWS = (2, 4, 8, 16)
N_GROUPS = len(POOL_WINDOWS)
N_SPLITS = 6

ADAM_LR = 0.001
ADAM_B1 = 0.9
ADAM_B2 = 0.999
ADAM_EPS = 1e-08
ADAM_WD = 0.01
ADAM_STEP = 10

LANES = 128
SUBLANES_F32 = 8
SUBLANES_BF16 = 16
VMEM_BYTES = 64 * 1024 * 1024
VMEM_CAP = VMEM_BYTES - 8 * 1024 * 1024
VMEM_FLOOR = 16 * 1024 * 1024

ANY = pl.BlockSpec(memory_space=pl.ANY)


def _tile(dim, pref, align):
    if dim <= pref:
        return dim
    t = (pref // align) * align
    while t >= align:
        if dim % t == 0:
            return t
        t -= align
    return dim


def _nbytes(shape, dtype):
    n = 1
    for s in shape:
        n *= s
    return n * jnp.dtype(dtype).itemsize


def _params(sem, block_bytes, temp_bytes=0, collective_id=None):
    need = 2 * block_bytes + temp_bytes + 4 * 1024 * 1024
    return pltpu.CompilerParams(dimension_semantics=sem, collective_id=collective_id,
                                vmem_limit_bytes=int(min(max(need, VMEM_FLOOR), VMEM_CAP)))


SIBLING = (0, 0, 1)
CHIPS = ((1, 0, 0), (0, 1, 0), (1, 1, 0))
EVERYONE = tuple((a, b, c) for a in range(2) for b in range(2) for c in range(2) if a + b + c)
PEER_SETS = (frozenset([SIBLING]), frozenset(CHIPS), frozenset(CHIPS + (SIBLING,)), frozenset(EVERYONE))
MID_AT = 0.75


def _collective_id(peers):
    return PEER_SETS.index(frozenset(peers))


def _handshake(peers):
    x, y, c = lax.axis_index("x"), lax.axis_index("y"), lax.axis_index("c")
    bar = pltpu.get_barrier_semaphore()
    for fx, fy, fc in sorted(peers):
        dev = (1 - x if fx else x, 1 - y if fy else y, 1 - c if fc else c)
        pl.semaphore_signal(bar, inc=1, device_id=dev, device_id_type=MESH)
    pl.semaphore_wait(bar, len(peers))


class _Comm:
    def __init__(self, ins, out_shapes, sems, start, finish, peers, aliases=None, mid=None):
        self.ins = list(ins)
        self.out_shapes = list(out_shapes)
        self.sems = list(sems)
        self.start = start
        self.finish = finish
        self.mid = mid
        self.peers = frozenset(peers)
        self.aliases = dict(aliases or {})


def _pcall(body, *, name, grid, in_specs, out_specs, out_shape, sem, blocks, temps=0, scratch_shapes=(),
           input_output_aliases=None, comm=None):
    in_specs = list(in_specs)
    out_specs = list(out_specs)
    out_shape = list(out_shape)
    scratch_shapes = list(scratch_shapes)
    aliases = dict(input_output_aliases or {})
    n_in, n_out, n_scr = len(in_specs), len(out_shape), len(scratch_shapes)
    if comm is None:
        call = pl.pallas_call(
            body, name=name, grid=grid, in_specs=in_specs, out_specs=out_specs, out_shape=out_shape,
            scratch_shapes=scratch_shapes, input_output_aliases=aliases,
            compiler_params=_params(sem, blocks, temps))
        return lambda *args: (list(call(*args)), [])

    nci, nco = len(comm.ins), len(comm.out_shapes)
    n_steps = 1
    for g in grid:
        n_steps *= g

    def hosted(*refs):
        ins = refs[:n_in]
        cins = refs[n_in:n_in + nci]
        outs = refs[n_in + nci:n_in + nci + n_out]
        couts = refs[n_in + nci + n_out:n_in + nci + n_out + nco]
        scr = refs[n_in + nci + n_out + nco:n_in + nci + n_out + nco + n_scr]
        csems = refs[n_in + nci + n_out + nco + n_scr:]
        first = None
        last = None
        step = 0
        for q, g in enumerate(grid):
            pid = pl.program_id(q)
            first = (pid == 0) if first is None else first & (pid == 0)
            last = (pid == g - 1) if last is None else last & (pid == g - 1)
            step = step * g + pid

        @pl.when(first)
        def _():
            _handshake(comm.peers)
            comm.start(cins, couts, csems)

        if comm.mid is not None:
            @pl.when(step == int(MID_AT * n_steps))
            def _():
                comm.mid(cins, couts, csems)

        body(*ins, *outs, *scr)

        @pl.when(last)
        def _():
            comm.finish(cins, couts, csems)

    for i, o in comm.aliases.items():
        aliases[n_in + i] = n_out + o
    call = pl.pallas_call(
        hosted, name=name, grid=grid, in_specs=in_specs + [ANY] * nci, out_specs=out_specs + [ANY] * nco,
        out_shape=out_shape + comm.out_shapes, scratch_shapes=scratch_shapes + comm.sems,
        input_output_aliases=aliases,
        compiler_params=_params(("arbitrary",) * len(grid), blocks, temps, _collective_id(comm.peers)))

    def run(*args):
        res = call(*args, *comm.ins)
        return list(res[:n_out]), list(res[n_out:])

    return run


def _run_comm(comm, name):
    def body(*refs):
        nci, nco = len(comm.ins), len(comm.out_shapes)
        cins, couts, csems = refs[:nci], refs[nci:nci + nco], refs[nci + nco:]
        _handshake(comm.peers)
        comm.start(cins, couts, csems)
        if comm.mid is not None:
            comm.mid(cins, couts, csems)
        comm.finish(cins, couts, csems)

    return list(pl.pallas_call(
        body, name=name, in_specs=[ANY] * len(comm.ins), out_specs=[ANY] * len(comm.out_shapes),
        out_shape=comm.out_shapes, scratch_shapes=comm.sems, input_output_aliases=comm.aliases,
        compiler_params=pltpu.CompilerParams(collective_id=_collective_id(comm.peers)),
    )(*comm.ins))


def _dot(a, b):
    return jnp.dot(a, b, preferred_element_type=F32)


def _dot_tb(a, b):
    return lax.dot_general(a, b, (((1,), (1,)), ((), ())), preferred_element_type=F32)


def _dot_ta(a, b):
    return lax.dot_general(a, b, (((0,), (0,)), ((), ())), preferred_element_type=F32)


def _rms_fwd(x):
    inv = lax.rsqrt(jnp.mean(x * x, axis=-1, keepdims=True) + RMS_EPS)
    return x * inv, inv


def _rms_bwd(dy, xhat, inv, g):
    gd = dy * g
    return inv * (gd - xhat * jnp.mean(gd * xhat, axis=-1, keepdims=True))


def _sigmoid(x):
    return 1.0 / (1.0 + jnp.exp(-x))


def _shift_down(x, k, row):
    return jnp.where(row >= k, pltpu.roll(x, k, 0), 0.0)


def _shift_up(x, k, row):
    s = x.shape[0]
    return jnp.where(row < s - k, pltpu.roll(x, s - k, 0), 0.0)


def _pool_fwd(u, win, row):
    s = u
    k = 1
    while k < win:
        s = s + _shift_down(s, k, row)
        k *= 2
    cnt = jnp.minimum(row + 1, win).astype(F32)
    return s / cnt - u


def _pool_bwd(dp, win, row):
    cnt = jnp.minimum(row + 1, win).astype(F32)
    s = dp / cnt
    k = 1
    while k < win:
        s = s + _shift_up(s, k, row)
        k *= 2
    return s - dp


def _acc_over(k, nk, part, acc, o_ref):
    @pl.when(k == 0)
    def _():
        acc[...] = part

    @pl.when(k > 0)
    def _():
        acc[...] += part

    @pl.when(k == nk - 1)
    def _():
        o_ref[...] = acc[...].astype(o_ref.dtype)


def _fwd_in(x, g, w_loc, order, comm):
    t, d = x.shape
    ws = w_loc.shape[1]
    nsh = order.shape[0]
    assert nsh == 4, "the shard walk below is written for the 2 x 2 chips of the mesh"
    tm = _tile(t, 1024, SUBLANES_BF16)
    ni = t // tm
    nci, nco = len(comm.ins), len(comm.out_shapes)
    all_peers = comm.peers | frozenset(CHIPS + (SIBLING,))

    def body(order_ref, x_ref, g_ref, loc_ref, *rest):
        del order_ref
        cins = rest[:nci]
        z_ref, h_ref, full_ref = rest[nci:nci + 3]
        couts = rest[nci + 3:nci + 3 + nco]
        (hs, wbuf, wsem, own_s, own_r, snd_s, snd_r, fwd_s, fwd_r, rly_s, rly_r) = rest[nci + 3 + nco:nci + 14 + nco]
        csems = rest[nci + 14 + nco:]
        j = pl.program_id(0)
        i = pl.program_id(1)
        x_, y_, c_ = _coords()
        own = 2 * x_ + y_
        sib = (x_, y_, 1 - c_)
        peers = _peer_chips(x_, y_)

        def sends():
            return [_remote(_half(0, loc_ref, c_), _piece(0, full_ref, own, c_), snd_s.at[p], snd_r.at[p], (px, py, c_))
                    for p, (px, py) in enumerate(peers[:2])]

        def relays():
            out = []
            for q, (src_p, dst_p) in enumerate(((0, 1), (1, 0))):
                sx, sy = peers[src_p]
                part = _rows_part(0, _piece(0, full_ref, 2 * sx + sy, c_), (q, q + 1, 2))
                out.append(_remote(part, part, rly_s.at[q], rly_r.at[q], (*peers[dst_p], c_)))
            return out

        def owns():
            return [_remote(_half(0, loc_ref, h), _piece(0, full_ref, own, h), own_s.at[h], own_r.at[h], sib)
                    for h in range(2)]

        def forward(p, half):
            px, py = peers[p]
            landed = _piece(0, full_ref, 2 * px + py, half)
            return _remote(landed, landed, fwd_s.at[p], fwd_r.at[p], sib)

        def load(src, slot):
            return pltpu.make_async_copy(src, wbuf.at[slot], wsem.at[slot])

        @pl.when((j == 0) & (i == 0))
        def _():
            _handshake(all_peers)
            for cp in sends() + owns():
                cp.start()
            load(loc_ref, 0).start()

        @pl.when(j == 0)
        def _():
            xh, _ = _rms_fwd(x_ref[...])
            h = (xh * g_ref[...]).astype(BF16)
            hs[pl.ds(pl.multiple_of(i * tm, tm), tm), :] = h
            h_ref[...] = h

        slot = j % 2

        @pl.when(i == 0)
        def _():
            load(loc_ref, slot).wait()

        z_ref[...] = _dot(hs[pl.ds(pl.multiple_of(i * tm, tm), tm), :], wbuf[slot]).astype(BF16)

        for p in range(nsh - 1):
            @pl.when((i == ni - 1) & (j == p))
            def _(p=p):
                px, py = peers[p]
                if p == 0:
                    for cp in sends():
                        cp.wait_recv()
                    for cp in relays():
                        cp.start()
                if p == 2:
                    for cp in relays():
                        cp.wait_recv()
                forward(p, c_).start()
                forward(p, 1 - c_).wait_recv()
                load(full_ref.at[2 * px + py], 1 - slot).start()
                if p == 0:
                    comm.start(cins, couts, csems)

        @pl.when((j == nsh - 1) & (i == ni - 1))
        def _():
            for cp in sends() + relays() + [forward(p, c_) for p in range(nsh - 1)]:
                cp.wait_send()
            for cp in owns():
                cp.wait()
            comm.finish(cins, couts, csems)

    last = ni - 1
    blocks = _nbytes((tm, d), F32) + _nbytes((tm, ws), BF16) + _nbytes((tm, d), BF16)
    scratch = _nbytes((t, d), BF16) + 2 * _nbytes((d, ws), BF16)
    res = pl.pallas_call(
        body, name="fwd_in",
        grid_spec=pltpu.PrefetchScalarGridSpec(
            num_scalar_prefetch=1, grid=(nsh, ni),
            in_specs=[pl.BlockSpec((tm, d), lambda j, i, o: (jnp.where(j == 0, i, last), 0)),
                      pl.BlockSpec((1, d), lambda j, i, o: (0, 0)), ANY] + [ANY] * nci,
            out_specs=[pl.BlockSpec((tm, ws), lambda j, i, o: (i, o[j])),
                       pl.BlockSpec((tm, d), lambda j, i, o: (jnp.where(j == 0, i, last), 0)), ANY] + [ANY] * nco,
            scratch_shapes=[pltpu.VMEM((t, d), BF16), pltpu.VMEM((2, d, ws), BF16)]
            + _dma_sems(2, 2, 2, 2, 2, nsh - 1, nsh - 1, 2, 2) + comm.sems),
        out_shape=[SDS((t, nsh * ws), BF16), SDS((t, d), BF16), SDS((nsh, d, ws), BF16)] + comm.out_shapes,
        input_output_aliases={4 + i: 3 + o for i, o in comm.aliases.items()},
        compiler_params=_params(("arbitrary", "arbitrary"), blocks, scratch + 3 * _nbytes((tm, d), F32),
                                _collective_id(all_peers)),
    )(order, x, g, w_loc, *comm.ins)
    return list(res[:3]), list(res[3:])


def _mixer_mid_fwd(z, pool_w, pool_scale, conv_w, nseq, comm=None):
    t = z.shape[0]
    d = pool_scale.shape[1]
    s = t // nseq
    c = d // N_GROUPS

    def body(zp, zb, zc, zv, pw, ps, cw, o):
        j = pl.program_id(1)
        row = lax.broadcasted_iota(jnp.int32, (s, c), 0)
        for gi, win in enumerate(POOL_WINDOWS):
            @pl.when(j == gi)
            def _(win=win):
                pooled = _pool_fwd(zp[...].astype(F32), win, row)
                o[0] = (_dot(pooled.astype(BF16), pw[...]) * ps[...]).astype(BF16)

        cv = zc[...].astype(F32) * zv[...].astype(F32)
        cc = (cw[pl.ds(2, 1), :] * cv + cw[pl.ds(1, 1), :] * _shift_down(cv, 1, row)
              + cw[pl.ds(0, 1), :] * _shift_down(cv, 2, row))
        o[1] = (zb[...].astype(F32) * cc).astype(BF16)

    blocks = 4 * _nbytes((s, c), BF16) + _nbytes((c, c), BF16) + _nbytes((2, s, c), BF16)
    return _pcall(
        body, name="mixer_mid_fwd", grid=(nseq, N_GROUPS),
        in_specs=[pl.BlockSpec((s, c), lambda b, j: (b, j)),
                  pl.BlockSpec((s, c), lambda b, j: (b, N_GROUPS + j)),
                  pl.BlockSpec((s, c), lambda b, j: (b, 2 * N_GROUPS + j)),
                  pl.BlockSpec((s, c), lambda b, j: (b, 3 * N_GROUPS + j)),
                  pl.BlockSpec((None, c, c), lambda b, j: (j, 0, 0)),
                  pl.BlockSpec((1, c), lambda b, j: (0, j)),
                  pl.BlockSpec((3, c), lambda b, j: (0, j))],
        out_specs=[pl.BlockSpec((2, s, c), lambda b, j: (0, b, j))],
        out_shape=[SDS((3, t, d), BF16)],
        sem=("parallel", "parallel"), blocks=blocks, temps=8 * _nbytes((s, c), F32), comm=comm,
    )(z, z, z, z, pool_w, pool_scale, conv_w)


def _mixer_out(lhs3, z, x, w3, g_ffn, comm=None):
    t, d = x.shape
    tm = _tile(t, 256, SUBLANES_BF16)

    def body(pq, zgp, zgc, x_ref, w_ref, g_ref, mrg, ypc, x1o, h2o):
        yp = _dot(pq[0], w_ref[0])
        yc = _dot(pq[1], w_ref[1])
        m = _sigmoid(zgp[...].astype(F32)) * yp + _sigmoid(zgc[...].astype(F32)) * yc
        mb = m.astype(BF16)
        x1 = x_ref[...] + _dot(mb, w_ref[2])
        ypc[0] = yp.astype(BF16)
        ypc[1] = yc.astype(BF16)
        mrg[...] = mb
        x1o[...] = x1
        xh, _ = _rms_fwd(x1)
        h2o[...] = (xh * g_ref[...]).astype(BF16)

    blocks = (_nbytes((2, tm, d), BF16) * 2 + _nbytes((tm, d), BF16) * 4 + _nbytes((tm, d), F32) * 2
              + _nbytes((3, d, d), BF16))
    return _pcall(
        body, name="mixer_out", grid=(t // tm,),
        in_specs=[pl.BlockSpec((2, tm, d), lambda i: (0, i, 0)),
                  pl.BlockSpec((tm, d), lambda i: (i, 4)),
                  pl.BlockSpec((tm, d), lambda i: (i, 5)),
                  pl.BlockSpec((tm, d), lambda i: (i, 0)),
                  pl.BlockSpec((3, d, d), lambda i: (0, 0, 0)),
                  pl.BlockSpec((1, d), lambda i: (0, 0))],
        out_specs=[pl.BlockSpec((None, tm, d), lambda i: (2, i, 0)),
                   pl.BlockSpec((2, tm, d), lambda i: (0, i, 0)),
                   pl.BlockSpec((tm, d), lambda i: (i, 0)),
                   pl.BlockSpec((tm, d), lambda i: (i, 0))],
        out_shape=[SDS(lhs3.shape, BF16), SDS((2, t, d), BF16), SDS((t, d), F32), SDS((t, d), BF16)],
        input_output_aliases={0: 0},
        sem=("parallel",), blocks=blocks, temps=8 * _nbytes((tm, d), F32), comm=comm,
    )(lhs3, z, z, x, w3, g_ffn)


def _ffn_up(h2, w_up, f, comm=None):
    t, d = h2.shape
    _, _, ws = w_up.shape
    tm = _tile(t, 1024, SUBLANES_BF16)
    tn = _tile(ws, 1408, LANES)
    nps = ws // tn
    npp = f // tn

    def body(h_ref, w_ref, o_ref):
        o_ref[...] = _dot(h_ref[...], w_ref[...]).astype(BF16)

    blocks = _nbytes((tm, d), BF16) + _nbytes((d, tn), BF16) + _nbytes((tm, tn), BF16)
    return _pcall(
        body, name="ffn_up", grid=(t // tm, 2 * npp),
        in_specs=[pl.BlockSpec((tm, d), lambda i, j: (i, 0)),
                  pl.BlockSpec((None, d, tn), lambda i, j: (j // nps, 0, j % nps))],
        out_specs=[pl.BlockSpec((None, tm, tn), lambda i, j: (j // npp, i, j % npp))],
        out_shape=[SDS((2, t, f), BF16)],
        sem=("parallel", "parallel"), blocks=blocks, temps=_nbytes((tm, tn), F32), comm=comm,
    )(h2, w_up)


def _conv3_rows(u, u1, u2, w_ref, p):
    return w_ref[p, pl.ds(2, 1), :] * u + w_ref[p, pl.ds(1, 1), :] * u1 + w_ref[p, pl.ds(0, 1), :] * u2


WGRAD_TOKENS = 2048
WGRAD_TOKENS_WIDE = 4096
CHUNK = 64
HALO = SUBLANES_F32


def _up1_up2(u, nxt):
    rows = u.shape[0]
    ext = jnp.concatenate([u, nxt], axis=0)
    n = rows + HALO
    return pltpu.roll(ext, n - 1, 0)[:rows], pltpu.roll(ext, n - 2, 0)[:rows]


def _fold8(x):
    return jnp.sum(x.reshape(x.shape[0] // SUBLANES_F32, SUBLANES_F32, x.shape[1]), axis=0)


def _ffn_mid_fwd(u0, cw, cb, nseq):
    _, t, f = u0.shape
    s = t // nseq
    c = _tile(f, 256, LANES)

    def body(u_ref, w_ref, b_ref, a_ref, uo_ref):
        row = lax.broadcasted_iota(jnp.int32, (s, c), 0)
        act = []
        for p in range(2):
            u = u_ref[p].astype(F32)
            act.append(_conv3_rows(u, _shift_down(u, 1, row), _shift_down(u, 2, row), w_ref, p) + b_ref[p])
            uo_ref[p] = act[p].astype(BF16)
        ug, uv = act
        a_ref[...] = (ug * _sigmoid(ug) * uv).astype(BF16)

    blocks = 2 * _nbytes((2, s, c), BF16) + _nbytes((s, c), BF16)
    outs, _ = _pcall(
        body, name="ffn_mid_fwd", grid=(f // c, nseq),
        in_specs=[pl.BlockSpec((2, s, c), lambda j, b: (0, b, j)),
                  pl.BlockSpec((2, 3, c), lambda j, b: (0, 0, j)),
                  pl.BlockSpec((2, 1, c), lambda j, b: (0, 0, j))],
        out_specs=[pl.BlockSpec((s, c), lambda j, b: (b, j)),
                   pl.BlockSpec((2, s, c), lambda j, b: (0, b, j))],
        out_shape=[SDS((t, f), BF16), SDS((2, t, f), BF16)],
        sem=("parallel", "parallel"), blocks=blocks, temps=8 * _nbytes((s, c), F32),
    )(u0, cw, cb)
    return outs


def _ffn_down_loss(a, w_down, x1, tgt, g_fin):
    t, f = a.shape
    d = x1.shape[1]
    tm = _tile(t, 256, SUBLANES_BF16)
    nsteps = t // tm

    def body(a_ref, w_ref, x1_ref, t_ref, g_ref, dx_ref, dxb_ref, loss_ref, gg_ref, lacc):
        i = pl.program_id(0)

        @pl.when(i == 0)
        def _():
            lacc[...] = jnp.zeros_like(lacc)
            gg_ref[...] = jnp.zeros_like(gg_ref)

        x2 = x1_ref[...] + _dot(a_ref[...], w_ref[...])
        xh, inv = _rms_fwd(x2)
        g = g_ref[...]
        e = xh * g - t_ref[...]
        lacc[...] += jnp.sum(e * e, axis=0, keepdims=True)
        dy = e * (1.0 / d)
        gg_ref[...] += jnp.sum(dy * xh, axis=0, keepdims=True)
        dx2 = _rms_bwd(dy, xh, inv, g)
        dx_ref[...] = dx2
        dxb_ref[...] = dx2.astype(BF16)

        @pl.when(i == nsteps - 1)
        def _():
            loss_ref[...] = jnp.sum(lacc[...], axis=1, keepdims=True) * (0.5 / d)

    blocks = (_nbytes((tm, f), BF16) + _nbytes((f, d), BF16) + 3 * _nbytes((tm, d), F32) + _nbytes((tm, d), BF16))
    outs, _ = _pcall(
        body, name="ffn_down_loss", grid=(nsteps,),
        in_specs=[pl.BlockSpec((tm, f), lambda i: (i, 0)), pl.BlockSpec((f, d), lambda i: (0, 0)),
                  pl.BlockSpec((tm, d), lambda i: (i, 0)), pl.BlockSpec((tm, d), lambda i: (i, 0)),
                  pl.BlockSpec((1, d), lambda i: (0, 0))],
        out_specs=[pl.BlockSpec((tm, d), lambda i: (i, 0)), pl.BlockSpec((tm, d), lambda i: (i, 0)),
                   pl.BlockSpec((1, 1), lambda i: (0, 0)), pl.BlockSpec((1, d), lambda i: (0, 0))],
        out_shape=[SDS((t, d), F32), SDS((t, d), BF16), SDS((1, 1), F32), SDS((1, d), F32)],
        scratch_shapes=[pltpu.VMEM((1, d), F32)],
        sem=("arbitrary",), blocks=blocks, temps=8 * _nbytes((tm, d), F32),
    )(a, w_down, x1, tgt, g_fin)
    return outs


def _ffn_bwd_da(dxb, w_down, comm=None):
    t, d = dxb.shape
    f = w_down.shape[0]
    tm = _tile(t, 512, SUBLANES_BF16)

    def body(x_ref, w_ref, o_ref):
        o_ref[...] = _dot_tb(x_ref[...], w_ref[...]).astype(BF16)

    blocks = _nbytes((tm, d), BF16) + _nbytes((tm, f), BF16)
    return _pcall(
        body, name="ffn_bwd_da", grid=(t // tm,),
        in_specs=[pl.BlockSpec((tm, d), lambda i: (i, 0)),
                  pl.BlockSpec((f, d), lambda i: (0, 0), pipeline_mode=pl.Buffered(1))],
        out_specs=[pl.BlockSpec((tm, f), lambda i: (i, 0))],
        out_shape=[SDS((t, f), BF16)],
        sem=("parallel",), blocks=blocks, temps=_nbytes((f, d), BF16) + _nbytes((tm, f), F32), comm=comm,
    )(dxb, w_down)


def _ffn_mid_bwd(da, u0, ua, cw, nseq, comm=None):
    _, t, f = u0.shape
    s = t // nseq
    c = _tile(f, 128, LANES)
    r = _tile(s, CHUNK, SUBLANES_BF16)
    n = s // r

    def body(da_ref, u_ref, ua_ref, w_ref, du_ref, gw_ref, gb_ref):
        @pl.when(pl.program_id(1) == 0)
        def _():
            gw_ref[...] = jnp.zeros_like(gw_ref)
            gb_ref[...] = jnp.zeros_like(gb_ref)

        def step(i, carry):
            nxt, sums = carry
            rows = pl.ds(pl.multiple_of((n - 1 - i) * r, r), r)
            ug = ua_ref[0, rows, :].astype(F32)
            uv = ua_ref[1, rows, :].astype(F32)
            sg = _sigmoid(ug)
            dacc = da_ref[rows, :].astype(F32)
            dus = (dacc * uv * sg * (1.0 + ug * (1.0 - sg)), dacc * (ug * sg))
            first, new_sums = [], []
            for p in range(2):
                du = dus[p]
                d1, d2 = _up1_up2(du, nxt[p])
                du_ref[p, rows, :] = _conv3_rows(du, d1, d2, w_ref, p).astype(BF16)
                u = u_ref[p, rows, :].astype(F32)
                sb, s0, s1, s2 = sums[p]
                new_sums.append((sb + _fold8(du), s0 + _fold8(d2 * u), s1 + _fold8(d1 * u), s2 + _fold8(du * u)))
                first.append(du[:HALO])
            return tuple(first), tuple(new_sums)

        zero = jnp.zeros((HALO, c), F32)
        _, sums = lax.fori_loop(0, n, step, ((zero, zero), ((zero,) * 4,) * 2))
        for p in range(2):
            sb, s0, s1, s2 = sums[p]
            gb_ref[p] += jnp.sum(sb, axis=0, keepdims=True)
            gw_ref[p, pl.ds(0, 1), :] += jnp.sum(s0, axis=0, keepdims=True)
            gw_ref[p, pl.ds(1, 1), :] += jnp.sum(s1, axis=0, keepdims=True)
            gw_ref[p, pl.ds(2, 1), :] += jnp.sum(s2, axis=0, keepdims=True)

    blocks = _nbytes((s, c), BF16) + 3 * _nbytes((2, s, c), BF16)
    return _pcall(
        body, name="ffn_mid_bwd", grid=(f // c, nseq),
        in_specs=[pl.BlockSpec((s, c), lambda j, b: (b, j)),
                  pl.BlockSpec((2, s, c), lambda j, b: (0, b, j)),
                  pl.BlockSpec((2, s, c), lambda j, b: (0, b, j)),
                  pl.BlockSpec((2, 3, c), lambda j, b: (0, 0, j))],
        out_specs=[pl.BlockSpec((2, s, c), lambda j, b: (0, b, j)),
                   pl.BlockSpec((2, 3, c), lambda j, b: (0, 0, j)),
                   pl.BlockSpec((2, 1, c), lambda j, b: (0, 0, j))],
        out_shape=[SDS((2, t, f), BF16), SDS((2, 3, f), F32), SDS((2, 1, f), F32)],
        sem=("parallel", "arbitrary"), blocks=blocks, temps=4 * 1024 * 1024, comm=comm,
    )(da, u0, ua, cw)


def _wgrad(a, b, name, *, tr, tn, b_plane_of=None, out_shards=None, comm=None):
    t, m = a.shape
    n_total = b.shape[-1] * (b.shape[0] if b.ndim == 3 else 1)
    tk = _tile(t, WGRAD_TOKENS_WIDE if n_total > tn and m == tr else WGRAD_TOKENS, SUBLANES_BF16)
    nk = t // tk
    once = pl.Buffered(1) if nk == 1 else None

    def body(a_ref, b_ref, o_ref, *acc):
        part = _dot_ta(a_ref[...], b_ref[...])
        if nk == 1:
            o_ref[...] = part.astype(BF16)
        else:
            _acc_over(pl.program_id(2), nk, part, acc[0], o_ref)

    if b.ndim == 3:
        b_spec = pl.BlockSpec((None, tk, tn), lambda r, n, k: (b_plane_of(n)[0], k, b_plane_of(n)[1]))
    else:
        b_spec = pl.BlockSpec((tk, tn), lambda r, n, k: (k, n), pipeline_mode=once if n_total == tn else None)
    if out_shards is None:
        o_spec = pl.BlockSpec((tr, tn), lambda r, n, k: (r, n))
        o_shape = SDS((m, n_total), BF16)
    else:
        nps = n_total // out_shards // tn
        o_spec = pl.BlockSpec((None, tr, tn), lambda r, n, k: (n // nps, r, n % nps))
        o_shape = SDS((out_shards, m, n_total // out_shards), BF16)
    blocks = _nbytes((tk, tr), BF16) + _nbytes((tk, tn), BF16) + _nbytes((tr, tn), BF16)
    return _pcall(
        body, name=name, grid=(m // tr, n_total // tn, nk),
        in_specs=[pl.BlockSpec((tk, tr), lambda r, n, k: (k, r), pipeline_mode=once if m == tr else None), b_spec],
        out_specs=[o_spec], out_shape=[o_shape],
        scratch_shapes=[] if nk == 1 else [pltpu.VMEM((tr, tn), F32)],
        sem=("parallel", "parallel", "arbitrary"), blocks=blocks, temps=2 * _nbytes((tr, tn), F32), comm=comm,
    )(a, b)


def _wgrad3(lhs3, rhs3, comm=None):
    nw, t, d = lhs3.shape
    tk = _tile(t, WGRAD_TOKENS, SUBLANES_BF16)
    nk = t // tk

    def body(a_ref, b_ref, o_ref, *acc):
        part = _dot_ta(a_ref[...], b_ref[...])
        if nk == 1:
            o_ref[...] = part.astype(BF16)
        else:
            _acc_over(pl.program_id(1), nk, part, acc[0], o_ref)

    blocks = 2 * _nbytes((tk, d), BF16) + _nbytes((d, d), BF16)
    return _pcall(
        body, name="wgrad_sq3", grid=(nw, nk),
        in_specs=[pl.BlockSpec((None, tk, d), lambda w, k: (w, k, 0)),
                  pl.BlockSpec((None, tk, d), lambda w, k: (w, k, 0))],
        out_specs=[pl.BlockSpec((None, d, d), lambda w, k: (w, 0, 0))],
        out_shape=[SDS((nw, d, d), BF16)],
        scratch_shapes=[] if nk == 1 else [pltpu.VMEM((d, d), F32)],
        sem=("parallel", "arbitrary"), blocks=blocks, temps=2 * _nbytes((d, d), F32), comm=comm,
    )(lhs3, rhs3)


def _ffn_bwd_dx1(du0, w_up, x1, dx2, g_ffn, n_planes_out, comm=None):
    _, t, f = du0.shape
    d = x1.shape[1]
    nsh, _, ws = w_up.shape
    tm = _tile(t, 256, SUBLANES_BF16)
    spp = f // ws

    def body(du_ref, w_ref, x1_ref, dx2_ref, g_ref, dx1_ref, dxb_ref, gg_ref):
        @pl.when(pl.program_id(0) == 0)
        def _():
            gg_ref[...] = jnp.zeros_like(gg_ref)

        dh = None
        for k in range(nsh):
            part = _dot_tb(du_ref[k // spp, :, (k % spp) * ws:(k % spp + 1) * ws], w_ref[k])
            dh = part if dh is None else dh + part
        xh, inv = _rms_fwd(x1_ref[...])
        gg_ref[...] += jnp.sum(dh * xh, axis=0, keepdims=True)
        dx1 = dx2_ref[...] + _rms_bwd(dh, xh, inv, g_ref[...])
        dx1_ref[...] = dx1
        dxb_ref[...] = dx1.astype(BF16)

    blocks = _nbytes((2, tm, f), BF16) + 3 * _nbytes((tm, d), F32) + _nbytes((tm, d), BF16)
    return _pcall(
        body, name="ffn_bwd_dx1", grid=(t // tm,),
        in_specs=[pl.BlockSpec((2, tm, f), lambda i: (0, i, 0)),
                  pl.BlockSpec((nsh, d, ws), lambda i: (0, 0, 0), pipeline_mode=pl.Buffered(1)),
                  pl.BlockSpec((tm, d), lambda i: (i, 0)),
                  pl.BlockSpec((tm, d), lambda i: (i, 0)),
                  pl.BlockSpec((1, d), lambda i: (0, 0))],
        out_specs=[pl.BlockSpec((tm, d), lambda i: (i, 0)),
                   pl.BlockSpec((None, tm, d), lambda i: (n_planes_out - 1, i, 0)),
                   pl.BlockSpec((1, d), lambda i: (0, 0))],
        out_shape=[SDS((t, d), F32), SDS((n_planes_out, t, d), BF16), SDS((1, d), F32)],
        sem=("arbitrary",), blocks=blocks, temps=_nbytes(w_up.shape, BF16) + 8 * _nbytes((tm, d), F32), comm=comm,
    )(du0, w_up, x1, dx2, g_ffn)


def _mixer_bwd(rhs3, z, ypc, w3, comm=None):
    _, t, d = rhs3.shape
    tm = _tile(t, 256, SUBLANES_BF16)

    def body(dx_ref, zgp, zgc, ypc_ref, w_ref, dyo, dzo, dpq):
        dm = _dot_tb(dx_ref[...], w_ref[2])
        sp = _sigmoid(zgp[...].astype(F32))
        sc = _sigmoid(zgc[...].astype(F32))
        dyp = (dm * sp).astype(BF16)
        dyc = (dm * sc).astype(BF16)
        dzo[0] = (dm * ypc_ref[0].astype(F32) * sp * (1.0 - sp)).astype(BF16)
        dzo[1] = (dm * ypc_ref[1].astype(F32) * sc * (1.0 - sc)).astype(BF16)
        dyo[0] = dyp
        dyo[1] = dyc
        dpq[0] = _dot_tb(dyp, w_ref[0]).astype(BF16)
        dpq[1] = _dot_tb(dyc, w_ref[1]).astype(BF16)

    blocks = _nbytes((tm, d), BF16) * 3 + _nbytes((2, tm, d), BF16) * 4 + _nbytes((3, d, d), BF16)
    return _pcall(
        body, name="mixer_bwd", grid=(t // tm,),
        in_specs=[pl.BlockSpec((None, tm, d), lambda i: (2, i, 0)),
                  pl.BlockSpec((tm, d), lambda i: (i, 4)),
                  pl.BlockSpec((tm, d), lambda i: (i, 5)),
                  pl.BlockSpec((2, tm, d), lambda i: (0, i, 0)),
                  pl.BlockSpec((3, d, d), lambda i: (0, 0, 0))],
        out_specs=[pl.BlockSpec((2, tm, d), lambda i: (0, i, 0)),
                   pl.BlockSpec((2, tm, d), lambda i: (2, i, 0)),
                   pl.BlockSpec((2, tm, d), lambda i: (0, i, 0))],
        out_shape=[SDS(rhs3.shape, BF16), SDS((N_SPLITS, t, d), BF16), SDS((2, t, d), BF16)],
        input_output_aliases={0: 0},
        sem=("parallel",), blocks=blocks, temps=8 * _nbytes((tm, d), F32), comm=comm,
    )(rhs3, z, z, ypc, w3)


def _conv_bwd(dz, dpq, z, conv_w, nseq, comm=None):
    _, t, d = dz.shape
    s = t // nseq
    c = _tile(d, 128, LANES)
    nb = d // c

    def body(dz_in, dq_ref, zb, zc, zv, cw, dzo, gw_ref):
        del dz_in

        @pl.when(pl.program_id(1) == 0)
        def _():
            gw_ref[...] = jnp.zeros_like(gw_ref)

        row = lax.broadcasted_iota(jnp.int32, (s, c), 0)
        b = zb[...].astype(F32)
        cm = zc[...].astype(F32)
        v = zv[...].astype(F32)
        cv = cm * v
        cv1 = _shift_down(cv, 1, row)
        cv2 = _shift_down(cv, 2, row)
        w0, w1, w2 = cw[pl.ds(0, 1), :], cw[pl.ds(1, 1), :], cw[pl.ds(2, 1), :]
        cc = w2 * cv + w1 * cv1 + w0 * cv2
        dq = dq_ref[...].astype(F32)
        dzo[0] = (dq * cc).astype(BF16)
        dcc = dq * b
        gw_ref[pl.ds(0, 1), :] += jnp.sum(dcc * cv2, axis=0, keepdims=True)
        gw_ref[pl.ds(1, 1), :] += jnp.sum(dcc * cv1, axis=0, keepdims=True)
        gw_ref[pl.ds(2, 1), :] += jnp.sum(dcc * cv, axis=0, keepdims=True)
        dcv = w2 * dcc + w1 * _shift_up(dcc, 1, row) + w0 * _shift_up(dcc, 2, row)
        dzo[1] = (dcv * v).astype(BF16)
        dzo[2] = (dcv * cm).astype(BF16)

    blocks = 4 * _nbytes((s, c), BF16) + _nbytes((3, s, c), BF16)
    return _pcall(
        body, name="conv_bwd", grid=(nb, nseq),
        in_specs=[ANY,
                  pl.BlockSpec((None, s, c), lambda j, b: (1, b, j)),
                  pl.BlockSpec((s, c), lambda j, b: (b, nb + j)),
                  pl.BlockSpec((s, c), lambda j, b: (b, 2 * nb + j)),
                  pl.BlockSpec((s, c), lambda j, b: (b, 3 * nb + j)),
                  pl.BlockSpec((3, c), lambda j, b: (0, j))],
        out_specs=[pl.BlockSpec((3, s, c), lambda j, b: (0, b, j)),
                   pl.BlockSpec((3, c), lambda j, b: (0, j))],
        out_shape=[SDS(dz.shape, BF16), SDS((3, d), F32)],
        input_output_aliases={0: 0},
        sem=("parallel", "arbitrary"), blocks=blocks, temps=16 * _nbytes((s, c), F32), comm=comm,
    )(dz, dpq, z, z, z, conv_w)


def _pool_bwd_call(dz, dpq, z, pool_w, pool_scale, nseq, comm=None):
    _, t, d = dz.shape
    s = t // nseq
    c = d // N_GROUPS

    def body(dz_in, dp_ref, zp, pw, ps, dzo, gpw_ref, gps_ref):
        del dz_in
        j = pl.program_id(0)

        @pl.when(pl.program_id(1) == 0)
        def _():
            gpw_ref[...] = jnp.zeros_like(gpw_ref)
            gps_ref[...] = jnp.zeros_like(gps_ref)

        row = lax.broadcasted_iota(jnp.int32, (s, c), 0)
        for gi, win in enumerate(POOL_WINDOWS):
            @pl.when(j == gi)
            def _(win=win):
                pb = _pool_fwd(zp[...].astype(F32), win, row).astype(BF16)
                plin = _dot(pb, pw[...])
                dps = dp_ref[...].astype(F32)
                gps_ref[...] += jnp.sum(dps * plin, axis=0, keepdims=True)
                dplb = (dps * ps[...]).astype(BF16)
                gpw_ref[...] += _dot_ta(pb, dplb)
                dzo[...] = _pool_bwd(_dot_tb(dplb, pw[...]), win, row).astype(BF16)

    blocks = 3 * _nbytes((s, c), BF16) + _nbytes((c, c), BF16) + _nbytes((c, c), F32)
    return _pcall(
        body, name="pool_bwd", grid=(N_GROUPS, nseq),
        in_specs=[ANY,
                  pl.BlockSpec((None, s, c), lambda j, b: (0, b, j)),
                  pl.BlockSpec((s, c), lambda j, b: (b, j)),
                  pl.BlockSpec((None, c, c), lambda j, b: (j, 0, 0)),
                  pl.BlockSpec((1, c), lambda j, b: (0, j))],
        out_specs=[pl.BlockSpec((None, s, c), lambda j, b: (3, b, j)),
                   pl.BlockSpec((None, c, c), lambda j, b: (j, 0, 0)),
                   pl.BlockSpec((1, c), lambda j, b: (0, j))],
        out_shape=[SDS(dz.shape, BF16), SDS((N_GROUPS, c, c), F32), SDS((1, d), F32)],
        input_output_aliases={0: 0},
        sem=("parallel", "arbitrary"), blocks=blocks, temps=10 * _nbytes((s, c), F32), comm=comm,
    )(dz, dpq, z, pool_w, pool_scale)


def _dz_plane(zb):
    return jnp.where(zb < 4, (zb + 3) % 4, zb)


def _wgrad_in(h1, dz, nsh, comm=None):
    t, d = h1.shape
    ws = N_SPLITS * d // nsh
    kb = _tile(math.gcd(d, ws), 512, LANES)
    npl = d // kb
    nps = ws // kb
    tk = _tile(t, WGRAD_TOKENS_WIDE, SUBLANES_BF16)
    nk = t // tk

    def body(a_ref, b_ref, o_ref, *acc):
        part = _dot_ta(a_ref[...], b_ref[...])
        if nk == 1:
            o_ref[...] = part.astype(BF16)
        else:
            _acc_over(pl.program_id(1), nk, part, acc[0], o_ref)

    blocks = _nbytes((tk, d), BF16) + _nbytes((tk, kb), BF16) + _nbytes((d, kb), BF16)
    return _pcall(
        body, name="wgrad_in", grid=(N_SPLITS * npl, nk),
        in_specs=[pl.BlockSpec((tk, d), lambda cb, k: (k, 0), pipeline_mode=pl.Buffered(1) if nk == 1 else None),
                  pl.BlockSpec((None, tk, kb), lambda cb, k: (_dz_plane(cb // npl), k, cb % npl))],
        out_specs=[pl.BlockSpec((None, d, kb), lambda cb, k: (cb // nps, 0, cb % nps))],
        out_shape=[SDS((nsh, d, ws), BF16)],
        scratch_shapes=[] if nk == 1 else [pltpu.VMEM((d, kb), F32)],
        sem=("parallel", "arbitrary"), blocks=blocks, temps=2 * _nbytes((d, kb), F32), comm=comm,
    )(h1, dz)


def _mixer_bwd_dx(dz, w_in, x, dx1, g_mix, comm=None):
    npln, t, d = dz.shape
    nsh, _, ws = w_in.shape
    tm = _tile(t, 256, SUBLANES_BF16)
    kb = _tile(math.gcd(d, ws), 512, LANES)
    npl = d // kb
    nps = ws // kb

    def body(dz_ref, w_ref, x_ref, dx1_ref, g_ref, dx_ref, gg_ref):
        @pl.when(pl.program_id(0) == 0)
        def _():
            gg_ref[...] = jnp.zeros_like(gg_ref)

        dh = None
        for cb in range(npln * npl):
            zb = cb // npl
            plane = (zb + 3) % 4 if zb < 4 else zb
            part = _dot_tb(dz_ref[plane, :, (cb % npl) * kb:(cb % npl + 1) * kb],
                           w_ref[cb // nps, :, (cb % nps) * kb:(cb % nps + 1) * kb])
            dh = part if dh is None else dh + part
        xh, inv = _rms_fwd(x_ref[...])
        gg_ref[...] += jnp.sum(dh * xh, axis=0, keepdims=True)
        dx_ref[...] = dx1_ref[...] + _rms_bwd(dh, xh, inv, g_ref[...])

    blocks = _nbytes((npln, tm, d), BF16) + 3 * _nbytes((tm, d), F32)
    return _pcall(
        body, name="mixer_bwd_dx", grid=(t // tm,),
        in_specs=[pl.BlockSpec((npln, tm, d), lambda i: (0, i, 0)),
                  pl.BlockSpec((nsh, d, ws), lambda i: (0, 0, 0), pipeline_mode=pl.Buffered(1)),
                  pl.BlockSpec((tm, d), lambda i: (i, 0)),
                  pl.BlockSpec((tm, d), lambda i: (i, 0)),
                  pl.BlockSpec((1, d), lambda i: (0, 0))],
        out_specs=[pl.BlockSpec((tm, d), lambda i: (i, 0)),
                   pl.BlockSpec((1, d), lambda i: (0, 0))],
        out_shape=[SDS((t, d), F32), SDS((1, d), F32)],
        sem=("arbitrary",), blocks=blocks, temps=_nbytes(w_in.shape, BF16) + 8 * _nbytes((tm, d), F32), comm=comm,
    )(dz, w_in, x, dx1, g_mix)


N_BIG = 5
SHARD_MAJOR = (0, 2)
ROWS_DIM1 = (1, 4)


def _ds(start, size, align):
    if isinstance(start, int):
        return pl.ds(start, size)
    return pl.ds(pl.multiple_of(start, align), size)


def _piece(a, ref, k, h):
    if a in SHARD_MAJOR:
        r = ref.shape[1] // 2
        return ref.at[k, _ds(h * r, r, SUBLANES_BF16), :]
    if a in ROWS_DIM1:
        r = ref.shape[1] // 8
        return ref.at[:, _ds((2 * k + h) * r, r, SUBLANES_BF16), :]
    r = ref.shape[0] // 8
    return ref.at[_ds((2 * k + h) * r, r, SUBLANES_BF16), :]


def _half(a, ref, h):
    if a in ROWS_DIM1:
        r = ref.shape[1] // 2
        return ref.at[:, _ds(h * r, r, SUBLANES_BF16), :]
    r = ref.shape[0] // 2
    return ref.at[_ds(h * r, r, SUBLANES_BF16), :]


def _piece_shape(a, full_shape):
    if a in SHARD_MAJOR:
        return (full_shape[1] // 2, full_shape[2])
    if a in ROWS_DIM1:
        return (full_shape[0], full_shape[1] // 8, full_shape[2])
    return (full_shape[0] // 8, full_shape[1])


def _shard_shape(a, full_shape):
    if a in SHARD_MAJOR:
        return (full_shape[1], full_shape[2])
    if a in ROWS_DIM1:
        return (full_shape[0], full_shape[1] // 4, full_shape[2])
    return (full_shape[0] // 4, full_shape[1])


def _rows_axis(a):
    return 1 if a in ROWS_DIM1 else 0


def _piece_block(a, full_shape):
    ps = _piece_shape(a, full_shape)
    if a in SHARD_MAJOR:
        return (None,) + ps, lambda k, c: (k, c, 0)
    if a in ROWS_DIM1:
        return ps, lambda k, c: (0, 2 * k + c, 0)
    return ps, lambda k, c: (2 * k + c, 0)


def _coords():
    return lax.axis_index("x"), lax.axis_index("y"), lax.axis_index("c")


def _peer_chips(x, y):
    return [(1 - x, y), (x, 1 - y), (1 - x, 1 - y)]


def _remote(src, dst, ssem, rsem, dev):
    return pltpu.make_async_remote_copy(src_ref=src, dst_ref=dst, send_sem=ssem, recv_sem=rsem,
                                        device_id=dev, device_id_type=MESH)


def _dma_sems(*counts):
    return [pltpu.SemaphoreType.DMA((n,)) for n in counts]


def _symmetric(ins, out_shapes, sems, copies, peers, aliases=None):
    def start(cins, couts, csems):
        for cp in copies(cins, couts, csems):
            cp.start()

    def finish(cins, couts, csems):
        for cp in copies(cins, couts, csems):
            cp.wait()

    return _Comm(ins, out_shapes, sems, start, finish, peers, aliases)


def _rows_part(a, ref, part):
    if part is None:
        return ref
    p, q, n = part
    ax = _rows_axis(a)
    r = ref.shape[ax] // n
    return ref.at[tuple(pl.ds(p * r, (q - p) * r) if d == ax else slice(None) for d in range(len(ref.shape)))]


def _merge(comms):
    ins, outs, sems, aliases, spans = [], [], [], {}, []
    for cm in comms:
        spans.append((len(ins), len(outs), len(sems)))
        for i, o in cm.aliases.items():
            aliases[len(ins) + i] = len(outs) + o
        ins += cm.ins
        outs += cm.out_shapes
        sems += cm.sems

    def each(fn_name):
        def run(cins, couts, csems):
            for cm, (i0, o0, s0) in zip(comms, spans):
                fn = getattr(cm, fn_name)
                if fn is not None:
                    fn(cins[i0:i0 + len(cm.ins)], couts[o0:o0 + len(cm.out_shapes)], csems[s0:s0 + len(cm.sems)])
        return run

    return _Comm(ins, outs, sems, each("start"), each("finish"), frozenset().union(*[cm.peers for cm in comms]),
                 aliases, mid=each("mid") if any(cm.mid is not None for cm in comms) else None)


def _gather_comm(arrs, locs, full_shapes, part=None, into=None):
    n = len(arrs)

    def own(cins, couts, csems):
        x, y, c = _coords()
        j = 2 * x + y
        return [_remote(_rows_part(a, _half(a, cins[q], h), part), _rows_part(a, _piece(a, couts[q], j, h), part),
                        csems[0].at[2 * q + h], csems[1].at[2 * q + h], (x, y, 1 - c))
                for q, a in enumerate(arrs) for h in range(2)]

    def sends(cins, couts, csems):
        x, y, c = _coords()
        j = 2 * x + y
        return [_remote(_rows_part(a, _half(a, cins[q], c), part), _rows_part(a, _piece(a, couts[q], j, c), part),
                        csems[2].at[3 * q + i], csems[3].at[3 * q + i], (px, py, c))
                for q, a in enumerate(arrs) for i, (px, py) in enumerate(_peer_chips(x, y))]

    def forwards(couts, csems, half_of):
        x, y, c = _coords()
        out = []
        for q, a in enumerate(arrs):
            for i, (px, py) in enumerate(_peer_chips(x, y)):
                landed = _rows_part(a, _piece(a, couts[q], 2 * px + py, half_of(c)), part)
                out.append(_remote(landed, landed, csems[4].at[3 * q + i], csems[5].at[3 * q + i], (x, y, 1 - c)))
        return out

    def start(cins, couts, csems):
        for cp in sends(cins, couts, csems) + own(cins, couts, csems):
            cp.start()

    def finish(cins, couts, csems):
        fw = forwards(couts, csems, lambda c: c)
        for cp, f in zip(sends(cins, couts, csems), fw):
            cp.wait_recv()
            f.start()
        for f in forwards(couts, csems, lambda c: 1 - c):
            f.wait_recv()
        for cp in sends(cins, couts, csems) + fw:
            cp.wait_send()
        for cp in own(cins, couts, csems):
            cp.wait()

    ins = [locs[a] for a in arrs] + ([into[a] for a in arrs] if into else [])
    return _Comm(ins, [SDS(full_shapes[a], BF16) for a in arrs],
                 _dma_sems(2 * n, 2 * n, 3 * n, 3 * n, 3 * n, 3 * n), start, finish, CHIPS + (SIBLING,),
                 aliases={n + q: q for q in range(n)} if into else None)


def _ring_gather_comm(arrs, locs, full_shapes):
    n = len(arrs)

    def own(cins, couts, csems):
        x, y, c = _coords()
        j = 2 * x + y
        return [_remote(_half(a, cins[q], h), _piece(a, couts[q], j, h), csems[0].at[2 * q + h],
                        csems[1].at[2 * q + h], (x, y, 1 - c)) for q, a in enumerate(arrs) for h in range(2)]

    def sends(cins, couts, csems):
        x, y, c = _coords()
        j = 2 * x + y
        return [_remote(_half(a, cins[q], c), _piece(a, couts[q], j, c), csems[2].at[2 * q + i],
                        csems[3].at[2 * q + i], (px, py, c))
                for q, a in enumerate(arrs) for i, (px, py) in enumerate(_peer_chips(x, y)[:2])]

    def relays(couts, csems):
        x, y, c = _coords()
        peers = _peer_chips(x, y)
        out = []
        for q, a in enumerate(arrs):
            for r, (src_p, dst_p) in enumerate(((0, 1), (1, 0))):
                sx, sy = peers[src_p]
                rows = _rows_part(a, _piece(a, couts[q], 2 * sx + sy, c), (r, r + 1, 2))
                out.append(_remote(rows, rows, csems[6].at[2 * q + r], csems[7].at[2 * q + r], (*peers[dst_p], c)))
        return out

    def forwards(couts, csems, half_of, which):
        x, y, c = _coords()
        out = []
        for q, a in enumerate(arrs):
            for i in which:
                px, py = _peer_chips(x, y)[i]
                landed = _piece(a, couts[q], 2 * px + py, half_of(c))
                out.append(_remote(landed, landed, csems[4].at[3 * q + i], csems[5].at[3 * q + i], (x, y, 1 - c)))
        return out

    def start(cins, couts, csems):
        for cp in sends(cins, couts, csems) + own(cins, couts, csems):
            cp.start()

    def mid(cins, couts, csems):
        for cp in sends(cins, couts, csems):
            cp.wait_recv()
        for cp in relays(couts, csems) + forwards(couts, csems, lambda c: c, (0, 1)):
            cp.start()

    def finish(cins, couts, csems):
        for cp in relays(couts, csems):
            cp.wait_recv()
        fw_diag = forwards(couts, csems, lambda c: c, (2,))
        for f in fw_diag:
            f.start()
        for f in forwards(couts, csems, lambda c: 1 - c, (0, 1, 2)):
            f.wait_recv()
        for cp in (sends(cins, couts, csems) + relays(couts, csems)
                   + forwards(couts, csems, lambda c: c, (0, 1)) + fw_diag):
            cp.wait_send()
        for cp in own(cins, couts, csems):
            cp.wait()

    return _Comm([locs[a] for a in arrs], [SDS(full_shapes[a], BF16) for a in arrs],
                 _dma_sems(2 * n, 2 * n, 2 * n, 2 * n, 3 * n, 3 * n, 2 * n, 2 * n), start, finish,
                 CHIPS + (SIBLING,), mid=mid)


def _halves_comm(arrs, gbs):
    n = len(arrs)

    def copies(cins, couts, csems):
        x, y, c = _coords()
        return [_remote(_piece(a, cins[q], k, 1 - c), couts[q].at[k], csems[0].at[4 * q + k], csems[1].at[4 * q + k],
                        (x, y, 1 - c)) for q, a in enumerate(arrs) for k in range(4)]

    return _symmetric([gbs[a] for a in arrs], [SDS((4,) + _piece_shape(a, gbs[a].shape), BF16) for a in arrs],
                      _dma_sems(4 * n, 4 * n), copies, [SIBLING])


def _chips_comm(arrs, ps, part=None, into=None):
    n = len(arrs)

    def copies(cins, couts, csems):
        x, y, c = _coords()
        return [_remote(_rows_part(a, cins[q].at[2 * px + py], part), _rows_part(a, couts[q].at[i], part),
                        csems[0].at[3 * q + i], csems[1].at[3 * q + i], (px, py, c))
                for q, a in enumerate(arrs) for i, (px, py) in enumerate(_peer_chips(x, y))]

    ins = [ps[a] for a in arrs] + ([into[a] for a in arrs] if into else [])
    return _symmetric(ins, [SDS((3,) + ps[a].shape[1:], BF16) for a in arrs], _dma_sems(3 * n, 3 * n), copies, CHIPS,
                      aliases={n + q: q for q in range(n)} if into else None)


def _result_comm(arrs, gs):
    n = len(arrs)

    def copies(cins, couts, csems):
        x, y, c = _coords()
        return [_remote(_half(a, cins[q], c), _half(a, couts[q], c), csems[0].at[q], csems[1].at[q], (x, y, 1 - c))
                for q, a in enumerate(arrs)]

    return _symmetric([gs[a] for a in arrs], [SDS(gs[a].shape, F32) for a in arrs], _dma_sems(n, n), copies,
                      [SIBLING], aliases={q: q for q in range(n)})


def _add_halves(arrs, gbs, lands, c_arr, name):
    n = len(arrs)

    def body(c_ref, *refs):
        del c_ref
        for q in range(n):
            refs[2 * n + q][...] = (refs[q][...].astype(F32) + refs[n + q][...].astype(F32)).astype(BF16)

    g_specs, l_specs, o_specs, blocks = [], [], [], 0
    for a in arrs:
        bs, imap = _piece_block(a, gbs[a].shape)
        ps = _piece_shape(a, gbs[a].shape)
        g_specs.append(pl.BlockSpec(bs, lambda k, c_ref, imap=imap: imap(k, c_ref[0])))
        nd = len(ps)
        l_specs.append(pl.BlockSpec((None,) + ps, lambda k, c_ref, nd=nd: (k,) + (0,) * nd))
        o_specs.append(pl.BlockSpec((None,) + ps, lambda k, c_ref, nd=nd: (k,) + (0,) * nd))
        blocks += 3 * _nbytes(ps, BF16)
    return list(pl.pallas_call(
        body, name=name,
        grid_spec=pltpu.PrefetchScalarGridSpec(
            num_scalar_prefetch=1, grid=(4,), in_specs=g_specs + l_specs, out_specs=o_specs),
        out_shape=[SDS((4,) + _piece_shape(a, gbs[a].shape), BF16) for a in arrs],
        compiler_params=_params(("parallel",), blocks, blocks),
    )(c_arr, *[gbs[a] for a in arrs], *lands))


def _sum_chips(a, p, land, shard_shape, jc_arr, name):
    ps = land.shape[1:]
    ax = _rows_axis(a)
    rows = ps[ax]
    nsub = 2 if rows % (2 * SUBLANES_BF16) == 0 else 1
    bs = tuple(r // nsub if q == ax else r for q, r in enumerate(ps))
    nd = len(ps)

    def at_rows(v):
        return tuple(v if q == ax else 0 for q in range(nd))

    def body(jc_ref, p_ref, l_ref, o_ref):
        del jc_ref
        acc = p_ref[...].astype(F32) + l_ref[0].astype(F32)
        acc = acc + l_ref[1].astype(F32)
        o_ref[...] = acc + l_ref[2].astype(F32)

    blocks = 4 * _nbytes(bs, BF16) + _nbytes(bs, F32)
    return pl.pallas_call(
        body, name=name,
        grid_spec=pltpu.PrefetchScalarGridSpec(
            num_scalar_prefetch=1, grid=(nsub,),
            in_specs=[pl.BlockSpec((None,) + bs, lambda s, jc: (jc[0],) + at_rows(s)),
                      pl.BlockSpec((3,) + bs, lambda s, jc: (0,) + at_rows(s))],
            out_specs=pl.BlockSpec(bs, lambda s, jc: at_rows(jc[1] * nsub + s))),
        out_shape=SDS(shard_shape, F32),
        compiler_params=_params(("parallel",), blocks, 2 * _nbytes(bs, F32)),
    )(jc_arr, p, land)


def _small_comm(v):
    rows = v.shape[0]

    def copies(cins, couts, csems):
        x, y, c = _coords()
        me = 4 * x + 2 * y + c
        out = [pltpu.make_async_copy(cins[0], couts[0].at[me], csems[0].at[0])]
        for dlt in range(1, 8):
            px = 1 - x if (dlt >> 2) & 1 else x
            py = 1 - y if (dlt >> 1) & 1 else y
            pc = 1 - c if dlt & 1 else c
            out.append(_remote(cins[0], couts[0].at[me], csems[1].at[dlt - 1], csems[2].at[dlt - 1], (px, py, pc)))
        return out

    return _symmetric([v], [SDS((8, rows, LANES), F32)], _dma_sems(1, 7, 7), copies, EVERYONE)


def _sum8(slots, name):
    def body(s_ref, o_ref):
        acc = s_ref[0]
        for i in range(1, 8):
            acc = acc + s_ref[i]
        o_ref[...] = acc

    return pl.pallas_call(
        body, name=name,
        in_specs=[pl.BlockSpec(memory_space=pltpu.VMEM)], out_specs=pl.BlockSpec(memory_space=pltpu.VMEM),
        out_shape=SDS(slots.shape[1:], F32),
    )(slots)


def _adamw(w, g, m, v, name, g_plane=None):
    rows, cols = w.shape
    tr = _tile(rows, max(SUBLANES_F32, (256 * 1024 // cols) // SUBLANES_F32 * SUBLANES_F32), SUBLANES_F32)

    def body(w_ref, g_ref, m_ref, v_ref, go_ref, d_ref, mo_ref, vo_ref):
        gr = g_ref[...]
        mn = ADAM_B1 * m_ref[...] + (1.0 - ADAM_B1) * gr
        vn = ADAM_B2 * v_ref[...] + (1.0 - ADAM_B2) * (gr * gr)
        m_hat = mn / (1.0 - ADAM_B1 ** ADAM_STEP)
        v_hat = vn / (1.0 - ADAM_B2 ** ADAM_STEP)
        d_ref[...] = -ADAM_LR * (m_hat / (jnp.sqrt(v_hat) + ADAM_EPS) + ADAM_WD * w_ref[...])
        go_ref[...] = gr
        mo_ref[...] = mn
        vo_ref[...] = vn

    spec = pl.BlockSpec((tr, cols), lambda i: (i, 0))
    g_spec = spec if g_plane is None else pl.BlockSpec((None, tr, cols), lambda i: (g_plane, i, 0))
    return pl.pallas_call(
        body, name=name, grid=(rows // tr,),
        in_specs=[spec, g_spec, spec, spec], out_specs=[spec, spec, spec, spec],
        out_shape=[SDS((rows, cols), F32)] * 4,
        compiler_params=_params(("parallel",), 8 * _nbytes((tr, cols), F32), 4 * _nbytes((tr, cols), F32)),
    )(w, g, m, v)


def _pack(parts):
    rows = []
    for p in parts:
        r = p.reshape(-1, LANES)
        pad = (-r.shape[0]) % SUBLANES_F32
        if pad:
            r = jnp.pad(r, ((0, pad), (0, 0)))
        rows.append(r)
    return jnp.concatenate(rows, axis=0)


def _unpack(packed, shapes):
    out, at = [], 0
    for s in shapes:
        n = 1
        for q in s:
            n *= q
        r = n // LANES
        out.append(packed[at:at + r].reshape(s))
        at += r + (-r) % SUBLANES_F32
    return out


def kernel(x, norm_mix, w_in, pool_w, pool_scale, w_pool_proj, conv_w, w_conv_out, w_o, norm_ffn, w_up, ffn_conv_w, ffn_conv_b, w_down, norm_final, loss_target, m_norm_mix, m_w_in, m_pool_w, m_pool_scale, m_w_pool_proj, m_conv_w, m_w_conv_out, m_w_o, m_norm_ffn, m_w_up, m_ffn_conv_w, m_ffn_conv_b, m_w_down, m_norm_final, v_norm_mix, v_w_in, v_pool_w, v_pool_scale, v_w_pool_proj, v_conv_w, v_w_conv_out, v_w_o, v_norm_ffn, v_w_up, v_ffn_conv_w, v_ffn_conv_b, v_w_down, v_norm_final):
    nseq, seq, d = x.shape
    t = nseq * seq
    f = w_down.shape[1] * 4
    c = d // N_GROUPS
    xy = lax.axis_index("x") * 2 + lax.axis_index("y")
    c_arr = lax.axis_index("c").astype(jnp.int32).reshape(1)
    jc_arr = jnp.stack([xy, lax.axis_index("c")]).astype(jnp.int32)
    nsh = 4
    zero = jnp.zeros((), jnp.int32)

    locs = [w_in[0].astype(BF16),
            jnp.stack([w_pool_proj[0], w_conv_out[0], w_o[0]]).astype(BF16),
            w_up[0].astype(BF16), w_down[0].astype(BF16), pool_w[0].astype(BF16)]
    full_shapes = [(nsh, d, N_SPLITS * d // nsh), (3, d, d), (nsh, d, 2 * f // nsh), (f, d), (N_GROUPS, c, c)]

    cw_pad = lax.dynamic_update_slice(jnp.zeros((3, d), F32), conv_w[0], (zero, xy * (d // 4)))
    fw_pad = lax.dynamic_update_slice(jnp.zeros((3, 2 * f), F32), ffn_conv_w[0], (zero, xy * (f // 2)))
    small_w = _pack([cw_pad, fw_pad]) * 0.5

    x2d = x.reshape(t, d)
    tgt = loss_target.reshape(t, d)
    ax, ay = lax.axis_index("x"), lax.axis_index("y")
    order = jnp.stack([xy, 2 * (1 - ax) + ay, 2 * ax + 1 - ay, 2 * (1 - ax) + 1 - ay]).astype(jnp.int32)
    (z, h1, w_in_f), (pool_w_f, w3_f, slots_w) = _fwd_in(
        x2d, norm_mix, locs[0], order,
        _merge([_gather_comm([4], locs, full_shapes), _gather_comm([1], locs, full_shapes, part=(0, 1, 2)),
                _small_comm(small_w)]))
    conv_w_f, ffn_cw_f = _unpack(_sum8(slots_w, "sum8_weights"), [(3, d), (3, 2 * f)])
    ffn_cw_p = ffn_cw_f.reshape(3, 2, f).transpose(1, 0, 2)
    ffn_cb_p = ffn_conv_b.reshape(2, 1, f)
    (lhs3,), (w3_f,) = _mixer_mid_fwd(z, pool_w_f, pool_scale, conv_w_f, nseq,
                                      _gather_comm([1], locs, full_shapes, part=(1, 2, 2), into={1: w3_f}))
    (lhs3, ypc, x1, h2), (w_up_f,) = _mixer_out(lhs3, z, x2d, w3_f, norm_ffn,
                                                _ring_gather_comm([2], locs, full_shapes))
    (u0,), (w_down_f,) = _ffn_up(h2, w_up_f, f, _gather_comm([3], locs, full_shapes))
    act, ua = _ffn_mid_fwd(u0, ffn_cw_p, ffn_cb_p, nseq)
    dx2, dx2b, loss11, g_norm_final = _ffn_down_loss(act, w_down_f, x1, tgt, norm_final.reshape(1, d))

    gbs, lands, ps, lands2, rs = {}, {}, {}, {}, {}
    tn_up = _tile(2 * f // nsh, 1408, LANES)
    npp = f // tn_up

    def add(arrs, name):
        for a, p in zip(arrs, _add_halves(arrs, gbs, [lands[a] for a in arrs], c_arr, name)):
            ps[a] = p

    def summed(a):
        rs[a] = _sum_chips(a, ps[a], lands2[a], _shard_shape(a, full_shapes[a]), jc_arr, "sum_chips_%d" % a)

    (gbs[3],), _ = _wgrad(act, dx2b, "wgrad_down", tr=tn_up, tn=d)
    (da,), (lands[3],) = _ffn_bwd_da(dx2b, w_down_f, _halves_comm([3], gbs))
    add([3], "add_halves_down")
    (du0, g_ffn_cw_p, g_ffn_cb_p), (lands2[3],) = _ffn_mid_bwd(da, u0, ua, ffn_cw_p, nseq, _chips_comm([3], ps))
    summed(3)
    (gbs[2],), (rs[3],) = _wgrad(h2, du0, "wgrad_up", tr=d, tn=tn_up, b_plane_of=lambda n: (n // npp, n % npp),
                                 out_shards=nsh, comm=_result_comm([3], rs))
    (dx1, rhs3, g_norm_ffn), (lands[2],) = _ffn_bwd_dx1(du0, w_up_f, x1, dx2, norm_ffn, 3, _halves_comm([2], gbs))
    add([2], "add_halves_up")
    (rhs3, dz, dpq), (lands2[2],) = _mixer_bwd(rhs3, z, ypc, w3_f, _chips_comm([2], ps, part=(0, 1, 2)))
    (gbs[1],), (lands2[2],) = _wgrad3(lhs3, rhs3, _chips_comm([2], ps, part=(1, 2, 2), into=lands2))
    summed(2)
    (dz, g_conv_w), (lands[1], rs[2]) = _conv_bwd(dz, dpq, z, conv_w_f, nseq,
                                                  _merge([_halves_comm([1], gbs), _result_comm([2], rs)]))
    add([1], "add_halves_sq3")
    (dz, g_pool_w, g_pool_scale), _ = _pool_bwd_call(dz, dpq, z, pool_w_f, pool_scale, nseq)
    gbs[4] = g_pool_w.astype(BF16)
    (gbs[0],), (lands2[1],) = _wgrad_in(h1, dz, nsh, _chips_comm([1], ps))
    summed(1)
    lands[0], lands[4] = _run_comm(_halves_comm([0, 4], gbs), "exchange_halves_in")
    add([0, 4], "add_halves_in")
    g_ffn_cw = g_ffn_cw_p.transpose(1, 0, 2).reshape(3, 2 * f)
    small_a = _pack([g_pool_scale, g_norm_ffn, g_ffn_cb_p.reshape(1, 2 * f), g_norm_final.reshape(d), g_conv_w,
                     g_ffn_cw, jnp.pad(loss11, ((0, SUBLANES_F32 - 1), (0, LANES - 1)))])
    (grad_x, g_norm_mix), (lands2[0], lands2[4], rs[1], slots_a) = _mixer_bwd_dx(
        dz, w_in_f, x2d, dx1, norm_mix,
        _merge([_chips_comm([0, 4], ps), _result_comm([1], rs), _small_comm(small_a)]))
    summed(0)
    summed(4)
    rs[0], rs[4], slots_b = _run_comm(_merge([_result_comm([0, 4], rs), _small_comm(_pack([g_norm_mix]))]),
                                      "exchange_result_in")
    shapes_a = [(1, d), (1, d), (1, 2 * f), (d,), (3, d), (3, 2 * f), (SUBLANES_F32, LANES)]
    gs_pool_scale, gs_norm_ffn, gs_ffn_cb, gs_norm_final, gs_conv_w, gs_ffn_cw, loss_blk = _unpack(
        _sum8(slots_a, "sum8_grads"), shapes_a)
    (gs_norm_mix,) = _unpack(_sum8(slots_b, "sum8_norm_mix"), [(1, d)])
    gs_conv_w = lax.dynamic_slice(gs_conv_w, (zero, xy * (d // 4)), (3, d // 4))
    gs_ffn_cw = lax.dynamic_slice(gs_ffn_cw, (zero, xy * (f // 2)), (3, f // 2))

    def upd(w, g, m, v, name, g_plane=None):
        shape = w.shape
        rows = 1
        for q in shape[:-1]:
            rows *= q
        g2 = g if g_plane is not None else g.reshape(rows, shape[-1])
        outs = _adamw(w.reshape(rows, shape[-1]), g2, m.reshape(rows, shape[-1]), v.reshape(rows, shape[-1]),
                      name, g_plane)
        return [o.reshape(shape) for o in outs]

    res = {
        "w_in": upd(w_in, rs[0], m_w_in, v_w_in, "adamw_w_in"),
        "pool_w": upd(pool_w, rs[4], m_pool_w, v_pool_w, "adamw_pool_w"),
        "w_pool_proj": upd(w_pool_proj, rs[1], m_w_pool_proj, v_w_pool_proj, "adamw_w_pool_proj", 0),
        "w_conv_out": upd(w_conv_out, rs[1], m_w_conv_out, v_w_conv_out, "adamw_w_conv_out", 1),
        "w_o": upd(w_o, rs[1], m_w_o, v_w_o, "adamw_w_o", 2),
        "w_up": upd(w_up, rs[2], m_w_up, v_w_up, "adamw_w_up"),
        "w_down": upd(w_down, rs[3], m_w_down, v_w_down, "adamw_w_down"),
    }

    small_names = ["norm_mix", "pool_scale", "norm_ffn", "ffn_conv_b", "norm_final", "conv_w", "ffn_conv_w"]
    small_ws = [norm_mix, pool_scale, norm_ffn, ffn_conv_b, norm_final, conv_w, ffn_conv_w]
    small_ms = [m_norm_mix, m_pool_scale, m_norm_ffn, m_ffn_conv_b, m_norm_final, m_conv_w, m_ffn_conv_w]
    small_vs = [v_norm_mix, v_pool_scale, v_norm_ffn, v_ffn_conv_b, v_norm_final, v_conv_w, v_ffn_conv_w]
    small_gs = [gs_norm_mix, gs_pool_scale, gs_norm_ffn, gs_ffn_cb, gs_norm_final, gs_conv_w, gs_ffn_cw]
    _, sd, sm, sv = _adamw(_pack(small_ws), _pack(small_gs), _pack(small_ms), _pack(small_vs), "adamw_small")
    shapes = [w.shape for w in small_ws]
    sd, sm, sv = _unpack(sd, shapes), _unpack(sm, shapes), _unpack(sv, shapes)
    for i, nm in enumerate(small_names):
        res[nm] = [small_gs[i].reshape(shapes[i]), sd[i], sm[i], sv[i]]

    order = ["norm_mix", "w_in", "pool_w", "pool_scale", "w_pool_proj", "conv_w", "w_conv_out", "w_o", "norm_ffn",
             "w_up", "ffn_conv_w", "ffn_conv_b", "w_down", "norm_final"]
    return (loss_blk[0, 0], grad_x.reshape(x.shape), *[res[n][0] for n in order], *[res[n][1] for n in order],
            *[res[n][2] for n in order], *[res[n][3] for n in order])
```

```python
import math

import jax
import jax.numpy as jnp
from jax import lax
from jax.experimental import pallas as pl
from jax.experimental.pallas import tpu as pltpu

F32 = jnp.float32
BF16 = jnp.bfloat16
SDS = jax.ShapeDtypeStruct
MESH = pl.DeviceIdType.MESH

RMS_EPS = 1e-6
POOL_WINDOWS = (2, 4, 8, 16)
N_GROUPS = len(POOL_WINDOWS)
N_SPLITS = 6

ADAM_LR = 0.001
ADAM_B1 = 0.9
ADAM_B2 = 0.999
ADAM_EPS = 1e-08
ADAM_WD = 0.01
ADAM_STEP = 10

LANES = 128
SUBLANES_F32 = 8
SUBLANES_BF16 = 16
VMEM_BYTES = 64 * 1024 * 1024
VMEM_CAP = VMEM_BYTES - 8 * 1024 * 1024
VMEM_FLOOR = 16 * 1024 * 1024

ANY = pl.BlockSpec(memory_space=pl.ANY)


def _tile(dim, pref, align):
    if dim <= pref:
        return dim
    t = (pref // align) * align
    while t >= align:
        if dim % t == 0:
            return t
        t -= align
    return dim


def _nbytes(shape, dtype):
    n = 1
    for s in shape:
        n *= s
    return n * jnp.dtype(dtype).itemsize


def _params(sem, block_bytes, temp_bytes=0, collective_id=None):
    need = 2 * block_bytes + temp_bytes + 4 * 1024 * 1024
    return pltpu.CompilerParams(dimension_semantics=sem, collective_id=collective_id,
                                vmem_limit_bytes=int(min(max(need, VMEM_FLOOR), VMEM_CAP)))


SIBLING = (0, 0, 1)
CHIPS = ((1, 0, 0), (0, 1, 0), (1, 1, 0))
EVERYONE = tuple((a, b, c) for a in range(2) for b in range(2) for c in range(2) if a + b + c)
PEER_SETS = (frozenset([SIBLING]), frozenset(CHIPS), frozenset(CHIPS + (SIBLING,)), frozenset(EVERYONE))
MID_AT = 0.75


def _collective_id(peers):
    return PEER_SETS.index(frozenset(peers))


def _handshake(peers):
    x, y, c = lax.axis_index("x"), lax.axis_index("y"), lax.axis_index("c")
    bar = pltpu.get_barrier_semaphore()
    for fx, fy, fc in sorted(peers):
        dev = (1 - x if fx else x, 1 - y if fy else y, 1 - c if fc else c)
        pl.semaphore_signal(bar, inc=1, device_id=dev, device_id_type=MESH)
    pl.semaphore_wait(bar, len(peers))


class _Comm:
    def __init__(self, ins, out_shapes, sems, start, finish, peers, aliases=None, mid=None):
        self.ins = list(ins)
        self.out_shapes = list(out_shapes)
        self.sems = list(sems)
        self.start = start
        self.finish = finish
        self.mid = mid
        self.peers = frozenset(peers)
        self.aliases = dict(aliases or {})


def _pcall(body, *, name, grid, in_specs, out_specs, out_shape, sem, blocks, temps=0, scratch_shapes=(),
           input_output_aliases=None, comm=None):
    in_specs = list(in_specs)
    out_specs = list(out_specs)
    out_shape = list(out_shape)
    scratch_shapes = list(scratch_shapes)
    aliases = dict(input_output_aliases or {})
    n_in, n_out, n_scr = len(in_specs), len(out_shape), len(scratch_shapes)
    if comm is None:
        call = pl.pallas_call(
            body, name=name, grid=grid, in_specs=in_specs, out_specs=out_specs, out_shape=out_shape,
            scratch_shapes=scratch_shapes, input_output_aliases=aliases,
            compiler_params=_params(sem, blocks, temps))
        return lambda *args: (list(call(*args)), [])

    nci, nco = len(comm.ins), len(comm.out_shapes)
    n_steps = 1
    for g in grid:
        n_steps *= g

    def hosted(*refs):
        ins = refs[:n_in]
        cins = refs[n_in:n_in + nci]
        outs = refs[n_in + nci:n_in + nci + n_out]
        couts = refs[n_in + nci + n_out:n_in + nci + n_out + nco]
        scr = refs[n_in + nci + n_out + nco:n_in + nci + n_out + nco + n_scr]
        csems = refs[n_in + nci + n_out + nco + n_scr:]
        first = None
        last = None
        step = 0
        for q, g in enumerate(grid):
            pid = pl.program_id(q)
            first = (pid == 0) if first is None else first & (pid == 0)
            last = (pid == g - 1) if last is None else last & (pid == g - 1)
            step = step * g + pid

        @pl.when(first)
        def _():
            _handshake(comm.peers)
            comm.start(cins, couts, csems)

        if comm.mid is not None:
            @pl.when(step == int(MID_AT * n_steps))
            def _():
                comm.mid(cins, couts, csems)

        body(*ins, *outs, *scr)

        @pl.when(last)
        def _():
            comm.finish(cins, couts, csems)

    for i, o in comm.aliases.items():
        aliases[n_in + i] = n_out + o
    call = pl.pallas_call(
        hosted, name=name, grid=grid, in_specs=in_specs + [ANY] * nci, out_specs=out_specs + [ANY] * nco,
        out_shape=out_shape + comm.out_shapes, scratch_shapes=scratch_shapes + comm.sems,
        input_output_aliases=aliases,
        compiler_params=_params(("arbitrary",) * len(grid), blocks, temps, _collective_id(comm.peers)))

    def run(*args):
        res = call(*args, *comm.ins)
        return list(res[:n_out]), list(res[n_out:])

    return run


def _run_comm(comm, name):
    def body(*refs):
        nci, nco = len(comm.ins), len(comm.out_shapes)
        cins, couts, csems = refs[:nci], refs[nci:nci + nco], refs[nci + nco:]
        _handshake(comm.peers)
        comm.start(cins, couts, csems)
        if comm.mid is not None:
            comm.mid(cins, couts, csems)
        comm.finish(cins, couts, csems)

    return list(pl.pallas_call(
        body, name=name, in_specs=[ANY] * len(comm.ins), out_specs=[ANY] * len(comm.out_shapes),
        out_shape=comm.out_shapes, scratch_shapes=comm.sems, input_output_aliases=comm.aliases,
        compiler_params=pltpu.CompilerParams(collective_id=_collective_id(comm.peers)),
    )(*comm.ins))


def _dot(a, b):
    return jnp.dot(a, b, preferred_element_type=F32)


def _dot_tb(a, b):
    return lax.dot_general(a, b, (((1,), (1,)), ((), ())), preferred_element_type=F32)


def _dot_ta(a, b):
    return lax.dot_general(a, b, (((0,), (0,)), ((), ())), preferred_element_type=F32)


def _rms_fwd(x):
    inv = lax.rsqrt(jnp.mean(x * x, axis=-1, keepdims=True) + RMS_EPS)
    return x * inv, inv


def _rms_bwd(dy, xhat, inv, g):
    gd = dy * g
    return inv * (gd - xhat * jnp.mean(gd * xhat, axis=-1, keepdims=True))


def _sigmoid(x):
    return 1.0 / (1.0 + jnp.exp(-x))


def _shift_down(x, k, row):
    return jnp.where(row >= k, pltpu.roll(x, k, 0), 0.0)


def _shift_up(x, k, row):
    s = x.shape[0]
    return jnp.where(row < s - k, pltpu.roll(x, s - k, 0), 0.0)


def _pool_fwd(u, win, row):
    s = u
    k = 1
    while k < win:
        s = s + _shift_down(s, k, row)
        k *= 2
    cnt = jnp.minimum(row + 1, win).astype(F32)
    return s / cnt - u


def _pool_bwd(dp, win, row):
    cnt = jnp.minimum(row + 1, win).astype(F32)
    s = dp / cnt
    k = 1
    while k < win:
        s = s + _shift_up(s, k, row)
        k *= 2
    return s - dp


def _acc_over(k, nk, part, acc, o_ref):
    @pl.when(k == 0)
    def _():
        acc[...] = part

    @pl.when(k > 0)
    def _():
        acc[...] += part

    @pl.when(k == nk - 1)
    def _():
        o_ref[...] = acc[...].astype(o_ref.dtype)


def _fwd_in(x, g, w_loc, order, comm):
    t, d = x.shape
    ws = w_loc.shape[1]
    nsh = order.shape[0]
    assert nsh == 4, "the shard walk below is written for the 2 x 2 chips of the mesh"
    tm = _tile(t, 1024, SUBLANES_BF16)
    ni = t // tm
    nci, nco = len(comm.ins), len(comm.out_shapes)
    all_peers = comm.peers | frozenset(CHIPS + (SIBLING,))

    def body(order_ref, x_ref, g_ref, loc_ref, *rest):
        del order_ref
        cins = rest[:nci]
        z_ref, h_ref, full_ref = rest[nci:nci + 3]
        couts = rest[nci + 3:nci + 3 + nco]
        (hs, wbuf, wsem, own_s, own_r, snd_s, snd_r, fwd_s, fwd_r, rly_s, rly_r) = rest[nci + 3 + nco:nci + 14 + nco]
        csems = rest[nci + 14 + nco:]
        j = pl.program_id(0)
        i = pl.program_id(1)
        x_, y_, c_ = _coords()
        own = 2 * x_ + y_
        sib = (x_, y_, 1 - c_)
        peers = _peer_chips(x_, y_)

        def sends():
            return [_remote(_half(0, loc_ref, c_), _piece(0, full_ref, own, c_), snd_s.at[p], snd_r.at[p], (px, py, c_))
                    for p, (px, py) in enumerate(peers[:2])]

        def relays():
            out = []
            for q, (src_p, dst_p) in enumerate(((0, 1), (1, 0))):
                sx, sy = peers[src_p]
                part = _rows_part(0, _piece(0, full_ref, 2 * sx + sy, c_), (q, q + 1, 2))
                out.append(_remote(part, part, rly_s.at[q], rly_r.at[q], (*peers[dst_p], c_)))
            return out

        def owns():
            return [_remote(_half(0, loc_ref, h), _piece(0, full_ref, own, h), own_s.at[h], own_r.at[h], sib)
                    for h in range(2)]

        def forward(p, half):
            px, py = peers[p]
            landed = _piece(0, full_ref, 2 * px + py, half)
            return _remote(landed, landed, fwd_s.at[p], fwd_r.at[p], sib)

        def load(src, slot):
            return pltpu.make_async_copy(src, wbuf.at[slot], wsem.at[slot])

        @pl.when((j == 0) & (i == 0))
        def _():
            _handshake(all_peers)
            for cp in sends() + owns():
                cp.start()
            load(loc_ref, 0).start()

        @pl.when(j == 0)
        def _():
            xh, _ = _rms_fwd(x_ref[...])
            h = (xh * g_ref[...]).astype(BF16)
            hs[pl.ds(pl.multiple_of(i * tm, tm), tm), :] = h
            h_ref[...] = h

        slot = j % 2

        @pl.when(i == 0)
        def _():
            load(loc_ref, slot).wait()

        z_ref[...] = _dot(hs[pl.ds(pl.multiple_of(i * tm, tm), tm), :], wbuf[slot]).astype(BF16)

        def load_shard(p, into):
            px, py = peers[p]
            forward(p, 1 - c_).wait_recv()
            load(full_ref.at[2 * px + py], into).start()

        @pl.when((j == 0) & (i == ni - 1))
        def _():
            for cp in sends():
                cp.wait_recv()
            for cp in relays() + [forward(0, c_), forward(1, c_)]:
                cp.start()
            load_shard(0, 1)
            comm.start(cins, couts, csems)

        @pl.when((j == 1) & (i == 0))
        def _():
            load_shard(1, 0)

        @pl.when((j == 2) & (i == max(ni - 2, 0)))
        def _():
            for cp in relays():
                cp.wait_recv()
            forward(2, c_).start()
            load_shard(2, 1)

        @pl.when((j == nsh - 1) & (i == ni - 1))
        def _():
            for cp in sends() + relays() + [forward(p, c_) for p in range(nsh - 1)]:
                cp.wait_send()
            for cp in owns():
                cp.wait()
            comm.finish(cins, couts, csems)

    last = ni - 1
    blocks = _nbytes((tm, d), F32) + _nbytes((tm, ws), BF16) + _nbytes((tm, d), BF16)
    scratch = _nbytes((t, d), BF16) + 2 * _nbytes((d, ws), BF16)
    res = pl.pallas_call(
        body, name="fwd_in",
        grid_spec=pltpu.PrefetchScalarGridSpec(
            num_scalar_prefetch=1, grid=(nsh, ni),
            in_specs=[pl.BlockSpec((tm, d), lambda j, i, o: (jnp.where(j == 0, i, last), 0)),
                      pl.BlockSpec((1, d), lambda j, i, o: (0, 0)), ANY] + [ANY] * nci,
            out_specs=[pl.BlockSpec((tm, ws), lambda j, i, o: (i, o[j])),
                       pl.BlockSpec((tm, d), lambda j, i, o: (jnp.where(j == 0, i, last), 0)), ANY] + [ANY] * nco,
            scratch_shapes=[pltpu.VMEM((t, d), BF16), pltpu.VMEM((2, d, ws), BF16)]
            + _dma_sems(2, 2, 2, 2, 2, nsh - 1, nsh - 1, 2, 2) + comm.sems),
        out_shape=[SDS((t, nsh * ws), BF16), SDS((t, d), BF16), SDS((nsh, d, ws), BF16)] + comm.out_shapes,
        input_output_aliases={4 + i: 3 + o for i, o in comm.aliases.items()},
        compiler_params=_params(("arbitrary", "arbitrary"), blocks, scratch + 3 * _nbytes((tm, d), F32),
                                _collective_id(all_peers)),
    )(order, x, g, w_loc, *comm.ins)
    return list(res[:3]), list(res[3:])


def _mixer_mid_fwd(z, pool_w, pool_scale, conv_w, nseq, comm=None):
    t = z.shape[0]
    d = pool_scale.shape[1]
    s = t // nseq
    c = d // N_GROUPS

    def body(zp, zb, zc, zv, pw, ps, cw, o):
        j = pl.program_id(1)
        row = lax.broadcasted_iota(jnp.int32, (s, c), 0)
        for gi, win in enumerate(POOL_WINDOWS):
            @pl.when(j == gi)
            def _(win=win):
                pooled = _pool_fwd(zp[...].astype(F32), win, row)
                o[0] = (_dot(pooled.astype(BF16), pw[...]) * ps[...]).astype(BF16)

        cv = zc[...].astype(F32) * zv[...].astype(F32)
        cc = (cw[pl.ds(2, 1), :] * cv + cw[pl.ds(1, 1), :] * _shift_down(cv, 1, row)
              + cw[pl.ds(0, 1), :] * _shift_down(cv, 2, row))
        o[1] = (zb[...].astype(F32) * cc).astype(BF16)

    blocks = 4 * _nbytes((s, c), BF16) + _nbytes((c, c), BF16) + _nbytes((2, s, c), BF16)
    return _pcall(
        body, name="mixer_mid_fwd", grid=(nseq, N_GROUPS),
        in_specs=[pl.BlockSpec((s, c), lambda b, j: (b, j)),
                  pl.BlockSpec((s, c), lambda b, j: (b, N_GROUPS + j)),
                  pl.BlockSpec((s, c), lambda b, j: (b, 2 * N_GROUPS + j)),
                  pl.BlockSpec((s, c), lambda b, j: (b, 3 * N_GROUPS + j)),
                  pl.BlockSpec((None, c, c), lambda b, j: (j, 0, 0)),
                  pl.BlockSpec((1, c), lambda b, j: (0, j)),
                  pl.BlockSpec((3, c), lambda b, j: (0, j))],
        out_specs=[pl.BlockSpec((2, s, c), lambda b, j: (0, b, j))],
        out_shape=[SDS((3, t, d), BF16)],
        sem=("parallel", "parallel"), blocks=blocks, temps=8 * _nbytes((s, c), F32), comm=comm,
    )(z, z, z, z, pool_w, pool_scale, conv_w)


def _mixer_out(lhs3, z, x, w3, g_ffn, comm=None):
    t, d = x.shape
    tm = _tile(t, 256, SUBLANES_BF16)

    def body(pq, zgp, zgc, x_ref, w_ref, g_ref, mrg, ypc, x1o, h2o):
        yp = _dot(pq[0], w_ref[0])
        yc = _dot(pq[1], w_ref[1])
        m = _sigmoid(zgp[...].astype(F32)) * yp + _sigmoid(zgc[...].astype(F32)) * yc
        mb = m.astype(BF16)
        x1 = x_ref[...] + _dot(mb, w_ref[2])
        ypc[0] = yp.astype(BF16)
        ypc[1] = yc.astype(BF16)
        mrg[...] = mb
        x1o[...] = x1
        xh, _ = _rms_fwd(x1)
        h2o[...] = (xh * g_ref[...]).astype(BF16)

    blocks = (_nbytes((2, tm, d), BF16) * 2 + _nbytes((tm, d), BF16) * 4 + _nbytes((tm, d), F32) * 2
              + _nbytes((3, d, d), BF16))
    return _pcall(
        body, name="mixer_out", grid=(t // tm,),
        in_specs=[pl.BlockSpec((2, tm, d), lambda i: (0, i, 0)),
                  pl.BlockSpec((tm, d), lambda i: (i, 4)),
                  pl.BlockSpec((tm, d), lambda i: (i, 5)),
                  pl.BlockSpec((tm, d), lambda i: (i, 0)),
                  pl.BlockSpec((3, d, d), lambda i: (0, 0, 0)),
                  pl.BlockSpec((1, d), lambda i: (0, 0))],
        out_specs=[pl.BlockSpec((None, tm, d), lambda i: (2, i, 0)),
                   pl.BlockSpec((2, tm, d), lambda i: (0, i, 0)),
                   pl.BlockSpec((tm, d), lambda i: (i, 0)),
                   pl.BlockSpec((tm, d), lambda i: (i, 0))],
        out_shape=[SDS(lhs3.shape, BF16), SDS((2, t, d), BF16), SDS((t, d), F32), SDS((t, d), BF16)],
        input_output_aliases={0: 0},
        sem=("parallel",), blocks=blocks, temps=8 * _nbytes((tm, d), F32), comm=comm,
    )(lhs3, z, z, x, w3, g_ffn)


def _ffn_up(h2, w_up, f, comm=None):
    t, d = h2.shape
    _, _, ws = w_up.shape
    tm = _tile(t, 1024, SUBLANES_BF16)
    tn = _tile(ws, 1408, LANES)
    nps = ws // tn
    npp = f // tn

    def body(h_ref, w_ref, o_ref):
        o_ref[...] = _dot(h_ref[...], w_ref[...]).astype(BF16)

    blocks = _nbytes((tm, d), BF16) + _nbytes((d, tn), BF16) + _nbytes((tm, tn), BF16)
    return _pcall(
        body, name="ffn_up", grid=(t // tm, 2 * npp),
        in_specs=[pl.BlockSpec((tm, d), lambda i, j: (i, 0)),
                  pl.BlockSpec((None, d, tn), lambda i, j: (j // nps, 0, j % nps))],
        out_specs=[pl.BlockSpec((None, tm, tn), lambda i, j: (j // npp, i, j % npp))],
        out_shape=[SDS((2, t, f), BF16)],
        sem=("parallel", "parallel"), blocks=blocks, temps=_nbytes((tm, tn), F32), comm=comm,
    )(h2, w_up)


def _conv3_rows(u, u1, u2, w_ref, p):
    return w_ref[p, pl.ds(2, 1), :] * u + w_ref[p, pl.ds(1, 1), :] * u1 + w_ref[p, pl.ds(0, 1), :] * u2


WGRAD_TOKENS = 2048
WGRAD_TOKENS_WIDE = 4096
CHUNK = 64
HALO = SUBLANES_F32


def _up1_up2(u, nxt):
    rows = u.shape[0]
    ext = jnp.concatenate([u, nxt], axis=0)
    n = rows + HALO
    return pltpu.roll(ext, n - 1, 0)[:rows], pltpu.roll(ext, n - 2, 0)[:rows]


def _fold8(x):
    return jnp.sum(x.reshape(x.shape[0] // SUBLANES_F32, SUBLANES_F32, x.shape[1]), axis=0)


def _ffn_mid_fwd(u0, cw, cb, nseq):
    _, t, f = u0.shape
    s = t // nseq
    c = _tile(f, 256, LANES)

    def body(u_ref, w_ref, b_ref, a_ref, uo_ref):
        row = lax.broadcasted_iota(jnp.int32, (s, c), 0)
        act = []
        for p in range(2):
            u = u_ref[p].astype(F32)
            act.append(_conv3_rows(u, _shift_down(u, 1, row), _shift_down(u, 2, row), w_ref, p) + b_ref[p])
            uo_ref[p] = act[p].astype(BF16)
        ug, uv = act
        a_ref[...] = (ug * _sigmoid(ug) * uv).astype(BF16)

    blocks = 2 * _nbytes((2, s, c), BF16) + _nbytes((s, c), BF16)
    outs, _ = _pcall(
        body, name="ffn_mid_fwd", grid=(f // c, nseq),
        in_specs=[pl.BlockSpec((2, s, c), lambda j, b: (0, b, j)),
                  pl.BlockSpec((2, 3, c), lambda j, b: (0, 0, j)),
                  pl.BlockSpec((2, 1, c), lambda j, b: (0, 0, j))],
        out_specs=[pl.BlockSpec((s, c), lambda j, b: (b, j)),
                   pl.BlockSpec((2, s, c), lambda j, b: (0, b, j))],
        out_shape=[SDS((t, f), BF16), SDS((2, t, f), BF16)],
        sem=("parallel", "parallel"), blocks=blocks, temps=8 * _nbytes((s, c), F32),
    )(u0, cw, cb)
    return outs


def _ffn_down_loss(a, w_down, x1, tgt, g_fin):
    t, f = a.shape
    d = x1.shape[1]
    tm = _tile(t, 256, SUBLANES_BF16)
    nsteps = t // tm

    def body(a_ref, w_ref, x1_ref, t_ref, g_ref, dx_ref, dxb_ref, loss_ref, gg_ref, lacc):
        i = pl.program_id(0)

        @pl.when(i == 0)
        def _():
            lacc[...] = jnp.zeros_like(lacc)
            gg_ref[...] = jnp.zeros_like(gg_ref)

        x2 = x1_ref[...] + _dot(a_ref[...], w_ref[...])
        xh, inv = _rms_fwd(x2)
        g = g_ref[...]
        e = xh * g - t_ref[...]
        lacc[...] += jnp.sum(e * e, axis=0, keepdims=True)
        dy = e * (1.0 / d)
        gg_ref[...] += jnp.sum(dy * xh, axis=0, keepdims=True)
        dx2 = _rms_bwd(dy, xh, inv, g)
        dx_ref[...] = dx2
        dxb_ref[...] = dx2.astype(BF16)

        @pl.when(i == nsteps - 1)
        def _():
            loss_ref[...] = jnp.sum(lacc[...], axis=1, keepdims=True) * (0.5 / d)

    blocks = (_nbytes((tm, f), BF16) + _nbytes((f, d), BF16) + 3 * _nbytes((tm, d), F32) + _nbytes((tm, d), BF16))
    outs, _ = _pcall(
        body, name="ffn_down_loss", grid=(nsteps,),
        in_specs=[pl.BlockSpec((tm, f), lambda i: (i, 0)), pl.BlockSpec((f, d), lambda i: (0, 0)),
                  pl.BlockSpec((tm, d), lambda i: (i, 0)), pl.BlockSpec((tm, d), lambda i: (i, 0)),
                  pl.BlockSpec((1, d), lambda i: (0, 0))],
        out_specs=[pl.BlockSpec((tm, d), lambda i: (i, 0)), pl.BlockSpec((tm, d), lambda i: (i, 0)),
                   pl.BlockSpec((1, 1), lambda i: (0, 0)), pl.BlockSpec((1, d), lambda i: (0, 0))],
        out_shape=[SDS((t, d), F32), SDS((t, d), BF16), SDS((1, 1), F32), SDS((1, d), F32)],
        scratch_shapes=[pltpu.VMEM((1, d), F32)],
        sem=("arbitrary",), blocks=blocks, temps=8 * _nbytes((tm, d), F32),
    )(a, w_down, x1, tgt, g_fin)
    return outs


def _ffn_bwd_da(dxb, w_down, comm=None):
    t, d = dxb.shape
    f = w_down.shape[0]
    tm = _tile(t, 512, SUBLANES_BF16)

    def body(x_ref, w_ref, o_ref):
        o_ref[...] = _dot_tb(x_ref[...], w_ref[...]).astype(BF16)

    blocks = _nbytes((tm, d), BF16) + _nbytes((tm, f), BF16)
    return _pcall(
        body, name="ffn_bwd_da", grid=(t // tm,),
        in_specs=[pl.BlockSpec((tm, d), lambda i: (i, 0)),
                  pl.BlockSpec((f, d), lambda i: (0, 0), pipeline_mode=pl.Buffered(1))],
        out_specs=[pl.BlockSpec((tm, f), lambda i: (i, 0))],
        out_shape=[SDS((t, f), BF16)],
        sem=("parallel",), blocks=blocks, temps=_nbytes((f, d), BF16) + _nbytes((tm, f), F32), comm=comm,
    )(dxb, w_down)


def _ffn_mid_bwd(da, u0, ua, cw, nseq, comm=None):
    _, t, f = u0.shape
    s = t // nseq
    c = _tile(f, 128, LANES)
    r = _tile(s, CHUNK, SUBLANES_BF16)
    n = s // r

    def body(da_ref, u_ref, ua_ref, w_ref, du_ref, gw_ref, gb_ref):
        @pl.when(pl.program_id(1) == 0)
        def _():
            gw_ref[...] = jnp.zeros_like(gw_ref)
            gb_ref[...] = jnp.zeros_like(gb_ref)

        def step(i, carry):
            nxt, sums = carry
            rows = pl.ds(pl.multiple_of((n - 1 - i) * r, r), r)
            ug = ua_ref[0, rows, :].astype(F32)
            uv = ua_ref[1, rows, :].astype(F32)
            sg = _sigmoid(ug)
            dacc = da_ref[rows, :].astype(F32)
            dus = (dacc * uv * sg * (1.0 + ug * (1.0 - sg)), dacc * (ug * sg))
            first, new_sums = [], []
            for p in range(2):
                du = dus[p]
                d1, d2 = _up1_up2(du, nxt[p])
                du_ref[p, rows, :] = _conv3_rows(du, d1, d2, w_ref, p).astype(BF16)
                u = u_ref[p, rows, :].astype(F32)
                sb, s0, s1, s2 = sums[p]
                new_sums.append((sb + _fold8(du), s0 + _fold8(d2 * u), s1 + _fold8(d1 * u), s2 + _fold8(du * u)))
                first.append(du[:HALO])
            return tuple(first), tuple(new_sums)

        zero = jnp.zeros((HALO, c), F32)
        _, sums = lax.fori_loop(0, n, step, ((zero, zero), ((zero,) * 4,) * 2))
        for p in range(2):
            sb, s0, s1, s2 = sums[p]
            gb_ref[p] += jnp.sum(sb, axis=0, keepdims=True)
            gw_ref[p, pl.ds(0, 1), :] += jnp.sum(s0, axis=0, keepdims=True)
            gw_ref[p, pl.ds(1, 1), :] += jnp.sum(s1, axis=0, keepdims=True)
            gw_ref[p, pl.ds(2, 1), :] += jnp.sum(s2, axis=0, keepdims=True)

    blocks = _nbytes((s, c), BF16) + 3 * _nbytes((2, s, c), BF16)
    return _pcall(
        body, name="ffn_mid_bwd", grid=(f // c, nseq),
        in_specs=[pl.BlockSpec((s, c), lambda j, b: (b, j)),
                  pl.BlockSpec((2, s, c), lambda j, b: (0, b, j)),
                  pl.BlockSpec((2, s, c), lambda j, b: (0, b, j)),
                  pl.BlockSpec((2, 3, c), lambda j, b: (0, 0, j))],
        out_specs=[pl.BlockSpec((2, s, c), lambda j, b: (0, b, j)),
                   pl.BlockSpec((2, 3, c), lambda j, b: (0, 0, j)),
                   pl.BlockSpec((2, 1, c), lambda j, b: (0, 0, j))],
        out_shape=[SDS((2, t, f), BF16), SDS((2, 3, f), F32), SDS((2, 1, f), F32)],
        sem=("parallel", "arbitrary"), blocks=blocks, temps=4 * 1024 * 1024, comm=comm,
    )(da, u0, ua, cw)


def _wgrad(a, b, name, *, tr, tn, b_plane_of=None, out_shards=None, comm=None):
    t, m = a.shape
    n_total = b.shape[-1] * (b.shape[0] if b.ndim == 3 else 1)
    tk = _tile(t, WGRAD_TOKENS_WIDE if n_total > tn and m == tr else WGRAD_TOKENS, SUBLANES_BF16)
    nk = t // tk
    once = pl.Buffered(1) if nk == 1 else None

    def body(a_ref, b_ref, o_ref, *acc):
        part = _dot_ta(a_ref[...], b_ref[...])
        if nk == 1:
            o_ref[...] = part.astype(BF16)
        else:
            _acc_over(pl.program_id(2), nk, part, acc[0], o_ref)

    if b.ndim == 3:
        b_spec = pl.BlockSpec((None, tk, tn), lambda r, n, k: (b_plane_of(n)[0], k, b_plane_of(n)[1]))
    else:
        b_spec = pl.BlockSpec((tk, tn), lambda r, n, k: (k, n), pipeline_mode=once if n_total == tn else None)
    if out_shards is None:
        o_spec = pl.BlockSpec((tr, tn), lambda r, n, k: (r, n))
        o_shape = SDS((m, n_total), BF16)
    else:
        nps = n_total // out_shards // tn
        o_spec = pl.BlockSpec((None, tr, tn), lambda r, n, k: (n // nps, r, n % nps))
        o_shape = SDS((out_shards, m, n_total // out_shards), BF16)
    blocks = _nbytes((tk, tr), BF16) + _nbytes((tk, tn), BF16) + _nbytes((tr, tn), BF16)
    return _pcall(
        body, name=name, grid=(m // tr, n_total // tn, nk),
        in_specs=[pl.BlockSpec((tk, tr), lambda r, n, k: (k, r), pipeline_mode=once if m == tr else None), b_spec],
        out_specs=[o_spec], out_shape=[o_shape],
        scratch_shapes=[] if nk == 1 else [pltpu.VMEM((tr, tn), F32)],
        sem=("parallel", "parallel", "arbitrary"), blocks=blocks, temps=2 * _nbytes((tr, tn), F32), comm=comm,
    )(a, b)


def _wgrad3(lhs3, rhs3, comm=None):
    nw, t, d = lhs3.shape
    tk = _tile(t, WGRAD_TOKENS, SUBLANES_BF16)
    nk = t // tk

    def body(a_ref, b_ref, o_ref, *acc):
        part = _dot_ta(a_ref[...], b_ref[...])
        if nk == 1:
            o_ref[...] = part.astype(BF16)
        else:
            _acc_over(pl.program_id(1), nk, part, acc[0], o_ref)

    blocks = 2 * _nbytes((tk, d), BF16) + _nbytes((d, d), BF16)
    return _pcall(
        body, name="wgrad_sq3", grid=(nw, nk),
        in_specs=[pl.BlockSpec((None, tk, d), lambda w, k: (w, k, 0)),
                  pl.BlockSpec((None, tk, d), lambda w, k: (w, k, 0))],
        out_specs=[pl.BlockSpec((None, d, d), lambda w, k: (w, 0, 0))],
        out_shape=[SDS((nw, d, d), BF16)],
        scratch_shapes=[] if nk == 1 else [pltpu.VMEM((d, d), F32)],
        sem=("parallel", "arbitrary"), blocks=blocks, temps=2 * _nbytes((d, d), F32), comm=comm,
    )(lhs3, rhs3)


def _ffn_bwd_dx1(du0, w_up, x1, dx2, g_ffn, n_planes_out, comm=None):
    _, t, f = du0.shape
    d = x1.shape[1]
    nsh, _, ws = w_up.shape
    tm = _tile(t, 256, SUBLANES_BF16)
    spp = f // ws

    def body(du_ref, w_ref, x1_ref, dx2_ref, g_ref, dx1_ref, dxb_ref, gg_ref):
        @pl.when(pl.program_id(0) == 0)
        def _():
            gg_ref[...] = jnp.zeros_like(gg_ref)

        dh = None
        for k in range(nsh):
            part = _dot_tb(du_ref[k // spp, :, (k % spp) * ws:(k % spp + 1) * ws], w_ref[k])
            dh = part if dh is None else dh + part
        xh, inv = _rms_fwd(x1_ref[...])
        gg_ref[...] += jnp.sum(dh * xh, axis=0, keepdims=True)
        dx1 = dx2_ref[...] + _rms_bwd(dh, xh, inv, g_ref[...])
        dx1_ref[...] = dx1
        dxb_ref[...] = dx1.astype(BF16)

    blocks = _nbytes((2, tm, f), BF16) + 3 * _nbytes((tm, d), F32) + _nbytes((tm, d), BF16)
    return _pcall(
        body, name="ffn_bwd_dx1", grid=(t // tm,),
        in_specs=[pl.BlockSpec((2, tm, f), lambda i: (0, i, 0)),
                  pl.BlockSpec((nsh, d, ws), lambda i: (0, 0, 0), pipeline_mode=pl.Buffered(1)),
                  pl.BlockSpec((tm, d), lambda i: (i, 0)),
                  pl.BlockSpec((tm, d), lambda i: (i, 0)),
                  pl.BlockSpec((1, d), lambda i: (0, 0))],
        out_specs=[pl.BlockSpec((tm, d), lambda i: (i, 0)),
                   pl.BlockSpec((None, tm, d), lambda i: (n_planes_out - 1, i, 0)),
                   pl.BlockSpec((1, d), lambda i: (0, 0))],
        out_shape=[SDS((t, d), F32), SDS((n_planes_out, t, d), BF16), SDS((1, d), F32)],
        sem=("arbitrary",), blocks=blocks, temps=_nbytes(w_up.shape, BF16) + 8 * _nbytes((tm, d), F32), comm=comm,
    )(du0, w_up, x1, dx2, g_ffn)


def _mixer_bwd(rhs3, z, ypc, w3, comm=None):
    _, t, d = rhs3.shape
    tm = _tile(t, 256, SUBLANES_BF16)

    def body(dx_ref, zgp, zgc, ypc_ref, w_ref, dyo, dzo, dpq):
        dm = _dot_tb(dx_ref[...], w_ref[2])
        sp = _sigmoid(zgp[...].astype(F32))
        sc = _sigmoid(zgc[...].astype(F32))
        dyp = (dm * sp).astype(BF16)
        dyc = (dm * sc).astype(BF16)
        dzo[0] = (dm * ypc_ref[0].astype(F32) * sp * (1.0 - sp)).astype(BF16)
        dzo[1] = (dm * ypc_ref[1].astype(F32) * sc * (1.0 - sc)).astype(BF16)
        dyo[0] = dyp
        dyo[1] = dyc
        dpq[0] = _dot_tb(dyp, w_ref[0]).astype(BF16)
        dpq[1] = _dot_tb(dyc, w_ref[1]).astype(BF16)

    blocks = _nbytes((tm, d), BF16) * 3 + _nbytes((2, tm, d), BF16) * 4 + _nbytes((3, d, d), BF16)
    return _pcall(
        body, name="mixer_bwd", grid=(t // tm,),
        in_specs=[pl.BlockSpec((None, tm, d), lambda i: (2, i, 0)),
                  pl.BlockSpec((tm, d), lambda i: (i, 4)),
                  pl.BlockSpec((tm, d), lambda i: (i, 5)),
                  pl.BlockSpec((2, tm, d), lambda i: (0, i, 0)),
                  pl.BlockSpec((3, d, d), lambda i: (0, 0, 0))],
        out_specs=[pl.BlockSpec((2, tm, d), lambda i: (0, i, 0)),
                   pl.BlockSpec((2, tm, d), lambda i: (2, i, 0)),
                   pl.BlockSpec((2, tm, d), lambda i: (0, i, 0))],
        out_shape=[SDS(rhs3.shape, BF16), SDS((N_SPLITS, t, d), BF16), SDS((2, t, d), BF16)],
        input_output_aliases={0: 0},
        sem=("parallel",), blocks=blocks, temps=8 * _nbytes((tm, d), F32), comm=comm,
    )(rhs3, z, z, ypc, w3)


def _conv_bwd(dz, dpq, z, conv_w, nseq, comm=None):
    _, t, d = dz.shape
    s = t // nseq
    c = _tile(d, 128, LANES)
    nb = d // c

    def body(dz_in, dq_ref, zb, zc, zv, cw, dzo, gw_ref):
        del dz_in

        @pl.when(pl.program_id(1) == 0)
        def _():
            gw_ref[...] = jnp.zeros_like(gw_ref)

        row = lax.broadcasted_iota(jnp.int32, (s, c), 0)
        b = zb[...].astype(F32)
        cm = zc[...].astype(F32)
        v = zv[...].astype(F32)
        cv = cm * v
        cv1 = _shift_down(cv, 1, row)
        cv2 = _shift_down(cv, 2, row)
        w0, w1, w2 = cw[pl.ds(0, 1), :], cw[pl.ds(1, 1), :], cw[pl.ds(2, 1), :]
        cc = w2 * cv + w1 * cv1 + w0 * cv2
        dq = dq_ref[...].astype(F32)
        dzo[0] = (dq * cc).astype(BF16)
        dcc = dq * b
        gw_ref[pl.ds(0, 1), :] += jnp.sum(dcc * cv2, axis=0, keepdims=True)
        gw_ref[pl.ds(1, 1), :] += jnp.sum(dcc * cv1, axis=0, keepdims=True)
        gw_ref[pl.ds(2, 1), :] += jnp.sum(dcc * cv, axis=0, keepdims=True)
        dcv = w2 * dcc + w1 * _shift_up(dcc, 1, row) + w0 * _shift_up(dcc, 2, row)
        dzo[1] = (dcv * v).astype(BF16)
        dzo[2] = (dcv * cm).astype(BF16)

    blocks = 4 * _nbytes((s, c), BF16) + _nbytes((3, s, c), BF16)
    return _pcall(
        body, name="conv_bwd", grid=(nb, nseq),
        in_specs=[ANY,
                  pl.BlockSpec((None, s, c), lambda j, b: (1, b, j)),
                  pl.BlockSpec((s, c), lambda j, b: (b, nb + j)),
                  pl.BlockSpec((s, c), lambda j, b: (b, 2 * nb + j)),
                  pl.BlockSpec((s, c), lambda j, b: (b, 3 * nb + j)),
                  pl.BlockSpec((3, c), lambda j, b: (0, j))],
        out_specs=[pl.BlockSpec((3, s, c), lambda j, b: (0, b, j)),
                   pl.BlockSpec((3, c), lambda j, b: (0, j))],
        out_shape=[SDS(dz.shape, BF16), SDS((3, d), F32)],
        input_output_aliases={0: 0},
        sem=("parallel", "arbitrary"), blocks=blocks, temps=16 * _nbytes((s, c), F32), comm=comm,
    )(dz, dpq, z, z, z, conv_w)


def _pool_bwd_call(dz, dpq, z, pool_w, pool_scale, nseq, comm=None):
    _, t, d = dz.shape
    s = t // nseq
    c = d // N_GROUPS

    def body(dz_in, dp_ref, zp, pw, ps, dzo, gpw_ref, gps_ref):
        del dz_in
        j = pl.program_id(0)

        @pl.when(pl.program_id(1) == 0)
        def _():
            gpw_ref[...] = jnp.zeros_like(gpw_ref)
            gps_ref[...] = jnp.zeros_like(gps_ref)

        row = lax.broadcasted_iota(jnp.int32, (s, c), 0)
        for gi, win in enumerate(POOL_WINDOWS):
            @pl.when(j == gi)
            def _(win=win):
                pb = _pool_fwd(zp[...].astype(F32), win, row).astype(BF16)
                plin = _dot(pb, pw[...])
                dps = dp_ref[...].astype(F32)
                gps_ref[...] += jnp.sum(dps * plin, axis=0, keepdims=True)
                dplb = (dps * ps[...]).astype(BF16)
                gpw_ref[...] += _dot_ta(pb, dplb)
                dzo[...] = _pool_bwd(_dot_tb(dplb, pw[...]), win, row).astype(BF16)

    blocks = 3 * _nbytes((s, c), BF16) + _nbytes((c, c), BF16) + _nbytes((c, c), F32)
    return _pcall(
        body, name="pool_bwd", grid=(N_GROUPS, nseq),
        in_specs=[ANY,
                  pl.BlockSpec((None, s, c), lambda j, b: (0, b, j)),
                  pl.BlockSpec((s, c), lambda j, b: (b, j)),
                  pl.BlockSpec((None, c, c), lambda j, b: (j, 0, 0)),
                  pl.BlockSpec((1, c), lambda j, b: (0, j))],
        out_specs=[pl.BlockSpec((None, s, c), lambda j, b: (3, b, j)),
                   pl.BlockSpec((None, c, c), lambda j, b: (j, 0, 0)),
                   pl.BlockSpec((1, c), lambda j, b: (0, j))],
        out_shape=[SDS(dz.shape, BF16), SDS((N_GROUPS, c, c), F32), SDS((1, d), F32)],
        input_output_aliases={0: 0},
        sem=("parallel", "arbitrary"), blocks=blocks, temps=10 * _nbytes((s, c), F32), comm=comm,
    )(dz, dpq, z, pool_w, pool_scale)


def _dz_plane(zb):
    return jnp.where(zb < 4, (zb + 3) % 4, zb)


def _wgrad_in(h1, dz, nsh, comm=None):
    t, d = h1.shape
    ws = N_SPLITS * d // nsh
    kb = _tile(math.gcd(d, ws), 512, LANES)
    npl = d // kb
    nps = ws // kb
    tk = _tile(t, WGRAD_TOKENS_WIDE, SUBLANES_BF16)
    nk = t // tk

    def body(a_ref, b_ref, o_ref, *acc):
        part = _dot_ta(a_ref[...], b_ref[...])
        if nk == 1:
            o_ref[...] = part.astype(BF16)
        else:
            _acc_over(pl.program_id(1), nk, part, acc[0], o_ref)

    blocks = _nbytes((tk, d), BF16) + _nbytes((tk, kb), BF16) + _nbytes((d, kb), BF16)
    return _pcall(
        body, name="wgrad_in", grid=(N_SPLITS * npl, nk),
        in_specs=[pl.BlockSpec((tk, d), lambda cb, k: (k, 0), pipeline_mode=pl.Buffered(1) if nk == 1 else None),
                  pl.BlockSpec((None, tk, kb), lambda cb, k: (_dz_plane(cb // npl), k, cb % npl))],
        out_specs=[pl.BlockSpec((None, d, kb), lambda cb, k: (cb // nps, 0, cb % nps))],
        out_shape=[SDS((nsh, d, ws), BF16)],
        scratch_shapes=[] if nk == 1 else [pltpu.VMEM((d, kb), F32)],
        sem=("parallel", "arbitrary"), blocks=blocks, temps=2 * _nbytes((d, kb), F32), comm=comm,
    )(h1, dz)


def _mixer_bwd_dx(dz, w_in, x, dx1, g_mix, comm=None):
    npln, t, d = dz.shape
    nsh, _, ws = w_in.shape
    tm = _tile(t, 256, SUBLANES_BF16)
    kb = _tile(math.gcd(d, ws), 512, LANES)
    npl = d // kb
    nps = ws // kb

    def body(dz_ref, w_ref, x_ref, dx1_ref, g_ref, dx_ref, gg_ref):
        @pl.when(pl.program_id(0) == 0)
        def _():
            gg_ref[...] = jnp.zeros_like(gg_ref)

        dh = None
        for cb in range(npln * npl):
            zb = cb // npl
            plane = (zb + 3) % 4 if zb < 4 else zb
            part = _dot_tb(dz_ref[plane, :, (cb % npl) * kb:(cb % npl + 1) * kb],
                           w_ref[cb // nps, :, (cb % nps) * kb:(cb % nps + 1) * kb])
            dh = part if dh is None else dh + part
        xh, inv = _rms_fwd(x_ref[...])
        gg_ref[...] += jnp.sum(dh * xh, axis=0, keepdims=True)
        dx_ref[...] = dx1_ref[...] + _rms_bwd(dh, xh, inv, g_ref[...])

    blocks = _nbytes((npln, tm, d), BF16) + 3 * _nbytes((tm, d), F32)
    return _pcall(
        body, name="mixer_bwd_dx", grid=(t // tm,),
        in_specs=[pl.BlockSpec((npln, tm, d), lambda i: (0, i, 0)),
                  pl.BlockSpec((nsh, d, ws), lambda i: (0, 0, 0), pipeline_mode=pl.Buffered(1)),
                  pl.BlockSpec((tm, d), lambda i: (i, 0)),
                  pl.BlockSpec((tm, d), lambda i: (i, 0)),
                  pl.BlockSpec((1, d), lambda i: (0, 0))],
        out_specs=[pl.BlockSpec((tm, d), lambda i: (i, 0)),
                   pl.BlockSpec((1, d), lambda i: (0, 0))],
        out_shape=[SDS((t, d), F32), SDS((1, d), F32)],
        sem=("arbitrary",), blocks=blocks, temps=_nbytes(w_in.shape, BF16) + 8 * _nbytes((tm, d), F32), comm=comm,
    )(dz, w_in, x, dx1, g_mix)


N_BIG = 5
SHARD_MAJOR = (0, 2)
ROWS_DIM1 = (1, 4)


def _ds(start, size, align):
    if isinstance(start, int):
        return pl.ds(start, size)
    return pl.ds(pl.multiple_of(start, align), size)


def _piece(a, ref, k, h):
    if a in SHARD_MAJOR:
        r = ref.shape[1] // 2
        return ref.at[k, _ds(h * r, r, SUBLANES_BF16), :]
    if a in ROWS_DIM1:
        r = ref.shape[1] // 8
        return ref.at[:, _ds((2 * k + h) * r, r, SUBLANES_BF16), :]
    r = ref.shape[0] // 8
    return ref.at[_ds((2 * k + h) * r, r, SUBLANES_BF16), :]


def _half(a, ref, h):
    if a in ROWS_DIM1:
        r = ref.shape[1] // 2
        return ref.at[:, _ds(h * r, r, SUBLANES_BF16), :]
    r = ref.shape[0] // 2
    return ref.at[_ds(h * r, r, SUBLANES_BF16), :]


def _piece_shape(a, full_shape):
    if a in SHARD_MAJOR:
        return (full_shape[1] // 2, full_shape[2])
    if a in ROWS_DIM1:
        return (full_shape[0], full_shape[1] // 8, full_shape[2])
    return (full_shape[0] // 8, full_shape[1])


def _shard_shape(a, full_shape):
    if a in SHARD_MAJOR:
        return (full_shape[1], full_shape[2])
    if a in ROWS_DIM1:
        return (full_shape[0], full_shape[1] // 4, full_shape[2])
    return (full_shape[0] // 4, full_shape[1])


def _rows_axis(a):
    return 1 if a in ROWS_DIM1 else 0


def _piece_block(a, full_shape):
    ps = _piece_shape(a, full_shape)
    if a in SHARD_MAJOR:
        return (None,) + ps, lambda k, c: (k, c, 0)
    if a in ROWS_DIM1:
        return ps, lambda k, c: (0, 2 * k + c, 0)
    return ps, lambda k, c: (2 * k + c, 0)


def _coords():
    return lax.axis_index("x"), lax.axis_index("y"), lax.axis_index("c")


def _peer_chips(x, y):
    return [(1 - x, y), (x, 1 - y), (1 - x, 1 - y)]


def _remote(src, dst, ssem, rsem, dev):
    return pltpu.make_async_remote_copy(src_ref=src, dst_ref=dst, send_sem=ssem, recv_sem=rsem,
                                        device_id=dev, device_id_type=MESH)


def _dma_sems(*counts):
    return [pltpu.SemaphoreType.DMA((n,)) for n in counts]


def _symmetric(ins, out_shapes, sems, copies, peers, aliases=None):
    def start(cins, couts, csems):
        for cp in copies(cins, couts, csems):
            cp.start()

    def finish(cins, couts, csems):
        for cp in copies(cins, couts, csems):
            cp.wait()

    return _Comm(ins, out_shapes, sems, start, finish, peers, aliases)


def _rows_part(a, ref, part):
    if part is None:
        return ref
    p, q, n = part
    ax = _rows_axis(a)
    r = ref.shape[ax] // n
    return ref.at[tuple(pl.ds(p * r, (q - p) * r) if d == ax else slice(None) for d in range(len(ref.shape)))]


def _merge(comms):
    ins, outs, sems, aliases, spans = [], [], [], {}, []
    for cm in comms:
        spans.append((len(ins), len(outs), len(sems)))
        for i, o in cm.aliases.items():
            aliases[len(ins) + i] = len(outs) + o
        ins += cm.ins
        outs += cm.out_shapes
        sems += cm.sems

    def each(fn_name):
        def run(cins, couts, csems):
            for cm, (i0, o0, s0) in zip(comms, spans):
                fn = getattr(cm, fn_name)
                if fn is not None:
                    fn(cins[i0:i0 + len(cm.ins)], couts[o0:o0 + len(cm.out_shapes)], csems[s0:s0 + len(cm.sems)])
        return run

    return _Comm(ins, outs, sems, each("start"), each("finish"), frozenset().union(*[cm.peers for cm in comms]),
                 aliases, mid=each("mid") if any(cm.mid is not None for cm in comms) else None)


def _gather_comm(arrs, locs, full_shapes, part=None, into=None):
    n = len(arrs)

    def own(cins, couts, csems):
        x, y, c = _coords()
        j = 2 * x + y
        return [_remote(_rows_part(a, _half(a, cins[q], h), part), _rows_part(a, _piece(a, couts[q], j, h), part),
                        csems[0].at[2 * q + h], csems[1].at[2 * q + h], (x, y, 1 - c))
                for q, a in enumerate(arrs) for h in range(2)]

    def sends(cins, couts, csems):
        x, y, c = _coords()
        j = 2 * x + y
        return [_remote(_rows_part(a, _half(a, cins[q], c), part), _rows_part(a, _piece(a, couts[q], j, c), part),
                        csems[2].at[3 * q + i], csems[3].at[3 * q + i], (px, py, c))
                for q, a in enumerate(arrs) for i, (px, py) in enumerate(_peer_chips(x, y))]

    def forwards(couts, csems, half_of):
        x, y, c = _coords()
        out = []
        for q, a in enumerate(arrs):
            for i, (px, py) in enumerate(_peer_chips(x, y)):
                landed = _rows_part(a, _piece(a, couts[q], 2 * px + py, half_of(c)), part)
                out.append(_remote(landed, landed, csems[4].at[3 * q + i], csems[5].at[3 * q + i], (x, y, 1 - c)))
        return out

    def start(cins, couts, csems):
        for cp in sends(cins, couts, csems) + own(cins, couts, csems):
            cp.start()

    def finish(cins, couts, csems):
        fw = forwards(couts, csems, lambda c: c)
        for cp, f in zip(sends(cins, couts, csems), fw):
            cp.wait_recv()
            f.start()
        for f in forwards(couts, csems, lambda c: 1 - c):
            f.wait_recv()
        for cp in sends(cins, couts, csems) + fw:
            cp.wait_send()
        for cp in own(cins, couts, csems):
            cp.wait()

    ins = [locs[a] for a in arrs] + ([into[a] for a in arrs] if into else [])
    return _Comm(ins, [SDS(full_shapes[a], BF16) for a in arrs],
                 _dma_sems(2 * n, 2 * n, 3 * n, 3 * n, 3 * n, 3 * n), start, finish, CHIPS + (SIBLING,),
                 aliases={n + q: q for q in range(n)} if into else None)


def _ring_gather_comm(arrs, locs, full_shapes):
    n = len(arrs)

    def own(cins, couts, csems):
        x, y, c = _coords()
        j = 2 * x + y
        return [_remote(_half(a, cins[q], h), _piece(a, couts[q], j, h), csems[0].at[2 * q + h],
                        csems[1].at[2 * q + h], (x, y, 1 - c)) for q, a in enumerate(arrs) for h in range(2)]

    def sends(cins, couts, csems):
        x, y, c = _coords()
        j = 2 * x + y
        return [_remote(_half(a, cins[q], c), _piece(a, couts[q], j, c), csems[2].at[2 * q + i],
                        csems[3].at[2 * q + i], (px, py, c))
                for q, a in enumerate(arrs) for i, (px, py) in enumerate(_peer_chips(x, y)[:2])]

    def relays(couts, csems):
        x, y, c = _coords()
        peers = _peer_chips(x, y)
        out = []
        for q, a in enumerate(arrs):
            for r, (src_p, dst_p) in enumerate(((0, 1), (1, 0))):
                sx, sy = peers[src_p]
                rows = _rows_part(a, _piece(a, couts[q], 2 * sx + sy, c), (r, r + 1, 2))
                out.append(_remote(rows, rows, csems[6].at[2 * q + r], csems[7].at[2 * q + r], (*peers[dst_p], c)))
        return out

    def forwards(couts, csems, half_of, which):
        x, y, c = _coords()
        out = []
        for q, a in enumerate(arrs):
            for i in which:
                px, py = _peer_chips(x, y)[i]
                landed = _piece(a, couts[q], 2 * px + py, half_of(c))
                out.append(_remote(landed, landed, csems[4].at[3 * q + i], csems[5].at[3 * q + i], (x, y, 1 - c)))
        return out

    def start(cins, couts, csems):
        for cp in sends(cins, couts, csems) + own(cins, couts, csems):
            cp.start()

    def mid(cins, couts, csems):
        for cp in sends(cins, couts, csems):
            cp.wait_recv()
        for cp in relays(couts, csems) + forwards(couts, csems, lambda c: c, (0, 1)):
            cp.start()

    def finish(cins, couts, csems):
        for cp in relays(couts, csems):
            cp.wait_recv()
        fw_diag = forwards(couts, csems, lambda c: c, (2,))
        for f in fw_diag:
            f.start()
        for f in forwards(couts, csems, lambda c: 1 - c, (0, 1, 2)):
            f.wait_recv()
        for cp in (sends(cins, couts, csems) + relays(couts, csems)
                   + forwards(couts, csems, lambda c: c, (0, 1)) + fw_diag):
            cp.wait_send()
        for cp in own(cins, couts, csems):
            cp.wait()

    return _Comm([locs[a] for a in arrs], [SDS(full_shapes[a], BF16) for a in arrs],
                 _dma_sems(2 * n, 2 * n, 2 * n, 2 * n, 3 * n, 3 * n, 2 * n, 2 * n), start, finish,
                 CHIPS + (SIBLING,), mid=mid)


def _halves_comm(arrs, gbs):
    n = len(arrs)

    def copies(cins, couts, csems):
        x, y, c = _coords()
        return [_remote(_piece(a, cins[q], k, 1 - c), couts[q].at[k], csems[0].at[4 * q + k], csems[1].at[4 * q + k],
                        (x, y, 1 - c)) for q, a in enumerate(arrs) for k in range(4)]

    return _symmetric([gbs[a] for a in arrs], [SDS((4,) + _piece_shape(a, gbs[a].shape), BF16) for a in arrs],
                      _dma_sems(4 * n, 4 * n), copies, [SIBLING])


def _chips_comm(arrs, ps, part=None, into=None):
    n = len(arrs)

    def copies(cins, couts, csems):
        x, y, c = _coords()
        return [_remote(_rows_part(a, cins[q].at[2 * px + py], part), _rows_part(a, couts[q].at[i], part),
                        csems[0].at[3 * q + i], csems[1].at[3 * q + i], (px, py, c))
                for q, a in enumerate(arrs) for i, (px, py) in enumerate(_peer_chips(x, y))]

    ins = [ps[a] for a in arrs] + ([into[a] for a in arrs] if into else [])
    return _symmetric(ins, [SDS((3,) + ps[a].shape[1:], BF16) for a in arrs], _dma_sems(3 * n, 3 * n), copies, CHIPS,
                      aliases={n + q: q for q in range(n)} if into else None)


def _result_comm(arrs, gs):
    n = len(arrs)

    def copies(cins, couts, csems):
        x, y, c = _coords()
        return [_remote(_half(a, cins[q], c), _half(a, couts[q], c), csems[0].at[q], csems[1].at[q], (x, y, 1 - c))
                for q, a in enumerate(arrs)]

    return _symmetric([gs[a] for a in arrs], [SDS(gs[a].shape, F32) for a in arrs], _dma_sems(n, n), copies,
                      [SIBLING], aliases={q: q for q in range(n)})


def _add_halves(arrs, gbs, lands, c_arr, name):
    n = len(arrs)

    def body(c_ref, *refs):
        del c_ref
        for q in range(n):
            refs[2 * n + q][...] = (refs[q][...].astype(F32) + refs[n + q][...].astype(F32)).astype(BF16)

    g_specs, l_specs, o_specs, blocks = [], [], [], 0
    for a in arrs:
        bs, imap = _piece_block(a, gbs[a].shape)
        ps = _piece_shape(a, gbs[a].shape)
        g_specs.append(pl.BlockSpec(bs, lambda k, c_ref, imap=imap: imap(k, c_ref[0])))
        nd = len(ps)
        l_specs.append(pl.BlockSpec((None,) + ps, lambda k, c_ref, nd=nd: (k,) + (0,) * nd))
        o_specs.append(pl.BlockSpec((None,) + ps, lambda k, c_ref, nd=nd: (k,) + (0,) * nd))
        blocks += 3 * _nbytes(ps, BF16)
    return list(pl.pallas_call(
        body, name=name,
        grid_spec=pltpu.PrefetchScalarGridSpec(
            num_scalar_prefetch=1, grid=(4,), in_specs=g_specs + l_specs, out_specs=o_specs),
        out_shape=[SDS((4,) + _piece_shape(a, gbs[a].shape), BF16) for a in arrs],
        compiler_params=_params(("parallel",), blocks, blocks),
    )(c_arr, *[gbs[a] for a in arrs], *lands))


def _sum_chips(a, p, land, shard_shape, jc_arr, name):
    ps = land.shape[1:]
    ax = _rows_axis(a)
    rows = ps[ax]
    nsub = 2 if rows % (2 * SUBLANES_BF16) == 0 else 1
    bs = tuple(r // nsub if q == ax else r for q, r in enumerate(ps))
    nd = len(ps)

    def at_rows(v):
        return tuple(v if q == ax else 0 for q in range(nd))

    def body(jc_ref, p_ref, l_ref, o_ref):
        del jc_ref
        acc = p_ref[...].astype(F32) + l_ref[0].astype(F32)
        acc = acc + l_ref[1].astype(F32)
        o_ref[...] = acc + l_ref[2].astype(F32)

    blocks = 4 * _nbytes(bs, BF16) + _nbytes(bs, F32)
    return pl.pallas_call(
        body, name=name,
        grid_spec=pltpu.PrefetchScalarGridSpec(
            num_scalar_prefetch=1, grid=(nsub,),
            in_specs=[pl.BlockSpec((None,) + bs, lambda s, jc: (jc[0],) + at_rows(s)),
                      pl.BlockSpec((3,) + bs, lambda s, jc: (0,) + at_rows(s))],
            out_specs=pl.BlockSpec(bs, lambda s, jc: at_rows(jc[1] * nsub + s))),
        out_shape=SDS(shard_shape, F32),
        compiler_params=_params(("parallel",), blocks, 2 * _nbytes(bs, F32)),
    )(jc_arr, p, land)


def _small_comm(v):
    rows = v.shape[0]

    def copies(cins, couts, csems):
        x, y, c = _coords()
        me = 4 * x + 2 * y + c
        out = [pltpu.make_async_copy(cins[0], couts[0].at[me], csems[0].at[0])]
        for dlt in range(1, 8):
            px = 1 - x if (dlt >> 2) & 1 else x
            py = 1 - y if (dlt >> 1) & 1 else y
            pc = 1 - c if dlt & 1 else c
            out.append(_remote(cins[0], couts[0].at[me], csems[1].at[dlt - 1], csems[2].at[dlt - 1], (px, py, pc)))
        return out

    return _symmetric([v], [SDS((8, rows, LANES), F32)], _dma_sems(1, 7, 7), copies, EVERYONE)


def _sum8(slots, name):
    def body(s_ref, o_ref):
        acc = s_ref[0]
        for i in range(1, 8):
            acc = acc + s_ref[i]
        o_ref[...] = acc

    return pl.pallas_call(
        body, name=name,
        in_specs=[pl.BlockSpec(memory_space=pltpu.VMEM)], out_specs=pl.BlockSpec(memory_space=pltpu.VMEM),
        out_shape=SDS(slots.shape[1:], F32),
    )(slots)


def _adamw(w, g, m, v, name, g_plane=None):
    rows, cols = w.shape
    tr = _tile(rows, max(SUBLANES_F32, (256 * 1024 // cols) // SUBLANES_F32 * SUBLANES_F32), SUBLANES_F32)

    def body(w_ref, g_ref, m_ref, v_ref, go_ref, d_ref, mo_ref, vo_ref):
        gr = g_ref[...]
        mn = ADAM_B1 * m_ref[...] + (1.0 - ADAM_B1) * gr
        vn = ADAM_B2 * v_ref[...] + (1.0 - ADAM_B2) * (gr * gr)
        m_hat = mn / (1.0 - ADAM_B1 ** ADAM_STEP)
        v_hat = vn / (1.0 - ADAM_B2 ** ADAM_STEP)
        d_ref[...] = -ADAM_LR * (m_hat / (jnp.sqrt(v_hat) + ADAM_EPS) + ADAM_WD * w_ref[...])
        go_ref[...] = gr
        mo_ref[...] = mn
        vo_ref[...] = vn

    spec = pl.BlockSpec((tr, cols), lambda i: (i, 0))
    g_spec = spec if g_plane is None else pl.BlockSpec((None, tr, cols), lambda i: (g_plane, i, 0))
    return pl.pallas_call(
        body, name=name, grid=(rows // tr,),
        in_specs=[spec, g_spec, spec, spec], out_specs=[spec, spec, spec, spec],
        out_shape=[SDS((rows, cols), F32)] * 4,
        compiler_params=_params(("parallel",), 8 * _nbytes((tr, cols), F32), 4 * _nbytes((tr, cols), F32)),
    )(w, g, m, v)


def _pack(parts):
    rows = []
    for p in parts:
        r = p.reshape(-1, LANES)
        pad = (-r.shape[0]) % SUBLANES_F32
        if pad:
            r = jnp.pad(r, ((0, pad), (0, 0)))
        rows.append(r)
    return jnp.concatenate(rows, axis=0)


def _unpack(packed, shapes):
    out, at = [], 0
    for s in shapes:
        n = 1
        for q in s:
            n *= q
        r = n // LANES
        out.append(packed[at:at + r].reshape(s))
        at += r + (-r) % SUBLANES_F32
    return out


def kernel(x, norm_mix, w_in, pool_w, pool_scale, w_pool_proj, conv_w, w_conv_out, w_o, norm_ffn, w_up, ffn_conv_w, ffn_conv_b, w_down, norm_final, loss_target, m_norm_mix, m_w_in, m_pool_w, m_pool_scale, m_w_pool_proj, m_conv_w, m_w_conv_out, m_w_o, m_norm_ffn, m_w_up, m_ffn_conv_w, m_ffn_conv_b, m_w_down, m_norm_final, v_norm_mix, v_w_in, v_pool_w, v_pool_scale, v_w_pool_proj, v_conv_w, v_w_conv_out, v_w_o, v_norm_ffn, v_w_up, v_ffn_conv_w, v_ffn_conv_b, v_w_down, v_norm_final):
    nseq, seq, d = x.shape
    t = nseq * seq
    f = w_down.shape[1] * 4
    c = d // N_GROUPS
    xy = lax.axis_index("x") * 2 + lax.axis_index("y")
    c_arr = lax.axis_index("c").astype(jnp.int32).reshape(1)
    jc_arr = jnp.stack([xy, lax.axis_index("c")]).astype(jnp.int32)
    nsh = 4
    zero = jnp.zeros((), jnp.int32)

    locs = [w_in[0].astype(BF16),
            jnp.stack([w_pool_proj[0], w_conv_out[0], w_o[0]]).astype(BF16),
            w_up[0].astype(BF16), w_down[0].astype(BF16), pool_w[0].astype(BF16)]
    full_shapes = [(nsh, d, N_SPLITS * d // nsh), (3, d, d), (nsh, d, 2 * f // nsh), (f, d), (N_GROUPS, c, c)]

    cw_pad = lax.dynamic_update_slice(jnp.zeros((3, d), F32), conv_w[0], (zero, xy * (d // 4)))
    fw_pad = lax.dynamic_update_slice(jnp.zeros((3, 2 * f), F32), ffn_conv_w[0], (zero, xy * (f // 2)))
    small_w = _pack([cw_pad, fw_pad]) * 0.5

    x2d = x.reshape(t, d)
    tgt = loss_target.reshape(t, d)
    ax, ay = lax.axis_index("x"), lax.axis_index("y")
    order = jnp.stack([xy, 2 * (1 - ax) + ay, 2 * ax + 1 - ay, 2 * (1 - ax) + 1 - ay]).astype(jnp.int32)
    (z, h1, w_in_f), (pool_w_f, w3_f, slots_w) = _fwd_in(
        x2d, norm_mix, locs[0], order,
        _merge([_gather_comm([4], locs, full_shapes), _gather_comm([1], locs, full_shapes, part=(0, 1, 2)),
                _small_comm(small_w)]))
    conv_w_f, ffn_cw_f = _unpack(_sum8(slots_w, "sum8_weights"), [(3, d), (3, 2 * f)])
    ffn_cw_p = ffn_cw_f.reshape(3, 2, f).transpose(1, 0, 2)
    ffn_cb_p = ffn_conv_b.reshape(2, 1, f)
    (lhs3,), (w3_f,) = _mixer_mid_fwd(z, pool_w_f, pool_scale, conv_w_f, nseq,
                                      _gather_comm([1], locs, full_shapes, part=(1, 2, 2), into={1: w3_f}))
    (lhs3, ypc, x1, h2), (w_up_f,) = _mixer_out(lhs3, z, x2d, w3_f, norm_ffn,
                                                _ring_gather_comm([2], locs, full_shapes))
    (u0,), (w_down_f,) = _ffn_up(h2, w_up_f, f, _gather_comm([3], locs, full_shapes))
    act, ua = _ffn_mid_fwd(u0, ffn_cw_p, ffn_cb_p, nseq)
    dx2, dx2b, loss11, g_norm_final = _ffn_down_loss(act, w_down_f, x1, tgt, norm_final.reshape(1, d))

    gbs, lands, ps, lands2, rs = {}, {}, {}, {}, {}
    tn_up = _tile(2 * f // nsh, 1408, LANES)
    npp = f // tn_up

    def add(arrs, name):
        for a, p in zip(arrs, _add_halves(arrs, gbs, [lands[a] for a in arrs], c_arr, name)):
            ps[a] = p

    def summed(a):
        rs[a] = _sum_chips(a, ps[a], lands2[a], _shard_shape(a, full_shapes[a]), jc_arr, "sum_chips_%d" % a)

    (gbs[3],), _ = _wgrad(act, dx2b, "wgrad_down", tr=tn_up, tn=d)
    (da,), (lands[3],) = _ffn_bwd_da(dx2b, w_down_f, _halves_comm([3], gbs))
    add([3], "add_halves_down")
    (du0, g_ffn_cw_p, g_ffn_cb_p), (lands2[3],) = _ffn_mid_bwd(da, u0, ua, ffn_cw_p, nseq, _chips_comm([3], ps))
    summed(3)
    (gbs[2],), (rs[3],) = _wgrad(h2, du0, "wgrad_up", tr=d, tn=tn_up, b_plane_of=lambda n: (n // npp, n % npp),
                                 out_shards=nsh, comm=_result_comm([3], rs))
    (dx1, rhs3, g_norm_ffn), (lands[2],) = _ffn_bwd_dx1(du0, w_up_f, x1, dx2, norm_ffn, 3, _halves_comm([2], gbs))
    add([2], "add_halves_up")
    (rhs3, dz, dpq), (lands2[2],) = _mixer_bwd(rhs3, z, ypc, w3_f, _chips_comm([2], ps, part=(0, 1, 2)))
    (gbs[1],), (lands2[2],) = _wgrad3(lhs3, rhs3, _chips_comm([2], ps, part=(1, 2, 2), into=lands2))
    summed(2)
    (dz, g_conv_w), (lands[1], rs[2]) = _conv_bwd(dz, dpq, z, conv_w_f, nseq,
                                                  _merge([_halves_comm([1], gbs), _result_comm([2], rs)]))
    add([1], "add_halves_sq3")
    (dz, g_pool_w, g_pool_scale), _ = _pool_bwd_call(dz, dpq, z, pool_w_f, pool_scale, nseq)
    gbs[4] = g_pool_w.astype(BF16)
    (gbs[0],), (lands2[1],) = _wgrad_in(h1, dz, nsh, _chips_comm([1], ps))
    summed(1)
    lands[0], lands[4] = _run_comm(_halves_comm([0, 4], gbs), "exchange_halves_in")
    add([0, 4], "add_halves_in")
    g_ffn_cw = g_ffn_cw_p.transpose(1, 0, 2).reshape(3, 2 * f)
    small_a = _pack([g_pool_scale, g_norm_ffn, g_ffn_cb_p.reshape(1, 2 * f), g_norm_final.reshape(d), g_conv_w,
                     g_ffn_cw, jnp.pad(loss11, ((0, SUBLANES_F32 - 1), (0, LANES - 1)))])
    (grad_x, g_norm_mix), (lands2[0], lands2[4], rs[1], slots_a) = _mixer_bwd_dx(
        dz, w_in_f, x2d, dx1, norm_mix,
        _merge([_chips_comm([0, 4], ps), _result_comm([1], rs), _small_comm(small_a)]))
    summed(0)
    summed(4)
    rs[0], rs[4], slots_b = _run_comm(_merge([_result_comm([0, 4], rs), _small_comm(_pack([g_norm_mix]))]),
                                      "exchange_result_in")
    shapes_a = [(1, d), (1, d), (1, 2 * f), (d,), (3, d), (3, 2 * f), (SUBLANES_F32, LANES)]
    gs_pool_scale, gs_norm_ffn, gs_ffn_cb, gs_norm_final, gs_conv_w, gs_ffn_cw, loss_blk = _unpack(
        _sum8(slots_a, "sum8_grads"), shapes_a)
    (gs_norm_mix,) = _unpack(_sum8(slots_b, "sum8_norm_mix"), [(1, d)])
    gs_conv_w = lax.dynamic_slice(gs_conv_w, (zero, xy * (d // 4)), (3, d // 4))
    gs_ffn_cw = lax.dynamic_slice(gs_ffn_cw, (zero, xy * (f // 2)), (3, f // 2))

    def upd(w, g, m, v, name, g_plane=None):
        shape = w.shape
        rows = 1
        for q in shape[:-1]:
            rows *= q
        g2 = g if g_plane is not None else g.reshape(rows, shape[-1])
        outs = _adamw(w.reshape(rows, shape[-1]), g2, m.reshape(rows, shape[-1]), v.reshape(rows, shape[-1]),
                      name, g_plane)
        return [o.reshape(shape) for o in outs]

    res = {
        "w_in": upd(w_in, rs[0], m_w_in, v_w_in, "adamw_w_in"),
        "pool_w": upd(pool_w, rs[4], m_pool_w, v_pool_w, "adamw_pool_w"),
        "w_pool_proj": upd(w_pool_proj, rs[1], m_w_pool_proj, v_w_pool_proj, "adamw_w_pool_proj", 0),
        "w_conv_out": upd(w_conv_out, rs[1], m_w_conv_out, v_w_conv_out, "adamw_w_conv_out", 1),
        "w_o": upd(w_o, rs[1], m_w_o, v_w_o, "adamw_w_o", 2),
        "w_up": upd(w_up, rs[2], m_w_up, v_w_up, "adamw_w_up"),
        "w_down": upd(w_down, rs[3], m_w_down, v_w_down, "adamw_w_down"),
    }

    small_names = ["norm_mix", "pool_scale", "norm_ffn", "ffn_conv_b", "norm_final", "conv_w", "ffn_conv_w"]
    small_ws = [norm_mix, pool_scale, norm_ffn, ffn_conv_b, norm_final, conv_w, ffn_conv_w]
    small_ms = [m_norm_mix, m_pool_scale, m_norm_ffn, m_ffn_conv_b, m_norm_final, m_conv_w, m_ffn_conv_w]
    small_vs = [v_norm_mix, v_pool_scale, v_norm_ffn, v_ffn_conv_b, v_norm_final, v_conv_w, v_ffn_conv_w]
    small_gs = [gs_norm_mix, gs_pool_scale, gs_norm_ffn, gs_ffn_cb, gs_norm_final, gs_conv_w, gs_ffn_cw]
    _, sd, sm, sv = _adamw(_pack(small_ws), _pack(small_gs), _pack(small_ms), _pack(small_vs), "adamw_small")
    shapes = [w.shape for w in small_ws]
    sd, sm, sv = _unpack(sd, shapes), _unpack(sm, shapes), _unpack(sv, shapes)
    for i, nm in enumerate(small_names):
        res[nm] = [small_gs[i].reshape(shapes[i]), sd[i], sm[i], sv[i]]

    order = ["norm_mix", "w_in", "pool_w", "pool_scale", "w_pool_proj", "conv_w", "w_conv_out", "w_o", "norm_ffn",
             "w_up", "ffn_conv_w", "ffn_conv_b", "w_down", "norm_final"]
    return (loss_blk[0, 0], grad_x.reshape(x.shape), *[res[n][0] for n in order], *[res[n][1] for n in order],
            *[res[n][2] for n in order], *[res[n][3] for n in order])
```

```python
import math

import jax
import jax.numpy as jnp
from jax import lax
from jax.experimental import pallas as pl
from jax.experimental.pallas import tpu as pltpu

F32 = jnp.float32
BF16 = jnp.bfloat16
SDS = jax.ShapeDtypeStruct
MESH = pl.DeviceIdType.MESH

RMS_EPS = 1e-6
POOL_WINDOWS = (2, 4, 8, 16)
N_GROUPS = len(POOL_WINDOWS)
N_SPLITS = 6

ADAM_LR = 0.001
ADAM_B1 = 0.9
ADAM_B2 = 0.999
ADAM_EPS = 1e-08
ADAM_WD = 0.01
ADAM_STEP = 10

LANES = 128
SUBLANES_F32 = 8
SUBLANES_BF16 = 16
VMEM_BYTES = 64 * 1024 * 1024
VMEM_CAP = VMEM_BYTES - 8 * 1024 * 1024
VMEM_FLOOR = 16 * 1024 * 1024

ANY = pl.BlockSpec(memory_space=pl.ANY)


def _tile(dim, pref, align):
    if dim <= pref:
        return dim
    t = (pref // align) * align
    while t >= align:
        if dim % t == 0:
            return t
        t -= align
    return dim


def _nbytes(shape, dtype):
    n = 1
    for s in shape:
        n *= s
    return n * jnp.dtype(dtype).itemsize


def _params(sem, block_bytes, temp_bytes=0, collective_id=None):
    need = 2 * block_bytes + temp_bytes + 4 * 1024 * 1024
    return pltpu.CompilerParams(dimension_semantics=sem, collective_id=collective_id,
                                vmem_limit_bytes=int(min(max(need, VMEM_FLOOR), VMEM_CAP)))


SIBLING = (0, 0, 1)
CHIPS = ((1, 0, 0), (0, 1, 0), (1, 1, 0))
EVERYONE = tuple((a, b, c) for a in range(2) for b in range(2) for c in range(2) if a + b + c)
PEER_SETS = (frozenset([SIBLING]), frozenset(CHIPS), frozenset(CHIPS + (SIBLING,)), frozenset(EVERYONE))
MID_AT = 0.75


def _collective_id(peers):
    return PEER_SETS.index(frozenset(peers))


def _handshake(peers):
    x, y, c = lax.axis_index("x"), lax.axis_index("y"), lax.axis_index("c")
    bar = pltpu.get_barrier_semaphore()
    for fx, fy, fc in sorted(peers):
        dev = (1 - x if fx else x, 1 - y if fy else y, 1 - c if fc else c)
        pl.semaphore_signal(bar, inc=1, device_id=dev, device_id_type=MESH)
    pl.semaphore_wait(bar, len(peers))


class _Comm:
    def __init__(self, ins, out_shapes, sems, start, finish, peers, aliases=None, mid=None):
        self.ins = list(ins)
        self.out_shapes = list(out_shapes)
        self.sems = list(sems)
        self.start = start
        self.finish = finish
        self.mid = mid
        self.peers = frozenset(peers)
        self.aliases = dict(aliases or {})


def _pcall(body, *, name, grid, in_specs, out_specs, out_shape, sem, blocks, temps=0, scratch_shapes=(),
           input_output_aliases=None, comm=None):
    in_specs = list(in_specs)
    out_specs = list(out_specs)
    out_shape = list(out_shape)
    scratch_shapes = list(scratch_shapes)
    aliases = dict(input_output_aliases or {})
    n_in, n_out, n_scr = len(in_specs), len(out_shape), len(scratch_shapes)
    if comm is None:
        call = pl.pallas_call(
            body, name=name, grid=grid, in_specs=in_specs, out_specs=out_specs, out_shape=out_shape,
            scratch_shapes=scratch_shapes, input_output_aliases=aliases,
            compiler_params=_params(sem, blocks, temps))
        return lambda *args: (list(call(*args)), [])

    nci, nco = len(comm.ins), len(comm.out_shapes)
    n_steps = 1
    for g in grid:
        n_steps *= g

    def hosted(*refs):
        ins = refs[:n_in]
        cins = refs[n_in:n_in + nci]
        outs = refs[n_in + nci:n_in + nci + n_out]
        couts = refs[n_in + nci + n_out:n_in + nci + n_out + nco]
        scr = refs[n_in + nci + n_out + nco:n_in + nci + n_out + nco + n_scr]
        csems = refs[n_in + nci + n_out + nco + n_scr:]
        first = None
        last = None
        step = 0
        for q, g in enumerate(grid):
            pid = pl.program_id(q)
            first = (pid == 0) if first is None else first & (pid == 0)
            last = (pid == g - 1) if last is None else last & (pid == g - 1)
            step = step * g + pid

        @pl.when(first)
        def _():
            _handshake(comm.peers)
            comm.start(cins, couts, csems)

        if comm.mid is not None:
            @pl.when(step == int(MID_AT * n_steps))
            def _():
                comm.mid(cins, couts, csems)

        body(*ins, *outs, *scr)

        @pl.when(last)
        def _():
            comm.finish(cins, couts, csems)

    for i, o in comm.aliases.items():
        aliases[n_in + i] = n_out + o
    call = pl.pallas_call(
        hosted, name=name, grid=grid, in_specs=in_specs + [ANY] * nci, out_specs=out_specs + [ANY] * nco,
        out_shape=out_shape + comm.out_shapes, scratch_shapes=scratch_shapes + comm.sems,
        input_output_aliases=aliases,
        compiler_params=_params(("arbitrary",) * len(grid), blocks, temps, _collective_id(comm.peers)))

    def run(*args):
        res = call(*args, *comm.ins)
        return list(res[:n_out]), list(res[n_out:])

    return run


def _dot(a, b):
    return jnp.dot(a, b, preferred_element_type=F32)


def _dot_tb(a, b):
    return lax.dot_general(a, b, (((1,), (1,)), ((), ())), preferred_element_type=F32)


def _dot_ta(a, b):
    return lax.dot_general(a, b, (((0,), (0,)), ((), ())), preferred_element_type=F32)


def _rms_fwd(x):
    inv = lax.rsqrt(jnp.mean(x * x, axis=-1, keepdims=True) + RMS_EPS)
    return x * inv, inv


def _rms_bwd(dy, xhat, inv, g):
    gd = dy * g
    return inv * (gd - xhat * jnp.mean(gd * xhat, axis=-1, keepdims=True))


def _sigmoid(x):
    return 1.0 / (1.0 + jnp.exp(-x))


def _shift_down(x, k, row):
    return jnp.where(row >= k, pltpu.roll(x, k, 0), 0.0)


def _shift_up(x, k, row):
    s = x.shape[0]
    return jnp.where(row < s - k, pltpu.roll(x, s - k, 0), 0.0)


def _pool_fwd(u, win, row):
    s = u
    k = 1
    while k < win:
        s = s + _shift_down(s, k, row)
        k *= 2
    cnt = jnp.minimum(row + 1, win).astype(F32)
    return s / cnt - u


def _pool_bwd(dp, win, row):
    cnt = jnp.minimum(row + 1, win).astype(F32)
    s = dp / cnt
    k = 1
    while k < win:
        s = s + _shift_up(s, k, row)
        k *= 2
    return s - dp


def _acc_over(k, nk, part, acc, o_ref):
    @pl.when(k == 0)
    def _():
        acc[...] = part

    @pl.when(k > 0)
    def _():
        acc[...] += part

    @pl.when(k == nk - 1)
    def _():
        o_ref[...] = acc[...].astype(o_ref.dtype)


def _fwd_in(x, g, w_loc, order, comm):
    t, d = x.shape
    ws = w_loc.shape[1]
    nsh = order.shape[0]
    assert nsh == 4, "the shard walk below is written for the 2 x 2 chips of the mesh"
    tm = _tile(t, 1024, SUBLANES_BF16)
    ni = t // tm
    nci, nco = len(comm.ins), len(comm.out_shapes)
    all_peers = comm.peers | frozenset(CHIPS + (SIBLING,))

    def body(order_ref, x_ref, g_ref, loc_ref, *rest):
        del order_ref
        cins = rest[:nci]
        z_ref, h_ref, full_ref = rest[nci:nci + 3]
        couts = rest[nci + 3:nci + 3 + nco]
        (hs, wbuf, wsem, own_s, own_r, snd_s, snd_r, fwd_s, fwd_r, rly_s, rly_r) = rest[nci + 3 + nco:nci + 14 + nco]
        csems = rest[nci + 14 + nco:]
        j = pl.program_id(0)
        i = pl.program_id(1)
        x_, y_, c_ = _coords()
        own = 2 * x_ + y_
        sib = (x_, y_, 1 - c_)
        peers = _peer_chips(x_, y_)

        def sends():
            return [_remote(_half(0, loc_ref, c_), _piece(0, full_ref, own, c_), snd_s.at[p], snd_r.at[p], (px, py, c_))
                    for p, (px, py) in enumerate(peers[:2])]

        def relays():
            out = []
            for q, (src_p, dst_p) in enumerate(((0, 1), (1, 0))):
                sx, sy = peers[src_p]
                part = _rows_part(0, _piece(0, full_ref, 2 * sx + sy, c_), (q, q + 1, 2))
                out.append(_remote(part, part, rly_s.at[q], rly_r.at[q], (*peers[dst_p], c_)))
            return out

        def owns():
            return [_remote(_half(0, loc_ref, h), _piece(0, full_ref, own, h), own_s.at[h], own_r.at[h], sib)
                    for h in range(2)]

        def forward(p, half):
            px, py = peers[p]
            landed = _piece(0, full_ref, 2 * px + py, half)
            return _remote(landed, landed, fwd_s.at[p], fwd_r.at[p], sib)

        def load(src, slot):
            return pltpu.make_async_copy(src, wbuf.at[slot], wsem.at[slot])

        @pl.when((j == 0) & (i == 0))
        def _():
            _handshake(all_peers)
            for cp in sends() + owns():
                cp.start()
            load(loc_ref, 0).start()

        @pl.when(j == 0)
        def _():
            xh, _ = _rms_fwd(x_ref[...])
            h = (xh * g_ref[...]).astype(BF16)
            hs[pl.ds(pl.multiple_of(i * tm, tm), tm), :] = h
            h_ref[...] = h

        slot = j % 2

        @pl.when(i == 0)
        def _():
            load(loc_ref, slot).wait()

        z_ref[...] = _dot(hs[pl.ds(pl.multiple_of(i * tm, tm), tm), :], wbuf[slot]).astype(BF16)

        def load_shard(p, into):
            px, py = peers[p]
            forward(p, 1 - c_).wait_recv()
            load(full_ref.at[2 * px + py], into).start()

        @pl.when((j == 0) & (i == ni - 1))
        def _():
            for cp in sends():
                cp.wait_recv()
            for cp in relays() + [forward(0, c_), forward(1, c_)]:
                cp.start()
            load_shard(0, 1)
            comm.start(cins, couts, csems)

        @pl.when((j == 1) & (i == 0))
        def _():
            load_shard(1, 0)

        @pl.when((j == 2) & (i == max(ni - 2, 0)))
        def _():
            for cp in relays():
                cp.wait_recv()
            forward(2, c_).start()
            load_shard(2, 1)

        @pl.when((j == nsh - 1) & (i == ni - 1))
        def _():
            for cp in sends() + relays() + [forward(p, c_) for p in range(nsh - 1)]:
                cp.wait_send()
            for cp in owns():
                cp.wait()
            comm.finish(cins, couts, csems)

    last = ni - 1
    blocks = _nbytes((tm, d), F32) + _nbytes((tm, ws), BF16) + _nbytes((tm, d), BF16)
    scratch = _nbytes((t, d), BF16) + 2 * _nbytes((d, ws), BF16)
    res = pl.pallas_call(
        body, name="fwd_in",
        grid_spec=pltpu.PrefetchScalarGridSpec(
            num_scalar_prefetch=1, grid=(nsh, ni),
            in_specs=[pl.BlockSpec((tm, d), lambda j, i, o: (jnp.where(j == 0, i, last), 0)),
                      pl.BlockSpec((1, d), lambda j, i, o: (0, 0)), ANY] + [ANY] * nci,
            out_specs=[pl.BlockSpec((tm, ws), lambda j, i, o: (i, o[j])),
                       pl.BlockSpec((tm, d), lambda j, i, o: (jnp.where(j == 0, i, last), 0)), ANY] + [ANY] * nco,
            scratch_shapes=[pltpu.VMEM((t, d), BF16), pltpu.VMEM((2, d, ws), BF16)]
            + _dma_sems(2, 2, 2, 2, 2, nsh - 1, nsh - 1, 2, 2) + comm.sems),
        out_shape=[SDS((t, nsh * ws), BF16), SDS((t, d), BF16), SDS((nsh, d, ws), BF16)] + comm.out_shapes,
        input_output_aliases={4 + i: 3 + o for i, o in comm.aliases.items()},
        compiler_params=_params(("arbitrary", "arbitrary"), blocks, scratch + 3 * _nbytes((tm, d), F32),
                                _collective_id(all_peers)),
    )(order, x, g, w_loc, *comm.ins)
    return list(res[:3]), list(res[3:])


def _mixer_mid_fwd(z, pool_w, pool_scale, conv_w, nseq, comm=None):
    t = z.shape[0]
    d = pool_scale.shape[1]
    s = t // nseq
    c = d // N_GROUPS

    def body(zp, zb, zc, zv, pw, ps, cw, o):
        j = pl.program_id(1)
        row = lax.broadcasted_iota(jnp.int32, (s, c), 0)
        for gi, win in enumerate(POOL_WINDOWS):
            @pl.when(j == gi)
            def _(win=win):
                pooled = _pool_fwd(zp[...].astype(F32), win, row)
                o[0] = (_dot(pooled.astype(BF16), pw[...]) * ps[...]).astype(BF16)

        cv = zc[...].astype(F32) * zv[...].astype(F32)
        cc = (cw[pl.ds(2, 1), :] * cv + cw[pl.ds(1, 1), :] * _shift_down(cv, 1, row)
              + cw[pl.ds(0, 1), :] * _shift_down(cv, 2, row))
        o[1] = (zb[...].astype(F32) * cc).astype(BF16)

    blocks = 4 * _nbytes((s, c), BF16) + _nbytes((c, c), BF16) + _nbytes((2, s, c), BF16)
    return _pcall(
        body, name="mixer_mid_fwd", grid=(nseq, N_GROUPS),
        in_specs=[pl.BlockSpec((s, c), lambda b, j: (b, j)),
                  pl.BlockSpec((s, c), lambda b, j: (b, N_GROUPS + j)),
                  pl.BlockSpec((s, c), lambda b, j: (b, 2 * N_GROUPS + j)),
                  pl.BlockSpec((s, c), lambda b, j: (b, 3 * N_GROUPS + j)),
                  pl.BlockSpec((None, c, c), lambda b, j: (j, 0, 0)),
                  pl.BlockSpec((1, c), lambda b, j: (0, j)),
                  pl.BlockSpec((3, c), lambda b, j: (0, j))],
        out_specs=[pl.BlockSpec((2, s, c), lambda b, j: (0, b, j))],
        out_shape=[SDS((3, t, d), BF16)],
        sem=("parallel", "parallel"), blocks=blocks, temps=8 * _nbytes((s, c), F32), comm=comm,
    )(z, z, z, z, pool_w, pool_scale, conv_w)


def _mixer_out(lhs3, z, x, w3, g_ffn, comm=None):
    t, d = x.shape
    tm = _tile(t, 256, SUBLANES_BF16)

    def body(pq, zgp, zgc, x_ref, w_ref, g_ref, mrg, ypc, x1o, h2o):
        yp = _dot(pq[0], w_ref[0])
        yc = _dot(pq[1], w_ref[1])
        m = _sigmoid(zgp[...].astype(F32)) * yp + _sigmoid(zgc[...].astype(F32)) * yc
        mb = m.astype(BF16)
        x1 = x_ref[...] + _dot(mb, w_ref[2])
        ypc[0] = yp.astype(BF16)
        ypc[1] = yc.astype(BF16)
        mrg[...] = mb
        x1o[...] = x1
        xh, _ = _rms_fwd(x1)
        h2o[...] = (xh * g_ref[...]).astype(BF16)

    blocks = (_nbytes((2, tm, d), BF16) * 2 + _nbytes((tm, d), BF16) * 4 + _nbytes((tm, d), F32) * 2
              + _nbytes((3, d, d), BF16))
    return _pcall(
        body, name="mixer_out", grid=(t // tm,),
        in_specs=[pl.BlockSpec((2, tm, d), lambda i: (0, i, 0)),
                  pl.BlockSpec((tm, d), lambda i: (i, 4)),
                  pl.BlockSpec((tm, d), lambda i: (i, 5)),
                  pl.BlockSpec((tm, d), lambda i: (i, 0)),
                  pl.BlockSpec((3, d, d), lambda i: (0, 0, 0)),
                  pl.BlockSpec((1, d), lambda i: (0, 0))],
        out_specs=[pl.BlockSpec((None, tm, d), lambda i: (2, i, 0)),
                   pl.BlockSpec((2, tm, d), lambda i: (0, i, 0)),
                   pl.BlockSpec((tm, d), lambda i: (i, 0)),
                   pl.BlockSpec((tm, d), lambda i: (i, 0))],
        out_shape=[SDS(lhs3.shape, BF16), SDS((2, t, d), BF16), SDS((t, d), F32), SDS((t, d), BF16)],
        input_output_aliases={0: 0},
        sem=("parallel",), blocks=blocks, temps=8 * _nbytes((tm, d), F32), comm=comm,
    )(lhs3, z, z, x, w3, g_ffn)


def _ffn_up(h2, w_up, f, comm=None):
    t, d = h2.shape
    _, _, ws = w_up.shape
    tm = _tile(t, 1024, SUBLANES_BF16)
    tn = _tile(ws, 1408, LANES)
    nps = ws // tn
    npp = f // tn

    def body(h_ref, w_ref, o_ref):
        o_ref[...] = _dot(h_ref[...], w_ref[...]).astype(BF16)

    blocks = _nbytes((tm, d), BF16) + _nbytes((d, tn), BF16) + _nbytes((tm, tn), BF16)
    return _pcall(
        body, name="ffn_up", grid=(t // tm, 2 * npp),
        in_specs=[pl.BlockSpec((tm, d), lambda i, j: (i, 0)),
                  pl.BlockSpec((None, d, tn), lambda i, j: (j // nps, 0, j % nps))],
        out_specs=[pl.BlockSpec((None, tm, tn), lambda i, j: (j // npp, i, j % npp))],
        out_shape=[SDS((2, t, f), BF16)],
        sem=("parallel", "parallel"), blocks=blocks, temps=_nbytes((tm, tn), F32), comm=comm,
    )(h2, w_up)


def _conv3_rows(u, u1, u2, w_ref, p):
    return w_ref[p, pl.ds(2, 1), :] * u + w_ref[p, pl.ds(1, 1), :] * u1 + w_ref[p, pl.ds(0, 1), :] * u2


WGRAD_TOKENS = 2048
WGRAD_TOKENS_WIDE = 4096
CHUNK = 64
HALO = SUBLANES_F32


def _up1_up2(u, nxt):
    rows = u.shape[0]
    ext = jnp.concatenate([u, nxt], axis=0)
    n = rows + HALO
    return pltpu.roll(ext, n - 1, 0)[:rows], pltpu.roll(ext, n - 2, 0)[:rows]


def _fold8(x):
    return jnp.sum(x.reshape(x.shape[0] // SUBLANES_F32, SUBLANES_F32, x.shape[1]), axis=0)


def _ffn_mid_fwd(u0, cw, cb, nseq):
    _, t, f = u0.shape
    s = t // nseq
    c = _tile(f, 256, LANES)

    def body(u_ref, w_ref, b_ref, a_ref, uo_ref):
        row = lax.broadcasted_iota(jnp.int32, (s, c), 0)
        act = []
        for p in range(2):
            u = u_ref[p].astype(F32)
            act.append(_conv3_rows(u, _shift_down(u, 1, row), _shift_down(u, 2, row), w_ref, p) + b_ref[p])
            uo_ref[p] = act[p].astype(BF16)
        ug, uv = act
        a_ref[...] = (ug * _sigmoid(ug) * uv).astype(BF16)

    blocks = 2 * _nbytes((2, s, c), BF16) + _nbytes((s, c), BF16)
    outs, _ = _pcall(
        body, name="ffn_mid_fwd", grid=(f // c, nseq),
        in_specs=[pl.BlockSpec((2, s, c), lambda j, b: (0, b, j)),
                  pl.BlockSpec((2, 3, c), lambda j, b: (0, 0, j)),
                  pl.BlockSpec((2, 1, c), lambda j, b: (0, 0, j))],
        out_specs=[pl.BlockSpec((s, c), lambda j, b: (b, j)),
                   pl.BlockSpec((2, s, c), lambda j, b: (0, b, j))],
        out_shape=[SDS((t, f), BF16), SDS((2, t, f), BF16)],
        sem=("parallel", "parallel"), blocks=blocks, temps=8 * _nbytes((s, c), F32),
    )(u0, cw, cb)
    return outs


def _ffn_down_loss(a, w_down, x1, tgt, g_fin):
    t, f = a.shape
    d = x1.shape[1]
    tm = _tile(t, 256, SUBLANES_BF16)
    nsteps = t // tm

    def body(a_ref, w_ref, x1_ref, t_ref, g_ref, dx_ref, dxb_ref, loss_ref, gg_ref, lacc):
        i = pl.program_id(0)

        @pl.when(i == 0)
        def _():
            lacc[...] = jnp.zeros_like(lacc)
            gg_ref[...] = jnp.zeros_like(gg_ref)

        x2 = x1_ref[...] + _dot(a_ref[...], w_ref[...])
        xh, inv = _rms_fwd(x2)
        g = g_ref[...]
        e = xh * g - t_ref[...]
        lacc[...] += jnp.sum(e * e, axis=0, keepdims=True)
        dy = e * (1.0 / d)
        gg_ref[...] += jnp.sum(dy * xh, axis=0, keepdims=True)
        dx2 = _rms_bwd(dy, xh, inv, g)
        dx_ref[...] = dx2
        dxb_ref[...] = dx2.astype(BF16)

        @pl.when(i == nsteps - 1)
        def _():
            loss_ref[...] = jnp.sum(lacc[...], axis=1, keepdims=True) * (0.5 / d)

    blocks = (_nbytes((tm, f), BF16) + _nbytes((f, d), BF16) + 3 * _nbytes((tm, d), F32) + _nbytes((tm, d), BF16))
    outs, _ = _pcall(
        body, name="ffn_down_loss", grid=(nsteps,),
        in_specs=[pl.BlockSpec((tm, f), lambda i: (i, 0)), pl.BlockSpec((f, d), lambda i: (0, 0)),
                  pl.BlockSpec((tm, d), lambda i: (i, 0)), pl.BlockSpec((tm, d), lambda i: (i, 0)),
                  pl.BlockSpec((1, d), lambda i: (0, 0))],
        out_specs=[pl.BlockSpec((tm, d), lambda i: (i, 0)), pl.BlockSpec((tm, d), lambda i: (i, 0)),
                   pl.BlockSpec((1, 1), lambda i: (0, 0)), pl.BlockSpec((1, d), lambda i: (0, 0))],
        out_shape=[SDS((t, d), F32), SDS((t, d), BF16), SDS((1, 1), F32), SDS((1, d), F32)],
        scratch_shapes=[pltpu.VMEM((1, d), F32)],
        sem=("arbitrary",), blocks=blocks, temps=8 * _nbytes((tm, d), F32),
    )(a, w_down, x1, tgt, g_fin)
    return outs


def _ffn_bwd_da(dxb, w_down, comm=None):
    t, d = dxb.shape
    f = w_down.shape[0]
    tm = _tile(t, 512, SUBLANES_BF16)

    def body(x_ref, w_ref, o_ref):
        o_ref[...] = _dot_tb(x_ref[...], w_ref[...]).astype(BF16)

    blocks = _nbytes((tm, d), BF16) + _nbytes((tm, f), BF16)
    return _pcall(
        body, name="ffn_bwd_da", grid=(t // tm,),
        in_specs=[pl.BlockSpec((tm, d), lambda i: (i, 0)),
                  pl.BlockSpec((f, d), lambda i: (0, 0), pipeline_mode=pl.Buffered(1))],
        out_specs=[pl.BlockSpec((tm, f), lambda i: (i, 0))],
        out_shape=[SDS((t, f), BF16)],
        sem=("parallel",), blocks=blocks, temps=_nbytes((f, d), BF16) + _nbytes((tm, f), F32), comm=comm,
    )(dxb, w_down)


def _ffn_mid_bwd(da, u0, ua, cw, nseq, comm=None):
    _, t, f = u0.shape
    s = t // nseq
    c = _tile(f, 128, LANES)
    r = _tile(s, CHUNK, SUBLANES_BF16)
    n = s // r

    def body(da_ref, u_ref, ua_ref, w_ref, du_ref, gw_ref, gb_ref):
        @pl.when(pl.program_id(1) == 0)
        def _():
            gw_ref[...] = jnp.zeros_like(gw_ref)
            gb_ref[...] = jnp.zeros_like(gb_ref)

        def step(i, carry):
            nxt, sums = carry
            rows = pl.ds(pl.multiple_of((n - 1 - i) * r, r), r)
            ug = ua_ref[0, rows, :].astype(F32)
            uv = ua_ref[1, rows, :].astype(F32)
            sg = _sigmoid(ug)
            dacc = da_ref[rows, :].astype(F32)
            dus = (dacc * uv * sg * (1.0 + ug * (1.0 - sg)), dacc * (ug * sg))
            first, new_sums = [], []
            for p in range(2):
                du = dus[p]
                d1, d2 = _up1_up2(du, nxt[p])
                du_ref[p, rows, :] = _conv3_rows(du, d1, d2, w_ref, p).astype(BF16)
                u = u_ref[p, rows, :].astype(F32)
                sb, s0, s1, s2 = sums[p]
                new_sums.append((sb + _fold8(du), s0 + _fold8(d2 * u), s1 + _fold8(d1 * u), s2 + _fold8(du * u)))
                first.append(du[:HALO])
            return tuple(first), tuple(new_sums)

        zero = jnp.zeros((HALO, c), F32)
        _, sums = lax.fori_loop(0, n, step, ((zero, zero), ((zero,) * 4,) * 2))
        for p in range(2):
            sb, s0, s1, s2 = sums[p]
            gb_ref[p] += jnp.sum(sb, axis=0, keepdims=True)
            gw_ref[p, pl.ds(0, 1), :] += jnp.sum(s0, axis=0, keepdims=True)
            gw_ref[p, pl.ds(1, 1), :] += jnp.sum(s1, axis=0, keepdims=True)
            gw_ref[p, pl.ds(2, 1), :] += jnp.sum(s2, axis=0, keepdims=True)

    blocks = _nbytes((s, c), BF16) + 3 * _nbytes((2, s, c), BF16)
    return _pcall(
        body, name="ffn_mid_bwd", grid=(f // c, nseq),
        in_specs=[pl.BlockSpec((s, c), lambda j, b: (b, j)),
                  pl.BlockSpec((2, s, c), lambda j, b: (0, b, j)),
                  pl.BlockSpec((2, s, c), lambda j, b: (0, b, j)),
                  pl.BlockSpec((2, 3, c), lambda j, b: (0, 0, j))],
        out_specs=[pl.BlockSpec((2, s, c), lambda j, b: (0, b, j)),
                   pl.BlockSpec((2, 3, c), lambda j, b: (0, 0, j)),
                   pl.BlockSpec((2, 1, c), lambda j, b: (0, 0, j))],
        out_shape=[SDS((2, t, f), BF16), SDS((2, 3, f), F32), SDS((2, 1, f), F32)],
        sem=("parallel", "arbitrary"), blocks=blocks, temps=4 * 1024 * 1024, comm=comm,
    )(da, u0, ua, cw)


def _wgrad(a, b, name, *, tr, tn, b_plane_of=None, out_shards=None, comm=None):
    t, m = a.shape
    n_total = b.shape[-1] * (b.shape[0] if b.ndim == 3 else 1)
    tk = _tile(t, WGRAD_TOKENS_WIDE if n_total > tn and m == tr else WGRAD_TOKENS, SUBLANES_BF16)
    nk = t // tk
    once = pl.Buffered(1) if nk == 1 else None

    def body(a_ref, b_ref, o_ref, *acc):
        part = _dot_ta(a_ref[...], b_ref[...])
        if nk == 1:
            o_ref[...] = part.astype(BF16)
        else:
            _acc_over(pl.program_id(2), nk, part, acc[0], o_ref)

    if b.ndim == 3:
        b_spec = pl.BlockSpec((None, tk, tn), lambda r, n, k: (b_plane_of(n)[0], k, b_plane_of(n)[1]))
    else:
        b_spec = pl.BlockSpec((tk, tn), lambda r, n, k: (k, n), pipeline_mode=once if n_total == tn else None)
    if out_shards is None:
        o_spec = pl.BlockSpec((tr, tn), lambda r, n, k: (r, n))
        o_shape = SDS((m, n_total), BF16)
    else:
        nps = n_total // out_shards // tn
        o_spec = pl.BlockSpec((None, tr, tn), lambda r, n, k: (n // nps, r, n % nps))
        o_shape = SDS((out_shards, m, n_total // out_shards), BF16)
    blocks = _nbytes((tk, tr), BF16) + _nbytes((tk, tn), BF16) + _nbytes((tr, tn), BF16)
    return _pcall(
        body, name=name, grid=(m // tr, n_total // tn, nk),
        in_specs=[pl.BlockSpec((tk, tr), lambda r, n, k: (k, r), pipeline_mode=once if m == tr else None), b_spec],
        out_specs=[o_spec], out_shape=[o_shape],
        scratch_shapes=[] if nk == 1 else [pltpu.VMEM((tr, tn), F32)],
        sem=("parallel", "parallel", "arbitrary"), blocks=blocks, temps=2 * _nbytes((tr, tn), F32), comm=comm,
    )(a, b)


def _wgrad3(lhs3, rhs3, comm=None):
    nw, t, d = lhs3.shape
    tk = _tile(t, WGRAD_TOKENS, SUBLANES_BF16)
    nk = t // tk

    def body(a_ref, b_ref, o_ref, *acc):
        part = _dot_ta(a_ref[...], b_ref[...])
        if nk == 1:
            o_ref[...] = part.astype(BF16)
        else:
            _acc_over(pl.program_id(1), nk, part, acc[0], o_ref)

    blocks = 2 * _nbytes((tk, d), BF16) + _nbytes((d, d), BF16)
    return _pcall(
        body, name="wgrad_sq3", grid=(nw, nk),
        in_specs=[pl.BlockSpec((None, tk, d), lambda w, k: (w, k, 0)),
                  pl.BlockSpec((None, tk, d), lambda w, k: (w, k, 0))],
        out_specs=[pl.BlockSpec((None, d, d), lambda w, k: (w, 0, 0))],
        out_shape=[SDS((nw, d, d), BF16)],
        scratch_shapes=[] if nk == 1 else [pltpu.VMEM((d, d), F32)],
        sem=("parallel", "arbitrary"), blocks=blocks, temps=2 * _nbytes((d, d), F32), comm=comm,
    )(lhs3, rhs3)


def _ffn_bwd_dx1(du0, w_up, x1, dx2, g_ffn, n_planes_out, comm=None):
    _, t, f = du0.shape
    d = x1.shape[1]
    nsh, _, ws = w_up.shape
    tm = _tile(t, 256, SUBLANES_BF16)
    spp = f // ws

    def body(du_ref, w_ref, x1_ref, dx2_ref, g_ref, dx1_ref, dxb_ref, gg_ref):
        @pl.when(pl.program_id(0) == 0)
        def _():
            gg_ref[...] = jnp.zeros_like(gg_ref)

        dh = None
        for k in range(nsh):
            part = _dot_tb(du_ref[k // spp, :, (k % spp) * ws:(k % spp + 1) * ws], w_ref[k])
            dh = part if dh is None else dh + part
        xh, inv = _rms_fwd(x1_ref[...])
        gg_ref[...] += jnp.sum(dh * xh, axis=0, keepdims=True)
        dx1 = dx2_ref[...] + _rms_bwd(dh, xh, inv, g_ref[...])
        dx1_ref[...] = dx1
        dxb_ref[...] = dx1.astype(BF16)

    blocks = _nbytes((2, tm, f), BF16) + 3 * _nbytes((tm, d), F32) + _nbytes((tm, d), BF16)
    return _pcall(
        body, name="ffn_bwd_dx1", grid=(t // tm,),
        in_specs=[pl.BlockSpec((2, tm, f), lambda i: (0, i, 0)),
                  pl.BlockSpec((nsh, d, ws), lambda i: (0, 0, 0), pipeline_mode=pl.Buffered(1)),
                  pl.BlockSpec((tm, d), lambda i: (i, 0)),
                  pl.BlockSpec((tm, d), lambda i: (i, 0)),
                  pl.BlockSpec((1, d), lambda i: (0, 0))],
        out_specs=[pl.BlockSpec((tm, d), lambda i: (i, 0)),
                   pl.BlockSpec((None, tm, d), lambda i: (n_planes_out - 1, i, 0)),
                   pl.BlockSpec((1, d), lambda i: (0, 0))],
        out_shape=[SDS((t, d), F32), SDS((n_planes_out, t, d), BF16), SDS((1, d), F32)],
        sem=("arbitrary",), blocks=blocks, temps=_nbytes(w_up.shape, BF16) + 8 * _nbytes((tm, d), F32), comm=comm,
    )(du0, w_up, x1, dx2, g_ffn)


def _mixer_bwd(rhs3, z, ypc, w3, comm=None):
    _, t, d = rhs3.shape
    tm = _tile(t, 256, SUBLANES_BF16)

    def body(dx_ref, zgp, zgc, ypc_ref, w_ref, dyo, dzo, dpq):
        dm = _dot_tb(dx_ref[...], w_ref[2])
        sp = _sigmoid(zgp[...].astype(F32))
        sc = _sigmoid(zgc[...].astype(F32))
        dyp = (dm * sp).astype(BF16)
        dyc = (dm * sc).astype(BF16)
        dzo[0] = (dm * ypc_ref[0].astype(F32) * sp * (1.0 - sp)).astype(BF16)
        dzo[1] = (dm * ypc_ref[1].astype(F32) * sc * (1.0 - sc)).astype(BF16)
        dyo[0] = dyp
        dyo[1] = dyc
        dpq[0] = _dot_tb(dyp, w_ref[0]).astype(BF16)
        dpq[1] = _dot_tb(dyc, w_ref[1]).astype(BF16)

    blocks = _nbytes((tm, d), BF16) * 3 + _nbytes((2, tm, d), BF16) * 4 + _nbytes((3, d, d), BF16)
    return _pcall(
        body, name="mixer_bwd", grid=(t // tm,),
        in_specs=[pl.BlockSpec((None, tm, d), lambda i: (2, i, 0)),
                  pl.BlockSpec((tm, d), lambda i: (i, 4)),
                  pl.BlockSpec((tm, d), lambda i: (i, 5)),
                  pl.BlockSpec((2, tm, d), lambda i: (0, i, 0)),
                  pl.BlockSpec((3, d, d), lambda i: (0, 0, 0))],
        out_specs=[pl.BlockSpec((2, tm, d), lambda i: (0, i, 0)),
                   pl.BlockSpec((2, tm, d), lambda i: (2, i, 0)),
                   pl.BlockSpec((2, tm, d), lambda i: (0, i, 0))],
        out_shape=[SDS(rhs3.shape, BF16), SDS((N_SPLITS, t, d), BF16), SDS((2, t, d), BF16)],
        input_output_aliases={0: 0},
        sem=("parallel",), blocks=blocks, temps=8 * _nbytes((tm, d), F32), comm=comm,
    )(rhs3, z, z, ypc, w3)


def _conv_bwd(dz, dpq, z, conv_w, nseq, comm=None):
    _, t, d = dz.shape
    s = t // nseq
    c = _tile(d, 128, LANES)
    nb = d // c

    def body(dz_in, dq_ref, zb, zc, zv, cw, dzo, gw_ref):
        del dz_in

        @pl.when(pl.program_id(1) == 0)
        def _():
            gw_ref[...] = jnp.zeros_like(gw_ref)

        row = lax.broadcasted_iota(jnp.int32, (s, c), 0)
        b = zb[...].astype(F32)
        cm = zc[...].astype(F32)
        v = zv[...].astype(F32)
        cv = cm * v
        cv1 = _shift_down(cv, 1, row)
        cv2 = _shift_down(cv, 2, row)
        w0, w1, w2 = cw[pl.ds(0, 1), :], cw[pl.ds(1, 1), :], cw[pl.ds(2, 1), :]
        cc = w2 * cv + w1 * cv1 + w0 * cv2
        dq = dq_ref[...].astype(F32)
        dzo[0] = (dq * cc).astype(BF16)
        dcc = dq * b
        gw_ref[pl.ds(0, 1), :] += jnp.sum(dcc * cv2, axis=0, keepdims=True)
        gw_ref[pl.ds(1, 1), :] += jnp.sum(dcc * cv1, axis=0, keepdims=True)
        gw_ref[pl.ds(2, 1), :] += jnp.sum(dcc * cv, axis=0, keepdims=True)
        dcv = w2 * dcc + w1 * _shift_up(dcc, 1, row) + w0 * _shift_up(dcc, 2, row)
        dzo[1] = (dcv * v).astype(BF16)
        dzo[2] = (dcv * cm).astype(BF16)

    blocks = 4 * _nbytes((s, c), BF16) + _nbytes((3, s, c), BF16)
    return _pcall(
        body, name="conv_bwd", grid=(nb, nseq),
        in_specs=[ANY,
                  pl.BlockSpec((None, s, c), lambda j, b: (1, b, j)),
                  pl.BlockSpec((s, c), lambda j, b: (b, nb + j)),
                  pl.BlockSpec((s, c), lambda j, b: (b, 2 * nb + j)),
                  pl.BlockSpec((s, c), lambda j, b: (b, 3 * nb + j)),
                  pl.BlockSpec((3, c), lambda j, b: (0, j))],
        out_specs=[pl.BlockSpec((3, s, c), lambda j, b: (0, b, j)),
                   pl.BlockSpec((3, c), lambda j, b: (0, j))],
        out_shape=[SDS(dz.shape, BF16), SDS((3, d), F32)],
        input_output_aliases={0: 0},
        sem=("parallel", "arbitrary"), blocks=blocks, temps=16 * _nbytes((s, c), F32), comm=comm,
    )(dz, dpq, z, z, z, conv_w)


def _pool_bwd_call(dz, dpq, z, pool_w, pool_scale, nseq, comm=None):
    _, t, d = dz.shape
    s = t // nseq
    c = d // N_GROUPS

    def body(dz_in, dp_ref, zp, pw, ps, dzo, gpw_ref, gps_ref):
        del dz_in
        j = pl.program_id(0)

        @pl.when(pl.program_id(1) == 0)
        def _():
            gpw_ref[...] = jnp.zeros_like(gpw_ref)
            gps_ref[...] = jnp.zeros_like(gps_ref)

        row = lax.broadcasted_iota(jnp.int32, (s, c), 0)
        for gi, win in enumerate(POOL_WINDOWS):
            @pl.when(j == gi)
            def _(win=win):
                pb = _pool_fwd(zp[...].astype(F32), win, row).astype(BF16)
                plin = _dot(pb, pw[...])
                dps = dp_ref[...].astype(F32)
                gps_ref[...] += jnp.sum(dps * plin, axis=0, keepdims=True)
                dplb = (dps * ps[...]).astype(BF16)
                gpw_ref[...] += _dot_ta(pb, dplb)
                dzo[...] = _pool_bwd(_dot_tb(dplb, pw[...]), win, row).astype(BF16)

    blocks = 3 * _nbytes((s, c), BF16) + _nbytes((c, c), BF16) + _nbytes((c, c), F32)
    return _pcall(
        body, name="pool_bwd", grid=(N_GROUPS, nseq),
        in_specs=[ANY,
                  pl.BlockSpec((None, s, c), lambda j, b: (0, b, j)),
                  pl.BlockSpec((s, c), lambda j, b: (b, j)),
                  pl.BlockSpec((None, c, c), lambda j, b: (j, 0, 0)),
                  pl.BlockSpec((1, c), lambda j, b: (0, j))],
        out_specs=[pl.BlockSpec((None, s, c), lambda j, b: (3, b, j)),
                   pl.BlockSpec((None, c, c), lambda j, b: (j, 0, 0)),
                   pl.BlockSpec((1, c), lambda j, b: (0, j))],
        out_shape=[SDS(dz.shape, BF16), SDS((N_GROUPS, c, c), F32), SDS((1, d), F32)],
        input_output_aliases={0: 0},
        sem=("parallel", "arbitrary"), blocks=blocks, temps=10 * _nbytes((s, c), F32), comm=comm,
    )(dz, dpq, z, pool_w, pool_scale)


def _dz_plane(zb):
    return jnp.where(zb < 4, (zb + 3) % 4, zb)


def _wgrad_in(h1, dz, nsh, comm=None):
    t, d = h1.shape
    ws = N_SPLITS * d // nsh
    kb = _tile(math.gcd(d, ws), 512, LANES)
    npl = d // kb
    nps = ws // kb
    tk = _tile(t, WGRAD_TOKENS_WIDE, SUBLANES_BF16)
    nk = t // tk

    def body(a_ref, b_ref, o_ref, *acc):
        part = _dot_ta(a_ref[...], b_ref[...])
        if nk == 1:
            o_ref[...] = part.astype(BF16)
        else:
            _acc_over(pl.program_id(1), nk, part, acc[0], o_ref)

    blocks = _nbytes((tk, d), BF16) + _nbytes((tk, kb), BF16) + _nbytes((d, kb), BF16)
    return _pcall(
        body, name="wgrad_in", grid=(N_SPLITS * npl, nk),
        in_specs=[pl.BlockSpec((tk, d), lambda cb, k: (k, 0), pipeline_mode=pl.Buffered(1) if nk == 1 else None),
                  pl.BlockSpec((None, tk, kb), lambda cb, k: (_dz_plane(cb // npl), k, cb % npl))],
        out_specs=[pl.BlockSpec((None, d, kb), lambda cb, k: (cb // nps, 0, cb % nps))],
        out_shape=[SDS((nsh, d, ws), BF16)],
        scratch_shapes=[] if nk == 1 else [pltpu.VMEM((d, kb), F32)],
        sem=("parallel", "arbitrary"), blocks=blocks, temps=2 * _nbytes((d, kb), F32), comm=comm,
    )(h1, dz)


def _mixer_bwd_dx(dz, w_in, x, dx1, g_mix, comm=None):
    npln, t, d = dz.shape
    nsh, _, ws = w_in.shape
    tm = _tile(t, 256, SUBLANES_BF16)
    kb = _tile(math.gcd(d, ws), 512, LANES)
    npl = d // kb
    nps = ws // kb

    def body(dz_ref, w_ref, x_ref, dx1_ref, g_ref, dx_ref, gg_ref):
        @pl.when(pl.program_id(0) == 0)
        def _():
            gg_ref[...] = jnp.zeros_like(gg_ref)

        dh = None
        for cb in range(npln * npl):
            zb = cb // npl
            plane = (zb + 3) % 4 if zb < 4 else zb
            part = _dot_tb(dz_ref[plane, :, (cb % npl) * kb:(cb % npl + 1) * kb],
                           w_ref[cb // nps, :, (cb % nps) * kb:(cb % nps + 1) * kb])
            dh = part if dh is None else dh + part
        xh, inv = _rms_fwd(x_ref[...])
        gg_ref[...] += jnp.sum(dh * xh, axis=0, keepdims=True)
        dx_ref[...] = dx1_ref[...] + _rms_bwd(dh, xh, inv, g_ref[...])

    blocks = _nbytes((npln, tm, d), BF16) + 3 * _nbytes((tm, d), F32)
    return _pcall(
        body, name="mixer_bwd_dx", grid=(t // tm,),
        in_specs=[pl.BlockSpec((npln, tm, d), lambda i: (0, i, 0)),
                  pl.BlockSpec((nsh, d, ws), lambda i: (0, 0, 0), pipeline_mode=pl.Buffered(1)),
                  pl.BlockSpec((tm, d), lambda i: (i, 0)),
                  pl.BlockSpec((tm, d), lambda i: (i, 0)),
                  pl.BlockSpec((1, d), lambda i: (0, 0))],
        out_specs=[pl.BlockSpec((tm, d), lambda i: (i, 0)),
                   pl.BlockSpec((1, d), lambda i: (0, 0))],
        out_shape=[SDS((t, d), F32), SDS((1, d), F32)],
        sem=("arbitrary",), blocks=blocks, temps=_nbytes(w_in.shape, BF16) + 8 * _nbytes((tm, d), F32), comm=comm,
    )(dz, w_in, x, dx1, g_mix)


N_BIG = 5
SHARD_MAJOR = (0, 2)
ROWS_DIM1 = (1, 4)


def _ds(start, size, align):
    if isinstance(start, int):
        return pl.ds(start, size)
    return pl.ds(pl.multiple_of(start, align), size)


def _piece(a, ref, k, h):
    if a in SHARD_MAJOR:
        r = ref.shape[1] // 2
        return ref.at[k, _ds(h * r, r, SUBLANES_BF16), :]
    if a in ROWS_DIM1:
        r = ref.shape[1] // 8
        return ref.at[:, _ds((2 * k + h) * r, r, SUBLANES_BF16), :]
    r = ref.shape[0] // 8
    return ref.at[_ds((2 * k + h) * r, r, SUBLANES_BF16), :]


def _half(a, ref, h):
    if a in ROWS_DIM1:
        r = ref.shape[1] // 2
        return ref.at[:, _ds(h * r, r, SUBLANES_BF16), :]
    r = ref.shape[0] // 2
    return ref.at[_ds(h * r, r, SUBLANES_BF16), :]


def _piece_shape(a, full_shape):
    if a in SHARD_MAJOR:
        return (full_shape[1] // 2, full_shape[2])
    if a in ROWS_DIM1:
        return (full_shape[0], full_shape[1] // 8, full_shape[2])
    return (full_shape[0] // 8, full_shape[1])


def _shard_shape(a, full_shape):
    if a in SHARD_MAJOR:
        return (full_shape[1], full_shape[2])
    if a in ROWS_DIM1:
        return (full_shape[0], full_shape[1] // 4, full_shape[2])
    return (full_shape[0] // 4, full_shape[1])


def _rows_axis(a):
    return 1 if a in ROWS_DIM1 else 0


def _piece_block(a, full_shape):
    ps = _piece_shape(a, full_shape)
    if a in SHARD_MAJOR:
        return (None,) + ps, lambda k, c: (k, c, 0)
    if a in ROWS_DIM1:
        return ps, lambda k, c: (0, 2 * k + c, 0)
    return ps, lambda k, c: (2 * k + c, 0)


def _coords():
    return lax.axis_index("x"), lax.axis_index("y"), lax.axis_index("c")


def _peer_chips(x, y):
    return [(1 - x, y), (x, 1 - y), (1 - x, 1 - y)]


def _remote(src, dst, ssem, rsem, dev):
    return pltpu.make_async_remote_copy(src_ref=src, dst_ref=dst, send_sem=ssem, recv_sem=rsem,
                                        device_id=dev, device_id_type=MESH)


def _dma_sems(*counts):
    return [pltpu.SemaphoreType.DMA((n,)) for n in counts]


def _symmetric(ins, out_shapes, sems, copies, peers, aliases=None):
    def start(cins, couts, csems):
        for cp in copies(cins, couts, csems):
            cp.start()

    def finish(cins, couts, csems):
        for cp in copies(cins, couts, csems):
            cp.wait()

    return _Comm(ins, out_shapes, sems, start, finish, peers, aliases)


def _rows_part(a, ref, part):
    if part is None:
        return ref
    p, q, n = part
    ax = _rows_axis(a)
    r = ref.shape[ax] // n
    return ref.at[tuple(pl.ds(p * r, (q - p) * r) if d == ax else slice(None) for d in range(len(ref.shape)))]


def _merge(comms):
    ins, outs, sems, aliases, spans = [], [], [], {}, []
    for cm in comms:
        spans.append((len(ins), len(outs), len(sems)))
        for i, o in cm.aliases.items():
            aliases[len(ins) + i] = len(outs) + o
        ins += cm.ins
        outs += cm.out_shapes
        sems += cm.sems

    def each(fn_name):
        def run(cins, couts, csems):
            for cm, (i0, o0, s0) in zip(comms, spans):
                fn = getattr(cm, fn_name)
                if fn is not None:
                    fn(cins[i0:i0 + len(cm.ins)], couts[o0:o0 + len(cm.out_shapes)], csems[s0:s0 + len(cm.sems)])
        return run

    return _Comm(ins, outs, sems, each("start"), each("finish"), frozenset().union(*[cm.peers for cm in comms]),
                 aliases, mid=each("mid") if any(cm.mid is not None for cm in comms) else None)


def _gather_comm(arrs, locs, full_shapes, part=None, into=None):
    n = len(arrs)

    def own(cins, couts, csems):
        x, y, c = _coords()
        j = 2 * x + y
        return [_remote(_rows_part(a, _half(a, cins[q], h), part), _rows_part(a, _piece(a, couts[q], j, h), part),
                        csems[0].at[2 * q + h], csems[1].at[2 * q + h], (x, y, 1 - c))
                for q, a in enumerate(arrs) for h in range(2)]

    def sends(cins, couts, csems):
        x, y, c = _coords()
        j = 2 * x + y
        return [_remote(_rows_part(a, _half(a, cins[q], c), part), _rows_part(a, _piece(a, couts[q], j, c), part),
                        csems[2].at[3 * q + i], csems[3].at[3 * q + i], (px, py, c))
                for q, a in enumerate(arrs) for i, (px, py) in enumerate(_peer_chips(x, y))]

    def forwards(couts, csems, half_of):
        x, y, c = _coords()
        out = []
        for q, a in enumerate(arrs):
            for i, (px, py) in enumerate(_peer_chips(x, y)):
                landed = _rows_part(a, _piece(a, couts[q], 2 * px + py, half_of(c)), part)
                out.append(_remote(landed, landed, csems[4].at[3 * q + i], csems[5].at[3 * q + i], (x, y, 1 - c)))
        return out

    def start(cins, couts, csems):
        for cp in sends(cins, couts, csems) + own(cins, couts, csems):
            cp.start()

    def finish(cins, couts, csems):
        fw = forwards(couts, csems, lambda c: c)
        for cp, f in zip(sends(cins, couts, csems), fw):
            cp.wait_recv()
            f.start()
        for f in forwards(couts, csems, lambda c: 1 - c):
            f.wait_recv()
        for cp in sends(cins, couts, csems) + fw:
            cp.wait_send()
        for cp in own(cins, couts, csems):
            cp.wait()

    ins = [locs[a] for a in arrs] + ([into[a] for a in arrs] if into else [])
    return _Comm(ins, [SDS(full_shapes[a], BF16) for a in arrs],
                 _dma_sems(2 * n, 2 * n, 3 * n, 3 * n, 3 * n, 3 * n), start, finish, CHIPS + (SIBLING,),
                 aliases={n + q: q for q in range(n)} if into else None)


def _ring_gather_comm(arrs, locs, full_shapes):
    n = len(arrs)

    def own(cins, couts, csems):
        x, y, c = _coords()
        j = 2 * x + y
        return [_remote(_half(a, cins[q], h), _piece(a, couts[q], j, h), csems[0].at[2 * q + h],
                        csems[1].at[2 * q + h], (x, y, 1 - c)) for q, a in enumerate(arrs) for h in range(2)]

    def sends(cins, couts, csems):
        x, y, c = _coords()
        j = 2 * x + y
        return [_remote(_half(a, cins[q], c), _piece(a, couts[q], j, c), csems[2].at[2 * q + i],
                        csems[3].at[2 * q + i], (px, py, c))
                for q, a in enumerate(arrs) for i, (px, py) in enumerate(_peer_chips(x, y)[:2])]

    def relays(couts, csems):
        x, y, c = _coords()
        peers = _peer_chips(x, y)
        out = []
        for q, a in enumerate(arrs):
            for r, (src_p, dst_p) in enumerate(((0, 1), (1, 0))):
                sx, sy = peers[src_p]
                rows = _rows_part(a, _piece(a, couts[q], 2 * sx + sy, c), (r, r + 1, 2))
                out.append(_remote(rows, rows, csems[6].at[2 * q + r], csems[7].at[2 * q + r], (*peers[dst_p], c)))
        return out

    def forwards(couts, csems, half_of, which):
        x, y, c = _coords()
        out = []
        for q, a in enumerate(arrs):
            for i in which:
                px, py = _peer_chips(x, y)[i]
                landed = _piece(a, couts[q], 2 * px + py, half_of(c))
                out.append(_remote(landed, landed, csems[4].at[3 * q + i], csems[5].at[3 * q + i], (x, y, 1 - c)))
        return out

    def start(cins, couts, csems):
        for cp in sends(cins, couts, csems) + own(cins, couts, csems):
            cp.start()

    def mid(cins, couts, csems):
        for cp in sends(cins, couts, csems):
            cp.wait_recv()
        for cp in relays(couts, csems) + forwards(couts, csems, lambda c: c, (0, 1)):
            cp.start()

    def finish(cins, couts, csems):
        for cp in relays(couts, csems):
            cp.wait_recv()
        fw_diag = forwards(couts, csems, lambda c: c, (2,))
        for f in fw_diag:
            f.start()
        for f in forwards(couts, csems, lambda c: 1 - c, (0, 1, 2)):
            f.wait_recv()
        for cp in (sends(cins, couts, csems) + relays(couts, csems)
                   + forwards(couts, csems, lambda c: c, (0, 1)) + fw_diag):
            cp.wait_send()
        for cp in own(cins, couts, csems):
            cp.wait()

    return _Comm([locs[a] for a in arrs], [SDS(full_shapes[a], BF16) for a in arrs],
                 _dma_sems(2 * n, 2 * n, 2 * n, 2 * n, 3 * n, 3 * n, 2 * n, 2 * n), start, finish,
                 CHIPS + (SIBLING,), mid=mid)


def _halves_comm(arrs, gbs):
    n = len(arrs)

    def copies(cins, couts, csems):
        x, y, c = _coords()
        return [_remote(_piece(a, cins[q], k, 1 - c), couts[q].at[k], csems[0].at[4 * q + k], csems[1].at[4 * q + k],
                        (x, y, 1 - c)) for q, a in enumerate(arrs) for k in range(4)]

    return _symmetric([gbs[a] for a in arrs], [SDS((4,) + _piece_shape(a, gbs[a].shape), BF16) for a in arrs],
                      _dma_sems(4 * n, 4 * n), copies, [SIBLING])


def _chips_comm(arrs, ps, part=None, into=None):
    n = len(arrs)

    def copies(cins, couts, csems):
        x, y, c = _coords()
        return [_remote(_rows_part(a, cins[q].at[2 * px + py], part), _rows_part(a, couts[q].at[i], part),
                        csems[0].at[3 * q + i], csems[1].at[3 * q + i], (px, py, c))
                for q, a in enumerate(arrs) for i, (px, py) in enumerate(_peer_chips(x, y))]

    ins = [ps[a] for a in arrs] + ([into[a] for a in arrs] if into else [])
    return _symmetric(ins, [SDS((3,) + ps[a].shape[1:], BF16) for a in arrs], _dma_sems(3 * n, 3 * n), copies, CHIPS,
                      aliases={n + q: q for q in range(n)} if into else None)


def _result_comm(arrs, gs):
    n = len(arrs)

    def copies(cins, couts, csems):
        x, y, c = _coords()
        return [_remote(_half(a, cins[q], c), _half(a, couts[q], c), csems[0].at[q], csems[1].at[q], (x, y, 1 - c))
                for q, a in enumerate(arrs)]

    return _symmetric([gs[a] for a in arrs], [SDS(gs[a].shape, F32) for a in arrs], _dma_sems(n, n), copies,
                      [SIBLING], aliases={q: q for q in range(n)})


def _add_halves(arrs, gbs, lands, c_arr, name):
    n = len(arrs)

    def body(c_ref, *refs):
        del c_ref
        for q in range(n):
            refs[2 * n + q][...] = (refs[q][...].astype(F32) + refs[n + q][...].astype(F32)).astype(BF16)

    g_specs, l_specs, o_specs, blocks = [], [], [], 0
    for a in arrs:
        bs, imap = _piece_block(a, gbs[a].shape)
        ps = _piece_shape(a, gbs[a].shape)
        g_specs.append(pl.BlockSpec(bs, lambda k, c_ref, imap=imap: imap(k, c_ref[0])))
        nd = len(ps)
        l_specs.append(pl.BlockSpec((None,) + ps, lambda k, c_ref, nd=nd: (k,) + (0,) * nd))
        o_specs.append(pl.BlockSpec((None,) + ps, lambda k, c_ref, nd=nd: (k,) + (0,) * nd))
        blocks += 3 * _nbytes(ps, BF16)
    return list(pl.pallas_call(
        body, name=name,
        grid_spec=pltpu.PrefetchScalarGridSpec(
            num_scalar_prefetch=1, grid=(4,), in_specs=g_specs + l_specs, out_specs=o_specs),
        out_shape=[SDS((4,) + _piece_shape(a, gbs[a].shape), BF16) for a in arrs],
        compiler_params=_params(("parallel",), blocks, blocks),
    )(c_arr, *[gbs[a] for a in arrs], *lands))


def _sum_chips(a, p, land, shard_shape, jc_arr, name):
    ps = land.shape[1:]
    ax = _rows_axis(a)
    rows = ps[ax]
    nsub = 2 if rows % (2 * SUBLANES_BF16) == 0 else 1
    bs = tuple(r // nsub if q == ax else r for q, r in enumerate(ps))
    nd = len(ps)

    def at_rows(v):
        return tuple(v if q == ax else 0 for q in range(nd))

    def body(jc_ref, p_ref, l_ref, o_ref):
        del jc_ref
        acc = p_ref[...].astype(F32) + l_ref[0].astype(F32)
        acc = acc + l_ref[1].astype(F32)
        o_ref[...] = acc + l_ref[2].astype(F32)

    blocks = 4 * _nbytes(bs, BF16) + _nbytes(bs, F32)
    return pl.pallas_call(
        body, name=name,
        grid_spec=pltpu.PrefetchScalarGridSpec(
            num_scalar_prefetch=1, grid=(nsub,),
            in_specs=[pl.BlockSpec((None,) + bs, lambda s, jc: (jc[0],) + at_rows(s)),
                      pl.BlockSpec((3,) + bs, lambda s, jc: (0,) + at_rows(s))],
            out_specs=pl.BlockSpec(bs, lambda s, jc: at_rows(jc[1] * nsub + s))),
        out_shape=SDS(shard_shape, F32),
        compiler_params=_params(("parallel",), blocks, 2 * _nbytes(bs, F32)),
    )(jc_arr, p, land)


def _small_comm(v):
    rows = v.shape[0]

    def copies(cins, couts, csems):
        x, y, c = _coords()
        me = 4 * x + 2 * y + c
        out = [pltpu.make_async_copy(cins[0], couts[0].at[me], csems[0].at[0])]
        for dlt in range(1, 8):
            px = 1 - x if (dlt >> 2) & 1 else x
            py = 1 - y if (dlt >> 1) & 1 else y
            pc = 1 - c if dlt & 1 else c
            out.append(_remote(cins[0], couts[0].at[me], csems[1].at[dlt - 1], csems[2].at[dlt - 1], (px, py, pc)))
        return out

    return _symmetric([v], [SDS((8, rows, LANES), F32)], _dma_sems(1, 7, 7), copies, EVERYONE)


def _sum8(slots, name):
    def body(s_ref, o_ref):
        acc = s_ref[0]
        for i in range(1, 8):
            acc = acc + s_ref[i]
        o_ref[...] = acc

    return pl.pallas_call(
        body, name=name,
        in_specs=[pl.BlockSpec(memory_space=pltpu.VMEM)], out_specs=pl.BlockSpec(memory_space=pltpu.VMEM),
        out_shape=SDS(slots.shape[1:], F32),
    )(slots)


def _adamw(w, g, m, v, name, g_plane=None, comm=None):
    rows, cols = w.shape
    tr = _tile(rows, max(SUBLANES_F32, (256 * 1024 // cols) // SUBLANES_F32 * SUBLANES_F32), SUBLANES_F32)

    def body(w_ref, g_ref, m_ref, v_ref, go_ref, d_ref, mo_ref, vo_ref):
        gr = g_ref[...]
        mn = ADAM_B1 * m_ref[...] + (1.0 - ADAM_B1) * gr
        vn = ADAM_B2 * v_ref[...] + (1.0 - ADAM_B2) * (gr * gr)
        m_hat = mn / (1.0 - ADAM_B1 ** ADAM_STEP)
        v_hat = vn / (1.0 - ADAM_B2 ** ADAM_STEP)
        d_ref[...] = -ADAM_LR * (m_hat / (jnp.sqrt(v_hat) + ADAM_EPS) + ADAM_WD * w_ref[...])
        go_ref[...] = gr
        mo_ref[...] = mn
        vo_ref[...] = vn

    spec = pl.BlockSpec((tr, cols), lambda i: (i, 0))
    g_spec = spec if g_plane is None else pl.BlockSpec((None, tr, cols), lambda i: (g_plane, i, 0))
    return _pcall(
        body, name=name, grid=(rows // tr,),
        in_specs=[spec, g_spec, spec, spec], out_specs=[spec, spec, spec, spec],
        out_shape=[SDS((rows, cols), F32)] * 4,
        sem=("parallel",), blocks=8 * _nbytes((tr, cols), F32), temps=4 * _nbytes((tr, cols), F32), comm=comm,
    )(w, g, m, v)


def _pack(parts):
    rows = []
    for p in parts:
        r = p.reshape(-1, LANES)
        pad = (-r.shape[0]) % SUBLANES_F32
        if pad:
            r = jnp.pad(r, ((0, pad), (0, 0)))
        rows.append(r)
    return jnp.concatenate(rows, axis=0)


def _unpack(packed, shapes):
    out, at = [], 0
    for s in shapes:
        n = 1
        for q in s:
            n *= q
        r = n // LANES
        out.append(packed[at:at + r].reshape(s))
        at += r + (-r) % SUBLANES_F32
    return out


def kernel(x, norm_mix, w_in, pool_w, pool_scale, w_pool_proj, conv_w, w_conv_out, w_o, norm_ffn, w_up, ffn_conv_w, ffn_conv_b, w_down, norm_final, loss_target, m_norm_mix, m_w_in, m_pool_w, m_pool_scale, m_w_pool_proj, m_conv_w, m_w_conv_out, m_w_o, m_norm_ffn, m_w_up, m_ffn_conv_w, m_ffn_conv_b, m_w_down, m_norm_final, v_norm_mix, v_w_in, v_pool_w, v_pool_scale, v_w_pool_proj, v_conv_w, v_w_conv_out, v_w_o, v_norm_ffn, v_w_up, v_ffn_conv_w, v_ffn_conv_b, v_w_down, v_norm_final):
    nseq, seq, d = x.shape
    t = nseq * seq
    f = w_down.shape[1] * 4
    c = d // N_GROUPS
    xy = lax.axis_index("x") * 2 + lax.axis_index("y")
    c_arr = lax.axis_index("c").astype(jnp.int32).reshape(1)
    jc_arr = jnp.stack([xy, lax.axis_index("c")]).astype(jnp.int32)
    nsh = 4
    zero = jnp.zeros((), jnp.int32)

    locs = [w_in[0].astype(BF16),
            jnp.stack([w_pool_proj[0], w_conv_out[0], w_o[0]]).astype(BF16),
            w_up[0].astype(BF16), w_down[0].astype(BF16), pool_w[0].astype(BF16)]
    full_shapes = [(nsh, d, N_SPLITS * d // nsh), (3, d, d), (nsh, d, 2 * f // nsh), (f, d), (N_GROUPS, c, c)]

    cw_pad = lax.dynamic_update_slice(jnp.zeros((3, d), F32), conv_w[0], (zero, xy * (d // 4)))
    fw_pad = lax.dynamic_update_slice(jnp.zeros((3, 2 * f), F32), ffn_conv_w[0], (zero, xy * (f // 2)))
    small_w = _pack([cw_pad, fw_pad]) * 0.5

    x2d = x.reshape(t, d)
    tgt = loss_target.reshape(t, d)
    ax, ay = lax.axis_index("x"), lax.axis_index("y")
    order = jnp.stack([xy, 2 * (1 - ax) + ay, 2 * ax + 1 - ay, 2 * (1 - ax) + 1 - ay]).astype(jnp.int32)
    (z, h1, w_in_f), (pool_w_f, w3_f, slots_w) = _fwd_in(
        x2d, norm_mix, locs[0], order,
        _merge([_gather_comm([4], locs, full_shapes), _gather_comm([1], locs, full_shapes, part=(0, 1, 2)),
                _small_comm(small_w)]))
    conv_w_f, ffn_cw_f = _unpack(_sum8(slots_w, "sum8_weights"), [(3, d), (3, 2 * f)])
    ffn_cw_p = ffn_cw_f.reshape(3, 2, f).transpose(1, 0, 2)
    ffn_cb_p = ffn_conv_b.reshape(2, 1, f)
    (lhs3,), (w3_f,) = _mixer_mid_fwd(z, pool_w_f, pool_scale, conv_w_f, nseq,
                                      _gather_comm([1], locs, full_shapes, part=(1, 2, 2), into={1: w3_f}))
    (lhs3, ypc, x1, h2), (w_up_f,) = _mixer_out(lhs3, z, x2d, w3_f, norm_ffn,
                                                _ring_gather_comm([2], locs, full_shapes))
    (u0,), (w_down_f,) = _ffn_up(h2, w_up_f, f, _gather_comm([3], locs, full_shapes))
    act, ua = _ffn_mid_fwd(u0, ffn_cw_p, ffn_cb_p, nseq)
    dx2, dx2b, loss11, g_norm_final = _ffn_down_loss(act, w_down_f, x1, tgt, norm_final.reshape(1, d))

    gbs, lands, ps, lands2, rs, res = {}, {}, {}, {}, {}, {}

    def upd(w, g, m, v, name, g_plane=None, comm=None):
        shape = w.shape
        rows = 1
        for q in shape[:-1]:
            rows *= q
        g2 = g if g_plane is not None else g.reshape(rows, shape[-1])
        outs, landed = _adamw(w.reshape(rows, shape[-1]), g2, m.reshape(rows, shape[-1]),
                              v.reshape(rows, shape[-1]), name, g_plane, comm)
        return [o.reshape(shape) for o in outs], landed
    tn_up = _tile(2 * f // nsh, 1408, LANES)
    npp = f // tn_up

    def add(arrs, name):
        for a, p in zip(arrs, _add_halves(arrs, gbs, [lands[a] for a in arrs], c_arr, name)):
            ps[a] = p

    def summed(a):
        rs[a] = _sum_chips(a, ps[a], lands2[a], _shard_shape(a, full_shapes[a]), jc_arr, "sum_chips_%d" % a)

    (gbs[3],), _ = _wgrad(act, dx2b, "wgrad_down", tr=tn_up, tn=d)
    (da,), (lands[3],) = _ffn_bwd_da(dx2b, w_down_f, _halves_comm([3], gbs))
    add([3], "add_halves_down")
    (du0, g_ffn_cw_p, g_ffn_cb_p), (lands2[3],) = _ffn_mid_bwd(da, u0, ua, ffn_cw_p, nseq, _chips_comm([3], ps))
    summed(3)
    (gbs[2],), (rs[3],) = _wgrad(h2, du0, "wgrad_up", tr=d, tn=tn_up, b_plane_of=lambda n: (n // npp, n % npp),
                                 out_shards=nsh, comm=_result_comm([3], rs))
    (dx1, rhs3, g_norm_ffn), (lands[2],) = _ffn_bwd_dx1(du0, w_up_f, x1, dx2, norm_ffn, 3, _halves_comm([2], gbs))
    add([2], "add_halves_up")
    (rhs3, dz, dpq), (lands2[2],) = _mixer_bwd(rhs3, z, ypc, w3_f, _chips_comm([2], ps, part=(0, 1, 2)))
    (gbs[1],), (lands2[2],) = _wgrad3(lhs3, rhs3, _chips_comm([2], ps, part=(1, 2, 2), into=lands2))
    summed(2)
    (dz, g_conv_w), (lands[1], rs[2]) = _conv_bwd(dz, dpq, z, conv_w_f, nseq,
                                                  _merge([_halves_comm([1], gbs), _result_comm([2], rs)]))
    add([1], "add_halves_sq3")
    (dz, g_pool_w, g_pool_scale), _ = _pool_bwd_call(dz, dpq, z, pool_w_f, pool_scale, nseq)
    gbs[4] = g_pool_w.astype(BF16)
    (gbs[0],), (lands2[1],) = _wgrad_in(h1, dz, nsh, _chips_comm([1], ps))
    summed(1)
    res["w_up"], (lands[0], lands[4]) = upd(w_up, rs[2], m_w_up, v_w_up, "adamw_w_up",
                                            comm=_halves_comm([0, 4], gbs))
    add([0, 4], "add_halves_in")
    g_ffn_cw = g_ffn_cw_p.transpose(1, 0, 2).reshape(3, 2 * f)
    small_a = _pack([g_pool_scale, g_norm_ffn, g_ffn_cb_p.reshape(1, 2 * f), g_norm_final.reshape(d), g_conv_w,
                     g_ffn_cw, jnp.pad(loss11, ((0, SUBLANES_F32 - 1), (0, LANES - 1)))])
    (grad_x, g_norm_mix), (lands2[0], lands2[4], rs[1], slots_a) = _mixer_bwd_dx(
        dz, w_in_f, x2d, dx1, norm_mix,
        _merge([_chips_comm([0, 4], ps), _result_comm([1], rs), _small_comm(small_a)]))
    summed(0)
    summed(4)
    res["w_down"], (rs[0], rs[4], slots_b) = upd(
        w_down, rs[3], m_w_down, v_w_down, "adamw_w_down",
        comm=_merge([_result_comm([0, 4], rs), _small_comm(_pack([g_norm_mix]))]))
    shapes_a = [(1, d), (1, d), (1, 2 * f), (d,), (3, d), (3, 2 * f), (SUBLANES_F32, LANES)]
    gs_pool_scale, gs_norm_ffn, gs_ffn_cb, gs_norm_final, gs_conv_w, gs_ffn_cw, loss_blk = _unpack(
        _sum8(slots_a, "sum8_grads"), shapes_a)
    (gs_norm_mix,) = _unpack(_sum8(slots_b, "sum8_norm_mix"), [(1, d)])
    gs_conv_w = lax.dynamic_slice(gs_conv_w, (zero, xy * (d // 4)), (3, d // 4))
    gs_ffn_cw = lax.dynamic_slice(gs_ffn_cw, (zero, xy * (f // 2)), (3, f // 2))

    res["w_in"], _ = upd(w_in, rs[0], m_w_in, v_w_in, "adamw_w_in")
    res["pool_w"], _ = upd(pool_w, rs[4], m_pool_w, v_pool_w, "adamw_pool_w")
    res["w_pool_proj"], _ = upd(w_pool_proj, rs[1], m_w_pool_proj, v_w_pool_proj, "adamw_w_pool_proj", 0)
    res["w_conv_out"], _ = upd(w_conv_out, rs[1], m_w_conv_out, v_w_conv_out, "adamw_w_conv_out", 1)
    res["w_o"], _ = upd(w_o, rs[1], m_w_o, v_w_o, "adamw_w_o", 2)

    small_names = ["norm_mix", "pool_scale", "norm_ffn", "ffn_conv_b", "norm_final", "conv_w", "ffn_conv_w"]
    small_ws = [norm_mix, pool_scale, norm_ffn, ffn_conv_b, norm_final, conv_w, ffn_conv_w]
    small_ms = [m_norm_mix, m_pool_scale, m_norm_ffn, m_ffn_conv_b, m_norm_final, m_conv_w, m_ffn_conv_w]
    small_vs = [v_norm_mix, v_pool_scale, v_norm_ffn, v_ffn_conv_b, v_norm_final, v_conv_w, v_ffn_conv_w]
    small_gs = [gs_norm_mix, gs_pool_scale, gs_norm_ffn, gs_ffn_cb, gs_norm_final, gs_conv_w, gs_ffn_cw]
    (_, sd, sm, sv), _ = _adamw(_pack(small_ws), _pack(small_gs), _pack(small_ms), _pack(small_vs), "adamw_small")
    shapes = [w.shape for w in small_ws]
    sd, sm, sv = _unpack(sd, shapes), _unpack(sm, shapes), _unpack(sv, shapes)
    for i, nm in enumerate(small_names):
        res[nm] = [small_gs[i].reshape(shapes[i]), sd[i], sm[i], sv[i]]

    order = ["norm_mix", "w_in", "pool_w", "pool_scale", "w_pool_proj", "conv_w", "w_conv_out", "w_o", "norm_ffn",
             "w_up", "ffn_conv_w", "ffn_conv_b", "w_down", "norm_final"]
    return (loss_blk[0, 0], grad_x.reshape(x.shape), *[res[n][0] for n in order], *[res[n][1] for n in order],
            *[res[n][2] for n in order], *[res[n][3] for n in order])
```

```python
import math

import jax
import jax.numpy as jnp
from jax import lax
from jax.experimental import pallas as pl
from jax.experimental.pallas import tpu as pltpu

F32 = jnp.float32
BF16 = jnp.bfloat16
SDS = jax.ShapeDtypeStruct
MESH = pl.DeviceIdType.MESH

RMS_EPS = 1e-6
POOL_WINDOWS = (2, 4, 8, 16)
N_GROUPS = len(POOL_WINDOWS)
N_SPLITS = 6

ADAM_LR = 0.001
ADAM_B1 = 0.9
ADAM_B2 = 0.999
ADAM_EPS = 1e-08
ADAM_WD = 0.01
ADAM_STEP = 10

LANES = 128
SUBLANES_F32 = 8
SUBLANES_BF16 = 16
VMEM_BYTES = 64 * 1024 * 1024
VMEM_CAP = VMEM_BYTES - 8 * 1024 * 1024
VMEM_FLOOR = 16 * 1024 * 1024

ANY = pl.BlockSpec(memory_space=pl.ANY)


def _tile(dim, pref, align):
    if dim <= pref:
        return dim
    t = (pref // align) * align
    while t >= align:
        if dim % t == 0:
            return t
        t -= align
    return dim


def _nbytes(shape, dtype):
    n = 1
    for s in shape:
        n *= s
    return n * jnp.dtype(dtype).itemsize


def _params(sem, block_bytes, temp_bytes=0, collective_id=None):
    need = 2 * block_bytes + temp_bytes + 4 * 1024 * 1024
    return pltpu.CompilerParams(dimension_semantics=sem, collective_id=collective_id,
                                vmem_limit_bytes=int(min(max(need, VMEM_FLOOR), VMEM_CAP)))


SIBLING = (0, 0, 1)
CHIPS = ((1, 0, 0), (0, 1, 0), (1, 1, 0))
EVERYONE = tuple((a, b, c) for a in range(2) for b in range(2) for c in range(2) if a + b + c)
PEER_SETS = (frozenset([SIBLING]), frozenset(CHIPS), frozenset(CHIPS + (SIBLING,)), frozenset(EVERYONE))
MID_AT = 0.75


def _collective_id(peers):
    return PEER_SETS.index(frozenset(peers))


def _handshake(peers):
    x, y, c = lax.axis_index("x"), lax.axis_index("y"), lax.axis_index("c")
    bar = pltpu.get_barrier_semaphore()
    for fx, fy, fc in sorted(peers):
        dev = (1 - x if fx else x, 1 - y if fy else y, 1 - c if fc else c)
        pl.semaphore_signal(bar, inc=1, device_id=dev, device_id_type=MESH)
    pl.semaphore_wait(bar, len(peers))


class _Comm:
    def __init__(self, ins, out_shapes, sems, start, finish, peers, aliases=None, mid=None):
        self.ins = list(ins)
        self.out_shapes = list(out_shapes)
        self.sems = list(sems)
        self.start = start
        self.finish = finish
        self.mid = mid
        self.peers = frozenset(peers)
        self.aliases = dict(aliases or {})


def _pcall(body, *, name, grid, in_specs, out_specs, out_shape, sem, blocks, temps=0, scratch_shapes=(),
           input_output_aliases=None, comm=None):
    in_specs = list(in_specs)
    out_specs = list(out_specs)
    out_shape = list(out_shape)
    scratch_shapes = list(scratch_shapes)
    aliases = dict(input_output_aliases or {})
    n_in, n_out, n_scr = len(in_specs), len(out_shape), len(scratch_shapes)
    if comm is None:
        call = pl.pallas_call(
            body, name=name, grid=grid, in_specs=in_specs, out_specs=out_specs, out_shape=out_shape,
            scratch_shapes=scratch_shapes, input_output_aliases=aliases,
            compiler_params=_params(sem, blocks, temps))
        return lambda *args: (list(call(*args)), [])

    nci, nco = len(comm.ins), len(comm.out_shapes)
    n_steps = 1
    for g in grid:
        n_steps *= g

    def hosted(*refs):
        ins = refs[:n_in]
        cins = refs[n_in:n_in + nci]
        outs = refs[n_in + nci:n_in + nci + n_out]
        couts = refs[n_in + nci + n_out:n_in + nci + n_out + nco]
        scr = refs[n_in + nci + n_out + nco:n_in + nci + n_out + nco + n_scr]
        csems = refs[n_in + nci + n_out + nco + n_scr:]
        first = None
        last = None
        step = 0
        for q, g in enumerate(grid):
            pid = pl.program_id(q)
            first = (pid == 0) if first is None else first & (pid == 0)
            last = (pid == g - 1) if last is None else last & (pid == g - 1)
            step = step * g + pid

        @pl.when(first)
        def _():
            _handshake(comm.peers)
            comm.start(cins, couts, csems)

        if comm.mid is not None:
            @pl.when(step == int(MID_AT * n_steps))
            def _():
                comm.mid(cins, couts, csems)

        body(*ins, *outs, *scr)

        @pl.when(last)
        def _():
            comm.finish(cins, couts, csems)

    for i, o in comm.aliases.items():
        aliases[n_in + i] = n_out + o
    call = pl.pallas_call(
        hosted, name=name, grid=grid, in_specs=in_specs + [ANY] * nci, out_specs=out_specs + [ANY] * nco,
        out_shape=out_shape + comm.out_shapes, scratch_shapes=scratch_shapes + comm.sems,
        input_output_aliases=aliases,
        compiler_params=_params(("arbitrary",) * len(grid), blocks, temps, _collective_id(comm.peers)))

    def run(*args):
        res = call(*args, *comm.ins)
        return list(res[:n_out]), list(res[n_out:])

    return run


def _run_comm(comm, name):
    def body(*refs):
        nci, nco = len(comm.ins), len(comm.out_shapes)
        cins, couts, csems = refs[:nci], refs[nci:nci + nco], refs[nci + nco:]
        _handshake(comm.peers)
        comm.start(cins, couts, csems)
        if comm.mid is not None:
            comm.mid(cins, couts, csems)
        comm.finish(cins, couts, csems)

    return list(pl.pallas_call(
        body, name=name, in_specs=[ANY] * len(comm.ins), out_specs=[ANY] * len(comm.out_shapes),
        out_shape=comm.out_shapes, scratch_shapes=comm.sems, input_output_aliases=comm.aliases,
        compiler_params=pltpu.CompilerParams(collective_id=_collective_id(comm.peers)),
    )(*comm.ins))


def _dot(a, b):
    return jnp.dot(a, b, preferred_element_type=F32)


def _dot_tb(a, b):
    return lax.dot_general(a, b, (((1,), (1,)), ((), ())), preferred_element_type=F32)


def _dot_ta(a, b):
    return lax.dot_general(a, b, (((0,), (0,)), ((), ())), preferred_element_type=F32)


def _rms_fwd(x):
    inv = lax.rsqrt(jnp.mean(x * x, axis=-1, keepdims=True) + RMS_EPS)
    return x * inv, inv


def _rms_bwd(dy, xhat, inv, g):
    gd = dy * g
    return inv * (gd - xhat * jnp.mean(gd * xhat, axis=-1, keepdims=True))


def _sigmoid(x):
    return 1.0 / (1.0 + jnp.exp(-x))


def _shift_down(x, k, row):
    return jnp.where(row >= k, pltpu.roll(x, k, 0), 0.0)


def _shift_up(x, k, row):
    s = x.shape[0]
    return jnp.where(row < s - k, pltpu.roll(x, s - k, 0), 0.0)


def _pool_fwd(u, win, row):
    s = u
    k = 1
    while k < win:
        s = s + _shift_down(s, k, row)
        k *= 2
    cnt = jnp.minimum(row + 1, win).astype(F32)
    return s / cnt - u


def _pool_bwd(dp, win, row):
    cnt = jnp.minimum(row + 1, win).astype(F32)
    s = dp / cnt
    k = 1
    while k < win:
        s = s + _shift_up(s, k, row)
        k *= 2
    return s - dp


def _row_parts(tm, n=2):
    sub = tm // n
    assert sub * n == tm and sub % SUBLANES_BF16 == 0, (tm, n)
    return [pl.ds(k * sub, sub) for k in range(n)]


def _acc_over(k, nk, part, acc, o_ref):
    @pl.when(k == 0)
    def _():
        acc[...] = part

    @pl.when(k > 0)
    def _():
        acc[...] += part

    @pl.when(k == nk - 1)
    def _():
        o_ref[...] = acc[...].astype(o_ref.dtype)


def _fwd_in(x, g, w_loc, order, comm):
    t, d = x.shape
    ws = w_loc.shape[1]
    nsh = order.shape[0]
    assert nsh == 4, "the shard walk below is written for the 2 x 2 chips of the mesh"
    tm = _tile(t, 1024, SUBLANES_BF16)
    ni = t // tm
    nci, nco = len(comm.ins), len(comm.out_shapes)
    all_peers = comm.peers | frozenset(CHIPS + (SIBLING,))

    def body(order_ref, x_ref, g_ref, loc_ref, *rest):
        del order_ref
        cins = rest[:nci]
        z_ref, h_ref, full_ref = rest[nci:nci + 3]
        couts = rest[nci + 3:nci + 3 + nco]
        (hs, wbuf, wsem, own_s, own_r, snd_s, snd_r, fwd_s, fwd_r, rly_s, rly_r) = rest[nci + 3 + nco:nci + 14 + nco]
        csems = rest[nci + 14 + nco:]
        j = pl.program_id(0)
        i = pl.program_id(1)
        x_, y_, c_ = _coords()
        own = 2 * x_ + y_
        sib = (x_, y_, 1 - c_)
        peers = _peer_chips(x_, y_)

        def sends():
            return [_remote(_half(0, loc_ref, c_), _piece(0, full_ref, own, c_), snd_s.at[p], snd_r.at[p], (px, py, c_))
                    for p, (px, py) in enumerate(peers[:2])]

        def relays():
            out = []
            for q, (src_p, dst_p) in enumerate(((0, 1), (1, 0))):
                sx, sy = peers[src_p]
                part = _rows_part(0, _piece(0, full_ref, 2 * sx + sy, c_), (q, q + 1, 2))
                out.append(_remote(part, part, rly_s.at[q], rly_r.at[q], (*peers[dst_p], c_)))
            return out

        def owns():
            return [_remote(_half(0, loc_ref, h), _piece(0, full_ref, own, h), own_s.at[h], own_r.at[h], sib)
                    for h in range(2)]

        def forward(p, half):
            px, py = peers[p]
            landed = _piece(0, full_ref, 2 * px + py, half)
            return _remote(landed, landed, fwd_s.at[p], fwd_r.at[p], sib)

        def load(src, slot):
            return pltpu.make_async_copy(src, wbuf.at[slot], wsem.at[slot])

        @pl.when((j == 0) & (i == 0))
        def _():
            _handshake(all_peers)
            for cp in sends() + owns():
                cp.start()
            load(loc_ref, 0).start()

        @pl.when(j == 0)
        def _():
            xh, _ = _rms_fwd(x_ref[...])
            h = (xh * g_ref[...]).astype(BF16)
            hs[pl.ds(pl.multiple_of(i * tm, tm), tm), :] = h
            h_ref[...] = h

        slot = j % 2

        @pl.when(i == 0)
        def _():
            load(loc_ref, slot).wait()

        z_ref[...] = _dot(hs[pl.ds(pl.multiple_of(i * tm, tm), tm), :], wbuf[slot]).astype(BF16)

        def load_shard(p, into):
            px, py = peers[p]
            forward(p, 1 - c_).wait_recv()
            load(full_ref.at[2 * px + py], into).start()

        @pl.when((j == 0) & (i == ni - 1))
        def _():
            for cp in sends():
                cp.wait_recv()
            for cp in relays() + [forward(0, c_), forward(1, c_)]:
                cp.start()
            load_shard(0, 1)
            comm.start(cins, couts, csems)

        @pl.when((j == 1) & (i == 0))
        def _():
            load_shard(1, 0)

        @pl.when((j == 2) & (i == max(ni - 2, 0)))
        def _():
            for cp in relays():
                cp.wait_recv()
            forward(2, c_).start()
            load_shard(2, 1)

        @pl.when((j == nsh - 1) & (i == ni - 1))
        def _():
            for cp in sends() + relays() + [forward(p, c_) for p in range(nsh - 1)]:
                cp.wait_send()
            for cp in owns():
                cp.wait()
            comm.finish(cins, couts, csems)

    last = ni - 1
    blocks = _nbytes((tm, d), F32) + _nbytes((tm, ws), BF16) + _nbytes((tm, d), BF16)
    scratch = _nbytes((t, d), BF16) + 2 * _nbytes((d, ws), BF16)
    res = pl.pallas_call(
        body, name="fwd_in",
        grid_spec=pltpu.PrefetchScalarGridSpec(
            num_scalar_prefetch=1, grid=(nsh, ni),
            in_specs=[pl.BlockSpec((tm, d), lambda j, i, o: (jnp.where(j == 0, i, last), 0)),
                      pl.BlockSpec((1, d), lambda j, i, o: (0, 0)), ANY] + [ANY] * nci,
            out_specs=[pl.BlockSpec((tm, ws), lambda j, i, o: (i, o[j])),
                       pl.BlockSpec((tm, d), lambda j, i, o: (jnp.where(j == 0, i, last), 0)), ANY] + [ANY] * nco,
            scratch_shapes=[pltpu.VMEM((t, d), BF16), pltpu.VMEM((2, d, ws), BF16)]
            + _dma_sems(2, 2, 2, 2, 2, nsh - 1, nsh - 1, 2, 2) + comm.sems),
        out_shape=[SDS((t, nsh * ws), BF16), SDS((t, d), BF16), SDS((nsh, d, ws), BF16)] + comm.out_shapes,
        input_output_aliases={4 + i: 3 + o for i, o in comm.aliases.items()},
        compiler_params=_params(("arbitrary", "arbitrary"), blocks, scratch + 3 * _nbytes((tm, d), F32),
                                _collective_id(all_peers)),
    )(order, x, g, w_loc, *comm.ins)
    return list(res[:3]), list(res[3:])


def _mixer_mid_fwd(z, pool_w, pool_scale, conv_w, nseq, comm=None):
    t = z.shape[0]
    d = pool_scale.shape[1]
    s = t // nseq
    c = d // N_GROUPS

    def body(zp, zb, zc, zv, pw, ps, cw, o):
        j = pl.program_id(1)
        row = lax.broadcasted_iota(jnp.int32, (s, c), 0)
        for gi, win in enumerate(POOL_WINDOWS):
            @pl.when(j == gi)
            def _(win=win):
                pooled = _pool_fwd(zp[...].astype(F32), win, row)
                o[0] = (_dot(pooled.astype(BF16), pw[...]) * ps[...]).astype(BF16)

        cv = zc[...].astype(F32) * zv[...].astype(F32)
        cc = (cw[pl.ds(2, 1), :] * cv + cw[pl.ds(1, 1), :] * _shift_down(cv, 1, row)
              + cw[pl.ds(0, 1), :] * _shift_down(cv, 2, row))
        o[1] = (zb[...].astype(F32) * cc).astype(BF16)

    blocks = 4 * _nbytes((s, c), BF16) + _nbytes((c, c), BF16) + _nbytes((2, s, c), BF16)
    return _pcall(
        body, name="mixer_mid_fwd", grid=(nseq, N_GROUPS),
        in_specs=[pl.BlockSpec((s, c), lambda b, j: (b, j)),
                  pl.BlockSpec((s, c), lambda b, j: (b, N_GROUPS + j)),
                  pl.BlockSpec((s, c), lambda b, j: (b, 2 * N_GROUPS + j)),
                  pl.BlockSpec((s, c), lambda b, j: (b, 3 * N_GROUPS + j)),
                  pl.BlockSpec((None, c, c), lambda b, j: (j, 0, 0)),
                  pl.BlockSpec((1, c), lambda b, j: (0, j)),
                  pl.BlockSpec((3, c), lambda b, j: (0, j))],
        out_specs=[pl.BlockSpec((2, s, c), lambda b, j: (0, b, j))],
        out_shape=[SDS((3, t, d), BF16)],
        sem=("parallel", "parallel"), blocks=blocks, temps=8 * _nbytes((s, c), F32), comm=comm,
    )(z, z, z, z, pool_w, pool_scale, conv_w)


def _mixer_out(lhs3, z, x, w3, g_ffn, comm=None):
    t, d = x.shape
    tm = _tile(t, 512, SUBLANES_BF16)

    def body(pq, zgp, zgc, x_ref, w_ref, g_ref, mrg, ypc, x1o, h2o):
        for rows in _row_parts(tm, 4):
            yp = _dot(pq[0, rows, :], w_ref[0])
            yc = _dot(pq[1, rows, :], w_ref[1])
            m = _sigmoid(zgp[rows, :].astype(F32)) * yp + _sigmoid(zgc[rows, :].astype(F32)) * yc
            mb = m.astype(BF16)
            x1 = x_ref[rows, :] + _dot(mb, w_ref[2])
            ypc[0, rows, :] = yp.astype(BF16)
            ypc[1, rows, :] = yc.astype(BF16)
            mrg[rows, :] = mb
            x1o[rows, :] = x1
            xh, _ = _rms_fwd(x1)
            h2o[rows, :] = (xh * g_ref[...]).astype(BF16)

    blocks = (_nbytes((2, tm, d), BF16) * 2 + _nbytes((tm, d), BF16) * 4 + _nbytes((tm, d), F32) * 2
              + _nbytes((3, d, d), BF16))
    return _pcall(
        body, name="mixer_out", grid=(t // tm,),
        in_specs=[pl.BlockSpec((2, tm, d), lambda i: (0, i, 0)),
                  pl.BlockSpec((tm, d), lambda i: (i, 4)),
                  pl.BlockSpec((tm, d), lambda i: (i, 5)),
                  pl.BlockSpec((tm, d), lambda i: (i, 0)),
                  pl.BlockSpec((3, d, d), lambda i: (0, 0, 0)),
                  pl.BlockSpec((1, d), lambda i: (0, 0))],
        out_specs=[pl.BlockSpec((None, tm, d), lambda i: (2, i, 0)),
                   pl.BlockSpec((2, tm, d), lambda i: (0, i, 0)),
                   pl.BlockSpec((tm, d), lambda i: (i, 0)),
                   pl.BlockSpec((tm, d), lambda i: (i, 0))],
        out_shape=[SDS(lhs3.shape, BF16), SDS((2, t, d), BF16), SDS((t, d), F32), SDS((t, d), BF16)],
        input_output_aliases={0: 0},
        sem=("parallel",), blocks=blocks, temps=8 * _nbytes((tm, d), F32), comm=comm,
    )(lhs3, z, z, x, w3, g_ffn)


def _ffn_up(h2, w_up, f, comm=None):
    t, d = h2.shape
    _, _, ws = w_up.shape
    tm = _tile(t, 1024, SUBLANES_BF16)
    tn = _tile(ws, 1408, LANES)
    nps = ws // tn
    npp = f // tn

    def body(h_ref, w_ref, o_ref):
        o_ref[...] = _dot(h_ref[...], w_ref[...]).astype(BF16)

    blocks = _nbytes((tm, d), BF16) + _nbytes((d, tn), BF16) + _nbytes((tm, tn), BF16)
    return _pcall(
        body, name="ffn_up", grid=(t // tm, 2 * npp),
        in_specs=[pl.BlockSpec((tm, d), lambda i, j: (i, 0)),
                  pl.BlockSpec((None, d, tn), lambda i, j: (j // nps, 0, j % nps))],
        out_specs=[pl.BlockSpec((None, tm, tn), lambda i, j: (j // npp, i, j % npp))],
        out_shape=[SDS((2, t, f), BF16)],
        sem=("parallel", "parallel"), blocks=blocks, temps=_nbytes((tm, tn), F32), comm=comm,
    )(h2, w_up)


def _conv3_rows(u, u1, u2, w_ref, p):
    return w_ref[p, pl.ds(2, 1), :] * u + w_ref[p, pl.ds(1, 1), :] * u1 + w_ref[p, pl.ds(0, 1), :] * u2


WGRAD_TOKENS = 2048
WGRAD_TOKENS_WIDE = 4096
CHUNK = 64
HALO = SUBLANES_F32


def _up1_up2(u, nxt):
    rows = u.shape[0]
    ext = jnp.concatenate([u, nxt], axis=0)
    n = rows + HALO
    return pltpu.roll(ext, n - 1, 0)[:rows], pltpu.roll(ext, n - 2, 0)[:rows]


def _fold8(x):
    return jnp.sum(x.reshape(x.shape[0] // SUBLANES_F32, SUBLANES_F32, x.shape[1]), axis=0)


def _ffn_mid_fwd(u0, cw, cb, nseq):
    _, t, f = u0.shape
    s = t // nseq
    c = _tile(f, 256, LANES)

    def body(u_ref, w_ref, b_ref, a_ref, uo_ref):
        row = lax.broadcasted_iota(jnp.int32, (s, c), 0)
        act = []
        for p in range(2):
            u = u_ref[p].astype(F32)
            act.append(_conv3_rows(u, _shift_down(u, 1, row), _shift_down(u, 2, row), w_ref, p) + b_ref[p])
            uo_ref[p] = act[p].astype(BF16)
        ug, uv = act
        a_ref[...] = (ug * _sigmoid(ug) * uv).astype(BF16)

    blocks = 2 * _nbytes((2, s, c), BF16) + _nbytes((s, c), BF16)
    outs, _ = _pcall(
        body, name="ffn_mid_fwd", grid=(f // c, nseq),
        in_specs=[pl.BlockSpec((2, s, c), lambda j, b: (0, b, j)),
                  pl.BlockSpec((2, 3, c), lambda j, b: (0, 0, j)),
                  pl.BlockSpec((2, 1, c), lambda j, b: (0, 0, j))],
        out_specs=[pl.BlockSpec((s, c), lambda j, b: (b, j)),
                   pl.BlockSpec((2, s, c), lambda j, b: (0, b, j))],
        out_shape=[SDS((t, f), BF16), SDS((2, t, f), BF16)],
        sem=("parallel", "parallel"), blocks=blocks, temps=8 * _nbytes((s, c), F32),
    )(u0, cw, cb)
    return outs


def _ffn_down_loss(a, w_down, x1, tgt, g_fin):
    t, f = a.shape
    d = x1.shape[1]
    tm = _tile(t, 512, SUBLANES_BF16)
    nsteps = t // tm

    def body(a_ref, w_ref, x1_ref, t_ref, g_ref, dx_ref, dxb_ref, loss_ref, gg_ref, lacc):
        i = pl.program_id(0)

        @pl.when(i == 0)
        def _():
            lacc[...] = jnp.zeros_like(lacc)
            gg_ref[...] = jnp.zeros_like(gg_ref)

        g = g_ref[...]
        sq, gg = 0.0, 0.0
        for rows in _row_parts(tm, 4):
            x2 = x1_ref[rows, :] + _dot(a_ref[rows, :], w_ref[...])
            xh, inv = _rms_fwd(x2)
            e = xh * g - t_ref[rows, :]
            sq = sq + jnp.sum(e * e, axis=0, keepdims=True)
            dy = e * (1.0 / d)
            gg = gg + jnp.sum(dy * xh, axis=0, keepdims=True)
            dx2 = _rms_bwd(dy, xh, inv, g)
            dx_ref[rows, :] = dx2
            dxb_ref[rows, :] = dx2.astype(BF16)
        lacc[...] += sq
        gg_ref[...] += gg

        @pl.when(i == nsteps - 1)
        def _():
            loss_ref[...] = jnp.sum(lacc[...], axis=1, keepdims=True) * (0.5 / d)

    blocks = (_nbytes((tm, f), BF16) + _nbytes((f, d), BF16) + 3 * _nbytes((tm, d), F32) + _nbytes((tm, d), BF16))
    outs, _ = _pcall(
        body, name="ffn_down_loss", grid=(nsteps,),
        in_specs=[pl.BlockSpec((tm, f), lambda i: (i, 0)), pl.BlockSpec((f, d), lambda i: (0, 0)),
                  pl.BlockSpec((tm, d), lambda i: (i, 0)), pl.BlockSpec((tm, d), lambda i: (i, 0)),
                  pl.BlockSpec((1, d), lambda i: (0, 0))],
        out_specs=[pl.BlockSpec((tm, d), lambda i: (i, 0)), pl.BlockSpec((tm, d), lambda i: (i, 0)),
                   pl.BlockSpec((1, 1), lambda i: (0, 0)), pl.BlockSpec((1, d), lambda i: (0, 0))],
        out_shape=[SDS((t, d), F32), SDS((t, d), BF16), SDS((1, 1), F32), SDS((1, d), F32)],
        scratch_shapes=[pltpu.VMEM((1, d), F32)],
        sem=("arbitrary",), blocks=blocks, temps=8 * _nbytes((tm, d), F32),
    )(a, w_down, x1, tgt, g_fin)
    return outs


def _ffn_bwd_da(dxb, w_down, comm=None):
    t, d = dxb.shape
    f = w_down.shape[0]
    tm = _tile(t, 512, SUBLANES_BF16)

    def body(x_ref, w_ref, o_ref):
        o_ref[...] = _dot_tb(x_ref[...], w_ref[...]).astype(BF16)

    blocks = _nbytes((tm, d), BF16) + _nbytes((tm, f), BF16)
    return _pcall(
        body, name="ffn_bwd_da", grid=(t // tm,),
        in_specs=[pl.BlockSpec((tm, d), lambda i: (i, 0)),
                  pl.BlockSpec((f, d), lambda i: (0, 0), pipeline_mode=pl.Buffered(1))],
        out_specs=[pl.BlockSpec((tm, f), lambda i: (i, 0))],
        out_shape=[SDS((t, f), BF16)],
        sem=("parallel",), blocks=blocks, temps=_nbytes((f, d), BF16) + _nbytes((tm, f), F32), comm=comm,
    )(dxb, w_down)


def _ffn_mid_bwd(da, u0, ua, cw, nseq, comm=None):
    _, t, f = u0.shape
    s = t // nseq
    c = _tile(f, 128, LANES)
    r = _tile(s, CHUNK, SUBLANES_BF16)
    n = s // r

    def body(da_ref, u_ref, ua_ref, w_ref, du_ref, gw_ref, gb_ref):
        @pl.when(pl.program_id(1) == 0)
        def _():
            gw_ref[...] = jnp.zeros_like(gw_ref)
            gb_ref[...] = jnp.zeros_like(gb_ref)

        def step(i, carry):
            nxt, sums = carry
            rows = pl.ds(pl.multiple_of((n - 1 - i) * r, r), r)
            ug = ua_ref[0, rows, :].astype(F32)
            uv = ua_ref[1, rows, :].astype(F32)
            sg = _sigmoid(ug)
            dacc = da_ref[rows, :].astype(F32)
            dus = (dacc * uv * sg * (1.0 + ug * (1.0 - sg)), dacc * (ug * sg))
            first, new_sums = [], []
            for p in range(2):
                du = dus[p]
                d1, d2 = _up1_up2(du, nxt[p])
                du_ref[p, rows, :] = _conv3_rows(du, d1, d2, w_ref, p).astype(BF16)
                u = u_ref[p, rows, :].astype(F32)
                sb, s0, s1, s2 = sums[p]
                new_sums.append((sb + _fold8(du), s0 + _fold8(d2 * u), s1 + _fold8(d1 * u), s2 + _fold8(du * u)))
                first.append(du[:HALO])
            return tuple(first), tuple(new_sums)

        zero = jnp.zeros((HALO, c), F32)
        _, sums = lax.fori_loop(0, n, step, ((zero, zero), ((zero,) * 4,) * 2))
        for p in range(2):
            sb, s0, s1, s2 = sums[p]
            gb_ref[p] += jnp.sum(sb, axis=0, keepdims=True)
            gw_ref[p, pl.ds(0, 1), :] += jnp.sum(s0, axis=0, keepdims=True)
            gw_ref[p, pl.ds(1, 1), :] += jnp.sum(s1, axis=0, keepdims=True)
            gw_ref[p, pl.ds(2, 1), :] += jnp.sum(s2, axis=0, keepdims=True)

    blocks = _nbytes((s, c), BF16) + 3 * _nbytes((2, s, c), BF16)
    return _pcall(
        body, name="ffn_mid_bwd", grid=(f // c, nseq),
        in_specs=[pl.BlockSpec((s, c), lambda j, b: (b, j)),
                  pl.BlockSpec((2, s, c), lambda j, b: (0, b, j)),
                  pl.BlockSpec((2, s, c), lambda j, b: (0, b, j)),
                  pl.BlockSpec((2, 3, c), lambda j, b: (0, 0, j))],
        out_specs=[pl.BlockSpec((2, s, c), lambda j, b: (0, b, j)),
                   pl.BlockSpec((2, 3, c), lambda j, b: (0, 0, j)),
                   pl.BlockSpec((2, 1, c), lambda j, b: (0, 0, j))],
        out_shape=[SDS((2, t, f), BF16), SDS((2, 3, f), F32), SDS((2, 1, f), F32)],
        sem=("parallel", "arbitrary"), blocks=blocks, temps=4 * 1024 * 1024, comm=comm,
    )(da, u0, ua, cw)


def _wgrad(a, b, name, *, tr, tn, b_plane_of=None, out_shards=None, comm=None):
    t, m = a.shape
    n_total = b.shape[-1] * (b.shape[0] if b.ndim == 3 else 1)
    tk = _tile(t, WGRAD_TOKENS_WIDE if n_total > tn and m == tr else WGRAD_TOKENS, SUBLANES_BF16)
    nk = t // tk
    once = pl.Buffered(1) if nk == 1 else None

    def body(a_ref, b_ref, o_ref, *acc):
        part = _dot_ta(a_ref[...], b_ref[...])
        if nk == 1:
            o_ref[...] = part.astype(BF16)
        else:
            _acc_over(pl.program_id(2), nk, part, acc[0], o_ref)

    if b.ndim == 3:
        b_spec = pl.BlockSpec((None, tk, tn), lambda r, n, k: (b_plane_of(n)[0], k, b_plane_of(n)[1]))
    else:
        b_spec = pl.BlockSpec((tk, tn), lambda r, n, k: (k, n), pipeline_mode=once if n_total == tn else None)
    if out_shards is None:
        o_spec = pl.BlockSpec((tr, tn), lambda r, n, k: (r, n))
        o_shape = SDS((m, n_total), BF16)
    else:
        nps = n_total // out_shards // tn
        o_spec = pl.BlockSpec((None, tr, tn), lambda r, n, k: (n // nps, r, n % nps))
        o_shape = SDS((out_shards, m, n_total // out_shards), BF16)
    blocks = _nbytes((tk, tr), BF16) + _nbytes((tk, tn), BF16) + _nbytes((tr, tn), BF16)
    return _pcall(
        body, name=name, grid=(m // tr, n_total // tn, nk),
        in_specs=[pl.BlockSpec((tk, tr), lambda r, n, k: (k, r), pipeline_mode=once if m == tr else None), b_spec],
        out_specs=[o_spec], out_shape=[o_shape],
        scratch_shapes=[] if nk == 1 else [pltpu.VMEM((tr, tn), F32)],
        sem=("parallel", "parallel", "arbitrary"), blocks=blocks, temps=2 * _nbytes((tr, tn), F32), comm=comm,
    )(a, b)


def _wgrad3(lhs3, rhs3, comm=None):
    nw, t, d = lhs3.shape
    tk = _tile(t, WGRAD_TOKENS, SUBLANES_BF16)
    nk = t // tk

    def body(a_ref, b_ref, o_ref, *acc):
        part = _dot_ta(a_ref[...], b_ref[...])
        if nk == 1:
            o_ref[...] = part.astype(BF16)
        else:
            _acc_over(pl.program_id(1), nk, part, acc[0], o_ref)

    blocks = 2 * _nbytes((tk, d), BF16) + _nbytes((d, d), BF16)
    return _pcall(
        body, name="wgrad_sq3", grid=(nw, nk),
        in_specs=[pl.BlockSpec((None, tk, d), lambda w, k: (w, k, 0)),
                  pl.BlockSpec((None, tk, d), lambda w, k: (w, k, 0))],
        out_specs=[pl.BlockSpec((None, d, d), lambda w, k: (w, 0, 0))],
        out_shape=[SDS((nw, d, d), BF16)],
        scratch_shapes=[] if nk == 1 else [pltpu.VMEM((d, d), F32)],
        sem=("parallel", "arbitrary"), blocks=blocks, temps=2 * _nbytes((d, d), F32), comm=comm,
    )(lhs3, rhs3)


def _ffn_bwd_dx1(du0, w_up, x1, dx2, g_ffn, n_planes_out, comm=None):
    _, t, f = du0.shape
    d = x1.shape[1]
    nsh, _, ws = w_up.shape
    tm = _tile(t, 256, SUBLANES_BF16)
    spp = f // ws

    def body(du_ref, w_ref, x1_ref, dx2_ref, g_ref, dx1_ref, dxb_ref, gg_ref):
        @pl.when(pl.program_id(0) == 0)
        def _():
            gg_ref[...] = jnp.zeros_like(gg_ref)

        dh = None
        for k in range(nsh):
            part = _dot_tb(du_ref[k // spp, :, (k % spp) * ws:(k % spp + 1) * ws], w_ref[k])
            dh = part if dh is None else dh + part
        xh, inv = _rms_fwd(x1_ref[...])
        gg_ref[...] += jnp.sum(dh * xh, axis=0, keepdims=True)
        dx1 = dx2_ref[...] + _rms_bwd(dh, xh, inv, g_ref[...])
        dx1_ref[...] = dx1
        dxb_ref[...] = dx1.astype(BF16)

    blocks = _nbytes((2, tm, f), BF16) + 3 * _nbytes((tm, d), F32) + _nbytes((tm, d), BF16)
    return _pcall(
        body, name="ffn_bwd_dx1", grid=(t // tm,),
        in_specs=[pl.BlockSpec((2, tm, f), lambda i: (0, i, 0)),
                  pl.BlockSpec((nsh, d, ws), lambda i: (0, 0, 0), pipeline_mode=pl.Buffered(1)),
                  pl.BlockSpec((tm, d), lambda i: (i, 0)),
                  pl.BlockSpec((tm, d), lambda i: (i, 0)),
                  pl.BlockSpec((1, d), lambda i: (0, 0))],
        out_specs=[pl.BlockSpec((tm, d), lambda i: (i, 0)),
                   pl.BlockSpec((None, tm, d), lambda i: (n_planes_out - 1, i, 0)),
                   pl.BlockSpec((1, d), lambda i: (0, 0))],
        out_shape=[SDS((t, d), F32), SDS((n_planes_out, t, d), BF16), SDS((1, d), F32)],
        sem=("arbitrary",), blocks=blocks, temps=_nbytes(w_up.shape, BF16) + 8 * _nbytes((tm, d), F32), comm=comm,
    )(du0, w_up, x1, dx2, g_ffn)


def _mixer_bwd(rhs3, z, ypc, w3, comm=None):
    _, t, d = rhs3.shape
    tm = _tile(t, 256, SUBLANES_BF16)

    def body(dx_ref, zgp, zgc, ypc_ref, w_ref, dyo, dzo, dpq):
        dm = _dot_tb(dx_ref[...], w_ref[2])
        sp = _sigmoid(zgp[...].astype(F32))
        sc = _sigmoid(zgc[...].astype(F32))
        dyp = (dm * sp).astype(BF16)
        dyc = (dm * sc).astype(BF16)
        dzo[0] = (dm * ypc_ref[0].astype(F32) * sp * (1.0 - sp)).astype(BF16)
        dzo[1] = (dm * ypc_ref[1].astype(F32) * sc * (1.0 - sc)).astype(BF16)
        dyo[0] = dyp
        dyo[1] = dyc
        dpq[0] = _dot_tb(dyp, w_ref[0]).astype(BF16)
        dpq[1] = _dot_tb(dyc, w_ref[1]).astype(BF16)

    blocks = _nbytes((tm, d), BF16) * 3 + _nbytes((2, tm, d), BF16) * 4 + _nbytes((3, d, d), BF16)
    return _pcall(
        body, name="mixer_bwd", grid=(t // tm,),
        in_specs=[pl.BlockSpec((None, tm, d), lambda i: (2, i, 0)),
                  pl.BlockSpec((tm, d), lambda i: (i, 4)),
                  pl.BlockSpec((tm, d), lambda i: (i, 5)),
                  pl.BlockSpec((2, tm, d), lambda i: (0, i, 0)),
                  pl.BlockSpec((3, d, d), lambda i: (0, 0, 0))],
        out_specs=[pl.BlockSpec((2, tm, d), lambda i: (0, i, 0)),
                   pl.BlockSpec((2, tm, d), lambda i: (2, i, 0)),
                   pl.BlockSpec((2, tm, d), lambda i: (0, i, 0))],
        out_shape=[SDS(rhs3.shape, BF16), SDS((N_SPLITS, t, d), BF16), SDS((2, t, d), BF16)],
        input_output_aliases={0: 0},
        sem=("parallel",), blocks=blocks, temps=8 * _nbytes((tm, d), F32), comm=comm,
    )(rhs3, z, z, ypc, w3)


def _conv_bwd(dz, dpq, z, conv_w, nseq, comm=None):
    _, t, d = dz.shape
    s = t // nseq
    c = _tile(d, 128, LANES)
    nb = d // c

    def body(dz_in, dq_ref, zb, zc, zv, cw, dzo, gw_ref):
        del dz_in

        @pl.when(pl.program_id(1) == 0)
        def _():
            gw_ref[...] = jnp.zeros_like(gw_ref)

        row = lax.broadcasted_iota(jnp.int32, (s, c), 0)
        b = zb[...].astype(F32)
        cm = zc[...].astype(F32)
        v = zv[...].astype(F32)
        cv = cm * v
        cv1 = _shift_down(cv, 1, row)
        cv2 = _shift_down(cv, 2, row)
        w0, w1, w2 = cw[pl.ds(0, 1), :], cw[pl.ds(1, 1), :], cw[pl.ds(2, 1), :]
        cc = w2 * cv + w1 * cv1 + w0 * cv2
        dq = dq_ref[...].astype(F32)
        dzo[0] = (dq * cc).astype(BF16)
        dcc = dq * b
        gw_ref[pl.ds(0, 1), :] += jnp.sum(dcc * cv2, axis=0, keepdims=True)
        gw_ref[pl.ds(1, 1), :] += jnp.sum(dcc * cv1, axis=0, keepdims=True)
        gw_ref[pl.ds(2, 1), :] += jnp.sum(dcc * cv, axis=0, keepdims=True)
        dcv = w2 * dcc + w1 * _shift_up(dcc, 1, row) + w0 * _shift_up(dcc, 2, row)
        dzo[1] = (dcv * v).astype(BF16)
        dzo[2] = (dcv * cm).astype(BF16)

    blocks = 4 * _nbytes((s, c), BF16) + _nbytes((3, s, c), BF16)
    return _pcall(
        body, name="conv_bwd", grid=(nb, nseq),
        in_specs=[ANY,
                  pl.BlockSpec((None, s, c), lambda j, b: (1, b, j)),
                  pl.BlockSpec((s, c), lambda j, b: (b, nb + j)),
                  pl.BlockSpec((s, c), lambda j, b: (b, 2 * nb + j)),
                  pl.BlockSpec((s, c), lambda j, b: (b, 3 * nb + j)),
                  pl.BlockSpec((3, c), lambda j, b: (0, j))],
        out_specs=[pl.BlockSpec((3, s, c), lambda j, b: (0, b, j)),
                   pl.BlockSpec((3, c), lambda j, b: (0, j))],
        out_shape=[SDS(dz.shape, BF16), SDS((3, d), F32)],
        input_output_aliases={0: 0},
        sem=("parallel", "arbitrary"), blocks=blocks, temps=16 * _nbytes((s, c), F32), comm=comm,
    )(dz, dpq, z, z, z, conv_w)


def _pool_bwd_call(dz, dpq, z, pool_w, pool_scale, nseq, comm=None):
    _, t, d = dz.shape
    s = t // nseq
    c = d // N_GROUPS

    def body(dz_in, dp_ref, zp, pw, ps, dzo, gpw_ref, gps_ref):
        del dz_in
        j = pl.program_id(0)

        @pl.when(pl.program_id(1) == 0)
        def _():
            gpw_ref[...] = jnp.zeros_like(gpw_ref)
            gps_ref[...] = jnp.zeros_like(gps_ref)

        row = lax.broadcasted_iota(jnp.int32, (s, c), 0)
        for gi, win in enumerate(POOL_WINDOWS):
            @pl.when(j == gi)
            def _(win=win):
                pb = _pool_fwd(zp[...].astype(F32), win, row).astype(BF16)
                plin = _dot(pb, pw[...])
                dps = dp_ref[...].astype(F32)
                gps_ref[...] += jnp.sum(dps * plin, axis=0, keepdims=True)
                dplb = (dps * ps[...]).astype(BF16)
                gpw_ref[...] += _dot_ta(pb, dplb)
                dzo[...] = _pool_bwd(_dot_tb(dplb, pw[...]), win, row).astype(BF16)

    blocks = 3 * _nbytes((s, c), BF16) + _nbytes((c, c), BF16) + _nbytes((c, c), F32)
    return _pcall(
        body, name="pool_bwd", grid=(N_GROUPS, nseq),
        in_specs=[ANY,
                  pl.BlockSpec((None, s, c), lambda j, b: (0, b, j)),
                  pl.BlockSpec((s, c), lambda j, b: (b, j)),
                  pl.BlockSpec((None, c, c), lambda j, b: (j, 0, 0)),
                  pl.BlockSpec((1, c), lambda j, b: (0, j))],
        out_specs=[pl.BlockSpec((None, s, c), lambda j, b: (3, b, j)),
                   pl.BlockSpec((None, c, c), lambda j, b: (j, 0, 0)),
                   pl.BlockSpec((1, c), lambda j, b: (0, j))],
        out_shape=[SDS(dz.shape, BF16), SDS((N_GROUPS, c, c), F32), SDS((1, d), F32)],
        input_output_aliases={0: 0},
        sem=("parallel", "arbitrary"), blocks=blocks, temps=10 * _nbytes((s, c), F32), comm=comm,
    )(dz, dpq, z, pool_w, pool_scale)


def _dz_plane(zb):
    return jnp.where(zb < 4, (zb + 3) % 4, zb)


def _wgrad_in(h1, dz, nsh, comm=None):
    t, d = h1.shape
    ws = N_SPLITS * d // nsh
    kb = _tile(math.gcd(d, ws), 512, LANES)
    npl = d // kb
    nps = ws // kb
    tk = _tile(t, WGRAD_TOKENS_WIDE, SUBLANES_BF16)
    nk = t // tk

    def body(a_ref, b_ref, o_ref, *acc):
        part = _dot_ta(a_ref[...], b_ref[...])
        if nk == 1:
            o_ref[...] = part.astype(BF16)
        else:
            _acc_over(pl.program_id(1), nk, part, acc[0], o_ref)

    blocks = _nbytes((tk, d), BF16) + _nbytes((tk, kb), BF16) + _nbytes((d, kb), BF16)
    return _pcall(
        body, name="wgrad_in", grid=(N_SPLITS * npl, nk),
        in_specs=[pl.BlockSpec((tk, d), lambda cb, k: (k, 0), pipeline_mode=pl.Buffered(1) if nk == 1 else None),
                  pl.BlockSpec((None, tk, kb), lambda cb, k: (_dz_plane(cb // npl), k, cb % npl))],
        out_specs=[pl.BlockSpec((None, d, kb), lambda cb, k: (cb // nps, 0, cb % nps))],
        out_shape=[SDS((nsh, d, ws), BF16)],
        scratch_shapes=[] if nk == 1 else [pltpu.VMEM((d, kb), F32)],
        sem=("parallel", "arbitrary"), blocks=blocks, temps=2 * _nbytes((d, kb), F32), comm=comm,
    )(h1, dz)


def _mixer_bwd_dx(dz, w_in, x, dx1, g_mix, comm=None):
    npln, t, d = dz.shape
    nsh, _, ws = w_in.shape
    tm = _tile(t, 256, SUBLANES_BF16)
    kb = _tile(math.gcd(d, ws), 512, LANES)
    npl = d // kb
    nps = ws // kb

    def body(dz_ref, w_ref, x_ref, dx1_ref, g_ref, dx_ref, gg_ref):
        @pl.when(pl.program_id(0) == 0)
        def _():
            gg_ref[...] = jnp.zeros_like(gg_ref)

        dh = None
        for cb in range(npln * npl):
            zb = cb // npl
            plane = (zb + 3) % 4 if zb < 4 else zb
            part = _dot_tb(dz_ref[plane, :, (cb % npl) * kb:(cb % npl + 1) * kb],
                           w_ref[cb // nps, :, (cb % nps) * kb:(cb % nps + 1) * kb])
            dh = part if dh is None else dh + part
        xh, inv = _rms_fwd(x_ref[...])
        gg_ref[...] += jnp.sum(dh * xh, axis=0, keepdims=True)
        dx_ref[...] = dx1_ref[...] + _rms_bwd(dh, xh, inv, g_ref[...])

    blocks = _nbytes((npln, tm, d), BF16) + 3 * _nbytes((tm, d), F32)
    return _pcall(
        body, name="mixer_bwd_dx", grid=(t // tm,),
        in_specs=[pl.BlockSpec((npln, tm, d), lambda i: (0, i, 0)),
                  pl.BlockSpec((nsh, d, ws), lambda i: (0, 0, 0), pipeline_mode=pl.Buffered(1)),
                  pl.BlockSpec((tm, d), lambda i: (i, 0)),
                  pl.BlockSpec((tm, d), lambda i: (i, 0)),
                  pl.BlockSpec((1, d), lambda i: (0, 0))],
        out_specs=[pl.BlockSpec((tm, d), lambda i: (i, 0)),
                   pl.BlockSpec((1, d), lambda i: (0, 0))],
        out_shape=[SDS((t, d), F32), SDS((1, d), F32)],
        sem=("arbitrary",), blocks=blocks, temps=_nbytes(w_in.shape, BF16) + 8 * _nbytes((tm, d), F32), comm=comm,
    )(dz, w_in, x, dx1, g_mix)


N_BIG = 5
SHARD_MAJOR = (0, 2)
ROWS_DIM1 = (1, 4)


def _ds(start, size, align):
    if isinstance(start, int):
        return pl.ds(start, size)
    return pl.ds(pl.multiple_of(start, align), size)


def _piece(a, ref, k, h):
    if a in SHARD_MAJOR:
        r = ref.shape[1] // 2
        return ref.at[k, _ds(h * r, r, SUBLANES_BF16), :]
    if a in ROWS_DIM1:
        r = ref.shape[1] // 8
        return ref.at[:, _ds((2 * k + h) * r, r, SUBLANES_BF16), :]
    r = ref.shape[0] // 8
    return ref.at[_ds((2 * k + h) * r, r, SUBLANES_BF16), :]


def _half(a, ref, h):
    if a in ROWS_DIM1:
        r = ref.shape[1] // 2
        return ref.at[:, _ds(h * r, r, SUBLANES_BF16), :]
    r = ref.shape[0] // 2
    return ref.at[_ds(h * r, r, SUBLANES_BF16), :]


def _piece_shape(a, full_shape):
    if a in SHARD_MAJOR:
        return (full_shape[1] // 2, full_shape[2])
    if a in ROWS_DIM1:
        return (full_shape[0], full_shape[1] // 8, full_shape[2])
    return (full_shape[0] // 8, full_shape[1])


def _shard_shape(a, full_shape):
    if a in SHARD_MAJOR:
        return (full_shape[1], full_shape[2])
    if a in ROWS_DIM1:
        return (full_shape[0], full_shape[1] // 4, full_shape[2])
    return (full_shape[0] // 4, full_shape[1])


def _rows_axis(a):
    return 1 if a in ROWS_DIM1 else 0


def _piece_block(a, full_shape):
    ps = _piece_shape(a, full_shape)
    if a in SHARD_MAJOR:
        return (None,) + ps, lambda k, c: (k, c, 0)
    if a in ROWS_DIM1:
        return ps, lambda k, c: (0, 2 * k + c, 0)
    return ps, lambda k, c: (2 * k + c, 0)


def _coords():
    return lax.axis_index("x"), lax.axis_index("y"), lax.axis_index("c")


def _peer_chips(x, y):
    return [(1 - x, y), (x, 1 - y), (1 - x, 1 - y)]


def _remote(src, dst, ssem, rsem, dev):
    return pltpu.make_async_remote_copy(src_ref=src, dst_ref=dst, send_sem=ssem, recv_sem=rsem,
                                        device_id=dev, device_id_type=MESH)


def _dma_sems(*counts):
    return [pltpu.SemaphoreType.DMA((n,)) for n in counts]


def _symmetric(ins, out_shapes, sems, copies, peers, aliases=None):
    def start(cins, couts, csems):
        for cp in copies(cins, couts, csems):
            cp.start()

    def finish(cins, couts, csems):
        for cp in copies(cins, couts, csems):
            cp.wait()

    return _Comm(ins, out_shapes, sems, start, finish, peers, aliases)


def _rows_part(a, ref, part):
    if part is None:
        return ref
    p, q, n = part
    ax = _rows_axis(a)
    r = ref.shape[ax] // n
    return ref.at[tuple(pl.ds(p * r, (q - p) * r) if d == ax else slice(None) for d in range(len(ref.shape)))]


def _merge(comms):
    ins, outs, sems, aliases, spans = [], [], [], {}, []
    for cm in comms:
        spans.append((len(ins), len(outs), len(sems)))
        for i, o in cm.aliases.items():
            aliases[len(ins) + i] = len(outs) + o
        ins += cm.ins
        outs += cm.out_shapes
        sems += cm.sems

    def each(fn_name):
        def run(cins, couts, csems):
            for cm, (i0, o0, s0) in zip(comms, spans):
                fn = getattr(cm, fn_name)
                if fn is not None:
                    fn(cins[i0:i0 + len(cm.ins)], couts[o0:o0 + len(cm.out_shapes)], csems[s0:s0 + len(cm.sems)])
        return run

    return _Comm(ins, outs, sems, each("start"), each("finish"), frozenset().union(*[cm.peers for cm in comms]),
                 aliases, mid=each("mid") if any(cm.mid is not None for cm in comms) else None)


def _gather_comm(arrs, locs, full_shapes, part=None, into=None):
    n = len(arrs)

    def own(cins, couts, csems):
        x, y, c = _coords()
        j = 2 * x + y
        return [_remote(_rows_part(a, _half(a, cins[q], h), part), _rows_part(a, _piece(a, couts[q], j, h), part),
                        csems[0].at[2 * q + h], csems[1].at[2 * q + h], (x, y, 1 - c))
                for q, a in enumerate(arrs) for h in range(2)]

    def sends(cins, couts, csems):
        x, y, c = _coords()
        j = 2 * x + y
        return [_remote(_rows_part(a, _half(a, cins[q], c), part), _rows_part(a, _piece(a, couts[q], j, c), part),
                        csems[2].at[3 * q + i], csems[3].at[3 * q + i], (px, py, c))
                for q, a in enumerate(arrs) for i, (px, py) in enumerate(_peer_chips(x, y))]

    def forwards(couts, csems, half_of):
        x, y, c = _coords()
        out = []
        for q, a in enumerate(arrs):
            for i, (px, py) in enumerate(_peer_chips(x, y)):
                landed = _rows_part(a, _piece(a, couts[q], 2 * px + py, half_of(c)), part)
                out.append(_remote(landed, landed, csems[4].at[3 * q + i], csems[5].at[3 * q + i], (x, y, 1 - c)))
        return out

    def start(cins, couts, csems):
        for cp in sends(cins, couts, csems) + own(cins, couts, csems):
            cp.start()

    def finish(cins, couts, csems):
        fw = forwards(couts, csems, lambda c: c)
        for cp, f in zip(sends(cins, couts, csems), fw):
            cp.wait_recv()
            f.start()
        for f in forwards(couts, csems, lambda c: 1 - c):
            f.wait_recv()
        for cp in sends(cins, couts, csems) + fw:
            cp.wait_send()
        for cp in own(cins, couts, csems):
            cp.wait()

    ins = [locs[a] for a in arrs] + ([into[a] for a in arrs] if into else [])
    return _Comm(ins, [SDS(full_shapes[a], BF16) for a in arrs],
                 _dma_sems(2 * n, 2 * n, 3 * n, 3 * n, 3 * n, 3 * n), start, finish, CHIPS + (SIBLING,),
                 aliases={n + q: q for q in range(n)} if into else None)


def _ring_gather_comm(arrs, locs, full_shapes):
    n = len(arrs)

    def own(cins, couts, csems):
        x, y, c = _coords()
        j = 2 * x + y
        return [_remote(_half(a, cins[q], h), _piece(a, couts[q], j, h), csems[0].at[2 * q + h],
                        csems[1].at[2 * q + h], (x, y, 1 - c)) for q, a in enumerate(arrs) for h in range(2)]

    def sends(cins, couts, csems):
        x, y, c = _coords()
        j = 2 * x + y
        return [_remote(_half(a, cins[q], c), _piece(a, couts[q], j, c), csems[2].at[2 * q + i],
                        csems[3].at[2 * q + i], (px, py, c))
                for q, a in enumerate(arrs) for i, (px, py) in enumerate(_peer_chips(x, y)[:2])]

    def relays(couts, csems):
        x, y, c = _coords()
        peers = _peer_chips(x, y)
        out = []
        for q, a in enumerate(arrs):
            for r, (src_p, dst_p) in enumerate(((0, 1), (1, 0))):
                sx, sy = peers[src_p]
                rows = _rows_part(a, _piece(a, couts[q], 2 * sx + sy, c), (r, r + 1, 2))
                out.append(_remote(rows, rows, csems[6].at[2 * q + r], csems[7].at[2 * q + r], (*peers[dst_p], c)))
        return out

    def forwards(couts, csems, half_of, which):
        x, y, c = _coords()
        out = []
        for q, a in enumerate(arrs):
            for i in which:
                px, py = _peer_chips(x, y)[i]
                landed = _piece(a, couts[q], 2 * px + py, half_of(c))
                out.append(_remote(landed, landed, csems[4].at[3 * q + i], csems[5].at[3 * q + i], (x, y, 1 - c)))
        return out

    def start(cins, couts, csems):
        for cp in sends(cins, couts, csems) + own(cins, couts, csems):
            cp.start()

    def mid(cins, couts, csems):
        for cp in sends(cins, couts, csems):
            cp.wait_recv()
        for cp in relays(couts, csems) + forwards(couts, csems, lambda c: c, (0, 1)):
            cp.start()

    def finish(cins, couts, csems):
        for cp in relays(couts, csems):
            cp.wait_recv()
        fw_diag = forwards(couts, csems, lambda c: c, (2,))
        for f in fw_diag:
            f.start()
        for f in forwards(couts, csems, lambda c: 1 - c, (0, 1, 2)):
            f.wait_recv()
        for cp in (sends(cins, couts, csems) + relays(couts, csems)
                   + forwards(couts, csems, lambda c: c, (0, 1)) + fw_diag):
            cp.wait_send()
        for cp in own(cins, couts, csems):
            cp.wait()

    return _Comm([locs[a] for a in arrs], [SDS(full_shapes[a], BF16) for a in arrs],
                 _dma_sems(2 * n, 2 * n, 2 * n, 2 * n, 3 * n, 3 * n, 2 * n, 2 * n), start, finish,
                 CHIPS + (SIBLING,), mid=mid)


def _halves_comm(arrs, gbs):
    n = len(arrs)

    def copies(cins, couts, csems):
        x, y, c = _coords()
        return [_remote(_piece(a, cins[q], k, 1 - c), couts[q].at[k], csems[0].at[4 * q + k], csems[1].at[4 * q + k],
                        (x, y, 1 - c)) for q, a in enumerate(arrs) for k in range(4)]

    return _symmetric([gbs[a] for a in arrs], [SDS((4,) + _piece_shape(a, gbs[a].shape), BF16) for a in arrs],
                      _dma_sems(4 * n, 4 * n), copies, [SIBLING])


def _chips_comm(arrs, ps, part=None, into=None):
    n = len(arrs)

    def copies(cins, couts, csems):
        x, y, c = _coords()
        return [_remote(_rows_part(a, cins[q].at[2 * px + py], part), _rows_part(a, couts[q].at[i], part),
                        csems[0].at[3 * q + i], csems[1].at[3 * q + i], (px, py, c))
                for q, a in enumerate(arrs) for i, (px, py) in enumerate(_peer_chips(x, y))]

    ins = [ps[a] for a in arrs] + ([into[a] for a in arrs] if into else [])
    return _symmetric(ins, [SDS((3,) + ps[a].shape[1:], BF16) for a in arrs], _dma_sems(3 * n, 3 * n), copies, CHIPS,
                      aliases={n + q: q for q in range(n)} if into else None)


def _result_comm(arrs, gs):
    n = len(arrs)

    def copies(cins, couts, csems):
        x, y, c = _coords()
        return [_remote(_half(a, cins[q], c), _half(a, couts[q], c), csems[0].at[q], csems[1].at[q], (x, y, 1 - c))
                for q, a in enumerate(arrs)]

    return _symmetric([gs[a] for a in arrs], [SDS(gs[a].shape, F32) for a in arrs], _dma_sems(n, n), copies,
                      [SIBLING], aliases={q: q for q in range(n)})


def _add_halves(arrs, gbs, lands, c_arr, name):
    n = len(arrs)

    def body(c_ref, *refs):
        del c_ref
        for q in range(n):
            refs[2 * n + q][...] = (refs[q][...].astype(F32) + refs[n + q][...].astype(F32)).astype(BF16)

    g_specs, l_specs, o_specs, blocks = [], [], [], 0
    for a in arrs:
        bs, imap = _piece_block(a, gbs[a].shape)
        ps = _piece_shape(a, gbs[a].shape)
        g_specs.append(pl.BlockSpec(bs, lambda k, c_ref, imap=imap: imap(k, c_ref[0])))
        nd = len(ps)
        l_specs.append(pl.BlockSpec((None,) + ps, lambda k, c_ref, nd=nd: (k,) + (0,) * nd))
        o_specs.append(pl.BlockSpec((None,) + ps, lambda k, c_ref, nd=nd: (k,) + (0,) * nd))
        blocks += 3 * _nbytes(ps, BF16)
    return list(pl.pallas_call(
        body, name=name,
        grid_spec=pltpu.PrefetchScalarGridSpec(
            num_scalar_prefetch=1, grid=(4,), in_specs=g_specs + l_specs, out_specs=o_specs),
        out_shape=[SDS((4,) + _piece_shape(a, gbs[a].shape), BF16) for a in arrs],
        compiler_params=_params(("parallel",), blocks, blocks),
    )(c_arr, *[gbs[a] for a in arrs], *lands))


def _sum_chips(a, p, land, shard_shape, jc_arr, name):
    ps = land.shape[1:]
    ax = _rows_axis(a)
    rows = ps[ax]
    nsub = 2 if rows % (2 * SUBLANES_BF16) == 0 else 1
    bs = tuple(r // nsub if q == ax else r for q, r in enumerate(ps))
    nd = len(ps)

    def at_rows(v):
        return tuple(v if q == ax else 0 for q in range(nd))

    def body(jc_ref, p_ref, l_ref, o_ref):
        del jc_ref
        acc = p_ref[...].astype(F32) + l_ref[0].astype(F32)
        acc = acc + l_ref[1].astype(F32)
        o_ref[...] = acc + l_ref[2].astype(F32)

    blocks = 4 * _nbytes(bs, BF16) + _nbytes(bs, F32)
    return pl.pallas_call(
        body, name=name,
        grid_spec=pltpu.PrefetchScalarGridSpec(
            num_scalar_prefetch=1, grid=(nsub,),
            in_specs=[pl.BlockSpec((None,) + bs, lambda s, jc: (jc[0],) + at_rows(s)),
                      pl.BlockSpec((3,) + bs, lambda s, jc: (0,) + at_rows(s))],
            out_specs=pl.BlockSpec(bs, lambda s, jc: at_rows(jc[1] * nsub + s))),
        out_shape=SDS(shard_shape, F32),
        compiler_params=_params(("parallel",), blocks, 2 * _nbytes(bs, F32)),
    )(jc_arr, p, land)


def _small_comm(v):
    rows = v.shape[0]

    def copies(cins, couts, csems):
        x, y, c = _coords()
        me = 4 * x + 2 * y + c
        out = [pltpu.make_async_copy(cins[0], couts[0].at[me], csems[0].at[0])]
        for dlt in range(1, 8):
            px = 1 - x if (dlt >> 2) & 1 else x
            py = 1 - y if (dlt >> 1) & 1 else y
            pc = 1 - c if dlt & 1 else c
            out.append(_remote(cins[0], couts[0].at[me], csems[1].at[dlt - 1], csems[2].at[dlt - 1], (px, py, pc)))
        return out

    return _symmetric([v], [SDS((8, rows, LANES), F32)], _dma_sems(1, 7, 7), copies, EVERYONE)


def _sum8(slots, name):
    def body(s_ref, o_ref):
        acc = s_ref[0]
        for i in range(1, 8):
            acc = acc + s_ref[i]
        o_ref[...] = acc

    return pl.pallas_call(
        body, name=name,
        in_specs=[pl.BlockSpec(memory_space=pltpu.VMEM)], out_specs=pl.BlockSpec(memory_space=pltpu.VMEM),
        out_shape=SDS(slots.shape[1:], F32),
    )(slots)


def _adamw(w, g, m, v, name, g_plane=None):
    rows, cols = w.shape
    tr = _tile(rows, max(SUBLANES_F32, (256 * 1024 // cols) // SUBLANES_F32 * SUBLANES_F32), SUBLANES_F32)

    def body(w_ref, g_ref, m_ref, v_ref, go_ref, d_ref, mo_ref, vo_ref):
        gr = g_ref[...]
        mn = ADAM_B1 * m_ref[...] + (1.0 - ADAM_B1) * gr
        vn = ADAM_B2 * v_ref[...] + (1.0 - ADAM_B2) * (gr * gr)
        m_hat = mn / (1.0 - ADAM_B1 ** ADAM_STEP)
        v_hat = vn / (1.0 - ADAM_B2 ** ADAM_STEP)
        d_ref[...] = -ADAM_LR * (m_hat / (jnp.sqrt(v_hat) + ADAM_EPS) + ADAM_WD * w_ref[...])
        go_ref[...] = gr
        mo_ref[...] = mn
        vo_ref[...] = vn

    spec = pl.BlockSpec((tr, cols), lambda i: (i, 0))
    g_spec = spec if g_plane is None else pl.BlockSpec((None, tr, cols), lambda i: (g_plane, i, 0))
    return pl.pallas_call(
        body, name=name, grid=(rows // tr,),
        in_specs=[spec, g_spec, spec, spec], out_specs=[spec, spec, spec, spec],
        out_shape=[SDS((rows, cols), F32)] * 4,
        compiler_params=_params(("parallel",), 8 * _nbytes((tr, cols), F32), 4 * _nbytes((tr, cols), F32)),
    )(w, g, m, v)


def _pack(parts):
    rows = []
    for p in parts:
        r = p.reshape(-1, LANES)
        pad = (-r.shape[0]) % SUBLANES_F32
        if pad:
            r = jnp.pad(r, ((0, pad), (0, 0)))
        rows.append(r)
    return jnp.concatenate(rows, axis=0)


def _unpack(packed, shapes):
    out, at = [], 0
    for s in shapes:
        n = 1
        for q in s:
            n *= q
        r = n // LANES
        out.append(packed[at:at + r].reshape(s))
        at += r + (-r) % SUBLANES_F32
    return out


def kernel(x, norm_mix, w_in, pool_w, pool_scale, w_pool_proj, conv_w, w_conv_out, w_o, norm_ffn, w_up, ffn_conv_w, ffn_conv_b, w_down, norm_final, loss_target, m_norm_mix, m_w_in, m_pool_w, m_pool_scale, m_w_pool_proj, m_conv_w, m_w_conv_out, m_w_o, m_norm_ffn, m_w_up, m_ffn_conv_w, m_ffn_conv_b, m_w_down, m_norm_final, v_norm_mix, v_w_in, v_pool_w, v_pool_scale, v_w_pool_proj, v_conv_w, v_w_conv_out, v_w_o, v_norm_ffn, v_w_up, v_ffn_conv_w, v_ffn_conv_b, v_w_down, v_norm_final):
    nseq, seq, d = x.shape
    t = nseq * seq
    f = w_down.shape[1] * 4
    c = d // N_GROUPS
    xy = lax.axis_index("x") * 2 + lax.axis_index("y")
    c_arr = lax.axis_index("c").astype(jnp.int32).reshape(1)
    jc_arr = jnp.stack([xy, lax.axis_index("c")]).astype(jnp.int32)
    nsh = 4
    zero = jnp.zeros((), jnp.int32)

    locs = [w_in[0].astype(BF16),
            jnp.stack([w_pool_proj[0], w_conv_out[0], w_o[0]]).astype(BF16),
            w_up[0].astype(BF16), w_down[0].astype(BF16), pool_w[0].astype(BF16)]
    full_shapes = [(nsh, d, N_SPLITS * d // nsh), (3, d, d), (nsh, d, 2 * f // nsh), (f, d), (N_GROUPS, c, c)]

    cw_pad = lax.dynamic_update_slice(jnp.zeros((3, d), F32), conv_w[0], (zero, xy * (d // 4)))
    fw_pad = lax.dynamic_update_slice(jnp.zeros((3, 2 * f), F32), ffn_conv_w[0], (zero, xy * (f // 2)))
    small_w = _pack([cw_pad, fw_pad]) * 0.5

    x2d = x.reshape(t, d)
    tgt = loss_target.reshape(t, d)
    ax, ay = lax.axis_index("x"), lax.axis_index("y")
    order = jnp.stack([xy, 2 * (1 - ax) + ay, 2 * ax + 1 - ay, 2 * (1 - ax) + 1 - ay]).astype(jnp.int32)
    (z, h1, w_in_f), (pool_w_f, w3_f, slots_w) = _fwd_in(
        x2d, norm_mix, locs[0], order,
        _merge([_gather_comm([4], locs, full_shapes), _gather_comm([1], locs, full_shapes, part=(0, 1, 2)),
                _small_comm(small_w)]))
    conv_w_f, ffn_cw_f = _unpack(_sum8(slots_w, "sum8_weights"), [(3, d), (3, 2 * f)])
    ffn_cw_p = ffn_cw_f.reshape(3, 2, f).transpose(1, 0, 2)
    ffn_cb_p = ffn_conv_b.reshape(2, 1, f)
    (lhs3,), (w3_f,) = _mixer_mid_fwd(z, pool_w_f, pool_scale, conv_w_f, nseq,
                                      _gather_comm([1], locs, full_shapes, part=(1, 2, 2), into={1: w3_f}))
    (lhs3, ypc, x1, h2), (w_up_f,) = _mixer_out(lhs3, z, x2d, w3_f, norm_ffn,
                                                _ring_gather_comm([2], locs, full_shapes))
    (u0,), (w_down_f,) = _ffn_up(h2, w_up_f, f, _gather_comm([3], locs, full_shapes))
    act, ua = _ffn_mid_fwd(u0, ffn_cw_p, ffn_cb_p, nseq)
    dx2, dx2b, loss11, g_norm_final = _ffn_down_loss(act, w_down_f, x1, tgt, norm_final.reshape(1, d))

    gbs, lands, ps, lands2, rs = {}, {}, {}, {}, {}
    tn_up = _tile(2 * f // nsh, 1408, LANES)
    npp = f // tn_up

    def add(arrs, name):
        for a, p in zip(arrs, _add_halves(arrs, gbs, [lands[a] for a in arrs], c_arr, name)):
            ps[a] = p

    def summed(a):
        rs[a] = _sum_chips(a, ps[a], lands2[a], _shard_shape(a, full_shapes[a]), jc_arr, "sum_chips_%d" % a)

    (gbs[3],), _ = _wgrad(act, dx2b, "wgrad_down", tr=tn_up, tn=d)
    (da,), (lands[3],) = _ffn_bwd_da(dx2b, w_down_f, _halves_comm([3], gbs))
    add([3], "add_halves_down")
    (du0, g_ffn_cw_p, g_ffn_cb_p), (lands2[3],) = _ffn_mid_bwd(da, u0, ua, ffn_cw_p, nseq, _chips_comm([3], ps))
    summed(3)
    (gbs[2],), (rs[3],) = _wgrad(h2, du0, "wgrad_up", tr=d, tn=tn_up, b_plane_of=lambda n: (n // npp, n % npp),
                                 out_shards=nsh, comm=_result_comm([3], rs))
    (dx1, rhs3, g_norm_ffn), (lands[2],) = _ffn_bwd_dx1(du0, w_up_f, x1, dx2, norm_ffn, 3, _halves_comm([2], gbs))
    add([2], "add_halves_up")
    (rhs3, dz, dpq), (lands2[2],) = _mixer_bwd(rhs3, z, ypc, w3_f, _chips_comm([2], ps, part=(0, 1, 2)))
    (gbs[1],), (lands2[2],) = _wgrad3(lhs3, rhs3, _chips_comm([2], ps, part=(1, 2, 2), into=lands2))
    summed(2)
    (dz, g_conv_w), (lands[1], rs[2]) = _conv_bwd(dz, dpq, z, conv_w_f, nseq,
                                                  _merge([_halves_comm([1], gbs), _result_comm([2], rs)]))
    add([1], "add_halves_sq3")
    (dz, g_pool_w, g_pool_scale), _ = _pool_bwd_call(dz, dpq, z, pool_w_f, pool_scale, nseq)
    gbs[4] = g_pool_w.astype(BF16)
    (gbs[0],), (lands2[1],) = _wgrad_in(h1, dz, nsh, _chips_comm([1], ps))
    summed(1)
    lands[0], lands[4] = _run_comm(_halves_comm([0, 4], gbs), "exchange_halves_in")
    add([0, 4], "add_halves_in")
    g_ffn_cw = g_ffn_cw_p.transpose(1, 0, 2).reshape(3, 2 * f)
    small_a = _pack([g_pool_scale, g_norm_ffn, g_ffn_cb_p.reshape(1, 2 * f), g_norm_final.reshape(d), g_conv_w,
                     g_ffn_cw, jnp.pad(loss11, ((0, SUBLANES_F32 - 1), (0, LANES - 1)))])
    (grad_x, g_norm_mix), (lands2[0], lands2[4], rs[1], slots_a) = _mixer_bwd_dx(
        dz, w_in_f, x2d, dx1, norm_mix,
        _merge([_chips_comm([0, 4], ps), _result_comm([1], rs), _small_comm(small_a)]))
    summed(0)
    summed(4)
    rs[0], rs[4], slots_b = _run_comm(_merge([_result_comm([0, 4], rs), _small_comm(_pack([g_norm_mix]))]),
                                      "exchange_result_in")
    shapes_a = [(1, d), (1, d), (1, 2 * f), (d,), (3, d), (3, 2 * f), (SUBLANES_F32, LANES)]
    gs_pool_scale, gs_norm_ffn, gs_ffn_cb, gs_norm_final, gs_conv_w, gs_ffn_cw, loss_blk = _unpack(
        _sum8(slots_a, "sum8_grads"), shapes_a)
    (gs_norm_mix,) = _unpack(_sum8(slots_b, "sum8_norm_mix"), [(1, d)])
    gs_conv_w = lax.dynamic_slice(gs_conv_w, (zero, xy * (d // 4)), (3, d // 4))
    gs_ffn_cw = lax.dynamic_slice(gs_ffn_cw, (zero, xy * (f // 2)), (3, f // 2))

    def upd(w, g, m, v, name, g_plane=None):
        shape = w.shape
        rows = 1
        for q in shape[:-1]:
            rows *= q
        g2 = g if g_plane is not None else g.reshape(rows, shape[-1])
        outs = _adamw(w.reshape(rows, shape[-1]), g2, m.reshape(rows, shape[-1]), v.reshape(rows, shape[-1]),
                      name, g_plane)
        return [o.reshape(shape) for o in outs]

    res = {
        "w_in": upd(w_in, rs[0], m_w_in, v_w_in, "adamw_w_in"),
        "pool_w": upd(pool_w, rs[4], m_pool_w, v_pool_w, "adamw_pool_w"),
        "w_pool_proj": upd(w_pool_proj, rs[1], m_w_pool_proj, v_w_pool_proj, "adamw_w_pool_proj", 0),
        "w_conv_out": upd(w_conv_out, rs[1], m_w_conv_out, v_w_conv_out, "adamw_w_conv_out", 1),
        "w_o": upd(w_o, rs[1], m_w_o, v_w_o, "adamw_w_o", 2),
        "w_up": upd(w_up, rs[2], m_w_up, v_w_up, "adamw_w_up"),
        "w_down": upd(w_down, rs[3], m_w_down, v_w_down, "adamw_w_down"),
    }

    small_names = ["norm_mix", "pool_scale", "norm_ffn", "ffn_conv_b", "norm_final", "conv_w", "ffn_conv_w"]
    small_ws = [norm_mix, pool_scale, norm_ffn, ffn_conv_b, norm_final, conv_w, ffn_conv_w]
    small_ms = [m_norm_mix, m_pool_scale, m_norm_ffn, m_ffn_conv_b, m_norm_final, m_conv_w, m_ffn_conv_w]
    small_vs = [v_norm_mix, v_pool_scale, v_norm_ffn, v_ffn_conv_b, v_norm_final, v_conv_w, v_ffn_conv_w]
    small_gs = [gs_norm_mix, gs_pool_scale, gs_norm_ffn, gs_ffn_cb, gs_norm_final, gs_conv_w, gs_ffn_cw]
    _, sd, sm, sv = _adamw(_pack(small_ws), _pack(small_gs), _pack(small_ms), _pack(small_vs), "adamw_small")
    shapes = [w.shape for w in small_ws]
    sd, sm, sv = _unpack(sd, shapes), _unpack(sm, shapes), _unpack(sv, shapes)
    for i, nm in enumerate(small_names):
        res[nm] = [small_gs[i].reshape(shapes[i]), sd[i], sm[i], sv[i]]

    order = ["norm_mix", "w_in", "pool_w", "pool_scale", "w_pool_proj", "conv_w", "w_conv_out", "w_o", "norm_ffn",
             "w_up", "ffn_conv_w", "ffn_conv_b", "w_down", "norm_final"]
    return (loss_blk[0, 0], grad_x.reshape(x.shape), *[res[n][0] for n in order], *[res[n][1] for n in order],
            *[res[n][2] for n in order], *[res[n][3] for n in order])
```

```python
import math

import jax
import jax.numpy as jnp
from jax import lax
from jax.experimental import pallas as pl
from jax.experimental.pallas import tpu as pltpu

F32 = jnp.float32
BF16 = jnp.bfloat16
SDS = jax.ShapeDtypeStruct
MESH = pl.DeviceIdType.MESH

RMS_EPS = 1e-6
POOL_WINDOWS = (2, 4, 8, 16)
N_GROUPS = len(POOL_WINDOWS)
N_SPLITS = 6

ADAM_LR = 0.001
ADAM_B1 = 0.9
ADAM_B2 = 0.999
ADAM_EPS = 1e-08
ADAM_WD = 0.01
ADAM_STEP = 10

LANES = 128
SUBLANES_F32 = 8
SUBLANES_BF16 = 16
VMEM_BYTES = 64 * 1024 * 1024
VMEM_CAP = VMEM_BYTES - 8 * 1024 * 1024
VMEM_FLOOR = 16 * 1024 * 1024

ANY = pl.BlockSpec(memory_space=pl.ANY)


def _tile(dim, pref, align):
    if dim <= pref:
        return dim
    t = (pref // align) * align
    while t >= align:
        if dim % t == 0:
            return t
        t -= align
    return dim


def _nbytes(shape, dtype):
    n = 1
    for s in shape:
        n *= s
    return n * jnp.dtype(dtype).itemsize


def _params(sem, block_bytes, temp_bytes=0, collective_id=None):
    need = 2 * block_bytes + temp_bytes + 4 * 1024 * 1024
    return pltpu.CompilerParams(dimension_semantics=sem, collective_id=collective_id,
                                vmem_limit_bytes=int(min(max(need, VMEM_FLOOR), VMEM_CAP)))


SIBLING = (0, 0, 1)
CHIPS = ((1, 0, 0), (0, 1, 0), (1, 1, 0))
EVERYONE = tuple((a, b, c) for a in range(2) for b in range(2) for c in range(2) if a + b + c)
PEER_SETS = (frozenset([SIBLING]), frozenset(CHIPS), frozenset(CHIPS + (SIBLING,)), frozenset(EVERYONE))
MID_AT = 0.75


def _collective_id(peers):
    return PEER_SETS.index(frozenset(peers))


def _handshake(peers):
    x, y, c = lax.axis_index("x"), lax.axis_index("y"), lax.axis_index("c")
    bar = pltpu.get_barrier_semaphore()
    for fx, fy, fc in sorted(peers):
        dev = (1 - x if fx else x, 1 - y if fy else y, 1 - c if fc else c)
        pl.semaphore_signal(bar, inc=1, device_id=dev, device_id_type=MESH)
    pl.semaphore_wait(bar, len(peers))


class _Comm:
    def __init__(self, ins, out_shapes, sems, start, finish, peers, aliases=None, mid=None):
        self.ins = list(ins)
        self.out_shapes = list(out_shapes)
        self.sems = list(sems)
        self.start = start
        self.finish = finish
        self.mid = mid
        self.peers = frozenset(peers)
        self.aliases = dict(aliases or {})


def _pcall(body, *, name, grid, in_specs, out_specs, out_shape, sem, blocks, temps=0, scratch_shapes=(),
           input_output_aliases=None, comm=None):
    in_specs = list(in_specs)
    out_specs = list(out_specs)
    out_shape = list(out_shape)
    scratch_shapes = list(scratch_shapes)
    aliases = dict(input_output_aliases or {})
    n_in, n_out, n_scr = len(in_specs), len(out_shape), len(scratch_shapes)
    if comm is None:
        call = pl.pallas_call(
            body, name=name, grid=grid, in_specs=in_specs, out_specs=out_specs, out_shape=out_shape,
            scratch_shapes=scratch_shapes, input_output_aliases=aliases,
            compiler_params=_params(sem, blocks, temps))
        return lambda *args: (list(call(*args)), [])

    nci, nco = len(comm.ins), len(comm.out_shapes)
    n_steps = 1
    for g in grid:
        n_steps *= g

    def hosted(*refs):
        ins = refs[:n_in]
        cins = refs[n_in:n_in + nci]
        outs = refs[n_in + nci:n_in + nci + n_out]
        couts = refs[n_in + nci + n_out:n_in + nci + n_out + nco]
        scr = refs[n_in + nci + n_out + nco:n_in + nci + n_out + nco + n_scr]
        csems = refs[n_in + nci + n_out + nco + n_scr:]
        first = None
        last = None
        step = 0
        for q, g in enumerate(grid):
            pid = pl.program_id(q)
            first = (pid == 0) if first is None else first & (pid == 0)
            last = (pid == g - 1) if last is None else last & (pid == g - 1)
            step = step * g + pid

        @pl.when(first)
        def _():
            _handshake(comm.peers)
            comm.start(cins, couts, csems)

        if comm.mid is not None:
            @pl.when(step == int(MID_AT * n_steps))
            def _():
                comm.mid(cins, couts, csems)

        body(*ins, *outs, *scr)

        @pl.when(last)
        def _():
            comm.finish(cins, couts, csems)

    for i, o in comm.aliases.items():
        aliases[n_in + i] = n_out + o
    call = pl.pallas_call(
        hosted, name=name, grid=grid, in_specs=in_specs + [ANY] * nci, out_specs=out_specs + [ANY] * nco,
        out_shape=out_shape + comm.out_shapes, scratch_shapes=scratch_shapes + comm.sems,
        input_output_aliases=aliases,
        compiler_params=_params(("arbitrary",) * len(grid), blocks, temps, _collective_id(comm.peers)))

    def run(*args):
        res = call(*args, *comm.ins)
        return list(res[:n_out]), list(res[n_out:])

    return run


def _run_comm(comm, name):
    def body(*refs):
        nci, nco = len(comm.ins), len(comm.out_shapes)
        cins, couts, csems = refs[:nci], refs[nci:nci + nco], refs[nci + nco:]
        _handshake(comm.peers)
        comm.start(cins, couts, csems)
        if comm.mid is not None:
            comm.mid(cins, couts, csems)
        comm.finish(cins, couts, csems)

    return list(pl.pallas_call(
        body, name=name, in_specs=[ANY] * len(comm.ins), out_specs=[ANY] * len(comm.out_shapes),
        out_shape=comm.out_shapes, scratch_shapes=comm.sems, input_output_aliases=comm.aliases,
        compiler_params=pltpu.CompilerParams(collective_id=_collective_id(comm.peers)),
    )(*comm.ins))


def _dot(a, b):
    return jnp.dot(a, b, preferred_element_type=F32)


def _dot_tb(a, b):
    return lax.dot_general(a, b, (((1,), (1,)), ((), ())), preferred_element_type=F32)


def _dot_ta(a, b):
    return lax.dot_general(a, b, (((0,), (0,)), ((), ())), preferred_element_type=F32)


def _rms_fwd(x):
    inv = lax.rsqrt(jnp.mean(x * x, axis=-1, keepdims=True) + RMS_EPS)
    return x * inv, inv


def _rms_bwd(dy, xhat, inv, g):
    gd = dy * g
    return inv * (gd - xhat * jnp.mean(gd * xhat, axis=-1, keepdims=True))


def _sigmoid(x):
    return 1.0 / (1.0 + jnp.exp(-x))


def _shift_down(x, k, row):
    return jnp.where(row >= k, pltpu.roll(x, k, 0), 0.0)


def _shift_up(x, k, row):
    s = x.shape[0]
    return jnp.where(row < s - k, pltpu.roll(x, s - k, 0), 0.0)


def _pool_fwd(u, win, row):
    s = u
    k = 1
    while k < win:
        s = s + _shift_down(s, k, row)
        k *= 2
    cnt = jnp.minimum(row + 1, win).astype(F32)
    return s / cnt - u


def _pool_bwd(dp, win, row):
    cnt = jnp.minimum(row + 1, win).astype(F32)
    s = dp / cnt
    k = 1
    while k < win:
        s = s + _shift_up(s, k, row)
        k *= 2
    return s - dp


def _acc_over(k, nk, part, acc, o_ref):
    @pl.when(k == 0)
    def _():
        acc[...] = part

    @pl.when(k > 0)
    def _():
        acc[...] += part

    @pl.when(k == nk - 1)
    def _():
        o_ref[...] = acc[...].astype(o_ref.dtype)


def _fwd_in(x, g, w_loc, order, comm):
    t, d = x.shape
    ws = w_loc.shape[1]
    nsh = order.shape[0]
    assert nsh == 4, "the shard walk below is written for the 2 x 2 chips of the mesh"
    tm = _tile(t, 1024, SUBLANES_BF16)
    ni = t // tm
    nci, nco = len(comm.ins), len(comm.out_shapes)
    all_peers = comm.peers | frozenset(CHIPS + (SIBLING,))

    def body(order_ref, x_ref, g_ref, loc_ref, *rest):
        del order_ref
        cins = rest[:nci]
        z_ref, h_ref, full_ref = rest[nci:nci + 3]
        couts = rest[nci + 3:nci + 3 + nco]
        (hs, wbuf, wsem, own_s, own_r, snd_s, snd_r, fwd_s, fwd_r, rly_s, rly_r) = rest[nci + 3 + nco:nci + 14 + nco]
        csems = rest[nci + 14 + nco:]
        j = pl.program_id(0)
        i = pl.program_id(1)
        x_, y_, c_ = _coords()
        own = 2 * x_ + y_
        sib = (x_, y_, 1 - c_)
        peers = _peer_chips(x_, y_)

        def sends():
            return [_remote(_half(0, loc_ref, c_), _piece(0, full_ref, own, c_), snd_s.at[p], snd_r.at[p], (px, py, c_))
                    for p, (px, py) in enumerate(peers[:2])]

        def relays():
            out = []
            for q, (src_p, dst_p) in enumerate(((0, 1), (1, 0))):
                sx, sy = peers[src_p]
                part = _rows_part(0, _piece(0, full_ref, 2 * sx + sy, c_), (q, q + 1, 2))
                out.append(_remote(part, part, rly_s.at[q], rly_r.at[q], (*peers[dst_p], c_)))
            return out

        def owns():
            return [_remote(_half(0, loc_ref, h), _piece(0, full_ref, own, h), own_s.at[h], own_r.at[h], sib)
                    for h in range(2)]

        def forward(p, half):
            px, py = peers[p]
            landed = _piece(0, full_ref, 2 * px + py, half)
            return _remote(landed, landed, fwd_s.at[p], fwd_r.at[p], sib)

        def load(src, slot):
            return pltpu.make_async_copy(src, wbuf.at[slot], wsem.at[slot])

        @pl.when((j == 0) & (i == 0))
        def _():
            _handshake(all_peers)
            for cp in sends() + owns():
                cp.start()
            load(loc_ref, 0).start()

        @pl.when(j == 0)
        def _():
            xh, _ = _rms_fwd(x_ref[...])
            h = (xh * g_ref[...]).astype(BF16)
            hs[pl.ds(pl.multiple_of(i * tm, tm), tm), :] = h
            h_ref[...] = h

        slot = j % 2

        @pl.when(i == 0)
        def _():
            load(loc_ref, slot).wait()

        z_ref[...] = _dot(hs[pl.ds(pl.multiple_of(i * tm, tm), tm), :], wbuf[slot]).astype(BF16)

        def load_shard(p, into):
            px, py = peers[p]
            forward(p, 1 - c_).wait_recv()
            load(full_ref.at[2 * px + py], into).start()

        @pl.when((j == 0) & (i == ni - 1))
        def _():
            for cp in sends():
                cp.wait_recv()
            for cp in relays() + [forward(0, c_), forward(1, c_)]:
                cp.start()
            load_shard(0, 1)
            comm.start(cins, couts, csems)

        @pl.when((j == 1) & (i == 0))
        def _():
            load_shard(1, 0)

        @pl.when((j == 2) & (i == max(ni - 2, 0)))
        def _():
            for cp in relays():
                cp.wait_recv()
            forward(2, c_).start()
            load_shard(2, 1)

        @pl.when((j == nsh - 1) & (i == ni - 1))
        def _():
            for cp in sends() + relays() + [forward(p, c_) for p in range(nsh - 1)]:
                cp.wait_send()
            for cp in owns():
                cp.wait()
            comm.finish(cins, couts, csems)

    last = ni - 1
    blocks = _nbytes((tm, d), F32) + _nbytes((tm, ws), BF16) + _nbytes((tm, d), BF16)
    scratch = _nbytes((t, d), BF16) + 2 * _nbytes((d, ws), BF16)
    res = pl.pallas_call(
        body, name="fwd_in",
        grid_spec=pltpu.PrefetchScalarGridSpec(
            num_scalar_prefetch=1, grid=(nsh, ni),
            in_specs=[pl.BlockSpec((tm, d), lambda j, i, o: (jnp.where(j == 0, i, last), 0)),
                      pl.BlockSpec((1, d), lambda j, i, o: (0, 0)), ANY] + [ANY] * nci,
            out_specs=[pl.BlockSpec((tm, ws), lambda j, i, o: (i, o[j])),
                       pl.BlockSpec((tm, d), lambda j, i, o: (jnp.where(j == 0, i, last), 0)), ANY] + [ANY] * nco,
            scratch_shapes=[pltpu.VMEM((t, d), BF16), pltpu.VMEM((2, d, ws), BF16)]
            + _dma_sems(2, 2, 2, 2, 2, nsh - 1, nsh - 1, 2, 2) + comm.sems),
        out_shape=[SDS((t, nsh * ws), BF16), SDS((t, d), BF16), SDS((nsh, d, ws), BF16)] + comm.out_shapes,
        input_output_aliases={4 + i: 3 + o for i, o in comm.aliases.items()},
        compiler_params=_params(("arbitrary", "arbitrary"), blocks, scratch + 3 * _nbytes((tm, d), F32),
                                _collective_id(all_peers)),
    )(order, x, g, w_loc, *comm.ins)
    return list(res[:3]), list(res[3:])


def _mixer_mid_fwd(z, pool_w, pool_scale, conv_w, nseq, comm=None):
    t = z.shape[0]
    d = pool_scale.shape[1]
    s = t // nseq
    c = d // N_GROUPS

    def body(zp, zb, zc, zv, pw, ps, cw, o):
        j = pl.program_id(1)
        row = lax.broadcasted_iota(jnp.int32, (s, c), 0)
        for gi, win in enumerate(POOL_WINDOWS):
            @pl.when(j == gi)
            def _(win=win):
                pooled = _pool_fwd(zp[...].astype(F32), win, row)
                o[0] = (_dot(pooled.astype(BF16), pw[...]) * ps[...]).astype(BF16)

        cv = zc[...].astype(F32) * zv[...].astype(F32)
        cc = (cw[pl.ds(2, 1), :] * cv + cw[pl.ds(1, 1), :] * _shift_down(cv, 1, row)
              + cw[pl.ds(0, 1), :] * _shift_down(cv, 2, row))
        o[1] = (zb[...].astype(F32) * cc).astype(BF16)

    blocks = 4 * _nbytes((s, c), BF16) + _nbytes((c, c), BF16) + _nbytes((2, s, c), BF16)
    return _pcall(
        body, name="mixer_mid_fwd", grid=(nseq, N_GROUPS),
        in_specs=[pl.BlockSpec((s, c), lambda b, j: (b, j)),
                  pl.BlockSpec((s, c), lambda b, j: (b, N_GROUPS + j)),
                  pl.BlockSpec((s, c), lambda b, j: (b, 2 * N_GROUPS + j)),
                  pl.BlockSpec((s, c), lambda b, j: (b, 3 * N_GROUPS + j)),
                  pl.BlockSpec((None, c, c), lambda b, j: (j, 0, 0)),
                  pl.BlockSpec((1, c), lambda b, j: (0, j)),
                  pl.BlockSpec((3, c), lambda b, j: (0, j))],
        out_specs=[pl.BlockSpec((2, s, c), lambda b, j: (0, b, j))],
        out_shape=[SDS((3, t, d), BF16)],
        sem=("parallel", "parallel"), blocks=blocks, temps=8 * _nbytes((s, c), F32), comm=comm,
    )(z, z, z, z, pool_w, pool_scale, conv_w)


def _mixer_out(lhs3, z, x, w3, g_ffn, comm=None):
    t, d = x.shape
    tm = _tile(t, 512, SUBLANES_BF16)

    def body(pq, zgp, zgc, x_ref, w_ref, g_ref, mrg, ypc, x1o, h2o):
        yp = _dot(pq[0], w_ref[0])
        yc = _dot(pq[1], w_ref[1])
        m = _sigmoid(zgp[...].astype(F32)) * yp + _sigmoid(zgc[...].astype(F32)) * yc
        mb = m.astype(BF16)
        x1 = x_ref[...] + _dot(mb, w_ref[2])
        ypc[0] = yp.astype(BF16)
        ypc[1] = yc.astype(BF16)
        mrg[...] = mb
        x1o[...] = x1
        xh, _ = _rms_fwd(x1)
        h2o[...] = (xh * g_ref[...]).astype(BF16)

    blocks = (_nbytes((2, tm, d), BF16) * 2 + _nbytes((tm, d), BF16) * 4 + _nbytes((tm, d), F32) * 2
              + _nbytes((3, d, d), BF16))
    return _pcall(
        body, name="mixer_out", grid=(t // tm,),
        in_specs=[pl.BlockSpec((2, tm, d), lambda i: (0, i, 0)),
                  pl.BlockSpec((tm, d), lambda i: (i, 4)),
                  pl.BlockSpec((tm, d), lambda i: (i, 5)),
                  pl.BlockSpec((tm, d), lambda i: (i, 0)),
                  pl.BlockSpec((3, d, d), lambda i: (0, 0, 0)),
                  pl.BlockSpec((1, d), lambda i: (0, 0))],
        out_specs=[pl.BlockSpec((None, tm, d), lambda i: (2, i, 0)),
                   pl.BlockSpec((2, tm, d), lambda i: (0, i, 0)),
                   pl.BlockSpec((tm, d), lambda i: (i, 0)),
                   pl.BlockSpec((tm, d), lambda i: (i, 0))],
        out_shape=[SDS(lhs3.shape, BF16), SDS((2, t, d), BF16), SDS((t, d), F32), SDS((t, d), BF16)],
        input_output_aliases={0: 0},
        sem=("parallel",), blocks=blocks, temps=8 * _nbytes((tm, d), F32), comm=comm,
    )(lhs3, z, z, x, w3, g_ffn)


def _ffn_up(h2, w_up, f, comm=None):
    t, d = h2.shape
    _, _, ws = w_up.shape
    tm = _tile(t, 1024, SUBLANES_BF16)
    tn = _tile(ws, 1408, LANES)
    nps = ws // tn
    npp = f // tn

    def body(h_ref, w_ref, o_ref):
        o_ref[...] = _dot(h_ref[...], w_ref[...]).astype(BF16)

    blocks = _nbytes((tm, d), BF16) + _nbytes((d, tn), BF16) + _nbytes((tm, tn), BF16)
    return _pcall(
        body, name="ffn_up", grid=(t // tm, 2 * npp),
        in_specs=[pl.BlockSpec((tm, d), lambda i, j: (i, 0)),
                  pl.BlockSpec((None, d, tn), lambda i, j: (j // nps, 0, j % nps))],
        out_specs=[pl.BlockSpec((None, tm, tn), lambda i, j: (j // npp, i, j % npp))],
        out_shape=[SDS((2, t, f), BF16)],
        sem=("parallel", "parallel"), blocks=blocks, temps=_nbytes((tm, tn), F32), comm=comm,
    )(h2, w_up)


def _conv3_rows(u, u1, u2, w_ref, p):
    return w_ref[p, pl.ds(2, 1), :] * u + w_ref[p, pl.ds(1, 1), :] * u1 + w_ref[p, pl.ds(0, 1), :] * u2


WGRAD_TOKENS = 2048
WGRAD_TOKENS_WIDE = 4096
CHUNK = 64
HALO = SUBLANES_F32


def _up1_up2(u, nxt):
    rows = u.shape[0]
    ext = jnp.concatenate([u, nxt], axis=0)
    n = rows + HALO
    return pltpu.roll(ext, n - 1, 0)[:rows], pltpu.roll(ext, n - 2, 0)[:rows]


def _fold8(x):
    return jnp.sum(x.reshape(x.shape[0] // SUBLANES_F32, SUBLANES_F32, x.shape[1]), axis=0)


def _ffn_mid_fwd(u0, cw, cb, nseq):
    _, t, f = u0.shape
    s = t // nseq
    c = _tile(f, 256, LANES)

    def body(u_ref, w_ref, b_ref, a_ref, uo_ref):
        row = lax.broadcasted_iota(jnp.int32, (s, c), 0)
        act = []
        for p in range(2):
            u = u_ref[p].astype(F32)
            act.append(_conv3_rows(u, _shift_down(u, 1, row), _shift_down(u, 2, row), w_ref, p) + b_ref[p])
            uo_ref[p] = act[p].astype(BF16)
        ug, uv = act
        a_ref[...] = (ug * _sigmoid(ug) * uv).astype(BF16)

    blocks = 2 * _nbytes((2, s, c), BF16) + _nbytes((s, c), BF16)
    outs, _ = _pcall(
        body, name="ffn_mid_fwd", grid=(f // c, nseq),
        in_specs=[pl.BlockSpec((2, s, c), lambda j, b: (0, b, j)),
                  pl.BlockSpec((2, 3, c), lambda j, b: (0, 0, j)),
                  pl.BlockSpec((2, 1, c), lambda j, b: (0, 0, j))],
        out_specs=[pl.BlockSpec((s, c), lambda j, b: (b, j)),
                   pl.BlockSpec((2, s, c), lambda j, b: (0, b, j))],
        out_shape=[SDS((t, f), BF16), SDS((2, t, f), BF16)],
        sem=("parallel", "parallel"), blocks=blocks, temps=8 * _nbytes((s, c), F32),
    )(u0, cw, cb)
    return outs


def _ffn_down_loss(a, w_down, x1, tgt, g_fin):
    t, f = a.shape
    d = x1.shape[1]
    tm = _tile(t, 512, SUBLANES_BF16)
    nsteps = t // tm

    def body(a_ref, w_ref, x1_ref, t_ref, g_ref, dx_ref, dxb_ref, loss_ref, gg_ref, lacc):
        i = pl.program_id(0)

        @pl.when(i == 0)
        def _():
            lacc[...] = jnp.zeros_like(lacc)
            gg_ref[...] = jnp.zeros_like(gg_ref)

        x2 = x1_ref[...] + _dot(a_ref[...], w_ref[...])
        xh, inv = _rms_fwd(x2)
        g = g_ref[...]
        e = xh * g - t_ref[...]
        lacc[...] += jnp.sum(e * e, axis=0, keepdims=True)
        dy = e * (1.0 / d)
        gg_ref[...] += jnp.sum(dy * xh, axis=0, keepdims=True)
        dx2 = _rms_bwd(dy, xh, inv, g)
        dx_ref[...] = dx2
        dxb_ref[...] = dx2.astype(BF16)

        @pl.when(i == nsteps - 1)
        def _():
            loss_ref[...] = jnp.sum(lacc[...], axis=1, keepdims=True) * (0.5 / d)

    blocks = (_nbytes((tm, f), BF16) + _nbytes((f, d), BF16) + 3 * _nbytes((tm, d), F32) + _nbytes((tm, d), BF16))
    outs, _ = _pcall(
        body, name="ffn_down_loss", grid=(nsteps,),
        in_specs=[pl.BlockSpec((tm, f), lambda i: (i, 0)), pl.BlockSpec((f, d), lambda i: (0, 0)),
                  pl.BlockSpec((tm, d), lambda i: (i, 0)), pl.BlockSpec((tm, d), lambda i: (i, 0)),
                  pl.BlockSpec((1, d), lambda i: (0, 0))],
        out_specs=[pl.BlockSpec((tm, d), lambda i: (i, 0)), pl.BlockSpec((tm, d), lambda i: (i, 0)),
                   pl.BlockSpec((1, 1), lambda i: (0, 0)), pl.BlockSpec((1, d), lambda i: (0, 0))],
        out_shape=[SDS((t, d), F32), SDS((t, d), BF16), SDS((1, 1), F32), SDS((1, d), F32)],
        scratch_shapes=[pltpu.VMEM((1, d), F32)],
        sem=("arbitrary",), blocks=blocks, temps=8 * _nbytes((tm, d), F32),
    )(a, w_down, x1, tgt, g_fin)
    return outs


def _ffn_bwd_da(dxb, w_down, comm=None):
    t, d = dxb.shape
    f = w_down.shape[0]
    tm = _tile(t, 512, SUBLANES_BF16)

    def body(x_ref, w_ref, o_ref):
        o_ref[...] = _dot_tb(x_ref[...], w_ref[...]).astype(BF16)

    blocks = _nbytes((tm, d), BF16) + _nbytes((tm, f), BF16)
    return _pcall(
        body, name="ffn_bwd_da", grid=(t // tm,),
        in_specs=[pl.BlockSpec((tm, d), lambda i: (i, 0)),
                  pl.BlockSpec((f, d), lambda i: (0, 0), pipeline_mode=pl.Buffered(1))],
        out_specs=[pl.BlockSpec((tm, f), lambda i: (i, 0))],
        out_shape=[SDS((t, f), BF16)],
        sem=("parallel",), blocks=blocks, temps=_nbytes((f, d), BF16) + _nbytes((tm, f), F32), comm=comm,
    )(dxb, w_down)


def _ffn_mid_bwd(da, u0, ua, cw, nseq, comm=None):
    _, t, f = u0.shape
    s = t // nseq
    c = _tile(f, 128, LANES)
    r = _tile(s, CHUNK, SUBLANES_BF16)
    n = s // r

    def body(da_ref, u_ref, ua_ref, w_ref, du_ref, gw_ref, gb_ref):
        @pl.when(pl.program_id(1) == 0)
        def _():
            gw_ref[...] = jnp.zeros_like(gw_ref)
            gb_ref[...] = jnp.zeros_like(gb_ref)

        def step(i, carry):
            nxt, sums = carry
            rows = pl.ds(pl.multiple_of((n - 1 - i) * r, r), r)
            ug = ua_ref[0, rows, :].astype(F32)
            uv = ua_ref[1, rows, :].astype(F32)
            sg = _sigmoid(ug)
            dacc = da_ref[rows, :].astype(F32)
            dus = (dacc * uv * sg * (1.0 + ug * (1.0 - sg)), dacc * (ug * sg))
            first, new_sums = [], []
            for p in range(2):
                du = dus[p]
                d1, d2 = _up1_up2(du, nxt[p])
                du_ref[p, rows, :] = _conv3_rows(du, d1, d2, w_ref, p).astype(BF16)
                u = u_ref[p, rows, :].astype(F32)
                sb, s0, s1, s2 = sums[p]
                new_sums.append((sb + _fold8(du), s0 + _fold8(d2 * u), s1 + _fold8(d1 * u), s2 + _fold8(du * u)))
                first.append(du[:HALO])
            return tuple(first), tuple(new_sums)

        zero = jnp.zeros((HALO, c), F32)
        _, sums = lax.fori_loop(0, n, step, ((zero, zero), ((zero,) * 4,) * 2))
        for p in range(2):
            sb, s0, s1, s2 = sums[p]
            gb_ref[p] += jnp.sum(sb, axis=0, keepdims=True)
            gw_ref[p, pl.ds(0, 1), :] += jnp.sum(s0, axis=0, keepdims=True)
            gw_ref[p, pl.ds(1, 1), :] += jnp.sum(s1, axis=0, keepdims=True)
            gw_ref[p, pl.ds(2, 1), :] += jnp.sum(s2, axis=0, keepdims=True)

    blocks = _nbytes((s, c), BF16) + 3 * _nbytes((2, s, c), BF16)
    return _pcall(
        body, name="ffn_mid_bwd", grid=(f // c, nseq),
        in_specs=[pl.BlockSpec((s, c), lambda j, b: (b, j)),
                  pl.BlockSpec((2, s, c), lambda j, b: (0, b, j)),
                  pl.BlockSpec((2, s, c), lambda j, b: (0, b, j)),
                  pl.BlockSpec((2, 3, c), lambda j, b: (0, 0, j))],
        out_specs=[pl.BlockSpec((2, s, c), lambda j, b: (0, b, j)),
                   pl.BlockSpec((2, 3, c), lambda j, b: (0, 0, j)),
                   pl.BlockSpec((2, 1, c), lambda j, b: (0, 0, j))],
        out_shape=[SDS((2, t, f), BF16), SDS((2, 3, f), F32), SDS((2, 1, f), F32)],
        sem=("parallel", "arbitrary"), blocks=blocks, temps=4 * 1024 * 1024, comm=comm,
    )(da, u0, ua, cw)


def _wgrad(a, b, name, *, tr, tn, b_plane_of=None, out_shards=None, comm=None):
    t, m = a.shape
    n_total = b.shape[-1] * (b.shape[0] if b.ndim == 3 else 1)
    tk = _tile(t, WGRAD_TOKENS_WIDE if n_total > tn and m == tr else WGRAD_TOKENS, SUBLANES_BF16)
    nk = t // tk
    once = pl.Buffered(1) if nk == 1 else None

    def body(a_ref, b_ref, o_ref, *acc):
        part = _dot_ta(a_ref[...], b_ref[...])
        if nk == 1:
            o_ref[...] = part.astype(BF16)
        else:
            _acc_over(pl.program_id(2), nk, part, acc[0], o_ref)

    if b.ndim == 3:
        b_spec = pl.BlockSpec((None, tk, tn), lambda r, n, k: (b_plane_of(n)[0], k, b_plane_of(n)[1]))
    else:
        b_spec = pl.BlockSpec((tk, tn), lambda r, n, k: (k, n), pipeline_mode=once if n_total == tn else None)
    if out_shards is None:
        o_spec = pl.BlockSpec((tr, tn), lambda r, n, k: (r, n))
        o_shape = SDS((m, n_total), BF16)
    else:
        nps = n_total // out_shards // tn
        o_spec = pl.BlockSpec((None, tr, tn), lambda r, n, k: (n // nps, r, n % nps))
        o_shape = SDS((out_shards, m, n_total // out_shards), BF16)
    blocks = _nbytes((tk, tr), BF16) + _nbytes((tk, tn), BF16) + _nbytes((tr, tn), BF16)
    return _pcall(
        body, name=name, grid=(m // tr, n_total // tn, nk),
        in_specs=[pl.BlockSpec((tk, tr), lambda r, n, k: (k, r), pipeline_mode=once if m == tr else None), b_spec],
        out_specs=[o_spec], out_shape=[o_shape],
        scratch_shapes=[] if nk == 1 else [pltpu.VMEM((tr, tn), F32)],
        sem=("parallel", "parallel", "arbitrary"), blocks=blocks, temps=2 * _nbytes((tr, tn), F32), comm=comm,
    )(a, b)


def _wgrad3(lhs3, rhs3, comm=None):
    nw, t, d = lhs3.shape
    tk = _tile(t, WGRAD_TOKENS, SUBLANES_BF16)
    nk = t // tk

    def body(a_ref, b_ref, o_ref, *acc):
        part = _dot_ta(a_ref[...], b_ref[...])
        if nk == 1:
            o_ref[...] = part.astype(BF16)
        else:
            _acc_over(pl.program_id(1), nk, part, acc[0], o_ref)

    blocks = 2 * _nbytes((tk, d), BF16) + _nbytes((d, d), BF16)
    return _pcall(
        body, name="wgrad_sq3", grid=(nw, nk),
        in_specs=[pl.BlockSpec((None, tk, d), lambda w, k: (w, k, 0)),
                  pl.BlockSpec((None, tk, d), lambda w, k: (w, k, 0))],
        out_specs=[pl.BlockSpec((None, d, d), lambda w, k: (w, 0, 0))],
        out_shape=[SDS((nw, d, d), BF16)],
        scratch_shapes=[] if nk == 1 else [pltpu.VMEM((d, d), F32)],
        sem=("parallel", "arbitrary"), blocks=blocks, temps=2 * _nbytes((d, d), F32), comm=comm,
    )(lhs3, rhs3)


def _ffn_bwd_dx1(du0, w_up, x1, dx2, g_ffn, n_planes_out, comm=None):
    _, t, f = du0.shape
    d = x1.shape[1]
    nsh, _, ws = w_up.shape
    tm = _tile(t, 512, SUBLANES_BF16)
    spp = f // ws

    def body(du_ref, w_ref, x1_ref, dx2_ref, g_ref, dx1_ref, dxb_ref, gg_ref):
        @pl.when(pl.program_id(0) == 0)
        def _():
            gg_ref[...] = jnp.zeros_like(gg_ref)

        dh = None
        for k in range(nsh):
            part = _dot_tb(du_ref[k // spp, :, (k % spp) * ws:(k % spp + 1) * ws], w_ref[k])
            dh = part if dh is None else dh + part
        xh, inv = _rms_fwd(x1_ref[...])
        gg_ref[...] += jnp.sum(dh * xh, axis=0, keepdims=True)
        dx1 = dx2_ref[...] + _rms_bwd(dh, xh, inv, g_ref[...])
        dx1_ref[...] = dx1
        dxb_ref[...] = dx1.astype(BF16)

    blocks = _nbytes((2, tm, f), BF16) + 3 * _nbytes((tm, d), F32) + _nbytes((tm, d), BF16)
    return _pcall(
        body, name="ffn_bwd_dx1", grid=(t // tm,),
        in_specs=[pl.BlockSpec((2, tm, f), lambda i: (0, i, 0)),
                  pl.BlockSpec((nsh, d, ws), lambda i: (0, 0, 0), pipeline_mode=pl.Buffered(1)),
                  pl.BlockSpec((tm, d), lambda i: (i, 0)),
                  pl.BlockSpec((tm, d), lambda i: (i, 0)),
                  pl.BlockSpec((1, d), lambda i: (0, 0))],
        out_specs=[pl.BlockSpec((tm, d), lambda i: (i, 0)),
                   pl.BlockSpec((None, tm, d), lambda i: (n_planes_out - 1, i, 0)),
                   pl.BlockSpec((1, d), lambda i: (0, 0))],
        out_shape=[SDS((t, d), F32), SDS((n_planes_out, t, d), BF16), SDS((1, d), F32)],
        sem=("arbitrary",), blocks=blocks, temps=_nbytes(w_up.shape, BF16) + 8 * _nbytes((tm, d), F32), comm=comm,
    )(du0, w_up, x1, dx2, g_ffn)


def _mixer_bwd(rhs3, z, ypc, w3, comm=None):
    _, t, d = rhs3.shape
    tm = _tile(t, 512, SUBLANES_BF16)

    def body(dx_ref, zgp, zgc, ypc_ref, w_ref, dyo, dzo, dpq):
        dm = _dot_tb(dx_ref[...], w_ref[2])
        sp = _sigmoid(zgp[...].astype(F32))
        sc = _sigmoid(zgc[...].astype(F32))
        dyp = (dm * sp).astype(BF16)
        dyc = (dm * sc).astype(BF16)
        dzo[0] = (dm * ypc_ref[0].astype(F32) * sp * (1.0 - sp)).astype(BF16)
        dzo[1] = (dm * ypc_ref[1].astype(F32) * sc * (1.0 - sc)).astype(BF16)
        dyo[0] = dyp
        dyo[1] = dyc
        dpq[0] = _dot_tb(dyp, w_ref[0]).astype(BF16)
        dpq[1] = _dot_tb(dyc, w_ref[1]).astype(BF16)

    blocks = _nbytes((tm, d), BF16) * 3 + _nbytes((2, tm, d), BF16) * 4 + _nbytes((3, d, d), BF16)
    return _pcall(
        body, name="mixer_bwd", grid=(t // tm,),
        in_specs=[pl.BlockSpec((None, tm, d), lambda i: (2, i, 0)),
                  pl.BlockSpec((tm, d), lambda i: (i, 4)),
                  pl.BlockSpec((tm, d), lambda i: (i, 5)),
                  pl.BlockSpec((2, tm, d), lambda i: (0, i, 0)),
                  pl.BlockSpec((3, d, d), lambda i: (0, 0, 0))],
        out_specs=[pl.BlockSpec((2, tm, d), lambda i: (0, i, 0)),
                   pl.BlockSpec((2, tm, d), lambda i: (2, i, 0)),
                   pl.BlockSpec((2, tm, d), lambda i: (0, i, 0))],
        out_shape=[SDS(rhs3.shape, BF16), SDS((N_SPLITS, t, d), BF16), SDS((2, t, d), BF16)],
        input_output_aliases={0: 0},
        sem=("parallel",), blocks=blocks, temps=8 * _nbytes((tm, d), F32), comm=comm,
    )(rhs3, z, z, ypc, w3)


def _conv_bwd(dz, dpq, z, conv_w, nseq, comm=None):
    _, t, d = dz.shape
    s = t // nseq
    c = _tile(d, 128, LANES)
    nb = d // c

    def body(dz_in, dq_ref, zb, zc, zv, cw, dzo, gw_ref):
        del dz_in

        @pl.when(pl.program_id(1) == 0)
        def _():
            gw_ref[...] = jnp.zeros_like(gw_ref)

        row = lax.broadcasted_iota(jnp.int32, (s, c), 0)
        b = zb[...].astype(F32)
        cm = zc[...].astype(F32)
        v = zv[...].astype(F32)
        cv = cm * v
        cv1 = _shift_down(cv, 1, row)
        cv2 = _shift_down(cv, 2, row)
        w0, w1, w2 = cw[pl.ds(0, 1), :], cw[pl.ds(1, 1), :], cw[pl.ds(2, 1), :]
        cc = w2 * cv + w1 * cv1 + w0 * cv2
        dq = dq_ref[...].astype(F32)
        dzo[0] = (dq * cc).astype(BF16)
        dcc = dq * b
        gw_ref[pl.ds(0, 1), :] += jnp.sum(dcc * cv2, axis=0, keepdims=True)
        gw_ref[pl.ds(1, 1), :] += jnp.sum(dcc * cv1, axis=0, keepdims=True)
        gw_ref[pl.ds(2, 1), :] += jnp.sum(dcc * cv, axis=0, keepdims=True)
        dcv = w2 * dcc + w1 * _shift_up(dcc, 1, row) + w0 * _shift_up(dcc, 2, row)
        dzo[1] = (dcv * v).astype(BF16)
        dzo[2] = (dcv * cm).astype(BF16)

    blocks = 4 * _nbytes((s, c), BF16) + _nbytes((3, s, c), BF16)
    return _pcall(
        body, name="conv_bwd", grid=(nb, nseq),
        in_specs=[ANY,
                  pl.BlockSpec((None, s, c), lambda j, b: (1, b, j)),
                  pl.BlockSpec((s, c), lambda j, b: (b, nb + j)),
                  pl.BlockSpec((s, c), lambda j, b: (b, 2 * nb + j)),
                  pl.BlockSpec((s, c), lambda j, b: (b, 3 * nb + j)),
                  pl.BlockSpec((3, c), lambda j, b: (0, j))],
        out_specs=[pl.BlockSpec((3, s, c), lambda j, b: (0, b, j)),
                   pl.BlockSpec((3, c), lambda j, b: (0, j))],
        out_shape=[SDS(dz.shape, BF16), SDS((3, d), F32)],
        input_output_aliases={0: 0},
        sem=("parallel", "arbitrary"), blocks=blocks, temps=16 * _nbytes((s, c), F32), comm=comm,
    )(dz, dpq, z, z, z, conv_w)


def _pool_bwd_call(dz, dpq, z, pool_w, pool_scale, nseq, comm=None):
    _, t, d = dz.shape
    s = t // nseq
    c = d // N_GROUPS

    def body(dz_in, dp_ref, zp, pw, ps, dzo, gpw_ref, gps_ref):
        del dz_in
        j = pl.program_id(0)

        @pl.when(pl.program_id(1) == 0)
        def _():
            gpw_ref[...] = jnp.zeros_like(gpw_ref)
            gps_ref[...] = jnp.zeros_like(gps_ref)

        row = lax.broadcasted_iota(jnp.int32, (s, c), 0)
        for gi, win in enumerate(POOL_WINDOWS):
            @pl.when(j == gi)
            def _(win=win):
                pb = _pool_fwd(zp[...].astype(F32), win, row).astype(BF16)
                plin = _dot(pb, pw[...])
                dps = dp_ref[...].astype(F32)
                gps_ref[...] += jnp.sum(dps * plin, axis=0, keepdims=True)
                dplb = (dps * ps[...]).astype(BF16)
                gpw_ref[...] += _dot_ta(pb, dplb)
                dzo[...] = _pool_bwd(_dot_tb(dplb, pw[...]), win, row).astype(BF16)

    blocks = 3 * _nbytes((s, c), BF16) + _nbytes((c, c), BF16) + _nbytes((c, c), F32)
    return _pcall(
        body, name="pool_bwd", grid=(N_GROUPS, nseq),
        in_specs=[ANY,
                  pl.BlockSpec((None, s, c), lambda j, b: (0, b, j)),
                  pl.BlockSpec((s, c), lambda j, b: (b, j)),
                  pl.BlockSpec((None, c, c), lambda j, b: (j, 0, 0)),
                  pl.BlockSpec((1, c), lambda j, b: (0, j))],
        out_specs=[pl.BlockSpec((None, s, c), lambda j, b: (3, b, j)),
                   pl.BlockSpec((None, c, c), lambda j, b: (j, 0, 0)),
                   pl.BlockSpec((1, c), lambda j, b: (0, j))],
        out_shape=[SDS(dz.shape, BF16), SDS((N_GROUPS, c, c), F32), SDS((1, d), F32)],
        input_output_aliases={0: 0},
        sem=("parallel", "arbitrary"), blocks=blocks, temps=10 * _nbytes((s, c), F32), comm=comm,
    )(dz, dpq, z, pool_w, pool_scale)


def _dz_plane(zb):
    return jnp.where(zb < 4, (zb + 3) % 4, zb)


def _wgrad_in(h1, dz, nsh, comm=None):
    t, d = h1.shape
    ws = N_SPLITS * d // nsh
    kb = _tile(math.gcd(d, ws), 512, LANES)
    npl = d // kb
    nps = ws // kb
    tk = _tile(t, WGRAD_TOKENS_WIDE, SUBLANES_BF16)
    nk = t // tk

    def body(a_ref, b_ref, o_ref, *acc):
        part = _dot_ta(a_ref[...], b_ref[...])
        if nk == 1:
            o_ref[...] = part.astype(BF16)
        else:
            _acc_over(pl.program_id(1), nk, part, acc[0], o_ref)

    blocks = _nbytes((tk, d), BF16) + _nbytes((tk, kb), BF16) + _nbytes((d, kb), BF16)
    return _pcall(
        body, name="wgrad_in", grid=(N_SPLITS * npl, nk),
        in_specs=[pl.BlockSpec((tk, d), lambda cb, k: (k, 0), pipeline_mode=pl.Buffered(1) if nk == 1 else None),
                  pl.BlockSpec((None, tk, kb), lambda cb, k: (_dz_plane(cb // npl), k, cb % npl))],
        out_specs=[pl.BlockSpec((None, d, kb), lambda cb, k: (cb // nps, 0, cb % nps))],
        out_shape=[SDS((nsh, d, ws), BF16)],
        scratch_shapes=[] if nk == 1 else [pltpu.VMEM((d, kb), F32)],
        sem=("parallel", "arbitrary"), blocks=blocks, temps=2 * _nbytes((d, kb), F32), comm=comm,
    )(h1, dz)


def _mixer_bwd_dx(dz, w_in, x, dx1, g_mix, comm=None):
    npln, t, d = dz.shape
    nsh, _, ws = w_in.shape
    tm = _tile(t, 512, SUBLANES_BF16)
    kb = _tile(math.gcd(d, ws), 512, LANES)
    npl = d // kb
    nps = ws // kb

    def body(dz_ref, w_ref, x_ref, dx1_ref, g_ref, dx_ref, gg_ref):
        @pl.when(pl.program_id(0) == 0)
        def _():
            gg_ref[...] = jnp.zeros_like(gg_ref)

        dh = None
        for cb in range(npln * npl):
            zb = cb // npl
            plane = (zb + 3) % 4 if zb < 4 else zb
            part = _dot_tb(dz_ref[plane, :, (cb % npl) * kb:(cb % npl + 1) * kb],
                           w_ref[cb // nps, :, (cb % nps) * kb:(cb % nps + 1) * kb])
            dh = part if dh is None else dh + part
        xh, inv = _rms_fwd(x_ref[...])
        gg_ref[...] += jnp.sum(dh * xh, axis=0, keepdims=True)
        dx_ref[...] = dx1_ref[...] + _rms_bwd(dh, xh, inv, g_ref[...])

    blocks = _nbytes((npln, tm, d), BF16) + 3 * _nbytes((tm, d), F32)
    return _pcall(
        body, name="mixer_bwd_dx", grid=(t // tm,),
        in_specs=[pl.BlockSpec((npln, tm, d), lambda i: (0, i, 0)),
                  pl.BlockSpec((nsh, d, ws), lambda i: (0, 0, 0), pipeline_mode=pl.Buffered(1)),
                  pl.BlockSpec((tm, d), lambda i: (i, 0)),
                  pl.BlockSpec((tm, d), lambda i: (i, 0)),
                  pl.BlockSpec((1, d), lambda i: (0, 0))],
        out_specs=[pl.BlockSpec((tm, d), lambda i: (i, 0)),
                   pl.BlockSpec((1, d), lambda i: (0, 0))],
        out_shape=[SDS((t, d), F32), SDS((1, d), F32)],
        sem=("arbitrary",), blocks=blocks, temps=_nbytes(w_in.shape, BF16) + 8 * _nbytes((tm, d), F32), comm=comm,
    )(dz, w_in, x, dx1, g_mix)


N_BIG = 5
SHARD_MAJOR = (0, 2)
ROWS_DIM1 = (1, 4)


def _ds(start, size, align):
    if isinstance(start, int):
        return pl.ds(start, size)
    return pl.ds(pl.multiple_of(start, align), size)


def _piece(a, ref, k, h):
    if a in SHARD_MAJOR:
        r = ref.shape[1] // 2
        return ref.at[k, _ds(h * r, r, SUBLANES_BF16), :]
    if a in ROWS_DIM1:
        r = ref.shape[1] // 8
        return ref.at[:, _ds((2 * k + h) * r, r, SUBLANES_BF16), :]
    r = ref.shape[0] // 8
    return ref.at[_ds((2 * k + h) * r, r, SUBLANES_BF16), :]


def _half(a, ref, h):
    if a in ROWS_DIM1:
        r = ref.shape[1] // 2
        return ref.at[:, _ds(h * r, r, SUBLANES_BF16), :]
    r = ref.shape[0] // 2
    return ref.at[_ds(h * r, r, SUBLANES_BF16), :]


def _piece_shape(a, full_shape):
    if a in SHARD_MAJOR:
        return (full_shape[1] // 2, full_shape[2])
    if a in ROWS_DIM1:
        return (full_shape[0], full_shape[1] // 8, full_shape[2])
    return (full_shape[0] // 8, full_shape[1])


def _shard_shape(a, full_shape):
    if a in SHARD_MAJOR:
        return (full_shape[1], full_shape[2])
    if a in ROWS_DIM1:
        return (full_shape[0], full_shape[1] // 4, full_shape[2])
    return (full_shape[0] // 4, full_shape[1])


def _rows_axis(a):
    return 1 if a in ROWS_DIM1 else 0


def _piece_block(a, full_shape):
    ps = _piece_shape(a, full_shape)
    if a in SHARD_MAJOR:
        return (None,) + ps, lambda k, c: (k, c, 0)
    if a in ROWS_DIM1:
        return ps, lambda k, c: (0, 2 * k + c, 0)
    return ps, lambda k, c: (2 * k + c, 0)


def _coords():
    return lax.axis_index("x"), lax.axis_index("y"), lax.axis_index("c")


def _peer_chips(x, y):
    return [(1 - x, y), (x, 1 - y), (1 - x, 1 - y)]


def _remote(src, dst, ssem, rsem, dev):
    return pltpu.make_async_remote_copy(src_ref=src, dst_ref=dst, send_sem=ssem, recv_sem=rsem,
                                        device_id=dev, device_id_type=MESH)


def _dma_sems(*counts):
    return [pltpu.SemaphoreType.DMA((n,)) for n in counts]


def _symmetric(ins, out_shapes, sems, copies, peers, aliases=None):
    def start(cins, couts, csems):
        for cp in copies(cins, couts, csems):
            cp.start()

    def finish(cins, couts, csems):
        for cp in copies(cins, couts, csems):
            cp.wait()

    return _Comm(ins, out_shapes, sems, start, finish, peers, aliases)


def _rows_part(a, ref, part):
    if part is None:
        return ref
    p, q, n = part
    ax = _rows_axis(a)
    r = ref.shape[ax] // n
    return ref.at[tuple(pl.ds(p * r, (q - p) * r) if d == ax else slice(None) for d in range(len(ref.shape)))]


def _merge(comms):
    ins, outs, sems, aliases, spans = [], [], [], {}, []
    for cm in comms:
        spans.append((len(ins), len(outs), len(sems)))
        for i, o in cm.aliases.items():
            aliases[len(ins) + i] = len(outs) + o
        ins += cm.ins
        outs += cm.out_shapes
        sems += cm.sems

    def each(fn_name):
        def run(cins, couts, csems):
            for cm, (i0, o0, s0) in zip(comms, spans):
                fn = getattr(cm, fn_name)
                if fn is not None:
                    fn(cins[i0:i0 + len(cm.ins)], couts[o0:o0 + len(cm.out_shapes)], csems[s0:s0 + len(cm.sems)])
        return run

    return _Comm(ins, outs, sems, each("start"), each("finish"), frozenset().union(*[cm.peers for cm in comms]),
                 aliases, mid=each("mid") if any(cm.mid is not None for cm in comms) else None)


def _gather_comm(arrs, locs, full_shapes, part=None, into=None):
    n = len(arrs)

    def own(cins, couts, csems):
        x, y, c = _coords()
        j = 2 * x + y
        return [_remote(_rows_part(a, _half(a, cins[q], h), part), _rows_part(a, _piece(a, couts[q], j, h), part),
                        csems[0].at[2 * q + h], csems[1].at[2 * q + h], (x, y, 1 - c))
                for q, a in enumerate(arrs) for h in range(2)]

    def sends(cins, couts, csems):
        x, y, c = _coords()
        j = 2 * x + y
        return [_remote(_rows_part(a, _half(a, cins[q], c), part), _rows_part(a, _piece(a, couts[q], j, c), part),
                        csems[2].at[3 * q + i], csems[3].at[3 * q + i], (px, py, c))
                for q, a in enumerate(arrs) for i, (px, py) in enumerate(_peer_chips(x, y))]

    def forwards(couts, csems, half_of):
        x, y, c = _coords()
        out = []
        for q, a in enumerate(arrs):
            for i, (px, py) in enumerate(_peer_chips(x, y)):
                landed = _rows_part(a, _piece(a, couts[q], 2 * px + py, half_of(c)), part)
                out.append(_remote(landed, landed, csems[4].at[3 * q + i], csems[5].at[3 * q + i], (x, y, 1 - c)))
        return out

    def start(cins, couts, csems):
        for cp in sends(cins, couts, csems) + own(cins, couts, csems):
            cp.start()

    def finish(cins, couts, csems):
        fw = forwards(couts, csems, lambda c: c)
        for cp, f in zip(sends(cins, couts, csems), fw):
            cp.wait_recv()
            f.start()
        for f in forwards(couts, csems, lambda c: 1 - c):
            f.wait_recv()
        for cp in sends(cins, couts, csems) + fw:
            cp.wait_send()
        for cp in own(cins, couts, csems):
            cp.wait()

    ins = [locs[a] for a in arrs] + ([into[a] for a in arrs] if into else [])
    return _Comm(ins, [SDS(full_shapes[a], BF16) for a in arrs],
                 _dma_sems(2 * n, 2 * n, 3 * n, 3 * n, 3 * n, 3 * n), start, finish, CHIPS + (SIBLING,),
                 aliases={n + q: q for q in range(n)} if into else None)


def _ring_gather_comm(arrs, locs, full_shapes):
    n = len(arrs)

    def own(cins, couts, csems):
        x, y, c = _coords()
        j = 2 * x + y
        return [_remote(_half(a, cins[q], h), _piece(a, couts[q], j, h), csems[0].at[2 * q + h],
                        csems[1].at[2 * q + h], (x, y, 1 - c)) for q, a in enumerate(arrs) for h in range(2)]

    def sends(cins, couts, csems):
        x, y, c = _coords()
        j = 2 * x + y
        return [_remote(_half(a, cins[q], c), _piece(a, couts[q], j, c), csems[2].at[2 * q + i],
                        csems[3].at[2 * q + i], (px, py, c))
                for q, a in enumerate(arrs) for i, (px, py) in enumerate(_peer_chips(x, y)[:2])]

    def relays(couts, csems):
        x, y, c = _coords()
        peers = _peer_chips(x, y)
        out = []
        for q, a in enumerate(arrs):
            for r, (src_p, dst_p) in enumerate(((0, 1), (1, 0))):
                sx, sy = peers[src_p]
                rows = _rows_part(a, _piece(a, couts[q], 2 * sx + sy, c), (r, r + 1, 2))
                out.append(_remote(rows, rows, csems[6].at[2 * q + r], csems[7].at[2 * q + r], (*peers[dst_p], c)))
        return out

    def forwards(couts, csems, half_of, which):
        x, y, c = _coords()
        out = []
        for q, a in enumerate(arrs):
            for i in which:
                px, py = _peer_chips(x, y)[i]
                landed = _piece(a, couts[q], 2 * px + py, half_of(c))
                out.append(_remote(landed, landed, csems[4].at[3 * q + i], csems[5].at[3 * q + i], (x, y, 1 - c)))
        return out

    def start(cins, couts, csems):
        for cp in sends(cins, couts, csems) + own(cins, couts, csems):
            cp.start()

    def mid(cins, couts, csems):
        for cp in sends(cins, couts, csems):
            cp.wait_recv()
        for cp in relays(couts, csems) + forwards(couts, csems, lambda c: c, (0, 1)):
            cp.start()

    def finish(cins, couts, csems):
        for cp in relays(couts, csems):
            cp.wait_recv()
        fw_diag = forwards(couts, csems, lambda c: c, (2,))
        for f in fw_diag:
            f.start()
        for f in forwards(couts, csems, lambda c: 1 - c, (0, 1, 2)):
            f.wait_recv()
        for cp in (sends(cins, couts, csems) + relays(couts, csems)
                   + forwards(couts, csems, lambda c: c, (0, 1)) + fw_diag):
            cp.wait_send()
        for cp in own(cins, couts, csems):
            cp.wait()

    return _Comm([locs[a] for a in arrs], [SDS(full_shapes[a], BF16) for a in arrs],
                 _dma_sems(2 * n, 2 * n, 2 * n, 2 * n, 3 * n, 3 * n, 2 * n, 2 * n), start, finish,
                 CHIPS + (SIBLING,), mid=mid)


def _halves_comm(arrs, gbs):
    n = len(arrs)

    def copies(cins, couts, csems):
        x, y, c = _coords()
        return [_remote(_piece(a, cins[q], k, 1 - c), couts[q].at[k], csems[0].at[4 * q + k], csems[1].at[4 * q + k],
                        (x, y, 1 - c)) for q, a in enumerate(arrs) for k in range(4)]

    return _symmetric([gbs[a] for a in arrs], [SDS((4,) + _piece_shape(a, gbs[a].shape), BF16) for a in arrs],
                      _dma_sems(4 * n, 4 * n), copies, [SIBLING])


def _chips_comm(arrs, ps, part=None, into=None):
    n = len(arrs)

    def copies(cins, couts, csems):
        x, y, c = _coords()
        return [_remote(_rows_part(a, cins[q].at[2 * px + py], part), _rows_part(a, couts[q].at[i], part),
                        csems[0].at[3 * q + i], csems[1].at[3 * q + i], (px, py, c))
                for q, a in enumerate(arrs) for i, (px, py) in enumerate(_peer_chips(x, y))]

    ins = [ps[a] for a in arrs] + ([into[a] for a in arrs] if into else [])
    return _symmetric(ins, [SDS((3,) + ps[a].shape[1:], BF16) for a in arrs], _dma_sems(3 * n, 3 * n), copies, CHIPS,
                      aliases={n + q: q for q in range(n)} if into else None)


def _result_comm(arrs, gs):
    n = len(arrs)

    def copies(cins, couts, csems):
        x, y, c = _coords()
        return [_remote(_half(a, cins[q], c), _half(a, couts[q], c), csems[0].at[q], csems[1].at[q], (x, y, 1 - c))
                for q, a in enumerate(arrs)]

    return _symmetric([gs[a] for a in arrs], [SDS(gs[a].shape, F32) for a in arrs], _dma_sems(n, n), copies,
                      [SIBLING], aliases={q: q for q in range(n)})


def _add_halves(arrs, gbs, lands, c_arr, name):
    n = len(arrs)

    def body(c_ref, *refs):
        del c_ref
        for q in range(n):
            refs[2 * n + q][...] = (refs[q][...].astype(F32) + refs[n + q][...].astype(F32)).astype(BF16)

    g_specs, l_specs, o_specs, blocks = [], [], [], 0
    for a in arrs:
        bs, imap = _piece_block(a, gbs[a].shape)
        ps = _piece_shape(a, gbs[a].shape)
        g_specs.append(pl.BlockSpec(bs, lambda k, c_ref, imap=imap: imap(k, c_ref[0])))
        nd = len(ps)
        l_specs.append(pl.BlockSpec((None,) + ps, lambda k, c_ref, nd=nd: (k,) + (0,) * nd))
        o_specs.append(pl.BlockSpec((None,) + ps, lambda k, c_ref, nd=nd: (k,) + (0,) * nd))
        blocks += 3 * _nbytes(ps, BF16)
    return list(pl.pallas_call(
        body, name=name,
        grid_spec=pltpu.PrefetchScalarGridSpec(
            num_scalar_prefetch=1, grid=(4,), in_specs=g_specs + l_specs, out_specs=o_specs),
        out_shape=[SDS((4,) + _piece_shape(a, gbs[a].shape), BF16) for a in arrs],
        compiler_params=_params(("parallel",), blocks, blocks),
    )(c_arr, *[gbs[a] for a in arrs], *lands))


def _sum_chips(a, p, land, shard_shape, jc_arr, name):
    ps = land.shape[1:]
    ax = _rows_axis(a)
    rows = ps[ax]
    nsub = 2 if rows % (2 * SUBLANES_BF16) == 0 else 1
    bs = tuple(r // nsub if q == ax else r for q, r in enumerate(ps))
    nd = len(ps)

    def at_rows(v):
        return tuple(v if q == ax else 0 for q in range(nd))

    def body(jc_ref, p_ref, l_ref, o_ref):
        del jc_ref
        acc = p_ref[...].astype(F32) + l_ref[0].astype(F32)
        acc = acc + l_ref[1].astype(F32)
        o_ref[...] = acc + l_ref[2].astype(F32)

    blocks = 4 * _nbytes(bs, BF16) + _nbytes(bs, F32)
    return pl.pallas_call(
        body, name=name,
        grid_spec=pltpu.PrefetchScalarGridSpec(
            num_scalar_prefetch=1, grid=(nsub,),
            in_specs=[pl.BlockSpec((None,) + bs, lambda s, jc: (jc[0],) + at_rows(s)),
                      pl.BlockSpec((3,) + bs, lambda s, jc: (0,) + at_rows(s))],
            out_specs=pl.BlockSpec(bs, lambda s, jc: at_rows(jc[1] * nsub + s))),
        out_shape=SDS(shard_shape, F32),
        compiler_params=_params(("parallel",), blocks, 2 * _nbytes(bs, F32)),
    )(jc_arr, p, land)


def _small_comm(v):
    rows = v.shape[0]

    def copies(cins, couts, csems):
        x, y, c = _coords()
        me = 4 * x + 2 * y + c
        out = [pltpu.make_async_copy(cins[0], couts[0].at[me], csems[0].at[0])]
        for dlt in range(1, 8):
            px = 1 - x if (dlt >> 2) & 1 else x
            py = 1 - y if (dlt >> 1) & 1 else y
            pc = 1 - c if dlt & 1 else c
            out.append(_remote(cins[0], couts[0].at[me], csems[1].at[dlt - 1], csems[2].at[dlt - 1], (px, py, pc)))
        return out

    return _symmetric([v], [SDS((8, rows, LANES), F32)], _dma_sems(1, 7, 7), copies, EVERYONE)


def _sum8(slots, name):
    def body(s_ref, o_ref):
        acc = s_ref[0]
        for i in range(1, 8):
            acc = acc + s_ref[i]
        o_ref[...] = acc

    return pl.pallas_call(
        body, name=name,
        in_specs=[pl.BlockSpec(memory_space=pltpu.VMEM)], out_specs=pl.BlockSpec(memory_space=pltpu.VMEM),
        out_shape=SDS(slots.shape[1:], F32),
    )(slots)


def _adamw(w, g, m, v, name, g_plane=None):
    rows, cols = w.shape
    tr = _tile(rows, max(SUBLANES_F32, (256 * 1024 // cols) // SUBLANES_F32 * SUBLANES_F32), SUBLANES_F32)

    def body(w_ref, g_ref, m_ref, v_ref, go_ref, d_ref, mo_ref, vo_ref):
        gr = g_ref[...]
        mn = ADAM_B1 * m_ref[...] + (1.0 - ADAM_B1) * gr
        vn = ADAM_B2 * v_ref[...] + (1.0 - ADAM_B2) * (gr * gr)
        m_hat = mn / (1.0 - ADAM_B1 ** ADAM_STEP)
        v_hat = vn / (1.0 - ADAM_B2 ** ADAM_STEP)
        d_ref[...] = -ADAM_LR * (m_hat / (jnp.sqrt(v_hat) + ADAM_EPS) + ADAM_WD * w_ref[...])
        go_ref[...] = gr
        mo_ref[...] = mn
        vo_ref[...] = vn

    spec = pl.BlockSpec((tr, cols), lambda i: (i, 0))
    g_spec = spec if g_plane is None else pl.BlockSpec((None, tr, cols), lambda i: (g_plane, i, 0))
    return pl.pallas_call(
        body, name=name, grid=(rows // tr,),
        in_specs=[spec, g_spec, spec, spec], out_specs=[spec, spec, spec, spec],
        out_shape=[SDS((rows, cols), F32)] * 4,
        compiler_params=_params(("parallel",), 8 * _nbytes((tr, cols), F32), 4 * _nbytes((tr, cols), F32)),
    )(w, g, m, v)


def _pack(parts):
    rows = []
    for p in parts:
        r = p.reshape(-1, LANES)
        pad = (-r.shape[0]) % SUBLANES_F32
        if pad:
            r = jnp.pad(r, ((0, pad), (0, 0)))
        rows.append(r)
    return jnp.concatenate(rows, axis=0)


def _unpack(packed, shapes):
    out, at = [], 0
    for s in shapes:
        n = 1
        for q in s:
            n *= q
        r = n // LANES
        out.append(packed[at:at + r].reshape(s))
        at += r + (-r) % SUBLANES_F32
    return out


def kernel(x, norm_mix, w_in, pool_w, pool_scale, w_pool_proj, conv_w, w_conv_out, w_o, norm_ffn, w_up, ffn_conv_w, ffn_conv_b, w_down, norm_final, loss_target, m_norm_mix, m_w_in, m_pool_w, m_pool_scale, m_w_pool_proj, m_conv_w, m_w_conv_out, m_w_o, m_norm_ffn, m_w_up, m_ffn_conv_w, m_ffn_conv_b, m_w_down, m_norm_final, v_norm_mix, v_w_in, v_pool_w, v_pool_scale, v_w_pool_proj, v_conv_w, v_w_conv_out, v_w_o, v_norm_ffn, v_w_up, v_ffn_conv_w, v_ffn_conv_b, v_w_down, v_norm_final):
    nseq, seq, d = x.shape
    t = nseq * seq
    f = w_down.shape[1] * 4
    c = d // N_GROUPS
    xy = lax.axis_index("x") * 2 + lax.axis_index("y")
    c_arr = lax.axis_index("c").astype(jnp.int32).reshape(1)
    jc_arr = jnp.stack([xy, lax.axis_index("c")]).astype(jnp.int32)
    nsh = 4
    zero = jnp.zeros((), jnp.int32)

    locs = [w_in[0].astype(BF16),
            jnp.stack([w_pool_proj[0], w_conv_out[0], w_o[0]]).astype(BF16),
            w_up[0].astype(BF16), w_down[0].astype(BF16), pool_w[0].astype(BF16)]
    full_shapes = [(nsh, d, N_SPLITS * d // nsh), (3, d, d), (nsh, d, 2 * f // nsh), (f, d), (N_GROUPS, c, c)]

    cw_pad = lax.dynamic_update_slice(jnp.zeros((3, d), F32), conv_w[0], (zero, xy * (d // 4)))
    fw_pad = lax.dynamic_update_slice(jnp.zeros((3, 2 * f), F32), ffn_conv_w[0], (zero, xy * (f // 2)))
    small_w = _pack([cw_pad, fw_pad]) * 0.5

    x2d = x.reshape(t, d)
    tgt = loss_target.reshape(t, d)
    ax, ay = lax.axis_index("x"), lax.axis_index("y")
    order = jnp.stack([xy, 2 * (1 - ax) + ay, 2 * ax + 1 - ay, 2 * (1 - ax) + 1 - ay]).astype(jnp.int32)
    (z, h1, w_in_f), (pool_w_f, w3_f, slots_w) = _fwd_in(
        x2d, norm_mix, locs[0], order,
        _merge([_gather_comm([4], locs, full_shapes), _gather_comm([1], locs, full_shapes, part=(0, 1, 2)),
                _small_comm(small_w)]))
    conv_w_f, ffn_cw_f = _unpack(_sum8(slots_w, "sum8_weights"), [(3, d), (3, 2 * f)])
    ffn_cw_p = ffn_cw_f.reshape(3, 2, f).transpose(1, 0, 2)
    ffn_cb_p = ffn_conv_b.reshape(2, 1, f)
    (lhs3,), (w3_f,) = _mixer_mid_fwd(z, pool_w_f, pool_scale, conv_w_f, nseq,
                                      _gather_comm([1], locs, full_shapes, part=(1, 2, 2), into={1: w3_f}))
    (lhs3, ypc, x1, h2), (w_up_f,) = _mixer_out(lhs3, z, x2d, w3_f, norm_ffn,
                                                _ring_gather_comm([2], locs, full_shapes))
    (u0,), (w_down_f,) = _ffn_up(h2, w_up_f, f, _gather_comm([3], locs, full_shapes))
    act, ua = _ffn_mid_fwd(u0, ffn_cw_p, ffn_cb_p, nseq)
    dx2, dx2b, loss11, g_norm_final = _ffn_down_loss(act, w_down_f, x1, tgt, norm_final.reshape(1, d))

    gbs, lands, ps, lands2, rs = {}, {}, {}, {}, {}
    tn_up = _tile(2 * f // nsh, 1408, LANES)
    npp = f // tn_up

    def add(arrs, name):
        for a, p in zip(arrs, _add_halves(arrs, gbs, [lands[a] for a in arrs], c_arr, name)):
            ps[a] = p

    def summed(a):
        rs[a] = _sum_chips(a, ps[a], lands2[a], _shard_shape(a, full_shapes[a]), jc_arr, "sum_chips_%d" % a)

    (gbs[3],), _ = _wgrad(act, dx2b, "wgrad_down", tr=tn_up, tn=d)
    (da,), (lands[3],) = _ffn_bwd_da(dx2b, w_down_f, _halves_comm([3], gbs))
    add([3], "add_halves_down")
    (du0, g_ffn_cw_p, g_ffn_cb_p), (lands2[3],) = _ffn_mid_bwd(da, u0, ua, ffn_cw_p, nseq, _chips_comm([3], ps))
    summed(3)
    (gbs[2],), (rs[3],) = _wgrad(h2, du0, "wgrad_up", tr=d, tn=tn_up, b_plane_of=lambda n: (n // npp, n % npp),
                                 out_shards=nsh, comm=_result_comm([3], rs))
    (dx1, rhs3, g_norm_ffn), (lands[2],) = _ffn_bwd_dx1(du0, w_up_f, x1, dx2, norm_ffn, 3, _halves_comm([2], gbs))
    add([2], "add_halves_up")
    (rhs3, dz, dpq), (lands2[2],) = _mixer_bwd(rhs3, z, ypc, w3_f, _chips_comm([2], ps, part=(0, 1, 2)))
    (gbs[1],), (lands2[2],) = _wgrad3(lhs3, rhs3, _chips_comm([2], ps, part=(1, 2, 2), into=lands2))
    summed(2)
    (dz, g_conv_w), (lands[1], rs[2]) = _conv_bwd(dz, dpq, z, conv_w_f, nseq,
                                                  _merge([_halves_comm([1], gbs), _result_comm([2], rs)]))
    add([1], "add_halves_sq3")
    (dz, g_pool_w, g_pool_scale), _ = _pool_bwd_call(dz, dpq, z, pool_w_f, pool_scale, nseq)
    gbs[4] = g_pool_w.astype(BF16)
    (gbs[0],), (lands2[1],) = _wgrad_in(h1, dz, nsh, _chips_comm([1], ps))
    summed(1)
    lands[0], lands[4] = _run_comm(_halves_comm([0, 4], gbs), "exchange_halves_in")
    add([0, 4], "add_halves_in")
    g_ffn_cw = g_ffn_cw_p.transpose(1, 0, 2).reshape(3, 2 * f)
    small_a = _pack([g_pool_scale, g_norm_ffn, g_ffn_cb_p.reshape(1, 2 * f), g_norm_final.reshape(d), g_conv_w,
                     g_ffn_cw, jnp.pad(loss11, ((0, SUBLANES_F32 - 1), (0, LANES - 1)))])
    (grad_x, g_norm_mix), (lands2[0], lands2[4], rs[1], slots_a) = _mixer_bwd_dx(
        dz, w_in_f, x2d, dx1, norm_mix,
        _merge([_chips_comm([0, 4], ps), _result_comm([1], rs), _small_comm(small_a)]))
    summed(0)
    summed(4)
    rs[0], rs[4], slots_b = _run_comm(_merge([_result_comm([0, 4], rs), _small_comm(_pack([g_norm_mix]))]),
                                      "exchange_result_in")
    shapes_a = [(1, d), (1, d), (1, 2 * f), (d,), (3, d), (3, 2 * f), (SUBLANES_F32, LANES)]
    gs_pool_scale, gs_norm_ffn, gs_ffn_cb, gs_norm_final, gs_conv_w, gs_ffn_cw, loss_blk = _unpack(
        _sum8(slots_a, "sum8_grads"), shapes_a)
    (gs_norm_mix,) = _unpack(_sum8(slots_b, "sum8_norm_mix"), [(1, d)])
    gs_conv_w = lax.dynamic_slice(gs_conv_w, (zero, xy * (d // 4)), (3, d // 4))
    gs_ffn_cw = lax.dynamic_slice(gs_ffn_cw, (zero, xy * (f // 2)), (3, f // 2))

    def upd(w, g, m, v, name, g_plane=None):
        shape = w.shape
        rows = 1
        for q in shape[:-1]:
            rows *= q
        g2 = g if g_plane is not None else g.reshape(rows, shape[-1])
        outs = _adamw(w.reshape(rows, shape[-1]), g2, m.reshape(rows, shape[-1]), v.reshape(rows, shape[-1]),
                      name, g_plane)
        return [o.reshape(shape) for o in outs]

    res = {
        "w_in": upd(w_in, rs[0], m_w_in, v_w_in, "adamw_w_in"),
        "pool_w": upd(pool_w, rs[4], m_pool_w, v_pool_w, "adamw_pool_w"),
        "w_pool_proj": upd(w_pool_proj, rs[1], m_w_pool_proj, v_w_pool_proj, "adamw_w_pool_proj", 0),
        "w_conv_out": upd(w_conv_out, rs[1], m_w_conv_out, v_w_conv_out, "adamw_w_conv_out", 1),
        "w_o": upd(w_o, rs[1], m_w_o, v_w_o, "adamw_w_o", 2),
        "w_up": upd(w_up, rs[2], m_w_up, v_w_up, "adamw_w_up"),
        "w_down": upd(w_down, rs[3], m_w_down, v_w_down, "adamw_w_down"),
    }

    small_names = ["norm_mix", "pool_scale", "norm_ffn", "ffn_conv_b", "norm_final", "conv_w", "ffn_conv_w"]
    small_ws = [norm_mix, pool_scale, norm_ffn, ffn_conv_b, norm_final, conv_w, ffn_conv_w]
    small_ms = [m_norm_mix, m_pool_scale, m_norm_ffn, m_ffn_conv_b, m_norm_final, m_conv_w, m_ffn_conv_w]
    small_vs = [v_norm_mix, v_pool_scale, v_norm_ffn, v_ffn_conv_b, v_norm_final, v_conv_w, v_ffn_conv_w]
    small_gs = [gs_norm_mix, gs_pool_scale, gs_norm_ffn, gs_ffn_cb, gs_norm_final, gs_conv_w, gs_ffn_cw]
    _, sd, sm, sv = _adamw(_pack(small_ws), _pack(small_gs), _pack(small_ms), _pack(small_vs), "adamw_small")
    shapes = [w.shape for w in small_ws]
    sd, sm, sv = _unpack(sd, shapes), _unpack(sm, shapes), _unpack(sv, shapes)
    for i, nm in enumerate(small_names):
        res[nm] = [small_gs[i].reshape(shapes[i]), sd[i], sm[i], sv[i]]

    order = ["norm_mix", "w_in", "pool_w", "pool_scale", "w_pool_proj", "conv_w", "w_conv_out", "w_o", "norm_ffn",
             "w_up", "ffn_conv_w", "ffn_conv_b", "w_down", "norm_final"]
    return (loss_blk[0, 0], grad_x.reshape(x.shape), *[res[n][0] for n in order], *[res[n][1] for n in order],
            *[res[n][2] for n in order], *[res[n][3] for n in order])
```

```python
import math

import jax
import jax.numpy as jnp
from jax import lax
from jax.experimental import pallas as pl
from jax.experimental.pallas import tpu as pltpu

F32 = jnp.float32
BF16 = jnp.bfloat16
SDS = jax.ShapeDtypeStruct
MESH = pl.DeviceIdType.MESH

RMS_EPS = 1e-6
POOL_WINDOWS = (2, 4, 8, 16)
N_GROUPS = len(POOL_WINDOWS)
N_SPLITS = 6

ADAM_LR = 0.001
ADAM_B1 = 0.9
ADAM_B2 = 0.999
ADAM_EPS = 1e-08
ADAM_WD = 0.01
ADAM_STEP = 10

LANES = 128
SUBLANES_F32 = 8
SUBLANES_BF16 = 16
VMEM_BYTES = 64 * 1024 * 1024
VMEM_CAP = VMEM_BYTES - 8 * 1024 * 1024
VMEM_FLOOR = 16 * 1024 * 1024

ANY = pl.BlockSpec(memory_space=pl.ANY)


def _tile(dim, pref, align):
    if dim <= pref:
        return dim
    t = (pref // align) * align
    while t >= align:
        if dim % t == 0:
            return t
        t -= align
    return dim


def _nbytes(shape, dtype):
    n = 1
    for s in shape:
        n *= s
    return n * jnp.dtype(dtype).itemsize


def _params(sem, block_bytes, temp_bytes=0, collective_id=None):
    need = 2 * block_bytes + temp_bytes + 4 * 1024 * 1024
    return pltpu.CompilerParams(dimension_semantics=sem, collective_id=collective_id,
                                vmem_limit_bytes=int(min(max(need, VMEM_FLOOR), VMEM_CAP)))


SIBLING = (0, 0, 1)
CHIPS = ((1, 0, 0), (0, 1, 0), (1, 1, 0))
EVERYONE = tuple((a, b, c) for a in range(2) for b in range(2) for c in range(2) if a + b + c)
PEER_SETS = (frozenset([SIBLING]), frozenset(CHIPS), frozenset(CHIPS + (SIBLING,)), frozenset(EVERYONE))
MID_AT = 0.75


def _collective_id(peers):
    return PEER_SETS.index(frozenset(peers))


def _handshake(peers):
    x, y, c = lax.axis_index("x"), lax.axis_index("y"), lax.axis_index("c")
    bar = pltpu.get_barrier_semaphore()
    for fx, fy, fc in sorted(peers):
        dev = (1 - x if fx else x, 1 - y if fy else y, 1 - c if fc else c)
        pl.semaphore_signal(bar, inc=1, device_id=dev, device_id_type=MESH)
    pl.semaphore_wait(bar, len(peers))


class _Comm:
    def __init__(self, ins, out_shapes, sems, start, finish, peers, aliases=None, mid=None):
        self.ins = list(ins)
        self.out_shapes = list(out_shapes)
        self.sems = list(sems)
        self.start = start
        self.finish = finish
        self.mid = mid
        self.peers = frozenset(peers)
        self.aliases = dict(aliases or {})


def _pcall(body, *, name, grid, in_specs, out_specs, out_shape, sem, blocks, temps=0, scratch_shapes=(),
           input_output_aliases=None, comm=None):
    in_specs = list(in_specs)
    out_specs = list(out_specs)
    out_shape = list(out_shape)
    scratch_shapes = list(scratch_shapes)
    aliases = dict(input_output_aliases or {})
    n_in, n_out, n_scr = len(in_specs), len(out_shape), len(scratch_shapes)
    if comm is None:
        call = pl.pallas_call(
            body, name=name, grid=grid, in_specs=in_specs, out_specs=out_specs, out_shape=out_shape,
            scratch_shapes=scratch_shapes, input_output_aliases=aliases,
            compiler_params=_params(sem, blocks, temps))
        return lambda *args: (list(call(*args)), [])

    nci, nco = len(comm.ins), len(comm.out_shapes)
    n_steps = 1
    for g in grid:
        n_steps *= g

    def hosted(*refs):
        ins = refs[:n_in]
        cins = refs[n_in:n_in + nci]
        outs = refs[n_in + nci:n_in + nci + n_out]
        couts = refs[n_in + nci + n_out:n_in + nci + n_out + nco]
        scr = refs[n_in + nci + n_out + nco:n_in + nci + n_out + nco + n_scr]
        csems = refs[n_in + nci + n_out + nco + n_scr:]
        first = None
        last = None
        step = 0
        for q, g in enumerate(grid):
            pid = pl.program_id(q)
            first = (pid == 0) if first is None else first & (pid == 0)
            last = (pid == g - 1) if last is None else last & (pid == g - 1)
            step = step * g + pid

        @pl.when(first)
        def _():
            _handshake(comm.peers)
            comm.start(cins, couts, csems)

        if comm.mid is not None:
            @pl.when(step == int(MID_AT * n_steps))
            def _():
                comm.mid(cins, couts, csems)

        body(*ins, *outs, *scr)

        @pl.when(last)
        def _():
            comm.finish(cins, couts, csems)

    for i, o in comm.aliases.items():
        aliases[n_in + i] = n_out + o
    call = pl.pallas_call(
        hosted, name=name, grid=grid, in_specs=in_specs + [ANY] * nci, out_specs=out_specs + [ANY] * nco,
        out_shape=out_shape + comm.out_shapes, scratch_shapes=scratch_shapes + comm.sems,
        input_output_aliases=aliases,
        compiler_params=_params(("arbitrary",) * len(grid), blocks, temps, _collective_id(comm.peers)))

    def run(*args):
        res = call(*args, *comm.ins)
        return list(res[:n_out]), list(res[n_out:])

    return run


def _run_comm(comm, name):
    def body(*refs):
        nci, nco = len(comm.ins), len(comm.out_shapes)
        cins, couts, csems = refs[:nci], refs[nci:nci + nco], refs[nci + nco:]
        _handshake(comm.peers)
        comm.start(cins, couts, csems)
        if comm.mid is not None:
            comm.mid(cins, couts, csems)
        comm.finish(cins, couts, csems)

    return list(pl.pallas_call(
        body, name=name, in_specs=[ANY] * len(comm.ins), out_specs=[ANY] * len(comm.out_shapes),
        out_shape=comm.out_shapes, scratch_shapes=comm.sems, input_output_aliases=comm.aliases,
        compiler_params=pltpu.CompilerParams(collective_id=_collective_id(comm.peers)),
    )(*comm.ins))


def _dot(a, b):
    return jnp.dot(a, b, preferred_element_type=F32)


def _dot_tb(a, b):
    return lax.dot_general(a, b, (((1,), (1,)), ((), ())), preferred_element_type=F32)


def _dot_ta(a, b):
    return lax.dot_general(a, b, (((0,), (0,)), ((), ())), preferred_element_type=F32)


def _rms_fwd(x):
    inv = lax.rsqrt(jnp.mean(x * x, axis=-1, keepdims=True) + RMS_EPS)
    return x * inv, inv


def _rms_bwd(dy, xhat, inv, g):
    gd = dy * g
    return inv * (gd - xhat * jnp.mean(gd * xhat, axis=-1, keepdims=True))


def _sigmoid(x):
    return 1.0 / (1.0 + jnp.exp(-x))


def _shift_down(x, k, row):
    return jnp.where(row >= k, pltpu.roll(x, k, 0), 0.0)


def _shift_up(x, k, row):
    s = x.shape[0]
    return jnp.where(row < s - k, pltpu.roll(x, s - k, 0), 0.0)


def _pool_fwd(u, win, row):
    s = u
    k = 1
    while k < win:
        s = s + _shift_down(s, k, row)
        k *= 2
    cnt = jnp.minimum(row + 1, win).astype(F32)
    return s / cnt - u


def _pool_bwd(dp, win, row):
    cnt = jnp.minimum(row + 1, win).astype(F32)
    s = dp / cnt
    k = 1
    while k < win:
        s = s + _shift_up(s, k, row)
        k *= 2
    return s - dp


def _acc_over(k, nk, part, acc, o_ref):
    @pl.when(k == 0)
    def _():
        acc[...] = part

    @pl.when(k > 0)
    def _():
        acc[...] += part

    @pl.when(k == nk - 1)
    def _():
        o_ref[...] = acc[...].astype(o_ref.dtype)


def _fwd_in(x, g, w_loc, order, comm):
    t, d = x.shape
    ws = w_loc.shape[1]
    nsh = order.shape[0]
    assert nsh == 4, "the shard walk below is written for the 2 x 2 chips of the mesh"
    tm = _tile(t, 1024, SUBLANES_BF16)
    ni = t // tm
    nci, nco = len(comm.ins), len(comm.out_shapes)
    all_peers = comm.peers | frozenset(CHIPS + (SIBLING,))

    def body(order_ref, x_ref, g_ref, loc_ref, *rest):
        del order_ref
        cins = rest[:nci]
        z_ref, h_ref, full_ref = rest[nci:nci + 3]
        couts = rest[nci + 3:nci + 3 + nco]
        (hs, wbuf, wsem, own_s, own_r, snd_s, snd_r, fwd_s, fwd_r, rly_s, rly_r) = rest[nci + 3 + nco:nci + 14 + nco]
        csems = rest[nci + 14 + nco:]
        j = pl.program_id(0)
        i = pl.program_id(1)
        x_, y_, c_ = _coords()
        own = 2 * x_ + y_
        sib = (x_, y_, 1 - c_)
        peers = _peer_chips(x_, y_)

        def sends():
            return [_remote(_half(0, loc_ref, c_), _piece(0, full_ref, own, c_), snd_s.at[p], snd_r.at[p], (px, py, c_))
                    for p, (px, py) in enumerate(peers[:2])]

        def relays():
            out = []
            for q, (src_p, dst_p) in enumerate(((0, 1), (1, 0))):
                sx, sy = peers[src_p]
                part = _rows_part(0, _piece(0, full_ref, 2 * sx + sy, c_), (q, q + 1, 2))
                out.append(_remote(part, part, rly_s.at[q], rly_r.at[q], (*peers[dst_p], c_)))
            return out

        def owns():
            return [_remote(_half(0, loc_ref, h), _piece(0, full_ref, own, h), own_s.at[h], own_r.at[h], sib)
                    for h in range(2)]

        def forward(p, half):
            px, py = peers[p]
            landed = _piece(0, full_ref, 2 * px + py, half)
            return _remote(landed, landed, fwd_s.at[p], fwd_r.at[p], sib)

        def load(src, slot):
            return pltpu.make_async_copy(src, wbuf.at[slot], wsem.at[slot])

        @pl.when((j == 0) & (i == 0))
        def _():
            _handshake(all_peers)
            for cp in sends() + owns():
                cp.start()
            load(loc_ref, 0).start()

        @pl.when(j == 0)
        def _():
            xh, _ = _rms_fwd(x_ref[...])
            h = (xh * g_ref[...]).astype(BF16)
            hs[pl.ds(pl.multiple_of(i * tm, tm), tm), :] = h
            h_ref[...] = h

        slot = j % 2

        @pl.when(i == 0)
        def _():
            load(loc_ref, slot).wait()

        z_ref[...] = _dot(hs[pl.ds(pl.multiple_of(i * tm, tm), tm), :], wbuf[slot]).astype(BF16)

        def load_shard(p, into):
            px, py = peers[p]
            forward(p, 1 - c_).wait_recv()
            load(full_ref.at[2 * px + py], into).start()

        @pl.when((j == 0) & (i == ni - 1))
        def _():
            for cp in sends():
                cp.wait_recv()
            for cp in relays() + [forward(0, c_), forward(1, c_)]:
                cp.start()
            load_shard(0, 1)
            comm.start(cins, couts, csems)

        @pl.when((j == 1) & (i == 0))
        def _():
            load_shard(1, 0)

        @pl.when((j == 2) & (i == max(ni - 2, 0)))
        def _():
            for cp in relays():
                cp.wait_recv()
            forward(2, c_).start()
            load_shard(2, 1)

        @pl.when((j == nsh - 1) & (i == ni - 1))
        def _():
            for cp in sends() + relays() + [forward(p, c_) for p in range(nsh - 1)]:
                cp.wait_send()
            for cp in owns():
                cp.wait()
            comm.finish(cins, couts, csems)

    last = ni - 1
    blocks = _nbytes((tm, d), F32) + _nbytes((tm, ws), BF16) + _nbytes((tm, d), BF16)
    scratch = _nbytes((t, d), BF16) + 2 * _nbytes((d, ws), BF16)
    res = pl.pallas_call(
        body, name="fwd_in",
        grid_spec=pltpu.PrefetchScalarGridSpec(
            num_scalar_prefetch=1, grid=(nsh, ni),
            in_specs=[pl.BlockSpec((tm, d), lambda j, i, o: (jnp.where(j == 0, i, last), 0)),
                      pl.BlockSpec((1, d), lambda j, i, o: (0, 0)), ANY] + [ANY] * nci,
            out_specs=[pl.BlockSpec((tm, ws), lambda j, i, o: (i, o[j])),
                       pl.BlockSpec((tm, d), lambda j, i, o: (jnp.where(j == 0, i, last), 0)), ANY] + [ANY] * nco,
            scratch_shapes=[pltpu.VMEM((t, d), BF16), pltpu.VMEM((2, d, ws), BF16)]
            + _dma_sems(2, 2, 2, 2, 2, nsh - 1, nsh - 1, 2, 2) + comm.sems),
        out_shape=[SDS((t, nsh * ws), BF16), SDS((t, d), BF16), SDS((nsh, d, ws), BF16)] + comm.out_shapes,
        input_output_aliases={4 + i: 3 + o for i, o in comm.aliases.items()},
        compiler_params=_params(("arbitrary", "arbitrary"), blocks, scratch + 3 * _nbytes((tm, d), F32),
                                _collective_id(all_peers)),
    )(order, x, g, w_loc, *comm.ins)
    return list(res[:3]), list(res[3:])


def _mixer_mid_fwd(z, pool_w, pool_scale, conv_w, nseq, comm=None):
    t = z.shape[0]
    d = pool_scale.shape[1]
    s = t // nseq
    c = d // N_GROUPS

    def body(zp, zb, zc, zv, pw, ps, cw, o):
        j = pl.program_id(1)
        row = lax.broadcasted_iota(jnp.int32, (s, c), 0)
        for gi, win in enumerate(POOL_WINDOWS):
            @pl.when(j == gi)
            def _(win=win):
                pooled = _pool_fwd(zp[...].astype(F32), win, row)
                o[0] = (_dot(pooled.astype(BF16), pw[...]) * ps[...]).astype(BF16)

        cv = zc[...].astype(F32) * zv[...].astype(F32)
        cc = (cw[pl.ds(2, 1), :] * cv + cw[pl.ds(1, 1), :] * _shift_down(cv, 1, row)
              + cw[pl.ds(0, 1), :] * _shift_down(cv, 2, row))
        o[1] = (zb[...].astype(F32) * cc).astype(BF16)

    blocks = 4 * _nbytes((s, c), BF16) + _nbytes((c, c), BF16) + _nbytes((2, s, c), BF16)
    return _pcall(
        body, name="mixer_mid_fwd", grid=(nseq, N_GROUPS),
        in_specs=[pl.BlockSpec((s, c), lambda b, j: (b, j)),
                  pl.BlockSpec((s, c), lambda b, j: (b, N_GROUPS + j)),
                  pl.BlockSpec((s, c), lambda b, j: (b, 2 * N_GROUPS + j)),
                  pl.BlockSpec((s, c), lambda b, j: (b, 3 * N_GROUPS + j)),
                  pl.BlockSpec((None, c, c), lambda b, j: (j, 0, 0)),
                  pl.BlockSpec((1, c), lambda b, j: (0, j)),
                  pl.BlockSpec((3, c), lambda b, j: (0, j))],
        out_specs=[pl.BlockSpec((2, s, c), lambda b, j: (0, b, j))],
        out_shape=[SDS((3, t, d), BF16)],
        sem=("parallel", "parallel"), blocks=blocks, temps=8 * _nbytes((s, c), F32), comm=comm,
    )(z, z, z, z, pool_w, pool_scale, conv_w)


def _mixer_out(lhs3, z, x, w3, g_ffn, comm=None):
    t, d = x.shape
    tm = _tile(t, 256, SUBLANES_BF16)

    def body(pq, zgp, zgc, x_ref, w_ref, g_ref, mrg, ypc, x1o, h2o):
        yp = _dot(pq[0], w_ref[0])
        yc = _dot(pq[1], w_ref[1])
        m = _sigmoid(zgp[...].astype(F32)) * yp + _sigmoid(zgc[...].astype(F32)) * yc
        mb = m.astype(BF16)
        x1 = x_ref[...] + _dot(mb, w_ref[2])
        ypc[0] = yp.astype(BF16)
        ypc[1] = yc.astype(BF16)
        mrg[...] = mb
        x1o[...] = x1
        xh, _ = _rms_fwd(x1)
        h2o[...] = (xh * g_ref[...]).astype(BF16)

    blocks = (_nbytes((2, tm, d), BF16) * 2 + _nbytes((tm, d), BF16) * 4 + _nbytes((tm, d), F32) * 2
              + _nbytes((3, d, d), BF16))
    return _pcall(
        body, name="mixer_out", grid=(t // tm,),
        in_specs=[pl.BlockSpec((2, tm, d), lambda i: (0, i, 0)),
                  pl.BlockSpec((tm, d), lambda i: (i, 4)),
                  pl.BlockSpec((tm, d), lambda i: (i, 5)),
                  pl.BlockSpec((tm, d), lambda i: (i, 0)),
                  pl.BlockSpec((3, d, d), lambda i: (0, 0, 0)),
                  pl.BlockSpec((1, d), lambda i: (0, 0))],
        out_specs=[pl.BlockSpec((None, tm, d), lambda i: (2, i, 0)),
                   pl.BlockSpec((2, tm, d), lambda i: (0, i, 0)),
                   pl.BlockSpec((tm, d), lambda i: (i, 0)),
                   pl.BlockSpec((tm, d), lambda i: (i, 0))],
        out_shape=[SDS(lhs3.shape, BF16), SDS((2, t, d), BF16), SDS((t, d), F32), SDS((t, d), BF16)],
        input_output_aliases={0: 0},
        sem=("parallel",), blocks=blocks, temps=8 * _nbytes((tm, d), F32), comm=comm,
    )(lhs3, z, z, x, w3, g_ffn)


def _ffn_up(h2, w_up, f, comm=None):
    t, d = h2.shape
    _, _, ws = w_up.shape
    tm = _tile(t, 1024, SUBLANES_BF16)
    tn = _tile(ws, 1408, LANES)
    nps = ws // tn
    npp = f // tn

    def body(h_ref, w_ref, o_ref):
        o_ref[...] = _dot(h_ref[...], w_ref[...]).astype(BF16)

    blocks = _nbytes((tm, d), BF16) + _nbytes((d, tn), BF16) + _nbytes((tm, tn), BF16)
    return _pcall(
        body, name="ffn_up", grid=(t // tm, 2 * npp),
        in_specs=[pl.BlockSpec((tm, d), lambda i, j: (i, 0)),
                  pl.BlockSpec((None, d, tn), lambda i, j: (j // nps, 0, j % nps))],
        out_specs=[pl.BlockSpec((None, tm, tn), lambda i, j: (j // npp, i, j % npp))],
        out_shape=[SDS((2, t, f), BF16)],
        sem=("parallel", "parallel"), blocks=blocks, temps=_nbytes((tm, tn), F32), comm=comm,
    )(h2, w_up)


def _conv3_rows(u, u1, u2, w_ref, p):
    return w_ref[p, pl.ds(2, 1), :] * u + w_ref[p, pl.ds(1, 1), :] * u1 + w_ref[p, pl.ds(0, 1), :] * u2


WGRAD_TOKENS = 2048
WGRAD_TOKENS_WIDE = 4096
CHUNK = 64
HALO = SUBLANES_F32


def _up1_up2(u, nxt):
    rows = u.shape[0]
    ext = jnp.concatenate([u, nxt], axis=0)
    n = rows + HALO
    return pltpu.roll(ext, n - 1, 0)[:rows], pltpu.roll(ext, n - 2, 0)[:rows]


def _fold8(x):
    return jnp.sum(x.reshape(x.shape[0] // SUBLANES_F32, SUBLANES_F32, x.shape[1]), axis=0)


def _ffn_mid_fwd(u0, cw, cb, nseq):
    _, t, f = u0.shape
    s = t // nseq
    c = _tile(f, 256, LANES)

    def body(u_ref, w_ref, b_ref, a_ref, uo_ref):
        row = lax.broadcasted_iota(jnp.int32, (s, c), 0)
        act = []
        for p in range(2):
            u = u_ref[p].astype(F32)
            act.append(_conv3_rows(u, _shift_down(u, 1, row), _shift_down(u, 2, row), w_ref, p) + b_ref[p])
            uo_ref[p] = act[p].astype(BF16)
        ug, uv = act
        a_ref[...] = (ug * _sigmoid(ug) * uv).astype(BF16)

    blocks = 2 * _nbytes((2, s, c), BF16) + _nbytes((s, c), BF16)
    outs, _ = _pcall(
        body, name="ffn_mid_fwd", grid=(f // c, nseq),
        in_specs=[pl.BlockSpec((2, s, c), lambda j, b: (0, b, j)),
                  pl.BlockSpec((2, 3, c), lambda j, b: (0, 0, j)),
                  pl.BlockSpec((2, 1, c), lambda j, b: (0, 0, j))],
        out_specs=[pl.BlockSpec((s, c), lambda j, b: (b, j)),
                   pl.BlockSpec((2, s, c), lambda j, b: (0, b, j))],
        out_shape=[SDS((t, f), BF16), SDS((2, t, f), BF16)],
        sem=("parallel", "parallel"), blocks=blocks, temps=8 * _nbytes((s, c), F32),
    )(u0, cw, cb)
    return outs


def _ffn_down_loss(a, w_down, x1, tgt, g_fin):
    t, f = a.shape
    d = x1.shape[1]
    tm = _tile(t, 256, SUBLANES_BF16)
    nsteps = t // tm

    def body(a_ref, w_ref, x1_ref, t_ref, g_ref, dx_ref, dxb_ref, loss_ref, gg_ref, lacc):
        i = pl.program_id(0)

        @pl.when(i == 0)
        def _():
            lacc[...] = jnp.zeros_like(lacc)
            gg_ref[...] = jnp.zeros_like(gg_ref)

        x2 = x1_ref[...] + _dot(a_ref[...], w_ref[...])
        xh, inv = _rms_fwd(x2)
        g = g_ref[...]
        e = xh * g - t_ref[...]
        lacc[...] += jnp.sum(e * e, axis=0, keepdims=True)
        dy = e * (1.0 / d)
        gg_ref[...] += jnp.sum(dy * xh, axis=0, keepdims=True)
        dx2 = _rms_bwd(dy, xh, inv, g)
        dx_ref[...] = dx2
        dxb_ref[...] = dx2.astype(BF16)

        @pl.when(i == nsteps - 1)
        def _():
            loss_ref[...] = jnp.sum(lacc[...], axis=1, keepdims=True) * (0.5 / d)

    blocks = (_nbytes((tm, f), BF16) + _nbytes((f, d), BF16) + 3 * _nbytes((tm, d), F32) + _nbytes((tm, d), BF16))
    outs, _ = _pcall(
        body, name="ffn_down_loss", grid=(nsteps,),
        in_specs=[pl.BlockSpec((tm, f), lambda i: (i, 0)), pl.BlockSpec((f, d), lambda i: (0, 0)),
                  pl.BlockSpec((tm, d), lambda i: (i, 0)), pl.BlockSpec((tm, d), lambda i: (i, 0)),
                  pl.BlockSpec((1, d), lambda i: (0, 0))],
        out_specs=[pl.BlockSpec((tm, d), lambda i: (i, 0)), pl.BlockSpec((tm, d), lambda i: (i, 0)),
                   pl.BlockSpec((1, 1), lambda i: (0, 0)), pl.BlockSpec((1, d), lambda i: (0, 0))],
        out_shape=[SDS((t, d), F32), SDS((t, d), BF16), SDS((1, 1), F32), SDS((1, d), F32)],
        scratch_shapes=[pltpu.VMEM((1, d), F32)],
        sem=("arbitrary",), blocks=blocks, temps=8 * _nbytes((tm, d), F32),
    )(a, w_down, x1, tgt, g_fin)
    return outs


def _ffn_bwd_da(dxb, w_down, comm=None):
    t, d = dxb.shape
    f = w_down.shape[0]
    tm = _tile(t, 512, SUBLANES_BF16)

    def body(x_ref, w_ref, o_ref):
        o_ref[...] = _dot_tb(x_ref[...], w_ref[...]).astype(BF16)

    blocks = _nbytes((tm, d), BF16) + _nbytes((tm, f), BF16)
    return _pcall(
        body, name="ffn_bwd_da", grid=(t // tm,),
        in_specs=[pl.BlockSpec((tm, d), lambda i: (i, 0)),
                  pl.BlockSpec((f, d), lambda i: (0, 0), pipeline_mode=pl.Buffered(1))],
        out_specs=[pl.BlockSpec((tm, f), lambda i: (i, 0))],
        out_shape=[SDS((t, f), BF16)],
        sem=("parallel",), blocks=blocks, temps=_nbytes((f, d), BF16) + _nbytes((tm, f), F32), comm=comm,
    )(dxb, w_down)


def _ffn_mid_bwd(da, u0, ua, cw, nseq, comm=None):
    _, t, f = u0.shape
    s = t // nseq
    c = _tile(f, 128, LANES)
    r = _tile(s, CHUNK, SUBLANES_BF16)
    n = s // r

    def body(da_ref, u_ref, ua_ref, w_ref, du_ref, gw_ref, gb_ref):
        @pl.when(pl.program_id(1) == 0)
        def _():
            gw_ref[...] = jnp.zeros_like(gw_ref)
            gb_ref[...] = jnp.zeros_like(gb_ref)

        def step(i, carry):
            nxt, sums = carry
            rows = pl.ds(pl.multiple_of((n - 1 - i) * r, r), r)
            ug = ua_ref[0, rows, :].astype(F32)
            uv = ua_ref[1, rows, :].astype(F32)
            sg = _sigmoid(ug)
            dacc = da_ref[rows, :].astype(F32)
            dus = (dacc * uv * sg * (1.0 + ug * (1.0 - sg)), dacc * (ug * sg))
            first, new_sums = [], []
            for p in range(2):
                du = dus[p]
                d1, d2 = _up1_up2(du, nxt[p])
                du_ref[p, rows, :] = _conv3_rows(du, d1, d2, w_ref, p).astype(BF16)
                u = u_ref[p, rows, :].astype(F32)
                sb, s0, s1, s2 = sums[p]
                new_sums.append((sb + _fold8(du), s0 + _fold8(d2 * u), s1 + _fold8(d1 * u), s2 + _fold8(du * u)))
                first.append(du[:HALO])
            return tuple(first), tuple(new_sums)

        zero = jnp.zeros((HALO, c), F32)
        _, sums = lax.fori_loop(0, n, step, ((zero, zero), ((zero,) * 4,) * 2))
        for p in range(2):
            sb, s0, s1, s2 = sums[p]
            gb_ref[p] += jnp.sum(sb, axis=0, keepdims=True)
            gw_ref[p, pl.ds(0, 1), :] += jnp.sum(s0, axis=0, keepdims=True)
            gw_ref[p, pl.ds(1, 1), :] += jnp.sum(s1, axis=0, keepdims=True)
            gw_ref[p, pl.ds(2, 1), :] += jnp.sum(s2, axis=0, keepdims=True)

    blocks = _nbytes((s, c), BF16) + 3 * _nbytes((2, s, c), BF16)
    return _pcall(
        body, name="ffn_mid_bwd", grid=(f // c, nseq),
        in_specs=[pl.BlockSpec((s, c), lambda j, b: (b, j)),
                  pl.BlockSpec((2, s, c), lambda j, b: (0, b, j)),
                  pl.BlockSpec((2, s, c), lambda j, b: (0, b, j)),
                  pl.BlockSpec((2, 3, c), lambda j, b: (0, 0, j))],
        out_specs=[pl.BlockSpec((2, s, c), lambda j, b: (0, b, j)),
                   pl.BlockSpec((2, 3, c), lambda j, b: (0, 0, j)),
                   pl.BlockSpec((2, 1, c), lambda j, b: (0, 0, j))],
        out_shape=[SDS((2, t, f), BF16), SDS((2, 3, f), F32), SDS((2, 1, f), F32)],
        sem=("parallel", "arbitrary"), blocks=blocks, temps=4 * 1024 * 1024, comm=comm,
    )(da, u0, ua, cw)


def _wgrad(a, b, name, *, tr, tn, b_plane_of=None, out_shards=None, comm=None):
    t, m = a.shape
    n_total = b.shape[-1] * (b.shape[0] if b.ndim == 3 else 1)
    tk = _tile(t, WGRAD_TOKENS_WIDE if n_total > tn and m == tr else WGRAD_TOKENS, SUBLANES_BF16)
    nk = t // tk
    once = pl.Buffered(1) if nk == 1 else None

    def body(a_ref, b_ref, o_ref, *acc):
        part = _dot_ta(a_ref[...], b_ref[...])
        if nk == 1:
            o_ref[...] = part.astype(BF16)
        else:
            _acc_over(pl.program_id(2), nk, part, acc[0], o_ref)

    if b.ndim == 3:
        b_spec = pl.BlockSpec((None, tk, tn), lambda r, n, k: (b_plane_of(n)[0], k, b_plane_of(n)[1]))
    else:
        b_spec = pl.BlockSpec((tk, tn), lambda r, n, k: (k, n), pipeline_mode=once if n_total == tn else None)
    if out_shards is None:
        o_spec = pl.BlockSpec((tr, tn), lambda r, n, k: (r, n))
        o_shape = SDS((m, n_total), BF16)
    else:
        nps = n_total // out_shards // tn
        o_spec = pl.BlockSpec((None, tr, tn), lambda r, n, k: (n // nps, r, n % nps))
        o_shape = SDS((out_shards, m, n_total // out_shards), BF16)
    blocks = _nbytes((tk, tr), BF16) + _nbytes((tk, tn), BF16) + _nbytes((tr, tn), BF16)
    return _pcall(
        body, name=name, grid=(m // tr, n_total // tn, nk),
        in_specs=[pl.BlockSpec((tk, tr), lambda r, n, k: (k, r), pipeline_mode=once if m == tr else None), b_spec],
        out_specs=[o_spec], out_shape=[o_shape],
        scratch_shapes=[] if nk == 1 else [pltpu.VMEM((tr, tn), F32)],
        sem=("parallel", "parallel", "arbitrary"), blocks=blocks, temps=2 * _nbytes((tr, tn), F32), comm=comm,
    )(a, b)


def _wgrad3(lhs3, rhs3, comm=None):
    nw, t, d = lhs3.shape
    tk = _tile(t, WGRAD_TOKENS, SUBLANES_BF16)
    nk = t // tk

    def body(a_ref, b_ref, o_ref, *acc):
        part = _dot_ta(a_ref[...], b_ref[...])
        if nk == 1:
            o_ref[...] = part.astype(BF16)
        else:
            _acc_over(pl.program_id(1), nk, part, acc[0], o_ref)

    blocks = 2 * _nbytes((tk, d), BF16) + _nbytes((d, d), BF16)
    return _pcall(
        body, name="wgrad_sq3", grid=(nw, nk),
        in_specs=[pl.BlockSpec((None, tk, d), lambda w, k: (w, k, 0)),
                  pl.BlockSpec((None, tk, d), lambda w, k: (w, k, 0))],
        out_specs=[pl.BlockSpec((None, d, d), lambda w, k: (w, 0, 0))],
        out_shape=[SDS((nw, d, d), BF16)],
        scratch_shapes=[] if nk == 1 else [pltpu.VMEM((d, d), F32)],
        sem=("parallel", "arbitrary"), blocks=blocks, temps=2 * _nbytes((d, d), F32), comm=comm,
    )(lhs3, rhs3)


def _ffn_bwd_dx1(du0, w_up, x1, dx2, g_ffn, n_planes_out, comm=None):
    _, t, f = du0.shape
    d = x1.shape[1]
    nsh, _, ws = w_up.shape
    tm = _tile(t, 256, SUBLANES_BF16)
    spp = f // ws

    def body(du_ref, w_ref, x1_ref, dx2_ref, g_ref, dx1_ref, dxb_ref, gg_ref):
        @pl.when(pl.program_id(0) == 0)
        def _():
            gg_ref[...] = jnp.zeros_like(gg_ref)

        dh = None
        for k in range(nsh):
            part = _dot_tb(du_ref[k // spp, :, (k % spp) * ws:(k % spp + 1) * ws], w_ref[k])
            dh = part if dh is None else dh + part
        xh, inv = _rms_fwd(x1_ref[...])
        gg_ref[...] += jnp.sum(dh * xh, axis=0, keepdims=True)
        dx1 = dx2_ref[...] + _rms_bwd(dh, xh, inv, g_ref[...])
        dx1_ref[...] = dx1
        dxb_ref[...] = dx1.astype(BF16)

    blocks = _nbytes((2, tm, f), BF16) + 3 * _nbytes((tm, d), F32) + _nbytes((tm, d), BF16)
    return _pcall(
        body, name="ffn_bwd_dx1", grid=(t // tm,),
        in_specs=[pl.BlockSpec((2, tm, f), lambda i: (0, i, 0)),
                  pl.BlockSpec((nsh, d, ws), lambda i: (0, 0, 0), pipeline_mode=pl.Buffered(1)),
                  pl.BlockSpec((tm, d), lambda i: (i, 0)),
                  pl.BlockSpec((tm, d), lambda i: (i, 0)),
                  pl.BlockSpec((1, d), lambda i: (0, 0))],
        out_specs=[pl.BlockSpec((tm, d), lambda i: (i, 0)),
                   pl.BlockSpec((None, tm, d), lambda i: (n_planes_out - 1, i, 0)),
                   pl.BlockSpec((1, d), lambda i: (0, 0))],
        out_shape=[SDS((t, d), F32), SDS((n_planes_out, t, d), BF16), SDS((1, d), F32)],
        sem=("arbitrary",), blocks=blocks, temps=_nbytes(w_up.shape, BF16) + 8 * _nbytes((tm, d), F32), comm=comm,
    )(du0, w_up, x1, dx2, g_ffn)


def _mixer_bwd(rhs3, z, ypc, w3, comm=None):
    _, t, d = rhs3.shape
    tm = _tile(t, 512, SUBLANES_BF16)

    def body(dx_ref, zgp, zgc, ypc_ref, w_ref, dyo, dzo, dpq):
        dm = _dot_tb(dx_ref[...], w_ref[2])
        sp = _sigmoid(zgp[...].astype(F32))
        sc = _sigmoid(zgc[...].astype(F32))
        dyp = (dm * sp).astype(BF16)
        dyc = (dm * sc).astype(BF16)
        dzo[0] = (dm * ypc_ref[0].astype(F32) * sp * (1.0 - sp)).astype(BF16)
        dzo[1] = (dm * ypc_ref[1].astype(F32) * sc * (1.0 - sc)).astype(BF16)
        dyo[0] = dyp
        dyo[1] = dyc
        dpq[0] = _dot_tb(dyp, w_ref[0]).astype(BF16)
        dpq[1] = _dot_tb(dyc, w_ref[1]).astype(BF16)

    blocks = _nbytes((tm, d), BF16) * 3 + _nbytes((2, tm, d), BF16) * 4 + _nbytes((3, d, d), BF16)
    return _pcall(
        body, name="mixer_bwd", grid=(t // tm,),
        in_specs=[pl.BlockSpec((None, tm, d), lambda i: (2, i, 0)),
                  pl.BlockSpec((tm, d), lambda i: (i, 4)),
                  pl.BlockSpec((tm, d), lambda i: (i, 5)),
                  pl.BlockSpec((2, tm, d), lambda i: (0, i, 0)),
                  pl.BlockSpec((3, d, d), lambda i: (0, 0, 0))],
        out_specs=[pl.BlockSpec((2, tm, d), lambda i: (0, i, 0)),
                   pl.BlockSpec((2, tm, d), lambda i: (2, i, 0)),
                   pl.BlockSpec((2, tm, d), lambda i: (0, i, 0))],
        out_shape=[SDS(rhs3.shape, BF16), SDS((N_SPLITS, t, d), BF16), SDS((2, t, d), BF16)],
        input_output_aliases={0: 0},
        sem=("parallel",), blocks=blocks, temps=8 * _nbytes((tm, d), F32), comm=comm,
    )(rhs3, z, z, ypc, w3)


def _conv_bwd(dz, dpq, z, conv_w, nseq, comm=None):
    _, t, d = dz.shape
    s = t // nseq
    c = _tile(d, 128, LANES)
    nb = d // c

    def body(dz_in, dq_ref, zb, zc, zv, cw, dzo, gw_ref):
        del dz_in

        @pl.when(pl.program_id(1) == 0)
        def _():
            gw_ref[...] = jnp.zeros_like(gw_ref)

        row = lax.broadcasted_iota(jnp.int32, (s, c), 0)
        b = zb[...].astype(F32)
        cm = zc[...].astype(F32)
        v = zv[...].astype(F32)
        cv = cm * v
        cv1 = _shift_down(cv, 1, row)
        cv2 = _shift_down(cv, 2, row)
        w0, w1, w2 = cw[pl.ds(0, 1), :], cw[pl.ds(1, 1), :], cw[pl.ds(2, 1), :]
        cc = w2 * cv + w1 * cv1 + w0 * cv2
        dq = dq_ref[...].astype(F32)
        dzo[0] = (dq * cc).astype(BF16)
        dcc = dq * b
        gw_ref[pl.ds(0, 1), :] += jnp.sum(dcc * cv2, axis=0, keepdims=True)
        gw_ref[pl.ds(1, 1), :] += jnp.sum(dcc * cv1, axis=0, keepdims=True)
        gw_ref[pl.ds(2, 1), :] += jnp.sum(dcc * cv, axis=0, keepdims=True)
        dcv = w2 * dcc + w1 * _shift_up(dcc, 1, row) + w0 * _shift_up(dcc, 2, row)
        dzo[1] = (dcv * v).astype(BF16)
        dzo[2] = (dcv * cm).astype(BF16)

    blocks = 4 * _nbytes((s, c), BF16) + _nbytes((3, s, c), BF16)
    return _pcall(
        body, name="conv_bwd", grid=(nb, nseq),
        in_specs=[ANY,
                  pl.BlockSpec((None, s, c), lambda j, b: (1, b, j)),
                  pl.BlockSpec((s, c), lambda j, b: (b, nb + j)),
                  pl.BlockSpec((s, c), lambda j, b: (b, 2 * nb + j)),
                  pl.BlockSpec((s, c), lambda j, b: (b, 3 * nb + j)),
                  pl.BlockSpec((3, c), lambda j, b: (0, j))],
        out_specs=[pl.BlockSpec((3, s, c), lambda j, b: (0, b, j)),
                   pl.BlockSpec((3, c), lambda j, b: (0, j))],
        out_shape=[SDS(dz.shape, BF16), SDS((3, d), F32)],
        input_output_aliases={0: 0},
        sem=("parallel", "arbitrary"), blocks=blocks, temps=16 * _nbytes((s, c), F32), comm=comm,
    )(dz, dpq, z, z, z, conv_w)


def _pool_bwd_call(dz, dpq, z, pool_w, pool_scale, nseq, comm=None):
    _, t, d = dz.shape
    s = t // nseq
    c = d // N_GROUPS

    def body(dz_in, dp_ref, zp, pw, ps, dzo, gpw_ref, gps_ref):
        del dz_in
        j = pl.program_id(0)

        @pl.when(pl.program_id(1) == 0)
        def _():
            gpw_ref[...] = jnp.zeros_like(gpw_ref)
            gps_ref[...] = jnp.zeros_like(gps_ref)

        row = lax.broadcasted_iota(jnp.int32, (s, c), 0)
        for gi, win in enumerate(POOL_WINDOWS):
            @pl.when(j == gi)
            def _(win=win):
                pb = _pool_fwd(zp[...].astype(F32), win, row).astype(BF16)
                plin = _dot(pb, pw[...])
                dps = dp_ref[...].astype(F32)
                gps_ref[...] += jnp.sum(dps * plin, axis=0, keepdims=True)
                dplb = (dps * ps[...]).astype(BF16)
                gpw_ref[...] += _dot_ta(pb, dplb)
                dzo[...] = _pool_bwd(_dot_tb(dplb, pw[...]), win, row).astype(BF16)

    blocks = 3 * _nbytes((s, c), BF16) + _nbytes((c, c), BF16) + _nbytes((c, c), F32)
    return _pcall(
        body, name="pool_bwd", grid=(N_GROUPS, nseq),
        in_specs=[ANY,
                  pl.BlockSpec((None, s, c), lambda j, b: (0, b, j)),
                  pl.BlockSpec((s, c), lambda j, b: (b, j)),
                  pl.BlockSpec((None, c, c), lambda j, b: (j, 0, 0)),
                  pl.BlockSpec((1, c), lambda j, b: (0, j))],
        out_specs=[pl.BlockSpec((None, s, c), lambda j, b: (3, b, j)),
                   pl.BlockSpec((None, c, c), lambda j, b: (j, 0, 0)),
                   pl.BlockSpec((1, c), lambda j, b: (0, j))],
        out_shape=[SDS(dz.shape, BF16), SDS((N_GROUPS, c, c), F32), SDS((1, d), F32)],
        input_output_aliases={0: 0},
        sem=("parallel", "arbitrary"), blocks=blocks, temps=10 * _nbytes((s, c), F32), comm=comm,
    )(dz, dpq, z, pool_w, pool_scale)


def _dz_plane(zb):
    return jnp.where(zb < 4, (zb + 3) % 4, zb)


def _wgrad_in(h1, dz, nsh, comm=None):
    t, d = h1.shape
    ws = N_SPLITS * d // nsh
    kb = _tile(math.gcd(d, ws), 512, LANES)
    npl = d // kb
    nps = ws // kb
    tk = _tile(t, WGRAD_TOKENS_WIDE, SUBLANES_BF16)
    nk = t // tk

    def body(a_ref, b_ref, o_ref, *acc):
        part = _dot_ta(a_ref[...], b_ref[...])
        if nk == 1:
            o_ref[...] = part.astype(BF16)
        else:
            _acc_over(pl.program_id(1), nk, part, acc[0], o_ref)

    blocks = _nbytes((tk, d), BF16) + _nbytes((tk, kb), BF16) + _nbytes((d, kb), BF16)
    return _pcall(
        body, name="wgrad_in", grid=(N_SPLITS * npl, nk),
        in_specs=[pl.BlockSpec((tk, d), lambda cb, k: (k, 0), pipeline_mode=pl.Buffered(1) if nk == 1 else None),
                  pl.BlockSpec((None, tk, kb), lambda cb, k: (_dz_plane(cb // npl), k, cb % npl))],
        out_specs=[pl.BlockSpec((None, d, kb), lambda cb, k: (cb // nps, 0, cb % nps))],
        out_shape=[SDS((nsh, d, ws), BF16)],
        scratch_shapes=[] if nk == 1 else [pltpu.VMEM((d, kb), F32)],
        sem=("parallel", "arbitrary"), blocks=blocks, temps=2 * _nbytes((d, kb), F32), comm=comm,
    )(h1, dz)


def _mixer_bwd_dx(dz, w_in, x, dx1, g_mix, comm=None):
    npln, t, d = dz.shape
    nsh, _, ws = w_in.shape
    tm = _tile(t, 256, SUBLANES_BF16)
    kb = _tile(math.gcd(d, ws), 512, LANES)
    npl = d // kb
    nps = ws // kb

    def body(dz_ref, w_ref, x_ref, dx1_ref, g_ref, dx_ref, gg_ref):
        @pl.when(pl.program_id(0) == 0)
        def _():
            gg_ref[...] = jnp.zeros_like(gg_ref)

        dh = None
        for cb in range(npln * npl):
            zb = cb // npl
            plane = (zb + 3) % 4 if zb < 4 else zb
            part = _dot_tb(dz_ref[plane, :, (cb % npl) * kb:(cb % npl + 1) * kb],
                           w_ref[cb // nps, :, (cb % nps) * kb:(cb % nps + 1) * kb])
            dh = part if dh is None else dh + part
        xh, inv = _rms_fwd(x_ref[...])
        gg_ref[...] += jnp.sum(dh * xh, axis=0, keepdims=True)
        dx_ref[...] = dx1_ref[...] + _rms_bwd(dh, xh, inv, g_ref[...])

    blocks = _nbytes((npln, tm, d), BF16) + 3 * _nbytes((tm, d), F32)
    return _pcall(
        body, name="mixer_bwd_dx", grid=(t // tm,),
        in_specs=[pl.BlockSpec((npln, tm, d), lambda i: (0, i, 0)),
                  pl.BlockSpec((nsh, d, ws), lambda i: (0, 0, 0), pipeline_mode=pl.Buffered(1)),
                  pl.BlockSpec((tm, d), lambda i: (i, 0)),
                  pl.BlockSpec((tm, d), lambda i: (i, 0)),
                  pl.BlockSpec((1, d), lambda i: (0, 0))],
        out_specs=[pl.BlockSpec((tm, d), lambda i: (i, 0)),
                   pl.BlockSpec((1, d), lambda i: (0, 0))],
        out_shape=[SDS((t, d), F32), SDS((1, d), F32)],
        sem=("arbitrary",), blocks=blocks, temps=_nbytes(w_in.shape, BF16) + 8 * _nbytes((tm, d), F32), comm=comm,
    )(dz, w_in, x, dx1, g_mix)


N_BIG = 5
SHARD_MAJOR = (0, 2)
ROWS_DIM1 = (1, 4)


def _ds(start, size, align):
    if isinstance(start, int):
        return pl.ds(start, size)
    return pl.ds(pl.multiple_of(start, align), size)


def _piece(a, ref, k, h):
    if a in SHARD_MAJOR:
        r = ref.shape[1] // 2
        return ref.at[k, _ds(h * r, r, SUBLANES_BF16), :]
    if a in ROWS_DIM1:
        r = ref.shape[1] // 8
        return ref.at[:, _ds((2 * k + h) * r, r, SUBLANES_BF16), :]
    r = ref.shape[0] // 8
    return ref.at[_ds((2 * k + h) * r, r, SUBLANES_BF16), :]


def _half(a, ref, h):
    if a in ROWS_DIM1:
        r = ref.shape[1] // 2
        return ref.at[:, _ds(h * r, r, SUBLANES_BF16), :]
    r = ref.shape[0] // 2
    return ref.at[_ds(h * r, r, SUBLANES_BF16), :]


def _piece_shape(a, full_shape):
    if a in SHARD_MAJOR:
        return (full_shape[1] // 2, full_shape[2])
    if a in ROWS_DIM1:
        return (full_shape[0], full_shape[1] // 8, full_shape[2])
    return (full_shape[0] // 8, full_shape[1])


def _shard_shape(a, full_shape):
    if a in SHARD_MAJOR:
        return (full_shape[1], full_shape[2])
    if a in ROWS_DIM1:
        return (full_shape[0], full_shape[1] // 4, full_shape[2])
    return (full_shape[0] // 4, full_shape[1])


def _rows_axis(a):
    return 1 if a in ROWS_DIM1 else 0


def _piece_block(a, full_shape):
    ps = _piece_shape(a, full_shape)
    if a in SHARD_MAJOR:
        return (None,) + ps, lambda k, c: (k, c, 0)
    if a in ROWS_DIM1:
        return ps, lambda k, c: (0, 2 * k + c, 0)
    return ps, lambda k, c: (2 * k + c, 0)


def _coords():
    return lax.axis_index("x"), lax.axis_index("y"), lax.axis_index("c")


def _peer_chips(x, y):
    return [(1 - x, y), (x, 1 - y), (1 - x, 1 - y)]


def _remote(src, dst, ssem, rsem, dev):
    return pltpu.make_async_remote_copy(src_ref=src, dst_ref=dst, send_sem=ssem, recv_sem=rsem,
                                        device_id=dev, device_id_type=MESH)


def _dma_sems(*counts):
    return [pltpu.SemaphoreType.DMA((n,)) for n in counts]


def _symmetric(ins, out_shapes, sems, copies, peers, aliases=None):
    def start(cins, couts, csems):
        for cp in copies(cins, couts, csems):
            cp.start()

    def finish(cins, couts, csems):
        for cp in copies(cins, couts, csems):
            cp.wait()

    return _Comm(ins, out_shapes, sems, start, finish, peers, aliases)


def _rows_part(a, ref, part):
    if part is None:
        return ref
    p, q, n = part
    ax = _rows_axis(a)
    r = ref.shape[ax] // n
    return ref.at[tuple(pl.ds(p * r, (q - p) * r) if d == ax else slice(None) for d in range(len(ref.shape)))]


def _merge(comms):
    ins, outs, sems, aliases, spans = [], [], [], {}, []
    for cm in comms:
        spans.append((len(ins), len(outs), len(sems)))
        for i, o in cm.aliases.items():
            aliases[len(ins) + i] = len(outs) + o
        ins += cm.ins
        outs += cm.out_shapes
        sems += cm.sems

    def each(fn_name):
        def run(cins, couts, csems):
            for cm, (i0, o0, s0) in zip(comms, spans):
                fn = getattr(cm, fn_name)
                if fn is not None:
                    fn(cins[i0:i0 + len(cm.ins)], couts[o0:o0 + len(cm.out_shapes)], csems[s0:s0 + len(cm.sems)])
        return run

    return _Comm(ins, outs, sems, each("start"), each("finish"), frozenset().union(*[cm.peers for cm in comms]),
                 aliases, mid=each("mid") if any(cm.mid is not None for cm in comms) else None)


def _gather_comm(arrs, locs, full_shapes, part=None, into=None):
    n = len(arrs)

    def own(cins, couts, csems):
        x, y, c = _coords()
        j = 2 * x + y
        return [_remote(_rows_part(a, _half(a, cins[q], h), part), _rows_part(a, _piece(a, couts[q], j, h), part),
                        csems[0].at[2 * q + h], csems[1].at[2 * q + h], (x, y, 1 - c))
                for q, a in enumerate(arrs) for h in range(2)]

    def sends(cins, couts, csems):
        x, y, c = _coords()
        j = 2 * x + y
        return [_remote(_rows_part(a, _half(a, cins[q], c), part), _rows_part(a, _piece(a, couts[q], j, c), part),
                        csems[2].at[3 * q + i], csems[3].at[3 * q + i], (px, py, c))
                for q, a in enumerate(arrs) for i, (px, py) in enumerate(_peer_chips(x, y))]

    def forwards(couts, csems, half_of):
        x, y, c = _coords()
        out = []
        for q, a in enumerate(arrs):
            for i, (px, py) in enumerate(_peer_chips(x, y)):
                landed = _rows_part(a, _piece(a, couts[q], 2 * px + py, half_of(c)), part)
                out.append(_remote(landed, landed, csems[4].at[3 * q + i], csems[5].at[3 * q + i], (x, y, 1 - c)))
        return out

    def start(cins, couts, csems):
        for cp in sends(cins, couts, csems) + own(cins, couts, csems):
            cp.start()

    def finish(cins, couts, csems):
        fw = forwards(couts, csems, lambda c: c)
        for cp, f in zip(sends(cins, couts, csems), fw):
            cp.wait_recv()
            f.start()
        for f in forwards(couts, csems, lambda c: 1 - c):
            f.wait_recv()
        for cp in sends(cins, couts, csems) + fw:
            cp.wait_send()
        for cp in own(cins, couts, csems):
            cp.wait()

    ins = [locs[a] for a in arrs] + ([into[a] for a in arrs] if into else [])
    return _Comm(ins, [SDS(full_shapes[a], BF16) for a in arrs],
                 _dma_sems(2 * n, 2 * n, 3 * n, 3 * n, 3 * n, 3 * n), start, finish, CHIPS + (SIBLING,),
                 aliases={n + q: q for q in range(n)} if into else None)


def _ring_gather_comm(arrs, locs, full_shapes):
    n = len(arrs)

    def own(cins, couts, csems):
        x, y, c = _coords()
        j = 2 * x + y
        return [_remote(_half(a, cins[q], h), _piece(a, couts[q], j, h), csems[0].at[2 * q + h],
                        csems[1].at[2 * q + h], (x, y, 1 - c)) for q, a in enumerate(arrs) for h in range(2)]

    def sends(cins, couts, csems):
        x, y, c = _coords()
        j = 2 * x + y
        return [_remote(_half(a, cins[q], c), _piece(a, couts[q], j, c), csems[2].at[2 * q + i],
                        csems[3].at[2 * q + i], (px, py, c))
                for q, a in enumerate(arrs) for i, (px, py) in enumerate(_peer_chips(x, y)[:2])]

    def relays(couts, csems):
        x, y, c = _coords()
        peers = _peer_chips(x, y)
        out = []
        for q, a in enumerate(arrs):
            for r, (src_p, dst_p) in enumerate(((0, 1), (1, 0))):
                sx, sy = peers[src_p]
                rows = _rows_part(a, _piece(a, couts[q], 2 * sx + sy, c), (r, r + 1, 2))
                out.append(_remote(rows, rows, csems[6].at[2 * q + r], csems[7].at[2 * q + r], (*peers[dst_p], c)))
        return out

    def forwards(couts, csems, half_of, which):
        x, y, c = _coords()
        out = []
        for q, a in enumerate(arrs):
            for i in which:
                px, py = _peer_chips(x, y)[i]
                landed = _piece(a, couts[q], 2 * px + py, half_of(c))
                out.append(_remote(landed, landed, csems[4].at[3 * q + i], csems[5].at[3 * q + i], (x, y, 1 - c)))
        return out

    def start(cins, couts, csems):
        for cp in sends(cins, couts, csems) + own(cins, couts, csems):
            cp.start()

    def mid(cins, couts, csems):
        for cp in sends(cins, couts, csems):
            cp.wait_recv()
        for cp in relays(couts, csems) + forwards(couts, csems, lambda c: c, (0, 1)):
            cp.start()

    def finish(cins, couts, csems):
        for cp in relays(couts, csems):
            cp.wait_recv()
        fw_diag = forwards(couts, csems, lambda c: c, (2,))
        for f in fw_diag:
            f.start()
        for f in forwards(couts, csems, lambda c: 1 - c, (0, 1, 2)):
            f.wait_recv()
        for cp in (sends(cins, couts, csems) + relays(couts, csems)
                   + forwards(couts, csems, lambda c: c, (0, 1)) + fw_diag):
            cp.wait_send()
        for cp in own(cins, couts, csems):
            cp.wait()

    return _Comm([locs[a] for a in arrs], [SDS(full_shapes[a], BF16) for a in arrs],
                 _dma_sems(2 * n, 2 * n, 2 * n, 2 * n, 3 * n, 3 * n, 2 * n, 2 * n), start, finish,
                 CHIPS + (SIBLING,), mid=mid)


def _halves_comm(arrs, gbs):
    n = len(arrs)

    def copies(cins, couts, csems):
        x, y, c = _coords()
        return [_remote(_piece(a, cins[q], k, 1 - c), couts[q].at[k], csems[0].at[4 * q + k], csems[1].at[4 * q + k],
                        (x, y, 1 - c)) for q, a in enumerate(arrs) for k in range(4)]

    return _symmetric([gbs[a] for a in arrs], [SDS((4,) + _piece_shape(a, gbs[a].shape), BF16) for a in arrs],
                      _dma_sems(4 * n, 4 * n), copies, [SIBLING])


def _chips_comm(arrs, ps, part=None, into=None):
    n = len(arrs)

    def copies(cins, couts, csems):
        x, y, c = _coords()
        return [_remote(_rows_part(a, cins[q].at[2 * px + py], part), _rows_part(a, couts[q].at[i], part),
                        csems[0].at[3 * q + i], csems[1].at[3 * q + i], (px, py, c))
                for q, a in enumerate(arrs) for i, (px, py) in enumerate(_peer_chips(x, y))]

    ins = [ps[a] for a in arrs] + ([into[a] for a in arrs] if into else [])
    return _symmetric(ins, [SDS((3,) + ps[a].shape[1:], BF16) for a in arrs], _dma_sems(3 * n, 3 * n), copies, CHIPS,
                      aliases={n + q: q for q in range(n)} if into else None)


def _result_comm(arrs, gs):
    n = len(arrs)

    def copies(cins, couts, csems):
        x, y, c = _coords()
        return [_remote(_half(a, cins[q], c), _half(a, couts[q], c), csems[0].at[q], csems[1].at[q], (x, y, 1 - c))
                for q, a in enumerate(arrs)]

    return _symmetric([gs[a] for a in arrs], [SDS(gs[a].shape, F32) for a in arrs], _dma_sems(n, n), copies,
                      [SIBLING], aliases={q: q for q in range(n)})


def _add_halves(arrs, gbs, lands, c_arr, name):
    n = len(arrs)

    def body(c_ref, *refs):
        del c_ref
        for q in range(n):
            refs[2 * n + q][...] = (refs[q][...].astype(F32) + refs[n + q][...].astype(F32)).astype(BF16)

    g_specs, l_specs, o_specs, blocks = [], [], [], 0
    for a in arrs:
        bs, imap = _piece_block(a, gbs[a].shape)
        ps = _piece_shape(a, gbs[a].shape)
        g_specs.append(pl.BlockSpec(bs, lambda k, c_ref, imap=imap: imap(k, c_ref[0])))
        nd = len(ps)
        l_specs.append(pl.BlockSpec((None,) + ps, lambda k, c_ref, nd=nd: (k,) + (0,) * nd))
        o_specs.append(pl.BlockSpec((None,) + ps, lambda k, c_ref, nd=nd: (k,) + (0,) * nd))
        blocks += 3 * _nbytes(ps, BF16)
    return list(pl.pallas_call(
        body, name=name,
        grid_spec=pltpu.PrefetchScalarGridSpec(
            num_scalar_prefetch=1, grid=(4,), in_specs=g_specs + l_specs, out_specs=o_specs),
        out_shape=[SDS((4,) + _piece_shape(a, gbs[a].shape), BF16) for a in arrs],
        compiler_params=_params(("parallel",), blocks, blocks),
    )(c_arr, *[gbs[a] for a in arrs], *lands))


def _sum_chips(a, p, land, shard_shape, jc_arr, name):
    ps = land.shape[1:]
    ax = _rows_axis(a)
    rows = ps[ax]
    nsub = 2 if rows % (2 * SUBLANES_BF16) == 0 else 1
    bs = tuple(r // nsub if q == ax else r for q, r in enumerate(ps))
    nd = len(ps)

    def at_rows(v):
        return tuple(v if q == ax else 0 for q in range(nd))

    def body(jc_ref, p_ref, l_ref, o_ref):
        del jc_ref
        acc = p_ref[...].astype(F32) + l_ref[0].astype(F32)
        acc = acc + l_ref[1].astype(F32)
        o_ref[...] = acc + l_ref[2].astype(F32)

    blocks = 4 * _nbytes(bs, BF16) + _nbytes(bs, F32)
    return pl.pallas_call(
        body, name=name,
        grid_spec=pltpu.PrefetchScalarGridSpec(
            num_scalar_prefetch=1, grid=(nsub,),
            in_specs=[pl.BlockSpec((None,) + bs, lambda s, jc: (jc[0],) + at_rows(s)),
                      pl.BlockSpec((3,) + bs, lambda s, jc: (0,) + at_rows(s))],
            out_specs=pl.BlockSpec(bs, lambda s, jc: at_rows(jc[1] * nsub + s))),
        out_shape=SDS(shard_shape, F32),
        compiler_params=_params(("parallel",), blocks, 2 * _nbytes(bs, F32)),
    )(jc_arr, p, land)


def _small_comm(v):
    rows = v.shape[0]

    def copies(cins, couts, csems):
        x, y, c = _coords()
        me = 4 * x + 2 * y + c
        out = [pltpu.make_async_copy(cins[0], couts[0].at[me], csems[0].at[0])]
        for dlt in range(1, 8):
            px = 1 - x if (dlt >> 2) & 1 else x
            py = 1 - y if (dlt >> 1) & 1 else y
            pc = 1 - c if dlt & 1 else c
            out.append(_remote(cins[0], couts[0].at[me], csems[1].at[dlt - 1], csems[2].at[dlt - 1], (px, py, pc)))
        return out

    return _symmetric([v], [SDS((8, rows, LANES), F32)], _dma_sems(1, 7, 7), copies, EVERYONE)


def _sum8(slots, name):
    def body(s_ref, o_ref):
        acc = s_ref[0]
        for i in range(1, 8):
            acc = acc + s_ref[i]
        o_ref[...] = acc

    return pl.pallas_call(
        body, name=name,
        in_specs=[pl.BlockSpec(memory_space=pltpu.VMEM)], out_specs=pl.BlockSpec(memory_space=pltpu.VMEM),
        out_shape=SDS(slots.shape[1:], F32),
    )(slots)


def _adamw(w, g, m, v, name, g_plane=None):
    rows, cols = w.shape
    tr = _tile(rows, max(SUBLANES_F32, (256 * 1024 // cols) // SUBLANES_F32 * SUBLANES_F32), SUBLANES_F32)

    def body(w_ref, g_ref, m_ref, v_ref, go_ref, d_ref, mo_ref, vo_ref):
        gr = g_ref[...]
        mn = ADAM_B1 * m_ref[...] + (1.0 - ADAM_B1) * gr
        vn = ADAM_B2 * v_ref[...] + (1.0 - ADAM_B2) * (gr * gr)
        m_hat = mn / (1.0 - ADAM_B1 ** ADAM_STEP)
        v_hat = vn / (1.0 - ADAM_B2 ** ADAM_STEP)
        d_ref[...] = -ADAM_LR * (m_hat / (jnp.sqrt(v_hat) + ADAM_EPS) + ADAM_WD * w_ref[...])
        go_ref[...] = gr
        mo_ref[...] = mn
        vo_ref[...] = vn

    spec = pl.BlockSpec((tr, cols), lambda i: (i, 0))
    g_spec = spec if g_plane is None else pl.BlockSpec((None, tr, cols), lambda i: (g_plane, i, 0))
    return pl.pallas_call(
        body, name=name, grid=(rows // tr,),
        in_specs=[spec, g_spec, spec, spec], out_specs=[spec, spec, spec, spec],
        out_shape=[SDS((rows, cols), F32)] * 4,
        compiler_params=_params(("parallel",), 8 * _nbytes((tr, cols), F32), 4 * _nbytes((tr, cols), F32)),
    )(w, g, m, v)


def _pack(parts):
    rows = []
    for p in parts:
        r = p.reshape(-1, LANES)
        pad = (-r.shape[0]) % SUBLANES_F32
        if pad:
            r = jnp.pad(r, ((0, pad), (0, 0)))
        rows.append(r)
    return jnp.concatenate(rows, axis=0)


def _unpack(packed, shapes):
    out, at = [], 0
    for s in shapes:
        n = 1
        for q in s:
            n *= q
        r = n // LANES
        out.append(packed[at:at + r].reshape(s))
        at += r + (-r) % SUBLANES_F32
    return out


def kernel(x, norm_mix, w_in, pool_w, pool_scale, w_pool_proj, conv_w, w_conv_out, w_o, norm_ffn, w_up, ffn_conv_w, ffn_conv_b, w_down, norm_final, loss_target, m_norm_mix, m_w_in, m_pool_w, m_pool_scale, m_w_pool_proj, m_conv_w, m_w_conv_out, m_w_o, m_norm_ffn, m_w_up, m_ffn_conv_w, m_ffn_conv_b, m_w_down, m_norm_final, v_norm_mix, v_w_in, v_pool_w, v_pool_scale, v_w_pool_proj, v_conv_w, v_w_conv_out, v_w_o, v_norm_ffn, v_w_up, v_ffn_conv_w, v_ffn_conv_b, v_w_down, v_norm_final):
    nseq, seq, d = x.shape
    t = nseq * seq
    f = w_down.shape[1] * 4
    c = d // N_GROUPS
    xy = lax.axis_index("x") * 2 + lax.axis_index("y")
    c_arr = lax.axis_index("c").astype(jnp.int32).reshape(1)
    jc_arr = jnp.stack([xy, lax.axis_index("c")]).astype(jnp.int32)
    nsh = 4
    zero = jnp.zeros((), jnp.int32)

    locs = [w_in[0].astype(BF16),
            jnp.stack([w_pool_proj[0], w_conv_out[0], w_o[0]]).astype(BF16),
            w_up[0].astype(BF16), w_down[0].astype(BF16), pool_w[0].astype(BF16)]
    full_shapes = [(nsh, d, N_SPLITS * d // nsh), (3, d, d), (nsh, d, 2 * f // nsh), (f, d), (N_GROUPS, c, c)]

    cw_pad = lax.dynamic_update_slice(jnp.zeros((3, d), F32), conv_w[0], (zero, xy * (d // 4)))
    fw_pad = lax.dynamic_update_slice(jnp.zeros((3, 2 * f), F32), ffn_conv_w[0], (zero, xy * (f // 2)))
    small_w = _pack([cw_pad, fw_pad]) * 0.5

    x2d = x.reshape(t, d)
    tgt = loss_target.reshape(t, d)
    ax, ay = lax.axis_index("x"), lax.axis_index("y")
    order = jnp.stack([xy, 2 * (1 - ax) + ay, 2 * ax + 1 - ay, 2 * (1 - ax) + 1 - ay]).astype(jnp.int32)
    (z, h1, w_in_f), (pool_w_f, w3_f, slots_w) = _fwd_in(
        x2d, norm_mix, locs[0], order,
        _merge([_gather_comm([4], locs, full_shapes), _gather_comm([1], locs, full_shapes, part=(0, 1, 2)),
                _small_comm(small_w)]))
    conv_w_f, ffn_cw_f = _unpack(_sum8(slots_w, "sum8_weights"), [(3, d), (3, 2 * f)])
    ffn_cw_p = ffn_cw_f.reshape(3, 2, f).transpose(1, 0, 2)
    ffn_cb_p = ffn_conv_b.reshape(2, 1, f)
    (lhs3,), (w3_f,) = _mixer_mid_fwd(z, pool_w_f, pool_scale, conv_w_f, nseq,
                                      _gather_comm([1], locs, full_shapes, part=(1, 2, 2), into={1: w3_f}))
    (lhs3, ypc, x1, h2), (w_up_f,) = _mixer_out(lhs3, z, x2d, w3_f, norm_ffn,
                                                _ring_gather_comm([2], locs, full_shapes))
    (u0,), (w_down_f,) = _ffn_up(h2, w_up_f, f, _gather_comm([3], locs, full_shapes))
    act, ua = _ffn_mid_fwd(u0, ffn_cw_p, ffn_cb_p, nseq)
    dx2, dx2b, loss11, g_norm_final = _ffn_down_loss(act, w_down_f, x1, tgt, norm_final.reshape(1, d))

    gbs, lands, ps, lands2, rs = {}, {}, {}, {}, {}
    tn_up = _tile(2 * f // nsh, 1408, LANES)
    npp = f // tn_up

    def add(arrs, name):
        for a, p in zip(arrs, _add_halves(arrs, gbs, [lands[a] for a in arrs], c_arr, name)):
            ps[a] = p

    def summed(a):
        rs[a] = _sum_chips(a, ps[a], lands2[a], _shard_shape(a, full_shapes[a]), jc_arr, "sum_chips_%d" % a)

    (gbs[3],), _ = _wgrad(act, dx2b, "wgrad_down", tr=tn_up, tn=d)
    (da,), (lands[3],) = _ffn_bwd_da(dx2b, w_down_f, _halves_comm([3], gbs))
    add([3], "add_halves_down")
    (du0, g_ffn_cw_p, g_ffn_cb_p), (lands2[3],) = _ffn_mid_bwd(da, u0, ua, ffn_cw_p, nseq, _chips_comm([3], ps))
    summed(3)
    (gbs[2],), (rs[3],) = _wgrad(h2, du0, "wgrad_up", tr=d, tn=tn_up, b_plane_of=lambda n: (n // npp, n % npp),
                                 out_shards=nsh, comm=_result_comm([3], rs))
    (dx1, rhs3, g_norm_ffn), (lands[2],) = _ffn_bwd_dx1(du0, w_up_f, x1, dx2, norm_ffn, 3, _halves_comm([2], gbs))
    add([2], "add_halves_up")
    (rhs3, dz, dpq), (lands2[2],) = _mixer_bwd(rhs3, z, ypc, w3_f, _chips_comm([2], ps, part=(0, 1, 2)))
    (gbs[1],), (lands2[2],) = _wgrad3(lhs3, rhs3, _chips_comm([2], ps, part=(1, 2, 2), into=lands2))
    summed(2)
    (dz, g_conv_w), (lands[1], rs[2]) = _conv_bwd(dz, dpq, z, conv_w_f, nseq,
                                                  _merge([_halves_comm([1], gbs), _result_comm([2], rs)]))
    add([1], "add_halves_sq3")
    (dz, g_pool_w, g_pool_scale), _ = _pool_bwd_call(dz, dpq, z, pool_w_f, pool_scale, nseq)
    gbs[4] = g_pool_w.astype(BF16)
    (gbs[0],), (lands2[1],) = _wgrad_in(h1, dz, nsh, _chips_comm([1], ps))
    summed(1)
    lands[0], lands[4] = _run_comm(_halves_comm([0, 4], gbs), "exchange_halves_in")
    add([0, 4], "add_halves_in")
    g_ffn_cw = g_ffn_cw_p.transpose(1, 0, 2).reshape(3, 2 * f)
    small_a = _pack([g_pool_scale, g_norm_ffn, g_ffn_cb_p.reshape(1, 2 * f), g_norm_final.reshape(d), g_conv_w,
                     g_ffn_cw, jnp.pad(loss11, ((0, SUBLANES_F32 - 1), (0, LANES - 1)))])
    (grad_x, g_norm_mix), (lands2[0], lands2[4], rs[1], slots_a) = _mixer_bwd_dx(
        dz, w_in_f, x2d, dx1, norm_mix,
        _merge([_chips_comm([0, 4], ps), _result_comm([1], rs), _small_comm(small_a)]))
    summed(0)
    summed(4)
    rs[0], rs[4], slots_b = _run_comm(_merge([_result_comm([0, 4], rs), _small_comm(_pack([g_norm_mix]))]),
                                      "exchange_result_in")
    shapes_a = [(1, d), (1, d), (1, 2 * f), (d,), (3, d), (3, 2 * f), (SUBLANES_F32, LANES)]
    gs_pool_scale, gs_norm_ffn, gs_ffn_cb, gs_norm_final, gs_conv_w, gs_ffn_cw, loss_blk = _unpack(
        _sum8(slots_a, "sum8_grads"), shapes_a)
    (gs_norm_mix,) = _unpack(_sum8(slots_b, "sum8_norm_mix"), [(1, d)])
    gs_conv_w = lax.dynamic_slice(gs_conv_w, (zero, xy * (d // 4)), (3, d // 4))
    gs_ffn_cw = lax.dynamic_slice(gs_ffn_cw, (zero, xy * (f // 2)), (3, f // 2))

    def upd(w, g, m, v, name, g_plane=None):
        shape = w.shape
        rows = 1
        for q in shape[:-1]:
            rows *= q
        g2 = g if g_plane is not None else g.reshape(rows, shape[-1])
        outs = _adamw(w.reshape(rows, shape[-1]), g2, m.reshape(rows, shape[-1]), v.reshape(rows, shape[-1]),
                      name, g_plane)
        return [o.reshape(shape) for o in outs]

    res = {
        "w_in": upd(w_in, rs[0], m_w_in, v_w_in, "adamw_w_in"),
        "pool_w": upd(pool_w, rs[4], m_pool_w, v_pool_w, "adamw_pool_w"),
        "w_pool_proj": upd(w_pool_proj, rs[1], m_w_pool_proj, v_w_pool_proj, "adamw_w_pool_proj", 0),
        "w_conv_out": upd(w_conv_out, rs[1], m_w_conv_out, v_w_conv_out, "adamw_w_conv_out", 1),
        "w_o": upd(w_o, rs[1], m_w_o, v_w_o, "adamw_w_o", 2),
        "w_up": upd(w_up, rs[2], m_w_up, v_w_up, "adamw_w_up"),
        "w_down": upd(w_down, rs[3], m_w_down, v_w_down, "adamw_w_down"),
    }

    small_names = ["norm_mix", "pool_scale", "norm_ffn", "ffn_conv_b", "norm_final", "conv_w", "ffn_conv_w"]
    small_ws = [norm_mix, pool_scale, norm_ffn, ffn_conv_b, norm_final, conv_w, ffn_conv_w]
    small_ms = [m_norm_mix, m_pool_scale, m_norm_ffn, m_ffn_conv_b, m_norm_final, m_conv_w, m_ffn_conv_w]
    small_vs = [v_norm_mix, v_pool_scale, v_norm_ffn, v_ffn_conv_b, v_norm_final, v_conv_w, v_ffn_conv_w]
    small_gs = [gs_norm_mix, gs_pool_scale, gs_norm_ffn, gs_ffn_cb, gs_norm_final, gs_conv_w, gs_ffn_cw]
    _, sd, sm, sv = _adamw(_pack(small_ws), _pack(small_gs), _pack(small_ms), _pack(small_vs), "adamw_small")
    shapes = [w.shape for w in small_ws]
    sd, sm, sv = _unpack(sd, shapes), _unpack(sm, shapes), _unpack(sv, shapes)
    for i, nm in enumerate(small_names):
        res[nm] = [small_gs[i].reshape(shapes[i]), sd[i], sm[i], sv[i]]

    order = ["norm_mix", "w_in", "pool_w", "pool_scale", "w_pool_proj", "conv_w", "w_conv_out", "w_o", "norm_ffn",
             "w_up", "ffn_conv_w", "ffn_conv_b", "w_down", "norm_final"]
    return (loss_blk[0, 0], grad_x.reshape(x.shape), *[res[n][0] for n in order], *[res[n][1] for n in order],
            *[res[n][2] for n in order], *[res[n][3] for n in order])
```

```python
import math

import jax
import jax.numpy as jnp
from jax import lax
from jax.experimental import pallas as pl
from jax.experimental.pallas import tpu as pltpu

F32 = jnp.float32
BF16 = jnp.bfloat16
SDS = jax.ShapeDtypeStruct
MESH = pl.DeviceIdType.MESH

RMS_EPS = 1e-6
POOL_WINDOWS = (2, 4, 8, 16)
N_GROUPS = len(POOL_WINDOWS)
N_SPLITS = 6

ADAM_LR = 0.001
ADAM_B1 = 0.9
ADAM_B2 = 0.999
ADAM_EPS = 1e-08
ADAM_WD = 0.01
ADAM_STEP = 10

LANES = 128
SUBLANES_F32 = 8
SUBLANES_BF16 = 16
VMEM_BYTES = 64 * 1024 * 1024
VMEM_CAP = VMEM_BYTES - 8 * 1024 * 1024
VMEM_FLOOR = 16 * 1024 * 1024

ANY = pl.BlockSpec(memory_space=pl.ANY)


def _tile(dim, pref, align):
    if dim <= pref:
        return dim
    t = (pref // align) * align
    while t >= align:
        if dim % t == 0:
            return t
        t -= align
    return dim


def _nbytes(shape, dtype):
    n = 1
    for s in shape:
        n *= s
    return n * jnp.dtype(dtype).itemsize


def _params(sem, block_bytes, temp_bytes=0, collective_id=None):
    need = 2 * block_bytes + temp_bytes + 4 * 1024 * 1024
    return pltpu.CompilerParams(dimension_semantics=sem, collective_id=collective_id,
                                vmem_limit_bytes=int(min(max(need, VMEM_FLOOR), VMEM_CAP)))


SIBLING = (0, 0, 1)
CHIPS = ((1, 0, 0), (0, 1, 0), (1, 1, 0))
EVERYONE = tuple((a, b, c) for a in range(2) for b in range(2) for c in range(2) if a + b + c)
PEER_SETS = (frozenset([SIBLING]), frozenset(CHIPS), frozenset(CHIPS + (SIBLING,)), frozenset(EVERYONE))
MID_AT = 0.75


def _collective_id(peers):
    return PEER_SETS.index(frozenset(peers))


def _handshake(peers):
    x, y, c = lax.axis_index("x"), lax.axis_index("y"), lax.axis_index("c")
    bar = pltpu.get_barrier_semaphore()
    for fx, fy, fc in sorted(peers):
        dev = (1 - x if fx else x, 1 - y if fy else y, 1 - c if fc else c)
        pl.semaphore_signal(bar, inc=1, device_id=dev, device_id_type=MESH)
    pl.semaphore_wait(bar, len(peers))


class _Comm:
    def __init__(self, ins, out_shapes, sems, start, finish, peers, aliases=None, mid=None):
        self.ins = list(ins)
        self.out_shapes = list(out_shapes)
        self.sems = list(sems)
        self.start = start
        self.finish = finish
        self.mid = mid
        self.peers = frozenset(peers)
        self.aliases = dict(aliases or {})


def _cost(flops, args, out_shape):
    seen = {id(a): a for a in args}
    nbytes = sum(_nbytes(a.shape, a.dtype) for a in list(seen.values()) + list(out_shape))
    return pl.CostEstimate(flops=int(flops), transcendentals=0, bytes_accessed=int(nbytes))


def _pcall(body, *, name, grid, in_specs, out_specs, out_shape, sem, blocks, temps=0, scratch_shapes=(),
           input_output_aliases=None, comm=None, flops=0):
    in_specs = list(in_specs)
    out_specs = list(out_specs)
    out_shape = list(out_shape)
    scratch_shapes = list(scratch_shapes)
    aliases = dict(input_output_aliases or {})
    n_in, n_out, n_scr = len(in_specs), len(out_shape), len(scratch_shapes)
    if comm is None:
        def run_alone(*args):
            return list(pl.pallas_call(
                body, name=name, grid=grid, in_specs=in_specs, out_specs=out_specs, out_shape=out_shape,
                scratch_shapes=scratch_shapes, input_output_aliases=aliases,
                cost_estimate=_cost(flops, args, out_shape),
                compiler_params=_params(sem, blocks, temps))(*args)), []

        return run_alone

    nci, nco = len(comm.ins), len(comm.out_shapes)
    n_steps = 1
    for g in grid:
        n_steps *= g

    def hosted(*refs):
        ins = refs[:n_in]
        cins = refs[n_in:n_in + nci]
        outs = refs[n_in + nci:n_in + nci + n_out]
        couts = refs[n_in + nci + n_out:n_in + nci + n_out + nco]
        scr = refs[n_in + nci + n_out + nco:n_in + nci + n_out + nco + n_scr]
        csems = refs[n_in + nci + n_out + nco + n_scr:]
        first = None
        last = None
        step = 0
        for q, g in enumerate(grid):
            pid = pl.program_id(q)
            first = (pid == 0) if first is None else first & (pid == 0)
            last = (pid == g - 1) if last is None else last & (pid == g - 1)
            step = step * g + pid

        @pl.when(first)
        def _():
            _handshake(comm.peers)
            comm.start(cins, couts, csems)

        if comm.mid is not None:
            @pl.when(step == int(MID_AT * n_steps))
            def _():
                comm.mid(cins, couts, csems)

        body(*ins, *outs, *scr)

        @pl.when(last)
        def _():
            comm.finish(cins, couts, csems)

    for i, o in comm.aliases.items():
        aliases[n_in + i] = n_out + o
    def run(*args):
        res = pl.pallas_call(
            hosted, name=name, grid=grid, in_specs=in_specs + [ANY] * nci, out_specs=out_specs + [ANY] * nco,
            out_shape=out_shape + comm.out_shapes, scratch_shapes=scratch_shapes + comm.sems,
            input_output_aliases=aliases,
            cost_estimate=_cost(flops, args, out_shape),
            compiler_params=_params(("arbitrary",) * len(grid), blocks, temps, _collective_id(comm.peers)),
        )(*args, *comm.ins)
        return list(res[:n_out]), list(res[n_out:])

    return run


def _run_comm(comm, name):
    def body(*refs):
        nci, nco = len(comm.ins), len(comm.out_shapes)
        cins, couts, csems = refs[:nci], refs[nci:nci + nco], refs[nci + nco:]
        _handshake(comm.peers)
        comm.start(cins, couts, csems)
        if comm.mid is not None:
            comm.mid(cins, couts, csems)
        comm.finish(cins, couts, csems)

    return list(pl.pallas_call(
        body, name=name, in_specs=[ANY] * len(comm.ins), out_specs=[ANY] * len(comm.out_shapes),
        out_shape=comm.out_shapes, scratch_shapes=comm.sems, input_output_aliases=comm.aliases,
        compiler_params=pltpu.CompilerParams(collective_id=_collective_id(comm.peers)),
    )(*comm.ins))


def _dot(a, b):
    return jnp.dot(a, b, preferred_element_type=F32)


def _dot_tb(a, b):
    return lax.dot_general(a, b, (((1,), (1,)), ((), ())), preferred_element_type=F32)


def _dot_ta(a, b):
    return lax.dot_general(a, b, (((0,), (0,)), ((), ())), preferred_element_type=F32)


def _rms_fwd(x):
    inv = lax.rsqrt(jnp.mean(x * x, axis=-1, keepdims=True) + RMS_EPS)
    return x * inv, inv


def _rms_bwd(dy, xhat, inv, g):
    gd = dy * g
    return inv * (gd - xhat * jnp.mean(gd * xhat, axis=-1, keepdims=True))


def _sigmoid(x):
    return 1.0 / (1.0 + jnp.exp(-x))


def _shift_down(x, k, row):
    return jnp.where(row >= k, pltpu.roll(x, k, 0), 0.0)


def _shift_up(x, k, row):
    s = x.shape[0]
    return jnp.where(row < s - k, pltpu.roll(x, s - k, 0), 0.0)


def _pool_fwd(u, win, row):
    s = u
    k = 1
    while k < win:
        s = s + _shift_down(s, k, row)
        k *= 2
    cnt = jnp.minimum(row + 1, win).astype(F32)
    return s / cnt - u


def _pool_bwd(dp, win, row):
    cnt = jnp.minimum(row + 1, win).astype(F32)
    s = dp / cnt
    k = 1
    while k < win:
        s = s + _shift_up(s, k, row)
        k *= 2
    return s - dp


def _acc_over(k, nk, part, acc, o_ref):
    @pl.when(k == 0)
    def _():
        acc[...] = part

    @pl.when(k > 0)
    def _():
        acc[...] += part

    @pl.when(k == nk - 1)
    def _():
        o_ref[...] = acc[...].astype(o_ref.dtype)


def _fwd_in(x, g, w_loc, order, comm):
    t, d = x.shape
    ws = w_loc.shape[1]
    nsh = order.shape[0]
    assert nsh == 4, "the shard walk below is written for the 2 x 2 chips of the mesh"
    tm = _tile(t, 1024, SUBLANES_BF16)
    ni = t // tm
    nci, nco = len(comm.ins), len(comm.out_shapes)
    all_peers = comm.peers | frozenset(CHIPS + (SIBLING,))

    def body(order_ref, x_ref, g_ref, loc_ref, *rest):
        del order_ref
        cins = rest[:nci]
        z_ref, h_ref, full_ref = rest[nci:nci + 3]
        couts = rest[nci + 3:nci + 3 + nco]
        (hs, wbuf, wsem, own_s, own_r, snd_s, snd_r, fwd_s, fwd_r, rly_s, rly_r) = rest[nci + 3 + nco:nci + 14 + nco]
        csems = rest[nci + 14 + nco:]
        j = pl.program_id(0)
        i = pl.program_id(1)
        x_, y_, c_ = _coords()
        own = 2 * x_ + y_
        sib = (x_, y_, 1 - c_)
        peers = _peer_chips(x_, y_)

        def sends():
            return [_remote(_half(0, loc_ref, c_), _piece(0, full_ref, own, c_), snd_s.at[p], snd_r.at[p], (px, py, c_))
                    for p, (px, py) in enumerate(peers[:2])]

        def relays():
            out = []
            for q, (src_p, dst_p) in enumerate(((0, 1), (1, 0))):
                sx, sy = peers[src_p]
                part = _rows_part(0, _piece(0, full_ref, 2 * sx + sy, c_), (q, q + 1, 2))
                out.append(_remote(part, part, rly_s.at[q], rly_r.at[q], (*peers[dst_p], c_)))
            return out

        def owns():
            return [_remote(_half(0, loc_ref, h), _piece(0, full_ref, own, h), own_s.at[h], own_r.at[h], sib)
                    for h in range(2)]

        def forward(p, half):
            px, py = peers[p]
            landed = _piece(0, full_ref, 2 * px + py, half)
            return _remote(landed, landed, fwd_s.at[p], fwd_r.at[p], sib)

        def load(src, slot):
            return pltpu.make_async_copy(src, wbuf.at[slot], wsem.at[slot])

        @pl.when((j == 0) & (i == 0))
        def _():
            _handshake(all_peers)
            for cp in sends() + owns():
                cp.start()
            load(loc_ref, 0).start()

        @pl.when(j == 0)
        def _():
            xh, _ = _rms_fwd(x_ref[...])
            h = (xh * g_ref[...]).astype(BF16)
            hs[pl.ds(pl.multiple_of(i * tm, tm), tm), :] = h
            h_ref[...] = h

        slot = j % 2

        @pl.when(i == 0)
        def _():
            load(loc_ref, slot).wait()

        z_ref[...] = _dot(hs[pl.ds(pl.multiple_of(i * tm, tm), tm), :], wbuf[slot]).astype(BF16)

        def load_shard(p, into):
            px, py = peers[p]
            forward(p, 1 - c_).wait_recv()
            load(full_ref.at[2 * px + py], into).start()

        @pl.when((j == 0) & (i == ni - 1))
        def _():
            for cp in sends():
                cp.wait_recv()
            for cp in relays() + [forward(0, c_), forward(1, c_)]:
                cp.start()
            load_shard(0, 1)
            comm.start(cins, couts, csems)

        @pl.when((j == 1) & (i == 0))
        def _():
            load_shard(1, 0)

        @pl.when((j == 2) & (i == max(ni - 2, 0)))
        def _():
            for cp in relays():
                cp.wait_recv()
            forward(2, c_).start()
            load_shard(2, 1)

        @pl.when((j == nsh - 1) & (i == ni - 1))
        def _():
            for cp in sends() + relays() + [forward(p, c_) for p in range(nsh - 1)]:
                cp.wait_send()
            for cp in owns():
                cp.wait()
            comm.finish(cins, couts, csems)

    last = ni - 1
    blocks = _nbytes((tm, d), F32) + _nbytes((tm, ws), BF16) + _nbytes((tm, d), BF16)
    scratch = _nbytes((t, d), BF16) + 2 * _nbytes((d, ws), BF16)
    res = pl.pallas_call(
        body, name="fwd_in",
        grid_spec=pltpu.PrefetchScalarGridSpec(
            num_scalar_prefetch=1, grid=(nsh, ni),
            in_specs=[pl.BlockSpec((tm, d), lambda j, i, o: (jnp.where(j == 0, i, last), 0)),
                      pl.BlockSpec((1, d), lambda j, i, o: (0, 0)), ANY] + [ANY] * nci,
            out_specs=[pl.BlockSpec((tm, ws), lambda j, i, o: (i, o[j])),
                       pl.BlockSpec((tm, d), lambda j, i, o: (jnp.where(j == 0, i, last), 0)), ANY] + [ANY] * nco,
            scratch_shapes=[pltpu.VMEM((t, d), BF16), pltpu.VMEM((2, d, ws), BF16)]
            + _dma_sems(2, 2, 2, 2, 2, nsh - 1, nsh - 1, 2, 2) + comm.sems),
        out_shape=[SDS((t, nsh * ws), BF16), SDS((t, d), BF16), SDS((nsh, d, ws), BF16)] + comm.out_shapes,
        input_output_aliases={4 + i: 3 + o for i, o in comm.aliases.items()},
        cost_estimate=_cost(2 * t * d * nsh * ws, (x, w_loc),
                            [SDS((t, nsh * ws), BF16), SDS((t, d), BF16), SDS((nsh, d, ws), BF16)]),
        compiler_params=_params(("arbitrary", "arbitrary"), blocks, scratch + 3 * _nbytes((tm, d), F32),
                                _collective_id(all_peers)),
    )(order, x, g, w_loc, *comm.ins)
    return list(res[:3]), list(res[3:])


def _mixer_mid_fwd(z, pool_w, pool_scale, conv_w, nseq, comm=None):
    t = z.shape[0]
    d = pool_scale.shape[1]
    s = t // nseq
    c = d // N_GROUPS

    def body(zp, zb, zc, zv, pw, ps, cw, o):
        j = pl.program_id(1)
        row = lax.broadcasted_iota(jnp.int32, (s, c), 0)
        for gi, win in enumerate(POOL_WINDOWS):
            @pl.when(j == gi)
            def _(win=win):
                pooled = _pool_fwd(zp[...].astype(F32), win, row)
                o[0] = (_dot(pooled.astype(BF16), pw[...]) * ps[...]).astype(BF16)

        cv = zc[...].astype(F32) * zv[...].astype(F32)
        cc = (cw[pl.ds(2, 1), :] * cv + cw[pl.ds(1, 1), :] * _shift_down(cv, 1, row)
              + cw[pl.ds(0, 1), :] * _shift_down(cv, 2, row))
        o[1] = (zb[...].astype(F32) * cc).astype(BF16)

    blocks = 4 * _nbytes((s, c), BF16) + _nbytes((c, c), BF16) + _nbytes((2, s, c), BF16)
    return _pcall(
        body, name="mixer_mid_fwd", grid=(nseq, N_GROUPS),
        in_specs=[pl.BlockSpec((s, c), lambda b, j: (b, j)),
                  pl.BlockSpec((s, c), lambda b, j: (b, N_GROUPS + j)),
                  pl.BlockSpec((s, c), lambda b, j: (b, 2 * N_GROUPS + j)),
                  pl.BlockSpec((s, c), lambda b, j: (b, 3 * N_GROUPS + j)),
                  pl.BlockSpec((None, c, c), lambda b, j: (j, 0, 0)),
                  pl.BlockSpec((1, c), lambda b, j: (0, j)),
                  pl.BlockSpec((3, c), lambda b, j: (0, j))],
        out_specs=[pl.BlockSpec((2, s, c), lambda b, j: (0, b, j))],
        out_shape=[SDS((3, t, d), BF16)],
        sem=("parallel", "parallel"), blocks=blocks, temps=8 * _nbytes((s, c), F32), comm=comm,
    )(z, z, z, z, pool_w, pool_scale, conv_w)


def _mixer_out(lhs3, z, x, w3, g_ffn, comm=None):
    t, d = x.shape
    tm = _tile(t, 256, SUBLANES_BF16)

    def body(pq, zgp, zgc, x_ref, w_ref, g_ref, mrg, ypc, x1o, h2o):
        yp = _dot(pq[0], w_ref[0])
        yc = _dot(pq[1], w_ref[1])
        m = _sigmoid(zgp[...].astype(F32)) * yp + _sigmoid(zgc[...].astype(F32)) * yc
        mb = m.astype(BF16)
        x1 = x_ref[...] + _dot(mb, w_ref[2])
        ypc[0] = yp.astype(BF16)
        ypc[1] = yc.astype(BF16)
        mrg[...] = mb
        x1o[...] = x1
        xh, _ = _rms_fwd(x1)
        h2o[...] = (xh * g_ref[...]).astype(BF16)

    blocks = (_nbytes((2, tm, d), BF16) * 2 + _nbytes((tm, d), BF16) * 4 + _nbytes((tm, d), F32) * 2
              + _nbytes((3, d, d), BF16))
    return _pcall(
        body, name="mixer_out", grid=(t // tm,),
        in_specs=[pl.BlockSpec((2, tm, d), lambda i: (0, i, 0)),
                  pl.BlockSpec((tm, d), lambda i: (i, 4)),
                  pl.BlockSpec((tm, d), lambda i: (i, 5)),
                  pl.BlockSpec((tm, d), lambda i: (i, 0)),
                  pl.BlockSpec((3, d, d), lambda i: (0, 0, 0)),
                  pl.BlockSpec((1, d), lambda i: (0, 0))],
        out_specs=[pl.BlockSpec((None, tm, d), lambda i: (2, i, 0)),
                   pl.BlockSpec((2, tm, d), lambda i: (0, i, 0)),
                   pl.BlockSpec((tm, d), lambda i: (i, 0)),
                   pl.BlockSpec((tm, d), lambda i: (i, 0))],
        out_shape=[SDS(lhs3.shape, BF16), SDS((2, t, d), BF16), SDS((t, d), F32), SDS((t, d), BF16)],
        input_output_aliases={0: 0},
        sem=("parallel",), blocks=blocks, temps=8 * _nbytes((tm, d), F32), comm=comm, flops=6 * t * d * d,
    )(lhs3, z, z, x, w3, g_ffn)


def _ffn_up(h2, w_up, f, comm=None):
    t, d = h2.shape
    _, _, ws = w_up.shape
    tm = _tile(t, 1024, SUBLANES_BF16)
    tn = _tile(ws, 1408, LANES)
    nps = ws // tn
    npp = f // tn

    def body(h_ref, w_ref, o_ref):
        o_ref[...] = _dot(h_ref[...], w_ref[...]).astype(BF16)

    blocks = _nbytes((tm, d), BF16) + _nbytes((d, tn), BF16) + _nbytes((tm, tn), BF16)
    return _pcall(
        body, name="ffn_up", grid=(t // tm, 2 * npp),
        in_specs=[pl.BlockSpec((tm, d), lambda i, j: (i, 0)),
                  pl.BlockSpec((None, d, tn), lambda i, j: (j // nps, 0, j % nps))],
        out_specs=[pl.BlockSpec((None, tm, tn), lambda i, j: (j // npp, i, j % npp))],
        out_shape=[SDS((2, t, f), BF16)],
        sem=("parallel", "parallel"), blocks=blocks, temps=_nbytes((tm, tn), F32), comm=comm, flops=4 * t * d * f,
    )(h2, w_up)


def _conv3_rows(u, u1, u2, w_ref, p):
    return w_ref[p, pl.ds(2, 1), :] * u + w_ref[p, pl.ds(1, 1), :] * u1 + w_ref[p, pl.ds(0, 1), :] * u2


WGRAD_TOKENS = 2048
WGRAD_TOKENS_WIDE = 4096
CHUNK = 64
HALO = SUBLANES_F32


def _up1_up2(u, nxt):
    rows = u.shape[0]
    ext = jnp.concatenate([u, nxt], axis=0)
    n = rows + HALO
    return pltpu.roll(ext, n - 1, 0)[:rows], pltpu.roll(ext, n - 2, 0)[:rows]


def _fold8(x):
    return jnp.sum(x.reshape(x.shape[0] // SUBLANES_F32, SUBLANES_F32, x.shape[1]), axis=0)


def _ffn_mid_fwd(u0, cw, cb, nseq):
    _, t, f = u0.shape
    s = t // nseq
    c = _tile(f, 256, LANES)

    def body(u_ref, w_ref, b_ref, a_ref, uo_ref):
        row = lax.broadcasted_iota(jnp.int32, (s, c), 0)
        act = []
        for p in range(2):
            u = u_ref[p].astype(F32)
            act.append(_conv3_rows(u, _shift_down(u, 1, row), _shift_down(u, 2, row), w_ref, p) + b_ref[p])
            uo_ref[p] = act[p].astype(BF16)
        ug, uv = act
        a_ref[...] = (ug * _sigmoid(ug) * uv).astype(BF16)

    blocks = 2 * _nbytes((2, s, c), BF16) + _nbytes((s, c), BF16)
    outs, _ = _pcall(
        body, name="ffn_mid_fwd", grid=(f // c, nseq),
        in_specs=[pl.BlockSpec((2, s, c), lambda j, b: (0, b, j)),
                  pl.BlockSpec((2, 3, c), lambda j, b: (0, 0, j)),
                  pl.BlockSpec((2, 1, c), lambda j, b: (0, 0, j))],
        out_specs=[pl.BlockSpec((s, c), lambda j, b: (b, j)),
                   pl.BlockSpec((2, s, c), lambda j, b: (0, b, j))],
        out_shape=[SDS((t, f), BF16), SDS((2, t, f), BF16)],
        sem=("parallel", "parallel"), blocks=blocks, temps=8 * _nbytes((s, c), F32),
    )(u0, cw, cb)
    return outs


def _ffn_down_loss(a, w_down, x1, tgt, g_fin):
    t, f = a.shape
    d = x1.shape[1]
    tm = _tile(t, 256, SUBLANES_BF16)
    nsteps = t // tm

    def body(a_ref, w_ref, x1_ref, t_ref, g_ref, dx_ref, dxb_ref, loss_ref, gg_ref, lacc):
        i = pl.program_id(0)

        @pl.when(i == 0)
        def _():
            lacc[...] = jnp.zeros_like(lacc)
            gg_ref[...] = jnp.zeros_like(gg_ref)

        x2 = x1_ref[...] + _dot(a_ref[...], w_ref[...])
        xh, inv = _rms_fwd(x2)
        g = g_ref[...]
        e = xh * g - t_ref[...]
        lacc[...] += jnp.sum(e * e, axis=0, keepdims=True)
        dy = e * (1.0 / d)
        gg_ref[...] += jnp.sum(dy * xh, axis=0, keepdims=True)
        dx2 = _rms_bwd(dy, xh, inv, g)
        dx_ref[...] = dx2
        dxb_ref[...] = dx2.astype(BF16)

        @pl.when(i == nsteps - 1)
        def _():
            loss_ref[...] = jnp.sum(lacc[...], axis=1, keepdims=True) * (0.5 / d)

    blocks = (_nbytes((tm, f), BF16) + _nbytes((f, d), BF16) + 3 * _nbytes((tm, d), F32) + _nbytes((tm, d), BF16))
    outs, _ = _pcall(
        body, name="ffn_down_loss", grid=(nsteps,),
        in_specs=[pl.BlockSpec((tm, f), lambda i: (i, 0)), pl.BlockSpec((f, d), lambda i: (0, 0)),
                  pl.BlockSpec((tm, d), lambda i: (i, 0)), pl.BlockSpec((tm, d), lambda i: (i, 0)),
                  pl.BlockSpec((1, d), lambda i: (0, 0))],
        out_specs=[pl.BlockSpec((tm, d), lambda i: (i, 0)), pl.BlockSpec((tm, d), lambda i: (i, 0)),
                   pl.BlockSpec((1, 1), lambda i: (0, 0)), pl.BlockSpec((1, d), lambda i: (0, 0))],
        out_shape=[SDS((t, d), F32), SDS((t, d), BF16), SDS((1, 1), F32), SDS((1, d), F32)],
        scratch_shapes=[pltpu.VMEM((1, d), F32)],
        sem=("arbitrary",), blocks=blocks, temps=8 * _nbytes((tm, d), F32), flops=2 * t * f * d,
    )(a, w_down, x1, tgt, g_fin)
    return outs


def _ffn_bwd_da(dxb, w_down, comm=None):
    t, d = dxb.shape
    f = w_down.shape[0]
    tm = _tile(t, 512, SUBLANES_BF16)

    def body(x_ref, w_ref, o_ref):
        o_ref[...] = _dot_tb(x_ref[...], w_ref[...]).astype(BF16)

    blocks = _nbytes((tm, d), BF16) + _nbytes((tm, f), BF16)
    return _pcall(
        body, name="ffn_bwd_da", grid=(t // tm,),
        in_specs=[pl.BlockSpec((tm, d), lambda i: (i, 0)),
                  pl.BlockSpec((f, d), lambda i: (0, 0), pipeline_mode=pl.Buffered(1))],
        out_specs=[pl.BlockSpec((tm, f), lambda i: (i, 0))],
        out_shape=[SDS((t, f), BF16)],
        sem=("parallel",), blocks=blocks, temps=_nbytes((f, d), BF16) + _nbytes((tm, f), F32), comm=comm,
        flops=2 * t * f * d,
    )(dxb, w_down)


def _ffn_mid_bwd(da, u0, ua, cw, nseq, comm=None):
    _, t, f = u0.shape
    s = t // nseq
    c = _tile(f, 128, LANES)
    r = _tile(s, CHUNK, SUBLANES_BF16)
    n = s // r

    def body(da_ref, u_ref, ua_ref, w_ref, du_ref, gw_ref, gb_ref):
        @pl.when(pl.program_id(1) == 0)
        def _():
            gw_ref[...] = jnp.zeros_like(gw_ref)
            gb_ref[...] = jnp.zeros_like(gb_ref)

        def step(i, carry):
            nxt, sums = carry
            rows = pl.ds(pl.multiple_of((n - 1 - i) * r, r), r)
            ug = ua_ref[0, rows, :].astype(F32)
            uv = ua_ref[1, rows, :].astype(F32)
            sg = _sigmoid(ug)
            dacc = da_ref[rows, :].astype(F32)
            dus = (dacc * uv * sg * (1.0 + ug * (1.0 - sg)), dacc * (ug * sg))
            first, new_sums = [], []
            for p in range(2):
                du = dus[p]
                d1, d2 = _up1_up2(du, nxt[p])
                du_ref[p, rows, :] = _conv3_rows(du, d1, d2, w_ref, p).astype(BF16)
                u = u_ref[p, rows, :].astype(F32)
                sb, s0, s1, s2 = sums[p]
                new_sums.append((sb + _fold8(du), s0 + _fold8(d2 * u), s1 + _fold8(d1 * u), s2 + _fold8(du * u)))
                first.append(du[:HALO])
            return tuple(first), tuple(new_sums)

        zero = jnp.zeros((HALO, c), F32)
        _, sums = lax.fori_loop(0, n, step, ((zero, zero), ((zero,) * 4,) * 2))
        for p in range(2):
            sb, s0, s1, s2 = sums[p]
            gb_ref[p] += jnp.sum(sb, axis=0, keepdims=True)
            gw_ref[p, pl.ds(0, 1), :] += jnp.sum(s0, axis=0, keepdims=True)
            gw_ref[p, pl.ds(1, 1), :] += jnp.sum(s1, axis=0, keepdims=True)
            gw_ref[p, pl.ds(2, 1), :] += jnp.sum(s2, axis=0, keepdims=True)

    blocks = _nbytes((s, c), BF16) + 3 * _nbytes((2, s, c), BF16)
    return _pcall(
        body, name="ffn_mid_bwd", grid=(f // c, nseq),
        in_specs=[pl.BlockSpec((s, c), lambda j, b: (b, j)),
                  pl.BlockSpec((2, s, c), lambda j, b: (0, b, j)),
                  pl.BlockSpec((2, s, c), lambda j, b: (0, b, j)),
                  pl.BlockSpec((2, 3, c), lambda j, b: (0, 0, j))],
        out_specs=[pl.BlockSpec((2, s, c), lambda j, b: (0, b, j)),
                   pl.BlockSpec((2, 3, c), lambda j, b: (0, 0, j)),
                   pl.BlockSpec((2, 1, c), lambda j, b: (0, 0, j))],
        out_shape=[SDS((2, t, f), BF16), SDS((2, 3, f), F32), SDS((2, 1, f), F32)],
        sem=("parallel", "arbitrary"), blocks=blocks, temps=4 * 1024 * 1024, comm=comm,
    )(da, u0, ua, cw)


def _wgrad(a, b, name, *, tr, tn, b_plane_of=None, out_shards=None, comm=None):
    t, m = a.shape
    n_total = b.shape[-1] * (b.shape[0] if b.ndim == 3 else 1)
    tk = _tile(t, WGRAD_TOKENS_WIDE if n_total > tn and m == tr else WGRAD_TOKENS, SUBLANES_BF16)
    nk = t // tk
    once = pl.Buffered(1) if nk == 1 else None

    def body(a_ref, b_ref, o_ref, *acc):
        part = _dot_ta(a_ref[...], b_ref[...])
        if nk == 1:
            o_ref[...] = part.astype(BF16)
        else:
            _acc_over(pl.program_id(2), nk, part, acc[0], o_ref)

    if b.ndim == 3:
        b_spec = pl.BlockSpec((None, tk, tn), lambda r, n, k: (b_plane_of(n)[0], k, b_plane_of(n)[1]))
    else:
        b_spec = pl.BlockSpec((tk, tn), lambda r, n, k: (k, n), pipeline_mode=once if n_total == tn else None)
    if out_shards is None:
        o_spec = pl.BlockSpec((tr, tn), lambda r, n, k: (r, n))
        o_shape = SDS((m, n_total), BF16)
    else:
        nps = n_total // out_shards // tn
        o_spec = pl.BlockSpec((None, tr, tn), lambda r, n, k: (n // nps, r, n % nps))
        o_shape = SDS((out_shards, m, n_total // out_shards), BF16)
    blocks = _nbytes((tk, tr), BF16) + _nbytes((tk, tn), BF16) + _nbytes((tr, tn), BF16)
    return _pcall(
        body, name=name, grid=(m // tr, n_total // tn, nk),
        in_specs=[pl.BlockSpec((tk, tr), lambda r, n, k: (k, r), pipeline_mode=once if m == tr else None), b_spec],
        out_specs=[o_spec], out_shape=[o_shape],
        scratch_shapes=[] if nk == 1 else [pltpu.VMEM((tr, tn), F32)],
        sem=("parallel", "parallel", "arbitrary"), blocks=blocks, temps=2 * _nbytes((tr, tn), F32), comm=comm,
        flops=2 * t * m * n_total,
    )(a, b)


def _wgrad3(lhs3, rhs3, comm=None):
    nw, t, d = lhs3.shape
    tk = _tile(t, WGRAD_TOKENS, SUBLANES_BF16)
    nk = t // tk

    def body(a_ref, b_ref, o_ref, *acc):
        part = _dot_ta(a_ref[...], b_ref[...])
        if nk == 1:
            o_ref[...] = part.astype(BF16)
        else:
            _acc_over(pl.program_id(1), nk, part, acc[0], o_ref)

    blocks = 2 * _nbytes((tk, d), BF16) + _nbytes((d, d), BF16)
    return _pcall(
        body, name="wgrad_sq3", grid=(nw, nk),
        in_specs=[pl.BlockSpec((None, tk, d), lambda w, k: (w, k, 0)),
                  pl.BlockSpec((None, tk, d), lambda w, k: (w, k, 0))],
        out_specs=[pl.BlockSpec((None, d, d), lambda w, k: (w, 0, 0))],
        out_shape=[SDS((nw, d, d), BF16)],
        scratch_shapes=[] if nk == 1 else [pltpu.VMEM((d, d), F32)],
        sem=("parallel", "arbitrary"), blocks=blocks, temps=2 * _nbytes((d, d), F32), comm=comm,
        flops=2 * nw * t * d * d,
    )(lhs3, rhs3)


def _ffn_bwd_dx1(du0, w_up, x1, dx2, g_ffn, n_planes_out, comm=None):
    _, t, f = du0.shape
    d = x1.shape[1]
    nsh, _, ws = w_up.shape
    tm = _tile(t, 256, SUBLANES_BF16)
    spp = f // ws

    def body(du_ref, w_ref, x1_ref, dx2_ref, g_ref, dx1_ref, dxb_ref, gg_ref):
        @pl.when(pl.program_id(0) == 0)
        def _():
            gg_ref[...] = jnp.zeros_like(gg_ref)

        dh = None
        for k in range(nsh):
            part = _dot_tb(du_ref[k // spp, :, (k % spp) * ws:(k % spp + 1) * ws], w_ref[k])
            dh = part if dh is None else dh + part
        xh, inv = _rms_fwd(x1_ref[...])
        gg_ref[...] += jnp.sum(dh * xh, axis=0, keepdims=True)
        dx1 = dx2_ref[...] + _rms_bwd(dh, xh, inv, g_ref[...])
        dx1_ref[...] = dx1
        dxb_ref[...] = dx1.astype(BF16)

    blocks = _nbytes((2, tm, f), BF16) + 3 * _nbytes((tm, d), F32) + _nbytes((tm, d), BF16)
    return _pcall(
        body, name="ffn_bwd_dx1", grid=(t // tm,),
        in_specs=[pl.BlockSpec((2, tm, f), lambda i: (0, i, 0)),
                  pl.BlockSpec((nsh, d, ws), lambda i: (0, 0, 0), pipeline_mode=pl.Buffered(1)),
                  pl.BlockSpec((tm, d), lambda i: (i, 0)),
                  pl.BlockSpec((tm, d), lambda i: (i, 0)),
                  pl.BlockSpec((1, d), lambda i: (0, 0))],
        out_specs=[pl.BlockSpec((tm, d), lambda i: (i, 0)),
                   pl.BlockSpec((None, tm, d), lambda i: (n_planes_out - 1, i, 0)),
                   pl.BlockSpec((1, d), lambda i: (0, 0))],
        out_shape=[SDS((t, d), F32), SDS((n_planes_out, t, d), BF16), SDS((1, d), F32)],
        sem=("arbitrary",), blocks=blocks, temps=_nbytes(w_up.shape, BF16) + 8 * _nbytes((tm, d), F32), comm=comm,
        flops=4 * t * f * d,
    )(du0, w_up, x1, dx2, g_ffn)


def _mixer_bwd(rhs3, z, ypc, w3, comm=None):
    _, t, d = rhs3.shape
    tm = _tile(t, 512, SUBLANES_BF16)

    def body(dx_ref, zgp, zgc, ypc_ref, w_ref, dyo, dzo, dpq):
        dm = _dot_tb(dx_ref[...], w_ref[2])
        sp = _sigmoid(zgp[...].astype(F32))
        sc = _sigmoid(zgc[...].astype(F32))
        dyp = (dm * sp).astype(BF16)
        dyc = (dm * sc).astype(BF16)
        dzo[0] = (dm * ypc_ref[0].astype(F32) * sp * (1.0 - sp)).astype(BF16)
        dzo[1] = (dm * ypc_ref[1].astype(F32) * sc * (1.0 - sc)).astype(BF16)
        dyo[0] = dyp
        dyo[1] = dyc
        dpq[0] = _dot_tb(dyp, w_ref[0]).astype(BF16)
        dpq[1] = _dot_tb(dyc, w_ref[1]).astype(BF16)

    blocks = _nbytes((tm, d), BF16) * 3 + _nbytes((2, tm, d), BF16) * 4 + _nbytes((3, d, d), BF16)
    return _pcall(
        body, name="mixer_bwd", grid=(t // tm,),
        in_specs=[pl.BlockSpec((None, tm, d), lambda i: (2, i, 0)),
                  pl.BlockSpec((tm, d), lambda i: (i, 4)),
                  pl.BlockSpec((tm, d), lambda i: (i, 5)),
                  pl.BlockSpec((2, tm, d), lambda i: (0, i, 0)),
                  pl.BlockSpec((3, d, d), lambda i: (0, 0, 0))],
        out_specs=[pl.BlockSpec((2, tm, d), lambda i: (0, i, 0)),
                   pl.BlockSpec((2, tm, d), lambda i: (2, i, 0)),
                   pl.BlockSpec((2, tm, d), lambda i: (0, i, 0))],
        out_shape=[SDS(rhs3.shape, BF16), SDS((N_SPLITS, t, d), BF16), SDS((2, t, d), BF16)],
        input_output_aliases={0: 0},
        sem=("parallel",), blocks=blocks, temps=8 * _nbytes((tm, d), F32), comm=comm, flops=6 * t * d * d,
    )(rhs3, z, z, ypc, w3)


def _conv_bwd(dz, dpq, z, conv_w, nseq, comm=None):
    _, t, d = dz.shape
    s = t // nseq
    c = _tile(d, 128, LANES)
    nb = d // c

    def body(dz_in, dq_ref, zb, zc, zv, cw, dzo, gw_ref):
        del dz_in

        @pl.when(pl.program_id(1) == 0)
        def _():
            gw_ref[...] = jnp.zeros_like(gw_ref)

        row = lax.broadcasted_iota(jnp.int32, (s, c), 0)
        b = zb[...].astype(F32)
        cm = zc[...].astype(F32)
        v = zv[...].astype(F32)
        cv = cm * v
        cv1 = _shift_down(cv, 1, row)
        cv2 = _shift_down(cv, 2, row)
        w0, w1, w2 = cw[pl.ds(0, 1), :], cw[pl.ds(1, 1), :], cw[pl.ds(2, 1), :]
        cc = w2 * cv + w1 * cv1 + w0 * cv2
        dq = dq_ref[...].astype(F32)
        dzo[0] = (dq * cc).astype(BF16)
        dcc = dq * b
        gw_ref[pl.ds(0, 1), :] += jnp.sum(dcc * cv2, axis=0, keepdims=True)
        gw_ref[pl.ds(1, 1), :] += jnp.sum(dcc * cv1, axis=0, keepdims=True)
        gw_ref[pl.ds(2, 1), :] += jnp.sum(dcc * cv, axis=0, keepdims=True)
        dcv = w2 * dcc + w1 * _shift_up(dcc, 1, row) + w0 * _shift_up(dcc, 2, row)
        dzo[1] = (dcv * v).astype(BF16)
        dzo[2] = (dcv * cm).astype(BF16)

    blocks = 4 * _nbytes((s, c), BF16) + _nbytes((3, s, c), BF16)
    return _pcall(
        body, name="conv_bwd", grid=(nb, nseq),
        in_specs=[ANY,
                  pl.BlockSpec((None, s, c), lambda j, b: (1, b, j)),
                  pl.BlockSpec((s, c), lambda j, b: (b, nb + j)),
                  pl.BlockSpec((s, c), lambda j, b: (b, 2 * nb + j)),
                  pl.BlockSpec((s, c), lambda j, b: (b, 3 * nb + j)),
                  pl.BlockSpec((3, c), lambda j, b: (0, j))],
        out_specs=[pl.BlockSpec((3, s, c), lambda j, b: (0, b, j)),
                   pl.BlockSpec((3, c), lambda j, b: (0, j))],
        out_shape=[SDS(dz.shape, BF16), SDS((3, d), F32)],
        input_output_aliases={0: 0},
        sem=("parallel", "arbitrary"), blocks=blocks, temps=16 * _nbytes((s, c), F32), comm=comm,
    )(dz, dpq, z, z, z, conv_w)


def _pool_bwd_call(dz, dpq, z, pool_w, pool_scale, nseq, comm=None):
    _, t, d = dz.shape
    s = t // nseq
    c = d // N_GROUPS

    def body(dz_in, dp_ref, zp, pw, ps, dzo, gpw_ref, gps_ref):
        del dz_in
        j = pl.program_id(0)

        @pl.when(pl.program_id(1) == 0)
        def _():
            gpw_ref[...] = jnp.zeros_like(gpw_ref)
            gps_ref[...] = jnp.zeros_like(gps_ref)

        row = lax.broadcasted_iota(jnp.int32, (s, c), 0)
        for gi, win in enumerate(POOL_WINDOWS):
            @pl.when(j == gi)
            def _(win=win):
                pb = _pool_fwd(zp[...].astype(F32), win, row).astype(BF16)
                plin = _dot(pb, pw[...])
                dps = dp_ref[...].astype(F32)
                gps_ref[...] += jnp.sum(dps * plin, axis=0, keepdims=True)
                dplb = (dps * ps[...]).astype(BF16)
                gpw_ref[...] += _dot_ta(pb, dplb)
                dzo[...] = _pool_bwd(_dot_tb(dplb, pw[...]), win, row).astype(BF16)

    blocks = 3 * _nbytes((s, c), BF16) + _nbytes((c, c), BF16) + _nbytes((c, c), F32)
    return _pcall(
        body, name="pool_bwd", grid=(N_GROUPS, nseq),
        in_specs=[ANY,
                  pl.BlockSpec((None, s, c), lambda j, b: (0, b, j)),
                  pl.BlockSpec((s, c), lambda j, b: (b, j)),
                  pl.BlockSpec((None, c, c), lambda j, b: (j, 0, 0)),
                  pl.BlockSpec((1, c), lambda j, b: (0, j))],
        out_specs=[pl.BlockSpec((None, s, c), lambda j, b: (3, b, j)),
                   pl.BlockSpec((None, c, c), lambda j, b: (j, 0, 0)),
                   pl.BlockSpec((1, c), lambda j, b: (0, j))],
        out_shape=[SDS(dz.shape, BF16), SDS((N_GROUPS, c, c), F32), SDS((1, d), F32)],
        input_output_aliases={0: 0},
        sem=("parallel", "arbitrary"), blocks=blocks, temps=10 * _nbytes((s, c), F32), comm=comm,
    )(dz, dpq, z, pool_w, pool_scale)


def _dz_plane(zb):
    return jnp.where(zb < 4, (zb + 3) % 4, zb)


def _wgrad_in(h1, dz, nsh, comm=None):
    t, d = h1.shape
    ws = N_SPLITS * d // nsh
    kb = _tile(math.gcd(d, ws), 512, LANES)
    npl = d // kb
    nps = ws // kb
    tk = _tile(t, WGRAD_TOKENS_WIDE, SUBLANES_BF16)
    nk = t // tk

    def body(a_ref, b_ref, o_ref, *acc):
        part = _dot_ta(a_ref[...], b_ref[...])
        if nk == 1:
            o_ref[...] = part.astype(BF16)
        else:
            _acc_over(pl.program_id(1), nk, part, acc[0], o_ref)

    blocks = _nbytes((tk, d), BF16) + _nbytes((tk, kb), BF16) + _nbytes((d, kb), BF16)
    return _pcall(
        body, name="wgrad_in", grid=(N_SPLITS * npl, nk),
        in_specs=[pl.BlockSpec((tk, d), lambda cb, k: (k, 0), pipeline_mode=pl.Buffered(1) if nk == 1 else None),
                  pl.BlockSpec((None, tk, kb), lambda cb, k: (_dz_plane(cb // npl), k, cb % npl))],
        out_specs=[pl.BlockSpec((None, d, kb), lambda cb, k: (cb // nps, 0, cb % nps))],
        out_shape=[SDS((nsh, d, ws), BF16)],
        scratch_shapes=[] if nk == 1 else [pltpu.VMEM((d, kb), F32)],
        sem=("parallel", "arbitrary"), blocks=blocks, temps=2 * _nbytes((d, kb), F32), comm=comm,
        flops=2 * t * d * nsh * ws,
    )(h1, dz)


def _mixer_bwd_dx(dz, w_in, x, dx1, g_mix, comm=None):
    npln, t, d = dz.shape
    nsh, _, ws = w_in.shape
    tm = _tile(t, 256, SUBLANES_BF16)
    kb = _tile(math.gcd(d, ws), 512, LANES)
    npl = d // kb
    nps = ws // kb

    def body(dz_ref, w_ref, x_ref, dx1_ref, g_ref, dx_ref, gg_ref):
        @pl.when(pl.program_id(0) == 0)
        def _():
            gg_ref[...] = jnp.zeros_like(gg_ref)

        dh = None
        for cb in range(npln * npl):
            zb = cb // npl
            plane = (zb + 3) % 4 if zb < 4 else zb
            part = _dot_tb(dz_ref[plane, :, (cb % npl) * kb:(cb % npl + 1) * kb],
                           w_ref[cb // nps, :, (cb % nps) * kb:(cb % nps + 1) * kb])
            dh = part if dh is None else dh + part
        xh, inv = _rms_fwd(x_ref[...])
        gg_ref[...] += jnp.sum(dh * xh, axis=0, keepdims=True)
        dx_ref[...] = dx1_ref[...] + _rms_bwd(dh, xh, inv, g_ref[...])

    blocks = _nbytes((npln, tm, d), BF16) + 3 * _nbytes((tm, d), F32)
    return _pcall(
        body, name="mixer_bwd_dx", grid=(t // tm,),
        in_specs=[pl.BlockSpec((npln, tm, d), lambda i: (0, i, 0)),
                  pl.BlockSpec((nsh, d, ws), lambda i: (0, 0, 0), pipeline_mode=pl.Buffered(1)),
                  pl.BlockSpec((tm, d), lambda i: (i, 0)),
                  pl.BlockSpec((tm, d), lambda i: (i, 0)),
                  pl.BlockSpec((1, d), lambda i: (0, 0))],
        out_specs=[pl.BlockSpec((tm, d), lambda i: (i, 0)),
                   pl.BlockSpec((1, d), lambda i: (0, 0))],
        out_shape=[SDS((t, d), F32), SDS((1, d), F32)],
        sem=("arbitrary",), blocks=blocks, temps=_nbytes(w_in.shape, BF16) + 8 * _nbytes((tm, d), F32), comm=comm,
        flops=2 * t * d * nsh * ws,
    )(dz, w_in, x, dx1, g_mix)


N_BIG = 5
SHARD_MAJOR = (0, 2)
ROWS_DIM1 = (1, 4)


def _ds(start, size, align):
    if isinstance(start, int):
        return pl.ds(start, size)
    return pl.ds(pl.multiple_of(start, align), size)


def _piece(a, ref, k, h):
    if a in SHARD_MAJOR:
        r = ref.shape[1] // 2
        return ref.at[k, _ds(h * r, r, SUBLANES_BF16), :]
    if a in ROWS_DIM1:
        r = ref.shape[1] // 8
        return ref.at[:, _ds((2 * k + h) * r, r, SUBLANES_BF16), :]
    r = ref.shape[0] // 8
    return ref.at[_ds((2 * k + h) * r, r, SUBLANES_BF16), :]


def _half(a, ref, h):
    if a in ROWS_DIM1:
        r = ref.shape[1] // 2
        return ref.at[:, _ds(h * r, r, SUBLANES_BF16), :]
    r = ref.shape[0] // 2
    return ref.at[_ds(h * r, r, SUBLANES_BF16), :]


def _piece_shape(a, full_shape):
    if a in SHARD_MAJOR:
        return (full_shape[1] // 2, full_shape[2])
    if a in ROWS_DIM1:
        return (full_shape[0], full_shape[1] // 8, full_shape[2])
    return (full_shape[0] // 8, full_shape[1])


def _shard_shape(a, full_shape):
    if a in SHARD_MAJOR:
        return (full_shape[1], full_shape[2])
    if a in ROWS_DIM1:
        return (full_shape[0], full_shape[1] // 4, full_shape[2])
    return (full_shape[0] // 4, full_shape[1])


def _rows_axis(a):
    return 1 if a in ROWS_DIM1 else 0


def _piece_block(a, full_shape):
    ps = _piece_shape(a, full_shape)
    if a in SHARD_MAJOR:
        return (None,) + ps, lambda k, c: (k, c, 0)
    if a in ROWS_DIM1:
        return ps, lambda k, c: (0, 2 * k + c, 0)
    return ps, lambda k, c: (2 * k + c, 0)


def _coords():
    return lax.axis_index("x"), lax.axis_index("y"), lax.axis_index("c")


def _peer_chips(x, y):
    return [(1 - x, y), (x, 1 - y), (1 - x, 1 - y)]


def _remote(src, dst, ssem, rsem, dev):
    return pltpu.make_async_remote_copy(src_ref=src, dst_ref=dst, send_sem=ssem, recv_sem=rsem,
                                        device_id=dev, device_id_type=MESH)


def _dma_sems(*counts):
    return [pltpu.SemaphoreType.DMA((n,)) for n in counts]


def _symmetric(ins, out_shapes, sems, copies, peers, aliases=None):
    def start(cins, couts, csems):
        for cp in copies(cins, couts, csems):
            cp.start()

    def finish(cins, couts, csems):
        for cp in copies(cins, couts, csems):
            cp.wait()

    return _Comm(ins, out_shapes, sems, start, finish, peers, aliases)


def _rows_part(a, ref, part):
    if part is None:
        return ref
    p, q, n = part
    ax = _rows_axis(a)
    r = ref.shape[ax] // n
    return ref.at[tuple(pl.ds(p * r, (q - p) * r) if d == ax else slice(None) for d in range(len(ref.shape)))]


def _merge(comms):
    ins, outs, sems, aliases, spans = [], [], [], {}, []
    for cm in comms:
        spans.append((len(ins), len(outs), len(sems)))
        for i, o in cm.aliases.items():
            aliases[len(ins) + i] = len(outs) + o
        ins += cm.ins
        outs += cm.out_shapes
        sems += cm.sems

    def each(fn_name):
        def run(cins, couts, csems):
            for cm, (i0, o0, s0) in zip(comms, spans):
                fn = getattr(cm, fn_name)
                if fn is not None:
                    fn(cins[i0:i0 + len(cm.ins)], couts[o0:o0 + len(cm.out_shapes)], csems[s0:s0 + len(cm.sems)])
        return run

    return _Comm(ins, outs, sems, each("start"), each("finish"), frozenset().union(*[cm.peers for cm in comms]),
                 aliases, mid=each("mid") if any(cm.mid is not None for cm in comms) else None)


def _gather_comm(arrs, locs, full_shapes, part=None, into=None):
    n = len(arrs)

    def own(cins, couts, csems):
        x, y, c = _coords()
        j = 2 * x + y
        return [_remote(_rows_part(a, _half(a, cins[q], h), part), _rows_part(a, _piece(a, couts[q], j, h), part),
                        csems[0].at[2 * q + h], csems[1].at[2 * q + h], (x, y, 1 - c))
                for q, a in enumerate(arrs) for h in range(2)]

    def sends(cins, couts, csems):
        x, y, c = _coords()
        j = 2 * x + y
        return [_remote(_rows_part(a, _half(a, cins[q], c), part), _rows_part(a, _piece(a, couts[q], j, c), part),
                        csems[2].at[3 * q + i], csems[3].at[3 * q + i], (px, py, c))
                for q, a in enumerate(arrs) for i, (px, py) in enumerate(_peer_chips(x, y))]

    def forwards(couts, csems, half_of):
        x, y, c = _coords()
        out = []
        for q, a in enumerate(arrs):
            for i, (px, py) in enumerate(_peer_chips(x, y)):
                landed = _rows_part(a, _piece(a, couts[q], 2 * px + py, half_of(c)), part)
                out.append(_remote(landed, landed, csems[4].at[3 * q + i], csems[5].at[3 * q + i], (x, y, 1 - c)))
        return out

    def start(cins, couts, csems):
        for cp in sends(cins, couts, csems) + own(cins, couts, csems):
            cp.start()

    def finish(cins, couts, csems):
        fw = forwards(couts, csems, lambda c: c)
        for cp, f in zip(sends(cins, couts, csems), fw):
            cp.wait_recv()
            f.start()
        for f in forwards(couts, csems, lambda c: 1 - c):
            f.wait_recv()
        for cp in sends(cins, couts, csems) + fw:
            cp.wait_send()
        for cp in own(cins, couts, csems):
            cp.wait()

    ins = [locs[a] for a in arrs] + ([into[a] for a in arrs] if into else [])
    return _Comm(ins, [SDS(full_shapes[a], BF16) for a in arrs],
                 _dma_sems(2 * n, 2 * n, 3 * n, 3 * n, 3 * n, 3 * n), start, finish, CHIPS + (SIBLING,),
                 aliases={n + q: q for q in range(n)} if into else None)


def _ring_gather_comm(arrs, locs, full_shapes):
    n = len(arrs)

    def own(cins, couts, csems):
        x, y, c = _coords()
        j = 2 * x + y
        return [_remote(_half(a, cins[q], h), _piece(a, couts[q], j, h), csems[0].at[2 * q + h],
                        csems[1].at[2 * q + h], (x, y, 1 - c)) for q, a in enumerate(arrs) for h in range(2)]

    def sends(cins, couts, csems):
        x, y, c = _coords()
        j = 2 * x + y
        return [_remote(_half(a, cins[q], c), _piece(a, couts[q], j, c), csems[2].at[2 * q + i],
                        csems[3].at[2 * q + i], (px, py, c))
                for q, a in enumerate(arrs) for i, (px, py) in enumerate(_peer_chips(x, y)[:2])]

    def relays(couts, csems):
        x, y, c = _coords()
        peers = _peer_chips(x, y)
        out = []
        for q, a in enumerate(arrs):
            for r, (src_p, dst_p) in enumerate(((0, 1), (1, 0))):
                sx, sy = peers[src_p]
                rows = _rows_part(a, _piece(a, couts[q], 2 * sx + sy, c), (r, r + 1, 2))
                out.append(_remote(rows, rows, csems[6].at[2 * q + r], csems[7].at[2 * q + r], (*peers[dst_p], c)))
        return out

    def forwards(couts, csems, half_of, which):
        x, y, c = _coords()
        out = []
        for q, a in enumerate(arrs):
            for i in which:
                px, py = _peer_chips(x, y)[i]
                landed = _piece(a, couts[q], 2 * px + py, half_of(c))
                out.append(_remote(landed, landed, csems[4].at[3 * q + i], csems[5].at[3 * q + i], (x, y, 1 - c)))
        return out

    def start(cins, couts, csems):
        for cp in sends(cins, couts, csems) + own(cins, couts, csems):
            cp.start()

    def mid(cins, couts, csems):
        for cp in sends(cins, couts, csems):
            cp.wait_recv()
        for cp in relays(couts, csems) + forwards(couts, csems, lambda c: c, (0, 1)):
            cp.start()

    def finish(cins, couts, csems):
        for cp in relays(couts, csems):
            cp.wait_recv()
        fw_diag = forwards(couts, csems, lambda c: c, (2,))
        for f in fw_diag:
            f.start()
        for f in forwards(couts, csems, lambda c: 1 - c, (0, 1, 2)):
            f.wait_recv()
        for cp in (sends(cins, couts, csems) + relays(couts, csems)
                   + forwards(couts, csems, lambda c: c, (0, 1)) + fw_diag):
            cp.wait_send()
        for cp in own(cins, couts, csems):
            cp.wait()

    return _Comm([locs[a] for a in arrs], [SDS(full_shapes[a], BF16) for a in arrs],
                 _dma_sems(2 * n, 2 * n, 2 * n, 2 * n, 3 * n, 3 * n, 2 * n, 2 * n), start, finish,
                 CHIPS + (SIBLING,), mid=mid)


def _halves_comm(arrs, gbs):
    n = len(arrs)

    def copies(cins, couts, csems):
        x, y, c = _coords()
        return [_remote(_piece(a, cins[q], k, 1 - c), couts[q].at[k], csems[0].at[4 * q + k], csems[1].at[4 * q + k],
                        (x, y, 1 - c)) for q, a in enumerate(arrs) for k in range(4)]

    return _symmetric([gbs[a] for a in arrs], [SDS((4,) + _piece_shape(a, gbs[a].shape), BF16) for a in arrs],
                      _dma_sems(4 * n, 4 * n), copies, [SIBLING])


def _chips_comm(arrs, ps, part=None, into=None):
    n = len(arrs)

    def copies(cins, couts, csems):
        x, y, c = _coords()
        return [_remote(_rows_part(a, cins[q].at[2 * px + py], part), _rows_part(a, couts[q].at[i], part),
                        csems[0].at[3 * q + i], csems[1].at[3 * q + i], (px, py, c))
                for q, a in enumerate(arrs) for i, (px, py) in enumerate(_peer_chips(x, y))]

    ins = [ps[a] for a in arrs] + ([into[a] for a in arrs] if into else [])
    return _symmetric(ins, [SDS((3,) + ps[a].shape[1:], BF16) for a in arrs], _dma_sems(3 * n, 3 * n), copies, CHIPS,
                      aliases={n + q: q for q in range(n)} if into else None)


def _result_comm(arrs, gs):
    n = len(arrs)

    def copies(cins, couts, csems):
        x, y, c = _coords()
        return [_remote(_half(a, cins[q], c), _half(a, couts[q], c), csems[0].at[q], csems[1].at[q], (x, y, 1 - c))
                for q, a in enumerate(arrs)]

    return _symmetric([gs[a] for a in arrs], [SDS(gs[a].shape, F32) for a in arrs], _dma_sems(n, n), copies,
                      [SIBLING], aliases={q: q for q in range(n)})


def _add_halves(arrs, gbs, lands, c_arr, name):
    n = len(arrs)

    def body(c_ref, *refs):
        del c_ref
        for q in range(n):
            refs[2 * n + q][...] = (refs[q][...].astype(F32) + refs[n + q][...].astype(F32)).astype(BF16)

    g_specs, l_specs, o_specs, blocks = [], [], [], 0
    for a in arrs:
        bs, imap = _piece_block(a, gbs[a].shape)
        ps = _piece_shape(a, gbs[a].shape)
        g_specs.append(pl.BlockSpec(bs, lambda k, c_ref, imap=imap: imap(k, c_ref[0])))
        nd = len(ps)
        l_specs.append(pl.BlockSpec((None,) + ps, lambda k, c_ref, nd=nd: (k,) + (0,) * nd))
        o_specs.append(pl.BlockSpec((None,) + ps, lambda k, c_ref, nd=nd: (k,) + (0,) * nd))
        blocks += 3 * _nbytes(ps, BF16)
    return list(pl.pallas_call(
        body, name=name,
        grid_spec=pltpu.PrefetchScalarGridSpec(
            num_scalar_prefetch=1, grid=(4,), in_specs=g_specs + l_specs, out_specs=o_specs),
        out_shape=[SDS((4,) + _piece_shape(a, gbs[a].shape), BF16) for a in arrs],
        compiler_params=_params(("parallel",), blocks, blocks),
    )(c_arr, *[gbs[a] for a in arrs], *lands))


def _sum_chips(a, p, land, shard_shape, jc_arr, name):
    ps = land.shape[1:]
    ax = _rows_axis(a)
    rows = ps[ax]
    nsub = 2 if rows % (2 * SUBLANES_BF16) == 0 else 1
    bs = tuple(r // nsub if q == ax else r for q, r in enumerate(ps))
    nd = len(ps)

    def at_rows(v):
        return tuple(v if q == ax else 0 for q in range(nd))

    def body(jc_ref, p_ref, l_ref, o_ref):
        del jc_ref
        acc = p_ref[...].astype(F32) + l_ref[0].astype(F32)
        acc = acc + l_ref[1].astype(F32)
        o_ref[...] = acc + l_ref[2].astype(F32)

    blocks = 4 * _nbytes(bs, BF16) + _nbytes(bs, F32)
    return pl.pallas_call(
        body, name=name,
        grid_spec=pltpu.PrefetchScalarGridSpec(
            num_scalar_prefetch=1, grid=(nsub,),
            in_specs=[pl.BlockSpec((None,) + bs, lambda s, jc: (jc[0],) + at_rows(s)),
                      pl.BlockSpec((3,) + bs, lambda s, jc: (0,) + at_rows(s))],
            out_specs=pl.BlockSpec(bs, lambda s, jc: at_rows(jc[1] * nsub + s))),
        out_shape=SDS(shard_shape, F32),
        compiler_params=_params(("parallel",), blocks, 2 * _nbytes(bs, F32)),
    )(jc_arr, p, land)


def _small_comm(v):
    rows = v.shape[0]

    def copies(cins, couts, csems):
        x, y, c = _coords()
        me = 4 * x + 2 * y + c
        out = [pltpu.make_async_copy(cins[0], couts[0].at[me], csems[0].at[0])]
        for dlt in range(1, 8):
            px = 1 - x if (dlt >> 2) & 1 else x
            py = 1 - y if (dlt >> 1) & 1 else y
            pc = 1 - c if dlt & 1 else c
            out.append(_remote(cins[0], couts[0].at[me], csems[1].at[dlt - 1], csems[2].at[dlt - 1], (px, py, pc)))
        return out

    return _symmetric([v], [SDS((8, rows, LANES), F32)], _dma_sems(1, 7, 7), copies, EVERYONE)


def _sum8(slots, name):
    def body(s_ref, o_ref):
        acc = s_ref[0]
        for i in range(1, 8):
            acc = acc + s_ref[i]
        o_ref[...] = acc

    return pl.pallas_call(
        body, name=name,
        in_specs=[pl.BlockSpec(memory_space=pltpu.VMEM)], out_specs=pl.BlockSpec(memory_space=pltpu.VMEM),
        out_shape=SDS(slots.shape[1:], F32),
    )(slots)


def _adamw(w, g, m, v, name, g_plane=None):
    rows, cols = w.shape
    tr = _tile(rows, max(SUBLANES_F32, (256 * 1024 // cols) // SUBLANES_F32 * SUBLANES_F32), SUBLANES_F32)

    def body(w_ref, g_ref, m_ref, v_ref, go_ref, d_ref, mo_ref, vo_ref):
        gr = g_ref[...]
        mn = ADAM_B1 * m_ref[...] + (1.0 - ADAM_B1) * gr
        vn = ADAM_B2 * v_ref[...] + (1.0 - ADAM_B2) * (gr * gr)
        m_hat = mn / (1.0 - ADAM_B1 ** ADAM_STEP)
        v_hat = vn / (1.0 - ADAM_B2 ** ADAM_STEP)
        d_ref[...] = -ADAM_LR * (m_hat / (jnp.sqrt(v_hat) + ADAM_EPS) + ADAM_WD * w_ref[...])
        go_ref[...] = gr
        mo_ref[...] = mn
        vo_ref[...] = vn

    spec = pl.BlockSpec((tr, cols), lambda i: (i, 0))
    g_spec = spec if g_plane is None else pl.BlockSpec((None, tr, cols), lambda i: (g_plane, i, 0))
    return pl.pallas_call(
        body, name=name, grid=(rows // tr,),
        in_specs=[spec, g_spec, spec, spec], out_specs=[spec, spec, spec, spec],
        out_shape=[SDS((rows, cols), F32)] * 4,
        cost_estimate=_cost(0, (w, m, v), [SDS((rows, cols), F32)] * 5),
        compiler_params=_params(("parallel",), 8 * _nbytes((tr, cols), F32), 4 * _nbytes((tr, cols), F32)),
    )(w, g, m, v)


def _pack(parts):
    rows = []
    for p in parts:
        r = p.reshape(-1, LANES)
        pad = (-r.shape[0]) % SUBLANES_F32
        if pad:
            r = jnp.pad(r, ((0, pad), (0, 0)))
        rows.append(r)
    return jnp.concatenate(rows, axis=0)


def _unpack(packed, shapes):
    out, at = [], 0
    for s in shapes:
        n = 1
        for q in s:
            n *= q
        r = n // LANES
        out.append(packed[at:at + r].reshape(s))
        at += r + (-r) % SUBLANES_F32
    return out


def kernel(x, norm_mix, w_in, pool_w, pool_scale, w_pool_proj, conv_w, w_conv_out, w_o, norm_ffn, w_up, ffn_conv_w, ffn_conv_b, w_down, norm_final, loss_target, m_norm_mix, m_w_in, m_pool_w, m_pool_scale, m_w_pool_proj, m_conv_w, m_w_conv_out, m_w_o, m_norm_ffn, m_w_up, m_ffn_conv_w, m_ffn_conv_b, m_w_down, m_norm_final, v_norm_mix, v_w_in, v_pool_w, v_pool_scale, v_w_pool_proj, v_conv_w, v_w_conv_out, v_w_o, v_norm_ffn, v_w_up, v_ffn_conv_w, v_ffn_conv_b, v_w_down, v_norm_final):
    nseq, seq, d = x.shape
    t = nseq * seq
    f = w_down.shape[1] * 4
    c = d // N_GROUPS
    xy = lax.axis_index("x") * 2 + lax.axis_index("y")
    c_arr = lax.axis_index("c").astype(jnp.int32).reshape(1)
    jc_arr = jnp.stack([xy, lax.axis_index("c")]).astype(jnp.int32)
    nsh = 4
    zero = jnp.zeros((), jnp.int32)

    locs = [w_in[0].astype(BF16),
            jnp.stack([w_pool_proj[0], w_conv_out[0], w_o[0]]).astype(BF16),
            w_up[0].astype(BF16), w_down[0].astype(BF16), pool_w[0].astype(BF16)]
    full_shapes = [(nsh, d, N_SPLITS * d // nsh), (3, d, d), (nsh, d, 2 * f // nsh), (f, d), (N_GROUPS, c, c)]

    cw_pad = lax.dynamic_update_slice(jnp.zeros((3, d), F32), conv_w[0], (zero, xy * (d // 4)))
    fw_pad = lax.dynamic_update_slice(jnp.zeros((3, 2 * f), F32), ffn_conv_w[0], (zero, xy * (f // 2)))
    small_w = _pack([cw_pad, fw_pad]) * 0.5

    x2d = x.reshape(t, d)
    tgt = loss_target.reshape(t, d)
    ax, ay = lax.axis_index("x"), lax.axis_index("y")
    order = jnp.stack([xy, 2 * (1 - ax) + ay, 2 * ax + 1 - ay, 2 * (1 - ax) + 1 - ay]).astype(jnp.int32)
    (z, h1, w_in_f), (pool_w_f, w3_f, slots_w) = _fwd_in(
        x2d, norm_mix, locs[0], order,
        _merge([_gather_comm([4], locs, full_shapes), _gather_comm([1], locs, full_shapes, part=(0, 1, 2)),
                _small_comm(small_w)]))
    conv_w_f, ffn_cw_f = _unpack(_sum8(slots_w, "sum8_weights"), [(3, d), (3, 2 * f)])
    ffn_cw_p = ffn_cw_f.reshape(3, 2, f).transpose(1, 0, 2)
    ffn_cb_p = ffn_conv_b.reshape(2, 1, f)
    (lhs3,), (w3_f,) = _mixer_mid_fwd(z, pool_w_f, pool_scale, conv_w_f, nseq,
                                      _gather_comm([1], locs, full_shapes, part=(1, 2, 2), into={1: w3_f}))
    (lhs3, ypc, x1, h2), (w_up_f,) = _mixer_out(lhs3, z, x2d, w3_f, norm_ffn,
                                                _ring_gather_comm([2], locs, full_shapes))
    (u0,), (w_down_f,) = _ffn_up(h2, w_up_f, f, _gather_comm([3], locs, full_shapes))
    act, ua = _ffn_mid_fwd(u0, ffn_cw_p, ffn_cb_p, nseq)
    dx2, dx2b, loss11, g_norm_final = _ffn_down_loss(act, w_down_f, x1, tgt, norm_final.reshape(1, d))

    gbs, lands, ps, lands2, rs = {}, {}, {}, {}, {}
    tn_up = _tile(2 * f // nsh, 1408, LANES)
    npp = f // tn_up

    def add(arrs, name):
        for a, p in zip(arrs, _add_halves(arrs, gbs, [lands[a] for a in arrs], c_arr, name)):
            ps[a] = p

    def summed(a):
        rs[a] = _sum_chips(a, ps[a], lands2[a], _shard_shape(a, full_shapes[a]), jc_arr, "sum_chips_%d" % a)

    (gbs[3],), _ = _wgrad(act, dx2b, "wgrad_down", tr=tn_up, tn=d)
    (da,), (lands[3],) = _ffn_bwd_da(dx2b, w_down_f, _halves_comm([3], gbs))
    add([3], "add_halves_down")
    (du0, g_ffn_cw_p, g_ffn_cb_p), (lands2[3],) = _ffn_mid_bwd(da, u0, ua, ffn_cw_p, nseq, _chips_comm([3], ps))
    summed(3)
    (gbs[2],), (rs[3],) = _wgrad(h2, du0, "wgrad_up", tr=d, tn=tn_up, b_plane_of=lambda n: (n // npp, n % npp),
                                 out_shards=nsh, comm=_result_comm([3], rs))
    (dx1, rhs3, g_norm_ffn), (lands[2],) = _ffn_bwd_dx1(du0, w_up_f, x1, dx2, norm_ffn, 3, _halves_comm([2], gbs))
    add([2], "add_halves_up")
    (rhs3, dz, dpq), (lands2[2],) = _mixer_bwd(rhs3, z, ypc, w3_f, _chips_comm([2], ps, part=(0, 1, 2)))
    (gbs[1],), (lands2[2],) = _wgrad3(lhs3, rhs3, _chips_comm([2], ps, part=(1, 2, 2), into=lands2))
    summed(2)
    (dz, g_conv_w), (lands[1], rs[2]) = _conv_bwd(dz, dpq, z, conv_w_f, nseq,
                                                  _merge([_halves_comm([1], gbs), _result_comm([2], rs)]))
    add([1], "add_halves_sq3")
    (dz, g_pool_w, g_pool_scale), _ = _pool_bwd_call(dz, dpq, z, pool_w_f, pool_scale, nseq)
    gbs[4] = g_pool_w.astype(BF16)
    (gbs[0],), (lands2[1],) = _wgrad_in(h1, dz, nsh, _chips_comm([1], ps))
    summed(1)
    lands[0], lands[4] = _run_comm(_halves_comm([0, 4], gbs), "exchange_halves_in")
    add([0, 4], "add_halves_in")
    g_ffn_cw = g_ffn_cw_p.transpose(1, 0, 2).reshape(3, 2 * f)
    small_a = _pack([g_pool_scale, g_norm_ffn, g_ffn_cb_p.reshape(1, 2 * f), g_norm_final.reshape(d), g_conv_w,
                     g_ffn_cw, jnp.pad(loss11, ((0, SUBLANES_F32 - 1), (0, LANES - 1)))])
    (grad_x, g_norm_mix), (lands2[0], lands2[4], rs[1], slots_a) = _mixer_bwd_dx(
        dz, w_in_f, x2d, dx1, norm_mix,
        _merge([_chips_comm([0, 4], ps), _result_comm([1], rs), _small_comm(small_a)]))
    summed(0)
    summed(4)
    rs[0], rs[4], slots_b = _run_comm(_merge([_result_comm([0, 4], rs), _small_comm(_pack([g_norm_mix]))]),
                                      "exchange_result_in")
    shapes_a = [(1, d), (1, d), (1, 2 * f), (d,), (3, d), (3, 2 * f), (SUBLANES_F32, LANES)]
    gs_pool_scale, gs_norm_ffn, gs_ffn_cb, gs_norm_final, gs_conv_w, gs_ffn_cw, loss_blk = _unpack(
        _sum8(slots_a, "sum8_grads"), shapes_a)
    (gs_norm_mix,) = _unpack(_sum8(slots_b, "sum8_norm_mix"), [(1, d)])
    gs_conv_w = lax.dynamic_slice(gs_conv_w, (zero, xy * (d // 4)), (3, d // 4))
    gs_ffn_cw = lax.dynamic_slice(gs_ffn_cw, (zero, xy * (f // 2)), (3, f // 2))

    def upd(w, g, m, v, name, g_plane=None):
        shape = w.shape
        rows = 1
        for q in shape[:-1]:
            rows *= q
        g2 = g if g_plane is not None else g.reshape(rows, shape[-1])
        outs = _adamw(w.reshape(rows, shape[-1]), g2, m.reshape(rows, shape[-1]), v.reshape(rows, shape[-1]),
                      name, g_plane)
        return [o.reshape(shape) for o in outs]

    res = {
        "w_in": upd(w_in, rs[0], m_w_in, v_w_in, "adamw_w_in"),
        "pool_w": upd(pool_w, rs[4], m_pool_w, v_pool_w, "adamw_pool_w"),
        "w_pool_proj": upd(w_pool_proj, rs[1], m_w_pool_proj, v_w_pool_proj, "adamw_w_pool_proj", 0),
        "w_conv_out": upd(w_conv_out, rs[1], m_w_conv_out, v_w_conv_out, "adamw_w_conv_out", 1),
        "w_o": upd(w_o, rs[1], m_w_o, v_w_o, "adamw_w_o", 2),
        "w_up": upd(w_up, rs[2], m_w_up, v_w_up, "adamw_w_up"),
        "w_down": upd(w_down, rs[3], m_w_down, v_w_down, "adamw_w_down"),
    }

    small_names = ["norm_mix", "pool_scale", "norm_ffn", "ffn_conv_b", "norm_final", "conv_w", "ffn_conv_w"]
    small_ws = [norm_mix, pool_scale, norm_ffn, ffn_conv_b, norm_final, conv_w, ffn_conv_w]
    small_ms = [m_norm_mix, m_pool_scale, m_norm_ffn, m_ffn_conv_b, m_norm_final, m_conv_w, m_ffn_conv_w]
    small_vs = [v_norm_mix, v_pool_scale, v_norm_ffn, v_ffn_conv_b, v_norm_final, v_conv_w, v_ffn_conv_w]
    small_gs = [gs_norm_mix, gs_pool_scale, gs_norm_ffn, gs_ffn_cb, gs_norm_final, gs_conv_w, gs_ffn_cw]
    _, sd, sm, sv = _adamw(_pack(small_ws), _pack(small_gs), _pack(small_ms), _pack(small_vs), "adamw_small")
    shapes = [w.shape for w in small_ws]
    sd, sm, sv = _unpack(sd, shapes), _unpack(sm, shapes), _unpack(sv, shapes)
    for i, nm in enumerate(small_names):
        res[nm] = [small_gs[i].reshape(shapes[i]), sd[i], sm[i], sv[i]]

    order = ["norm_mix", "w_in", "pool_w", "pool_scale", "w_pool_proj", "conv_w", "w_conv_out", "w_o", "norm_ffn",
             "w_up", "ffn_conv_w", "ffn_conv_b", "w_down", "norm_final"]
    return (loss_blk[0, 0], grad_x.reshape(x.shape), *[res[n][0] for n in order], *[res[n][1] for n in order],
            *[res[n][2] for n in order], *[res[n][3] for n in order])
```

```python
import math

import jax
import jax.numpy as jnp
from jax import lax
from jax.experimental import pallas as pl
from jax.experimental.pallas import tpu as pltpu

F32 = jnp.float32
BF16 = jnp.bfloat16
SDS = jax.ShapeDtypeStruct
MESH = pl.DeviceIdType.MESH

RMS_EPS = 1e-6
POOL_WINDOWS = (2, 4, 8, 16)
N_GROUPS = len(POOL_WINDOWS)
N_SPLITS = 6

ADAM_LR = 0.001
ADAM_B1 = 0.9
ADAM_B2 = 0.999
ADAM_EPS = 1e-08
ADAM_WD = 0.01
ADAM_STEP = 10

LANES = 128
SUBLANES_F32 = 8
SUBLANES_BF16 = 16
VMEM_BYTES = 64 * 1024 * 1024
VMEM_CAP = VMEM_BYTES - 8 * 1024 * 1024
VMEM_FLOOR = 16 * 1024 * 1024

ANY = pl.BlockSpec(memory_space=pl.ANY)


def _tile(dim, pref, align):
    if dim <= pref:
        return dim
    t = (pref // align) * align
    while t >= align:
        if dim % t == 0:
            return t
        t -= align
    return dim


def _nbytes(shape, dtype):
    n = 1
    for s in shape:
        n *= s
    return n * jnp.dtype(dtype).itemsize


def _params(sem, block_bytes, temp_bytes=0, collective_id=None):
    need = 2 * block_bytes + temp_bytes + 4 * 1024 * 1024
    return pltpu.CompilerParams(dimension_semantics=sem, collective_id=collective_id,
                                vmem_limit_bytes=int(min(max(need, VMEM_FLOOR), VMEM_CAP)))


SIBLING = (0, 0, 1)
CHIPS = ((1, 0, 0), (0, 1, 0), (1, 1, 0))
EVERYONE = tuple((a, b, c) for a in range(2) for b in range(2) for c in range(2) if a + b + c)
PEER_SETS = (frozenset([SIBLING]), frozenset(CHIPS), frozenset(CHIPS + (SIBLING,)), frozenset(EVERYONE))
MID_AT = 0.75


def _collective_id(peers):
    return PEER_SETS.index(frozenset(peers))


def _handshake(peers):
    x, y, c = lax.axis_index("x"), lax.axis_index("y"), lax.axis_index("c")
    bar = pltpu.get_barrier_semaphore()
    for fx, fy, fc in sorted(peers):
        dev = (1 - x if fx else x, 1 - y if fy else y, 1 - c if fc else c)
        pl.semaphore_signal(bar, inc=1, device_id=dev, device_id_type=MESH)
    pl.semaphore_wait(bar, len(peers))


class _Comm:
    def __init__(self, ins, out_shapes, sems, start, finish, peers, aliases=None, mid=None):
        self.ins = list(ins)
        self.out_shapes = list(out_shapes)
        self.sems = list(sems)
        self.start = start
        self.finish = finish
        self.mid = mid
        self.peers = frozenset(peers)
        self.aliases = dict(aliases or {})


def _pcall(body, *, name, grid, in_specs, out_specs, out_shape, sem, blocks, temps=0, scratch_shapes=(),
           input_output_aliases=None, comm=None):
    in_specs = list(in_specs)
    out_specs = list(out_specs)
    out_shape = list(out_shape)
    scratch_shapes = list(scratch_shapes)
    aliases = dict(input_output_aliases or {})
    n_in, n_out, n_scr = len(in_specs), len(out_shape), len(scratch_shapes)
    if comm is None:
        call = pl.pallas_call(
            body, name=name, grid=grid, in_specs=in_specs, out_specs=out_specs, out_shape=out_shape,
            scratch_shapes=scratch_shapes, input_output_aliases=aliases,
            compiler_params=_params(sem, blocks, temps))
        return lambda *args: (list(call(*args)), [])

    nci, nco = len(comm.ins), len(comm.out_shapes)
    n_steps = 1
    for g in grid:
        n_steps *= g

    def hosted(*refs):
        ins = refs[:n_in]
        cins = refs[n_in:n_in + nci]
        outs = refs[n_in + nci:n_in + nci + n_out]
        couts = refs[n_in + nci + n_out:n_in + nci + n_out + nco]
        scr = refs[n_in + nci + n_out + nco:n_in + nci + n_out + nco + n_scr]
        csems = refs[n_in + nci + n_out + nco + n_scr:]
        first = None
        last = None
        step = 0
        for q, g in enumerate(grid):
            pid = pl.program_id(q)
            first = (pid == 0) if first is None else first & (pid == 0)
            last = (pid == g - 1) if last is None else last & (pid == g - 1)
            step = step * g + pid

        @pl.when(first)
        def _():
            _handshake(comm.peers)
            comm.start(cins, couts, csems)

        if comm.mid is not None:
            @pl.when(step == int(MID_AT * n_steps))
            def _():
                comm.mid(cins, couts, csems)

        body(*ins, *outs, *scr)

        @pl.when(last)
        def _():
            comm.finish(cins, couts, csems)

    for i, o in comm.aliases.items():
        aliases[n_in + i] = n_out + o
    call = pl.pallas_call(
        hosted, name=name, grid=grid, in_specs=in_specs + [ANY] * nci, out_specs=out_specs + [ANY] * nco,
        out_shape=out_shape + comm.out_shapes, scratch_shapes=scratch_shapes + comm.sems,
        input_output_aliases=aliases,
        compiler_params=_params(("arbitrary",) * len(grid), blocks, temps, _collective_id(comm.peers)))

    def run(*args):
        res = call(*args, *comm.ins)
        return list(res[:n_out]), list(res[n_out:])

    return run


def _run_comm(comm, name):
    def body(*refs):
        nci, nco = len(comm.ins), len(comm.out_shapes)
        cins, couts, csems = refs[:nci], refs[nci:nci + nco], refs[nci + nco:]
        _handshake(comm.peers)
        comm.start(cins, couts, csems)
        if comm.mid is not None:
            comm.mid(cins, couts, csems)
        comm.finish(cins, couts, csems)

    return list(pl.pallas_call(
        body, name=name, in_specs=[ANY] * len(comm.ins), out_specs=[ANY] * len(comm.out_shapes),
        out_shape=comm.out_shapes, scratch_shapes=comm.sems, input_output_aliases=comm.aliases,
        compiler_params=pltpu.CompilerParams(collective_id=_collective_id(comm.peers)),
    )(*comm.ins))


def _dot(a, b):
    return jnp.dot(a, b, preferred_element_type=F32)


def _dot_tb(a, b):
    return lax.dot_general(a, b, (((1,), (1,)), ((), ())), preferred_element_type=F32)


def _dot_ta(a, b):
    return lax.dot_general(a, b, (((0,), (0,)), ((), ())), preferred_element_type=F32)


def _rms_fwd(x):
    inv = lax.rsqrt(jnp.mean(x * x, axis=-1, keepdims=True) + RMS_EPS)
    return x * inv, inv


def _rms_bwd(dy, xhat, inv, g):
    gd = dy * g
    return inv * (gd - xhat * jnp.mean(gd * xhat, axis=-1, keepdims=True))


def _sigmoid(x):
    return 1.0 / (1.0 + jnp.exp(-x))


def _shift_down(x, k, row):
    return jnp.where(row >= k, pltpu.roll(x, k, 0), 0.0)


def _shift_up(x, k, row):
    s = x.shape[0]
    return jnp.where(row < s - k, pltpu.roll(x, s - k, 0), 0.0)


def _pool_fwd(u, win, row):
    s = u
    k = 1
    while k < win:
        s = s + _shift_down(s, k, row)
        k *= 2
    cnt = jnp.minimum(row + 1, win).astype(F32)
    return s / cnt - u


def _pool_bwd(dp, win, row):
    cnt = jnp.minimum(row + 1, win).astype(F32)
    s = dp / cnt
    k = 1
    while k < win:
        s = s + _shift_up(s, k, row)
        k *= 2
    return s - dp


def _acc_over(k, nk, part, acc, o_ref):
    @pl.when(k == 0)
    def _():
        acc[...] = part

    @pl.when(k > 0)
    def _():
        acc[...] += part

    @pl.when(k == nk - 1)
    def _():
        o_ref[...] = acc[...].astype(o_ref.dtype)


def _fwd_in(x, g, w_loc, order, comm):
    t, d = x.shape
    ws = w_loc.shape[1]
    nsh = order.shape[0]
    assert nsh == 4, "the shard walk below is written for the 2 x 2 chips of the mesh"
    tm = _tile(t, 1024, SUBLANES_BF16)
    ni = t // tm
    nci, nco = len(comm.ins), len(comm.out_shapes)
    all_peers = comm.peers | frozenset(CHIPS + (SIBLING,))

    def body(order_ref, x_ref, g_ref, loc_ref, *rest):
        del order_ref
        cins = rest[:nci]
        z_ref, h_ref, full_ref = rest[nci:nci + 3]
        couts = rest[nci + 3:nci + 3 + nco]
        (hs, wbuf, wsem, own_s, own_r, snd_s, snd_r, fwd_s, fwd_r, rly_s, rly_r) = rest[nci + 3 + nco:nci + 14 + nco]
        csems = rest[nci + 14 + nco:]
        j = pl.program_id(0)
        i = pl.program_id(1)
        x_, y_, c_ = _coords()
        own = 2 * x_ + y_
        sib = (x_, y_, 1 - c_)
        peers = _peer_chips(x_, y_)

        def sends():
            return [_remote(_half(0, loc_ref, c_), _piece(0, full_ref, own, c_), snd_s.at[p], snd_r.at[p], (px, py, c_))
                    for p, (px, py) in enumerate(peers[:2])]

        def relays():
            out = []
            for q, (src_p, dst_p) in enumerate(((0, 1), (1, 0))):
                sx, sy = peers[src_p]
                part = _rows_part(0, _piece(0, full_ref, 2 * sx + sy, c_), (q, q + 1, 2))
                out.append(_remote(part, part, rly_s.at[q], rly_r.at[q], (*peers[dst_p], c_)))
            return out

        def owns():
            return [_remote(_half(0, loc_ref, h), _piece(0, full_ref, own, h), own_s.at[h], own_r.at[h], sib)
                    for h in range(2)]

        def forward(p, half):
            px, py = peers[p]
            landed = _piece(0, full_ref, 2 * px + py, half)
            return _remote(landed, landed, fwd_s.at[p], fwd_r.at[p], sib)

        def load(src, slot):
            return pltpu.make_async_copy(src, wbuf.at[slot], wsem.at[slot])

        @pl.when((j == 0) & (i == 0))
        def _():
            _handshake(all_peers)
            for cp in sends() + owns():
                cp.start()
            load(loc_ref, 0).start()

        @pl.when(j == 0)
        def _():
            xh, _ = _rms_fwd(x_ref[...])
            h = (xh * g_ref[...]).astype(BF16)
            hs[pl.ds(pl.multiple_of(i * tm, tm), tm), :] = h
            h_ref[...] = h

        slot = j % 2

        @pl.when(i == 0)
        def _():
            load(loc_ref, slot).wait()

        z_ref[...] = _dot(hs[pl.ds(pl.multiple_of(i * tm, tm), tm), :], wbuf[slot]).astype(BF16)

        def load_shard(p, into):
            px, py = peers[p]
            forward(p, 1 - c_).wait_recv()
            load(full_ref.at[2 * px + py], into).start()

        @pl.when((j == 0) & (i == ni - 1))
        def _():
            for cp in sends():
                cp.wait_recv()
            for cp in relays() + [forward(0, c_), forward(1, c_)]:
                cp.start()
            load_shard(0, 1)
            comm.start(cins, couts, csems)

        @pl.when((j == 1) & (i == 0))
        def _():
            load_shard(1, 0)

        @pl.when((j == 2) & (i == max(ni - 2, 0)))
        def _():
            for cp in relays():
                cp.wait_recv()
            forward(2, c_).start()
            load_shard(2, 1)

        @pl.when((j == nsh - 1) & (i == ni - 1))
        def _():
            for cp in sends() + relays() + [forward(p, c_) for p in range(nsh - 1)]:
                cp.wait_send()
            for cp in owns():
                cp.wait()
            comm.finish(cins, couts, csems)

    last = ni - 1
    blocks = _nbytes((tm, d), F32) + _nbytes((tm, ws), BF16) + _nbytes((tm, d), BF16)
    scratch = _nbytes((t, d), BF16) + 2 * _nbytes((d, ws), BF16)
    res = pl.pallas_call(
        body, name="fwd_in",
        grid_spec=pltpu.PrefetchScalarGridSpec(
            num_scalar_prefetch=1, grid=(nsh, ni),
            in_specs=[pl.BlockSpec((tm, d), lambda j, i, o: (jnp.where(j == 0, i, last), 0)),
                      pl.BlockSpec((1, d), lambda j, i, o: (0, 0)), ANY] + [ANY] * nci,
            out_specs=[pl.BlockSpec((tm, ws), lambda j, i, o: (i, o[j])),
                       pl.BlockSpec((tm, d), lambda j, i, o: (jnp.where(j == 0, i, last), 0)), ANY] + [ANY] * nco,
            scratch_shapes=[pltpu.VMEM((t, d), BF16), pltpu.VMEM((2, d, ws), BF16)]
            + _dma_sems(2, 2, 2, 2, 2, nsh - 1, nsh - 1, 2, 2) + comm.sems),
        out_shape=[SDS((t, nsh * ws), BF16), SDS((t, d), BF16), SDS((nsh, d, ws), BF16)] + comm.out_shapes,
        input_output_aliases={4 + i: 3 + o for i, o in comm.aliases.items()},
        compiler_params=_params(("arbitrary", "arbitrary"), blocks, scratch + 3 * _nbytes((tm, d), F32),
                                _collective_id(all_peers)),
    )(order, x, g, w_loc, *comm.ins)
    return list(res[:3]), list(res[3:])


def _mixer_mid_fwd(z, pool_w, pool_scale, conv_w, nseq, comm=None):
    t = z.shape[0]
    d = pool_scale.shape[1]
    s = t // nseq
    c = d // N_GROUPS

    def body(zp, zb, zc, zv, pw, ps, cw, o):
        j = pl.program_id(1)
        row = lax.broadcasted_iota(jnp.int32, (s, c), 0)
        for gi, win in enumerate(POOL_WINDOWS):
            @pl.when(j == gi)
            def _(win=win):
                pooled = _pool_fwd(zp[...].astype(F32), win, row)
                o[0] = (_dot(pooled.astype(BF16), pw[...]) * ps[...]).astype(BF16)

        cv = zc[...].astype(F32) * zv[...].astype(F32)
        cc = (cw[pl.ds(2, 1), :] * cv + cw[pl.ds(1, 1), :] * _shift_down(cv, 1, row)
              + cw[pl.ds(0, 1), :] * _shift_down(cv, 2, row))
        o[1] = (zb[...].astype(F32) * cc).astype(BF16)

    blocks = 4 * _nbytes((s, c), BF16) + _nbytes((c, c), BF16) + _nbytes((2, s, c), BF16)
    return _pcall(
        body, name="mixer_mid_fwd", grid=(nseq, N_GROUPS),
        in_specs=[pl.BlockSpec((s, c), lambda b, j: (b, j)),
                  pl.BlockSpec((s, c), lambda b, j: (b, N_GROUPS + j)),
                  pl.BlockSpec((s, c), lambda b, j: (b, 2 * N_GROUPS + j)),
                  pl.BlockSpec((s, c), lambda b, j: (b, 3 * N_GROUPS + j)),
                  pl.BlockSpec((None, c, c), lambda b, j: (j, 0, 0)),
                  pl.BlockSpec((1, c), lambda b, j: (0, j)),
                  pl.BlockSpec((3, c), lambda b, j: (0, j))],
        out_specs=[pl.BlockSpec((2, s, c), lambda b, j: (0, b, j))],
        out_shape=[SDS((3, t, d), BF16)],
        sem=("parallel", "parallel"), blocks=blocks, temps=8 * _nbytes((s, c), F32), comm=comm,
    )(z, z, z, z, pool_w, pool_scale, conv_w)


def _mixer_out(lhs3, z, x, w3, g_ffn, comm=None):
    t, d = x.shape
    tm = _tile(t, 256, SUBLANES_BF16)

    def body(pq, zgp, zgc, x_ref, w_ref, g_ref, mrg, ypc, x1o, h2o):
        yp = _dot(pq[0], w_ref[0])
        yc = _dot(pq[1], w_ref[1])
        m = _sigmoid(zgp[...].astype(F32)) * yp + _sigmoid(zgc[...].astype(F32)) * yc
        mb = m.astype(BF16)
        x1 = x_ref[...] + _dot(mb, w_ref[2])
        ypc[0] = yp.astype(BF16)
        ypc[1] = yc.astype(BF16)
        mrg[...] = mb
        x1o[...] = x1
        xh, _ = _rms_fwd(x1)
        h2o[...] = (xh * g_ref[...]).astype(BF16)

    blocks = (_nbytes((2, tm, d), BF16) * 2 + _nbytes((tm, d), BF16) * 4 + _nbytes((tm, d), F32) * 2
              + _nbytes((3, d, d), BF16))
    return _pcall(
        body, name="mixer_out", grid=(t // tm,),
        in_specs=[pl.BlockSpec((2, tm, d), lambda i: (0, i, 0)),
                  pl.BlockSpec((tm, d), lambda i: (i, 4)),
                  pl.BlockSpec((tm, d), lambda i: (i, 5)),
                  pl.BlockSpec((tm, d), lambda i: (i, 0)),
                  pl.BlockSpec((3, d, d), lambda i: (0, 0, 0)),
                  pl.BlockSpec((1, d), lambda i: (0, 0))],
        out_specs=[pl.BlockSpec((None, tm, d), lambda i: (2, i, 0)),
                   pl.BlockSpec((2, tm, d), lambda i: (0, i, 0)),
                   pl.BlockSpec((tm, d), lambda i: (i, 0)),
                   pl.BlockSpec((tm, d), lambda i: (i, 0))],
        out_shape=[SDS(lhs3.shape, BF16), SDS((2, t, d), BF16), SDS((t, d), F32), SDS((t, d), BF16)],
        input_output_aliases={0: 0},
        sem=("parallel",), blocks=blocks, temps=8 * _nbytes((tm, d), F32), comm=comm,
    )(lhs3, z, z, x, w3, g_ffn)


def _ffn_up(h2, w_up, f, comm=None):
    t, d = h2.shape
    _, _, ws = w_up.shape
    tm = _tile(t, 2048, SUBLANES_BF16)
    tn = _tile(ws, 1408, LANES)
    nps = ws // tn
    npp = f // tn

    def body(h_ref, w_ref, o_ref):
        o_ref[...] = _dot(h_ref[...], w_ref[...]).astype(BF16)

    blocks = _nbytes((tm, d), BF16) + _nbytes((d, tn), BF16) + _nbytes((tm, tn), BF16)
    return _pcall(
        body, name="ffn_up", grid=(t // tm, 2 * npp),
        in_specs=[pl.BlockSpec((tm, d), lambda i, j: (i, 0)),
                  pl.BlockSpec((None, d, tn), lambda i, j: (j // nps, 0, j % nps))],
        out_specs=[pl.BlockSpec((None, tm, tn), lambda i, j: (j // npp, i, j % npp))],
        out_shape=[SDS((2, t, f), BF16)],
        sem=("parallel", "parallel"), blocks=blocks, temps=_nbytes((tm, tn), F32), comm=comm,
    )(h2, w_up)


def _conv3_rows(u, u1, u2, w_ref, p):
    return w_ref[p, pl.ds(2, 1), :] * u + w_ref[p, pl.ds(1, 1), :] * u1 + w_ref[p, pl.ds(0, 1), :] * u2


WGRAD_TOKENS = 2048
WGRAD_TOKENS_WIDE = 4096
CHUNK = 64
HALO = SUBLANES_F32


def _up1_up2(u, nxt):
    rows = u.shape[0]
    ext = jnp.concatenate([u, nxt], axis=0)
    n = rows + HALO
    return pltpu.roll(ext, n - 1, 0)[:rows], pltpu.roll(ext, n - 2, 0)[:rows]


def _fold8(x):
    return jnp.sum(x.reshape(x.shape[0] // SUBLANES_F32, SUBLANES_F32, x.shape[1]), axis=0)


def _ffn_mid_fwd(u0, cw, cb, nseq):
    _, t, f = u0.shape
    s = t // nseq
    c = _tile(f, 256, LANES)

    def body(u_ref, w_ref, b_ref, a_ref, uo_ref):
        row = lax.broadcasted_iota(jnp.int32, (s, c), 0)
        act = []
        for p in range(2):
            u = u_ref[p].astype(F32)
            act.append(_conv3_rows(u, _shift_down(u, 1, row), _shift_down(u, 2, row), w_ref, p) + b_ref[p])
            uo_ref[p] = act[p].astype(BF16)
        ug, uv = act
        a_ref[...] = (ug * _sigmoid(ug) * uv).astype(BF16)

    blocks = 2 * _nbytes((2, s, c), BF16) + _nbytes((s, c), BF16)
    outs, _ = _pcall(
        body, name="ffn_mid_fwd", grid=(f // c, nseq),
        in_specs=[pl.BlockSpec((2, s, c), lambda j, b: (0, b, j)),
                  pl.BlockSpec((2, 3, c), lambda j, b: (0, 0, j)),
                  pl.BlockSpec((2, 1, c), lambda j, b: (0, 0, j))],
        out_specs=[pl.BlockSpec((s, c), lambda j, b: (b, j)),
                   pl.BlockSpec((2, s, c), lambda j, b: (0, b, j))],
        out_shape=[SDS((t, f), BF16), SDS((2, t, f), BF16)],
        sem=("parallel", "parallel"), blocks=blocks, temps=8 * _nbytes((s, c), F32),
    )(u0, cw, cb)
    return outs


def _ffn_down_loss(a, w_down, x1, tgt, g_fin):
    t, f = a.shape
    d = x1.shape[1]
    tm = _tile(t, 256, SUBLANES_BF16)
    nsteps = t // tm

    def body(a_ref, w_ref, x1_ref, t_ref, g_ref, dx_ref, dxb_ref, loss_ref, gg_ref, lacc):
        i = pl.program_id(0)

        @pl.when(i == 0)
        def _():
            lacc[...] = jnp.zeros_like(lacc)
            gg_ref[...] = jnp.zeros_like(gg_ref)

        x2 = x1_ref[...] + _dot(a_ref[...], w_ref[...])
        xh, inv = _rms_fwd(x2)
        g = g_ref[...]
        e = xh * g - t_ref[...]
        lacc[...] += jnp.sum(e * e, axis=0, keepdims=True)
        dy = e * (1.0 / d)
        gg_ref[...] += jnp.sum(dy * xh, axis=0, keepdims=True)
        dx2 = _rms_bwd(dy, xh, inv, g)
        dx_ref[...] = dx2
        dxb_ref[...] = dx2.astype(BF16)

        @pl.when(i == nsteps - 1)
        def _():
            loss_ref[...] = jnp.sum(lacc[...], axis=1, keepdims=True) * (0.5 / d)

    blocks = (_nbytes((tm, f), BF16) + _nbytes((f, d), BF16) + 3 * _nbytes((tm, d), F32) + _nbytes((tm, d), BF16))
    outs, _ = _pcall(
        body, name="ffn_down_loss", grid=(nsteps,),
        in_specs=[pl.BlockSpec((tm, f), lambda i: (i, 0)), pl.BlockSpec((f, d), lambda i: (0, 0)),
                  pl.BlockSpec((tm, d), lambda i: (i, 0)), pl.BlockSpec((tm, d), lambda i: (i, 0)),
                  pl.BlockSpec((1, d), lambda i: (0, 0))],
        out_specs=[pl.BlockSpec((tm, d), lambda i: (i, 0)), pl.BlockSpec((tm, d), lambda i: (i, 0)),
                   pl.BlockSpec((1, 1), lambda i: (0, 0)), pl.BlockSpec((1, d), lambda i: (0, 0))],
        out_shape=[SDS((t, d), F32), SDS((t, d), BF16), SDS((1, 1), F32), SDS((1, d), F32)],
        scratch_shapes=[pltpu.VMEM((1, d), F32)],
        sem=("arbitrary",), blocks=blocks, temps=8 * _nbytes((tm, d), F32),
    )(a, w_down, x1, tgt, g_fin)
    return outs


def _ffn_bwd_da(dxb, w_down, comm=None):
    t, d = dxb.shape
    f = w_down.shape[0]
    tm = _tile(t, 512, SUBLANES_BF16)

    def body(x_ref, w_ref, o_ref):
        o_ref[...] = _dot_tb(x_ref[...], w_ref[...]).astype(BF16)

    blocks = _nbytes((tm, d), BF16) + _nbytes((tm, f), BF16)
    return _pcall(
        body, name="ffn_bwd_da", grid=(t // tm,),
        in_specs=[pl.BlockSpec((tm, d), lambda i: (i, 0)),
                  pl.BlockSpec((f, d), lambda i: (0, 0), pipeline_mode=pl.Buffered(1))],
        out_specs=[pl.BlockSpec((tm, f), lambda i: (i, 0))],
        out_shape=[SDS((t, f), BF16)],
        sem=("parallel",), blocks=blocks, temps=_nbytes((f, d), BF16) + _nbytes((tm, f), F32), comm=comm,
    )(dxb, w_down)


def _ffn_mid_bwd(da, u0, ua, cw, nseq, comm=None):
    _, t, f = u0.shape
    s = t // nseq
    c = _tile(f, 128, LANES)
    r = _tile(s, CHUNK, SUBLANES_BF16)
    n = s // r

    def body(da_ref, u_ref, ua_ref, w_ref, du_ref, gw_ref, gb_ref):
        @pl.when(pl.program_id(1) == 0)
        def _():
            gw_ref[...] = jnp.zeros_like(gw_ref)
            gb_ref[...] = jnp.zeros_like(gb_ref)

        def step(i, carry):
            nxt, sums = carry
            rows = pl.ds(pl.multiple_of((n - 1 - i) * r, r), r)
            ug = ua_ref[0, rows, :].astype(F32)
            uv = ua_ref[1, rows, :].astype(F32)
            sg = _sigmoid(ug)
            dacc = da_ref[rows, :].astype(F32)
            dus = (dacc * uv * sg * (1.0 + ug * (1.0 - sg)), dacc * (ug * sg))
            first, new_sums = [], []
            for p in range(2):
                du = dus[p]
                d1, d2 = _up1_up2(du, nxt[p])
                du_ref[p, rows, :] = _conv3_rows(du, d1, d2, w_ref, p).astype(BF16)
                u = u_ref[p, rows, :].astype(F32)
                sb, s0, s1, s2 = sums[p]
                new_sums.append((sb + _fold8(du), s0 + _fold8(d2 * u), s1 + _fold8(d1 * u), s2 + _fold8(du * u)))
                first.append(du[:HALO])
            return tuple(first), tuple(new_sums)

        zero = jnp.zeros((HALO, c), F32)
        _, sums = lax.fori_loop(0, n, step, ((zero, zero), ((zero,) * 4,) * 2))
        for p in range(2):
            sb, s0, s1, s2 = sums[p]
            gb_ref[p] += jnp.sum(sb, axis=0, keepdims=True)
            gw_ref[p, pl.ds(0, 1), :] += jnp.sum(s0, axis=0, keepdims=True)
            gw_ref[p, pl.ds(1, 1), :] += jnp.sum(s1, axis=0, keepdims=True)
            gw_ref[p, pl.ds(2, 1), :] += jnp.sum(s2, axis=0, keepdims=True)

    blocks = _nbytes((s, c), BF16) + 3 * _nbytes((2, s, c), BF16)
    return _pcall(
        body, name="ffn_mid_bwd", grid=(f // c, nseq),
        in_specs=[pl.BlockSpec((s, c), lambda j, b: (b, j)),
                  pl.BlockSpec((2, s, c), lambda j, b: (0, b, j)),
                  pl.BlockSpec((2, s, c), lambda j, b: (0, b, j)),
                  pl.BlockSpec((2, 3, c), lambda j, b: (0, 0, j))],
        out_specs=[pl.BlockSpec((2, s, c), lambda j, b: (0, b, j)),
                   pl.BlockSpec((2, 3, c), lambda j, b: (0, 0, j)),
                   pl.BlockSpec((2, 1, c), lambda j, b: (0, 0, j))],
        out_shape=[SDS((2, t, f), BF16), SDS((2, 3, f), F32), SDS((2, 1, f), F32)],
        sem=("parallel", "arbitrary"), blocks=blocks, temps=4 * 1024 * 1024, comm=comm,
    )(da, u0, ua, cw)


def _wgrad(a, b, name, *, tr, tn, b_plane_of=None, out_shards=None, comm=None):
    t, m = a.shape
    n_total = b.shape[-1] * (b.shape[0] if b.ndim == 3 else 1)
    tk = _tile(t, WGRAD_TOKENS_WIDE if n_total > tn and m == tr else WGRAD_TOKENS, SUBLANES_BF16)
    nk = t // tk
    once = pl.Buffered(1) if nk == 1 else None

    def body(a_ref, b_ref, o_ref, *acc):
        part = _dot_ta(a_ref[...], b_ref[...])
        if nk == 1:
            o_ref[...] = part.astype(BF16)
        else:
            _acc_over(pl.program_id(2), nk, part, acc[0], o_ref)

    if b.ndim == 3:
        b_spec = pl.BlockSpec((None, tk, tn), lambda r, n, k: (b_plane_of(n)[0], k, b_plane_of(n)[1]))
    else:
        b_spec = pl.BlockSpec((tk, tn), lambda r, n, k: (k, n), pipeline_mode=once if n_total == tn else None)
    if out_shards is None:
        o_spec = pl.BlockSpec((tr, tn), lambda r, n, k: (r, n))
        o_shape = SDS((m, n_total), BF16)
    else:
        nps = n_total // out_shards // tn
        o_spec = pl.BlockSpec((None, tr, tn), lambda r, n, k: (n // nps, r, n % nps))
        o_shape = SDS((out_shards, m, n_total // out_shards), BF16)
    blocks = _nbytes((tk, tr), BF16) + _nbytes((tk, tn), BF16) + _nbytes((tr, tn), BF16)
    return _pcall(
        body, name=name, grid=(m // tr, n_total // tn, nk),
        in_specs=[pl.BlockSpec((tk, tr), lambda r, n, k: (k, r), pipeline_mode=once if m == tr else None), b_spec],
        out_specs=[o_spec], out_shape=[o_shape],
        scratch_shapes=[] if nk == 1 else [pltpu.VMEM((tr, tn), F32)],
        sem=("parallel", "parallel", "arbitrary"), blocks=blocks, temps=2 * _nbytes((tr, tn), F32), comm=comm,
    )(a, b)


def _wgrad3(lhs3, rhs3, comm=None):
    nw, t, d = lhs3.shape
    tk = _tile(t, WGRAD_TOKENS, SUBLANES_BF16)
    nk = t // tk

    def body(a_ref, b_ref, o_ref, *acc):
        part = _dot_ta(a_ref[...], b_ref[...])
        if nk == 1:
            o_ref[...] = part.astype(BF16)
        else:
            _acc_over(pl.program_id(1), nk, part, acc[0], o_ref)

    blocks = 2 * _nbytes((tk, d), BF16) + _nbytes((d, d), BF16)
    return _pcall(
        body, name="wgrad_sq3", grid=(nw, nk),
        in_specs=[pl.BlockSpec((None, tk, d), lambda w, k: (w, k, 0)),
                  pl.BlockSpec((None, tk, d), lambda w, k: (w, k, 0))],
        out_specs=[pl.BlockSpec((None, d, d), lambda w, k: (w, 0, 0))],
        out_shape=[SDS((nw, d, d), BF16)],
        scratch_shapes=[] if nk == 1 else [pltpu.VMEM((d, d), F32)],
        sem=("parallel", "arbitrary"), blocks=blocks, temps=2 * _nbytes((d, d), F32), comm=comm,
    )(lhs3, rhs3)


def _ffn_bwd_dx1(du0, w_up, x1, dx2, g_ffn, n_planes_out, comm=None):
    _, t, f = du0.shape
    d = x1.shape[1]
    nsh, _, ws = w_up.shape
    tm = _tile(t, 256, SUBLANES_BF16)
    spp = f // ws

    def body(du_ref, w_ref, x1_ref, dx2_ref, g_ref, dx1_ref, dxb_ref, gg_ref):
        @pl.when(pl.program_id(0) == 0)
        def _():
            gg_ref[...] = jnp.zeros_like(gg_ref)

        dh = None
        for k in range(nsh):
            part = _dot_tb(du_ref[k // spp, :, (k % spp) * ws:(k % spp + 1) * ws], w_ref[k])
            dh = part if dh is None else dh + part
        xh, inv = _rms_fwd(x1_ref[...])
        gg_ref[...] += jnp.sum(dh * xh, axis=0, keepdims=True)
        dx1 = dx2_ref[...] + _rms_bwd(dh, xh, inv, g_ref[...])
        dx1_ref[...] = dx1
        dxb_ref[...] = dx1.astype(BF16)

    blocks = _nbytes((2, tm, f), BF16) + 3 * _nbytes((tm, d), F32) + _nbytes((tm, d), BF16)
    return _pcall(
        body, name="ffn_bwd_dx1", grid=(t // tm,),
        in_specs=[pl.BlockSpec((2, tm, f), lambda i: (0, i, 0)),
                  pl.BlockSpec((nsh, d, ws), lambda i: (0, 0, 0), pipeline_mode=pl.Buffered(1)),
                  pl.BlockSpec((tm, d), lambda i: (i, 0)),
                  pl.BlockSpec((tm, d), lambda i: (i, 0)),
                  pl.BlockSpec((1, d), lambda i: (0, 0))],
        out_specs=[pl.BlockSpec((tm, d), lambda i: (i, 0)),
                   pl.BlockSpec((None, tm, d), lambda i: (n_planes_out - 1, i, 0)),
                   pl.BlockSpec((1, d), lambda i: (0, 0))],
        out_shape=[SDS((t, d), F32), SDS((n_planes_out, t, d), BF16), SDS((1, d), F32)],
        sem=("arbitrary",), blocks=blocks, temps=_nbytes(w_up.shape, BF16) + 8 * _nbytes((tm, d), F32), comm=comm,
    )(du0, w_up, x1, dx2, g_ffn)


def _mixer_bwd(rhs3, z, ypc, w3, comm=None):
    _, t, d = rhs3.shape
    tm = _tile(t, 512, SUBLANES_BF16)

    def body(dx_ref, zgp, zgc, ypc_ref, w_ref, dyo, dzo, dpq):
        dm = _dot_tb(dx_ref[...], w_ref[2])
        sp = _sigmoid(zgp[...].astype(F32))
        sc = _sigmoid(zgc[...].astype(F32))
        dyp = (dm * sp).astype(BF16)
        dyc = (dm * sc).astype(BF16)
        dzo[0] = (dm * ypc_ref[0].astype(F32) * sp * (1.0 - sp)).astype(BF16)
        dzo[1] = (dm * ypc_ref[1].astype(F32) * sc * (1.0 - sc)).astype(BF16)
        dyo[0] = dyp
        dyo[1] = dyc
        dpq[0] = _dot_tb(dyp, w_ref[0]).astype(BF16)
        dpq[1] = _dot_tb(dyc, w_ref[1]).astype(BF16)

    blocks = _nbytes((tm, d), BF16) * 3 + _nbytes((2, tm, d), BF16) * 4 + _nbytes((3, d, d), BF16)
    return _pcall(
        body, name="mixer_bwd", grid=(t // tm,),
        in_specs=[pl.BlockSpec((None, tm, d), lambda i: (2, i, 0)),
                  pl.BlockSpec((tm, d), lambda i: (i, 4)),
                  pl.BlockSpec((tm, d), lambda i: (i, 5)),
                  pl.BlockSpec((2, tm, d), lambda i: (0, i, 0)),
                  pl.BlockSpec((3, d, d), lambda i: (0, 0, 0))],
        out_specs=[pl.BlockSpec((2, tm, d), lambda i: (0, i, 0)),
                   pl.BlockSpec((2, tm, d), lambda i: (2, i, 0)),
                   pl.BlockSpec((2, tm, d), lambda i: (0, i, 0))],
        out_shape=[SDS(rhs3.shape, BF16), SDS((N_SPLITS, t, d), BF16), SDS((2, t, d), BF16)],
        input_output_aliases={0: 0},
        sem=("parallel",), blocks=blocks, temps=8 * _nbytes((tm, d), F32), comm=comm,
    )(rhs3, z, z, ypc, w3)


def _conv_bwd(dz, dpq, z, conv_w, nseq, comm=None):
    _, t, d = dz.shape
    s = t // nseq
    c = _tile(d, 256, LANES)
    nb = d // c

    def body(dz_in, dq_ref, zb, zc, zv, cw, dzo, gw_ref):
        del dz_in

        @pl.when(pl.program_id(1) == 0)
        def _():
            gw_ref[...] = jnp.zeros_like(gw_ref)

        row = lax.broadcasted_iota(jnp.int32, (s, c), 0)
        b = zb[...].astype(F32)
        cm = zc[...].astype(F32)
        v = zv[...].astype(F32)
        cv = cm * v
        cv1 = _shift_down(cv, 1, row)
        cv2 = _shift_down(cv, 2, row)
        w0, w1, w2 = cw[pl.ds(0, 1), :], cw[pl.ds(1, 1), :], cw[pl.ds(2, 1), :]
        cc = w2 * cv + w1 * cv1 + w0 * cv2
        dq = dq_ref[...].astype(F32)
        dzo[0] = (dq * cc).astype(BF16)
        dcc = dq * b
        gw_ref[pl.ds(0, 1), :] += jnp.sum(dcc * cv2, axis=0, keepdims=True)
        gw_ref[pl.ds(1, 1), :] += jnp.sum(dcc * cv1, axis=0, keepdims=True)
        gw_ref[pl.ds(2, 1), :] += jnp.sum(dcc * cv, axis=0, keepdims=True)
        dcv = w2 * dcc + w1 * _shift_up(dcc, 1, row) + w0 * _shift_up(dcc, 2, row)
        dzo[1] = (dcv * v).astype(BF16)
        dzo[2] = (dcv * cm).astype(BF16)

    blocks = 4 * _nbytes((s, c), BF16) + _nbytes((3, s, c), BF16)
    return _pcall(
        body, name="conv_bwd", grid=(nb, nseq),
        in_specs=[ANY,
                  pl.BlockSpec((None, s, c), lambda j, b: (1, b, j)),
                  pl.BlockSpec((s, c), lambda j, b: (b, nb + j)),
                  pl.BlockSpec((s, c), lambda j, b: (b, 2 * nb + j)),
                  pl.BlockSpec((s, c), lambda j, b: (b, 3 * nb + j)),
                  pl.BlockSpec((3, c), lambda j, b: (0, j))],
        out_specs=[pl.BlockSpec((3, s, c), lambda j, b: (0, b, j)),
                   pl.BlockSpec((3, c), lambda j, b: (0, j))],
        out_shape=[SDS(dz.shape, BF16), SDS((3, d), F32)],
        input_output_aliases={0: 0},
        sem=("parallel", "arbitrary"), blocks=blocks, temps=16 * _nbytes((s, c), F32), comm=comm,
    )(dz, dpq, z, z, z, conv_w)


def _pool_bwd_call(dz, dpq, z, pool_w, pool_scale, nseq, comm=None):
    _, t, d = dz.shape
    s = t // nseq
    c = d // N_GROUPS

    def body(dz_in, dp_ref, zp, pw, ps, dzo, gpw_ref, gps_ref):
        del dz_in
        j = pl.program_id(0)

        @pl.when(pl.program_id(1) == 0)
        def _():
            gpw_ref[...] = jnp.zeros_like(gpw_ref)
            gps_ref[...] = jnp.zeros_like(gps_ref)

        row = lax.broadcasted_iota(jnp.int32, (s, c), 0)
        for gi, win in enumerate(POOL_WINDOWS):
            @pl.when(j == gi)
            def _(win=win):
                pb = _pool_fwd(zp[...].astype(F32), win, row).astype(BF16)
                plin = _dot(pb, pw[...])
                dps = dp_ref[...].astype(F32)
                gps_ref[...] += jnp.sum(dps * plin, axis=0, keepdims=True)
                dplb = (dps * ps[...]).astype(BF16)
                gpw_ref[...] += _dot_ta(pb, dplb)
                dzo[...] = _pool_bwd(_dot_tb(dplb, pw[...]), win, row).astype(BF16)

    blocks = 3 * _nbytes((s, c), BF16) + _nbytes((c, c), BF16) + _nbytes((c, c), F32)
    return _pcall(
        body, name="pool_bwd", grid=(N_GROUPS, nseq),
        in_specs=[ANY,
                  pl.BlockSpec((None, s, c), lambda j, b: (0, b, j)),
                  pl.BlockSpec((s, c), lambda j, b: (b, j)),
                  pl.BlockSpec((None, c, c), lambda j, b: (j, 0, 0)),
                  pl.BlockSpec((1, c), lambda j, b: (0, j))],
        out_specs=[pl.BlockSpec((None, s, c), lambda j, b: (3, b, j)),
                   pl.BlockSpec((None, c, c), lambda j, b: (j, 0, 0)),
                   pl.BlockSpec((1, c), lambda j, b: (0, j))],
        out_shape=[SDS(dz.shape, BF16), SDS((N_GROUPS, c, c), F32), SDS((1, d), F32)],
        input_output_aliases={0: 0},
        sem=("parallel", "arbitrary"), blocks=blocks, temps=10 * _nbytes((s, c), F32), comm=comm,
    )(dz, dpq, z, pool_w, pool_scale)


def _dz_plane(zb):
    return jnp.where(zb < 4, (zb + 3) % 4, zb)


def _wgrad_in(h1, dz, nsh, comm=None):
    t, d = h1.shape
    ws = N_SPLITS * d // nsh
    kb = _tile(math.gcd(d, ws), 512, LANES)
    npl = d // kb
    nps = ws // kb
    tk = _tile(t, WGRAD_TOKENS_WIDE, SUBLANES_BF16)
    nk = t // tk

    def body(a_ref, b_ref, o_ref, *acc):
        part = _dot_ta(a_ref[...], b_ref[...])
        if nk == 1:
            o_ref[...] = part.astype(BF16)
        else:
            _acc_over(pl.program_id(1), nk, part, acc[0], o_ref)

    blocks = _nbytes((tk, d), BF16) + _nbytes((tk, kb), BF16) + _nbytes((d, kb), BF16)
    return _pcall(
        body, name="wgrad_in", grid=(N_SPLITS * npl, nk),
        in_specs=[pl.BlockSpec((tk, d), lambda cb, k: (k, 0), pipeline_mode=pl.Buffered(1) if nk == 1 else None),
                  pl.BlockSpec((None, tk, kb), lambda cb, k: (_dz_plane(cb // npl), k, cb % npl))],
        out_specs=[pl.BlockSpec((None, d, kb), lambda cb, k: (cb // nps, 0, cb % nps))],
        out_shape=[SDS((nsh, d, ws), BF16)],
        scratch_shapes=[] if nk == 1 else [pltpu.VMEM((d, kb), F32)],
        sem=("parallel", "arbitrary"), blocks=blocks, temps=2 * _nbytes((d, kb), F32), comm=comm,
    )(h1, dz)


def _mixer_bwd_dx(dz, w_in, x, dx1, g_mix, comm=None):
    npln, t, d = dz.shape
    nsh, _, ws = w_in.shape
    tm = _tile(t, 256, SUBLANES_BF16)
    kb = _tile(math.gcd(d, ws), 512, LANES)
    npl = d // kb
    nps = ws // kb

    def body(dz_ref, w_ref, x_ref, dx1_ref, g_ref, dx_ref, gg_ref):
        @pl.when(pl.program_id(0) == 0)
        def _():
            gg_ref[...] = jnp.zeros_like(gg_ref)

        dh = None
        for cb in range(npln * npl):
            zb = cb // npl
            plane = (zb + 3) % 4 if zb < 4 else zb
            part = _dot_tb(dz_ref[plane, :, (cb % npl) * kb:(cb % npl + 1) * kb],
                           w_ref[cb // nps, :, (cb % nps) * kb:(cb % nps + 1) * kb])
            dh = part if dh is None else dh + part
        xh, inv = _rms_fwd(x_ref[...])
        gg_ref[...] += jnp.sum(dh * xh, axis=0, keepdims=True)
        dx_ref[...] = dx1_ref[...] + _rms_bwd(dh, xh, inv, g_ref[...])

    blocks = _nbytes((npln, tm, d), BF16) + 3 * _nbytes((tm, d), F32)
    return _pcall(
        body, name="mixer_bwd_dx", grid=(t // tm,),
        in_specs=[pl.BlockSpec((npln, tm, d), lambda i: (0, i, 0)),
                  pl.BlockSpec((nsh, d, ws), lambda i: (0, 0, 0), pipeline_mode=pl.Buffered(1)),
                  pl.BlockSpec((tm, d), lambda i: (i, 0)),
                  pl.BlockSpec((tm, d), lambda i: (i, 0)),
                  pl.BlockSpec((1, d), lambda i: (0, 0))],
        out_specs=[pl.BlockSpec((tm, d), lambda i: (i, 0)),
                   pl.BlockSpec((1, d), lambda i: (0, 0))],
        out_shape=[SDS((t, d), F32), SDS((1, d), F32)],
        sem=("arbitrary",), blocks=blocks, temps=_nbytes(w_in.shape, BF16) + 8 * _nbytes((tm, d), F32), comm=comm,
    )(dz, w_in, x, dx1, g_mix)


N_BIG = 5
SHARD_MAJOR = (0, 2)
ROWS_DIM1 = (1, 4)


def _ds(start, size, align):
    if isinstance(start, int):
        return pl.ds(start, size)
    return pl.ds(pl.multiple_of(start, align), size)


def _piece(a, ref, k, h):
    if a in SHARD_MAJOR:
        r = ref.shape[1] // 2
        return ref.at[k, _ds(h * r, r, SUBLANES_BF16), :]
    if a in ROWS_DIM1:
        r = ref.shape[1] // 8
        return ref.at[:, _ds((2 * k + h) * r, r, SUBLANES_BF16), :]
    r = ref.shape[0] // 8
    return ref.at[_ds((2 * k + h) * r, r, SUBLANES_BF16), :]


def _half(a, ref, h):
    if a in ROWS_DIM1:
        r = ref.shape[1] // 2
        return ref.at[:, _ds(h * r, r, SUBLANES_BF16), :]
    r = ref.shape[0] // 2
    return ref.at[_ds(h * r, r, SUBLANES_BF16), :]


def _piece_shape(a, full_shape):
    if a in SHARD_MAJOR:
        return (full_shape[1] // 2, full_shape[2])
    if a in ROWS_DIM1:
        return (full_shape[0], full_shape[1] // 8, full_shape[2])
    return (full_shape[0] // 8, full_shape[1])


def _shard_shape(a, full_shape):
    if a in SHARD_MAJOR:
        return (full_shape[1], full_shape[2])
    if a in ROWS_DIM1:
        return (full_shape[0], full_shape[1] // 4, full_shape[2])
    return (full_shape[0] // 4, full_shape[1])


def _rows_axis(a):
    return 1 if a in ROWS_DIM1 else 0


def _piece_block(a, full_shape):
    ps = _piece_shape(a, full_shape)
    if a in SHARD_MAJOR:
        return (None,) + ps, lambda k, c: (k, c, 0)
    if a in ROWS_DIM1:
        return ps, lambda k, c: (0, 2 * k + c, 0)
    return ps, lambda k, c: (2 * k + c, 0)


def _coords():
    return lax.axis_index("x"), lax.axis_index("y"), lax.axis_index("c")


def _peer_chips(x, y):
    return [(1 - x, y), (x, 1 - y), (1 - x, 1 - y)]


def _remote(src, dst, ssem, rsem, dev):
    return pltpu.make_async_remote_copy(src_ref=src, dst_ref=dst, send_sem=ssem, recv_sem=rsem,
                                        device_id=dev, device_id_type=MESH)


def _dma_sems(*counts):
    return [pltpu.SemaphoreType.DMA((n,)) for n in counts]


def _symmetric(ins, out_shapes, sems, copies, peers, aliases=None):
    def start(cins, couts, csems):
        for cp in copies(cins, couts, csems):
            cp.start()

    def finish(cins, couts, csems):
        for cp in copies(cins, couts, csems):
            cp.wait()

    return _Comm(ins, out_shapes, sems, start, finish, peers, aliases)


def _rows_part(a, ref, part):
    if part is None:
        return ref
    p, q, n = part
    ax = _rows_axis(a)
    r = ref.shape[ax] // n
    return ref.at[tuple(pl.ds(p * r, (q - p) * r) if d == ax else slice(None) for d in range(len(ref.shape)))]


def _merge(comms):
    ins, outs, sems, aliases, spans = [], [], [], {}, []
    for cm in comms:
        spans.append((len(ins), len(outs), len(sems)))
        for i, o in cm.aliases.items():
            aliases[len(ins) + i] = len(outs) + o
        ins += cm.ins
        outs += cm.out_shapes
        sems += cm.sems

    def each(fn_name):
        def run(cins, couts, csems):
            for cm, (i0, o0, s0) in zip(comms, spans):
                fn = getattr(cm, fn_name)
                if fn is not None:
                    fn(cins[i0:i0 + len(cm.ins)], couts[o0:o0 + len(cm.out_shapes)], csems[s0:s0 + len(cm.sems)])
        return run

    return _Comm(ins, outs, sems, each("start"), each("finish"), frozenset().union(*[cm.peers for cm in comms]),
                 aliases, mid=each("mid") if any(cm.mid is not None for cm in comms) else None)


def _gather_comm(arrs, locs, full_shapes, part=None, into=None):
    n = len(arrs)

    def own(cins, couts, csems):
        x, y, c = _coords()
        j = 2 * x + y
        return [_remote(_rows_part(a, _half(a, cins[q], h), part), _rows_part(a, _piece(a, couts[q], j, h), part),
                        csems[0].at[2 * q + h], csems[1].at[2 * q + h], (x, y, 1 - c))
                for q, a in enumerate(arrs) for h in range(2)]

    def sends(cins, couts, csems):
        x, y, c = _coords()
        j = 2 * x + y
        return [_remote(_rows_part(a, _half(a, cins[q], c), part), _rows_part(a, _piece(a, couts[q], j, c), part),
                        csems[2].at[3 * q + i], csems[3].at[3 * q + i], (px, py, c))
                for q, a in enumerate(arrs) for i, (px, py) in enumerate(_peer_chips(x, y))]

    def forwards(couts, csems, half_of):
        x, y, c = _coords()
        out = []
        for q, a in enumerate(arrs):
            for i, (px, py) in enumerate(_peer_chips(x, y)):
                landed = _rows_part(a, _piece(a, couts[q], 2 * px + py, half_of(c)), part)
                out.append(_remote(landed, landed, csems[4].at[3 * q + i], csems[5].at[3 * q + i], (x, y, 1 - c)))
        return out

    def start(cins, couts, csems):
        for cp in sends(cins, couts, csems) + own(cins, couts, csems):
            cp.start()

    def finish(cins, couts, csems):
        fw = forwards(couts, csems, lambda c: c)
        for cp, f in zip(sends(cins, couts, csems), fw):
            cp.wait_recv()
            f.start()
        for f in forwards(couts, csems, lambda c: 1 - c):
            f.wait_recv()
        for cp in sends(cins, couts, csems) + fw:
            cp.wait_send()
        for cp in own(cins, couts, csems):
            cp.wait()

    ins = [locs[a] for a in arrs] + ([into[a] for a in arrs] if into else [])
    return _Comm(ins, [SDS(full_shapes[a], BF16) for a in arrs],
                 _dma_sems(2 * n, 2 * n, 3 * n, 3 * n, 3 * n, 3 * n), start, finish, CHIPS + (SIBLING,),
                 aliases={n + q: q for q in range(n)} if into else None)


def _ring_gather_comm(arrs, locs, full_shapes):
    n = len(arrs)

    def own(cins, couts, csems):
        x, y, c = _coords()
        j = 2 * x + y
        return [_remote(_half(a, cins[q], h), _piece(a, couts[q], j, h), csems[0].at[2 * q + h],
                        csems[1].at[2 * q + h], (x, y, 1 - c)) for q, a in enumerate(arrs) for h in range(2)]

    def sends(cins, couts, csems):
        x, y, c = _coords()
        j = 2 * x + y
        return [_remote(_half(a, cins[q], c), _piece(a, couts[q], j, c), csems[2].at[2 * q + i],
                        csems[3].at[2 * q + i], (px, py, c))
                for q, a in enumerate(arrs) for i, (px, py) in enumerate(_peer_chips(x, y)[:2])]

    def relays(couts, csems):
        x, y, c = _coords()
        peers = _peer_chips(x, y)
        out = []
        for q, a in enumerate(arrs):
            for r, (src_p, dst_p) in enumerate(((0, 1), (1, 0))):
                sx, sy = peers[src_p]
                rows = _rows_part(a, _piece(a, couts[q], 2 * sx + sy, c), (r, r + 1, 2))
                out.append(_remote(rows, rows, csems[6].at[2 * q + r], csems[7].at[2 * q + r], (*peers[dst_p], c)))
        return out

    def forwards(couts, csems, half_of, which):
        x, y, c = _coords()
        out = []
        for q, a in enumerate(arrs):
            for i in which:
                px, py = _peer_chips(x, y)[i]
                landed = _piece(a, couts[q], 2 * px + py, half_of(c))
                out.append(_remote(landed, landed, csems[4].at[3 * q + i], csems[5].at[3 * q + i], (x, y, 1 - c)))
        return out

    def start(cins, couts, csems):
        for cp in sends(cins, couts, csems) + own(cins, couts, csems):
            cp.start()

    def mid(cins, couts, csems):
        for cp in sends(cins, couts, csems):
            cp.wait_recv()
        for cp in relays(couts, csems) + forwards(couts, csems, lambda c: c, (0, 1)):
            cp.start()

    def finish(cins, couts, csems):
        for cp in relays(couts, csems):
            cp.wait_recv()
        fw_diag = forwards(couts, csems, lambda c: c, (2,))
        for f in fw_diag:
            f.start()
        for f in forwards(couts, csems, lambda c: 1 - c, (0, 1, 2)):
            f.wait_recv()
        for cp in (sends(cins, couts, csems) + relays(couts, csems)
                   + forwards(couts, csems, lambda c: c, (0, 1)) + fw_diag):
            cp.wait_send()
        for cp in own(cins, couts, csems):
            cp.wait()

    return _Comm([locs[a] for a in arrs], [SDS(full_shapes[a], BF16) for a in arrs],
                 _dma_sems(2 * n, 2 * n, 2 * n, 2 * n, 3 * n, 3 * n, 2 * n, 2 * n), start, finish,
                 CHIPS + (SIBLING,), mid=mid)


def _halves_comm(arrs, gbs):
    n = len(arrs)

    def copies(cins, couts, csems):
        x, y, c = _coords()
        return [_remote(_piece(a, cins[q], k, 1 - c), couts[q].at[k], csems[0].at[4 * q + k], csems[1].at[4 * q + k],
                        (x, y, 1 - c)) for q, a in enumerate(arrs) for k in range(4)]

    return _symmetric([gbs[a] for a in arrs], [SDS((4,) + _piece_shape(a, gbs[a].shape), BF16) for a in arrs],
                      _dma_sems(4 * n, 4 * n), copies, [SIBLING])


def _chips_comm(arrs, ps, part=None, into=None):
    n = len(arrs)

    def copies(cins, couts, csems):
        x, y, c = _coords()
        return [_remote(_rows_part(a, cins[q].at[2 * px + py], part), _rows_part(a, couts[q].at[i], part),
                        csems[0].at[3 * q + i], csems[1].at[3 * q + i], (px, py, c))
                for q, a in enumerate(arrs) for i, (px, py) in enumerate(_peer_chips(x, y))]

    ins = [ps[a] for a in arrs] + ([into[a] for a in arrs] if into else [])
    return _symmetric(ins, [SDS((3,) + ps[a].shape[1:], BF16) for a in arrs], _dma_sems(3 * n, 3 * n), copies, CHIPS,
                      aliases={n + q: q for q in range(n)} if into else None)


def _result_comm(arrs, gs):
    n = len(arrs)

    def copies(cins, couts, csems):
        x, y, c = _coords()
        return [_remote(_half(a, cins[q], c), _half(a, couts[q], c), csems[0].at[q], csems[1].at[q], (x, y, 1 - c))
                for q, a in enumerate(arrs)]

    return _symmetric([gs[a] for a in arrs], [SDS(gs[a].shape, F32) for a in arrs], _dma_sems(n, n), copies,
                      [SIBLING], aliases={q: q for q in range(n)})


def _add_halves(arrs, gbs, lands, c_arr, name):
    n = len(arrs)

    def body(c_ref, *refs):
        del c_ref
        for q in range(n):
            refs[2 * n + q][...] = (refs[q][...].astype(F32) + refs[n + q][...].astype(F32)).astype(BF16)

    g_specs, l_specs, o_specs, blocks = [], [], [], 0
    for a in arrs:
        bs, imap = _piece_block(a, gbs[a].shape)
        ps = _piece_shape(a, gbs[a].shape)
        g_specs.append(pl.BlockSpec(bs, lambda k, c_ref, imap=imap: imap(k, c_ref[0])))
        nd = len(ps)
        l_specs.append(pl.BlockSpec((None,) + ps, lambda k, c_ref, nd=nd: (k,) + (0,) * nd))
        o_specs.append(pl.BlockSpec((None,) + ps, lambda k, c_ref, nd=nd: (k,) + (0,) * nd))
        blocks += 3 * _nbytes(ps, BF16)
    return list(pl.pallas_call(
        body, name=name,
        grid_spec=pltpu.PrefetchScalarGridSpec(
            num_scalar_prefetch=1, grid=(4,), in_specs=g_specs + l_specs, out_specs=o_specs),
        out_shape=[SDS((4,) + _piece_shape(a, gbs[a].shape), BF16) for a in arrs],
        compiler_params=_params(("parallel",), blocks, blocks),
    )(c_arr, *[gbs[a] for a in arrs], *lands))


def _sum_chips(a, p, land, shard_shape, jc_arr, name):
    ps = land.shape[1:]
    ax = _rows_axis(a)
    rows = ps[ax]
    nsub = 2 if rows % (2 * SUBLANES_BF16) == 0 else 1
    bs = tuple(r // nsub if q == ax else r for q, r in enumerate(ps))
    nd = len(ps)

    def at_rows(v):
        return tuple(v if q == ax else 0 for q in range(nd))

    def body(jc_ref, p_ref, l_ref, o_ref):
        del jc_ref
        acc = p_ref[...].astype(F32) + l_ref[0].astype(F32)
        acc = acc + l_ref[1].astype(F32)
        o_ref[...] = acc + l_ref[2].astype(F32)

    blocks = 4 * _nbytes(bs, BF16) + _nbytes(bs, F32)
    return pl.pallas_call(
        body, name=name,
        grid_spec=pltpu.PrefetchScalarGridSpec(
            num_scalar_prefetch=1, grid=(nsub,),
            in_specs=[pl.BlockSpec((None,) + bs, lambda s, jc: (jc[0],) + at_rows(s)),
                      pl.BlockSpec((3,) + bs, lambda s, jc: (0,) + at_rows(s))],
            out_specs=pl.BlockSpec(bs, lambda s, jc: at_rows(jc[1] * nsub + s))),
        out_shape=SDS(shard_shape, F32),
        compiler_params=_params(("parallel",), blocks, 2 * _nbytes(bs, F32)),
    )(jc_arr, p, land)


def _small_comm(v):
    rows = v.shape[0]

    def copies(cins, couts, csems):
        x, y, c = _coords()
        me = 4 * x + 2 * y + c
        out = [pltpu.make_async_copy(cins[0], couts[0].at[me], csems[0].at[0])]
        for dlt in range(1, 8):
            px = 1 - x if (dlt >> 2) & 1 else x
            py = 1 - y if (dlt >> 1) & 1 else y
            pc = 1 - c if dlt & 1 else c
            out.append(_remote(cins[0], couts[0].at[me], csems[1].at[dlt - 1], csems[2].at[dlt - 1], (px, py, pc)))
        return out

    return _symmetric([v], [SDS((8, rows, LANES), F32)], _dma_sems(1, 7, 7), copies, EVERYONE)


def _sum8(slots, name):
    def body(s_ref, o_ref):
        acc = s_ref[0]
        for i in range(1, 8):
            acc = acc + s_ref[i]
        o_ref[...] = acc

    return pl.pallas_call(
        body, name=name,
        in_specs=[pl.BlockSpec(memory_space=pltpu.VMEM)], out_specs=pl.BlockSpec(memory_space=pltpu.VMEM),
        out_shape=SDS(slots.shape[1:], F32),
    )(slots)


def _adamw(w, g, m, v, name, g_plane=None):
    rows, cols = w.shape
    tr = _tile(rows, max(SUBLANES_F32, (256 * 1024 // cols) // SUBLANES_F32 * SUBLANES_F32), SUBLANES_F32)

    def body(w_ref, g_ref, m_ref, v_ref, go_ref, d_ref, mo_ref, vo_ref):
        gr = g_ref[...]
        mn = ADAM_B1 * m_ref[...] + (1.0 - ADAM_B1) * gr
        vn = ADAM_B2 * v_ref[...] + (1.0 - ADAM_B2) * (gr * gr)
        m_hat = mn / (1.0 - ADAM_B1 ** ADAM_STEP)
        v_hat = vn / (1.0 - ADAM_B2 ** ADAM_STEP)
        d_ref[...] = -ADAM_LR * (m_hat / (jnp.sqrt(v_hat) + ADAM_EPS) + ADAM_WD * w_ref[...])
        go_ref[...] = gr
        mo_ref[...] = mn
        vo_ref[...] = vn

    spec = pl.BlockSpec((tr, cols), lambda i: (i, 0))
    g_spec = spec if g_plane is None else pl.BlockSpec((None, tr, cols), lambda i: (g_plane, i, 0))
    return pl.pallas_call(
        body, name=name, grid=(rows // tr,),
        in_specs=[spec, g_spec, spec, spec], out_specs=[spec, spec, spec, spec],
        out_shape=[SDS((rows, cols), F32)] * 4,
        compiler_params=_params(("parallel",), 8 * _nbytes((tr, cols), F32), 4 * _nbytes((tr, cols), F32)),
    )(w, g, m, v)


def _pack(parts):
    rows = []
    for p in parts:
        r = p.reshape(-1, LANES)
        pad = (-r.shape[0]) % SUBLANES_F32
        if pad:
            r = jnp.pad(r, ((0, pad), (0, 0)))
        rows.append(r)
    return jnp.concatenate(rows, axis=0)


def _unpack(packed, shapes):
    out, at = [], 0
    for s in shapes:
        n = 1
        for q in s:
            n *= q
        r = n // LANES
        out.append(packed[at:at + r].reshape(s))
        at += r + (-r) % SUBLANES_F32
    return out


def kernel(x, norm_mix, w_in, pool_w, pool_scale, w_pool_proj, conv_w, w_conv_out, w_o, norm_ffn, w_up, ffn_conv_w, ffn_conv_b, w_down, norm_final, loss_target, m_norm_mix, m_w_in, m_pool_w, m_pool_scale, m_w_pool_proj, m_conv_w, m_w_conv_out, m_w_o, m_norm_ffn, m_w_up, m_ffn_conv_w, m_ffn_conv_b, m_w_down, m_norm_final, v_norm_mix, v_w_in, v_pool_w, v_pool_scale, v_w_pool_proj, v_conv_w, v_w_conv_out, v_w_o, v_norm_ffn, v_w_up, v_ffn_conv_w, v_ffn_conv_b, v_w_down, v_norm_final):
    nseq, seq, d = x.shape
    t = nseq * seq
    f = w_down.shape[1] * 4
    c = d // N_GROUPS
    xy = lax.axis_index("x") * 2 + lax.axis_index("y")
    c_arr = lax.axis_index("c").astype(jnp.int32).reshape(1)
    jc_arr = jnp.stack([xy, lax.axis_index("c")]).astype(jnp.int32)
    nsh = 4
    zero = jnp.zeros((), jnp.int32)

    locs = [w_in[0].astype(BF16),
            jnp.stack([w_pool_proj[0], w_conv_out[0], w_o[0]]).astype(BF16),
            w_up[0].astype(BF16), w_down[0].astype(BF16), pool_w[0].astype(BF16)]
    full_shapes = [(nsh, d, N_SPLITS * d // nsh), (3, d, d), (nsh, d, 2 * f // nsh), (f, d), (N_GROUPS, c, c)]

    cw_pad = lax.dynamic_update_slice(jnp.zeros((3, d), F32), conv_w[0], (zero, xy * (d // 4)))
    fw_pad = lax.dynamic_update_slice(jnp.zeros((3, 2 * f), F32), ffn_conv_w[0], (zero, xy * (f // 2)))
    small_w = _pack([cw_pad, fw_pad]) * 0.5

    x2d = x.reshape(t, d)
    tgt = loss_target.reshape(t, d)
    ax, ay = lax.axis_index("x"), lax.axis_index("y")
    order = jnp.stack([xy, 2 * (1 - ax) + ay, 2 * ax + 1 - ay, 2 * (1 - ax) + 1 - ay]).astype(jnp.int32)
    (z, h1, w_in_f), (pool_w_f, w3_f, slots_w) = _fwd_in(
        x2d, norm_mix, locs[0], order,
        _merge([_gather_comm([4], locs, full_shapes), _gather_comm([1], locs, full_shapes, part=(0, 1, 2)),
                _small_comm(small_w)]))
    conv_w_f, ffn_cw_f = _unpack(_sum8(slots_w, "sum8_weights"), [(3, d), (3, 2 * f)])
    ffn_cw_p = ffn_cw_f.reshape(3, 2, f).transpose(1, 0, 2)
    ffn_cb_p = ffn_conv_b.reshape(2, 1, f)
    (lhs3,), (w3_f,) = _mixer_mid_fwd(z, pool_w_f, pool_scale, conv_w_f, nseq,
                                      _gather_comm([1], locs, full_shapes, part=(1, 2, 2), into={1: w3_f}))
    (lhs3, ypc, x1, h2), (w_up_f,) = _mixer_out(lhs3, z, x2d, w3_f, norm_ffn,
                                                _ring_gather_comm([2], locs, full_shapes))
    (u0,), (w_down_f,) = _ffn_up(h2, w_up_f, f, _gather_comm([3], locs, full_shapes))
    act, ua = _ffn_mid_fwd(u0, ffn_cw_p, ffn_cb_p, nseq)
    dx2, dx2b, loss11, g_norm_final = _ffn_down_loss(act, w_down_f, x1, tgt, norm_final.reshape(1, d))

    gbs, lands, ps, lands2, rs = {}, {}, {}, {}, {}
    tn_up = _tile(2 * f // nsh, 1408, LANES)
    npp = f // tn_up

    def add(arrs, name):
        for a, p in zip(arrs, _add_halves(arrs, gbs, [lands[a] for a in arrs], c_arr, name)):
            ps[a] = p

    def summed(a):
        rs[a] = _sum_chips(a, ps[a], lands2[a], _shard_shape(a, full_shapes[a]), jc_arr, "sum_chips_%d" % a)

    (gbs[3],), _ = _wgrad(act, dx2b, "wgrad_down", tr=tn_up, tn=d)
    (da,), (lands[3],) = _ffn_bwd_da(dx2b, w_down_f, _halves_comm([3], gbs))
    add([3], "add_halves_down")
    (du0, g_ffn_cw_p, g_ffn_cb_p), (lands2[3],) = _ffn_mid_bwd(da, u0, ua, ffn_cw_p, nseq, _chips_comm([3], ps))
    summed(3)
    (gbs[2],), (rs[3],) = _wgrad(h2, du0, "wgrad_up", tr=d, tn=tn_up, b_plane_of=lambda n: (n // npp, n % npp),
                                 out_shards=nsh, comm=_result_comm([3], rs))
    (dx1, rhs3, g_norm_ffn), (lands[2],) = _ffn_bwd_dx1(du0, w_up_f, x1, dx2, norm_ffn, 3, _halves_comm([2], gbs))
    add([2], "add_halves_up")
    (rhs3, dz, dpq), (lands2[2],) = _mixer_bwd(rhs3, z, ypc, w3_f, _chips_comm([2], ps, part=(0, 1, 2)))
    (gbs[1],), (lands2[2],) = _wgrad3(lhs3, rhs3, _chips_comm([2], ps, part=(1, 2, 2), into=lands2))
    summed(2)
    (dz, g_conv_w), (lands[1], rs[2]) = _conv_bwd(dz, dpq, z, conv_w_f, nseq,
                                                  _merge([_halves_comm([1], gbs), _result_comm([2], rs)]))
    add([1], "add_halves_sq3")
    (dz, g_pool_w, g_pool_scale), _ = _pool_bwd_call(dz, dpq, z, pool_w_f, pool_scale, nseq)
    gbs[4] = g_pool_w.astype(BF16)
    (gbs[0],), (lands2[1],) = _wgrad_in(h1, dz, nsh, _chips_comm([1], ps))
    summed(1)
    lands[0], lands[4] = _run_comm(_halves_comm([0, 4], gbs), "exchange_halves_in")
    add([0, 4], "add_halves_in")
    g_ffn_cw = g_ffn_cw_p.transpose(1, 0, 2).reshape(3, 2 * f)
    small_a = _pack([g_pool_scale, g_norm_ffn, g_ffn_cb_p.reshape(1, 2 * f), g_norm_final.reshape(d), g_conv_w,
                     g_ffn_cw, jnp.pad(loss11, ((0, SUBLANES_F32 - 1), (0, LANES - 1)))])
    (grad_x, g_norm_mix), (lands2[0], lands2[4], rs[1], slots_a) = _mixer_bwd_dx(
        dz, w_in_f, x2d, dx1, norm_mix,
        _merge([_chips_comm([0, 4], ps), _result_comm([1], rs), _small_comm(small_a)]))
    summed(0)
    summed(4)
    rs[0], rs[4], slots_b = _run_comm(_merge([_result_comm([0, 4], rs), _small_comm(_pack([g_norm_mix]))]),
                                      "exchange_result_in")
    shapes_a = [(1, d), (1, d), (1, 2 * f), (d,), (3, d), (3, 2 * f), (SUBLANES_F32, LANES)]
    gs_pool_scale, gs_norm_ffn, gs_ffn_cb, gs_norm_final, gs_conv_w, gs_ffn_cw, loss_blk = _unpack(
        _sum8(slots_a, "sum8_grads"), shapes_a)
    (gs_norm_mix,) = _unpack(_sum8(slots_b, "sum8_norm_mix"), [(1, d)])
    gs_conv_w = lax.dynamic_slice(gs_conv_w, (zero, xy * (d // 4)), (3, d // 4))
    gs_ffn_cw = lax.dynamic_slice(gs_ffn_cw, (zero, xy * (f // 2)), (3, f // 2))

    def upd(w, g, m, v, name, g_plane=None):
        shape = w.shape
        rows = 1
        for q in shape[:-1]:
            rows *= q
        g2 = g if g_plane is not None else g.reshape(rows, shape[-1])
        outs = _adamw(w.reshape(rows, shape[-1]), g2, m.reshape(rows, shape[-1]), v.reshape(rows, shape[-1]),
                      name, g_plane)
        return [o.reshape(shape) for o in outs]

    res = {
        "w_in": upd(w_in, rs[0], m_w_in, v_w_in, "adamw_w_in"),
        "pool_w": upd(pool_w, rs[4], m_pool_w, v_pool_w, "adamw_pool_w"),
        "w_pool_proj": upd(w_pool_proj, rs[1], m_w_pool_proj, v_w_pool_proj, "adamw_w_pool_proj", 0),
        "w_conv_out": upd(w_conv_out, rs[1], m_w_conv_out, v_w_conv_out, "adamw_w_conv_out", 1),
        "w_o": upd(w_o, rs[1], m_w_o, v_w_o, "adamw_w_o", 2),
        "w_up": upd(w_up, rs[2], m_w_up, v_w_up, "adamw_w_up"),
        "w_down": upd(w_down, rs[3], m_w_down, v_w_down, "adamw_w_down"),
    }

    small_names = ["norm_mix", "pool_scale", "norm_ffn", "ffn_conv_b", "norm_final", "conv_w", "ffn_conv_w"]
    small_ws = [norm_mix, pool_scale, norm_ffn, ffn_conv_b, norm_final, conv_w, ffn_conv_w]
    small_ms = [m_norm_mix, m_pool_scale, m_norm_ffn, m_ffn_conv_b, m_norm_final, m_conv_w, m_ffn_conv_w]
    small_vs = [v_norm_mix, v_pool_scale, v_norm_ffn, v_ffn_conv_b, v_norm_final, v_conv_w, v_ffn_conv_w]
    small_gs = [gs_norm_mix, gs_pool_scale, gs_norm_ffn, gs_ffn_cb, gs_norm_final, gs_conv_w, gs_ffn_cw]
    _, sd, sm, sv = _adamw(_pack(small_ws), _pack(small_gs), _pack(small_ms), _pack(small_vs), "adamw_small")
    shapes = [w.shape for w in small_ws]
    sd, sm, sv = _unpack(sd, shapes), _unpack(sm, shapes), _unpack(sv, shapes)
    for i, nm in enumerate(small_names):
        res[nm] = [small_gs[i].reshape(shapes[i]), sd[i], sm[i], sv[i]]

    order = ["norm_mix", "w_in", "pool_w", "pool_scale", "w_pool_proj", "conv_w", "w_conv_out", "w_o", "norm_ffn",
             "w_up", "ffn_conv_w", "ffn_conv_b", "w_down", "norm_final"]
    return (loss_blk[0, 0], grad_x.reshape(x.shape), *[res[n][0] for n in order], *[res[n][1] for n in order],
            *[res[n][2] for n in order], *[res[n][3] for n in order])
```

```python
import math

import jax
import jax.numpy as jnp
from jax import lax
from jax.experimental import pallas as pl
from jax.experimental.pallas import tpu as pltpu

F32 = jnp.float32
BF16 = jnp.bfloat16
SDS = jax.ShapeDtypeStruct
MESH = pl.DeviceIdType.MESH

RMS_EPS = 1e-6
POOL_WINDOWS = (2, 4, 8, 16)
N_GROUPS = len(POOL_WINDOWS)
N_SPLITS = 6

ADAM_LR = 0.001
ADAM_B1 = 0.9
ADAM_B2 = 0.999
ADAM_EPS = 1e-08
ADAM_WD = 0.01
ADAM_STEP = 10

LANES = 128
SUBLANES_F32 = 8
SUBLANES_BF16 = 16
VMEM_BYTES = 64 * 1024 * 1024
VMEM_CAP = VMEM_BYTES - 8 * 1024 * 1024
VMEM_FLOOR = 16 * 1024 * 1024

ANY = pl.BlockSpec(memory_space=pl.ANY)


def _tile(dim, pref, align):
    if dim <= pref:
        return dim
    t = (pref // align) * align
    while t >= align:
        if dim % t == 0:
            return t
        t -= align
    return dim


def _nbytes(shape, dtype):
    n = 1
    for s in shape:
        n *= s
    return n * jnp.dtype(dtype).itemsize


def _params(sem, block_bytes, temp_bytes=0, collective_id=None):
    need = 2 * block_bytes + temp_bytes + 4 * 1024 * 1024
    return pltpu.CompilerParams(dimension_semantics=sem, collective_id=collective_id,
                                vmem_limit_bytes=int(min(max(need, VMEM_FLOOR), VMEM_CAP)))


SIBLING = (0, 0, 1)
CHIPS = ((1, 0, 0), (0, 1, 0), (1, 1, 0))
EVERYONE = tuple((a, b, c) for a in range(2) for b in range(2) for c in range(2) if a + b + c)
PEER_SETS = (frozenset([SIBLING]), frozenset(CHIPS), frozenset(CHIPS + (SIBLING,)), frozenset(EVERYONE))
MID_AT = 0.75


def _collective_id(peers):
    return PEER_SETS.index(frozenset(peers))


def _handshake(peers):
    x, y, c = lax.axis_index("x"), lax.axis_index("y"), lax.axis_index("c")
    bar = pltpu.get_barrier_semaphore()
    for fx, fy, fc in sorted(peers):
        dev = (1 - x if fx else x, 1 - y if fy else y, 1 - c if fc else c)
        pl.semaphore_signal(bar, inc=1, device_id=dev, device_id_type=MESH)
    pl.semaphore_wait(bar, len(peers))


class _Comm:
    def __init__(self, ins, out_shapes, sems, start, finish, peers, aliases=None, mid=None):
        self.ins = list(ins)
        self.out_shapes = list(out_shapes)
        self.sems = list(sems)
        self.start = start
        self.finish = finish
        self.mid = mid
        self.peers = frozenset(peers)
        self.aliases = dict(aliases or {})


def _pcall(body, *, name, grid, in_specs, out_specs, out_shape, sem, blocks, temps=0, scratch_shapes=(),
           input_output_aliases=None, comm=None):
    in_specs = list(in_specs)
    out_specs = list(out_specs)
    out_shape = list(out_shape)
    scratch_shapes = list(scratch_shapes)
    aliases = dict(input_output_aliases or {})
    n_in, n_out, n_scr = len(in_specs), len(out_shape), len(scratch_shapes)
    if comm is None:
        call = pl.pallas_call(
            body, name=name, grid=grid, in_specs=in_specs, out_specs=out_specs, out_shape=out_shape,
            scratch_shapes=scratch_shapes, input_output_aliases=aliases,
            compiler_params=_params(sem, blocks, temps))
        return lambda *args: (list(call(*args)), [])

    nci, nco = len(comm.ins), len(comm.out_shapes)
    n_steps = 1
    for g in grid:
        n_steps *= g

    def hosted(*refs):
        ins = refs[:n_in]
        cins = refs[n_in:n_in + nci]
        outs = refs[n_in + nci:n_in + nci + n_out]
        couts = refs[n_in + nci + n_out:n_in + nci + n_out + nco]
        scr = refs[n_in + nci + n_out + nco:n_in + nci + n_out + nco + n_scr]
        csems = refs[n_in + nci + n_out + nco + n_scr:]
        first = None
        last = None
        step = 0
        for q, g in enumerate(grid):
            pid = pl.program_id(q)
            first = (pid == 0) if first is None else first & (pid == 0)
            last = (pid == g - 1) if last is None else last & (pid == g - 1)
            step = step * g + pid

        @pl.when(first)
        def _():
            _handshake(comm.peers)
            comm.start(cins, couts, csems)

        if comm.mid is not None:
            @pl.when(step == int(MID_AT * n_steps))
            def _():
                comm.mid(cins, couts, csems)

        body(*ins, *outs, *scr)

        @pl.when(last)
        def _():
            comm.finish(cins, couts, csems)

    for i, o in comm.aliases.items():
        aliases[n_in + i] = n_out + o
    call = pl.pallas_call(
        hosted, name=name, grid=grid, in_specs=in_specs + [ANY] * nci, out_specs=out_specs + [ANY] * nco,
        out_shape=out_shape + comm.out_shapes, scratch_shapes=scratch_shapes + comm.sems,
        input_output_aliases=aliases,
        compiler_params=_params(("arbitrary",) * len(grid), blocks, temps, _collective_id(comm.peers)))

    def run(*args):
        res = call(*args, *comm.ins)
        return list(res[:n_out]), list(res[n_out:])

    return run


def _run_comm(comm, name):
    def body(*refs):
        nci, nco = len(comm.ins), len(comm.out_shapes)
        cins, couts, csems = refs[:nci], refs[nci:nci + nco], refs[nci + nco:]
        _handshake(comm.peers)
        comm.start(cins, couts, csems)
        if comm.mid is not None:
            comm.mid(cins, couts, csems)
        comm.finish(cins, couts, csems)

    return list(pl.pallas_call(
        body, name=name, in_specs=[ANY] * len(comm.ins), out_specs=[ANY] * len(comm.out_shapes),
        out_shape=comm.out_shapes, scratch_shapes=comm.sems, input_output_aliases=comm.aliases,
        compiler_params=pltpu.CompilerParams(collective_id=_collective_id(comm.peers)),
    )(*comm.ins))


def _dot(a, b):
    return jnp.dot(a, b, preferred_element_type=F32)


def _dot_tb(a, b):
    return lax.dot_general(a, b, (((1,), (1,)), ((), ())), preferred_element_type=F32)


def _dot_ta(a, b):
    return lax.dot_general(a, b, (((0,), (0,)), ((), ())), preferred_element_type=F32)


def _rms_fwd(x):
    inv = lax.rsqrt(jnp.mean(x * x, axis=-1, keepdims=True) + RMS_EPS)
    return x * inv, inv


def _rms_bwd(dy, xhat, inv, g):
    gd = dy * g
    return inv * (gd - xhat * jnp.mean(gd * xhat, axis=-1, keepdims=True))


def _sigmoid(x):
    return 1.0 / (1.0 + jnp.exp(-x))


def _shift_down(x, k, row):
    return jnp.where(row >= k, pltpu.roll(x, k, 0), 0.0)


def _shift_up(x, k, row):
    s = x.shape[0]
    return jnp.where(row < s - k, pltpu.roll(x, s - k, 0), 0.0)


def _pool_fwd(u, win, row):
    s = u
    k = 1
    while k < win:
        s = s + _shift_down(s, k, row)
        k *= 2
    cnt = jnp.minimum(row + 1, win).astype(F32)
    return s / cnt - u


def _pool_bwd(dp, win, row):
    cnt = jnp.minimum(row + 1, win).astype(F32)
    s = dp / cnt
    k = 1
    while k < win:
        s = s + _shift_up(s, k, row)
        k *= 2
    return s - dp


def _acc_over(k, nk, part, acc, o_ref):
    @pl.when(k == 0)
    def _():
        acc[...] = part

    @pl.when(k > 0)
    def _():
        acc[...] += part

    @pl.when(k == nk - 1)
    def _():
        o_ref[...] = acc[...].astype(o_ref.dtype)


def _fwd_in(x, g, w_loc, order, comm):
    t, d = x.shape
    ws = w_loc.shape[1]
    nsh = order.shape[0]
    assert nsh == 4, "the shard walk below is written for the 2 x 2 chips of the mesh"
    tm = _tile(t, 1024, SUBLANES_BF16)
    ni = t // tm
    nci, nco = len(comm.ins), len(comm.out_shapes)
    all_peers = comm.peers | frozenset(CHIPS + (SIBLING,))

    def body(order_ref, x_ref, g_ref, loc_ref, *rest):
        del order_ref
        cins = rest[:nci]
        z_ref, h_ref, full_ref = rest[nci:nci + 3]
        couts = rest[nci + 3:nci + 3 + nco]
        (hs, wbuf, wsem, own_s, own_r, snd_s, snd_r, fwd_s, fwd_r, rly_s, rly_r) = rest[nci + 3 + nco:nci + 14 + nco]
        csems = rest[nci + 14 + nco:]
        j = pl.program_id(0)
        i = pl.program_id(1)
        x_, y_, c_ = _coords()
        own = 2 * x_ + y_
        sib = (x_, y_, 1 - c_)
        peers = _peer_chips(x_, y_)

        def sends():
            return [_remote(_half(0, loc_ref, c_), _piece(0, full_ref, own, c_), snd_s.at[p], snd_r.at[p], (px, py, c_))
                    for p, (px, py) in enumerate(peers[:2])]

        def relays():
            out = []
            for q, (src_p, dst_p) in enumerate(((0, 1), (1, 0))):
                sx, sy = peers[src_p]
                part = _rows_part(0, _piece(0, full_ref, 2 * sx + sy, c_), (q, q + 1, 2))
                out.append(_remote(part, part, rly_s.at[q], rly_r.at[q], (*peers[dst_p], c_)))
            return out

        def owns():
            return [_remote(_half(0, loc_ref, h), _piece(0, full_ref, own, h), own_s.at[h], own_r.at[h], sib)
                    for h in range(2)]

        def forward(p, half):
            px, py = peers[p]
            landed = _piece(0, full_ref, 2 * px + py, half)
            return _remote(landed, landed, fwd_s.at[p], fwd_r.at[p], sib)

        def load(src, slot):
            return pltpu.make_async_copy(src, wbuf.at[slot], wsem.at[slot])

        @pl.when((j == 0) & (i == 0))
        def _():
            _handshake(all_peers)
            for cp in sends() + owns():
                cp.start()
            load(loc_ref, 0).start()

        @pl.when(j == 0)
        def _():
            xh, _ = _rms_fwd(x_ref[...])
            h = (xh * g_ref[...]).astype(BF16)
            hs[pl.ds(pl.multiple_of(i * tm, tm), tm), :] = h
            h_ref[...] = h

        slot = j % 2

        @pl.when(i == 0)
        def _():
            load(loc_ref, slot).wait()

        z_ref[...] = _dot(hs[pl.ds(pl.multiple_of(i * tm, tm), tm), :], wbuf[slot]).astype(BF16)

        def load_shard(p, into):
            px, py = peers[p]
            forward(p, 1 - c_).wait_recv()
            load(full_ref.at[2 * px + py], into).start()

        @pl.when((j == 0) & (i == ni - 1))
        def _():
            for cp in sends():
                cp.wait_recv()
            for cp in relays() + [forward(0, c_), forward(1, c_)]:
                cp.start()
            load_shard(0, 1)
            comm.start(cins, couts, csems)

        @pl.when((j == 1) & (i == 0))
        def _():
            load_shard(1, 0)

        @pl.when((j == 2) & (i == max(ni - 2, 0)))
        def _():
            for cp in relays():
                cp.wait_recv()
            forward(2, c_).start()
            load_shard(2, 1)

        @pl.when((j == nsh - 1) & (i == ni - 1))
        def _():
            for cp in sends() + relays() + [forward(p, c_) for p in range(nsh - 1)]:
                cp.wait_send()
            for cp in owns():
                cp.wait()
            comm.finish(cins, couts, csems)

    last = ni - 1
    blocks = _nbytes((tm, d), F32) + _nbytes((tm, ws), BF16) + _nbytes((tm, d), BF16)
    scratch = _nbytes((t, d), BF16) + 2 * _nbytes((d, ws), BF16)
    res = pl.pallas_call(
        body, name="fwd_in",
        grid_spec=pltpu.PrefetchScalarGridSpec(
            num_scalar_prefetch=1, grid=(nsh, ni),
            in_specs=[pl.BlockSpec((tm, d), lambda j, i, o: (jnp.where(j == 0, i, last), 0)),
                      pl.BlockSpec((1, d), lambda j, i, o: (0, 0)), ANY] + [ANY] * nci,
            out_specs=[pl.BlockSpec((tm, ws), lambda j, i, o: (i, o[j])),
                       pl.BlockSpec((tm, d), lambda j, i, o: (jnp.where(j == 0, i, last), 0)), ANY] + [ANY] * nco,
            scratch_shapes=[pltpu.VMEM((t, d), BF16), pltpu.VMEM((2, d, ws), BF16)]
            + _dma_sems(2, 2, 2, 2, 2, nsh - 1, nsh - 1, 2, 2) + comm.sems),
        out_shape=[SDS((t, nsh * ws), BF16), SDS((t, d), BF16), SDS((nsh, d, ws), BF16)] + comm.out_shapes,
        input_output_aliases={4 + i: 3 + o for i, o in comm.aliases.items()},
        compiler_params=_params(("arbitrary", "arbitrary"), blocks, scratch + 3 * _nbytes((tm, d), F32),
                                _collective_id(all_peers)),
    )(order, x, g, w_loc, *comm.ins)
    return list(res[:3]), list(res[3:])


def _mixer_mid_fwd(z, pool_w, pool_scale, conv_w, nseq, comm=None):
    t = z.shape[0]
    d = pool_scale.shape[1]
    s = t // nseq
    c = d // N_GROUPS

    def body(zp, zb, zc, zv, pw, ps, cw, o):
        j = pl.program_id(1)
        row = lax.broadcasted_iota(jnp.int32, (s, c), 0)
        for gi, win in enumerate(POOL_WINDOWS):
            @pl.when(j == gi)
            def _(win=win):
                pooled = _pool_fwd(zp[...].astype(F32), win, row)
                o[0] = (_dot(pooled.astype(BF16), pw[...]) * ps[...]).astype(BF16)

        cv = zc[...].astype(F32) * zv[...].astype(F32)
        cc = (cw[pl.ds(2, 1), :] * cv + cw[pl.ds(1, 1), :] * _shift_down(cv, 1, row)
              + cw[pl.ds(0, 1), :] * _shift_down(cv, 2, row))
        o[1] = (zb[...].astype(F32) * cc).astype(BF16)

    blocks = 4 * _nbytes((s, c), BF16) + _nbytes((c, c), BF16) + _nbytes((2, s, c), BF16)
    return _pcall(
        body, name="mixer_mid_fwd", grid=(nseq, N_GROUPS),
        in_specs=[pl.BlockSpec((s, c), lambda b, j: (b, j)),
                  pl.BlockSpec((s, c), lambda b, j: (b, N_GROUPS + j)),
                  pl.BlockSpec((s, c), lambda b, j: (b, 2 * N_GROUPS + j)),
                  pl.BlockSpec((s, c), lambda b, j: (b, 3 * N_GROUPS + j)),
                  pl.BlockSpec((None, c, c), lambda b, j: (j, 0, 0)),
                  pl.BlockSpec((1, c), lambda b, j: (0, j)),
                  pl.BlockSpec((3, c), lambda b, j: (0, j))],
        out_specs=[pl.BlockSpec((2, s, c), lambda b, j: (0, b, j))],
        out_shape=[SDS((3, t, d), BF16)],
        sem=("parallel", "parallel"), blocks=blocks, temps=8 * _nbytes((s, c), F32), comm=comm,
    )(z, z, z, z, pool_w, pool_scale, conv_w)


def _mixer_out(lhs3, z, x, w3, g_ffn, comm=None):
    t, d = x.shape
    tm = _tile(t, 256, SUBLANES_BF16)

    def body(pq, zgp, zgc, x_ref, w_ref, g_ref, mrg, ypc, x1o, h2o):
        yp = _dot(pq[0], w_ref[0])
        yc = _dot(pq[1], w_ref[1])
        m = _sigmoid(zgp[...].astype(F32)) * yp + _sigmoid(zgc[...].astype(F32)) * yc
        mb = m.astype(BF16)
        x1 = x_ref[...] + _dot(mb, w_ref[2])
        ypc[0] = yp.astype(BF16)
        ypc[1] = yc.astype(BF16)
        mrg[...] = mb
        x1o[...] = x1
        xh, _ = _rms_fwd(x1)
        h2o[...] = (xh * g_ref[...]).astype(BF16)

    blocks = (_nbytes((2, tm, d), BF16) * 2 + _nbytes((tm, d), BF16) * 4 + _nbytes((tm, d), F32) * 2
              + _nbytes((3, d, d), BF16))
    return _pcall(
        body, name="mixer_out", grid=(t // tm,),
        in_specs=[pl.BlockSpec((2, tm, d), lambda i: (0, i, 0)),
                  pl.BlockSpec((tm, d), lambda i: (i, 4)),
                  pl.BlockSpec((tm, d), lambda i: (i, 5)),
                  pl.BlockSpec((tm, d), lambda i: (i, 0)),
                  pl.BlockSpec((3, d, d), lambda i: (0, 0, 0)),
                  pl.BlockSpec((1, d), lambda i: (0, 0))],
        out_specs=[pl.BlockSpec((None, tm, d), lambda i: (2, i, 0)),
                   pl.BlockSpec((2, tm, d), lambda i: (0, i, 0)),
                   pl.BlockSpec((tm, d), lambda i: (i, 0)),
                   pl.BlockSpec((tm, d), lambda i: (i, 0))],
        out_shape=[SDS(lhs3.shape, BF16), SDS((2, t, d), BF16), SDS((t, d), F32), SDS((t, d), BF16)],
        input_output_aliases={0: 0},
        sem=("parallel",), blocks=blocks, temps=8 * _nbytes((tm, d), F32), comm=comm,
    )(lhs3, z, z, x, w3, g_ffn)


def _ffn_up(h2, w_up, f, comm=None):
    t, d = h2.shape
    _, _, ws = w_up.shape
    tm = _tile(t, 1024, SUBLANES_BF16)
    tn = _tile(ws, 1408, LANES)
    nps = ws // tn
    npp = f // tn

    def body(h_ref, w_ref, o_ref):
        o_ref[...] = _dot(h_ref[...], w_ref[...]).astype(BF16)

    blocks = _nbytes((tm, d), BF16) + _nbytes((d, tn), BF16) + _nbytes((tm, tn), BF16)
    return _pcall(
        body, name="ffn_up", grid=(t // tm, 2 * npp),
        in_specs=[pl.BlockSpec((tm, d), lambda i, j: (i, 0)),
                  pl.BlockSpec((None, d, tn), lambda i, j: (j // nps, 0, j % nps))],
        out_specs=[pl.BlockSpec((None, tm, tn), lambda i, j: (j // npp, i, j % npp))],
        out_shape=[SDS((2, t, f), BF16)],
        sem=("parallel", "parallel"), blocks=blocks, temps=_nbytes((tm, tn), F32), comm=comm,
    )(h2, w_up)


def _conv3_rows(u, u1, u2, w_ref, p):
    return w_ref[p, pl.ds(2, 1), :] * u + w_ref[p, pl.ds(1, 1), :] * u1 + w_ref[p, pl.ds(0, 1), :] * u2


WGRAD_TOKENS = 2048
WGRAD_TOKENS_WIDE = 4096
CHUNK = 64
HALO = SUBLANES_F32


def _up1_up2(u, nxt):
    rows = u.shape[0]
    ext = jnp.concatenate([u, nxt], axis=0)
    n = rows + HALO
    return pltpu.roll(ext, n - 1, 0)[:rows], pltpu.roll(ext, n - 2, 0)[:rows]


def _fold8(x):
    return jnp.sum(x.reshape(x.shape[0] // SUBLANES_F32, SUBLANES_F32, x.shape[1]), axis=0)


def _ffn_mid_fwd(u0, cw, cb, nseq):
    _, t, f = u0.shape
    s = t // nseq
    c = _tile(f, 256, LANES)

    def body(u_ref, w_ref, b_ref, a_ref, uo_ref):
        row = lax.broadcasted_iota(jnp.int32, (s, c), 0)
        act = []
        for p in range(2):
            u = u_ref[p].astype(F32)
            act.append(_conv3_rows(u, _shift_down(u, 1, row), _shift_down(u, 2, row), w_ref, p) + b_ref[p])
            uo_ref[p] = act[p].astype(BF16)
        ug, uv = act
        a_ref[...] = (ug * _sigmoid(ug) * uv).astype(BF16)

    blocks = 2 * _nbytes((2, s, c), BF16) + _nbytes((s, c), BF16)
    outs, _ = _pcall(
        body, name="ffn_mid_fwd", grid=(f // c, nseq),
        in_specs=[pl.BlockSpec((2, s, c), lambda j, b: (0, b, j)),
                  pl.BlockSpec((2, 3, c), lambda j, b: (0, 0, j)),
                  pl.BlockSpec((2, 1, c), lambda j, b: (0, 0, j))],
        out_specs=[pl.BlockSpec((s, c), lambda j, b: (b, j)),
                   pl.BlockSpec((2, s, c), lambda j, b: (0, b, j))],
        out_shape=[SDS((t, f), BF16), SDS((2, t, f), BF16)],
        sem=("parallel", "parallel"), blocks=blocks, temps=8 * _nbytes((s, c), F32),
    )(u0, cw, cb)
    return outs


def _ffn_down_loss(a, w_down, x1, tgt, g_fin):
    t, f = a.shape
    d = x1.shape[1]
    tm = _tile(t, 256, SUBLANES_BF16)
    nsteps = t // tm

    def body(a_ref, w_ref, x1_ref, t_ref, g_ref, dx_ref, dxb_ref, loss_ref, gg_ref, lacc):
        i = pl.program_id(0)

        @pl.when(i == 0)
        def _():
            lacc[...] = jnp.zeros_like(lacc)
            gg_ref[...] = jnp.zeros_like(gg_ref)

        x2 = x1_ref[...] + _dot(a_ref[...], w_ref[...])
        xh, inv = _rms_fwd(x2)
        g = g_ref[...]
        e = xh * g - t_ref[...]
        lacc[...] += jnp.sum(e * e, axis=0, keepdims=True)
        dy = e * (1.0 / d)
        gg_ref[...] += jnp.sum(dy * xh, axis=0, keepdims=True)
        dx2 = _rms_bwd(dy, xh, inv, g)
        dx_ref[...] = dx2
        dxb_ref[...] = dx2.astype(BF16)

        @pl.when(i == nsteps - 1)
        def _():
            loss_ref[...] = jnp.sum(lacc[...], axis=1, keepdims=True) * (0.5 / d)

    blocks = (_nbytes((tm, f), BF16) + _nbytes((f, d), BF16) + 3 * _nbytes((tm, d), F32) + _nbytes((tm, d), BF16))
    outs, _ = _pcall(
        body, name="ffn_down_loss", grid=(nsteps,),
        in_specs=[pl.BlockSpec((tm, f), lambda i: (i, 0)), pl.BlockSpec((f, d), lambda i: (0, 0)),
                  pl.BlockSpec((tm, d), lambda i: (i, 0)), pl.BlockSpec((tm, d), lambda i: (i, 0)),
                  pl.BlockSpec((1, d), lambda i: (0, 0))],
        out_specs=[pl.BlockSpec((tm, d), lambda i: (i, 0)), pl.BlockSpec((tm, d), lambda i: (i, 0)),
                   pl.BlockSpec((1, 1), lambda i: (0, 0)), pl.BlockSpec((1, d), lambda i: (0, 0))],
        out_shape=[SDS((t, d), F32), SDS((t, d), BF16), SDS((1, 1), F32), SDS((1, d), F32)],
        scratch_shapes=[pltpu.VMEM((1, d), F32)],
        sem=("arbitrary",), blocks=blocks, temps=8 * _nbytes((tm, d), F32),
    )(a, w_down, x1, tgt, g_fin)
    return outs


def _ffn_bwd_da(dxb, w_down, comm=None):
    t, d = dxb.shape
    f = w_down.shape[0]
    tm = _tile(t, 512, SUBLANES_BF16)

    def body(x_ref, w_ref, o_ref):
        o_ref[...] = _dot_tb(x_ref[...], w_ref[...]).astype(BF16)

    blocks = _nbytes((tm, d), BF16) + _nbytes((tm, f), BF16)
    return _pcall(
        body, name="ffn_bwd_da", grid=(t // tm,),
        in_specs=[pl.BlockSpec((tm, d), lambda i: (i, 0)),
                  pl.BlockSpec((f, d), lambda i: (0, 0), pipeline_mode=pl.Buffered(1))],
        out_specs=[pl.BlockSpec((tm, f), lambda i: (i, 0))],
        out_shape=[SDS((t, f), BF16)],
        sem=("parallel",), blocks=blocks, temps=_nbytes((f, d), BF16) + _nbytes((tm, f), F32), comm=comm,
    )(dxb, w_down)


def _ffn_mid_bwd(da, u0, ua, cw, nseq, comm=None):
    _, t, f = u0.shape
    s = t // nseq
    c = _tile(f, 128, LANES)
    r = _tile(s, CHUNK, SUBLANES_BF16)
    n = s // r

    def body(da_ref, u_ref, ua_ref, w_ref, du_ref, gw_ref, gb_ref):
        @pl.when(pl.program_id(1) == 0)
        def _():
            gw_ref[...] = jnp.zeros_like(gw_ref)
            gb_ref[...] = jnp.zeros_like(gb_ref)

        def step(i, carry):
            nxt, sums = carry
            rows = pl.ds(pl.multiple_of((n - 1 - i) * r, r), r)
            ug = ua_ref[0, rows, :].astype(F32)
            uv = ua_ref[1, rows, :].astype(F32)
            sg = _sigmoid(ug)
            dacc = da_ref[rows, :].astype(F32)
            dus = (dacc * uv * sg * (1.0 + ug * (1.0 - sg)), dacc * (ug * sg))
            first, new_sums = [], []
            for p in range(2):
                du = dus[p]
                d1, d2 = _up1_up2(du, nxt[p])
                du_ref[p, rows, :] = _conv3_rows(du, d1, d2, w_ref, p).astype(BF16)
                u = u_ref[p, rows, :].astype(F32)
                sb, s0, s1, s2 = sums[p]
                new_sums.append((sb + _fold8(du), s0 + _fold8(d2 * u), s1 + _fold8(d1 * u), s2 + _fold8(du * u)))
                first.append(du[:HALO])
            return tuple(first), tuple(new_sums)

        zero = jnp.zeros((HALO, c), F32)
        _, sums = lax.fori_loop(0, n, step, ((zero, zero), ((zero,) * 4,) * 2))
        for p in range(2):
            sb, s0, s1, s2 = sums[p]
            gb_ref[p] += jnp.sum(sb, axis=0, keepdims=True)
            gw_ref[p, pl.ds(0, 1), :] += jnp.sum(s0, axis=0, keepdims=True)
            gw_ref[p, pl.ds(1, 1), :] += jnp.sum(s1, axis=0, keepdims=True)
            gw_ref[p, pl.ds(2, 1), :] += jnp.sum(s2, axis=0, keepdims=True)

    blocks = _nbytes((s, c), BF16) + 3 * _nbytes((2, s, c), BF16)
    return _pcall(
        body, name="ffn_mid_bwd", grid=(f // c, nseq),
        in_specs=[pl.BlockSpec((s, c), lambda j, b: (b, j)),
                  pl.BlockSpec((2, s, c), lambda j, b: (0, b, j)),
                  pl.BlockSpec((2, s, c), lambda j, b: (0, b, j)),
                  pl.BlockSpec((2, 3, c), lambda j, b: (0, 0, j))],
        out_specs=[pl.BlockSpec((2, s, c), lambda j, b: (0, b, j)),
                   pl.BlockSpec((2, 3, c), lambda j, b: (0, 0, j)),
                   pl.BlockSpec((2, 1, c), lambda j, b: (0, 0, j))],
        out_shape=[SDS((2, t, f), BF16), SDS((2, 3, f), F32), SDS((2, 1, f), F32)],
        sem=("parallel", "arbitrary"), blocks=blocks, temps=4 * 1024 * 1024, comm=comm,
    )(da, u0, ua, cw)


def _wgrad(a, b, name, *, tr, tn, b_plane_of=None, out_shards=None, comm=None):
    t, m = a.shape
    n_total = b.shape[-1] * (b.shape[0] if b.ndim == 3 else 1)
    tk = _tile(t, WGRAD_TOKENS_WIDE if n_total > tn and m == tr else WGRAD_TOKENS, SUBLANES_BF16)
    nk = t // tk
    once = pl.Buffered(1) if nk == 1 else None

    def body(a_ref, b_ref, o_ref, *acc):
        part = _dot_ta(a_ref[...], b_ref[...])
        if nk == 1:
            o_ref[...] = part.astype(BF16)
        else:
            _acc_over(pl.program_id(2), nk, part, acc[0], o_ref)

    if b.ndim == 3:
        b_spec = pl.BlockSpec((None, tk, tn), lambda r, n, k: (b_plane_of(n)[0], k, b_plane_of(n)[1]))
    else:
        b_spec = pl.BlockSpec((tk, tn), lambda r, n, k: (k, n), pipeline_mode=once if n_total == tn else None)
    if out_shards is None:
        o_spec = pl.BlockSpec((tr, tn), lambda r, n, k: (r, n))
        o_shape = SDS((m, n_total), BF16)
    else:
        nps = n_total // out_shards // tn
        o_spec = pl.BlockSpec((None, tr, tn), lambda r, n, k: (n // nps, r, n % nps))
        o_shape = SDS((out_shards, m, n_total // out_shards), BF16)
    blocks = _nbytes((tk, tr), BF16) + _nbytes((tk, tn), BF16) + _nbytes((tr, tn), BF16)
    return _pcall(
        body, name=name, grid=(m // tr, n_total // tn, nk),
        in_specs=[pl.BlockSpec((tk, tr), lambda r, n, k: (k, r), pipeline_mode=once if m == tr else None), b_spec],
        out_specs=[o_spec], out_shape=[o_shape],
        scratch_shapes=[] if nk == 1 else [pltpu.VMEM((tr, tn), F32)],
        sem=("parallel", "parallel", "arbitrary"), blocks=blocks, temps=2 * _nbytes((tr, tn), F32), comm=comm,
    )(a, b)


def _wgrad3(lhs3, rhs3, comm=None):
    nw, t, d = lhs3.shape
    tk = _tile(t, WGRAD_TOKENS, SUBLANES_BF16)
    nk = t // tk

    def body(a_ref, b_ref, o_ref, *acc):
        part = _dot_ta(a_ref[...], b_ref[...])
        if nk == 1:
            o_ref[...] = part.astype(BF16)
        else:
            _acc_over(pl.program_id(1), nk, part, acc[0], o_ref)

    blocks = 2 * _nbytes((tk, d), BF16) + _nbytes((d, d), BF16)
    return _pcall(
        body, name="wgrad_sq3", grid=(nw, nk),
        in_specs=[pl.BlockSpec((None, tk, d), lambda w, k: (w, k, 0)),
                  pl.BlockSpec((None, tk, d), lambda w, k: (w, k, 0))],
        out_specs=[pl.BlockSpec((None, d, d), lambda w, k: (w, 0, 0))],
        out_shape=[SDS((nw, d, d), BF16)],
        scratch_shapes=[] if nk == 1 else [pltpu.VMEM((d, d), F32)],
        sem=("parallel", "arbitrary"), blocks=blocks, temps=2 * _nbytes((d, d), F32), comm=comm,
    )(lhs3, rhs3)


def _ffn_bwd_dx1(du0, w_up, x1, dx2, g_ffn, n_planes_out, comm=None):
    _, t, f = du0.shape
    d = x1.shape[1]
    nsh, _, ws = w_up.shape
    tm = _tile(t, 256, SUBLANES_BF16)
    spp = f // ws

    def body(du_ref, w_ref, x1_ref, dx2_ref, g_ref, dx1_ref, dxb_ref, gg_ref):
        @pl.when(pl.program_id(0) == 0)
        def _():
            gg_ref[...] = jnp.zeros_like(gg_ref)

        dh = None
        for k in range(nsh):
            part = _dot_tb(du_ref[k // spp, :, (k % spp) * ws:(k % spp + 1) * ws], w_ref[k])
            dh = part if dh is None else dh + part
        xh, inv = _rms_fwd(x1_ref[...])
        gg_ref[...] += jnp.sum(dh * xh, axis=0, keepdims=True)
        dx1 = dx2_ref[...] + _rms_bwd(dh, xh, inv, g_ref[...])
        dx1_ref[...] = dx1
        dxb_ref[...] = dx1.astype(BF16)

    blocks = _nbytes((2, tm, f), BF16) + 3 * _nbytes((tm, d), F32) + _nbytes((tm, d), BF16)
    return _pcall(
        body, name="ffn_bwd_dx1", grid=(t // tm,),
        in_specs=[pl.BlockSpec((2, tm, f), lambda i: (0, i, 0)),
                  pl.BlockSpec((nsh, d, ws), lambda i: (0, 0, 0), pipeline_mode=pl.Buffered(1)),
                  pl.BlockSpec((tm, d), lambda i: (i, 0)),
                  pl.BlockSpec((tm, d), lambda i: (i, 0)),
                  pl.BlockSpec((1, d), lambda i: (0, 0))],
        out_specs=[pl.BlockSpec((tm, d), lambda i: (i, 0)),
                   pl.BlockSpec((None, tm, d), lambda i: (n_planes_out - 1, i, 0)),
                   pl.BlockSpec((1, d), lambda i: (0, 0))],
        out_shape=[SDS((t, d), F32), SDS((n_planes_out, t, d), BF16), SDS((1, d), F32)],
        sem=("arbitrary",), blocks=blocks, temps=_nbytes(w_up.shape, BF16) + 8 * _nbytes((tm, d), F32), comm=comm,
    )(du0, w_up, x1, dx2, g_ffn)


def _mixer_bwd(rhs3, z, ypc, w3, comm=None):
    _, t, d = rhs3.shape
    tm = _tile(t, 512, SUBLANES_BF16)

    def body(dx_ref, zgp, zgc, ypc_ref, w_ref, dyo, dzo, dpq):
        dm = _dot_tb(dx_ref[...], w_ref[2])
        sp = _sigmoid(zgp[...].astype(F32))
        sc = _sigmoid(zgc[...].astype(F32))
        dyp = (dm * sp).astype(BF16)
        dyc = (dm * sc).astype(BF16)
        dzo[0] = (dm * ypc_ref[0].astype(F32) * sp * (1.0 - sp)).astype(BF16)
        dzo[1] = (dm * ypc_ref[1].astype(F32) * sc * (1.0 - sc)).astype(BF16)
        dyo[0] = dyp
        dyo[1] = dyc
        dpq[0] = _dot_tb(dyp, w_ref[0]).astype(BF16)
        dpq[1] = _dot_tb(dyc, w_ref[1]).astype(BF16)

    blocks = _nbytes((tm, d), BF16) * 3 + _nbytes((2, tm, d), BF16) * 4 + _nbytes((3, d, d), BF16)
    return _pcall(
        body, name="mixer_bwd", grid=(t // tm,),
        in_specs=[pl.BlockSpec((None, tm, d), lambda i: (2, i, 0)),
                  pl.BlockSpec((tm, d), lambda i: (i, 4)),
                  pl.BlockSpec((tm, d), lambda i: (i, 5)),
                  pl.BlockSpec((2, tm, d), lambda i: (0, i, 0)),
                  pl.BlockSpec((3, d, d), lambda i: (0, 0, 0))],
        out_specs=[pl.BlockSpec((2, tm, d), lambda i: (0, i, 0)),
                   pl.BlockSpec((2, tm, d), lambda i: (2, i, 0)),
                   pl.BlockSpec((2, tm, d), lambda i: (0, i, 0))],
        out_shape=[SDS(rhs3.shape, BF16), SDS((N_SPLITS, t, d), BF16), SDS((2, t, d), BF16)],
        input_output_aliases={0: 0},
        sem=("parallel",), blocks=blocks, temps=8 * _nbytes((tm, d), F32), comm=comm,
    )(rhs3, z, z, ypc, w3)


def _conv_bwd(dz, dpq, z, conv_w, nseq, comm=None):
    _, t, d = dz.shape
    s = t // nseq
    c = _tile(d, 256, LANES)
    nb = d // c

    def body(dz_in, dq_ref, zb, zc, zv, cw, dzo, gw_ref):
        del dz_in

        @pl.when(pl.program_id(1) == 0)
        def _():
            gw_ref[...] = jnp.zeros_like(gw_ref)

        row = lax.broadcasted_iota(jnp.int32, (s, c), 0)
        b = zb[...].astype(F32)
        cm = zc[...].astype(F32)
        v = zv[...].astype(F32)
        cv = cm * v
        cv1 = _shift_down(cv, 1, row)
        cv2 = _shift_down(cv, 2, row)
        w0, w1, w2 = cw[pl.ds(0, 1), :], cw[pl.ds(1, 1), :], cw[pl.ds(2, 1), :]
        cc = w2 * cv + w1 * cv1 + w0 * cv2
        dq = dq_ref[...].astype(F32)
        dzo[0] = (dq * cc).astype(BF16)
        dcc = dq * b
        gw_ref[pl.ds(0, 1), :] += jnp.sum(dcc * cv2, axis=0, keepdims=True)
        gw_ref[pl.ds(1, 1), :] += jnp.sum(dcc * cv1, axis=0, keepdims=True)
        gw_ref[pl.ds(2, 1), :] += jnp.sum(dcc * cv, axis=0, keepdims=True)
        dcv = w2 * dcc + w1 * _shift_up(dcc, 1, row) + w0 * _shift_up(dcc, 2, row)
        dzo[1] = (dcv * v).astype(BF16)
        dzo[2] = (dcv * cm).astype(BF16)

    blocks = 4 * _nbytes((s, c), BF16) + _nbytes((3, s, c), BF16)
    return _pcall(
        body, name="conv_bwd", grid=(nb, nseq),
        in_specs=[ANY,
                  pl.BlockSpec((None, s, c), lambda j, b: (1, b, j)),
                  pl.BlockSpec((s, c), lambda j, b: (b, nb + j)),
                  pl.BlockSpec((s, c), lambda j, b: (b, 2 * nb + j)),
                  pl.BlockSpec((s, c), lambda j, b: (b, 3 * nb + j)),
                  pl.BlockSpec((3, c), lambda j, b: (0, j))],
        out_specs=[pl.BlockSpec((3, s, c), lambda j, b: (0, b, j)),
                   pl.BlockSpec((3, c), lambda j, b: (0, j))],
        out_shape=[SDS(dz.shape, BF16), SDS((3, d), F32)],
        input_output_aliases={0: 0},
        sem=("parallel", "arbitrary"), blocks=blocks, temps=16 * _nbytes((s, c), F32), comm=comm,
    )(dz, dpq, z, z, z, conv_w)


def _pool_bwd_call(dz, dpq, z, pool_w, pool_scale, nseq, comm=None):
    _, t, d = dz.shape
    s = t // nseq
    c = d // N_GROUPS

    def body(dz_in, dp_ref, zp, pw, ps, dzo, gpw_ref, gps_ref):
        del dz_in
        j = pl.program_id(0)

        @pl.when(pl.program_id(1) == 0)
        def _():
            gpw_ref[...] = jnp.zeros_like(gpw_ref)
            gps_ref[...] = jnp.zeros_like(gps_ref)

        row = lax.broadcasted_iota(jnp.int32, (s, c), 0)
        for gi, win in enumerate(POOL_WINDOWS):
            @pl.when(j == gi)
            def _(win=win):
                pb = _pool_fwd(zp[...].astype(F32), win, row).astype(BF16)
                plin = _dot(pb, pw[...])
                dps = dp_ref[...].astype(F32)
                gps_ref[...] += jnp.sum(dps * plin, axis=0, keepdims=True)
                dplb = (dps * ps[...]).astype(BF16)
                gpw_ref[...] += _dot_ta(pb, dplb)
                dzo[...] = _pool_bwd(_dot_tb(dplb, pw[...]), win, row).astype(BF16)

    blocks = 3 * _nbytes((s, c), BF16) + _nbytes((c, c), BF16) + _nbytes((c, c), F32)
    return _pcall(
        body, name="pool_bwd", grid=(N_GROUPS, nseq),
        in_specs=[ANY,
                  pl.BlockSpec((None, s, c), lambda j, b: (0, b, j)),
                  pl.BlockSpec((s, c), lambda j, b: (b, j)),
                  pl.BlockSpec((None, c, c), lambda j, b: (j, 0, 0)),
                  pl.BlockSpec((1, c), lambda j, b: (0, j))],
        out_specs=[pl.BlockSpec((None, s, c), lambda j, b: (3, b, j)),
                   pl.BlockSpec((None, c, c), lambda j, b: (j, 0, 0)),
                   pl.BlockSpec((1, c), lambda j, b: (0, j))],
        out_shape=[SDS(dz.shape, BF16), SDS((N_GROUPS, c, c), F32), SDS((1, d), F32)],
        input_output_aliases={0: 0},
        sem=("parallel", "arbitrary"), blocks=blocks, temps=10 * _nbytes((s, c), F32), comm=comm,
    )(dz, dpq, z, pool_w, pool_scale)


def _dz_plane(zb):
    return jnp.where(zb < 4, (zb + 3) % 4, zb)


def _wgrad_in(h1, dz, nsh, comm=None):
    t, d = h1.shape
    ws = N_SPLITS * d // nsh
    kb = _tile(math.gcd(d, ws), 512, LANES)
    npl = d // kb
    nps = ws // kb
    tk = _tile(t, WGRAD_TOKENS_WIDE, SUBLANES_BF16)
    nk = t // tk

    def body(a_ref, b_ref, o_ref, *acc):
        part = _dot_ta(a_ref[...], b_ref[...])
        if nk == 1:
            o_ref[...] = part.astype(BF16)
        else:
            _acc_over(pl.program_id(1), nk, part, acc[0], o_ref)

    blocks = _nbytes((tk, d), BF16) + _nbytes((tk, kb), BF16) + _nbytes((d, kb), BF16)
    return _pcall(
        body, name="wgrad_in", grid=(N_SPLITS * npl, nk),
        in_specs=[pl.BlockSpec((tk, d), lambda cb, k: (k, 0), pipeline_mode=pl.Buffered(1) if nk == 1 else None),
                  pl.BlockSpec((None, tk, kb), lambda cb, k: (_dz_plane(cb // npl), k, cb % npl))],
        out_specs=[pl.BlockSpec((None, d, kb), lambda cb, k: (cb // nps, 0, cb % nps))],
        out_shape=[SDS((nsh, d, ws), BF16)],
        scratch_shapes=[] if nk == 1 else [pltpu.VMEM((d, kb), F32)],
        sem=("parallel", "arbitrary"), blocks=blocks, temps=2 * _nbytes((d, kb), F32), comm=comm,
    )(h1, dz)


def _mixer_bwd_dx(dz, w_in, x, dx1, g_mix, comm=None):
    npln, t, d = dz.shape
    nsh, _, ws = w_in.shape
    tm = _tile(t, 256, SUBLANES_BF16)
    kb = _tile(math.gcd(d, ws), 512, LANES)
    npl = d // kb
    nps = ws // kb

    def body(dz_ref, w_ref, x_ref, dx1_ref, g_ref, dx_ref, gg_ref):
        @pl.when(pl.program_id(0) == 0)
        def _():
            gg_ref[...] = jnp.zeros_like(gg_ref)

        dh = None
        for cb in range(npln * npl):
            zb = cb // npl
            plane = (zb + 3) % 4 if zb < 4 else zb
            part = _dot_tb(dz_ref[plane, :, (cb % npl) * kb:(cb % npl + 1) * kb],
                           w_ref[cb // nps, :, (cb % nps) * kb:(cb % nps + 1) * kb])
            dh = part if dh is None else dh + part
        xh, inv = _rms_fwd(x_ref[...])
        gg_ref[...] += jnp.sum(dh * xh, axis=0, keepdims=True)
        dx_ref[...] = dx1_ref[...] + _rms_bwd(dh, xh, inv, g_ref[...])

    blocks = _nbytes((npln, tm, d), BF16) + 3 * _nbytes((tm, d), F32)
    return _pcall(
        body, name="mixer_bwd_dx", grid=(t // tm,),
        in_specs=[pl.BlockSpec((npln, tm, d), lambda i: (0, i, 0)),
                  pl.BlockSpec((nsh, d, ws), lambda i: (0, 0, 0), pipeline_mode=pl.Buffered(1)),
                  pl.BlockSpec((tm, d), lambda i: (i, 0)),
                  pl.BlockSpec((tm, d), lambda i: (i, 0)),
                  pl.BlockSpec((1, d), lambda i: (0, 0))],
        out_specs=[pl.BlockSpec((tm, d), lambda i: (i, 0)),
                   pl.BlockSpec((1, d), lambda i: (0, 0))],
        out_shape=[SDS((t, d), F32), SDS((1, d), F32)],
        sem=("arbitrary",), blocks=blocks, temps=_nbytes(w_in.shape, BF16) + 8 * _nbytes((tm, d), F32), comm=comm,
    )(dz, w_in, x, dx1, g_mix)


N_BIG = 5
SHARD_MAJOR = (0, 2)
ROWS_DIM1 = (1, 4)


def _ds(start, size, align):
    if isinstance(start, int):
        return pl.ds(start, size)
    return pl.ds(pl.multiple_of(start, align), size)


def _piece(a, ref, k, h):
    if a in SHARD_MAJOR:
        r = ref.shape[1] // 2
        return ref.at[k, _ds(h * r, r, SUBLANES_BF16), :]
    if a in ROWS_DIM1:
        r = ref.shape[1] // 8
        return ref.at[:, _ds((2 * k + h) * r, r, SUBLANES_BF16), :]
    r = ref.shape[0] // 8
    return ref.at[_ds((2 * k + h) * r, r, SUBLANES_BF16), :]


def _half(a, ref, h):
    if a in ROWS_DIM1:
        r = ref.shape[1] // 2
        return ref.at[:, _ds(h * r, r, SUBLANES_BF16), :]
    r = ref.shape[0] // 2
    return ref.at[_ds(h * r, r, SUBLANES_BF16), :]


def _piece_shape(a, full_shape):
    if a in SHARD_MAJOR:
        return (full_shape[1] // 2, full_shape[2])
    if a in ROWS_DIM1:
        return (full_shape[0], full_shape[1] // 8, full_shape[2])
    return (full_shape[0] // 8, full_shape[1])


def _shard_shape(a, full_shape):
    if a in SHARD_MAJOR:
        return (full_shape[1], full_shape[2])
    if a in ROWS_DIM1:
        return (full_shape[0], full_shape[1] // 4, full_shape[2])
    return (full_shape[0] // 4, full_shape[1])


def _rows_axis(a):
    return 1 if a in ROWS_DIM1 else 0


def _piece_block(a, full_shape):
    ps = _piece_shape(a, full_shape)
    if a in SHARD_MAJOR:
        return (None,) + ps, lambda k, c: (k, c, 0)
    if a in ROWS_DIM1:
        return ps, lambda k, c: (0, 2 * k + c, 0)
    return ps, lambda k, c: (2 * k + c, 0)


def _coords():
    return lax.axis_index("x"), lax.axis_index("y"), lax.axis_index("c")


def _peer_chips(x, y):
    return [(1 - x, y), (x, 1 - y), (1 - x, 1 - y)]


def _remote(src, dst, ssem, rsem, dev):
    return pltpu.make_async_remote_copy(src_ref=src, dst_ref=dst, send_sem=ssem, recv_sem=rsem,
                                        device_id=dev, device_id_type=MESH)


def _dma_sems(*counts):
    return [pltpu.SemaphoreType.DMA((n,)) for n in counts]


def _symmetric(ins, out_shapes, sems, copies, peers, aliases=None):
    def start(cins, couts, csems):
        for cp in copies(cins, couts, csems):
            cp.start()

    def finish(cins, couts, csems):
        for cp in copies(cins, couts, csems):
            cp.wait()

    return _Comm(ins, out_shapes, sems, start, finish, peers, aliases)


def _rows_part(a, ref, part):
    if part is None:
        return ref
    p, q, n = part
    ax = _rows_axis(a)
    r = ref.shape[ax] // n
    return ref.at[tuple(pl.ds(p * r, (q - p) * r) if d == ax else slice(None) for d in range(len(ref.shape)))]


def _merge(comms):
    ins, outs, sems, aliases, spans = [], [], [], {}, []
    for cm in comms:
        spans.append((len(ins), len(outs), len(sems)))
        for i, o in cm.aliases.items():
            aliases[len(ins) + i] = len(outs) + o
        ins += cm.ins
        outs += cm.out_shapes
        sems += cm.sems

    def each(fn_name):
        def run(cins, couts, csems):
            for cm, (i0, o0, s0) in zip(comms, spans):
                fn = getattr(cm, fn_name)
                if fn is not None:
                    fn(cins[i0:i0 + len(cm.ins)], couts[o0:o0 + len(cm.out_shapes)], csems[s0:s0 + len(cm.sems)])
        return run

    return _Comm(ins, outs, sems, each("start"), each("finish"), frozenset().union(*[cm.peers for cm in comms]),
                 aliases, mid=each("mid") if any(cm.mid is not None for cm in comms) else None)


def _gather_comm(arrs, locs, full_shapes, part=None, into=None):
    n = len(arrs)

    def own(cins, couts, csems):
        x, y, c = _coords()
        j = 2 * x + y
        return [_remote(_rows_part(a, _half(a, cins[q], h), part), _rows_part(a, _piece(a, couts[q], j, h), part),
                        csems[0].at[2 * q + h], csems[1].at[2 * q + h], (x, y, 1 - c))
                for q, a in enumerate(arrs) for h in range(2)]

    def sends(cins, couts, csems):
        x, y, c = _coords()
        j = 2 * x + y
        return [_remote(_rows_part(a, _half(a, cins[q], c), part), _rows_part(a, _piece(a, couts[q], j, c), part),
                        csems[2].at[3 * q + i], csems[3].at[3 * q + i], (px, py, c))
                for q, a in enumerate(arrs) for i, (px, py) in enumerate(_peer_chips(x, y))]

    def forwards(couts, csems, half_of):
        x, y, c = _coords()
        out = []
        for q, a in enumerate(arrs):
            for i, (px, py) in enumerate(_peer_chips(x, y)):
                landed = _rows_part(a, _piece(a, couts[q], 2 * px + py, half_of(c)), part)
                out.append(_remote(landed, landed, csems[4].at[3 * q + i], csems[5].at[3 * q + i], (x, y, 1 - c)))
        return out

    def start(cins, couts, csems):
        for cp in sends(cins, couts, csems) + own(cins, couts, csems):
            cp.start()

    def finish(cins, couts, csems):
        fw = forwards(couts, csems, lambda c: c)
        for cp, f in zip(sends(cins, couts, csems), fw):
            cp.wait_recv()
            f.start()
        for f in forwards(couts, csems, lambda c: 1 - c):
            f.wait_recv()
        for cp in sends(cins, couts, csems) + fw:
            cp.wait_send()
        for cp in own(cins, couts, csems):
            cp.wait()

    ins = [locs[a] for a in arrs] + ([into[a] for a in arrs] if into else [])
    return _Comm(ins, [SDS(full_shapes[a], BF16) for a in arrs],
                 _dma_sems(2 * n, 2 * n, 3 * n, 3 * n, 3 * n, 3 * n), start, finish, CHIPS + (SIBLING,),
                 aliases={n + q: q for q in range(n)} if into else None)


def _ring_gather_comm(arrs, locs, full_shapes):
    n = len(arrs)

    def own(cins, couts, csems):
        x, y, c = _coords()
        j = 2 * x + y
        return [_remote(_half(a, cins[q], h), _piece(a, couts[q], j, h), csems[0].at[2 * q + h],
                        csems[1].at[2 * q + h], (x, y, 1 - c)) for q, a in enumerate(arrs) for h in range(2)]

    def sends(cins, couts, csems):
        x, y, c = _coords()
        j = 2 * x + y
        return [_remote(_half(a, cins[q], c), _piece(a, couts[q], j, c), csems[2].at[2 * q + i],
                        csems[3].at[2 * q + i], (px, py, c))
                for q, a in enumerate(arrs) for i, (px, py) in enumerate(_peer_chips(x, y)[:2])]

    def relays(couts, csems):
        x, y, c = _coords()
        peers = _peer_chips(x, y)
        out = []
        for q, a in enumerate(arrs):
            for r, (src_p, dst_p) in enumerate(((0, 1), (1, 0))):
                sx, sy = peers[src_p]
                rows = _rows_part(a, _piece(a, couts[q], 2 * sx + sy, c), (r, r + 1, 2))
                out.append(_remote(rows, rows, csems[6].at[2 * q + r], csems[7].at[2 * q + r], (*peers[dst_p], c)))
        return out

    def forwards(couts, csems, half_of, which):
        x, y, c = _coords()
        out = []
        for q, a in enumerate(arrs):
            for i in which:
                px, py = _peer_chips(x, y)[i]
                landed = _piece(a, couts[q], 2 * px + py, half_of(c))
                out.append(_remote(landed, landed, csems[4].at[3 * q + i], csems[5].at[3 * q + i], (x, y, 1 - c)))
        return out

    def start(cins, couts, csems):
        for cp in sends(cins, couts, csems) + own(cins, couts, csems):
            cp.start()

    def mid(cins, couts, csems):
        for cp in sends(cins, couts, csems):
            cp.wait_recv()
        for cp in relays(couts, csems) + forwards(couts, csems, lambda c: c, (0, 1)):
            cp.start()

    def finish(cins, couts, csems):
        for cp in relays(couts, csems):
            cp.wait_recv()
        fw_diag = forwards(couts, csems, lambda c: c, (2,))
        for f in fw_diag:
            f.start()
        for f in forwards(couts, csems, lambda c: 1 - c, (0, 1, 2)):
            f.wait_recv()
        for cp in (sends(cins, couts, csems) + relays(couts, csems)
                   + forwards(couts, csems, lambda c: c, (0, 1)) + fw_diag):
            cp.wait_send()
        for cp in own(cins, couts, csems):
            cp.wait()

    return _Comm([locs[a] for a in arrs], [SDS(full_shapes[a], BF16) for a in arrs],
                 _dma_sems(2 * n, 2 * n, 2 * n, 2 * n, 3 * n, 3 * n, 2 * n, 2 * n), start, finish,
                 CHIPS + (SIBLING,), mid=mid)


def _halves_comm(arrs, gbs):
    n = len(arrs)

    def copies(cins, couts, csems):
        x, y, c = _coords()
        return [_remote(_piece(a, cins[q], k, 1 - c), couts[q].at[k], csems[0].at[4 * q + k], csems[1].at[4 * q + k],
                        (x, y, 1 - c)) for q, a in enumerate(arrs) for k in range(4)]

    return _symmetric([gbs[a] for a in arrs], [SDS((4,) + _piece_shape(a, gbs[a].shape), BF16) for a in arrs],
                      _dma_sems(4 * n, 4 * n), copies, [SIBLING])


def _chips_comm(arrs, ps, part=None, into=None):
    n = len(arrs)

    def copies(cins, couts, csems):
        x, y, c = _coords()
        return [_remote(_rows_part(a, cins[q].at[2 * px + py], part), _rows_part(a, couts[q].at[i], part),
                        csems[0].at[3 * q + i], csems[1].at[3 * q + i], (px, py, c))
                for q, a in enumerate(arrs) for i, (px, py) in enumerate(_peer_chips(x, y))]

    ins = [ps[a] for a in arrs] + ([into[a] for a in arrs] if into else [])
    return _symmetric(ins, [SDS((3,) + ps[a].shape[1:], BF16) for a in arrs], _dma_sems(3 * n, 3 * n), copies, CHIPS,
                      aliases={n + q: q for q in range(n)} if into else None)


def _result_comm(arrs, gs):
    n = len(arrs)

    def copies(cins, couts, csems):
        x, y, c = _coords()
        return [_remote(_half(a, cins[q], c), _half(a, couts[q], c), csems[0].at[q], csems[1].at[q], (x, y, 1 - c))
                for q, a in enumerate(arrs)]

    return _symmetric([gs[a] for a in arrs], [SDS(gs[a].shape, F32) for a in arrs], _dma_sems(n, n), copies,
                      [SIBLING], aliases={q: q for q in range(n)})


def _add_halves(arrs, gbs, lands, c_arr, name):
    n = len(arrs)

    def body(c_ref, *refs):
        del c_ref
        for q in range(n):
            refs[2 * n + q][...] = (refs[q][...].astype(F32) + refs[n + q][...].astype(F32)).astype(BF16)

    g_specs, l_specs, o_specs, blocks = [], [], [], 0
    for a in arrs:
        bs, imap = _piece_block(a, gbs[a].shape)
        ps = _piece_shape(a, gbs[a].shape)
        g_specs.append(pl.BlockSpec(bs, lambda k, c_ref, imap=imap: imap(k, c_ref[0])))
        nd = len(ps)
        l_specs.append(pl.BlockSpec((None,) + ps, lambda k, c_ref, nd=nd: (k,) + (0,) * nd))
        o_specs.append(pl.BlockSpec((None,) + ps, lambda k, c_ref, nd=nd: (k,) + (0,) * nd))
        blocks += 3 * _nbytes(ps, BF16)
    return list(pl.pallas_call(
        body, name=name,
        grid_spec=pltpu.PrefetchScalarGridSpec(
            num_scalar_prefetch=1, grid=(4,), in_specs=g_specs + l_specs, out_specs=o_specs),
        out_shape=[SDS((4,) + _piece_shape(a, gbs[a].shape), BF16) for a in arrs],
        compiler_params=_params(("parallel",), blocks, blocks),
    )(c_arr, *[gbs[a] for a in arrs], *lands))


def _sum_chips(a, p, land, shard_shape, jc_arr, name):
    ps = land.shape[1:]
    ax = _rows_axis(a)
    rows = ps[ax]
    nsub = 2 if rows % (2 * SUBLANES_BF16) == 0 else 1
    bs = tuple(r // nsub if q == ax else r for q, r in enumerate(ps))
    nd = len(ps)

    def at_rows(v):
        return tuple(v if q == ax else 0 for q in range(nd))

    def body(jc_ref, p_ref, l_ref, o_ref):
        del jc_ref
        acc = p_ref[...].astype(F32) + l_ref[0].astype(F32)
        acc = acc + l_ref[1].astype(F32)
        o_ref[...] = acc + l_ref[2].astype(F32)

    blocks = 4 * _nbytes(bs, BF16) + _nbytes(bs, F32)
    return pl.pallas_call(
        body, name=name,
        grid_spec=pltpu.PrefetchScalarGridSpec(
            num_scalar_prefetch=1, grid=(nsub,),
            in_specs=[pl.BlockSpec((None,) + bs, lambda s, jc: (jc[0],) + at_rows(s)),
                      pl.BlockSpec((3,) + bs, lambda s, jc: (0,) + at_rows(s))],
            out_specs=pl.BlockSpec(bs, lambda s, jc: at_rows(jc[1] * nsub + s))),
        out_shape=SDS(shard_shape, F32),
        compiler_params=_params(("parallel",), blocks, 2 * _nbytes(bs, F32)),
    )(jc_arr, p, land)


def _small_comm(v):
    rows = v.shape[0]

    def copies(cins, couts, csems):
        x, y, c = _coords()
        me = 4 * x + 2 * y + c
        out = [pltpu.make_async_copy(cins[0], couts[0].at[me], csems[0].at[0])]
        for dlt in range(1, 8):
            px = 1 - x if (dlt >> 2) & 1 else x
            py = 1 - y if (dlt >> 1) & 1 else y
            pc = 1 - c if dlt & 1 else c
            out.append(_remote(cins[0], couts[0].at[me], csems[1].at[dlt - 1], csems[2].at[dlt - 1], (px, py, pc)))
        return out

    return _symmetric([v], [SDS((8, rows, LANES), F32)], _dma_sems(1, 7, 7), copies, EVERYONE)


def _sum8(slots, name):
    def body(s_ref, o_ref):
        acc = s_ref[0]
        for i in range(1, 8):
            acc = acc + s_ref[i]
        o_ref[...] = acc

    return pl.pallas_call(
        body, name=name,
        in_specs=[pl.BlockSpec(memory_space=pltpu.VMEM)], out_specs=pl.BlockSpec(memory_space=pltpu.VMEM),
        out_shape=SDS(slots.shape[1:], F32),
    )(slots)


def _adamw(w, g, m, v, name, g_plane=None):
    rows, cols = w.shape
    tr = _tile(rows, max(SUBLANES_F32, (256 * 1024 // cols) // SUBLANES_F32 * SUBLANES_F32), SUBLANES_F32)

    def body(w_ref, g_ref, m_ref, v_ref, go_ref, d_ref, mo_ref, vo_ref):
        gr = g_ref[...]
        mn = ADAM_B1 * m_ref[...] + (1.0 - ADAM_B1) * gr
        vn = ADAM_B2 * v_ref[...] + (1.0 - ADAM_B2) * (gr * gr)
        m_hat = mn / (1.0 - ADAM_B1 ** ADAM_STEP)
        v_hat = vn / (1.0 - ADAM_B2 ** ADAM_STEP)
        d_ref[...] = -ADAM_LR * (m_hat / (jnp.sqrt(v_hat) + ADAM_EPS) + ADAM_WD * w_ref[...])
        go_ref[...] = gr
        mo_ref[...] = mn
        vo_ref[...] = vn

    spec = pl.BlockSpec((tr, cols), lambda i: (i, 0))
    g_spec = spec if g_plane is None else pl.BlockSpec((None, tr, cols), lambda i: (g_plane, i, 0))
    return pl.pallas_call(
        body, name=name, grid=(rows // tr,),
        in_specs=[spec, g_spec, spec, spec], out_specs=[spec, spec, spec, spec],
        out_shape=[SDS((rows, cols), F32)] * 4,
        compiler_params=_params(("parallel",), 8 * _nbytes((tr, cols), F32), 4 * _nbytes((tr, cols), F32)),
    )(w, g, m, v)


def _pack(parts):
    rows = []
    for p in parts:
        r = p.reshape(-1, LANES)
        pad = (-r.shape[0]) % SUBLANES_F32
        if pad:
            r = jnp.pad(r, ((0, pad), (0, 0)))
        rows.append(r)
    return jnp.concatenate(rows, axis=0)


def _unpack(packed, shapes):
    out, at = [], 0
    for s in shapes:
        n = 1
        for q in s:
            n *= q
        r = n // LANES
        out.append(packed[at:at + r].reshape(s))
        at += r + (-r) % SUBLANES_F32
    return out


def kernel(x, norm_mix, w_in, pool_w, pool_scale, w_pool_proj, conv_w, w_conv_out, w_o, norm_ffn, w_up, ffn_conv_w, ffn_conv_b, w_down, norm_final, loss_target, m_norm_mix, m_w_in, m_pool_w, m_pool_scale, m_w_pool_proj, m_conv_w, m_w_conv_out, m_w_o, m_norm_ffn, m_w_up, m_ffn_conv_w, m_ffn_conv_b, m_w_down, m_norm_final, v_norm_mix, v_w_in, v_pool_w, v_pool_scale, v_w_pool_proj, v_conv_w, v_w_conv_out, v_w_o, v_norm_ffn, v_w_up, v_ffn_conv_w, v_ffn_conv_b, v_w_down, v_norm_final):
    nseq, seq, d = x.shape
    t = nseq * seq
    f = w_down.shape[1] * 4
    c = d // N_GROUPS
    xy = lax.axis_index("x") * 2 + lax.axis_index("y")
    c_arr = lax.axis_index("c").astype(jnp.int32).reshape(1)
    jc_arr = jnp.stack([xy, lax.axis_index("c")]).astype(jnp.int32)
    nsh = 4
    zero = jnp.zeros((), jnp.int32)

    locs = [w_in[0].astype(BF16),
            jnp.stack([w_pool_proj[0], w_conv_out[0], w_o[0]]).astype(BF16),
            w_up[0].astype(BF16), w_down[0].astype(BF16), pool_w[0].astype(BF16)]
    full_shapes = [(nsh, d, N_SPLITS * d // nsh), (3, d, d), (nsh, d, 2 * f // nsh), (f, d), (N_GROUPS, c, c)]

    cw_pad = lax.dynamic_update_slice(jnp.zeros((3, d), F32), conv_w[0], (zero, xy * (d // 4)))
    fw_pad = lax.dynamic_update_slice(jnp.zeros((3, 2 * f), F32), ffn_conv_w[0], (zero, xy * (f // 2)))
    small_w = _pack([cw_pad, fw_pad]) * 0.5

    x2d = x.reshape(t, d)
    tgt = loss_target.reshape(t, d)
    ax, ay = lax.axis_index("x"), lax.axis_index("y")
    order = jnp.stack([xy, 2 * (1 - ax) + ay, 2 * ax + 1 - ay, 2 * (1 - ax) + 1 - ay]).astype(jnp.int32)
    (z, h1, w_in_f), (pool_w_f, w3_f, slots_w) = _fwd_in(
        x2d, norm_mix, locs[0], order,
        _merge([_gather_comm([4], locs, full_shapes), _gather_comm([1], locs, full_shapes, part=(0, 1, 2)),
                _small_comm(small_w)]))
    conv_w_f, ffn_cw_f = _unpack(_sum8(slots_w, "sum8_weights"), [(3, d), (3, 2 * f)])
    ffn_cw_p = ffn_cw_f.reshape(3, 2, f).transpose(1, 0, 2)
    ffn_cb_p = ffn_conv_b.reshape(2, 1, f)
    (lhs3,), (w3_f,) = _mixer_mid_fwd(z, pool_w_f, pool_scale, conv_w_f, nseq,
                                      _gather_comm([1], locs, full_shapes, part=(1, 2, 2), into={1: w3_f}))
    (lhs3, ypc, x1, h2), (w_up_f,) = _mixer_out(lhs3, z, x2d, w3_f, norm_ffn,
                                                _ring_gather_comm([2], locs, full_shapes))
    (u0,), (w_down_f,) = _ffn_up(h2, w_up_f, f, _gather_comm([3], locs, full_shapes))
    act, ua = _ffn_mid_fwd(u0, ffn_cw_p, ffn_cb_p, nseq)
    dx2, dx2b, loss11, g_norm_final = _ffn_down_loss(act, w_down_f, x1, tgt, norm_final.reshape(1, d))

    gbs, lands, ps, lands2, rs = {}, {}, {}, {}, {}
    tn_up = _tile(2 * f // nsh, 1408, LANES)
    npp = f // tn_up

    def add(arrs, name):
        for a, p in zip(arrs, _add_halves(arrs, gbs, [lands[a] for a in arrs], c_arr, name)):
            ps[a] = p

    def summed(a):
        rs[a] = _sum_chips(a, ps[a], lands2[a], _shard_shape(a, full_shapes[a]), jc_arr, "sum_chips_%d" % a)

    (gbs[3],), _ = _wgrad(act, dx2b, "wgrad_down", tr=tn_up, tn=d)
    (da,), (lands[3],) = _ffn_bwd_da(dx2b, w_down_f, _halves_comm([3], gbs))
    add([3], "add_halves_down")
    (du0, g_ffn_cw_p, g_ffn_cb_p), (lands2[3],) = _ffn_mid_bwd(da, u0, ua, ffn_cw_p, nseq, _chips_comm([3], ps))
    summed(3)
    (gbs[2],), (rs[3],) = _wgrad(h2, du0, "wgrad_up", tr=d, tn=tn_up, b_plane_of=lambda n: (n // npp, n % npp),
                                 out_shards=nsh, comm=_result_comm([3], rs))
    (dx1, rhs3, g_norm_ffn), (lands[2],) = _ffn_bwd_dx1(du0, w_up_f, x1, dx2, norm_ffn, 3, _halves_comm([2], gbs))
    add([2], "add_halves_up")
    (rhs3, dz, dpq), (lands2[2],) = _mixer_bwd(rhs3, z, ypc, w3_f, _chips_comm([2], ps, part=(0, 1, 2)))
    (gbs[1],), (lands2[2],) = _wgrad3(lhs3, rhs3, _chips_comm([2], ps, part=(1, 2, 2), into=lands2))
    summed(2)
    (dz, g_conv_w), (lands[1], rs[2]) = _conv_bwd(dz, dpq, z, conv_w_f, nseq,
                                                  _merge([_halves_comm([1], gbs), _result_comm([2], rs)]))
    add([1], "add_halves_sq3")
    (dz, g_pool_w, g_pool_scale), _ = _pool_bwd_call(dz, dpq, z, pool_w_f, pool_scale, nseq)
    gbs[4] = g_pool_w.astype(BF16)
    (gbs[0],), (lands2[1],) = _wgrad_in(h1, dz, nsh, _chips_comm([1], ps))
    summed(1)
    lands[0], lands[4] = _run_comm(_halves_comm([0, 4], gbs), "exchange_halves_in")
    add([0, 4], "add_halves_in")
    g_ffn_cw = g_ffn_cw_p.transpose(1, 0, 2).reshape(3, 2 * f)
    small_a = _pack([g_pool_scale, g_norm_ffn, g_ffn_cb_p.reshape(1, 2 * f), g_norm_final.reshape(d), g_conv_w,
                     g_ffn_cw, jnp.pad(loss11, ((0, SUBLANES_F32 - 1), (0, LANES - 1)))])
    (grad_x, g_norm_mix), (lands2[0], lands2[4], rs[1], slots_a) = _mixer_bwd_dx(
        dz, w_in_f, x2d, dx1, norm_mix,
        _merge([_chips_comm([0, 4], ps), _result_comm([1], rs), _small_comm(small_a)]))
    summed(0)
    summed(4)
    rs[0], rs[4], slots_b = _run_comm(_merge([_result_comm([0, 4], rs), _small_comm(_pack([g_norm_mix]))]),
                                      "exchange_result_in")
    shapes_a = [(1, d), (1, d), (1, 2 * f), (d,), (3, d), (3, 2 * f), (SUBLANES_F32, LANES)]
    gs_pool_scale, gs_norm_ffn, gs_ffn_cb, gs_norm_final, gs_conv_w, gs_ffn_cw, loss_blk = _unpack(
        _sum8(slots_a, "sum8_grads"), shapes_a)
    (gs_norm_mix,) = _unpack(_sum8(slots_b, "sum8_norm_mix"), [(1, d)])
    gs_conv_w = lax.dynamic_slice(gs_conv_w, (zero, xy * (d // 4)), (3, d // 4))
    gs_ffn_cw = lax.dynamic_slice(gs_ffn_cw, (zero, xy * (f // 2)), (3, f // 2))

    def upd(w, g, m, v, name, g_plane=None):
        shape = w.shape
        rows = 1
        for q in shape[:-1]:
            rows *= q
        g2 = g if g_plane is not None else g.reshape(rows, shape[-1])
        outs = _adamw(w.reshape(rows, shape[-1]), g2, m.reshape(rows, shape[-1]), v.reshape(rows, shape[-1]),
                      name, g_plane)
        return [o.reshape(shape) for o in outs]

    res = {
        "w_in": upd(w_in, rs[0], m_w_in, v_w_in, "adamw_w_in"),
        "pool_w": upd(pool_w, rs[4], m_pool_w, v_pool_w, "adamw_pool_w"),
        "w_pool_proj": upd(w_pool_proj, rs[1], m_w_pool_proj, v_w_pool_proj, "adamw_w_pool_proj", 0),
        "w_conv_out": upd(w_conv_out, rs[1], m_w_conv_out, v_w_conv_out, "adamw_w_conv_out", 1),
        "w_o": upd(w_o, rs[1], m_w_o, v_w_o, "adamw_w_o", 2),
        "w_up": upd(w_up, rs[2], m_w_up, v_w_up, "adamw_w_up"),
        "w_down": upd(w_down, rs[3], m_w_down, v_w_down, "adamw_w_down"),
    }

    small_names = ["norm_mix", "pool_scale", "norm_ffn", "ffn_conv_b", "norm_final", "conv_w", "ffn_conv_w"]
    small_ws = [norm_mix, pool_scale, norm_ffn, ffn_conv_b, norm_final, conv_w, ffn_conv_w]
    small_ms = [m_norm_mix, m_pool_scale, m_norm_ffn, m_ffn_conv_b, m_norm_final, m_conv_w, m_ffn_conv_w]
    small_vs = [v_norm_mix, v_pool_scale, v_norm_ffn, v_ffn_conv_b, v_norm_final, v_conv_w, v_ffn_conv_w]
    small_gs = [gs_norm_mix, gs_pool_scale, gs_norm_ffn, gs_ffn_cb, gs_norm_final, gs_conv_w, gs_ffn_cw]
    _, sd, sm, sv = _adamw(_pack(small_ws), _pack(small_gs), _pack(small_ms), _pack(small_vs), "adamw_small")
    shapes = [w.shape for w in small_ws]
    sd, sm, sv = _unpack(sd, shapes), _unpack(sm, shapes), _unpack(sv, shapes)
    for i, nm in enumerate(small_names):
        res[nm] = [small_gs[i].reshape(shapes[i]), sd[i], sm[i], sv[i]]

    order = ["norm_mix", "w_in", "pool_w", "pool_scale", "w_pool_proj", "conv_w", "w_conv_out", "w_o", "norm_ffn",
             "w_up", "ffn_conv_w", "ffn_conv_b", "w_down", "norm_final"]
    return (loss_blk[0, 0], grad_x.reshape(x.shape), *[res[n][0] for n in order], *[res[n][1] for n in order],
            *[res[n][2] for n in order], *[res[n][3] for n in order])
```

```python
import math

import jax
import jax.numpy as jnp
from jax import lax
from jax.experimental import pallas as pl
from jax.experimental.pallas import tpu as pltpu

F32 = jnp.float32
BF16 = jnp.bfloat16
SDS = jax.ShapeDtypeStruct
MESH = pl.DeviceIdType.MESH

RMS_EPS = 1e-6
POOL_WINDOWS = (2, 4, 8, 16)
N_GROUPS = len(POOL_WINDOWS)
N_SPLITS = 6

ADAM_LR = 0.001
ADAM_B1 = 0.9
ADAM_B2 = 0.999
ADAM_EPS = 1e-08
ADAM_WD = 0.01
ADAM_STEP = 10

LANES = 128
SUBLANES_F32 = 8
SUBLANES_BF16 = 16
VMEM_BYTES = 64 * 1024 * 1024
VMEM_CAP = VMEM_BYTES - 8 * 1024 * 1024
VMEM_FLOOR = 16 * 1024 * 1024

ANY = pl.BlockSpec(memory_space=pl.ANY)


def _tile(dim, pref, align):
    if dim <= pref:
        return dim
    t = (pref // align) * align
    while t >= align:
        if dim % t == 0:
            return t
        t -= align
    return dim


def _nbytes(shape, dtype):
    n = 1
    for s in shape:
        n *= s
    return n * jnp.dtype(dtype).itemsize


def _params(sem, block_bytes, temp_bytes=0, collective_id=None):
    need = 2 * block_bytes + temp_bytes + 4 * 1024 * 1024
    return pltpu.CompilerParams(dimension_semantics=sem, collective_id=collective_id,
                                vmem_limit_bytes=int(min(max(need, VMEM_FLOOR), VMEM_CAP)))


SIBLING = (0, 0, 1)
CHIPS = ((1, 0, 0), (0, 1, 0), (1, 1, 0))
EVERYONE = tuple((a, b, c) for a in range(2) for b in range(2) for c in range(2) if a + b + c)
PEER_SETS = (frozenset([SIBLING]), frozenset(CHIPS), frozenset(CHIPS + (SIBLING,)), frozenset(EVERYONE))
MID_AT = 0.75


def _collective_id(peers):
    return PEER_SETS.index(frozenset(peers))


def _handshake(peers):
    x, y, c = lax.axis_index("x"), lax.axis_index("y"), lax.axis_index("c")
    bar = pltpu.get_barrier_semaphore()
    for fx, fy, fc in sorted(peers):
        dev = (1 - x if fx else x, 1 - y if fy else y, 1 - c if fc else c)
        pl.semaphore_signal(bar, inc=1, device_id=dev, device_id_type=MESH)
    pl.semaphore_wait(bar, len(peers))


class _Comm:
    def __init__(self, ins, out_shapes, sems, start, finish, peers, aliases=None, mid=None):
        self.ins = list(ins)
        self.out_shapes = list(out_shapes)
        self.sems = list(sems)
        self.start = start
        self.finish = finish
        self.mid = mid
        self.peers = frozenset(peers)
        self.aliases = dict(aliases or {})


def _pcall(body, *, name, grid, in_specs, out_specs, out_shape, sem, blocks, temps=0, scratch_shapes=(),
           input_output_aliases=None, comm=None):
    in_specs = list(in_specs)
    out_specs = list(out_specs)
    out_shape = list(out_shape)
    scratch_shapes = list(scratch_shapes)
    aliases = dict(input_output_aliases or {})
    n_in, n_out, n_scr = len(in_specs), len(out_shape), len(scratch_shapes)
    if comm is None:
        call = pl.pallas_call(
            body, name=name, grid=grid, in_specs=in_specs, out_specs=out_specs, out_shape=out_shape,
            scratch_shapes=scratch_shapes, input_output_aliases=aliases,
            compiler_params=_params(sem, blocks, temps))
        return lambda *args: (list(call(*args)), [])

    nci, nco = len(comm.ins), len(comm.out_shapes)
    n_steps = 1
    for g in grid:
        n_steps *= g

    def hosted(*refs):
        ins = refs[:n_in]
        cins = refs[n_in:n_in + nci]
        outs = refs[n_in + nci:n_in + nci + n_out]
        couts = refs[n_in + nci + n_out:n_in + nci + n_out + nco]
        scr = refs[n_in + nci + n_out + nco:n_in + nci + n_out + nco + n_scr]
        csems = refs[n_in + nci + n_out + nco + n_scr:]
        first = None
        last = None
        step = 0
        for q, g in enumerate(grid):
            pid = pl.program_id(q)
            first = (pid == 0) if first is None else first & (pid == 0)
            last = (pid == g - 1) if last is None else last & (pid == g - 1)
            step = step * g + pid

        @pl.when(first)
        def _():
            _handshake(comm.peers)
            comm.start(cins, couts, csems)

        if comm.mid is not None:
            @pl.when(step == int(MID_AT * n_steps))
            def _():
                comm.mid(cins, couts, csems)

        body(*ins, *outs, *scr)

        @pl.when(last)
        def _():
            comm.finish(cins, couts, csems)

    for i, o in comm.aliases.items():
        aliases[n_in + i] = n_out + o
    call = pl.pallas_call(
        hosted, name=name, grid=grid, in_specs=in_specs + [ANY] * nci, out_specs=out_specs + [ANY] * nco,
        out_shape=out_shape + comm.out_shapes, scratch_shapes=scratch_shapes + comm.sems,
        input_output_aliases=aliases,
        compiler_params=_params(("arbitrary",) * len(grid), blocks, temps, _collective_id(comm.peers)))

    def run(*args):
        res = call(*args, *comm.ins)
        return list(res[:n_out]), list(res[n_out:])

    return run


def _run_comm(comm, name):
    def body(*refs):
        nci, nco = len(comm.ins), len(comm.out_shapes)
        cins, couts, csems = refs[:nci], refs[nci:nci + nco], refs[nci + nco:]
        _handshake(comm.peers)
        comm.start(cins, couts, csems)
        if comm.mid is not None:
            comm.mid(cins, couts, csems)
        comm.finish(cins, couts, csems)

    return list(pl.pallas_call(
        body, name=name, in_specs=[ANY] * len(comm.ins), out_specs=[ANY] * len(comm.out_shapes),
        out_shape=comm.out_shapes, scratch_shapes=comm.sems, input_output_aliases=comm.aliases,
        compiler_params=pltpu.CompilerParams(collective_id=_collective_id(comm.peers)),
    )(*comm.ins))


def _dot(a, b):
    return jnp.dot(a, b, preferred_element_type=F32)


def _dot_tb(a, b):
    return lax.dot_general(a, b, (((1,), (1,)), ((), ())), preferred_element_type=F32)


def _dot_ta(a, b):
    return lax.dot_general(a, b, (((0,), (0,)), ((), ())), preferred_element_type=F32)


def _rms_fwd(x):
    inv = lax.rsqrt(jnp.mean(x * x, axis=-1, keepdims=True) + RMS_EPS)
    return x * inv, inv


def _rms_bwd(dy, xhat, inv, g):
    gd = dy * g
    return inv * (gd - xhat * jnp.mean(gd * xhat, axis=-1, keepdims=True))


def _sigmoid(x):
    return 1.0 / (1.0 + jnp.exp(-x))


def _shift_down(x, k, row):
    return jnp.where(row >= k, pltpu.roll(x, k, 0), 0.0)


def _shift_up(x, k, row):
    s = x.shape[0]
    return jnp.where(row < s - k, pltpu.roll(x, s - k, 0), 0.0)


def _pool_fwd(u, win, row):
    s = u
    k = 1
    while k < win:
        s = s + _shift_down(s, k, row)
        k *= 2
    cnt = jnp.minimum(row + 1, win).astype(F32)
    return s / cnt - u


def _pool_bwd(dp, win, row):
    cnt = jnp.minimum(row + 1, win).astype(F32)
    s = dp / cnt
    k = 1
    while k < win:
        s = s + _shift_up(s, k, row)
        k *= 2
    return s - dp


def _acc_over(k, nk, part, acc, o_ref):
    @pl.when(k == 0)
    def _():
        acc[...] = part

    @pl.when(k > 0)
    def _():
        acc[...] += part

    @pl.when(k == nk - 1)
    def _():
        o_ref[...] = acc[...].astype(o_ref.dtype)


def _fwd_in(x, g, w_loc, order, comm):
    t, d = x.shape
    ws = w_loc.shape[1]
    nsh = order.shape[0]
    assert nsh == 4, "the shard walk below is written for the 2 x 2 chips of the mesh"
    tm = _tile(t, 1024, SUBLANES_BF16)
    ni = t // tm
    nci, nco = len(comm.ins), len(comm.out_shapes)
    all_peers = comm.peers | frozenset(CHIPS + (SIBLING,))

    def body(order_ref, x_ref, g_ref, loc_ref, *rest):
        del order_ref
        cins = rest[:nci]
        z_ref, h_ref, full_ref = rest[nci:nci + 3]
        couts = rest[nci + 3:nci + 3 + nco]
        (hs, wbuf, wsem, own_s, own_r, snd_s, snd_r, fwd_s, fwd_r, rly_s, rly_r) = rest[nci + 3 + nco:nci + 14 + nco]
        csems = rest[nci + 14 + nco:]
        j = pl.program_id(0)
        i = pl.program_id(1)
        x_, y_, c_ = _coords()
        own = 2 * x_ + y_
        sib = (x_, y_, 1 - c_)
        peers = _peer_chips(x_, y_)

        def sends():
            return [_remote(_half(0, loc_ref, c_), _piece(0, full_ref, own, c_), snd_s.at[p], snd_r.at[p], (px, py, c_))
                    for p, (px, py) in enumerate(peers[:2])]

        def relays():
            out = []
            for q, (src_p, dst_p) in enumerate(((0, 1), (1, 0))):
                sx, sy = peers[src_p]
                part = _rows_part(0, _piece(0, full_ref, 2 * sx + sy, c_), (q, q + 1, 2))
                out.append(_remote(part, part, rly_s.at[q], rly_r.at[q], (*peers[dst_p], c_)))
            return out

        def owns():
            return [_remote(_half(0, loc_ref, h), _piece(0, full_ref, own, h), own_s.at[h], own_r.at[h], sib)
                    for h in range(2)]

        def forward(p, half):
            px, py = peers[p]
            landed = _piece(0, full_ref, 2 * px + py, half)
            return _remote(landed, landed, fwd_s.at[p], fwd_r.at[p], sib)

        def load(src, slot):
            return pltpu.make_async_copy(src, wbuf.at[slot], wsem.at[slot])

        @pl.when((j == 0) & (i == 0))
        def _():
            _handshake(all_peers)
            for cp in sends() + owns():
                cp.start()
            load(loc_ref, 0).start()

        @pl.when(j == 0)
        def _():
            xh, _ = _rms_fwd(x_ref[...])
            h = (xh * g_ref[...]).astype(BF16)
            hs[pl.ds(pl.multiple_of(i * tm, tm), tm), :] = h
            h_ref[...] = h

        slot = j % 2

        @pl.when(i == 0)
        def _():
            load(loc_ref, slot).wait()

        z_ref[...] = _dot(hs[pl.ds(pl.multiple_of(i * tm, tm), tm), :], wbuf[slot]).astype(BF16)

        def load_shard(p, into):
            px, py = peers[p]
            forward(p, 1 - c_).wait_recv()
            load(full_ref.at[2 * px + py], into).start()

        @pl.when((j == 0) & (i == ni - 1))
        def _():
            for cp in sends():
                cp.wait_recv()
            for cp in relays() + [forward(0, c_), forward(1, c_)]:
                cp.start()
            load_shard(0, 1)
            comm.start(cins, couts, csems)

        @pl.when((j == 1) & (i == 0))
        def _():
            load_shard(1, 0)

        @pl.when((j == 2) & (i == max(ni - 2, 0)))
        def _():
            for cp in relays():
                cp.wait_recv()
            forward(2, c_).start()
            load_shard(2, 1)

        @pl.when((j == nsh - 1) & (i == ni - 1))
        def _():
            for cp in sends() + relays() + [forward(p, c_) for p in range(nsh - 1)]:
                cp.wait_send()
            for cp in owns():
                cp.wait()
            comm.finish(cins, couts, csems)

    last = ni - 1
    blocks = _nbytes((tm, d), F32) + _nbytes((tm, ws), BF16) + _nbytes((tm, d), BF16)
    scratch = _nbytes((t, d), BF16) + 2 * _nbytes((d, ws), BF16)
    res = pl.pallas_call(
        body, name="fwd_in",
        grid_spec=pltpu.PrefetchScalarGridSpec(
            num_scalar_prefetch=1, grid=(nsh, ni),
            in_specs=[pl.BlockSpec((tm, d), lambda j, i, o: (jnp.where(j == 0, i, last), 0)),
                      pl.BlockSpec((1, d), lambda j, i, o: (0, 0)), ANY] + [ANY] * nci,
            out_specs=[pl.BlockSpec((tm, ws), lambda j, i, o: (i, o[j])),
                       pl.BlockSpec((tm, d), lambda j, i, o: (jnp.where(j == 0, i, last), 0)), ANY] + [ANY] * nco,
            scratch_shapes=[pltpu.VMEM((t, d), BF16), pltpu.VMEM((2, d, ws), BF16)]
            + _dma_sems(2, 2, 2, 2, 2, nsh - 1, nsh - 1, 2, 2) + comm.sems),
        out_shape=[SDS((t, nsh * ws), BF16), SDS((t, d), BF16), SDS((nsh, d, ws), BF16)] + comm.out_shapes,
        input_output_aliases={4 + i: 3 + o for i, o in comm.aliases.items()},
        compiler_params=_params(("arbitrary", "arbitrary"), blocks, scratch + 3 * _nbytes((tm, d), F32),
                                _collective_id(all_peers)),
    )(order, x, g, w_loc, *comm.ins)
    return list(res[:3]), list(res[3:])


def _mixer_mid_fwd(z, pool_w, pool_scale, conv_w, nseq, comm=None):
    t = z.shape[0]
    d = pool_scale.shape[1]
    s = t // nseq
    c = d // N_GROUPS

    def body(zp, zb, zc, zv, pw, ps, cw, o):
        j = pl.program_id(1)
        row = lax.broadcasted_iota(jnp.int32, (s, c), 0)
        for gi, win in enumerate(POOL_WINDOWS):
            @pl.when(j == gi)
            def _(win=win):
                pooled = _pool_fwd(zp[...].astype(F32), win, row)
                o[0] = (_dot(pooled.astype(BF16), pw[...]) * ps[...]).astype(BF16)

        cv = zc[...].astype(F32) * zv[...].astype(F32)
        cc = (cw[pl.ds(2, 1), :] * cv + cw[pl.ds(1, 1), :] * _shift_down(cv, 1, row)
              + cw[pl.ds(0, 1), :] * _shift_down(cv, 2, row))
        o[1] = (zb[...].astype(F32) * cc).astype(BF16)

    blocks = 4 * _nbytes((s, c), BF16) + _nbytes((c, c), BF16) + _nbytes((2, s, c), BF16)
    return _pcall(
        body, name="mixer_mid_fwd", grid=(nseq, N_GROUPS),
        in_specs=[pl.BlockSpec((s, c), lambda b, j: (b, j)),
                  pl.BlockSpec((s, c), lambda b, j: (b, N_GROUPS + j)),
                  pl.BlockSpec((s, c), lambda b, j: (b, 2 * N_GROUPS + j)),
                  pl.BlockSpec((s, c), lambda b, j: (b, 3 * N_GROUPS + j)),
                  pl.BlockSpec((None, c, c), lambda b, j: (j, 0, 0)),
                  pl.BlockSpec((1, c), lambda b, j: (0, j)),
                  pl.BlockSpec((3, c), lambda b, j: (0, j))],
        out_specs=[pl.BlockSpec((2, s, c), lambda b, j: (0, b, j))],
        out_shape=[SDS((3, t, d), BF16)],
        sem=("parallel", "parallel"), blocks=blocks, temps=8 * _nbytes((s, c), F32), comm=comm,
    )(z, z, z, z, pool_w, pool_scale, conv_w)


def _mixer_out(lhs3, z, x, w3, g_ffn, comm=None):
    t, d = x.shape
    tm = _tile(t, 256, SUBLANES_BF16)

    def body(pq, zgp, zgc, x_ref, w_ref, g_ref, mrg, ypc, x1o, h2o):
        yp = _dot(pq[0], w_ref[0])
        yc = _dot(pq[1], w_ref[1])
        m = _sigmoid(zgp[...].astype(F32)) * yp + _sigmoid(zgc[...].astype(F32)) * yc
        mb = m.astype(BF16)
        x1 = x_ref[...] + _dot(mb, w_ref[2])
        ypc[0] = yp.astype(BF16)
        ypc[1] = yc.astype(BF16)
        mrg[...] = mb
        x1o[...] = x1
        xh, _ = _rms_fwd(x1)
        h2o[...] = (xh * g_ref[...]).astype(BF16)

    blocks = (_nbytes((2, tm, d), BF16) * 2 + _nbytes((tm, d), BF16) * 4 + _nbytes((tm, d), F32) * 2
              + _nbytes((3, d, d), BF16))
    return _pcall(
        body, name="mixer_out", grid=(t // tm,),
        in_specs=[pl.BlockSpec((2, tm, d), lambda i: (0, i, 0)),
                  pl.BlockSpec((tm, d), lambda i: (i, 4)),
                  pl.BlockSpec((tm, d), lambda i: (i, 5)),
                  pl.BlockSpec((tm, d), lambda i: (i, 0)),
                  pl.BlockSpec((3, d, d), lambda i: (0, 0, 0)),
                  pl.BlockSpec((1, d), lambda i: (0, 0))],
        out_specs=[pl.BlockSpec((None, tm, d), lambda i: (2, i, 0)),
                   pl.BlockSpec((2, tm, d), lambda i: (0, i, 0)),
                   pl.BlockSpec((tm, d), lambda i: (i, 0)),
                   pl.BlockSpec((tm, d), lambda i: (i, 0))],
        out_shape=[SDS(lhs3.shape, BF16), SDS((2, t, d), BF16), SDS((t, d), F32), SDS((t, d), BF16)],
        input_output_aliases={0: 0},
        sem=("parallel",), blocks=blocks, temps=8 * _nbytes((tm, d), F32), comm=comm,
    )(lhs3, z, z, x, w3, g_ffn)


def _ffn_up(h2, w_up, f, comm=None):
    t, d = h2.shape
    _, _, ws = w_up.shape
    tm = _tile(t, 2048, SUBLANES_BF16)
    tn = _tile(ws, 1408, LANES)
    nps = ws // tn
    npp = f // tn

    def body(h_ref, w_ref, o_ref):
        o_ref[...] = _dot(h_ref[...], w_ref[...]).astype(BF16)

    blocks = _nbytes((tm, d), BF16) + _nbytes((d, tn), BF16) + _nbytes((tm, tn), BF16)
    return _pcall(
        body, name="ffn_up", grid=(t // tm, 2 * npp),
        in_specs=[pl.BlockSpec((tm, d), lambda i, j: (i, 0)),
                  pl.BlockSpec((None, d, tn), lambda i, j: (j // nps, 0, j % nps))],
        out_specs=[pl.BlockSpec((None, tm, tn), lambda i, j: (j // npp, i, j % npp))],
        out_shape=[SDS((2, t, f), BF16)],
        sem=("parallel", "parallel"), blocks=blocks, temps=_nbytes((tm, tn), F32), comm=comm,
    )(h2, w_up)


def _conv3_rows(u, u1, u2, w_ref, p):
    return w_ref[p, pl.ds(2, 1), :] * u + w_ref[p, pl.ds(1, 1), :] * u1 + w_ref[p, pl.ds(0, 1), :] * u2


WGRAD_TOKENS = 2048
WGRAD_TOKENS_WIDE = 4096
CHUNK = 64
HALO = SUBLANES_F32


def _up1_up2(u, nxt):
    rows = u.shape[0]
    ext = jnp.concatenate([u, nxt], axis=0)
    n = rows + HALO
    return pltpu.roll(ext, n - 1, 0)[:rows], pltpu.roll(ext, n - 2, 0)[:rows]


def _fold8(x):
    return jnp.sum(x.reshape(x.shape[0] // SUBLANES_F32, SUBLANES_F32, x.shape[1]), axis=0)


def _ffn_mid_fwd(u0, cw, cb, nseq):
    _, t, f = u0.shape
    s = t // nseq
    c = _tile(f, 256, LANES)

    def body(u_ref, w_ref, b_ref, a_ref, uo_ref):
        row = lax.broadcasted_iota(jnp.int32, (s, c), 0)
        act = []
        for p in range(2):
            u = u_ref[p].astype(F32)
            act.append(_conv3_rows(u, _shift_down(u, 1, row), _shift_down(u, 2, row), w_ref, p) + b_ref[p])
            uo_ref[p] = act[p].astype(BF16)
        ug, uv = act
        a_ref[...] = (ug * _sigmoid(ug) * uv).astype(BF16)

    blocks = 2 * _nbytes((2, s, c), BF16) + _nbytes((s, c), BF16)
    outs, _ = _pcall(
        body, name="ffn_mid_fwd", grid=(f // c, nseq),
        in_specs=[pl.BlockSpec((2, s, c), lambda j, b: (0, b, j)),
                  pl.BlockSpec((2, 3, c), lambda j, b: (0, 0, j)),
                  pl.BlockSpec((2, 1, c), lambda j, b: (0, 0, j))],
        out_specs=[pl.BlockSpec((s, c), lambda j, b: (b, j)),
                   pl.BlockSpec((2, s, c), lambda j, b: (0, b, j))],
        out_shape=[SDS((t, f), BF16), SDS((2, t, f), BF16)],
        sem=("parallel", "parallel"), blocks=blocks, temps=8 * _nbytes((s, c), F32),
    )(u0, cw, cb)
    return outs


def _ffn_down_loss(a, w_down, x1, tgt, g_fin):
    t, f = a.shape
    d = x1.shape[1]
    tm = _tile(t, 256, SUBLANES_BF16)
    nsteps = t // tm

    def body(a_ref, w_ref, x1_ref, t_ref, g_ref, dx_ref, dxb_ref, loss_ref, gg_ref, lacc):
        i = pl.program_id(0)

        @pl.when(i == 0)
        def _():
            lacc[...] = jnp.zeros_like(lacc)
            gg_ref[...] = jnp.zeros_like(gg_ref)

        x2 = x1_ref[...] + _dot(a_ref[...], w_ref[...])
        xh, inv = _rms_fwd(x2)
        g = g_ref[...]
        e = xh * g - t_ref[...]
        lacc[...] += jnp.sum(e * e, axis=0, keepdims=True)
        dy = e * (1.0 / d)
        gg_ref[...] += jnp.sum(dy * xh, axis=0, keepdims=True)
        dx2 = _rms_bwd(dy, xh, inv, g)
        dx_ref[...] = dx2
        dxb_ref[...] = dx2.astype(BF16)

        @pl.when(i == nsteps - 1)
        def _():
            loss_ref[...] = jnp.sum(lacc[...], axis=1, keepdims=True) * (0.5 / d)

    blocks = (_nbytes((tm, f), BF16) + _nbytes((f, d), BF16) + 3 * _nbytes((tm, d), F32) + _nbytes((tm, d), BF16))
    outs, _ = _pcall(
        body, name="ffn_down_loss", grid=(nsteps,),
        in_specs=[pl.BlockSpec((tm, f), lambda i: (i, 0)), pl.BlockSpec((f, d), lambda i: (0, 0)),
                  pl.BlockSpec((tm, d), lambda i: (i, 0)), pl.BlockSpec((tm, d), lambda i: (i, 0)),
                  pl.BlockSpec((1, d), lambda i: (0, 0))],
        out_specs=[pl.BlockSpec((tm, d), lambda i: (i, 0)), pl.BlockSpec((tm, d), lambda i: (i, 0)),
                   pl.BlockSpec((1, 1), lambda i: (0, 0)), pl.BlockSpec((1, d), lambda i: (0, 0))],
        out_shape=[SDS((t, d), F32), SDS((t, d), BF16), SDS((1, 1), F32), SDS((1, d), F32)],
        scratch_shapes=[pltpu.VMEM((1, d), F32)],
        sem=("arbitrary",), blocks=blocks, temps=8 * _nbytes((tm, d), F32),
    )(a, w_down, x1, tgt, g_fin)
    return outs


def _ffn_bwd_da(dxb, w_down, comm=None):
    t, d = dxb.shape
    f = w_down.shape[0]
    tm = _tile(t, 1024, SUBLANES_BF16)

    def body(x_ref, w_ref, o_ref):
        o_ref[...] = _dot_tb(x_ref[...], w_ref[...]).astype(BF16)

    blocks = _nbytes((tm, d), BF16) + _nbytes((tm, f), BF16)
    return _pcall(
        body, name="ffn_bwd_da", grid=(t // tm,),
        in_specs=[pl.BlockSpec((tm, d), lambda i: (i, 0)),
                  pl.BlockSpec((f, d), lambda i: (0, 0), pipeline_mode=pl.Buffered(1))],
        out_specs=[pl.BlockSpec((tm, f), lambda i: (i, 0))],
        out_shape=[SDS((t, f), BF16)],
        sem=("parallel",), blocks=blocks, temps=_nbytes((f, d), BF16) + _nbytes((tm, f), F32), comm=comm,
    )(dxb, w_down)


def _ffn_mid_bwd(da, u0, ua, cw, nseq, comm=None):
    _, t, f = u0.shape
    s = t // nseq
    c = _tile(f, 128, LANES)
    r = _tile(s, CHUNK, SUBLANES_BF16)
    n = s // r

    def body(da_ref, u_ref, ua_ref, w_ref, du_ref, gw_ref, gb_ref):
        @pl.when(pl.program_id(1) == 0)
        def _():
            gw_ref[...] = jnp.zeros_like(gw_ref)
            gb_ref[...] = jnp.zeros_like(gb_ref)

        def step(i, carry):
            nxt, sums = carry
            rows = pl.ds(pl.multiple_of((n - 1 - i) * r, r), r)
            ug = ua_ref[0, rows, :].astype(F32)
            uv = ua_ref[1, rows, :].astype(F32)
            sg = _sigmoid(ug)
            dacc = da_ref[rows, :].astype(F32)
            dus = (dacc * uv * sg * (1.0 + ug * (1.0 - sg)), dacc * (ug * sg))
            first, new_sums = [], []
            for p in range(2):
                du = dus[p]
                d1, d2 = _up1_up2(du, nxt[p])
                du_ref[p, rows, :] = _conv3_rows(du, d1, d2, w_ref, p).astype(BF16)
                u = u_ref[p, rows, :].astype(F32)
                sb, s0, s1, s2 = sums[p]
                new_sums.append((sb + _fold8(du), s0 + _fold8(d2 * u), s1 + _fold8(d1 * u), s2 + _fold8(du * u)))
                first.append(du[:HALO])
            return tuple(first), tuple(new_sums)

        zero = jnp.zeros((HALO, c), F32)
        _, sums = lax.fori_loop(0, n, step, ((zero, zero), ((zero,) * 4,) * 2))
        for p in range(2):
            sb, s0, s1, s2 = sums[p]
            gb_ref[p] += jnp.sum(sb, axis=0, keepdims=True)
            gw_ref[p, pl.ds(0, 1), :] += jnp.sum(s0, axis=0, keepdims=True)
            gw_ref[p, pl.ds(1, 1), :] += jnp.sum(s1, axis=0, keepdims=True)
            gw_ref[p, pl.ds(2, 1), :] += jnp.sum(s2, axis=0, keepdims=True)

    blocks = _nbytes((s, c), BF16) + 3 * _nbytes((2, s, c), BF16)
    return _pcall(
        body, name="ffn_mid_bwd", grid=(f // c, nseq),
        in_specs=[pl.BlockSpec((s, c), lambda j, b: (b, j)),
                  pl.BlockSpec((2, s, c), lambda j, b: (0, b, j)),
                  pl.BlockSpec((2, s, c), lambda j, b: (0, b, j)),
                  pl.BlockSpec((2, 3, c), lambda j, b: (0, 0, j))],
        out_specs=[pl.BlockSpec((2, s, c), lambda j, b: (0, b, j)),
                   pl.BlockSpec((2, 3, c), lambda j, b: (0, 0, j)),
                   pl.BlockSpec((2, 1, c), lambda j, b: (0, 0, j))],
        out_shape=[SDS((2, t, f), BF16), SDS((2, 3, f), F32), SDS((2, 1, f), F32)],
        sem=("parallel", "arbitrary"), blocks=blocks, temps=4 * 1024 * 1024, comm=comm,
    )(da, u0, ua, cw)


def _wgrad(a, b, name, *, tr, tn, b_plane_of=None, out_shards=None, comm=None):
    t, m = a.shape
    n_total = b.shape[-1] * (b.shape[0] if b.ndim == 3 else 1)
    tk = _tile(t, WGRAD_TOKENS_WIDE if n_total > tn and m == tr else WGRAD_TOKENS, SUBLANES_BF16)
    nk = t // tk
    once = pl.Buffered(1) if nk == 1 else None

    def body(a_ref, b_ref, o_ref, *acc):
        part = _dot_ta(a_ref[...], b_ref[...])
        if nk == 1:
            o_ref[...] = part.astype(BF16)
        else:
            _acc_over(pl.program_id(2), nk, part, acc[0], o_ref)

    if b.ndim == 3:
        b_spec = pl.BlockSpec((None, tk, tn), lambda r, n, k: (b_plane_of(n)[0], k, b_plane_of(n)[1]))
    else:
        b_spec = pl.BlockSpec((tk, tn), lambda r, n, k: (k, n), pipeline_mode=once if n_total == tn else None)
    if out_shards is None:
        o_spec = pl.BlockSpec((tr, tn), lambda r, n, k: (r, n))
        o_shape = SDS((m, n_total), BF16)
    else:
        nps = n_total // out_shards // tn
        o_spec = pl.BlockSpec((None, tr, tn), lambda r, n, k: (n // nps, r, n % nps))
        o_shape = SDS((out_shards, m, n_total // out_shards), BF16)
    blocks = _nbytes((tk, tr), BF16) + _nbytes((tk, tn), BF16) + _nbytes((tr, tn), BF16)
    return _pcall(
        body, name=name, grid=(m // tr, n_total // tn, nk),
        in_specs=[pl.BlockSpec((tk, tr), lambda r, n, k: (k, r), pipeline_mode=once if m == tr else None), b_spec],
        out_specs=[o_spec], out_shape=[o_shape],
        scratch_shapes=[] if nk == 1 else [pltpu.VMEM((tr, tn), F32)],
        sem=("parallel", "parallel", "arbitrary"), blocks=blocks, temps=2 * _nbytes((tr, tn), F32), comm=comm,
    )(a, b)


def _wgrad3(lhs3, rhs3, comm=None):
    nw, t, d = lhs3.shape
    tk = _tile(t, WGRAD_TOKENS, SUBLANES_BF16)
    nk = t // tk

    def body(a_ref, b_ref, o_ref, *acc):
        part = _dot_ta(a_ref[...], b_ref[...])
        if nk == 1:
            o_ref[...] = part.astype(BF16)
        else:
            _acc_over(pl.program_id(1), nk, part, acc[0], o_ref)

    blocks = 2 * _nbytes((tk, d), BF16) + _nbytes((d, d), BF16)
    return _pcall(
        body, name="wgrad_sq3", grid=(nw, nk),
        in_specs=[pl.BlockSpec((None, tk, d), lambda w, k: (w, k, 0)),
                  pl.BlockSpec((None, tk, d), lambda w, k: (w, k, 0))],
        out_specs=[pl.BlockSpec((None, d, d), lambda w, k: (w, 0, 0))],
        out_shape=[SDS((nw, d, d), BF16)],
        scratch_shapes=[] if nk == 1 else [pltpu.VMEM((d, d), F32)],
        sem=("parallel", "arbitrary"), blocks=blocks, temps=2 * _nbytes((d, d), F32), comm=comm,
    )(lhs3, rhs3)


def _ffn_bwd_dx1(du0, w_up, x1, dx2, g_ffn, n_planes_out, comm=None):
    _, t, f = du0.shape
    d = x1.shape[1]
    nsh, _, ws = w_up.shape
    tm = _tile(t, 256, SUBLANES_BF16)
    spp = f // ws

    def body(du_ref, w_ref, x1_ref, dx2_ref, g_ref, dx1_ref, dxb_ref, gg_ref):
        @pl.when(pl.program_id(0) == 0)
        def _():
            gg_ref[...] = jnp.zeros_like(gg_ref)

        dh = None
        for k in range(nsh):
            part = _dot_tb(du_ref[k // spp, :, (k % spp) * ws:(k % spp + 1) * ws], w_ref[k])
            dh = part if dh is None else dh + part
        xh, inv = _rms_fwd(x1_ref[...])
        gg_ref[...] += jnp.sum(dh * xh, axis=0, keepdims=True)
        dx1 = dx2_ref[...] + _rms_bwd(dh, xh, inv, g_ref[...])
        dx1_ref[...] = dx1
        dxb_ref[...] = dx1.astype(BF16)

    blocks = _nbytes((2, tm, f), BF16) + 3 * _nbytes((tm, d), F32) + _nbytes((tm, d), BF16)
    return _pcall(
        body, name="ffn_bwd_dx1", grid=(t // tm,),
        in_specs=[pl.BlockSpec((2, tm, f), lambda i: (0, i, 0)),
                  pl.BlockSpec((nsh, d, ws), lambda i: (0, 0, 0), pipeline_mode=pl.Buffered(1)),
                  pl.BlockSpec((tm, d), lambda i: (i, 0)),
                  pl.BlockSpec((tm, d), lambda i: (i, 0)),
                  pl.BlockSpec((1, d), lambda i: (0, 0))],
        out_specs=[pl.BlockSpec((tm, d), lambda i: (i, 0)),
                   pl.BlockSpec((None, tm, d), lambda i: (n_planes_out - 1, i, 0)),
                   pl.BlockSpec((1, d), lambda i: (0, 0))],
        out_shape=[SDS((t, d), F32), SDS((n_planes_out, t, d), BF16), SDS((1, d), F32)],
        sem=("arbitrary",), blocks=blocks, temps=_nbytes(w_up.shape, BF16) + 8 * _nbytes((tm, d), F32), comm=comm,
    )(du0, w_up, x1, dx2, g_ffn)


def _mixer_bwd(rhs3, z, ypc, w3, comm=None):
    _, t, d = rhs3.shape
    tm = _tile(t, 512, SUBLANES_BF16)

    def body(dx_ref, zgp, zgc, ypc_ref, w_ref, dyo, dzo, dpq):
        dm = _dot_tb(dx_ref[...], w_ref[2])
        sp = _sigmoid(zgp[...].astype(F32))
        sc = _sigmoid(zgc[...].astype(F32))
        dyp = (dm * sp).astype(BF16)
        dyc = (dm * sc).astype(BF16)
        dzo[0] = (dm * ypc_ref[0].astype(F32) * sp * (1.0 - sp)).astype(BF16)
        dzo[1] = (dm * ypc_ref[1].astype(F32) * sc * (1.0 - sc)).astype(BF16)
        dyo[0] = dyp
        dyo[1] = dyc
        dpq[0] = _dot_tb(dyp, w_ref[0]).astype(BF16)
        dpq[1] = _dot_tb(dyc, w_ref[1]).astype(BF16)

    blocks = _nbytes((tm, d), BF16) * 3 + _nbytes((2, tm, d), BF16) * 4 + _nbytes((3, d, d), BF16)
    return _pcall(
        body, name="mixer_bwd", grid=(t // tm,),
        in_specs=[pl.BlockSpec((None, tm, d), lambda i: (2, i, 0)),
                  pl.BlockSpec((tm, d), lambda i: (i, 4)),
                  pl.BlockSpec((tm, d), lambda i: (i, 5)),
                  pl.BlockSpec((2, tm, d), lambda i: (0, i, 0)),
                  pl.BlockSpec((3, d, d), lambda i: (0, 0, 0))],
        out_specs=[pl.BlockSpec((2, tm, d), lambda i: (0, i, 0)),
                   pl.BlockSpec((2, tm, d), lambda i: (2, i, 0)),
                   pl.BlockSpec((2, tm, d), lambda i: (0, i, 0))],
        out_shape=[SDS(rhs3.shape, BF16), SDS((N_SPLITS, t, d), BF16), SDS((2, t, d), BF16)],
        input_output_aliases={0: 0},
        sem=("parallel",), blocks=blocks, temps=8 * _nbytes((tm, d), F32), comm=comm,
    )(rhs3, z, z, ypc, w3)


def _conv_bwd(dz, dpq, z, conv_w, nseq, comm=None):
    _, t, d = dz.shape
    s = t // nseq
    c = _tile(d, 256, LANES)
    nb = d // c

    def body(dz_in, dq_ref, zb, zc, zv, cw, dzo, gw_ref):
        del dz_in

        @pl.when(pl.program_id(1) == 0)
        def _():
            gw_ref[...] = jnp.zeros_like(gw_ref)

        row = lax.broadcasted_iota(jnp.int32, (s, c), 0)
        b = zb[...].astype(F32)
        cm = zc[...].astype(F32)
        v = zv[...].astype(F32)
        cv = cm * v
        cv1 = _shift_down(cv, 1, row)
        cv2 = _shift_down(cv, 2, row)
        w0, w1, w2 = cw[pl.ds(0, 1), :], cw[pl.ds(1, 1), :], cw[pl.ds(2, 1), :]
        cc = w2 * cv + w1 * cv1 + w0 * cv2
        dq = dq_ref[...].astype(F32)
        dzo[0] = (dq * cc).astype(BF16)
        dcc = dq * b
        gw_ref[pl.ds(0, 1), :] += jnp.sum(dcc * cv2, axis=0, keepdims=True)
        gw_ref[pl.ds(1, 1), :] += jnp.sum(dcc * cv1, axis=0, keepdims=True)
        gw_ref[pl.ds(2, 1), :] += jnp.sum(dcc * cv, axis=0, keepdims=True)
        dcv = w2 * dcc + w1 * _shift_up(dcc, 1, row) + w0 * _shift_up(dcc, 2, row)
        dzo[1] = (dcv * v).astype(BF16)
        dzo[2] = (dcv * cm).astype(BF16)

    blocks = 4 * _nbytes((s, c), BF16) + _nbytes((3, s, c), BF16)
    return _pcall(
        body, name="conv_bwd", grid=(nb, nseq),
        in_specs=[ANY,
                  pl.BlockSpec((None, s, c), lambda j, b: (1, b, j)),
                  pl.BlockSpec((s, c), lambda j, b: (b, nb + j)),
                  pl.BlockSpec((s, c), lambda j, b: (b, 2 * nb + j)),
                  pl.BlockSpec((s, c), lambda j, b: (b, 3 * nb + j)),
                  pl.BlockSpec((3, c), lambda j, b: (0, j))],
        out_specs=[pl.BlockSpec((3, s, c), lambda j, b: (0, b, j)),
                   pl.BlockSpec((3, c), lambda j, b: (0, j))],
        out_shape=[SDS(dz.shape, BF16), SDS((3, d), F32)],
        input_output_aliases={0: 0},
        sem=("parallel", "arbitrary"), blocks=blocks, temps=16 * _nbytes((s, c), F32), comm=comm,
    )(dz, dpq, z, z, z, conv_w)


def _pool_bwd_call(dz, dpq, z, pool_w, pool_scale, nseq, comm=None):
    _, t, d = dz.shape
    s = t // nseq
    c = d // N_GROUPS

    def body(dz_in, dp_ref, zp, pw, ps, dzo, gpw_ref, gps_ref):
        del dz_in
        j = pl.program_id(0)

        @pl.when(pl.program_id(1) == 0)
        def _():
            gpw_ref[...] = jnp.zeros_like(gpw_ref)
            gps_ref[...] = jnp.zeros_like(gps_ref)

        row = lax.broadcasted_iota(jnp.int32, (s, c), 0)
        for gi, win in enumerate(POOL_WINDOWS):
            @pl.when(j == gi)
            def _(win=win):
                pb = _pool_fwd(zp[...].astype(F32), win, row).astype(BF16)
                plin = _dot(pb, pw[...])
                dps = dp_ref[...].astype(F32)
                gps_ref[...] += jnp.sum(dps * plin, axis=0, keepdims=True)
                dplb = (dps * ps[...]).astype(BF16)
                gpw_ref[...] += _dot_ta(pb, dplb)
                dzo[...] = _pool_bwd(_dot_tb(dplb, pw[...]), win, row).astype(BF16)

    blocks = 3 * _nbytes((s, c), BF16) + _nbytes((c, c), BF16) + _nbytes((c, c), F32)
    return _pcall(
        body, name="pool_bwd", grid=(N_GROUPS, nseq),
        in_specs=[ANY,
                  pl.BlockSpec((None, s, c), lambda j, b: (0, b, j)),
                  pl.BlockSpec((s, c), lambda j, b: (b, j)),
                  pl.BlockSpec((None, c, c), lambda j, b: (j, 0, 0)),
                  pl.BlockSpec((1, c), lambda j, b: (0, j))],
        out_specs=[pl.BlockSpec((None, s, c), lambda j, b: (3, b, j)),
                   pl.BlockSpec((None, c, c), lambda j, b: (j, 0, 0)),
                   pl.BlockSpec((1, c), lambda j, b: (0, j))],
        out_shape=[SDS(dz.shape, BF16), SDS((N_GROUPS, c, c), F32), SDS((1, d), F32)],
        input_output_aliases={0: 0},
        sem=("parallel", "arbitrary"), blocks=blocks, temps=10 * _nbytes((s, c), F32), comm=comm,
    )(dz, dpq, z, pool_w, pool_scale)


def _dz_plane(zb):
    return jnp.where(zb < 4, (zb + 3) % 4, zb)


def _wgrad_in(h1, dz, nsh, comm=None):
    t, d = h1.shape
    ws = N_SPLITS * d // nsh
    kb = _tile(math.gcd(d, ws), 512, LANES)
    npl = d // kb
    nps = ws // kb
    tk = _tile(t, WGRAD_TOKENS_WIDE, SUBLANES_BF16)
    nk = t // tk

    def body(a_ref, b_ref, o_ref, *acc):
        part = _dot_ta(a_ref[...], b_ref[...])
        if nk == 1:
            o_ref[...] = part.astype(BF16)
        else:
            _acc_over(pl.program_id(1), nk, part, acc[0], o_ref)

    blocks = _nbytes((tk, d), BF16) + _nbytes((tk, kb), BF16) + _nbytes((d, kb), BF16)
    return _pcall(
        body, name="wgrad_in", grid=(N_SPLITS * npl, nk),
        in_specs=[pl.BlockSpec((tk, d), lambda cb, k: (k, 0), pipeline_mode=pl.Buffered(1) if nk == 1 else None),
                  pl.BlockSpec((None, tk, kb), lambda cb, k: (_dz_plane(cb // npl), k, cb % npl))],
        out_specs=[pl.BlockSpec((None, d, kb), lambda cb, k: (cb // nps, 0, cb % nps))],
        out_shape=[SDS((nsh, d, ws), BF16)],
        scratch_shapes=[] if nk == 1 else [pltpu.VMEM((d, kb), F32)],
        sem=("parallel", "arbitrary"), blocks=blocks, temps=2 * _nbytes((d, kb), F32), comm=comm,
    )(h1, dz)


def _mixer_bwd_dx(dz, w_in, x, dx1, g_mix, comm=None):
    npln, t, d = dz.shape
    nsh, _, ws = w_in.shape
    tm = _tile(t, 256, SUBLANES_BF16)
    kb = _tile(math.gcd(d, ws), 512, LANES)
    npl = d // kb
    nps = ws // kb

    def body(dz_ref, w_ref, x_ref, dx1_ref, g_ref, dx_ref, gg_ref):
        @pl.when(pl.program_id(0) == 0)
        def _():
            gg_ref[...] = jnp.zeros_like(gg_ref)

        dh = None
        for cb in range(npln * npl):
            zb = cb // npl
            plane = (zb + 3) % 4 if zb < 4 else zb
            part = _dot_tb(dz_ref[plane, :, (cb % npl) * kb:(cb % npl + 1) * kb],
                           w_ref[cb // nps, :, (cb % nps) * kb:(cb % nps + 1) * kb])
            dh = part if dh is None else dh + part
        xh, inv = _rms_fwd(x_ref[...])
        gg_ref[...] += jnp.sum(dh * xh, axis=0, keepdims=True)
        dx_ref[...] = dx1_ref[...] + _rms_bwd(dh, xh, inv, g_ref[...])

    blocks = _nbytes((npln, tm, d), BF16) + 3 * _nbytes((tm, d), F32)
    return _pcall(
        body, name="mixer_bwd_dx", grid=(t // tm,),
        in_specs=[pl.BlockSpec((npln, tm, d), lambda i: (0, i, 0)),
                  pl.BlockSpec((nsh, d, ws), lambda i: (0, 0, 0), pipeline_mode=pl.Buffered(1)),
                  pl.BlockSpec((tm, d), lambda i: (i, 0)),
                  pl.BlockSpec((tm, d), lambda i: (i, 0)),
                  pl.BlockSpec((1, d), lambda i: (0, 0))],
        out_specs=[pl.BlockSpec((tm, d), lambda i: (i, 0)),
                   pl.BlockSpec((1, d), lambda i: (0, 0))],
        out_shape=[SDS((t, d), F32), SDS((1, d), F32)],
        sem=("arbitrary",), blocks=blocks, temps=_nbytes(w_in.shape, BF16) + 8 * _nbytes((tm, d), F32), comm=comm,
    )(dz, w_in, x, dx1, g_mix)


N_BIG = 5
SHARD_MAJOR = (0, 2)
ROWS_DIM1 = (1, 4)


def _ds(start, size, align):
    if isinstance(start, int):
        return pl.ds(start, size)
    return pl.ds(pl.multiple_of(start, align), size)


def _piece(a, ref, k, h):
    if a in SHARD_MAJOR:
        r = ref.shape[1] // 2
        return ref.at[k, _ds(h * r, r, SUBLANES_BF16), :]
    if a in ROWS_DIM1:
        r = ref.shape[1] // 8
        return ref.at[:, _ds((2 * k + h) * r, r, SUBLANES_BF16), :]
    r = ref.shape[0] // 8
    return ref.at[_ds((2 * k + h) * r, r, SUBLANES_BF16), :]


def _half(a, ref, h):
    if a in ROWS_DIM1:
        r = ref.shape[1] // 2
        return ref.at[:, _ds(h * r, r, SUBLANES_BF16), :]
    r = ref.shape[0] // 2
    return ref.at[_ds(h * r, r, SUBLANES_BF16), :]


def _piece_shape(a, full_shape):
    if a in SHARD_MAJOR:
        return (full_shape[1] // 2, full_shape[2])
    if a in ROWS_DIM1:
        return (full_shape[0], full_shape[1] // 8, full_shape[2])
    return (full_shape[0] // 8, full_shape[1])


def _shard_shape(a, full_shape):
    if a in SHARD_MAJOR:
        return (full_shape[1], full_shape[2])
    if a in ROWS_DIM1:
        return (full_shape[0], full_shape[1] // 4, full_shape[2])
    return (full_shape[0] // 4, full_shape[1])


def _rows_axis(a):
    return 1 if a in ROWS_DIM1 else 0


def _piece_block(a, full_shape):
    ps = _piece_shape(a, full_shape)
    if a in SHARD_MAJOR:
        return (None,) + ps, lambda k, c: (k, c, 0)
    if a in ROWS_DIM1:
        return ps, lambda k, c: (0, 2 * k + c, 0)
    return ps, lambda k, c: (2 * k + c, 0)


def _coords():
    return lax.axis_index("x"), lax.axis_index("y"), lax.axis_index("c")


def _peer_chips(x, y):
    return [(1 - x, y), (x, 1 - y), (1 - x, 1 - y)]


def _remote(src, dst, ssem, rsem, dev):
    return pltpu.make_async_remote_copy(src_ref=src, dst_ref=dst, send_sem=ssem, recv_sem=rsem,
                                        device_id=dev, device_id_type=MESH)


def _dma_sems(*counts):
    return [pltpu.SemaphoreType.DMA((n,)) for n in counts]


def _symmetric(ins, out_shapes, sems, copies, peers, aliases=None):
    def start(cins, couts, csems):
        for cp in copies(cins, couts, csems):
            cp.start()

    def finish(cins, couts, csems):
        for cp in copies(cins, couts, csems):
            cp.wait()

    return _Comm(ins, out_shapes, sems, start, finish, peers, aliases)


def _rows_part(a, ref, part):
    if part is None:
        return ref
    p, q, n = part
    ax = _rows_axis(a)
    r = ref.shape[ax] // n
    return ref.at[tuple(pl.ds(p * r, (q - p) * r) if d == ax else slice(None) for d in range(len(ref.shape)))]


def _merge(comms):
    ins, outs, sems, aliases, spans = [], [], [], {}, []
    for cm in comms:
        spans.append((len(ins), len(outs), len(sems)))
        for i, o in cm.aliases.items():
            aliases[len(ins) + i] = len(outs) + o
        ins += cm.ins
        outs += cm.out_shapes
        sems += cm.sems

    def each(fn_name):
        def run(cins, couts, csems):
            for cm, (i0, o0, s0) in zip(comms, spans):
                fn = getattr(cm, fn_name)
                if fn is not None:
                    fn(cins[i0:i0 + len(cm.ins)], couts[o0:o0 + len(cm.out_shapes)], csems[s0:s0 + len(cm.sems)])
        return run

    return _Comm(ins, outs, sems, each("start"), each("finish"), frozenset().union(*[cm.peers for cm in comms]),
                 aliases, mid=each("mid") if any(cm.mid is not None for cm in comms) else None)


def _gather_comm(arrs, locs, full_shapes, part=None, into=None):
    n = len(arrs)

    def own(cins, couts, csems):
        x, y, c = _coords()
        j = 2 * x + y
        return [_remote(_rows_part(a, _half(a, cins[q], h), part), _rows_part(a, _piece(a, couts[q], j, h), part),
                        csems[0].at[2 * q + h], csems[1].at[2 * q + h], (x, y, 1 - c))
                for q, a in enumerate(arrs) for h in range(2)]

    def sends(cins, couts, csems):
        x, y, c = _coords()
        j = 2 * x + y
        return [_remote(_rows_part(a, _half(a, cins[q], c), part), _rows_part(a, _piece(a, couts[q], j, c), part),
                        csems[2].at[3 * q + i], csems[3].at[3 * q + i], (px, py, c))
                for q, a in enumerate(arrs) for i, (px, py) in enumerate(_peer_chips(x, y))]

    def forwards(couts, csems, half_of):
        x, y, c = _coords()
        out = []
        for q, a in enumerate(arrs):
            for i, (px, py) in enumerate(_peer_chips(x, y)):
                landed = _rows_part(a, _piece(a, couts[q], 2 * px + py, half_of(c)), part)
                out.append(_remote(landed, landed, csems[4].at[3 * q + i], csems[5].at[3 * q + i], (x, y, 1 - c)))
        return out

    def start(cins, couts, csems):
        for cp in sends(cins, couts, csems) + own(cins, couts, csems):
            cp.start()

    def finish(cins, couts, csems):
        fw = forwards(couts, csems, lambda c: c)
        for cp, f in zip(sends(cins, couts, csems), fw):
            cp.wait_recv()
            f.start()
        for f in forwards(couts, csems, lambda c: 1 - c):
            f.wait_recv()
        for cp in sends(cins, couts, csems) + fw:
            cp.wait_send()
        for cp in own(cins, couts, csems):
            cp.wait()

    ins = [locs[a] for a in arrs] + ([into[a] for a in arrs] if into else [])
    return _Comm(ins, [SDS(full_shapes[a], BF16) for a in arrs],
                 _dma_sems(2 * n, 2 * n, 3 * n, 3 * n, 3 * n, 3 * n), start, finish, CHIPS + (SIBLING,),
                 aliases={n + q: q for q in range(n)} if into else None)


def _ring_gather_comm(arrs, locs, full_shapes):
    n = len(arrs)

    def own(cins, couts, csems):
        x, y, c = _coords()
        j = 2 * x + y
        return [_remote(_half(a, cins[q], h), _piece(a, couts[q], j, h), csems[0].at[2 * q + h],
                        csems[1].at[2 * q + h], (x, y, 1 - c)) for q, a in enumerate(arrs) for h in range(2)]

    def sends(cins, couts, csems):
        x, y, c = _coords()
        j = 2 * x + y
        return [_remote(_half(a, cins[q], c), _piece(a, couts[q], j, c), csems[2].at[2 * q + i],
                        csems[3].at[2 * q + i], (px, py, c))
                for q, a in enumerate(arrs) for i, (px, py) in enumerate(_peer_chips(x, y)[:2])]

    def relays(couts, csems):
        x, y, c = _coords()
        peers = _peer_chips(x, y)
        out = []
        for q, a in enumerate(arrs):
            for r, (src_p, dst_p) in enumerate(((0, 1), (1, 0))):
                sx, sy = peers[src_p]
                rows = _rows_part(a, _piece(a, couts[q], 2 * sx + sy, c), (r, r + 1, 2))
                out.append(_remote(rows, rows, csems[6].at[2 * q + r], csems[7].at[2 * q + r], (*peers[dst_p], c)))
        return out

    def forwards(couts, csems, half_of, which):
        x, y, c = _coords()
        out = []
        for q, a in enumerate(arrs):
            for i in which:
                px, py = _peer_chips(x, y)[i]
                landed = _piece(a, couts[q], 2 * px + py, half_of(c))
                out.append(_remote(landed, landed, csems[4].at[3 * q + i], csems[5].at[3 * q + i], (x, y, 1 - c)))
        return out

    def start(cins, couts, csems):
        for cp in sends(cins, couts, csems) + own(cins, couts, csems):
            cp.start()

    def mid(cins, couts, csems):
        for cp in sends(cins, couts, csems):
            cp.wait_recv()
        for cp in relays(couts, csems) + forwards(couts, csems, lambda c: c, (0, 1)):
            cp.start()

    def finish(cins, couts, csems):
        for cp in relays(couts, csems):
            cp.wait_recv()
        fw_diag = forwards(couts, csems, lambda c: c, (2,))
        for f in fw_diag:
            f.start()
        for f in forwards(couts, csems, lambda c: 1 - c, (0, 1, 2)):
            f.wait_recv()
        for cp in (sends(cins, couts, csems) + relays(couts, csems)
                   + forwards(couts, csems, lambda c: c, (0, 1)) + fw_diag):
            cp.wait_send()
        for cp in own(cins, couts, csems):
            cp.wait()

    return _Comm([locs[a] for a in arrs], [SDS(full_shapes[a], BF16) for a in arrs],
                 _dma_sems(2 * n, 2 * n, 2 * n, 2 * n, 3 * n, 3 * n, 2 * n, 2 * n), start, finish,
                 CHIPS + (SIBLING,), mid=mid)


def _halves_comm(arrs, gbs):
    n = len(arrs)

    def copies(cins, couts, csems):
        x, y, c = _coords()
        return [_remote(_piece(a, cins[q], k, 1 - c), couts[q].at[k], csems[0].at[4 * q + k], csems[1].at[4 * q + k],
                        (x, y, 1 - c)) for q, a in enumerate(arrs) for k in range(4)]

    return _symmetric([gbs[a] for a in arrs], [SDS((4,) + _piece_shape(a, gbs[a].shape), BF16) for a in arrs],
                      _dma_sems(4 * n, 4 * n), copies, [SIBLING])


def _chips_comm(arrs, ps, part=None, into=None):
    n = len(arrs)

    def copies(cins, couts, csems):
        x, y, c = _coords()
        return [_remote(_rows_part(a, cins[q].at[2 * px + py], part), _rows_part(a, couts[q].at[i], part),
                        csems[0].at[3 * q + i], csems[1].at[3 * q + i], (px, py, c))
                for q, a in enumerate(arrs) for i, (px, py) in enumerate(_peer_chips(x, y))]

    ins = [ps[a] for a in arrs] + ([into[a] for a in arrs] if into else [])
    return _symmetric(ins, [SDS((3,) + ps[a].shape[1:], BF16) for a in arrs], _dma_sems(3 * n, 3 * n), copies, CHIPS,
                      aliases={n + q: q for q in range(n)} if into else None)


def _result_comm(arrs, gs):
    n = len(arrs)

    def copies(cins, couts, csems):
        x, y, c = _coords()
        return [_remote(_half(a, cins[q], c), _half(a, couts[q], c), csems[0].at[q], csems[1].at[q], (x, y, 1 - c))
                for q, a in enumerate(arrs)]

    return _symmetric([gs[a] for a in arrs], [SDS(gs[a].shape, F32) for a in arrs], _dma_sems(n, n), copies,
                      [SIBLING], aliases={q: q for q in range(n)})


def _add_halves(arrs, gbs, lands, c_arr, name):
    n = len(arrs)

    def body(c_ref, *refs):
        del c_ref
        for q in range(n):
            refs[2 * n + q][...] = (refs[q][...].astype(F32) + refs[n + q][...].astype(F32)).astype(BF16)

    g_specs, l_specs, o_specs, blocks = [], [], [], 0
    for a in arrs:
        bs, imap = _piece_block(a, gbs[a].shape)
        ps = _piece_shape(a, gbs[a].shape)
        g_specs.append(pl.BlockSpec(bs, lambda k, c_ref, imap=imap: imap(k, c_ref[0])))
        nd = len(ps)
        l_specs.append(pl.BlockSpec((None,) + ps, lambda k, c_ref, nd=nd: (k,) + (0,) * nd))
        o_specs.append(pl.BlockSpec((None,) + ps, lambda k, c_ref, nd=nd: (k,) + (0,) * nd))
        blocks += 3 * _nbytes(ps, BF16)
    return list(pl.pallas_call(
        body, name=name,
        grid_spec=pltpu.PrefetchScalarGridSpec(
            num_scalar_prefetch=1, grid=(4,), in_specs=g_specs + l_specs, out_specs=o_specs),
        out_shape=[SDS((4,) + _piece_shape(a, gbs[a].shape), BF16) for a in arrs],
        compiler_params=_params(("parallel",), blocks, blocks),
    )(c_arr, *[gbs[a] for a in arrs], *lands))


def _sum_chips(a, p, land, shard_shape, jc_arr, name):
    ps = land.shape[1:]
    ax = _rows_axis(a)
    rows = ps[ax]
    nsub = 2 if rows % (2 * SUBLANES_BF16) == 0 else 1
    bs = tuple(r // nsub if q == ax else r for q, r in enumerate(ps))
    nd = len(ps)

    def at_rows(v):
        return tuple(v if q == ax else 0 for q in range(nd))

    def body(jc_ref, p_ref, l_ref, o_ref):
        del jc_ref
        acc = p_ref[...].astype(F32) + l_ref[0].astype(F32)
        acc = acc + l_ref[1].astype(F32)
        o_ref[...] = acc + l_ref[2].astype(F32)

    blocks = 4 * _nbytes(bs, BF16) + _nbytes(bs, F32)
    return pl.pallas_call(
        body, name=name,
        grid_spec=pltpu.PrefetchScalarGridSpec(
            num_scalar_prefetch=1, grid=(nsub,),
            in_specs=[pl.BlockSpec((None,) + bs, lambda s, jc: (jc[0],) + at_rows(s)),
                      pl.BlockSpec((3,) + bs, lambda s, jc: (0,) + at_rows(s))],
            out_specs=pl.BlockSpec(bs, lambda s, jc: at_rows(jc[1] * nsub + s))),
        out_shape=SDS(shard_shape, F32),
        compiler_params=_params(("parallel",), blocks, 2 * _nbytes(bs, F32)),
    )(jc_arr, p, land)


def _small_comm(v):
    rows = v.shape[0]

    def copies(cins, couts, csems):
        x, y, c = _coords()
        me = 4 * x + 2 * y + c
        out = [pltpu.make_async_copy(cins[0], couts[0].at[me], csems[0].at[0])]
        for dlt in range(1, 8):
            px = 1 - x if (dlt >> 2) & 1 else x
            py = 1 - y if (dlt >> 1) & 1 else y
            pc = 1 - c if dlt & 1 else c
            out.append(_remote(cins[0], couts[0].at[me], csems[1].at[dlt - 1], csems[2].at[dlt - 1], (px, py, pc)))
        return out

    return _symmetric([v], [SDS((8, rows, LANES), F32)], _dma_sems(1, 7, 7), copies, EVERYONE)


def _sum8(slots, name):
    def body(s_ref, o_ref):
        acc = s_ref[0]
        for i in range(1, 8):
            acc = acc + s_ref[i]
        o_ref[...] = acc

    return pl.pallas_call(
        body, name=name,
        in_specs=[pl.BlockSpec(memory_space=pltpu.VMEM)], out_specs=pl.BlockSpec(memory_space=pltpu.VMEM),
        out_shape=SDS(slots.shape[1:], F32),
    )(slots)


def _adamw(w, g, m, v, name, g_plane=None):
    rows, cols = w.shape
    tr = _tile(rows, max(SUBLANES_F32, (256 * 1024 // cols) // SUBLANES_F32 * SUBLANES_F32), SUBLANES_F32)

    def body(w_ref, g_ref, m_ref, v_ref, go_ref, d_ref, mo_ref, vo_ref):
        gr = g_ref[...]
        mn = ADAM_B1 * m_ref[...] + (1.0 - ADAM_B1) * gr
        vn = ADAM_B2 * v_ref[...] + (1.0 - ADAM_B2) * (gr * gr)
        m_hat = mn / (1.0 - ADAM_B1 ** ADAM_STEP)
        v_hat = vn / (1.0 - ADAM_B2 ** ADAM_STEP)
        d_ref[...] = -ADAM_LR * (m_hat / (jnp.sqrt(v_hat) + ADAM_EPS) + ADAM_WD * w_ref[...])
        go_ref[...] = gr
        mo_ref[...] = mn
        vo_ref[...] = vn

    spec = pl.BlockSpec((tr, cols), lambda i: (i, 0))
    g_spec = spec if g_plane is None else pl.BlockSpec((None, tr, cols), lambda i: (g_plane, i, 0))
    return pl.pallas_call(
        body, name=name, grid=(rows // tr,),
        in_specs=[spec, g_spec, spec, spec], out_specs=[spec, spec, spec, spec],
        out_shape=[SDS((rows, cols), F32)] * 4,
        compiler_params=_params(("parallel",), 8 * _nbytes((tr, cols), F32), 4 * _nbytes((tr, cols), F32)),
    )(w, g, m, v)


def _pack(parts):
    rows = []
    for p in parts:
        r = p.reshape(-1, LANES)
        pad = (-r.shape[0]) % SUBLANES_F32
        if pad:
            r = jnp.pad(r, ((0, pad), (0, 0)))
        rows.append(r)
    return jnp.concatenate(rows, axis=0)


def _unpack(packed, shapes):
    out, at = [], 0
    for s in shapes:
        n = 1
        for q in s:
            n *= q
        r = n // LANES
        out.append(packed[at:at + r].reshape(s))
        at += r + (-r) % SUBLANES_F32
    return out


def kernel(x, norm_mix, w_in, pool_w, pool_scale, w_pool_proj, conv_w, w_conv_out, w_o, norm_ffn, w_up, ffn_conv_w, ffn_conv_b, w_down, norm_final, loss_target, m_norm_mix, m_w_in, m_pool_w, m_pool_scale, m_w_pool_proj, m_conv_w, m_w_conv_out, m_w_o, m_norm_ffn, m_w_up, m_ffn_conv_w, m_ffn_conv_b, m_w_down, m_norm_final, v_norm_mix, v_w_in, v_pool_w, v_pool_scale, v_w_pool_proj, v_conv_w, v_w_conv_out, v_w_o, v_norm_ffn, v_w_up, v_ffn_conv_w, v_ffn_conv_b, v_w_down, v_norm_final):
    nseq, seq, d = x.shape
    t = nseq * seq
    f = w_down.shape[1] * 4
    c = d // N_GROUPS
    xy = lax.axis_index("x") * 2 + lax.axis_index("y")
    c_arr = lax.axis_index("c").astype(jnp.int32).reshape(1)
    jc_arr = jnp.stack([xy, lax.axis_index("c")]).astype(jnp.int32)
    nsh = 4
    zero = jnp.zeros((), jnp.int32)

    locs = [w_in[0].astype(BF16),
            jnp.stack([w_pool_proj[0], w_conv_out[0], w_o[0]]).astype(BF16),
            w_up[0].astype(BF16), w_down[0].astype(BF16), pool_w[0].astype(BF16)]
    full_shapes = [(nsh, d, N_SPLITS * d // nsh), (3, d, d), (nsh, d, 2 * f // nsh), (f, d), (N_GROUPS, c, c)]

    cw_pad = lax.dynamic_update_slice(jnp.zeros((3, d), F32), conv_w[0], (zero, xy * (d // 4)))
    fw_pad = lax.dynamic_update_slice(jnp.zeros((3, 2 * f), F32), ffn_conv_w[0], (zero, xy * (f // 2)))
    small_w = _pack([cw_pad, fw_pad]) * 0.5

    x2d = x.reshape(t, d)
    tgt = loss_target.reshape(t, d)
    ax, ay = lax.axis_index("x"), lax.axis_index("y")
    order = jnp.stack([xy, 2 * (1 - ax) + ay, 2 * ax + 1 - ay, 2 * (1 - ax) + 1 - ay]).astype(jnp.int32)
    (z, h1, w_in_f), (pool_w_f, w3_f, slots_w) = _fwd_in(
        x2d, norm_mix, locs[0], order,
        _merge([_gather_comm([4], locs, full_shapes), _gather_comm([1], locs, full_shapes, part=(0, 1, 2)),
                _small_comm(small_w)]))
    conv_w_f, ffn_cw_f = _unpack(_sum8(slots_w, "sum8_weights"), [(3, d), (3, 2 * f)])
    ffn_cw_p = ffn_cw_f.reshape(3, 2, f).transpose(1, 0, 2)
    ffn_cb_p = ffn_conv_b.reshape(2, 1, f)
    (lhs3,), (w3_f,) = _mixer_mid_fwd(z, pool_w_f, pool_scale, conv_w_f, nseq,
                                      _gather_comm([1], locs, full_shapes, part=(1, 2, 2), into={1: w3_f}))
    (lhs3, ypc, x1, h2), (w_up_f,) = _mixer_out(lhs3, z, x2d, w3_f, norm_ffn,
                                                _ring_gather_comm([2], locs, full_shapes))
    (u0,), (w_down_f,) = _ffn_up(h2, w_up_f, f, _gather_comm([3], locs, full_shapes))
    act, ua = _ffn_mid_fwd(u0, ffn_cw_p, ffn_cb_p, nseq)
    dx2, dx2b, loss11, g_norm_final = _ffn_down_loss(act, w_down_f, x1, tgt, norm_final.reshape(1, d))

    gbs, lands, ps, lands2, rs = {}, {}, {}, {}, {}
    tn_up = _tile(2 * f // nsh, 1408, LANES)
    npp = f // tn_up

    def add(arrs, name):
        for a, p in zip(arrs, _add_halves(arrs, gbs, [lands[a] for a in arrs], c_arr, name)):
            ps[a] = p

    def summed(a):
        rs[a] = _sum_chips(a, ps[a], lands2[a], _shard_shape(a, full_shapes[a]), jc_arr, "sum_chips_%d" % a)

    (gbs[3],), _ = _wgrad(act, dx2b, "wgrad_down", tr=tn_up, tn=d)
    (da,), (lands[3],) = _ffn_bwd_da(dx2b, w_down_f, _halves_comm([3], gbs))
    add([3], "add_halves_down")
    (du0, g_ffn_cw_p, g_ffn_cb_p), (lands2[3],) = _ffn_mid_bwd(da, u0, ua, ffn_cw_p, nseq, _chips_comm([3], ps))
    summed(3)
    (gbs[2],), (rs[3],) = _wgrad(h2, du0, "wgrad_up", tr=d, tn=tn_up, b_plane_of=lambda n: (n // npp, n % npp),
                                 out_shards=nsh, comm=_result_comm([3], rs))
    (dx1, rhs3, g_norm_ffn), (lands[2],) = _ffn_bwd_dx1(du0, w_up_f, x1, dx2, norm_ffn, 3, _halves_comm([2], gbs))
    add([2], "add_halves_up")
    (rhs3, dz, dpq), (lands2[2],) = _mixer_bwd(rhs3, z, ypc, w3_f, _chips_comm([2], ps, part=(0, 1, 2)))
    (gbs[1],), (lands2[2],) = _wgrad3(lhs3, rhs3, _chips_comm([2], ps, part=(1, 2, 2), into=lands2))
    summed(2)
    (dz, g_conv_w), (lands[1], rs[2]) = _conv_bwd(dz, dpq, z, conv_w_f, nseq,
                                                  _merge([_halves_comm([1], gbs), _result_comm([2], rs)]))
    add([1], "add_halves_sq3")
    (dz, g_pool_w, g_pool_scale), _ = _pool_bwd_call(dz, dpq, z, pool_w_f, pool_scale, nseq)
    gbs[4] = g_pool_w.astype(BF16)
    (gbs[0],), (lands2[1],) = _wgrad_in(h1, dz, nsh, _chips_comm([1], ps))
    summed(1)
    lands[0], lands[4] = _run_comm(_halves_comm([0, 4], gbs), "exchange_halves_in")
    add([0, 4], "add_halves_in")
    g_ffn_cw = g_ffn_cw_p.transpose(1, 0, 2).reshape(3, 2 * f)
    small_a = _pack([g_pool_scale, g_norm_ffn, g_ffn_cb_p.reshape(1, 2 * f), g_norm_final.reshape(d), g_conv_w,
                     g_ffn_cw, jnp.pad(loss11, ((0, SUBLANES_F32 - 1), (0, LANES - 1)))])
    (grad_x, g_norm_mix), (lands2[0], lands2[4], rs[1], slots_a) = _mixer_bwd_dx(
        dz, w_in_f, x2d, dx1, norm_mix,
        _merge([_chips_comm([0, 4], ps), _result_comm([1], rs), _small_comm(small_a)]))
    summed(0)
    summed(4)
    rs[0], rs[4], slots_b = _run_comm(_merge([_result_comm([0, 4], rs), _small_comm(_pack([g_norm_mix]))]),
                                      "exchange_result_in")
    shapes_a = [(1, d), (1, d), (1, 2 * f), (d,), (3, d), (3, 2 * f), (SUBLANES_F32, LANES)]
    gs_pool_scale, gs_norm_ffn, gs_ffn_cb, gs_norm_final, gs_conv_w, gs_ffn_cw, loss_blk = _unpack(
        _sum8(slots_a, "sum8_grads"), shapes_a)
    (gs_norm_mix,) = _unpack(_sum8(slots_b, "sum8_norm_mix"), [(1, d)])
    gs_conv_w = lax.dynamic_slice(gs_conv_w, (zero, xy * (d // 4)), (3, d // 4))
    gs_ffn_cw = lax.dynamic_slice(gs_ffn_cw, (zero, xy * (f // 2)), (3, f // 2))

    def upd(w, g, m, v, name, g_plane=None):
        shape = w.shape
        rows = 1
        for q in shape[:-1]:
            rows *= q
        g2 = g if g_plane is not None else g.reshape(rows, shape[-1])
        outs = _adamw(w.reshape(rows, shape[-1]), g2, m.reshape(rows, shape[-1]), v.reshape(rows, shape[-1]),
                      name, g_plane)
        return [o.reshape(shape) for o in outs]

    res = {
        "w_in": upd(w_in, rs[0], m_w_in, v_w_in, "adamw_w_in"),
        "pool_w": upd(pool_w, rs[4], m_pool_w, v_pool_w, "adamw_pool_w"),
        "w_pool_proj": upd(w_pool_proj, rs[1], m_w_pool_proj, v_w_pool_proj, "adamw_w_pool_proj", 0),
        "w_conv_out": upd(w_conv_out, rs[1], m_w_conv_out, v_w_conv_out, "adamw_w_conv_out", 1),
        "w_o": upd(w_o, rs[1], m_w_o, v_w_o, "adamw_w_o", 2),
        "w_up": upd(w_up, rs[2], m_w_up, v_w_up, "adamw_w_up"),
        "w_down": upd(w_down, rs[3], m_w_down, v_w_down, "adamw_w_down"),
    }

    small_names = ["norm_mix", "pool_scale", "norm_ffn", "ffn_conv_b", "norm_final", "conv_w", "ffn_conv_w"]
    small_ws = [norm_mix, pool_scale, norm_ffn, ffn_conv_b, norm_final, conv_w, ffn_conv_w]
    small_ms = [m_norm_mix, m_pool_scale, m_norm_ffn, m_ffn_conv_b, m_norm_final, m_conv_w, m_ffn_conv_w]
    small_vs = [v_norm_mix, v_pool_scale, v_norm_ffn, v_ffn_conv_b, v_norm_final, v_conv_w, v_ffn_conv_w]
    small_gs = [gs_norm_mix, gs_pool_scale, gs_norm_ffn, gs_ffn_cb, gs_norm_final, gs_conv_w, gs_ffn_cw]
    _, sd, sm, sv = _adamw(_pack(small_ws), _pack(small_gs), _pack(small_ms), _pack(small_vs), "adamw_small")
    shapes = [w.shape for w in small_ws]
    sd, sm, sv = _unpack(sd, shapes), _unpack(sm, shapes), _unpack(sv, shapes)
    for i, nm in enumerate(small_names):
        res[nm] = [small_gs[i].reshape(shapes[i]), sd[i], sm[i], sv[i]]

    order = ["norm_mix", "w_in", "pool_w", "pool_scale", "w_pool_proj", "conv_w", "w_conv_out", "w_o", "norm_ffn",
             "w_up", "ffn_conv_w", "ffn_conv_b", "w_down", "norm_final"]
    return (loss_blk[0, 0], grad_x.reshape(x.shape), *[res[n][0] for n in order], *[res[n][1] for n in order],
            *[res[n][2] for n in order], *[res[n][3] for n in order])
```

```python
import math

import jax
import jax.numpy as jnp
from jax import lax
from jax.experimental import pallas as pl
from jax.experimental.pallas import tpu as pltpu

F32 = jnp.float32
BF16 = jnp.bfloat16
SDS = jax.ShapeDtypeStruct
MESH = pl.DeviceIdType.MESH

RMS_EPS = 1e-6
POOL_WINDOWS = (2, 4, 8, 16)
N_GROUPS = len(POOL_WINDOWS)
N_SPLITS = 6

ADAM_LR = 0.001
ADAM_B1 = 0.9
ADAM_B2 = 0.999
ADAM_EPS = 1e-08
ADAM_WD = 0.01
ADAM_STEP = 10

LANES = 128
SUBLANES_F32 = 8
SUBLANES_BF16 = 16
VMEM_BYTES = 64 * 1024 * 1024
VMEM_CAP = VMEM_BYTES - 8 * 1024 * 1024
VMEM_FLOOR = 16 * 1024 * 1024

ANY = pl.BlockSpec(memory_space=pl.ANY)


def _tile(dim, pref, align):
    if dim <= pref:
        return dim
    t = (pref // align) * align
    while t >= align:
        if dim % t == 0:
            return t
        t -= align
    return dim


def _nbytes(shape, dtype):
    n = 1
    for s in shape:
        n *= s
    return n * jnp.dtype(dtype).itemsize


def _params(sem, block_bytes, temp_bytes=0, collective_id=None):
    need = 2 * block_bytes + temp_bytes + 4 * 1024 * 1024
    return pltpu.CompilerParams(dimension_semantics=sem, collective_id=collective_id,
                                vmem_limit_bytes=int(min(max(need, VMEM_FLOOR), VMEM_CAP)))


SIBLING = (0, 0, 1)
CHIPS = ((1, 0, 0), (0, 1, 0), (1, 1, 0))
EVERYONE = tuple((a, b, c) for a in range(2) for b in range(2) for c in range(2) if a + b + c)
PEER_SETS = (frozenset([SIBLING]), frozenset(CHIPS), frozenset(CHIPS + (SIBLING,)), frozenset(EVERYONE))
MID_AT = 0.75


def _collective_id(peers):
    return PEER_SETS.index(frozenset(peers))


def _handshake(peers):
    x, y, c = lax.axis_index("x"), lax.axis_index("y"), lax.axis_index("c")
    bar = pltpu.get_barrier_semaphore()
    for fx, fy, fc in sorted(peers):
        dev = (1 - x if fx else x, 1 - y if fy else y, 1 - c if fc else c)
        pl.semaphore_signal(bar, inc=1, device_id=dev, device_id_type=MESH)
    pl.semaphore_wait(bar, len(peers))


class _Comm:
    def __init__(self, ins, out_shapes, sems, start, finish, peers, aliases=None, mid=None):
        self.ins = list(ins)
        self.out_shapes = list(out_shapes)
        self.sems = list(sems)
        self.start = start
        self.finish = finish
        self.mid = mid
        self.peers = frozenset(peers)
        self.aliases = dict(aliases or {})


def _pcall(body, *, name, grid, in_specs, out_specs, out_shape, sem, blocks, temps=0, scratch_shapes=(),
           input_output_aliases=None, comm=None):
    in_specs = list(in_specs)
    out_specs = list(out_specs)
    out_shape = list(out_shape)
    scratch_shapes = list(scratch_shapes)
    aliases = dict(input_output_aliases or {})
    n_in, n_out, n_scr = len(in_specs), len(out_shape), len(scratch_shapes)
    if comm is None:
        call = pl.pallas_call(
            body, name=name, grid=grid, in_specs=in_specs, out_specs=out_specs, out_shape=out_shape,
            scratch_shapes=scratch_shapes, input_output_aliases=aliases,
            compiler_params=_params(sem, blocks, temps))
        return lambda *args: (list(call(*args)), [])

    nci, nco = len(comm.ins), len(comm.out_shapes)
    n_steps = 1
    for g in grid:
        n_steps *= g

    def hosted(*refs):
        ins = refs[:n_in]
        cins = refs[n_in:n_in + nci]
        outs = refs[n_in + nci:n_in + nci + n_out]
        couts = refs[n_in + nci + n_out:n_in + nci + n_out + nco]
        scr = refs[n_in + nci + n_out + nco:n_in + nci + n_out + nco + n_scr]
        csems = refs[n_in + nci + n_out + nco + n_scr:]
        first = None
        last = None
        step = 0
        for q, g in enumerate(grid):
            pid = pl.program_id(q)
            first = (pid == 0) if first is None else first & (pid == 0)
            last = (pid == g - 1) if last is None else last & (pid == g - 1)
            step = step * g + pid

        @pl.when(first)
        def _():
            _handshake(comm.peers)
            comm.start(cins, couts, csems)

        if comm.mid is not None:
            @pl.when(step == int(MID_AT * n_steps))
            def _():
                comm.mid(cins, couts, csems)

        body(*ins, *outs, *scr)

        @pl.when(last)
        def _():
            comm.finish(cins, couts, csems)

    for i, o in comm.aliases.items():
        aliases[n_in + i] = n_out + o
    call = pl.pallas_call(
        hosted, name=name, grid=grid, in_specs=in_specs + [ANY] * nci, out_specs=out_specs + [ANY] * nco,
        out_shape=out_shape + comm.out_shapes, scratch_shapes=scratch_shapes + comm.sems,
        input_output_aliases=aliases,
        compiler_params=_params(("arbitrary",) * len(grid), blocks, temps, _collective_id(comm.peers)))

    def run(*args):
        res = call(*args, *comm.ins)
        return list(res[:n_out]), list(res[n_out:])

    return run


def _run_comm(comm, name):
    def body(*refs):
        nci, nco = len(comm.ins), len(comm.out_shapes)
        cins, couts, csems = refs[:nci], refs[nci:nci + nco], refs[nci + nco:]
        _handshake(comm.peers)
        comm.start(cins, couts, csems)
        if comm.mid is not None:
            comm.mid(cins, couts, csems)
        comm.finish(cins, couts, csems)

    return list(pl.pallas_call(
        body, name=name, in_specs=[ANY] * len(comm.ins), out_specs=[ANY] * len(comm.out_shapes),
        out_shape=comm.out_shapes, scratch_shapes=comm.sems, input_output_aliases=comm.aliases,
        compiler_params=pltpu.CompilerParams(collective_id=_collective_id(comm.peers)),
    )(*comm.ins))


def _dot(a, b):
    return jnp.dot(a, b, preferred_element_type=F32)


def _dot_tb(a, b):
    return lax.dot_general(a, b, (((1,), (1,)), ((), ())), preferred_element_type=F32)


def _dot_ta(a, b):
    return lax.dot_general(a, b, (((0,), (0,)), ((), ())), preferred_element_type=F32)


def _rms_fwd(x):
    inv = lax.rsqrt(jnp.mean(x * x, axis=-1, keepdims=True) + RMS_EPS)
    return x * inv, inv


def _rms_bwd(dy, xhat, inv, g):
    gd = dy * g
    return inv * (gd - xhat * jnp.mean(gd * xhat, axis=-1, keepdims=True))


def _sigmoid(x):
    return 1.0 / (1.0 + jnp.exp(-x))


def _shift_down(x, k, row):
    return jnp.where(row >= k, pltpu.roll(x, k, 0), 0.0)


def _shift_up(x, k, row):
    s = x.shape[0]
    return jnp.where(row < s - k, pltpu.roll(x, s - k, 0), 0.0)


def _pool_fwd(u, win, row):
    s = u
    k = 1
    while k < win:
        s = s + _shift_down(s, k, row)
        k *= 2
    cnt = jnp.minimum(row + 1, win).astype(F32)
    return s / cnt - u


def _pool_bwd(dp, win, row):
    cnt = jnp.minimum(row + 1, win).astype(F32)
    s = dp / cnt
    k = 1
    while k < win:
        s = s + _shift_up(s, k, row)
        k *= 2
    return s - dp


def _acc_over(k, nk, part, acc, o_ref):
    @pl.when(k == 0)
    def _():
        acc[...] = part

    @pl.when(k > 0)
    def _():
        acc[...] += part

    @pl.when(k == nk - 1)
    def _():
        o_ref[...] = acc[...].astype(o_ref.dtype)


def _fwd_in(x, g, w_loc, order, comm):
    t, d = x.shape
    ws = w_loc.shape[1]
    nsh = order.shape[0]
    assert nsh == 4, "the shard walk below is written for the 2 x 2 chips of the mesh"
    tm = _tile(t, 1024, SUBLANES_BF16)
    ni = t // tm
    nci, nco = len(comm.ins), len(comm.out_shapes)
    all_peers = comm.peers | frozenset(CHIPS + (SIBLING,))

    def body(order_ref, x_ref, g_ref, loc_ref, *rest):
        del order_ref
        cins = rest[:nci]
        z_ref, h_ref, full_ref = rest[nci:nci + 3]
        couts = rest[nci + 3:nci + 3 + nco]
        (hs, wbuf, wsem, own_s, own_r, snd_s, snd_r, fwd_s, fwd_r, rly_s, rly_r) = rest[nci + 3 + nco:nci + 14 + nco]
        csems = rest[nci + 14 + nco:]
        j = pl.program_id(0)
        i = pl.program_id(1)
        x_, y_, c_ = _coords()
        own = 2 * x_ + y_
        sib = (x_, y_, 1 - c_)
        peers = _peer_chips(x_, y_)

        def sends():
            return [_remote(_half(0, loc_ref, c_), _piece(0, full_ref, own, c_), snd_s.at[p], snd_r.at[p], (px, py, c_))
                    for p, (px, py) in enumerate(peers[:2])]

        def relays():
            out = []
            for q, (src_p, dst_p) in enumerate(((0, 1), (1, 0))):
                sx, sy = peers[src_p]
                part = _rows_part(0, _piece(0, full_ref, 2 * sx + sy, c_), (q, q + 1, 2))
                out.append(_remote(part, part, rly_s.at[q], rly_r.at[q], (*peers[dst_p], c_)))
            return out

        def owns():
            return [_remote(_half(0, loc_ref, h), _piece(0, full_ref, own, h), own_s.at[h], own_r.at[h], sib)
                    for h in range(2)]

        def forward(p, half):
            px, py = peers[p]
            landed = _piece(0, full_ref, 2 * px + py, half)
            return _remote(landed, landed, fwd_s.at[p], fwd_r.at[p], sib)

        def load(src, slot):
            return pltpu.make_async_copy(src, wbuf.at[slot], wsem.at[slot])

        @pl.when((j == 0) & (i == 0))
        def _():
            _handshake(all_peers)
            for cp in sends() + owns():
                cp.start()
            load(loc_ref, 0).start()

        @pl.when(j == 0)
        def _():
            xh, _ = _rms_fwd(x_ref[...])
            h = (xh * g_ref[...]).astype(BF16)
            hs[pl.ds(pl.multiple_of(i * tm, tm), tm), :] = h
            h_ref[...] = h

        slot = j % 2

        @pl.when(i == 0)
        def _():
            load(loc_ref, slot).wait()

        z_ref[...] = _dot(hs[pl.ds(pl.multiple_of(i * tm, tm), tm), :], wbuf[slot]).astype(BF16)

        def load_shard(p, into):
            px, py = peers[p]
            forward(p, 1 - c_).wait_recv()
            load(full_ref.at[2 * px + py], into).start()

        @pl.when((j == 0) & (i == ni - 1))
        def _():
            for cp in sends():
                cp.wait_recv()
            for cp in relays() + [forward(0, c_), forward(1, c_)]:
                cp.start()
            load_shard(0, 1)
            comm.start(cins, couts, csems)

        @pl.when((j == 1) & (i == 0))
        def _():
            load_shard(1, 0)

        @pl.when((j == 2) & (i == max(ni - 2, 0)))
        def _():
            for cp in relays():
                cp.wait_recv()
            forward(2, c_).start()
            load_shard(2, 1)

        @pl.when((j == nsh - 1) & (i == ni - 1))
        def _():
            for cp in sends() + relays() + [forward(p, c_) for p in range(nsh - 1)]:
                cp.wait_send()
            for cp in owns():
                cp.wait()
            comm.finish(cins, couts, csems)

    last = ni - 1
    blocks = _nbytes((tm, d), F32) + _nbytes((tm, ws), BF16) + _nbytes((tm, d), BF16)
    scratch = _nbytes((t, d), BF16) + 2 * _nbytes((d, ws), BF16)
    res = pl.pallas_call(
        body, name="fwd_in",
        grid_spec=pltpu.PrefetchScalarGridSpec(
            num_scalar_prefetch=1, grid=(nsh, ni),
            in_specs=[pl.BlockSpec((tm, d), lambda j, i, o: (jnp.where(j == 0, i, last), 0)),
                      pl.BlockSpec((1, d), lambda j, i, o: (0, 0)), ANY] + [ANY] * nci,
            out_specs=[pl.BlockSpec((tm, ws), lambda j, i, o: (i, o[j])),
                       pl.BlockSpec((tm, d), lambda j, i, o: (jnp.where(j == 0, i, last), 0)), ANY] + [ANY] * nco,
            scratch_shapes=[pltpu.VMEM((t, d), BF16), pltpu.VMEM((2, d, ws), BF16)]
            + _dma_sems(2, 2, 2, 2, 2, nsh - 1, nsh - 1, 2, 2) + comm.sems),
        out_shape=[SDS((t, nsh * ws), BF16), SDS((t, d), BF16), SDS((nsh, d, ws), BF16)] + comm.out_shapes,
        input_output_aliases={4 + i: 3 + o for i, o in comm.aliases.items()},
        compiler_params=_params(("arbitrary", "arbitrary"), blocks, scratch + 3 * _nbytes((tm, d), F32),
                                _collective_id(all_peers)),
    )(order, x, g, w_loc, *comm.ins)
    return list(res[:3]), list(res[3:])


def _mixer_mid_fwd(z, pool_w, pool_scale, conv_w, nseq, comm=None):
    t = z.shape[0]
    d = pool_scale.shape[1]
    s = t // nseq
    c = d // N_GROUPS

    def body(zp, zb, zc, zv, pw, ps, cw, o):
        j = pl.program_id(1)
        row = lax.broadcasted_iota(jnp.int32, (s, c), 0)
        for gi, win in enumerate(POOL_WINDOWS):
            @pl.when(j == gi)
            def _(win=win):
                pooled = _pool_fwd(zp[...].astype(F32), win, row)
                o[0] = (_dot(pooled.astype(BF16), pw[...]) * ps[...]).astype(BF16)

        cv = zc[...].astype(F32) * zv[...].astype(F32)
        cc = (cw[pl.ds(2, 1), :] * cv + cw[pl.ds(1, 1), :] * _shift_down(cv, 1, row)
              + cw[pl.ds(0, 1), :] * _shift_down(cv, 2, row))
        o[1] = (zb[...].astype(F32) * cc).astype(BF16)

    blocks = 4 * _nbytes((s, c), BF16) + _nbytes((c, c), BF16) + _nbytes((2, s, c), BF16)
    return _pcall(
        body, name="mixer_mid_fwd", grid=(nseq, N_GROUPS),
        in_specs=[pl.BlockSpec((s, c), lambda b, j: (b, j)),
                  pl.BlockSpec((s, c), lambda b, j: (b, N_GROUPS + j)),
                  pl.BlockSpec((s, c), lambda b, j: (b, 2 * N_GROUPS + j)),
                  pl.BlockSpec((s, c), lambda b, j: (b, 3 * N_GROUPS + j)),
                  pl.BlockSpec((None, c, c), lambda b, j: (j, 0, 0)),
                  pl.BlockSpec((1, c), lambda b, j: (0, j)),
                  pl.BlockSpec((3, c), lambda b, j: (0, j))],
        out_specs=[pl.BlockSpec((2, s, c), lambda b, j: (0, b, j))],
        out_shape=[SDS((3, t, d), BF16)],
        sem=("parallel", "parallel"), blocks=blocks, temps=8 * _nbytes((s, c), F32), comm=comm,
    )(z, z, z, z, pool_w, pool_scale, conv_w)


def _mixer_out(lhs3, z, x, w3, g_ffn, comm=None):
    t, d = x.shape
    tm = _tile(t, 256, SUBLANES_BF16)

    def body(pq, zgp, zgc, x_ref, w_ref, g_ref, mrg, ypc, x1o, h2o):
        yp = _dot(pq[0], w_ref[0])
        yc = _dot(pq[1], w_ref[1])
        m = _sigmoid(zgp[...].astype(F32)) * yp + _sigmoid(zgc[...].astype(F32)) * yc
        mb = m.astype(BF16)
        x1 = x_ref[...] + _dot(mb, w_ref[2])
        ypc[0] = yp.astype(BF16)
        ypc[1] = yc.astype(BF16)
        mrg[...] = mb
        x1o[...] = x1
        xh, _ = _rms_fwd(x1)
        h2o[...] = (xh * g_ref[...]).astype(BF16)

    blocks = (_nbytes((2, tm, d), BF16) * 2 + _nbytes((tm, d), BF16) * 4 + _nbytes((tm, d), F32) * 2
              + _nbytes((3, d, d), BF16))
    return _pcall(
        body, name="mixer_out", grid=(t // tm,),
        in_specs=[pl.BlockSpec((2, tm, d), lambda i: (0, i, 0)),
                  pl.BlockSpec((tm, d), lambda i: (i, 4)),
                  pl.BlockSpec((tm, d), lambda i: (i, 5)),
                  pl.BlockSpec((tm, d), lambda i: (i, 0)),
                  pl.BlockSpec((3, d, d), lambda i: (0, 0, 0)),
                  pl.BlockSpec((1, d), lambda i: (0, 0))],
        out_specs=[pl.BlockSpec((None, tm, d), lambda i: (2, i, 0)),
                   pl.BlockSpec((2, tm, d), lambda i: (0, i, 0)),
                   pl.BlockSpec((tm, d), lambda i: (i, 0)),
                   pl.BlockSpec((tm, d), lambda i: (i, 0))],
        out_shape=[SDS(lhs3.shape, BF16), SDS((2, t, d), BF16), SDS((t, d), F32), SDS((t, d), BF16)],
        input_output_aliases={0: 0},
        sem=("parallel",), blocks=blocks, temps=8 * _nbytes((tm, d), F32), comm=comm,
    )(lhs3, z, z, x, w3, g_ffn)


def _ffn_up(h2, w_up, f, comm=None):
    t, d = h2.shape
    _, _, ws = w_up.shape
    tm = _tile(t, 2048, SUBLANES_BF16)
    tn = _tile(ws, 1408, LANES)
    nps = ws // tn
    npp = f // tn

    def body(h_ref, w_ref, o_ref):
        o_ref[...] = _dot(h_ref[...], w_ref[...]).astype(BF16)

    blocks = _nbytes((tm, d), BF16) + _nbytes((d, tn), BF16) + _nbytes((tm, tn), BF16)
    return _pcall(
        body, name="ffn_up", grid=(t // tm, 2 * npp),
        in_specs=[pl.BlockSpec((tm, d), lambda i, j: (i, 0)),
                  pl.BlockSpec((None, d, tn), lambda i, j: (j // nps, 0, j % nps))],
        out_specs=[pl.BlockSpec((None, tm, tn), lambda i, j: (j // npp, i, j % npp))],
        out_shape=[SDS((2, t, f), BF16)],
        sem=("parallel", "parallel"), blocks=blocks, temps=_nbytes((tm, tn), F32), comm=comm,
    )(h2, w_up)


def _conv3_rows(u, u1, u2, w_ref, p):
    return w_ref[p, pl.ds(2, 1), :] * u + w_ref[p, pl.ds(1, 1), :] * u1 + w_ref[p, pl.ds(0, 1), :] * u2


WGRAD_TOKENS = 2048
WGRAD_TOKENS_WIDE = 4096
CHUNK = 64
HALO = SUBLANES_F32


def _up1_up2(u, nxt):
    rows = u.shape[0]
    ext = jnp.concatenate([u, nxt], axis=0)
    n = rows + HALO
    return pltpu.roll(ext, n - 1, 0)[:rows], pltpu.roll(ext, n - 2, 0)[:rows]


def _fold8(x):
    return jnp.sum(x.reshape(x.shape[0] // SUBLANES_F32, SUBLANES_F32, x.shape[1]), axis=0)


def _ffn_mid_fwd(u0, cw, cb, nseq):
    _, t, f = u0.shape
    s = t // nseq
    c = _tile(f, 256, LANES)

    def body(u_ref, w_ref, b_ref, a_ref, uo_ref):
        row = lax.broadcasted_iota(jnp.int32, (s, c), 0)
        act = []
        for p in range(2):
            u = u_ref[p].astype(F32)
            act.append(_conv3_rows(u, _shift_down(u, 1, row), _shift_down(u, 2, row), w_ref, p) + b_ref[p])
            uo_ref[p] = act[p].astype(BF16)
        ug, uv = act
        a_ref[...] = (ug * _sigmoid(ug) * uv).astype(BF16)

    blocks = 2 * _nbytes((2, s, c), BF16) + _nbytes((s, c), BF16)
    outs, _ = _pcall(
        body, name="ffn_mid_fwd", grid=(f // c, nseq),
        in_specs=[pl.BlockSpec((2, s, c), lambda j, b: (0, b, j)),
                  pl.BlockSpec((2, 3, c), lambda j, b: (0, 0, j)),
                  pl.BlockSpec((2, 1, c), lambda j, b: (0, 0, j))],
        out_specs=[pl.BlockSpec((s, c), lambda j, b: (b, j)),
                   pl.BlockSpec((2, s, c), lambda j, b: (0, b, j))],
        out_shape=[SDS((t, f), BF16), SDS((2, t, f), BF16)],
        sem=("parallel", "parallel"), blocks=blocks, temps=8 * _nbytes((s, c), F32),
    )(u0, cw, cb)
    return outs


def _ffn_down_loss(a, w_down, x1, tgt, g_fin):
    t, f = a.shape
    d = x1.shape[1]
    tm = _tile(t, 256, SUBLANES_BF16)
    nsteps = t // tm

    def body(a_ref, w_ref, x1_ref, t_ref, g_ref, dx_ref, dxb_ref, loss_ref, gg_ref, lacc):
        i = pl.program_id(0)

        @pl.when(i == 0)
        def _():
            lacc[...] = jnp.zeros_like(lacc)
            gg_ref[...] = jnp.zeros_like(gg_ref)

        x2 = x1_ref[...] + _dot(a_ref[...], w_ref[...])
        xh, inv = _rms_fwd(x2)
        g = g_ref[...]
        e = xh * g - t_ref[...]
        lacc[...] += jnp.sum(e * e, axis=0, keepdims=True)
        dy = e * (1.0 / d)
        gg_ref[...] += jnp.sum(dy * xh, axis=0, keepdims=True)
        dx2 = _rms_bwd(dy, xh, inv, g)
        dx_ref[...] = dx2
        dxb_ref[...] = dx2.astype(BF16)

        @pl.when(i == nsteps - 1)
        def _():
            loss_ref[...] = jnp.sum(lacc[...], axis=1, keepdims=True) * (0.5 / d)

    blocks = (_nbytes((tm, f), BF16) + _nbytes((f, d), BF16) + 3 * _nbytes((tm, d), F32) + _nbytes((tm, d), BF16))
    outs, _ = _pcall(
        body, name="ffn_down_loss", grid=(nsteps,),
        in_specs=[pl.BlockSpec((tm, f), lambda i: (i, 0)), pl.BlockSpec((f, d), lambda i: (0, 0)),
                  pl.BlockSpec((tm, d), lambda i: (i, 0)), pl.BlockSpec((tm, d), lambda i: (i, 0)),
                  pl.BlockSpec((1, d), lambda i: (0, 0))],
        out_specs=[pl.BlockSpec((tm, d), lambda i: (i, 0)), pl.BlockSpec((tm, d), lambda i: (i, 0)),
                   pl.BlockSpec((1, 1), lambda i: (0, 0)), pl.BlockSpec((1, d), lambda i: (0, 0))],
        out_shape=[SDS((t, d), F32), SDS((t, d), BF16), SDS((1, 1), F32), SDS((1, d), F32)],
        scratch_shapes=[pltpu.VMEM((1, d), F32)],
        sem=("arbitrary",), blocks=blocks, temps=8 * _nbytes((tm, d), F32),
    )(a, w_down, x1, tgt, g_fin)
    return outs


def _ffn_bwd_da(dxb, w_down, comm=None):
    t, d = dxb.shape
    f = w_down.shape[0]
    tm = _tile(t, 512, SUBLANES_BF16)

    def body(x_ref, w_ref, o_ref):
        o_ref[...] = _dot_tb(x_ref[...], w_ref[...]).astype(BF16)

    blocks = _nbytes((tm, d), BF16) + _nbytes((tm, f), BF16)
    return _pcall(
        body, name="ffn_bwd_da", grid=(t // tm,),
        in_specs=[pl.BlockSpec((tm, d), lambda i: (i, 0)),
                  pl.BlockSpec((f, d), lambda i: (0, 0), pipeline_mode=pl.Buffered(1))],
        out_specs=[pl.BlockSpec((tm, f), lambda i: (i, 0))],
        out_shape=[SDS((t, f), BF16)],
        sem=("parallel",), blocks=blocks, temps=_nbytes((f, d), BF16) + _nbytes((tm, f), F32), comm=comm,
    )(dxb, w_down)


def _ffn_mid_bwd(da, u0, ua, cw, nseq, comm=None):
    _, t, f = u0.shape
    s = t // nseq
    c = _tile(f, 128, LANES)
    r = _tile(s, CHUNK, SUBLANES_BF16)
    n = s // r

    def body(da_ref, u_ref, ua_ref, w_ref, du_ref, gw_ref, gb_ref):
        @pl.when(pl.program_id(1) == 0)
        def _():
            gw_ref[...] = jnp.zeros_like(gw_ref)
            gb_ref[...] = jnp.zeros_like(gb_ref)

        def step(i, carry):
            nxt, sums = carry
            rows = pl.ds(pl.multiple_of((n - 1 - i) * r, r), r)
            ug = ua_ref[0, rows, :].astype(F32)
            uv = ua_ref[1, rows, :].astype(F32)
            sg = _sigmoid(ug)
            dacc = da_ref[rows, :].astype(F32)
            dus = (dacc * uv * sg * (1.0 + ug * (1.0 - sg)), dacc * (ug * sg))
            first, new_sums = [], []
            for p in range(2):
                du = dus[p]
                d1, d2 = _up1_up2(du, nxt[p])
                du_ref[p, rows, :] = _conv3_rows(du, d1, d2, w_ref, p).astype(BF16)
                u = u_ref[p, rows, :].astype(F32)
                sb, s0, s1, s2 = sums[p]
                new_sums.append((sb + _fold8(du), s0 + _fold8(d2 * u), s1 + _fold8(d1 * u), s2 + _fold8(du * u)))
                first.append(du[:HALO])
            return tuple(first), tuple(new_sums)

        zero = jnp.zeros((HALO, c), F32)
        _, sums = lax.fori_loop(0, n, step, ((zero, zero), ((zero,) * 4,) * 2))
        for p in range(2):
            sb, s0, s1, s2 = sums[p]
            gb_ref[p] += jnp.sum(sb, axis=0, keepdims=True)
            gw_ref[p, pl.ds(0, 1), :] += jnp.sum(s0, axis=0, keepdims=True)
            gw_ref[p, pl.ds(1, 1), :] += jnp.sum(s1, axis=0, keepdims=True)
            gw_ref[p, pl.ds(2, 1), :] += jnp.sum(s2, axis=0, keepdims=True)

    blocks = _nbytes((s, c), BF16) + 3 * _nbytes((2, s, c), BF16)
    return _pcall(
        body, name="ffn_mid_bwd", grid=(f // c, nseq),
        in_specs=[pl.BlockSpec((s, c), lambda j, b: (b, j)),
                  pl.BlockSpec((2, s, c), lambda j, b: (0, b, j)),
                  pl.BlockSpec((2, s, c), lambda j, b: (0, b, j)),
                  pl.BlockSpec((2, 3, c), lambda j, b: (0, 0, j))],
        out_specs=[pl.BlockSpec((2, s, c), lambda j, b: (0, b, j)),
                   pl.BlockSpec((2, 3, c), lambda j, b: (0, 0, j)),
                   pl.BlockSpec((2, 1, c), lambda j, b: (0, 0, j))],
        out_shape=[SDS((2, t, f), BF16), SDS((2, 3, f), F32), SDS((2, 1, f), F32)],
        sem=("parallel", "arbitrary"), blocks=blocks, temps=4 * 1024 * 1024, comm=comm,
    )(da, u0, ua, cw)


def _wgrad(a, b, name, *, tr, tn, b_plane_of=None, out_shards=None, comm=None):
    t, m = a.shape
    n_total = b.shape[-1] * (b.shape[0] if b.ndim == 3 else 1)
    tk = _tile(t, WGRAD_TOKENS_WIDE if n_total > tn and m == tr else WGRAD_TOKENS, SUBLANES_BF16)
    nk = t // tk
    once = pl.Buffered(1) if nk == 1 else None

    def body(a_ref, b_ref, o_ref, *acc):
        part = _dot_ta(a_ref[...], b_ref[...])
        if nk == 1:
            o_ref[...] = part.astype(BF16)
        else:
            _acc_over(pl.program_id(2), nk, part, acc[0], o_ref)

    if b.ndim == 3:
        b_spec = pl.BlockSpec((None, tk, tn), lambda r, n, k: (b_plane_of(n)[0], k, b_plane_of(n)[1]))
    else:
        b_spec = pl.BlockSpec((tk, tn), lambda r, n, k: (k, n), pipeline_mode=once if n_total == tn else None)
    if out_shards is None:
        o_spec = pl.BlockSpec((tr, tn), lambda r, n, k: (r, n))
        o_shape = SDS((m, n_total), BF16)
    else:
        nps = n_total // out_shards // tn
        o_spec = pl.BlockSpec((None, tr, tn), lambda r, n, k: (n // nps, r, n % nps))
        o_shape = SDS((out_shards, m, n_total // out_shards), BF16)
    blocks = _nbytes((tk, tr), BF16) + _nbytes((tk, tn), BF16) + _nbytes((tr, tn), BF16)
    return _pcall(
        body, name=name, grid=(m // tr, n_total // tn, nk),
        in_specs=[pl.BlockSpec((tk, tr), lambda r, n, k: (k, r), pipeline_mode=once if m == tr else None), b_spec],
        out_specs=[o_spec], out_shape=[o_shape],
        scratch_shapes=[] if nk == 1 else [pltpu.VMEM((tr, tn), F32)],
        sem=("parallel", "parallel", "arbitrary"), blocks=blocks, temps=2 * _nbytes((tr, tn), F32), comm=comm,
    )(a, b)


def _wgrad3(lhs3, rhs3, comm=None):
    nw, t, d = lhs3.shape
    tk = _tile(t, WGRAD_TOKENS, SUBLANES_BF16)
    nk = t // tk

    def body(a_ref, b_ref, o_ref, *acc):
        part = _dot_ta(a_ref[...], b_ref[...])
        if nk == 1:
            o_ref[...] = part.astype(BF16)
        else:
            _acc_over(pl.program_id(1), nk, part, acc[0], o_ref)

    blocks = 2 * _nbytes((tk, d), BF16) + _nbytes((d, d), BF16)
    return _pcall(
        body, name="wgrad_sq3", grid=(nw, nk),
        in_specs=[pl.BlockSpec((None, tk, d), lambda w, k: (w, k, 0)),
                  pl.BlockSpec((None, tk, d), lambda w, k: (w, k, 0))],
        out_specs=[pl.BlockSpec((None, d, d), lambda w, k: (w, 0, 0))],
        out_shape=[SDS((nw, d, d), BF16)],
        scratch_shapes=[] if nk == 1 else [pltpu.VMEM((d, d), F32)],
        sem=("parallel", "arbitrary"), blocks=blocks, temps=2 * _nbytes((d, d), F32), comm=comm,
    )(lhs3, rhs3)


def _ffn_bwd_dx1(du0, w_up, x1, dx2, g_ffn, n_planes_out, comm=None):
    _, t, f = du0.shape
    d = x1.shape[1]
    nsh, _, ws = w_up.shape
    tm = _tile(t, 256, SUBLANES_BF16)
    spp = f // ws

    def body(du_ref, w_ref, x1_ref, dx2_ref, g_ref, dx1_ref, dxb_ref, gg_ref):
        @pl.when(pl.program_id(0) == 0)
        def _():
            gg_ref[...] = jnp.zeros_like(gg_ref)

        dh = None
        for k in range(nsh):
            part = _dot_tb(du_ref[k // spp, :, (k % spp) * ws:(k % spp + 1) * ws], w_ref[k])
            dh = part if dh is None else dh + part
        xh, inv = _rms_fwd(x1_ref[...])
        gg_ref[...] += jnp.sum(dh * xh, axis=0, keepdims=True)
        dx1 = dx2_ref[...] + _rms_bwd(dh, xh, inv, g_ref[...])
        dx1_ref[...] = dx1
        dxb_ref[...] = dx1.astype(BF16)

    blocks = _nbytes((2, tm, f), BF16) + 3 * _nbytes((tm, d), F32) + _nbytes((tm, d), BF16)
    return _pcall(
        body, name="ffn_bwd_dx1", grid=(t // tm,),
        in_specs=[pl.BlockSpec((2, tm, f), lambda i: (0, i, 0)),
                  pl.BlockSpec((nsh, d, ws), lambda i: (0, 0, 0), pipeline_mode=pl.Buffered(1)),
                  pl.BlockSpec((tm, d), lambda i: (i, 0)),
                  pl.BlockSpec((tm, d), lambda i: (i, 0)),
                  pl.BlockSpec((1, d), lambda i: (0, 0))],
        out_specs=[pl.BlockSpec((tm, d), lambda i: (i, 0)),
                   pl.BlockSpec((None, tm, d), lambda i: (n_planes_out - 1, i, 0)),
                   pl.BlockSpec((1, d), lambda i: (0, 0))],
        out_shape=[SDS((t, d), F32), SDS((n_planes_out, t, d), BF16), SDS((1, d), F32)],
        sem=("arbitrary",), blocks=blocks, temps=_nbytes(w_up.shape, BF16) + 8 * _nbytes((tm, d), F32), comm=comm,
    )(du0, w_up, x1, dx2, g_ffn)


def _mixer_bwd(rhs3, z, ypc, w3, comm=None):
    _, t, d = rhs3.shape
    tm = _tile(t, 512, SUBLANES_BF16)

    def body(dx_ref, zgp, zgc, ypc_ref, w_ref, dyo, dzo, dpq):
        dm = _dot_tb(dx_ref[...], w_ref[2])
        sp = _sigmoid(zgp[...].astype(F32))
        sc = _sigmoid(zgc[...].astype(F32))
        dyp = (dm * sp).astype(BF16)
        dyc = (dm * sc).astype(BF16)
        dzo[0] = (dm * ypc_ref[0].astype(F32) * sp * (1.0 - sp)).astype(BF16)
        dzo[1] = (dm * ypc_ref[1].astype(F32) * sc * (1.0 - sc)).astype(BF16)
        dyo[0] = dyp
        dyo[1] = dyc
        dpq[0] = _dot_tb(dyp, w_ref[0]).astype(BF16)
        dpq[1] = _dot_tb(dyc, w_ref[1]).astype(BF16)

    blocks = _nbytes((tm, d), BF16) * 3 + _nbytes((2, tm, d), BF16) * 4 + _nbytes((3, d, d), BF16)
    return _pcall(
        body, name="mixer_bwd", grid=(t // tm,),
        in_specs=[pl.BlockSpec((None, tm, d), lambda i: (2, i, 0)),
                  pl.BlockSpec((tm, d), lambda i: (i, 4)),
                  pl.BlockSpec((tm, d), lambda i: (i, 5)),
                  pl.BlockSpec((2, tm, d), lambda i: (0, i, 0)),
                  pl.BlockSpec((3, d, d), lambda i: (0, 0, 0))],
        out_specs=[pl.BlockSpec((2, tm, d), lambda i: (0, i, 0)),
                   pl.BlockSpec((2, tm, d), lambda i: (2, i, 0)),
                   pl.BlockSpec((2, tm, d), lambda i: (0, i, 0))],
        out_shape=[SDS(rhs3.shape, BF16), SDS((N_SPLITS, t, d), BF16), SDS((2, t, d), BF16)],
        input_output_aliases={0: 0},
        sem=("parallel",), blocks=blocks, temps=8 * _nbytes((tm, d), F32), comm=comm,
    )(rhs3, z, z, ypc, w3)


def _conv_bwd(dz, dpq, z, conv_w, nseq, comm=None):
    _, t, d = dz.shape
    s = t // nseq
    c = _tile(d, 256, LANES)
    nb = d // c

    def body(dz_in, dq_ref, zb, zc, zv, cw, dzo, gw_ref):
        del dz_in

        @pl.when(pl.program_id(1) == 0)
        def _():
            gw_ref[...] = jnp.zeros_like(gw_ref)

        row = lax.broadcasted_iota(jnp.int32, (s, c), 0)
        b = zb[...].astype(F32)
        cm = zc[...].astype(F32)
        v = zv[...].astype(F32)
        cv = cm * v
        cv1 = _shift_down(cv, 1, row)
        cv2 = _shift_down(cv, 2, row)
        w0, w1, w2 = cw[pl.ds(0, 1), :], cw[pl.ds(1, 1), :], cw[pl.ds(2, 1), :]
        cc = w2 * cv + w1 * cv1 + w0 * cv2
        dq = dq_ref[...].astype(F32)
        dzo[0] = (dq * cc).astype(BF16)
        dcc = dq * b
        gw_ref[pl.ds(0, 1), :] += jnp.sum(dcc * cv2, axis=0, keepdims=True)
        gw_ref[pl.ds(1, 1), :] += jnp.sum(dcc * cv1, axis=0, keepdims=True)
        gw_ref[pl.ds(2, 1), :] += jnp.sum(dcc * cv, axis=0, keepdims=True)
        dcv = w2 * dcc + w1 * _shift_up(dcc, 1, row) + w0 * _shift_up(dcc, 2, row)
        dzo[1] = (dcv * v).astype(BF16)
        dzo[2] = (dcv * cm).astype(BF16)

    blocks = 4 * _nbytes((s, c), BF16) + _nbytes((3, s, c), BF16)
    return _pcall(
        body, name="conv_bwd", grid=(nb, nseq),
        in_specs=[ANY,
                  pl.BlockSpec((None, s, c), lambda j, b: (1, b, j)),
                  pl.BlockSpec((s, c), lambda j, b: (b, nb + j)),
                  pl.BlockSpec((s, c), lambda j, b: (b, 2 * nb + j)),
                  pl.BlockSpec((s, c), lambda j, b: (b, 3 * nb + j)),
                  pl.BlockSpec((3, c), lambda j, b: (0, j))],
        out_specs=[pl.BlockSpec((3, s, c), lambda j, b: (0, b, j)),
                   pl.BlockSpec((3, c), lambda j, b: (0, j))],
        out_shape=[SDS(dz.shape, BF16), SDS((3, d), F32)],
        input_output_aliases={0: 0},
        sem=("parallel", "arbitrary"), blocks=blocks, temps=16 * _nbytes((s, c), F32), comm=comm,
    )(dz, dpq, z, z, z, conv_w)


def _pool_bwd_call(dz, dpq, z, pool_w, pool_scale, nseq, comm=None):
    _, t, d = dz.shape
    s = t // nseq
    c = d // N_GROUPS

    def body(dz_in, dp_ref, zp, pw, ps, dzo, gpw_ref, gps_ref):
        del dz_in
        j = pl.program_id(0)

        @pl.when(pl.program_id(1) == 0)
        def _():
            gpw_ref[...] = jnp.zeros_like(gpw_ref)
            gps_ref[...] = jnp.zeros_like(gps_ref)

        row = lax.broadcasted_iota(jnp.int32, (s, c), 0)
        for gi, win in enumerate(POOL_WINDOWS):
            @pl.when(j == gi)
            def _(win=win):
                pb = _pool_fwd(zp[...].astype(F32), win, row).astype(BF16)
                plin = _dot(pb, pw[...])
                dps = dp_ref[...].astype(F32)
                gps_ref[...] += jnp.sum(dps * plin, axis=0, keepdims=True)
                dplb = (dps * ps[...]).astype(BF16)
                gpw_ref[...] += _dot_ta(pb, dplb)
                dzo[...] = _pool_bwd(_dot_tb(dplb, pw[...]), win, row).astype(BF16)

    blocks = 3 * _nbytes((s, c), BF16) + _nbytes((c, c), BF16) + _nbytes((c, c), F32)
    return _pcall(
        body, name="pool_bwd", grid=(N_GROUPS, nseq),
        in_specs=[ANY,
                  pl.BlockSpec((None, s, c), lambda j, b: (0, b, j)),
                  pl.BlockSpec((s, c), lambda j, b: (b, j)),
                  pl.BlockSpec((None, c, c), lambda j, b: (j, 0, 0)),
                  pl.BlockSpec((1, c), lambda j, b: (0, j))],
        out_specs=[pl.BlockSpec((None, s, c), lambda j, b: (3, b, j)),
                   pl.BlockSpec((None, c, c), lambda j, b: (j, 0, 0)),
                   pl.BlockSpec((1, c), lambda j, b: (0, j))],
        out_shape=[SDS(dz.shape, BF16), SDS((N_GROUPS, c, c), F32), SDS((1, d), F32)],
        input_output_aliases={0: 0},
        sem=("parallel", "arbitrary"), blocks=blocks, temps=10 * _nbytes((s, c), F32), comm=comm,
    )(dz, dpq, z, pool_w, pool_scale)


def _dz_plane(zb):
    return jnp.where(zb < 4, (zb + 3) % 4, zb)


def _wgrad_in(h1, dz, nsh, comm=None):
    t, d = h1.shape
    ws = N_SPLITS * d // nsh
    kb = _tile(math.gcd(d, ws), 512, LANES)
    npl = d // kb
    nps = ws // kb
    tk = _tile(t, WGRAD_TOKENS_WIDE, SUBLANES_BF16)
    nk = t // tk

    def body(a_ref, b_ref, o_ref, *acc):
        part = _dot_ta(a_ref[...], b_ref[...])
        if nk == 1:
            o_ref[...] = part.astype(BF16)
        else:
            _acc_over(pl.program_id(1), nk, part, acc[0], o_ref)

    blocks = _nbytes((tk, d), BF16) + _nbytes((tk, kb), BF16) + _nbytes((d, kb), BF16)
    return _pcall(
        body, name="wgrad_in", grid=(N_SPLITS * npl, nk),
        in_specs=[pl.BlockSpec((tk, d), lambda cb, k: (k, 0), pipeline_mode=pl.Buffered(1) if nk == 1 else None),
                  pl.BlockSpec((None, tk, kb), lambda cb, k: (_dz_plane(cb // npl), k, cb % npl))],
        out_specs=[pl.BlockSpec((None, d, kb), lambda cb, k: (cb // nps, 0, cb % nps))],
        out_shape=[SDS((nsh, d, ws), BF16)],
        scratch_shapes=[] if nk == 1 else [pltpu.VMEM((d, kb), F32)],
        sem=("parallel", "arbitrary"), blocks=blocks, temps=2 * _nbytes((d, kb), F32), comm=comm,
    )(h1, dz)


def _mixer_bwd_dx(dz, w_in, x, dx1, g_mix, comm=None):
    npln, t, d = dz.shape
    nsh, _, ws = w_in.shape
    tm = _tile(t, 256, SUBLANES_BF16)
    kb = _tile(math.gcd(d, ws), 512, LANES)
    npl = d // kb
    nps = ws // kb

    def body(dz_ref, w_ref, x_ref, dx1_ref, g_ref, dx_ref, gg_ref):
        @pl.when(pl.program_id(0) == 0)
        def _():
            gg_ref[...] = jnp.zeros_like(gg_ref)

        dh = None
        for cb in range(npln * npl):
            zb = cb // npl
            plane = (zb + 3) % 4 if zb < 4 else zb
            part = _dot_tb(dz_ref[plane, :, (cb % npl) * kb:(cb % npl + 1) * kb],
                           w_ref[cb // nps, :, (cb % nps) * kb:(cb % nps + 1) * kb])
            dh = part if dh is None else dh + part
        xh, inv = _rms_fwd(x_ref[...])
        gg_ref[...] += jnp.sum(dh * xh, axis=0, keepdims=True)
        dx_ref[...] = dx1_ref[...] + _rms_bwd(dh, xh, inv, g_ref[...])

    blocks = _nbytes((npln, tm, d), BF16) + 3 * _nbytes((tm, d), F32)
    return _pcall(
        body, name="mixer_bwd_dx", grid=(t // tm,),
        in_specs=[pl.BlockSpec((npln, tm, d), lambda i: (0, i, 0)),
                  pl.BlockSpec((nsh, d, ws), lambda i: (0, 0, 0), pipeline_mode=pl.Buffered(1)),
                  pl.BlockSpec((tm, d), lambda i: (i, 0)),
                  pl.BlockSpec((tm, d), lambda i: (i, 0)),
                  pl.BlockSpec((1, d), lambda i: (0, 0))],
        out_specs=[pl.BlockSpec((tm, d), lambda i: (i, 0)),
                   pl.BlockSpec((1, d), lambda i: (0, 0))],
        out_shape=[SDS((t, d), F32), SDS((1, d), F32)],
        sem=("arbitrary",), blocks=blocks, temps=_nbytes(w_in.shape, BF16) + 8 * _nbytes((tm, d), F32), comm=comm,
    )(dz, w_in, x, dx1, g_mix)


N_BIG = 5
SHARD_MAJOR = (0, 2)
ROWS_DIM1 = (1, 4)


def _ds(start, size, align):
    if isinstance(start, int):
        return pl.ds(start, size)
    return pl.ds(pl.multiple_of(start, align), size)


def _piece(a, ref, k, h):
    if a in SHARD_MAJOR:
        r = ref.shape[1] // 2
        return ref.at[k, _ds(h * r, r, SUBLANES_BF16), :]
    if a in ROWS_DIM1:
        r = ref.shape[1] // 8
        return ref.at[:, _ds((2 * k + h) * r, r, SUBLANES_BF16), :]
    r = ref.shape[0] // 8
    return ref.at[_ds((2 * k + h) * r, r, SUBLANES_BF16), :]


def _half(a, ref, h):
    if a in ROWS_DIM1:
        r = ref.shape[1] // 2
        return ref.at[:, _ds(h * r, r, SUBLANES_BF16), :]
    r = ref.shape[0] // 2
    return ref.at[_ds(h * r, r, SUBLANES_BF16), :]


def _piece_shape(a, full_shape):
    if a in SHARD_MAJOR:
        return (full_shape[1] // 2, full_shape[2])
    if a in ROWS_DIM1:
        return (full_shape[0], full_shape[1] // 8, full_shape[2])
    return (full_shape[0] // 8, full_shape[1])


def _shard_shape(a, full_shape):
    if a in SHARD_MAJOR:
        return (full_shape[1], full_shape[2])
    if a in ROWS_DIM1:
        return (full_shape[0], full_shape[1] // 4, full_shape[2])
    return (full_shape[0] // 4, full_shape[1])


def _rows_axis(a):
    return 1 if a in ROWS_DIM1 else 0


def _piece_block(a, full_shape):
    ps = _piece_shape(a, full_shape)
    if a in SHARD_MAJOR:
        return (None,) + ps, lambda k, c: (k, c, 0)
    if a in ROWS_DIM1:
        return ps, lambda k, c: (0, 2 * k + c, 0)
    return ps, lambda k, c: (2 * k + c, 0)


def _coords():
    return lax.axis_index("x"), lax.axis_index("y"), lax.axis_index("c")


def _peer_chips(x, y):
    return [(1 - x, y), (x, 1 - y), (1 - x, 1 - y)]


def _remote(src, dst, ssem, rsem, dev):
    return pltpu.make_async_remote_copy(src_ref=src, dst_ref=dst, send_sem=ssem, recv_sem=rsem,
                                        device_id=dev, device_id_type=MESH)


def _dma_sems(*counts):
    return [pltpu.SemaphoreType.DMA((n,)) for n in counts]


def _symmetric(ins, out_shapes, sems, copies, peers, aliases=None):
    def start(cins, couts, csems):
        for cp in copies(cins, couts, csems):
            cp.start()

    def finish(cins, couts, csems):
        for cp in copies(cins, couts, csems):
            cp.wait()

    return _Comm(ins, out_shapes, sems, start, finish, peers, aliases)


def _rows_part(a, ref, part):
    if part is None:
        return ref
    p, q, n = part
    ax = _rows_axis(a)
    r = ref.shape[ax] // n
    return ref.at[tuple(pl.ds(p * r, (q - p) * r) if d == ax else slice(None) for d in range(len(ref.shape)))]


def _merge(comms):
    ins, outs, sems, aliases, spans = [], [], [], {}, []
    for cm in comms:
        spans.append((len(ins), len(outs), len(sems)))
        for i, o in cm.aliases.items():
            aliases[len(ins) + i] = len(outs) + o
        ins += cm.ins
        outs += cm.out_shapes
        sems += cm.sems

    def each(fn_name):
        def run(cins, couts, csems):
            for cm, (i0, o0, s0) in zip(comms, spans):
                fn = getattr(cm, fn_name)
                if fn is not None:
                    fn(cins[i0:i0 + len(cm.ins)], couts[o0:o0 + len(cm.out_shapes)], csems[s0:s0 + len(cm.sems)])
        return run

    return _Comm(ins, outs, sems, each("start"), each("finish"), frozenset().union(*[cm.peers for cm in comms]),
                 aliases, mid=each("mid") if any(cm.mid is not None for cm in comms) else None)


def _gather_comm(arrs, locs, full_shapes, part=None, into=None):
    n = len(arrs)

    def own(cins, couts, csems):
        x, y, c = _coords()
        j = 2 * x + y
        return [_remote(_rows_part(a, _half(a, cins[q], h), part), _rows_part(a, _piece(a, couts[q], j, h), part),
                        csems[0].at[2 * q + h], csems[1].at[2 * q + h], (x, y, 1 - c))
                for q, a in enumerate(arrs) for h in range(2)]

    def sends(cins, couts, csems):
        x, y, c = _coords()
        j = 2 * x + y
        return [_remote(_rows_part(a, _half(a, cins[q], c), part), _rows_part(a, _piece(a, couts[q], j, c), part),
                        csems[2].at[3 * q + i], csems[3].at[3 * q + i], (px, py, c))
                for q, a in enumerate(arrs) for i, (px, py) in enumerate(_peer_chips(x, y))]

    def forwards(couts, csems, half_of):
        x, y, c = _coords()
        out = []
        for q, a in enumerate(arrs):
            for i, (px, py) in enumerate(_peer_chips(x, y)):
                landed = _rows_part(a, _piece(a, couts[q], 2 * px + py, half_of(c)), part)
                out.append(_remote(landed, landed, csems[4].at[3 * q + i], csems[5].at[3 * q + i], (x, y, 1 - c)))
        return out

    def start(cins, couts, csems):
        for cp in sends(cins, couts, csems) + own(cins, couts, csems):
            cp.start()

    def finish(cins, couts, csems):
        fw = forwards(couts, csems, lambda c: c)
        for cp, f in zip(sends(cins, couts, csems), fw):
            cp.wait_recv()
            f.start()
        for f in forwards(couts, csems, lambda c: 1 - c):
            f.wait_recv()
        for cp in sends(cins, couts, csems) + fw:
            cp.wait_send()
        for cp in own(cins, couts, csems):
            cp.wait()

    ins = [locs[a] for a in arrs] + ([into[a] for a in arrs] if into else [])
    return _Comm(ins, [SDS(full_shapes[a], BF16) for a in arrs],
                 _dma_sems(2 * n, 2 * n, 3 * n, 3 * n, 3 * n, 3 * n), start, finish, CHIPS + (SIBLING,),
                 aliases={n + q: q for q in range(n)} if into else None)


def _ring_gather_comm(arrs, locs, full_shapes):
    n = len(arrs)

    def own(cins, couts, csems):
        x, y, c = _coords()
        j = 2 * x + y
        return [_remote(_half(a, cins[q], h), _piece(a, couts[q], j, h), csems[0].at[2 * q + h],
                        csems[1].at[2 * q + h], (x, y, 1 - c)) for q, a in enumerate(arrs) for h in range(2)]

    def sends(cins, couts, csems):
        x, y, c = _coords()
        j = 2 * x + y
        return [_remote(_half(a, cins[q], c), _piece(a, couts[q], j, c), csems[2].at[2 * q + i],
                        csems[3].at[2 * q + i], (px, py, c))
                for q, a in enumerate(arrs) for i, (px, py) in enumerate(_peer_chips(x, y)[:2])]

    def relays(couts, csems):
        x, y, c = _coords()
        peers = _peer_chips(x, y)
        out = []
        for q, a in enumerate(arrs):
            for r, (src_p, dst_p) in enumerate(((0, 1), (1, 0))):
                sx, sy = peers[src_p]
                rows = _rows_part(a, _piece(a, couts[q], 2 * sx + sy, c), (r, r + 1, 2))
                out.append(_remote(rows, rows, csems[6].at[2 * q + r], csems[7].at[2 * q + r], (*peers[dst_p], c)))
        return out

    def forwards(couts, csems, half_of, which):
        x, y, c = _coords()
        out = []
        for q, a in enumerate(arrs):
            for i in which:
                px, py = _peer_chips(x, y)[i]
                landed = _piece(a, couts[q], 2 * px + py, half_of(c))
                out.append(_remote(landed, landed, csems[4].at[3 * q + i], csems[5].at[3 * q + i], (x, y, 1 - c)))
        return out

    def start(cins, couts, csems):
        for cp in sends(cins, couts, csems) + own(cins, couts, csems):
            cp.start()

    def mid(cins, couts, csems):
        for cp in sends(cins, couts, csems):
            cp.wait_recv()
        for cp in relays(couts, csems) + forwards(couts, csems, lambda c: c, (0, 1)):
            cp.start()

    def finish(cins, couts, csems):
        for cp in relays(couts, csems):
            cp.wait_recv()
        fw_diag = forwards(couts, csems, lambda c: c, (2,))
        for f in fw_diag:
            f.start()
        for f in forwards(couts, csems, lambda c: 1 - c, (0, 1, 2)):
            f.wait_recv()
        for cp in (sends(cins, couts, csems) + relays(couts, csems)
                   + forwards(couts, csems, lambda c: c, (0, 1)) + fw_diag):
            cp.wait_send()
        for cp in own(cins, couts, csems):
            cp.wait()

    return _Comm([locs[a] for a in arrs], [SDS(full_shapes[a], BF16) for a in arrs],
                 _dma_sems(2 * n, 2 * n, 2 * n, 2 * n, 3 * n, 3 * n, 2 * n, 2 * n), start, finish,
                 CHIPS + (SIBLING,), mid=mid)


def _halves_comm(arrs, gbs):
    n = len(arrs)

    def copies(cins, couts, csems):
        x, y, c = _coords()
        return [_remote(_piece(a, cins[q], k, 1 - c), couts[q].at[k], csems[0].at[4 * q + k], csems[1].at[4 * q + k],
                        (x, y, 1 - c)) for q, a in enumerate(arrs) for k in range(4)]

    return _symmetric([gbs[a] for a in arrs], [SDS((4,) + _piece_shape(a, gbs[a].shape), BF16) for a in arrs],
                      _dma_sems(4 * n, 4 * n), copies, [SIBLING])


def _chips_comm(arrs, ps, part=None, into=None):
    n = len(arrs)

    def copies(cins, couts, csems):
        x, y, c = _coords()
        return [_remote(_rows_part(a, cins[q].at[2 * px + py], part), _rows_part(a, couts[q].at[i], part),
                        csems[0].at[3 * q + i], csems[1].at[3 * q + i], (px, py, c))
                for q, a in enumerate(arrs) for i, (px, py) in enumerate(_peer_chips(x, y))]

    ins = [ps[a] for a in arrs] + ([into[a] for a in arrs] if into else [])
    return _symmetric(ins, [SDS((3,) + ps[a].shape[1:], BF16) for a in arrs], _dma_sems(3 * n, 3 * n), copies, CHIPS,
                      aliases={n + q: q for q in range(n)} if into else None)


def _result_comm(arrs, gs):
    n = len(arrs)

    def copies(cins, couts, csems):
        x, y, c = _coords()
        return [_remote(_half(a, cins[q], c), _half(a, couts[q], c), csems[0].at[q], csems[1].at[q], (x, y, 1 - c))
                for q, a in enumerate(arrs)]

    return _symmetric([gs[a] for a in arrs], [SDS(gs[a].shape, F32) for a in arrs], _dma_sems(n, n), copies,
                      [SIBLING], aliases={q: q for q in range(n)})


def _add_halves(arrs, gbs, lands, c_arr, name):
    n = len(arrs)

    def body(c_ref, *refs):
        del c_ref
        for q in range(n):
            refs[2 * n + q][...] = (refs[q][...].astype(F32) + refs[n + q][...].astype(F32)).astype(BF16)

    g_specs, l_specs, o_specs, blocks = [], [], [], 0
    for a in arrs:
        bs, imap = _piece_block(a, gbs[a].shape)
        ps = _piece_shape(a, gbs[a].shape)
        g_specs.append(pl.BlockSpec(bs, lambda k, c_ref, imap=imap: imap(k, c_ref[0])))
        nd = len(ps)
        l_specs.append(pl.BlockSpec((None,) + ps, lambda k, c_ref, nd=nd: (k,) + (0,) * nd))
        o_specs.append(pl.BlockSpec((None,) + ps, lambda k, c_ref, nd=nd: (k,) + (0,) * nd))
        blocks += 3 * _nbytes(ps, BF16)
    return list(pl.pallas_call(
        body, name=name,
        grid_spec=pltpu.PrefetchScalarGridSpec(
            num_scalar_prefetch=1, grid=(4,), in_specs=g_specs + l_specs, out_specs=o_specs),
        out_shape=[SDS((4,) + _piece_shape(a, gbs[a].shape), BF16) for a in arrs],
        compiler_params=_params(("parallel",), blocks, blocks),
    )(c_arr, *[gbs[a] for a in arrs], *lands))


def _sum_chips(a, p, land, shard_shape, jc_arr, name):
    ps = land.shape[1:]
    ax = _rows_axis(a)
    rows = ps[ax]
    nsub = 2 if rows % (2 * SUBLANES_BF16) == 0 else 1
    bs = tuple(r // nsub if q == ax else r for q, r in enumerate(ps))
    nd = len(ps)

    def at_rows(v):
        return tuple(v if q == ax else 0 for q in range(nd))

    def body(jc_ref, p_ref, l_ref, o_ref):
        del jc_ref
        acc = p_ref[...].astype(F32) + l_ref[0].astype(F32)
        acc = acc + l_ref[1].astype(F32)
        o_ref[...] = acc + l_ref[2].astype(F32)

    blocks = 4 * _nbytes(bs, BF16) + _nbytes(bs, F32)
    return pl.pallas_call(
        body, name=name,
        grid_spec=pltpu.PrefetchScalarGridSpec(
            num_scalar_prefetch=1, grid=(nsub,),
            in_specs=[pl.BlockSpec((None,) + bs, lambda s, jc: (jc[0],) + at_rows(s)),
                      pl.BlockSpec((3,) + bs, lambda s, jc: (0,) + at_rows(s))],
            out_specs=pl.BlockSpec(bs, lambda s, jc: at_rows(jc[1] * nsub + s))),
        out_shape=SDS(shard_shape, F32),
        compiler_params=_params(("parallel",), blocks, 2 * _nbytes(bs, F32)),
    )(jc_arr, p, land)


def _small_comm(v):
    rows = v.shape[0]

    def copies(cins, couts, csems):
        x, y, c = _coords()
        me = 4 * x + 2 * y + c
        out = [pltpu.make_async_copy(cins[0], couts[0].at[me], csems[0].at[0])]
        for dlt in range(1, 8):
            px = 1 - x if (dlt >> 2) & 1 else x
            py = 1 - y if (dlt >> 1) & 1 else y
            pc = 1 - c if dlt & 1 else c
            out.append(_remote(cins[0], couts[0].at[me], csems[1].at[dlt - 1], csems[2].at[dlt - 1], (px, py, pc)))
        return out

    return _symmetric([v], [SDS((8, rows, LANES), F32)], _dma_sems(1, 7, 7), copies, EVERYONE)


def _sum8(slots, name):
    def body(s_ref, o_ref):
        acc = s_ref[0]
        for i in range(1, 8):
            acc = acc + s_ref[i]
        o_ref[...] = acc

    return pl.pallas_call(
        body, name=name,
        in_specs=[pl.BlockSpec(memory_space=pltpu.VMEM)], out_specs=pl.BlockSpec(memory_space=pltpu.VMEM),
        out_shape=SDS(slots.shape[1:], F32),
    )(slots)


def _adamw(w, g, m, v, name, g_plane=None):
    rows, cols = w.shape
    tr = _tile(rows, max(SUBLANES_F32, (512 * 1024 // cols) // SUBLANES_F32 * SUBLANES_F32), SUBLANES_F32)

    def body(w_ref, g_ref, m_ref, v_ref, go_ref, d_ref, mo_ref, vo_ref):
        gr = g_ref[...]
        mn = ADAM_B1 * m_ref[...] + (1.0 - ADAM_B1) * gr
        vn = ADAM_B2 * v_ref[...] + (1.0 - ADAM_B2) * (gr * gr)
        m_hat = mn / (1.0 - ADAM_B1 ** ADAM_STEP)
        v_hat = vn / (1.0 - ADAM_B2 ** ADAM_STEP)
        d_ref[...] = -ADAM_LR * (m_hat / (jnp.sqrt(v_hat) + ADAM_EPS) + ADAM_WD * w_ref[...])
        go_ref[...] = gr
        mo_ref[...] = mn
        vo_ref[...] = vn

    spec = pl.BlockSpec((tr, cols), lambda i: (i, 0))
    g_spec = spec if g_plane is None else pl.BlockSpec((None, tr, cols), lambda i: (g_plane, i, 0))
    return pl.pallas_call(
        body, name=name, grid=(rows // tr,),
        in_specs=[spec, g_spec, spec, spec], out_specs=[spec, spec, spec, spec],
        out_shape=[SDS((rows, cols), F32)] * 4,
        compiler_params=_params(("parallel",), 8 * _nbytes((tr, cols), F32), 4 * _nbytes((tr, cols), F32)),
    )(w, g, m, v)


def _pack(parts):
    rows = []
    for p in parts:
        r = p.reshape(-1, LANES)
        pad = (-r.shape[0]) % SUBLANES_F32
        if pad:
            r = jnp.pad(r, ((0, pad), (0, 0)))
        rows.append(r)
    return jnp.concatenate(rows, axis=0)


def _unpack(packed, shapes):
    out, at = [], 0
    for s in shapes:
        n = 1
        for q in s:
            n *= q
        r = n // LANES
        out.append(packed[at:at + r].reshape(s))
        at += r + (-r) % SUBLANES_F32
    return out


def kernel(x, norm_mix, w_in, pool_w, pool_scale, w_pool_proj, conv_w, w_conv_out, w_o, norm_ffn, w_up, ffn_conv_w, ffn_conv_b, w_down, norm_final, loss_target, m_norm_mix, m_w_in, m_pool_w, m_pool_scale, m_w_pool_proj, m_conv_w, m_w_conv_out, m_w_o, m_norm_ffn, m_w_up, m_ffn_conv_w, m_ffn_conv_b, m_w_down, m_norm_final, v_norm_mix, v_w_in, v_pool_w, v_pool_scale, v_w_pool_proj, v_conv_w, v_w_conv_out, v_w_o, v_norm_ffn, v_w_up, v_ffn_conv_w, v_ffn_conv_b, v_w_down, v_norm_final):
    nseq, seq, d = x.shape
    t = nseq * seq
    f = w_down.shape[1] * 4
    c = d // N_GROUPS
    xy = lax.axis_index("x") * 2 + lax.axis_index("y")
    c_arr = lax.axis_index("c").astype(jnp.int32).reshape(1)
    jc_arr = jnp.stack([xy, lax.axis_index("c")]).astype(jnp.int32)
    nsh = 4
    zero = jnp.zeros((), jnp.int32)

    locs = [w_in[0].astype(BF16),
            jnp.stack([w_pool_proj[0], w_conv_out[0], w_o[0]]).astype(BF16),
            w_up[0].astype(BF16), w_down[0].astype(BF16), pool_w[0].astype(BF16)]
    full_shapes = [(nsh, d, N_SPLITS * d // nsh), (3, d, d), (nsh, d, 2 * f // nsh), (f, d), (N_GROUPS, c, c)]

    cw_pad = lax.dynamic_update_slice(jnp.zeros((3, d), F32), conv_w[0], (zero, xy * (d // 4)))
    fw_pad = lax.dynamic_update_slice(jnp.zeros((3, 2 * f), F32), ffn_conv_w[0], (zero, xy * (f // 2)))
    small_w = _pack([cw_pad, fw_pad]) * 0.5

    x2d = x.reshape(t, d)
    tgt = loss_target.reshape(t, d)
    ax, ay = lax.axis_index("x"), lax.axis_index("y")
    order = jnp.stack([xy, 2 * (1 - ax) + ay, 2 * ax + 1 - ay, 2 * (1 - ax) + 1 - ay]).astype(jnp.int32)
    (z, h1, w_in_f), (pool_w_f, w3_f, slots_w) = _fwd_in(
        x2d, norm_mix, locs[0], order,
        _merge([_gather_comm([4], locs, full_shapes), _gather_comm([1], locs, full_shapes, part=(0, 1, 2)),
                _small_comm(small_w)]))
    conv_w_f, ffn_cw_f = _unpack(_sum8(slots_w, "sum8_weights"), [(3, d), (3, 2 * f)])
    ffn_cw_p = ffn_cw_f.reshape(3, 2, f).transpose(1, 0, 2)
    ffn_cb_p = ffn_conv_b.reshape(2, 1, f)
    (lhs3,), (w3_f,) = _mixer_mid_fwd(z, pool_w_f, pool_scale, conv_w_f, nseq,
                                      _gather_comm([1], locs, full_shapes, part=(1, 2, 2), into={1: w3_f}))
    (lhs3, ypc, x1, h2), (w_up_f,) = _mixer_out(lhs3, z, x2d, w3_f, norm_ffn,
                                                _ring_gather_comm([2], locs, full_shapes))
    (u0,), (w_down_f,) = _ffn_up(h2, w_up_f, f, _gather_comm([3], locs, full_shapes))
    act, ua = _ffn_mid_fwd(u0, ffn_cw_p, ffn_cb_p, nseq)
    dx2, dx2b, loss11, g_norm_final = _ffn_down_loss(act, w_down_f, x1, tgt, norm_final.reshape(1, d))

    gbs, lands, ps, lands2, rs = {}, {}, {}, {}, {}
    tn_up = _tile(2 * f // nsh, 1408, LANES)
    npp = f // tn_up

    def add(arrs, name):
        for a, p in zip(arrs, _add_halves(arrs, gbs, [lands[a] for a in arrs], c_arr, name)):
            ps[a] = p

    def summed(a):
        rs[a] = _sum_chips(a, ps[a], lands2[a], _shard_shape(a, full_shapes[a]), jc_arr, "sum_chips_%d" % a)

    (gbs[3],), _ = _wgrad(act, dx2b, "wgrad_down", tr=tn_up, tn=d)
    (da,), (lands[3],) = _ffn_bwd_da(dx2b, w_down_f, _halves_comm([3], gbs))
    add([3], "add_halves_down")
    (du0, g_ffn_cw_p, g_ffn_cb_p), (lands2[3],) = _ffn_mid_bwd(da, u0, ua, ffn_cw_p, nseq, _chips_comm([3], ps))
    summed(3)
    (gbs[2],), (rs[3],) = _wgrad(h2, du0, "wgrad_up", tr=d, tn=tn_up, b_plane_of=lambda n: (n // npp, n % npp),
                                 out_shards=nsh, comm=_result_comm([3], rs))
    (dx1, rhs3, g_norm_ffn), (lands[2],) = _ffn_bwd_dx1(du0, w_up_f, x1, dx2, norm_ffn, 3, _halves_comm([2], gbs))
    add([2], "add_halves_up")
    (rhs3, dz, dpq), (lands2[2],) = _mixer_bwd(rhs3, z, ypc, w3_f, _chips_comm([2], ps, part=(0, 1, 2)))
    (gbs[1],), (lands2[2],) = _wgrad3(lhs3, rhs3, _chips_comm([2], ps, part=(1, 2, 2), into=lands2))
    summed(2)
    (dz, g_conv_w), (lands[1], rs[2]) = _conv_bwd(dz, dpq, z, conv_w_f, nseq,
                                                  _merge([_halves_comm([1], gbs), _result_comm([2], rs)]))
    add([1], "add_halves_sq3")
    (dz, g_pool_w, g_pool_scale), _ = _pool_bwd_call(dz, dpq, z, pool_w_f, pool_scale, nseq)
    gbs[4] = g_pool_w.astype(BF16)
    (gbs[0],), (lands2[1],) = _wgrad_in(h1, dz, nsh, _chips_comm([1], ps))
    summed(1)
    lands[0], lands[4] = _run_comm(_halves_comm([0, 4], gbs), "exchange_halves_in")
    add([0, 4], "add_halves_in")
    g_ffn_cw = g_ffn_cw_p.transpose(1, 0, 2).reshape(3, 2 * f)
    small_a = _pack([g_pool_scale, g_norm_ffn, g_ffn_cb_p.reshape(1, 2 * f), g_norm_final.reshape(d), g_conv_w,
                     g_ffn_cw, jnp.pad(loss11, ((0, SUBLANES_F32 - 1), (0, LANES - 1)))])
    (grad_x, g_norm_mix), (lands2[0], lands2[4], rs[1], slots_a) = _mixer_bwd_dx(
        dz, w_in_f, x2d, dx1, norm_mix,
        _merge([_chips_comm([0, 4], ps), _result_comm([1], rs), _small_comm(small_a)]))
    summed(0)
    summed(4)
    rs[0], rs[4], slots_b = _run_comm(_merge([_result_comm([0, 4], rs), _small_comm(_pack([g_norm_mix]))]),
                                      "exchange_result_in")
    shapes_a = [(1, d), (1, d), (1, 2 * f), (d,), (3, d), (3, 2 * f), (SUBLANES_F32, LANES)]
    gs_pool_scale, gs_norm_ffn, gs_ffn_cb, gs_norm_final, gs_conv_w, gs_ffn_cw, loss_blk = _unpack(
        _sum8(slots_a, "sum8_grads"), shapes_a)
    (gs_norm_mix,) = _unpack(_sum8(slots_b, "sum8_norm_mix"), [(1, d)])
    gs_conv_w = lax.dynamic_slice(gs_conv_w, (zero, xy * (d // 4)), (3, d // 4))
    gs_ffn_cw = lax.dynamic_slice(gs_ffn_cw, (zero, xy * (f // 2)), (3, f // 2))

    def upd(w, g, m, v, name, g_plane=None):
        shape = w.shape
        rows = 1
        for q in shape[:-1]:
            rows *= q
        g2 = g if g_plane is not None else g.reshape(rows, shape[-1])
        outs = _adamw(w.reshape(rows, shape[-1]), g2, m.reshape(rows, shape[-1]), v.reshape(rows, shape[-1]),
                      name, g_plane)
        return [o.reshape(shape) for o in outs]

    res = {
        "w_in": upd(w_in, rs[0], m_w_in, v_w_in, "adamw_w_in"),
        "pool_w": upd(pool_w, rs[4], m_pool_w, v_pool_w, "adamw_pool_w"),
        "w_pool_proj": upd(w_pool_proj, rs[1], m_w_pool_proj, v_w_pool_proj, "adamw_w_pool_proj", 0),
        "w_conv_out": upd(w_conv_out, rs[1], m_w_conv_out, v_w_conv_out, "adamw_w_conv_out", 1),
        "w_o": upd(w_o, rs[1], m_w_o, v_w_o, "adamw_w_o", 2),
        "w_up": upd(w_up, rs[2], m_w_up, v_w_up, "adamw_w_up"),
        "w_down": upd(w_down, rs[3], m_w_down, v_w_down, "adamw_w_down"),
    }

    small_names = ["norm_mix", "pool_scale", "norm_ffn", "ffn_conv_b", "norm_final", "conv_w", "ffn_conv_w"]
    small_ws = [norm_mix, pool_scale, norm_ffn, ffn_conv_b, norm_final, conv_w, ffn_conv_w]
    small_ms = [m_norm_mix, m_pool_scale, m_norm_ffn, m_ffn_conv_b, m_norm_final, m_conv_w, m_ffn_conv_w]
    small_vs = [v_norm_mix, v_pool_scale, v_norm_ffn, v_ffn_conv_b, v_norm_final, v_conv_w, v_ffn_conv_w]
    small_gs = [gs_norm_mix, gs_pool_scale, gs_norm_ffn, gs_ffn_cb, gs_norm_final, gs_conv_w, gs_ffn_cw]
    _, sd, sm, sv = _adamw(_pack(small_ws), _pack(small_gs), _pack(small_ms), _pack(small_vs), "adamw_small")
    shapes = [w.shape for w in small_ws]
    sd, sm, sv = _unpack(sd, shapes), _unpack(sm, shapes), _unpack(sv, shapes)
    for i, nm in enumerate(small_names):
        res[nm] = [small_gs[i].reshape(shapes[i]), sd[i], sm[i], sv[i]]

    order = ["norm_mix", "w_in", "pool_w", "pool_scale", "w_pool_proj", "conv_w", "w_conv_out", "w_o", "norm_ffn",
             "w_up", "ffn_conv_w", "ffn_conv_b", "w_down", "norm_final"]
    return (loss_blk[0, 0], grad_x.reshape(x.shape), *[res[n][0] for n in order], *[res[n][1] for n in order],
            *[res[n][2] for n in order], *[res[n][3] for n in order])
```

```python
import math

import jax
import jax.numpy as jnp
from jax import lax
from jax.experimental import pallas as pl
from jax.experimental.pallas import tpu as pltpu

F32 = jnp.float32
BF16 = jnp.bfloat16
SDS = jax.ShapeDtypeStruct
MESH = pl.DeviceIdType.MESH

RMS_EPS = 1e-6
POOL_WINDOWS = (2, 4, 8, 16)
N_GROUPS = len(POOL_WINDOWS)
N_SPLITS = 6

ADAM_LR = 0.001
ADAM_B1 = 0.9
ADAM_B2 = 0.999
ADAM_EPS = 1e-08
ADAM_WD = 0.01
ADAM_STEP = 10

LANES = 128
SUBLANES_F32 = 8
SUBLANES_BF16 = 16
VMEM_BYTES = 64 * 1024 * 1024
VMEM_CAP = VMEM_BYTES - 8 * 1024 * 1024
VMEM_FLOOR = 16 * 1024 * 1024

ANY = pl.BlockSpec(memory_space=pl.ANY)


def _tile(dim, pref, align):
    if dim <= pref:
        return dim
    t = (pref // align) * align
    while t >= align:
        if dim % t == 0:
            return t
        t -= align
    return dim


def _nbytes(shape, dtype):
    n = 1
    for s in shape:
        n *= s
    return n * jnp.dtype(dtype).itemsize


def _params(sem, block_bytes, temp_bytes=0, collective_id=None):
    need = 2 * block_bytes + temp_bytes + 4 * 1024 * 1024
    return pltpu.CompilerParams(dimension_semantics=sem, collective_id=collective_id,
                                vmem_limit_bytes=int(min(max(need, VMEM_FLOOR), VMEM_CAP)))


SIBLING = (0, 0, 1)
CHIPS = ((1, 0, 0), (0, 1, 0), (1, 1, 0))
EVERYONE = tuple((a, b, c) for a in range(2) for b in range(2) for c in range(2) if a + b + c)
PEER_SETS = (frozenset([SIBLING]), frozenset(CHIPS), frozenset(CHIPS + (SIBLING,)), frozenset(EVERYONE))
MID_AT = 0.75


def _collective_id(peers):
    return PEER_SETS.index(frozenset(peers))


def _handshake(peers):
    x, y, c = lax.axis_index("x"), lax.axis_index("y"), lax.axis_index("c")
    bar = pltpu.get_barrier_semaphore()
    for fx, fy, fc in sorted(peers):
        dev = (1 - x if fx else x, 1 - y if fy else y, 1 - c if fc else c)
        pl.semaphore_signal(bar, inc=1, device_id=dev, device_id_type=MESH)
    pl.semaphore_wait(bar, len(peers))


class _Comm:
    def __init__(self, ins, out_shapes, sems, start, finish, peers, aliases=None, mid=None):
        self.ins = list(ins)
        self.out_shapes = list(out_shapes)
        self.sems = list(sems)
        self.start = start
        self.finish = finish
        self.mid = mid
        self.peers = frozenset(peers)
        self.aliases = dict(aliases or {})


def _pcall(body, *, name, grid, in_specs, out_specs, out_shape, sem, blocks, temps=0, scratch_shapes=(),
           input_output_aliases=None, comm=None):
    in_specs = list(in_specs)
    out_specs = list(out_specs)
    out_shape = list(out_shape)
    scratch_shapes = list(scratch_shapes)
    aliases = dict(input_output_aliases or {})
    n_in, n_out, n_scr = len(in_specs), len(out_shape), len(scratch_shapes)
    if comm is None:
        call = pl.pallas_call(
            body, name=name, grid=grid, in_specs=in_specs, out_specs=out_specs, out_shape=out_shape,
            scratch_shapes=scratch_shapes, input_output_aliases=aliases,
            compiler_params=_params(sem, blocks, temps))
        return lambda *args: (list(call(*args)), [])

    nci, nco = len(comm.ins), len(comm.out_shapes)
    n_steps = 1
    for g in grid:
        n_steps *= g

    def hosted(*refs):
        ins = refs[:n_in]
        cins = refs[n_in:n_in + nci]
        outs = refs[n_in + nci:n_in + nci + n_out]
        couts = refs[n_in + nci + n_out:n_in + nci + n_out + nco]
        scr = refs[n_in + nci + n_out + nco:n_in + nci + n_out + nco + n_scr]
        csems = refs[n_in + nci + n_out + nco + n_scr:]
        first = None
        last = None
        step = 0
        for q, g in enumerate(grid):
            pid = pl.program_id(q)
            first = (pid == 0) if first is None else first & (pid == 0)
            last = (pid == g - 1) if last is None else last & (pid == g - 1)
            step = step * g + pid

        @pl.when(first)
        def _():
            _handshake(comm.peers)
            comm.start(cins, couts, csems)

        if comm.mid is not None:
            @pl.when(step == int(MID_AT * n_steps))
            def _():
                comm.mid(cins, couts, csems)

        body(*ins, *outs, *scr)

        @pl.when(last)
        def _():
            comm.finish(cins, couts, csems)

    for i, o in comm.aliases.items():
        aliases[n_in + i] = n_out + o
    call = pl.pallas_call(
        hosted, name=name, grid=grid, in_specs=in_specs + [ANY] * nci, out_specs=out_specs + [ANY] * nco,
        out_shape=out_shape + comm.out_shapes, scratch_shapes=scratch_shapes + comm.sems,
        input_output_aliases=aliases,
        compiler_params=_params(("arbitrary",) * len(grid), blocks, temps, _collective_id(comm.peers)))

    def run(*args):
        res = call(*args, *comm.ins)
        return list(res[:n_out]), list(res[n_out:])

    return run


def _run_comm(comm, name):
    def body(*refs):
        nci, nco = len(comm.ins), len(comm.out_shapes)
        cins, couts, csems = refs[:nci], refs[nci:nci + nco], refs[nci + nco:]
        _handshake(comm.peers)
        comm.start(cins, couts, csems)
        if comm.mid is not None:
            comm.mid(cins, couts, csems)
        comm.finish(cins, couts, csems)

    return list(pl.pallas_call(
        body, name=name, in_specs=[ANY] * len(comm.ins), out_specs=[ANY] * len(comm.out_shapes),
        out_shape=comm.out_shapes, scratch_shapes=comm.sems, input_output_aliases=comm.aliases,
        compiler_params=pltpu.CompilerParams(collective_id=_collective_id(comm.peers)),
    )(*comm.ins))


def _dot(a, b):
    return jnp.dot(a, b, preferred_element_type=F32)


def _dot_tb(a, b):
    return lax.dot_general(a, b, (((1,), (1,)), ((), ())), preferred_element_type=F32)


def _dot_ta(a, b):
    return lax.dot_general(a, b, (((0,), (0,)), ((), ())), preferred_element_type=F32)


def _rms_fwd(x):
    inv = lax.rsqrt(jnp.mean(x * x, axis=-1, keepdims=True) + RMS_EPS)
    return x * inv, inv


def _rms_bwd(dy, xhat, inv, g):
    gd = dy * g
    return inv * (gd - xhat * jnp.mean(gd * xhat, axis=-1, keepdims=True))


def _sigmoid(x):
    return 1.0 / (1.0 + jnp.exp(-x))


def _shift_down(x, k, row):
    return jnp.where(row >= k, pltpu.roll(x, k, 0), 0.0)


def _shift_up(x, k, row):
    s = x.shape[0]
    return jnp.where(row < s - k, pltpu.roll(x, s - k, 0), 0.0)


def _pool_fwd(u, win, row):
    s = u
    k = 1
    while k < win:
        s = s + _shift_down(s, k, row)
        k *= 2
    cnt = jnp.minimum(row + 1, win).astype(F32)
    return s / cnt - u


def _pool_bwd(dp, win, row):
    cnt = jnp.minimum(row + 1, win).astype(F32)
    s = dp / cnt
    k = 1
    while k < win:
        s = s + _shift_up(s, k, row)
        k *= 2
    return s - dp


def _acc_over(k, nk, part, acc, o_ref):
    @pl.when(k == 0)
    def _():
        acc[...] = part

    @pl.when(k > 0)
    def _():
        acc[...] += part

    @pl.when(k == nk - 1)
    def _():
        o_ref[...] = acc[...].astype(o_ref.dtype)


def _fwd_in(x, g, w_loc, order, comm):
    t, d = x.shape
    ws = w_loc.shape[1]
    nsh = order.shape[0]
    assert nsh == 4, "the shard walk below is written for the 2 x 2 chips of the mesh"
    tm = _tile(t, 1024, SUBLANES_BF16)
    ni = t // tm
    nci, nco = len(comm.ins), len(comm.out_shapes)
    all_peers = comm.peers | frozenset(CHIPS + (SIBLING,))

    def body(order_ref, x_ref, g_ref, loc_ref, *rest):
        del order_ref
        cins = rest[:nci]
        z_ref, h_ref, full_ref = rest[nci:nci + 3]
        couts = rest[nci + 3:nci + 3 + nco]
        (hs, wbuf, wsem, own_s, own_r, snd_s, snd_r, fwd_s, fwd_r, rly_s, rly_r) = rest[nci + 3 + nco:nci + 14 + nco]
        csems = rest[nci + 14 + nco:]
        j = pl.program_id(0)
        i = pl.program_id(1)
        x_, y_, c_ = _coords()
        own = 2 * x_ + y_
        sib = (x_, y_, 1 - c_)
        peers = _peer_chips(x_, y_)

        def sends():
            return [_remote(_half(0, loc_ref, c_), _piece(0, full_ref, own, c_), snd_s.at[p], snd_r.at[p], (px, py, c_))
                    for p, (px, py) in enumerate(peers[:2])]

        def relays():
            out = []
            for q, (src_p, dst_p) in enumerate(((0, 1), (1, 0))):
                sx, sy = peers[src_p]
                part = _rows_part(0, _piece(0, full_ref, 2 * sx + sy, c_), (q, q + 1, 2))
                out.append(_remote(part, part, rly_s.at[q], rly_r.at[q], (*peers[dst_p], c_)))
            return out

        def owns():
            return [_remote(_half(0, loc_ref, h), _piece(0, full_ref, own, h), own_s.at[h], own_r.at[h], sib)
                    for h in range(2)]

        def forward(p, half):
            px, py = peers[p]
            landed = _piece(0, full_ref, 2 * px + py, half)
            return _remote(landed, landed, fwd_s.at[p], fwd_r.at[p], sib)

        def load(src, slot):
            return pltpu.make_async_copy(src, wbuf.at[slot], wsem.at[slot])

        @pl.when((j == 0) & (i == 0))
        def _():
            _handshake(all_peers)
            for cp in sends() + owns():
                cp.start()
            load(loc_ref, 0).start()
            comm.start(cins, couts, csems)

        @pl.when(j == 0)
        def _():
            xh, _ = _rms_fwd(x_ref[...])
            h = (xh * g_ref[...]).astype(BF16)
            hs[pl.ds(pl.multiple_of(i * tm, tm), tm), :] = h
            h_ref[...] = h

        slot = j % 2

        @pl.when(i == 0)
        def _():
            load(loc_ref, slot).wait()

        z_ref[...] = _dot(hs[pl.ds(pl.multiple_of(i * tm, tm), tm), :], wbuf[slot]).astype(BF16)

        def load_shard(p, into):
            px, py = peers[p]
            forward(p, 1 - c_).wait_recv()
            load(full_ref.at[2 * px + py], into).start()

        @pl.when((j == 0) & (i == ni - 1))
        def _():
            for cp in sends():
                cp.wait_recv()
            for cp in relays() + [forward(0, c_), forward(1, c_)]:
                cp.start()
            load_shard(0, 1)

        @pl.when((j == 1) & (i == 0))
        def _():
            load_shard(1, 0)

        @pl.when((j == 2) & (i == max(ni - 2, 0)))
        def _():
            for cp in relays():
                cp.wait_recv()
            forward(2, c_).start()
            load_shard(2, 1)

        @pl.when((j == nsh - 1) & (i == ni - 1))
        def _():
            for cp in sends() + relays() + [forward(p, c_) for p in range(nsh - 1)]:
                cp.wait_send()
            for cp in owns():
                cp.wait()
            comm.finish(cins, couts, csems)

    last = ni - 1
    blocks = _nbytes((tm, d), F32) + _nbytes((tm, ws), BF16) + _nbytes((tm, d), BF16)
    scratch = _nbytes((t, d), BF16) + 2 * _nbytes((d, ws), BF16)
    res = pl.pallas_call(
        body, name="fwd_in",
        grid_spec=pltpu.PrefetchScalarGridSpec(
            num_scalar_prefetch=1, grid=(nsh, ni),
            in_specs=[pl.BlockSpec((tm, d), lambda j, i, o: (jnp.where(j == 0, i, last), 0)),
                      pl.BlockSpec((1, d), lambda j, i, o: (0, 0)), ANY] + [ANY] * nci,
            out_specs=[pl.BlockSpec((tm, ws), lambda j, i, o: (i, o[j])),
                       pl.BlockSpec((tm, d), lambda j, i, o: (jnp.where(j == 0, i, last), 0)), ANY] + [ANY] * nco,
            scratch_shapes=[pltpu.VMEM((t, d), BF16), pltpu.VMEM((2, d, ws), BF16)]
            + _dma_sems(2, 2, 2, 2, 2, nsh - 1, nsh - 1, 2, 2) + comm.sems),
        out_shape=[SDS((t, nsh * ws), BF16), SDS((t, d), BF16), SDS((nsh, d, ws), BF16)] + comm.out_shapes,
        input_output_aliases={4 + i: 3 + o for i, o in comm.aliases.items()},
        compiler_params=_params(("arbitrary", "arbitrary"), blocks, scratch + 3 * _nbytes((tm, d), F32),
                                _collective_id(all_peers)),
    )(order, x, g, w_loc, *comm.ins)
    return list(res[:3]), list(res[3:])


def _mixer_mid_fwd(z, pool_w, pool_scale, conv_w, nseq, comm=None):
    t = z.shape[0]
    d = pool_scale.shape[1]
    s = t // nseq
    c = d // N_GROUPS

    def body(zp, zb, zc, zv, pw, ps, cw, o):
        j = pl.program_id(1)
        row = lax.broadcasted_iota(jnp.int32, (s, c), 0)
        for gi, win in enumerate(POOL_WINDOWS):
            @pl.when(j == gi)
            def _(win=win):
                pooled = _pool_fwd(zp[...].astype(F32), win, row)
                o[0] = (_dot(pooled.astype(BF16), pw[...]) * ps[...]).astype(BF16)

        cv = zc[...].astype(F32) * zv[...].astype(F32)
        cc = (cw[pl.ds(2, 1), :] * cv + cw[pl.ds(1, 1), :] * _shift_down(cv, 1, row)
              + cw[pl.ds(0, 1), :] * _shift_down(cv, 2, row))
        o[1] = (zb[...].astype(F32) * cc).astype(BF16)

    blocks = 4 * _nbytes((s, c), BF16) + _nbytes((c, c), BF16) + _nbytes((2, s, c), BF16)
    return _pcall(
        body, name="mixer_mid_fwd", grid=(nseq, N_GROUPS),
        in_specs=[pl.BlockSpec((s, c), lambda b, j: (b, j)),
                  pl.BlockSpec((s, c), lambda b, j: (b, N_GROUPS + j)),
                  pl.BlockSpec((s, c), lambda b, j: (b, 2 * N_GROUPS + j)),
                  pl.BlockSpec((s, c), lambda b, j: (b, 3 * N_GROUPS + j)),
                  pl.BlockSpec((None, c, c), lambda b, j: (j, 0, 0)),
                  pl.BlockSpec((1, c), lambda b, j: (0, j)),
                  pl.BlockSpec((3, c), lambda b, j: (0, j))],
        out_specs=[pl.BlockSpec((2, s, c), lambda b, j: (0, b, j))],
        out_shape=[SDS((3, t, d), BF16)],
        sem=("parallel", "parallel"), blocks=blocks, temps=8 * _nbytes((s, c), F32), comm=comm,
    )(z, z, z, z, pool_w, pool_scale, conv_w)


def _mixer_out(lhs3, z, x, w3, g_ffn, comm=None):
    t, d = x.shape
    tm = _tile(t, 256, SUBLANES_BF16)

    def body(pq, zgp, zgc, x_ref, w_ref, g_ref, mrg, ypc, x1o, h2o):
        yp = _dot(pq[0], w_ref[0])
        yc = _dot(pq[1], w_ref[1])
        m = _sigmoid(zgp[...].astype(F32)) * yp + _sigmoid(zgc[...].astype(F32)) * yc
        mb = m.astype(BF16)
        x1 = x_ref[...] + _dot(mb, w_ref[2])
        ypc[0] = yp.astype(BF16)
        ypc[1] = yc.astype(BF16)
        mrg[...] = mb
        x1o[...] = x1
        xh, _ = _rms_fwd(x1)
        h2o[...] = (xh * g_ref[...]).astype(BF16)

    blocks = (_nbytes((2, tm, d), BF16) * 2 + _nbytes((tm, d), BF16) * 4 + _nbytes((tm, d), F32) * 2
              + _nbytes((3, d, d), BF16))
    return _pcall(
        body, name="mixer_out", grid=(t // tm,),
        in_specs=[pl.BlockSpec((2, tm, d), lambda i: (0, i, 0)),
                  pl.BlockSpec((tm, d), lambda i: (i, 4)),
                  pl.BlockSpec((tm, d), lambda i: (i, 5)),
                  pl.BlockSpec((tm, d), lambda i: (i, 0)),
                  pl.BlockSpec((3, d, d), lambda i: (0, 0, 0)),
                  pl.BlockSpec((1, d), lambda i: (0, 0))],
        out_specs=[pl.BlockSpec((None, tm, d), lambda i: (2, i, 0)),
                   pl.BlockSpec((2, tm, d), lambda i: (0, i, 0)),
                   pl.BlockSpec((tm, d), lambda i: (i, 0)),
                   pl.BlockSpec((tm, d), lambda i: (i, 0))],
        out_shape=[SDS(lhs3.shape, BF16), SDS((2, t, d), BF16), SDS((t, d), F32), SDS((t, d), BF16)],
        input_output_aliases={0: 0},
        sem=("parallel",), blocks=blocks, temps=8 * _nbytes((tm, d), F32), comm=comm,
    )(lhs3, z, z, x, w3, g_ffn)


def _ffn_up(h2, w_up, f, comm=None):
    t, d = h2.shape
    _, _, ws = w_up.shape
    tm = _tile(t, 2048, SUBLANES_BF16)
    tn = _tile(ws, 1408, LANES)
    nps = ws // tn
    npp = f // tn

    def body(h_ref, w_ref, o_ref):
        o_ref[...] = _dot(h_ref[...], w_ref[...]).astype(BF16)

    blocks = _nbytes((tm, d), BF16) + _nbytes((d, tn), BF16) + _nbytes((tm, tn), BF16)
    return _pcall(
        body, name="ffn_up", grid=(t // tm, 2 * npp),
        in_specs=[pl.BlockSpec((tm, d), lambda i, j: (i, 0)),
                  pl.BlockSpec((None, d, tn), lambda i, j: (j // nps, 0, j % nps))],
        out_specs=[pl.BlockSpec((None, tm, tn), lambda i, j: (j // npp, i, j % npp))],
        out_shape=[SDS((2, t, f), BF16)],
        sem=("parallel", "parallel"), blocks=blocks, temps=_nbytes((tm, tn), F32), comm=comm,
    )(h2, w_up)


def _conv3_rows(u, u1, u2, w_ref, p):
    return w_ref[p, pl.ds(2, 1), :] * u + w_ref[p, pl.ds(1, 1), :] * u1 + w_ref[p, pl.ds(0, 1), :] * u2


WGRAD_TOKENS = 2048
WGRAD_TOKENS_WIDE = 4096
CHUNK = 64
HALO = SUBLANES_F32


def _up1_up2(u, nxt):
    rows = u.shape[0]
    ext = jnp.concatenate([u, nxt], axis=0)
    n = rows + HALO
    return pltpu.roll(ext, n - 1, 0)[:rows], pltpu.roll(ext, n - 2, 0)[:rows]


def _fold8(x):
    return jnp.sum(x.reshape(x.shape[0] // SUBLANES_F32, SUBLANES_F32, x.shape[1]), axis=0)


def _ffn_mid_fwd(u0, cw, cb, nseq):
    _, t, f = u0.shape
    s = t // nseq
    c = _tile(f, 256, LANES)

    def body(u_ref, w_ref, b_ref, a_ref, uo_ref):
        row = lax.broadcasted_iota(jnp.int32, (s, c), 0)
        act = []
        for p in range(2):
            u = u_ref[p].astype(F32)
            act.append(_conv3_rows(u, _shift_down(u, 1, row), _shift_down(u, 2, row), w_ref, p) + b_ref[p])
            uo_ref[p] = act[p].astype(BF16)
        ug, uv = act
        a_ref[...] = (ug * _sigmoid(ug) * uv).astype(BF16)

    blocks = 2 * _nbytes((2, s, c), BF16) + _nbytes((s, c), BF16)
    outs, _ = _pcall(
        body, name="ffn_mid_fwd", grid=(f // c, nseq),
        in_specs=[pl.BlockSpec((2, s, c), lambda j, b: (0, b, j)),
                  pl.BlockSpec((2, 3, c), lambda j, b: (0, 0, j)),
                  pl.BlockSpec((2, 1, c), lambda j, b: (0, 0, j))],
        out_specs=[pl.BlockSpec((s, c), lambda j, b: (b, j)),
                   pl.BlockSpec((2, s, c), lambda j, b: (0, b, j))],
        out_shape=[SDS((t, f), BF16), SDS((2, t, f), BF16)],
        sem=("parallel", "parallel"), blocks=blocks, temps=8 * _nbytes((s, c), F32),
    )(u0, cw, cb)
    return outs


def _ffn_down_loss(a, w_down, x1, tgt, g_fin):
    t, f = a.shape
    d = x1.shape[1]
    tm = _tile(t, 256, SUBLANES_BF16)
    nsteps = t // tm

    def body(a_ref, w_ref, x1_ref, t_ref, g_ref, dx_ref, dxb_ref, loss_ref, gg_ref, lacc):
        i = pl.program_id(0)

        @pl.when(i == 0)
        def _():
            lacc[...] = jnp.zeros_like(lacc)
            gg_ref[...] = jnp.zeros_like(gg_ref)

        x2 = x1_ref[...] + _dot(a_ref[...], w_ref[...])
        xh, inv = _rms_fwd(x2)
        g = g_ref[...]
        e = xh * g - t_ref[...]
        lacc[...] += jnp.sum(e * e, axis=0, keepdims=True)
        dy = e * (1.0 / d)
        gg_ref[...] += jnp.sum(dy * xh, axis=0, keepdims=True)
        dx2 = _rms_bwd(dy, xh, inv, g)
        dx_ref[...] = dx2
        dxb_ref[...] = dx2.astype(BF16)

        @pl.when(i == nsteps - 1)
        def _():
            loss_ref[...] = jnp.sum(lacc[...], axis=1, keepdims=True) * (0.5 / d)

    blocks = (_nbytes((tm, f), BF16) + _nbytes((f, d), BF16) + 3 * _nbytes((tm, d), F32) + _nbytes((tm, d), BF16))
    outs, _ = _pcall(
        body, name="ffn_down_loss", grid=(nsteps,),
        in_specs=[pl.BlockSpec((tm, f), lambda i: (i, 0)), pl.BlockSpec((f, d), lambda i: (0, 0)),
                  pl.BlockSpec((tm, d), lambda i: (i, 0)), pl.BlockSpec((tm, d), lambda i: (i, 0)),
                  pl.BlockSpec((1, d), lambda i: (0, 0))],
        out_specs=[pl.BlockSpec((tm, d), lambda i: (i, 0)), pl.BlockSpec((tm, d), lambda i: (i, 0)),
                   pl.BlockSpec((1, 1), lambda i: (0, 0)), pl.BlockSpec((1, d), lambda i: (0, 0))],
        out_shape=[SDS((t, d), F32), SDS((t, d), BF16), SDS((1, 1), F32), SDS((1, d), F32)],
        scratch_shapes=[pltpu.VMEM((1, d), F32)],
        sem=("arbitrary",), blocks=blocks, temps=8 * _nbytes((tm, d), F32),
    )(a, w_down, x1, tgt, g_fin)
    return outs


def _ffn_bwd_da(dxb, w_down, comm=None):
    t, d = dxb.shape
    f = w_down.shape[0]
    tm = _tile(t, 512, SUBLANES_BF16)

    def body(x_ref, w_ref, o_ref):
        o_ref[...] = _dot_tb(x_ref[...], w_ref[...]).astype(BF16)

    blocks = _nbytes((tm, d), BF16) + _nbytes((tm, f), BF16)
    return _pcall(
        body, name="ffn_bwd_da", grid=(t // tm,),
        in_specs=[pl.BlockSpec((tm, d), lambda i: (i, 0)),
                  pl.BlockSpec((f, d), lambda i: (0, 0), pipeline_mode=pl.Buffered(1))],
        out_specs=[pl.BlockSpec((tm, f), lambda i: (i, 0))],
        out_shape=[SDS((t, f), BF16)],
        sem=("parallel",), blocks=blocks, temps=_nbytes((f, d), BF16) + _nbytes((tm, f), F32), comm=comm,
    )(dxb, w_down)


def _ffn_mid_bwd(da, u0, ua, cw, nseq, comm=None):
    _, t, f = u0.shape
    s = t // nseq
    c = _tile(f, 128, LANES)
    r = _tile(s, CHUNK, SUBLANES_BF16)
    n = s // r

    def body(da_ref, u_ref, ua_ref, w_ref, du_ref, gw_ref, gb_ref):
        @pl.when(pl.program_id(1) == 0)
        def _():
            gw_ref[...] = jnp.zeros_like(gw_ref)
            gb_ref[...] = jnp.zeros_like(gb_ref)

        def step(i, carry):
            nxt, sums = carry
            rows = pl.ds(pl.multiple_of((n - 1 - i) * r, r), r)
            ug = ua_ref[0, rows, :].astype(F32)
            uv = ua_ref[1, rows, :].astype(F32)
            sg = _sigmoid(ug)
            dacc = da_ref[rows, :].astype(F32)
            dus = (dacc * uv * sg * (1.0 + ug * (1.0 - sg)), dacc * (ug * sg))
            first, new_sums = [], []
            for p in range(2):
                du = dus[p]
                d1, d2 = _up1_up2(du, nxt[p])
                du_ref[p, rows, :] = _conv3_rows(du, d1, d2, w_ref, p).astype(BF16)
                u = u_ref[p, rows, :].astype(F32)
                sb, s0, s1, s2 = sums[p]
                new_sums.append((sb + _fold8(du), s0 + _fold8(d2 * u), s1 + _fold8(d1 * u), s2 + _fold8(du * u)))
                first.append(du[:HALO])
            return tuple(first), tuple(new_sums)

        zero = jnp.zeros((HALO, c), F32)
        _, sums = lax.fori_loop(0, n, step, ((zero, zero), ((zero,) * 4,) * 2))
        for p in range(2):
            sb, s0, s1, s2 = sums[p]
            gb_ref[p] += jnp.sum(sb, axis=0, keepdims=True)
            gw_ref[p, pl.ds(0, 1), :] += jnp.sum(s0, axis=0, keepdims=True)
            gw_ref[p, pl.ds(1, 1), :] += jnp.sum(s1, axis=0, keepdims=True)
            gw_ref[p, pl.ds(2, 1), :] += jnp.sum(s2, axis=0, keepdims=True)

    blocks = _nbytes((s, c), BF16) + 3 * _nbytes((2, s, c), BF16)
    return _pcall(
        body, name="ffn_mid_bwd", grid=(f // c, nseq),
        in_specs=[pl.BlockSpec((s, c), lambda j, b: (b, j)),
                  pl.BlockSpec((2, s, c), lambda j, b: (0, b, j)),
                  pl.BlockSpec((2, s, c), lambda j, b: (0, b, j)),
                  pl.BlockSpec((2, 3, c), lambda j, b: (0, 0, j))],
        out_specs=[pl.BlockSpec((2, s, c), lambda j, b: (0, b, j)),
                   pl.BlockSpec((2, 3, c), lambda j, b: (0, 0, j)),
                   pl.BlockSpec((2, 1, c), lambda j, b: (0, 0, j))],
        out_shape=[SDS((2, t, f), BF16), SDS((2, 3, f), F32), SDS((2, 1, f), F32)],
        sem=("parallel", "arbitrary"), blocks=blocks, temps=4 * 1024 * 1024, comm=comm,
    )(da, u0, ua, cw)


def _wgrad(a, b, name, *, tr, tn, b_plane_of=None, out_shards=None, comm=None):
    t, m = a.shape
    n_total = b.shape[-1] * (b.shape[0] if b.ndim == 3 else 1)
    tk = _tile(t, WGRAD_TOKENS_WIDE if n_total > tn and m == tr else WGRAD_TOKENS, SUBLANES_BF16)
    nk = t // tk
    once = pl.Buffered(1) if nk == 1 else None

    def body(a_ref, b_ref, o_ref, *acc):
        part = _dot_ta(a_ref[...], b_ref[...])
        if nk == 1:
            o_ref[...] = part.astype(BF16)
        else:
            _acc_over(pl.program_id(2), nk, part, acc[0], o_ref)

    if b.ndim == 3:
        b_spec = pl.BlockSpec((None, tk, tn), lambda r, n, k: (b_plane_of(n)[0], k, b_plane_of(n)[1]))
    else:
        b_spec = pl.BlockSpec((tk, tn), lambda r, n, k: (k, n), pipeline_mode=once if n_total == tn else None)
    if out_shards is None:
        o_spec = pl.BlockSpec((tr, tn), lambda r, n, k: (r, n))
        o_shape = SDS((m, n_total), BF16)
    else:
        nps = n_total // out_shards // tn
        o_spec = pl.BlockSpec((None, tr, tn), lambda r, n, k: (n // nps, r, n % nps))
        o_shape = SDS((out_shards, m, n_total // out_shards), BF16)
    blocks = _nbytes((tk, tr), BF16) + _nbytes((tk, tn), BF16) + _nbytes((tr, tn), BF16)
    return _pcall(
        body, name=name, grid=(m // tr, n_total // tn, nk),
        in_specs=[pl.BlockSpec((tk, tr), lambda r, n, k: (k, r), pipeline_mode=once if m == tr else None), b_spec],
        out_specs=[o_spec], out_shape=[o_shape],
        scratch_shapes=[] if nk == 1 else [pltpu.VMEM((tr, tn), F32)],
        sem=("parallel", "parallel", "arbitrary"), blocks=blocks, temps=2 * _nbytes((tr, tn), F32), comm=comm,
    )(a, b)


def _wgrad3(lhs3, rhs3, comm=None):
    nw, t, d = lhs3.shape
    tk = _tile(t, WGRAD_TOKENS, SUBLANES_BF16)
    nk = t // tk

    def body(a_ref, b_ref, o_ref, *acc):
        part = _dot_ta(a_ref[...], b_ref[...])
        if nk == 1:
            o_ref[...] = part.astype(BF16)
        else:
            _acc_over(pl.program_id(1), nk, part, acc[0], o_ref)

    blocks = 2 * _nbytes((tk, d), BF16) + _nbytes((d, d), BF16)
    return _pcall(
        body, name="wgrad_sq3", grid=(nw, nk),
        in_specs=[pl.BlockSpec((None, tk, d), lambda w, k: (w, k, 0)),
                  pl.BlockSpec((None, tk, d), lambda w, k: (w, k, 0))],
        out_specs=[pl.BlockSpec((None, d, d), lambda w, k: (w, 0, 0))],
        out_shape=[SDS((nw, d, d), BF16)],
        scratch_shapes=[] if nk == 1 else [pltpu.VMEM((d, d), F32)],
        sem=("parallel", "arbitrary"), blocks=blocks, temps=2 * _nbytes((d, d), F32), comm=comm,
    )(lhs3, rhs3)


def _ffn_bwd_dx1(du0, w_up, x1, dx2, g_ffn, n_planes_out, comm=None):
    _, t, f = du0.shape
    d = x1.shape[1]
    nsh, _, ws = w_up.shape
    tm = _tile(t, 256, SUBLANES_BF16)
    spp = f // ws

    def body(du_ref, w_ref, x1_ref, dx2_ref, g_ref, dx1_ref, dxb_ref, gg_ref):
        @pl.when(pl.program_id(0) == 0)
        def _():
            gg_ref[...] = jnp.zeros_like(gg_ref)

        dh = None
        for k in range(nsh):
            part = _dot_tb(du_ref[k // spp, :, (k % spp) * ws:(k % spp + 1) * ws], w_ref[k])
            dh = part if dh is None else dh + part
        xh, inv = _rms_fwd(x1_ref[...])
        gg_ref[...] += jnp.sum(dh * xh, axis=0, keepdims=True)
        dx1 = dx2_ref[...] + _rms_bwd(dh, xh, inv, g_ref[...])
        dx1_ref[...] = dx1
        dxb_ref[...] = dx1.astype(BF16)

    blocks = _nbytes((2, tm, f), BF16) + 3 * _nbytes((tm, d), F32) + _nbytes((tm, d), BF16)
    return _pcall(
        body, name="ffn_bwd_dx1", grid=(t // tm,),
        in_specs=[pl.BlockSpec((2, tm, f), lambda i: (0, i, 0)),
                  pl.BlockSpec((nsh, d, ws), lambda i: (0, 0, 0), pipeline_mode=pl.Buffered(1)),
                  pl.BlockSpec((tm, d), lambda i: (i, 0)),
                  pl.BlockSpec((tm, d), lambda i: (i, 0)),
                  pl.BlockSpec((1, d), lambda i: (0, 0))],
        out_specs=[pl.BlockSpec((tm, d), lambda i: (i, 0)),
                   pl.BlockSpec((None, tm, d), lambda i: (n_planes_out - 1, i, 0)),
                   pl.BlockSpec((1, d), lambda i: (0, 0))],
        out_shape=[SDS((t, d), F32), SDS((n_planes_out, t, d), BF16), SDS((1, d), F32)],
        sem=("arbitrary",), blocks=blocks, temps=_nbytes(w_up.shape, BF16) + 8 * _nbytes((tm, d), F32), comm=comm,
    )(du0, w_up, x1, dx2, g_ffn)


def _mixer_bwd(rhs3, z, ypc, w3, comm=None):
    _, t, d = rhs3.shape
    tm = _tile(t, 512, SUBLANES_BF16)

    def body(dx_ref, zgp, zgc, ypc_ref, w_ref, dyo, dzo, dpq):
        dm = _dot_tb(dx_ref[...], w_ref[2])
        sp = _sigmoid(zgp[...].astype(F32))
        sc = _sigmoid(zgc[...].astype(F32))
        dyp = (dm * sp).astype(BF16)
        dyc = (dm * sc).astype(BF16)
        dzo[0] = (dm * ypc_ref[0].astype(F32) * sp * (1.0 - sp)).astype(BF16)
        dzo[1] = (dm * ypc_ref[1].astype(F32) * sc * (1.0 - sc)).astype(BF16)
        dyo[0] = dyp
        dyo[1] = dyc
        dpq[0] = _dot_tb(dyp, w_ref[0]).astype(BF16)
        dpq[1] = _dot_tb(dyc, w_ref[1]).astype(BF16)

    blocks = _nbytes((tm, d), BF16) * 3 + _nbytes((2, tm, d), BF16) * 4 + _nbytes((3, d, d), BF16)
    return _pcall(
        body, name="mixer_bwd", grid=(t // tm,),
        in_specs=[pl.BlockSpec((None, tm, d), lambda i: (2, i, 0)),
                  pl.BlockSpec((tm, d), lambda i: (i, 4)),
                  pl.BlockSpec((tm, d), lambda i: (i, 5)),
                  pl.BlockSpec((2, tm, d), lambda i: (0, i, 0)),
                  pl.BlockSpec((3, d, d), lambda i: (0, 0, 0))],
        out_specs=[pl.BlockSpec((2, tm, d), lambda i: (0, i, 0)),
                   pl.BlockSpec((2, tm, d), lambda i: (2, i, 0)),
                   pl.BlockSpec((2, tm, d), lambda i: (0, i, 0))],
        out_shape=[SDS(rhs3.shape, BF16), SDS((N_SPLITS, t, d), BF16), SDS((2, t, d), BF16)],
        input_output_aliases={0: 0},
        sem=("parallel",), blocks=blocks, temps=8 * _nbytes((tm, d), F32), comm=comm,
    )(rhs3, z, z, ypc, w3)


def _conv_bwd(dz, dpq, z, conv_w, nseq, comm=None):
    _, t, d = dz.shape
    s = t // nseq
    c = _tile(d, 256, LANES)
    nb = d // c

    def body(dz_in, dq_ref, zb, zc, zv, cw, dzo, gw_ref):
        del dz_in

        @pl.when(pl.program_id(1) == 0)
        def _():
            gw_ref[...] = jnp.zeros_like(gw_ref)

        row = lax.broadcasted_iota(jnp.int32, (s, c), 0)
        b = zb[...].astype(F32)
        cm = zc[...].astype(F32)
        v = zv[...].astype(F32)
        cv = cm * v
        cv1 = _shift_down(cv, 1, row)
        cv2 = _shift_down(cv, 2, row)
        w0, w1, w2 = cw[pl.ds(0, 1), :], cw[pl.ds(1, 1), :], cw[pl.ds(2, 1), :]
        cc = w2 * cv + w1 * cv1 + w0 * cv2
        dq = dq_ref[...].astype(F32)
        dzo[0] = (dq * cc).astype(BF16)
        dcc = dq * b
        gw_ref[pl.ds(0, 1), :] += jnp.sum(dcc * cv2, axis=0, keepdims=True)
        gw_ref[pl.ds(1, 1), :] += jnp.sum(dcc * cv1, axis=0, keepdims=True)
        gw_ref[pl.ds(2, 1), :] += jnp.sum(dcc * cv, axis=0, keepdims=True)
        dcv = w2 * dcc + w1 * _shift_up(dcc, 1, row) + w0 * _shift_up(dcc, 2, row)
        dzo[1] = (dcv * v).astype(BF16)
        dzo[2] = (dcv * cm).astype(BF16)

    blocks = 4 * _nbytes((s, c), BF16) + _nbytes((3, s, c), BF16)
    return _pcall(
        body, name="conv_bwd", grid=(nb, nseq),
        in_specs=[ANY,
                  pl.BlockSpec((None, s, c), lambda j, b: (1, b, j)),
                  pl.BlockSpec((s, c), lambda j, b: (b, nb + j)),
                  pl.BlockSpec((s, c), lambda j, b: (b, 2 * nb + j)),
                  pl.BlockSpec((s, c), lambda j, b: (b, 3 * nb + j)),
                  pl.BlockSpec((3, c), lambda j, b: (0, j))],
        out_specs=[pl.BlockSpec((3, s, c), lambda j, b: (0, b, j)),
                   pl.BlockSpec((3, c), lambda j, b: (0, j))],
        out_shape=[SDS(dz.shape, BF16), SDS((3, d), F32)],
        input_output_aliases={0: 0},
        sem=("parallel", "arbitrary"), blocks=blocks, temps=16 * _nbytes((s, c), F32), comm=comm,
    )(dz, dpq, z, z, z, conv_w)


def _pool_bwd_call(dz, dpq, z, pool_w, pool_scale, nseq, comm=None):
    _, t, d = dz.shape
    s = t // nseq
    c = d // N_GROUPS

    def body(dz_in, dp_ref, zp, pw, ps, dzo, gpw_ref, gps_ref):
        del dz_in
        j = pl.program_id(0)

        @pl.when(pl.program_id(1) == 0)
        def _():
            gpw_ref[...] = jnp.zeros_like(gpw_ref)
            gps_ref[...] = jnp.zeros_like(gps_ref)

        row = lax.broadcasted_iota(jnp.int32, (s, c), 0)
        for gi, win in enumerate(POOL_WINDOWS):
            @pl.when(j == gi)
            def _(win=win):
                pb = _pool_fwd(zp[...].astype(F32), win, row).astype(BF16)
                plin = _dot(pb, pw[...])
                dps = dp_ref[...].astype(F32)
                gps_ref[...] += jnp.sum(dps * plin, axis=0, keepdims=True)
                dplb = (dps * ps[...]).astype(BF16)
                gpw_ref[...] += _dot_ta(pb, dplb)
                dzo[...] = _pool_bwd(_dot_tb(dplb, pw[...]), win, row).astype(BF16)

    blocks = 3 * _nbytes((s, c), BF16) + _nbytes((c, c), BF16) + _nbytes((c, c), F32)
    return _pcall(
        body, name="pool_bwd", grid=(N_GROUPS, nseq),
        in_specs=[ANY,
                  pl.BlockSpec((None, s, c), lambda j, b: (0, b, j)),
                  pl.BlockSpec((s, c), lambda j, b: (b, j)),
                  pl.BlockSpec((None, c, c), lambda j, b: (j, 0, 0)),
                  pl.BlockSpec((1, c), lambda j, b: (0, j))],
        out_specs=[pl.BlockSpec((None, s, c), lambda j, b: (3, b, j)),
                   pl.BlockSpec((None, c, c), lambda j, b: (j, 0, 0)),
                   pl.BlockSpec((1, c), lambda j, b: (0, j))],
        out_shape=[SDS(dz.shape, BF16), SDS((N_GROUPS, c, c), F32), SDS((1, d), F32)],
        input_output_aliases={0: 0},
        sem=("parallel", "arbitrary"), blocks=blocks, temps=10 * _nbytes((s, c), F32), comm=comm,
    )(dz, dpq, z, pool_w, pool_scale)


def _dz_plane(zb):
    return jnp.where(zb < 4, (zb + 3) % 4, zb)


def _wgrad_in(h1, dz, nsh, comm=None):
    t, d = h1.shape
    ws = N_SPLITS * d // nsh
    kb = _tile(math.gcd(d, ws), 512, LANES)
    npl = d // kb
    nps = ws // kb
    tk = _tile(t, WGRAD_TOKENS_WIDE, SUBLANES_BF16)
    nk = t // tk

    def body(a_ref, b_ref, o_ref, *acc):
        part = _dot_ta(a_ref[...], b_ref[...])
        if nk == 1:
            o_ref[...] = part.astype(BF16)
        else:
            _acc_over(pl.program_id(1), nk, part, acc[0], o_ref)

    blocks = _nbytes((tk, d), BF16) + _nbytes((tk, kb), BF16) + _nbytes((d, kb), BF16)
    return _pcall(
        body, name="wgrad_in", grid=(N_SPLITS * npl, nk),
        in_specs=[pl.BlockSpec((tk, d), lambda cb, k: (k, 0), pipeline_mode=pl.Buffered(1) if nk == 1 else None),
                  pl.BlockSpec((None, tk, kb), lambda cb, k: (_dz_plane(cb // npl), k, cb % npl))],
        out_specs=[pl.BlockSpec((None, d, kb), lambda cb, k: (cb // nps, 0, cb % nps))],
        out_shape=[SDS((nsh, d, ws), BF16)],
        scratch_shapes=[] if nk == 1 else [pltpu.VMEM((d, kb), F32)],
        sem=("parallel", "arbitrary"), blocks=blocks, temps=2 * _nbytes((d, kb), F32), comm=comm,
    )(h1, dz)


def _mixer_bwd_dx(dz, w_in, x, dx1, g_mix, comm=None):
    npln, t, d = dz.shape
    nsh, _, ws = w_in.shape
    tm = _tile(t, 256, SUBLANES_BF16)
    kb = _tile(math.gcd(d, ws), 512, LANES)
    npl = d // kb
    nps = ws // kb

    def body(dz_ref, w_ref, x_ref, dx1_ref, g_ref, dx_ref, gg_ref):
        @pl.when(pl.program_id(0) == 0)
        def _():
            gg_ref[...] = jnp.zeros_like(gg_ref)

        dh = None
        for cb in range(npln * npl):
            zb = cb // npl
            plane = (zb + 3) % 4 if zb < 4 else zb
            part = _dot_tb(dz_ref[plane, :, (cb % npl) * kb:(cb % npl + 1) * kb],
                           w_ref[cb // nps, :, (cb % nps) * kb:(cb % nps + 1) * kb])
            dh = part if dh is None else dh + part
        xh, inv = _rms_fwd(x_ref[...])
        gg_ref[...] += jnp.sum(dh * xh, axis=0, keepdims=True)
        dx_ref[...] = dx1_ref[...] + _rms_bwd(dh, xh, inv, g_ref[...])

    blocks = _nbytes((npln, tm, d), BF16) + 3 * _nbytes((tm, d), F32)
    return _pcall(
        body, name="mixer_bwd_dx", grid=(t // tm,),
        in_specs=[pl.BlockSpec((npln, tm, d), lambda i: (0, i, 0)),
                  pl.BlockSpec((nsh, d, ws), lambda i: (0, 0, 0), pipeline_mode=pl.Buffered(1)),
                  pl.BlockSpec((tm, d), lambda i: (i, 0)),
                  pl.BlockSpec((tm, d), lambda i: (i, 0)),
                  pl.BlockSpec((1, d), lambda i: (0, 0))],
        out_specs=[pl.BlockSpec((tm, d), lambda i: (i, 0)),
                   pl.BlockSpec((1, d), lambda i: (0, 0))],
        out_shape=[SDS((t, d), F32), SDS((1, d), F32)],
        sem=("arbitrary",), blocks=blocks, temps=_nbytes(w_in.shape, BF16) + 8 * _nbytes((tm, d), F32), comm=comm,
    )(dz, w_in, x, dx1, g_mix)


N_BIG = 5
SHARD_MAJOR = (0, 2)
ROWS_DIM1 = (1, 4)


def _ds(start, size, align):
    if isinstance(start, int):
        return pl.ds(start, size)
    return pl.ds(pl.multiple_of(start, align), size)


def _piece(a, ref, k, h):
    if a in SHARD_MAJOR:
        r = ref.shape[1] // 2
        return ref.at[k, _ds(h * r, r, SUBLANES_BF16), :]
    if a in ROWS_DIM1:
        r = ref.shape[1] // 8
        return ref.at[:, _ds((2 * k + h) * r, r, SUBLANES_BF16), :]
    r = ref.shape[0] // 8
    return ref.at[_ds((2 * k + h) * r, r, SUBLANES_BF16), :]


def _half(a, ref, h):
    if a in ROWS_DIM1:
        r = ref.shape[1] // 2
        return ref.at[:, _ds(h * r, r, SUBLANES_BF16), :]
    r = ref.shape[0] // 2
    return ref.at[_ds(h * r, r, SUBLANES_BF16), :]


def _piece_shape(a, full_shape):
    if a in SHARD_MAJOR:
        return (full_shape[1] // 2, full_shape[2])
    if a in ROWS_DIM1:
        return (full_shape[0], full_shape[1] // 8, full_shape[2])
    return (full_shape[0] // 8, full_shape[1])


def _shard_shape(a, full_shape):
    if a in SHARD_MAJOR:
        return (full_shape[1], full_shape[2])
    if a in ROWS_DIM1:
        return (full_shape[0], full_shape[1] // 4, full_shape[2])
    return (full_shape[0] // 4, full_shape[1])


def _rows_axis(a):
    return 1 if a in ROWS_DIM1 else 0


def _piece_block(a, full_shape):
    ps = _piece_shape(a, full_shape)
    if a in SHARD_MAJOR:
        return (None,) + ps, lambda k, c: (k, c, 0)
    if a in ROWS_DIM1:
        return ps, lambda k, c: (0, 2 * k + c, 0)
    return ps, lambda k, c: (2 * k + c, 0)


def _coords():
    return lax.axis_index("x"), lax.axis_index("y"), lax.axis_index("c")


def _peer_chips(x, y):
    return [(1 - x, y), (x, 1 - y), (1 - x, 1 - y)]


def _remote(src, dst, ssem, rsem, dev):
    return pltpu.make_async_remote_copy(src_ref=src, dst_ref=dst, send_sem=ssem, recv_sem=rsem,
                                        device_id=dev, device_id_type=MESH)


def _dma_sems(*counts):
    return [pltpu.SemaphoreType.DMA((n,)) for n in counts]


def _symmetric(ins, out_shapes, sems, copies, peers, aliases=None):
    def start(cins, couts, csems):
        for cp in copies(cins, couts, csems):
            cp.start()

    def finish(cins, couts, csems):
        for cp in copies(cins, couts, csems):
            cp.wait()

    return _Comm(ins, out_shapes, sems, start, finish, peers, aliases)


def _rows_part(a, ref, part):
    if part is None:
        return ref
    p, q, n = part
    ax = _rows_axis(a)
    r = ref.shape[ax] // n
    return ref.at[tuple(pl.ds(p * r, (q - p) * r) if d == ax else slice(None) for d in range(len(ref.shape)))]


def _merge(comms):
    ins, outs, sems, aliases, spans = [], [], [], {}, []
    for cm in comms:
        spans.append((len(ins), len(outs), len(sems)))
        for i, o in cm.aliases.items():
            aliases[len(ins) + i] = len(outs) + o
        ins += cm.ins
        outs += cm.out_shapes
        sems += cm.sems

    def each(fn_name):
        def run(cins, couts, csems):
            for cm, (i0, o0, s0) in zip(comms, spans):
                fn = getattr(cm, fn_name)
                if fn is not None:
                    fn(cins[i0:i0 + len(cm.ins)], couts[o0:o0 + len(cm.out_shapes)], csems[s0:s0 + len(cm.sems)])
        return run

    return _Comm(ins, outs, sems, each("start"), each("finish"), frozenset().union(*[cm.peers for cm in comms]),
                 aliases, mid=each("mid") if any(cm.mid is not None for cm in comms) else None)


def _gather_comm(arrs, locs, full_shapes, part=None, into=None):
    n = len(arrs)

    def own(cins, couts, csems):
        x, y, c = _coords()
        j = 2 * x + y
        return [_remote(_rows_part(a, _half(a, cins[q], h), part), _rows_part(a, _piece(a, couts[q], j, h), part),
                        csems[0].at[2 * q + h], csems[1].at[2 * q + h], (x, y, 1 - c))
                for q, a in enumerate(arrs) for h in range(2)]

    def sends(cins, couts, csems):
        x, y, c = _coords()
        j = 2 * x + y
        return [_remote(_rows_part(a, _half(a, cins[q], c), part), _rows_part(a, _piece(a, couts[q], j, c), part),
                        csems[2].at[3 * q + i], csems[3].at[3 * q + i], (px, py, c))
                for q, a in enumerate(arrs) for i, (px, py) in enumerate(_peer_chips(x, y))]

    def forwards(couts, csems, half_of):
        x, y, c = _coords()
        out = []
        for q, a in enumerate(arrs):
            for i, (px, py) in enumerate(_peer_chips(x, y)):
                landed = _rows_part(a, _piece(a, couts[q], 2 * px + py, half_of(c)), part)
                out.append(_remote(landed, landed, csems[4].at[3 * q + i], csems[5].at[3 * q + i], (x, y, 1 - c)))
        return out

    def start(cins, couts, csems):
        for cp in sends(cins, couts, csems) + own(cins, couts, csems):
            cp.start()

    def finish(cins, couts, csems):
        fw = forwards(couts, csems, lambda c: c)
        for cp, f in zip(sends(cins, couts, csems), fw):
            cp.wait_recv()
            f.start()
        for f in forwards(couts, csems, lambda c: 1 - c):
            f.wait_recv()
        for cp in sends(cins, couts, csems) + fw:
            cp.wait_send()
        for cp in own(cins, couts, csems):
            cp.wait()

    ins = [locs[a] for a in arrs] + ([into[a] for a in arrs] if into else [])
    return _Comm(ins, [SDS(full_shapes[a], BF16) for a in arrs],
                 _dma_sems(2 * n, 2 * n, 3 * n, 3 * n, 3 * n, 3 * n), start, finish, CHIPS + (SIBLING,),
                 aliases={n + q: q for q in range(n)} if into else None)


def _ring_gather_comm(arrs, locs, full_shapes):
    n = len(arrs)

    def own(cins, couts, csems):
        x, y, c = _coords()
        j = 2 * x + y
        return [_remote(_half(a, cins[q], h), _piece(a, couts[q], j, h), csems[0].at[2 * q + h],
                        csems[1].at[2 * q + h], (x, y, 1 - c)) for q, a in enumerate(arrs) for h in range(2)]

    def sends(cins, couts, csems):
        x, y, c = _coords()
        j = 2 * x + y
        return [_remote(_half(a, cins[q], c), _piece(a, couts[q], j, c), csems[2].at[2 * q + i],
                        csems[3].at[2 * q + i], (px, py, c))
                for q, a in enumerate(arrs) for i, (px, py) in enumerate(_peer_chips(x, y)[:2])]

    def relays(couts, csems):
        x, y, c = _coords()
        peers = _peer_chips(x, y)
        out = []
        for q, a in enumerate(arrs):
            for r, (src_p, dst_p) in enumerate(((0, 1), (1, 0))):
                sx, sy = peers[src_p]
                rows = _rows_part(a, _piece(a, couts[q], 2 * sx + sy, c), (r, r + 1, 2))
                out.append(_remote(rows, rows, csems[6].at[2 * q + r], csems[7].at[2 * q + r], (*peers[dst_p], c)))
        return out

    def forwards(couts, csems, half_of, which):
        x, y, c = _coords()
        out = []
        for q, a in enumerate(arrs):
            for i in which:
                px, py = _peer_chips(x, y)[i]
                landed = _piece(a, couts[q], 2 * px + py, half_of(c))
                out.append(_remote(landed, landed, csems[4].at[3 * q + i], csems[5].at[3 * q + i], (x, y, 1 - c)))
        return out

    def start(cins, couts, csems):
        for cp in sends(cins, couts, csems) + own(cins, couts, csems):
            cp.start()

    def mid(cins, couts, csems):
        for cp in sends(cins, couts, csems):
            cp.wait_recv()
        for cp in relays(couts, csems) + forwards(couts, csems, lambda c: c, (0, 1)):
            cp.start()

    def finish(cins, couts, csems):
        for cp in relays(couts, csems):
            cp.wait_recv()
        fw_diag = forwards(couts, csems, lambda c: c, (2,))
        for f in fw_diag:
            f.start()
        for f in forwards(couts, csems, lambda c: 1 - c, (0, 1, 2)):
            f.wait_recv()
        for cp in (sends(cins, couts, csems) + relays(couts, csems)
                   + forwards(couts, csems, lambda c: c, (0, 1)) + fw_diag):
            cp.wait_send()
        for cp in own(cins, couts, csems):
            cp.wait()

    return _Comm([locs[a] for a in arrs], [SDS(full_shapes[a], BF16) for a in arrs],
                 _dma_sems(2 * n, 2 * n, 2 * n, 2 * n, 3 * n, 3 * n, 2 * n, 2 * n), start, finish,
                 CHIPS + (SIBLING,), mid=mid)


def _halves_comm(arrs, gbs):
    n = len(arrs)

    def copies(cins, couts, csems):
        x, y, c = _coords()
        return [_remote(_piece(a, cins[q], k, 1 - c), couts[q].at[k], csems[0].at[4 * q + k], csems[1].at[4 * q + k],
                        (x, y, 1 - c)) for q, a in enumerate(arrs) for k in range(4)]

    return _symmetric([gbs[a] for a in arrs], [SDS((4,) + _piece_shape(a, gbs[a].shape), BF16) for a in arrs],
                      _dma_sems(4 * n, 4 * n), copies, [SIBLING])


def _chips_comm(arrs, ps, part=None, into=None):
    n = len(arrs)

    def copies(cins, couts, csems):
        x, y, c = _coords()
        return [_remote(_rows_part(a, cins[q].at[2 * px + py], part), _rows_part(a, couts[q].at[i], part),
                        csems[0].at[3 * q + i], csems[1].at[3 * q + i], (px, py, c))
                for q, a in enumerate(arrs) for i, (px, py) in enumerate(_peer_chips(x, y))]

    ins = [ps[a] for a in arrs] + ([into[a] for a in arrs] if into else [])
    return _symmetric(ins, [SDS((3,) + ps[a].shape[1:], BF16) for a in arrs], _dma_sems(3 * n, 3 * n), copies, CHIPS,
                      aliases={n + q: q for q in range(n)} if into else None)


def _result_comm(arrs, gs):
    n = len(arrs)

    def copies(cins, couts, csems):
        x, y, c = _coords()
        return [_remote(_half(a, cins[q], c), _half(a, couts[q], c), csems[0].at[q], csems[1].at[q], (x, y, 1 - c))
                for q, a in enumerate(arrs)]

    return _symmetric([gs[a] for a in arrs], [SDS(gs[a].shape, F32) for a in arrs], _dma_sems(n, n), copies,
                      [SIBLING], aliases={q: q for q in range(n)})


def _add_halves(arrs, gbs, lands, c_arr, name):
    n = len(arrs)

    def body(c_ref, *refs):
        del c_ref
        for q in range(n):
            refs[2 * n + q][...] = (refs[q][...].astype(F32) + refs[n + q][...].astype(F32)).astype(BF16)

    g_specs, l_specs, o_specs, blocks = [], [], [], 0
    for a in arrs:
        bs, imap = _piece_block(a, gbs[a].shape)
        ps = _piece_shape(a, gbs[a].shape)
        g_specs.append(pl.BlockSpec(bs, lambda k, c_ref, imap=imap: imap(k, c_ref[0])))
        nd = len(ps)
        l_specs.append(pl.BlockSpec((None,) + ps, lambda k, c_ref, nd=nd: (k,) + (0,) * nd))
        o_specs.append(pl.BlockSpec((None,) + ps, lambda k, c_ref, nd=nd: (k,) + (0,) * nd))
        blocks += 3 * _nbytes(ps, BF16)
    return list(pl.pallas_call(
        body, name=name,
        grid_spec=pltpu.PrefetchScalarGridSpec(
            num_scalar_prefetch=1, grid=(4,), in_specs=g_specs + l_specs, out_specs=o_specs),
        out_shape=[SDS((4,) + _piece_shape(a, gbs[a].shape), BF16) for a in arrs],
        compiler_params=_params(("parallel",), blocks, blocks),
    )(c_arr, *[gbs[a] for a in arrs], *lands))


def _sum_chips(a, p, land, shard_shape, jc_arr, name):
    ps = land.shape[1:]
    ax = _rows_axis(a)
    rows = ps[ax]
    nsub = 2 if rows % (2 * SUBLANES_BF16) == 0 else 1
    bs = tuple(r // nsub if q == ax else r for q, r in enumerate(ps))
    nd = len(ps)

    def at_rows(v):
        return tuple(v if q == ax else 0 for q in range(nd))

    def body(jc_ref, p_ref, l_ref, o_ref):
        del jc_ref
        acc = p_ref[...].astype(F32) + l_ref[0].astype(F32)
        acc = acc + l_ref[1].astype(F32)
        o_ref[...] = acc + l_ref[2].astype(F32)

    blocks = 4 * _nbytes(bs, BF16) + _nbytes(bs, F32)
    return pl.pallas_call(
        body, name=name,
        grid_spec=pltpu.PrefetchScalarGridSpec(
            num_scalar_prefetch=1, grid=(nsub,),
            in_specs=[pl.BlockSpec((None,) + bs, lambda s, jc: (jc[0],) + at_rows(s)),
                      pl.BlockSpec((3,) + bs, lambda s, jc: (0,) + at_rows(s))],
            out_specs=pl.BlockSpec(bs, lambda s, jc: at_rows(jc[1] * nsub + s))),
        out_shape=SDS(shard_shape, F32),
        compiler_params=_params(("parallel",), blocks, 2 * _nbytes(bs, F32)),
    )(jc_arr, p, land)


def _small_comm(v):
    rows = v.shape[0]

    def copies(cins, couts, csems):
        x, y, c = _coords()
        me = 4 * x + 2 * y + c
        out = [pltpu.make_async_copy(cins[0], couts[0].at[me], csems[0].at[0])]
        for dlt in range(1, 8):
            px = 1 - x if (dlt >> 2) & 1 else x
            py = 1 - y if (dlt >> 1) & 1 else y
            pc = 1 - c if dlt & 1 else c
            out.append(_remote(cins[0], couts[0].at[me], csems[1].at[dlt - 1], csems[2].at[dlt - 1], (px, py, pc)))
        return out

    return _symmetric([v], [SDS((8, rows, LANES), F32)], _dma_sems(1, 7, 7), copies, EVERYONE)


def _sum8(slots, name):
    def body(s_ref, o_ref):
        acc = s_ref[0]
        for i in range(1, 8):
            acc = acc + s_ref[i]
        o_ref[...] = acc

    return pl.pallas_call(
        body, name=name,
        in_specs=[pl.BlockSpec(memory_space=pltpu.VMEM)], out_specs=pl.BlockSpec(memory_space=pltpu.VMEM),
        out_shape=SDS(slots.shape[1:], F32),
    )(slots)


def _adamw(w, g, m, v, name, g_plane=None):
    rows, cols = w.shape
    tr = _tile(rows, max(SUBLANES_F32, (256 * 1024 // cols) // SUBLANES_F32 * SUBLANES_F32), SUBLANES_F32)

    def body(w_ref, g_ref, m_ref, v_ref, go_ref, d_ref, mo_ref, vo_ref):
        gr = g_ref[...]
        mn = ADAM_B1 * m_ref[...] + (1.0 - ADAM_B1) * gr
        vn = ADAM_B2 * v_ref[...] + (1.0 - ADAM_B2) * (gr * gr)
        m_hat = mn / (1.0 - ADAM_B1 ** ADAM_STEP)
        v_hat = vn / (1.0 - ADAM_B2 ** ADAM_STEP)
        d_ref[...] = -ADAM_LR * (m_hat / (jnp.sqrt(v_hat) + ADAM_EPS) + ADAM_WD * w_ref[...])
        go_ref[...] = gr
        mo_ref[...] = mn
        vo_ref[...] = vn

    spec = pl.BlockSpec((tr, cols), lambda i: (i, 0))
    g_spec = spec if g_plane is None else pl.BlockSpec((None, tr, cols), lambda i: (g_plane, i, 0))
    return pl.pallas_call(
        body, name=name, grid=(rows // tr,),
        in_specs=[spec, g_spec, spec, spec], out_specs=[spec, spec, spec, spec],
        out_shape=[SDS((rows, cols), F32)] * 4,
        compiler_params=_params(("parallel",), 8 * _nbytes((tr, cols), F32), 4 * _nbytes((tr, cols), F32)),
    )(w, g, m, v)


def _pack(parts):
    rows = []
    for p in parts:
        r = p.reshape(-1, LANES)
        pad = (-r.shape[0]) % SUBLANES_F32
        if pad:
            r = jnp.pad(r, ((0, pad), (0, 0)))
        rows.append(r)
    return jnp.concatenate(rows, axis=0)


def _unpack(packed, shapes):
    out, at = [], 0
    for s in shapes:
        n = 1
        for q in s:
            n *= q
        r = n // LANES
        out.append(packed[at:at + r].reshape(s))
        at += r + (-r) % SUBLANES_F32
    return out


def kernel(x, norm_mix, w_in, pool_w, pool_scale, w_pool_proj, conv_w, w_conv_out, w_o, norm_ffn, w_up, ffn_conv_w, ffn_conv_b, w_down, norm_final, loss_target, m_norm_mix, m_w_in, m_pool_w, m_pool_scale, m_w_pool_proj, m_conv_w, m_w_conv_out, m_w_o, m_norm_ffn, m_w_up, m_ffn_conv_w, m_ffn_conv_b, m_w_down, m_norm_final, v_norm_mix, v_w_in, v_pool_w, v_pool_scale, v_w_pool_proj, v_conv_w, v_w_conv_out, v_w_o, v_norm_ffn, v_w_up, v_ffn_conv_w, v_ffn_conv_b, v_w_down, v_norm_final):
    nseq, seq, d = x.shape
    t = nseq * seq
    f = w_down.shape[1] * 4
    c = d // N_GROUPS
    xy = lax.axis_index("x") * 2 + lax.axis_index("y")
    c_arr = lax.axis_index("c").astype(jnp.int32).reshape(1)
    jc_arr = jnp.stack([xy, lax.axis_index("c")]).astype(jnp.int32)
    nsh = 4
    zero = jnp.zeros((), jnp.int32)

    locs = [w_in[0].astype(BF16),
            jnp.stack([w_pool_proj[0], w_conv_out[0], w_o[0]]).astype(BF16),
            w_up[0].astype(BF16), w_down[0].astype(BF16), pool_w[0].astype(BF16)]
    full_shapes = [(nsh, d, N_SPLITS * d // nsh), (3, d, d), (nsh, d, 2 * f // nsh), (f, d), (N_GROUPS, c, c)]

    cw_pad = lax.dynamic_update_slice(jnp.zeros((3, d), F32), conv_w[0], (zero, xy * (d // 4)))
    fw_pad = lax.dynamic_update_slice(jnp.zeros((3, 2 * f), F32), ffn_conv_w[0], (zero, xy * (f // 2)))
    small_w = _pack([cw_pad, fw_pad]) * 0.5

    x2d = x.reshape(t, d)
    tgt = loss_target.reshape(t, d)
    ax, ay = lax.axis_index("x"), lax.axis_index("y")
    order = jnp.stack([xy, 2 * (1 - ax) + ay, 2 * ax + 1 - ay, 2 * (1 - ax) + 1 - ay]).astype(jnp.int32)
    (z, h1, w_in_f), (pool_w_f, w3_f, slots_w) = _fwd_in(
        x2d, norm_mix, locs[0], order,
        _merge([_gather_comm([4], locs, full_shapes), _gather_comm([1], locs, full_shapes, part=(0, 1, 2)),
                _small_comm(small_w)]))
    conv_w_f, ffn_cw_f = _unpack(_sum8(slots_w, "sum8_weights"), [(3, d), (3, 2 * f)])
    ffn_cw_p = ffn_cw_f.reshape(3, 2, f).transpose(1, 0, 2)
    ffn_cb_p = ffn_conv_b.reshape(2, 1, f)
    (lhs3,), (w3_f,) = _mixer_mid_fwd(z, pool_w_f, pool_scale, conv_w_f, nseq,
                                      _gather_comm([1], locs, full_shapes, part=(1, 2, 2), into={1: w3_f}))
    (lhs3, ypc, x1, h2), (w_up_f,) = _mixer_out(lhs3, z, x2d, w3_f, norm_ffn,
                                                _ring_gather_comm([2], locs, full_shapes))
    (u0,), (w_down_f,) = _ffn_up(h2, w_up_f, f, _gather_comm([3], locs, full_shapes))
    act, ua = _ffn_mid_fwd(u0, ffn_cw_p, ffn_cb_p, nseq)
    dx2, dx2b, loss11, g_norm_final = _ffn_down_loss(act, w_down_f, x1, tgt, norm_final.reshape(1, d))

    gbs, lands, ps, lands2, rs = {}, {}, {}, {}, {}
    tn_up = _tile(2 * f // nsh, 1408, LANES)
    npp = f // tn_up

    def add(arrs, name):
        for a, p in zip(arrs, _add_halves(arrs, gbs, [lands[a] for a in arrs], c_arr, name)):
            ps[a] = p

    def summed(a):
        rs[a] = _sum_chips(a, ps[a], lands2[a], _shard_shape(a, full_shapes[a]), jc_arr, "sum_chips_%d" % a)

    (gbs[3],), _ = _wgrad(act, dx2b, "wgrad_down", tr=tn_up, tn=d)
    (da,), (lands[3],) = _ffn_bwd_da(dx2b, w_down_f, _halves_comm([3], gbs))
    add([3], "add_halves_down")
    (du0, g_ffn_cw_p, g_ffn_cb_p), (lands2[3],) = _ffn_mid_bwd(da, u0, ua, ffn_cw_p, nseq, _chips_comm([3], ps))
    summed(3)
    (gbs[2],), (rs[3],) = _wgrad(h2, du0, "wgrad_up", tr=d, tn=tn_up, b_plane_of=lambda n: (n // npp, n % npp),
                                 out_shards=nsh, comm=_result_comm([3], rs))
    (dx1, rhs3, g_norm_ffn), (lands[2],) = _ffn_bwd_dx1(du0, w_up_f, x1, dx2, norm_ffn, 3, _halves_comm([2], gbs))
    add([2], "add_halves_up")
    (rhs3, dz, dpq), (lands2[2],) = _mixer_bwd(rhs3, z, ypc, w3_f, _chips_comm([2], ps, part=(0, 1, 2)))
    (gbs[1],), (lands2[2],) = _wgrad3(lhs3, rhs3, _chips_comm([2], ps, part=(1, 2, 2), into=lands2))
    summed(2)
    (dz, g_conv_w), (lands[1], rs[2]) = _conv_bwd(dz, dpq, z, conv_w_f, nseq,
                                                  _merge([_halves_comm([1], gbs), _result_comm([2], rs)]))
    add([1], "add_halves_sq3")
    (dz, g_pool_w, g_pool_scale), _ = _pool_bwd_call(dz, dpq, z, pool_w_f, pool_scale, nseq)
    gbs[4] = g_pool_w.astype(BF16)
    (gbs[0],), (lands2[1],) = _wgrad_in(h1, dz, nsh, _chips_comm([1], ps))
    summed(1)
    lands[0], lands[4] = _run_comm(_halves_comm([0, 4], gbs), "exchange_halves_in")
    add([0, 4], "add_halves_in")
    g_ffn_cw = g_ffn_cw_p.transpose(1, 0, 2).reshape(3, 2 * f)
    small_a = _pack([g_pool_scale, g_norm_ffn, g_ffn_cb_p.reshape(1, 2 * f), g_norm_final.reshape(d), g_conv_w,
                     g_ffn_cw, jnp.pad(loss11, ((0, SUBLANES_F32 - 1), (0, LANES - 1)))])
    (grad_x, g_norm_mix), (lands2[0], lands2[4], rs[1], slots_a) = _mixer_bwd_dx(
        dz, w_in_f, x2d, dx1, norm_mix,
        _merge([_chips_comm([0, 4], ps), _result_comm([1], rs), _small_comm(small_a)]))
    summed(0)
    summed(4)
    rs[0], rs[4], slots_b = _run_comm(_merge([_result_comm([0, 4], rs), _small_comm(_pack([g_norm_mix]))]),
                                      "exchange_result_in")
    shapes_a = [(1, d), (1, d), (1, 2 * f), (d,), (3, d), (3, 2 * f), (SUBLANES_F32, LANES)]
    gs_pool_scale, gs_norm_ffn, gs_ffn_cb, gs_norm_final, gs_conv_w, gs_ffn_cw, loss_blk = _unpack(
        _sum8(slots_a, "sum8_grads"), shapes_a)
    (gs_norm_mix,) = _unpack(_sum8(slots_b, "sum8_norm_mix"), [(1, d)])
    gs_conv_w = lax.dynamic_slice(gs_conv_w, (zero, xy * (d // 4)), (3, d // 4))
    gs_ffn_cw = lax.dynamic_slice(gs_ffn_cw, (zero, xy * (f // 2)), (3, f // 2))

    def upd(w, g, m, v, name, g_plane=None):
        shape = w.shape
        rows = 1
        for q in shape[:-1]:
            rows *= q
        g2 = g if g_plane is not None else g.reshape(rows, shape[-1])
        outs = _adamw(w.reshape(rows, shape[-1]), g2, m.reshape(rows, shape[-1]), v.reshape(rows, shape[-1]),
                      name, g_plane)
        return [o.reshape(shape) for o in outs]

    res = {
        "w_in": upd(w_in, rs[0], m_w_in, v_w_in, "adamw_w_in"),
        "pool_w": upd(pool_w, rs[4], m_pool_w, v_pool_w, "adamw_pool_w"),
        "w_pool_proj": upd(w_pool_proj, rs[1], m_w_pool_proj, v_w_pool_proj, "adamw_w_pool_proj", 0),
        "w_conv_out": upd(w_conv_out, rs[1], m_w_conv_out, v_w_conv_out, "adamw_w_conv_out", 1),
        "w_o": upd(w_o, rs[1], m_w_o, v_w_o, "adamw_w_o", 2),
        "w_up": upd(w_up, rs[2], m_w_up, v_w_up, "adamw_w_up"),
        "w_down": upd(w_down, rs[3], m_w_down, v_w_down, "adamw_w_down"),
    }

    small_names = ["norm_mix", "pool_scale", "norm_ffn", "ffn_conv_b", "norm_final", "conv_w", "ffn_conv_w"]
    small_ws = [norm_mix, pool_scale, norm_ffn, ffn_conv_b, norm_final, conv_w, ffn_conv_w]
    small_ms = [m_norm_mix, m_pool_scale, m_norm_ffn, m_ffn_conv_b, m_norm_final, m_conv_w, m_ffn_conv_w]
    small_vs = [v_norm_mix, v_pool_scale, v_norm_ffn, v_ffn_conv_b, v_norm_final, v_conv_w, v_ffn_conv_w]
    small_gs = [gs_norm_mix, gs_pool_scale, gs_norm_ffn, gs_ffn_cb, gs_norm_final, gs_conv_w, gs_ffn_cw]
    _, sd, sm, sv = _adamw(_pack(small_ws), _pack(small_gs), _pack(small_ms), _pack(small_vs), "adamw_small")
    shapes = [w.shape for w in small_ws]
    sd, sm, sv = _unpack(sd, shapes), _unpack(sm, shapes), _unpack(sv, shapes)
    for i, nm in enumerate(small_names):
        res[nm] = [small_gs[i].reshape(shapes[i]), sd[i], sm[i], sv[i]]

    order = ["norm_mix", "w_in", "pool_w", "pool_scale", "w_pool_proj", "conv_w", "w_conv_out", "w_o", "norm_ffn",
             "w_up", "ffn_conv_w", "ffn_conv_b", "w_down", "norm_final"]
    return (loss_blk[0, 0], grad_x.reshape(x.shape), *[res[n][0] for n in order], *[res[n][1] for n in order],
            *[res[n][2] for n in order], *[res[n][3] for n in order])
```

```python
import math

import jax
import jax.numpy as jnp
from jax import lax
from jax.experimental import pallas as pl
from jax.experimental.pallas import tpu as pltpu

F32 = jnp.float32
BF16 = jnp.bfloat16
SDS = jax.ShapeDtypeStruct
MESH = pl.DeviceIdType.MESH

RMS_EPS = 1e-6
POOL_WINDOWS = (2, 4, 8, 16)
N_GROUPS = len(POOL_WINDOWS)
N_SPLITS = 6

ADAM_LR = 0.001
ADAM_B1 = 0.9
ADAM_B2 = 0.999
ADAM_EPS = 1e-08
ADAM_WD = 0.01
ADAM_STEP = 10

LANES = 128
SUBLANES_F32 = 8
SUBLANES_BF16 = 16
VMEM_BYTES = 64 * 1024 * 1024
VMEM_CAP = VMEM_BYTES - 8 * 1024 * 1024
VMEM_FLOOR = 16 * 1024 * 1024

ANY = pl.BlockSpec(memory_space=pl.ANY)


def _tile(dim, pref, align):
    if dim <= pref:
        return dim
    t = (pref // align) * align
    while t >= align:
        if dim % t == 0:
            return t
        t -= align
    return dim


def _nbytes(shape, dtype):
    n = 1
    for s in shape:
        n *= s
    return n * jnp.dtype(dtype).itemsize


def _params(sem, block_bytes, temp_bytes=0, collective_id=None):
    need = 2 * block_bytes + temp_bytes + 4 * 1024 * 1024
    return pltpu.CompilerParams(dimension_semantics=sem, collective_id=collective_id,
                                vmem_limit_bytes=int(min(max(need, VMEM_FLOOR), VMEM_CAP)))


SIBLING = (0, 0, 1)
CHIPS = ((1, 0, 0), (0, 1, 0), (1, 1, 0))
EVERYONE = tuple((a, b, c) for a in range(2) for b in range(2) for c in range(2) if a + b + c)
PEER_SETS = (frozenset([SIBLING]), frozenset(CHIPS), frozenset(CHIPS + (SIBLING,)), frozenset(EVERYONE))
MID_AT = 0.75


def _collective_id(peers):
    return PEER_SETS.index(frozenset(peers))


def _handshake(peers):
    x, y, c = lax.axis_index("x"), lax.axis_index("y"), lax.axis_index("c")
    bar = pltpu.get_barrier_semaphore()
    for fx, fy, fc in sorted(peers):
        dev = (1 - x if fx else x, 1 - y if fy else y, 1 - c if fc else c)
        pl.semaphore_signal(bar, inc=1, device_id=dev, device_id_type=MESH)
    pl.semaphore_wait(bar, len(peers))


class _Comm:
    def __init__(self, ins, out_shapes, sems, start, finish, peers, aliases=None, mid=None):
        self.ins = list(ins)
        self.out_shapes = list(out_shapes)
        self.sems = list(sems)
        self.start = start
        self.finish = finish
        self.mid = mid
        self.peers = frozenset(peers)
        self.aliases = dict(aliases or {})


def _pcall(body, *, name, grid, in_specs, out_specs, out_shape, sem, blocks, temps=0, scratch_shapes=(),
           input_output_aliases=None, comm=None):
    in_specs = list(in_specs)
    out_specs = list(out_specs)
    out_shape = list(out_shape)
    scratch_shapes = list(scratch_shapes)
    aliases = dict(input_output_aliases or {})
    n_in, n_out, n_scr = len(in_specs), len(out_shape), len(scratch_shapes)
    if comm is None:
        call = pl.pallas_call(
            body, name=name, grid=grid, in_specs=in_specs, out_specs=out_specs, out_shape=out_shape,
            scratch_shapes=scratch_shapes, input_output_aliases=aliases,
            compiler_params=_params(sem, blocks, temps))
        return lambda *args: (list(call(*args)), [])

    nci, nco = len(comm.ins), len(comm.out_shapes)
    n_steps = 1
    for g in grid:
        n_steps *= g

    def hosted(*refs):
        ins = refs[:n_in]
        cins = refs[n_in:n_in + nci]
        outs = refs[n_in + nci:n_in + nci + n_out]
        couts = refs[n_in + nci + n_out:n_in + nci + n_out + nco]
        scr = refs[n_in + nci + n_out + nco:n_in + nci + n_out + nco + n_scr]
        csems = refs[n_in + nci + n_out + nco + n_scr:]
        first = None
        last = None
        step = 0
        for q, g in enumerate(grid):
            pid = pl.program_id(q)
            first = (pid == 0) if first is None else first & (pid == 0)
            last = (pid == g - 1) if last is None else last & (pid == g - 1)
            step = step * g + pid

        @pl.when(first)
        def _():
            _handshake(comm.peers)
            comm.start(cins, couts, csems)

        if comm.mid is not None:
            @pl.when(step == int(MID_AT * n_steps))
            def _():
                comm.mid(cins, couts, csems)

        body(*ins, *outs, *scr)

        @pl.when(last)
        def _():
            comm.finish(cins, couts, csems)

    for i, o in comm.aliases.items():
        aliases[n_in + i] = n_out + o
    call = pl.pallas_call(
        hosted, name=name, grid=grid, in_specs=in_specs + [ANY] * nci, out_specs=out_specs + [ANY] * nco,
        out_shape=out_shape + comm.out_shapes, scratch_shapes=scratch_shapes + comm.sems,
        input_output_aliases=aliases,
        compiler_params=_params(("arbitrary",) * len(grid), blocks, temps, _collective_id(comm.peers)))

    def run(*args):
        res = call(*args, *comm.ins)
        return list(res[:n_out]), list(res[n_out:])

    return run


def _run_comm(comm, name):
    def body(*refs):
        nci, nco = len(comm.ins), len(comm.out_shapes)
        cins, couts, csems = refs[:nci], refs[nci:nci + nco], refs[nci + nco:]
        _handshake(comm.peers)
        comm.start(cins, couts, csems)
        if comm.mid is not None:
            comm.mid(cins, couts, csems)
        comm.finish(cins, couts, csems)

    return list(pl.pallas_call(
        body, name=name, in_specs=[ANY] * len(comm.ins), out_specs=[ANY] * len(comm.out_shapes),
        out_shape=comm.out_shapes, scratch_shapes=comm.sems, input_output_aliases=comm.aliases,
        compiler_params=pltpu.CompilerParams(collective_id=_collective_id(comm.peers)),
    )(*comm.ins))


def _dot(a, b):
    return jnp.dot(a, b, preferred_element_type=F32)


def _dot_tb(a, b):
    return lax.dot_general(a, b, (((1,), (1,)), ((), ())), preferred_element_type=F32)


def _dot_ta(a, b):
    return lax.dot_general(a, b, (((0,), (0,)), ((), ())), preferred_element_type=F32)


def _rms_fwd(x):
    inv = lax.rsqrt(jnp.mean(x * x, axis=-1, keepdims=True) + RMS_EPS)
    return x * inv, inv


def _rms_bwd(dy, xhat, inv, g):
    gd = dy * g
    return inv * (gd - xhat * jnp.mean(gd * xhat, axis=-1, keepdims=True))


def _sigmoid(x):
    return 1.0 / (1.0 + jnp.exp(-x))


def _shift_down(x, k, row):
    return jnp.where(row >= k, pltpu.roll(x, k, 0), 0.0)


def _shift_up(x, k, row):
    s = x.shape[0]
    return jnp.where(row < s - k, pltpu.roll(x, s - k, 0), 0.0)


def _pool_fwd(u, win, row):
    s = u
    k = 1
    while k < win:
        s = s + _shift_down(s, k, row)
        k *= 2
    cnt = jnp.minimum(row + 1, win).astype(F32)
    return s / cnt - u


def _pool_bwd(dp, win, row):
    cnt = jnp.minimum(row + 1, win).astype(F32)
    s = dp / cnt
    k = 1
    while k < win:
        s = s + _shift_up(s, k, row)
        k *= 2
    return s - dp


def _acc_over(k, nk, part, acc, o_ref):
    @pl.when(k == 0)
    def _():
        acc[...] = part

    @pl.when(k > 0)
    def _():
        acc[...] += part

    @pl.when(k == nk - 1)
    def _():
        o_ref[...] = acc[...].astype(o_ref.dtype)


def _fwd_in(x, g, w_loc, order, comm):
    t, d = x.shape
    ws = w_loc.shape[1]
    nsh = order.shape[0]
    assert nsh == 4, "the shard walk below is written for the 2 x 2 chips of the mesh"
    tm = _tile(t, 1024, SUBLANES_BF16)
    ni = t // tm
    nci, nco = len(comm.ins), len(comm.out_shapes)
    all_peers = comm.peers | frozenset(CHIPS + (SIBLING,))

    def body(order_ref, x_ref, g_ref, loc_ref, *rest):
        del order_ref
        cins = rest[:nci]
        z_ref, h_ref, full_ref = rest[nci:nci + 3]
        couts = rest[nci + 3:nci + 3 + nco]
        (hs, wbuf, wsem, own_s, own_r, snd_s, snd_r, fwd_s, fwd_r, rly_s, rly_r) = rest[nci + 3 + nco:nci + 14 + nco]
        csems = rest[nci + 14 + nco:]
        j = pl.program_id(0)
        i = pl.program_id(1)
        x_, y_, c_ = _coords()
        own = 2 * x_ + y_
        sib = (x_, y_, 1 - c_)
        peers = _peer_chips(x_, y_)

        def sends():
            return [_remote(_half(0, loc_ref, c_), _piece(0, full_ref, own, c_), snd_s.at[p], snd_r.at[p], (px, py, c_))
                    for p, (px, py) in enumerate(peers[:2])]

        def relays():
            out = []
            for q, (src_p, dst_p) in enumerate(((0, 1), (1, 0))):
                sx, sy = peers[src_p]
                part = _rows_part(0, _piece(0, full_ref, 2 * sx + sy, c_), (q, q + 1, 2))
                out.append(_remote(part, part, rly_s.at[q], rly_r.at[q], (*peers[dst_p], c_)))
            return out

        def owns():
            return [_remote(_half(0, loc_ref, h), _piece(0, full_ref, own, h), own_s.at[h], own_r.at[h], sib)
                    for h in range(2)]

        def forward(p, half):
            px, py = peers[p]
            landed = _piece(0, full_ref, 2 * px + py, half)
            return _remote(landed, landed, fwd_s.at[p], fwd_r.at[p], sib)

        def load(src, slot):
            return pltpu.make_async_copy(src, wbuf.at[slot], wsem.at[slot])

        @pl.when((j == 0) & (i == 0))
        def _():
            _handshake(all_peers)
            for cp in sends() + owns():
                cp.start()
            load(loc_ref, 0).start(priority=1)

        @pl.when(j == 0)
        def _():
            xh, _ = _rms_fwd(x_ref[...])
            h = (xh * g_ref[...]).astype(BF16)
            hs[pl.ds(pl.multiple_of(i * tm, tm), tm), :] = h
            h_ref[...] = h

        slot = j % 2

        @pl.when(i == 0)
        def _():
            load(loc_ref, slot).wait()

        z_ref[...] = _dot(hs[pl.ds(pl.multiple_of(i * tm, tm), tm), :], wbuf[slot]).astype(BF16)

        def load_shard(p, into):
            px, py = peers[p]
            forward(p, 1 - c_).wait_recv()
            load(full_ref.at[2 * px + py], into).start(priority=1)

        @pl.when((j == 0) & (i == ni - 1))
        def _():
            for cp in sends():
                cp.wait_recv()
            for cp in relays() + [forward(0, c_), forward(1, c_)]:
                cp.start()
            load_shard(0, 1)
            comm.start(cins, couts, csems)

        @pl.when((j == 1) & (i == 0))
        def _():
            load_shard(1, 0)

        @pl.when((j == 2) & (i == max(ni - 2, 0)))
        def _():
            for cp in relays():
                cp.wait_recv()
            forward(2, c_).start()
            load_shard(2, 1)

        @pl.when((j == nsh - 1) & (i == ni - 1))
        def _():
            for cp in sends() + relays() + [forward(p, c_) for p in range(nsh - 1)]:
                cp.wait_send()
            for cp in owns():
                cp.wait()
            comm.finish(cins, couts, csems)

    last = ni - 1
    blocks = _nbytes((tm, d), F32) + _nbytes((tm, ws), BF16) + _nbytes((tm, d), BF16)
    scratch = _nbytes((t, d), BF16) + 2 * _nbytes((d, ws), BF16)
    res = pl.pallas_call(
        body, name="fwd_in",
        grid_spec=pltpu.PrefetchScalarGridSpec(
            num_scalar_prefetch=1, grid=(nsh, ni),
            in_specs=[pl.BlockSpec((tm, d), lambda j, i, o: (jnp.where(j == 0, i, last), 0)),
                      pl.BlockSpec((1, d), lambda j, i, o: (0, 0)), ANY] + [ANY] * nci,
            out_specs=[pl.BlockSpec((tm, ws), lambda j, i, o: (i, o[j])),
                       pl.BlockSpec((tm, d), lambda j, i, o: (jnp.where(j == 0, i, last), 0)), ANY] + [ANY] * nco,
            scratch_shapes=[pltpu.VMEM((t, d), BF16), pltpu.VMEM((2, d, ws), BF16)]
            + _dma_sems(2, 2, 2, 2, 2, nsh - 1, nsh - 1, 2, 2) + comm.sems),
        out_shape=[SDS((t, nsh * ws), BF16), SDS((t, d), BF16), SDS((nsh, d, ws), BF16)] + comm.out_shapes,
        input_output_aliases={4 + i: 3 + o for i, o in comm.aliases.items()},
        compiler_params=_params(("arbitrary", "arbitrary"), blocks, scratch + 3 * _nbytes((tm, d), F32),
                                _collective_id(all_peers)),
    )(order, x, g, w_loc, *comm.ins)
    return list(res[:3]), list(res[3:])


def _mixer_mid_fwd(z, pool_w, pool_scale, conv_w, nseq, comm=None):
    t = z.shape[0]
    d = pool_scale.shape[1]
    s = t // nseq
    c = d // N_GROUPS

    def body(zp, zb, zc, zv, pw, ps, cw, o):
        j = pl.program_id(1)
        row = lax.broadcasted_iota(jnp.int32, (s, c), 0)
        for gi, win in enumerate(POOL_WINDOWS):
            @pl.when(j == gi)
            def _(win=win):
                pooled = _pool_fwd(zp[...].astype(F32), win, row)
                o[0] = (_dot(pooled.astype(BF16), pw[...]) * ps[...]).astype(BF16)

        cv = zc[...].astype(F32) * zv[...].astype(F32)
        cc = (cw[pl.ds(2, 1), :] * cv + cw[pl.ds(1, 1), :] * _shift_down(cv, 1, row)
              + cw[pl.ds(0, 1), :] * _shift_down(cv, 2, row))
        o[1] = (zb[...].astype(F32) * cc).astype(BF16)

    blocks = 4 * _nbytes((s, c), BF16) + _nbytes((c, c), BF16) + _nbytes((2, s, c), BF16)
    return _pcall(
        body, name="mixer_mid_fwd", grid=(nseq, N_GROUPS),
        in_specs=[pl.BlockSpec((s, c), lambda b, j: (b, j)),
                  pl.BlockSpec((s, c), lambda b, j: (b, N_GROUPS + j)),
                  pl.BlockSpec((s, c), lambda b, j: (b, 2 * N_GROUPS + j)),
                  pl.BlockSpec((s, c), lambda b, j: (b, 3 * N_GROUPS + j)),
                  pl.BlockSpec((None, c, c), lambda b, j: (j, 0, 0)),
                  pl.BlockSpec((1, c), lambda b, j: (0, j)),
                  pl.BlockSpec((3, c), lambda b, j: (0, j))],
        out_specs=[pl.BlockSpec((2, s, c), lambda b, j: (0, b, j))],
        out_shape=[SDS((3, t, d), BF16)],
        sem=("parallel", "parallel"), blocks=blocks, temps=8 * _nbytes((s, c), F32), comm=comm,
    )(z, z, z, z, pool_w, pool_scale, conv_w)


def _mixer_out(lhs3, z, x, w3, g_ffn, comm=None):
    t, d = x.shape
    tm = _tile(t, 256, SUBLANES_BF16)

    def body(pq, zgp, zgc, x_ref, w_ref, g_ref, mrg, ypc, x1o, h2o):
        yp = _dot(pq[0], w_ref[0])
        yc = _dot(pq[1], w_ref[1])
        m = _sigmoid(zgp[...].astype(F32)) * yp + _sigmoid(zgc[...].astype(F32)) * yc
        mb = m.astype(BF16)
        x1 = x_ref[...] + _dot(mb, w_ref[2])
        ypc[0] = yp.astype(BF16)
        ypc[1] = yc.astype(BF16)
        mrg[...] = mb
        x1o[...] = x1
        xh, _ = _rms_fwd(x1)
        h2o[...] = (xh * g_ref[...]).astype(BF16)

    blocks = (_nbytes((2, tm, d), BF16) * 2 + _nbytes((tm, d), BF16) * 4 + _nbytes((tm, d), F32) * 2
              + _nbytes((3, d, d), BF16))
    return _pcall(
        body, name="mixer_out", grid=(t // tm,),
        in_specs=[pl.BlockSpec((2, tm, d), lambda i: (0, i, 0)),
                  pl.BlockSpec((tm, d), lambda i: (i, 4)),
                  pl.BlockSpec((tm, d), lambda i: (i, 5)),
                  pl.BlockSpec((tm, d), lambda i: (i, 0)),
                  pl.BlockSpec((3, d, d), lambda i: (0, 0, 0)),
                  pl.BlockSpec((1, d), lambda i: (0, 0))],
        out_specs=[pl.BlockSpec((None, tm, d), lambda i: (2, i, 0)),
                   pl.BlockSpec((2, tm, d), lambda i: (0, i, 0)),
                   pl.BlockSpec((tm, d), lambda i: (i, 0)),
                   pl.BlockSpec((tm, d), lambda i: (i, 0))],
        out_shape=[SDS(lhs3.shape, BF16), SDS((2, t, d), BF16), SDS((t, d), F32), SDS((t, d), BF16)],
        input_output_aliases={0: 0},
        sem=("parallel",), blocks=blocks, temps=8 * _nbytes((tm, d), F32), comm=comm,
    )(lhs3, z, z, x, w3, g_ffn)


def _ffn_up(h2, w_up, f, comm=None):
    t, d = h2.shape
    _, _, ws = w_up.shape
    tm = _tile(t, 2048, SUBLANES_BF16)
    tn = _tile(ws, 1408, LANES)
    nps = ws // tn
    npp = f // tn

    def body(h_ref, w_ref, o_ref):
        o_ref[...] = _dot(h_ref[...], w_ref[...]).astype(BF16)

    blocks = _nbytes((tm, d), BF16) + _nbytes((d, tn), BF16) + _nbytes((tm, tn), BF16)
    return _pcall(
        body, name="ffn_up", grid=(t // tm, 2 * npp),
        in_specs=[pl.BlockSpec((tm, d), lambda i, j: (i, 0)),
                  pl.BlockSpec((None, d, tn), lambda i, j: (j // nps, 0, j % nps))],
        out_specs=[pl.BlockSpec((None, tm, tn), lambda i, j: (j // npp, i, j % npp))],
        out_shape=[SDS((2, t, f), BF16)],
        sem=("parallel", "parallel"), blocks=blocks, temps=_nbytes((tm, tn), F32), comm=comm,
    )(h2, w_up)


def _conv3_rows(u, u1, u2, w_ref, p):
    return w_ref[p, pl.ds(2, 1), :] * u + w_ref[p, pl.ds(1, 1), :] * u1 + w_ref[p, pl.ds(0, 1), :] * u2


WGRAD_TOKENS = 2048
WGRAD_TOKENS_WIDE = 4096
CHUNK = 64
HALO = SUBLANES_F32


def _up1_up2(u, nxt):
    rows = u.shape[0]
    ext = jnp.concatenate([u, nxt], axis=0)
    n = rows + HALO
    return pltpu.roll(ext, n - 1, 0)[:rows], pltpu.roll(ext, n - 2, 0)[:rows]


def _fold8(x):
    return jnp.sum(x.reshape(x.shape[0] // SUBLANES_F32, SUBLANES_F32, x.shape[1]), axis=0)


def _ffn_mid_fwd(u0, cw, cb, nseq):
    _, t, f = u0.shape
    s = t // nseq
    c = _tile(f, 256, LANES)

    def body(u_ref, w_ref, b_ref, a_ref, uo_ref):
        row = lax.broadcasted_iota(jnp.int32, (s, c), 0)
        act = []
        for p in range(2):
            u = u_ref[p].astype(F32)
            act.append(_conv3_rows(u, _shift_down(u, 1, row), _shift_down(u, 2, row), w_ref, p) + b_ref[p])
            uo_ref[p] = act[p].astype(BF16)
        ug, uv = act
        a_ref[...] = (ug * _sigmoid(ug) * uv).astype(BF16)

    blocks = 2 * _nbytes((2, s, c), BF16) + _nbytes((s, c), BF16)
    outs, _ = _pcall(
        body, name="ffn_mid_fwd", grid=(f // c, nseq),
        in_specs=[pl.BlockSpec((2, s, c), lambda j, b: (0, b, j)),
                  pl.BlockSpec((2, 3, c), lambda j, b: (0, 0, j)),
                  pl.BlockSpec((2, 1, c), lambda j, b: (0, 0, j))],
        out_specs=[pl.BlockSpec((s, c), lambda j, b: (b, j)),
                   pl.BlockSpec((2, s, c), lambda j, b: (0, b, j))],
        out_shape=[SDS((t, f), BF16), SDS((2, t, f), BF16)],
        sem=("parallel", "parallel"), blocks=blocks, temps=8 * _nbytes((s, c), F32),
    )(u0, cw, cb)
    return outs


def _ffn_down_loss(a, w_down, x1, tgt, g_fin):
    t, f = a.shape
    d = x1.shape[1]
    tm = _tile(t, 256, SUBLANES_BF16)
    nsteps = t // tm

    def body(a_ref, w_ref, x1_ref, t_ref, g_ref, dx_ref, dxb_ref, loss_ref, gg_ref, lacc):
        i = pl.program_id(0)

        @pl.when(i == 0)
        def _():
            lacc[...] = jnp.zeros_like(lacc)
            gg_ref[...] = jnp.zeros_like(gg_ref)

        x2 = x1_ref[...] + _dot(a_ref[...], w_ref[...])
        xh, inv = _rms_fwd(x2)
        g = g_ref[...]
        e = xh * g - t_ref[...]
        lacc[...] += jnp.sum(e * e, axis=0, keepdims=True)
        dy = e * (1.0 / d)
        gg_ref[...] += jnp.sum(dy * xh, axis=0, keepdims=True)
        dx2 = _rms_bwd(dy, xh, inv, g)
        dx_ref[...] = dx2
        dxb_ref[...] = dx2.astype(BF16)

        @pl.when(i == nsteps - 1)
        def _():
            loss_ref[...] = jnp.sum(lacc[...], axis=1, keepdims=True) * (0.5 / d)

    blocks = (_nbytes((tm, f), BF16) + _nbytes((f, d), BF16) + 3 * _nbytes((tm, d), F32) + _nbytes((tm, d), BF16))
    outs, _ = _pcall(
        body, name="ffn_down_loss", grid=(nsteps,),
        in_specs=[pl.BlockSpec((tm, f), lambda i: (i, 0)), pl.BlockSpec((f, d), lambda i: (0, 0)),
                  pl.BlockSpec((tm, d), lambda i: (i, 0)), pl.BlockSpec((tm, d), lambda i: (i, 0)),
                  pl.BlockSpec((1, d), lambda i: (0, 0))],
        out_specs=[pl.BlockSpec((tm, d), lambda i: (i, 0)), pl.BlockSpec((tm, d), lambda i: (i, 0)),
                   pl.BlockSpec((1, 1), lambda i: (0, 0)), pl.BlockSpec((1, d), lambda i: (0, 0))],
        out_shape=[SDS((t, d), F32), SDS((t, d), BF16), SDS((1, 1), F32), SDS((1, d), F32)],
        scratch_shapes=[pltpu.VMEM((1, d), F32)],
        sem=("arbitrary",), blocks=blocks, temps=8 * _nbytes((tm, d), F32),
    )(a, w_down, x1, tgt, g_fin)
    return outs


def _ffn_bwd_da(dxb, w_down, comm=None):
    t, d = dxb.shape
    f = w_down.shape[0]
    tm = _tile(t, 512, SUBLANES_BF16)

    def body(x_ref, w_ref, o_ref):
        o_ref[...] = _dot_tb(x_ref[...], w_ref[...]).astype(BF16)

    blocks = _nbytes((tm, d), BF16) + _nbytes((tm, f), BF16)
    return _pcall(
        body, name="ffn_bwd_da", grid=(t // tm,),
        in_specs=[pl.BlockSpec((tm, d), lambda i: (i, 0)),
                  pl.BlockSpec((f, d), lambda i: (0, 0), pipeline_mode=pl.Buffered(1))],
        out_specs=[pl.BlockSpec((tm, f), lambda i: (i, 0))],
        out_shape=[SDS((t, f), BF16)],
        sem=("parallel",), blocks=blocks, temps=_nbytes((f, d), BF16) + _nbytes((tm, f), F32), comm=comm,
    )(dxb, w_down)


def _ffn_mid_bwd(da, u0, ua, cw, nseq, comm=None):
    _, t, f = u0.shape
    s = t // nseq
    c = _tile(f, 128, LANES)
    r = _tile(s, CHUNK, SUBLANES_BF16)
    n = s // r

    def body(da_ref, u_ref, ua_ref, w_ref, du_ref, gw_ref, gb_ref):
        @pl.when(pl.program_id(1) == 0)
        def _():
            gw_ref[...] = jnp.zeros_like(gw_ref)
            gb_ref[...] = jnp.zeros_like(gb_ref)

        def step(i, carry):
            nxt, sums = carry
            rows = pl.ds(pl.multiple_of((n - 1 - i) * r, r), r)
            ug = ua_ref[0, rows, :].astype(F32)
            uv = ua_ref[1, rows, :].astype(F32)
            sg = _sigmoid(ug)
            dacc = da_ref[rows, :].astype(F32)
            dus = (dacc * uv * sg * (1.0 + ug * (1.0 - sg)), dacc * (ug * sg))
            first, new_sums = [], []
            for p in range(2):
                du = dus[p]
                d1, d2 = _up1_up2(du, nxt[p])
                du_ref[p, rows, :] = _conv3_rows(du, d1, d2, w_ref, p).astype(BF16)
                u = u_ref[p, rows, :].astype(F32)
                sb, s0, s1, s2 = sums[p]
                new_sums.append((sb + _fold8(du), s0 + _fold8(d2 * u), s1 + _fold8(d1 * u), s2 + _fold8(du * u)))
                first.append(du[:HALO])
            return tuple(first), tuple(new_sums)

        zero = jnp.zeros((HALO, c), F32)
        _, sums = lax.fori_loop(0, n, step, ((zero, zero), ((zero,) * 4,) * 2))
        for p in range(2):
            sb, s0, s1, s2 = sums[p]
            gb_ref[p] += jnp.sum(sb, axis=0, keepdims=True)
            gw_ref[p, pl.ds(0, 1), :] += jnp.sum(s0, axis=0, keepdims=True)
            gw_ref[p, pl.ds(1, 1), :] += jnp.sum(s1, axis=0, keepdims=True)
            gw_ref[p, pl.ds(2, 1), :] += jnp.sum(s2, axis=0, keepdims=True)

    blocks = _nbytes((s, c), BF16) + 3 * _nbytes((2, s, c), BF16)
    return _pcall(
        body, name="ffn_mid_bwd", grid=(f // c, nseq),
        in_specs=[pl.BlockSpec((s, c), lambda j, b: (b, j)),
                  pl.BlockSpec((2, s, c), lambda j, b: (0, b, j)),
                  pl.BlockSpec((2, s, c), lambda j, b: (0, b, j)),
                  pl.BlockSpec((2, 3, c), lambda j, b: (0, 0, j))],
        out_specs=[pl.BlockSpec((2, s, c), lambda j, b: (0, b, j)),
                   pl.BlockSpec((2, 3, c), lambda j, b: (0, 0, j)),
                   pl.BlockSpec((2, 1, c), lambda j, b: (0, 0, j))],
        out_shape=[SDS((2, t, f), BF16), SDS((2, 3, f), F32), SDS((2, 1, f), F32)],
        sem=("parallel", "arbitrary"), blocks=blocks, temps=4 * 1024 * 1024, comm=comm,
    )(da, u0, ua, cw)


def _wgrad(a, b, name, *, tr, tn, b_plane_of=None, out_shards=None, comm=None):
    t, m = a.shape
    n_total = b.shape[-1] * (b.shape[0] if b.ndim == 3 else 1)
    tk = _tile(t, WGRAD_TOKENS_WIDE if n_total > tn and m == tr else WGRAD_TOKENS, SUBLANES_BF16)
    nk = t // tk
    once = pl.Buffered(1) if nk == 1 else None

    def body(a_ref, b_ref, o_ref, *acc):
        part = _dot_ta(a_ref[...], b_ref[...])
        if nk == 1:
            o_ref[...] = part.astype(BF16)
        else:
            _acc_over(pl.program_id(2), nk, part, acc[0], o_ref)

    if b.ndim == 3:
        b_spec = pl.BlockSpec((None, tk, tn), lambda r, n, k: (b_plane_of(n)[0], k, b_plane_of(n)[1]))
    else:
        b_spec = pl.BlockSpec((tk, tn), lambda r, n, k: (k, n), pipeline_mode=once if n_total == tn else None)
    if out_shards is None:
        o_spec = pl.BlockSpec((tr, tn), lambda r, n, k: (r, n))
        o_shape = SDS((m, n_total), BF16)
    else:
        nps = n_total // out_shards // tn
        o_spec = pl.BlockSpec((None, tr, tn), lambda r, n, k: (n // nps, r, n % nps))
        o_shape = SDS((out_shards, m, n_total // out_shards), BF16)
    blocks = _nbytes((tk, tr), BF16) + _nbytes((tk, tn), BF16) + _nbytes((tr, tn), BF16)
    return _pcall(
        body, name=name, grid=(m // tr, n_total // tn, nk),
        in_specs=[pl.BlockSpec((tk, tr), lambda r, n, k: (k, r), pipeline_mode=once if m == tr else None), b_spec],
        out_specs=[o_spec], out_shape=[o_shape],
        scratch_shapes=[] if nk == 1 else [pltpu.VMEM((tr, tn), F32)],
        sem=("parallel", "parallel", "arbitrary"), blocks=blocks, temps=2 * _nbytes((tr, tn), F32), comm=comm,
    )(a, b)


def _wgrad3(lhs3, rhs3, comm=None):
    nw, t, d = lhs3.shape
    tk = _tile(t, WGRAD_TOKENS, SUBLANES_BF16)
    nk = t // tk

    def body(a_ref, b_ref, o_ref, *acc):
        part = _dot_ta(a_ref[...], b_ref[...])
        if nk == 1:
            o_ref[...] = part.astype(BF16)
        else:
            _acc_over(pl.program_id(1), nk, part, acc[0], o_ref)

    blocks = 2 * _nbytes((tk, d), BF16) + _nbytes((d, d), BF16)
    return _pcall(
        body, name="wgrad_sq3", grid=(nw, nk),
        in_specs=[pl.BlockSpec((None, tk, d), lambda w, k: (w, k, 0)),
                  pl.BlockSpec((None, tk, d), lambda w, k: (w, k, 0))],
        out_specs=[pl.BlockSpec((None, d, d), lambda w, k: (w, 0, 0))],
        out_shape=[SDS((nw, d, d), BF16)],
        scratch_shapes=[] if nk == 1 else [pltpu.VMEM((d, d), F32)],
        sem=("parallel", "arbitrary"), blocks=blocks, temps=2 * _nbytes((d, d), F32), comm=comm,
    )(lhs3, rhs3)


def _ffn_bwd_dx1(du0, w_up, x1, dx2, g_ffn, n_planes_out, comm=None):
    _, t, f = du0.shape
    d = x1.shape[1]
    nsh, _, ws = w_up.shape
    tm = _tile(t, 256, SUBLANES_BF16)
    spp = f // ws

    def body(du_ref, w_ref, x1_ref, dx2_ref, g_ref, dx1_ref, dxb_ref, gg_ref):
        @pl.when(pl.program_id(0) == 0)
        def _():
            gg_ref[...] = jnp.zeros_like(gg_ref)

        dh = None
        for k in range(nsh):
            part = _dot_tb(du_ref[k // spp, :, (k % spp) * ws:(k % spp + 1) * ws], w_ref[k])
            dh = part if dh is None else dh + part
        xh, inv = _rms_fwd(x1_ref[...])
        gg_ref[...] += jnp.sum(dh * xh, axis=0, keepdims=True)
        dx1 = dx2_ref[...] + _rms_bwd(dh, xh, inv, g_ref[...])
        dx1_ref[...] = dx1
        dxb_ref[...] = dx1.astype(BF16)

    blocks = _nbytes((2, tm, f), BF16) + 3 * _nbytes((tm, d), F32) + _nbytes((tm, d), BF16)
    return _pcall(
        body, name="ffn_bwd_dx1", grid=(t // tm,),
        in_specs=[pl.BlockSpec((2, tm, f), lambda i: (0, i, 0)),
                  pl.BlockSpec((nsh, d, ws), lambda i: (0, 0, 0), pipeline_mode=pl.Buffered(1)),
                  pl.BlockSpec((tm, d), lambda i: (i, 0)),
                  pl.BlockSpec((tm, d), lambda i: (i, 0)),
                  pl.BlockSpec((1, d), lambda i: (0, 0))],
        out_specs=[pl.BlockSpec((tm, d), lambda i: (i, 0)),
                   pl.BlockSpec((None, tm, d), lambda i: (n_planes_out - 1, i, 0)),
                   pl.BlockSpec((1, d), lambda i: (0, 0))],
        out_shape=[SDS((t, d), F32), SDS((n_planes_out, t, d), BF16), SDS((1, d), F32)],
        sem=("arbitrary",), blocks=blocks, temps=_nbytes(w_up.shape, BF16) + 8 * _nbytes((tm, d), F32), comm=comm,
    )(du0, w_up, x1, dx2, g_ffn)


def _mixer_bwd(rhs3, z, ypc, w3, comm=None):
    _, t, d = rhs3.shape
    tm = _tile(t, 512, SUBLANES_BF16)

    def body(dx_ref, zgp, zgc, ypc_ref, w_ref, dyo, dzo, dpq):
        dm = _dot_tb(dx_ref[...], w_ref[2])
        sp = _sigmoid(zgp[...].astype(F32))
        sc = _sigmoid(zgc[...].astype(F32))
        dyp = (dm * sp).astype(BF16)
        dyc = (dm * sc).astype(BF16)
        dzo[0] = (dm * ypc_ref[0].astype(F32) * sp * (1.0 - sp)).astype(BF16)
        dzo[1] = (dm * ypc_ref[1].astype(F32) * sc * (1.0 - sc)).astype(BF16)
        dyo[0] = dyp
        dyo[1] = dyc
        dpq[0] = _dot_tb(dyp, w_ref[0]).astype(BF16)
        dpq[1] = _dot_tb(dyc, w_ref[1]).astype(BF16)

    blocks = _nbytes((tm, d), BF16) * 3 + _nbytes((2, tm, d), BF16) * 4 + _nbytes((3, d, d), BF16)
    return _pcall(
        body, name="mixer_bwd", grid=(t // tm,),
        in_specs=[pl.BlockSpec((None, tm, d), lambda i: (2, i, 0)),
                  pl.BlockSpec((tm, d), lambda i: (i, 4)),
                  pl.BlockSpec((tm, d), lambda i: (i, 5)),
                  pl.BlockSpec((2, tm, d), lambda i: (0, i, 0)),
                  pl.BlockSpec((3, d, d), lambda i: (0, 0, 0))],
        out_specs=[pl.BlockSpec((2, tm, d), lambda i: (0, i, 0)),
                   pl.BlockSpec((2, tm, d), lambda i: (2, i, 0)),
                   pl.BlockSpec((2, tm, d), lambda i: (0, i, 0))],
        out_shape=[SDS(rhs3.shape, BF16), SDS((N_SPLITS, t, d), BF16), SDS((2, t, d), BF16)],
        input_output_aliases={0: 0},
        sem=("parallel",), blocks=blocks, temps=8 * _nbytes((tm, d), F32), comm=comm,
    )(rhs3, z, z, ypc, w3)


def _conv_bwd(dz, dpq, z, conv_w, nseq, comm=None):
    _, t, d = dz.shape
    s = t // nseq
    c = _tile(d, 256, LANES)
    nb = d // c

    def body(dz_in, dq_ref, zb, zc, zv, cw, dzo, gw_ref):
        del dz_in

        @pl.when(pl.program_id(1) == 0)
        def _():
            gw_ref[...] = jnp.zeros_like(gw_ref)

        row = lax.broadcasted_iota(jnp.int32, (s, c), 0)
        b = zb[...].astype(F32)
        cm = zc[...].astype(F32)
        v = zv[...].astype(F32)
        cv = cm * v
        cv1 = _shift_down(cv, 1, row)
        cv2 = _shift_down(cv, 2, row)
        w0, w1, w2 = cw[pl.ds(0, 1), :], cw[pl.ds(1, 1), :], cw[pl.ds(2, 1), :]
        cc = w2 * cv + w1 * cv1 + w0 * cv2
        dq = dq_ref[...].astype(F32)
        dzo[0] = (dq * cc).astype(BF16)
        dcc = dq * b
        gw_ref[pl.ds(0, 1), :] += jnp.sum(dcc * cv2, axis=0, keepdims=True)
        gw_ref[pl.ds(1, 1), :] += jnp.sum(dcc * cv1, axis=0, keepdims=True)
        gw_ref[pl.ds(2, 1), :] += jnp.sum(dcc * cv, axis=0, keepdims=True)
        dcv = w2 * dcc + w1 * _shift_up(dcc, 1, row) + w0 * _shift_up(dcc, 2, row)
        dzo[1] = (dcv * v).astype(BF16)
        dzo[2] = (dcv * cm).astype(BF16)

    blocks = 4 * _nbytes((s, c), BF16) + _nbytes((3, s, c), BF16)
    return _pcall(
        body, name="conv_bwd", grid=(nb, nseq),
        in_specs=[ANY,
                  pl.BlockSpec((None, s, c), lambda j, b: (1, b, j)),
                  pl.BlockSpec((s, c), lambda j, b: (b, nb + j)),
                  pl.BlockSpec((s, c), lambda j, b: (b, 2 * nb + j)),
                  pl.BlockSpec((s, c), lambda j, b: (b, 3 * nb + j)),
                  pl.BlockSpec((3, c), lambda j, b: (0, j))],
        out_specs=[pl.BlockSpec((3, s, c), lambda j, b: (0, b, j)),
                   pl.BlockSpec((3, c), lambda j, b: (0, j))],
        out_shape=[SDS(dz.shape, BF16), SDS((3, d), F32)],
        input_output_aliases={0: 0},
        sem=("parallel", "arbitrary"), blocks=blocks, temps=16 * _nbytes((s, c), F32), comm=comm,
    )(dz, dpq, z, z, z, conv_w)


def _pool_bwd_call(dz, dpq, z, pool_w, pool_scale, nseq, comm=None):
    _, t, d = dz.shape
    s = t // nseq
    c = d // N_GROUPS

    def body(dz_in, dp_ref, zp, pw, ps, dzo, gpw_ref, gps_ref):
        del dz_in
        j = pl.program_id(0)

        @pl.when(pl.program_id(1) == 0)
        def _():
            gpw_ref[...] = jnp.zeros_like(gpw_ref)
            gps_ref[...] = jnp.zeros_like(gps_ref)

        row = lax.broadcasted_iota(jnp.int32, (s, c), 0)
        for gi, win in enumerate(POOL_WINDOWS):
            @pl.when(j == gi)
            def _(win=win):
                pb = _pool_fwd(zp[...].astype(F32), win, row).astype(BF16)
                plin = _dot(pb, pw[...])
                dps = dp_ref[...].astype(F32)
                gps_ref[...] += jnp.sum(dps * plin, axis=0, keepdims=True)
                dplb = (dps * ps[...]).astype(BF16)
                gpw_ref[...] += _dot_ta(pb, dplb)
                dzo[...] = _pool_bwd(_dot_tb(dplb, pw[...]), win, row).astype(BF16)

    blocks = 3 * _nbytes((s, c), BF16) + _nbytes((c, c), BF16) + _nbytes((c, c), F32)
    return _pcall(
        body, name="pool_bwd", grid=(N_GROUPS, nseq),
        in_specs=[ANY,
                  pl.BlockSpec((None, s, c), lambda j, b: (0, b, j)),
                  pl.BlockSpec((s, c), lambda j, b: (b, j)),
                  pl.BlockSpec((None, c, c), lambda j, b: (j, 0, 0)),
                  pl.BlockSpec((1, c), lambda j, b: (0, j))],
        out_specs=[pl.BlockSpec((None, s, c), lambda j, b: (3, b, j)),
                   pl.BlockSpec((None, c, c), lambda j, b: (j, 0, 0)),
                   pl.BlockSpec((1, c), lambda j, b: (0, j))],
        out_shape=[SDS(dz.shape, BF16), SDS((N_GROUPS, c, c), F32), SDS((1, d), F32)],
        input_output_aliases={0: 0},
        sem=("parallel", "arbitrary"), blocks=blocks, temps=10 * _nbytes((s, c), F32), comm=comm,
    )(dz, dpq, z, pool_w, pool_scale)


def _dz_plane(zb):
    return jnp.where(zb < 4, (zb + 3) % 4, zb)


def _wgrad_in(h1, dz, nsh, comm=None):
    t, d = h1.shape
    ws = N_SPLITS * d // nsh
    kb = _tile(math.gcd(d, ws), 512, LANES)
    npl = d // kb
    nps = ws // kb
    tk = _tile(t, WGRAD_TOKENS_WIDE, SUBLANES_BF16)
    nk = t // tk

    def body(a_ref, b_ref, o_ref, *acc):
        part = _dot_ta(a_ref[...], b_ref[...])
        if nk == 1:
            o_ref[...] = part.astype(BF16)
        else:
            _acc_over(pl.program_id(1), nk, part, acc[0], o_ref)

    blocks = _nbytes((tk, d), BF16) + _nbytes((tk, kb), BF16) + _nbytes((d, kb), BF16)
    return _pcall(
        body, name="wgrad_in", grid=(N_SPLITS * npl, nk),
        in_specs=[pl.BlockSpec((tk, d), lambda cb, k: (k, 0), pipeline_mode=pl.Buffered(1) if nk == 1 else None),
                  pl.BlockSpec((None, tk, kb), lambda cb, k: (_dz_plane(cb // npl), k, cb % npl))],
        out_specs=[pl.BlockSpec((None, d, kb), lambda cb, k: (cb // nps, 0, cb % nps))],
        out_shape=[SDS((nsh, d, ws), BF16)],
        scratch_shapes=[] if nk == 1 else [pltpu.VMEM((d, kb), F32)],
        sem=("parallel", "arbitrary"), blocks=blocks, temps=2 * _nbytes((d, kb), F32), comm=comm,
    )(h1, dz)


def _mixer_bwd_dx(dz, w_in, x, dx1, g_mix, comm=None):
    npln, t, d = dz.shape
    nsh, _, ws = w_in.shape
    tm = _tile(t, 256, SUBLANES_BF16)
    kb = _tile(math.gcd(d, ws), 512, LANES)
    npl = d // kb
    nps = ws // kb

    def body(dz_ref, w_ref, x_ref, dx1_ref, g_ref, dx_ref, gg_ref):
        @pl.when(pl.program_id(0) == 0)
        def _():
            gg_ref[...] = jnp.zeros_like(gg_ref)

        dh = None
        for cb in range(npln * npl):
            zb = cb // npl
            plane = (zb + 3) % 4 if zb < 4 else zb
            part = _dot_tb(dz_ref[plane, :, (cb % npl) * kb:(cb % npl + 1) * kb],
                           w_ref[cb // nps, :, (cb % nps) * kb:(cb % nps + 1) * kb])
            dh = part if dh is None else dh + part
        xh, inv = _rms_fwd(x_ref[...])
        gg_ref[...] += jnp.sum(dh * xh, axis=0, keepdims=True)
        dx_ref[...] = dx1_ref[...] + _rms_bwd(dh, xh, inv, g_ref[...])

    blocks = _nbytes((npln, tm, d), BF16) + 3 * _nbytes((tm, d), F32)
    return _pcall(
        body, name="mixer_bwd_dx", grid=(t // tm,),
        in_specs=[pl.BlockSpec((npln, tm, d), lambda i: (0, i, 0)),
                  pl.BlockSpec((nsh, d, ws), lambda i: (0, 0, 0), pipeline_mode=pl.Buffered(1)),
                  pl.BlockSpec((tm, d), lambda i: (i, 0)),
                  pl.BlockSpec((tm, d), lambda i: (i, 0)),
                  pl.BlockSpec((1, d), lambda i: (0, 0))],
        out_specs=[pl.BlockSpec((tm, d), lambda i: (i, 0)),
                   pl.BlockSpec((1, d), lambda i: (0, 0))],
        out_shape=[SDS((t, d), F32), SDS((1, d), F32)],
        sem=("arbitrary",), blocks=blocks, temps=_nbytes(w_in.shape, BF16) + 8 * _nbytes((tm, d), F32), comm=comm,
    )(dz, w_in, x, dx1, g_mix)


N_BIG = 5
SHARD_MAJOR = (0, 2)
ROWS_DIM1 = (1, 4)


def _ds(start, size, align):
    if isinstance(start, int):
        return pl.ds(start, size)
    return pl.ds(pl.multiple_of(start, align), size)


def _piece(a, ref, k, h):
    if a in SHARD_MAJOR:
        r = ref.shape[1] // 2
        return ref.at[k, _ds(h * r, r, SUBLANES_BF16), :]
    if a in ROWS_DIM1:
        r = ref.shape[1] // 8
        return ref.at[:, _ds((2 * k + h) * r, r, SUBLANES_BF16), :]
    r = ref.shape[0] // 8
    return ref.at[_ds((2 * k + h) * r, r, SUBLANES_BF16), :]


def _half(a, ref, h):
    if a in ROWS_DIM1:
        r = ref.shape[1] // 2
        return ref.at[:, _ds(h * r, r, SUBLANES_BF16), :]
    r = ref.shape[0] // 2
    return ref.at[_ds(h * r, r, SUBLANES_BF16), :]


def _piece_shape(a, full_shape):
    if a in SHARD_MAJOR:
        return (full_shape[1] // 2, full_shape[2])
    if a in ROWS_DIM1:
        return (full_shape[0], full_shape[1] // 8, full_shape[2])
    return (full_shape[0] // 8, full_shape[1])


def _shard_shape(a, full_shape):
    if a in SHARD_MAJOR:
        return (full_shape[1], full_shape[2])
    if a in ROWS_DIM1:
        return (full_shape[0], full_shape[1] // 4, full_shape[2])
    return (full_shape[0] // 4, full_shape[1])


def _rows_axis(a):
    return 1 if a in ROWS_DIM1 else 0


def _piece_block(a, full_shape):
    ps = _piece_shape(a, full_shape)
    if a in SHARD_MAJOR:
        return (None,) + ps, lambda k, c: (k, c, 0)
    if a in ROWS_DIM1:
        return ps, lambda k, c: (0, 2 * k + c, 0)
    return ps, lambda k, c: (2 * k + c, 0)


def _coords():
    return lax.axis_index("x"), lax.axis_index("y"), lax.axis_index("c")


def _peer_chips(x, y):
    return [(1 - x, y), (x, 1 - y), (1 - x, 1 - y)]


def _remote(src, dst, ssem, rsem, dev):
    return pltpu.make_async_remote_copy(src_ref=src, dst_ref=dst, send_sem=ssem, recv_sem=rsem,
                                        device_id=dev, device_id_type=MESH)


def _dma_sems(*counts):
    return [pltpu.SemaphoreType.DMA((n,)) for n in counts]


def _symmetric(ins, out_shapes, sems, copies, peers, aliases=None):
    def start(cins, couts, csems):
        for cp in copies(cins, couts, csems):
            cp.start()

    def finish(cins, couts, csems):
        for cp in copies(cins, couts, csems):
            cp.wait()

    return _Comm(ins, out_shapes, sems, start, finish, peers, aliases)


def _rows_part(a, ref, part):
    if part is None:
        return ref
    p, q, n = part
    ax = _rows_axis(a)
    r = ref.shape[ax] // n
    return ref.at[tuple(pl.ds(p * r, (q - p) * r) if d == ax else slice(None) for d in range(len(ref.shape)))]


def _merge(comms):
    ins, outs, sems, aliases, spans = [], [], [], {}, []
    for cm in comms:
        spans.append((len(ins), len(outs), len(sems)))
        for i, o in cm.aliases.items():
            aliases[len(ins) + i] = len(outs) + o
        ins += cm.ins
        outs += cm.out_shapes
        sems += cm.sems

    def each(fn_name):
        def run(cins, couts, csems):
            for cm, (i0, o0, s0) in zip(comms, spans):
                fn = getattr(cm, fn_name)
                if fn is not None:
                    fn(cins[i0:i0 + len(cm.ins)], couts[o0:o0 + len(cm.out_shapes)], csems[s0:s0 + len(cm.sems)])
        return run

    return _Comm(ins, outs, sems, each("start"), each("finish"), frozenset().union(*[cm.peers for cm in comms]),
                 aliases, mid=each("mid") if any(cm.mid is not None for cm in comms) else None)


def _gather_comm(arrs, locs, full_shapes, part=None, into=None):
    n = len(arrs)

    def own(cins, couts, csems):
        x, y, c = _coords()
        j = 2 * x + y
        return [_remote(_rows_part(a, _half(a, cins[q], h), part), _rows_part(a, _piece(a, couts[q], j, h), part),
                        csems[0].at[2 * q + h], csems[1].at[2 * q + h], (x, y, 1 - c))
                for q, a in enumerate(arrs) for h in range(2)]

    def sends(cins, couts, csems):
        x, y, c = _coords()
        j = 2 * x + y
        return [_remote(_rows_part(a, _half(a, cins[q], c), part), _rows_part(a, _piece(a, couts[q], j, c), part),
                        csems[2].at[3 * q + i], csems[3].at[3 * q + i], (px, py, c))
                for q, a in enumerate(arrs) for i, (px, py) in enumerate(_peer_chips(x, y))]

    def forwards(couts, csems, half_of):
        x, y, c = _coords()
        out = []
        for q, a in enumerate(arrs):
            for i, (px, py) in enumerate(_peer_chips(x, y)):
                landed = _rows_part(a, _piece(a, couts[q], 2 * px + py, half_of(c)), part)
                out.append(_remote(landed, landed, csems[4].at[3 * q + i], csems[5].at[3 * q + i], (x, y, 1 - c)))
        return out

    def start(cins, couts, csems):
        for cp in sends(cins, couts, csems) + own(cins, couts, csems):
            cp.start()

    def finish(cins, couts, csems):
        fw = forwards(couts, csems, lambda c: c)
        for cp, f in zip(sends(cins, couts, csems), fw):
            cp.wait_recv()
            f.start()
        for f in forwards(couts, csems, lambda c: 1 - c):
            f.wait_recv()
        for cp in sends(cins, couts, csems) + fw:
            cp.wait_send()
        for cp in own(cins, couts, csems):
            cp.wait()

    ins = [locs[a] for a in arrs] + ([into[a] for a in arrs] if into else [])
    return _Comm(ins, [SDS(full_shapes[a], BF16) for a in arrs],
                 _dma_sems(2 * n, 2 * n, 3 * n, 3 * n, 3 * n, 3 * n), start, finish, CHIPS + (SIBLING,),
                 aliases={n + q: q for q in range(n)} if into else None)


def _ring_gather_comm(arrs, locs, full_shapes):
    n = len(arrs)

    def own(cins, couts, csems):
        x, y, c = _coords()
        j = 2 * x + y
        return [_remote(_half(a, cins[q], h), _piece(a, couts[q], j, h), csems[0].at[2 * q + h],
                        csems[1].at[2 * q + h], (x, y, 1 - c)) for q, a in enumerate(arrs) for h in range(2)]

    def sends(cins, couts, csems):
        x, y, c = _coords()
        j = 2 * x + y
        return [_remote(_half(a, cins[q], c), _piece(a, couts[q], j, c), csems[2].at[2 * q + i],
                        csems[3].at[2 * q + i], (px, py, c))
                for q, a in enumerate(arrs) for i, (px, py) in enumerate(_peer_chips(x, y)[:2])]

    def relays(couts, csems):
        x, y, c = _coords()
        peers = _peer_chips(x, y)
        out = []
        for q, a in enumerate(arrs):
            for r, (src_p, dst_p) in enumerate(((0, 1), (1, 0))):
                sx, sy = peers[src_p]
                rows = _rows_part(a, _piece(a, couts[q], 2 * sx + sy, c), (r, r + 1, 2))
                out.append(_remote(rows, rows, csems[6].at[2 * q + r], csems[7].at[2 * q + r], (*peers[dst_p], c)))
        return out

    def forwards(couts, csems, half_of, which):
        x, y, c = _coords()
        out = []
        for q, a in enumerate(arrs):
            for i in which:
                px, py = _peer_chips(x, y)[i]
                landed = _piece(a, couts[q], 2 * px + py, half_of(c))
                out.append(_remote(landed, landed, csems[4].at[3 * q + i], csems[5].at[3 * q + i], (x, y, 1 - c)))
        return out

    def start(cins, couts, csems):
        for cp in sends(cins, couts, csems) + own(cins, couts, csems):
            cp.start()

    def mid(cins, couts, csems):
        for cp in sends(cins, couts, csems):
            cp.wait_recv()
        for cp in relays(couts, csems) + forwards(couts, csems, lambda c: c, (0, 1)):
            cp.start()

    def finish(cins, couts, csems):
        for cp in relays(couts, csems):
            cp.wait_recv()
        fw_diag = forwards(couts, csems, lambda c: c, (2,))
        for f in fw_diag:
            f.start()
        for f in forwards(couts, csems, lambda c: 1 - c, (0, 1, 2)):
            f.wait_recv()
        for cp in (sends(cins, couts, csems) + relays(couts, csems)
                   + forwards(couts, csems, lambda c: c, (0, 1)) + fw_diag):
            cp.wait_send()
        for cp in own(cins, couts, csems):
            cp.wait()

    return _Comm([locs[a] for a in arrs], [SDS(full_shapes[a], BF16) for a in arrs],
                 _dma_sems(2 * n, 2 * n, 2 * n, 2 * n, 3 * n, 3 * n, 2 * n, 2 * n), start, finish,
                 CHIPS + (SIBLING,), mid=mid)


def _halves_comm(arrs, gbs):
    n = len(arrs)

    def copies(cins, couts, csems):
        x, y, c = _coords()
        return [_remote(_piece(a, cins[q], k, 1 - c), couts[q].at[k], csems[0].at[4 * q + k], csems[1].at[4 * q + k],
                        (x, y, 1 - c)) for q, a in enumerate(arrs) for k in range(4)]

    return _symmetric([gbs[a] for a in arrs], [SDS((4,) + _piece_shape(a, gbs[a].shape), BF16) for a in arrs],
                      _dma_sems(4 * n, 4 * n), copies, [SIBLING])


def _chips_comm(arrs, ps, part=None, into=None):
    n = len(arrs)

    def copies(cins, couts, csems):
        x, y, c = _coords()
        return [_remote(_rows_part(a, cins[q].at[2 * px + py], part), _rows_part(a, couts[q].at[i], part),
                        csems[0].at[3 * q + i], csems[1].at[3 * q + i], (px, py, c))
                for q, a in enumerate(arrs) for i, (px, py) in enumerate(_peer_chips(x, y))]

    ins = [ps[a] for a in arrs] + ([into[a] for a in arrs] if into else [])
    return _symmetric(ins, [SDS((3,) + ps[a].shape[1:], BF16) for a in arrs], _dma_sems(3 * n, 3 * n), copies, CHIPS,
                      aliases={n + q: q for q in range(n)} if into else None)


def _result_comm(arrs, gs):
    n = len(arrs)

    def copies(cins, couts, csems):
        x, y, c = _coords()
        return [_remote(_half(a, cins[q], c), _half(a, couts[q], c), csems[0].at[q], csems[1].at[q], (x, y, 1 - c))
                for q, a in enumerate(arrs)]

    return _symmetric([gs[a] for a in arrs], [SDS(gs[a].shape, F32) for a in arrs], _dma_sems(n, n), copies,
                      [SIBLING], aliases={q: q for q in range(n)})


def _add_halves(arrs, gbs, lands, c_arr, name):
    n = len(arrs)

    def body(c_ref, *refs):
        del c_ref
        for q in range(n):
            refs[2 * n + q][...] = (refs[q][...].astype(F32) + refs[n + q][...].astype(F32)).astype(BF16)

    g_specs, l_specs, o_specs, blocks = [], [], [], 0
    for a in arrs:
        bs, imap = _piece_block(a, gbs[a].shape)
        ps = _piece_shape(a, gbs[a].shape)
        g_specs.append(pl.BlockSpec(bs, lambda k, c_ref, imap=imap: imap(k, c_ref[0])))
        nd = len(ps)
        l_specs.append(pl.BlockSpec((None,) + ps, lambda k, c_ref, nd=nd: (k,) + (0,) * nd))
        o_specs.append(pl.BlockSpec((None,) + ps, lambda k, c_ref, nd=nd: (k,) + (0,) * nd))
        blocks += 3 * _nbytes(ps, BF16)
    return list(pl.pallas_call(
        body, name=name,
        grid_spec=pltpu.PrefetchScalarGridSpec(
            num_scalar_prefetch=1, grid=(4,), in_specs=g_specs + l_specs, out_specs=o_specs),
        out_shape=[SDS((4,) + _piece_shape(a, gbs[a].shape), BF16) for a in arrs],
        compiler_params=_params(("parallel",), blocks, blocks),
    )(c_arr, *[gbs[a] for a in arrs], *lands))


def _sum_chips(a, p, land, shard_shape, jc_arr, name):
    ps = land.shape[1:]
    ax = _rows_axis(a)
    rows = ps[ax]
    nsub = 2 if rows % (2 * SUBLANES_BF16) == 0 else 1
    bs = tuple(r // nsub if q == ax else r for q, r in enumerate(ps))
    nd = len(ps)

    def at_rows(v):
        return tuple(v if q == ax else 0 for q in range(nd))

    def body(jc_ref, p_ref, l_ref, o_ref):
        del jc_ref
        acc = p_ref[...].astype(F32) + l_ref[0].astype(F32)
        acc = acc + l_ref[1].astype(F32)
        o_ref[...] = acc + l_ref[2].astype(F32)

    blocks = 4 * _nbytes(bs, BF16) + _nbytes(bs, F32)
    return pl.pallas_call(
        body, name=name,
        grid_spec=pltpu.PrefetchScalarGridSpec(
            num_scalar_prefetch=1, grid=(nsub,),
            in_specs=[pl.BlockSpec((None,) + bs, lambda s, jc: (jc[0],) + at_rows(s)),
                      pl.BlockSpec((3,) + bs, lambda s, jc: (0,) + at_rows(s))],
            out_specs=pl.BlockSpec(bs, lambda s, jc: at_rows(jc[1] * nsub + s))),
        out_shape=SDS(shard_shape, F32),
        compiler_params=_params(("parallel",), blocks, 2 * _nbytes(bs, F32)),
    )(jc_arr, p, land)


def _small_comm(v):
    rows = v.shape[0]

    def copies(cins, couts, csems):
        x, y, c = _coords()
        me = 4 * x + 2 * y + c
        out = [pltpu.make_async_copy(cins[0], couts[0].at[me], csems[0].at[0])]
        for dlt in range(1, 8):
            px = 1 - x if (dlt >> 2) & 1 else x
            py = 1 - y if (dlt >> 1) & 1 else y
            pc = 1 - c if dlt & 1 else c
            out.append(_remote(cins[0], couts[0].at[me], csems[1].at[dlt - 1], csems[2].at[dlt - 1], (px, py, pc)))
        return out

    return _symmetric([v], [SDS((8, rows, LANES), F32)], _dma_sems(1, 7, 7), copies, EVERYONE)


def _sum8(slots, name):
    def body(s_ref, o_ref):
        acc = s_ref[0]
        for i in range(1, 8):
            acc = acc + s_ref[i]
        o_ref[...] = acc

    return pl.pallas_call(
        body, name=name,
        in_specs=[pl.BlockSpec(memory_space=pltpu.VMEM)], out_specs=pl.BlockSpec(memory_space=pltpu.VMEM),
        out_shape=SDS(slots.shape[1:], F32),
    )(slots)


def _adamw(w, g, m, v, name, g_plane=None):
    rows, cols = w.shape
    tr = _tile(rows, max(SUBLANES_F32, (256 * 1024 // cols) // SUBLANES_F32 * SUBLANES_F32), SUBLANES_F32)

    def body(w_ref, g_ref, m_ref, v_ref, go_ref, d_ref, mo_ref, vo_ref):
        gr = g_ref[...]
        mn = ADAM_B1 * m_ref[...] + (1.0 - ADAM_B1) * gr
        vn = ADAM_B2 * v_ref[...] + (1.0 - ADAM_B2) * (gr * gr)
        m_hat = mn / (1.0 - ADAM_B1 ** ADAM_STEP)
        v_hat = vn / (1.0 - ADAM_B2 ** ADAM_STEP)
        d_ref[...] = -ADAM_LR * (m_hat / (jnp.sqrt(v_hat) + ADAM_EPS) + ADAM_WD * w_ref[...])
        go_ref[...] = gr
        mo_ref[...] = mn
        vo_ref[...] = vn

    spec = pl.BlockSpec((tr, cols), lambda i: (i, 0))
    g_spec = spec if g_plane is None else pl.BlockSpec((None, tr, cols), lambda i: (g_plane, i, 0))
    return pl.pallas_call(
        body, name=name, grid=(rows // tr,),
        in_specs=[spec, g_spec, spec, spec], out_specs=[spec, spec, spec, spec],
        out_shape=[SDS((rows, cols), F32)] * 4,
        compiler_params=_params(("parallel",), 8 * _nbytes((tr, cols), F32), 4 * _nbytes((tr, cols), F32)),
    )(w, g, m, v)


def _pack(parts):
    rows = []
    for p in parts:
        r = p.reshape(-1, LANES)
        pad = (-r.shape[0]) % SUBLANES_F32
        if pad:
            r = jnp.pad(r, ((0, pad), (0, 0)))
        rows.append(r)
    return jnp.concatenate(rows, axis=0)


def _unpack(packed, shapes):
    out, at = [], 0
    for s in shapes:
        n = 1
        for q in s:
            n *= q
        r = n // LANES
        out.append(packed[at:at + r].reshape(s))
        at += r + (-r) % SUBLANES_F32
    return out


def kernel(x, norm_mix, w_in, pool_w, pool_scale, w_pool_proj, conv_w, w_conv_out, w_o, norm_ffn, w_up, ffn_conv_w, ffn_conv_b, w_down, norm_final, loss_target, m_norm_mix, m_w_in, m_pool_w, m_pool_scale, m_w_pool_proj, m_conv_w, m_w_conv_out, m_w_o, m_norm_ffn, m_w_up, m_ffn_conv_w, m_ffn_conv_b, m_w_down, m_norm_final, v_norm_mix, v_w_in, v_pool_w, v_pool_scale, v_w_pool_proj, v_conv_w, v_w_conv_out, v_w_o, v_norm_ffn, v_w_up, v_ffn_conv_w, v_ffn_conv_b, v_w_down, v_norm_final):
    nseq, seq, d = x.shape
    t = nseq * seq
    f = w_down.shape[1] * 4
    c = d // N_GROUPS
    xy = lax.axis_index("x") * 2 + lax.axis_index("y")
    c_arr = lax.axis_index("c").astype(jnp.int32).reshape(1)
    jc_arr = jnp.stack([xy, lax.axis_index("c")]).astype(jnp.int32)
    nsh = 4
    zero = jnp.zeros((), jnp.int32)

    locs = [w_in[0].astype(BF16),
            jnp.stack([w_pool_proj[0], w_conv_out[0], w_o[0]]).astype(BF16),
            w_up[0].astype(BF16), w_down[0].astype(BF16), pool_w[0].astype(BF16)]
    full_shapes = [(nsh, d, N_SPLITS * d // nsh), (3, d, d), (nsh, d, 2 * f // nsh), (f, d), (N_GROUPS, c, c)]

    cw_pad = lax.dynamic_update_slice(jnp.zeros((3, d), F32), conv_w[0], (zero, xy * (d // 4)))
    fw_pad = lax.dynamic_update_slice(jnp.zeros((3, 2 * f), F32), ffn_conv_w[0], (zero, xy * (f // 2)))
    small_w = _pack([cw_pad, fw_pad]) * 0.5

    x2d = x.reshape(t, d)
    tgt = loss_target.reshape(t, d)
    ax, ay = lax.axis_index("x"), lax.axis_index("y")
    order = jnp.stack([xy, 2 * (1 - ax) + ay, 2 * ax + 1 - ay, 2 * (1 - ax) + 1 - ay]).astype(jnp.int32)
    (z, h1, w_in_f), (pool_w_f, w3_f, slots_w) = _fwd_in(
        x2d, norm_mix, locs[0], order,
        _merge([_gather_comm([4], locs, full_shapes), _gather_comm([1], locs, full_shapes, part=(0, 1, 2)),
                _small_comm(small_w)]))
    conv_w_f, ffn_cw_f = _unpack(_sum8(slots_w, "sum8_weights"), [(3, d), (3, 2 * f)])
    ffn_cw_p = ffn_cw_f.reshape(3, 2, f).transpose(1, 0, 2)
    ffn_cb_p = ffn_conv_b.reshape(2, 1, f)
    (lhs3,), (w3_f,) = _mixer_mid_fwd(z, pool_w_f, pool_scale, conv_w_f, nseq,
                                      _gather_comm([1], locs, full_shapes, part=(1, 2, 2), into={1: w3_f}))
    (lhs3, ypc, x1, h2), (w_up_f,) = _mixer_out(lhs3, z, x2d, w3_f, norm_ffn,
                                                _ring_gather_comm([2], locs, full_shapes))
    (u0,), (w_down_f,) = _ffn_up(h2, w_up_f, f, _gather_comm([3], locs, full_shapes))
    act, ua = _ffn_mid_fwd(u0, ffn_cw_p, ffn_cb_p, nseq)
    dx2, dx2b, loss11, g_norm_final = _ffn_down_loss(act, w_down_f, x1, tgt, norm_final.reshape(1, d))

    gbs, lands, ps, lands2, rs = {}, {}, {}, {}, {}
    tn_up = _tile(2 * f // nsh, 1408, LANES)
    npp = f // tn_up

    def add(arrs, name):
        for a, p in zip(arrs, _add_halves(arrs, gbs, [lands[a] for a in arrs], c_arr, name)):
            ps[a] = p

    def summed(a):
        rs[a] = _sum_chips(a, ps[a], lands2[a], _shard_shape(a, full_shapes[a]), jc_arr, "sum_chips_%d" % a)

    (gbs[3],), _ = _wgrad(act, dx2b, "wgrad_down", tr=tn_up, tn=d)
    (da,), (lands[3],) = _ffn_bwd_da(dx2b, w_down_f, _halves_comm([3], gbs))
    add([3], "add_halves_down")
    (du0, g_ffn_cw_p, g_ffn_cb_p), (lands2[3],) = _ffn_mid_bwd(da, u0, ua, ffn_cw_p, nseq, _chips_comm([3], ps))
    summed(3)
    (gbs[2],), (rs[3],) = _wgrad(h2, du0, "wgrad_up", tr=d, tn=tn_up, b_plane_of=lambda n: (n // npp, n % npp),
                                 out_shards=nsh, comm=_result_comm([3], rs))
    (dx1, rhs3, g_norm_ffn), (lands[2],) = _ffn_bwd_dx1(du0, w_up_f, x1, dx2, norm_ffn, 3, _halves_comm([2], gbs))
    add([2], "add_halves_up")
    (rhs3, dz, dpq), (lands2[2],) = _mixer_bwd(rhs3, z, ypc, w3_f, _chips_comm([2], ps, part=(0, 1, 2)))
    (gbs[1],), (lands2[2],) = _wgrad3(lhs3, rhs3, _chips_comm([2], ps, part=(1, 2, 2), into=lands2))
    summed(2)
    (dz, g_conv_w), (lands[1], rs[2]) = _conv_bwd(dz, dpq, z, conv_w_f, nseq,
                                                  _merge([_halves_comm([1], gbs), _result_comm([2], rs)]))
    add([1], "add_halves_sq3")
    (dz, g_pool_w, g_pool_scale), _ = _pool_bwd_call(dz, dpq, z, pool_w_f, pool_scale, nseq)
    gbs[4] = g_pool_w.astype(BF16)
    (gbs[0],), (lands2[1],) = _wgrad_in(h1, dz, nsh, _chips_comm([1], ps))
    summed(1)
    lands[0], lands[4] = _run_comm(_halves_comm([0, 4], gbs), "exchange_halves_in")
    add([0, 4], "add_halves_in")
    g_ffn_cw = g_ffn_cw_p.transpose(1, 0, 2).reshape(3, 2 * f)
    small_a = _pack([g_pool_scale, g_norm_ffn, g_ffn_cb_p.reshape(1, 2 * f), g_norm_final.reshape(d), g_conv_w,
                     g_ffn_cw, jnp.pad(loss11, ((0, SUBLANES_F32 - 1), (0, LANES - 1)))])
    (grad_x, g_norm_mix), (lands2[0], lands2[4], rs[1], slots_a) = _mixer_bwd_dx(
        dz, w_in_f, x2d, dx1, norm_mix,
        _merge([_chips_comm([0, 4], ps), _result_comm([1], rs), _small_comm(small_a)]))
    summed(0)
    summed(4)
    rs[0], rs[4], slots_b = _run_comm(_merge([_result_comm([0, 4], rs), _small_comm(_pack([g_norm_mix]))]),
                                      "exchange_result_in")
    shapes_a = [(1, d), (1, d), (1, 2 * f), (d,), (3, d), (3, 2 * f), (SUBLANES_F32, LANES)]
    gs_pool_scale, gs_norm_ffn, gs_ffn_cb, gs_norm_final, gs_conv_w, gs_ffn_cw, loss_blk = _unpack(
        _sum8(slots_a, "sum8_grads"), shapes_a)
    (gs_norm_mix,) = _unpack(_sum8(slots_b, "sum8_norm_mix"), [(1, d)])
    gs_conv_w = lax.dynamic_slice(gs_conv_w, (zero, xy * (d // 4)), (3, d // 4))
    gs_ffn_cw = lax.dynamic_slice(gs_ffn_cw, (zero, xy * (f // 2)), (3, f // 2))

    def upd(w, g, m, v, name, g_plane=None):
        shape = w.shape
        rows = 1
        for q in shape[:-1]:
            rows *= q
        g2 = g if g_plane is not None else g.reshape(rows, shape[-1])
        outs = _adamw(w.reshape(rows, shape[-1]), g2, m.reshape(rows, shape[-1]), v.reshape(rows, shape[-1]),
                      name, g_plane)
        return [o.reshape(shape) for o in outs]

    res = {
        "w_in": upd(w_in, rs[0], m_w_in, v_w_in, "adamw_w_in"),
        "pool_w": upd(pool_w, rs[4], m_pool_w, v_pool_w, "adamw_pool_w"),
        "w_pool_proj": upd(w_pool_proj, rs[1], m_w_pool_proj, v_w_pool_proj, "adamw_w_pool_proj", 0),
        "w_conv_out": upd(w_conv_out, rs[1], m_w_conv_out, v_w_conv_out, "adamw_w_conv_out", 1),
        "w_o": upd(w_o, rs[1], m_w_o, v_w_o, "adamw_w_o", 2),
        "w_up": upd(w_up, rs[2], m_w_up, v_w_up, "adamw_w_up"),
        "w_down": upd(w_down, rs[3], m_w_down, v_w_down, "adamw_w_down"),
    }

    small_names = ["norm_mix", "pool_scale", "norm_ffn", "ffn_conv_b", "norm_final", "conv_w", "ffn_conv_w"]
    small_ws = [norm_mix, pool_scale, norm_ffn, ffn_conv_b, norm_final, conv_w, ffn_conv_w]
    small_ms = [m_norm_mix, m_pool_scale, m_norm_ffn, m_ffn_conv_b, m_norm_final, m_conv_w, m_ffn_conv_w]
    small_vs = [v_norm_mix, v_pool_scale, v_norm_ffn, v_ffn_conv_b, v_norm_final, v_conv_w, v_ffn_conv_w]
    small_gs = [gs_norm_mix, gs_pool_scale, gs_norm_ffn, gs_ffn_cb, gs_norm_final, gs_conv_w, gs_ffn_cw]
    _, sd, sm, sv = _adamw(_pack(small_ws), _pack(small_gs), _pack(small_ms), _pack(small_vs), "adamw_small")
    shapes = [w.shape for w in small_ws]
    sd, sm, sv = _unpack(sd, shapes), _unpack(sm, shapes), _unpack(sv, shapes)
    for i, nm in enumerate(small_names):
        res[nm] = [small_gs[i].reshape(shapes[i]), sd[i], sm[i], sv[i]]

    order = ["norm_mix", "w_in", "pool_w", "pool_scale", "w_pool_proj", "conv_w", "w_conv_out", "w_o", "norm_ffn",
             "w_up", "ffn_conv_w", "ffn_conv_b", "w_down", "norm_final"]
    return (loss_blk[0, 0], grad_x.reshape(x.shape), *[res[n][0] for n in order], *[res[n][1] for n in order],
            *[res[n][2] for n in order], *[res[n][3] for n in order])
```

```python
import math

import jax
import jax.numpy as jnp
from jax import lax
from jax.experimental import pallas as pl
from jax.experimental.pallas import tpu as pltpu

F32 = jnp.float32
BF16 = jnp.bfloat16
SDS = jax.ShapeDtypeStruct
MESH = pl.DeviceIdType.MESH

RMS_EPS = 1e-6
POOL_WINDOWS = (2, 4, 8, 16)
N_GROUPS = len(POOL_WINDOWS)
N_SPLITS = 6

ADAM_LR = 0.001
ADAM_B1 = 0.9
ADAM_B2 = 0.999
ADAM_EPS = 1e-08
ADAM_WD = 0.01
ADAM_STEP = 10

LANES = 128
SUBLANES_F32 = 8
SUBLANES_BF16 = 16
VMEM_BYTES = 64 * 1024 * 1024
VMEM_CAP = VMEM_BYTES - 8 * 1024 * 1024
VMEM_FLOOR = 16 * 1024 * 1024

ANY = pl.BlockSpec(memory_space=pl.ANY)


def _tile(dim, pref, align):
    if dim <= pref:
        return dim
    t = (pref // align) * align
    while t >= align:
        if dim % t == 0:
            return t
        t -= align
    return dim


def _nbytes(shape, dtype):
    n = 1
    for s in shape:
        n *= s
    return n * jnp.dtype(dtype).itemsize


def _params(sem, block_bytes, temp_bytes=0, collective_id=None):
    need = 2 * block_bytes + temp_bytes + 4 * 1024 * 1024
    return pltpu.CompilerParams(dimension_semantics=sem, collective_id=collective_id,
                                vmem_limit_bytes=int(min(max(need, VMEM_FLOOR), VMEM_CAP)))


SIBLING = (0, 0, 1)
CHIPS = ((1, 0, 0), (0, 1, 0), (1, 1, 0))
EVERYONE = tuple((a, b, c) for a in range(2) for b in range(2) for c in range(2) if a + b + c)
PEER_SETS = (frozenset([SIBLING]), frozenset(CHIPS), frozenset(CHIPS + (SIBLING,)), frozenset(EVERYONE))
MID_AT = 0.5


def _collective_id(peers):
    return PEER_SETS.index(frozenset(peers))


def _handshake(peers):
    x, y, c = lax.axis_index("x"), lax.axis_index("y"), lax.axis_index("c")
    bar = pltpu.get_barrier_semaphore()
    for fx, fy, fc in sorted(peers):
        dev = (1 - x if fx else x, 1 - y if fy else y, 1 - c if fc else c)
        pl.semaphore_signal(bar, inc=1, device_id=dev, device_id_type=MESH)
    pl.semaphore_wait(bar, len(peers))


class _Comm:
    def __init__(self, ins, out_shapes, sems, start, finish, peers, aliases=None, mid=None):
        self.ins = list(ins)
        self.out_shapes = list(out_shapes)
        self.sems = list(sems)
        self.start = start
        self.finish = finish
        self.mid = mid
        self.peers = frozenset(peers)
        self.aliases = dict(aliases or {})


def _pcall(body, *, name, grid, in_specs, out_specs, out_shape, sem, blocks, temps=0, scratch_shapes=(),
           input_output_aliases=None, comm=None):
    in_specs = list(in_specs)
    out_specs = list(out_specs)
    out_shape = list(out_shape)
    scratch_shapes = list(scratch_shapes)
    aliases = dict(input_output_aliases or {})
    n_in, n_out, n_scr = len(in_specs), len(out_shape), len(scratch_shapes)
    if comm is None:
        call = pl.pallas_call(
            body, name=name, grid=grid, in_specs=in_specs, out_specs=out_specs, out_shape=out_shape,
            scratch_shapes=scratch_shapes, input_output_aliases=aliases,
            compiler_params=_params(sem, blocks, temps))
        return lambda *args: (list(call(*args)), [])

    nci, nco = len(comm.ins), len(comm.out_shapes)
    n_steps = 1
    for g in grid:
        n_steps *= g

    def hosted(*refs):
        ins = refs[:n_in]
        cins = refs[n_in:n_in + nci]
        outs = refs[n_in + nci:n_in + nci + n_out]
        couts = refs[n_in + nci + n_out:n_in + nci + n_out + nco]
        scr = refs[n_in + nci + n_out + nco:n_in + nci + n_out + nco + n_scr]
        csems = refs[n_in + nci + n_out + nco + n_scr:]
        first = None
        last = None
        step = 0
        for q, g in enumerate(grid):
            pid = pl.program_id(q)
            first = (pid == 0) if first is None else first & (pid == 0)
            last = (pid == g - 1) if last is None else last & (pid == g - 1)
            step = step * g + pid

        @pl.when(first)
        def _():
            _handshake(comm.peers)
            comm.start(cins, couts, csems)

        if comm.mid is not None:
            @pl.when(step == int(MID_AT * n_steps))
            def _():
                comm.mid(cins, couts, csems)

        body(*ins, *outs, *scr)

        @pl.when(last)
        def _():
            comm.finish(cins, couts, csems)

    for i, o in comm.aliases.items():
        aliases[n_in + i] = n_out + o
    call = pl.pallas_call(
        hosted, name=name, grid=grid, in_specs=in_specs + [ANY] * nci, out_specs=out_specs + [ANY] * nco,
        out_shape=out_shape + comm.out_shapes, scratch_shapes=scratch_shapes + comm.sems,
        input_output_aliases=aliases,
        compiler_params=_params(("arbitrary",) * len(grid), blocks, temps, _collective_id(comm.peers)))

    def run(*args):
        res = call(*args, *comm.ins)
        return list(res[:n_out]), list(res[n_out:])

    return run


def _run_comm(comm, name):
    def body(*refs):
        nci, nco = len(comm.ins), len(comm.out_shapes)
        cins, couts, csems = refs[:nci], refs[nci:nci + nco], refs[nci + nco:]
        _handshake(comm.peers)
        comm.start(cins, couts, csems)
        if comm.mid is not None:
            comm.mid(cins, couts, csems)
        comm.finish(cins, couts, csems)

    return list(pl.pallas_call(
        body, name=name, in_specs=[ANY] * len(comm.ins), out_specs=[ANY] * len(comm.out_shapes),
        out_shape=comm.out_shapes, scratch_shapes=comm.sems, input_output_aliases=comm.aliases,
        compiler_params=pltpu.CompilerParams(collective_id=_collective_id(comm.peers)),
    )(*comm.ins))


def _dot(a, b):
    return jnp.dot(a, b, preferred_element_type=F32)


def _dot_tb(a, b):
    return lax.dot_general(a, b, (((1,), (1,)), ((), ())), preferred_element_type=F32)


def _dot_ta(a, b):
    return lax.dot_general(a, b, (((0,), (0,)), ((), ())), preferred_element_type=F32)


def _rms_fwd(x):
    inv = lax.rsqrt(jnp.mean(x * x, axis=-1, keepdims=True) + RMS_EPS)
    return x * inv, inv


def _rms_bwd(dy, xhat, inv, g):
    gd = dy * g
    return inv * (gd - xhat * jnp.mean(gd * xhat, axis=-1, keepdims=True))


def _sigmoid(x):
    return 1.0 / (1.0 + jnp.exp(-x))


def _shift_down(x, k, row):
    return jnp.where(row >= k, pltpu.roll(x, k, 0), 0.0)


def _shift_up(x, k, row):
    s = x.shape[0]
    return jnp.where(row < s - k, pltpu.roll(x, s - k, 0), 0.0)


def _pool_fwd(u, win, row):
    s = u
    k = 1
    while k < win:
        s = s + _shift_down(s, k, row)
        k *= 2
    cnt = jnp.minimum(row + 1, win).astype(F32)
    return s / cnt - u


def _pool_bwd(dp, win, row):
    cnt = jnp.minimum(row + 1, win).astype(F32)
    s = dp / cnt
    k = 1
    while k < win:
        s = s + _shift_up(s, k, row)
        k *= 2
    return s - dp


def _acc_over(k, nk, part, acc, o_ref):
    @pl.when(k == 0)
    def _():
        acc[...] = part

    @pl.when(k > 0)
    def _():
        acc[...] += part

    @pl.when(k == nk - 1)
    def _():
        o_ref[...] = acc[...].astype(o_ref.dtype)


def _fwd_in(x, g, w_loc, order, comm):
    t, d = x.shape
    ws = w_loc.shape[1]
    nsh = order.shape[0]
    assert nsh == 4, "the shard walk below is written for the 2 x 2 chips of the mesh"
    tm = _tile(t, 1024, SUBLANES_BF16)
    ni = t // tm
    nci, nco = len(comm.ins), len(comm.out_shapes)
    all_peers = comm.peers | frozenset(CHIPS + (SIBLING,))

    def body(order_ref, x_ref, g_ref, loc_ref, *rest):
        del order_ref
        cins = rest[:nci]
        z_ref, h_ref, full_ref = rest[nci:nci + 3]
        couts = rest[nci + 3:nci + 3 + nco]
        (hs, wbuf, wsem, own_s, own_r, snd_s, snd_r, fwd_s, fwd_r, rly_s, rly_r) = rest[nci + 3 + nco:nci + 14 + nco]
        csems = rest[nci + 14 + nco:]
        j = pl.program_id(0)
        i = pl.program_id(1)
        x_, y_, c_ = _coords()
        own = 2 * x_ + y_
        sib = (x_, y_, 1 - c_)
        peers = _peer_chips(x_, y_)

        def sends():
            return [_remote(_half(0, loc_ref, c_), _piece(0, full_ref, own, c_), snd_s.at[p], snd_r.at[p], (px, py, c_))
                    for p, (px, py) in enumerate(peers[:2])]

        def relays():
            out = []
            for q, (src_p, dst_p) in enumerate(((0, 1), (1, 0))):
                sx, sy = peers[src_p]
                part = _rows_part(0, _piece(0, full_ref, 2 * sx + sy, c_), (q, q + 1, 2))
                out.append(_remote(part, part, rly_s.at[q], rly_r.at[q], (*peers[dst_p], c_)))
            return out

        def owns():
            return [_remote(_half(0, loc_ref, h), _piece(0, full_ref, own, h), own_s.at[h], own_r.at[h], sib)
                    for h in range(2)]

        def forward(p, half):
            px, py = peers[p]
            landed = _piece(0, full_ref, 2 * px + py, half)
            return _remote(landed, landed, fwd_s.at[p], fwd_r.at[p], sib)

        def load(src, slot):
            return pltpu.make_async_copy(src, wbuf.at[slot], wsem.at[slot])

        @pl.when((j == 0) & (i == 0))
        def _():
            _handshake(all_peers)
            for cp in sends() + owns():
                cp.start()
            load(loc_ref, 0).start()

        @pl.when(j == 0)
        def _():
            xh, _ = _rms_fwd(x_ref[...])
            h = (xh * g_ref[...]).astype(BF16)
            hs[pl.ds(pl.multiple_of(i * tm, tm), tm), :] = h
            h_ref[...] = h

        slot = j % 2

        @pl.when(i == 0)
        def _():
            load(loc_ref, slot).wait()

        z_ref[...] = _dot(hs[pl.ds(pl.multiple_of(i * tm, tm), tm), :], wbuf[slot]).astype(BF16)

        def load_shard(p, into):
            px, py = peers[p]
            forward(p, 1 - c_).wait_recv()
            load(full_ref.at[2 * px + py], into).start()

        @pl.when((j == 0) & (i == ni - 1))
        def _():
            for cp in sends():
                cp.wait_recv()
            for cp in relays() + [forward(0, c_), forward(1, c_)]:
                cp.start()
            load_shard(0, 1)
            comm.start(cins, couts, csems)

        @pl.when((j == 1) & (i == 0))
        def _():
            load_shard(1, 0)

        @pl.when((j == 2) & (i == max(ni - 2, 0)))
        def _():
            for cp in relays():
                cp.wait_recv()
            forward(2, c_).start()
            load_shard(2, 1)

        @pl.when((j == nsh - 1) & (i == ni - 1))
        def _():
            for cp in sends() + relays() + [forward(p, c_) for p in range(nsh - 1)]:
                cp.wait_send()
            for cp in owns():
                cp.wait()
            comm.finish(cins, couts, csems)

    last = ni - 1
    blocks = _nbytes((tm, d), F32) + _nbytes((tm, ws), BF16) + _nbytes((tm, d), BF16)
    scratch = _nbytes((t, d), BF16) + 2 * _nbytes((d, ws), BF16)
    res = pl.pallas_call(
        body, name="fwd_in",
        grid_spec=pltpu.PrefetchScalarGridSpec(
            num_scalar_prefetch=1, grid=(nsh, ni),
            in_specs=[pl.BlockSpec((tm, d), lambda j, i, o: (jnp.where(j == 0, i, last), 0)),
                      pl.BlockSpec((1, d), lambda j, i, o: (0, 0)), ANY] + [ANY] * nci,
            out_specs=[pl.BlockSpec((tm, ws), lambda j, i, o: (i, o[j])),
                       pl.BlockSpec((tm, d), lambda j, i, o: (jnp.where(j == 0, i, last), 0)), ANY] + [ANY] * nco,
            scratch_shapes=[pltpu.VMEM((t, d), BF16), pltpu.VMEM((2, d, ws), BF16)]
            + _dma_sems(2, 2, 2, 2, 2, nsh - 1, nsh - 1, 2, 2) + comm.sems),
        out_shape=[SDS((t, nsh * ws), BF16), SDS((t, d), BF16), SDS((nsh, d, ws), BF16)] + comm.out_shapes,
        input_output_aliases={4 + i: 3 + o for i, o in comm.aliases.items()},
        compiler_params=_params(("arbitrary", "arbitrary"), blocks, scratch + 3 * _nbytes((tm, d), F32),
                                _collective_id(all_peers)),
    )(order, x, g, w_loc, *comm.ins)
    return list(res[:3]), list(res[3:])


def _mixer_mid_fwd(z, pool_w, pool_scale, conv_w, nseq, comm=None):
    t = z.shape[0]
    d = pool_scale.shape[1]
    s = t // nseq
    c = d // N_GROUPS

    def body(zp, zb, zc, zv, pw, ps, cw, o):
        j = pl.program_id(1)
        row = lax.broadcasted_iota(jnp.int32, (s, c), 0)
        for gi, win in enumerate(POOL_WINDOWS):
            @pl.when(j == gi)
            def _(win=win):
                pooled = _pool_fwd(zp[...].astype(F32), win, row)
                o[0] = (_dot(pooled.astype(BF16), pw[...]) * ps[...]).astype(BF16)

        cv = zc[...].astype(F32) * zv[...].astype(F32)
        cc = (cw[pl.ds(2, 1), :] * cv + cw[pl.ds(1, 1), :] * _shift_down(cv, 1, row)
              + cw[pl.ds(0, 1), :] * _shift_down(cv, 2, row))
        o[1] = (zb[...].astype(F32) * cc).astype(BF16)

    blocks = 4 * _nbytes((s, c), BF16) + _nbytes((c, c), BF16) + _nbytes((2, s, c), BF16)
    return _pcall(
        body, name="mixer_mid_fwd", grid=(nseq, N_GROUPS),
        in_specs=[pl.BlockSpec((s, c), lambda b, j: (b, j)),
                  pl.BlockSpec((s, c), lambda b, j: (b, N_GROUPS + j)),
                  pl.BlockSpec((s, c), lambda b, j: (b, 2 * N_GROUPS + j)),
                  pl.BlockSpec((s, c), lambda b, j: (b, 3 * N_GROUPS + j)),
                  pl.BlockSpec((None, c, c), lambda b, j: (j, 0, 0)),
                  pl.BlockSpec((1, c), lambda b, j: (0, j)),
                  pl.BlockSpec((3, c), lambda b, j: (0, j))],
        out_specs=[pl.BlockSpec((2, s, c), lambda b, j: (0, b, j))],
        out_shape=[SDS((3, t, d), BF16)],
        sem=("parallel", "parallel"), blocks=blocks, temps=8 * _nbytes((s, c), F32), comm=comm,
    )(z, z, z, z, pool_w, pool_scale, conv_w)


def _mixer_out(lhs3, z, x, w3, g_ffn, comm=None):
    t, d = x.shape
    tm = _tile(t, 256, SUBLANES_BF16)

    def body(pq, zgp, zgc, x_ref, w_ref, g_ref, mrg, ypc, x1o, h2o):
        yp = _dot(pq[0], w_ref[0])
        yc = _dot(pq[1], w_ref[1])
        m = _sigmoid(zgp[...].astype(F32)) * yp + _sigmoid(zgc[...].astype(F32)) * yc
        mb = m.astype(BF16)
        x1 = x_ref[...] + _dot(mb, w_ref[2])
        ypc[0] = yp.astype(BF16)
        ypc[1] = yc.astype(BF16)
        mrg[...] = mb
        x1o[...] = x1
        xh, _ = _rms_fwd(x1)
        h2o[...] = (xh * g_ref[...]).astype(BF16)

    blocks = (_nbytes((2, tm, d), BF16) * 2 + _nbytes((tm, d), BF16) * 4 + _nbytes((tm, d), F32) * 2
              + _nbytes((3, d, d), BF16))
    return _pcall(
        body, name="mixer_out", grid=(t // tm,),
        in_specs=[pl.BlockSpec((2, tm, d), lambda i: (0, i, 0)),
                  pl.BlockSpec((tm, d), lambda i: (i, 4)),
                  pl.BlockSpec((tm, d), lambda i: (i, 5)),
                  pl.BlockSpec((tm, d), lambda i: (i, 0)),
                  pl.BlockSpec((3, d, d), lambda i: (0, 0, 0)),
                  pl.BlockSpec((1, d), lambda i: (0, 0))],
        out_specs=[pl.BlockSpec((None, tm, d), lambda i: (2, i, 0)),
                   pl.BlockSpec((2, tm, d), lambda i: (0, i, 0)),
                   pl.BlockSpec((tm, d), lambda i: (i, 0)),
                   pl.BlockSpec((tm, d), lambda i: (i, 0))],
        out_shape=[SDS(lhs3.shape, BF16), SDS((2, t, d), BF16), SDS((t, d), F32), SDS((t, d), BF16)],
        input_output_aliases={0: 0},
        sem=("parallel",), blocks=blocks, temps=8 * _nbytes((tm, d), F32), comm=comm,
    )(lhs3, z, z, x, w3, g_ffn)


def _ffn_up(h2, w_up, f, comm=None):
    t, d = h2.shape
    _, _, ws = w_up.shape
    tm = _tile(t, 2048, SUBLANES_BF16)
    tn = _tile(ws, 1408, LANES)
    nps = ws // tn
    npp = f // tn

    def body(h_ref, w_ref, o_ref):
        o_ref[...] = _dot(h_ref[...], w_ref[...]).astype(BF16)

    blocks = _nbytes((tm, d), BF16) + _nbytes((d, tn), BF16) + _nbytes((tm, tn), BF16)
    return _pcall(
        body, name="ffn_up", grid=(t // tm, 2 * npp),
        in_specs=[pl.BlockSpec((tm, d), lambda i, j: (i, 0)),
                  pl.BlockSpec((None, d, tn), lambda i, j: (j // nps, 0, j % nps))],
        out_specs=[pl.BlockSpec((None, tm, tn), lambda i, j: (j // npp, i, j % npp))],
        out_shape=[SDS((2, t, f), BF16)],
        sem=("parallel", "parallel"), blocks=blocks, temps=_nbytes((tm, tn), F32), comm=comm,
    )(h2, w_up)


def _conv3_rows(u, u1, u2, w_ref, p):
    return w_ref[p, pl.ds(2, 1), :] * u + w_ref[p, pl.ds(1, 1), :] * u1 + w_ref[p, pl.ds(0, 1), :] * u2


WGRAD_TOKENS = 2048
WGRAD_TOKENS_WIDE = 4096
CHUNK = 64
HALO = SUBLANES_F32


def _up1_up2(u, nxt):
    rows = u.shape[0]
    ext = jnp.concatenate([u, nxt], axis=0)
    n = rows + HALO
    return pltpu.roll(ext, n - 1, 0)[:rows], pltpu.roll(ext, n - 2, 0)[:rows]


def _fold8(x):
    return jnp.sum(x.reshape(x.shape[0] // SUBLANES_F32, SUBLANES_F32, x.shape[1]), axis=0)


def _ffn_mid_fwd(u0, cw, cb, nseq):
    _, t, f = u0.shape
    s = t // nseq
    c = _tile(f, 256, LANES)

    def body(u_ref, w_ref, b_ref, a_ref, uo_ref):
        row = lax.broadcasted_iota(jnp.int32, (s, c), 0)
        act = []
        for p in range(2):
            u = u_ref[p].astype(F32)
            act.append(_conv3_rows(u, _shift_down(u, 1, row), _shift_down(u, 2, row), w_ref, p) + b_ref[p])
            uo_ref[p] = act[p].astype(BF16)
        ug, uv = act
        a_ref[...] = (ug * _sigmoid(ug) * uv).astype(BF16)

    blocks = 2 * _nbytes((2, s, c), BF16) + _nbytes((s, c), BF16)
    outs, _ = _pcall(
        body, name="ffn_mid_fwd", grid=(f // c, nseq),
        in_specs=[pl.BlockSpec((2, s, c), lambda j, b: (0, b, j)),
                  pl.BlockSpec((2, 3, c), lambda j, b: (0, 0, j)),
                  pl.BlockSpec((2, 1, c), lambda j, b: (0, 0, j))],
        out_specs=[pl.BlockSpec((s, c), lambda j, b: (b, j)),
                   pl.BlockSpec((2, s, c), lambda j, b: (0, b, j))],
        out_shape=[SDS((t, f), BF16), SDS((2, t, f), BF16)],
        sem=("parallel", "parallel"), blocks=blocks, temps=8 * _nbytes((s, c), F32),
    )(u0, cw, cb)
    return outs


def _ffn_down_loss(a, w_down, x1, tgt, g_fin):
    t, f = a.shape
    d = x1.shape[1]
    tm = _tile(t, 256, SUBLANES_BF16)
    nsteps = t // tm

    def body(a_ref, w_ref, x1_ref, t_ref, g_ref, dx_ref, dxb_ref, loss_ref, gg_ref, lacc):
        i = pl.program_id(0)

        @pl.when(i == 0)
        def _():
            lacc[...] = jnp.zeros_like(lacc)
            gg_ref[...] = jnp.zeros_like(gg_ref)

        x2 = x1_ref[...] + _dot(a_ref[...], w_ref[...])
        xh, inv = _rms_fwd(x2)
        g = g_ref[...]
        e = xh * g - t_ref[...]
        lacc[...] += jnp.sum(e * e, axis=0, keepdims=True)
        dy = e * (1.0 / d)
        gg_ref[...] += jnp.sum(dy * xh, axis=0, keepdims=True)
        dx2 = _rms_bwd(dy, xh, inv, g)
        dx_ref[...] = dx2
        dxb_ref[...] = dx2.astype(BF16)

        @pl.when(i == nsteps - 1)
        def _():
            loss_ref[...] = jnp.sum(lacc[...], axis=1, keepdims=True) * (0.5 / d)

    blocks = (_nbytes((tm, f), BF16) + _nbytes((f, d), BF16) + 3 * _nbytes((tm, d), F32) + _nbytes((tm, d), BF16))
    outs, _ = _pcall(
        body, name="ffn_down_loss", grid=(nsteps,),
        in_specs=[pl.BlockSpec((tm, f), lambda i: (i, 0)), pl.BlockSpec((f, d), lambda i: (0, 0)),
                  pl.BlockSpec((tm, d), lambda i: (i, 0)), pl.BlockSpec((tm, d), lambda i: (i, 0)),
                  pl.BlockSpec((1, d), lambda i: (0, 0))],
        out_specs=[pl.BlockSpec((tm, d), lambda i: (i, 0)), pl.BlockSpec((tm, d), lambda i: (i, 0)),
                   pl.BlockSpec((1, 1), lambda i: (0, 0)), pl.BlockSpec((1, d), lambda i: (0, 0))],
        out_shape=[SDS((t, d), F32), SDS((t, d), BF16), SDS((1, 1), F32), SDS((1, d), F32)],
        scratch_shapes=[pltpu.VMEM((1, d), F32)],
        sem=("arbitrary",), blocks=blocks, temps=8 * _nbytes((tm, d), F32),
    )(a, w_down, x1, tgt, g_fin)
    return outs


def _ffn_bwd_da(dxb, w_down, comm=None):
    t, d = dxb.shape
    f = w_down.shape[0]
    tm = _tile(t, 512, SUBLANES_BF16)

    def body(x_ref, w_ref, o_ref):
        o_ref[...] = _dot_tb(x_ref[...], w_ref[...]).astype(BF16)

    blocks = _nbytes((tm, d), BF16) + _nbytes((tm, f), BF16)
    return _pcall(
        body, name="ffn_bwd_da", grid=(t // tm,),
        in_specs=[pl.BlockSpec((tm, d), lambda i: (i, 0)),
                  pl.BlockSpec((f, d), lambda i: (0, 0), pipeline_mode=pl.Buffered(1))],
        out_specs=[pl.BlockSpec((tm, f), lambda i: (i, 0))],
        out_shape=[SDS((t, f), BF16)],
        sem=("parallel",), blocks=blocks, temps=_nbytes((f, d), BF16) + _nbytes((tm, f), F32), comm=comm,
    )(dxb, w_down)


def _ffn_mid_bwd(da, u0, ua, cw, nseq, comm=None):
    _, t, f = u0.shape
    s = t // nseq
    c = _tile(f, 128, LANES)
    r = _tile(s, CHUNK, SUBLANES_BF16)
    n = s // r

    def body(da_ref, u_ref, ua_ref, w_ref, du_ref, gw_ref, gb_ref):
        @pl.when(pl.program_id(1) == 0)
        def _():
            gw_ref[...] = jnp.zeros_like(gw_ref)
            gb_ref[...] = jnp.zeros_like(gb_ref)

        def step(i, carry):
            nxt, sums = carry
            rows = pl.ds(pl.multiple_of((n - 1 - i) * r, r), r)
            ug = ua_ref[0, rows, :].astype(F32)
            uv = ua_ref[1, rows, :].astype(F32)
            sg = _sigmoid(ug)
            dacc = da_ref[rows, :].astype(F32)
            dus = (dacc * uv * sg * (1.0 + ug * (1.0 - sg)), dacc * (ug * sg))
            first, new_sums = [], []
            for p in range(2):
                du = dus[p]
                d1, d2 = _up1_up2(du, nxt[p])
                du_ref[p, rows, :] = _conv3_rows(du, d1, d2, w_ref, p).astype(BF16)
                u = u_ref[p, rows, :].astype(F32)
                sb, s0, s1, s2 = sums[p]
                new_sums.append((sb + _fold8(du), s0 + _fold8(d2 * u), s1 + _fold8(d1 * u), s2 + _fold8(du * u)))
                first.append(du[:HALO])
            return tuple(first), tuple(new_sums)

        zero = jnp.zeros((HALO, c), F32)
        _, sums = lax.fori_loop(0, n, step, ((zero, zero), ((zero,) * 4,) * 2))
        for p in range(2):
            sb, s0, s1, s2 = sums[p]
            gb_ref[p] += jnp.sum(sb, axis=0, keepdims=True)
            gw_ref[p, pl.ds(0, 1), :] += jnp.sum(s0, axis=0, keepdims=True)
            gw_ref[p, pl.ds(1, 1), :] += jnp.sum(s1, axis=0, keepdims=True)
            gw_ref[p, pl.ds(2, 1), :] += jnp.sum(s2, axis=0, keepdims=True)

    blocks = _nbytes((s, c), BF16) + 3 * _nbytes((2, s, c), BF16)
    return _pcall(
        body, name="ffn_mid_bwd", grid=(f // c, nseq),
        in_specs=[pl.BlockSpec((s, c), lambda j, b: (b, j)),
                  pl.BlockSpec((2, s, c), lambda j, b: (0, b, j)),
                  pl.BlockSpec((2, s, c), lambda j, b: (0, b, j)),
                  pl.BlockSpec((2, 3, c), lambda j, b: (0, 0, j))],
        out_specs=[pl.BlockSpec((2, s, c), lambda j, b: (0, b, j)),
                   pl.BlockSpec((2, 3, c), lambda j, b: (0, 0, j)),
                   pl.BlockSpec((2, 1, c), lambda j, b: (0, 0, j))],
        out_shape=[SDS((2, t, f), BF16), SDS((2, 3, f), F32), SDS((2, 1, f), F32)],
        sem=("parallel", "arbitrary"), blocks=blocks, temps=4 * 1024 * 1024, comm=comm,
    )(da, u0, ua, cw)


def _wgrad(a, b, name, *, tr, tn, b_plane_of=None, out_shards=None, comm=None):
    t, m = a.shape
    n_total = b.shape[-1] * (b.shape[0] if b.ndim == 3 else 1)
    tk = _tile(t, WGRAD_TOKENS_WIDE if n_total > tn and m == tr else WGRAD_TOKENS, SUBLANES_BF16)
    nk = t // tk
    once = pl.Buffered(1) if nk == 1 else None

    def body(a_ref, b_ref, o_ref, *acc):
        part = _dot_ta(a_ref[...], b_ref[...])
        if nk == 1:
            o_ref[...] = part.astype(BF16)
        else:
            _acc_over(pl.program_id(2), nk, part, acc[0], o_ref)

    if b.ndim == 3:
        b_spec = pl.BlockSpec((None, tk, tn), lambda r, n, k: (b_plane_of(n)[0], k, b_plane_of(n)[1]))
    else:
        b_spec = pl.BlockSpec((tk, tn), lambda r, n, k: (k, n), pipeline_mode=once if n_total == tn else None)
    if out_shards is None:
        o_spec = pl.BlockSpec((tr, tn), lambda r, n, k: (r, n))
        o_shape = SDS((m, n_total), BF16)
    else:
        nps = n_total // out_shards // tn
        o_spec = pl.BlockSpec((None, tr, tn), lambda r, n, k: (n // nps, r, n % nps))
        o_shape = SDS((out_shards, m, n_total // out_shards), BF16)
    blocks = _nbytes((tk, tr), BF16) + _nbytes((tk, tn), BF16) + _nbytes((tr, tn), BF16)
    return _pcall(
        body, name=name, grid=(m // tr, n_total // tn, nk),
        in_specs=[pl.BlockSpec((tk, tr), lambda r, n, k: (k, r), pipeline_mode=once if m == tr else None), b_spec],
        out_specs=[o_spec], out_shape=[o_shape],
        scratch_shapes=[] if nk == 1 else [pltpu.VMEM((tr, tn), F32)],
        sem=("parallel", "parallel", "arbitrary"), blocks=blocks, temps=2 * _nbytes((tr, tn), F32), comm=comm,
    )(a, b)


def _wgrad3(lhs3, rhs3, comm=None):
    nw, t, d = lhs3.shape
    tk = _tile(t, WGRAD_TOKENS, SUBLANES_BF16)
    nk = t // tk

    def body(a_ref, b_ref, o_ref, *acc):
        part = _dot_ta(a_ref[...], b_ref[...])
        if nk == 1:
            o_ref[...] = part.astype(BF16)
        else:
            _acc_over(pl.program_id(1), nk, part, acc[0], o_ref)

    blocks = 2 * _nbytes((tk, d), BF16) + _nbytes((d, d), BF16)
    return _pcall(
        body, name="wgrad_sq3", grid=(nw, nk),
        in_specs=[pl.BlockSpec((None, tk, d), lambda w, k: (w, k, 0)),
                  pl.BlockSpec((None, tk, d), lambda w, k: (w, k, 0))],
        out_specs=[pl.BlockSpec((None, d, d), lambda w, k: (w, 0, 0))],
        out_shape=[SDS((nw, d, d), BF16)],
        scratch_shapes=[] if nk == 1 else [pltpu.VMEM((d, d), F32)],
        sem=("parallel", "arbitrary"), blocks=blocks, temps=2 * _nbytes((d, d), F32), comm=comm,
    )(lhs3, rhs3)


def _ffn_bwd_dx1(du0, w_up, x1, dx2, g_ffn, n_planes_out, comm=None):
    _, t, f = du0.shape
    d = x1.shape[1]
    nsh, _, ws = w_up.shape
    tm = _tile(t, 256, SUBLANES_BF16)
    spp = f // ws

    def body(du_ref, w_ref, x1_ref, dx2_ref, g_ref, dx1_ref, dxb_ref, gg_ref):
        @pl.when(pl.program_id(0) == 0)
        def _():
            gg_ref[...] = jnp.zeros_like(gg_ref)

        dh = None
        for k in range(nsh):
            part = _dot_tb(du_ref[k // spp, :, (k % spp) * ws:(k % spp + 1) * ws], w_ref[k])
            dh = part if dh is None else dh + part
        xh, inv = _rms_fwd(x1_ref[...])
        gg_ref[...] += jnp.sum(dh * xh, axis=0, keepdims=True)
        dx1 = dx2_ref[...] + _rms_bwd(dh, xh, inv, g_ref[...])
        dx1_ref[...] = dx1
        dxb_ref[...] = dx1.astype(BF16)

    blocks = _nbytes((2, tm, f), BF16) + 3 * _nbytes((tm, d), F32) + _nbytes((tm, d), BF16)
    return _pcall(
        body, name="ffn_bwd_dx1", grid=(t // tm,),
        in_specs=[pl.BlockSpec((2, tm, f), lambda i: (0, i, 0)),
                  pl.BlockSpec((nsh, d, ws), lambda i: (0, 0, 0), pipeline_mode=pl.Buffered(1)),
                  pl.BlockSpec((tm, d), lambda i: (i, 0)),
                  pl.BlockSpec((tm, d), lambda i: (i, 0)),
                  pl.BlockSpec((1, d), lambda i: (0, 0))],
        out_specs=[pl.BlockSpec((tm, d), lambda i: (i, 0)),
                   pl.BlockSpec((None, tm, d), lambda i: (n_planes_out - 1, i, 0)),
                   pl.BlockSpec((1, d), lambda i: (0, 0))],
        out_shape=[SDS((t, d), F32), SDS((n_planes_out, t, d), BF16), SDS((1, d), F32)],
        sem=("arbitrary",), blocks=blocks, temps=_nbytes(w_up.shape, BF16) + 8 * _nbytes((tm, d), F32), comm=comm,
    )(du0, w_up, x1, dx2, g_ffn)


def _mixer_bwd(rhs3, z, ypc, w3, comm=None):
    _, t, d = rhs3.shape
    tm = _tile(t, 512, SUBLANES_BF16)

    def body(dx_ref, zgp, zgc, ypc_ref, w_ref, dyo, dzo, dpq):
        dm = _dot_tb(dx_ref[...], w_ref[2])
        sp = _sigmoid(zgp[...].astype(F32))
        sc = _sigmoid(zgc[...].astype(F32))
        dyp = (dm * sp).astype(BF16)
        dyc = (dm * sc).astype(BF16)
        dzo[0] = (dm * ypc_ref[0].astype(F32) * sp * (1.0 - sp)).astype(BF16)
        dzo[1] = (dm * ypc_ref[1].astype(F32) * sc * (1.0 - sc)).astype(BF16)
        dyo[0] = dyp
        dyo[1] = dyc
        dpq[0] = _dot_tb(dyp, w_ref[0]).astype(BF16)
        dpq[1] = _dot_tb(dyc, w_ref[1]).astype(BF16)

    blocks = _nbytes((tm, d), BF16) * 3 + _nbytes((2, tm, d), BF16) * 4 + _nbytes((3, d, d), BF16)
    return _pcall(
        body, name="mixer_bwd", grid=(t // tm,),
        in_specs=[pl.BlockSpec((None, tm, d), lambda i: (2, i, 0)),
                  pl.BlockSpec((tm, d), lambda i: (i, 4)),
                  pl.BlockSpec((tm, d), lambda i: (i, 5)),
                  pl.BlockSpec((2, tm, d), lambda i: (0, i, 0)),
                  pl.BlockSpec((3, d, d), lambda i: (0, 0, 0))],
        out_specs=[pl.BlockSpec((2, tm, d), lambda i: (0, i, 0)),
                   pl.BlockSpec((2, tm, d), lambda i: (2, i, 0)),
                   pl.BlockSpec((2, tm, d), lambda i: (0, i, 0))],
        out_shape=[SDS(rhs3.shape, BF16), SDS((N_SPLITS, t, d), BF16), SDS((2, t, d), BF16)],
        input_output_aliases={0: 0},
        sem=("parallel",), blocks=blocks, temps=8 * _nbytes((tm, d), F32), comm=comm,
    )(rhs3, z, z, ypc, w3)


def _conv_bwd(dz, dpq, z, conv_w, nseq, comm=None):
    _, t, d = dz.shape
    s = t // nseq
    c = _tile(d, 256, LANES)
    nb = d // c

    def body(dz_in, dq_ref, zb, zc, zv, cw, dzo, gw_ref):
        del dz_in

        @pl.when(pl.program_id(1) == 0)
        def _():
            gw_ref[...] = jnp.zeros_like(gw_ref)

        row = lax.broadcasted_iota(jnp.int32, (s, c), 0)
        b = zb[...].astype(F32)
        cm = zc[...].astype(F32)
        v = zv[...].astype(F32)
        cv = cm * v
        cv1 = _shift_down(cv, 1, row)
        cv2 = _shift_down(cv, 2, row)
        w0, w1, w2 = cw[pl.ds(0, 1), :], cw[pl.ds(1, 1), :], cw[pl.ds(2, 1), :]
        cc = w2 * cv + w1 * cv1 + w0 * cv2
        dq = dq_ref[...].astype(F32)
        dzo[0] = (dq * cc).astype(BF16)
        dcc = dq * b
        gw_ref[pl.ds(0, 1), :] += jnp.sum(dcc * cv2, axis=0, keepdims=True)
        gw_ref[pl.ds(1, 1), :] += jnp.sum(dcc * cv1, axis=0, keepdims=True)
        gw_ref[pl.ds(2, 1), :] += jnp.sum(dcc * cv, axis=0, keepdims=True)
        dcv = w2 * dcc + w1 * _shift_up(dcc, 1, row) + w0 * _shift_up(dcc, 2, row)
        dzo[1] = (dcv * v).astype(BF16)
        dzo[2] = (dcv * cm).astype(BF16)

    blocks = 4 * _nbytes((s, c), BF16) + _nbytes((3, s, c), BF16)
    return _pcall(
        body, name="conv_bwd", grid=(nb, nseq),
        in_specs=[ANY,
                  pl.BlockSpec((None, s, c), lambda j, b: (1, b, j)),
                  pl.BlockSpec((s, c), lambda j, b: (b, nb + j)),
                  pl.BlockSpec((s, c), lambda j, b: (b, 2 * nb + j)),
                  pl.BlockSpec((s, c), lambda j, b: (b, 3 * nb + j)),
                  pl.BlockSpec((3, c), lambda j, b: (0, j))],
        out_specs=[pl.BlockSpec((3, s, c), lambda j, b: (0, b, j)),
                   pl.BlockSpec((3, c), lambda j, b: (0, j))],
        out_shape=[SDS(dz.shape, BF16), SDS((3, d), F32)],
        input_output_aliases={0: 0},
        sem=("parallel", "arbitrary"), blocks=blocks, temps=16 * _nbytes((s, c), F32), comm=comm,
    )(dz, dpq, z, z, z, conv_w)


def _pool_bwd_call(dz, dpq, z, pool_w, pool_scale, nseq, comm=None):
    _, t, d = dz.shape
    s = t // nseq
    c = d // N_GROUPS

    def body(dz_in, dp_ref, zp, pw, ps, dzo, gpw_ref, gps_ref):
        del dz_in
        j = pl.program_id(0)

        @pl.when(pl.program_id(1) == 0)
        def _():
            gpw_ref[...] = jnp.zeros_like(gpw_ref)
            gps_ref[...] = jnp.zeros_like(gps_ref)

        row = lax.broadcasted_iota(jnp.int32, (s, c), 0)
        for gi, win in enumerate(POOL_WINDOWS):
            @pl.when(j == gi)
            def _(win=win):
                pb = _pool_fwd(zp[...].astype(F32), win, row).astype(BF16)
                plin = _dot(pb, pw[...])
                dps = dp_ref[...].astype(F32)
                gps_ref[...] += jnp.sum(dps * plin, axis=0, keepdims=True)
                dplb = (dps * ps[...]).astype(BF16)
                gpw_ref[...] += _dot_ta(pb, dplb)
                dzo[...] = _pool_bwd(_dot_tb(dplb, pw[...]), win, row).astype(BF16)

    blocks = 3 * _nbytes((s, c), BF16) + _nbytes((c, c), BF16) + _nbytes((c, c), F32)
    return _pcall(
        body, name="pool_bwd", grid=(N_GROUPS, nseq),
        in_specs=[ANY,
                  pl.BlockSpec((None, s, c), lambda j, b: (0, b, j)),
                  pl.BlockSpec((s, c), lambda j, b: (b, j)),
                  pl.BlockSpec((None, c, c), lambda j, b: (j, 0, 0)),
                  pl.BlockSpec((1, c), lambda j, b: (0, j))],
        out_specs=[pl.BlockSpec((None, s, c), lambda j, b: (3, b, j)),
                   pl.BlockSpec((None, c, c), lambda j, b: (j, 0, 0)),
                   pl.BlockSpec((1, c), lambda j, b: (0, j))],
        out_shape=[SDS(dz.shape, BF16), SDS((N_GROUPS, c, c), F32), SDS((1, d), F32)],
        input_output_aliases={0: 0},
        sem=("parallel", "arbitrary"), blocks=blocks, temps=10 * _nbytes((s, c), F32), comm=comm,
    )(dz, dpq, z, pool_w, pool_scale)


def _dz_plane(zb):
    return jnp.where(zb < 4, (zb + 3) % 4, zb)


def _wgrad_in(h1, dz, nsh, comm=None):
    t, d = h1.shape
    ws = N_SPLITS * d // nsh
    kb = _tile(math.gcd(d, ws), 512, LANES)
    npl = d // kb
    nps = ws // kb
    tk = _tile(t, WGRAD_TOKENS_WIDE, SUBLANES_BF16)
    nk = t // tk

    def body(a_ref, b_ref, o_ref, *acc):
        part = _dot_ta(a_ref[...], b_ref[...])
        if nk == 1:
            o_ref[...] = part.astype(BF16)
        else:
            _acc_over(pl.program_id(1), nk, part, acc[0], o_ref)

    blocks = _nbytes((tk, d), BF16) + _nbytes((tk, kb), BF16) + _nbytes((d, kb), BF16)
    return _pcall(
        body, name="wgrad_in", grid=(N_SPLITS * npl, nk),
        in_specs=[pl.BlockSpec((tk, d), lambda cb, k: (k, 0), pipeline_mode=pl.Buffered(1) if nk == 1 else None),
                  pl.BlockSpec((None, tk, kb), lambda cb, k: (_dz_plane(cb // npl), k, cb % npl))],
        out_specs=[pl.BlockSpec((None, d, kb), lambda cb, k: (cb // nps, 0, cb % nps))],
        out_shape=[SDS((nsh, d, ws), BF16)],
        scratch_shapes=[] if nk == 1 else [pltpu.VMEM((d, kb), F32)],
        sem=("parallel", "arbitrary"), blocks=blocks, temps=2 * _nbytes((d, kb), F32), comm=comm,
    )(h1, dz)


def _mixer_bwd_dx(dz, w_in, x, dx1, g_mix, comm=None):
    npln, t, d = dz.shape
    nsh, _, ws = w_in.shape
    tm = _tile(t, 256, SUBLANES_BF16)
    kb = _tile(math.gcd(d, ws), 512, LANES)
    npl = d // kb
    nps = ws // kb

    def body(dz_ref, w_ref, x_ref, dx1_ref, g_ref, dx_ref, gg_ref):
        @pl.when(pl.program_id(0) == 0)
        def _():
            gg_ref[...] = jnp.zeros_like(gg_ref)

        dh = None
        for cb in range(npln * npl):
            zb = cb // npl
            plane = (zb + 3) % 4 if zb < 4 else zb
            part = _dot_tb(dz_ref[plane, :, (cb % npl) * kb:(cb % npl + 1) * kb],
                           w_ref[cb // nps, :, (cb % nps) * kb:(cb % nps + 1) * kb])
            dh = part if dh is None else dh + part
        xh, inv = _rms_fwd(x_ref[...])
        gg_ref[...] += jnp.sum(dh * xh, axis=0, keepdims=True)
        dx_ref[...] = dx1_ref[...] + _rms_bwd(dh, xh, inv, g_ref[...])

    blocks = _nbytes((npln, tm, d), BF16) + 3 * _nbytes((tm, d), F32)
    return _pcall(
        body, name="mixer_bwd_dx", grid=(t // tm,),
        in_specs=[pl.BlockSpec((npln, tm, d), lambda i: (0, i, 0)),
                  pl.BlockSpec((nsh, d, ws), lambda i: (0, 0, 0), pipeline_mode=pl.Buffered(1)),
                  pl.BlockSpec((tm, d), lambda i: (i, 0)),
                  pl.BlockSpec((tm, d), lambda i: (i, 0)),
                  pl.BlockSpec((1, d), lambda i: (0, 0))],
        out_specs=[pl.BlockSpec((tm, d), lambda i: (i, 0)),
                   pl.BlockSpec((1, d), lambda i: (0, 0))],
        out_shape=[SDS((t, d), F32), SDS((1, d), F32)],
        sem=("arbitrary",), blocks=blocks, temps=_nbytes(w_in.shape, BF16) + 8 * _nbytes((tm, d), F32), comm=comm,
    )(dz, w_in, x, dx1, g_mix)


N_BIG = 5
SHARD_MAJOR = (0, 2)
ROWS_DIM1 = (1, 4)


def _ds(start, size, align):
    if isinstance(start, int):
        return pl.ds(start, size)
    return pl.ds(pl.multiple_of(start, align), size)


def _piece(a, ref, k, h):
    if a in SHARD_MAJOR:
        r = ref.shape[1] // 2
        return ref.at[k, _ds(h * r, r, SUBLANES_BF16), :]
    if a in ROWS_DIM1:
        r = ref.shape[1] // 8
        return ref.at[:, _ds((2 * k + h) * r, r, SUBLANES_BF16), :]
    r = ref.shape[0] // 8
    return ref.at[_ds((2 * k + h) * r, r, SUBLANES_BF16), :]


def _half(a, ref, h):
    if a in ROWS_DIM1:
        r = ref.shape[1] // 2
        return ref.at[:, _ds(h * r, r, SUBLANES_BF16), :]
    r = ref.shape[0] // 2
    return ref.at[_ds(h * r, r, SUBLANES_BF16), :]


def _piece_shape(a, full_shape):
    if a in SHARD_MAJOR:
        return (full_shape[1] // 2, full_shape[2])
    if a in ROWS_DIM1:
        return (full_shape[0], full_shape[1] // 8, full_shape[2])
    return (full_shape[0] // 8, full_shape[1])


def _shard_shape(a, full_shape):
    if a in SHARD_MAJOR:
        return (full_shape[1], full_shape[2])
    if a in ROWS_DIM1:
        return (full_shape[0], full_shape[1] // 4, full_shape[2])
    return (full_shape[0] // 4, full_shape[1])


def _rows_axis(a):
    return 1 if a in ROWS_DIM1 else 0


def _piece_block(a, full_shape):
    ps = _piece_shape(a, full_shape)
    if a in SHARD_MAJOR:
        return (None,) + ps, lambda k, c: (k, c, 0)
    if a in ROWS_DIM1:
        return ps, lambda k, c: (0, 2 * k + c, 0)
    return ps, lambda k, c: (2 * k + c, 0)


def _coords():
    return lax.axis_index("x"), lax.axis_index("y"), lax.axis_index("c")


def _peer_chips(x, y):
    return [(1 - x, y), (x, 1 - y), (1 - x, 1 - y)]


def _remote(src, dst, ssem, rsem, dev):
    return pltpu.make_async_remote_copy(src_ref=src, dst_ref=dst, send_sem=ssem, recv_sem=rsem,
                                        device_id=dev, device_id_type=MESH)


def _dma_sems(*counts):
    return [pltpu.SemaphoreType.DMA((n,)) for n in counts]


def _symmetric(ins, out_shapes, sems, copies, peers, aliases=None):
    def start(cins, couts, csems):
        for cp in copies(cins, couts, csems):
            cp.start()

    def finish(cins, couts, csems):
        for cp in copies(cins, couts, csems):
            cp.wait()

    return _Comm(ins, out_shapes, sems, start, finish, peers, aliases)


def _rows_part(a, ref, part):
    if part is None:
        return ref
    p, q, n = part
    ax = _rows_axis(a)
    r = ref.shape[ax] // n
    return ref.at[tuple(pl.ds(p * r, (q - p) * r) if d == ax else slice(None) for d in range(len(ref.shape)))]


def _merge(comms):
    ins, outs, sems, aliases, spans = [], [], [], {}, []
    for cm in comms:
        spans.append((len(ins), len(outs), len(sems)))
        for i, o in cm.aliases.items():
            aliases[len(ins) + i] = len(outs) + o
        ins += cm.ins
        outs += cm.out_shapes
        sems += cm.sems

    def each(fn_name):
        def run(cins, couts, csems):
            for cm, (i0, o0, s0) in zip(comms, spans):
                fn = getattr(cm, fn_name)
                if fn is not None:
                    fn(cins[i0:i0 + len(cm.ins)], couts[o0:o0 + len(cm.out_shapes)], csems[s0:s0 + len(cm.sems)])
        return run

    return _Comm(ins, outs, sems, each("start"), each("finish"), frozenset().union(*[cm.peers for cm in comms]),
                 aliases, mid=each("mid") if any(cm.mid is not None for cm in comms) else None)


def _gather_comm(arrs, locs, full_shapes, part=None, into=None):
    n = len(arrs)

    def own(cins, couts, csems):
        x, y, c = _coords()
        j = 2 * x + y
        return [_remote(_rows_part(a, _half(a, cins[q], h), part), _rows_part(a, _piece(a, couts[q], j, h), part),
                        csems[0].at[2 * q + h], csems[1].at[2 * q + h], (x, y, 1 - c))
                for q, a in enumerate(arrs) for h in range(2)]

    def sends(cins, couts, csems):
        x, y, c = _coords()
        j = 2 * x + y
        return [_remote(_rows_part(a, _half(a, cins[q], c), part), _rows_part(a, _piece(a, couts[q], j, c), part),
                        csems[2].at[3 * q + i], csems[3].at[3 * q + i], (px, py, c))
                for q, a in enumerate(arrs) for i, (px, py) in enumerate(_peer_chips(x, y))]

    def forwards(couts, csems, half_of):
        x, y, c = _coords()
        out = []
        for q, a in enumerate(arrs):
            for i, (px, py) in enumerate(_peer_chips(x, y)):
                landed = _rows_part(a, _piece(a, couts[q], 2 * px + py, half_of(c)), part)
                out.append(_remote(landed, landed, csems[4].at[3 * q + i], csems[5].at[3 * q + i], (x, y, 1 - c)))
        return out

    def start(cins, couts, csems):
        for cp in sends(cins, couts, csems) + own(cins, couts, csems):
            cp.start()

    def finish(cins, couts, csems):
        fw = forwards(couts, csems, lambda c: c)
        for cp, f in zip(sends(cins, couts, csems), fw):
            cp.wait_recv()
            f.start()
        for f in forwards(couts, csems, lambda c: 1 - c):
            f.wait_recv()
        for cp in sends(cins, couts, csems) + fw:
            cp.wait_send()
        for cp in own(cins, couts, csems):
            cp.wait()

    ins = [locs[a] for a in arrs] + ([into[a] for a in arrs] if into else [])
    return _Comm(ins, [SDS(full_shapes[a], BF16) for a in arrs],
                 _dma_sems(2 * n, 2 * n, 3 * n, 3 * n, 3 * n, 3 * n), start, finish, CHIPS + (SIBLING,),
                 aliases={n + q: q for q in range(n)} if into else None)


def _ring_gather_comm(arrs, locs, full_shapes):
    n = len(arrs)

    def own(cins, couts, csems):
        x, y, c = _coords()
        j = 2 * x + y
        return [_remote(_half(a, cins[q], h), _piece(a, couts[q], j, h), csems[0].at[2 * q + h],
                        csems[1].at[2 * q + h], (x, y, 1 - c)) for q, a in enumerate(arrs) for h in range(2)]

    def sends(cins, couts, csems):
        x, y, c = _coords()
        j = 2 * x + y
        return [_remote(_half(a, cins[q], c), _piece(a, couts[q], j, c), csems[2].at[2 * q + i],
                        csems[3].at[2 * q + i], (px, py, c))
                for q, a in enumerate(arrs) for i, (px, py) in enumerate(_peer_chips(x, y)[:2])]

    def relays(couts, csems):
        x, y, c = _coords()
        peers = _peer_chips(x, y)
        out = []
        for q, a in enumerate(arrs):
            for r, (src_p, dst_p) in enumerate(((0, 1), (1, 0))):
                sx, sy = peers[src_p]
                rows = _rows_part(a, _piece(a, couts[q], 2 * sx + sy, c), (r, r + 1, 2))
                out.append(_remote(rows, rows, csems[6].at[2 * q + r], csems[7].at[2 * q + r], (*peers[dst_p], c)))
        return out

    def forwards(couts, csems, half_of, which):
        x, y, c = _coords()
        out = []
        for q, a in enumerate(arrs):
            for i in which:
                px, py = _peer_chips(x, y)[i]
                landed = _piece(a, couts[q], 2 * px + py, half_of(c))
                out.append(_remote(landed, landed, csems[4].at[3 * q + i], csems[5].at[3 * q + i], (x, y, 1 - c)))
        return out

    def start(cins, couts, csems):
        for cp in sends(cins, couts, csems) + own(cins, couts, csems):
            cp.start()

    def mid(cins, couts, csems):
        for cp in sends(cins, couts, csems):
            cp.wait_recv()
        for cp in relays(couts, csems) + forwards(couts, csems, lambda c: c, (0, 1)):
            cp.start()

    def finish(cins, couts, csems):
        for cp in relays(couts, csems):
            cp.wait_recv()
        fw_diag = forwards(couts, csems, lambda c: c, (2,))
        for f in fw_diag:
            f.start()
        for f in forwards(couts, csems, lambda c: 1 - c, (0, 1, 2)):
            f.wait_recv()
        for cp in (sends(cins, couts, csems) + relays(couts, csems)
                   + forwards(couts, csems, lambda c: c, (0, 1)) + fw_diag):
            cp.wait_send()
        for cp in own(cins, couts, csems):
            cp.wait()

    return _Comm([locs[a] for a in arrs], [SDS(full_shapes[a], BF16) for a in arrs],
                 _dma_sems(2 * n, 2 * n, 2 * n, 2 * n, 3 * n, 3 * n, 2 * n, 2 * n), start, finish,
                 CHIPS + (SIBLING,), mid=mid)


def _halves_comm(arrs, gbs):
    n = len(arrs)

    def copies(cins, couts, csems):
        x, y, c = _coords()
        return [_remote(_piece(a, cins[q], k, 1 - c), couts[q].at[k], csems[0].at[4 * q + k], csems[1].at[4 * q + k],
                        (x, y, 1 - c)) for q, a in enumerate(arrs) for k in range(4)]

    return _symmetric([gbs[a] for a in arrs], [SDS((4,) + _piece_shape(a, gbs[a].shape), BF16) for a in arrs],
                      _dma_sems(4 * n, 4 * n), copies, [SIBLING])


def _chips_comm(arrs, ps, part=None, into=None):
    n = len(arrs)

    def copies(cins, couts, csems):
        x, y, c = _coords()
        return [_remote(_rows_part(a, cins[q].at[2 * px + py], part), _rows_part(a, couts[q].at[i], part),
                        csems[0].at[3 * q + i], csems[1].at[3 * q + i], (px, py, c))
                for q, a in enumerate(arrs) for i, (px, py) in enumerate(_peer_chips(x, y))]

    ins = [ps[a] for a in arrs] + ([into[a] for a in arrs] if into else [])
    return _symmetric(ins, [SDS((3,) + ps[a].shape[1:], BF16) for a in arrs], _dma_sems(3 * n, 3 * n), copies, CHIPS,
                      aliases={n + q: q for q in range(n)} if into else None)


def _result_comm(arrs, gs):
    n = len(arrs)

    def copies(cins, couts, csems):
        x, y, c = _coords()
        return [_remote(_half(a, cins[q], c), _half(a, couts[q], c), csems[0].at[q], csems[1].at[q], (x, y, 1 - c))
                for q, a in enumerate(arrs)]

    return _symmetric([gs[a] for a in arrs], [SDS(gs[a].shape, F32) for a in arrs], _dma_sems(n, n), copies,
                      [SIBLING], aliases={q: q for q in range(n)})


def _add_halves(arrs, gbs, lands, c_arr, name):
    n = len(arrs)

    def body(c_ref, *refs):
        del c_ref
        for q in range(n):
            refs[2 * n + q][...] = (refs[q][...].astype(F32) + refs[n + q][...].astype(F32)).astype(BF16)

    g_specs, l_specs, o_specs, blocks = [], [], [], 0
    for a in arrs:
        bs, imap = _piece_block(a, gbs[a].shape)
        ps = _piece_shape(a, gbs[a].shape)
        g_specs.append(pl.BlockSpec(bs, lambda k, c_ref, imap=imap: imap(k, c_ref[0])))
        nd = len(ps)
        l_specs.append(pl.BlockSpec((None,) + ps, lambda k, c_ref, nd=nd: (k,) + (0,) * nd))
        o_specs.append(pl.BlockSpec((None,) + ps, lambda k, c_ref, nd=nd: (k,) + (0,) * nd))
        blocks += 3 * _nbytes(ps, BF16)
    return list(pl.pallas_call(
        body, name=name,
        grid_spec=pltpu.PrefetchScalarGridSpec(
            num_scalar_prefetch=1, grid=(4,), in_specs=g_specs + l_specs, out_specs=o_specs),
        out_shape=[SDS((4,) + _piece_shape(a, gbs[a].shape), BF16) for a in arrs],
        compiler_params=_params(("parallel",), blocks, blocks),
    )(c_arr, *[gbs[a] for a in arrs], *lands))


def _sum_chips(a, p, land, shard_shape, jc_arr, name):
    ps = land.shape[1:]
    ax = _rows_axis(a)
    rows = ps[ax]
    nsub = 2 if rows % (2 * SUBLANES_BF16) == 0 else 1
    bs = tuple(r // nsub if q == ax else r for q, r in enumerate(ps))
    nd = len(ps)

    def at_rows(v):
        return tuple(v if q == ax else 0 for q in range(nd))

    def body(jc_ref, p_ref, l_ref, o_ref):
        del jc_ref
        acc = p_ref[...].astype(F32) + l_ref[0].astype(F32)
        acc = acc + l_ref[1].astype(F32)
        o_ref[...] = acc + l_ref[2].astype(F32)

    blocks = 4 * _nbytes(bs, BF16) + _nbytes(bs, F32)
    return pl.pallas_call(
        body, name=name,
        grid_spec=pltpu.PrefetchScalarGridSpec(
            num_scalar_prefetch=1, grid=(nsub,),
            in_specs=[pl.BlockSpec((None,) + bs, lambda s, jc: (jc[0],) + at_rows(s)),
                      pl.BlockSpec((3,) + bs, lambda s, jc: (0,) + at_rows(s))],
            out_specs=pl.BlockSpec(bs, lambda s, jc: at_rows(jc[1] * nsub + s))),
        out_shape=SDS(shard_shape, F32),
        compiler_params=_params(("parallel",), blocks, 2 * _nbytes(bs, F32)),
    )(jc_arr, p, land)


def _small_comm(v):
    rows = v.shape[0]

    def copies(cins, couts, csems):
        x, y, c = _coords()
        me = 4 * x + 2 * y + c
        out = [pltpu.make_async_copy(cins[0], couts[0].at[me], csems[0].at[0])]
        for dlt in range(1, 8):
            px = 1 - x if (dlt >> 2) & 1 else x
            py = 1 - y if (dlt >> 1) & 1 else y
            pc = 1 - c if dlt & 1 else c
            out.append(_remote(cins[0], couts[0].at[me], csems[1].at[dlt - 1], csems[2].at[dlt - 1], (px, py, pc)))
        return out

    return _symmetric([v], [SDS((8, rows, LANES), F32)], _dma_sems(1, 7, 7), copies, EVERYONE)


def _sum8(slots, name):
    def body(s_ref, o_ref):
        acc = s_ref[0]
        for i in range(1, 8):
            acc = acc + s_ref[i]
        o_ref[...] = acc

    return pl.pallas_call(
        body, name=name,
        in_specs=[pl.BlockSpec(memory_space=pltpu.VMEM)], out_specs=pl.BlockSpec(memory_space=pltpu.VMEM),
        out_shape=SDS(slots.shape[1:], F32),
    )(slots)


def _adamw(w, g, m, v, name, g_plane=None):
    rows, cols = w.shape
    tr = _tile(rows, max(SUBLANES_F32, (256 * 1024 // cols) // SUBLANES_F32 * SUBLANES_F32), SUBLANES_F32)

    def body(w_ref, g_ref, m_ref, v_ref, go_ref, d_ref, mo_ref, vo_ref):
        gr = g_ref[...]
        mn = ADAM_B1 * m_ref[...] + (1.0 - ADAM_B1) * gr
        vn = ADAM_B2 * v_ref[...] + (1.0 - ADAM_B2) * (gr * gr)
        m_hat = mn / (1.0 - ADAM_B1 ** ADAM_STEP)
        v_hat = vn / (1.0 - ADAM_B2 ** ADAM_STEP)
        d_ref[...] = -ADAM_LR * (m_hat / (jnp.sqrt(v_hat) + ADAM_EPS) + ADAM_WD * w_ref[...])
        go_ref[...] = gr
        mo_ref[...] = mn
        vo_ref[...] = vn

    spec = pl.BlockSpec((tr, cols), lambda i: (i, 0))
    g_spec = spec if g_plane is None else pl.BlockSpec((None, tr, cols), lambda i: (g_plane, i, 0))
    return pl.pallas_call(
        body, name=name, grid=(rows // tr,),
        in_specs=[spec, g_spec, spec, spec], out_specs=[spec, spec, spec, spec],
        out_shape=[SDS((rows, cols), F32)] * 4,
        compiler_params=_params(("parallel",), 8 * _nbytes((tr, cols), F32), 4 * _nbytes((tr, cols), F32)),
    )(w, g, m, v)


def _pack(parts):
    rows = []
    for p in parts:
        r = p.reshape(-1, LANES)
        pad = (-r.shape[0]) % SUBLANES_F32
        if pad:
            r = jnp.pad(r, ((0, pad), (0, 0)))
        rows.append(r)
    return jnp.concatenate(rows, axis=0)


def _unpack(packed, shapes):
    out, at = [], 0
    for s in shapes:
        n = 1
        for q in s:
            n *= q
        r = n // LANES
        out.append(packed[at:at + r].reshape(s))
        at += r + (-r) % SUBLANES_F32
    return out


def kernel(x, norm_mix, w_in, pool_w, pool_scale, w_pool_proj, conv_w, w_conv_out, w_o, norm_ffn, w_up, ffn_conv_w, ffn_conv_b, w_down, norm_final, loss_target, m_norm_mix, m_w_in, m_pool_w, m_pool_scale, m_w_pool_proj, m_conv_w, m_w_conv_out, m_w_o, m_norm_ffn, m_w_up, m_ffn_conv_w, m_ffn_conv_b, m_w_down, m_norm_final, v_norm_mix, v_w_in, v_pool_w, v_pool_scale, v_w_pool_proj, v_conv_w, v_w_conv_out, v_w_o, v_norm_ffn, v_w_up, v_ffn_conv_w, v_ffn_conv_b, v_w_down, v_norm_final):
    nseq, seq, d = x.shape
    t = nseq * seq
    f = w_down.shape[1] * 4
    c = d // N_GROUPS
    xy = lax.axis_index("x") * 2 + lax.axis_index("y")
    c_arr = lax.axis_index("c").astype(jnp.int32).reshape(1)
    jc_arr = jnp.stack([xy, lax.axis_index("c")]).astype(jnp.int32)
    nsh = 4
    zero = jnp.zeros((), jnp.int32)

    locs = [w_in[0].astype(BF16),
            jnp.stack([w_pool_proj[0], w_conv_out[0], w_o[0]]).astype(BF16),
            w_up[0].astype(BF16), w_down[0].astype(BF16), pool_w[0].astype(BF16)]
    full_shapes = [(nsh, d, N_SPLITS * d // nsh), (3, d, d), (nsh, d, 2 * f // nsh), (f, d), (N_GROUPS, c, c)]

    cw_pad = lax.dynamic_update_slice(jnp.zeros((3, d), F32), conv_w[0], (zero, xy * (d // 4)))
    fw_pad = lax.dynamic_update_slice(jnp.zeros((3, 2 * f), F32), ffn_conv_w[0], (zero, xy * (f // 2)))
    small_w = _pack([cw_pad, fw_pad]) * 0.5

    x2d = x.reshape(t, d)
    tgt = loss_target.reshape(t, d)
    ax, ay = lax.axis_index("x"), lax.axis_index("y")
    order = jnp.stack([xy, 2 * (1 - ax) + ay, 2 * ax + 1 - ay, 2 * (1 - ax) + 1 - ay]).astype(jnp.int32)
    (z, h1, w_in_f), (pool_w_f, w3_f, slots_w) = _fwd_in(
        x2d, norm_mix, locs[0], order,
        _merge([_gather_comm([4], locs, full_shapes), _gather_comm([1], locs, full_shapes, part=(0, 1, 2)),
                _small_comm(small_w)]))
    conv_w_f, ffn_cw_f = _unpack(_sum8(slots_w, "sum8_weights"), [(3, d), (3, 2 * f)])
    ffn_cw_p = ffn_cw_f.reshape(3, 2, f).transpose(1, 0, 2)
    ffn_cb_p = ffn_conv_b.reshape(2, 1, f)
    (lhs3,), (w3_f,) = _mixer_mid_fwd(z, pool_w_f, pool_scale, conv_w_f, nseq,
                                      _gather_comm([1], locs, full_shapes, part=(1, 2, 2), into={1: w3_f}))
    (lhs3, ypc, x1, h2), (w_up_f,) = _mixer_out(lhs3, z, x2d, w3_f, norm_ffn,
                                                _ring_gather_comm([2], locs, full_shapes))
    (u0,), (w_down_f,) = _ffn_up(h2, w_up_f, f, _gather_comm([3], locs, full_shapes))
    act, ua = _ffn_mid_fwd(u0, ffn_cw_p, ffn_cb_p, nseq)
    dx2, dx2b, loss11, g_norm_final = _ffn_down_loss(act, w_down_f, x1, tgt, norm_final.reshape(1, d))

    gbs, lands, ps, lands2, rs = {}, {}, {}, {}, {}
    tn_up = _tile(2 * f // nsh, 1408, LANES)
    npp = f // tn_up

    def add(arrs, name):
        for a, p in zip(arrs, _add_halves(arrs, gbs, [lands[a] for a in arrs], c_arr, name)):
            ps[a] = p

    def summed(a):
        rs[a] = _sum_chips(a, ps[a], lands2[a], _shard_shape(a, full_shapes[a]), jc_arr, "sum_chips_%d" % a)

    (gbs[3],), _ = _wgrad(act, dx2b, "wgrad_down", tr=tn_up, tn=d)
    (da,), (lands[3],) = _ffn_bwd_da(dx2b, w_down_f, _halves_comm([3], gbs))
    add([3], "add_halves_down")
    (du0, g_ffn_cw_p, g_ffn_cb_p), (lands2[3],) = _ffn_mid_bwd(da, u0, ua, ffn_cw_p, nseq, _chips_comm([3], ps))
    summed(3)
    (gbs[2],), (rs[3],) = _wgrad(h2, du0, "wgrad_up", tr=d, tn=tn_up, b_plane_of=lambda n: (n // npp, n % npp),
                                 out_shards=nsh, comm=_result_comm([3], rs))
    (dx1, rhs3, g_norm_ffn), (lands[2],) = _ffn_bwd_dx1(du0, w_up_f, x1, dx2, norm_ffn, 3, _halves_comm([2], gbs))
    add([2], "add_halves_up")
    (rhs3, dz, dpq), (lands2[2],) = _mixer_bwd(rhs3, z, ypc, w3_f, _chips_comm([2], ps, part=(0, 1, 2)))
    (gbs[1],), (lands2[2],) = _wgrad3(lhs3, rhs3, _chips_comm([2], ps, part=(1, 2, 2), into=lands2))
    summed(2)
    (dz, g_conv_w), (lands[1], rs[2]) = _conv_bwd(dz, dpq, z, conv_w_f, nseq,
                                                  _merge([_halves_comm([1], gbs), _result_comm([2], rs)]))
    add([1], "add_halves_sq3")
    (dz, g_pool_w, g_pool_scale), _ = _pool_bwd_call(dz, dpq, z, pool_w_f, pool_scale, nseq)
    gbs[4] = g_pool_w.astype(BF16)
    (gbs[0],), (lands2[1],) = _wgrad_in(h1, dz, nsh, _chips_comm([1], ps))
    summed(1)
    lands[0], lands[4] = _run_comm(_halves_comm([0, 4], gbs), "exchange_halves_in")
    add([0, 4], "add_halves_in")
    g_ffn_cw = g_ffn_cw_p.transpose(1, 0, 2).reshape(3, 2 * f)
    small_a = _pack([g_pool_scale, g_norm_ffn, g_ffn_cb_p.reshape(1, 2 * f), g_norm_final.reshape(d), g_conv_w,
                     g_ffn_cw, jnp.pad(loss11, ((0, SUBLANES_F32 - 1), (0, LANES - 1)))])
    (grad_x, g_norm_mix), (lands2[0], lands2[4], rs[1], slots_a) = _mixer_bwd_dx(
        dz, w_in_f, x2d, dx1, norm_mix,
        _merge([_chips_comm([0, 4], ps), _result_comm([1], rs), _small_comm(small_a)]))
    summed(0)
    summed(4)
    rs[0], rs[4], slots_b = _run_comm(_merge([_result_comm([0, 4], rs), _small_comm(_pack([g_norm_mix]))]),
                                      "exchange_result_in")
    shapes_a = [(1, d), (1, d), (1, 2 * f), (d,), (3, d), (3, 2 * f), (SUBLANES_F32, LANES)]
    gs_pool_scale, gs_norm_ffn, gs_ffn_cb, gs_norm_final, gs_conv_w, gs_ffn_cw, loss_blk = _unpack(
        _sum8(slots_a, "sum8_grads"), shapes_a)
    (gs_norm_mix,) = _unpack(_sum8(slots_b, "sum8_norm_mix"), [(1, d)])
    gs_conv_w = lax.dynamic_slice(gs_conv_w, (zero, xy * (d // 4)), (3, d // 4))
    gs_ffn_cw = lax.dynamic_slice(gs_ffn_cw, (zero, xy * (f // 2)), (3, f // 2))

    def upd(w, g, m, v, name, g_plane=None):
        shape = w.shape
        rows = 1
        for q in shape[:-1]:
            rows *= q
        g2 = g if g_plane is not None else g.reshape(rows, shape[-1])
        outs = _adamw(w.reshape(rows, shape[-1]), g2, m.reshape(rows, shape[-1]), v.reshape(rows, shape[-1]),
                      name, g_plane)
        return [o.reshape(shape) for o in outs]

    res = {
        "w_in": upd(w_in, rs[0], m_w_in, v_w_in, "adamw_w_in"),
        "pool_w": upd(pool_w, rs[4], m_pool_w, v_pool_w, "adamw_pool_w"),
        "w_pool_proj": upd(w_pool_proj, rs[1], m_w_pool_proj, v_w_pool_proj, "adamw_w_pool_proj", 0),
        "w_conv_out": upd(w_conv_out, rs[1], m_w_conv_out, v_w_conv_out, "adamw_w_conv_out", 1),
        "w_o": upd(w_o, rs[1], m_w_o, v_w_o, "adamw_w_o", 2),
        "w_up": upd(w_up, rs[2], m_w_up, v_w_up, "adamw_w_up"),
        "w_down": upd(w_down, rs[3], m_w_down, v_w_down, "adamw_w_down"),
    }

    small_names = ["norm_mix", "pool_scale", "norm_ffn", "ffn_conv_b", "norm_final", "conv_w", "ffn_conv_w"]
    small_ws = [norm_mix, pool_scale, norm_ffn, ffn_conv_b, norm_final, conv_w, ffn_conv_w]
    small_ms = [m_norm_mix, m_pool_scale, m_norm_ffn, m_ffn_conv_b, m_norm_final, m_conv_w, m_ffn_conv_w]
    small_vs = [v_norm_mix, v_pool_scale, v_norm_ffn, v_ffn_conv_b, v_norm_final, v_conv_w, v_ffn_conv_w]
    small_gs = [gs_norm_mix, gs_pool_scale, gs_norm_ffn, gs_ffn_cb, gs_norm_final, gs_conv_w, gs_ffn_cw]
    _, sd, sm, sv = _adamw(_pack(small_ws), _pack(small_gs), _pack(small_ms), _pack(small_vs), "adamw_small")
    shapes = [w.shape for w in small_ws]
    sd, sm, sv = _unpack(sd, shapes), _unpack(sm, shapes), _unpack(sv, shapes)
    for i, nm in enumerate(small_names):
        res[nm] = [small_gs[i].reshape(shapes[i]), sd[i], sm[i], sv[i]]

    order = ["norm_mix", "w_in", "pool_w", "pool_scale", "w_pool_proj", "conv_w", "w_conv_out", "w_o", "norm_ffn",
             "w_up", "ffn_conv_w", "ffn_conv_b", "w_down", "norm_final"]
    return (loss_blk[0, 0], grad_x.reshape(x.shape), *[res[n][0] for n in order], *[res[n][1] for n in order],
            *[res[n][2] for n in order], *[res[n][3] for n in order])
```
